```python
import math
import jax, jax.numpy as jnp
from jax import lax
import numpy as np

D_MODEL = 1024
BATCH = 8
SEQ = 8192
DEPTH = 1

N_MEM = 256
MIX_WIDTH = D_MODEL
CONV_WIDTH = MIX_WIDTH // 2
CONV_GROUPS = 8
CONV_K = 3
GLA_HEADS = 4
GLA_DV = (MIX_WIDTH - CONV_WIDTH) // GLA_HEADS
GLA_DK = GLA_DV // 2
GLA_K_TOTAL = GLA_HEADS * GLA_DK
GLA_V_TOTAL = GLA_HEADS * GLA_DV
GLA_LOWRANK = 16
GLA_GATE_NORM = 16.0
GLA_CHUNK = 64
XA_HEADS = 4
XA_HEAD_DIM = D_MODEL // XA_HEADS
D_FF = 4 * D_MODEL
EPS = 1e-6

SPLITS = [CONV_WIDTH, CONV_WIDTH, CONV_WIDTH,
          GLA_K_TOTAL, GLA_K_TOTAL, GLA_V_TOTAL, GLA_V_TOTAL,
          GLA_LOWRANK, GLA_LOWRANK]
W_IN_COLS = int(sum(SPLITS))

kernel_name = "hybrid_conv_gla_memxattn_encoder_block"


def rms_norm(x, g):
    xf = x.astype(jnp.float32)
    y = xf * lax.rsqrt(jnp.mean(xf * xf, axis=-1, keepdims=True) + EPS)
    return (y * g.astype(jnp.float32)).astype(x.dtype)


def split_cols(z):
    offs = np.cumsum(SPLITS)[:-1].tolist()
    return jnp.split(z, offs, axis=-1)


def short_conv_mixer(b_gate, c_gate, u, conv_w):
    h = c_gate * u
    hp = jnp.pad(h, ((0, 0), (1, 1), (0, 0)))
    y = (conv_w[0] * hp[:, :-2] + conv_w[1] * hp[:, 1:-1] + conv_w[2] * hp[:, 2:])
    return b_gate * y


def gla_chunked(q, k, v, log_a):
    bsz, s, h, dk = q.shape
    dv = v.shape[-1]
    nc = s // GLA_CHUNK

    def to_chunks(t):
        return t.reshape(bsz, nc, GLA_CHUNK, h, t.shape[-1]).transpose(0, 3, 1, 2, 4)

    q, k, v, la = to_chunks(q), to_chunks(k), to_chunks(v), to_chunks(log_a)
    b = jnp.cumsum(la, axis=3)
    q_t = q * jnp.exp(b)
    k_t = k * jnp.exp(-b)
    mask = jnp.tril(jnp.ones((GLA_CHUNK, GLA_CHUNK), dtype=bool))
    attn = jnp.einsum('bhncd,bhnsd->bhncs', q_t, k_t)
    attn = jnp.where(mask, attn, 0.0)
    o_intra = jnp.einsum('bhncs,bhnsv->bhncv', attn, v)

    g_tot = b[:, :, :, -1, :]
    k_hat = k * jnp.exp(g_tot[:, :, :, None, :] - b)
    u_chunk = jnp.einsum('bhncd,bhncv->bhndv', k_hat, v)

    def step(state, inp):
        g_n, u_n = inp
        new_state = jnp.exp(g_n)[..., None] * state + u_n
        return new_state, state

    s0 = jnp.zeros((bsz, h, dk, dv), jnp.float32)
    _, s_before = lax.scan(step, s0, (jnp.moveaxis(g_tot, 2, 0), jnp.moveaxis(u_chunk, 2, 0)))
    o_inter = jnp.einsum('bhncd,nbhdv->bhncv', q_t, s_before)
    o = o_intra + o_inter
    return o.transpose(0, 2, 3, 1, 4).reshape(bsz, s, h, dv)


def bidirectional_gla(q, k, v, la_f, la_b):
    fwd = gla_chunked(q, k, v, la_f)
    flip = lambda t: jnp.flip(t, axis=1)
    bwd = flip(gla_chunked(flip(q), flip(k), flip(v), flip(la_b)))
    diag = jnp.einsum('bshd,bshd->bsh', q, k)[..., None] * v
    return fwd + bwd - diag


def mixer_layer(x, mix_norm, w_in, conv_w, conv_norm, w_af, b_af, w_ab, b_ab, gla_norm, w_out):
    bsz, s, _ = x.shape
    h = rms_norm(x, mix_norm)
    z = h @ w_in
    cb, cc, cu, q, k, v, g, lr_f, lr_b = split_cols(z)

    y_conv = short_conv_mixer(cb, cc, cu, conv_w)
    y_conv = rms_norm(y_conv.reshape(bsz, s, CONV_GROUPS, CONV_WIDTH // CONV_GROUPS),
                      conv_norm.reshape(CONV_GROUPS, -1)).reshape(bsz, s, CONV_WIDTH)

    f32 = jnp.float32
    qh = q.astype(f32).reshape(bsz, s, GLA_HEADS, GLA_DK) * (GLA_DK ** -0.5)
    kh = k.astype(f32).reshape(bsz, s, GLA_HEADS, GLA_DK)
    vh = v.astype(f32).reshape(bsz, s, GLA_HEADS, GLA_DV)
    la_f = jax.nn.log_sigmoid((lr_f @ w_af + b_af).astype(f32)) / GLA_GATE_NORM
    la_b = jax.nn.log_sigmoid((lr_b @ w_ab + b_ab).astype(f32)) / GLA_GATE_NORM
    la_f = la_f.reshape(bsz, s, GLA_HEADS, GLA_DK)
    la_b = la_b.reshape(bsz, s, GLA_HEADS, GLA_DK)
    o = bidirectional_gla(qh, kh, vh, la_f, la_b)
    o = rms_norm(o, gla_norm).reshape(bsz, s, GLA_V_TOTAL).astype(x.dtype)
    y_gla = o * jax.nn.silu(g)

    y = jnp.concatenate([y_conv, y_gla], axis=-1)
    return y @ w_out


def memory_cross_attention(x, mem, xa_norm, mem_norm, w_xq, w_xkv, w_xo):
    bsz, s, _ = x.shape
    hq = rms_norm(x, xa_norm) @ w_xq
    kv = rms_norm(mem, mem_norm) @ w_xkv
    km, vm = jnp.split(kv, 2, axis=-1)
    qh = hq.reshape(bsz, s, XA_HEADS, XA_HEAD_DIM)
    kh = km.reshape(bsz, N_MEM, XA_HEADS, XA_HEAD_DIM)
    vh = vm.reshape(bsz, N_MEM, XA_HEADS, XA_HEAD_DIM)
    scores = jnp.einsum('bqhd,bmhd->bhqm', qh, kh).astype(jnp.float32) / math.sqrt(XA_HEAD_DIM)
    p = jax.nn.softmax(scores, axis=-1).astype(x.dtype)
    o = jnp.einsum('bhqm,bmhd->bqhd', p, vh).reshape(bsz, s, D_MODEL)
    return o @ w_xo


def sq_relu_mlp(x, mlp_norm, w_up, w_down):
    h = rms_norm(x, mlp_norm) @ w_up
    return jnp.square(jax.nn.relu(h)) @ w_down


def _fwd_setup_inputs(seed: int = 0) -> dict:
    key = jax.random.key(seed)
    ks = jax.random.split(key, 24)
    L, D = DEPTH, D_MODEL
    nrm = lambda k, shape, fan_in: jax.random.normal(k, shape, jnp.float32) * (fan_in ** -0.5)
    gain = lambda k, shape: 1.0 + 0.01 * jax.random.normal(k, shape, jnp.float32)
    bias = lambda k, shape: 0.1 * jax.random.normal(k, shape, jnp.float32)
    return {
        "x": jax.random.normal(ks[0], (BATCH, SEQ, D), jnp.float32),
        "mem": jax.random.normal(ks[1], (BATCH, N_MEM, D), jnp.float32),
        "mix_norm": gain(ks[2], (L, D)),
        "w_in": nrm(ks[3], (L, D, W_IN_COLS), D),
        "conv_w": nrm(ks[4], (L, CONV_K, CONV_WIDTH), CONV_K),
        "conv_norm": gain(ks[5], (L, CONV_WIDTH)),
        "w_af": nrm(ks[6], (L, GLA_LOWRANK, GLA_K_TOTAL), GLA_LOWRANK),
        "b_af": bias(ks[7], (L, GLA_K_TOTAL)),
        "w_ab": nrm(ks[8], (L, GLA_LOWRANK, GLA_K_TOTAL), GLA_LOWRANK),
        "b_ab": bias(ks[9], (L, GLA_K_TOTAL)),
        "gla_norm": gain(ks[10], (L, GLA_DV)),
        "w_out": nrm(ks[11], (L, MIX_WIDTH, D), MIX_WIDTH),
        "xa_norm": gain(ks[12], (L, D)),
        "mem_norm": gain(ks[13], (L, D)),
        "w_xq": nrm(ks[14], (L, D, D), D),
        "w_xkv": nrm(ks[15], (L, D, 2 * D), D),
        "w_xo": nrm(ks[16], (L, D, D), D),
        "mlp_norm": gain(ks[17], (L, D)),
        "w_up": nrm(ks[18], (L, D, D_FF), D),
        "w_down": nrm(ks[19], (L, D_FF, D), D_FF),
        "final_norm": gain(ks[20], (D,)),
    }


def _fwd_reference(x, mem, mix_norm, w_in, conv_w, conv_norm, w_af, b_af, w_ab, b_ab, gla_norm,
              w_out, xa_norm, mem_norm, w_xq, w_xkv, w_xo, mlp_norm, w_up, w_down, final_norm):
    for l in range(DEPTH):
        x = x + mixer_layer(x, mix_norm[l], w_in[l], conv_w[l], conv_norm[l], w_af[l], b_af[l],
                            w_ab[l], b_ab[l], gla_norm[l], w_out[l])
        x = x + memory_cross_attention(x, mem, xa_norm[l], mem_norm[l], w_xq[l], w_xkv[l], w_xo[l])
        x = x + sq_relu_mlp(x, mlp_norm[l], w_up[l], w_down[l])
    return rms_norm(x, final_norm)


import jax as _jax
import jax.numpy as _jnp

TWIN_FORMAT = 'train_step'
FWD_PARAMS = ['x', 'mem', 'mix_norm', 'w_in', 'conv_w', 'conv_norm', 'w_af', 'b_af', 'w_ab', 'b_ab', 'gla_norm', 'w_out', 'xa_norm', 'mem_norm', 'w_xq', 'w_xkv', 'w_xo', 'mlp_norm', 'w_up', 'w_down', 'final_norm']
TWIN_WEIGHTS = ['mix_norm', 'w_in', 'conv_w', 'conv_norm', 'w_af', 'b_af', 'w_ab', 'b_ab', 'gla_norm', 'w_out', 'xa_norm', 'mem_norm', 'w_xq', 'w_xkv', 'w_xo', 'mlp_norm', 'w_up', 'w_down', 'final_norm']
TWIN_DIFF_INPUT = 'x'
TWIN_INPUTS = ['x', 'mem', 'mix_norm', 'w_in', 'conv_w', 'conv_norm', 'w_af', 'b_af', 'w_ab', 'b_ab', 'gla_norm', 'w_out', 'xa_norm', 'mem_norm', 'w_xq', 'w_xkv', 'w_xo', 'mlp_norm', 'w_up', 'w_down', 'final_norm', 'loss_target', 'm_mix_norm', 'm_w_in', 'm_conv_w', 'm_conv_norm', 'm_w_af', 'm_b_af', 'm_w_ab', 'm_b_ab', 'm_gla_norm', 'm_w_out', 'm_xa_norm', 'm_mem_norm', 'm_w_xq', 'm_w_xkv', 'm_w_xo', 'm_mlp_norm', 'm_w_up', 'm_w_down', 'm_final_norm', 'v_mix_norm', 'v_w_in', 'v_conv_w', 'v_conv_norm', 'v_w_af', 'v_b_af', 'v_w_ab', 'v_b_ab', 'v_gla_norm', 'v_w_out', 'v_xa_norm', 'v_mem_norm', 'v_w_xq', 'v_w_xkv', 'v_w_xo', 'v_mlp_norm', 'v_w_up', 'v_w_down', 'v_final_norm']
TWIN_OUTPUTS = ['loss', 'grad_x', 'grad_mix_norm', 'grad_w_in', 'grad_conv_w', 'grad_conv_norm', 'grad_w_af', 'grad_b_af', 'grad_w_ab', 'grad_b_ab', 'grad_gla_norm', 'grad_w_out', 'grad_xa_norm', 'grad_mem_norm', 'grad_w_xq', 'grad_w_xkv', 'grad_w_xo', 'grad_mlp_norm', 'grad_w_up', 'grad_w_down', 'grad_final_norm', 'delta_mix_norm', 'delta_w_in', 'delta_conv_w', 'delta_conv_norm', 'delta_w_af', 'delta_b_af', 'delta_w_ab', 'delta_b_ab', 'delta_gla_norm', 'delta_w_out', 'delta_xa_norm', 'delta_mem_norm', 'delta_w_xq', 'delta_w_xkv', 'delta_w_xo', 'delta_mlp_norm', 'delta_w_up', 'delta_w_down', 'delta_final_norm', 'new_m_mix_norm', 'new_m_w_in', 'new_m_conv_w', 'new_m_conv_norm', 'new_m_w_af', 'new_m_b_af', 'new_m_w_ab', 'new_m_b_ab', 'new_m_gla_norm', 'new_m_w_out', 'new_m_xa_norm', 'new_m_mem_norm', 'new_m_w_xq', 'new_m_w_xkv', 'new_m_w_xo', 'new_m_mlp_norm', 'new_m_w_up', 'new_m_w_down', 'new_m_final_norm', 'new_v_mix_norm', 'new_v_w_in', 'new_v_conv_w', 'new_v_conv_norm', 'new_v_w_af', 'new_v_b_af', 'new_v_w_ab', 'new_v_b_ab', 'new_v_gla_norm', 'new_v_w_out', 'new_v_xa_norm', 'new_v_mem_norm', 'new_v_w_xq', 'new_v_w_xkv', 'new_v_w_xo', 'new_v_mlp_norm', 'new_v_w_up', 'new_v_w_down', 'new_v_final_norm']
TWIN_LEAF_KINDS = {'loss': 'loss', 'grad_x': 'grad_x', 'grad_mix_norm': 'grad_w', 'grad_w_in': 'grad_w', 'grad_conv_w': 'grad_w', 'grad_conv_norm': 'grad_w', 'grad_w_af': 'grad_w', 'grad_b_af': 'grad_w', 'grad_w_ab': 'grad_w', 'grad_b_ab': 'grad_w', 'grad_gla_norm': 'grad_w', 'grad_w_out': 'grad_w', 'grad_xa_norm': 'grad_w', 'grad_mem_norm': 'grad_w', 'grad_w_xq': 'grad_w', 'grad_w_xkv': 'grad_w', 'grad_w_xo': 'grad_w', 'grad_mlp_norm': 'grad_w', 'grad_w_up': 'grad_w', 'grad_w_down': 'grad_w', 'grad_final_norm': 'grad_w', 'delta_mix_norm': 'delta_w', 'delta_w_in': 'delta_w', 'delta_conv_w': 'delta_w', 'delta_conv_norm': 'delta_w', 'delta_w_af': 'delta_w', 'delta_b_af': 'delta_w', 'delta_w_ab': 'delta_w', 'delta_b_ab': 'delta_w', 'delta_gla_norm': 'delta_w', 'delta_w_out': 'delta_w', 'delta_xa_norm': 'delta_w', 'delta_mem_norm': 'delta_w', 'delta_w_xq': 'delta_w', 'delta_w_xkv': 'delta_w', 'delta_w_xo': 'delta_w', 'delta_mlp_norm': 'delta_w', 'delta_w_up': 'delta_w', 'delta_w_down': 'delta_w', 'delta_final_norm': 'delta_w', 'new_m_mix_norm': 'new_m', 'new_m_w_in': 'new_m', 'new_m_conv_w': 'new_m', 'new_m_conv_norm': 'new_m', 'new_m_w_af': 'new_m', 'new_m_b_af': 'new_m', 'new_m_w_ab': 'new_m', 'new_m_b_ab': 'new_m', 'new_m_gla_norm': 'new_m', 'new_m_w_out': 'new_m', 'new_m_xa_norm': 'new_m', 'new_m_mem_norm': 'new_m', 'new_m_w_xq': 'new_m', 'new_m_w_xkv': 'new_m', 'new_m_w_xo': 'new_m', 'new_m_mlp_norm': 'new_m', 'new_m_w_up': 'new_m', 'new_m_w_down': 'new_m', 'new_m_final_norm': 'new_m', 'new_v_mix_norm': 'new_v', 'new_v_w_in': 'new_v', 'new_v_conv_w': 'new_v', 'new_v_conv_norm': 'new_v', 'new_v_w_af': 'new_v', 'new_v_b_af': 'new_v', 'new_v_w_ab': 'new_v', 'new_v_b_ab': 'new_v', 'new_v_gla_norm': 'new_v', 'new_v_w_out': 'new_v', 'new_v_xa_norm': 'new_v', 'new_v_mem_norm': 'new_v', 'new_v_w_xq': 'new_v', 'new_v_w_xkv': 'new_v', 'new_v_w_xo': 'new_v', 'new_v_mlp_norm': 'new_v', 'new_v_w_up': 'new_v', 'new_v_w_down': 'new_v', 'new_v_final_norm': 'new_v'}


def _forward(args):
    return _fwd_reference(*[args[k] for k in FWD_PARAMS])


def _output_shape():
    def fwd():
        inp = _fwd_setup_inputs(0)
        return _fwd_reference(*[inp[k] for k in FWD_PARAMS])
    out = _jax.eval_shape(fwd)
    return out.shape, out.dtype

N_MICROBATCH = 1
ADAM_LR = 0.001
ADAM_B1 = 0.9
ADAM_B2 = 0.999
ADAM_EPS = 1e-08
ADAM_WD = 0.01
ADAM_STEP = 10
PER_EXAMPLE_BATCH_AXIS = {'x': 0, 'mem': 0, 'loss_target': 0}
SHARED_INPUTS = []
_WEIGHT_DTYPES = {'mix_norm': _jnp.float32, 'w_in': _jnp.float32, 'conv_w': _jnp.float32, 'conv_norm': _jnp.float32, 'w_af': _jnp.float32, 'b_af': _jnp.float32, 'w_ab': _jnp.float32, 'b_ab': _jnp.float32, 'gla_norm': _jnp.float32, 'w_out': _jnp.float32, 'xa_norm': _jnp.float32, 'mem_norm': _jnp.float32, 'w_xq': _jnp.float32, 'w_xkv': _jnp.float32, 'w_xo': _jnp.float32, 'mlp_norm': _jnp.float32, 'w_up': _jnp.float32, 'w_down': _jnp.float32, 'final_norm': _jnp.float32}
MOMENT_SCALE = {'mix_norm': 3.305808e-01, 'w_in': 1.852847e-01, 'conv_w': 2.433821e-01, 'conv_norm': 2.136451e-01, 'w_af': 1.379663e-02, 'b_af': 6.925649e-02, 'w_ab': 1.420639e-02, 'b_ab': 6.198014e-02, 'gla_norm': 2.599707e-01, 'w_out': 1.717412e-01, 'xa_norm': 2.371974e-02, 'mem_norm': 3.278734e-02, 'w_xq': 2.194061e-02, 'w_xkv': 2.219400e-02, 'w_xo': 2.233933e-02, 'mlp_norm': 2.174314e-01, 'w_up': 9.948577e-02, 'w_down': 1.829087e-01, 'final_norm': 6.442213e+01}


def _to_microbatches(a, axis):
    t = _jnp.moveaxis(a, axis, 0)
    t = t.reshape((N_MICROBATCH, t.shape[0] // N_MICROBATCH) + t.shape[1:])
    return _jnp.moveaxis(t, 1, axis + 1)


def setup_inputs(seed: int = 0) -> dict:
    inp = _fwd_setup_inputs(seed)
    key = _jax.random.fold_in(_jax.random.key(seed), 7919)
    shape, _ = _output_shape()
    out = dict(inp)
    out["loss_target"] = _jax.random.normal(_jax.random.fold_in(key, 0), shape, _jnp.float32)
    for i, name in enumerate(TWIN_WEIGHTS):
        w = inp[name].astype(_jnp.float32)
        if MOMENT_SCALE is None:
            s = _jnp.sqrt(_jnp.mean(_jnp.square(w)) + 1e-30)
        else:
            s = MOMENT_SCALE[name]
        km, kv = _jax.random.split(_jax.random.fold_in(key, i + 1))
        out[name] = w
        out["m_" + name] = s * _jax.random.normal(km, w.shape, _jnp.float32)
        out["v_" + name] = (s * s) * _jax.random.uniform(kv, w.shape, _jnp.float32, 0.5, 1.5)
    if N_MICROBATCH > 1:
        for name, axis in PER_EXAMPLE_BATCH_AXIS.items():
            out[name] = _to_microbatches(out[name], axis)
    return {'x': out['x'], 'mem': out['mem'], 'mix_norm': out['mix_norm'], 'w_in': out['w_in'], 'conv_w': out['conv_w'], 'conv_norm': out['conv_norm'], 'w_af': out['w_af'], 'b_af': out['b_af'], 'w_ab': out['w_ab'], 'b_ab': out['b_ab'], 'gla_norm': out['gla_norm'], 'w_out': out['w_out'], 'xa_norm': out['xa_norm'], 'mem_norm': out['mem_norm'], 'w_xq': out['w_xq'], 'w_xkv': out['w_xkv'], 'w_xo': out['w_xo'], 'mlp_norm': out['mlp_norm'], 'w_up': out['w_up'], 'w_down': out['w_down'], 'final_norm': out['final_norm'], 'loss_target': out['loss_target'], 'm_mix_norm': out['m_mix_norm'], 'm_w_in': out['m_w_in'], 'm_conv_w': out['m_conv_w'], 'm_conv_norm': out['m_conv_norm'], 'm_w_af': out['m_w_af'], 'm_b_af': out['m_b_af'], 'm_w_ab': out['m_w_ab'], 'm_b_ab': out['m_b_ab'], 'm_gla_norm': out['m_gla_norm'], 'm_w_out': out['m_w_out'], 'm_xa_norm': out['m_xa_norm'], 'm_mem_norm': out['m_mem_norm'], 'm_w_xq': out['m_w_xq'], 'm_w_xkv': out['m_w_xkv'], 'm_w_xo': out['m_w_xo'], 'm_mlp_norm': out['m_mlp_norm'], 'm_w_up': out['m_w_up'], 'm_w_down': out['m_w_down'], 'm_final_norm': out['m_final_norm'], 'v_mix_norm': out['v_mix_norm'], 'v_w_in': out['v_w_in'], 'v_conv_w': out['v_conv_w'], 'v_conv_norm': out['v_conv_norm'], 'v_w_af': out['v_w_af'], 'v_b_af': out['v_b_af'], 'v_w_ab': out['v_w_ab'], 'v_b_ab': out['v_b_ab'], 'v_gla_norm': out['v_gla_norm'], 'v_w_out': out['v_w_out'], 'v_xa_norm': out['v_xa_norm'], 'v_mem_norm': out['v_mem_norm'], 'v_w_xq': out['v_w_xq'], 'v_w_xkv': out['v_w_xkv'], 'v_w_xo': out['v_w_xo'], 'v_mlp_norm': out['v_mlp_norm'], 'v_w_up': out['v_w_up'], 'v_w_down': out['v_w_down'], 'v_final_norm': out['v_final_norm']}


def _loss(weights, diff, rest, loss_target):
    with _jax.named_scope("forward"):
        args = {**rest, TWIN_DIFF_INPUT: diff, **{k: w.astype(_WEIGHT_DTYPES[k]) for k, w in weights.items()}}
        y = _forward(args)
    with _jax.named_scope("loss_head"):
        err = _jnp.square(y.astype(_jnp.float32) - loss_target)
        return 0.5 * _jnp.sum(_jnp.mean(err, axis=-1)) if err.ndim else 0.5 * err


def _adamw(w, g, m, v):
    m = ADAM_B1 * m + (1.0 - ADAM_B1) * g
    v = ADAM_B2 * v + (1.0 - ADAM_B2) * _jnp.square(g)
    m_hat = m / (1.0 - ADAM_B1 ** ADAM_STEP)
    v_hat = v / (1.0 - ADAM_B2 ** ADAM_STEP)
    delta = -ADAM_LR * (m_hat / (_jnp.sqrt(v_hat) + ADAM_EPS) + ADAM_WD * w)
    return delta, m, v


def reference(x, mem, mix_norm, w_in, conv_w, conv_norm, w_af, b_af, w_ab, b_ab, gla_norm, w_out, xa_norm, mem_norm, w_xq, w_xkv, w_xo, mlp_norm, w_up, w_down, final_norm, loss_target, m_mix_norm, m_w_in, m_conv_w, m_conv_norm, m_w_af, m_b_af, m_w_ab, m_b_ab, m_gla_norm, m_w_out, m_xa_norm, m_mem_norm, m_w_xq, m_w_xkv, m_w_xo, m_mlp_norm, m_w_up, m_w_down, m_final_norm, v_mix_norm, v_w_in, v_conv_w, v_conv_norm, v_w_af, v_b_af, v_w_ab, v_b_ab, v_gla_norm, v_w_out, v_xa_norm, v_mem_norm, v_w_xq, v_w_xkv, v_w_xo, v_mlp_norm, v_w_up, v_w_down, v_final_norm):
    given = dict(x=x, mem=mem, mix_norm=mix_norm, w_in=w_in, conv_w=conv_w, conv_norm=conv_norm, w_af=w_af, b_af=b_af, w_ab=w_ab, b_ab=b_ab, gla_norm=gla_norm, w_out=w_out, xa_norm=xa_norm, mem_norm=mem_norm, w_xq=w_xq, w_xkv=w_xkv, w_xo=w_xo, mlp_norm=mlp_norm, w_up=w_up, w_down=w_down, final_norm=final_norm, loss_target=loss_target, m_mix_norm=m_mix_norm, m_w_in=m_w_in, m_conv_w=m_conv_w, m_conv_norm=m_conv_norm, m_w_af=m_w_af, m_b_af=m_b_af, m_w_ab=m_w_ab, m_b_ab=m_b_ab, m_gla_norm=m_gla_norm, m_w_out=m_w_out, m_xa_norm=m_xa_norm, m_mem_norm=m_mem_norm, m_w_xq=m_w_xq, m_w_xkv=m_w_xkv, m_w_xo=m_w_xo, m_mlp_norm=m_mlp_norm, m_w_up=m_w_up, m_w_down=m_w_down, m_final_norm=m_final_norm, v_mix_norm=v_mix_norm, v_w_in=v_w_in, v_conv_w=v_conv_w, v_conv_norm=v_conv_norm, v_w_af=v_w_af, v_b_af=v_b_af, v_w_ab=v_w_ab, v_b_ab=v_b_ab, v_gla_norm=v_gla_norm, v_w_out=v_w_out, v_xa_norm=v_xa_norm, v_mem_norm=v_mem_norm, v_w_xq=v_w_xq, v_w_xkv=v_w_xkv, v_w_xo=v_w_xo, v_mlp_norm=v_mlp_norm, v_w_up=v_w_up, v_w_down=v_w_down, v_final_norm=v_final_norm)
    weights = {n: given[n] for n in TWIN_WEIGHTS}
    shared = {n: given[n] for n in SHARED_INPUTS}
    per_example = {n: given[n] for n in ['x', 'mem']}
    grad_fn = _jax.value_and_grad(_loss, argnums=(0, 1))

    def one_microbatch(ex, loss_target):
        ex = dict(ex)
        diff = ex.pop(TWIN_DIFF_INPUT)
        return grad_fn(weights, diff, {**shared, **ex}, loss_target)

    if N_MICROBATCH == 1:
        loss, (grad_w, grad_x) = one_microbatch(per_example, given["loss_target"])
    else:
        def body(carry, xs):
            loss_sum, grad_sum = carry
            l_k, (gw_k, gx_k) = one_microbatch(xs[0], xs[1])
            with _jax.named_scope("update"):
                return (loss_sum + l_k, _jax.tree.map(_jnp.add, grad_sum, gw_k)), gx_k

        init = (_jnp.zeros((), _jnp.float32), _jax.tree.map(_jnp.zeros_like, weights))
        (loss, grad_w), grad_x = _jax.lax.scan(body, init, (per_example, given["loss_target"]))
    with _jax.named_scope("update"):
        delta_w, new_m, new_v = {}, {}, {}
        for n in TWIN_WEIGHTS:
            delta_w[n], new_m[n], new_v[n] = _adamw(weights[n], grad_w[n], given["m_" + n], given["v_" + n])
    return (loss, grad_x, *[grad_w[n] for n in TWIN_WEIGHTS], *[delta_w[n] for n in TWIN_WEIGHTS],
            *[new_m[n] for n in TWIN_WEIGHTS], *[new_v[n] for n in TWIN_WEIGHTS])
```

```python
import functools

import jax
import jax.numpy as jnp
from jax import lax
from jax.experimental import pallas as pl
from jax.experimental.pallas import tpu as pltpu

F32 = jnp.float32
BF16 = jnp.bfloat16
_CD = jnp.bfloat16
_TD = jnp.bfloat16

D_MODEL = 1024
N_MEM = 256
CONV_WIDTH = 512
CONV_GROUP = 64
CONV_K = 3
GLA_HEADS = 4
GLA_DK = 64
GLA_DV = 128
GLA_K_TOTAL = 256
GLA_V_TOTAL = 512
GLA_LOWRANK = 16
GLA_GATE_SCALE = 1.0 / 16.0
GLA_CHUNK = 64
XA_HEADS = 4
XA_HEAD_DIM = 256
D_FF = 4096
EPS = 1e-6
W_IN_COLS = 3104
Z_COLS = 3200
LR_COL = 3072

ADAM_LR = 0.001
ADAM_B1 = 0.9
ADAM_B2 = 0.999
ADAM_EPS = 1e-08
ADAM_WD = 0.01
ADAM_STEP = 10

N_CHIPS = 4
PACK_W = 1024
PACK_ROWS = 4160
PACK_TILE = 160
SMALL_ROWS = 128

_TS = 512
_VMEM = 44 * 1024 * 1024
MESH = pl.DeviceIdType.MESH
ANY = pl.BlockSpec(memory_space=pl.ANY)


def _cp(sem=None, **kw):
    return pltpu.CompilerParams(dimension_semantics=sem, vmem_limit_bytes=_VMEM, **kw)


def _dot(a, b):
    return jnp.dot(a.astype(_CD), b.astype(_CD), preferred_element_type=F32)


def _dot_nt(a, b):
    return lax.dot_general(a.astype(_CD), b.astype(_CD), (((1,), (1,)), ((), ())), preferred_element_type=F32)


def _dot_tn(a, b):
    return lax.dot_general(a.astype(_CD), b.astype(_CD), (((0,), (0,)), ((), ())), preferred_element_type=F32)


def _dot_split(x, ones):
    hi = x.astype(BF16)
    r = x - hi.astype(F32)
    mid = r.astype(BF16)
    lo = (r - mid.astype(F32)).astype(BF16)
    d = lambda p: jnp.dot(p, ones, preferred_element_type=F32)
    return d(hi) + d(mid) + d(lo)


def _pick(n, cands=(1024, 640, 512, 256, 128)):
    for t in cands:
        if n % t == 0:
            return t
    return n


def _rows(s):
    return min(_TS, s)


def _sigmoid(v):
    e = jnp.exp(-jnp.abs(v))
    return jnp.where(v >= 0, 1.0 / (1.0 + e), e / (1.0 + e))


def _mm(a, b, *, mode, name, out_dtypes=(F32,), extras=(), epilogue=None, tm=None, tn=None, tk=None):
    m, k = a.shape
    n = b.shape[1] if mode == "nn" else b.shape[0]
    tm = tm or min(m, 512)
    tn = tn or _pick(n)
    tk = tk or _pick(k)
    nk = k // tk
    n_ex, n_out = len(extras), len(out_dtypes)

    def body(*refs):
        a_ref, b_ref = refs[:2]
        ex = refs[2 : 2 + n_ex]
        outs = refs[2 + n_ex : 2 + n_ex + n_out]
        part = _dot(a_ref[...], b_ref[...]) if mode == "nn" else _dot_nt(a_ref[...], b_ref[...])

        def finish(acc):
            res = epilogue(acc, *[e[...] for e in ex]) if epilogue else (acc,)
            for o, r in zip(outs, res):
                o[...] = r.astype(o.dtype)

        if nk == 1:
            finish(part)
        else:
            acc_ref = refs[-1]
            kk = pl.program_id(2)

            @pl.when(kk == 0)
            def _():
                acc_ref[...] = part

            @pl.when(kk > 0)
            def _():
                acc_ref[...] += part

            @pl.when(kk == nk - 1)
            def _():
                finish(acc_ref[...])

    b_spec = pl.BlockSpec((tk, tn), lambda i, j, kk: (kk, j)) if mode == "nn" else pl.BlockSpec((tn, tk), lambda i, j, kk: (j, kk))
    tile = pl.BlockSpec((tm, tn), lambda i, j, kk: (i, j))
    out = pl.pallas_call(
        body,
        name=name,
        grid=(m // tm, n // tn, nk),
        in_specs=[pl.BlockSpec((tm, tk), lambda i, j, kk: (i, kk)), b_spec] + [tile] * n_ex,
        out_specs=[tile] * n_out,
        out_shape=[jax.ShapeDtypeStruct((m, n), dt) for dt in out_dtypes],
        scratch_shapes=[pltpu.VMEM((tm, tn), F32)] if nk > 1 else [],
        compiler_params=_cp(("parallel", "parallel", "arbitrary")),
    )(a, b, *extras)
    return out[0] if n_out == 1 else out


def _mm_tn(a, b, *, name):
    s, m = a.shape
    n = b.shape[1]
    ts = min(s, 512)
    tm = min(m, 512, max(128, 1 << (((1 << 20) // n).bit_length() - 1)))
    ns = s // ts

    def body(a_ref, b_ref, o_ref):
        part = _dot_tn(a_ref[...], b_ref[...])
        if ns == 1:
            o_ref[...] = part
        else:
            ss = pl.program_id(1)

            @pl.when(ss == 0)
            def _():
                o_ref[...] = part

            @pl.when(ss > 0)
            def _():
                o_ref[...] += part

    return pl.pallas_call(
        body,
        name=name,
        grid=(m // tm, ns),
        in_specs=[pl.BlockSpec((ts, tm), lambda i, ss: (ss, i)), pl.BlockSpec((ts, n), lambda i, ss: (ss, 0))],
        out_specs=pl.BlockSpec((tm, n), lambda i, ss: (i, 0)),
        out_shape=jax.ShapeDtypeStruct((m, n), F32),
        compiler_params=_cp(("parallel", "arbitrary")),
    )(a, b)


def _rms_fwd(x, g, *, name):
    s, d = x.shape
    ts = _rows(s)

    def body(x_ref, g_ref, o_ref):
        xf = x_ref[...]
        r = lax.rsqrt(jnp.mean(xf * xf, axis=-1, keepdims=True) + EPS)
        o_ref[...] = (xf * r * g_ref[...]).astype(o_ref.dtype)

    return pl.pallas_call(
        body,
        name=name,
        grid=(s // ts,),
        in_specs=[pl.BlockSpec((ts, d), lambda i: (i, 0)), pl.BlockSpec((1, d), lambda i: (0, 0))],
        out_specs=pl.BlockSpec((ts, d), lambda i: (i, 0)),
        out_shape=jax.ShapeDtypeStruct((s, d), _CD),
        compiler_params=_cp(("parallel",)),
    )(x, g)


def _rms_bwd(x, g, dy, dres=None, *, name, want_dx=True, want_lo=True):
    s, d = x.shape
    ts = _rows(s)
    has_res = dres is not None

    def body(*refs):
        x_ref, g_ref, dy_ref = refs[:3]
        pos = 3
        dres_ref = refs[pos] if has_res else None
        pos += has_res
        dx_ref = refs[pos] if want_dx else None
        pos += want_dx
        lo_ref = refs[pos] if want_lo else None
        pos += want_lo
        dg_ref = refs[pos]
        xf = x_ref[...]
        r = lax.rsqrt(jnp.mean(xf * xf, axis=-1, keepdims=True) + EPS)
        xh = xf * r
        dyf = dy_ref[...]
        part = jnp.sum(dyf * xh, axis=0, keepdims=True)

        @pl.when(pl.program_id(0) == 0)
        def _():
            dg_ref[...] = part

        @pl.when(pl.program_id(0) > 0)
        def _():
            dg_ref[...] += part

        if want_dx or want_lo:
            dxh = dyf * g_ref[...]
            dx = r * (dxh - xh * jnp.mean(dxh * xh, axis=-1, keepdims=True))
            if has_res:
                dx = dx + dres_ref[...]
            if want_dx:
                dx_ref[...] = dx
            if want_lo:
                lo_ref[...] = dx.astype(lo_ref.dtype)

    tile = pl.BlockSpec((ts, d), lambda i: (i, 0))
    vec = pl.BlockSpec((1, d), lambda i: (0, 0))
    out_specs, out_shape = [], []
    if want_dx:
        out_specs.append(tile)
        out_shape.append(jax.ShapeDtypeStruct((s, d), F32))
    if want_lo:
        out_specs.append(tile)
        out_shape.append(jax.ShapeDtypeStruct((s, d), _CD))
    out_specs.append(vec)
    out_shape.append(jax.ShapeDtypeStruct((1, d), F32))
    return pl.pallas_call(
        body,
        name=name,
        grid=(s // ts,),
        in_specs=[tile, vec, tile] + ([tile] if has_res else []),
        out_specs=out_specs,
        out_shape=out_shape,
        compiler_params=_cp(("arbitrary",)),
    )(x, g, dy, *([dres] if has_res else []))


def _final_loss(x3, g, tgt, *, name):
    s, d = x3.shape
    ts = _rows(s)

    def body(x_ref, g_ref, t_ref, dx_ref, lo_ref, loss_ref, dg_ref):
        xf = x_ref[...]
        r = lax.rsqrt(jnp.mean(xf * xf, axis=-1, keepdims=True) + EPS)
        xh = xf * r
        gg = g_ref[...]
        err = xh * gg - t_ref[...]
        lpart = jnp.zeros((1, 128), F32) + 0.5 * jnp.sum(jnp.mean(err * err, axis=-1, keepdims=True))
        dy = err * (1.0 / d)
        gpart = jnp.sum(dy * xh, axis=0, keepdims=True)

        @pl.when(pl.program_id(0) == 0)
        def _():
            loss_ref[...] = lpart
            dg_ref[...] = gpart

        @pl.when(pl.program_id(0) > 0)
        def _():
            loss_ref[...] += lpart
            dg_ref[...] += gpart

        dxh = dy * gg
        dx = r * (dxh - xh * jnp.mean(dxh * xh, axis=-1, keepdims=True))
        dx_ref[...] = dx
        lo_ref[...] = dx.astype(lo_ref.dtype)

    tile = pl.BlockSpec((ts, d), lambda i: (i, 0))
    vec = pl.BlockSpec((1, d), lambda i: (0, 0))
    return pl.pallas_call(
        body,
        name=name,
        grid=(s // ts,),
        in_specs=[tile, vec, tile],
        out_specs=[tile, tile, pl.BlockSpec((1, 128), lambda i: (0, 0)), vec],
        out_shape=[
            jax.ShapeDtypeStruct((s, d), F32),
            jax.ShapeDtypeStruct((s, d), _CD),
            jax.ShapeDtypeStruct((1, 128), F32),
            jax.ShapeDtypeStruct((1, d), F32),
        ],
        compiler_params=_cp(("arbitrary",)),
    )(x3, g, tgt)


def _chunk_scan(v, row_in_chunk, suffix):
    t = v.shape[0]
    step = 1
    while step < GLA_CHUNK:
        if suffix:
            v = v + jnp.where(row_in_chunk < GLA_CHUNK - step, pltpu.roll(v, t - step, 0), 0.0)
        else:
            v = v + jnp.where(row_in_chunk >= step, pltpu.roll(v, step, 0), 0.0)
        step *= 2
    return v


def _gate_pre(lr, w_ref, b_ref):
    return _dot(lr, w_ref[...]) + b_ref[...]


def _gate_fwd(z, waf, wab, baf, bab, *, name):
    s = z.shape[0]
    ts = _rows(s)

    def body(lr_ref, waf_ref, wab_ref, baf_ref, bab_ref, bf_ref, bb_ref):
        lr = lr_ref[...]
        ric = lax.broadcasted_iota(jnp.int32, (ts, GLA_K_TOTAL), 0) & (GLA_CHUNK - 1)
        for w_ref, b_ref, o_ref, suffix in ((waf_ref, baf_ref, bf_ref, False), (wab_ref, bab_ref, bb_ref, True)):
            pre = _gate_pre(lr, w_ref, b_ref)
            la = (jnp.minimum(pre, 0.0) - jnp.log(1.0 + jnp.exp(-jnp.abs(pre)))) * GLA_GATE_SCALE
            o_ref[...] = _chunk_scan(la, ric, suffix)

    wspec = pl.BlockSpec((128, GLA_K_TOTAL), lambda i: (0, 0))
    bspec = pl.BlockSpec((1, GLA_K_TOTAL), lambda i: (0, 0))
    tile = pl.BlockSpec((ts, GLA_K_TOTAL), lambda i: (i, 0))
    return pl.pallas_call(
        body,
        name=name,
        grid=(s // ts,),
        in_specs=[pl.BlockSpec((ts, 128), lambda i: (i, LR_COL // 128)), wspec, wspec, bspec, bspec],
        out_specs=[tile, tile],
        out_shape=[jax.ShapeDtypeStruct((s, GLA_K_TOTAL), F32)] * 2,
        compiler_params=_cp(("parallel",)),
    )(z, waf, wab, baf, bab)


def _gate_bwd(z, waf, wab, baf, bab, dbf, dbb, dqkv_f, dqkv_b, *, name):
    s = z.shape[0]
    ts = _rows(s)

    def body(lr_ref, waf_ref, wab_ref, baf_ref, bab_ref, dbf_ref, dbb_ref, gf_ref, gb_ref, dqkv_ref, dlr_ref, dwf_ref, dwb_ref, dbaf_ref, dbab_ref):
        lr = lr_ref[...]
        ric = lax.broadcasted_iota(jnp.int32, (ts, GLA_K_TOTAL), 0) & (GLA_CHUNK - 1)
        first = pl.program_id(0) == 0
        dlr = None
        for w_ref, b_ref, db_ref, dw_ref, dbias_ref, suffix in (
            (waf_ref, baf_ref, dbf_ref, dwf_ref, dbaf_ref, True),
            (wab_ref, bab_ref, dbb_ref, dwb_ref, dbab_ref, False),
        ):
            pre = _gate_pre(lr, w_ref, b_ref)
            dla = _chunk_scan(db_ref[...], ric, suffix)
            dpre = dla * GLA_GATE_SCALE * _sigmoid(-pre)
            part = _dot_nt(dpre, w_ref[...])
            dlr = part if dlr is None else dlr + part
            dw = _dot_tn(lr, dpre)
            dbias = jnp.sum(dpre, axis=0, keepdims=True)

            @pl.when(first)
            def _():
                dw_ref[...] = dw
                dbias_ref[...] = dbias

            @pl.when(jnp.logical_not(first))
            def _():
                dw_ref[...] += dw
                dbias_ref[...] += dbias

        dlr_ref[...] = dlr.astype(dlr_ref.dtype)
        dqkv_ref[...] = (gf_ref[...] + gb_ref[...]).astype(dqkv_ref.dtype)

    wspec = pl.BlockSpec((128, GLA_K_TOTAL), lambda i: (0, 0))
    bspec = pl.BlockSpec((1, GLA_K_TOTAL), lambda i: (0, 0))
    tile = pl.BlockSpec((ts, GLA_K_TOTAL), lambda i: (i, 0))
    wide = pl.BlockSpec((ts, 2 * GLA_K_TOTAL + GLA_V_TOTAL), lambda i: (i, 0))
    return pl.pallas_call(
        body,
        name=name,
        grid=(s // ts,),
        in_specs=[pl.BlockSpec((ts, 128), lambda i: (i, LR_COL // 128)), wspec, wspec, bspec, bspec, tile, tile, wide, wide],
        out_specs=[wide, pl.BlockSpec((ts, 128), lambda i: (i, 0)), wspec, wspec, bspec, bspec],
        out_shape=[
            jax.ShapeDtypeStruct((s, 2 * GLA_K_TOTAL + GLA_V_TOTAL), _CD),
            jax.ShapeDtypeStruct((s, 128), _CD),
            jax.ShapeDtypeStruct((128, GLA_K_TOTAL), F32),
            jax.ShapeDtypeStruct((128, GLA_K_TOTAL), F32),
            jax.ShapeDtypeStruct((1, GLA_K_TOTAL), F32),
            jax.ShapeDtypeStruct((1, GLA_K_TOTAL), F32),
        ],
        compiler_params=_cp(("arbitrary",)),
    )(z, waf, wab, baf, bab, dbf, dbb, dqkv_f, dqkv_b)


def _gla_masks(rev):
    lane_head = lax.broadcasted_iota(jnp.int32, (1, GLA_K_TOTAL), 1) >> 6
    head_masks = [lane_head == h for h in range(GLA_HEADS)]
    st_rows = lax.broadcasted_iota(jnp.int32, (GLA_V_TOTAL, GLA_K_TOTAL), 0) >> 7
    st_lanes = lax.broadcasted_iota(jnp.int32, (GLA_V_TOTAL, GLA_K_TOTAL), 1) >> 6
    block_mask = st_rows == st_lanes
    t = lax.broadcasted_iota(jnp.int32, (GLA_HEADS * GLA_CHUNK, GLA_CHUNK), 0) & (GLA_CHUNK - 1)
    u = lax.broadcasted_iota(jnp.int32, (GLA_HEADS * GLA_CHUNK, GLA_CHUNK), 1)
    tri = (u > t) if rev else (u <= t)
    row = lax.broadcasted_iota(jnp.int32, (GLA_CHUNK, GLA_K_TOTAL), 0)
    total_row = row == (0 if rev else GLA_CHUNK - 1)
    return head_masks, block_mask, tri, total_row


def _gla_chunk_terms(q_ref, k_ref, v_ref, b_ref, rows, head_masks, tri, total_row):
    q = q_ref[rows, :] * (GLA_DK**-0.5)
    k = k_ref[rows, :]
    v = v_ref[rows, :]
    b = b_ref[rows, :]
    eb = jnp.exp(b)
    enb = jnp.exp(-b)
    g = jnp.sum(jnp.where(total_row, b, 0.0), axis=0, keepdims=True)
    egb = jnp.exp(g - b)
    qt = q * eb
    kt = k * enb
    kh = k * egb
    q_heads = jnp.concatenate([jnp.where(m, qt, 0.0) for m in head_masks], axis=0)
    attn = jnp.where(tri, _dot_nt(q_heads, kt), 0.0)
    return v, eb, enb, egb, jnp.exp(g), qt, kt, kh, attn


def _gla_specs(s, tb, rev_blocks):
    nb = s // tb
    rb = (lambda i: nb - 1 - i) if rev_blocks else (lambda i: i)
    q_spec = pl.BlockSpec((tb, GLA_K_TOTAL), lambda i: (rb(i), 1536 // GLA_K_TOTAL))
    k_spec = pl.BlockSpec((tb, GLA_K_TOTAL), lambda i: (rb(i), 1792 // GLA_K_TOTAL))
    v_spec = pl.BlockSpec((tb, GLA_V_TOTAL), lambda i: (rb(i), 2048 // GLA_V_TOTAL))
    b_spec = pl.BlockSpec((tb, GLA_K_TOTAL), lambda i: (rb(i), 0))
    o_spec = pl.BlockSpec((tb, GLA_V_TOTAL), lambda i: (rb(i), 0))
    st_spec = pl.BlockSpec((tb // GLA_CHUNK, GLA_DV, GLA_K_TOTAL), lambda i: (rb(i), 0, 0))
    return nb, q_spec, k_spec, v_spec, b_spec, o_spec, st_spec


def _gla_fwd(z, b, *, rev, name):
    s = z.shape[0]
    tb = _rows(s)
    cpb = tb // GLA_CHUNK
    nb, q_spec, k_spec, v_spec, b_spec, o_spec, st_spec = _gla_specs(s, tb, rev)

    def body(q_ref, k_ref, v_ref, b_ref, o_ref, sv_ref, st_ref):
        head_masks, block_mask, tri, total_row = _gla_masks(rev)

        @pl.when(pl.program_id(0) == 0)
        def _():
            st_ref[...] = jnp.zeros_like(st_ref)

        def chunk(ci, carry):
            cidx = cpb - 1 - ci if rev else ci
            rows = pl.ds(pl.multiple_of(cidx * GLA_CHUNK, GLA_CHUNK), GLA_CHUNK)
            v, _, _, _, eg, qt, _, kh, attn = _gla_chunk_terms(q_ref, k_ref, v_ref, b_ref, rows, head_masks, tri, total_row)
            o = jnp.concatenate(
                [_dot(attn[GLA_CHUNK * h : GLA_CHUNK * (h + 1)], v[:, GLA_DV * h : GLA_DV * (h + 1)]) for h in range(GLA_HEADS)], axis=1
            )
            st = st_ref[...]
            o_ref[rows, :] = o + _dot_nt(qt, st)
            sv_ref[cidx] = st[0:128] + st[128:256] + st[256:384] + st[384:512]
            st_ref[...] = st * eg + jnp.where(block_mask, _dot_tn(v, kh), 0.0)
            return carry

        lax.fori_loop(0, cpb, chunk, 0)

    return pl.pallas_call(
        body,
        name=name,
        grid=(nb,),
        in_specs=[q_spec, k_spec, v_spec, b_spec],
        out_specs=[o_spec, st_spec],
        out_shape=[jax.ShapeDtypeStruct((s, GLA_V_TOTAL), F32), jax.ShapeDtypeStruct((s // GLA_CHUNK, GLA_DV, GLA_K_TOTAL), F32)],
        scratch_shapes=[pltpu.VMEM((GLA_V_TOTAL, GLA_K_TOTAL), F32)],
        compiler_params=_cp(("arbitrary",)),
    )(z, z, z, b)


def _gla_bwd(z, b, do, states, *, rev, name):
    s = z.shape[0]
    tb = _rows(s)
    cpb = tb // GLA_CHUNK
    nb, q_spec, k_spec, v_spec, b_spec, o_spec, st_spec = _gla_specs(s, tb, not rev)
    rb = (lambda i: nb - 1 - i) if not rev else (lambda i: i)

    def body(q_ref, k_ref, v_ref, b_ref, do_ref, sv_ref, dqkv_ref, db_ref, dst_ref):
        head_masks, block_mask, tri, total_row = _gla_masks(rev)

        @pl.when(pl.program_id(0) == 0)
        def _():
            dst_ref[...] = jnp.zeros_like(dst_ref)

        def chunk(ci, carry):
            cidx = ci if rev else cpb - 1 - ci
            rows = pl.ds(pl.multiple_of(cidx * GLA_CHUNK, GLA_CHUNK), GLA_CHUNK)
            v, eb, enb, egb, eg, qt, kt, kh, attn = _gla_chunk_terms(q_ref, k_ref, v_ref, b_ref, rows, head_masks, tri, total_row)
            do_c = do_ref[rows, :]
            saved = sv_ref[cidx]
            st = jnp.where(block_mask, jnp.concatenate([saved] * GLA_HEADS, axis=0), 0.0)
            dst = dst_ref[...]
            hs = lambda a, h: a[GLA_CHUNK * h : GLA_CHUNK * (h + 1)]
            vs = lambda a, h: a[:, GLA_DV * h : GLA_DV * (h + 1)]
            dattn = jnp.concatenate([_dot_nt(vs(do_c, h), vs(v, h)) for h in range(GLA_HEADS)], axis=0)
            dattn = jnp.where(tri, dattn, 0.0)
            dv = jnp.concatenate([_dot_tn(hs(attn, h), vs(do_c, h)) for h in range(GLA_HEADS)], axis=1) + _dot_nt(kh, dst)
            dqt = _dot(do_c, st)
            dkt = jnp.zeros_like(dqt)
            for h in range(GLA_HEADS):
                dqt = dqt + jnp.where(head_masks[h], _dot(hs(dattn, h), kt), 0.0)
                dkt = dkt + jnp.where(head_masks[h], _dot_tn(hs(dattn, h), qt), 0.0)
            dkh = _dot(v, dst)
            dg = jnp.sum(dkh * kh, axis=0, keepdims=True) + jnp.sum(dst * st, axis=0, keepdims=True) * eg
            db = dqt * qt - dkt * kt - dkh * kh + jnp.where(total_row, dg, 0.0)
            dq = dqt * eb * (GLA_DK**-0.5)
            dk = dkt * enb + dkh * egb
            dqkv_ref[rows, :] = jnp.concatenate([dq, dk, dv], axis=1)
            db_ref[rows, :] = db
            dst_ref[...] = dst * eg + jnp.where(block_mask, _dot_tn(do_c, qt), 0.0)
            return carry

        lax.fori_loop(0, cpb, chunk, 0)

    wide = 2 * GLA_K_TOTAL + GLA_V_TOTAL
    return pl.pallas_call(
        body,
        name=name,
        grid=(nb,),
        in_specs=[q_spec, k_spec, v_spec, b_spec, o_spec, st_spec],
        out_specs=[pl.BlockSpec((tb, wide), lambda i: (rb(i), 0)), b_spec],
        out_shape=[jax.ShapeDtypeStruct((s, wide), F32), jax.ShapeDtypeStruct((s, GLA_K_TOTAL), F32)],
        scratch_shapes=[pltpu.VMEM((GLA_V_TOTAL, GLA_K_TOTAL), F32)],
        compiler_params=_cp(("arbitrary",)),
    )(z, z, z, b, do, states)


HALO = 8


def _halo_specs(s, ts, width, col):
    last = s // HALO - 1
    per = ts // HALO
    prev = pl.BlockSpec((HALO, width), lambda i: (jnp.maximum(i * per - 1, 0), col))
    nxt = pl.BlockSpec((HALO, width), lambda i: (jnp.minimum((i + 1) * per, last), col))
    return prev, nxt


def _group_ones():
    r = lax.broadcasted_iota(jnp.int32, (CONV_WIDTH, CONV_WIDTH), 0) >> 6
    c = lax.broadcasted_iota(jnp.int32, (CONV_WIDTH, CONV_WIDTH), 1) >> 6
    return (r == c).astype(BF16)


def _conv_terms(cc_ext, cu_ext, cw, valid):
    n = cc_ext.shape[0]
    hc = jnp.where(valid, cc_ext * cu_ext, 0.0)
    hc_prev = pltpu.roll(hc, 1, 0)
    hc_next = pltpu.roll(hc, n - 1, 0)
    conv = cw[0:1] * hc_prev + cw[1:2] * hc + cw[2:3] * hc_next
    return hc, hc_prev, hc_next, conv


def _ext(prev_ref, cur_ref, next_ref):
    return jnp.concatenate([prev_ref[...], cur_ref[...], next_ref[...]], axis=0)


def _valid_rows(ts, s):
    row = lax.broadcasted_iota(jnp.int32, (ts + 2 * HALO, 1), 0) + (pl.program_id(0) * ts - HALO)
    return (row >= 0) & (row < s)


def _head_norm(o, gn):
    out = []
    for h in range(GLA_HEADS):
        oh = o[:, GLA_DV * h : GLA_DV * (h + 1)]
        r = lax.rsqrt(jnp.mean(oh * oh, axis=-1, keepdims=True) + EPS)
        out.append((oh * r, r))
    return out


def _mix_fwd(z, o_f, o_b, conv_w, conv_norm, gla_norm, *, name):
    s = z.shape[0]
    ts = _rows(s)
    cprev, cnext = _halo_specs(s, ts, CONV_WIDTH, 1)
    uprev, unext = _halo_specs(s, ts, CONV_WIDTH, 2)

    def body(cb_ref, cc_ref, cu_ref, ccp_ref, ccn_ref, cup_ref, cun_ref, g_ref, of_ref, ob_ref, cw_ref, cn_ref, gn_ref, y_ref):
        valid = _valid_rows(ts, s)
        _, _, _, conv = _conv_terms(_ext(ccp_ref, cc_ref, ccn_ref), _ext(cup_ref, cu_ref, cun_ref), cw_ref[...], valid)
        yc = cb_ref[...] * conv[HALO : HALO + ts]
        ms = _dot_split(yc * yc, _group_ones()) * (1.0 / CONV_GROUP)
        y_conv = yc * lax.rsqrt(ms + EPS) * cn_ref[...]
        gate = g_ref[...]
        silu = gate * _sigmoid(gate)
        gn = gn_ref[...]
        y_gla = jnp.concatenate([oh * gn for oh, _ in _head_norm(of_ref[...] + ob_ref[...], gn)], axis=1) * silu
        y_ref[...] = jnp.concatenate([y_conv, y_gla], axis=1).astype(y_ref.dtype)

    col = lambda c, w=CONV_WIDTH: pl.BlockSpec((ts, w), lambda i: (i, c))
    return pl.pallas_call(
        body,
        name=name,
        grid=(s // ts,),
        in_specs=[col(0), col(1), col(2), cprev, cnext, uprev, unext, col(5), col(0), col(0),
                  pl.BlockSpec((CONV_K, CONV_WIDTH), lambda i: (0, 0)), pl.BlockSpec((1, CONV_WIDTH), lambda i: (0, 0)),
                  pl.BlockSpec((1, GLA_DV), lambda i: (0, 0))],
        out_specs=pl.BlockSpec((ts, D_MODEL), lambda i: (i, 0)),
        out_shape=jax.ShapeDtypeStruct((s, D_MODEL), _CD),
        compiler_params=_cp(("parallel",)),
    )(z, z, z, z, z, z, z, z, o_f, o_b, conv_w, conv_norm, gla_norm)


def _mix_bwd(z, o_f, o_b, dy, conv_w, conv_norm, gla_norm, *, name):
    s = z.shape[0]
    ts = _rows(s)
    halos = [_halo_specs(s, ts, CONV_WIDTH, c) for c in (0, 1, 2)]
    dprev, dnext = _halo_specs(s, ts, CONV_WIDTH, 0)

    def body(cb_ref, cc_ref, cu_ref, cbp_ref, cbn_ref, ccp_ref, ccn_ref, cup_ref, cun_ref, g_ref, of_ref, ob_ref,
             dyc_ref, dyg_ref, dyp_ref, dyn_ref, cw_ref, cn_ref, gn_ref, dconv_ref, dgate_ref, do_ref, dcw_ref, dcn_ref, dgn_ref):
        n = ts + 2 * HALO
        valid = _valid_rows(ts, s)
        cw = cw_ref[...]
        cn = cn_ref[...]
        ones = _group_ones()
        cb = _ext(cbp_ref, cb_ref, cbn_ref)
        cc = _ext(ccp_ref, cc_ref, ccn_ref)
        cu = _ext(cup_ref, cu_ref, cun_ref)
        dy = _ext(dyp_ref, dyc_ref, dyn_ref)
        hc, hc_prev, hc_next, conv = _conv_terms(cc, cu, cw, valid)
        yc = cb * conv
        r = lax.rsqrt(_dot_split(yc * yc, ones) * (1.0 / CONV_GROUP) + EPS)
        yh = yc * r
        dyh = dy * cn
        dyc = r * (dyh - yh * (_dot_split(dyh * yh, ones) * (1.0 / CONV_GROUP)))
        dconv = jnp.where(valid, dyc * cb, 0.0)
        dhc = cw[0:1] * pltpu.roll(dconv, n - 1, 0) + cw[1:2] * dconv + cw[2:3] * pltpu.roll(dconv, 1, 0)
        mid = lambda a: a[HALO : HALO + ts]
        dconv_ref[...] = jnp.concatenate([mid(dyc * conv), mid(dhc * cu), mid(dhc * cc)], axis=1).astype(dconv_ref.dtype)
        dconv_m = mid(dconv)
        colsum = lambda a: jnp.sum(a, axis=0, keepdims=True)
        dcw = jnp.concatenate([colsum(dconv_m * mid(hc_prev)), colsum(dconv_m * mid(hc)), colsum(dconv_m * mid(hc_next))], axis=0)
        dcn = colsum(mid(dy * yh))

        gate = g_ref[...]
        sg = _sigmoid(gate)
        silu = gate * sg
        gn = gn_ref[...]
        dyg = dyg_ref[...]
        don = dyg * silu
        heads = _head_norm(of_ref[...] + ob_ref[...], gn)
        on = jnp.concatenate([oh * gn for oh, _ in heads], axis=1)
        dgate_ref[...] = (dyg * on * (sg * (1.0 + gate * (1.0 - sg)))).astype(dgate_ref.dtype)
        dgn = jnp.zeros((1, GLA_DV), F32)
        dos = []
        for h, (oh, rh) in enumerate(heads):
            donh = don[:, GLA_DV * h : GLA_DV * (h + 1)]
            dgn = dgn + colsum(donh * oh)
            doh = donh * gn
            dos.append(rh * (doh - oh * jnp.mean(doh * oh, axis=-1, keepdims=True)))
        do_ref[...] = jnp.concatenate(dos, axis=1)

        first = pl.program_id(0) == 0

        @pl.when(first)
        def _():
            dcw_ref[...] = dcw
            dcn_ref[...] = dcn
            dgn_ref[...] = dgn

        @pl.when(jnp.logical_not(first))
        def _():
            dcw_ref[...] += dcw
            dcn_ref[...] += dcn
            dgn_ref[...] += dgn

    col = lambda c, w=CONV_WIDTH: pl.BlockSpec((ts, w), lambda i: (i, c))
    cw_spec = pl.BlockSpec((CONV_K, CONV_WIDTH), lambda i: (0, 0))
    cn_spec = pl.BlockSpec((1, CONV_WIDTH), lambda i: (0, 0))
    gn_spec = pl.BlockSpec((1, GLA_DV), lambda i: (0, 0))
    return pl.pallas_call(
        body,
        name=name,
        grid=(s // ts,),
        in_specs=[col(0), col(1), col(2), halos[0][0], halos[0][1], halos[1][0], halos[1][1], halos[2][0], halos[2][1],
                  col(5), col(0), col(0), col(0), col(1), dprev, dnext, cw_spec, cn_spec, gn_spec],
        out_specs=[pl.BlockSpec((ts, 3 * CONV_WIDTH), lambda i: (i, 0)), col(0), col(0), cw_spec, cn_spec, gn_spec],
        out_shape=[
            jax.ShapeDtypeStruct((s, 3 * CONV_WIDTH), _CD),
            jax.ShapeDtypeStruct((s, GLA_V_TOTAL), _CD),
            jax.ShapeDtypeStruct((s, GLA_V_TOTAL), F32),
            jax.ShapeDtypeStruct((CONV_K, CONV_WIDTH), F32),
            jax.ShapeDtypeStruct((1, CONV_WIDTH), F32),
            jax.ShapeDtypeStruct((1, GLA_DV), F32),
        ],
        compiler_params=_cp(("arbitrary",)),
    )(z, z, z, z, z, z, z, z, z, z, o_f, o_b, dy, dy, dy, dy, conv_w, conv_norm, gla_norm)


def _xa_probs(q_ref, kv_ref, h):
    qh = q_ref[:, XA_HEAD_DIM * h : XA_HEAD_DIM * (h + 1)]
    kh = kv_ref[:, XA_HEAD_DIM * h : XA_HEAD_DIM * (h + 1)]
    vh = kv_ref[:, D_MODEL + XA_HEAD_DIM * h : D_MODEL + XA_HEAD_DIM * (h + 1)]
    sc = _dot_nt(qh, kh) * (XA_HEAD_DIM**-0.5)
    e = jnp.exp(sc - jnp.max(sc, axis=-1, keepdims=True))
    return qh, kh, vh, e / jnp.sum(e, axis=-1, keepdims=True)


def _xattn_fwd(qx, kv, *, name):
    s = qx.shape[0]
    ts = _rows(s)

    def body(q_ref, kv_ref, o_ref):
        outs = []
        for h in range(XA_HEADS):
            _, _, vh, p = _xa_probs(q_ref, kv_ref, h)
            outs.append(_dot(p, vh))
        o_ref[...] = jnp.concatenate(outs, axis=1).astype(o_ref.dtype)

    return pl.pallas_call(
        body,
        name=name,
        grid=(s // ts,),
        in_specs=[pl.BlockSpec((ts, D_MODEL), lambda i: (i, 0)), pl.BlockSpec((N_MEM, 2 * D_MODEL), lambda i: (0, 0))],
        out_specs=pl.BlockSpec((ts, D_MODEL), lambda i: (i, 0)),
        out_shape=jax.ShapeDtypeStruct((s, D_MODEL), _CD),
        compiler_params=_cp(("parallel",)),
    )(qx, kv)


def _xattn_bwd(qx, kv, dox, *, name):
    s = qx.shape[0]
    ts = _rows(s)

    def body(q_ref, kv_ref, do_ref, dq_ref, dkv_ref):
        dqs, dks, dvs = [], [], []
        for h in range(XA_HEADS):
            qh, kh, vh, p = _xa_probs(q_ref, kv_ref, h)
            doh = do_ref[:, XA_HEAD_DIM * h : XA_HEAD_DIM * (h + 1)]
            dp = _dot_nt(doh, vh)
            ds = p * (dp - jnp.sum(dp * p, axis=-1, keepdims=True)) * (XA_HEAD_DIM**-0.5)
            dqs.append(_dot(ds, kh))
            dks.append(_dot_tn(ds, qh))
            dvs.append(_dot_tn(p, doh))
        dq_ref[...] = jnp.concatenate(dqs, axis=1).astype(dq_ref.dtype)
        dkv = jnp.concatenate(dks + dvs, axis=1)

        @pl.when(pl.program_id(0) == 0)
        def _():
            dkv_ref[...] = dkv

        @pl.when(pl.program_id(0) > 0)
        def _():
            dkv_ref[...] += dkv

    tile = pl.BlockSpec((ts, D_MODEL), lambda i: (i, 0))
    kv_spec = pl.BlockSpec((N_MEM, 2 * D_MODEL), lambda i: (0, 0))
    return pl.pallas_call(
        body,
        name=name,
        grid=(s // ts,),
        in_specs=[tile, kv_spec, tile],
        out_specs=[tile, kv_spec],
        out_shape=[jax.ShapeDtypeStruct((s, D_MODEL), _CD), jax.ShapeDtypeStruct((N_MEM, 2 * D_MODEL), F32)],
        compiler_params=_cp(("arbitrary",)),
    )(qx, kv, dox)


def _adamw_math(w, g, m, v):
    m = ADAM_B1 * m + (1.0 - ADAM_B1) * g
    v = ADAM_B2 * v + (1.0 - ADAM_B2) * (g * g)
    m_hat = m / (1.0 - ADAM_B1**ADAM_STEP)
    v_hat = v / (1.0 - ADAM_B2**ADAM_STEP)
    delta = -ADAM_LR * (m_hat / (jnp.sqrt(v_hat) + ADAM_EPS) + ADAM_WD * w)
    return delta, m, v


def _adamw(w, g, m, v, *, name):
    r, c = w.shape
    tr = _pick(r, (256, 128, 64, 32, 16, 8))

    def body(w_ref, g_ref, m_ref, v_ref, d_ref, nm_ref, nv_ref):
        d_ref[...], nm_ref[...], nv_ref[...] = _adamw_math(w_ref[...], g_ref[...], m_ref[...], v_ref[...])

    tile = pl.BlockSpec((tr, c), lambda i: (i, 0))
    return pl.pallas_call(
        body,
        name=name,
        grid=(r // tr,),
        in_specs=[tile] * 4,
        out_specs=[tile] * 3,
        out_shape=[jax.ShapeDtypeStruct((r, c), F32)] * 3,
        compiler_params=_cp(("parallel",)),
    )(w, g, m, v)


def _adamw_small(groups, *, name):
    n = len(groups)

    def body(*refs):
        ins, outs = refs[: 4 * n], refs[4 * n :]
        for i in range(n):
            w_ref, g_ref, m_ref, v_ref = ins[4 * i : 4 * i + 4]
            outs[3 * i][...], outs[3 * i + 1][...], outs[3 * i + 2][...] = _adamw_math(w_ref[...], g_ref[...], m_ref[...], v_ref[...])

    flat = [a for grp in groups for a in grp]
    vm = pl.BlockSpec(memory_space=pltpu.VMEM)
    res = pl.pallas_call(
        body,
        name=name,
        in_specs=[vm] * (4 * n),
        out_specs=[vm] * (3 * n),
        out_shape=[jax.ShapeDtypeStruct(grp[0].shape, F32) for grp in groups for _ in range(3)],
        compiler_params=_cp(),
    )(*flat)
    return [tuple(res[3 * i : 3 * i + 3]) for i in range(n)]


def _place():
    return lax.axis_index("x"), lax.axis_index("y"), lax.axis_index("c")


def _rel_chip(x, y, k):
    return (1 - x if k & 2 else x), (1 - y if k & 1 else y)


def _half(c, rh):
    return pl.ds(pl.multiple_of(c * rh, 16), rh)


def _gather_weights(pack):
    r, w = pack.shape
    rh = r // 2

    def body(p_ref, q_ref, send_sems, recv_sems, local_sem):
        x, y, c = _place()
        j = 2 * x + y
        rows = _half(c, rh)
        mine = pltpu.make_async_copy(p_ref, q_ref.at[j], local_sem)
        mine.start()

        def to_chip(k):
            cx, cy = _rel_chip(x, y, k)
            return pltpu.make_async_remote_copy(
                src_ref=p_ref.at[rows], dst_ref=q_ref.at[j, rows], send_sem=send_sems.at[k - 1], recv_sem=recv_sems.at[k - 1],
                device_id=(cx, cy, c), device_id_type=MESH)

        def to_sibling(k):
            cx, cy = _rel_chip(x, y, k)
            slot = q_ref.at[2 * cx + cy, rows]
            return pltpu.make_async_remote_copy(
                src_ref=slot, dst_ref=slot, send_sem=send_sems.at[2 + k], recv_sem=recv_sems.at[2 + k],
                device_id=(x, y, 1 - c), device_id_type=MESH)

        first = [to_chip(k) for k in range(1, N_CHIPS)]
        passed = [to_sibling(k) for k in range(1, N_CHIPS)]
        for cp in first:
            cp.start()
        for cp, fw in zip(first, passed):
            cp.wait_recv()
            fw.start()
        for fw in passed:
            fw.wait_recv()
        for cp in first + passed:
            cp.wait_send()
        mine.wait()

    return pl.pallas_call(
        body,
        name="gather_weights",
        in_specs=[ANY],
        out_specs=ANY,
        out_shape=jax.ShapeDtypeStruct((N_CHIPS, r, w), pack.dtype),
        scratch_shapes=[pltpu.SemaphoreType.DMA((6,)), pltpu.SemaphoreType.DMA((6,)), pltpu.SemaphoreType.DMA],
        compiler_params=pltpu.CompilerParams(has_side_effects=True),
    )(pack)


def _swap_halves(g):
    n, r, w = g.shape
    rh = r // 2

    def body(g_ref, o_ref, send_sem, recv_sem):
        x, y, c = _place()
        cp = pltpu.make_async_remote_copy(
            src_ref=g_ref.at[:, _half(1 - c, rh)], dst_ref=o_ref, send_sem=send_sem, recv_sem=recv_sem,
            device_id=(x, y, 1 - c), device_id_type=MESH)
        cp.start()
        cp.wait()

    return pl.pallas_call(
        body,
        name="grads_to_sibling",
        in_specs=[ANY],
        out_specs=ANY,
        out_shape=jax.ShapeDtypeStruct((n, rh, w), g.dtype),
        scratch_shapes=[pltpu.SemaphoreType.DMA, pltpu.SemaphoreType.DMA],
        compiler_params=pltpu.CompilerParams(has_side_effects=True),
    )(g)


def _chip_sums(g, got, where):
    n, r, w = g.shape
    rh = r // 2
    nt = rh // PACK_TILE

    def body(where_ref, g_ref, got_ref, o_ref):
        o_ref[...] = (g_ref[...] + got_ref[...]).astype(o_ref.dtype)

    return pl.pallas_call(
        body,
        name="chip_sums",
        grid_spec=pltpu.PrefetchScalarGridSpec(
            num_scalar_prefetch=1,
            grid=(n, nt),
            in_specs=[pl.BlockSpec((1, PACK_TILE, w), lambda a, i, wh: (a, wh[0] * nt + i, 0)),
                      pl.BlockSpec((1, PACK_TILE, w), lambda a, i, wh: (a, i, 0))],
            out_specs=pl.BlockSpec((1, PACK_TILE, w), lambda a, i, wh: (a, i, 0)),
        ),
        out_shape=jax.ShapeDtypeStruct((n, rh, w), _TD),
        compiler_params=_cp(("parallel", "parallel")),
    )(where, g, got)


def _exchange_chip_sums(h):
    n, rh, w = h.shape

    def body(h_ref, o_ref, send_sems, recv_sems):
        x, y, c = _place()
        j = 2 * x + y
        copies = []
        for k in range(1, N_CHIPS):
            cx, cy = _rel_chip(x, y, k)
            copies.append(pltpu.make_async_remote_copy(
                src_ref=h_ref.at[2 * cx + cy], dst_ref=o_ref.at[k - 1], send_sem=send_sems.at[k - 1], recv_sem=recv_sems.at[k - 1],
                device_id=(cx, cy, c), device_id_type=MESH))
        for cp in copies:
            cp.start()
        for cp in copies:
            cp.wait()

    return pl.pallas_call(
        body,
        name="chip_sums_exchange",
        in_specs=[ANY],
        out_specs=ANY,
        out_shape=jax.ShapeDtypeStruct((N_CHIPS - 1, rh, w), h.dtype),
        scratch_shapes=[pltpu.SemaphoreType.DMA((3,)), pltpu.SemaphoreType.DMA((3,))],
        compiler_params=pltpu.CompilerParams(has_side_effects=True),
    )(h)


def _shard_sum(g, got, others, where):
    n, r, w = g.shape
    rh = r // 2
    nt = rh // PACK_TILE

    def body(where_ref, g_ref, got_ref, oth_ref, o_ref):
        acc = g_ref[0] + got_ref[0]
        for k in range(N_CHIPS - 1):
            acc = acc + oth_ref[k].astype(F32)
        o_ref[...] = acc

    return pl.pallas_call(
        body,
        name="shard_sum",
        grid_spec=pltpu.PrefetchScalarGridSpec(
            num_scalar_prefetch=1,
            grid=(nt,),
            in_specs=[pl.BlockSpec((1, PACK_TILE, w), lambda i, wh: (wh[1], wh[0] * nt + i, 0)),
                      pl.BlockSpec((1, PACK_TILE, w), lambda i, wh: (wh[1], i, 0)),
                      pl.BlockSpec((N_CHIPS - 1, PACK_TILE, w), lambda i, wh: (0, i, 0))],
            out_specs=pl.BlockSpec((PACK_TILE, w), lambda i, wh: (i, 0)),
        ),
        out_shape=jax.ShapeDtypeStruct((rh, w), F32),
        compiler_params=_cp(("parallel",)),
    )(where, g, got, others)


def _join_halves(e):
    rh, w = e.shape

    def body(e_ref, o_ref, send_sem, recv_sem, local_sem):
        x, y, c = _place()
        rows = _half(c, rh)
        mine = pltpu.make_async_copy(e_ref, o_ref.at[rows], local_sem)
        mine.start()
        cp = pltpu.make_async_remote_copy(
            src_ref=e_ref, dst_ref=o_ref.at[rows], send_sem=send_sem, recv_sem=recv_sem, device_id=(x, y, 1 - c), device_id_type=MESH)
        cp.start()
        cp.wait()
        mine.wait()

    return pl.pallas_call(
        body,
        name="shard_to_sibling",
        in_specs=[ANY],
        out_specs=ANY,
        out_shape=jax.ShapeDtypeStruct((2 * rh, w), e.dtype),
        scratch_shapes=[pltpu.SemaphoreType.DMA, pltpu.SemaphoreType.DMA, pltpu.SemaphoreType.DMA],
        compiler_params=pltpu.CompilerParams(has_side_effects=True),
    )(e)


def _sum_small(small):
    n_dev = 8

    def body(s_ref, o_ref, all_ref, send_sems, recv_sems):
        x, y, c = _place()
        me = 4 * x + 2 * y + c
        all_ref[me] = s_ref[...]
        copies = []
        for k in range(1, n_dev):
            cx, cy = _rel_chip(x, y, k >> 1)
            cc = 1 - c if k & 1 else c
            copies.append(pltpu.make_async_remote_copy(
                src_ref=s_ref, dst_ref=all_ref.at[me], send_sem=send_sems.at[k - 1], recv_sem=recv_sems.at[k - 1],
                device_id=(cx, cy, cc), device_id_type=MESH))
        for cp in copies:
            cp.start()
        for cp in copies:
            cp.wait()
        acc = all_ref[0]
        for a in range(1, n_dev):
            acc = acc + all_ref[a]
        o_ref[...] = acc

    vm = pl.BlockSpec(memory_space=pltpu.VMEM)
    return pl.pallas_call(
        body,
        name="sum_small",
        in_specs=[vm],
        out_specs=vm,
        out_shape=jax.ShapeDtypeStruct(small.shape, F32),
        scratch_shapes=[pltpu.VMEM((n_dev,) + small.shape, F32), pltpu.SemaphoreType.DMA((n_dev - 1,)), pltpu.SemaphoreType.DMA((n_dev - 1,))],
        compiler_params=pltpu.CompilerParams(has_side_effects=True),
    )(small)


BIG = (
    ("w_in", 1, (D_MODEL, W_IN_COLS)),
    ("w_out", 0, (D_MODEL, D_MODEL)),
    ("w_xq", 0, (D_MODEL, D_MODEL)),
    ("w_xkv", 1, (D_MODEL, 2 * D_MODEL)),
    ("w_xo", 0, (D_MODEL, D_MODEL)),
    ("w_up", 1, (D_MODEL, D_FF)),
    ("w_down", 0, (D_FF, D_MODEL)),
)


def _shard_shape(axis, full):
    return (full[0] // N_CHIPS, full[1]) if axis == 0 else (full[0], full[1] // N_CHIPS)


def _pack_rows(pieces):
    rows = jnp.concatenate([p.reshape(-1, PACK_W) for p in pieces], axis=0)
    return jnp.pad(rows, ((0, PACK_ROWS - rows.shape[0]), (0, 0)))


def _unpack_rows(rows):
    out, off = {}, 0
    for name, axis, full in BIG:
        shp = _shard_shape(axis, full)
        n = shp[0] * shp[1] // PACK_W
        out[name] = rows[off : off + n].reshape(shp)
        off += n
    return out


SMALL = (
    ("mix_norm", 1024), ("conv_norm", 512), ("b_af", 256), ("b_ab", 256), ("gla_norm", 128), ("xa_norm", 1024), ("mem_norm", 1024),
    ("mlp_norm", 1024), ("final_norm", 1024), ("conv_w", 1536), ("w_af", 4096), ("w_ab", 4096), ("loss", 128),
)


def kernel(x, mem, mix_norm, w_in, conv_w, conv_norm, w_af, b_af, w_ab, b_ab, gla_norm, w_out, xa_norm, mem_norm, w_xq, w_xkv, w_xo, mlp_norm, w_up, w_down, final_norm, loss_target, m_mix_norm, m_w_in, m_conv_w, m_conv_norm, m_w_af, m_b_af, m_w_ab, m_b_ab, m_gla_norm, m_w_out, m_xa_norm, m_mem_norm, m_w_xq, m_w_xkv, m_w_xo, m_mlp_norm, m_w_up, m_w_down, m_final_norm, v_mix_norm, v_w_in, v_conv_w, v_conv_norm, v_w_af, v_b_af, v_w_ab, v_b_ab, v_gla_norm, v_w_out, v_xa_norm, v_mem_norm, v_w_xq, v_w_xkv, v_w_xo, v_mlp_norm, v_w_up, v_w_down, v_final_norm):
    given = dict(locals())
    xi, yi, ci = _place()
    chip = 2 * xi + yi
    where = jnp.stack([ci, chip]).astype(jnp.int32)

    shards = {name: given[name][0] for name, _, _ in BIG}
    gathered = _gather_weights(_pack_rows([shards[name].astype(_CD) for name, _, _ in BIG]))
    per_chip = [_unpack_rows(gathered[a]) for a in range(N_CHIPS)]
    wfull = {name: jnp.concatenate([per_chip[a][name] for a in range(N_CHIPS)], axis=axis) for name, axis, _ in BIG}
    w_in_p = jnp.pad(wfull["w_in"], ((0, 0), (0, Z_COLS - W_IN_COLS)))

    def placed(shard, full_shape, col):
        return lax.dynamic_update_slice(jnp.zeros(full_shape, F32), shard, (0, col)).reshape(-1, 128)

    sw = jnp.concatenate([
        placed(conv_w[0], (CONV_K, CONV_WIDTH), 128 * chip),
        placed(w_af[0], (GLA_LOWRANK, GLA_K_TOTAL), 64 * chip),
        placed(w_ab[0], (GLA_LOWRANK, GLA_K_TOTAL), 64 * chip),
    ], axis=0)
    sw = jnp.pad(sw, ((0, SMALL_ROWS - sw.shape[0]), (0, 0))) * (ci == 0).astype(F32)
    sw = _sum_small(sw)
    conv_w_full = sw[0:12].reshape(CONV_K, CONV_WIDTH)
    w_af_full = sw[12:44].reshape(GLA_LOWRANK, GLA_K_TOTAL)
    w_ab_full = sw[44:76].reshape(GLA_LOWRANK, GLA_K_TOTAL)
    waf_p = jnp.pad(w_af_full, ((0, 128 - GLA_LOWRANK), (0, 0))).astype(_CD)
    wab_p = jnp.pad(w_ab_full, ((GLA_LOWRANK, 128 - 2 * GLA_LOWRANK), (0, 0))).astype(_CD)

    xs, mems, tgt = x[0], mem[0], loss_target[0]
    add_res = lambda acc, res: (acc + res,)

    h1 = _rms_fwd(xs, mix_norm, name="norm_mix")
    z = _mm(h1, w_in_p, mode="nn", name="proj_in")
    b_f, b_b = _gate_fwd(z, waf_p, wab_p, b_af, b_ab, name="gates")
    o_f, st_f = _gla_fwd(z, b_f, rev=False, name="gla_scan_fwd")
    o_b, st_b = _gla_fwd(z, b_b, rev=True, name="gla_scan_rev")
    y = _mix_fwd(z, o_f, o_b, conv_w_full, conv_norm, gla_norm, name="mix_out")
    x1 = _mm(y, wfull["w_out"], mode="nn", name="proj_out", extras=(xs,), epilogue=add_res)
    hx = _rms_fwd(x1, xa_norm, name="norm_xa")
    qx = _mm(hx, wfull["w_xq"], mode="nn", name="proj_xq", out_dtypes=(_CD,))
    hmem = _rms_fwd(mems, mem_norm, name="norm_mem")
    kv = _mm(hmem, wfull["w_xkv"], mode="nn", name="proj_xkv", out_dtypes=(_CD,))
    ox = _xattn_fwd(qx, kv, name="xattn")
    x2 = _mm(ox, wfull["w_xo"], mode="nn", name="proj_xo", extras=(x1,), epilogue=add_res)
    hm = _rms_fwd(x2, mlp_norm, name="norm_mlp")
    u, act = _mm(hm, wfull["w_up"], mode="nn", name="mlp_up", out_dtypes=(F32, _CD),
                 epilogue=lambda acc: (acc, jnp.square(jnp.maximum(acc, 0.0))))
    x3 = _mm(act, wfull["w_down"], mode="nn", name="mlp_down", extras=(x2,), epilogue=add_res)
    dx3, dx3_lo, loss_part, g_final_norm = _final_loss(x3, final_norm.reshape(1, D_MODEL), tgt, name="loss_head")

    du = _mm(dx3_lo, wfull["w_down"], mode="nt", name="mlp_down_dx", out_dtypes=(_CD,), extras=(u,),
             epilogue=lambda acc, uu: (acc * (2.0 * jnp.maximum(uu, 0.0)),))
    g_w_down = _mm_tn(act, dx3_lo, name="mlp_down_dw")
    g_w_up = _mm_tn(hm, du, name="mlp_up_dw")
    dhm = _mm(du, wfull["w_up"], mode="nt", name="mlp_up_dx")
    dx2, dx2_lo, g_mlp_norm = _rms_bwd(x2, mlp_norm, dhm, dx3, name="norm_mlp_bwd")
    dox = _mm(dx2_lo, wfull["w_xo"], mode="nt", name="proj_xo_dx", out_dtypes=(_CD,))
    g_w_xo = _mm_tn(ox, dx2_lo, name="proj_xo_dw")
    dqx, dkv = _xattn_bwd(qx, kv, dox, name="xattn_bwd")
    g_w_xq = _mm_tn(hx, dqx, name="proj_xq_dw")
    dhx = _mm(dqx, wfull["w_xq"], mode="nt", name="proj_xq_dx")
    dx1, dx1_lo, g_xa_norm = _rms_bwd(x1, xa_norm, dhx, dx2, name="norm_xa_bwd")
    dkv_lo = dkv.astype(_CD)
    g_w_xkv = _mm_tn(hmem, dkv_lo, name="proj_xkv_dw")
    dhmem = _mm(dkv_lo, wfull["w_xkv"], mode="nt", name="proj_xkv_dx")
    (g_mem_norm,) = _rms_bwd(mems, mem_norm, dhmem, name="norm_mem_bwd", want_dx=False, want_lo=False)
    dy = _mm(dx1_lo, wfull["w_out"], mode="nt", name="proj_out_dx")
    g_w_out = _mm_tn(y, dx1_lo, name="proj_out_dw")
    dz_conv, dz_gate, do, g_conv_w, g_conv_norm, g_gla_norm = _mix_bwd(z, o_f, o_b, dy, conv_w_full, conv_norm, gla_norm, name="mix_out_bwd")
    dqkv_f, db_f = _gla_bwd(z, b_f, do, st_f, rev=False, name="gla_scan_fwd_bwd")
    dqkv_b, db_b = _gla_bwd(z, b_b, do, st_b, rev=True, name="gla_scan_rev_bwd")
    dqkv, dlr, g_waf_p, g_wab_p, g_b_af, g_b_ab = _gate_bwd(z, waf_p, wab_p, b_af, b_ab, db_f, db_b, dqkv_f, dqkv_b, name="gates_bwd")
    dz = jnp.concatenate([dz_conv, dqkv, dz_gate, dlr], axis=1)
    g_w_in = _mm_tn(h1, dz, name="proj_in_dw")[:, :W_IN_COLS]
    dh1 = _mm(dz, w_in_p, mode="nt", name="proj_in_dx")
    grad_x, g_mix_norm = _rms_bwd(xs, mix_norm, dh1, dx1, name="norm_mix_bwd", want_lo=False)

    g_full = dict(w_in=g_w_in, w_out=g_w_out, w_xq=g_w_xq, w_xkv=g_w_xkv, w_xo=g_w_xo, w_up=g_w_up, w_down=g_w_down)

    def shard_of(name, axis, a):
        size = g_full[name].shape[axis] // N_CHIPS
        return lax.slice_in_dim(g_full[name], a * size, (a + 1) * size, axis=axis)

    gpack = jnp.stack([_pack_rows([shard_of(name, axis, a) for name, axis, _ in BIG]) for a in range(N_CHIPS)])
    got = _swap_halves(gpack)
    others = _exchange_chip_sums(_chip_sums(gpack, got, where))
    g_shard = _unpack_rows(_join_halves(_shard_sum(gpack, got, others, where)))

    small_vals = dict(mix_norm=g_mix_norm, conv_norm=g_conv_norm, b_af=g_b_af, b_ab=g_b_ab, gla_norm=g_gla_norm, xa_norm=g_xa_norm,
                      mem_norm=g_mem_norm, mlp_norm=g_mlp_norm, final_norm=g_final_norm, conv_w=g_conv_w,
                      w_af=g_waf_p[0:GLA_LOWRANK], w_ab=g_wab_p[GLA_LOWRANK : 2 * GLA_LOWRANK], loss=loss_part)
    small = jnp.concatenate([small_vals[name].reshape(-1, 128) for name, _ in SMALL], axis=0)
    small = _sum_small(jnp.pad(small, ((0, SMALL_ROWS - small.shape[0]), (0, 0))))
    g_small, off = {}, 0
    for name, n in SMALL:
        g_small[name] = small[off : off + n // 128]
        off += n // 128
    loss = g_small["loss"][0, 0]
    g_small["conv_w"] = lax.dynamic_slice(g_small["conv_w"].reshape(CONV_K, CONV_WIDTH), (0, 128 * chip), (CONV_K, 128))
    g_small["w_af"] = lax.dynamic_slice(g_small["w_af"].reshape(GLA_LOWRANK, GLA_K_TOTAL), (0, 64 * chip), (GLA_LOWRANK, 64))
    g_small["w_ab"] = lax.dynamic_slice(g_small["w_ab"].reshape(GLA_LOWRANK, GLA_K_TOTAL), (0, 64 * chip), (GLA_LOWRANK, 64))

    names = ["mix_norm", "w_in", "conv_w", "conv_norm", "w_af", "b_af", "w_ab", "b_ab", "gla_norm", "w_out", "xa_norm", "mem_norm",
             "w_xq", "w_xkv", "w_xo", "mlp_norm", "w_up", "w_down", "final_norm"]
    big_names = [name for name, _, _ in BIG]
    as2d = lambda a: a.reshape(1, -1) if a.ndim == 1 else a.reshape(a.shape[-2:])
    grads, deltas, new_m, new_v = {}, {}, {}, {}
    for name in big_names:
        grads[name] = g_shard[name]
        deltas[name], new_m[name], new_v[name] = _adamw(as2d(given[name]), g_shard[name], as2d(given["m_" + name]),
                                                         as2d(given["v_" + name]), name="adamw_" + name)
    small_names = [name for name in names if name not in big_names]
    groups = []
    for name in small_names:
        grads[name] = g_small[name].reshape(as2d(given[name]).shape)
        groups.append((as2d(given[name]), grads[name], as2d(given["m_" + name]), as2d(given["v_" + name])))
    for name, res in zip(small_names, _adamw_small(groups, name="adamw_small")):
        deltas[name], new_m[name], new_v[name] = res

    like = lambda name, a: a.reshape(given[name].shape)
    return (loss, grad_x[None], *[like(n, grads[n]) for n in names], *[like(n, deltas[n]) for n in names],
            *[like(n, new_m[n]) for n in names], *[like(n, new_v[n]) for n in names])
```

```python
import functools

import jax
import jax.numpy as jnp
from jax import lax
from jax.experimental import pallas as pl
from jax.experimental.pallas import tpu as pltpu

F32 = jnp.float32
BF16 = jnp.bfloat16
_CD = jnp.bfloat16
_TD = jnp.bfloat16

D_MODEL = 1024
N_MEM = 256
CONV_WIDTH = 512
CONV_GROUP = 64
CONV_K = 3
GLA_HEADS = 4
GLA_DK = 64
GLA_DV = 128
GLA_K_TOTAL = 256
GLA_V_TOTAL = 512
GLA_LOWRANK = 16
GLA_GATE_SCALE = 1.0 / 16.0
GLA_CHUNK = 64
XA_HEADS = 4
XA_HEAD_DIM = 256
D_FF = 4096
EPS = 1e-6
W_IN_COLS = 3104
Z_COLS = 3200
LR_COL = 3072

ADAM_LR = 0.001
ADAM_B1 = 0.9
ADAM_B2 = 0.999
ADAM_EPS = 1e-08
ADAM_WD = 0.01
ADAM_STEP = 10

N_CHIPS = 4
PACK_W = 1024
PACK_ROWS = 4160
PACK_TILE = 160
SMALL_ROWS = 128

_TS = 512
_VMEM = 44 * 1024 * 1024
MESH = pl.DeviceIdType.MESH
ANY = pl.BlockSpec(memory_space=pl.ANY)


def _cp(sem=None, **kw):
    return pltpu.CompilerParams(dimension_semantics=sem, vmem_limit_bytes=_VMEM, **kw)


def _dot(a, b):
    return jnp.dot(a.astype(_CD), b.astype(_CD), preferred_element_type=F32)


def _dot_nt(a, b):
    return lax.dot_general(a.astype(_CD), b.astype(_CD), (((1,), (1,)), ((), ())), preferred_element_type=F32)


def _dot_tn(a, b):
    return lax.dot_general(a.astype(_CD), b.astype(_CD), (((0,), (0,)), ((), ())), preferred_element_type=F32)


def _dot_split(x, ones):
    hi = x.astype(BF16)
    r = x - hi.astype(F32)
    mid = r.astype(BF16)
    lo = (r - mid.astype(F32)).astype(BF16)
    d = lambda p: jnp.dot(p, ones, preferred_element_type=F32)
    return d(hi) + d(mid) + d(lo)


def _pick(n, cands=(1024, 640, 512, 256, 128)):
    for t in cands:
        if n % t == 0:
            return t
    return n


def _rows(s):
    return min(_TS, s)


def _sigmoid(v):
    e = jnp.exp(-jnp.abs(v))
    return jnp.where(v >= 0, 1.0 / (1.0 + e), e / (1.0 + e))


def _mm(a, b, *, mode, name, out_dtypes=(F32,), extras=(), epilogue=None, tm=None, tn=None, tk=None):
    m, k = a.shape
    n = b.shape[1] if mode == "nn" else b.shape[0]
    tm = tm or min(m, 1024)
    tn = tn or _pick(n)
    tk = tk or _pick(k)
    nk = k // tk
    n_ex, n_out = len(extras), len(out_dtypes)

    def body(*refs):
        a_ref, b_ref = refs[:2]
        ex = refs[2 : 2 + n_ex]
        outs = refs[2 + n_ex : 2 + n_ex + n_out]
        part = _dot(a_ref[...], b_ref[...]) if mode == "nn" else _dot_nt(a_ref[...], b_ref[...])

        def finish(acc):
            res = epilogue(acc, *[e[...] for e in ex]) if epilogue else (acc,)
            for o, r in zip(outs, res):
                o[...] = r.astype(o.dtype)

        if nk == 1:
            finish(part)
        else:
            acc_ref = refs[-1]
            kk = pl.program_id(2)

            @pl.when(kk == 0)
            def _():
                acc_ref[...] = part

            @pl.when(kk > 0)
            def _():
                acc_ref[...] += part

            @pl.when(kk == nk - 1)
            def _():
                finish(acc_ref[...])

    b_spec = pl.BlockSpec((tk, tn), lambda i, j, kk: (kk, j)) if mode == "nn" else pl.BlockSpec((tn, tk), lambda i, j, kk: (j, kk))
    tile = pl.BlockSpec((tm, tn), lambda i, j, kk: (i, j))
    out = pl.pallas_call(
        body,
        name=name,
        grid=(m // tm, n // tn, nk),
        in_specs=[pl.BlockSpec((tm, tk), lambda i, j, kk: (i, kk)), b_spec] + [tile] * n_ex,
        out_specs=[tile] * n_out,
        out_shape=[jax.ShapeDtypeStruct((m, n), dt) for dt in out_dtypes],
        scratch_shapes=[pltpu.VMEM((tm, tn), F32)] if nk > 1 else [],
        compiler_params=_cp(("parallel", "parallel", "arbitrary")),
    )(a, b, *extras)
    return out[0] if n_out == 1 else out


def _mm_tn(a, b, *, name):
    s, m = a.shape
    n = b.shape[1]
    tm = min(m, 512, max(128, 1 << (((1 << 20) // n).bit_length() - 1)))
    ts = min(s, 1 << (((1 << 22) // n).bit_length() - 1))
    ns = s // ts

    def body(a_ref, b_ref, o_ref):
        part = _dot_tn(a_ref[...], b_ref[...])
        if ns == 1:
            o_ref[...] = part
        else:
            ss = pl.program_id(1)

            @pl.when(ss == 0)
            def _():
                o_ref[...] = part

            @pl.when(ss > 0)
            def _():
                o_ref[...] += part

    return pl.pallas_call(
        body,
        name=name,
        grid=(m // tm, ns),
        in_specs=[pl.BlockSpec((ts, tm), lambda i, ss: (ss, i)), pl.BlockSpec((ts, n), lambda i, ss: (ss, 0))],
        out_specs=pl.BlockSpec((tm, n), lambda i, ss: (i, 0)),
        out_shape=jax.ShapeDtypeStruct((m, n), F32),
        compiler_params=_cp(("parallel", "arbitrary")),
    )(a, b)


def _rms_fwd(x, g, *, name):
    s, d = x.shape
    ts = _rows(s)

    def body(x_ref, g_ref, o_ref):
        xf = x_ref[...]
        r = lax.rsqrt(jnp.mean(xf * xf, axis=-1, keepdims=True) + EPS)
        o_ref[...] = (xf * r * g_ref[...]).astype(o_ref.dtype)

    return pl.pallas_call(
        body,
        name=name,
        grid=(s // ts,),
        in_specs=[pl.BlockSpec((ts, d), lambda i: (i, 0)), pl.BlockSpec((1, d), lambda i: (0, 0))],
        out_specs=pl.BlockSpec((ts, d), lambda i: (i, 0)),
        out_shape=jax.ShapeDtypeStruct((s, d), _CD),
        compiler_params=_cp(("parallel",)),
    )(x, g)


def _rms_bwd(x, g, dy, dres=None, *, name, want_dx=True, want_lo=True):
    s, d = x.shape
    ts = _rows(s)
    has_res = dres is not None

    def body(*refs):
        x_ref, g_ref, dy_ref = refs[:3]
        pos = 3
        dres_ref = refs[pos] if has_res else None
        pos += has_res
        dx_ref = refs[pos] if want_dx else None
        pos += want_dx
        lo_ref = refs[pos] if want_lo else None
        pos += want_lo
        dg_ref = refs[pos]
        xf = x_ref[...]
        r = lax.rsqrt(jnp.mean(xf * xf, axis=-1, keepdims=True) + EPS)
        xh = xf * r
        dyf = dy_ref[...]
        part = jnp.sum(dyf * xh, axis=0, keepdims=True)

        @pl.when(pl.program_id(0) == 0)
        def _():
            dg_ref[...] = part

        @pl.when(pl.program_id(0) > 0)
        def _():
            dg_ref[...] += part

        if want_dx or want_lo:
            dxh = dyf * g_ref[...]
            dx = r * (dxh - xh * jnp.mean(dxh * xh, axis=-1, keepdims=True))
            if has_res:
                dx = dx + dres_ref[...]
            if want_dx:
                dx_ref[...] = dx
            if want_lo:
                lo_ref[...] = dx.astype(lo_ref.dtype)

    tile = pl.BlockSpec((ts, d), lambda i: (i, 0))
    vec = pl.BlockSpec((1, d), lambda i: (0, 0))
    out_specs, out_shape = [], []
    if want_dx:
        out_specs.append(tile)
        out_shape.append(jax.ShapeDtypeStruct((s, d), F32))
    if want_lo:
        out_specs.append(tile)
        out_shape.append(jax.ShapeDtypeStruct((s, d), _CD))
    out_specs.append(vec)
    out_shape.append(jax.ShapeDtypeStruct((1, d), F32))
    return pl.pallas_call(
        body,
        name=name,
        grid=(s // ts,),
        in_specs=[tile, vec, tile] + ([tile] if has_res else []),
        out_specs=out_specs,
        out_shape=out_shape,
        compiler_params=_cp(("arbitrary",)),
    )(x, g, dy, *([dres] if has_res else []))


def _final_loss(x3, g, tgt, *, name):
    s, d = x3.shape
    ts = _rows(s)

    def body(x_ref, g_ref, t_ref, dx_ref, lo_ref, loss_ref, dg_ref):
        xf = x_ref[...]
        r = lax.rsqrt(jnp.mean(xf * xf, axis=-1, keepdims=True) + EPS)
        xh = xf * r
        gg = g_ref[...]
        err = xh * gg - t_ref[...]
        lpart = jnp.zeros((1, 128), F32) + 0.5 * jnp.sum(jnp.mean(err * err, axis=-1, keepdims=True))
        dy = err * (1.0 / d)
        gpart = jnp.sum(dy * xh, axis=0, keepdims=True)

        @pl.when(pl.program_id(0) == 0)
        def _():
            loss_ref[...] = lpart
            dg_ref[...] = gpart

        @pl.when(pl.program_id(0) > 0)
        def _():
            loss_ref[...] += lpart
            dg_ref[...] += gpart

        dxh = dy * gg
        dx = r * (dxh - xh * jnp.mean(dxh * xh, axis=-1, keepdims=True))
        dx_ref[...] = dx
        lo_ref[...] = dx.astype(lo_ref.dtype)

    tile = pl.BlockSpec((ts, d), lambda i: (i, 0))
    vec = pl.BlockSpec((1, d), lambda i: (0, 0))
    return pl.pallas_call(
        body,
        name=name,
        grid=(s // ts,),
        in_specs=[tile, vec, tile],
        out_specs=[tile, tile, pl.BlockSpec((1, 128), lambda i: (0, 0)), vec],
        out_shape=[
            jax.ShapeDtypeStruct((s, d), F32),
            jax.ShapeDtypeStruct((s, d), _CD),
            jax.ShapeDtypeStruct((1, 128), F32),
            jax.ShapeDtypeStruct((1, d), F32),
        ],
        compiler_params=_cp(("arbitrary",)),
    )(x3, g, tgt)


def _chunk_scan(v, row_in_chunk, suffix):
    t = v.shape[0]
    step = 1
    while step < GLA_CHUNK:
        if suffix:
            v = v + jnp.where(row_in_chunk < GLA_CHUNK - step, pltpu.roll(v, t - step, 0), 0.0)
        else:
            v = v + jnp.where(row_in_chunk >= step, pltpu.roll(v, step, 0), 0.0)
        step *= 2
    return v


def _gate_pre(lr, w_ref, b_ref):
    return _dot(lr, w_ref[...]) + b_ref[...]


def _gate_fwd(z, waf, wab, baf, bab, *, name):
    s = z.shape[0]
    ts = _rows(s)

    def body(lr_ref, waf_ref, wab_ref, baf_ref, bab_ref, bf_ref, bb_ref):
        lr = lr_ref[...]
        ric = lax.broadcasted_iota(jnp.int32, (ts, GLA_K_TOTAL), 0) & (GLA_CHUNK - 1)
        for w_ref, b_ref, o_ref, suffix in ((waf_ref, baf_ref, bf_ref, False), (wab_ref, bab_ref, bb_ref, True)):
            pre = _gate_pre(lr, w_ref, b_ref)
            la = (jnp.minimum(pre, 0.0) - jnp.log(1.0 + jnp.exp(-jnp.abs(pre)))) * GLA_GATE_SCALE
            o_ref[...] = _chunk_scan(la, ric, suffix)

    wspec = pl.BlockSpec((128, GLA_K_TOTAL), lambda i: (0, 0))
    bspec = pl.BlockSpec((1, GLA_K_TOTAL), lambda i: (0, 0))
    tile = pl.BlockSpec((ts, GLA_K_TOTAL), lambda i: (i, 0))
    return pl.pallas_call(
        body,
        name=name,
        grid=(s // ts,),
        in_specs=[pl.BlockSpec((ts, 128), lambda i: (i, LR_COL // 128)), wspec, wspec, bspec, bspec],
        out_specs=[tile, tile],
        out_shape=[jax.ShapeDtypeStruct((s, GLA_K_TOTAL), F32)] * 2,
        compiler_params=_cp(("parallel",)),
    )(z, waf, wab, baf, bab)


def _gate_bwd(z, waf, wab, baf, bab, dbf, dbb, dqkv_f, dqkv_b, *, name):
    s = z.shape[0]
    ts = _rows(s)

    def body(lr_ref, waf_ref, wab_ref, baf_ref, bab_ref, dbf_ref, dbb_ref, gf_ref, gb_ref, dqkv_ref, dlr_ref, dwf_ref, dwb_ref, dbaf_ref, dbab_ref):
        lr = lr_ref[...]
        ric = lax.broadcasted_iota(jnp.int32, (ts, GLA_K_TOTAL), 0) & (GLA_CHUNK - 1)
        first = pl.program_id(0) == 0
        dlr = None
        for w_ref, b_ref, db_ref, dw_ref, dbias_ref, suffix in (
            (waf_ref, baf_ref, dbf_ref, dwf_ref, dbaf_ref, True),
            (wab_ref, bab_ref, dbb_ref, dwb_ref, dbab_ref, False),
        ):
            pre = _gate_pre(lr, w_ref, b_ref)
            dla = _chunk_scan(db_ref[...], ric, suffix)
            dpre = dla * GLA_GATE_SCALE * _sigmoid(-pre)
            part = _dot_nt(dpre, w_ref[...])
            dlr = part if dlr is None else dlr + part
            dw = _dot_tn(lr, dpre)
            dbias = jnp.sum(dpre, axis=0, keepdims=True)

            @pl.when(first)
            def _():
                dw_ref[...] = dw
                dbias_ref[...] = dbias

            @pl.when(jnp.logical_not(first))
            def _():
                dw_ref[...] += dw
                dbias_ref[...] += dbias

        dlr_ref[...] = dlr.astype(dlr_ref.dtype)
        dqkv_ref[...] = (gf_ref[...] + gb_ref[...]).astype(dqkv_ref.dtype)

    wspec = pl.BlockSpec((128, GLA_K_TOTAL), lambda i: (0, 0))
    bspec = pl.BlockSpec((1, GLA_K_TOTAL), lambda i: (0, 0))
    tile = pl.BlockSpec((ts, GLA_K_TOTAL), lambda i: (i, 0))
    wide = pl.BlockSpec((ts, 2 * GLA_K_TOTAL + GLA_V_TOTAL), lambda i: (i, 0))
    return pl.pallas_call(
        body,
        name=name,
        grid=(s // ts,),
        in_specs=[pl.BlockSpec((ts, 128), lambda i: (i, LR_COL // 128)), wspec, wspec, bspec, bspec, tile, tile, wide, wide],
        out_specs=[wide, pl.BlockSpec((ts, 128), lambda i: (i, 0)), wspec, wspec, bspec, bspec],
        out_shape=[
            jax.ShapeDtypeStruct((s, 2 * GLA_K_TOTAL + GLA_V_TOTAL), _CD),
            jax.ShapeDtypeStruct((s, 128), _CD),
            jax.ShapeDtypeStruct((128, GLA_K_TOTAL), F32),
            jax.ShapeDtypeStruct((128, GLA_K_TOTAL), F32),
            jax.ShapeDtypeStruct((1, GLA_K_TOTAL), F32),
            jax.ShapeDtypeStruct((1, GLA_K_TOTAL), F32),
        ],
        compiler_params=_cp(("arbitrary",)),
    )(z, waf, wab, baf, bab, dbf, dbb, dqkv_f, dqkv_b)


def _gla_masks(rev):
    lane_head = lax.broadcasted_iota(jnp.int32, (1, GLA_K_TOTAL), 1) >> 6
    head_masks = [lane_head == h for h in range(GLA_HEADS)]
    st_rows = lax.broadcasted_iota(jnp.int32, (GLA_V_TOTAL, GLA_K_TOTAL), 0) >> 7
    st_lanes = lax.broadcasted_iota(jnp.int32, (GLA_V_TOTAL, GLA_K_TOTAL), 1) >> 6
    block_mask = st_rows == st_lanes
    t = lax.broadcasted_iota(jnp.int32, (GLA_HEADS * GLA_CHUNK, GLA_CHUNK), 0) & (GLA_CHUNK - 1)
    u = lax.broadcasted_iota(jnp.int32, (GLA_HEADS * GLA_CHUNK, GLA_CHUNK), 1)
    tri = (u > t) if rev else (u <= t)
    row = lax.broadcasted_iota(jnp.int32, (GLA_CHUNK, GLA_K_TOTAL), 0)
    total_row = row == (0 if rev else GLA_CHUNK - 1)
    return head_masks, block_mask, tri, total_row


def _gla_chunk_terms(q_ref, k_ref, v_ref, b_ref, rows, head_masks, tri, total_row):
    q = q_ref[rows, :] * (GLA_DK**-0.5)
    k = k_ref[rows, :]
    v = v_ref[rows, :]
    b = b_ref[rows, :]
    eb = jnp.exp(b)
    enb = jnp.exp(-b)
    g = jnp.sum(jnp.where(total_row, b, 0.0), axis=0, keepdims=True)
    egb = jnp.exp(g - b)
    qt = q * eb
    kt = k * enb
    kh = k * egb
    q_heads = jnp.concatenate([jnp.where(m, qt, 0.0) for m in head_masks], axis=0)
    attn = jnp.where(tri, _dot_nt(q_heads, kt), 0.0)
    return v, eb, enb, egb, jnp.exp(g), qt, kt, kh, attn


def _gla_specs(s, tb, rev_blocks):
    nb = s // tb
    rb = (lambda i: nb - 1 - i) if rev_blocks else (lambda i: i)
    q_spec = pl.BlockSpec((tb, GLA_K_TOTAL), lambda i: (rb(i), 1536 // GLA_K_TOTAL))
    k_spec = pl.BlockSpec((tb, GLA_K_TOTAL), lambda i: (rb(i), 1792 // GLA_K_TOTAL))
    v_spec = pl.BlockSpec((tb, GLA_V_TOTAL), lambda i: (rb(i), 2048 // GLA_V_TOTAL))
    b_spec = pl.BlockSpec((tb, GLA_K_TOTAL), lambda i: (rb(i), 0))
    o_spec = pl.BlockSpec((tb, GLA_V_TOTAL), lambda i: (rb(i), 0))
    st_spec = pl.BlockSpec((tb // GLA_CHUNK, GLA_DV, GLA_K_TOTAL), lambda i: (rb(i), 0, 0))
    return nb, q_spec, k_spec, v_spec, b_spec, o_spec, st_spec


def _gla_fwd(z, b, *, rev, name):
    s = z.shape[0]
    tb = _rows(s)
    cpb = tb // GLA_CHUNK
    nb, q_spec, k_spec, v_spec, b_spec, o_spec, st_spec = _gla_specs(s, tb, rev)

    def body(q_ref, k_ref, v_ref, b_ref, o_ref, sv_ref, st_ref):
        head_masks, block_mask, tri, total_row = _gla_masks(rev)

        @pl.when(pl.program_id(0) == 0)
        def _():
            st_ref[...] = jnp.zeros_like(st_ref)

        def chunk(ci, carry):
            cidx = cpb - 1 - ci if rev else ci
            rows = pl.ds(pl.multiple_of(cidx * GLA_CHUNK, GLA_CHUNK), GLA_CHUNK)
            v, _, _, _, eg, qt, _, kh, attn = _gla_chunk_terms(q_ref, k_ref, v_ref, b_ref, rows, head_masks, tri, total_row)
            o = jnp.concatenate(
                [_dot(attn[GLA_CHUNK * h : GLA_CHUNK * (h + 1)], v[:, GLA_DV * h : GLA_DV * (h + 1)]) for h in range(GLA_HEADS)], axis=1
            )
            st = st_ref[...]
            o_ref[rows, :] = o + _dot_nt(qt, st)
            sv_ref[cidx] = st[0:128] + st[128:256] + st[256:384] + st[384:512]
            st_ref[...] = st * eg + jnp.where(block_mask, _dot_tn(v, kh), 0.0)
            return carry

        lax.fori_loop(0, cpb, chunk, 0)

    return pl.pallas_call(
        body,
        name=name,
        grid=(nb,),
        in_specs=[q_spec, k_spec, v_spec, b_spec],
        out_specs=[o_spec, st_spec],
        out_shape=[jax.ShapeDtypeStruct((s, GLA_V_TOTAL), F32), jax.ShapeDtypeStruct((s // GLA_CHUNK, GLA_DV, GLA_K_TOTAL), F32)],
        scratch_shapes=[pltpu.VMEM((GLA_V_TOTAL, GLA_K_TOTAL), F32)],
        compiler_params=_cp(("arbitrary",)),
    )(z, z, z, b)


def _gla_bwd(z, b, do, states, *, rev, name):
    s = z.shape[0]
    tb = _rows(s)
    cpb = tb // GLA_CHUNK
    nb, q_spec, k_spec, v_spec, b_spec, o_spec, st_spec = _gla_specs(s, tb, not rev)
    rb = (lambda i: nb - 1 - i) if not rev else (lambda i: i)

    def body(q_ref, k_ref, v_ref, b_ref, do_ref, sv_ref, dqkv_ref, db_ref, dst_ref):
        head_masks, block_mask, tri, total_row = _gla_masks(rev)

        @pl.when(pl.program_id(0) == 0)
        def _():
            dst_ref[...] = jnp.zeros_like(dst_ref)

        def chunk(ci, carry):
            cidx = ci if rev else cpb - 1 - ci
            rows = pl.ds(pl.multiple_of(cidx * GLA_CHUNK, GLA_CHUNK), GLA_CHUNK)
            v, eb, enb, egb, eg, qt, kt, kh, attn = _gla_chunk_terms(q_ref, k_ref, v_ref, b_ref, rows, head_masks, tri, total_row)
            do_c = do_ref[rows, :]
            saved = sv_ref[cidx]
            st = jnp.where(block_mask, jnp.concatenate([saved] * GLA_HEADS, axis=0), 0.0)
            dst = dst_ref[...]
            hs = lambda a, h: a[GLA_CHUNK * h : GLA_CHUNK * (h + 1)]
            vs = lambda a, h: a[:, GLA_DV * h : GLA_DV * (h + 1)]
            dattn = jnp.concatenate([_dot_nt(vs(do_c, h), vs(v, h)) for h in range(GLA_HEADS)], axis=0)
            dattn = jnp.where(tri, dattn, 0.0)
            dv = jnp.concatenate([_dot_tn(hs(attn, h), vs(do_c, h)) for h in range(GLA_HEADS)], axis=1) + _dot_nt(kh, dst)
            dqt = _dot(do_c, st)
            dkt = jnp.zeros_like(dqt)
            for h in range(GLA_HEADS):
                dqt = dqt + jnp.where(head_masks[h], _dot(hs(dattn, h), kt), 0.0)
                dkt = dkt + jnp.where(head_masks[h], _dot_tn(hs(dattn, h), qt), 0.0)
            dkh = _dot(v, dst)
            dg = jnp.sum(dkh * kh, axis=0, keepdims=True) + jnp.sum(dst * st, axis=0, keepdims=True) * eg
            db = dqt * qt - dkt * kt - dkh * kh + jnp.where(total_row, dg, 0.0)
            dq = dqt * eb * (GLA_DK**-0.5)
            dk = dkt * enb + dkh * egb
            dqkv_ref[rows, :] = jnp.concatenate([dq, dk, dv], axis=1)
            db_ref[rows, :] = db
            dst_ref[...] = dst * eg + jnp.where(block_mask, _dot_tn(do_c, qt), 0.0)
            return carry

        lax.fori_loop(0, cpb, chunk, 0)

    wide = 2 * GLA_K_TOTAL + GLA_V_TOTAL
    return pl.pallas_call(
        body,
        name=name,
        grid=(nb,),
        in_specs=[q_spec, k_spec, v_spec, b_spec, o_spec, st_spec],
        out_specs=[pl.BlockSpec((tb, wide), lambda i: (rb(i), 0)), b_spec],
        out_shape=[jax.ShapeDtypeStruct((s, wide), F32), jax.ShapeDtypeStruct((s, GLA_K_TOTAL), F32)],
        scratch_shapes=[pltpu.VMEM((GLA_V_TOTAL, GLA_K_TOTAL), F32)],
        compiler_params=_cp(("arbitrary",)),
    )(z, z, z, b, do, states)


HALO = 8


def _halo_specs(s, ts, width, col):
    last = s // HALO - 1
    per = ts // HALO
    prev = pl.BlockSpec((HALO, width), lambda i: (jnp.maximum(i * per - 1, 0), col))
    nxt = pl.BlockSpec((HALO, width), lambda i: (jnp.minimum((i + 1) * per, last), col))
    return prev, nxt


def _group_ones():
    r = lax.broadcasted_iota(jnp.int32, (CONV_WIDTH, CONV_WIDTH), 0) >> 6
    c = lax.broadcasted_iota(jnp.int32, (CONV_WIDTH, CONV_WIDTH), 1) >> 6
    return (r == c).astype(BF16)


def _conv_terms(cc_ext, cu_ext, cw, valid):
    n = cc_ext.shape[0]
    hc = jnp.where(valid, cc_ext * cu_ext, 0.0)
    hc_prev = pltpu.roll(hc, 1, 0)
    hc_next = pltpu.roll(hc, n - 1, 0)
    conv = cw[0:1] * hc_prev + cw[1:2] * hc + cw[2:3] * hc_next
    return hc, hc_prev, hc_next, conv


def _ext(prev_ref, cur_ref, next_ref):
    return jnp.concatenate([prev_ref[...], cur_ref[...], next_ref[...]], axis=0)


def _valid_rows(ts, s):
    row = lax.broadcasted_iota(jnp.int32, (ts + 2 * HALO, 1), 0) + (pl.program_id(0) * ts - HALO)
    return (row >= 0) & (row < s)


def _head_norm(o, gn):
    out = []
    for h in range(GLA_HEADS):
        oh = o[:, GLA_DV * h : GLA_DV * (h + 1)]
        r = lax.rsqrt(jnp.mean(oh * oh, axis=-1, keepdims=True) + EPS)
        out.append((oh * r, r))
    return out


def _mix_fwd(z, o_f, o_b, conv_w, conv_norm, gla_norm, *, name):
    s = z.shape[0]
    ts = _rows(s)
    cprev, cnext = _halo_specs(s, ts, CONV_WIDTH, 1)
    uprev, unext = _halo_specs(s, ts, CONV_WIDTH, 2)

    def body(cb_ref, cc_ref, cu_ref, ccp_ref, ccn_ref, cup_ref, cun_ref, g_ref, of_ref, ob_ref, cw_ref, cn_ref, gn_ref, y_ref):
        valid = _valid_rows(ts, s)
        _, _, _, conv = _conv_terms(_ext(ccp_ref, cc_ref, ccn_ref), _ext(cup_ref, cu_ref, cun_ref), cw_ref[...], valid)
        yc = cb_ref[...] * conv[HALO : HALO + ts]
        ms = _dot_split(yc * yc, _group_ones()) * (1.0 / CONV_GROUP)
        y_conv = yc * lax.rsqrt(ms + EPS) * cn_ref[...]
        gate = g_ref[...]
        silu = gate * _sigmoid(gate)
        gn = gn_ref[...]
        y_gla = jnp.concatenate([oh * gn for oh, _ in _head_norm(of_ref[...] + ob_ref[...], gn)], axis=1) * silu
        y_ref[...] = jnp.concatenate([y_conv, y_gla], axis=1).astype(y_ref.dtype)

    col = lambda c, w=CONV_WIDTH: pl.BlockSpec((ts, w), lambda i: (i, c))
    return pl.pallas_call(
        body,
        name=name,
        grid=(s // ts,),
        in_specs=[col(0), col(1), col(2), cprev, cnext, uprev, unext, col(5), col(0), col(0),
                  pl.BlockSpec((CONV_K, CONV_WIDTH), lambda i: (0, 0)), pl.BlockSpec((1, CONV_WIDTH), lambda i: (0, 0)),
                  pl.BlockSpec((1, GLA_DV), lambda i: (0, 0))],
        out_specs=pl.BlockSpec((ts, D_MODEL), lambda i: (i, 0)),
        out_shape=jax.ShapeDtypeStruct((s, D_MODEL), _CD),
        compiler_params=_cp(("parallel",)),
    )(z, z, z, z, z, z, z, z, o_f, o_b, conv_w, conv_norm, gla_norm)


def _mix_bwd(z, o_f, o_b, dy, conv_w, conv_norm, gla_norm, *, name):
    s = z.shape[0]
    ts = _rows(s)
    halos = [_halo_specs(s, ts, CONV_WIDTH, c) for c in (0, 1, 2)]
    dprev, dnext = _halo_specs(s, ts, CONV_WIDTH, 0)

    def body(cb_ref, cc_ref, cu_ref, cbp_ref, cbn_ref, ccp_ref, ccn_ref, cup_ref, cun_ref, g_ref, of_ref, ob_ref,
             dyc_ref, dyg_ref, dyp_ref, dyn_ref, cw_ref, cn_ref, gn_ref, dconv_ref, dgate_ref, do_ref, dcw_ref, dcn_ref, dgn_ref):
        n = ts + 2 * HALO
        valid = _valid_rows(ts, s)
        cw = cw_ref[...]
        cn = cn_ref[...]
        ones = _group_ones()
        cb = _ext(cbp_ref, cb_ref, cbn_ref)
        cc = _ext(ccp_ref, cc_ref, ccn_ref)
        cu = _ext(cup_ref, cu_ref, cun_ref)
        dy = _ext(dyp_ref, dyc_ref, dyn_ref)
        hc, hc_prev, hc_next, conv = _conv_terms(cc, cu, cw, valid)
        yc = cb * conv
        r = lax.rsqrt(_dot_split(yc * yc, ones) * (1.0 / CONV_GROUP) + EPS)
        yh = yc * r
        dyh = dy * cn
        dyc = r * (dyh - yh * (_dot_split(dyh * yh, ones) * (1.0 / CONV_GROUP)))
        dconv = jnp.where(valid, dyc * cb, 0.0)
        dhc = cw[0:1] * pltpu.roll(dconv, n - 1, 0) + cw[1:2] * dconv + cw[2:3] * pltpu.roll(dconv, 1, 0)
        mid = lambda a: a[HALO : HALO + ts]
        dconv_ref[...] = jnp.concatenate([mid(dyc * conv), mid(dhc * cu), mid(dhc * cc)], axis=1).astype(dconv_ref.dtype)
        dconv_m = mid(dconv)
        colsum = lambda a: jnp.sum(a, axis=0, keepdims=True)
        dcw = jnp.concatenate([colsum(dconv_m * mid(hc_prev)), colsum(dconv_m * mid(hc)), colsum(dconv_m * mid(hc_next))], axis=0)
        dcn = colsum(mid(dy * yh))

        gate = g_ref[...]
        sg = _sigmoid(gate)
        silu = gate * sg
        gn = gn_ref[...]
        dyg = dyg_ref[...]
        don = dyg * silu
        heads = _head_norm(of_ref[...] + ob_ref[...], gn)
        on = jnp.concatenate([oh * gn for oh, _ in heads], axis=1)
        dgate_ref[...] = (dyg * on * (sg * (1.0 + gate * (1.0 - sg)))).astype(dgate_ref.dtype)
        dgn = jnp.zeros((1, GLA_DV), F32)
        dos = []
        for h, (oh, rh) in enumerate(heads):
            donh = don[:, GLA_DV * h : GLA_DV * (h + 1)]
            dgn = dgn + colsum(donh * oh)
            doh = donh * gn
            dos.append(rh * (doh - oh * jnp.mean(doh * oh, axis=-1, keepdims=True)))
        do_ref[...] = jnp.concatenate(dos, axis=1)

        first = pl.program_id(0) == 0

        @pl.when(first)
        def _():
            dcw_ref[...] = dcw
            dcn_ref[...] = dcn
            dgn_ref[...] = dgn

        @pl.when(jnp.logical_not(first))
        def _():
            dcw_ref[...] += dcw
            dcn_ref[...] += dcn
            dgn_ref[...] += dgn

    col = lambda c, w=CONV_WIDTH: pl.BlockSpec((ts, w), lambda i: (i, c))
    cw_spec = pl.BlockSpec((CONV_K, CONV_WIDTH), lambda i: (0, 0))
    cn_spec = pl.BlockSpec((1, CONV_WIDTH), lambda i: (0, 0))
    gn_spec = pl.BlockSpec((1, GLA_DV), lambda i: (0, 0))
    return pl.pallas_call(
        body,
        name=name,
        grid=(s // ts,),
        in_specs=[col(0), col(1), col(2), halos[0][0], halos[0][1], halos[1][0], halos[1][1], halos[2][0], halos[2][1],
                  col(5), col(0), col(0), col(0), col(1), dprev, dnext, cw_spec, cn_spec, gn_spec],
        out_specs=[pl.BlockSpec((ts, 3 * CONV_WIDTH), lambda i: (i, 0)), col(0), col(0), cw_spec, cn_spec, gn_spec],
        out_shape=[
            jax.ShapeDtypeStruct((s, 3 * CONV_WIDTH), _CD),
            jax.ShapeDtypeStruct((s, GLA_V_TOTAL), _CD),
            jax.ShapeDtypeStruct((s, GLA_V_TOTAL), F32),
            jax.ShapeDtypeStruct((CONV_K, CONV_WIDTH), F32),
            jax.ShapeDtypeStruct((1, CONV_WIDTH), F32),
            jax.ShapeDtypeStruct((1, GLA_DV), F32),
        ],
        compiler_params=_cp(("arbitrary",)),
    )(z, z, z, z, z, z, z, z, z, z, o_f, o_b, dy, dy, dy, dy, conv_w, conv_norm, gla_norm)


def _xa_probs(q_ref, kv_ref, h):
    qh = q_ref[:, XA_HEAD_DIM * h : XA_HEAD_DIM * (h + 1)]
    kh = kv_ref[:, XA_HEAD_DIM * h : XA_HEAD_DIM * (h + 1)]
    vh = kv_ref[:, D_MODEL + XA_HEAD_DIM * h : D_MODEL + XA_HEAD_DIM * (h + 1)]
    sc = _dot_nt(qh, kh) * (XA_HEAD_DIM**-0.5)
    e = jnp.exp(sc - jnp.max(sc, axis=-1, keepdims=True))
    return qh, kh, vh, e / jnp.sum(e, axis=-1, keepdims=True)


def _xattn_fwd(qx, kv, *, name):
    s = qx.shape[0]
    ts = _rows(s)

    def body(q_ref, kv_ref, o_ref):
        outs = []
        for h in range(XA_HEADS):
            _, _, vh, p = _xa_probs(q_ref, kv_ref, h)
            outs.append(_dot(p, vh))
        o_ref[...] = jnp.concatenate(outs, axis=1).astype(o_ref.dtype)

    return pl.pallas_call(
        body,
        name=name,
        grid=(s // ts,),
        in_specs=[pl.BlockSpec((ts, D_MODEL), lambda i: (i, 0)), pl.BlockSpec((N_MEM, 2 * D_MODEL), lambda i: (0, 0))],
        out_specs=pl.BlockSpec((ts, D_MODEL), lambda i: (i, 0)),
        out_shape=jax.ShapeDtypeStruct((s, D_MODEL), _CD),
        compiler_params=_cp(("parallel",)),
    )(qx, kv)


def _xattn_bwd(qx, kv, dox, *, name):
    s = qx.shape[0]
    ts = _rows(s)

    def body(q_ref, kv_ref, do_ref, dq_ref, dkv_ref):
        dqs, dks, dvs = [], [], []
        for h in range(XA_HEADS):
            qh, kh, vh, p = _xa_probs(q_ref, kv_ref, h)
            doh = do_ref[:, XA_HEAD_DIM * h : XA_HEAD_DIM * (h + 1)]
            dp = _dot_nt(doh, vh)
            ds = p * (dp - jnp.sum(dp * p, axis=-1, keepdims=True)) * (XA_HEAD_DIM**-0.5)
            dqs.append(_dot(ds, kh))
            dks.append(_dot_tn(ds, qh))
            dvs.append(_dot_tn(p, doh))
        dq_ref[...] = jnp.concatenate(dqs, axis=1).astype(dq_ref.dtype)
        dkv = jnp.concatenate(dks + dvs, axis=1)

        @pl.when(pl.program_id(0) == 0)
        def _():
            dkv_ref[...] = dkv

        @pl.when(pl.program_id(0) > 0)
        def _():
            dkv_ref[...] += dkv

    tile = pl.BlockSpec((ts, D_MODEL), lambda i: (i, 0))
    kv_spec = pl.BlockSpec((N_MEM, 2 * D_MODEL), lambda i: (0, 0))
    return pl.pallas_call(
        body,
        name=name,
        grid=(s // ts,),
        in_specs=[tile, kv_spec, tile],
        out_specs=[tile, kv_spec],
        out_shape=[jax.ShapeDtypeStruct((s, D_MODEL), _CD), jax.ShapeDtypeStruct((N_MEM, 2 * D_MODEL), F32)],
        compiler_params=_cp(("arbitrary",)),
    )(qx, kv, dox)


def _adamw_math(w, g, m, v):
    m = ADAM_B1 * m + (1.0 - ADAM_B1) * g
    v = ADAM_B2 * v + (1.0 - ADAM_B2) * (g * g)
    m_hat = m / (1.0 - ADAM_B1**ADAM_STEP)
    v_hat = v / (1.0 - ADAM_B2**ADAM_STEP)
    delta = -ADAM_LR * (m_hat / (jnp.sqrt(v_hat) + ADAM_EPS) + ADAM_WD * w)
    return delta, m, v


def _adamw(w, g, m, v, *, name):
    r, c = w.shape
    tr = _pick(r, (256, 128, 64, 32, 16, 8))

    def body(w_ref, g_ref, m_ref, v_ref, d_ref, nm_ref, nv_ref):
        d_ref[...], nm_ref[...], nv_ref[...] = _adamw_math(w_ref[...], g_ref[...], m_ref[...], v_ref[...])

    tile = pl.BlockSpec((tr, c), lambda i: (i, 0))
    return pl.pallas_call(
        body,
        name=name,
        grid=(r // tr,),
        in_specs=[tile] * 4,
        out_specs=[tile] * 3,
        out_shape=[jax.ShapeDtypeStruct((r, c), F32)] * 3,
        compiler_params=_cp(("parallel",)),
    )(w, g, m, v)


def _adamw_small(groups, *, name):
    n = len(groups)

    def body(*refs):
        ins, outs = refs[: 4 * n], refs[4 * n :]
        for i in range(n):
            w_ref, g_ref, m_ref, v_ref = ins[4 * i : 4 * i + 4]
            outs[3 * i][...], outs[3 * i + 1][...], outs[3 * i + 2][...] = _adamw_math(w_ref[...], g_ref[...], m_ref[...], v_ref[...])

    flat = [a for grp in groups for a in grp]
    vm = pl.BlockSpec(memory_space=pltpu.VMEM)
    res = pl.pallas_call(
        body,
        name=name,
        in_specs=[vm] * (4 * n),
        out_specs=[vm] * (3 * n),
        out_shape=[jax.ShapeDtypeStruct(grp[0].shape, F32) for grp in groups for _ in range(3)],
        compiler_params=_cp(),
    )(*flat)
    return [tuple(res[3 * i : 3 * i + 3]) for i in range(n)]


def _place():
    return lax.axis_index("x"), lax.axis_index("y"), lax.axis_index("c")


def _rel_chip(x, y, k):
    return (1 - x if k & 2 else x), (1 - y if k & 1 else y)


def _half(c, rh):
    return pl.ds(pl.multiple_of(c * rh, 16), rh)


def _gather_weights(pack):
    r, w = pack.shape
    rh = r // 2

    def body(p_ref, q_ref, send_sems, recv_sems, local_sem):
        x, y, c = _place()
        j = 2 * x + y
        rows = _half(c, rh)
        mine = pltpu.make_async_copy(p_ref, q_ref.at[j], local_sem)
        mine.start()

        def to_chip(k):
            cx, cy = _rel_chip(x, y, k)
            return pltpu.make_async_remote_copy(
                src_ref=p_ref.at[rows], dst_ref=q_ref.at[j, rows], send_sem=send_sems.at[k - 1], recv_sem=recv_sems.at[k - 1],
                device_id=(cx, cy, c), device_id_type=MESH)

        def to_sibling(k):
            cx, cy = _rel_chip(x, y, k)
            slot = q_ref.at[2 * cx + cy, rows]
            return pltpu.make_async_remote_copy(
                src_ref=slot, dst_ref=slot, send_sem=send_sems.at[2 + k], recv_sem=recv_sems.at[2 + k],
                device_id=(x, y, 1 - c), device_id_type=MESH)

        first = [to_chip(k) for k in range(1, N_CHIPS)]
        passed = [to_sibling(k) for k in range(1, N_CHIPS)]
        for cp in first:
            cp.start()
        for cp, fw in zip(first, passed):
            cp.wait_recv()
            fw.start()
        for fw in passed:
            fw.wait_recv()
        for cp in first + passed:
            cp.wait_send()
        mine.wait()

    return pl.pallas_call(
        body,
        name="gather_weights",
        in_specs=[ANY],
        out_specs=ANY,
        out_shape=jax.ShapeDtypeStruct((N_CHIPS, r, w), pack.dtype),
        scratch_shapes=[pltpu.SemaphoreType.DMA((6,)), pltpu.SemaphoreType.DMA((6,)), pltpu.SemaphoreType.DMA],
        compiler_params=pltpu.CompilerParams(has_side_effects=True),
    )(pack)


def _swap_halves(g):
    n, r, w = g.shape
    rh = r // 2

    def body(g_ref, o_ref, send_sem, recv_sem):
        x, y, c = _place()
        cp = pltpu.make_async_remote_copy(
            src_ref=g_ref.at[:, _half(1 - c, rh)], dst_ref=o_ref, send_sem=send_sem, recv_sem=recv_sem,
            device_id=(x, y, 1 - c), device_id_type=MESH)
        cp.start()
        cp.wait()

    return pl.pallas_call(
        body,
        name="grads_to_sibling",
        in_specs=[ANY],
        out_specs=ANY,
        out_shape=jax.ShapeDtypeStruct((n, rh, w), g.dtype),
        scratch_shapes=[pltpu.SemaphoreType.DMA, pltpu.SemaphoreType.DMA],
        compiler_params=pltpu.CompilerParams(has_side_effects=True),
    )(g)


def _chip_sums(g, got, where):
    n, r, w = g.shape
    rh = r // 2
    nt = rh // PACK_TILE

    def body(where_ref, g_ref, got_ref, o_ref):
        o_ref[...] = (g_ref[...] + got_ref[...]).astype(o_ref.dtype)

    return pl.pallas_call(
        body,
        name="chip_sums",
        grid_spec=pltpu.PrefetchScalarGridSpec(
            num_scalar_prefetch=1,
            grid=(n, nt),
            in_specs=[pl.BlockSpec((1, PACK_TILE, w), lambda a, i, wh: (a, wh[0] * nt + i, 0)),
                      pl.BlockSpec((1, PACK_TILE, w), lambda a, i, wh: (a, i, 0))],
            out_specs=pl.BlockSpec((1, PACK_TILE, w), lambda a, i, wh: (a, i, 0)),
        ),
        out_shape=jax.ShapeDtypeStruct((n, rh, w), _TD),
        compiler_params=_cp(("parallel", "parallel")),
    )(where, g, got)


def _exchange_chip_sums(h):
    n, rh, w = h.shape

    def body(h_ref, o_ref, send_sems, recv_sems):
        x, y, c = _place()
        j = 2 * x + y
        copies = []
        for k in range(1, N_CHIPS):
            cx, cy = _rel_chip(x, y, k)
            copies.append(pltpu.make_async_remote_copy(
                src_ref=h_ref.at[2 * cx + cy], dst_ref=o_ref.at[k - 1], send_sem=send_sems.at[k - 1], recv_sem=recv_sems.at[k - 1],
                device_id=(cx, cy, c), device_id_type=MESH))
        for cp in copies:
            cp.start()
        for cp in copies:
            cp.wait()

    return pl.pallas_call(
        body,
        name="chip_sums_exchange",
        in_specs=[ANY],
        out_specs=ANY,
        out_shape=jax.ShapeDtypeStruct((N_CHIPS - 1, rh, w), h.dtype),
        scratch_shapes=[pltpu.SemaphoreType.DMA((3,)), pltpu.SemaphoreType.DMA((3,))],
        compiler_params=pltpu.CompilerParams(has_side_effects=True),
    )(h)


def _shard_sum(g, got, others, where):
    n, r, w = g.shape
    rh = r // 2
    nt = rh // PACK_TILE

    def body(where_ref, g_ref, got_ref, oth_ref, o_ref):
        acc = g_ref[0] + got_ref[0]
        for k in range(N_CHIPS - 1):
            acc = acc + oth_ref[k].astype(F32)
        o_ref[...] = acc

    return pl.pallas_call(
        body,
        name="shard_sum",
        grid_spec=pltpu.PrefetchScalarGridSpec(
            num_scalar_prefetch=1,
            grid=(nt,),
            in_specs=[pl.BlockSpec((1, PACK_TILE, w), lambda i, wh: (wh[1], wh[0] * nt + i, 0)),
                      pl.BlockSpec((1, PACK_TILE, w), lambda i, wh: (wh[1], i, 0)),
                      pl.BlockSpec((N_CHIPS - 1, PACK_TILE, w), lambda i, wh: (0, i, 0))],
            out_specs=pl.BlockSpec((PACK_TILE, w), lambda i, wh: (i, 0)),
        ),
        out_shape=jax.ShapeDtypeStruct((rh, w), F32),
        compiler_params=_cp(("parallel",)),
    )(where, g, got, others)


def _join_halves(e):
    rh, w = e.shape

    def body(e_ref, o_ref, send_sem, recv_sem, local_sem):
        x, y, c = _place()
        rows = _half(c, rh)
        mine = pltpu.make_async_copy(e_ref, o_ref.at[rows], local_sem)
        mine.start()
        cp = pltpu.make_async_remote_copy(
            src_ref=e_ref, dst_ref=o_ref.at[rows], send_sem=send_sem, recv_sem=recv_sem, device_id=(x, y, 1 - c), device_id_type=MESH)
        cp.start()
        cp.wait()
        mine.wait()

    return pl.pallas_call(
        body,
        name="shard_to_sibling",
        in_specs=[ANY],
        out_specs=ANY,
        out_shape=jax.ShapeDtypeStruct((2 * rh, w), e.dtype),
        scratch_shapes=[pltpu.SemaphoreType.DMA, pltpu.SemaphoreType.DMA, pltpu.SemaphoreType.DMA],
        compiler_params=pltpu.CompilerParams(has_side_effects=True),
    )(e)


def _sum_small(small):
    n_dev = 8

    def body(s_ref, o_ref, all_ref, send_sems, recv_sems):
        x, y, c = _place()
        me = 4 * x + 2 * y + c
        all_ref[me] = s_ref[...]
        copies = []
        for k in range(1, n_dev):
            cx, cy = _rel_chip(x, y, k >> 1)
            cc = 1 - c if k & 1 else c
            copies.append(pltpu.make_async_remote_copy(
                src_ref=s_ref, dst_ref=all_ref.at[me], send_sem=send_sems.at[k - 1], recv_sem=recv_sems.at[k - 1],
                device_id=(cx, cy, cc), device_id_type=MESH))
        for cp in copies:
            cp.start()
        for cp in copies:
            cp.wait()
        acc = all_ref[0]
        for a in range(1, n_dev):
            acc = acc + all_ref[a]
        o_ref[...] = acc

    vm = pl.BlockSpec(memory_space=pltpu.VMEM)
    return pl.pallas_call(
        body,
        name="sum_small",
        in_specs=[vm],
        out_specs=vm,
        out_shape=jax.ShapeDtypeStruct(small.shape, F32),
        scratch_shapes=[pltpu.VMEM((n_dev,) + small.shape, F32), pltpu.SemaphoreType.DMA((n_dev - 1,)), pltpu.SemaphoreType.DMA((n_dev - 1,))],
        compiler_params=pltpu.CompilerParams(has_side_effects=True),
    )(small)


BIG = (
    ("w_in", 1, (D_MODEL, W_IN_COLS)),
    ("w_out", 0, (D_MODEL, D_MODEL)),
    ("w_xq", 0, (D_MODEL, D_MODEL)),
    ("w_xkv", 1, (D_MODEL, 2 * D_MODEL)),
    ("w_xo", 0, (D_MODEL, D_MODEL)),
    ("w_up", 1, (D_MODEL, D_FF)),
    ("w_down", 0, (D_FF, D_MODEL)),
)


def _shard_shape(axis, full):
    return (full[0] // N_CHIPS, full[1]) if axis == 0 else (full[0], full[1] // N_CHIPS)


def _pack_rows(pieces):
    rows = jnp.concatenate([p.reshape(-1, PACK_W) for p in pieces], axis=0)
    return jnp.pad(rows, ((0, PACK_ROWS - rows.shape[0]), (0, 0)))


def _unpack_rows(rows):
    out, off = {}, 0
    for name, axis, full in BIG:
        shp = _shard_shape(axis, full)
        n = shp[0] * shp[1] // PACK_W
        out[name] = rows[off : off + n].reshape(shp)
        off += n
    return out


SMALL = (
    ("mix_norm", 1024), ("conv_norm", 512), ("b_af", 256), ("b_ab", 256), ("gla_norm", 128), ("xa_norm", 1024), ("mem_norm", 1024),
    ("mlp_norm", 1024), ("final_norm", 1024), ("conv_w", 1536), ("w_af", 4096), ("w_ab", 4096), ("loss", 128),
)


def kernel(x, mem, mix_norm, w_in, conv_w, conv_norm, w_af, b_af, w_ab, b_ab, gla_norm, w_out, xa_norm, mem_norm, w_xq, w_xkv, w_xo, mlp_norm, w_up, w_down, final_norm, loss_target, m_mix_norm, m_w_in, m_conv_w, m_conv_norm, m_w_af, m_b_af, m_w_ab, m_b_ab, m_gla_norm, m_w_out, m_xa_norm, m_mem_norm, m_w_xq, m_w_xkv, m_w_xo, m_mlp_norm, m_w_up, m_w_down, m_final_norm, v_mix_norm, v_w_in, v_conv_w, v_conv_norm, v_w_af, v_b_af, v_w_ab, v_b_ab, v_gla_norm, v_w_out, v_xa_norm, v_mem_norm, v_w_xq, v_w_xkv, v_w_xo, v_mlp_norm, v_w_up, v_w_down, v_final_norm):
    given = dict(locals())
    xi, yi, ci = _place()
    chip = 2 * xi + yi
    where = jnp.stack([ci, chip]).astype(jnp.int32)

    shards = {name: given[name][0] for name, _, _ in BIG}
    gathered = _gather_weights(_pack_rows([shards[name].astype(_CD) for name, _, _ in BIG]))
    per_chip = [_unpack_rows(gathered[a]) for a in range(N_CHIPS)]
    wfull = {name: jnp.concatenate([per_chip[a][name] for a in range(N_CHIPS)], axis=axis) for name, axis, _ in BIG}
    w_in_p = jnp.pad(wfull["w_in"], ((0, 0), (0, Z_COLS - W_IN_COLS)))

    def placed(shard, full_shape, col):
        return lax.dynamic_update_slice(jnp.zeros(full_shape, F32), shard, (0, col)).reshape(-1, 128)

    sw = jnp.concatenate([
        placed(conv_w[0], (CONV_K, CONV_WIDTH), 128 * chip),
        placed(w_af[0], (GLA_LOWRANK, GLA_K_TOTAL), 64 * chip),
        placed(w_ab[0], (GLA_LOWRANK, GLA_K_TOTAL), 64 * chip),
    ], axis=0)
    sw = jnp.pad(sw, ((0, SMALL_ROWS - sw.shape[0]), (0, 0))) * (ci == 0).astype(F32)
    sw = _sum_small(sw)
    conv_w_full = sw[0:12].reshape(CONV_K, CONV_WIDTH)
    w_af_full = sw[12:44].reshape(GLA_LOWRANK, GLA_K_TOTAL)
    w_ab_full = sw[44:76].reshape(GLA_LOWRANK, GLA_K_TOTAL)
    waf_p = jnp.pad(w_af_full, ((0, 128 - GLA_LOWRANK), (0, 0))).astype(_CD)
    wab_p = jnp.pad(w_ab_full, ((GLA_LOWRANK, 128 - 2 * GLA_LOWRANK), (0, 0))).astype(_CD)

    xs, mems, tgt = x[0], mem[0], loss_target[0]
    add_res = lambda acc, res: (acc + res,)

    h1 = _rms_fwd(xs, mix_norm, name="norm_mix")
    z = _mm(h1, w_in_p, mode="nn", name="proj_in", tm=512, tn=Z_COLS)
    b_f, b_b = _gate_fwd(z, waf_p, wab_p, b_af, b_ab, name="gates")
    o_f, st_f = _gla_fwd(z, b_f, rev=False, name="gla_scan_fwd")
    o_b, st_b = _gla_fwd(z, b_b, rev=True, name="gla_scan_rev")
    y = _mix_fwd(z, o_f, o_b, conv_w_full, conv_norm, gla_norm, name="mix_out")
    x1 = _mm(y, wfull["w_out"], mode="nn", name="proj_out", extras=(xs,), epilogue=add_res)
    hx = _rms_fwd(x1, xa_norm, name="norm_xa")
    qx = _mm(hx, wfull["w_xq"], mode="nn", name="proj_xq", out_dtypes=(_CD,))
    hmem = _rms_fwd(mems, mem_norm, name="norm_mem")
    kv = _mm(hmem, wfull["w_xkv"], mode="nn", name="proj_xkv", out_dtypes=(_CD,))
    ox = _xattn_fwd(qx, kv, name="xattn")
    x2 = _mm(ox, wfull["w_xo"], mode="nn", name="proj_xo", extras=(x1,), epilogue=add_res)
    hm = _rms_fwd(x2, mlp_norm, name="norm_mlp")
    act = _mm(hm, wfull["w_up"], mode="nn", name="mlp_up", out_dtypes=(_CD,), epilogue=lambda acc: (jnp.square(jnp.maximum(acc, 0.0)),))
    x3 = _mm(act, wfull["w_down"], mode="nn", name="mlp_down", extras=(x2,), epilogue=add_res, tm=512, tk=D_FF)
    dx3, dx3_lo, loss_part, g_final_norm = _final_loss(x3, final_norm.reshape(1, D_MODEL), tgt, name="loss_head")

    du = _mm(dx3_lo, wfull["w_down"], mode="nt", name="mlp_down_dx", out_dtypes=(_CD,), extras=(act,),
             epilogue=lambda acc, aa: (acc * (2.0 * jnp.sqrt(aa.astype(F32))),))
    g_w_down = _mm_tn(act, dx3_lo, name="mlp_down_dw")
    g_w_up = _mm_tn(hm, du, name="mlp_up_dw")
    dhm = _mm(du, wfull["w_up"], mode="nt", name="mlp_up_dx", tm=512, tk=D_FF)
    dx2, dx2_lo, g_mlp_norm = _rms_bwd(x2, mlp_norm, dhm, dx3, name="norm_mlp_bwd")
    dox = _mm(dx2_lo, wfull["w_xo"], mode="nt", name="proj_xo_dx", out_dtypes=(_CD,))
    g_w_xo = _mm_tn(ox, dx2_lo, name="proj_xo_dw")
    dqx, dkv = _xattn_bwd(qx, kv, dox, name="xattn_bwd")
    g_w_xq = _mm_tn(hx, dqx, name="proj_xq_dw")
    dhx = _mm(dqx, wfull["w_xq"], mode="nt", name="proj_xq_dx")
    dx1, dx1_lo, g_xa_norm = _rms_bwd(x1, xa_norm, dhx, dx2, name="norm_xa_bwd")
    dkv_lo = dkv.astype(_CD)
    g_w_xkv = _mm_tn(hmem, dkv_lo, name="proj_xkv_dw")
    dhmem = _mm(dkv_lo, wfull["w_xkv"], mode="nt", name="proj_xkv_dx")
    (g_mem_norm,) = _rms_bwd(mems, mem_norm, dhmem, name="norm_mem_bwd", want_dx=False, want_lo=False)
    dy = _mm(dx1_lo, wfull["w_out"], mode="nt", name="proj_out_dx")
    g_w_out = _mm_tn(y, dx1_lo, name="proj_out_dw")
    dz_conv, dz_gate, do, g_conv_w, g_conv_norm, g_gla_norm = _mix_bwd(z, o_f, o_b, dy, conv_w_full, conv_norm, gla_norm, name="mix_out_bwd")
    dqkv_f, db_f = _gla_bwd(z, b_f, do, st_f, rev=False, name="gla_scan_fwd_bwd")
    dqkv_b, db_b = _gla_bwd(z, b_b, do, st_b, rev=True, name="gla_scan_rev_bwd")
    dqkv, dlr, g_waf_p, g_wab_p, g_b_af, g_b_ab = _gate_bwd(z, waf_p, wab_p, b_af, b_ab, db_f, db_b, dqkv_f, dqkv_b, name="gates_bwd")
    dz = jnp.concatenate([dz_conv, dqkv, dz_gate, dlr], axis=1)
    g_w_in = _mm_tn(h1, dz, name="proj_in_dw")[:, :W_IN_COLS]
    dh1 = _mm(dz, w_in_p, mode="nt", name="proj_in_dx", tm=512, tk=Z_COLS)
    grad_x, g_mix_norm = _rms_bwd(xs, mix_norm, dh1, dx1, name="norm_mix_bwd", want_lo=False)

    g_full = dict(w_in=g_w_in, w_out=g_w_out, w_xq=g_w_xq, w_xkv=g_w_xkv, w_xo=g_w_xo, w_up=g_w_up, w_down=g_w_down)

    def shard_of(name, axis, a):
        size = g_full[name].shape[axis] // N_CHIPS
        return lax.slice_in_dim(g_full[name], a * size, (a + 1) * size, axis=axis)

    gpack = jnp.stack([_pack_rows([shard_of(name, axis, a) for name, axis, _ in BIG]) for a in range(N_CHIPS)])
    got = _swap_halves(gpack)
    others = _exchange_chip_sums(_chip_sums(gpack, got, where))
    g_shard = _unpack_rows(_join_halves(_shard_sum(gpack, got, others, where)))

    small_vals = dict(mix_norm=g_mix_norm, conv_norm=g_conv_norm, b_af=g_b_af, b_ab=g_b_ab, gla_norm=g_gla_norm, xa_norm=g_xa_norm,
                      mem_norm=g_mem_norm, mlp_norm=g_mlp_norm, final_norm=g_final_norm, conv_w=g_conv_w,
                      w_af=g_waf_p[0:GLA_LOWRANK], w_ab=g_wab_p[GLA_LOWRANK : 2 * GLA_LOWRANK], loss=loss_part)
    small = jnp.concatenate([small_vals[name].reshape(-1, 128) for name, _ in SMALL], axis=0)
    small = _sum_small(jnp.pad(small, ((0, SMALL_ROWS - small.shape[0]), (0, 0))))
    g_small, off = {}, 0
    for name, n in SMALL:
        g_small[name] = small[off : off + n // 128]
        off += n // 128
    loss = g_small["loss"][0, 0]
    g_small["conv_w"] = lax.dynamic_slice(g_small["conv_w"].reshape(CONV_K, CONV_WIDTH), (0, 128 * chip), (CONV_K, 128))
    g_small["w_af"] = lax.dynamic_slice(g_small["w_af"].reshape(GLA_LOWRANK, GLA_K_TOTAL), (0, 64 * chip), (GLA_LOWRANK, 64))
    g_small["w_ab"] = lax.dynamic_slice(g_small["w_ab"].reshape(GLA_LOWRANK, GLA_K_TOTAL), (0, 64 * chip), (GLA_LOWRANK, 64))

    names = ["mix_norm", "w_in", "conv_w", "conv_norm", "w_af", "b_af", "w_ab", "b_ab", "gla_norm", "w_out", "xa_norm", "mem_norm",
             "w_xq", "w_xkv", "w_xo", "mlp_norm", "w_up", "w_down", "final_norm"]
    big_names = [name for name, _, _ in BIG]
    as2d = lambda a: a.reshape(1, -1) if a.ndim == 1 else a.reshape(a.shape[-2:])
    grads, deltas, new_m, new_v = {}, {}, {}, {}
    for name in big_names:
        grads[name] = g_shard[name]
        deltas[name], new_m[name], new_v[name] = _adamw(as2d(given[name]), g_shard[name], as2d(given["m_" + name]),
                                                         as2d(given["v_" + name]), name="adamw_" + name)
    small_names = [name for name in names if name not in big_names]
    groups = []
    for name in small_names:
        grads[name] = g_small[name].reshape(as2d(given[name]).shape)
        groups.append((as2d(given[name]), grads[name], as2d(given["m_" + name]), as2d(given["v_" + name])))
    for name, res in zip(small_names, _adamw_small(groups, name="adamw_small")):
        deltas[name], new_m[name], new_v[name] = res

    like = lambda name, a: a.reshape(given[name].shape)
    return (loss, grad_x[None], *[like(n, grads[n]) for n in names], *[like(n, deltas[n]) for n in names],
            *[like(n, new_m[n]) for n in names], *[like(n, new_v[n]) for n in names])
```

```python
import functools

import jax
import jax.numpy as jnp
from jax import lax
from jax.experimental import pallas as pl
from jax.experimental.pallas import tpu as pltpu

F32 = jnp.float32
BF16 = jnp.bfloat16
_CD = jnp.bfloat16
_TD = jnp.bfloat16

D_MODEL = 1024
N_MEM = 256
CONV_WIDTH = 512
CONV_GROUP = 64
CONV_K = 3
GLA_HEADS = 4
GLA_DK = 64
GLA_DV = 128
GLA_K_TOTAL = 256
GLA_V_TOTAL = 512
GLA_LOWRANK = 16
GLA_GATE_SCALE = 1.0 / 16.0
GLA_CHUNK = 64
XA_HEADS = 4
XA_HEAD_DIM = 256
D_FF = 4096
EPS = 1e-6
W_IN_COLS = 3104
Z_COLS = 3200
LR_COL = 3072

ADAM_LR = 0.001
ADAM_B1 = 0.9
ADAM_B2 = 0.999
ADAM_EPS = 1e-08
ADAM_WD = 0.01
ADAM_STEP = 10

N_CHIPS = 4
PACK_W = 1024
PACK_ROWS = 4160
PACK_TILE = 160
SMALL_ROWS = 128

_TS = 512
_VMEM = 44 * 1024 * 1024
MESH = pl.DeviceIdType.MESH
ANY = pl.BlockSpec(memory_space=pl.ANY)


def _cp(sem=None, **kw):
    return pltpu.CompilerParams(dimension_semantics=sem, vmem_limit_bytes=_VMEM, **kw)


def _dot(a, b):
    return jnp.dot(a.astype(_CD), b.astype(_CD), preferred_element_type=F32)


def _dot_nt(a, b):
    return lax.dot_general(a.astype(_CD), b.astype(_CD), (((1,), (1,)), ((), ())), preferred_element_type=F32)


def _dot_tn(a, b):
    return lax.dot_general(a.astype(_CD), b.astype(_CD), (((0,), (0,)), ((), ())), preferred_element_type=F32)


def _dot_split(x, ones):
    hi = x.astype(BF16)
    r = x - hi.astype(F32)
    mid = r.astype(BF16)
    lo = (r - mid.astype(F32)).astype(BF16)
    d = lambda p: jnp.dot(p, ones, preferred_element_type=F32)
    return d(hi) + d(mid) + d(lo)


def _pick(n, cands=(1024, 640, 512, 256, 128)):
    for t in cands:
        if n % t == 0:
            return t
    return n


def _rows(s):
    return min(_TS, s)


def _sigmoid(v):
    e = jnp.exp(-jnp.abs(v))
    return jnp.where(v >= 0, 1.0 / (1.0 + e), e / (1.0 + e))


def _mm(a, b, *, mode, name, out_dtypes=(F32,), extras=(), epilogue=None, tm=None, tn=None, tk=None):
    m, k = a.shape
    n = b.shape[1] if mode == "nn" else b.shape[0]
    tm = min(m, tm or 1024)
    tn = tn or _pick(n)
    tk = tk or _pick(k)
    nk = k // tk
    n_ex, n_out = len(extras), len(out_dtypes)

    def body(*refs):
        a_ref, b_ref = refs[:2]
        ex = refs[2 : 2 + n_ex]
        outs = refs[2 + n_ex : 2 + n_ex + n_out]
        part = _dot(a_ref[...], b_ref[...]) if mode == "nn" else _dot_nt(a_ref[...], b_ref[...])

        def finish(acc):
            res = epilogue(acc, *[e[...] for e in ex]) if epilogue else (acc,)
            for o, r in zip(outs, res):
                o[...] = r.astype(o.dtype)

        if nk == 1:
            finish(part)
        else:
            acc_ref = refs[-1]
            kk = pl.program_id(2)

            @pl.when(kk == 0)
            def _():
                acc_ref[...] = part

            @pl.when(kk > 0)
            def _():
                acc_ref[...] += part

            @pl.when(kk == nk - 1)
            def _():
                finish(acc_ref[...])

    b_spec = pl.BlockSpec((tk, tn), lambda i, j, kk: (kk, j)) if mode == "nn" else pl.BlockSpec((tn, tk), lambda i, j, kk: (j, kk))
    tile = pl.BlockSpec((tm, tn), lambda i, j, kk: (i, j))
    out = pl.pallas_call(
        body,
        name=name,
        grid=(m // tm, n // tn, nk),
        in_specs=[pl.BlockSpec((tm, tk), lambda i, j, kk: (i, kk)), b_spec] + [tile] * n_ex,
        out_specs=[tile] * n_out,
        out_shape=[jax.ShapeDtypeStruct((m, n), dt) for dt in out_dtypes],
        scratch_shapes=[pltpu.VMEM((tm, tn), F32)] if nk > 1 else [],
        compiler_params=_cp(("parallel", "parallel", "arbitrary")),
    )(a, b, *extras)
    return out[0] if n_out == 1 else out


def _mm_tn(a, b, *, name):
    s, m = a.shape
    n = b.shape[1]
    cap = max(128, (1 << 20) // n)
    tm = _pick(m, tuple(t for t in (512, 640, 256, 128) if t <= max(cap, 128)))
    ts = min(s, 1 << (((1 << 22) // n).bit_length() - 1))
    ns = s // ts

    def body(a_ref, b_ref, o_ref):
        part = _dot_tn(a_ref[...], b_ref[...])
        if ns == 1:
            o_ref[...] = part
        else:
            ss = pl.program_id(1)

            @pl.when(ss == 0)
            def _():
                o_ref[...] = part

            @pl.when(ss > 0)
            def _():
                o_ref[...] += part

    return pl.pallas_call(
        body,
        name=name,
        grid=(m // tm, ns),
        in_specs=[pl.BlockSpec((ts, tm), lambda i, ss: (ss, i)), pl.BlockSpec((ts, n), lambda i, ss: (ss, 0))],
        out_specs=pl.BlockSpec((tm, n), lambda i, ss: (i, 0)),
        out_shape=jax.ShapeDtypeStruct((m, n), F32),
        compiler_params=_cp(("parallel", "arbitrary")),
    )(a, b)


def _rms_fwd(x, g, *, name):
    s, d = x.shape
    ts = _rows(s)

    def body(x_ref, g_ref, o_ref):
        xf = x_ref[...]
        r = lax.rsqrt(jnp.mean(xf * xf, axis=-1, keepdims=True) + EPS)
        o_ref[...] = (xf * r * g_ref[...]).astype(o_ref.dtype)

    return pl.pallas_call(
        body,
        name=name,
        grid=(s // ts,),
        in_specs=[pl.BlockSpec((ts, d), lambda i: (i, 0)), pl.BlockSpec((1, d), lambda i: (0, 0))],
        out_specs=pl.BlockSpec((ts, d), lambda i: (i, 0)),
        out_shape=jax.ShapeDtypeStruct((s, d), _CD),
        compiler_params=_cp(("parallel",)),
    )(x, g)


def _rms_bwd(x, g, dy, dres=None, *, name, want_dx=True, want_lo=True):
    s, d = x.shape
    ts = _rows(s)
    has_res = dres is not None

    def body(*refs):
        x_ref, g_ref, dy_ref = refs[:3]
        pos = 3
        dres_ref = refs[pos] if has_res else None
        pos += has_res
        dx_ref = refs[pos] if want_dx else None
        pos += want_dx
        lo_ref = refs[pos] if want_lo else None
        pos += want_lo
        dg_ref = refs[pos]
        xf = x_ref[...]
        r = lax.rsqrt(jnp.mean(xf * xf, axis=-1, keepdims=True) + EPS)
        xh = xf * r
        dyf = dy_ref[...]
        part = jnp.sum(dyf * xh, axis=0, keepdims=True)

        @pl.when(pl.program_id(0) == 0)
        def _():
            dg_ref[...] = part

        @pl.when(pl.program_id(0) > 0)
        def _():
            dg_ref[...] += part

        if want_dx or want_lo:
            dxh = dyf * g_ref[...]
            dx = r * (dxh - xh * jnp.mean(dxh * xh, axis=-1, keepdims=True))
            if has_res:
                dx = dx + dres_ref[...]
            if want_dx:
                dx_ref[...] = dx
            if want_lo:
                lo_ref[...] = dx.astype(lo_ref.dtype)

    tile = pl.BlockSpec((ts, d), lambda i: (i, 0))
    vec = pl.BlockSpec((1, d), lambda i: (0, 0))
    out_specs, out_shape = [], []
    if want_dx:
        out_specs.append(tile)
        out_shape.append(jax.ShapeDtypeStruct((s, d), F32))
    if want_lo:
        out_specs.append(tile)
        out_shape.append(jax.ShapeDtypeStruct((s, d), _CD))
    out_specs.append(vec)
    out_shape.append(jax.ShapeDtypeStruct((1, d), F32))
    return pl.pallas_call(
        body,
        name=name,
        grid=(s // ts,),
        in_specs=[tile, vec, tile] + ([tile] if has_res else []),
        out_specs=out_specs,
        out_shape=out_shape,
        compiler_params=_cp(("arbitrary",)),
    )(x, g, dy, *([dres] if has_res else []))


def _final_loss(x3, g, tgt, *, name):
    s, d = x3.shape
    ts = _rows(s)

    def body(x_ref, g_ref, t_ref, dx_ref, lo_ref, loss_ref, dg_ref):
        xf = x_ref[...]
        r = lax.rsqrt(jnp.mean(xf * xf, axis=-1, keepdims=True) + EPS)
        xh = xf * r
        gg = g_ref[...]
        err = xh * gg - t_ref[...]
        lpart = jnp.zeros((1, 128), F32) + 0.5 * jnp.sum(jnp.mean(err * err, axis=-1, keepdims=True))
        dy = err * (1.0 / d)
        gpart = jnp.sum(dy * xh, axis=0, keepdims=True)

        @pl.when(pl.program_id(0) == 0)
        def _():
            loss_ref[...] = lpart
            dg_ref[...] = gpart

        @pl.when(pl.program_id(0) > 0)
        def _():
            loss_ref[...] += lpart
            dg_ref[...] += gpart

        dxh = dy * gg
        dx = r * (dxh - xh * jnp.mean(dxh * xh, axis=-1, keepdims=True))
        dx_ref[...] = dx
        lo_ref[...] = dx.astype(lo_ref.dtype)

    tile = pl.BlockSpec((ts, d), lambda i: (i, 0))
    vec = pl.BlockSpec((1, d), lambda i: (0, 0))
    return pl.pallas_call(
        body,
        name=name,
        grid=(s // ts,),
        in_specs=[tile, vec, tile],
        out_specs=[tile, tile, pl.BlockSpec((1, 128), lambda i: (0, 0)), vec],
        out_shape=[
            jax.ShapeDtypeStruct((s, d), F32),
            jax.ShapeDtypeStruct((s, d), _CD),
            jax.ShapeDtypeStruct((1, 128), F32),
            jax.ShapeDtypeStruct((1, d), F32),
        ],
        compiler_params=_cp(("arbitrary",)),
    )(x3, g, tgt)


def _chunk_scan(v, row_in_chunk, suffix):
    t = v.shape[0]
    step = 1
    while step < GLA_CHUNK:
        if suffix:
            v = v + jnp.where(row_in_chunk < GLA_CHUNK - step, pltpu.roll(v, t - step, 0), 0.0)
        else:
            v = v + jnp.where(row_in_chunk >= step, pltpu.roll(v, step, 0), 0.0)
        step *= 2
    return v


def _gate_pre(lr, w_ref, b_ref):
    return _dot(lr, w_ref[...]) + b_ref[...]


def _gate_fwd(z, waf, wab, baf, bab, *, name):
    s = z.shape[0]
    ts = _rows(s)

    def body(lr_ref, waf_ref, wab_ref, baf_ref, bab_ref, bf_ref, bb_ref):
        lr = lr_ref[...]
        ric = lax.broadcasted_iota(jnp.int32, (ts, GLA_K_TOTAL), 0) & (GLA_CHUNK - 1)
        for w_ref, b_ref, o_ref, suffix in ((waf_ref, baf_ref, bf_ref, False), (wab_ref, bab_ref, bb_ref, True)):
            pre = _gate_pre(lr, w_ref, b_ref)
            la = (jnp.minimum(pre, 0.0) - jnp.log(1.0 + jnp.exp(-jnp.abs(pre)))) * GLA_GATE_SCALE
            o_ref[...] = _chunk_scan(la, ric, suffix)

    wspec = pl.BlockSpec((128, GLA_K_TOTAL), lambda i: (0, 0))
    bspec = pl.BlockSpec((1, GLA_K_TOTAL), lambda i: (0, 0))
    tile = pl.BlockSpec((ts, GLA_K_TOTAL), lambda i: (i, 0))
    return pl.pallas_call(
        body,
        name=name,
        grid=(s // ts,),
        in_specs=[pl.BlockSpec((ts, 128), lambda i: (i, LR_COL // 128)), wspec, wspec, bspec, bspec],
        out_specs=[tile, tile],
        out_shape=[jax.ShapeDtypeStruct((s, GLA_K_TOTAL), F32)] * 2,
        compiler_params=_cp(("parallel",)),
    )(z, waf, wab, baf, bab)


def _gate_bwd(z, waf, wab, baf, bab, dbf, dbb, dqkv_f, dqkv_b, *, name):
    s = z.shape[0]
    ts = _rows(s)

    def body(lr_ref, waf_ref, wab_ref, baf_ref, bab_ref, dbf_ref, dbb_ref, gf_ref, gb_ref, dqkv_ref, dlr_ref, dwf_ref, dwb_ref, dbaf_ref, dbab_ref):
        lr = lr_ref[...]
        ric = lax.broadcasted_iota(jnp.int32, (ts, GLA_K_TOTAL), 0) & (GLA_CHUNK - 1)
        first = pl.program_id(0) == 0
        dlr = None
        for w_ref, b_ref, db_ref, dw_ref, dbias_ref, suffix in (
            (waf_ref, baf_ref, dbf_ref, dwf_ref, dbaf_ref, True),
            (wab_ref, bab_ref, dbb_ref, dwb_ref, dbab_ref, False),
        ):
            pre = _gate_pre(lr, w_ref, b_ref)
            dla = _chunk_scan(db_ref[...], ric, suffix)
            dpre = dla * GLA_GATE_SCALE * _sigmoid(-pre)
            part = _dot_nt(dpre, w_ref[...])
            dlr = part if dlr is None else dlr + part
            dw = _dot_tn(lr, dpre)
            dbias = jnp.sum(dpre, axis=0, keepdims=True)

            @pl.when(first)
            def _():
                dw_ref[...] = dw
                dbias_ref[...] = dbias

            @pl.when(jnp.logical_not(first))
            def _():
                dw_ref[...] += dw
                dbias_ref[...] += dbias

        dlr_ref[...] = dlr.astype(dlr_ref.dtype)
        dqkv_ref[...] = (gf_ref[...] + gb_ref[...]).astype(dqkv_ref.dtype)

    wspec = pl.BlockSpec((128, GLA_K_TOTAL), lambda i: (0, 0))
    bspec = pl.BlockSpec((1, GLA_K_TOTAL), lambda i: (0, 0))
    tile = pl.BlockSpec((ts, GLA_K_TOTAL), lambda i: (i, 0))
    wide = pl.BlockSpec((ts, 2 * GLA_K_TOTAL + GLA_V_TOTAL), lambda i: (i, 0))
    return pl.pallas_call(
        body,
        name=name,
        grid=(s // ts,),
        in_specs=[pl.BlockSpec((ts, 128), lambda i: (i, LR_COL // 128)), wspec, wspec, bspec, bspec, tile, tile, wide, wide],
        out_specs=[wide, pl.BlockSpec((ts, 128), lambda i: (i, 0)), wspec, wspec, bspec, bspec],
        out_shape=[
            jax.ShapeDtypeStruct((s, 2 * GLA_K_TOTAL + GLA_V_TOTAL), _CD),
            jax.ShapeDtypeStruct((s, 128), _CD),
            jax.ShapeDtypeStruct((128, GLA_K_TOTAL), F32),
            jax.ShapeDtypeStruct((128, GLA_K_TOTAL), F32),
            jax.ShapeDtypeStruct((1, GLA_K_TOTAL), F32),
            jax.ShapeDtypeStruct((1, GLA_K_TOTAL), F32),
        ],
        compiler_params=_cp(("arbitrary",)),
    )(z, waf, wab, baf, bab, dbf, dbb, dqkv_f, dqkv_b)


def _gla_masks(rev):
    lane_head = lax.broadcasted_iota(jnp.int32, (1, GLA_K_TOTAL), 1) >> 6
    head_masks = [lane_head == h for h in range(GLA_HEADS)]
    st_rows = lax.broadcasted_iota(jnp.int32, (GLA_V_TOTAL, GLA_K_TOTAL), 0) >> 7
    st_lanes = lax.broadcasted_iota(jnp.int32, (GLA_V_TOTAL, GLA_K_TOTAL), 1) >> 6
    block_mask = st_rows == st_lanes
    t = lax.broadcasted_iota(jnp.int32, (GLA_HEADS * GLA_CHUNK, GLA_CHUNK), 0) & (GLA_CHUNK - 1)
    u = lax.broadcasted_iota(jnp.int32, (GLA_HEADS * GLA_CHUNK, GLA_CHUNK), 1)
    tri = (u > t) if rev else (u <= t)
    row = lax.broadcasted_iota(jnp.int32, (GLA_CHUNK, GLA_K_TOTAL), 0)
    total_row = row == (0 if rev else GLA_CHUNK - 1)
    return head_masks, block_mask, tri, total_row


def _gla_chunk_terms(q_ref, k_ref, v_ref, b_ref, rows, head_masks, tri, total_row):
    q = q_ref[rows, :] * (GLA_DK**-0.5)
    k = k_ref[rows, :]
    v = v_ref[rows, :]
    b = b_ref[rows, :]
    eb = jnp.exp(b)
    enb = jnp.exp(-b)
    g = jnp.sum(jnp.where(total_row, b, 0.0), axis=0, keepdims=True)
    egb = jnp.exp(g - b)
    qt = q * eb
    kt = k * enb
    kh = k * egb
    q_heads = jnp.concatenate([jnp.where(m, qt, 0.0) for m in head_masks], axis=0)
    attn = jnp.where(tri, _dot_nt(q_heads, kt), 0.0)
    return v, eb, enb, egb, jnp.exp(g), qt, kt, kh, attn


def _gla_specs(s, tb, rev_blocks):
    nb = s // tb
    rb = (lambda i: nb - 1 - i) if rev_blocks else (lambda i: i)
    q_spec = pl.BlockSpec((tb, GLA_K_TOTAL), lambda i: (rb(i), 1536 // GLA_K_TOTAL))
    k_spec = pl.BlockSpec((tb, GLA_K_TOTAL), lambda i: (rb(i), 1792 // GLA_K_TOTAL))
    v_spec = pl.BlockSpec((tb, GLA_V_TOTAL), lambda i: (rb(i), 2048 // GLA_V_TOTAL))
    b_spec = pl.BlockSpec((tb, GLA_K_TOTAL), lambda i: (rb(i), 0))
    o_spec = pl.BlockSpec((tb, GLA_V_TOTAL), lambda i: (rb(i), 0))
    st_spec = pl.BlockSpec((tb // GLA_CHUNK, GLA_DV, GLA_K_TOTAL), lambda i: (rb(i), 0, 0))
    return nb, q_spec, k_spec, v_spec, b_spec, o_spec, st_spec


def _gla_fwd(z, b, *, rev, name):
    s = z.shape[0]
    tb = _rows(s)
    cpb = tb // GLA_CHUNK
    nb, q_spec, k_spec, v_spec, b_spec, o_spec, st_spec = _gla_specs(s, tb, rev)

    def body(q_ref, k_ref, v_ref, b_ref, o_ref, sv_ref, st_ref):
        head_masks, block_mask, tri, total_row = _gla_masks(rev)

        @pl.when(pl.program_id(0) == 0)
        def _():
            st_ref[...] = jnp.zeros_like(st_ref)

        def chunk(ci, carry):
            cidx = cpb - 1 - ci if rev else ci
            rows = pl.ds(pl.multiple_of(cidx * GLA_CHUNK, GLA_CHUNK), GLA_CHUNK)
            v, _, _, _, eg, qt, _, kh, attn = _gla_chunk_terms(q_ref, k_ref, v_ref, b_ref, rows, head_masks, tri, total_row)
            o = jnp.concatenate(
                [_dot(attn[GLA_CHUNK * h : GLA_CHUNK * (h + 1)], v[:, GLA_DV * h : GLA_DV * (h + 1)]) for h in range(GLA_HEADS)], axis=1
            )
            st = st_ref[...]
            o_ref[rows, :] = o + _dot_nt(qt, st)
            sv_ref[cidx] = st[0:128] + st[128:256] + st[256:384] + st[384:512]
            st_ref[...] = st * eg + jnp.where(block_mask, _dot_tn(v, kh), 0.0)
            return carry

        lax.fori_loop(0, cpb, chunk, 0)

    return pl.pallas_call(
        body,
        name=name,
        grid=(nb,),
        in_specs=[q_spec, k_spec, v_spec, b_spec],
        out_specs=[o_spec, st_spec],
        out_shape=[jax.ShapeDtypeStruct((s, GLA_V_TOTAL), F32), jax.ShapeDtypeStruct((s // GLA_CHUNK, GLA_DV, GLA_K_TOTAL), F32)],
        scratch_shapes=[pltpu.VMEM((GLA_V_TOTAL, GLA_K_TOTAL), F32)],
        compiler_params=_cp(("arbitrary",)),
    )(z, z, z, b)


def _gla_bwd(z, b, do, states, *, rev, name):
    s = z.shape[0]
    tb = _rows(s)
    cpb = tb // GLA_CHUNK
    nb, q_spec, k_spec, v_spec, b_spec, o_spec, st_spec = _gla_specs(s, tb, not rev)
    rb = (lambda i: nb - 1 - i) if not rev else (lambda i: i)

    def body(q_ref, k_ref, v_ref, b_ref, do_ref, sv_ref, dqkv_ref, db_ref, dst_ref):
        head_masks, block_mask, tri, total_row = _gla_masks(rev)

        @pl.when(pl.program_id(0) == 0)
        def _():
            dst_ref[...] = jnp.zeros_like(dst_ref)

        def chunk(ci, carry):
            cidx = ci if rev else cpb - 1 - ci
            rows = pl.ds(pl.multiple_of(cidx * GLA_CHUNK, GLA_CHUNK), GLA_CHUNK)
            v, eb, enb, egb, eg, qt, kt, kh, attn = _gla_chunk_terms(q_ref, k_ref, v_ref, b_ref, rows, head_masks, tri, total_row)
            do_c = do_ref[rows, :]
            saved = sv_ref[cidx]
            st = jnp.where(block_mask, jnp.concatenate([saved] * GLA_HEADS, axis=0), 0.0)
            dst = dst_ref[...]
            hs = lambda a, h: a[GLA_CHUNK * h : GLA_CHUNK * (h + 1)]
            vs = lambda a, h: a[:, GLA_DV * h : GLA_DV * (h + 1)]
            dattn = jnp.concatenate([_dot_nt(vs(do_c, h), vs(v, h)) for h in range(GLA_HEADS)], axis=0)
            dattn = jnp.where(tri, dattn, 0.0)
            dv = jnp.concatenate([_dot_tn(hs(attn, h), vs(do_c, h)) for h in range(GLA_HEADS)], axis=1) + _dot_nt(kh, dst)
            dqt = _dot(do_c, st)
            dkt = jnp.zeros_like(dqt)
            for h in range(GLA_HEADS):
                dqt = dqt + jnp.where(head_masks[h], _dot(hs(dattn, h), kt), 0.0)
                dkt = dkt + jnp.where(head_masks[h], _dot_tn(hs(dattn, h), qt), 0.0)
            dkh = _dot(v, dst)
            dg = jnp.sum(dkh * kh, axis=0, keepdims=True) + jnp.sum(dst * st, axis=0, keepdims=True) * eg
            db = dqt * qt - dkt * kt - dkh * kh + jnp.where(total_row, dg, 0.0)
            dq = dqt * eb * (GLA_DK**-0.5)
            dk = dkt * enb + dkh * egb
            dqkv_ref[rows, :] = jnp.concatenate([dq, dk, dv], axis=1)
            db_ref[rows, :] = db
            dst_ref[...] = dst * eg + jnp.where(block_mask, _dot_tn(do_c, qt), 0.0)
            return carry

        lax.fori_loop(0, cpb, chunk, 0)

    wide = 2 * GLA_K_TOTAL + GLA_V_TOTAL
    return pl.pallas_call(
        body,
        name=name,
        grid=(nb,),
        in_specs=[q_spec, k_spec, v_spec, b_spec, o_spec, st_spec],
        out_specs=[pl.BlockSpec((tb, wide), lambda i: (rb(i), 0)), b_spec],
        out_shape=[jax.ShapeDtypeStruct((s, wide), F32), jax.ShapeDtypeStruct((s, GLA_K_TOTAL), F32)],
        scratch_shapes=[pltpu.VMEM((GLA_V_TOTAL, GLA_K_TOTAL), F32)],
        compiler_params=_cp(("arbitrary",)),
    )(z, z, z, b, do, states)


HALO = 8


def _halo_specs(s, ts, width, col):
    last = s // HALO - 1
    per = ts // HALO
    prev = pl.BlockSpec((HALO, width), lambda i: (jnp.maximum(i * per - 1, 0), col))
    nxt = pl.BlockSpec((HALO, width), lambda i: (jnp.minimum((i + 1) * per, last), col))
    return prev, nxt


def _group_ones():
    r = lax.broadcasted_iota(jnp.int32, (CONV_WIDTH, CONV_WIDTH), 0) >> 6
    c = lax.broadcasted_iota(jnp.int32, (CONV_WIDTH, CONV_WIDTH), 1) >> 6
    return (r == c).astype(BF16)


def _conv_terms(cc_ext, cu_ext, cw, valid):
    n = cc_ext.shape[0]
    hc = jnp.where(valid, cc_ext * cu_ext, 0.0)
    hc_prev = pltpu.roll(hc, 1, 0)
    hc_next = pltpu.roll(hc, n - 1, 0)
    conv = cw[0:1] * hc_prev + cw[1:2] * hc + cw[2:3] * hc_next
    return hc, hc_prev, hc_next, conv


def _ext(prev_ref, cur_ref, next_ref):
    return jnp.concatenate([prev_ref[...], cur_ref[...], next_ref[...]], axis=0)


def _valid_rows(ts, s):
    row = lax.broadcasted_iota(jnp.int32, (ts + 2 * HALO, 1), 0) + (pl.program_id(0) * ts - HALO)
    return (row >= 0) & (row < s)


def _head_norm(o, gn):
    out = []
    for h in range(GLA_HEADS):
        oh = o[:, GLA_DV * h : GLA_DV * (h + 1)]
        r = lax.rsqrt(jnp.mean(oh * oh, axis=-1, keepdims=True) + EPS)
        out.append((oh * r, r))
    return out


def _mix_fwd(z, o_f, o_b, conv_w, conv_norm, gla_norm, *, name):
    s = z.shape[0]
    ts = _rows(s)
    cprev, cnext = _halo_specs(s, ts, CONV_WIDTH, 1)
    uprev, unext = _halo_specs(s, ts, CONV_WIDTH, 2)

    def body(cb_ref, cc_ref, cu_ref, ccp_ref, ccn_ref, cup_ref, cun_ref, g_ref, of_ref, ob_ref, cw_ref, cn_ref, gn_ref, y_ref):
        valid = _valid_rows(ts, s)
        _, _, _, conv = _conv_terms(_ext(ccp_ref, cc_ref, ccn_ref), _ext(cup_ref, cu_ref, cun_ref), cw_ref[...], valid)
        yc = cb_ref[...] * conv[HALO : HALO + ts]
        ms = _dot_split(yc * yc, _group_ones()) * (1.0 / CONV_GROUP)
        y_conv = yc * lax.rsqrt(ms + EPS) * cn_ref[...]
        gate = g_ref[...]
        silu = gate * _sigmoid(gate)
        gn = gn_ref[...]
        y_gla = jnp.concatenate([oh * gn for oh, _ in _head_norm(of_ref[...] + ob_ref[...], gn)], axis=1) * silu
        y_ref[...] = jnp.concatenate([y_conv, y_gla], axis=1).astype(y_ref.dtype)

    col = lambda c, w=CONV_WIDTH: pl.BlockSpec((ts, w), lambda i: (i, c))
    return pl.pallas_call(
        body,
        name=name,
        grid=(s // ts,),
        in_specs=[col(0), col(1), col(2), cprev, cnext, uprev, unext, col(5), col(0), col(0),
                  pl.BlockSpec((CONV_K, CONV_WIDTH), lambda i: (0, 0)), pl.BlockSpec((1, CONV_WIDTH), lambda i: (0, 0)),
                  pl.BlockSpec((1, GLA_DV), lambda i: (0, 0))],
        out_specs=pl.BlockSpec((ts, D_MODEL), lambda i: (i, 0)),
        out_shape=jax.ShapeDtypeStruct((s, D_MODEL), _CD),
        compiler_params=_cp(("parallel",)),
    )(z, z, z, z, z, z, z, z, o_f, o_b, conv_w, conv_norm, gla_norm)


def _mix_bwd(z, o_f, o_b, dy, conv_w, conv_norm, gla_norm, *, name):
    s = z.shape[0]
    ts = _rows(s)
    halos = [_halo_specs(s, ts, CONV_WIDTH, c) for c in (0, 1, 2)]
    dprev, dnext = _halo_specs(s, ts, CONV_WIDTH, 0)

    def body(cb_ref, cc_ref, cu_ref, cbp_ref, cbn_ref, ccp_ref, ccn_ref, cup_ref, cun_ref, g_ref, of_ref, ob_ref,
             dyc_ref, dyg_ref, dyp_ref, dyn_ref, cw_ref, cn_ref, gn_ref, dconv_ref, dgate_ref, do_ref, dcw_ref, dcn_ref, dgn_ref):
        n = ts + 2 * HALO
        valid = _valid_rows(ts, s)
        cw = cw_ref[...]
        cn = cn_ref[...]
        ones = _group_ones()
        cb = _ext(cbp_ref, cb_ref, cbn_ref)
        cc = _ext(ccp_ref, cc_ref, ccn_ref)
        cu = _ext(cup_ref, cu_ref, cun_ref)
        dy = _ext(dyp_ref, dyc_ref, dyn_ref)
        hc, hc_prev, hc_next, conv = _conv_terms(cc, cu, cw, valid)
        yc = cb * conv
        r = lax.rsqrt(_dot_split(yc * yc, ones) * (1.0 / CONV_GROUP) + EPS)
        yh = yc * r
        dyh = dy * cn
        dyc = r * (dyh - yh * (_dot_split(dyh * yh, ones) * (1.0 / CONV_GROUP)))
        dconv = jnp.where(valid, dyc * cb, 0.0)
        dhc = cw[0:1] * pltpu.roll(dconv, n - 1, 0) + cw[1:2] * dconv + cw[2:3] * pltpu.roll(dconv, 1, 0)
        mid = lambda a: a[HALO : HALO + ts]
        dconv_ref[...] = jnp.concatenate([mid(dyc * conv), mid(dhc * cu), mid(dhc * cc)], axis=1).astype(dconv_ref.dtype)
        dconv_m = mid(dconv)
        colsum = lambda a: jnp.sum(a, axis=0, keepdims=True)
        dcw = jnp.concatenate([colsum(dconv_m * mid(hc_prev)), colsum(dconv_m * mid(hc)), colsum(dconv_m * mid(hc_next))], axis=0)
        dcn = colsum(mid(dy * yh))

        gate = g_ref[...]
        sg = _sigmoid(gate)
        silu = gate * sg
        gn = gn_ref[...]
        dyg = dyg_ref[...]
        don = dyg * silu
        heads = _head_norm(of_ref[...] + ob_ref[...], gn)
        on = jnp.concatenate([oh * gn for oh, _ in heads], axis=1)
        dgate_ref[...] = (dyg * on * (sg * (1.0 + gate * (1.0 - sg)))).astype(dgate_ref.dtype)
        dgn = jnp.zeros((1, GLA_DV), F32)
        dos = []
        for h, (oh, rh) in enumerate(heads):
            donh = don[:, GLA_DV * h : GLA_DV * (h + 1)]
            dgn = dgn + colsum(donh * oh)
            doh = donh * gn
            dos.append(rh * (doh - oh * jnp.mean(doh * oh, axis=-1, keepdims=True)))
        do_ref[...] = jnp.concatenate(dos, axis=1)

        first = pl.program_id(0) == 0

        @pl.when(first)
        def _():
            dcw_ref[...] = dcw
            dcn_ref[...] = dcn
            dgn_ref[...] = dgn

        @pl.when(jnp.logical_not(first))
        def _():
            dcw_ref[...] += dcw
            dcn_ref[...] += dcn
            dgn_ref[...] += dgn

    col = lambda c, w=CONV_WIDTH: pl.BlockSpec((ts, w), lambda i: (i, c))
    cw_spec = pl.BlockSpec((CONV_K, CONV_WIDTH), lambda i: (0, 0))
    cn_spec = pl.BlockSpec((1, CONV_WIDTH), lambda i: (0, 0))
    gn_spec = pl.BlockSpec((1, GLA_DV), lambda i: (0, 0))
    return pl.pallas_call(
        body,
        name=name,
        grid=(s // ts,),
        in_specs=[col(0), col(1), col(2), halos[0][0], halos[0][1], halos[1][0], halos[1][1], halos[2][0], halos[2][1],
                  col(5), col(0), col(0), col(0), col(1), dprev, dnext, cw_spec, cn_spec, gn_spec],
        out_specs=[pl.BlockSpec((ts, 3 * CONV_WIDTH), lambda i: (i, 0)), col(0), col(0), cw_spec, cn_spec, gn_spec],
        out_shape=[
            jax.ShapeDtypeStruct((s, 3 * CONV_WIDTH), _CD),
            jax.ShapeDtypeStruct((s, GLA_V_TOTAL), _CD),
            jax.ShapeDtypeStruct((s, GLA_V_TOTAL), F32),
            jax.ShapeDtypeStruct((CONV_K, CONV_WIDTH), F32),
            jax.ShapeDtypeStruct((1, CONV_WIDTH), F32),
            jax.ShapeDtypeStruct((1, GLA_DV), F32),
        ],
        compiler_params=_cp(("arbitrary",)),
    )(z, z, z, z, z, z, z, z, z, z, o_f, o_b, dy, dy, dy, dy, conv_w, conv_norm, gla_norm)


def _xa_probs(q_ref, kv_ref, h):
    qh = q_ref[:, XA_HEAD_DIM * h : XA_HEAD_DIM * (h + 1)]
    kh = kv_ref[:, XA_HEAD_DIM * h : XA_HEAD_DIM * (h + 1)]
    vh = kv_ref[:, D_MODEL + XA_HEAD_DIM * h : D_MODEL + XA_HEAD_DIM * (h + 1)]
    sc = _dot_nt(qh, kh) * (XA_HEAD_DIM**-0.5)
    e = jnp.exp(sc - jnp.max(sc, axis=-1, keepdims=True))
    return qh, kh, vh, e / jnp.sum(e, axis=-1, keepdims=True)


def _xattn_fwd(qx, kv, *, name):
    s = qx.shape[0]
    ts = _rows(s)

    def body(q_ref, kv_ref, o_ref):
        outs = []
        for h in range(XA_HEADS):
            _, _, vh, p = _xa_probs(q_ref, kv_ref, h)
            outs.append(_dot(p, vh))
        o_ref[...] = jnp.concatenate(outs, axis=1).astype(o_ref.dtype)

    return pl.pallas_call(
        body,
        name=name,
        grid=(s // ts,),
        in_specs=[pl.BlockSpec((ts, D_MODEL), lambda i: (i, 0)), pl.BlockSpec((N_MEM, 2 * D_MODEL), lambda i: (0, 0))],
        out_specs=pl.BlockSpec((ts, D_MODEL), lambda i: (i, 0)),
        out_shape=jax.ShapeDtypeStruct((s, D_MODEL), _CD),
        compiler_params=_cp(("parallel",)),
    )(qx, kv)


def _xattn_bwd(qx, kv, dox, *, name):
    s = qx.shape[0]
    ts = _rows(s)

    def body(q_ref, kv_ref, do_ref, dq_ref, dkv_ref):
        dqs, dks, dvs = [], [], []
        for h in range(XA_HEADS):
            qh, kh, vh, p = _xa_probs(q_ref, kv_ref, h)
            doh = do_ref[:, XA_HEAD_DIM * h : XA_HEAD_DIM * (h + 1)]
            dp = _dot_nt(doh, vh)
            ds = p * (dp - jnp.sum(dp * p, axis=-1, keepdims=True)) * (XA_HEAD_DIM**-0.5)
            dqs.append(_dot(ds, kh))
            dks.append(_dot_tn(ds, qh))
            dvs.append(_dot_tn(p, doh))
        dq_ref[...] = jnp.concatenate(dqs, axis=1).astype(dq_ref.dtype)
        dkv = jnp.concatenate(dks + dvs, axis=1)

        @pl.when(pl.program_id(0) == 0)
        def _():
            dkv_ref[...] = dkv

        @pl.when(pl.program_id(0) > 0)
        def _():
            dkv_ref[...] += dkv

    tile = pl.BlockSpec((ts, D_MODEL), lambda i: (i, 0))
    kv_spec = pl.BlockSpec((N_MEM, 2 * D_MODEL), lambda i: (0, 0))
    return pl.pallas_call(
        body,
        name=name,
        grid=(s // ts,),
        in_specs=[tile, kv_spec, tile],
        out_specs=[tile, kv_spec],
        out_shape=[jax.ShapeDtypeStruct((s, D_MODEL), _CD), jax.ShapeDtypeStruct((N_MEM, 2 * D_MODEL), F32)],
        compiler_params=_cp(("arbitrary",)),
    )(qx, kv, dox)


def _adamw_math(w, g, m, v):
    m = ADAM_B1 * m + (1.0 - ADAM_B1) * g
    v = ADAM_B2 * v + (1.0 - ADAM_B2) * (g * g)
    m_hat = m / (1.0 - ADAM_B1**ADAM_STEP)
    v_hat = v / (1.0 - ADAM_B2**ADAM_STEP)
    delta = -ADAM_LR * (m_hat / (jnp.sqrt(v_hat) + ADAM_EPS) + ADAM_WD * w)
    return delta, m, v


def _adamw(w, g, m, v, *, name):
    r, c = w.shape
    tr = _pick(r, (256, 128, 64, 32, 16, 8))

    def body(w_ref, g_ref, m_ref, v_ref, d_ref, nm_ref, nv_ref):
        d_ref[...], nm_ref[...], nv_ref[...] = _adamw_math(w_ref[...], g_ref[...], m_ref[...], v_ref[...])

    tile = pl.BlockSpec((tr, c), lambda i: (i, 0))
    return pl.pallas_call(
        body,
        name=name,
        grid=(r // tr,),
        in_specs=[tile] * 4,
        out_specs=[tile] * 3,
        out_shape=[jax.ShapeDtypeStruct((r, c), F32)] * 3,
        compiler_params=_cp(("parallel",)),
    )(w, g, m, v)


def _adamw_small(groups, *, name):
    n = len(groups)

    def body(*refs):
        ins, outs = refs[: 4 * n], refs[4 * n :]
        for i in range(n):
            w_ref, g_ref, m_ref, v_ref = ins[4 * i : 4 * i + 4]
            outs[3 * i][...], outs[3 * i + 1][...], outs[3 * i + 2][...] = _adamw_math(w_ref[...], g_ref[...], m_ref[...], v_ref[...])

    flat = [a for grp in groups for a in grp]
    vm = pl.BlockSpec(memory_space=pltpu.VMEM)
    res = pl.pallas_call(
        body,
        name=name,
        in_specs=[vm] * (4 * n),
        out_specs=[vm] * (3 * n),
        out_shape=[jax.ShapeDtypeStruct(grp[0].shape, F32) for grp in groups for _ in range(3)],
        compiler_params=_cp(),
    )(*flat)
    return [tuple(res[3 * i : 3 * i + 3]) for i in range(n)]


def _place():
    return lax.axis_index("x"), lax.axis_index("y"), lax.axis_index("c")


def _rel_chip(x, y, k):
    return (1 - x if k & 2 else x), (1 - y if k & 1 else y)


def _half(c, rh):
    return pl.ds(pl.multiple_of(c * rh, 16), rh)


def _gather_weights(pack):
    r, w = pack.shape
    rh = r // 2

    def body(p_ref, q_ref, send_sems, recv_sems, local_sem):
        x, y, c = _place()
        j = 2 * x + y
        rows = _half(c, rh)
        mine = pltpu.make_async_copy(p_ref, q_ref.at[j], local_sem)
        mine.start()

        def to_chip(k):
            cx, cy = _rel_chip(x, y, k)
            return pltpu.make_async_remote_copy(
                src_ref=p_ref.at[rows], dst_ref=q_ref.at[j, rows], send_sem=send_sems.at[k - 1], recv_sem=recv_sems.at[k - 1],
                device_id=(cx, cy, c), device_id_type=MESH)

        def to_sibling(k):
            cx, cy = _rel_chip(x, y, k)
            slot = q_ref.at[2 * cx + cy, rows]
            return pltpu.make_async_remote_copy(
                src_ref=slot, dst_ref=slot, send_sem=send_sems.at[2 + k], recv_sem=recv_sems.at[2 + k],
                device_id=(x, y, 1 - c), device_id_type=MESH)

        first = [to_chip(k) for k in range(1, N_CHIPS)]
        passed = [to_sibling(k) for k in range(1, N_CHIPS)]
        for cp in first:
            cp.start()
        for cp, fw in zip(first, passed):
            cp.wait_recv()
            fw.start()
        for fw in passed:
            fw.wait_recv()
        for cp in first + passed:
            cp.wait_send()
        mine.wait()

    return pl.pallas_call(
        body,
        name="gather_weights",
        in_specs=[ANY],
        out_specs=ANY,
        out_shape=jax.ShapeDtypeStruct((N_CHIPS, r, w), pack.dtype),
        scratch_shapes=[pltpu.SemaphoreType.DMA((6,)), pltpu.SemaphoreType.DMA((6,)), pltpu.SemaphoreType.DMA],
        compiler_params=pltpu.CompilerParams(has_side_effects=True),
    )(pack)


def _swap_halves(g):
    n, r, w = g.shape
    rh = r // 2

    def body(g_ref, o_ref, send_sem, recv_sem):
        x, y, c = _place()
        cp = pltpu.make_async_remote_copy(
            src_ref=g_ref.at[:, _half(1 - c, rh)], dst_ref=o_ref, send_sem=send_sem, recv_sem=recv_sem,
            device_id=(x, y, 1 - c), device_id_type=MESH)
        cp.start()
        cp.wait()

    return pl.pallas_call(
        body,
        name="grads_to_sibling",
        in_specs=[ANY],
        out_specs=ANY,
        out_shape=jax.ShapeDtypeStruct((n, rh, w), g.dtype),
        scratch_shapes=[pltpu.SemaphoreType.DMA, pltpu.SemaphoreType.DMA],
        compiler_params=pltpu.CompilerParams(has_side_effects=True),
    )(g)


def _chip_sums(g, got, where):
    n, r, w = g.shape
    rh = r // 2
    nt = rh // PACK_TILE

    def body(where_ref, g_ref, got_ref, o_ref):
        o_ref[...] = (g_ref[...] + got_ref[...]).astype(o_ref.dtype)

    return pl.pallas_call(
        body,
        name="chip_sums",
        grid_spec=pltpu.PrefetchScalarGridSpec(
            num_scalar_prefetch=1,
            grid=(n, nt),
            in_specs=[pl.BlockSpec((1, PACK_TILE, w), lambda a, i, wh: (a, wh[0] * nt + i, 0)),
                      pl.BlockSpec((1, PACK_TILE, w), lambda a, i, wh: (a, i, 0))],
            out_specs=pl.BlockSpec((1, PACK_TILE, w), lambda a, i, wh: (a, i, 0)),
        ),
        out_shape=jax.ShapeDtypeStruct((n, rh, w), _TD),
        compiler_params=_cp(("parallel", "parallel")),
    )(where, g, got)


def _exchange_chip_sums(h):
    n, rh, w = h.shape

    def body(h_ref, o_ref, send_sems, recv_sems):
        x, y, c = _place()
        j = 2 * x + y
        copies = []
        for k in range(1, N_CHIPS):
            cx, cy = _rel_chip(x, y, k)
            copies.append(pltpu.make_async_remote_copy(
                src_ref=h_ref.at[2 * cx + cy], dst_ref=o_ref.at[k - 1], send_sem=send_sems.at[k - 1], recv_sem=recv_sems.at[k - 1],
                device_id=(cx, cy, c), device_id_type=MESH))
        for cp in copies:
            cp.start()
        for cp in copies:
            cp.wait()

    return pl.pallas_call(
        body,
        name="chip_sums_exchange",
        in_specs=[ANY],
        out_specs=ANY,
        out_shape=jax.ShapeDtypeStruct((N_CHIPS - 1, rh, w), h.dtype),
        scratch_shapes=[pltpu.SemaphoreType.DMA((3,)), pltpu.SemaphoreType.DMA((3,))],
        compiler_params=pltpu.CompilerParams(has_side_effects=True),
    )(h)


def _shard_sum(g, got, others, where):
    n, r, w = g.shape
    rh = r // 2
    nt = rh // PACK_TILE

    def body(where_ref, g_ref, got_ref, oth_ref, o_ref):
        acc = g_ref[0] + got_ref[0]
        for k in range(N_CHIPS - 1):
            acc = acc + oth_ref[k].astype(F32)
        o_ref[...] = acc

    return pl.pallas_call(
        body,
        name="shard_sum",
        grid_spec=pltpu.PrefetchScalarGridSpec(
            num_scalar_prefetch=1,
            grid=(nt,),
            in_specs=[pl.BlockSpec((1, PACK_TILE, w), lambda i, wh: (wh[1], wh[0] * nt + i, 0)),
                      pl.BlockSpec((1, PACK_TILE, w), lambda i, wh: (wh[1], i, 0)),
                      pl.BlockSpec((N_CHIPS - 1, PACK_TILE, w), lambda i, wh: (0, i, 0))],
            out_specs=pl.BlockSpec((PACK_TILE, w), lambda i, wh: (i, 0)),
        ),
        out_shape=jax.ShapeDtypeStruct((rh, w), F32),
        compiler_params=_cp(("parallel",)),
    )(where, g, got, others)


def _join_halves(e):
    rh, w = e.shape

    def body(e_ref, o_ref, send_sem, recv_sem, local_sem):
        x, y, c = _place()
        rows = _half(c, rh)
        mine = pltpu.make_async_copy(e_ref, o_ref.at[rows], local_sem)
        mine.start()
        cp = pltpu.make_async_remote_copy(
            src_ref=e_ref, dst_ref=o_ref.at[rows], send_sem=send_sem, recv_sem=recv_sem, device_id=(x, y, 1 - c), device_id_type=MESH)
        cp.start()
        cp.wait()
        mine.wait()

    return pl.pallas_call(
        body,
        name="shard_to_sibling",
        in_specs=[ANY],
        out_specs=ANY,
        out_shape=jax.ShapeDtypeStruct((2 * rh, w), e.dtype),
        scratch_shapes=[pltpu.SemaphoreType.DMA, pltpu.SemaphoreType.DMA, pltpu.SemaphoreType.DMA],
        compiler_params=pltpu.CompilerParams(has_side_effects=True),
    )(e)


HBM = pl.BlockSpec(memory_space=pltpu.HBM)
SEM = pl.BlockSpec(memory_space=pltpu.SEMAPHORE)
EFFECT = pltpu.SideEffectType.DATAFLOW_SIDE_EFFECTING


def _in_hbm(a):
    return pltpu.with_memory_space_constraint(a, pltpu.HBM)


def _gather_copies(p_ref, land_ref, send_sems, recv_sems):
    rh = p_ref.shape[0] // 2
    x, y, c = _place()
    rows = _half(c, rh)
    copies = []
    for k in range(1, N_CHIPS):
        cx, cy = _rel_chip(x, y, k)
        copies.append(pltpu.make_async_remote_copy(
            src_ref=p_ref.at[rows], dst_ref=land_ref.at[2 * x + y, rows], send_sem=send_sems.at[k - 1], recv_sem=recv_sems.at[k - 1],
            device_id=(cx, cy, c), device_id_type=MESH))
    return copies


def _gather_start(pack, *, name):
    r, w = pack.shape

    def body(p_ref, land_ref, send_sems, recv_sems, p_thru, land_thru, token):
        for cp in _gather_copies(p_ref, land_ref, send_sems, recv_sems):
            cp.start()
        token[...] = jnp.zeros_like(token)

    return pl.pallas_call(
        body,
        name=name,
        out_shape=(pltpu.SemaphoreType.DMA((N_CHIPS - 1,)), pltpu.SemaphoreType.DMA((N_CHIPS - 1,)), pltpu.HBM((r, w), pack.dtype),
                   pltpu.HBM((N_CHIPS, r, w), pack.dtype), jax.ShapeDtypeStruct((8, 128), F32)),
        in_specs=(HBM, HBM),
        out_specs=(SEM, SEM, HBM, HBM, pl.BlockSpec(memory_space=pltpu.VMEM)),
        input_output_aliases={0: 2, 1: 3},
        compiler_params=pltpu.CompilerParams(has_side_effects=EFFECT),
    )(_in_hbm(pack), _in_hbm(lax.empty((N_CHIPS, r, w), pack.dtype)))


def _gather_wait(send_sems, recv_sems, pack, land, after, *, name):
    def body(p_ref, land_ref, send_sems, recv_sems, after_ref, p_out, land_out):
        for cp in _gather_copies(p_ref, land_ref, send_sems, recv_sems):
            cp.wait_send()
            cp.wait_recv()

    return pl.pallas_call(
        body,
        name=name,
        out_shape=(pltpu.HBM(pack.shape, pack.dtype), pltpu.HBM(land.shape, land.dtype)),
        in_specs=(HBM, HBM, SEM, SEM, ANY),
        out_specs=(HBM, HBM),
        input_output_aliases={0: 0, 1: 1},
        compiler_params=pltpu.CompilerParams(has_side_effects=EFFECT),
    )(pack, land, send_sems, recv_sems, after)


def _gather_spread(pack, land, *, name):
    r, w = pack.shape
    rh = r // 2

    def body(p_ref, land_ref, o_ref, send_sems, recv_sems, local_sems):
        x, y, c = _place()
        rows = _half(c, rh)
        copies = [pltpu.make_async_copy(p_ref, o_ref.at[2 * x + y], local_sems.at[0])]
        for k in range(1, N_CHIPS):
            cx, cy = _rel_chip(x, y, k)
            got = land_ref.at[2 * cx + cy, rows]
            copies.append(pltpu.make_async_copy(got, o_ref.at[2 * cx + cy, rows], local_sems.at[k]))
            copies.append(pltpu.make_async_remote_copy(
                src_ref=got, dst_ref=o_ref.at[2 * cx + cy, rows], send_sem=send_sems.at[k - 1], recv_sem=recv_sems.at[k - 1],
                device_id=(x, y, 1 - c), device_id_type=MESH))
        for cp in copies:
            cp.start()
        for cp in copies:
            cp.wait()

    return pl.pallas_call(
        body,
        name=name,
        in_specs=[ANY, ANY],
        out_specs=ANY,
        out_shape=jax.ShapeDtypeStruct((N_CHIPS, r, w), pack.dtype),
        scratch_shapes=[pltpu.SemaphoreType.DMA((N_CHIPS - 1,)), pltpu.SemaphoreType.DMA((N_CHIPS - 1,)), pltpu.SemaphoreType.DMA((N_CHIPS,))],
        compiler_params=pltpu.CompilerParams(has_side_effects=True),
    )(pack, land)


N_PARTS = 2 * (N_CHIPS - 1)


def _scatter_copies(lo_ref, g_ref, land_lo_ref, land_f_ref, send_sems, recv_sems, starting):
    rh = g_ref.shape[1] // 2
    x, y, c = _place()
    copies = []
    for k in range(1, N_CHIPS):
        cx, cy = _rel_chip(x, y, k)
        for i in range(2):
            part = 2 * (k - 1) + (c if starting else i)
            copies.append(pltpu.make_async_remote_copy(
                src_ref=lo_ref.at[2 * cx + cy, pl.ds(i * rh, rh)], dst_ref=land_lo_ref.at[part],
                send_sem=send_sems.at[2 * (k - 1) + i], recv_sem=recv_sems.at[part], device_id=(cx, cy, i), device_id_type=MESH))
    copies.append(pltpu.make_async_remote_copy(
        src_ref=g_ref.at[2 * x + y, _half(1 - c, rh)], dst_ref=land_f_ref, send_sem=send_sems.at[N_PARTS], recv_sem=recv_sems.at[N_PARTS],
        device_id=(x, y, 1 - c), device_id_type=MESH))
    return copies


def _scatter_start(g_lo, g, *, name):
    n, r, w = g.shape
    rh = r // 2

    def body(lo_ref, g_ref, land_lo_ref, land_f_ref, send_sems, recv_sems, lo_thru, g_thru, land_lo_thru, land_f_thru, token):
        for cp in _scatter_copies(lo_ref, g_ref, land_lo_ref, land_f_ref, send_sems, recv_sems, True):
            cp.start()
        token[...] = jnp.zeros_like(token)

    return pl.pallas_call(
        body,
        name=name,
        out_shape=(pltpu.SemaphoreType.DMA((N_PARTS + 1,)), pltpu.SemaphoreType.DMA((N_PARTS + 1,)), pltpu.HBM(g_lo.shape, g_lo.dtype),
                   pltpu.HBM(g.shape, g.dtype), pltpu.HBM((N_PARTS, rh, w), g_lo.dtype), pltpu.HBM((rh, w), g.dtype),
                   jax.ShapeDtypeStruct((8, 128), F32)),
        in_specs=(HBM, HBM, HBM, HBM),
        out_specs=(SEM, SEM, HBM, HBM, HBM, HBM, pl.BlockSpec(memory_space=pltpu.VMEM)),
        input_output_aliases={0: 2, 1: 3, 2: 4, 3: 5},
        compiler_params=pltpu.CompilerParams(has_side_effects=EFFECT),
    )(_in_hbm(g_lo), _in_hbm(g), _in_hbm(lax.empty((N_PARTS, rh, w), g_lo.dtype)), _in_hbm(lax.empty((rh, w), g.dtype)))


def _scatter_wait(send_sems, recv_sems, g_lo, g, land_lo, land_f, after, *, name):
    def body(lo_ref, g_ref, land_lo_ref, land_f_ref, send_sems, recv_sems, after_ref, o0, o1, o2, o3):
        for cp in _scatter_copies(lo_ref, g_ref, land_lo_ref, land_f_ref, send_sems, recv_sems, False):
            cp.wait_send()
            cp.wait_recv()

    arrays = (g_lo, g, land_lo, land_f)
    return pl.pallas_call(
        body,
        name=name,
        out_shape=tuple(pltpu.HBM(a.shape, a.dtype) for a in arrays),
        in_specs=(HBM, HBM, HBM, HBM, SEM, SEM, ANY),
        out_specs=(HBM, HBM, HBM, HBM),
        input_output_aliases={0: 0, 1: 1, 2: 2, 3: 3},
        compiler_params=pltpu.CompilerParams(has_side_effects=EFFECT),
    )(*arrays, send_sems, recv_sems, after)


def _scatter_sum(g, land_lo, land_f, where, *, name):
    n, r, w = g.shape
    rh = r // 2
    tr = _pick(rh, (256, 160, 80))
    nt = rh // tr

    def body(where_ref, g_ref, f_ref, lo_ref, o_ref):
        acc = g_ref[0] + f_ref[...]
        for part in range(N_PARTS):
            acc = acc + lo_ref[part].astype(F32)
        o_ref[...] = acc

    return pl.pallas_call(
        body,
        name=name,
        grid_spec=pltpu.PrefetchScalarGridSpec(
            num_scalar_prefetch=1,
            grid=(nt,),
            in_specs=[pl.BlockSpec((1, tr, w), lambda i, wh: (wh[1], wh[0] * nt + i, 0)),
                      pl.BlockSpec((tr, w), lambda i, wh: (i, 0)),
                      pl.BlockSpec((N_PARTS, tr, w), lambda i, wh: (0, i, 0))],
            out_specs=pl.BlockSpec((tr, w), lambda i, wh: (i, 0)),
        ),
        out_shape=jax.ShapeDtypeStruct((rh, w), F32),
        compiler_params=_cp(("parallel",)),
    )(where, g, land_f, land_lo)


def _join_all(halves, *, name):
    n = len(halves)

    def body(*refs):
        ins, outs = refs[:n], refs[n : 2 * n]
        send_sems, recv_sems, local_sems = refs[2 * n :]
        x, y, c = _place()
        copies = []
        for i, (e_ref, o_ref) in enumerate(zip(ins, outs)):
            rows = _half(c, e_ref.shape[0])
            copies.append(pltpu.make_async_copy(e_ref, o_ref.at[rows], local_sems.at[i]))
            copies.append(pltpu.make_async_remote_copy(
                src_ref=e_ref, dst_ref=o_ref.at[rows], send_sem=send_sems.at[i], recv_sem=recv_sems.at[i],
                device_id=(x, y, 1 - c), device_id_type=MESH))
        for cp in copies:
            cp.start()
        for cp in copies:
            cp.wait()

    return pl.pallas_call(
        body,
        name=name,
        in_specs=[ANY] * n,
        out_specs=[ANY] * n,
        out_shape=[jax.ShapeDtypeStruct((2 * e.shape[0], e.shape[1]), e.dtype) for e in halves],
        scratch_shapes=[pltpu.SemaphoreType.DMA((n,)), pltpu.SemaphoreType.DMA((n,)), pltpu.SemaphoreType.DMA((n,))],
        compiler_params=pltpu.CompilerParams(has_side_effects=True),
    )(*halves)


def _sum_small(small):
    n_dev = 8

    def body(s_ref, o_ref, all_ref, send_sems, recv_sems):
        x, y, c = _place()
        me = 4 * x + 2 * y + c
        all_ref[me] = s_ref[...]
        copies = []
        for k in range(1, n_dev):
            cx, cy = _rel_chip(x, y, k >> 1)
            cc = 1 - c if k & 1 else c
            copies.append(pltpu.make_async_remote_copy(
                src_ref=s_ref, dst_ref=all_ref.at[me], send_sem=send_sems.at[k - 1], recv_sem=recv_sems.at[k - 1],
                device_id=(cx, cy, cc), device_id_type=MESH))
        for cp in copies:
            cp.start()
        for cp in copies:
            cp.wait()
        acc = all_ref[0]
        for a in range(1, n_dev):
            acc = acc + all_ref[a]
        o_ref[...] = acc

    vm = pl.BlockSpec(memory_space=pltpu.VMEM)
    return pl.pallas_call(
        body,
        name="sum_small",
        in_specs=[vm],
        out_specs=vm,
        out_shape=jax.ShapeDtypeStruct(small.shape, F32),
        scratch_shapes=[pltpu.VMEM((n_dev,) + small.shape, F32), pltpu.SemaphoreType.DMA((n_dev - 1,)), pltpu.SemaphoreType.DMA((n_dev - 1,))],
        compiler_params=pltpu.CompilerParams(has_side_effects=True),
    )(small)


MATS = {"w_in": (776, True), "w_out": (256, False), "w_xq": (256, False), "w_xkv": (512, True), "w_xo": (256, False),
        "w_up": (1024, True), "w_down": (1024, False)}
GATHER_FIRST = ("w_in",)
GATHER_REST = ("w_out", "w_xq", "w_xkv", "w_xo", "w_up", "w_down")
GRAD_GROUPS = (("w_up", "w_down"), ("w_out", "w_xq", "w_xkv", "w_xo"), ("w_in",))


def _group_rows(names):
    n = sum(MATS[name][0] for name in names)
    return n + (-n) % 32


def _pack(pieces, rows):
    p = jnp.concatenate(pieces, axis=0) if len(pieces) > 1 else pieces[0]
    return jnp.pad(p, ((0, rows - p.shape[0]), (0, 0))) if rows > p.shape[0] else p


def _unpack(rows, names):
    out, off = {}, 0
    for name in names:
        out[name] = rows[off : off + MATS[name][0]]
        off += MATS[name][0]
    return out


SMALL = (
    ("mix_norm", 1024), ("conv_norm", 512), ("b_af", 256), ("b_ab", 256), ("gla_norm", 128), ("xa_norm", 1024), ("mem_norm", 1024),
    ("mlp_norm", 1024), ("final_norm", 1024), ("conv_w", 1536), ("w_af", 4096), ("w_ab", 4096), ("loss", 128),
)


def kernel(x, mem, mix_norm, w_in, conv_w, conv_norm, w_af, b_af, w_ab, b_ab, gla_norm, w_out, xa_norm, mem_norm, w_xq, w_xkv, w_xo, mlp_norm, w_up, w_down, final_norm, loss_target, m_mix_norm, m_w_in, m_conv_w, m_conv_norm, m_w_af, m_b_af, m_w_ab, m_b_ab, m_gla_norm, m_w_out, m_xa_norm, m_mem_norm, m_w_xq, m_w_xkv, m_w_xo, m_mlp_norm, m_w_up, m_w_down, m_final_norm, v_mix_norm, v_w_in, v_conv_w, v_conv_norm, v_w_af, v_b_af, v_w_ab, v_b_ab, v_gla_norm, v_w_out, v_xa_norm, v_mem_norm, v_w_xq, v_w_xkv, v_w_xo, v_mlp_norm, v_w_up, v_w_down, v_final_norm):
    given = dict(locals())
    xi, yi, ci = _place()
    chip = 2 * xi + yi
    where = jnp.stack([ci, chip]).astype(jnp.int32)

    lo = {name: (given[name][0].T if MATS[name][1] else given[name][0]).astype(_CD) for name in MATS}
    pack_rest = _pack([lo[name] for name in GATHER_REST], _group_rows(GATHER_REST))
    got_first = _gather_weights(_pack([lo[name] for name in GATHER_FIRST], _group_rows(GATHER_FIRST)))
    pack_rest, got_first = lax.optimization_barrier((pack_rest, got_first))
    rest_send, rest_recv, pack_rest, land_rest, rest_token = _gather_start(pack_rest, name="gather_rest_start")
    w_in_t = jnp.concatenate([got_first[a, : MATS["w_in"][0]] for a in range(N_CHIPS)]
                             + [jnp.zeros((Z_COLS - W_IN_COLS, D_MODEL), _CD)], axis=0)

    def placed(shard, full_shape, col):
        return lax.dynamic_update_slice(jnp.zeros(full_shape, F32), shard, (0, col)).reshape(-1, 128)

    sw = jnp.concatenate([
        placed(conv_w[0], (CONV_K, CONV_WIDTH), 128 * chip),
        placed(w_af[0], (GLA_LOWRANK, GLA_K_TOTAL), 64 * chip),
        placed(w_ab[0], (GLA_LOWRANK, GLA_K_TOTAL), 64 * chip),
    ], axis=0)
    sw = jnp.pad(sw, ((0, SMALL_ROWS - sw.shape[0]), (0, 0))) * (ci == 0).astype(F32)
    sw = _sum_small(sw)
    conv_w_full = sw[0:12].reshape(CONV_K, CONV_WIDTH)
    w_af_full = sw[12:44].reshape(GLA_LOWRANK, GLA_K_TOTAL)
    w_ab_full = sw[44:76].reshape(GLA_LOWRANK, GLA_K_TOTAL)
    waf_p = jnp.pad(w_af_full, ((0, 128 - GLA_LOWRANK), (0, 0))).astype(_CD)
    wab_p = jnp.pad(w_ab_full, ((GLA_LOWRANK, 128 - 2 * GLA_LOWRANK), (0, 0))).astype(_CD)

    mems, tgt = mem[0], loss_target[0]
    xs, _ = lax.optimization_barrier((x[0], rest_token))
    add_res = lambda acc, res: (acc + res,)

    h1 = _rms_fwd(xs, mix_norm, name="norm_mix")
    z = _mm(h1, w_in_t, mode="nt", name="proj_in", tm=512, tn=Z_COLS)
    b_f, b_b = _gate_fwd(z, waf_p, wab_p, b_af, b_ab, name="gates")
    o_f, st_f = _gla_fwd(z, b_f, rev=False, name="gla_scan_fwd")
    o_b, st_b = _gla_fwd(z, b_b, rev=True, name="gla_scan_rev")
    y = _mix_fwd(z, o_f, o_b, conv_w_full, conv_norm, gla_norm, name="mix_out")
    pack_rest, land_rest = _gather_wait(rest_send, rest_recv, pack_rest, land_rest, y, name="gather_rest_wait")
    gathered = _gather_spread(pack_rest, land_rest, name="gather_rest_spread")
    wt, off = {}, 0
    for name in GATHER_REST:
        wt[name] = jnp.concatenate([gathered[a, off : off + MATS[name][0]] for a in range(N_CHIPS)], axis=0)
        off += MATS[name][0]
    x1 = _mm(y, wt["w_out"], mode="nn", name="proj_out", extras=(xs,), epilogue=add_res)
    hx = _rms_fwd(x1, xa_norm, name="norm_xa")
    qx = _mm(hx, wt["w_xq"], mode="nn", name="proj_xq", out_dtypes=(_CD,))
    hmem = _rms_fwd(mems, mem_norm, name="norm_mem")
    kv = _mm(hmem, wt["w_xkv"], mode="nt", name="proj_xkv", out_dtypes=(_CD,))
    ox = _xattn_fwd(qx, kv, name="xattn")
    x2 = _mm(ox, wt["w_xo"], mode="nn", name="proj_xo", extras=(x1,), epilogue=add_res)
    hm = _rms_fwd(x2, mlp_norm, name="norm_mlp")
    act = _mm(hm, wt["w_up"], mode="nt", name="mlp_up", out_dtypes=(_CD,), epilogue=lambda acc: (jnp.square(jnp.maximum(acc, 0.0)),))
    x3 = _mm(act, wt["w_down"], mode="nn", name="mlp_down", extras=(x2,), epilogue=add_res, tm=512, tk=D_FF)
    dx3, dx3_lo, loss_part, g_final_norm = _final_loss(x3, final_norm.reshape(1, D_MODEL), tgt, name="loss_head")

    grads_t = {}

    def start_group(names, tag):
        rows = _group_rows(names)
        g = jnp.stack([_pack([grads_t[name][a * MATS[name][0] : (a + 1) * MATS[name][0]] for name in names], rows) for a in range(N_CHIPS)])
        return _scatter_start(g.astype(_TD), g, name="grads_" + tag + "_start")

    def finish_group(state, after, tag):
        send_sems, recv_sems, g_lo, g, land_lo, land_f, _ = state
        g_lo, g, land_lo, land_f = _scatter_wait(send_sems, recv_sems, g_lo, g, land_lo, land_f, after, name="grads_" + tag + "_wait")
        return _scatter_sum(g, land_lo, land_f, where, name="grads_" + tag + "_sum")

    du = _mm(dx3_lo, wt["w_down"], mode="nt", name="mlp_down_dx", out_dtypes=(_CD,), extras=(act,),
             epilogue=lambda acc, aa: (acc * (2.0 * jnp.sqrt(aa.astype(F32))),))
    grads_t["w_down"] = _mm_tn(act, dx3_lo, name="mlp_down_dw")
    grads_t["w_up"] = _mm_tn(du, hm, name="mlp_up_dw")
    mlp_state = start_group(GRAD_GROUPS[0], "mlp")
    du, _ = lax.optimization_barrier((du, mlp_state[-1]))
    dhm = _mm(du, wt["w_up"], mode="nn", name="mlp_up_dx", tm=512, tk=D_FF)
    dx2, dx2_lo, g_mlp_norm = _rms_bwd(x2, mlp_norm, dhm, dx3, name="norm_mlp_bwd")
    dox = _mm(dx2_lo, wt["w_xo"], mode="nt", name="proj_xo_dx", out_dtypes=(_CD,))
    grads_t["w_xo"] = _mm_tn(ox, dx2_lo, name="proj_xo_dw")
    dqx, dkv = _xattn_bwd(qx, kv, dox, name="xattn_bwd")
    grads_t["w_xq"] = _mm_tn(hx, dqx, name="proj_xq_dw")
    dhx = _mm(dqx, wt["w_xq"], mode="nt", name="proj_xq_dx")
    dx1, dx1_lo, g_xa_norm = _rms_bwd(x1, xa_norm, dhx, dx2, name="norm_xa_bwd")
    dkv_lo = dkv.astype(_CD)
    grads_t["w_xkv"] = _mm_tn(dkv_lo, hmem, name="proj_xkv_dw")
    dhmem = _mm(dkv_lo, wt["w_xkv"], mode="nn", name="proj_xkv_dx")
    (g_mem_norm,) = _rms_bwd(mems, mem_norm, dhmem, name="norm_mem_bwd", want_dx=False, want_lo=False)
    dy = _mm(dx1_lo, wt["w_out"], mode="nt", name="proj_out_dx")
    grads_t["w_out"] = _mm_tn(y, dx1_lo, name="proj_out_dw")
    attn_state = start_group(GRAD_GROUPS[1], "attn")
    dy, _ = lax.optimization_barrier((dy, attn_state[-1]))
    dz_conv, dz_gate, do, g_conv_w, g_conv_norm, g_gla_norm = _mix_bwd(z, o_f, o_b, dy, conv_w_full, conv_norm, gla_norm, name="mix_out_bwd")
    dqkv_f, db_f = _gla_bwd(z, b_f, do, st_f, rev=False, name="gla_scan_fwd_bwd")
    dqkv_b, db_b = _gla_bwd(z, b_b, do, st_b, rev=True, name="gla_scan_rev_bwd")
    dqkv, dlr, g_waf_p, g_wab_p, g_b_af, g_b_ab = _gate_bwd(z, waf_p, wab_p, b_af, b_ab, db_f, db_b, dqkv_f, dqkv_b, name="gates_bwd")
    dz = jnp.concatenate([dz_conv, dqkv, dz_gate, dlr], axis=1)
    grads_t["w_in"] = _mm_tn(dz, h1, name="proj_in_dw")
    in_state = start_group(GRAD_GROUPS[2], "in")
    dh1 = _mm(dz, w_in_t, mode="nn", name="proj_in_dx", tm=512, tk=Z_COLS)
    grad_x, g_mix_norm = _rms_bwd(xs, mix_norm, dh1, dx1, name="norm_mix_bwd", want_lo=False)

    half_mlp = finish_group(mlp_state, grad_x, "mlp")
    half_attn = finish_group(attn_state, half_mlp, "attn")
    half_in = finish_group(in_state, half_attn, "in")
    g_shard = {}
    for names, rows in zip(GRAD_GROUPS, _join_all([half_mlp, half_attn, half_in], name="shards_to_sibling")):
        for name, g in _unpack(rows, names).items():
            g_shard[name] = g.T if MATS[name][1] else g

    small_vals = dict(mix_norm=g_mix_norm, conv_norm=g_conv_norm, b_af=g_b_af, b_ab=g_b_ab, gla_norm=g_gla_norm, xa_norm=g_xa_norm,
                      mem_norm=g_mem_norm, mlp_norm=g_mlp_norm, final_norm=g_final_norm, conv_w=g_conv_w,
                      w_af=g_waf_p[0:GLA_LOWRANK], w_ab=g_wab_p[GLA_LOWRANK : 2 * GLA_LOWRANK], loss=loss_part)
    small = jnp.concatenate([small_vals[name].reshape(-1, 128) for name, _ in SMALL], axis=0)
    small = _sum_small(jnp.pad(small, ((0, SMALL_ROWS - small.shape[0]), (0, 0))))
    g_small, off = {}, 0
    for name, n in SMALL:
        g_small[name] = small[off : off + n // 128]
        off += n // 128
    loss = g_small["loss"][0, 0]
    g_small["conv_w"] = lax.dynamic_slice(g_small["conv_w"].reshape(CONV_K, CONV_WIDTH), (0, 128 * chip), (CONV_K, 128))
    g_small["w_af"] = lax.dynamic_slice(g_small["w_af"].reshape(GLA_LOWRANK, GLA_K_TOTAL), (0, 64 * chip), (GLA_LOWRANK, 64))
    g_small["w_ab"] = lax.dynamic_slice(g_small["w_ab"].reshape(GLA_LOWRANK, GLA_K_TOTAL), (0, 64 * chip), (GLA_LOWRANK, 64))

    names = ["mix_norm", "w_in", "conv_w", "conv_norm", "w_af", "b_af", "w_ab", "b_ab", "gla_norm", "w_out", "xa_norm", "mem_norm",
             "w_xq", "w_xkv", "w_xo", "mlp_norm", "w_up", "w_down", "final_norm"]
    big_names = list(MATS)
    as2d = lambda a: a.reshape(1, -1) if a.ndim == 1 else a.reshape(a.shape[-2:])
    grads, deltas, new_m, new_v = {}, {}, {}, {}
    for name in big_names:
        grads[name] = g_shard[name]
        deltas[name], new_m[name], new_v[name] = _adamw(as2d(given[name]), g_shard[name], as2d(given["m_" + name]),
                                                         as2d(given["v_" + name]), name="adamw_" + name)
    small_names = [name for name in names if name not in big_names]
    groups = []
    for name in small_names:
        grads[name] = g_small[name].reshape(as2d(given[name]).shape)
        groups.append((as2d(given[name]), grads[name], as2d(given["m_" + name]), as2d(given["v_" + name])))
    for name, res in zip(small_names, _adamw_small(groups, name="adamw_small")):
        deltas[name], new_m[name], new_v[name] = res

    like = lambda name, a: a.reshape(given[name].shape)
    return (loss, grad_x[None], *[like(n, grads[n]) for n in names], *[like(n, deltas[n]) for n in names],
            *[like(n, new_m[n]) for n in names], *[like(n, new_v[n]) for n in names])
```

```python
import functools

import jax
import jax.numpy as jnp
from jax import lax
from jax.experimental import pallas as pl
from jax.experimental.pallas import tpu as pltpu

F32 = jnp.float32
BF16 = jnp.bfloat16
_CD = jnp.bfloat16
_TD = jnp.bfloat16

D_MODEL = 1024
N_MEM = 256
CONV_WIDTH = 512
CONV_GROUP = 64
CONV_K = 3
GLA_HEADS = 4
GLA_DK = 64
GLA_DV = 128
GLA_K_TOTAL = 256
GLA_V_TOTAL = 512
GLA_LOWRANK = 16
GLA_GATE_SCALE = 1.0 / 16.0
GLA_CHUNK = 64
XA_HEADS = 4
XA_HEAD_DIM = 256
D_FF = 4096
EPS = 1e-6
W_IN_COLS = 3104
Z_COLS = 3200
LR_COL = 3072

ADAM_LR = 0.001
ADAM_B1 = 0.9
ADAM_B2 = 0.999
ADAM_EPS = 1e-08
ADAM_WD = 0.01
ADAM_STEP = 10

N_CHIPS = 4
PACK_W = 1024
PACK_ROWS = 4160
PACK_TILE = 160
SMALL_ROWS = 128

_TS = 512
_VMEM = 44 * 1024 * 1024
MESH = pl.DeviceIdType.MESH
ANY = pl.BlockSpec(memory_space=pl.ANY)


def _cp(sem=None, **kw):
    return pltpu.CompilerParams(dimension_semantics=sem, vmem_limit_bytes=_VMEM, **kw)


def _dot(a, b):
    return jnp.dot(a.astype(_CD), b.astype(_CD), preferred_element_type=F32)


def _dot_nt(a, b):
    return lax.dot_general(a.astype(_CD), b.astype(_CD), (((1,), (1,)), ((), ())), preferred_element_type=F32)


def _dot_tn(a, b):
    return lax.dot_general(a.astype(_CD), b.astype(_CD), (((0,), (0,)), ((), ())), preferred_element_type=F32)


def _dot_split(x, ones):
    hi = x.astype(BF16)
    r = x - hi.astype(F32)
    mid = r.astype(BF16)
    lo = (r - mid.astype(F32)).astype(BF16)
    d = lambda p: jnp.dot(p, ones, preferred_element_type=F32)
    return d(hi) + d(mid) + d(lo)


def _pick(n, cands=(1024, 640, 512, 256, 128)):
    for t in cands:
        if n % t == 0:
            return t
    return n


def _rows(s):
    return min(_TS, s)


def _sigmoid(v):
    e = jnp.exp(-jnp.abs(v))
    return jnp.where(v >= 0, 1.0 / (1.0 + e), e / (1.0 + e))


def _mm(a, b, *, mode, name, out_dtypes=(F32,), extras=(), epilogue=None, tm=None, tn=None, tk=None):
    m, k = a.shape
    n = b.shape[1] if mode == "nn" else b.shape[0]
    tm = min(m, tm or 1024)
    tn = tn or _pick(n)
    tk = tk or _pick(k)
    nk = k // tk
    n_ex, n_out = len(extras), len(out_dtypes)

    def body(*refs):
        a_ref, b_ref = refs[:2]
        ex = refs[2 : 2 + n_ex]
        outs = refs[2 + n_ex : 2 + n_ex + n_out]
        part = _dot(a_ref[...], b_ref[...]) if mode == "nn" else _dot_nt(a_ref[...], b_ref[...])

        def finish(acc):
            res = epilogue(acc, *[e[...] for e in ex]) if epilogue else (acc,)
            for o, r in zip(outs, res):
                o[...] = r.astype(o.dtype)

        if nk == 1:
            finish(part)
        else:
            acc_ref = refs[-1]
            kk = pl.program_id(2)

            @pl.when(kk == 0)
            def _():
                acc_ref[...] = part

            @pl.when(kk > 0)
            def _():
                acc_ref[...] += part

            @pl.when(kk == nk - 1)
            def _():
                finish(acc_ref[...])

    b_spec = pl.BlockSpec((tk, tn), lambda i, j, kk: (kk, j)) if mode == "nn" else pl.BlockSpec((tn, tk), lambda i, j, kk: (j, kk))
    tile = pl.BlockSpec((tm, tn), lambda i, j, kk: (i, j))
    out = pl.pallas_call(
        body,
        name=name,
        grid=(m // tm, n // tn, nk),
        in_specs=[pl.BlockSpec((tm, tk), lambda i, j, kk: (i, kk)), b_spec] + [tile] * n_ex,
        out_specs=[tile] * n_out,
        out_shape=[jax.ShapeDtypeStruct((m, n), dt) for dt in out_dtypes],
        scratch_shapes=[pltpu.VMEM((tm, tn), F32)] if nk > 1 else [],
        compiler_params=_cp(("parallel", "parallel", "arbitrary")),
    )(a, b, *extras)
    return out[0] if n_out == 1 else out


def _mm_tn(a, b, *, name):
    s, m = a.shape
    n = b.shape[1]
    cap = max(128, (1 << 20) // n)
    tm = _pick(m, tuple(t for t in (512, 640, 256, 128) if t <= max(cap, 128)))
    ts = min(s, 1 << (((1 << 22) // n).bit_length() - 1))
    ns = s // ts

    def body(a_ref, b_ref, o_ref):
        part = _dot_tn(a_ref[...], b_ref[...])
        if ns == 1:
            o_ref[...] = part
        else:
            ss = pl.program_id(1)

            @pl.when(ss == 0)
            def _():
                o_ref[...] = part

            @pl.when(ss > 0)
            def _():
                o_ref[...] += part

    return pl.pallas_call(
        body,
        name=name,
        grid=(m // tm, ns),
        in_specs=[pl.BlockSpec((ts, tm), lambda i, ss: (ss, i)), pl.BlockSpec((ts, n), lambda i, ss: (ss, 0))],
        out_specs=pl.BlockSpec((tm, n), lambda i, ss: (i, 0)),
        out_shape=jax.ShapeDtypeStruct((m, n), F32),
        compiler_params=_cp(("parallel", "arbitrary")),
    )(a, b)


def _rms_fwd(x, g, *, name):
    s, d = x.shape
    ts = _rows(s)

    def body(x_ref, g_ref, o_ref):
        xf = x_ref[...]
        r = lax.rsqrt(jnp.mean(xf * xf, axis=-1, keepdims=True) + EPS)
        o_ref[...] = (xf * r * g_ref[...]).astype(o_ref.dtype)

    return pl.pallas_call(
        body,
        name=name,
        grid=(s // ts,),
        in_specs=[pl.BlockSpec((ts, d), lambda i: (i, 0)), pl.BlockSpec((1, d), lambda i: (0, 0))],
        out_specs=pl.BlockSpec((ts, d), lambda i: (i, 0)),
        out_shape=jax.ShapeDtypeStruct((s, d), _CD),
        compiler_params=_cp(("parallel",)),
    )(x, g)


def _rms_bwd(x, g, dy, dres=None, *, name, want_dx=True, want_lo=True):
    s, d = x.shape
    ts = _rows(s)
    has_res = dres is not None

    def body(*refs):
        x_ref, g_ref, dy_ref = refs[:3]
        pos = 3
        dres_ref = refs[pos] if has_res else None
        pos += has_res
        dx_ref = refs[pos] if want_dx else None
        pos += want_dx
        lo_ref = refs[pos] if want_lo else None
        pos += want_lo
        dg_ref = refs[pos]
        xf = x_ref[...]
        r = lax.rsqrt(jnp.mean(xf * xf, axis=-1, keepdims=True) + EPS)
        xh = xf * r
        dyf = dy_ref[...]
        part = jnp.sum(dyf * xh, axis=0, keepdims=True)

        @pl.when(pl.program_id(0) == 0)
        def _():
            dg_ref[...] = part

        @pl.when(pl.program_id(0) > 0)
        def _():
            dg_ref[...] += part

        if want_dx or want_lo:
            dxh = dyf * g_ref[...]
            dx = r * (dxh - xh * jnp.mean(dxh * xh, axis=-1, keepdims=True))
            if has_res:
                dx = dx + dres_ref[...]
            if want_dx:
                dx_ref[...] = dx
            if want_lo:
                lo_ref[...] = dx.astype(lo_ref.dtype)

    tile = pl.BlockSpec((ts, d), lambda i: (i, 0))
    vec = pl.BlockSpec((1, d), lambda i: (0, 0))
    out_specs, out_shape = [], []
    if want_dx:
        out_specs.append(tile)
        out_shape.append(jax.ShapeDtypeStruct((s, d), F32))
    if want_lo:
        out_specs.append(tile)
        out_shape.append(jax.ShapeDtypeStruct((s, d), _CD))
    out_specs.append(vec)
    out_shape.append(jax.ShapeDtypeStruct((1, d), F32))
    return pl.pallas_call(
        body,
        name=name,
        grid=(s // ts,),
        in_specs=[tile, vec, tile] + ([tile] if has_res else []),
        out_specs=out_specs,
        out_shape=out_shape,
        compiler_params=_cp(("arbitrary",)),
    )(x, g, dy, *([dres] if has_res else []))


def _final_loss(x3, g, tgt, *, name):
    s, d = x3.shape
    ts = _rows(s)

    def body(x_ref, g_ref, t_ref, dx_ref, lo_ref, loss_ref, dg_ref):
        xf = x_ref[...]
        r = lax.rsqrt(jnp.mean(xf * xf, axis=-1, keepdims=True) + EPS)
        xh = xf * r
        gg = g_ref[...]
        err = xh * gg - t_ref[...]
        lpart = jnp.zeros((1, 128), F32) + 0.5 * jnp.sum(jnp.mean(err * err, axis=-1, keepdims=True))
        dy = err * (1.0 / d)
        gpart = jnp.sum(dy * xh, axis=0, keepdims=True)

        @pl.when(pl.program_id(0) == 0)
        def _():
            loss_ref[...] = lpart
            dg_ref[...] = gpart

        @pl.when(pl.program_id(0) > 0)
        def _():
            loss_ref[...] += lpart
            dg_ref[...] += gpart

        dxh = dy * gg
        dx = r * (dxh - xh * jnp.mean(dxh * xh, axis=-1, keepdims=True))
        dx_ref[...] = dx
        lo_ref[...] = dx.astype(lo_ref.dtype)

    tile = pl.BlockSpec((ts, d), lambda i: (i, 0))
    vec = pl.BlockSpec((1, d), lambda i: (0, 0))
    return pl.pallas_call(
        body,
        name=name,
        grid=(s // ts,),
        in_specs=[tile, vec, tile],
        out_specs=[tile, tile, pl.BlockSpec((1, 128), lambda i: (0, 0)), vec],
        out_shape=[
            jax.ShapeDtypeStruct((s, d), F32),
            jax.ShapeDtypeStruct((s, d), _CD),
            jax.ShapeDtypeStruct((1, 128), F32),
            jax.ShapeDtypeStruct((1, d), F32),
        ],
        compiler_params=_cp(("arbitrary",)),
    )(x3, g, tgt)


def _chunk_scan(v, row_in_chunk, suffix):
    t = v.shape[0]
    step = 1
    while step < GLA_CHUNK:
        if suffix:
            v = v + jnp.where(row_in_chunk < GLA_CHUNK - step, pltpu.roll(v, t - step, 0), 0.0)
        else:
            v = v + jnp.where(row_in_chunk >= step, pltpu.roll(v, step, 0), 0.0)
        step *= 2
    return v


def _gate_pre(lr, w_ref, b_ref):
    return _dot(lr, w_ref[...]) + b_ref[...]


def _gate_fwd(z, waf, wab, baf, bab, *, name):
    s = z.shape[0]
    ts = _rows(s)

    def body(lr_ref, waf_ref, wab_ref, baf_ref, bab_ref, bf_ref, bb_ref):
        lr = lr_ref[...]
        ric = lax.broadcasted_iota(jnp.int32, (ts, GLA_K_TOTAL), 0) & (GLA_CHUNK - 1)
        for w_ref, b_ref, o_ref, suffix in ((waf_ref, baf_ref, bf_ref, False), (wab_ref, bab_ref, bb_ref, True)):
            pre = _gate_pre(lr, w_ref, b_ref)
            la = (jnp.minimum(pre, 0.0) - jnp.log(1.0 + jnp.exp(-jnp.abs(pre)))) * GLA_GATE_SCALE
            o_ref[...] = _chunk_scan(la, ric, suffix)

    wspec = pl.BlockSpec((128, GLA_K_TOTAL), lambda i: (0, 0))
    bspec = pl.BlockSpec((1, GLA_K_TOTAL), lambda i: (0, 0))
    tile = pl.BlockSpec((ts, GLA_K_TOTAL), lambda i: (i, 0))
    return pl.pallas_call(
        body,
        name=name,
        grid=(s // ts,),
        in_specs=[pl.BlockSpec((ts, 128), lambda i: (i, LR_COL // 128)), wspec, wspec, bspec, bspec],
        out_specs=[tile, tile],
        out_shape=[jax.ShapeDtypeStruct((s, GLA_K_TOTAL), F32)] * 2,
        compiler_params=_cp(("parallel",)),
    )(z, waf, wab, baf, bab)


def _gate_bwd(z, waf, wab, baf, bab, dbf, dbb, dqkv_f, dqkv_b, *, name):
    s = z.shape[0]
    ts = _rows(s)

    def body(lr_ref, waf_ref, wab_ref, baf_ref, bab_ref, dbf_ref, dbb_ref, gf_ref, gb_ref, dqkv_ref, dlr_ref, dwf_ref, dwb_ref, dbaf_ref, dbab_ref):
        lr = lr_ref[...]
        ric = lax.broadcasted_iota(jnp.int32, (ts, GLA_K_TOTAL), 0) & (GLA_CHUNK - 1)
        first = pl.program_id(0) == 0
        dlr = None
        for w_ref, b_ref, db_ref, dw_ref, dbias_ref, suffix in (
            (waf_ref, baf_ref, dbf_ref, dwf_ref, dbaf_ref, True),
            (wab_ref, bab_ref, dbb_ref, dwb_ref, dbab_ref, False),
        ):
            pre = _gate_pre(lr, w_ref, b_ref)
            dla = _chunk_scan(db_ref[...], ric, suffix)
            dpre = dla * GLA_GATE_SCALE * _sigmoid(-pre)
            part = _dot_nt(dpre, w_ref[...])
            dlr = part if dlr is None else dlr + part
            dw = _dot_tn(lr, dpre)
            dbias = jnp.sum(dpre, axis=0, keepdims=True)

            @pl.when(first)
            def _():
                dw_ref[...] = dw
                dbias_ref[...] = dbias

            @pl.when(jnp.logical_not(first))
            def _():
                dw_ref[...] += dw
                dbias_ref[...] += dbias

        dlr_ref[...] = dlr.astype(dlr_ref.dtype)
        dqkv_ref[...] = (gf_ref[...] + gb_ref[...]).astype(dqkv_ref.dtype)

    wspec = pl.BlockSpec((128, GLA_K_TOTAL), lambda i: (0, 0))
    bspec = pl.BlockSpec((1, GLA_K_TOTAL), lambda i: (0, 0))
    tile = pl.BlockSpec((ts, GLA_K_TOTAL), lambda i: (i, 0))
    wide = pl.BlockSpec((ts, 2 * GLA_K_TOTAL + GLA_V_TOTAL), lambda i: (i, 0))
    return pl.pallas_call(
        body,
        name=name,
        grid=(s // ts,),
        in_specs=[pl.BlockSpec((ts, 128), lambda i: (i, LR_COL // 128)), wspec, wspec, bspec, bspec, tile, tile, wide, wide],
        out_specs=[wide, pl.BlockSpec((ts, 128), lambda i: (i, 0)), wspec, wspec, bspec, bspec],
        out_shape=[
            jax.ShapeDtypeStruct((s, 2 * GLA_K_TOTAL + GLA_V_TOTAL), _CD),
            jax.ShapeDtypeStruct((s, 128), _CD),
            jax.ShapeDtypeStruct((128, GLA_K_TOTAL), F32),
            jax.ShapeDtypeStruct((128, GLA_K_TOTAL), F32),
            jax.ShapeDtypeStruct((1, GLA_K_TOTAL), F32),
            jax.ShapeDtypeStruct((1, GLA_K_TOTAL), F32),
        ],
        compiler_params=_cp(("arbitrary",)),
    )(z, waf, wab, baf, bab, dbf, dbb, dqkv_f, dqkv_b)


def _gla_masks(rev):
    lane_head = lax.broadcasted_iota(jnp.int32, (1, GLA_K_TOTAL), 1) >> 6
    head_masks = [lane_head == h for h in range(GLA_HEADS)]
    st_rows = lax.broadcasted_iota(jnp.int32, (GLA_V_TOTAL, GLA_K_TOTAL), 0) >> 7
    st_lanes = lax.broadcasted_iota(jnp.int32, (GLA_V_TOTAL, GLA_K_TOTAL), 1) >> 6
    block_mask = st_rows == st_lanes
    t = lax.broadcasted_iota(jnp.int32, (GLA_HEADS * GLA_CHUNK, GLA_CHUNK), 0) & (GLA_CHUNK - 1)
    u = lax.broadcasted_iota(jnp.int32, (GLA_HEADS * GLA_CHUNK, GLA_CHUNK), 1)
    tri = (u > t) if rev else (u <= t)
    row = lax.broadcasted_iota(jnp.int32, (GLA_CHUNK, GLA_K_TOTAL), 0)
    total_row = row == (0 if rev else GLA_CHUNK - 1)
    return head_masks, block_mask, tri, total_row


def _gla_chunk_terms(q_ref, k_ref, v_ref, b_ref, rows, head_masks, tri, total_row):
    q = q_ref[rows, :] * (GLA_DK**-0.5)
    k = k_ref[rows, :]
    v = v_ref[rows, :]
    b = b_ref[rows, :]
    eb = jnp.exp(b)
    enb = jnp.exp(-b)
    g = jnp.sum(jnp.where(total_row, b, 0.0), axis=0, keepdims=True)
    egb = jnp.exp(g - b)
    qt = q * eb
    kt = k * enb
    kh = k * egb
    q_heads = jnp.concatenate([jnp.where(m, qt, 0.0) for m in head_masks], axis=0)
    attn = jnp.where(tri, _dot_nt(q_heads, kt), 0.0)
    return v, eb, enb, egb, jnp.exp(g), qt, kt, kh, attn


def _gla_specs(s, tb, rev_blocks):
    nb = s // tb
    rb = (lambda i: nb - 1 - i) if rev_blocks else (lambda i: i)
    q_spec = pl.BlockSpec((tb, GLA_K_TOTAL), lambda i: (rb(i), 1536 // GLA_K_TOTAL))
    k_spec = pl.BlockSpec((tb, GLA_K_TOTAL), lambda i: (rb(i), 1792 // GLA_K_TOTAL))
    v_spec = pl.BlockSpec((tb, GLA_V_TOTAL), lambda i: (rb(i), 2048 // GLA_V_TOTAL))
    b_spec = pl.BlockSpec((tb, GLA_K_TOTAL), lambda i: (rb(i), 0))
    o_spec = pl.BlockSpec((tb, GLA_V_TOTAL), lambda i: (rb(i), 0))
    st_spec = pl.BlockSpec((tb // GLA_CHUNK, GLA_DV, GLA_K_TOTAL), lambda i: (rb(i), 0, 0))
    return nb, q_spec, k_spec, v_spec, b_spec, o_spec, st_spec


def _gla_fwd(z, b, *, rev, name):
    s = z.shape[0]
    tb = _rows(s)
    cpb = tb // GLA_CHUNK
    nb, q_spec, k_spec, v_spec, b_spec, o_spec, st_spec = _gla_specs(s, tb, rev)

    def body(q_ref, k_ref, v_ref, b_ref, o_ref, sv_ref, st_ref):
        head_masks, block_mask, tri, total_row = _gla_masks(rev)

        @pl.when(pl.program_id(0) == 0)
        def _():
            st_ref[...] = jnp.zeros_like(st_ref)

        def chunk(ci, carry):
            cidx = cpb - 1 - ci if rev else ci
            rows = pl.ds(pl.multiple_of(cidx * GLA_CHUNK, GLA_CHUNK), GLA_CHUNK)
            v, _, _, _, eg, qt, _, kh, attn = _gla_chunk_terms(q_ref, k_ref, v_ref, b_ref, rows, head_masks, tri, total_row)
            o = jnp.concatenate(
                [_dot(attn[GLA_CHUNK * h : GLA_CHUNK * (h + 1)], v[:, GLA_DV * h : GLA_DV * (h + 1)]) for h in range(GLA_HEADS)], axis=1
            )
            st = st_ref[...]
            o_ref[rows, :] = o + _dot_nt(qt, st)
            sv_ref[cidx] = st[0:128] + st[128:256] + st[256:384] + st[384:512]
            st_ref[...] = st * eg + jnp.where(block_mask, _dot_tn(v, kh), 0.0)
            return carry

        lax.fori_loop(0, cpb, chunk, 0)

    return pl.pallas_call(
        body,
        name=name,
        grid=(nb,),
        in_specs=[q_spec, k_spec, v_spec, b_spec],
        out_specs=[o_spec, st_spec],
        out_shape=[jax.ShapeDtypeStruct((s, GLA_V_TOTAL), F32), jax.ShapeDtypeStruct((s // GLA_CHUNK, GLA_DV, GLA_K_TOTAL), F32)],
        scratch_shapes=[pltpu.VMEM((GLA_V_TOTAL, GLA_K_TOTAL), F32)],
        compiler_params=_cp(("arbitrary",)),
    )(z, z, z, b)


def _gla_bwd(z, b, do, states, *, rev, name):
    s = z.shape[0]
    tb = _rows(s)
    cpb = tb // GLA_CHUNK
    nb, q_spec, k_spec, v_spec, b_spec, o_spec, st_spec = _gla_specs(s, tb, not rev)
    rb = (lambda i: nb - 1 - i) if not rev else (lambda i: i)

    def body(q_ref, k_ref, v_ref, b_ref, do_ref, sv_ref, dqkv_ref, db_ref, dst_ref):
        head_masks, block_mask, tri, total_row = _gla_masks(rev)

        @pl.when(pl.program_id(0) == 0)
        def _():
            dst_ref[...] = jnp.zeros_like(dst_ref)

        def chunk(ci, carry):
            cidx = ci if rev else cpb - 1 - ci
            rows = pl.ds(pl.multiple_of(cidx * GLA_CHUNK, GLA_CHUNK), GLA_CHUNK)
            v, eb, enb, egb, eg, qt, kt, kh, attn = _gla_chunk_terms(q_ref, k_ref, v_ref, b_ref, rows, head_masks, tri, total_row)
            do_c = do_ref[rows, :]
            saved = sv_ref[cidx]
            st = jnp.where(block_mask, jnp.concatenate([saved] * GLA_HEADS, axis=0), 0.0)
            dst = dst_ref[...]
            hs = lambda a, h: a[GLA_CHUNK * h : GLA_CHUNK * (h + 1)]
            vs = lambda a, h: a[:, GLA_DV * h : GLA_DV * (h + 1)]
            dattn = jnp.concatenate([_dot_nt(vs(do_c, h), vs(v, h)) for h in range(GLA_HEADS)], axis=0)
            dattn = jnp.where(tri, dattn, 0.0)
            dv = jnp.concatenate([_dot_tn(hs(attn, h), vs(do_c, h)) for h in range(GLA_HEADS)], axis=1) + _dot_nt(kh, dst)
            dqt = _dot(do_c, st)
            dkt = jnp.zeros_like(dqt)
            for h in range(GLA_HEADS):
                dqt = dqt + jnp.where(head_masks[h], _dot(hs(dattn, h), kt), 0.0)
                dkt = dkt + jnp.where(head_masks[h], _dot_tn(hs(dattn, h), qt), 0.0)
            dkh = _dot(v, dst)
            dg = jnp.sum(dkh * kh, axis=0, keepdims=True) + jnp.sum(dst * st, axis=0, keepdims=True) * eg
            db = dqt * qt - dkt * kt - dkh * kh + jnp.where(total_row, dg, 0.0)
            dq = dqt * eb * (GLA_DK**-0.5)
            dk = dkt * enb + dkh * egb
            dqkv_ref[rows, :] = jnp.concatenate([dq, dk, dv], axis=1)
            db_ref[rows, :] = db
            dst_ref[...] = dst * eg + jnp.where(block_mask, _dot_tn(do_c, qt), 0.0)
            return carry

        lax.fori_loop(0, cpb, chunk, 0)

    wide = 2 * GLA_K_TOTAL + GLA_V_TOTAL
    return pl.pallas_call(
        body,
        name=name,
        grid=(nb,),
        in_specs=[q_spec, k_spec, v_spec, b_spec, o_spec, st_spec],
        out_specs=[pl.BlockSpec((tb, wide), lambda i: (rb(i), 0)), b_spec],
        out_shape=[jax.ShapeDtypeStruct((s, wide), F32), jax.ShapeDtypeStruct((s, GLA_K_TOTAL), F32)],
        scratch_shapes=[pltpu.VMEM((GLA_V_TOTAL, GLA_K_TOTAL), F32)],
        compiler_params=_cp(("arbitrary",)),
    )(z, z, z, b, do, states)


HALO = 8


def _halo_specs(s, ts, width, col):
    last = s // HALO - 1
    per = ts // HALO
    prev = pl.BlockSpec((HALO, width), lambda i: (jnp.maximum(i * per - 1, 0), col))
    nxt = pl.BlockSpec((HALO, width), lambda i: (jnp.minimum((i + 1) * per, last), col))
    return prev, nxt


def _group_ones():
    r = lax.broadcasted_iota(jnp.int32, (CONV_WIDTH, CONV_WIDTH), 0) >> 6
    c = lax.broadcasted_iota(jnp.int32, (CONV_WIDTH, CONV_WIDTH), 1) >> 6
    return (r == c).astype(BF16)


def _conv_terms(cc_ext, cu_ext, cw, valid):
    n = cc_ext.shape[0]
    hc = jnp.where(valid, cc_ext * cu_ext, 0.0)
    hc_prev = pltpu.roll(hc, 1, 0)
    hc_next = pltpu.roll(hc, n - 1, 0)
    conv = cw[0:1] * hc_prev + cw[1:2] * hc + cw[2:3] * hc_next
    return hc, hc_prev, hc_next, conv


def _ext(prev_ref, cur_ref, next_ref):
    return jnp.concatenate([prev_ref[...], cur_ref[...], next_ref[...]], axis=0)


def _valid_rows(ts, s):
    row = lax.broadcasted_iota(jnp.int32, (ts + 2 * HALO, 1), 0) + (pl.program_id(0) * ts - HALO)
    return (row >= 0) & (row < s)


def _head_norm(o, gn):
    out = []
    for h in range(GLA_HEADS):
        oh = o[:, GLA_DV * h : GLA_DV * (h + 1)]
        r = lax.rsqrt(jnp.mean(oh * oh, axis=-1, keepdims=True) + EPS)
        out.append((oh * r, r))
    return out


def _mix_fwd(z, o_f, o_b, conv_w, conv_norm, gla_norm, *, name):
    s = z.shape[0]
    ts = _rows(s)
    cprev, cnext = _halo_specs(s, ts, CONV_WIDTH, 1)
    uprev, unext = _halo_specs(s, ts, CONV_WIDTH, 2)

    def body(cb_ref, cc_ref, cu_ref, ccp_ref, ccn_ref, cup_ref, cun_ref, g_ref, of_ref, ob_ref, cw_ref, cn_ref, gn_ref, y_ref):
        valid = _valid_rows(ts, s)
        _, _, _, conv = _conv_terms(_ext(ccp_ref, cc_ref, ccn_ref), _ext(cup_ref, cu_ref, cun_ref), cw_ref[...], valid)
        yc = cb_ref[...] * conv[HALO : HALO + ts]
        ms = _dot_split(yc * yc, _group_ones()) * (1.0 / CONV_GROUP)
        y_conv = yc * lax.rsqrt(ms + EPS) * cn_ref[...]
        gate = g_ref[...]
        silu = gate * _sigmoid(gate)
        gn = gn_ref[...]
        y_gla = jnp.concatenate([oh * gn for oh, _ in _head_norm(of_ref[...] + ob_ref[...], gn)], axis=1) * silu
        y_ref[...] = jnp.concatenate([y_conv, y_gla], axis=1).astype(y_ref.dtype)

    col = lambda c, w=CONV_WIDTH: pl.BlockSpec((ts, w), lambda i: (i, c))
    return pl.pallas_call(
        body,
        name=name,
        grid=(s // ts,),
        in_specs=[col(0), col(1), col(2), cprev, cnext, uprev, unext, col(5), col(0), col(0),
                  pl.BlockSpec((CONV_K, CONV_WIDTH), lambda i: (0, 0)), pl.BlockSpec((1, CONV_WIDTH), lambda i: (0, 0)),
                  pl.BlockSpec((1, GLA_DV), lambda i: (0, 0))],
        out_specs=pl.BlockSpec((ts, D_MODEL), lambda i: (i, 0)),
        out_shape=jax.ShapeDtypeStruct((s, D_MODEL), _CD),
        compiler_params=_cp(("parallel",)),
    )(z, z, z, z, z, z, z, z, o_f, o_b, conv_w, conv_norm, gla_norm)


def _mix_bwd(z, o_f, o_b, dy, conv_w, conv_norm, gla_norm, *, name):
    s = z.shape[0]
    ts = _rows(s)
    halos = [_halo_specs(s, ts, CONV_WIDTH, c) for c in (0, 1, 2)]
    dprev, dnext = _halo_specs(s, ts, CONV_WIDTH, 0)

    def body(cb_ref, cc_ref, cu_ref, cbp_ref, cbn_ref, ccp_ref, ccn_ref, cup_ref, cun_ref, g_ref, of_ref, ob_ref,
             dyc_ref, dyg_ref, dyp_ref, dyn_ref, cw_ref, cn_ref, gn_ref, dconv_ref, dgate_ref, do_ref, dcw_ref, dcn_ref, dgn_ref):
        n = ts + 2 * HALO
        valid = _valid_rows(ts, s)
        cw = cw_ref[...]
        cn = cn_ref[...]
        ones = _group_ones()
        cb = _ext(cbp_ref, cb_ref, cbn_ref)
        cc = _ext(ccp_ref, cc_ref, ccn_ref)
        cu = _ext(cup_ref, cu_ref, cun_ref)
        dy = _ext(dyp_ref, dyc_ref, dyn_ref)
        hc, hc_prev, hc_next, conv = _conv_terms(cc, cu, cw, valid)
        yc = cb * conv
        r = lax.rsqrt(_dot_split(yc * yc, ones) * (1.0 / CONV_GROUP) + EPS)
        yh = yc * r
        dyh = dy * cn
        dyc = r * (dyh - yh * (_dot_split(dyh * yh, ones) * (1.0 / CONV_GROUP)))
        dconv = jnp.where(valid, dyc * cb, 0.0)
        dhc = cw[0:1] * pltpu.roll(dconv, n - 1, 0) + cw[1:2] * dconv + cw[2:3] * pltpu.roll(dconv, 1, 0)
        mid = lambda a: a[HALO : HALO + ts]
        dconv_ref[...] = jnp.concatenate([mid(dyc * conv), mid(dhc * cu), mid(dhc * cc)], axis=1).astype(dconv_ref.dtype)
        dconv_m = mid(dconv)
        colsum = lambda a: jnp.sum(a, axis=0, keepdims=True)
        dcw = jnp.concatenate([colsum(dconv_m * mid(hc_prev)), colsum(dconv_m * mid(hc)), colsum(dconv_m * mid(hc_next))], axis=0)
        dcn = colsum(mid(dy * yh))

        gate = g_ref[...]
        sg = _sigmoid(gate)
        silu = gate * sg
        gn = gn_ref[...]
        dyg = dyg_ref[...]
        don = dyg * silu
        heads = _head_norm(of_ref[...] + ob_ref[...], gn)
        on = jnp.concatenate([oh * gn for oh, _ in heads], axis=1)
        dgate_ref[...] = (dyg * on * (sg * (1.0 + gate * (1.0 - sg)))).astype(dgate_ref.dtype)
        dgn = jnp.zeros((1, GLA_DV), F32)
        dos = []
        for h, (oh, rh) in enumerate(heads):
            donh = don[:, GLA_DV * h : GLA_DV * (h + 1)]
            dgn = dgn + colsum(donh * oh)
            doh = donh * gn
            dos.append(rh * (doh - oh * jnp.mean(doh * oh, axis=-1, keepdims=True)))
        do_ref[...] = jnp.concatenate(dos, axis=1)

        first = pl.program_id(0) == 0

        @pl.when(first)
        def _():
            dcw_ref[...] = dcw
            dcn_ref[...] = dcn
            dgn_ref[...] = dgn

        @pl.when(jnp.logical_not(first))
        def _():
            dcw_ref[...] += dcw
            dcn_ref[...] += dcn
            dgn_ref[...] += dgn

    col = lambda c, w=CONV_WIDTH: pl.BlockSpec((ts, w), lambda i: (i, c))
    cw_spec = pl.BlockSpec((CONV_K, CONV_WIDTH), lambda i: (0, 0))
    cn_spec = pl.BlockSpec((1, CONV_WIDTH), lambda i: (0, 0))
    gn_spec = pl.BlockSpec((1, GLA_DV), lambda i: (0, 0))
    return pl.pallas_call(
        body,
        name=name,
        grid=(s // ts,),
        in_specs=[col(0), col(1), col(2), halos[0][0], halos[0][1], halos[1][0], halos[1][1], halos[2][0], halos[2][1],
                  col(5), col(0), col(0), col(0), col(1), dprev, dnext, cw_spec, cn_spec, gn_spec],
        out_specs=[pl.BlockSpec((ts, 3 * CONV_WIDTH), lambda i: (i, 0)), col(0), col(0), cw_spec, cn_spec, gn_spec],
        out_shape=[
            jax.ShapeDtypeStruct((s, 3 * CONV_WIDTH), _CD),
            jax.ShapeDtypeStruct((s, GLA_V_TOTAL), _CD),
            jax.ShapeDtypeStruct((s, GLA_V_TOTAL), F32),
            jax.ShapeDtypeStruct((CONV_K, CONV_WIDTH), F32),
            jax.ShapeDtypeStruct((1, CONV_WIDTH), F32),
            jax.ShapeDtypeStruct((1, GLA_DV), F32),
        ],
        compiler_params=_cp(("arbitrary",)),
    )(z, z, z, z, z, z, z, z, z, z, o_f, o_b, dy, dy, dy, dy, conv_w, conv_norm, gla_norm)


def _xa_probs(q_ref, kv_ref, h):
    qh = q_ref[:, XA_HEAD_DIM * h : XA_HEAD_DIM * (h + 1)]
    kh = kv_ref[:, XA_HEAD_DIM * h : XA_HEAD_DIM * (h + 1)]
    vh = kv_ref[:, D_MODEL + XA_HEAD_DIM * h : D_MODEL + XA_HEAD_DIM * (h + 1)]
    sc = _dot_nt(qh, kh) * (XA_HEAD_DIM**-0.5)
    e = jnp.exp(sc - jnp.max(sc, axis=-1, keepdims=True))
    return qh, kh, vh, e / jnp.sum(e, axis=-1, keepdims=True)


def _xattn_fwd(qx, kv, *, name):
    s = qx.shape[0]
    ts = _rows(s)

    def body(q_ref, kv_ref, o_ref):
        outs = []
        for h in range(XA_HEADS):
            _, _, vh, p = _xa_probs(q_ref, kv_ref, h)
            outs.append(_dot(p, vh))
        o_ref[...] = jnp.concatenate(outs, axis=1).astype(o_ref.dtype)

    return pl.pallas_call(
        body,
        name=name,
        grid=(s // ts,),
        in_specs=[pl.BlockSpec((ts, D_MODEL), lambda i: (i, 0)), pl.BlockSpec((N_MEM, 2 * D_MODEL), lambda i: (0, 0))],
        out_specs=pl.BlockSpec((ts, D_MODEL), lambda i: (i, 0)),
        out_shape=jax.ShapeDtypeStruct((s, D_MODEL), _CD),
        compiler_params=_cp(("parallel",)),
    )(qx, kv)


def _xattn_bwd(qx, kv, dox, *, name):
    s = qx.shape[0]
    ts = _rows(s)

    def body(q_ref, kv_ref, do_ref, dq_ref, dkv_ref):
        dqs, dks, dvs = [], [], []
        for h in range(XA_HEADS):
            qh, kh, vh, p = _xa_probs(q_ref, kv_ref, h)
            doh = do_ref[:, XA_HEAD_DIM * h : XA_HEAD_DIM * (h + 1)]
            dp = _dot_nt(doh, vh)
            ds = p * (dp - jnp.sum(dp * p, axis=-1, keepdims=True)) * (XA_HEAD_DIM**-0.5)
            dqs.append(_dot(ds, kh))
            dks.append(_dot_tn(ds, qh))
            dvs.append(_dot_tn(p, doh))
        dq_ref[...] = jnp.concatenate(dqs, axis=1).astype(dq_ref.dtype)
        dkv = jnp.concatenate(dks + dvs, axis=1)

        @pl.when(pl.program_id(0) == 0)
        def _():
            dkv_ref[...] = dkv

        @pl.when(pl.program_id(0) > 0)
        def _():
            dkv_ref[...] += dkv

    tile = pl.BlockSpec((ts, D_MODEL), lambda i: (i, 0))
    kv_spec = pl.BlockSpec((N_MEM, 2 * D_MODEL), lambda i: (0, 0))
    return pl.pallas_call(
        body,
        name=name,
        grid=(s // ts,),
        in_specs=[tile, kv_spec, tile],
        out_specs=[tile, kv_spec],
        out_shape=[jax.ShapeDtypeStruct((s, D_MODEL), _CD), jax.ShapeDtypeStruct((N_MEM, 2 * D_MODEL), F32)],
        compiler_params=_cp(("arbitrary",)),
    )(qx, kv, dox)


def _adamw_math(w, g, m, v):
    m = ADAM_B1 * m + (1.0 - ADAM_B1) * g
    v = ADAM_B2 * v + (1.0 - ADAM_B2) * (g * g)
    m_hat = m / (1.0 - ADAM_B1**ADAM_STEP)
    v_hat = v / (1.0 - ADAM_B2**ADAM_STEP)
    delta = -ADAM_LR * (m_hat / (jnp.sqrt(v_hat) + ADAM_EPS) + ADAM_WD * w)
    return delta, m, v


def _adamw(w, g, m, v, *, name):
    r, c = w.shape
    tr = _pick(r, (256, 128, 64, 32, 16, 8))

    def body(w_ref, g_ref, m_ref, v_ref, d_ref, nm_ref, nv_ref):
        d_ref[...], nm_ref[...], nv_ref[...] = _adamw_math(w_ref[...], g_ref[...], m_ref[...], v_ref[...])

    tile = pl.BlockSpec((tr, c), lambda i: (i, 0))
    return pl.pallas_call(
        body,
        name=name,
        grid=(r // tr,),
        in_specs=[tile] * 4,
        out_specs=[tile] * 3,
        out_shape=[jax.ShapeDtypeStruct((r, c), F32)] * 3,
        compiler_params=_cp(("parallel",)),
    )(w, g, m, v)


def _adamw_small(groups, *, name):
    n = len(groups)

    def body(*refs):
        ins, outs = refs[: 4 * n], refs[4 * n :]
        for i in range(n):
            w_ref, g_ref, m_ref, v_ref = ins[4 * i : 4 * i + 4]
            outs[3 * i][...], outs[3 * i + 1][...], outs[3 * i + 2][...] = _adamw_math(w_ref[...], g_ref[...], m_ref[...], v_ref[...])

    flat = [a for grp in groups for a in grp]
    vm = pl.BlockSpec(memory_space=pltpu.VMEM)
    res = pl.pallas_call(
        body,
        name=name,
        in_specs=[vm] * (4 * n),
        out_specs=[vm] * (3 * n),
        out_shape=[jax.ShapeDtypeStruct(grp[0].shape, F32) for grp in groups for _ in range(3)],
        compiler_params=_cp(),
    )(*flat)
    return [tuple(res[3 * i : 3 * i + 3]) for i in range(n)]


def _place():
    return lax.axis_index("x"), lax.axis_index("y"), lax.axis_index("c")


def _rel_chip(x, y, k):
    return (1 - x if k & 2 else x), (1 - y if k & 1 else y)


def _half(c, rh):
    return pl.ds(pl.multiple_of(c * rh, 16), rh)


def _gather_weights(pack):
    r, w = pack.shape
    rh = r // 2

    def body(p_ref, q_ref, send_sems, recv_sems):
        x, y, c = _place()
        j = 2 * x + y
        rows = _half(c, rh)

        def to_chip(k):
            cx, cy = _rel_chip(x, y, k)
            return pltpu.make_async_remote_copy(
                src_ref=p_ref.at[rows], dst_ref=q_ref.at[j, rows], send_sem=send_sems.at[k - 1], recv_sem=recv_sems.at[k - 1],
                device_id=(cx, cy, c), device_id_type=MESH)

        def to_sibling(k):
            cx, cy = _rel_chip(x, y, k)
            slot = q_ref.at[2 * cx + cy, rows]
            return pltpu.make_async_remote_copy(
                src_ref=slot, dst_ref=slot, send_sem=send_sems.at[2 + k], recv_sem=recv_sems.at[2 + k],
                device_id=(x, y, 1 - c), device_id_type=MESH)

        first = [to_chip(k) for k in range(1, N_CHIPS)]
        passed = [to_sibling(k) for k in range(1, N_CHIPS)]
        for cp in first:
            cp.start()
        for cp, fw in zip(first, passed):
            cp.wait_recv()
            fw.start()
        for fw in passed:
            fw.wait_recv()
        for cp in first + passed:
            cp.wait_send()

    return pl.pallas_call(
        body,
        name="gather_weights",
        in_specs=[ANY],
        out_specs=ANY,
        out_shape=jax.ShapeDtypeStruct((N_CHIPS, r, w), pack.dtype),
        scratch_shapes=[pltpu.SemaphoreType.DMA((6,)), pltpu.SemaphoreType.DMA((6,))],
        compiler_params=pltpu.CompilerParams(has_side_effects=True),
    )(pack)


def _swap_halves(g):
    n, r, w = g.shape
    rh = r // 2

    def body(g_ref, o_ref, send_sem, recv_sem):
        x, y, c = _place()
        cp = pltpu.make_async_remote_copy(
            src_ref=g_ref.at[:, _half(1 - c, rh)], dst_ref=o_ref, send_sem=send_sem, recv_sem=recv_sem,
            device_id=(x, y, 1 - c), device_id_type=MESH)
        cp.start()
        cp.wait()

    return pl.pallas_call(
        body,
        name="grads_to_sibling",
        in_specs=[ANY],
        out_specs=ANY,
        out_shape=jax.ShapeDtypeStruct((n, rh, w), g.dtype),
        scratch_shapes=[pltpu.SemaphoreType.DMA, pltpu.SemaphoreType.DMA],
        compiler_params=pltpu.CompilerParams(has_side_effects=True),
    )(g)


def _chip_sums(g, got, where):
    n, r, w = g.shape
    rh = r // 2
    nt = rh // PACK_TILE

    def body(where_ref, g_ref, got_ref, o_ref):
        o_ref[...] = (g_ref[...] + got_ref[...]).astype(o_ref.dtype)

    return pl.pallas_call(
        body,
        name="chip_sums",
        grid_spec=pltpu.PrefetchScalarGridSpec(
            num_scalar_prefetch=1,
            grid=(n, nt),
            in_specs=[pl.BlockSpec((1, PACK_TILE, w), lambda a, i, wh: (a, wh[0] * nt + i, 0)),
                      pl.BlockSpec((1, PACK_TILE, w), lambda a, i, wh: (a, i, 0))],
            out_specs=pl.BlockSpec((1, PACK_TILE, w), lambda a, i, wh: (a, i, 0)),
        ),
        out_shape=jax.ShapeDtypeStruct((n, rh, w), _TD),
        compiler_params=_cp(("parallel", "parallel")),
    )(where, g, got)


def _exchange_chip_sums(h):
    n, rh, w = h.shape

    def body(h_ref, o_ref, send_sems, recv_sems):
        x, y, c = _place()
        j = 2 * x + y
        copies = []
        for k in range(1, N_CHIPS):
            cx, cy = _rel_chip(x, y, k)
            copies.append(pltpu.make_async_remote_copy(
                src_ref=h_ref.at[2 * cx + cy], dst_ref=o_ref.at[k - 1], send_sem=send_sems.at[k - 1], recv_sem=recv_sems.at[k - 1],
                device_id=(cx, cy, c), device_id_type=MESH))
        for cp in copies:
            cp.start()
        for cp in copies:
            cp.wait()

    return pl.pallas_call(
        body,
        name="chip_sums_exchange",
        in_specs=[ANY],
        out_specs=ANY,
        out_shape=jax.ShapeDtypeStruct((N_CHIPS - 1, rh, w), h.dtype),
        scratch_shapes=[pltpu.SemaphoreType.DMA((3,)), pltpu.SemaphoreType.DMA((3,))],
        compiler_params=pltpu.CompilerParams(has_side_effects=True),
    )(h)


def _shard_sum(g, got, others, where):
    n, r, w = g.shape
    rh = r // 2
    nt = rh // PACK_TILE

    def body(where_ref, g_ref, got_ref, oth_ref, o_ref):
        acc = g_ref[0] + got_ref[0]
        for k in range(N_CHIPS - 1):
            acc = acc + oth_ref[k].astype(F32)
        o_ref[...] = acc

    return pl.pallas_call(
        body,
        name="shard_sum",
        grid_spec=pltpu.PrefetchScalarGridSpec(
            num_scalar_prefetch=1,
            grid=(nt,),
            in_specs=[pl.BlockSpec((1, PACK_TILE, w), lambda i, wh: (wh[1], wh[0] * nt + i, 0)),
                      pl.BlockSpec((1, PACK_TILE, w), lambda i, wh: (wh[1], i, 0)),
                      pl.BlockSpec((N_CHIPS - 1, PACK_TILE, w), lambda i, wh: (0, i, 0))],
            out_specs=pl.BlockSpec((PACK_TILE, w), lambda i, wh: (i, 0)),
        ),
        out_shape=jax.ShapeDtypeStruct((rh, w), F32),
        compiler_params=_cp(("parallel",)),
    )(where, g, got, others)


def _join_halves(e):
    rh, w = e.shape

    def body(e_ref, o_ref, send_sem, recv_sem, local_sem):
        x, y, c = _place()
        rows = _half(c, rh)
        mine = pltpu.make_async_copy(e_ref, o_ref.at[rows], local_sem)
        mine.start()
        cp = pltpu.make_async_remote_copy(
            src_ref=e_ref, dst_ref=o_ref.at[rows], send_sem=send_sem, recv_sem=recv_sem, device_id=(x, y, 1 - c), device_id_type=MESH)
        cp.start()
        cp.wait()
        mine.wait()

    return pl.pallas_call(
        body,
        name="shard_to_sibling",
        in_specs=[ANY],
        out_specs=ANY,
        out_shape=jax.ShapeDtypeStruct((2 * rh, w), e.dtype),
        scratch_shapes=[pltpu.SemaphoreType.DMA, pltpu.SemaphoreType.DMA, pltpu.SemaphoreType.DMA],
        compiler_params=pltpu.CompilerParams(has_side_effects=True),
    )(e)


HBM = pl.BlockSpec(memory_space=pltpu.HBM)
SEM = pl.BlockSpec(memory_space=pltpu.SEMAPHORE)
EFFECT = pltpu.SideEffectType.DATAFLOW_SIDE_EFFECTING


def _in_hbm(a):
    return pltpu.with_memory_space_constraint(a, pltpu.HBM)


def _gather_copies(p_ref, land_ref, send_sems, recv_sems):
    rh = p_ref.shape[0] // 2
    x, y, c = _place()
    rows = _half(c, rh)
    copies = []
    for k in range(1, N_CHIPS):
        cx, cy = _rel_chip(x, y, k)
        copies.append(pltpu.make_async_remote_copy(
            src_ref=p_ref.at[rows], dst_ref=land_ref.at[2 * x + y, rows], send_sem=send_sems.at[k - 1], recv_sem=recv_sems.at[k - 1],
            device_id=(cx, cy, c), device_id_type=MESH))
    return copies


def _gather_start(pack, *, name):
    r, w = pack.shape

    def body(p_ref, land_ref, send_sems, recv_sems, p_thru, land_thru, token):
        for cp in _gather_copies(p_ref, land_ref, send_sems, recv_sems):
            cp.start()
        token[...] = jnp.zeros_like(token)

    return pl.pallas_call(
        body,
        name=name,
        out_shape=(pltpu.SemaphoreType.DMA((N_CHIPS - 1,)), pltpu.SemaphoreType.DMA((N_CHIPS - 1,)), pltpu.HBM((r, w), pack.dtype),
                   pltpu.HBM((N_CHIPS, r, w), pack.dtype), jax.ShapeDtypeStruct((8, 128), F32)),
        in_specs=(HBM, HBM),
        out_specs=(SEM, SEM, HBM, HBM, pl.BlockSpec(memory_space=pltpu.VMEM)),
        input_output_aliases={0: 2, 1: 3},
        compiler_params=pltpu.CompilerParams(has_side_effects=EFFECT),
    )(_in_hbm(pack), _in_hbm(lax.empty((N_CHIPS, r, w), pack.dtype)))


def _gather_wait(send_sems, recv_sems, pack, land, after, *, name):
    def body(p_ref, land_ref, send_sems, recv_sems, after_ref, p_out, land_out):
        for cp in _gather_copies(p_ref, land_ref, send_sems, recv_sems):
            cp.wait_send()
            cp.wait_recv()

    return pl.pallas_call(
        body,
        name=name,
        out_shape=(pltpu.HBM(pack.shape, pack.dtype), pltpu.HBM(land.shape, land.dtype)),
        in_specs=(HBM, HBM, SEM, SEM, ANY),
        out_specs=(HBM, HBM),
        input_output_aliases={0: 0, 1: 1},
        compiler_params=pltpu.CompilerParams(has_side_effects=EFFECT),
    )(pack, land, send_sems, recv_sems, after)


def _gather_spread(land, *, name):
    n, r, w = land.shape
    rh = r // 2

    def body(land_ref, o_ref, send_sems, recv_sems):
        x, y, c = _place()
        rows = _half(c, rh)
        copies = []
        for k in range(1, N_CHIPS):
            cx, cy = _rel_chip(x, y, k)
            copies.append(pltpu.make_async_remote_copy(
                src_ref=land_ref.at[2 * cx + cy, rows], dst_ref=o_ref.at[2 * cx + cy, rows], send_sem=send_sems.at[k - 1],
                recv_sem=recv_sems.at[k - 1], device_id=(x, y, 1 - c), device_id_type=MESH))
        for cp in copies:
            cp.start()
        for cp in copies:
            cp.wait()

    return pl.pallas_call(
        body,
        name=name,
        in_specs=[ANY],
        out_specs=ANY,
        out_shape=jax.ShapeDtypeStruct(land.shape, land.dtype),
        input_output_aliases={0: 0},
        scratch_shapes=[pltpu.SemaphoreType.DMA((N_CHIPS - 1,)), pltpu.SemaphoreType.DMA((N_CHIPS - 1,))],
        compiler_params=pltpu.CompilerParams(has_side_effects=True),
    )(land)


N_PARTS = 2 * (N_CHIPS - 1)


def _scatter_copies(lo_ref, g_ref, land_lo_ref, land_f_ref, send_sems, recv_sems, starting):
    rh = g_ref.shape[1] // 2
    x, y, c = _place()
    copies = []
    for k in range(1, N_CHIPS):
        cx, cy = _rel_chip(x, y, k)
        for i in range(2):
            part = 2 * (k - 1) + (c if starting else i)
            copies.append(pltpu.make_async_remote_copy(
                src_ref=lo_ref.at[2 * cx + cy, pl.ds(i * rh, rh)], dst_ref=land_lo_ref.at[part],
                send_sem=send_sems.at[2 * (k - 1) + i], recv_sem=recv_sems.at[part], device_id=(cx, cy, i), device_id_type=MESH))
    copies.append(pltpu.make_async_remote_copy(
        src_ref=g_ref.at[2 * x + y, _half(1 - c, rh)], dst_ref=land_f_ref, send_sem=send_sems.at[N_PARTS], recv_sem=recv_sems.at[N_PARTS],
        device_id=(x, y, 1 - c), device_id_type=MESH))
    return copies


def _scatter_start(g_lo, g, *, name):
    n, r, w = g.shape
    rh = r // 2

    def body(lo_ref, g_ref, land_lo_ref, land_f_ref, send_sems, recv_sems, lo_thru, g_thru, land_lo_thru, land_f_thru, token):
        for cp in _scatter_copies(lo_ref, g_ref, land_lo_ref, land_f_ref, send_sems, recv_sems, True):
            cp.start()
        token[...] = jnp.zeros_like(token)

    return pl.pallas_call(
        body,
        name=name,
        out_shape=(pltpu.SemaphoreType.DMA((N_PARTS + 1,)), pltpu.SemaphoreType.DMA((N_PARTS + 1,)), pltpu.HBM(g_lo.shape, g_lo.dtype),
                   pltpu.HBM(g.shape, g.dtype), pltpu.HBM((N_PARTS, rh, w), g_lo.dtype), pltpu.HBM((rh, w), g.dtype),
                   jax.ShapeDtypeStruct((8, 128), F32)),
        in_specs=(HBM, HBM, HBM, HBM),
        out_specs=(SEM, SEM, HBM, HBM, HBM, HBM, pl.BlockSpec(memory_space=pltpu.VMEM)),
        input_output_aliases={0: 2, 1: 3, 2: 4, 3: 5},
        compiler_params=pltpu.CompilerParams(has_side_effects=EFFECT),
    )(_in_hbm(g_lo), _in_hbm(g), _in_hbm(lax.empty((N_PARTS, rh, w), g_lo.dtype)), _in_hbm(lax.empty((rh, w), g.dtype)))


def _scatter_wait(send_sems, recv_sems, g_lo, g, land_lo, land_f, after, *, name):
    def body(lo_ref, g_ref, land_lo_ref, land_f_ref, send_sems, recv_sems, after_ref, o0, o1, o2, o3):
        for cp in _scatter_copies(lo_ref, g_ref, land_lo_ref, land_f_ref, send_sems, recv_sems, False):
            cp.wait_send()
            cp.wait_recv()

    arrays = (g_lo, g, land_lo, land_f)
    return pl.pallas_call(
        body,
        name=name,
        out_shape=tuple(pltpu.HBM(a.shape, a.dtype) for a in arrays),
        in_specs=(HBM, HBM, HBM, HBM, SEM, SEM, ANY),
        out_specs=(HBM, HBM, HBM, HBM),
        input_output_aliases={0: 0, 1: 1, 2: 2, 3: 3},
        compiler_params=pltpu.CompilerParams(has_side_effects=EFFECT),
    )(*arrays, send_sems, recv_sems, after)


def _scatter_sum(g, land_lo, land_f, where, *, name):
    n, r, w = g.shape
    rh = r // 2
    tr = _pick(rh, (256, 160, 80))
    nt = rh // tr

    def body(where_ref, g_ref, f_ref, lo_ref, o_ref):
        acc = g_ref[0] + f_ref[...]
        for part in range(N_PARTS):
            acc = acc + lo_ref[part].astype(F32)
        o_ref[...] = acc

    return pl.pallas_call(
        body,
        name=name,
        grid_spec=pltpu.PrefetchScalarGridSpec(
            num_scalar_prefetch=1,
            grid=(nt,),
            in_specs=[pl.BlockSpec((1, tr, w), lambda i, wh: (wh[1], wh[0] * nt + i, 0)),
                      pl.BlockSpec((tr, w), lambda i, wh: (i, 0)),
                      pl.BlockSpec((N_PARTS, tr, w), lambda i, wh: (0, i, 0))],
            out_specs=pl.BlockSpec((tr, w), lambda i, wh: (i, 0)),
        ),
        out_shape=jax.ShapeDtypeStruct((rh, w), F32),
        compiler_params=_cp(("parallel",)),
    )(where, g, land_f, land_lo)


def _swap_all(halves, *, name):
    n = len(halves)

    def body(*refs):
        ins, outs = refs[:n], refs[n : 2 * n]
        send_sems, recv_sems = refs[2 * n :]
        x, y, c = _place()
        copies = [pltpu.make_async_remote_copy(src_ref=e_ref, dst_ref=o_ref, send_sem=send_sems.at[i], recv_sem=recv_sems.at[i],
                                               device_id=(x, y, 1 - c), device_id_type=MESH)
                  for i, (e_ref, o_ref) in enumerate(zip(ins, outs))]
        for cp in copies:
            cp.start()
        for cp in copies:
            cp.wait()

    return pl.pallas_call(
        body,
        name=name,
        in_specs=[ANY] * n,
        out_specs=[ANY] * n,
        out_shape=[jax.ShapeDtypeStruct(e.shape, e.dtype) for e in halves],
        scratch_shapes=[pltpu.SemaphoreType.DMA((n,)), pltpu.SemaphoreType.DMA((n,))],
        compiler_params=pltpu.CompilerParams(has_side_effects=True),
    )(*halves)


def _sum_small(small):
    n_dev = 8

    def body(s_ref, o_ref, all_ref, send_sems, recv_sems):
        x, y, c = _place()
        me = 4 * x + 2 * y + c
        all_ref[me] = s_ref[...]
        copies = []
        for k in range(1, n_dev):
            cx, cy = _rel_chip(x, y, k >> 1)
            cc = 1 - c if k & 1 else c
            copies.append(pltpu.make_async_remote_copy(
                src_ref=s_ref, dst_ref=all_ref.at[me], send_sem=send_sems.at[k - 1], recv_sem=recv_sems.at[k - 1],
                device_id=(cx, cy, cc), device_id_type=MESH))
        for cp in copies:
            cp.start()
        for cp in copies:
            cp.wait()
        acc = all_ref[0]
        for a in range(1, n_dev):
            acc = acc + all_ref[a]
        o_ref[...] = acc

    vm = pl.BlockSpec(memory_space=pltpu.VMEM)
    return pl.pallas_call(
        body,
        name="sum_small",
        in_specs=[vm],
        out_specs=vm,
        out_shape=jax.ShapeDtypeStruct(small.shape, F32),
        scratch_shapes=[pltpu.VMEM((n_dev,) + small.shape, F32), pltpu.SemaphoreType.DMA((n_dev - 1,)), pltpu.SemaphoreType.DMA((n_dev - 1,))],
        compiler_params=pltpu.CompilerParams(has_side_effects=True),
    )(small)


MATS = {"w_in": (776, True), "w_out": (256, False), "w_xq": (256, False), "w_xkv": (512, True), "w_xo": (256, False),
        "w_up": (1024, True), "w_down": (1024, False)}
GATHER_FIRST = ("w_in",)
GATHER_REST = ("w_out", "w_xq", "w_xkv", "w_xo", "w_up", "w_down")
GRAD_GROUPS = (("w_up", "w_down"), ("w_out", "w_xq", "w_xkv", "w_xo"), ("w_in",))


def _group_rows(names):
    n = sum(MATS[name][0] for name in names)
    return n + (-n) % 32


def _pack(pieces, rows):
    p = jnp.concatenate(pieces, axis=0) if len(pieces) > 1 else pieces[0]
    return jnp.pad(p, ((0, rows - p.shape[0]), (0, 0))) if rows > p.shape[0] else p


def _unpack(rows, names):
    out, off = {}, 0
    for name in names:
        out[name] = rows[off : off + MATS[name][0]]
        off += MATS[name][0]
    return out


SMALL = (
    ("mix_norm", 1024), ("conv_norm", 512), ("b_af", 256), ("b_ab", 256), ("gla_norm", 128), ("xa_norm", 1024), ("mem_norm", 1024),
    ("mlp_norm", 1024), ("final_norm", 1024), ("conv_w", 1536), ("w_af", 4096), ("w_ab", 4096), ("loss", 128),
)


def kernel(x, mem, mix_norm, w_in, conv_w, conv_norm, w_af, b_af, w_ab, b_ab, gla_norm, w_out, xa_norm, mem_norm, w_xq, w_xkv, w_xo, mlp_norm, w_up, w_down, final_norm, loss_target, m_mix_norm, m_w_in, m_conv_w, m_conv_norm, m_w_af, m_b_af, m_w_ab, m_b_ab, m_gla_norm, m_w_out, m_xa_norm, m_mem_norm, m_w_xq, m_w_xkv, m_w_xo, m_mlp_norm, m_w_up, m_w_down, m_final_norm, v_mix_norm, v_w_in, v_conv_w, v_conv_norm, v_w_af, v_b_af, v_w_ab, v_b_ab, v_gla_norm, v_w_out, v_xa_norm, v_mem_norm, v_w_xq, v_w_xkv, v_w_xo, v_mlp_norm, v_w_up, v_w_down, v_final_norm):
    given = dict(locals())
    xi, yi, ci = _place()
    chip = 2 * xi + yi
    where = jnp.stack([ci, chip]).astype(jnp.int32)

    lo = {name: (given[name][0].T if MATS[name][1] else given[name][0]).astype(_CD) for name in MATS}
    pack_rest = _pack([lo[name] for name in GATHER_REST], _group_rows(GATHER_REST))
    pack_first = _pack([lo[name] for name in GATHER_FIRST], _group_rows(GATHER_FIRST))
    got_first = _gather_weights(pack_first)
    pack_rest, got_first = lax.optimization_barrier((pack_rest, got_first))
    rest_send, rest_recv, pack_rest, land_rest, rest_token = _gather_start(pack_rest, name="gather_rest_start")

    def whole(own, got, off, rows):
        return [jnp.where(chip == a, own[off : off + rows], got[a, off : off + rows]) for a in range(N_CHIPS)]

    w_in_t = jnp.concatenate(whole(pack_first, got_first, 0, MATS["w_in"][0])
                             + [jnp.zeros((Z_COLS - W_IN_COLS, D_MODEL), _CD)], axis=0)

    def placed(shard, full_shape, col):
        return lax.dynamic_update_slice(jnp.zeros(full_shape, F32), shard, (0, col)).reshape(-1, 128)

    sw = jnp.concatenate([
        placed(conv_w[0], (CONV_K, CONV_WIDTH), 128 * chip),
        placed(w_af[0], (GLA_LOWRANK, GLA_K_TOTAL), 64 * chip),
        placed(w_ab[0], (GLA_LOWRANK, GLA_K_TOTAL), 64 * chip),
    ], axis=0)
    sw = jnp.pad(sw, ((0, SMALL_ROWS - sw.shape[0]), (0, 0))) * (ci == 0).astype(F32)
    sw = _sum_small(sw)
    conv_w_full = sw[0:12].reshape(CONV_K, CONV_WIDTH)
    w_af_full = sw[12:44].reshape(GLA_LOWRANK, GLA_K_TOTAL)
    w_ab_full = sw[44:76].reshape(GLA_LOWRANK, GLA_K_TOTAL)
    waf_p = jnp.pad(w_af_full, ((0, 128 - GLA_LOWRANK), (0, 0))).astype(_CD)
    wab_p = jnp.pad(w_ab_full, ((GLA_LOWRANK, 128 - 2 * GLA_LOWRANK), (0, 0))).astype(_CD)

    mems, tgt = mem[0], loss_target[0]
    xs, _ = lax.optimization_barrier((x[0], rest_token))
    add_res = lambda acc, res: (acc + res,)

    h1 = _rms_fwd(xs, mix_norm, name="norm_mix")
    z = _mm(h1, w_in_t, mode="nt", name="proj_in", tm=512, tn=Z_COLS)
    b_f, b_b = _gate_fwd(z, waf_p, wab_p, b_af, b_ab, name="gates")
    o_f, st_f = _gla_fwd(z, b_f, rev=False, name="gla_scan_fwd")
    o_b, st_b = _gla_fwd(z, b_b, rev=True, name="gla_scan_rev")
    y = _mix_fwd(z, o_f, o_b, conv_w_full, conv_norm, gla_norm, name="mix_out")
    pack_rest, land_rest = _gather_wait(rest_send, rest_recv, pack_rest, land_rest, y, name="gather_rest_wait")
    gathered = _gather_spread(land_rest, name="gather_rest_spread")
    wt, off = {}, 0
    for name in GATHER_REST:
        wt[name] = jnp.concatenate(whole(pack_rest, gathered, off, MATS[name][0]), axis=0)
        off += MATS[name][0]
    x1 = _mm(y, wt["w_out"], mode="nn", name="proj_out", extras=(xs,), epilogue=add_res)
    hx = _rms_fwd(x1, xa_norm, name="norm_xa")
    qx = _mm(hx, wt["w_xq"], mode="nn", name="proj_xq", out_dtypes=(_CD,))
    hmem = _rms_fwd(mems, mem_norm, name="norm_mem")
    kv = _mm(hmem, wt["w_xkv"], mode="nt", name="proj_xkv", out_dtypes=(_CD,))
    ox = _xattn_fwd(qx, kv, name="xattn")
    x2 = _mm(ox, wt["w_xo"], mode="nn", name="proj_xo", extras=(x1,), epilogue=add_res)
    hm = _rms_fwd(x2, mlp_norm, name="norm_mlp")
    act = _mm(hm, wt["w_up"], mode="nt", name="mlp_up", out_dtypes=(_CD,), epilogue=lambda acc: (jnp.square(jnp.maximum(acc, 0.0)),))
    x3 = _mm(act, wt["w_down"], mode="nn", name="mlp_down", extras=(x2,), epilogue=add_res, tm=512, tk=D_FF)
    dx3, dx3_lo, loss_part, g_final_norm = _final_loss(x3, final_norm.reshape(1, D_MODEL), tgt, name="loss_head")

    grads_t = {}

    def start_group(names, tag):
        rows = _group_rows(names)
        g = jnp.stack([_pack([grads_t[name][a * MATS[name][0] : (a + 1) * MATS[name][0]] for name in names], rows) for a in range(N_CHIPS)])
        return _scatter_start(g.astype(_TD), g, name="grads_" + tag + "_start")

    def finish_group(state, after, tag):
        send_sems, recv_sems, g_lo, g, land_lo, land_f, _ = state
        g_lo, g, land_lo, land_f = _scatter_wait(send_sems, recv_sems, g_lo, g, land_lo, land_f, after, name="grads_" + tag + "_wait")
        return _scatter_sum(g, land_lo, land_f, where, name="grads_" + tag + "_sum")

    du = _mm(dx3_lo, wt["w_down"], mode="nt", name="mlp_down_dx", out_dtypes=(_CD,), extras=(act,),
             epilogue=lambda acc, aa: (acc * (2.0 * jnp.sqrt(aa.astype(F32))),))
    grads_t["w_down"] = _mm_tn(act, dx3_lo, name="mlp_down_dw")
    grads_t["w_up"] = _mm_tn(du, hm, name="mlp_up_dw")
    mlp_state = start_group(GRAD_GROUPS[0], "mlp")
    du, _ = lax.optimization_barrier((du, mlp_state[-1]))
    dhm = _mm(du, wt["w_up"], mode="nn", name="mlp_up_dx", tm=512, tk=D_FF)
    dx2, dx2_lo, g_mlp_norm = _rms_bwd(x2, mlp_norm, dhm, dx3, name="norm_mlp_bwd")
    dox = _mm(dx2_lo, wt["w_xo"], mode="nt", name="proj_xo_dx", out_dtypes=(_CD,))
    grads_t["w_xo"] = _mm_tn(ox, dx2_lo, name="proj_xo_dw")
    dqx, dkv = _xattn_bwd(qx, kv, dox, name="xattn_bwd")
    grads_t["w_xq"] = _mm_tn(hx, dqx, name="proj_xq_dw")
    dhx = _mm(dqx, wt["w_xq"], mode="nt", name="proj_xq_dx")
    dx1, dx1_lo, g_xa_norm = _rms_bwd(x1, xa_norm, dhx, dx2, name="norm_xa_bwd")
    dkv_lo = dkv.astype(_CD)
    grads_t["w_xkv"] = _mm_tn(dkv_lo, hmem, name="proj_xkv_dw")
    dhmem = _mm(dkv_lo, wt["w_xkv"], mode="nn", name="proj_xkv_dx")
    (g_mem_norm,) = _rms_bwd(mems, mem_norm, dhmem, name="norm_mem_bwd", want_dx=False, want_lo=False)
    dy = _mm(dx1_lo, wt["w_out"], mode="nt", name="proj_out_dx")
    grads_t["w_out"] = _mm_tn(y, dx1_lo, name="proj_out_dw")
    attn_state = start_group(GRAD_GROUPS[1], "attn")
    dy, _ = lax.optimization_barrier((dy, attn_state[-1]))
    dz_conv, dz_gate, do, g_conv_w, g_conv_norm, g_gla_norm = _mix_bwd(z, o_f, o_b, dy, conv_w_full, conv_norm, gla_norm, name="mix_out_bwd")
    dqkv_f, db_f = _gla_bwd(z, b_f, do, st_f, rev=False, name="gla_scan_fwd_bwd")
    dqkv_b, db_b = _gla_bwd(z, b_b, do, st_b, rev=True, name="gla_scan_rev_bwd")
    dqkv, dlr, g_waf_p, g_wab_p, g_b_af, g_b_ab = _gate_bwd(z, waf_p, wab_p, b_af, b_ab, db_f, db_b, dqkv_f, dqkv_b, name="gates_bwd")
    dz = jnp.concatenate([dz_conv, dqkv, dz_gate, dlr], axis=1)
    grads_t["w_in"] = _mm_tn(dz, h1, name="proj_in_dw")
    in_state = start_group(GRAD_GROUPS[2], "in")
    dh1 = _mm(dz, w_in_t, mode="nn", name="proj_in_dx", tm=512, tk=Z_COLS)
    grad_x, g_mix_norm = _rms_bwd(xs, mix_norm, dh1, dx1, name="norm_mix_bwd", want_lo=False)

    half_mlp = finish_group(mlp_state, grad_x, "mlp")
    half_attn = finish_group(attn_state, half_mlp, "attn")
    half_in = finish_group(in_state, half_attn, "in")
    g_shard = {}
    mine = [half_mlp, half_attn, half_in]
    for names, own, got in zip(GRAD_GROUPS, mine, _swap_all(mine, name="shards_to_sibling")):
        rows = jnp.where(ci == 0, jnp.concatenate([own, got], axis=0), jnp.concatenate([got, own], axis=0))
        for name, g in _unpack(rows, names).items():
            g_shard[name] = g.T if MATS[name][1] else g

    small_vals = dict(mix_norm=g_mix_norm, conv_norm=g_conv_norm, b_af=g_b_af, b_ab=g_b_ab, gla_norm=g_gla_norm, xa_norm=g_xa_norm,
                      mem_norm=g_mem_norm, mlp_norm=g_mlp_norm, final_norm=g_final_norm, conv_w=g_conv_w,
                      w_af=g_waf_p[0:GLA_LOWRANK], w_ab=g_wab_p[GLA_LOWRANK : 2 * GLA_LOWRANK], loss=loss_part)
    small = jnp.concatenate([small_vals[name].reshape(-1, 128) for name, _ in SMALL], axis=0)
    small = _sum_small(jnp.pad(small, ((0, SMALL_ROWS - small.shape[0]), (0, 0))))
    g_small, off = {}, 0
    for name, n in SMALL:
        g_small[name] = small[off : off + n // 128]
        off += n // 128
    loss = g_small["loss"][0, 0]
    g_small["conv_w"] = lax.dynamic_slice(g_small["conv_w"].reshape(CONV_K, CONV_WIDTH), (0, 128 * chip), (CONV_K, 128))
    g_small["w_af"] = lax.dynamic_slice(g_small["w_af"].reshape(GLA_LOWRANK, GLA_K_TOTAL), (0, 64 * chip), (GLA_LOWRANK, 64))
    g_small["w_ab"] = lax.dynamic_slice(g_small["w_ab"].reshape(GLA_LOWRANK, GLA_K_TOTAL), (0, 64 * chip), (GLA_LOWRANK, 64))

    names = ["mix_norm", "w_in", "conv_w", "conv_norm", "w_af", "b_af", "w_ab", "b_ab", "gla_norm", "w_out", "xa_norm", "mem_norm",
             "w_xq", "w_xkv", "w_xo", "mlp_norm", "w_up", "w_down", "final_norm"]
    big_names = list(MATS)
    as2d = lambda a: a.reshape(1, -1) if a.ndim == 1 else a.reshape(a.shape[-2:])
    grads, deltas, new_m, new_v = {}, {}, {}, {}
    for name in big_names:
        grads[name] = g_shard[name]
        deltas[name], new_m[name], new_v[name] = _adamw(as2d(given[name]), g_shard[name], as2d(given["m_" + name]),
                                                         as2d(given["v_" + name]), name="adamw_" + name)
    small_names = [name for name in names if name not in big_names]
    groups = []
    for name in small_names:
        grads[name] = g_small[name].reshape(as2d(given[name]).shape)
        groups.append((as2d(given[name]), grads[name], as2d(given["m_" + name]), as2d(given["v_" + name])))
    for name, res in zip(small_names, _adamw_small(groups, name="adamw_small")):
        deltas[name], new_m[name], new_v[name] = res

    like = lambda name, a: a.reshape(given[name].shape)
    return (loss, grad_x[None], *[like(n, grads[n]) for n in names], *[like(n, deltas[n]) for n in names],
            *[like(n, new_m[n]) for n in names], *[like(n, new_v[n]) for n in names])
```

```python
import functools

import jax
import jax.numpy as jnp
from jax import lax
from jax.experimental import pallas as pl
from jax.experimental.pallas import tpu as pltpu

F32 = jnp.float32
BF16 = jnp.bfloat16
_CD = jnp.bfloat16
_TD = jnp.bfloat16

D_MODEL = 1024
N_MEM = 256
CONV_WIDTH = 512
CONV_GROUP = 64
CONV_K = 3
GLA_HEADS = 4
GLA_DK = 64
GLA_DV = 128
GLA_K_TOTAL = 256
GLA_V_TOTAL = 512
GLA_LOWRANK = 16
GLA_GATE_SCALE = 1.0 / 16.0
GLA_CHUNK = 64
XA_HEADS = 4
XA_HEAD_DIM = 256
D_FF = 4096
EPS = 1e-6
W_IN_COLS = 3104
ZA_COLS = 2048
ZB_COLS = 1152
LR_COL = 1024

ADAM_LR = 0.001
ADAM_B1 = 0.9
ADAM_B2 = 0.999
ADAM_EPS = 1e-08
ADAM_WD = 0.01
ADAM_STEP = 10

N_CHIPS = 4
PACK_W = 1024
PACK_ROWS = 4160
PACK_TILE = 160
SMALL_ROWS = 128

_TS = 512
_VMEM = 44 * 1024 * 1024
MESH = pl.DeviceIdType.MESH
ANY = pl.BlockSpec(memory_space=pl.ANY)


def _cp(sem=None, **kw):
    return pltpu.CompilerParams(dimension_semantics=sem, vmem_limit_bytes=_VMEM, **kw)


def _dot(a, b):
    return jnp.dot(a.astype(_CD), b.astype(_CD), preferred_element_type=F32)


def _dot_nt(a, b):
    return lax.dot_general(a.astype(_CD), b.astype(_CD), (((1,), (1,)), ((), ())), preferred_element_type=F32)


def _dot_tn(a, b):
    return lax.dot_general(a.astype(_CD), b.astype(_CD), (((0,), (0,)), ((), ())), preferred_element_type=F32)


def _dot_split(x, ones):
    hi = x.astype(BF16)
    r = x - hi.astype(F32)
    mid = r.astype(BF16)
    lo = (r - mid.astype(F32)).astype(BF16)
    d = lambda p: jnp.dot(p, ones, preferred_element_type=F32)
    return d(hi) + d(mid) + d(lo)


def _pick(n, cands=(1024, 640, 512, 256, 128)):
    for t in cands:
        if n % t == 0:
            return t
    return n


def _rows(s):
    return min(_TS, s)


def _sigmoid(v):
    e = jnp.exp(-jnp.abs(v))
    return jnp.where(v >= 0, 1.0 / (1.0 + e), e / (1.0 + e))


def _mm(a, b, *, mode, name, out_dtypes=(F32,), extras=(), epilogue=None, tm=None, tn=None, tk=None):
    m, k = a.shape
    n = b.shape[1] if mode == "nn" else b.shape[0]
    tm = min(m, tm or 1024)
    tn = tn or _pick(n)
    tk = tk or _pick(k)
    nk = k // tk
    n_ex, n_out = len(extras), len(out_dtypes)

    def body(*refs):
        a_ref, b_ref = refs[:2]
        ex = refs[2 : 2 + n_ex]
        outs = refs[2 + n_ex : 2 + n_ex + n_out]
        part = _dot(a_ref[...], b_ref[...]) if mode == "nn" else _dot_nt(a_ref[...], b_ref[...])

        def finish(acc):
            res = epilogue(acc, *[e[...] for e in ex]) if epilogue else (acc,)
            for o, r in zip(outs, res):
                o[...] = r.astype(o.dtype)

        if nk == 1:
            finish(part)
        else:
            acc_ref = refs[-1]
            kk = pl.program_id(2)

            @pl.when(kk == 0)
            def _():
                acc_ref[...] = part

            @pl.when(kk > 0)
            def _():
                acc_ref[...] += part

            @pl.when(kk == nk - 1)
            def _():
                finish(acc_ref[...])

    b_spec = pl.BlockSpec((tk, tn), lambda i, j, kk: (kk, j)) if mode == "nn" else pl.BlockSpec((tn, tk), lambda i, j, kk: (j, kk))
    tile = pl.BlockSpec((tm, tn), lambda i, j, kk: (i, j))
    out = pl.pallas_call(
        body,
        name=name,
        grid=(m // tm, n // tn, nk),
        in_specs=[pl.BlockSpec((tm, tk), lambda i, j, kk: (i, kk)), b_spec] + [tile] * n_ex,
        out_specs=[tile] * n_out,
        out_shape=[jax.ShapeDtypeStruct((m, n), dt) for dt in out_dtypes],
        scratch_shapes=[pltpu.VMEM((tm, tn), F32)] if nk > 1 else [],
        compiler_params=_cp(("parallel", "parallel", "arbitrary")),
    )(a, b, *extras)
    return out[0] if n_out == 1 else out


def _mm_tn(a, b, *, name):
    s, m = a.shape
    n = b.shape[1]
    cap = max(128, (1 << 20) // n)
    tm = _pick(m, tuple(t for t in (512, 640, 384, 256, 128) if t <= max(cap, 128)))
    ts = min(s, 1 << (((1 << 22) // n).bit_length() - 1))
    ns = s // ts

    def body(a_ref, b_ref, o_ref):
        part = _dot_tn(a_ref[...], b_ref[...])
        if ns == 1:
            o_ref[...] = part
        else:
            ss = pl.program_id(1)

            @pl.when(ss == 0)
            def _():
                o_ref[...] = part

            @pl.when(ss > 0)
            def _():
                o_ref[...] += part

    return pl.pallas_call(
        body,
        name=name,
        grid=(m // tm, ns),
        in_specs=[pl.BlockSpec((ts, tm), lambda i, ss: (ss, i)), pl.BlockSpec((ts, n), lambda i, ss: (ss, 0))],
        out_specs=pl.BlockSpec((tm, n), lambda i, ss: (i, 0)),
        out_shape=jax.ShapeDtypeStruct((m, n), F32),
        compiler_params=_cp(("parallel", "arbitrary")),
    )(a, b)


def _rms_fwd(x, g, *, name):
    s, d = x.shape
    ts = _rows(s)

    def body(x_ref, g_ref, o_ref):
        xf = x_ref[...]
        r = lax.rsqrt(jnp.mean(xf * xf, axis=-1, keepdims=True) + EPS)
        o_ref[...] = (xf * r * g_ref[...]).astype(o_ref.dtype)

    return pl.pallas_call(
        body,
        name=name,
        grid=(s // ts,),
        in_specs=[pl.BlockSpec((ts, d), lambda i: (i, 0)), pl.BlockSpec((1, d), lambda i: (0, 0))],
        out_specs=pl.BlockSpec((ts, d), lambda i: (i, 0)),
        out_shape=jax.ShapeDtypeStruct((s, d), _CD),
        compiler_params=_cp(("parallel",)),
    )(x, g)


def _rms_bwd(x, g, dy, dres=None, *, name, want_dx=True, want_lo=True):
    s, d = x.shape
    ts = _rows(s)
    has_res = dres is not None

    def body(*refs):
        x_ref, g_ref, dy_ref = refs[:3]
        pos = 3
        dres_ref = refs[pos] if has_res else None
        pos += has_res
        dx_ref = refs[pos] if want_dx else None
        pos += want_dx
        lo_ref = refs[pos] if want_lo else None
        pos += want_lo
        dg_ref = refs[pos]
        xf = x_ref[...]
        r = lax.rsqrt(jnp.mean(xf * xf, axis=-1, keepdims=True) + EPS)
        xh = xf * r
        dyf = dy_ref[...]
        part = jnp.sum(dyf * xh, axis=0, keepdims=True)

        @pl.when(pl.program_id(0) == 0)
        def _():
            dg_ref[...] = part

        @pl.when(pl.program_id(0) > 0)
        def _():
            dg_ref[...] += part

        if want_dx or want_lo:
            dxh = dyf * g_ref[...]
            dx = r * (dxh - xh * jnp.mean(dxh * xh, axis=-1, keepdims=True))
            if has_res:
                dx = dx + dres_ref[...]
            if want_dx:
                dx_ref[...] = dx
            if want_lo:
                lo_ref[...] = dx.astype(lo_ref.dtype)

    tile = pl.BlockSpec((ts, d), lambda i: (i, 0))
    vec = pl.BlockSpec((1, d), lambda i: (0, 0))
    out_specs, out_shape = [], []
    if want_dx:
        out_specs.append(tile)
        out_shape.append(jax.ShapeDtypeStruct((s, d), F32))
    if want_lo:
        out_specs.append(tile)
        out_shape.append(jax.ShapeDtypeStruct((s, d), _CD))
    out_specs.append(vec)
    out_shape.append(jax.ShapeDtypeStruct((1, d), F32))
    return pl.pallas_call(
        body,
        name=name,
        grid=(s // ts,),
        in_specs=[tile, vec, tile] + ([tile] if has_res else []),
        out_specs=out_specs,
        out_shape=out_shape,
        compiler_params=_cp(("arbitrary",)),
    )(x, g, dy, *([dres] if has_res else []))


def _final_loss(x3, g, tgt, *, name):
    s, d = x3.shape
    ts = _rows(s)

    def body(x_ref, g_ref, t_ref, dx_ref, lo_ref, loss_ref, dg_ref):
        xf = x_ref[...]
        r = lax.rsqrt(jnp.mean(xf * xf, axis=-1, keepdims=True) + EPS)
        xh = xf * r
        gg = g_ref[...]
        err = xh * gg - t_ref[...]
        lpart = jnp.zeros((1, 128), F32) + 0.5 * jnp.sum(jnp.mean(err * err, axis=-1, keepdims=True))
        dy = err * (1.0 / d)
        gpart = jnp.sum(dy * xh, axis=0, keepdims=True)

        @pl.when(pl.program_id(0) == 0)
        def _():
            loss_ref[...] = lpart
            dg_ref[...] = gpart

        @pl.when(pl.program_id(0) > 0)
        def _():
            loss_ref[...] += lpart
            dg_ref[...] += gpart

        dxh = dy * gg
        dx = r * (dxh - xh * jnp.mean(dxh * xh, axis=-1, keepdims=True))
        dx_ref[...] = dx
        lo_ref[...] = dx.astype(lo_ref.dtype)

    tile = pl.BlockSpec((ts, d), lambda i: (i, 0))
    vec = pl.BlockSpec((1, d), lambda i: (0, 0))
    return pl.pallas_call(
        body,
        name=name,
        grid=(s // ts,),
        in_specs=[tile, vec, tile],
        out_specs=[tile, tile, pl.BlockSpec((1, 128), lambda i: (0, 0)), vec],
        out_shape=[
            jax.ShapeDtypeStruct((s, d), F32),
            jax.ShapeDtypeStruct((s, d), _CD),
            jax.ShapeDtypeStruct((1, 128), F32),
            jax.ShapeDtypeStruct((1, d), F32),
        ],
        compiler_params=_cp(("arbitrary",)),
    )(x3, g, tgt)


def _chunk_scan(v, row_in_chunk, suffix):
    t = v.shape[0]
    step = 1
    while step < GLA_CHUNK:
        if suffix:
            v = v + jnp.where(row_in_chunk < GLA_CHUNK - step, pltpu.roll(v, t - step, 0), 0.0)
        else:
            v = v + jnp.where(row_in_chunk >= step, pltpu.roll(v, step, 0), 0.0)
        step *= 2
    return v


def _gate_pre(lr, w_ref, b_ref):
    return _dot(lr, w_ref[...]) + b_ref[...]


def _gate_fwd(z, waf, wab, baf, bab, *, name):
    s = z.shape[0]
    ts = _rows(s)

    def body(lr_ref, waf_ref, wab_ref, baf_ref, bab_ref, bf_ref, bb_ref):
        lr = lr_ref[...]
        ric = lax.broadcasted_iota(jnp.int32, (ts, GLA_K_TOTAL), 0) & (GLA_CHUNK - 1)
        for w_ref, b_ref, o_ref, suffix in ((waf_ref, baf_ref, bf_ref, False), (wab_ref, bab_ref, bb_ref, True)):
            pre = _gate_pre(lr, w_ref, b_ref)
            la = (jnp.minimum(pre, 0.0) - jnp.log(1.0 + jnp.exp(-jnp.abs(pre)))) * GLA_GATE_SCALE
            o_ref[...] = _chunk_scan(la, ric, suffix)

    wspec = pl.BlockSpec((128, GLA_K_TOTAL), lambda i: (0, 0))
    bspec = pl.BlockSpec((1, GLA_K_TOTAL), lambda i: (0, 0))
    tile = pl.BlockSpec((ts, GLA_K_TOTAL), lambda i: (i, 0))
    return pl.pallas_call(
        body,
        name=name,
        grid=(s // ts,),
        in_specs=[pl.BlockSpec((ts, 128), lambda i: (i, LR_COL // 128)), wspec, wspec, bspec, bspec],
        out_specs=[tile, tile],
        out_shape=[jax.ShapeDtypeStruct((s, GLA_K_TOTAL), F32)] * 2,
        compiler_params=_cp(("parallel",)),
    )(z, waf, wab, baf, bab)


def _gate_bwd(z, waf, wab, baf, bab, dbf, dbb, dqkv_f, dqkv_b, *, name):
    s = z.shape[0]
    ts = _rows(s)

    def body(lr_ref, waf_ref, wab_ref, baf_ref, bab_ref, dbf_ref, dbb_ref, gf_ref, gb_ref, dzb_ref, dwf_ref, dwb_ref, dbaf_ref, dbab_ref):
        lr = lr_ref[...]
        ric = lax.broadcasted_iota(jnp.int32, (ts, GLA_K_TOTAL), 0) & (GLA_CHUNK - 1)
        first = pl.program_id(0) == 0
        dlr = None
        for w_ref, b_ref, db_ref, dw_ref, dbias_ref, suffix in (
            (waf_ref, baf_ref, dbf_ref, dwf_ref, dbaf_ref, True),
            (wab_ref, bab_ref, dbb_ref, dwb_ref, dbab_ref, False),
        ):
            pre = _gate_pre(lr, w_ref, b_ref)
            dla = _chunk_scan(db_ref[...], ric, suffix)
            dpre = dla * GLA_GATE_SCALE * _sigmoid(-pre)
            part = _dot_nt(dpre, w_ref[...])
            dlr = part if dlr is None else dlr + part
            dw = _dot_tn(lr, dpre)
            dbias = jnp.sum(dpre, axis=0, keepdims=True)

            @pl.when(first)
            def _():
                dw_ref[...] = dw
                dbias_ref[...] = dbias

            @pl.when(jnp.logical_not(first))
            def _():
                dw_ref[...] += dw
                dbias_ref[...] += dbias

        dzb_ref[...] = jnp.concatenate([gf_ref[...] + gb_ref[...], dlr], axis=1).astype(dzb_ref.dtype)

    wspec = pl.BlockSpec((128, GLA_K_TOTAL), lambda i: (0, 0))
    bspec = pl.BlockSpec((1, GLA_K_TOTAL), lambda i: (0, 0))
    tile = pl.BlockSpec((ts, GLA_K_TOTAL), lambda i: (i, 0))
    wide = pl.BlockSpec((ts, 2 * GLA_K_TOTAL + GLA_V_TOTAL), lambda i: (i, 0))
    return pl.pallas_call(
        body,
        name=name,
        grid=(s // ts,),
        in_specs=[pl.BlockSpec((ts, 128), lambda i: (i, LR_COL // 128)), wspec, wspec, bspec, bspec, tile, tile, wide, wide],
        out_specs=[pl.BlockSpec((ts, ZB_COLS), lambda i: (i, 0)), wspec, wspec, bspec, bspec],
        out_shape=[
            jax.ShapeDtypeStruct((s, ZB_COLS), _CD),
            jax.ShapeDtypeStruct((128, GLA_K_TOTAL), F32),
            jax.ShapeDtypeStruct((128, GLA_K_TOTAL), F32),
            jax.ShapeDtypeStruct((1, GLA_K_TOTAL), F32),
            jax.ShapeDtypeStruct((1, GLA_K_TOTAL), F32),
        ],
        compiler_params=_cp(("arbitrary",)),
    )(z, waf, wab, baf, bab, dbf, dbb, dqkv_f, dqkv_b)


def _gla_masks(rev):
    lane_head = lax.broadcasted_iota(jnp.int32, (1, GLA_K_TOTAL), 1) >> 6
    head_masks = [lane_head == h for h in range(GLA_HEADS)]
    st_rows = lax.broadcasted_iota(jnp.int32, (GLA_V_TOTAL, GLA_K_TOTAL), 0) >> 7
    st_lanes = lax.broadcasted_iota(jnp.int32, (GLA_V_TOTAL, GLA_K_TOTAL), 1) >> 6
    block_mask = st_rows == st_lanes
    t = lax.broadcasted_iota(jnp.int32, (GLA_HEADS * GLA_CHUNK, GLA_CHUNK), 0) & (GLA_CHUNK - 1)
    u = lax.broadcasted_iota(jnp.int32, (GLA_HEADS * GLA_CHUNK, GLA_CHUNK), 1)
    tri = (u > t) if rev else (u <= t)
    row = lax.broadcasted_iota(jnp.int32, (GLA_CHUNK, GLA_K_TOTAL), 0)
    total_row = row == (0 if rev else GLA_CHUNK - 1)
    return head_masks, block_mask, tri, total_row


def _gla_chunk_terms(q_ref, k_ref, v_ref, b_ref, rows, head_masks, tri, total_row):
    q = q_ref[rows, :] * (GLA_DK**-0.5)
    k = k_ref[rows, :]
    v = v_ref[rows, :]
    b = b_ref[rows, :]
    eb = jnp.exp(b)
    enb = jnp.exp(-b)
    g = jnp.sum(jnp.where(total_row, b, 0.0), axis=0, keepdims=True)
    egb = jnp.exp(g - b)
    qt = q * eb
    kt = k * enb
    kh = k * egb
    q_heads = jnp.concatenate([jnp.where(m, qt, 0.0) for m in head_masks], axis=0)
    attn = jnp.where(tri, _dot_nt(q_heads, kt), 0.0)
    return v, eb, enb, egb, jnp.exp(g), qt, kt, kh, attn


def _gla_specs(s, tb, rev_blocks):
    nb = s // tb
    rb = (lambda i: nb - 1 - i) if rev_blocks else (lambda i: i)
    q_spec = pl.BlockSpec((tb, GLA_K_TOTAL), lambda i: (rb(i), 0))
    k_spec = pl.BlockSpec((tb, GLA_K_TOTAL), lambda i: (rb(i), 1))
    v_spec = pl.BlockSpec((tb, GLA_V_TOTAL), lambda i: (rb(i), 1))
    b_spec = pl.BlockSpec((tb, GLA_K_TOTAL), lambda i: (rb(i), 0))
    o_spec = pl.BlockSpec((tb, GLA_V_TOTAL), lambda i: (rb(i), 0))
    st_spec = pl.BlockSpec((tb // GLA_CHUNK, GLA_DV, GLA_K_TOTAL), lambda i: (rb(i), 0, 0))
    return nb, q_spec, k_spec, v_spec, b_spec, o_spec, st_spec


def _gla_fwd(z, b, *, rev, name):
    s = z.shape[0]
    tb = _rows(s)
    cpb = tb // GLA_CHUNK
    nb, q_spec, k_spec, v_spec, b_spec, o_spec, st_spec = _gla_specs(s, tb, rev)

    def body(q_ref, k_ref, v_ref, b_ref, o_ref, sv_ref, st_ref):
        head_masks, block_mask, tri, total_row = _gla_masks(rev)

        @pl.when(pl.program_id(0) == 0)
        def _():
            st_ref[...] = jnp.zeros_like(st_ref)

        def chunk(ci, carry):
            cidx = cpb - 1 - ci if rev else ci
            rows = pl.ds(pl.multiple_of(cidx * GLA_CHUNK, GLA_CHUNK), GLA_CHUNK)
            v, _, _, _, eg, qt, _, kh, attn = _gla_chunk_terms(q_ref, k_ref, v_ref, b_ref, rows, head_masks, tri, total_row)
            o = jnp.concatenate(
                [_dot(attn[GLA_CHUNK * h : GLA_CHUNK * (h + 1)], v[:, GLA_DV * h : GLA_DV * (h + 1)]) for h in range(GLA_HEADS)], axis=1
            )
            st = st_ref[...]
            o_ref[rows, :] = o + _dot_nt(qt, st)
            sv_ref[cidx] = st[0:128] + st[128:256] + st[256:384] + st[384:512]
            st_ref[...] = st * eg + jnp.where(block_mask, _dot_tn(v, kh), 0.0)
            return carry

        lax.fori_loop(0, cpb, chunk, 0)

    return pl.pallas_call(
        body,
        name=name,
        grid=(nb,),
        in_specs=[q_spec, k_spec, v_spec, b_spec],
        out_specs=[o_spec, st_spec],
        out_shape=[jax.ShapeDtypeStruct((s, GLA_V_TOTAL), F32), jax.ShapeDtypeStruct((s // GLA_CHUNK, GLA_DV, GLA_K_TOTAL), F32)],
        scratch_shapes=[pltpu.VMEM((GLA_V_TOTAL, GLA_K_TOTAL), F32)],
        compiler_params=_cp(("arbitrary",)),
    )(z, z, z, b)


def _gla_bwd(z, b, do, states, *, rev, name):
    s = z.shape[0]
    tb = _rows(s)
    cpb = tb // GLA_CHUNK
    nb, q_spec, k_spec, v_spec, b_spec, o_spec, st_spec = _gla_specs(s, tb, not rev)
    rb = (lambda i: nb - 1 - i) if not rev else (lambda i: i)

    def body(q_ref, k_ref, v_ref, b_ref, do_ref, sv_ref, dqkv_ref, db_ref, dst_ref):
        head_masks, block_mask, tri, total_row = _gla_masks(rev)

        @pl.when(pl.program_id(0) == 0)
        def _():
            dst_ref[...] = jnp.zeros_like(dst_ref)

        def chunk(ci, carry):
            cidx = ci if rev else cpb - 1 - ci
            rows = pl.ds(pl.multiple_of(cidx * GLA_CHUNK, GLA_CHUNK), GLA_CHUNK)
            v, eb, enb, egb, eg, qt, kt, kh, attn = _gla_chunk_terms(q_ref, k_ref, v_ref, b_ref, rows, head_masks, tri, total_row)
            do_c = do_ref[rows, :]
            saved = sv_ref[cidx]
            st = jnp.where(block_mask, jnp.concatenate([saved] * GLA_HEADS, axis=0), 0.0)
            dst = dst_ref[...]
            hs = lambda a, h: a[GLA_CHUNK * h : GLA_CHUNK * (h + 1)]
            vs = lambda a, h: a[:, GLA_DV * h : GLA_DV * (h + 1)]
            dattn = jnp.concatenate([_dot_nt(vs(do_c, h), vs(v, h)) for h in range(GLA_HEADS)], axis=0)
            dattn = jnp.where(tri, dattn, 0.0)
            dv = jnp.concatenate([_dot_tn(hs(attn, h), vs(do_c, h)) for h in range(GLA_HEADS)], axis=1) + _dot_nt(kh, dst)
            dqt = _dot(do_c, st)
            dkt = jnp.zeros_like(dqt)
            for h in range(GLA_HEADS):
                dqt = dqt + jnp.where(head_masks[h], _dot(hs(dattn, h), kt), 0.0)
                dkt = dkt + jnp.where(head_masks[h], _dot_tn(hs(dattn, h), qt), 0.0)
            dkh = _dot(v, dst)
            dg = jnp.sum(dkh * kh, axis=0, keepdims=True) + jnp.sum(dst * st, axis=0, keepdims=True) * eg
            db = dqt * qt - dkt * kt - dkh * kh + jnp.where(total_row, dg, 0.0)
            dq = dqt * eb * (GLA_DK**-0.5)
            dk = dkt * enb + dkh * egb
            dqkv_ref[rows, :] = jnp.concatenate([dq, dk, dv], axis=1)
            db_ref[rows, :] = db
            dst_ref[...] = dst * eg + jnp.where(block_mask, _dot_tn(do_c, qt), 0.0)
            return carry

        lax.fori_loop(0, cpb, chunk, 0)

    wide = 2 * GLA_K_TOTAL + GLA_V_TOTAL
    return pl.pallas_call(
        body,
        name=name,
        grid=(nb,),
        in_specs=[q_spec, k_spec, v_spec, b_spec, o_spec, st_spec],
        out_specs=[pl.BlockSpec((tb, wide), lambda i: (rb(i), 0)), b_spec],
        out_shape=[jax.ShapeDtypeStruct((s, wide), F32), jax.ShapeDtypeStruct((s, GLA_K_TOTAL), F32)],
        scratch_shapes=[pltpu.VMEM((GLA_V_TOTAL, GLA_K_TOTAL), F32)],
        compiler_params=_cp(("arbitrary",)),
    )(z, z, z, b, do, states)


HALO = 8


def _halo_specs(s, ts, width, col):
    last = s // HALO - 1
    per = ts // HALO
    prev = pl.BlockSpec((HALO, width), lambda i: (jnp.maximum(i * per - 1, 0), col))
    nxt = pl.BlockSpec((HALO, width), lambda i: (jnp.minimum((i + 1) * per, last), col))
    return prev, nxt


def _group_ones():
    r = lax.broadcasted_iota(jnp.int32, (CONV_WIDTH, CONV_WIDTH), 0) >> 6
    c = lax.broadcasted_iota(jnp.int32, (CONV_WIDTH, CONV_WIDTH), 1) >> 6
    return (r == c).astype(BF16)


def _conv_terms(cc_ext, cu_ext, cw, valid):
    n = cc_ext.shape[0]
    hc = jnp.where(valid, cc_ext * cu_ext, 0.0)
    hc_prev = pltpu.roll(hc, 1, 0)
    hc_next = pltpu.roll(hc, n - 1, 0)
    conv = cw[0:1] * hc_prev + cw[1:2] * hc + cw[2:3] * hc_next
    return hc, hc_prev, hc_next, conv


def _ext(prev_ref, cur_ref, next_ref):
    return jnp.concatenate([prev_ref[...], cur_ref[...], next_ref[...]], axis=0)


def _valid_rows(ts, s):
    row = lax.broadcasted_iota(jnp.int32, (ts + 2 * HALO, 1), 0) + (pl.program_id(0) * ts - HALO)
    return (row >= 0) & (row < s)


def _head_norm(o, gn):
    out = []
    for h in range(GLA_HEADS):
        oh = o[:, GLA_DV * h : GLA_DV * (h + 1)]
        r = lax.rsqrt(jnp.mean(oh * oh, axis=-1, keepdims=True) + EPS)
        out.append((oh * r, r))
    return out


def _mix_fwd(z, o_f, o_b, conv_w, conv_norm, gla_norm, *, name):
    s = z.shape[0]
    ts = _rows(s)
    cprev, cnext = _halo_specs(s, ts, CONV_WIDTH, 1)
    uprev, unext = _halo_specs(s, ts, CONV_WIDTH, 2)

    def body(cb_ref, cc_ref, cu_ref, ccp_ref, ccn_ref, cup_ref, cun_ref, g_ref, of_ref, ob_ref, cw_ref, cn_ref, gn_ref, y_ref):
        valid = _valid_rows(ts, s)
        _, _, _, conv = _conv_terms(_ext(ccp_ref, cc_ref, ccn_ref), _ext(cup_ref, cu_ref, cun_ref), cw_ref[...], valid)
        yc = cb_ref[...] * conv[HALO : HALO + ts]
        ms = _dot_split(yc * yc, _group_ones()) * (1.0 / CONV_GROUP)
        y_conv = yc * lax.rsqrt(ms + EPS) * cn_ref[...]
        gate = g_ref[...]
        silu = gate * _sigmoid(gate)
        gn = gn_ref[...]
        y_gla = jnp.concatenate([oh * gn for oh, _ in _head_norm(of_ref[...] + ob_ref[...], gn)], axis=1) * silu
        y_ref[...] = jnp.concatenate([y_conv, y_gla], axis=1).astype(y_ref.dtype)

    col = lambda c, w=CONV_WIDTH: pl.BlockSpec((ts, w), lambda i: (i, c))
    return pl.pallas_call(
        body,
        name=name,
        grid=(s // ts,),
        in_specs=[col(0), col(1), col(2), cprev, cnext, uprev, unext, col(3), col(0), col(0),
                  pl.BlockSpec((CONV_K, CONV_WIDTH), lambda i: (0, 0)), pl.BlockSpec((1, CONV_WIDTH), lambda i: (0, 0)),
                  pl.BlockSpec((1, GLA_DV), lambda i: (0, 0))],
        out_specs=pl.BlockSpec((ts, D_MODEL), lambda i: (i, 0)),
        out_shape=jax.ShapeDtypeStruct((s, D_MODEL), _CD),
        compiler_params=_cp(("parallel",)),
    )(z, z, z, z, z, z, z, z, o_f, o_b, conv_w, conv_norm, gla_norm)


def _mix_bwd(z, o_f, o_b, dy, conv_w, conv_norm, gla_norm, *, name):
    s = z.shape[0]
    ts = _rows(s)
    halos = [_halo_specs(s, ts, CONV_WIDTH, c) for c in (0, 1, 2)]
    dprev, dnext = _halo_specs(s, ts, CONV_WIDTH, 0)

    def body(cb_ref, cc_ref, cu_ref, cbp_ref, cbn_ref, ccp_ref, ccn_ref, cup_ref, cun_ref, g_ref, of_ref, ob_ref,
             dyc_ref, dyg_ref, dyp_ref, dyn_ref, cw_ref, cn_ref, gn_ref, dza_ref, do_ref, dcw_ref, dcn_ref, dgn_ref):
        n = ts + 2 * HALO
        valid = _valid_rows(ts, s)
        cw = cw_ref[...]
        cn = cn_ref[...]
        ones = _group_ones()
        cb = _ext(cbp_ref, cb_ref, cbn_ref)
        cc = _ext(ccp_ref, cc_ref, ccn_ref)
        cu = _ext(cup_ref, cu_ref, cun_ref)
        dy = _ext(dyp_ref, dyc_ref, dyn_ref)
        hc, hc_prev, hc_next, conv = _conv_terms(cc, cu, cw, valid)
        yc = cb * conv
        r = lax.rsqrt(_dot_split(yc * yc, ones) * (1.0 / CONV_GROUP) + EPS)
        yh = yc * r
        dyh = dy * cn
        dyc = r * (dyh - yh * (_dot_split(dyh * yh, ones) * (1.0 / CONV_GROUP)))
        dconv = jnp.where(valid, dyc * cb, 0.0)
        dhc = cw[0:1] * pltpu.roll(dconv, n - 1, 0) + cw[1:2] * dconv + cw[2:3] * pltpu.roll(dconv, 1, 0)
        mid = lambda a: a[HALO : HALO + ts]
        dza_ref[:, 0 : 3 * CONV_WIDTH] = jnp.concatenate([mid(dyc * conv), mid(dhc * cu), mid(dhc * cc)], axis=1).astype(dza_ref.dtype)
        dconv_m = mid(dconv)
        colsum = lambda a: jnp.sum(a, axis=0, keepdims=True)
        dcw = jnp.concatenate([colsum(dconv_m * mid(hc_prev)), colsum(dconv_m * mid(hc)), colsum(dconv_m * mid(hc_next))], axis=0)
        dcn = colsum(mid(dy * yh))

        gate = g_ref[...]
        sg = _sigmoid(gate)
        silu = gate * sg
        gn = gn_ref[...]
        dyg = dyg_ref[...]
        don = dyg * silu
        heads = _head_norm(of_ref[...] + ob_ref[...], gn)
        on = jnp.concatenate([oh * gn for oh, _ in heads], axis=1)
        dza_ref[:, 3 * CONV_WIDTH : ZA_COLS] = (dyg * on * (sg * (1.0 + gate * (1.0 - sg)))).astype(dza_ref.dtype)
        dgn = jnp.zeros((1, GLA_DV), F32)
        dos = []
        for h, (oh, rh) in enumerate(heads):
            donh = don[:, GLA_DV * h : GLA_DV * (h + 1)]
            dgn = dgn + colsum(donh * oh)
            doh = donh * gn
            dos.append(rh * (doh - oh * jnp.mean(doh * oh, axis=-1, keepdims=True)))
        do_ref[...] = jnp.concatenate(dos, axis=1)

        first = pl.program_id(0) == 0

        @pl.when(first)
        def _():
            dcw_ref[...] = dcw
            dcn_ref[...] = dcn
            dgn_ref[...] = dgn

        @pl.when(jnp.logical_not(first))
        def _():
            dcw_ref[...] += dcw
            dcn_ref[...] += dcn
            dgn_ref[...] += dgn

    col = lambda c, w=CONV_WIDTH: pl.BlockSpec((ts, w), lambda i: (i, c))
    cw_spec = pl.BlockSpec((CONV_K, CONV_WIDTH), lambda i: (0, 0))
    cn_spec = pl.BlockSpec((1, CONV_WIDTH), lambda i: (0, 0))
    gn_spec = pl.BlockSpec((1, GLA_DV), lambda i: (0, 0))
    return pl.pallas_call(
        body,
        name=name,
        grid=(s // ts,),
        in_specs=[col(0), col(1), col(2), halos[0][0], halos[0][1], halos[1][0], halos[1][1], halos[2][0], halos[2][1],
                  col(3), col(0), col(0), col(0), col(1), dprev, dnext, cw_spec, cn_spec, gn_spec],
        out_specs=[pl.BlockSpec((ts, ZA_COLS), lambda i: (i, 0)), col(0), cw_spec, cn_spec, gn_spec],
        out_shape=[
            jax.ShapeDtypeStruct((s, ZA_COLS), _CD),
            jax.ShapeDtypeStruct((s, GLA_V_TOTAL), F32),
            jax.ShapeDtypeStruct((CONV_K, CONV_WIDTH), F32),
            jax.ShapeDtypeStruct((1, CONV_WIDTH), F32),
            jax.ShapeDtypeStruct((1, GLA_DV), F32),
        ],
        compiler_params=_cp(("arbitrary",)),
    )(z, z, z, z, z, z, z, z, z, z, o_f, o_b, dy, dy, dy, dy, conv_w, conv_norm, gla_norm)


def _xa_probs(q_ref, kv_ref, h):
    qh = q_ref[:, XA_HEAD_DIM * h : XA_HEAD_DIM * (h + 1)]
    kh = kv_ref[:, XA_HEAD_DIM * h : XA_HEAD_DIM * (h + 1)]
    vh = kv_ref[:, D_MODEL + XA_HEAD_DIM * h : D_MODEL + XA_HEAD_DIM * (h + 1)]
    sc = _dot_nt(qh, kh) * (XA_HEAD_DIM**-0.5)
    e = jnp.exp(sc - jnp.max(sc, axis=-1, keepdims=True))
    return qh, kh, vh, e / jnp.sum(e, axis=-1, keepdims=True)


def _xattn_fwd(qx, kv, *, name):
    s = qx.shape[0]
    ts = _rows(s)

    def body(q_ref, kv_ref, o_ref):
        outs = []
        for h in range(XA_HEADS):
            _, _, vh, p = _xa_probs(q_ref, kv_ref, h)
            outs.append(_dot(p, vh))
        o_ref[...] = jnp.concatenate(outs, axis=1).astype(o_ref.dtype)

    return pl.pallas_call(
        body,
        name=name,
        grid=(s // ts,),
        in_specs=[pl.BlockSpec((ts, D_MODEL), lambda i: (i, 0)), pl.BlockSpec((N_MEM, 2 * D_MODEL), lambda i: (0, 0))],
        out_specs=pl.BlockSpec((ts, D_MODEL), lambda i: (i, 0)),
        out_shape=jax.ShapeDtypeStruct((s, D_MODEL), _CD),
        compiler_params=_cp(("parallel",)),
    )(qx, kv)


def _xattn_bwd(qx, kv, dox, *, name):
    s = qx.shape[0]
    ts = _rows(s)

    def body(q_ref, kv_ref, do_ref, dq_ref, dkv_ref):
        dqs, dks, dvs = [], [], []
        for h in range(XA_HEADS):
            qh, kh, vh, p = _xa_probs(q_ref, kv_ref, h)
            doh = do_ref[:, XA_HEAD_DIM * h : XA_HEAD_DIM * (h + 1)]
            dp = _dot_nt(doh, vh)
            ds = p * (dp - jnp.sum(dp * p, axis=-1, keepdims=True)) * (XA_HEAD_DIM**-0.5)
            dqs.append(_dot(ds, kh))
            dks.append(_dot_tn(ds, qh))
            dvs.append(_dot_tn(p, doh))
        dq_ref[...] = jnp.concatenate(dqs, axis=1).astype(dq_ref.dtype)
        dkv = jnp.concatenate(dks + dvs, axis=1)

        @pl.when(pl.program_id(0) == 0)
        def _():
            dkv_ref[...] = dkv

        @pl.when(pl.program_id(0) > 0)
        def _():
            dkv_ref[...] += dkv

    tile = pl.BlockSpec((ts, D_MODEL), lambda i: (i, 0))
    kv_spec = pl.BlockSpec((N_MEM, 2 * D_MODEL), lambda i: (0, 0))
    return pl.pallas_call(
        body,
        name=name,
        grid=(s // ts,),
        in_specs=[tile, kv_spec, tile],
        out_specs=[tile, kv_spec],
        out_shape=[jax.ShapeDtypeStruct((s, D_MODEL), _CD), jax.ShapeDtypeStruct((N_MEM, 2 * D_MODEL), F32)],
        compiler_params=_cp(("arbitrary",)),
    )(qx, kv, dox)


def _adamw_math(w, g, m, v):
    m = ADAM_B1 * m + (1.0 - ADAM_B1) * g
    v = ADAM_B2 * v + (1.0 - ADAM_B2) * (g * g)
    m_hat = m / (1.0 - ADAM_B1**ADAM_STEP)
    v_hat = v / (1.0 - ADAM_B2**ADAM_STEP)
    delta = -ADAM_LR * (m_hat / (jnp.sqrt(v_hat) + ADAM_EPS) + ADAM_WD * w)
    return delta, m, v


def _adamw(w, g, m, v, *, name):
    r, c = w.shape
    tr = _pick(r, (256, 128, 64, 32, 16, 8))

    def body(w_ref, g_ref, m_ref, v_ref, d_ref, nm_ref, nv_ref):
        d_ref[...], nm_ref[...], nv_ref[...] = _adamw_math(w_ref[...], g_ref[...], m_ref[...], v_ref[...])

    tile = pl.BlockSpec((tr, c), lambda i: (i, 0))
    return pl.pallas_call(
        body,
        name=name,
        grid=(r // tr,),
        in_specs=[tile] * 4,
        out_specs=[tile] * 3,
        out_shape=[jax.ShapeDtypeStruct((r, c), F32)] * 3,
        compiler_params=_cp(("parallel",)),
    )(w, g, m, v)


def _adamw_small(groups, *, name):
    n = len(groups)

    def body(*refs):
        ins, outs = refs[: 4 * n], refs[4 * n :]
        for i in range(n):
            w_ref, g_ref, m_ref, v_ref = ins[4 * i : 4 * i + 4]
            outs[3 * i][...], outs[3 * i + 1][...], outs[3 * i + 2][...] = _adamw_math(w_ref[...], g_ref[...], m_ref[...], v_ref[...])

    flat = [a for grp in groups for a in grp]
    vm = pl.BlockSpec(memory_space=pltpu.VMEM)
    res = pl.pallas_call(
        body,
        name=name,
        in_specs=[vm] * (4 * n),
        out_specs=[vm] * (3 * n),
        out_shape=[jax.ShapeDtypeStruct(grp[0].shape, F32) for grp in groups for _ in range(3)],
        compiler_params=_cp(),
    )(*flat)
    return [tuple(res[3 * i : 3 * i + 3]) for i in range(n)]


def _place():
    return lax.axis_index("x"), lax.axis_index("y"), lax.axis_index("c")


def _rel_chip(x, y, k):
    return (1 - x if k & 2 else x), (1 - y if k & 1 else y)


def _half(c, rh):
    return pl.ds(pl.multiple_of(c * rh, 16), rh)


def _gather_weights(pack):
    r, w = pack.shape
    rh = r // 2

    def body(p_ref, q_ref, send_sems, recv_sems):
        x, y, c = _place()
        j = 2 * x + y
        rows = _half(c, rh)

        def to_chip(k):
            cx, cy = _rel_chip(x, y, k)
            return pltpu.make_async_remote_copy(
                src_ref=p_ref.at[rows], dst_ref=q_ref.at[j, rows], send_sem=send_sems.at[k - 1], recv_sem=recv_sems.at[k - 1],
                device_id=(cx, cy, c), device_id_type=MESH)

        def to_sibling(k):
            cx, cy = _rel_chip(x, y, k)
            slot = q_ref.at[2 * cx + cy, rows]
            return pltpu.make_async_remote_copy(
                src_ref=slot, dst_ref=slot, send_sem=send_sems.at[2 + k], recv_sem=recv_sems.at[2 + k],
                device_id=(x, y, 1 - c), device_id_type=MESH)

        first = [to_chip(k) for k in range(1, N_CHIPS)]
        passed = [to_sibling(k) for k in range(1, N_CHIPS)]
        for cp in first:
            cp.start()
        for cp, fw in zip(first, passed):
            cp.wait_recv()
            fw.start()
        for fw in passed:
            fw.wait_recv()
        for cp in first + passed:
            cp.wait_send()

    return pl.pallas_call(
        body,
        name="gather_weights",
        in_specs=[ANY],
        out_specs=ANY,
        out_shape=jax.ShapeDtypeStruct((N_CHIPS, r, w), pack.dtype),
        scratch_shapes=[pltpu.SemaphoreType.DMA((6,)), pltpu.SemaphoreType.DMA((6,))],
        compiler_params=pltpu.CompilerParams(has_side_effects=True),
    )(pack)


def _swap_halves(g):
    n, r, w = g.shape
    rh = r // 2

    def body(g_ref, o_ref, send_sem, recv_sem):
        x, y, c = _place()
        cp = pltpu.make_async_remote_copy(
            src_ref=g_ref.at[:, _half(1 - c, rh)], dst_ref=o_ref, send_sem=send_sem, recv_sem=recv_sem,
            device_id=(x, y, 1 - c), device_id_type=MESH)
        cp.start()
        cp.wait()

    return pl.pallas_call(
        body,
        name="grads_to_sibling",
        in_specs=[ANY],
        out_specs=ANY,
        out_shape=jax.ShapeDtypeStruct((n, rh, w), g.dtype),
        scratch_shapes=[pltpu.SemaphoreType.DMA, pltpu.SemaphoreType.DMA],
        compiler_params=pltpu.CompilerParams(has_side_effects=True),
    )(g)


def _chip_sums(g, got, where):
    n, r, w = g.shape
    rh = r // 2
    nt = rh // PACK_TILE

    def body(where_ref, g_ref, got_ref, o_ref):
        o_ref[...] = (g_ref[...] + got_ref[...]).astype(o_ref.dtype)

    return pl.pallas_call(
        body,
        name="chip_sums",
        grid_spec=pltpu.PrefetchScalarGridSpec(
            num_scalar_prefetch=1,
            grid=(n, nt),
            in_specs=[pl.BlockSpec((1, PACK_TILE, w), lambda a, i, wh: (a, wh[0] * nt + i, 0)),
                      pl.BlockSpec((1, PACK_TILE, w), lambda a, i, wh: (a, i, 0))],
            out_specs=pl.BlockSpec((1, PACK_TILE, w), lambda a, i, wh: (a, i, 0)),
        ),
        out_shape=jax.ShapeDtypeStruct((n, rh, w), _TD),
        compiler_params=_cp(("parallel", "parallel")),
    )(where, g, got)


def _exchange_chip_sums(h):
    n, rh, w = h.shape

    def body(h_ref, o_ref, send_sems, recv_sems):
        x, y, c = _place()
        j = 2 * x + y
        copies = []
        for k in range(1, N_CHIPS):
            cx, cy = _rel_chip(x, y, k)
            copies.append(pltpu.make_async_remote_copy(
                src_ref=h_ref.at[2 * cx + cy], dst_ref=o_ref.at[k - 1], send_sem=send_sems.at[k - 1], recv_sem=recv_sems.at[k - 1],
                device_id=(cx, cy, c), device_id_type=MESH))
        for cp in copies:
            cp.start()
        for cp in copies:
            cp.wait()

    return pl.pallas_call(
        body,
        name="chip_sums_exchange",
        in_specs=[ANY],
        out_specs=ANY,
        out_shape=jax.ShapeDtypeStruct((N_CHIPS - 1, rh, w), h.dtype),
        scratch_shapes=[pltpu.SemaphoreType.DMA((3,)), pltpu.SemaphoreType.DMA((3,))],
        compiler_params=pltpu.CompilerParams(has_side_effects=True),
    )(h)


def _shard_sum(g, got, others, where):
    n, r, w = g.shape
    rh = r // 2
    nt = rh // PACK_TILE

    def body(where_ref, g_ref, got_ref, oth_ref, o_ref):
        acc = g_ref[0] + got_ref[0]
        for k in range(N_CHIPS - 1):
            acc = acc + oth_ref[k].astype(F32)
        o_ref[...] = acc

    return pl.pallas_call(
        body,
        name="shard_sum",
        grid_spec=pltpu.PrefetchScalarGridSpec(
            num_scalar_prefetch=1,
            grid=(nt,),
            in_specs=[pl.BlockSpec((1, PACK_TILE, w), lambda i, wh: (wh[1], wh[0] * nt + i, 0)),
                      pl.BlockSpec((1, PACK_TILE, w), lambda i, wh: (wh[1], i, 0)),
                      pl.BlockSpec((N_CHIPS - 1, PACK_TILE, w), lambda i, wh: (0, i, 0))],
            out_specs=pl.BlockSpec((PACK_TILE, w), lambda i, wh: (i, 0)),
        ),
        out_shape=jax.ShapeDtypeStruct((rh, w), F32),
        compiler_params=_cp(("parallel",)),
    )(where, g, got, others)


def _join_halves(e):
    rh, w = e.shape

    def body(e_ref, o_ref, send_sem, recv_sem, local_sem):
        x, y, c = _place()
        rows = _half(c, rh)
        mine = pltpu.make_async_copy(e_ref, o_ref.at[rows], local_sem)
        mine.start()
        cp = pltpu.make_async_remote_copy(
            src_ref=e_ref, dst_ref=o_ref.at[rows], send_sem=send_sem, recv_sem=recv_sem, device_id=(x, y, 1 - c), device_id_type=MESH)
        cp.start()
        cp.wait()
        mine.wait()

    return pl.pallas_call(
        body,
        name="shard_to_sibling",
        in_specs=[ANY],
        out_specs=ANY,
        out_shape=jax.ShapeDtypeStruct((2 * rh, w), e.dtype),
        scratch_shapes=[pltpu.SemaphoreType.DMA, pltpu.SemaphoreType.DMA, pltpu.SemaphoreType.DMA],
        compiler_params=pltpu.CompilerParams(has_side_effects=True),
    )(e)


HBM = pl.BlockSpec(memory_space=pltpu.HBM)
SEM = pl.BlockSpec(memory_space=pltpu.SEMAPHORE)
EFFECT = pltpu.SideEffectType.DATAFLOW_SIDE_EFFECTING


def _in_hbm(a):
    return pltpu.with_memory_space_constraint(a, pltpu.HBM)


def _gather_copies(p_ref, land_ref, send_sems, recv_sems):
    rh = p_ref.shape[0] // 2
    x, y, c = _place()
    rows = _half(c, rh)
    copies = []
    for k in range(1, N_CHIPS):
        cx, cy = _rel_chip(x, y, k)
        copies.append(pltpu.make_async_remote_copy(
            src_ref=p_ref.at[rows], dst_ref=land_ref.at[2 * x + y, rows], send_sem=send_sems.at[k - 1], recv_sem=recv_sems.at[k - 1],
            device_id=(cx, cy, c), device_id_type=MESH))
    return copies


def _gather_start(pack, after, *, name):
    r, w = pack.shape

    def body(p_ref, land_ref, after_ref, send_sems, recv_sems, p_thru, land_thru, token):
        for cp in _gather_copies(p_ref, land_ref, send_sems, recv_sems):
            cp.start()
        token[...] = jnp.zeros_like(token)

    return pl.pallas_call(
        body,
        name=name,
        out_shape=(pltpu.SemaphoreType.DMA((N_CHIPS - 1,)), pltpu.SemaphoreType.DMA((N_CHIPS - 1,)), pltpu.HBM((r, w), pack.dtype),
                   pltpu.HBM((N_CHIPS, r, w), pack.dtype), jax.ShapeDtypeStruct((8, 128), F32)),
        in_specs=(HBM, HBM, ANY),
        out_specs=(SEM, SEM, HBM, HBM, pl.BlockSpec(memory_space=pltpu.VMEM)),
        input_output_aliases={0: 2, 1: 3},
        compiler_params=pltpu.CompilerParams(has_side_effects=EFFECT),
    )(_in_hbm(pack), _in_hbm(lax.empty((N_CHIPS, r, w), pack.dtype)), after)


def _gather_wait(send_sems, recv_sems, pack, land, after, *, name):
    def body(p_ref, land_ref, send_sems, recv_sems, after_ref, p_out, land_out):
        for cp in _gather_copies(p_ref, land_ref, send_sems, recv_sems):
            cp.wait_send()
            cp.wait_recv()

    return pl.pallas_call(
        body,
        name=name,
        out_shape=(pltpu.HBM(pack.shape, pack.dtype), pltpu.HBM(land.shape, land.dtype)),
        in_specs=(HBM, HBM, SEM, SEM, ANY),
        out_specs=(HBM, HBM),
        input_output_aliases={0: 0, 1: 1},
        compiler_params=pltpu.CompilerParams(has_side_effects=EFFECT),
    )(pack, land, send_sems, recv_sems, after)


def _gather_spread(land, *, name):
    n, r, w = land.shape
    rh = r // 2

    def body(land_ref, o_ref, send_sems, recv_sems):
        x, y, c = _place()
        rows = _half(c, rh)
        copies = []
        for k in range(1, N_CHIPS):
            cx, cy = _rel_chip(x, y, k)
            copies.append(pltpu.make_async_remote_copy(
                src_ref=land_ref.at[2 * cx + cy, rows], dst_ref=o_ref.at[2 * cx + cy, rows], send_sem=send_sems.at[k - 1],
                recv_sem=recv_sems.at[k - 1], device_id=(x, y, 1 - c), device_id_type=MESH))
        for cp in copies:
            cp.start()
        for cp in copies:
            cp.wait()

    return pl.pallas_call(
        body,
        name=name,
        in_specs=[ANY],
        out_specs=ANY,
        out_shape=jax.ShapeDtypeStruct(land.shape, land.dtype),
        input_output_aliases={0: 0},
        scratch_shapes=[pltpu.SemaphoreType.DMA((N_CHIPS - 1,)), pltpu.SemaphoreType.DMA((N_CHIPS - 1,))],
        compiler_params=pltpu.CompilerParams(has_side_effects=True),
    )(land)


N_PARTS = 2 * (N_CHIPS - 1)


def _scatter_copies(lo_ref, g_ref, land_lo_ref, land_f_ref, send_sems, recv_sems, starting):
    rh = g_ref.shape[1] // 2
    x, y, c = _place()
    copies = []
    for k in range(1, N_CHIPS):
        cx, cy = _rel_chip(x, y, k)
        for i in range(2):
            part = 2 * (k - 1) + (c if starting else i)
            copies.append(pltpu.make_async_remote_copy(
                src_ref=lo_ref.at[2 * cx + cy, pl.ds(i * rh, rh)], dst_ref=land_lo_ref.at[part],
                send_sem=send_sems.at[2 * (k - 1) + i], recv_sem=recv_sems.at[part], device_id=(cx, cy, i), device_id_type=MESH))
    copies.append(pltpu.make_async_remote_copy(
        src_ref=g_ref.at[2 * x + y, _half(1 - c, rh)], dst_ref=land_f_ref, send_sem=send_sems.at[N_PARTS], recv_sem=recv_sems.at[N_PARTS],
        device_id=(x, y, 1 - c), device_id_type=MESH))
    return copies


def _scatter_start(g_lo, g, *, name):
    n, r, w = g.shape
    rh = r // 2

    def body(lo_ref, g_ref, land_lo_ref, land_f_ref, send_sems, recv_sems, lo_thru, g_thru, land_lo_thru, land_f_thru, token):
        for cp in _scatter_copies(lo_ref, g_ref, land_lo_ref, land_f_ref, send_sems, recv_sems, True):
            cp.start()
        token[...] = jnp.zeros_like(token)

    return pl.pallas_call(
        body,
        name=name,
        out_shape=(pltpu.SemaphoreType.DMA((N_PARTS + 1,)), pltpu.SemaphoreType.DMA((N_PARTS + 1,)), pltpu.HBM(g_lo.shape, g_lo.dtype),
                   pltpu.HBM(g.shape, g.dtype), pltpu.HBM((N_PARTS, rh, w), g_lo.dtype), pltpu.HBM((rh, w), g.dtype),
                   jax.ShapeDtypeStruct((8, 128), F32)),
        in_specs=(HBM, HBM, HBM, HBM),
        out_specs=(SEM, SEM, HBM, HBM, HBM, HBM, pl.BlockSpec(memory_space=pltpu.VMEM)),
        input_output_aliases={0: 2, 1: 3, 2: 4, 3: 5},
        compiler_params=pltpu.CompilerParams(has_side_effects=EFFECT),
    )(_in_hbm(g_lo), _in_hbm(g), _in_hbm(lax.empty((N_PARTS, rh, w), g_lo.dtype)), _in_hbm(lax.empty((rh, w), g.dtype)))


def _scatter_wait(send_sems, recv_sems, g_lo, g, land_lo, land_f, after, *, name):
    def body(lo_ref, g_ref, land_lo_ref, land_f_ref, send_sems, recv_sems, after_ref, o0, o1, o2, o3):
        for cp in _scatter_copies(lo_ref, g_ref, land_lo_ref, land_f_ref, send_sems, recv_sems, False):
            cp.wait_send()
            cp.wait_recv()

    arrays = (g_lo, g, land_lo, land_f)
    return pl.pallas_call(
        body,
        name=name,
        out_shape=tuple(pltpu.HBM(a.shape, a.dtype) for a in arrays),
        in_specs=(HBM, HBM, HBM, HBM, SEM, SEM, ANY),
        out_specs=(HBM, HBM, HBM, HBM),
        input_output_aliases={0: 0, 1: 1, 2: 2, 3: 3},
        compiler_params=pltpu.CompilerParams(has_side_effects=EFFECT),
    )(*arrays, send_sems, recv_sems, after)


def _scatter_sum(g, land_lo, land_f, where, *, name):
    n, r, w = g.shape
    rh = r // 2
    tr = _pick(rh, (256, 160, 80))
    nt = rh // tr

    def body(where_ref, g_ref, f_ref, lo_ref, o_ref):
        acc = g_ref[0] + f_ref[...]
        for part in range(N_PARTS):
            acc = acc + lo_ref[part].astype(F32)
        o_ref[...] = acc

    return pl.pallas_call(
        body,
        name=name,
        grid_spec=pltpu.PrefetchScalarGridSpec(
            num_scalar_prefetch=1,
            grid=(nt,),
            in_specs=[pl.BlockSpec((1, tr, w), lambda i, wh: (wh[1], wh[0] * nt + i, 0)),
                      pl.BlockSpec((tr, w), lambda i, wh: (i, 0)),
                      pl.BlockSpec((N_PARTS, tr, w), lambda i, wh: (0, i, 0))],
            out_specs=pl.BlockSpec((tr, w), lambda i, wh: (i, 0)),
        ),
        out_shape=jax.ShapeDtypeStruct((rh, w), F32),
        compiler_params=_cp(("parallel",)),
    )(where, g, land_f, land_lo)


def _swap_all(halves, *, name):
    n = len(halves)

    def body(*refs):
        ins, outs = refs[:n], refs[n : 2 * n]
        send_sems, recv_sems = refs[2 * n :]
        x, y, c = _place()
        copies = [pltpu.make_async_remote_copy(src_ref=e_ref, dst_ref=o_ref, send_sem=send_sems.at[i], recv_sem=recv_sems.at[i],
                                               device_id=(x, y, 1 - c), device_id_type=MESH)
                  for i, (e_ref, o_ref) in enumerate(zip(ins, outs))]
        for cp in copies:
            cp.start()
        for cp in copies:
            cp.wait()

    return pl.pallas_call(
        body,
        name=name,
        in_specs=[ANY] * n,
        out_specs=[ANY] * n,
        out_shape=[jax.ShapeDtypeStruct(e.shape, e.dtype) for e in halves],
        scratch_shapes=[pltpu.SemaphoreType.DMA((n,)), pltpu.SemaphoreType.DMA((n,))],
        compiler_params=pltpu.CompilerParams(has_side_effects=True),
    )(*halves)


def _sum_small(small):
    n_dev = 8

    def body(s_ref, o_ref, all_ref, send_sems, recv_sems):
        x, y, c = _place()
        me = 4 * x + 2 * y + c
        all_ref[me] = s_ref[...]
        copies = []
        for k in range(1, n_dev):
            cx, cy = _rel_chip(x, y, k >> 1)
            cc = 1 - c if k & 1 else c
            copies.append(pltpu.make_async_remote_copy(
                src_ref=s_ref, dst_ref=all_ref.at[me], send_sem=send_sems.at[k - 1], recv_sem=recv_sems.at[k - 1],
                device_id=(cx, cy, cc), device_id_type=MESH))
        for cp in copies:
            cp.start()
        for cp in copies:
            cp.wait()
        acc = all_ref[0]
        for a in range(1, n_dev):
            acc = acc + all_ref[a]
        o_ref[...] = acc

    vm = pl.BlockSpec(memory_space=pltpu.VMEM)
    return pl.pallas_call(
        body,
        name="sum_small",
        in_specs=[vm],
        out_specs=vm,
        out_shape=jax.ShapeDtypeStruct(small.shape, F32),
        scratch_shapes=[pltpu.VMEM((n_dev,) + small.shape, F32), pltpu.SemaphoreType.DMA((n_dev - 1,)), pltpu.SemaphoreType.DMA((n_dev - 1,))],
        compiler_params=pltpu.CompilerParams(has_side_effects=True),
    )(small)


MATS = {"w_in": (776, True), "w_out": (256, False), "w_xq": (256, False), "w_xkv": (512, True), "w_xo": (256, False),
        "w_up": (1024, True), "w_down": (1024, False)}
GATHER_FIRST = ("w_in",)
GATHER_REST = ("w_out", "w_xq", "w_xkv", "w_xo", "w_up", "w_down")
GRAD_GROUPS = (("w_up", "w_down"), ("w_out", "w_xq", "w_xkv", "w_xo"), ("w_in",))


def _group_rows(names):
    n = sum(MATS[name][0] for name in names)
    return n + (-n) % 32


def _pack(pieces, rows):
    p = jnp.concatenate(pieces, axis=0) if len(pieces) > 1 else pieces[0]
    return jnp.pad(p, ((0, rows - p.shape[0]), (0, 0))) if rows > p.shape[0] else p


def _unpack(rows, names):
    out, off = {}, 0
    for name in names:
        out[name] = rows[off : off + MATS[name][0]]
        off += MATS[name][0]
    return out


SMALL = (
    ("mix_norm", 1024), ("conv_norm", 512), ("b_af", 256), ("b_ab", 256), ("gla_norm", 128), ("xa_norm", 1024), ("mem_norm", 1024),
    ("mlp_norm", 1024), ("final_norm", 1024), ("conv_w", 1536), ("w_af", 4096), ("w_ab", 4096), ("loss", 128),
)


def kernel(x, mem, mix_norm, w_in, conv_w, conv_norm, w_af, b_af, w_ab, b_ab, gla_norm, w_out, xa_norm, mem_norm, w_xq, w_xkv, w_xo, mlp_norm, w_up, w_down, final_norm, loss_target, m_mix_norm, m_w_in, m_conv_w, m_conv_norm, m_w_af, m_b_af, m_w_ab, m_b_ab, m_gla_norm, m_w_out, m_xa_norm, m_mem_norm, m_w_xq, m_w_xkv, m_w_xo, m_mlp_norm, m_w_up, m_w_down, m_final_norm, v_mix_norm, v_w_in, v_conv_w, v_conv_norm, v_w_af, v_b_af, v_w_ab, v_b_ab, v_gla_norm, v_w_out, v_xa_norm, v_mem_norm, v_w_xq, v_w_xkv, v_w_xo, v_mlp_norm, v_w_up, v_w_down, v_final_norm):
    given = dict(locals())
    xi, yi, ci = _place()
    chip = 2 * xi + yi
    where = jnp.stack([ci, chip]).astype(jnp.int32)

    lo = {name: (given[name][0].T if MATS[name][1] else given[name][0]).astype(_CD) for name in MATS}
    pack_rest = _pack([lo[name] for name in GATHER_REST], _group_rows(GATHER_REST))
    pack_first = _pack([lo[name] for name in GATHER_FIRST], _group_rows(GATHER_FIRST))
    got_first = _gather_weights(pack_first)
    rest_send, rest_recv, pack_rest, land_rest, rest_token = _gather_start(pack_rest, got_first, name="gather_rest_start")

    def whole(own, got, off, rows):
        return [jnp.where(chip == a, own[off : off + rows], got[a, off : off + rows]) for a in range(N_CHIPS)]

    w_in_t = jnp.concatenate(whole(pack_first, got_first, 0, MATS["w_in"][0]), axis=0)
    w_za = jnp.concatenate([w_in_t[0:1536], w_in_t[2560:3072]], axis=0)
    w_zb = jnp.concatenate([w_in_t[1536:2560], w_in_t[3072:W_IN_COLS], jnp.zeros((ZB_COLS - 1056, D_MODEL), _CD)], axis=0)

    def placed(shard, full_shape, col):
        return lax.dynamic_update_slice(jnp.zeros(full_shape, F32), shard, (0, col)).reshape(-1, 128)

    sw = jnp.concatenate([
        placed(conv_w[0], (CONV_K, CONV_WIDTH), 128 * chip),
        placed(w_af[0], (GLA_LOWRANK, GLA_K_TOTAL), 64 * chip),
        placed(w_ab[0], (GLA_LOWRANK, GLA_K_TOTAL), 64 * chip),
    ], axis=0)
    sw = jnp.pad(sw, ((0, SMALL_ROWS - sw.shape[0]), (0, 0))) * (ci == 0).astype(F32)
    sw = _sum_small(sw)
    conv_w_full = sw[0:12].reshape(CONV_K, CONV_WIDTH)
    w_af_full = sw[12:44].reshape(GLA_LOWRANK, GLA_K_TOTAL)
    w_ab_full = sw[44:76].reshape(GLA_LOWRANK, GLA_K_TOTAL)
    waf_p = jnp.pad(w_af_full, ((0, 128 - GLA_LOWRANK), (0, 0))).astype(_CD)
    wab_p = jnp.pad(w_ab_full, ((GLA_LOWRANK, 128 - 2 * GLA_LOWRANK), (0, 0))).astype(_CD)

    xs, mems, tgt = x[0], mem[0], loss_target[0]
    add_res = lambda acc, res: (acc + res,)
    behind = lambda gain, token: gain + token[0, 0]

    h1 = _rms_fwd(xs, behind(mix_norm, rest_token), name="norm_mix")
    z_b = _mm(h1, w_zb, mode="nt", name="proj_in_b", tn=ZB_COLS)
    z_a = _mm(h1, w_za, mode="nt", name="proj_in_a", tm=512, tn=ZA_COLS)
    b_f, b_b = _gate_fwd(z_b, waf_p, wab_p, b_af, b_ab, name="gates")
    o_f, st_f = _gla_fwd(z_b, b_f, rev=False, name="gla_scan_fwd")
    o_b, st_b = _gla_fwd(z_b, b_b, rev=True, name="gla_scan_rev")
    y = _mix_fwd(z_a, o_f, o_b, conv_w_full, conv_norm, gla_norm, name="mix_out")
    pack_rest, land_rest = _gather_wait(rest_send, rest_recv, pack_rest, land_rest, y, name="gather_rest_wait")
    gathered = _gather_spread(land_rest, name="gather_rest_spread")
    wt, off = {}, 0
    for name in GATHER_REST:
        wt[name] = jnp.concatenate(whole(pack_rest, gathered, off, MATS[name][0]), axis=0)
        off += MATS[name][0]
    x1 = _mm(y, wt["w_out"], mode="nn", name="proj_out", extras=(xs,), epilogue=add_res)
    hx = _rms_fwd(x1, xa_norm, name="norm_xa")
    qx = _mm(hx, wt["w_xq"], mode="nn", name="proj_xq", out_dtypes=(_CD,))
    hmem = _rms_fwd(mems, mem_norm, name="norm_mem")
    kv = _mm(hmem, wt["w_xkv"], mode="nt", name="proj_xkv", out_dtypes=(_CD,))
    ox = _xattn_fwd(qx, kv, name="xattn")
    x2 = _mm(ox, wt["w_xo"], mode="nn", name="proj_xo", extras=(x1,), epilogue=add_res)
    hm = _rms_fwd(x2, mlp_norm, name="norm_mlp")
    act = _mm(hm, wt["w_up"], mode="nt", name="mlp_up", out_dtypes=(_CD,), epilogue=lambda acc: (jnp.square(jnp.maximum(acc, 0.0)),))
    x3 = _mm(act, wt["w_down"], mode="nn", name="mlp_down", extras=(x2,), epilogue=add_res, tm=512, tk=D_FF)
    dx3, dx3_lo, loss_part, g_final_norm = _final_loss(x3, final_norm.reshape(1, D_MODEL), tgt, name="loss_head")

    grads_t = {}

    def start_group(names, tag):
        rows = _group_rows(names)
        g = jnp.stack([_pack([grads_t[name][a * MATS[name][0] : (a + 1) * MATS[name][0]] for name in names], rows) for a in range(N_CHIPS)])
        return _scatter_start(g.astype(_TD), g, name="grads_" + tag + "_start")

    def finish_group(state, after, tag):
        send_sems, recv_sems, g_lo, g, land_lo, land_f, _ = state
        g_lo, g, land_lo, land_f = _scatter_wait(send_sems, recv_sems, g_lo, g, land_lo, land_f, after, name="grads_" + tag + "_wait")
        return _scatter_sum(g, land_lo, land_f, where, name="grads_" + tag + "_sum")

    du = _mm(dx3_lo, wt["w_down"], mode="nt", name="mlp_down_dx", out_dtypes=(_CD,), extras=(act,),
             epilogue=lambda acc, aa: (acc * (2.0 * jnp.sqrt(aa.astype(F32))),))
    grads_t["w_down"] = _mm_tn(act, dx3_lo, name="mlp_down_dw")
    grads_t["w_up"] = _mm_tn(du, hm, name="mlp_up_dw")
    mlp_state = start_group(GRAD_GROUPS[0], "mlp")
    dhm = _mm(du, wt["w_up"], mode="nn", name="mlp_up_dx", tm=512, tk=D_FF)
    dx2, dx2_lo, g_mlp_norm = _rms_bwd(x2, behind(mlp_norm, mlp_state[-1]), dhm, dx3, name="norm_mlp_bwd")
    dox = _mm(dx2_lo, wt["w_xo"], mode="nt", name="proj_xo_dx", out_dtypes=(_CD,))
    grads_t["w_xo"] = _mm_tn(ox, dx2_lo, name="proj_xo_dw")
    dqx, dkv = _xattn_bwd(qx, kv, dox, name="xattn_bwd")
    grads_t["w_xq"] = _mm_tn(hx, dqx, name="proj_xq_dw")
    dhx = _mm(dqx, wt["w_xq"], mode="nt", name="proj_xq_dx")
    dx1, dx1_lo, g_xa_norm = _rms_bwd(x1, xa_norm, dhx, dx2, name="norm_xa_bwd")
    dkv_lo = dkv.astype(_CD)
    grads_t["w_xkv"] = _mm_tn(dkv_lo, hmem, name="proj_xkv_dw")
    dhmem = _mm(dkv_lo, wt["w_xkv"], mode="nn", name="proj_xkv_dx")
    (g_mem_norm,) = _rms_bwd(mems, mem_norm, dhmem, name="norm_mem_bwd", want_dx=False, want_lo=False)
    dy = _mm(dx1_lo, wt["w_out"], mode="nt", name="proj_out_dx")
    grads_t["w_out"] = _mm_tn(y, dx1_lo, name="proj_out_dw")
    attn_state = start_group(GRAD_GROUPS[1], "attn")
    dz_a, do, g_conv_w, g_conv_norm, g_gla_norm = _mix_bwd(z_a, o_f, o_b, dy, conv_w_full, behind(conv_norm, attn_state[-1]), gla_norm, name="mix_out_bwd")
    dqkv_f, db_f = _gla_bwd(z_b, b_f, do, st_f, rev=False, name="gla_scan_fwd_bwd")
    dqkv_b, db_b = _gla_bwd(z_b, b_b, do, st_b, rev=True, name="gla_scan_rev_bwd")
    dz_b, g_waf_p, g_wab_p, g_b_af, g_b_ab = _gate_bwd(z_b, waf_p, wab_p, b_af, b_ab, db_f, db_b, dqkv_f, dqkv_b, name="gates_bwd")
    g_za = _mm_tn(dz_a, h1, name="proj_in_a_dw")
    g_zb = _mm_tn(dz_b, h1, name="proj_in_b_dw")
    grads_t["w_in"] = jnp.concatenate([g_za[0:1536], g_zb[0:1024], g_za[1536:2048], g_zb[1024:1056]], axis=0)
    in_state = start_group(GRAD_GROUPS[2], "in")
    dh1 = _mm(dz_a, w_za, mode="nn", name="proj_in_a_dx", tm=512, tk=ZA_COLS)
    dh1 = _mm(dz_b, w_zb, mode="nn", name="proj_in_b_dx", tm=512, tk=ZB_COLS, extras=(dh1,), epilogue=add_res)
    grad_x, g_mix_norm = _rms_bwd(xs, behind(mix_norm, in_state[-1]), dh1, dx1, name="norm_mix_bwd", want_lo=False)

    half_mlp = finish_group(mlp_state, grad_x, "mlp")
    half_attn = finish_group(attn_state, half_mlp, "attn")
    half_in = finish_group(in_state, half_attn, "in")
    g_shard = {}
    mine = [half_mlp, half_attn, half_in]
    for names, own, got in zip(GRAD_GROUPS, mine, _swap_all(mine, name="shards_to_sibling")):
        rows = jnp.where(ci == 0, jnp.concatenate([own, got], axis=0), jnp.concatenate([got, own], axis=0))
        for name, g in _unpack(rows, names).items():
            g_shard[name] = g.T if MATS[name][1] else g

    small_vals = dict(mix_norm=g_mix_norm, conv_norm=g_conv_norm, b_af=g_b_af, b_ab=g_b_ab, gla_norm=g_gla_norm, xa_norm=g_xa_norm,
                      mem_norm=g_mem_norm, mlp_norm=g_mlp_norm, final_norm=g_final_norm, conv_w=g_conv_w,
                      w_af=g_waf_p[0:GLA_LOWRANK], w_ab=g_wab_p[GLA_LOWRANK : 2 * GLA_LOWRANK], loss=loss_part)
    small = jnp.concatenate([small_vals[name].reshape(-1, 128) for name, _ in SMALL], axis=0)
    small = _sum_small(jnp.pad(small, ((0, SMALL_ROWS - small.shape[0]), (0, 0))))
    g_small, off = {}, 0
    for name, n in SMALL:
        g_small[name] = small[off : off + n // 128]
        off += n // 128
    loss = g_small["loss"][0, 0]
    g_small["conv_w"] = lax.dynamic_slice(g_small["conv_w"].reshape(CONV_K, CONV_WIDTH), (0, 128 * chip), (CONV_K, 128))
    g_small["w_af"] = lax.dynamic_slice(g_small["w_af"].reshape(GLA_LOWRANK, GLA_K_TOTAL), (0, 64 * chip), (GLA_LOWRANK, 64))
    g_small["w_ab"] = lax.dynamic_slice(g_small["w_ab"].reshape(GLA_LOWRANK, GLA_K_TOTAL), (0, 64 * chip), (GLA_LOWRANK, 64))

    names = ["mix_norm", "w_in", "conv_w", "conv_norm", "w_af", "b_af", "w_ab", "b_ab", "gla_norm", "w_out", "xa_norm", "mem_norm",
             "w_xq", "w_xkv", "w_xo", "mlp_norm", "w_up", "w_down", "final_norm"]
    big_names = list(MATS)
    as2d = lambda a: a.reshape(1, -1) if a.ndim == 1 else a.reshape(a.shape[-2:])
    grads, deltas, new_m, new_v = {}, {}, {}, {}
    for name in big_names:
        grads[name] = g_shard[name]
        deltas[name], new_m[name], new_v[name] = _adamw(as2d(given[name]), g_shard[name], as2d(given["m_" + name]),
                                                         as2d(given["v_" + name]), name="adamw_" + name)
    small_names = [name for name in names if name not in big_names]
    groups = []
    for name in small_names:
        grads[name] = g_small[name].reshape(as2d(given[name]).shape)
        groups.append((as2d(given[name]), grads[name], as2d(given["m_" + name]), as2d(given["v_" + name])))
    for name, res in zip(small_names, _adamw_small(groups, name="adamw_small")):
        deltas[name], new_m[name], new_v[name] = res

    like = lambda name, a: a.reshape(given[name].shape)
    return (loss, grad_x[None], *[like(n, grads[n]) for n in names], *[like(n, deltas[n]) for n in names],
            *[like(n, new_m[n]) for n in names], *[like(n, new_v[n]) for n in names])
```

```python
import functools

import jax
import jax.numpy as jnp
from jax import lax
from jax.experimental import pallas as pl
from jax.experimental.pallas import tpu as pltpu

F32 = jnp.float32
BF16 = jnp.bfloat16
_CD = jnp.bfloat16
_TD = jnp.bfloat16

D_MODEL = 1024
N_MEM = 256
CONV_WIDTH = 512
CONV_GROUP = 64
CONV_K = 3
GLA_HEADS = 4
GLA_DK = 64
GLA_DV = 128
GLA_K_TOTAL = 256
GLA_V_TOTAL = 512
GLA_LOWRANK = 16
GLA_GATE_SCALE = 1.0 / 16.0
GLA_CHUNK = 64
XA_HEADS = 4
XA_HEAD_DIM = 256
D_FF = 4096
EPS = 1e-6
W_IN_COLS = 3104
ZA_COLS = 2048
ZB_COLS = 1152
LR_COL = 1024

ADAM_LR = 0.001
ADAM_B1 = 0.9
ADAM_B2 = 0.999
ADAM_EPS = 1e-08
ADAM_WD = 0.01
ADAM_STEP = 10

N_CHIPS = 4
PACK_W = 1024
PACK_ROWS = 4160
PACK_TILE = 160
SMALL_ROWS = 128

_TS = 512
_VMEM = 44 * 1024 * 1024
MESH = pl.DeviceIdType.MESH
ANY = pl.BlockSpec(memory_space=pl.ANY)


def _cp(sem=None, **kw):
    return pltpu.CompilerParams(dimension_semantics=sem, vmem_limit_bytes=_VMEM, **kw)


def _dot(a, b):
    return jnp.dot(a.astype(_CD), b.astype(_CD), preferred_element_type=F32)


def _dot_nt(a, b):
    return lax.dot_general(a.astype(_CD), b.astype(_CD), (((1,), (1,)), ((), ())), preferred_element_type=F32)


def _dot_tn(a, b):
    return lax.dot_general(a.astype(_CD), b.astype(_CD), (((0,), (0,)), ((), ())), preferred_element_type=F32)


def _dot_split(x, ones):
    hi = x.astype(BF16)
    r = x - hi.astype(F32)
    mid = r.astype(BF16)
    lo = (r - mid.astype(F32)).astype(BF16)
    d = lambda p: jnp.dot(p, ones, preferred_element_type=F32)
    return d(hi) + d(mid) + d(lo)


def _pick(n, cands=(1024, 640, 512, 256, 128)):
    for t in cands:
        if n % t == 0:
            return t
    return n


def _rows(s):
    return min(_TS, s)


def _sigmoid(v):
    e = jnp.exp(-jnp.abs(v))
    return jnp.where(v >= 0, 1.0 / (1.0 + e), e / (1.0 + e))


def _mm(a, b, *, mode, name, out_dtypes=(F32,), extras=(), epilogue=None, tm=None, tn=None, tk=None):
    m, k = a.shape
    n = b.shape[1] if mode == "nn" else b.shape[0]
    tm = min(m, tm or 1024)
    tn = tn or _pick(n)
    tk = tk or _pick(k)
    nk = k // tk
    n_ex, n_out = len(extras), len(out_dtypes)

    def body(*refs):
        a_ref, b_ref = refs[:2]
        ex = refs[2 : 2 + n_ex]
        outs = refs[2 + n_ex : 2 + n_ex + n_out]
        part = _dot(a_ref[...], b_ref[...]) if mode == "nn" else _dot_nt(a_ref[...], b_ref[...])

        def finish(acc):
            res = epilogue(acc, *[e[...] for e in ex]) if epilogue else (acc,)
            for o, r in zip(outs, res):
                o[...] = r.astype(o.dtype)

        if nk == 1:
            finish(part)
        else:
            acc_ref = refs[-1]
            kk = pl.program_id(2)

            @pl.when(kk == 0)
            def _():
                acc_ref[...] = part

            @pl.when(kk > 0)
            def _():
                acc_ref[...] += part

            @pl.when(kk == nk - 1)
            def _():
                finish(acc_ref[...])

    b_spec = pl.BlockSpec((tk, tn), lambda i, j, kk: (kk, j)) if mode == "nn" else pl.BlockSpec((tn, tk), lambda i, j, kk: (j, kk))
    tile = pl.BlockSpec((tm, tn), lambda i, j, kk: (i, j))
    out = pl.pallas_call(
        body,
        name=name,
        grid=(m // tm, n // tn, nk),
        in_specs=[pl.BlockSpec((tm, tk), lambda i, j, kk: (i, kk)), b_spec] + [tile] * n_ex,
        out_specs=[tile] * n_out,
        out_shape=[jax.ShapeDtypeStruct((m, n), dt) for dt in out_dtypes],
        scratch_shapes=[pltpu.VMEM((tm, tn), F32)] if nk > 1 else [],
        compiler_params=_cp(("parallel", "parallel", "arbitrary")),
    )(a, b, *extras)
    return out[0] if n_out == 1 else out


def _mm_tn(a, b, *, name):
    s, m = a.shape
    n = b.shape[1]
    cap = max(128, (1 << 20) // n)
    tm = _pick(m, tuple(t for t in (512, 640, 384, 256, 128) if t <= max(cap, 128)))
    ts = min(s, 1 << (((1 << 22) // n).bit_length() - 1))
    ns = s // ts

    def body(a_ref, b_ref, o_ref):
        part = _dot_tn(a_ref[...], b_ref[...])
        if ns == 1:
            o_ref[...] = part
        else:
            ss = pl.program_id(1)

            @pl.when(ss == 0)
            def _():
                o_ref[...] = part

            @pl.when(ss > 0)
            def _():
                o_ref[...] += part

    return pl.pallas_call(
        body,
        name=name,
        grid=(m // tm, ns),
        in_specs=[pl.BlockSpec((ts, tm), lambda i, ss: (ss, i)), pl.BlockSpec((ts, n), lambda i, ss: (ss, 0))],
        out_specs=pl.BlockSpec((tm, n), lambda i, ss: (i, 0)),
        out_shape=jax.ShapeDtypeStruct((m, n), F32),
        compiler_params=_cp(("parallel", "arbitrary")),
    )(a, b)


def _mm_rows(a, b, *, mode, name, rows=(), vecs=(), out_rows=(), out_vecs=(), epilogue, tm=512):
    m, k = a.shape
    n = b.shape[1] if mode == "nn" else b.shape[0]
    tm = min(m, tm)
    n_r, n_v, n_or, n_ov = len(rows), len(vecs), len(out_rows), len(out_vecs)

    def body(*refs):
        a_ref, b_ref = refs[:2]
        r_refs = refs[2 : 2 + n_r]
        v_refs = refs[2 + n_r : 2 + n_r + n_v]
        or_refs = refs[2 + n_r + n_v : 2 + n_r + n_v + n_or]
        ov_refs = refs[2 + n_r + n_v + n_or :]
        acc = _dot(a_ref[...], b_ref[...]) if mode == "nn" else _dot_nt(a_ref[...], b_ref[...])
        res_rows, res_vecs = epilogue(acc, [r[...] for r in r_refs], [v[...] for v in v_refs])
        for o, r in zip(or_refs, res_rows):
            o[...] = r.astype(o.dtype)
        if n_ov:
            first = pl.program_id(0) == 0

            @pl.when(first)
            def _():
                for o, r in zip(ov_refs, res_vecs):
                    o[...] = r

            @pl.when(jnp.logical_not(first))
            def _():
                for o, r in zip(ov_refs, res_vecs):
                    o[...] += r

    tile = pl.BlockSpec((tm, n), lambda i: (i, 0))
    whole = lambda arr: pl.BlockSpec(arr.shape, lambda i: (0, 0))
    vec = lambda w: pl.BlockSpec((1, w), lambda i: (0, 0))
    out = pl.pallas_call(
        body,
        name=name,
        grid=(m // tm,),
        in_specs=[pl.BlockSpec((tm, k), lambda i: (i, 0)), whole(b)] + [tile] * n_r + [vec(v.shape[1]) for v in vecs],
        out_specs=[tile] * n_or + [vec(w) for w in out_vecs],
        out_shape=[jax.ShapeDtypeStruct((m, n), dt) for dt in out_rows] + [jax.ShapeDtypeStruct((1, w), F32) for w in out_vecs],
        compiler_params=_cp(("arbitrary",) if n_ov else ("parallel",)),
    )(a, b, *rows, *vecs)
    return out


def _ep_residual_norm(acc, rows, vecs):
    x = acc + rows[0]
    r = lax.rsqrt(jnp.mean(x * x, axis=-1, keepdims=True) + EPS)
    return [x, x * r * vecs[0]], []


def _ep_norm_bwd(acc, rows, vecs):
    dy = acc
    for extra in rows[2:]:
        dy = dy + extra
    x, dres = rows[0], rows[1]
    r = lax.rsqrt(jnp.mean(x * x, axis=-1, keepdims=True) + EPS)
    xh = x * r
    dxh = dy * vecs[0]
    dx = r * (dxh - xh * jnp.mean(dxh * xh, axis=-1, keepdims=True)) + dres
    return [dx, dx], [jnp.sum(dy * xh, axis=0, keepdims=True)]


def _ep_loss(acc, rows, vecs):
    x = acc + rows[0]
    d = x.shape[-1]
    r = lax.rsqrt(jnp.mean(x * x, axis=-1, keepdims=True) + EPS)
    xh = x * r
    err = xh * vecs[0] - rows[1]
    loss = jnp.zeros((1, 128), F32) + 0.5 * jnp.sum(jnp.mean(err * err, axis=-1, keepdims=True))
    dy = err * (1.0 / d)
    dxh = dy * vecs[0]
    dx = r * (dxh - xh * jnp.mean(dxh * xh, axis=-1, keepdims=True))
    return [dx, dx], [loss, jnp.sum(dy * xh, axis=0, keepdims=True)]


def _rms_fwd(x, g, *, name):
    s, d = x.shape
    ts = _rows(s)

    def body(x_ref, g_ref, o_ref):
        xf = x_ref[...]
        r = lax.rsqrt(jnp.mean(xf * xf, axis=-1, keepdims=True) + EPS)
        o_ref[...] = (xf * r * g_ref[...]).astype(o_ref.dtype)

    return pl.pallas_call(
        body,
        name=name,
        grid=(s // ts,),
        in_specs=[pl.BlockSpec((ts, d), lambda i: (i, 0)), pl.BlockSpec((1, d), lambda i: (0, 0))],
        out_specs=pl.BlockSpec((ts, d), lambda i: (i, 0)),
        out_shape=jax.ShapeDtypeStruct((s, d), _CD),
        compiler_params=_cp(("parallel",)),
    )(x, g)


def _rms_bwd(x, g, dy, dres=None, *, name, want_dx=True, want_lo=True):
    s, d = x.shape
    ts = _rows(s)
    has_res = dres is not None

    def body(*refs):
        x_ref, g_ref, dy_ref = refs[:3]
        pos = 3
        dres_ref = refs[pos] if has_res else None
        pos += has_res
        dx_ref = refs[pos] if want_dx else None
        pos += want_dx
        lo_ref = refs[pos] if want_lo else None
        pos += want_lo
        dg_ref = refs[pos]
        xf = x_ref[...]
        r = lax.rsqrt(jnp.mean(xf * xf, axis=-1, keepdims=True) + EPS)
        xh = xf * r
        dyf = dy_ref[...]
        part = jnp.sum(dyf * xh, axis=0, keepdims=True)

        @pl.when(pl.program_id(0) == 0)
        def _():
            dg_ref[...] = part

        @pl.when(pl.program_id(0) > 0)
        def _():
            dg_ref[...] += part

        if want_dx or want_lo:
            dxh = dyf * g_ref[...]
            dx = r * (dxh - xh * jnp.mean(dxh * xh, axis=-1, keepdims=True))
            if has_res:
                dx = dx + dres_ref[...]
            if want_dx:
                dx_ref[...] = dx
            if want_lo:
                lo_ref[...] = dx.astype(lo_ref.dtype)

    tile = pl.BlockSpec((ts, d), lambda i: (i, 0))
    vec = pl.BlockSpec((1, d), lambda i: (0, 0))
    out_specs, out_shape = [], []
    if want_dx:
        out_specs.append(tile)
        out_shape.append(jax.ShapeDtypeStruct((s, d), F32))
    if want_lo:
        out_specs.append(tile)
        out_shape.append(jax.ShapeDtypeStruct((s, d), _CD))
    out_specs.append(vec)
    out_shape.append(jax.ShapeDtypeStruct((1, d), F32))
    return pl.pallas_call(
        body,
        name=name,
        grid=(s // ts,),
        in_specs=[tile, vec, tile] + ([tile] if has_res else []),
        out_specs=out_specs,
        out_shape=out_shape,
        compiler_params=_cp(("arbitrary",)),
    )(x, g, dy, *([dres] if has_res else []))


def _final_loss(x3, g, tgt, *, name):
    s, d = x3.shape
    ts = _rows(s)

    def body(x_ref, g_ref, t_ref, dx_ref, lo_ref, loss_ref, dg_ref):
        xf = x_ref[...]
        r = lax.rsqrt(jnp.mean(xf * xf, axis=-1, keepdims=True) + EPS)
        xh = xf * r
        gg = g_ref[...]
        err = xh * gg - t_ref[...]
        lpart = jnp.zeros((1, 128), F32) + 0.5 * jnp.sum(jnp.mean(err * err, axis=-1, keepdims=True))
        dy = err * (1.0 / d)
        gpart = jnp.sum(dy * xh, axis=0, keepdims=True)

        @pl.when(pl.program_id(0) == 0)
        def _():
            loss_ref[...] = lpart
            dg_ref[...] = gpart

        @pl.when(pl.program_id(0) > 0)
        def _():
            loss_ref[...] += lpart
            dg_ref[...] += gpart

        dxh = dy * gg
        dx = r * (dxh - xh * jnp.mean(dxh * xh, axis=-1, keepdims=True))
        dx_ref[...] = dx
        lo_ref[...] = dx.astype(lo_ref.dtype)

    tile = pl.BlockSpec((ts, d), lambda i: (i, 0))
    vec = pl.BlockSpec((1, d), lambda i: (0, 0))
    return pl.pallas_call(
        body,
        name=name,
        grid=(s // ts,),
        in_specs=[tile, vec, tile],
        out_specs=[tile, tile, pl.BlockSpec((1, 128), lambda i: (0, 0)), vec],
        out_shape=[
            jax.ShapeDtypeStruct((s, d), F32),
            jax.ShapeDtypeStruct((s, d), _CD),
            jax.ShapeDtypeStruct((1, 128), F32),
            jax.ShapeDtypeStruct((1, d), F32),
        ],
        compiler_params=_cp(("arbitrary",)),
    )(x3, g, tgt)


def _chunk_scan(v, row_in_chunk, suffix):
    t = v.shape[0]
    step = 1
    while step < GLA_CHUNK:
        if suffix:
            v = v + jnp.where(row_in_chunk < GLA_CHUNK - step, pltpu.roll(v, t - step, 0), 0.0)
        else:
            v = v + jnp.where(row_in_chunk >= step, pltpu.roll(v, step, 0), 0.0)
        step *= 2
    return v


def _gate_pre(lr, w_ref, b_ref):
    return _dot(lr, w_ref[...]) + b_ref[...]


def _gate_fwd(z, waf, wab, baf, bab, *, name):
    s = z.shape[0]
    ts = _rows(s)

    def body(lr_ref, waf_ref, wab_ref, baf_ref, bab_ref, bf_ref, bb_ref):
        lr = lr_ref[...]
        ric = lax.broadcasted_iota(jnp.int32, (ts, GLA_K_TOTAL), 0) & (GLA_CHUNK - 1)
        for w_ref, b_ref, o_ref, suffix in ((waf_ref, baf_ref, bf_ref, False), (wab_ref, bab_ref, bb_ref, True)):
            pre = _gate_pre(lr, w_ref, b_ref)
            la = (jnp.minimum(pre, 0.0) - jnp.log(1.0 + jnp.exp(-jnp.abs(pre)))) * GLA_GATE_SCALE
            o_ref[...] = _chunk_scan(la, ric, suffix)

    wspec = pl.BlockSpec((128, GLA_K_TOTAL), lambda i: (0, 0))
    bspec = pl.BlockSpec((1, GLA_K_TOTAL), lambda i: (0, 0))
    tile = pl.BlockSpec((ts, GLA_K_TOTAL), lambda i: (i, 0))
    return pl.pallas_call(
        body,
        name=name,
        grid=(s // ts,),
        in_specs=[pl.BlockSpec((ts, 128), lambda i: (i, LR_COL // 128)), wspec, wspec, bspec, bspec],
        out_specs=[tile, tile],
        out_shape=[jax.ShapeDtypeStruct((s, GLA_K_TOTAL), F32)] * 2,
        compiler_params=_cp(("parallel",)),
    )(z, waf, wab, baf, bab)


def _gate_bwd(z, waf, wab, baf, bab, dbf, dbb, dqkv_f, dqkv_b, *, name):
    s = z.shape[0]
    ts = _rows(s)

    def body(lr_ref, waf_ref, wab_ref, baf_ref, bab_ref, dbf_ref, dbb_ref, gf_ref, gb_ref, dzb_ref, dwf_ref, dwb_ref, dbaf_ref, dbab_ref):
        lr = lr_ref[...]
        ric = lax.broadcasted_iota(jnp.int32, (ts, GLA_K_TOTAL), 0) & (GLA_CHUNK - 1)
        first = pl.program_id(0) == 0
        dlr = None
        for w_ref, b_ref, db_ref, dw_ref, dbias_ref, suffix in (
            (waf_ref, baf_ref, dbf_ref, dwf_ref, dbaf_ref, True),
            (wab_ref, bab_ref, dbb_ref, dwb_ref, dbab_ref, False),
        ):
            pre = _gate_pre(lr, w_ref, b_ref)
            dla = _chunk_scan(db_ref[...], ric, suffix)
            dpre = dla * GLA_GATE_SCALE * _sigmoid(-pre)
            part = _dot_nt(dpre, w_ref[...])
            dlr = part if dlr is None else dlr + part
            dw = _dot_tn(lr, dpre)
            dbias = jnp.sum(dpre, axis=0, keepdims=True)

            @pl.when(first)
            def _():
                dw_ref[...] = dw
                dbias_ref[...] = dbias

            @pl.when(jnp.logical_not(first))
            def _():
                dw_ref[...] += dw
                dbias_ref[...] += dbias

        dzb_ref[...] = jnp.concatenate([gf_ref[...] + gb_ref[...], dlr], axis=1).astype(dzb_ref.dtype)

    wspec = pl.BlockSpec((128, GLA_K_TOTAL), lambda i: (0, 0))
    bspec = pl.BlockSpec((1, GLA_K_TOTAL), lambda i: (0, 0))
    tile = pl.BlockSpec((ts, GLA_K_TOTAL), lambda i: (i, 0))
    wide = pl.BlockSpec((ts, 2 * GLA_K_TOTAL + GLA_V_TOTAL), lambda i: (i, 0))
    return pl.pallas_call(
        body,
        name=name,
        grid=(s // ts,),
        in_specs=[pl.BlockSpec((ts, 128), lambda i: (i, LR_COL // 128)), wspec, wspec, bspec, bspec, tile, tile, wide, wide],
        out_specs=[pl.BlockSpec((ts, ZB_COLS), lambda i: (i, 0)), wspec, wspec, bspec, bspec],
        out_shape=[
            jax.ShapeDtypeStruct((s, ZB_COLS), _CD),
            jax.ShapeDtypeStruct((128, GLA_K_TOTAL), F32),
            jax.ShapeDtypeStruct((128, GLA_K_TOTAL), F32),
            jax.ShapeDtypeStruct((1, GLA_K_TOTAL), F32),
            jax.ShapeDtypeStruct((1, GLA_K_TOTAL), F32),
        ],
        compiler_params=_cp(("arbitrary",)),
    )(z, waf, wab, baf, bab, dbf, dbb, dqkv_f, dqkv_b)


def _gla_masks(rev):
    lane_head = lax.broadcasted_iota(jnp.int32, (1, GLA_K_TOTAL), 1) >> 6
    head_masks = [lane_head == h for h in range(GLA_HEADS)]
    st_rows = lax.broadcasted_iota(jnp.int32, (GLA_V_TOTAL, GLA_K_TOTAL), 0) >> 7
    st_lanes = lax.broadcasted_iota(jnp.int32, (GLA_V_TOTAL, GLA_K_TOTAL), 1) >> 6
    block_mask = st_rows == st_lanes
    t = lax.broadcasted_iota(jnp.int32, (GLA_HEADS * GLA_CHUNK, GLA_CHUNK), 0) & (GLA_CHUNK - 1)
    u = lax.broadcasted_iota(jnp.int32, (GLA_HEADS * GLA_CHUNK, GLA_CHUNK), 1)
    tri = (u > t) if rev else (u <= t)
    row = lax.broadcasted_iota(jnp.int32, (GLA_CHUNK, GLA_K_TOTAL), 0)
    total_row = row == (0 if rev else GLA_CHUNK - 1)
    return head_masks, block_mask, tri, total_row


def _gla_chunk_terms(q_ref, k_ref, v_ref, b_ref, rows, head_masks, tri, total_row):
    q = q_ref[rows, :] * (GLA_DK**-0.5)
    k = k_ref[rows, :]
    v = v_ref[rows, :]
    b = b_ref[rows, :]
    eb = jnp.exp(b)
    enb = jnp.exp(-b)
    g = jnp.sum(jnp.where(total_row, b, 0.0), axis=0, keepdims=True)
    egb = jnp.exp(g - b)
    qt = q * eb
    kt = k * enb
    kh = k * egb
    q_heads = jnp.concatenate([jnp.where(m, qt, 0.0) for m in head_masks], axis=0)
    attn = jnp.where(tri, _dot_nt(q_heads, kt), 0.0)
    return v, eb, enb, egb, jnp.exp(g), qt, kt, kh, attn


def _gla_specs(s, tb, rev_blocks):
    nb = s // tb
    rb = (lambda i: nb - 1 - i) if rev_blocks else (lambda i: i)
    q_spec = pl.BlockSpec((tb, GLA_K_TOTAL), lambda i: (rb(i), 0))
    k_spec = pl.BlockSpec((tb, GLA_K_TOTAL), lambda i: (rb(i), 1))
    v_spec = pl.BlockSpec((tb, GLA_V_TOTAL), lambda i: (rb(i), 1))
    b_spec = pl.BlockSpec((tb, GLA_K_TOTAL), lambda i: (rb(i), 0))
    o_spec = pl.BlockSpec((tb, GLA_V_TOTAL), lambda i: (rb(i), 0))
    st_spec = pl.BlockSpec((tb // GLA_CHUNK, GLA_DV, GLA_K_TOTAL), lambda i: (rb(i), 0, 0))
    return nb, q_spec, k_spec, v_spec, b_spec, o_spec, st_spec


def _gla_fwd(z, b, *, rev, name):
    s = z.shape[0]
    tb = _rows(s)
    cpb = tb // GLA_CHUNK
    nb, q_spec, k_spec, v_spec, b_spec, o_spec, st_spec = _gla_specs(s, tb, rev)

    def body(q_ref, k_ref, v_ref, b_ref, o_ref, sv_ref, st_ref):
        head_masks, block_mask, tri, total_row = _gla_masks(rev)

        @pl.when(pl.program_id(0) == 0)
        def _():
            st_ref[...] = jnp.zeros_like(st_ref)

        def chunk(ci, carry):
            cidx = cpb - 1 - ci if rev else ci
            rows = pl.ds(pl.multiple_of(cidx * GLA_CHUNK, GLA_CHUNK), GLA_CHUNK)
            v, _, _, _, eg, qt, _, kh, attn = _gla_chunk_terms(q_ref, k_ref, v_ref, b_ref, rows, head_masks, tri, total_row)
            o = jnp.concatenate(
                [_dot(attn[GLA_CHUNK * h : GLA_CHUNK * (h + 1)], v[:, GLA_DV * h : GLA_DV * (h + 1)]) for h in range(GLA_HEADS)], axis=1
            )
            st = st_ref[...]
            o_ref[rows, :] = o + _dot_nt(qt, st)
            sv_ref[cidx] = st[0:128] + st[128:256] + st[256:384] + st[384:512]
            st_ref[...] = st * eg + jnp.where(block_mask, _dot_tn(v, kh), 0.0)
            return carry

        lax.fori_loop(0, cpb, chunk, 0)

    return pl.pallas_call(
        body,
        name=name,
        grid=(nb,),
        in_specs=[q_spec, k_spec, v_spec, b_spec],
        out_specs=[o_spec, st_spec],
        out_shape=[jax.ShapeDtypeStruct((s, GLA_V_TOTAL), F32), jax.ShapeDtypeStruct((s // GLA_CHUNK, GLA_DV, GLA_K_TOTAL), F32)],
        scratch_shapes=[pltpu.VMEM((GLA_V_TOTAL, GLA_K_TOTAL), F32)],
        compiler_params=_cp(("arbitrary",)),
    )(z, z, z, b)


def _gla_bwd(z, b, do, states, *, rev, name):
    s = z.shape[0]
    tb = _rows(s)
    cpb = tb // GLA_CHUNK
    nb, q_spec, k_spec, v_spec, b_spec, o_spec, st_spec = _gla_specs(s, tb, not rev)
    rb = (lambda i: nb - 1 - i) if not rev else (lambda i: i)

    def body(q_ref, k_ref, v_ref, b_ref, do_ref, sv_ref, dqkv_ref, db_ref, dst_ref):
        head_masks, block_mask, tri, total_row = _gla_masks(rev)

        @pl.when(pl.program_id(0) == 0)
        def _():
            dst_ref[...] = jnp.zeros_like(dst_ref)

        def chunk(ci, carry):
            cidx = ci if rev else cpb - 1 - ci
            rows = pl.ds(pl.multiple_of(cidx * GLA_CHUNK, GLA_CHUNK), GLA_CHUNK)
            v, eb, enb, egb, eg, qt, kt, kh, attn = _gla_chunk_terms(q_ref, k_ref, v_ref, b_ref, rows, head_masks, tri, total_row)
            do_c = do_ref[rows, :]
            saved = sv_ref[cidx]
            st = jnp.where(block_mask, jnp.concatenate([saved] * GLA_HEADS, axis=0), 0.0)
            dst = dst_ref[...]
            hs = lambda a, h: a[GLA_CHUNK * h : GLA_CHUNK * (h + 1)]
            vs = lambda a, h: a[:, GLA_DV * h : GLA_DV * (h + 1)]
            dattn = jnp.concatenate([_dot_nt(vs(do_c, h), vs(v, h)) for h in range(GLA_HEADS)], axis=0)
            dattn = jnp.where(tri, dattn, 0.0)
            dv = jnp.concatenate([_dot_tn(hs(attn, h), vs(do_c, h)) for h in range(GLA_HEADS)], axis=1) + _dot_nt(kh, dst)
            dqt = _dot(do_c, st)
            dkt = jnp.zeros_like(dqt)
            for h in range(GLA_HEADS):
                dqt = dqt + jnp.where(head_masks[h], _dot(hs(dattn, h), kt), 0.0)
                dkt = dkt + jnp.where(head_masks[h], _dot_tn(hs(dattn, h), qt), 0.0)
            dkh = _dot(v, dst)
            dg = jnp.sum(dkh * kh, axis=0, keepdims=True) + jnp.sum(dst * st, axis=0, keepdims=True) * eg
            db = dqt * qt - dkt * kt - dkh * kh + jnp.where(total_row, dg, 0.0)
            dq = dqt * eb * (GLA_DK**-0.5)
            dk = dkt * enb + dkh * egb
            dqkv_ref[rows, :] = jnp.concatenate([dq, dk, dv], axis=1)
            db_ref[rows, :] = db
            dst_ref[...] = dst * eg + jnp.where(block_mask, _dot_tn(do_c, qt), 0.0)
            return carry

        lax.fori_loop(0, cpb, chunk, 0)

    wide = 2 * GLA_K_TOTAL + GLA_V_TOTAL
    return pl.pallas_call(
        body,
        name=name,
        grid=(nb,),
        in_specs=[q_spec, k_spec, v_spec, b_spec, o_spec, st_spec],
        out_specs=[pl.BlockSpec((tb, wide), lambda i: (rb(i), 0)), b_spec],
        out_shape=[jax.ShapeDtypeStruct((s, wide), F32), jax.ShapeDtypeStruct((s, GLA_K_TOTAL), F32)],
        scratch_shapes=[pltpu.VMEM((GLA_V_TOTAL, GLA_K_TOTAL), F32)],
        compiler_params=_cp(("arbitrary",)),
    )(z, z, z, b, do, states)


HALO = 8


def _halo_specs(s, ts, width, col):
    last = s // HALO - 1
    per = ts // HALO
    prev = pl.BlockSpec((HALO, width), lambda i: (jnp.maximum(i * per - 1, 0), col))
    nxt = pl.BlockSpec((HALO, width), lambda i: (jnp.minimum((i + 1) * per, last), col))
    return prev, nxt


def _group_ones():
    r = lax.broadcasted_iota(jnp.int32, (CONV_WIDTH, CONV_WIDTH), 0) >> 6
    c = lax.broadcasted_iota(jnp.int32, (CONV_WIDTH, CONV_WIDTH), 1) >> 6
    return (r == c).astype(BF16)


def _conv_terms(cc_ext, cu_ext, cw, valid):
    n = cc_ext.shape[0]
    hc = jnp.where(valid, cc_ext * cu_ext, 0.0)
    hc_prev = pltpu.roll(hc, 1, 0)
    hc_next = pltpu.roll(hc, n - 1, 0)
    conv = cw[0:1] * hc_prev + cw[1:2] * hc + cw[2:3] * hc_next
    return hc, hc_prev, hc_next, conv


def _ext(prev_ref, cur_ref, next_ref):
    return jnp.concatenate([prev_ref[...], cur_ref[...], next_ref[...]], axis=0)


def _valid_rows(ts, s):
    row = lax.broadcasted_iota(jnp.int32, (ts + 2 * HALO, 1), 0) + (pl.program_id(0) * ts - HALO)
    return (row >= 0) & (row < s)


def _head_norm(o, gn):
    out = []
    for h in range(GLA_HEADS):
        oh = o[:, GLA_DV * h : GLA_DV * (h + 1)]
        r = lax.rsqrt(jnp.mean(oh * oh, axis=-1, keepdims=True) + EPS)
        out.append((oh * r, r))
    return out


def _mix_fwd(z, o_f, o_b, conv_w, conv_norm, gla_norm, *, name):
    s = z.shape[0]
    ts = _rows(s)
    cprev, cnext = _halo_specs(s, ts, CONV_WIDTH, 1)
    uprev, unext = _halo_specs(s, ts, CONV_WIDTH, 2)

    def body(cb_ref, cc_ref, cu_ref, ccp_ref, ccn_ref, cup_ref, cun_ref, g_ref, of_ref, ob_ref, cw_ref, cn_ref, gn_ref, y_ref):
        valid = _valid_rows(ts, s)
        _, _, _, conv = _conv_terms(_ext(ccp_ref, cc_ref, ccn_ref), _ext(cup_ref, cu_ref, cun_ref), cw_ref[...], valid)
        yc = cb_ref[...] * conv[HALO : HALO + ts]
        ms = _dot_split(yc * yc, _group_ones()) * (1.0 / CONV_GROUP)
        y_conv = yc * lax.rsqrt(ms + EPS) * cn_ref[...]
        gate = g_ref[...]
        silu = gate * _sigmoid(gate)
        gn = gn_ref[...]
        y_gla = jnp.concatenate([oh * gn for oh, _ in _head_norm(of_ref[...] + ob_ref[...], gn)], axis=1) * silu
        y_ref[...] = jnp.concatenate([y_conv, y_gla], axis=1).astype(y_ref.dtype)

    col = lambda c, w=CONV_WIDTH: pl.BlockSpec((ts, w), lambda i: (i, c))
    return pl.pallas_call(
        body,
        name=name,
        grid=(s // ts,),
        in_specs=[col(0), col(1), col(2), cprev, cnext, uprev, unext, col(3), col(0), col(0),
                  pl.BlockSpec((CONV_K, CONV_WIDTH), lambda i: (0, 0)), pl.BlockSpec((1, CONV_WIDTH), lambda i: (0, 0)),
                  pl.BlockSpec((1, GLA_DV), lambda i: (0, 0))],
        out_specs=pl.BlockSpec((ts, D_MODEL), lambda i: (i, 0)),
        out_shape=jax.ShapeDtypeStruct((s, D_MODEL), _CD),
        compiler_params=_cp(("parallel",)),
    )(z, z, z, z, z, z, z, z, o_f, o_b, conv_w, conv_norm, gla_norm)


def _mix_bwd(z, o_f, o_b, dy, conv_w, conv_norm, gla_norm, *, name):
    s = z.shape[0]
    ts = _rows(s)
    halos = [_halo_specs(s, ts, CONV_WIDTH, c) for c in (0, 1, 2)]
    dprev, dnext = _halo_specs(s, ts, CONV_WIDTH, 0)

    def body(cb_ref, cc_ref, cu_ref, cbp_ref, cbn_ref, ccp_ref, ccn_ref, cup_ref, cun_ref, g_ref, of_ref, ob_ref,
             dyc_ref, dyg_ref, dyp_ref, dyn_ref, cw_ref, cn_ref, gn_ref, dza_ref, do_ref, dcw_ref, dcn_ref, dgn_ref):
        n = ts + 2 * HALO
        valid = _valid_rows(ts, s)
        cw = cw_ref[...]
        cn = cn_ref[...]
        ones = _group_ones()
        cb = _ext(cbp_ref, cb_ref, cbn_ref)
        cc = _ext(ccp_ref, cc_ref, ccn_ref)
        cu = _ext(cup_ref, cu_ref, cun_ref)
        dy = _ext(dyp_ref, dyc_ref, dyn_ref)
        hc, hc_prev, hc_next, conv = _conv_terms(cc, cu, cw, valid)
        yc = cb * conv
        r = lax.rsqrt(_dot_split(yc * yc, ones) * (1.0 / CONV_GROUP) + EPS)
        yh = yc * r
        dyh = dy * cn
        dyc = r * (dyh - yh * (_dot_split(dyh * yh, ones) * (1.0 / CONV_GROUP)))
        dconv = jnp.where(valid, dyc * cb, 0.0)
        dhc = cw[0:1] * pltpu.roll(dconv, n - 1, 0) + cw[1:2] * dconv + cw[2:3] * pltpu.roll(dconv, 1, 0)
        mid = lambda a: a[HALO : HALO + ts]
        dza_ref[:, 0 : 3 * CONV_WIDTH] = jnp.concatenate([mid(dyc * conv), mid(dhc * cu), mid(dhc * cc)], axis=1).astype(dza_ref.dtype)
        dconv_m = mid(dconv)
        colsum = lambda a: jnp.sum(a, axis=0, keepdims=True)
        dcw = jnp.concatenate([colsum(dconv_m * mid(hc_prev)), colsum(dconv_m * mid(hc)), colsum(dconv_m * mid(hc_next))], axis=0)
        dcn = colsum(mid(dy * yh))

        gate = g_ref[...]
        sg = _sigmoid(gate)
        silu = gate * sg
        gn = gn_ref[...]
        dyg = dyg_ref[...]
        don = dyg * silu
        heads = _head_norm(of_ref[...] + ob_ref[...], gn)
        on = jnp.concatenate([oh * gn for oh, _ in heads], axis=1)
        dza_ref[:, 3 * CONV_WIDTH : ZA_COLS] = (dyg * on * (sg * (1.0 + gate * (1.0 - sg)))).astype(dza_ref.dtype)
        dgn = jnp.zeros((1, GLA_DV), F32)
        dos = []
        for h, (oh, rh) in enumerate(heads):
            donh = don[:, GLA_DV * h : GLA_DV * (h + 1)]
            dgn = dgn + colsum(donh * oh)
            doh = donh * gn
            dos.append(rh * (doh - oh * jnp.mean(doh * oh, axis=-1, keepdims=True)))
        do_ref[...] = jnp.concatenate(dos, axis=1)

        first = pl.program_id(0) == 0

        @pl.when(first)
        def _():
            dcw_ref[...] = dcw
            dcn_ref[...] = dcn
            dgn_ref[...] = dgn

        @pl.when(jnp.logical_not(first))
        def _():
            dcw_ref[...] += dcw
            dcn_ref[...] += dcn
            dgn_ref[...] += dgn

    col = lambda c, w=CONV_WIDTH: pl.BlockSpec((ts, w), lambda i: (i, c))
    cw_spec = pl.BlockSpec((CONV_K, CONV_WIDTH), lambda i: (0, 0))
    cn_spec = pl.BlockSpec((1, CONV_WIDTH), lambda i: (0, 0))
    gn_spec = pl.BlockSpec((1, GLA_DV), lambda i: (0, 0))
    return pl.pallas_call(
        body,
        name=name,
        grid=(s // ts,),
        in_specs=[col(0), col(1), col(2), halos[0][0], halos[0][1], halos[1][0], halos[1][1], halos[2][0], halos[2][1],
                  col(3), col(0), col(0), col(0), col(1), dprev, dnext, cw_spec, cn_spec, gn_spec],
        out_specs=[pl.BlockSpec((ts, ZA_COLS), lambda i: (i, 0)), col(0), cw_spec, cn_spec, gn_spec],
        out_shape=[
            jax.ShapeDtypeStruct((s, ZA_COLS), _CD),
            jax.ShapeDtypeStruct((s, GLA_V_TOTAL), F32),
            jax.ShapeDtypeStruct((CONV_K, CONV_WIDTH), F32),
            jax.ShapeDtypeStruct((1, CONV_WIDTH), F32),
            jax.ShapeDtypeStruct((1, GLA_DV), F32),
        ],
        compiler_params=_cp(("arbitrary",)),
    )(z, z, z, z, z, z, z, z, z, z, o_f, o_b, dy, dy, dy, dy, conv_w, conv_norm, gla_norm)


def _xa_probs(q_ref, kv_ref, h):
    qh = q_ref[:, XA_HEAD_DIM * h : XA_HEAD_DIM * (h + 1)]
    kh = kv_ref[:, XA_HEAD_DIM * h : XA_HEAD_DIM * (h + 1)]
    vh = kv_ref[:, D_MODEL + XA_HEAD_DIM * h : D_MODEL + XA_HEAD_DIM * (h + 1)]
    sc = _dot_nt(qh, kh) * (XA_HEAD_DIM**-0.5)
    e = jnp.exp(sc - jnp.max(sc, axis=-1, keepdims=True))
    return qh, kh, vh, e / jnp.sum(e, axis=-1, keepdims=True)


def _xattn_fwd(qx, kv, *, name):
    s = qx.shape[0]
    ts = _rows(s)

    def body(q_ref, kv_ref, o_ref):
        outs = []
        for h in range(XA_HEADS):
            _, _, vh, p = _xa_probs(q_ref, kv_ref, h)
            outs.append(_dot(p, vh))
        o_ref[...] = jnp.concatenate(outs, axis=1).astype(o_ref.dtype)

    return pl.pallas_call(
        body,
        name=name,
        grid=(s // ts,),
        in_specs=[pl.BlockSpec((ts, D_MODEL), lambda i: (i, 0)), pl.BlockSpec((N_MEM, 2 * D_MODEL), lambda i: (0, 0))],
        out_specs=pl.BlockSpec((ts, D_MODEL), lambda i: (i, 0)),
        out_shape=jax.ShapeDtypeStruct((s, D_MODEL), _CD),
        compiler_params=_cp(("parallel",)),
    )(qx, kv)


def _xattn_bwd(qx, kv, dox, *, name):
    s = qx.shape[0]
    ts = _rows(s)

    def body(q_ref, kv_ref, do_ref, dq_ref, dkv_ref):
        dqs, dks, dvs = [], [], []
        for h in range(XA_HEADS):
            qh, kh, vh, p = _xa_probs(q_ref, kv_ref, h)
            doh = do_ref[:, XA_HEAD_DIM * h : XA_HEAD_DIM * (h + 1)]
            dp = _dot_nt(doh, vh)
            ds = p * (dp - jnp.sum(dp * p, axis=-1, keepdims=True)) * (XA_HEAD_DIM**-0.5)
            dqs.append(_dot(ds, kh))
            dks.append(_dot_tn(ds, qh))
            dvs.append(_dot_tn(p, doh))
        dq_ref[...] = jnp.concatenate(dqs, axis=1).astype(dq_ref.dtype)
        dkv = jnp.concatenate(dks + dvs, axis=1)

        @pl.when(pl.program_id(0) == 0)
        def _():
            dkv_ref[...] = dkv

        @pl.when(pl.program_id(0) > 0)
        def _():
            dkv_ref[...] += dkv

    tile = pl.BlockSpec((ts, D_MODEL), lambda i: (i, 0))
    kv_spec = pl.BlockSpec((N_MEM, 2 * D_MODEL), lambda i: (0, 0))
    return pl.pallas_call(
        body,
        name=name,
        grid=(s // ts,),
        in_specs=[tile, kv_spec, tile],
        out_specs=[tile, kv_spec],
        out_shape=[jax.ShapeDtypeStruct((s, D_MODEL), _CD), jax.ShapeDtypeStruct((N_MEM, 2 * D_MODEL), F32)],
        compiler_params=_cp(("arbitrary",)),
    )(qx, kv, dox)


def _adamw_math(w, g, m, v):
    m = ADAM_B1 * m + (1.0 - ADAM_B1) * g
    v = ADAM_B2 * v + (1.0 - ADAM_B2) * (g * g)
    m_hat = m / (1.0 - ADAM_B1**ADAM_STEP)
    v_hat = v / (1.0 - ADAM_B2**ADAM_STEP)
    delta = -ADAM_LR * (m_hat / (jnp.sqrt(v_hat) + ADAM_EPS) + ADAM_WD * w)
    return delta, m, v


def _adamw(w, g, m, v, *, name):
    r, c = w.shape
    tr = _pick(r, (256, 128, 64, 32, 16, 8))

    def body(w_ref, g_ref, m_ref, v_ref, d_ref, nm_ref, nv_ref):
        d_ref[...], nm_ref[...], nv_ref[...] = _adamw_math(w_ref[...], g_ref[...], m_ref[...], v_ref[...])

    tile = pl.BlockSpec((tr, c), lambda i: (i, 0))
    return pl.pallas_call(
        body,
        name=name,
        grid=(r // tr,),
        in_specs=[tile] * 4,
        out_specs=[tile] * 3,
        out_shape=[jax.ShapeDtypeStruct((r, c), F32)] * 3,
        compiler_params=_cp(("parallel",)),
    )(w, g, m, v)


def _adamw_small(groups, *, name):
    n = len(groups)

    def body(*refs):
        ins, outs = refs[: 4 * n], refs[4 * n :]
        for i in range(n):
            w_ref, g_ref, m_ref, v_ref = ins[4 * i : 4 * i + 4]
            outs[3 * i][...], outs[3 * i + 1][...], outs[3 * i + 2][...] = _adamw_math(w_ref[...], g_ref[...], m_ref[...], v_ref[...])

    flat = [a for grp in groups for a in grp]
    vm = pl.BlockSpec(memory_space=pltpu.VMEM)
    res = pl.pallas_call(
        body,
        name=name,
        in_specs=[vm] * (4 * n),
        out_specs=[vm] * (3 * n),
        out_shape=[jax.ShapeDtypeStruct(grp[0].shape, F32) for grp in groups for _ in range(3)],
        compiler_params=_cp(),
    )(*flat)
    return [tuple(res[3 * i : 3 * i + 3]) for i in range(n)]


def _place():
    return lax.axis_index("x"), lax.axis_index("y"), lax.axis_index("c")


def _rel_chip(x, y, k):
    return (1 - x if k & 2 else x), (1 - y if k & 1 else y)


def _half(c, rh):
    return pl.ds(pl.multiple_of(c * rh, 16), rh)


def _gather_weights(pack):
    r, w = pack.shape
    rh = r // 2

    def body(p_ref, q_ref, send_sems, recv_sems):
        x, y, c = _place()
        j = 2 * x + y
        rows = _half(c, rh)

        def to_chip(k):
            cx, cy = _rel_chip(x, y, k)
            return pltpu.make_async_remote_copy(
                src_ref=p_ref.at[rows], dst_ref=q_ref.at[j, rows], send_sem=send_sems.at[k - 1], recv_sem=recv_sems.at[k - 1],
                device_id=(cx, cy, c), device_id_type=MESH)

        def to_sibling(k):
            cx, cy = _rel_chip(x, y, k)
            slot = q_ref.at[2 * cx + cy, rows]
            return pltpu.make_async_remote_copy(
                src_ref=slot, dst_ref=slot, send_sem=send_sems.at[2 + k], recv_sem=recv_sems.at[2 + k],
                device_id=(x, y, 1 - c), device_id_type=MESH)

        first = [to_chip(k) for k in range(1, N_CHIPS)]
        passed = [to_sibling(k) for k in range(1, N_CHIPS)]
        for cp in first:
            cp.start()
        for cp, fw in zip(first, passed):
            cp.wait_recv()
            fw.start()
        for fw in passed:
            fw.wait_recv()
        for cp in first + passed:
            cp.wait_send()

    return pl.pallas_call(
        body,
        name="gather_weights",
        in_specs=[ANY],
        out_specs=ANY,
        out_shape=jax.ShapeDtypeStruct((N_CHIPS, r, w), pack.dtype),
        scratch_shapes=[pltpu.SemaphoreType.DMA((6,)), pltpu.SemaphoreType.DMA((6,))],
        compiler_params=pltpu.CompilerParams(has_side_effects=True),
    )(pack)


def _swap_halves(g):
    n, r, w = g.shape
    rh = r // 2

    def body(g_ref, o_ref, send_sem, recv_sem):
        x, y, c = _place()
        cp = pltpu.make_async_remote_copy(
            src_ref=g_ref.at[:, _half(1 - c, rh)], dst_ref=o_ref, send_sem=send_sem, recv_sem=recv_sem,
            device_id=(x, y, 1 - c), device_id_type=MESH)
        cp.start()
        cp.wait()

    return pl.pallas_call(
        body,
        name="grads_to_sibling",
        in_specs=[ANY],
        out_specs=ANY,
        out_shape=jax.ShapeDtypeStruct((n, rh, w), g.dtype),
        scratch_shapes=[pltpu.SemaphoreType.DMA, pltpu.SemaphoreType.DMA],
        compiler_params=pltpu.CompilerParams(has_side_effects=True),
    )(g)


def _chip_sums(g, got, where):
    n, r, w = g.shape
    rh = r // 2
    nt = rh // PACK_TILE

    def body(where_ref, g_ref, got_ref, o_ref):
        o_ref[...] = (g_ref[...] + got_ref[...]).astype(o_ref.dtype)

    return pl.pallas_call(
        body,
        name="chip_sums",
        grid_spec=pltpu.PrefetchScalarGridSpec(
            num_scalar_prefetch=1,
            grid=(n, nt),
            in_specs=[pl.BlockSpec((1, PACK_TILE, w), lambda a, i, wh: (a, wh[0] * nt + i, 0)),
                      pl.BlockSpec((1, PACK_TILE, w), lambda a, i, wh: (a, i, 0))],
            out_specs=pl.BlockSpec((1, PACK_TILE, w), lambda a, i, wh: (a, i, 0)),
        ),
        out_shape=jax.ShapeDtypeStruct((n, rh, w), _TD),
        compiler_params=_cp(("parallel", "parallel")),
    )(where, g, got)


def _exchange_chip_sums(h):
    n, rh, w = h.shape

    def body(h_ref, o_ref, send_sems, recv_sems):
        x, y, c = _place()
        j = 2 * x + y
        copies = []
        for k in range(1, N_CHIPS):
            cx, cy = _rel_chip(x, y, k)
            copies.append(pltpu.make_async_remote_copy(
                src_ref=h_ref.at[2 * cx + cy], dst_ref=o_ref.at[k - 1], send_sem=send_sems.at[k - 1], recv_sem=recv_sems.at[k - 1],
                device_id=(cx, cy, c), device_id_type=MESH))
        for cp in copies:
            cp.start()
        for cp in copies:
            cp.wait()

    return pl.pallas_call(
        body,
        name="chip_sums_exchange",
        in_specs=[ANY],
        out_specs=ANY,
        out_shape=jax.ShapeDtypeStruct((N_CHIPS - 1, rh, w), h.dtype),
        scratch_shapes=[pltpu.SemaphoreType.DMA((3,)), pltpu.SemaphoreType.DMA((3,))],
        compiler_params=pltpu.CompilerParams(has_side_effects=True),
    )(h)


def _shard_sum(g, got, others, where):
    n, r, w = g.shape
    rh = r // 2
    nt = rh // PACK_TILE

    def body(where_ref, g_ref, got_ref, oth_ref, o_ref):
        acc = g_ref[0] + got_ref[0]
        for k in range(N_CHIPS - 1):
            acc = acc + oth_ref[k].astype(F32)
        o_ref[...] = acc

    return pl.pallas_call(
        body,
        name="shard_sum",
        grid_spec=pltpu.PrefetchScalarGridSpec(
            num_scalar_prefetch=1,
            grid=(nt,),
            in_specs=[pl.BlockSpec((1, PACK_TILE, w), lambda i, wh: (wh[1], wh[0] * nt + i, 0)),
                      pl.BlockSpec((1, PACK_TILE, w), lambda i, wh: (wh[1], i, 0)),
                      pl.BlockSpec((N_CHIPS - 1, PACK_TILE, w), lambda i, wh: (0, i, 0))],
            out_specs=pl.BlockSpec((PACK_TILE, w), lambda i, wh: (i, 0)),
        ),
        out_shape=jax.ShapeDtypeStruct((rh, w), F32),
        compiler_params=_cp(("parallel",)),
    )(where, g, got, others)


def _join_halves(e):
    rh, w = e.shape

    def body(e_ref, o_ref, send_sem, recv_sem, local_sem):
        x, y, c = _place()
        rows = _half(c, rh)
        mine = pltpu.make_async_copy(e_ref, o_ref.at[rows], local_sem)
        mine.start()
        cp = pltpu.make_async_remote_copy(
            src_ref=e_ref, dst_ref=o_ref.at[rows], send_sem=send_sem, recv_sem=recv_sem, device_id=(x, y, 1 - c), device_id_type=MESH)
        cp.start()
        cp.wait()
        mine.wait()

    return pl.pallas_call(
        body,
        name="shard_to_sibling",
        in_specs=[ANY],
        out_specs=ANY,
        out_shape=jax.ShapeDtypeStruct((2 * rh, w), e.dtype),
        scratch_shapes=[pltpu.SemaphoreType.DMA, pltpu.SemaphoreType.DMA, pltpu.SemaphoreType.DMA],
        compiler_params=pltpu.CompilerParams(has_side_effects=True),
    )(e)


HBM = pl.BlockSpec(memory_space=pltpu.HBM)
SEM = pl.BlockSpec(memory_space=pltpu.SEMAPHORE)
EFFECT = pltpu.SideEffectType.DATAFLOW_SIDE_EFFECTING


def _in_hbm(a):
    return pltpu.with_memory_space_constraint(a, pltpu.HBM)


def _gather_copies(p_ref, land_ref, send_sems, recv_sems):
    rh = p_ref.shape[0] // 2
    x, y, c = _place()
    rows = _half(c, rh)
    copies = []
    for k in range(1, N_CHIPS):
        cx, cy = _rel_chip(x, y, k)
        copies.append(pltpu.make_async_remote_copy(
            src_ref=p_ref.at[rows], dst_ref=land_ref.at[2 * x + y, rows], send_sem=send_sems.at[k - 1], recv_sem=recv_sems.at[k - 1],
            device_id=(cx, cy, c), device_id_type=MESH))
    return copies


def _gather_start(pack, after, *, name):
    r, w = pack.shape

    def body(p_ref, land_ref, after_ref, send_sems, recv_sems, p_thru, land_thru, token):
        for cp in _gather_copies(p_ref, land_ref, send_sems, recv_sems):
            cp.start()
        token[...] = jnp.zeros_like(token)

    return pl.pallas_call(
        body,
        name=name,
        out_shape=(pltpu.SemaphoreType.DMA((N_CHIPS - 1,)), pltpu.SemaphoreType.DMA((N_CHIPS - 1,)), pltpu.HBM((r, w), pack.dtype),
                   pltpu.HBM((N_CHIPS, r, w), pack.dtype), jax.ShapeDtypeStruct((8, 128), F32)),
        in_specs=(HBM, HBM, ANY),
        out_specs=(SEM, SEM, HBM, HBM, pl.BlockSpec(memory_space=pltpu.VMEM)),
        input_output_aliases={0: 2, 1: 3},
        compiler_params=pltpu.CompilerParams(has_side_effects=EFFECT),
    )(_in_hbm(pack), _in_hbm(lax.empty((N_CHIPS, r, w), pack.dtype)), after)


def _gather_wait(send_sems, recv_sems, pack, land, after, *, name):
    def body(p_ref, land_ref, send_sems, recv_sems, after_ref, p_out, land_out):
        for cp in _gather_copies(p_ref, land_ref, send_sems, recv_sems):
            cp.wait_send()
            cp.wait_recv()

    return pl.pallas_call(
        body,
        name=name,
        out_shape=(pltpu.HBM(pack.shape, pack.dtype), pltpu.HBM(land.shape, land.dtype)),
        in_specs=(HBM, HBM, SEM, SEM, ANY),
        out_specs=(HBM, HBM),
        input_output_aliases={0: 0, 1: 1},
        compiler_params=pltpu.CompilerParams(has_side_effects=EFFECT),
    )(pack, land, send_sems, recv_sems, after)


def _gather_spread(land, *, name):
    n, r, w = land.shape
    rh = r // 2

    def body(land_ref, o_ref, send_sems, recv_sems):
        x, y, c = _place()
        rows = _half(c, rh)
        copies = []
        for k in range(1, N_CHIPS):
            cx, cy = _rel_chip(x, y, k)
            copies.append(pltpu.make_async_remote_copy(
                src_ref=land_ref.at[2 * cx + cy, rows], dst_ref=o_ref.at[2 * cx + cy, rows], send_sem=send_sems.at[k - 1],
                recv_sem=recv_sems.at[k - 1], device_id=(x, y, 1 - c), device_id_type=MESH))
        for cp in copies:
            cp.start()
        for cp in copies:
            cp.wait()

    return pl.pallas_call(
        body,
        name=name,
        in_specs=[ANY],
        out_specs=ANY,
        out_shape=jax.ShapeDtypeStruct(land.shape, land.dtype),
        input_output_aliases={0: 0},
        scratch_shapes=[pltpu.SemaphoreType.DMA((N_CHIPS - 1,)), pltpu.SemaphoreType.DMA((N_CHIPS - 1,))],
        compiler_params=pltpu.CompilerParams(has_side_effects=True),
    )(land)


N_PARTS = 2 * (N_CHIPS - 1)


def _scatter_copies(lo_ref, g_ref, land_lo_ref, land_f_ref, send_sems, recv_sems, starting):
    rh = g_ref.shape[1] // 2
    x, y, c = _place()
    copies = []
    for k in range(1, N_CHIPS):
        cx, cy = _rel_chip(x, y, k)
        for i in range(2):
            part = 2 * (k - 1) + (c if starting else i)
            copies.append(pltpu.make_async_remote_copy(
                src_ref=lo_ref.at[2 * cx + cy, pl.ds(i * rh, rh)], dst_ref=land_lo_ref.at[part],
                send_sem=send_sems.at[2 * (k - 1) + i], recv_sem=recv_sems.at[part], device_id=(cx, cy, i), device_id_type=MESH))
    copies.append(pltpu.make_async_remote_copy(
        src_ref=g_ref.at[2 * x + y, _half(1 - c, rh)], dst_ref=land_f_ref, send_sem=send_sems.at[N_PARTS], recv_sem=recv_sems.at[N_PARTS],
        device_id=(x, y, 1 - c), device_id_type=MESH))
    return copies


def _scatter_start(g_lo, g, *, name):
    n, r, w = g.shape
    rh = r // 2

    def body(lo_ref, g_ref, land_lo_ref, land_f_ref, send_sems, recv_sems, lo_thru, g_thru, land_lo_thru, land_f_thru, token):
        for cp in _scatter_copies(lo_ref, g_ref, land_lo_ref, land_f_ref, send_sems, recv_sems, True):
            cp.start()
        token[...] = jnp.zeros_like(token)

    return pl.pallas_call(
        body,
        name=name,
        out_shape=(pltpu.SemaphoreType.DMA((N_PARTS + 1,)), pltpu.SemaphoreType.DMA((N_PARTS + 1,)), pltpu.HBM(g_lo.shape, g_lo.dtype),
                   pltpu.HBM(g.shape, g.dtype), pltpu.HBM((N_PARTS, rh, w), g_lo.dtype), pltpu.HBM((rh, w), g.dtype),
                   jax.ShapeDtypeStruct((8, 128), F32)),
        in_specs=(HBM, HBM, HBM, HBM),
        out_specs=(SEM, SEM, HBM, HBM, HBM, HBM, pl.BlockSpec(memory_space=pltpu.VMEM)),
        input_output_aliases={0: 2, 1: 3, 2: 4, 3: 5},
        compiler_params=pltpu.CompilerParams(has_side_effects=EFFECT),
    )(_in_hbm(g_lo), _in_hbm(g), _in_hbm(lax.empty((N_PARTS, rh, w), g_lo.dtype)), _in_hbm(lax.empty((rh, w), g.dtype)))


def _scatter_wait(send_sems, recv_sems, g_lo, g, land_lo, land_f, after, *, name):
    def body(lo_ref, g_ref, land_lo_ref, land_f_ref, send_sems, recv_sems, after_ref, o0, o1, o2, o3):
        for cp in _scatter_copies(lo_ref, g_ref, land_lo_ref, land_f_ref, send_sems, recv_sems, False):
            cp.wait_send()
            cp.wait_recv()

    arrays = (g_lo, g, land_lo, land_f)
    return pl.pallas_call(
        body,
        name=name,
        out_shape=tuple(pltpu.HBM(a.shape, a.dtype) for a in arrays),
        in_specs=(HBM, HBM, HBM, HBM, SEM, SEM, ANY),
        out_specs=(HBM, HBM, HBM, HBM),
        input_output_aliases={0: 0, 1: 1, 2: 2, 3: 3},
        compiler_params=pltpu.CompilerParams(has_side_effects=EFFECT),
    )(*arrays, send_sems, recv_sems, after)


def _scatter_sum(g, land_lo, land_f, where, *, name):
    n, r, w = g.shape
    rh = r // 2
    tr = _pick(rh, (256, 160, 80))
    nt = rh // tr

    def body(where_ref, g_ref, f_ref, lo_ref, o_ref):
        acc = g_ref[0] + f_ref[...]
        for part in range(N_PARTS):
            acc = acc + lo_ref[part].astype(F32)
        o_ref[...] = acc

    return pl.pallas_call(
        body,
        name=name,
        grid_spec=pltpu.PrefetchScalarGridSpec(
            num_scalar_prefetch=1,
            grid=(nt,),
            in_specs=[pl.BlockSpec((1, tr, w), lambda i, wh: (wh[1], wh[0] * nt + i, 0)),
                      pl.BlockSpec((tr, w), lambda i, wh: (i, 0)),
                      pl.BlockSpec((N_PARTS, tr, w), lambda i, wh: (0, i, 0))],
            out_specs=pl.BlockSpec((tr, w), lambda i, wh: (i, 0)),
        ),
        out_shape=jax.ShapeDtypeStruct((rh, w), F32),
        compiler_params=_cp(("parallel",)),
    )(where, g, land_f, land_lo)


def _swap_all(halves, *, name):
    n = len(halves)

    def body(*refs):
        ins, outs = refs[:n], refs[n : 2 * n]
        send_sems, recv_sems = refs[2 * n :]
        x, y, c = _place()
        copies = [pltpu.make_async_remote_copy(src_ref=e_ref, dst_ref=o_ref, send_sem=send_sems.at[i], recv_sem=recv_sems.at[i],
                                               device_id=(x, y, 1 - c), device_id_type=MESH)
                  for i, (e_ref, o_ref) in enumerate(zip(ins, outs))]
        for cp in copies:
            cp.start()
        for cp in copies:
            cp.wait()

    return pl.pallas_call(
        body,
        name=name,
        in_specs=[ANY] * n,
        out_specs=[ANY] * n,
        out_shape=[jax.ShapeDtypeStruct(e.shape, e.dtype) for e in halves],
        scratch_shapes=[pltpu.SemaphoreType.DMA((n,)), pltpu.SemaphoreType.DMA((n,))],
        compiler_params=pltpu.CompilerParams(has_side_effects=True),
    )(*halves)


def _sum_small(small, after):
    n_dev = 8

    def body(s_ref, after_ref, o_ref, all_ref, send_sems, recv_sems):
        x, y, c = _place()
        me = 4 * x + 2 * y + c
        all_ref[me] = s_ref[...]
        copies = []
        for k in range(1, n_dev):
            cx, cy = _rel_chip(x, y, k >> 1)
            cc = 1 - c if k & 1 else c
            copies.append(pltpu.make_async_remote_copy(
                src_ref=s_ref, dst_ref=all_ref.at[me], send_sem=send_sems.at[k - 1], recv_sem=recv_sems.at[k - 1],
                device_id=(cx, cy, cc), device_id_type=MESH))
        for cp in copies:
            cp.start()
        for cp in copies:
            cp.wait()
        acc = all_ref[0]
        for a in range(1, n_dev):
            acc = acc + all_ref[a]
        o_ref[...] = acc

    vm = pl.BlockSpec(memory_space=pltpu.VMEM)
    return pl.pallas_call(
        body,
        name="sum_small",
        in_specs=[vm, ANY],
        out_specs=vm,
        out_shape=jax.ShapeDtypeStruct(small.shape, F32),
        scratch_shapes=[pltpu.VMEM((n_dev,) + small.shape, F32), pltpu.SemaphoreType.DMA((n_dev - 1,)), pltpu.SemaphoreType.DMA((n_dev - 1,))],
        compiler_params=pltpu.CompilerParams(has_side_effects=True),
    )(small, after)


MATS = {"w_in": (776, True), "w_out": (256, False), "w_xq": (256, False), "w_xkv": (512, True), "w_xo": (256, False),
        "w_up": (1024, True), "w_down": (1024, False)}
GATHER_FIRST = ("w_in",)
GATHER_REST = ("w_out", "w_xq", "w_xkv", "w_xo", "w_up", "w_down")
GRAD_GROUPS = (("w_up", "w_down"), ("w_out", "w_xq", "w_xkv", "w_xo"), ("w_in",))


def _group_rows(names):
    n = sum(MATS[name][0] for name in names)
    return n + (-n) % 32


def _pack(pieces, rows):
    p = jnp.concatenate(pieces, axis=0) if len(pieces) > 1 else pieces[0]
    return jnp.pad(p, ((0, rows - p.shape[0]), (0, 0))) if rows > p.shape[0] else p


def _unpack(rows, names):
    out, off = {}, 0
    for name in names:
        out[name] = rows[off : off + MATS[name][0]]
        off += MATS[name][0]
    return out


SMALL = (
    ("mix_norm", 1024), ("conv_norm", 512), ("b_af", 256), ("b_ab", 256), ("gla_norm", 128), ("xa_norm", 1024), ("mem_norm", 1024),
    ("mlp_norm", 1024), ("final_norm", 1024), ("conv_w", 1536), ("w_af", 4096), ("w_ab", 4096), ("loss", 128),
)


def kernel(x, mem, mix_norm, w_in, conv_w, conv_norm, w_af, b_af, w_ab, b_ab, gla_norm, w_out, xa_norm, mem_norm, w_xq, w_xkv, w_xo, mlp_norm, w_up, w_down, final_norm, loss_target, m_mix_norm, m_w_in, m_conv_w, m_conv_norm, m_w_af, m_b_af, m_w_ab, m_b_ab, m_gla_norm, m_w_out, m_xa_norm, m_mem_norm, m_w_xq, m_w_xkv, m_w_xo, m_mlp_norm, m_w_up, m_w_down, m_final_norm, v_mix_norm, v_w_in, v_conv_w, v_conv_norm, v_w_af, v_b_af, v_w_ab, v_b_ab, v_gla_norm, v_w_out, v_xa_norm, v_mem_norm, v_w_xq, v_w_xkv, v_w_xo, v_mlp_norm, v_w_up, v_w_down, v_final_norm):
    given = dict(locals())
    xi, yi, ci = _place()
    chip = 2 * xi + yi
    where = jnp.stack([ci, chip]).astype(jnp.int32)

    lo = {name: (given[name][0].T if MATS[name][1] else given[name][0]).astype(_CD) for name in MATS}
    pack_rest = _pack([lo[name] for name in GATHER_REST], _group_rows(GATHER_REST))
    pack_first = _pack([lo[name] for name in GATHER_FIRST], _group_rows(GATHER_FIRST))
    got_first = _gather_weights(pack_first)

    def whole(own, got, off, rows):
        return [jnp.where(chip == a, own[off : off + rows], got[a, off : off + rows]) for a in range(N_CHIPS)]

    w_in_t = jnp.concatenate(whole(pack_first, got_first, 0, MATS["w_in"][0]), axis=0)
    w_za = jnp.concatenate([w_in_t[0:1536], w_in_t[2560:3072]], axis=0)
    w_zb = jnp.concatenate([w_in_t[1536:2560], w_in_t[3072:W_IN_COLS], jnp.zeros((ZB_COLS - 1056, D_MODEL), _CD)], axis=0)

    def placed(shard, full_shape, col):
        return lax.dynamic_update_slice(jnp.zeros(full_shape, F32), shard, (0, col)).reshape(-1, 128)

    sw = jnp.concatenate([
        placed(conv_w[0], (CONV_K, CONV_WIDTH), 128 * chip),
        placed(w_af[0], (GLA_LOWRANK, GLA_K_TOTAL), 64 * chip),
        placed(w_ab[0], (GLA_LOWRANK, GLA_K_TOTAL), 64 * chip),
    ], axis=0)
    sw = jnp.pad(sw, ((0, SMALL_ROWS - sw.shape[0]), (0, 0))) * (ci == 0).astype(F32)
    sw = _sum_small(sw, got_first)
    rest_send, rest_recv, pack_rest, land_rest, rest_token = _gather_start(pack_rest, sw, name="gather_rest_start")
    conv_w_full = sw[0:12].reshape(CONV_K, CONV_WIDTH)
    w_af_full = sw[12:44].reshape(GLA_LOWRANK, GLA_K_TOTAL)
    w_ab_full = sw[44:76].reshape(GLA_LOWRANK, GLA_K_TOTAL)
    waf_p = jnp.pad(w_af_full, ((0, 128 - GLA_LOWRANK), (0, 0))).astype(_CD)
    wab_p = jnp.pad(w_ab_full, ((GLA_LOWRANK, 128 - 2 * GLA_LOWRANK), (0, 0))).astype(_CD)

    xs, mems, tgt = x[0], mem[0], loss_target[0]
    add_res = lambda acc, res: (acc + res,)
    behind = lambda gain, token: gain + token[0, 0]

    h1 = _rms_fwd(xs, behind(mix_norm, rest_token), name="norm_mix")
    z_b = _mm(h1, w_zb, mode="nt", name="proj_in_b", tn=ZB_COLS)
    z_a = _mm(h1, w_za, mode="nt", name="proj_in_a", tm=512, tn=ZA_COLS)
    b_f, b_b = _gate_fwd(z_b, waf_p, wab_p, b_af, b_ab, name="gates")
    o_f, st_f = _gla_fwd(z_b, b_f, rev=False, name="gla_scan_fwd")
    o_b, st_b = _gla_fwd(z_b, b_b, rev=True, name="gla_scan_rev")
    y = _mix_fwd(z_a, o_f, o_b, conv_w_full, conv_norm, gla_norm, name="mix_out")
    pack_rest, land_rest = _gather_wait(rest_send, rest_recv, pack_rest, land_rest, y, name="gather_rest_wait")
    gathered = _gather_spread(land_rest, name="gather_rest_spread")
    wt, off = {}, 0
    for name in GATHER_REST:
        wt[name] = jnp.concatenate(whole(pack_rest, gathered, off, MATS[name][0]), axis=0)
        off += MATS[name][0]
    x1, hx = _mm_rows(y, wt["w_out"], mode="nn", name="proj_out", rows=(xs,), vecs=(xa_norm,), out_rows=(F32, _CD), epilogue=_ep_residual_norm)
    qx = _mm(hx, wt["w_xq"], mode="nn", name="proj_xq", out_dtypes=(_CD,))
    hmem = _rms_fwd(mems, mem_norm, name="norm_mem")
    kv = _mm(hmem, wt["w_xkv"], mode="nt", name="proj_xkv", out_dtypes=(_CD,))
    ox = _xattn_fwd(qx, kv, name="xattn")
    x2, hm = _mm_rows(ox, wt["w_xo"], mode="nn", name="proj_xo", rows=(x1,), vecs=(mlp_norm,), out_rows=(F32, _CD), epilogue=_ep_residual_norm)
    act = _mm(hm, wt["w_up"], mode="nt", name="mlp_up", out_dtypes=(_CD,), epilogue=lambda acc: (jnp.square(jnp.maximum(acc, 0.0)),))
    dx3, dx3_lo, loss_part, g_final_norm = _mm_rows(
        act, wt["w_down"], mode="nn", name="mlp_down", rows=(x2, tgt), vecs=(final_norm.reshape(1, D_MODEL),),
        out_rows=(F32, _CD), out_vecs=(128, D_MODEL), epilogue=_ep_loss)

    grads_t = {}

    def start_group(names, tag):
        rows = _group_rows(names)
        g = jnp.stack([_pack([grads_t[name][a * MATS[name][0] : (a + 1) * MATS[name][0]] for name in names], rows) for a in range(N_CHIPS)])
        return _scatter_start(g.astype(_TD), g, name="grads_" + tag + "_start")

    def finish_group(state, after, tag):
        send_sems, recv_sems, g_lo, g, land_lo, land_f, _ = state
        g_lo, g, land_lo, land_f = _scatter_wait(send_sems, recv_sems, g_lo, g, land_lo, land_f, after, name="grads_" + tag + "_wait")
        return _scatter_sum(g, land_lo, land_f, where, name="grads_" + tag + "_sum")

    du = _mm(dx3_lo, wt["w_down"], mode="nt", name="mlp_down_dx", out_dtypes=(_CD,), extras=(act,),
             epilogue=lambda acc, aa: (acc * (2.0 * jnp.sqrt(aa.astype(F32))),))
    grads_t["w_down"] = _mm_tn(act, dx3_lo, name="mlp_down_dw")
    grads_t["w_up"] = _mm_tn(du, hm, name="mlp_up_dw")
    mlp_state = start_group(GRAD_GROUPS[0], "mlp")
    dx2, dx2_lo, g_mlp_norm = _mm_rows(
        du, wt["w_up"], mode="nn", name="mlp_up_dx", rows=(x2, dx3), vecs=(behind(mlp_norm, mlp_state[-1]),),
        out_rows=(F32, _CD), out_vecs=(D_MODEL,), epilogue=_ep_norm_bwd)
    dox = _mm(dx2_lo, wt["w_xo"], mode="nt", name="proj_xo_dx", out_dtypes=(_CD,))
    grads_t["w_xo"] = _mm_tn(ox, dx2_lo, name="proj_xo_dw")
    dqx, dkv = _xattn_bwd(qx, kv, dox, name="xattn_bwd")
    grads_t["w_xq"] = _mm_tn(hx, dqx, name="proj_xq_dw")
    dx1, dx1_lo, g_xa_norm = _mm_rows(
        dqx, wt["w_xq"], mode="nt", name="proj_xq_dx", rows=(x1, dx2), vecs=(xa_norm,),
        out_rows=(F32, _CD), out_vecs=(D_MODEL,), epilogue=_ep_norm_bwd)
    dkv_lo = dkv.astype(_CD)
    grads_t["w_xkv"] = _mm_tn(dkv_lo, hmem, name="proj_xkv_dw")
    dhmem = _mm(dkv_lo, wt["w_xkv"], mode="nn", name="proj_xkv_dx")
    (g_mem_norm,) = _rms_bwd(mems, mem_norm, dhmem, name="norm_mem_bwd", want_dx=False, want_lo=False)
    dy = _mm(dx1_lo, wt["w_out"], mode="nt", name="proj_out_dx")
    grads_t["w_out"] = _mm_tn(y, dx1_lo, name="proj_out_dw")
    attn_state = start_group(GRAD_GROUPS[1], "attn")
    dz_a, do, g_conv_w, g_conv_norm, g_gla_norm = _mix_bwd(z_a, o_f, o_b, dy, conv_w_full, behind(conv_norm, attn_state[-1]), gla_norm, name="mix_out_bwd")
    dqkv_f, db_f = _gla_bwd(z_b, b_f, do, st_f, rev=False, name="gla_scan_fwd_bwd")
    dqkv_b, db_b = _gla_bwd(z_b, b_b, do, st_b, rev=True, name="gla_scan_rev_bwd")
    dz_b, g_waf_p, g_wab_p, g_b_af, g_b_ab = _gate_bwd(z_b, waf_p, wab_p, b_af, b_ab, db_f, db_b, dqkv_f, dqkv_b, name="gates_bwd")
    g_za = _mm_tn(dz_a, h1, name="proj_in_a_dw")
    g_zb = _mm_tn(dz_b, h1, name="proj_in_b_dw")
    grads_t["w_in"] = jnp.concatenate([g_za[0:1536], g_zb[0:1024], g_za[1536:2048], g_zb[1024:1056]], axis=0)
    in_state = start_group(GRAD_GROUPS[2], "in")
    dh1_a = _mm(dz_a, w_za, mode="nn", name="proj_in_a_dx", tm=512, tk=ZA_COLS)
    grad_x, g_mix_norm = _mm_rows(
        dz_b, w_zb, mode="nn", name="proj_in_b_dx", rows=(xs, dx1, dh1_a), vecs=(behind(mix_norm, in_state[-1]),),
        out_rows=(F32,), out_vecs=(D_MODEL,), epilogue=_ep_norm_bwd)

    half_mlp = finish_group(mlp_state, grad_x, "mlp")
    half_attn = finish_group(attn_state, half_mlp, "attn")
    half_in = finish_group(in_state, half_attn, "in")
    g_shard = {}
    mine = [half_mlp, half_attn, half_in]
    for names, own, got in zip(GRAD_GROUPS, mine, _swap_all(mine, name="shards_to_sibling")):
        rows = jnp.where(ci == 0, jnp.concatenate([own, got], axis=0), jnp.concatenate([got, own], axis=0))
        for name, g in _unpack(rows, names).items():
            g_shard[name] = g.T if MATS[name][1] else g

    small_vals = dict(mix_norm=g_mix_norm, conv_norm=g_conv_norm, b_af=g_b_af, b_ab=g_b_ab, gla_norm=g_gla_norm, xa_norm=g_xa_norm,
                      mem_norm=g_mem_norm, mlp_norm=g_mlp_norm, final_norm=g_final_norm, conv_w=g_conv_w,
                      w_af=g_waf_p[0:GLA_LOWRANK], w_ab=g_wab_p[GLA_LOWRANK : 2 * GLA_LOWRANK], loss=loss_part)
    small = jnp.concatenate([small_vals[name].reshape(-1, 128) for name, _ in SMALL], axis=0)
    small = _sum_small(jnp.pad(small, ((0, SMALL_ROWS - small.shape[0]), (0, 0))), loss_part)
    g_small, off = {}, 0
    for name, n in SMALL:
        g_small[name] = small[off : off + n // 128]
        off += n // 128
    loss = g_small["loss"][0, 0]
    g_small["conv_w"] = lax.dynamic_slice(g_small["conv_w"].reshape(CONV_K, CONV_WIDTH), (0, 128 * chip), (CONV_K, 128))
    g_small["w_af"] = lax.dynamic_slice(g_small["w_af"].reshape(GLA_LOWRANK, GLA_K_TOTAL), (0, 64 * chip), (GLA_LOWRANK, 64))
    g_small["w_ab"] = lax.dynamic_slice(g_small["w_ab"].reshape(GLA_LOWRANK, GLA_K_TOTAL), (0, 64 * chip), (GLA_LOWRANK, 64))

    names = ["mix_norm", "w_in", "conv_w", "conv_norm", "w_af", "b_af", "w_ab", "b_ab", "gla_norm", "w_out", "xa_norm", "mem_norm",
             "w_xq", "w_xkv", "w_xo", "mlp_norm", "w_up", "w_down", "final_norm"]
    big_names = list(MATS)
    as2d = lambda a: a.reshape(1, -1) if a.ndim == 1 else a.reshape(a.shape[-2:])
    grads, deltas, new_m, new_v = {}, {}, {}, {}
    for name in big_names:
        grads[name] = g_shard[name]
        deltas[name], new_m[name], new_v[name] = _adamw(as2d(given[name]), g_shard[name], as2d(given["m_" + name]),
                                                         as2d(given["v_" + name]), name="adamw_" + name)
    small_names = [name for name in names if name not in big_names]
    groups = []
    for name in small_names:
        grads[name] = g_small[name].reshape(as2d(given[name]).shape)
        groups.append((as2d(given[name]), grads[name], as2d(given["m_" + name]), as2d(given["v_" + name])))
    for name, res in zip(small_names, _adamw_small(groups, name="adamw_small")):
        deltas[name], new_m[name], new_v[name] = res

    like = lambda name, a: a.reshape(given[name].shape)
    return (loss, grad_x[None], *[like(n, grads[n]) for n in names], *[like(n, deltas[n]) for n in names],
            *[like(n, new_m[n]) for n in names], *[like(n, new_v[n]) for n in names])
```

```python
import functools

import jax
import jax.numpy as jnp
from jax import lax
from jax.experimental import pallas as pl
from jax.experimental.pallas import tpu as pltpu

F32 = jnp.float32
BF16 = jnp.bfloat16
_CD = jnp.bfloat16
_TD = jnp.bfloat16

D_MODEL = 1024
N_MEM = 256
CONV_WIDTH = 512
CONV_GROUP = 64
CONV_K = 3
GLA_HEADS = 4
GLA_DK = 64
GLA_DV = 128
GLA_K_TOTAL = 256
GLA_V_TOTAL = 512
GLA_LOWRANK = 16
GLA_GATE_SCALE = 1.0 / 16.0
GLA_CHUNK = 64
XA_HEADS = 4
XA_HEAD_DIM = 256
D_FF = 4096
EPS = 1e-6
W_IN_COLS = 3104
ZA_COLS = 2048
ZB_COLS = 1152
LR_COL = 1024

ADAM_LR = 0.001
ADAM_B1 = 0.9
ADAM_B2 = 0.999
ADAM_EPS = 1e-08
ADAM_WD = 0.01
ADAM_STEP = 10

N_CHIPS = 4
PACK_W = 1024
PACK_ROWS = 4160
PACK_TILE = 160
SMALL_ROWS = 128

_TS = 512
_VMEM = 44 * 1024 * 1024
MESH = pl.DeviceIdType.MESH
ANY = pl.BlockSpec(memory_space=pl.ANY)


def _cp(sem=None, **kw):
    return pltpu.CompilerParams(dimension_semantics=sem, vmem_limit_bytes=_VMEM, **kw)


def _dot(a, b):
    return jnp.dot(a.astype(_CD), b.astype(_CD), preferred_element_type=F32)


def _dot_nt(a, b):
    return lax.dot_general(a.astype(_CD), b.astype(_CD), (((1,), (1,)), ((), ())), preferred_element_type=F32)


def _dot_tn(a, b):
    return lax.dot_general(a.astype(_CD), b.astype(_CD), (((0,), (0,)), ((), ())), preferred_element_type=F32)


def _dot_split(x, ones):
    hi = x.astype(BF16)
    r = x - hi.astype(F32)
    mid = r.astype(BF16)
    lo = (r - mid.astype(F32)).astype(BF16)
    d = lambda p: jnp.dot(p, ones, preferred_element_type=F32)
    return d(hi) + d(mid) + d(lo)


def _pick(n, cands=(1024, 640, 512, 256, 128)):
    for t in cands:
        if n % t == 0:
            return t
    return n


def _rows(s):
    return min(_TS, s)


def _sigmoid(v):
    e = jnp.exp(-jnp.abs(v))
    return jnp.where(v >= 0, 1.0 / (1.0 + e), e / (1.0 + e))


def _mm(a, b, *, mode, name, out_dtypes=(F32,), extras=(), epilogue=None, tm=None, tn=None, tk=None):
    m, k = a.shape
    n = b.shape[1] if mode == "nn" else b.shape[0]
    tm = min(m, tm or 1024)
    tn = tn or _pick(n)
    tk = tk or _pick(k)
    nk = k // tk
    n_ex, n_out = len(extras), len(out_dtypes)

    def body(*refs):
        a_ref, b_ref = refs[:2]
        ex = refs[2 : 2 + n_ex]
        outs = refs[2 + n_ex : 2 + n_ex + n_out]
        part = _dot(a_ref[...], b_ref[...]) if mode == "nn" else _dot_nt(a_ref[...], b_ref[...])

        def finish(acc):
            res = epilogue(acc, *[e[...] for e in ex]) if epilogue else (acc,)
            for o, r in zip(outs, res):
                o[...] = r.astype(o.dtype)

        if nk == 1:
            finish(part)
        else:
            acc_ref = refs[-1]
            kk = pl.program_id(2)

            @pl.when(kk == 0)
            def _():
                acc_ref[...] = part

            @pl.when(kk > 0)
            def _():
                acc_ref[...] += part

            @pl.when(kk == nk - 1)
            def _():
                finish(acc_ref[...])

    b_spec = pl.BlockSpec((tk, tn), lambda i, j, kk: (kk, j)) if mode == "nn" else pl.BlockSpec((tn, tk), lambda i, j, kk: (j, kk))
    tile = pl.BlockSpec((tm, tn), lambda i, j, kk: (i, j))
    out = pl.pallas_call(
        body,
        name=name,
        grid=(m // tm, n // tn, nk),
        in_specs=[pl.BlockSpec((tm, tk), lambda i, j, kk: (i, kk)), b_spec] + [tile] * n_ex,
        out_specs=[tile] * n_out,
        out_shape=[jax.ShapeDtypeStruct((m, n), dt) for dt in out_dtypes],
        scratch_shapes=[pltpu.VMEM((tm, tn), F32)] if nk > 1 else [],
        compiler_params=_cp(("parallel", "parallel", "arbitrary")),
    )(a, b, *extras)
    return out[0] if n_out == 1 else out


def _mm_tn(a, b, *, name):
    s, m = a.shape
    n = b.shape[1]
    cap = max(128, (1 << 20) // n)
    tm = _pick(m, tuple(t for t in (512, 640, 384, 256, 128) if t <= max(cap, 128)))
    ts = min(s, 1 << (((1 << 22) // n).bit_length() - 1))
    ns = s // ts

    def body(a_ref, b_ref, o_ref):
        part = _dot_tn(a_ref[...], b_ref[...])
        if ns == 1:
            o_ref[...] = part
        else:
            ss = pl.program_id(1)

            @pl.when(ss == 0)
            def _():
                o_ref[...] = part

            @pl.when(ss > 0)
            def _():
                o_ref[...] += part

    return pl.pallas_call(
        body,
        name=name,
        grid=(m // tm, ns),
        in_specs=[pl.BlockSpec((ts, tm), lambda i, ss: (ss, i)), pl.BlockSpec((ts, n), lambda i, ss: (ss, 0))],
        out_specs=pl.BlockSpec((tm, n), lambda i, ss: (i, 0)),
        out_shape=jax.ShapeDtypeStruct((m, n), F32),
        compiler_params=_cp(("parallel", "arbitrary")),
    )(a, b)


def _mm_tn_into(a, b, packs, *, rows, off, name):
    s, m = a.shape
    n = b.shape[1]
    tm = 512 if rows % 512 == 0 and off % 512 == 0 else 256
    per = rows // tm
    ts = min(s, 1 << (((1 << 22) // n).bit_length() - 1))
    ns = s // ts

    def body(a_ref, b_ref, f_in, lo_in, f_ref, lo_ref):
        part = _dot_tn(a_ref[...], b_ref[...])
        if ns == 1:
            f_ref[0] = part
            lo_ref[0] = part.astype(lo_ref.dtype)
        else:
            ss = pl.program_id(1)

            @pl.when(ss == 0)
            def _():
                f_ref[0] = part

            @pl.when(ss > 0)
            def _():
                f_ref[0] += part

            @pl.when(ss == ns - 1)
            def _():
                lo_ref[0] = f_ref[0].astype(lo_ref.dtype)

    spec = pl.BlockSpec((1, tm, n), lambda i, ss: (i // per, off // tm + i % per, 0))
    return pl.pallas_call(
        body,
        name=name,
        grid=(m // tm, ns),
        in_specs=[pl.BlockSpec((ts, tm), lambda i, ss: (ss, i)), pl.BlockSpec((ts, n), lambda i, ss: (ss, 0)), ANY, ANY],
        out_specs=[spec, spec],
        out_shape=[jax.ShapeDtypeStruct(p.shape, p.dtype) for p in packs],
        input_output_aliases={2: 0, 3: 1},
        compiler_params=_cp(("parallel", "arbitrary")),
    )(a, b, *packs)


def _mm_rows(a, b, *, mode, name, rows=(), vecs=(), out_rows=(), out_vecs=(), epilogue, tm=512):
    m, k = a.shape
    n = b.shape[1] if mode == "nn" else b.shape[0]
    tm = min(m, tm)
    n_r, n_v, n_or, n_ov = len(rows), len(vecs), len(out_rows), len(out_vecs)

    def body(*refs):
        a_ref, b_ref = refs[:2]
        r_refs = refs[2 : 2 + n_r]
        v_refs = refs[2 + n_r : 2 + n_r + n_v]
        or_refs = refs[2 + n_r + n_v : 2 + n_r + n_v + n_or]
        ov_refs = refs[2 + n_r + n_v + n_or :]
        acc = _dot(a_ref[...], b_ref[...]) if mode == "nn" else _dot_nt(a_ref[...], b_ref[...])
        res_rows, res_vecs = epilogue(acc, [r[...] for r in r_refs], [v[...] for v in v_refs])
        for o, r in zip(or_refs, res_rows):
            o[...] = r.astype(o.dtype)
        if n_ov:
            first = pl.program_id(0) == 0

            @pl.when(first)
            def _():
                for o, r in zip(ov_refs, res_vecs):
                    o[...] = r

            @pl.when(jnp.logical_not(first))
            def _():
                for o, r in zip(ov_refs, res_vecs):
                    o[...] += r

    tile = pl.BlockSpec((tm, n), lambda i: (i, 0))
    whole = lambda arr: pl.BlockSpec(arr.shape, lambda i: (0, 0))
    vec = lambda w: pl.BlockSpec((1, w), lambda i: (0, 0))
    out = pl.pallas_call(
        body,
        name=name,
        grid=(m // tm,),
        in_specs=[pl.BlockSpec((tm, k), lambda i: (i, 0)), whole(b)] + [tile] * n_r + [vec(v.shape[1]) for v in vecs],
        out_specs=[tile] * n_or + [vec(w) for w in out_vecs],
        out_shape=[jax.ShapeDtypeStruct((m, n), dt) for dt in out_rows] + [jax.ShapeDtypeStruct((1, w), F32) for w in out_vecs],
        compiler_params=_cp(("arbitrary",) if n_ov else ("parallel",)),
    )(a, b, *rows, *vecs)
    return out


def _ep_residual_norm(acc, rows, vecs):
    x = acc + rows[0]
    r = lax.rsqrt(jnp.mean(x * x, axis=-1, keepdims=True) + EPS)
    return [x, x * r * vecs[0]], []


def _ep_norm_bwd(acc, rows, vecs):
    dy = acc
    for extra in rows[2:]:
        dy = dy + extra
    x, dres = rows[0], rows[1]
    r = lax.rsqrt(jnp.mean(x * x, axis=-1, keepdims=True) + EPS)
    xh = x * r
    dxh = dy * vecs[0]
    dx = r * (dxh - xh * jnp.mean(dxh * xh, axis=-1, keepdims=True)) + dres
    return [dx, dx], [jnp.sum(dy * xh, axis=0, keepdims=True)]


def _ep_loss(acc, rows, vecs):
    x = acc + rows[0]
    d = x.shape[-1]
    r = lax.rsqrt(jnp.mean(x * x, axis=-1, keepdims=True) + EPS)
    xh = x * r
    err = xh * vecs[0] - rows[1]
    loss = jnp.zeros((1, 128), F32) + 0.5 * jnp.sum(jnp.mean(err * err, axis=-1, keepdims=True))
    dy = err * (1.0 / d)
    dxh = dy * vecs[0]
    dx = r * (dxh - xh * jnp.mean(dxh * xh, axis=-1, keepdims=True))
    return [dx, dx], [loss, jnp.sum(dy * xh, axis=0, keepdims=True)]


def _rms_fwd(x, g, *, name):
    s, d = x.shape
    ts = _rows(s)

    def body(x_ref, g_ref, o_ref):
        xf = x_ref[...]
        r = lax.rsqrt(jnp.mean(xf * xf, axis=-1, keepdims=True) + EPS)
        o_ref[...] = (xf * r * g_ref[...]).astype(o_ref.dtype)

    return pl.pallas_call(
        body,
        name=name,
        grid=(s // ts,),
        in_specs=[pl.BlockSpec((ts, d), lambda i: (i, 0)), pl.BlockSpec((1, d), lambda i: (0, 0))],
        out_specs=pl.BlockSpec((ts, d), lambda i: (i, 0)),
        out_shape=jax.ShapeDtypeStruct((s, d), _CD),
        compiler_params=_cp(("parallel",)),
    )(x, g)


def _rms_bwd(x, g, dy, dres=None, *, name, want_dx=True, want_lo=True):
    s, d = x.shape
    ts = _rows(s)
    has_res = dres is not None

    def body(*refs):
        x_ref, g_ref, dy_ref = refs[:3]
        pos = 3
        dres_ref = refs[pos] if has_res else None
        pos += has_res
        dx_ref = refs[pos] if want_dx else None
        pos += want_dx
        lo_ref = refs[pos] if want_lo else None
        pos += want_lo
        dg_ref = refs[pos]
        xf = x_ref[...]
        r = lax.rsqrt(jnp.mean(xf * xf, axis=-1, keepdims=True) + EPS)
        xh = xf * r
        dyf = dy_ref[...]
        part = jnp.sum(dyf * xh, axis=0, keepdims=True)

        @pl.when(pl.program_id(0) == 0)
        def _():
            dg_ref[...] = part

        @pl.when(pl.program_id(0) > 0)
        def _():
            dg_ref[...] += part

        if want_dx or want_lo:
            dxh = dyf * g_ref[...]
            dx = r * (dxh - xh * jnp.mean(dxh * xh, axis=-1, keepdims=True))
            if has_res:
                dx = dx + dres_ref[...]
            if want_dx:
                dx_ref[...] = dx
            if want_lo:
                lo_ref[...] = dx.astype(lo_ref.dtype)

    tile = pl.BlockSpec((ts, d), lambda i: (i, 0))
    vec = pl.BlockSpec((1, d), lambda i: (0, 0))
    out_specs, out_shape = [], []
    if want_dx:
        out_specs.append(tile)
        out_shape.append(jax.ShapeDtypeStruct((s, d), F32))
    if want_lo:
        out_specs.append(tile)
        out_shape.append(jax.ShapeDtypeStruct((s, d), _CD))
    out_specs.append(vec)
    out_shape.append(jax.ShapeDtypeStruct((1, d), F32))
    return pl.pallas_call(
        body,
        name=name,
        grid=(s // ts,),
        in_specs=[tile, vec, tile] + ([tile] if has_res else []),
        out_specs=out_specs,
        out_shape=out_shape,
        compiler_params=_cp(("arbitrary",)),
    )(x, g, dy, *([dres] if has_res else []))


def _final_loss(x3, g, tgt, *, name):
    s, d = x3.shape
    ts = _rows(s)

    def body(x_ref, g_ref, t_ref, dx_ref, lo_ref, loss_ref, dg_ref):
        xf = x_ref[...]
        r = lax.rsqrt(jnp.mean(xf * xf, axis=-1, keepdims=True) + EPS)
        xh = xf * r
        gg = g_ref[...]
        err = xh * gg - t_ref[...]
        lpart = jnp.zeros((1, 128), F32) + 0.5 * jnp.sum(jnp.mean(err * err, axis=-1, keepdims=True))
        dy = err * (1.0 / d)
        gpart = jnp.sum(dy * xh, axis=0, keepdims=True)

        @pl.when(pl.program_id(0) == 0)
        def _():
            loss_ref[...] = lpart
            dg_ref[...] = gpart

        @pl.when(pl.program_id(0) > 0)
        def _():
            loss_ref[...] += lpart
            dg_ref[...] += gpart

        dxh = dy * gg
        dx = r * (dxh - xh * jnp.mean(dxh * xh, axis=-1, keepdims=True))
        dx_ref[...] = dx
        lo_ref[...] = dx.astype(lo_ref.dtype)

    tile = pl.BlockSpec((ts, d), lambda i: (i, 0))
    vec = pl.BlockSpec((1, d), lambda i: (0, 0))
    return pl.pallas_call(
        body,
        name=name,
        grid=(s // ts,),
        in_specs=[tile, vec, tile],
        out_specs=[tile, tile, pl.BlockSpec((1, 128), lambda i: (0, 0)), vec],
        out_shape=[
            jax.ShapeDtypeStruct((s, d), F32),
            jax.ShapeDtypeStruct((s, d), _CD),
            jax.ShapeDtypeStruct((1, 128), F32),
            jax.ShapeDtypeStruct((1, d), F32),
        ],
        compiler_params=_cp(("arbitrary",)),
    )(x3, g, tgt)


def _chunk_scan(v, row_in_chunk, suffix):
    t = v.shape[0]
    step = 1
    while step < GLA_CHUNK:
        if suffix:
            v = v + jnp.where(row_in_chunk < GLA_CHUNK - step, pltpu.roll(v, t - step, 0), 0.0)
        else:
            v = v + jnp.where(row_in_chunk >= step, pltpu.roll(v, step, 0), 0.0)
        step *= 2
    return v


def _gate_pre(lr, w_ref, b_ref):
    return _dot(lr, w_ref[...]) + b_ref[...]


def _gate_fwd(z, waf, wab, baf, bab, *, name):
    s = z.shape[0]
    ts = _rows(s)

    def body(lr_ref, waf_ref, wab_ref, baf_ref, bab_ref, bf_ref, bb_ref):
        lr = lr_ref[...]
        ric = lax.broadcasted_iota(jnp.int32, (ts, GLA_K_TOTAL), 0) & (GLA_CHUNK - 1)
        for w_ref, b_ref, o_ref, suffix in ((waf_ref, baf_ref, bf_ref, False), (wab_ref, bab_ref, bb_ref, True)):
            pre = _gate_pre(lr, w_ref, b_ref)
            la = (jnp.minimum(pre, 0.0) - jnp.log(1.0 + jnp.exp(-jnp.abs(pre)))) * GLA_GATE_SCALE
            o_ref[...] = _chunk_scan(la, ric, suffix)

    wspec = pl.BlockSpec((128, GLA_K_TOTAL), lambda i: (0, 0))
    bspec = pl.BlockSpec((1, GLA_K_TOTAL), lambda i: (0, 0))
    tile = pl.BlockSpec((ts, GLA_K_TOTAL), lambda i: (i, 0))
    return pl.pallas_call(
        body,
        name=name,
        grid=(s // ts,),
        in_specs=[pl.BlockSpec((ts, 128), lambda i: (i, LR_COL // 128)), wspec, wspec, bspec, bspec],
        out_specs=[tile, tile],
        out_shape=[jax.ShapeDtypeStruct((s, GLA_K_TOTAL), F32)] * 2,
        compiler_params=_cp(("parallel",)),
    )(z, waf, wab, baf, bab)


def _gate_bwd(z, waf, wab, baf, bab, dbf, dbb, dqkv_f, dqkv_b, *, name):
    s = z.shape[0]
    ts = _rows(s)

    def body(lr_ref, waf_ref, wab_ref, baf_ref, bab_ref, dbf_ref, dbb_ref, gf_ref, gb_ref, dzb_ref, dwf_ref, dwb_ref, dbaf_ref, dbab_ref):
        lr = lr_ref[...]
        ric = lax.broadcasted_iota(jnp.int32, (ts, GLA_K_TOTAL), 0) & (GLA_CHUNK - 1)
        first = pl.program_id(0) == 0
        dlr = None
        for w_ref, b_ref, db_ref, dw_ref, dbias_ref, suffix in (
            (waf_ref, baf_ref, dbf_ref, dwf_ref, dbaf_ref, True),
            (wab_ref, bab_ref, dbb_ref, dwb_ref, dbab_ref, False),
        ):
            pre = _gate_pre(lr, w_ref, b_ref)
            dla = _chunk_scan(db_ref[...], ric, suffix)
            dpre = dla * GLA_GATE_SCALE * _sigmoid(-pre)
            part = _dot_nt(dpre, w_ref[...])
            dlr = part if dlr is None else dlr + part
            dw = _dot_tn(lr, dpre)
            dbias = jnp.sum(dpre, axis=0, keepdims=True)

            @pl.when(first)
            def _():
                dw_ref[...] = dw
                dbias_ref[...] = dbias

            @pl.when(jnp.logical_not(first))
            def _():
                dw_ref[...] += dw
                dbias_ref[...] += dbias

        dzb_ref[...] = jnp.concatenate([gf_ref[...] + gb_ref[...], dlr], axis=1).astype(dzb_ref.dtype)

    wspec = pl.BlockSpec((128, GLA_K_TOTAL), lambda i: (0, 0))
    bspec = pl.BlockSpec((1, GLA_K_TOTAL), lambda i: (0, 0))
    tile = pl.BlockSpec((ts, GLA_K_TOTAL), lambda i: (i, 0))
    wide = pl.BlockSpec((ts, 2 * GLA_K_TOTAL + GLA_V_TOTAL), lambda i: (i, 0))
    return pl.pallas_call(
        body,
        name=name,
        grid=(s // ts,),
        in_specs=[pl.BlockSpec((ts, 128), lambda i: (i, LR_COL // 128)), wspec, wspec, bspec, bspec, tile, tile, wide, wide],
        out_specs=[pl.BlockSpec((ts, ZB_COLS), lambda i: (i, 0)), wspec, wspec, bspec, bspec],
        out_shape=[
            jax.ShapeDtypeStruct((s, ZB_COLS), _CD),
            jax.ShapeDtypeStruct((128, GLA_K_TOTAL), F32),
            jax.ShapeDtypeStruct((128, GLA_K_TOTAL), F32),
            jax.ShapeDtypeStruct((1, GLA_K_TOTAL), F32),
            jax.ShapeDtypeStruct((1, GLA_K_TOTAL), F32),
        ],
        compiler_params=_cp(("arbitrary",)),
    )(z, waf, wab, baf, bab, dbf, dbb, dqkv_f, dqkv_b)


def _gla_masks(rev):
    lane_head = lax.broadcasted_iota(jnp.int32, (1, GLA_K_TOTAL), 1) >> 6
    head_masks = [lane_head == h for h in range(GLA_HEADS)]
    st_rows = lax.broadcasted_iota(jnp.int32, (GLA_V_TOTAL, GLA_K_TOTAL), 0) >> 7
    st_lanes = lax.broadcasted_iota(jnp.int32, (GLA_V_TOTAL, GLA_K_TOTAL), 1) >> 6
    block_mask = st_rows == st_lanes
    t = lax.broadcasted_iota(jnp.int32, (GLA_HEADS * GLA_CHUNK, GLA_CHUNK), 0) & (GLA_CHUNK - 1)
    u = lax.broadcasted_iota(jnp.int32, (GLA_HEADS * GLA_CHUNK, GLA_CHUNK), 1)
    tri = (u > t) if rev else (u <= t)
    row = lax.broadcasted_iota(jnp.int32, (GLA_CHUNK, GLA_K_TOTAL), 0)
    total_row = row == (0 if rev else GLA_CHUNK - 1)
    return head_masks, block_mask, tri, total_row


def _gla_chunk_terms(q_ref, k_ref, v_ref, b_ref, rows, head_masks, tri, total_row):
    q = q_ref[rows, :] * (GLA_DK**-0.5)
    k = k_ref[rows, :]
    v = v_ref[rows, :]
    b = b_ref[rows, :]
    eb = jnp.exp(b)
    enb = jnp.exp(-b)
    g = jnp.sum(jnp.where(total_row, b, 0.0), axis=0, keepdims=True)
    egb = jnp.exp(g - b)
    qt = q * eb
    kt = k * enb
    kh = k * egb
    q_heads = jnp.concatenate([jnp.where(m, qt, 0.0) for m in head_masks], axis=0)
    attn = jnp.where(tri, _dot_nt(q_heads, kt), 0.0)
    return v, eb, enb, egb, jnp.exp(g), qt, kt, kh, attn


def _gla_specs(s, tb, rev_blocks):
    nb = s // tb
    rb = (lambda i: nb - 1 - i) if rev_blocks else (lambda i: i)
    q_spec = pl.BlockSpec((tb, GLA_K_TOTAL), lambda i: (rb(i), 0))
    k_spec = pl.BlockSpec((tb, GLA_K_TOTAL), lambda i: (rb(i), 1))
    v_spec = pl.BlockSpec((tb, GLA_V_TOTAL), lambda i: (rb(i), 1))
    b_spec = pl.BlockSpec((tb, GLA_K_TOTAL), lambda i: (rb(i), 0))
    o_spec = pl.BlockSpec((tb, GLA_V_TOTAL), lambda i: (rb(i), 0))
    st_spec = pl.BlockSpec((tb // GLA_CHUNK, GLA_DV, GLA_K_TOTAL), lambda i: (rb(i), 0, 0))
    return nb, q_spec, k_spec, v_spec, b_spec, o_spec, st_spec


def _gla_fwd_chunk(cidx, q_ref, k_ref, v_ref, b_ref, o_ref, sv_ref, st_ref, masks):
    head_masks, block_mask, tri, total_row = masks
    rows = pl.ds(pl.multiple_of(cidx * GLA_CHUNK, GLA_CHUNK), GLA_CHUNK)
    v, _, _, _, eg, qt, _, kh, attn = _gla_chunk_terms(q_ref, k_ref, v_ref, b_ref, rows, head_masks, tri, total_row)
    o = jnp.concatenate(
        [_dot(attn[GLA_CHUNK * h : GLA_CHUNK * (h + 1)], v[:, GLA_DV * h : GLA_DV * (h + 1)]) for h in range(GLA_HEADS)], axis=1
    )
    st = st_ref[...]
    o_ref[rows, :] = o + _dot_nt(qt, st)
    sv_ref[cidx] = st[0:128] + st[128:256] + st[256:384] + st[384:512]
    st_ref[...] = st * eg + jnp.where(block_mask, _dot_tn(v, kh), 0.0)


def _gla_fwd(z, b_f, b_b, *, name):
    s = z.shape[0]
    tb = _rows(s)
    cpb = tb // GLA_CHUNK
    nb, qf, kf, vf, bf, of, sf = _gla_specs(s, tb, False)
    _, qr, kr, vr, br, orr, sr = _gla_specs(s, tb, True)

    def body(qf_ref, kf_ref, vf_ref, bf_ref, qr_ref, kr_ref, vr_ref, br_ref, of_ref, svf_ref, or_ref, svr_ref, stf_ref, str_ref):
        masks_f, masks_r = _gla_masks(False), _gla_masks(True)

        @pl.when(pl.program_id(0) == 0)
        def _():
            stf_ref[...] = jnp.zeros_like(stf_ref)
            str_ref[...] = jnp.zeros_like(str_ref)

        def chunk(ci, carry):
            _gla_fwd_chunk(ci, qf_ref, kf_ref, vf_ref, bf_ref, of_ref, svf_ref, stf_ref, masks_f)
            _gla_fwd_chunk(cpb - 1 - ci, qr_ref, kr_ref, vr_ref, br_ref, or_ref, svr_ref, str_ref, masks_r)
            return carry

        lax.fori_loop(0, cpb, chunk, 0)

    o_shape = jax.ShapeDtypeStruct((s, GLA_V_TOTAL), F32)
    st_shape = jax.ShapeDtypeStruct((s // GLA_CHUNK, GLA_DV, GLA_K_TOTAL), F32)
    return pl.pallas_call(
        body,
        name=name,
        grid=(nb,),
        in_specs=[qf, kf, vf, bf, qr, kr, vr, br],
        out_specs=[of, sf, orr, sr],
        out_shape=[o_shape, st_shape, o_shape, st_shape],
        scratch_shapes=[pltpu.VMEM((GLA_V_TOTAL, GLA_K_TOTAL), F32)] * 2,
        compiler_params=_cp(("arbitrary",)),
    )(z, z, z, b_f, z, z, z, b_b)


def _gla_bwd_chunk(cidx, q_ref, k_ref, v_ref, b_ref, do_ref, sv_ref, dqkv_ref, db_ref, dst_ref, masks):
    head_masks, block_mask, tri, total_row = masks
    rows = pl.ds(pl.multiple_of(cidx * GLA_CHUNK, GLA_CHUNK), GLA_CHUNK)
    v, eb, enb, egb, eg, qt, kt, kh, attn = _gla_chunk_terms(q_ref, k_ref, v_ref, b_ref, rows, head_masks, tri, total_row)
    do_c = do_ref[rows, :]
    saved = sv_ref[cidx]
    st = jnp.where(block_mask, jnp.concatenate([saved] * GLA_HEADS, axis=0), 0.0)
    dst = dst_ref[...]
    hs = lambda a, h: a[GLA_CHUNK * h : GLA_CHUNK * (h + 1)]
    vs = lambda a, h: a[:, GLA_DV * h : GLA_DV * (h + 1)]
    dattn = jnp.concatenate([_dot_nt(vs(do_c, h), vs(v, h)) for h in range(GLA_HEADS)], axis=0)
    dattn = jnp.where(tri, dattn, 0.0)
    dv = jnp.concatenate([_dot_tn(hs(attn, h), vs(do_c, h)) for h in range(GLA_HEADS)], axis=1) + _dot_nt(kh, dst)
    dqt = _dot(do_c, st)
    dkt = jnp.zeros_like(dqt)
    for h in range(GLA_HEADS):
        dqt = dqt + jnp.where(head_masks[h], _dot(hs(dattn, h), kt), 0.0)
        dkt = dkt + jnp.where(head_masks[h], _dot_tn(hs(dattn, h), qt), 0.0)
    dkh = _dot(v, dst)
    dg = jnp.sum(dkh * kh, axis=0, keepdims=True) + jnp.sum(dst * st, axis=0, keepdims=True) * eg
    db = dqt * qt - dkt * kt - dkh * kh + jnp.where(total_row, dg, 0.0)
    dq = dqt * eb * (GLA_DK**-0.5)
    dk = dkt * enb + dkh * egb
    dqkv_ref[rows, :] = jnp.concatenate([dq, dk, dv], axis=1)
    db_ref[rows, :] = db
    dst_ref[...] = dst * eg + jnp.where(block_mask, _dot_tn(do_c, qt), 0.0)


def _gla_bwd(z, b_f, b_b, do, st_f, st_b, *, name):
    s = z.shape[0]
    tb = _rows(s)
    cpb = tb // GLA_CHUNK
    wide = 2 * GLA_K_TOTAL + GLA_V_TOTAL
    nb, qf, kf, vf, bf, of, sf = _gla_specs(s, tb, True)
    _, qr, kr, vr, br, orr, sr = _gla_specs(s, tb, False)
    gf = pl.BlockSpec((tb, wide), lambda i: (nb - 1 - i, 0))
    gr = pl.BlockSpec((tb, wide), lambda i: (i, 0))

    def body(qf_ref, kf_ref, vf_ref, bf_ref, dof_ref, svf_ref, qr_ref, kr_ref, vr_ref, br_ref, dor_ref, svr_ref,
             gf_ref, dbf_ref, gr_ref, dbr_ref, dstf_ref, dstr_ref):
        masks_f, masks_r = _gla_masks(False), _gla_masks(True)

        @pl.when(pl.program_id(0) == 0)
        def _():
            dstf_ref[...] = jnp.zeros_like(dstf_ref)
            dstr_ref[...] = jnp.zeros_like(dstr_ref)

        def chunk(ci, carry):
            _gla_bwd_chunk(cpb - 1 - ci, qf_ref, kf_ref, vf_ref, bf_ref, dof_ref, svf_ref, gf_ref, dbf_ref, dstf_ref, masks_f)
            _gla_bwd_chunk(ci, qr_ref, kr_ref, vr_ref, br_ref, dor_ref, svr_ref, gr_ref, dbr_ref, dstr_ref, masks_r)
            return carry

        lax.fori_loop(0, cpb, chunk, 0)

    g_shape = jax.ShapeDtypeStruct((s, wide), F32)
    db_shape = jax.ShapeDtypeStruct((s, GLA_K_TOTAL), F32)
    return pl.pallas_call(
        body,
        name=name,
        grid=(nb,),
        in_specs=[qf, kf, vf, bf, of, sf, qr, kr, vr, br, orr, sr],
        out_specs=[gf, bf, gr, br],
        out_shape=[g_shape, db_shape, g_shape, db_shape],
        scratch_shapes=[pltpu.VMEM((GLA_V_TOTAL, GLA_K_TOTAL), F32)] * 2,
        compiler_params=_cp(("arbitrary",)),
    )(z, z, z, b_f, do, st_f, z, z, z, b_b, do, st_b)


HALO = 8


def _halo_specs(s, ts, width, col):
    last = s // HALO - 1
    per = ts // HALO
    prev = pl.BlockSpec((HALO, width), lambda i: (jnp.maximum(i * per - 1, 0), col))
    nxt = pl.BlockSpec((HALO, width), lambda i: (jnp.minimum((i + 1) * per, last), col))
    return prev, nxt


def _group_ones():
    r = lax.broadcasted_iota(jnp.int32, (CONV_WIDTH, CONV_WIDTH), 0) >> 6
    c = lax.broadcasted_iota(jnp.int32, (CONV_WIDTH, CONV_WIDTH), 1) >> 6
    return (r == c).astype(BF16)


def _conv_terms(cc_ext, cu_ext, cw, valid):
    n = cc_ext.shape[0]
    hc = jnp.where(valid, cc_ext * cu_ext, 0.0)
    hc_prev = pltpu.roll(hc, 1, 0)
    hc_next = pltpu.roll(hc, n - 1, 0)
    conv = cw[0:1] * hc_prev + cw[1:2] * hc + cw[2:3] * hc_next
    return hc, hc_prev, hc_next, conv


def _ext(prev_ref, cur_ref, next_ref):
    return jnp.concatenate([prev_ref[...], cur_ref[...], next_ref[...]], axis=0)


def _valid_rows(ts, s):
    row = lax.broadcasted_iota(jnp.int32, (ts + 2 * HALO, 1), 0) + (pl.program_id(0) * ts - HALO)
    return (row >= 0) & (row < s)


def _head_norm(o, gn):
    out = []
    for h in range(GLA_HEADS):
        oh = o[:, GLA_DV * h : GLA_DV * (h + 1)]
        r = lax.rsqrt(jnp.mean(oh * oh, axis=-1, keepdims=True) + EPS)
        out.append((oh * r, r))
    return out


def _mix_fwd(z, o_f, o_b, conv_w, conv_norm, gla_norm, *, name):
    s = z.shape[0]
    ts = _rows(s)
    cprev, cnext = _halo_specs(s, ts, CONV_WIDTH, 1)
    uprev, unext = _halo_specs(s, ts, CONV_WIDTH, 2)

    def body(cb_ref, cc_ref, cu_ref, ccp_ref, ccn_ref, cup_ref, cun_ref, g_ref, of_ref, ob_ref, cw_ref, cn_ref, gn_ref, y_ref):
        valid = _valid_rows(ts, s)
        _, _, _, conv = _conv_terms(_ext(ccp_ref, cc_ref, ccn_ref), _ext(cup_ref, cu_ref, cun_ref), cw_ref[...], valid)
        yc = cb_ref[...] * conv[HALO : HALO + ts]
        ms = _dot_split(yc * yc, _group_ones()) * (1.0 / CONV_GROUP)
        y_conv = yc * lax.rsqrt(ms + EPS) * cn_ref[...]
        gate = g_ref[...]
        silu = gate * _sigmoid(gate)
        gn = gn_ref[...]
        y_gla = jnp.concatenate([oh * gn for oh, _ in _head_norm(of_ref[...] + ob_ref[...], gn)], axis=1) * silu
        y_ref[...] = jnp.concatenate([y_conv, y_gla], axis=1).astype(y_ref.dtype)

    col = lambda c, w=CONV_WIDTH: pl.BlockSpec((ts, w), lambda i: (i, c))
    return pl.pallas_call(
        body,
        name=name,
        grid=(s // ts,),
        in_specs=[col(0), col(1), col(2), cprev, cnext, uprev, unext, col(3), col(0), col(0),
                  pl.BlockSpec((CONV_K, CONV_WIDTH), lambda i: (0, 0)), pl.BlockSpec((1, CONV_WIDTH), lambda i: (0, 0)),
                  pl.BlockSpec((1, GLA_DV), lambda i: (0, 0))],
        out_specs=pl.BlockSpec((ts, D_MODEL), lambda i: (i, 0)),
        out_shape=jax.ShapeDtypeStruct((s, D_MODEL), _CD),
        compiler_params=_cp(("parallel",)),
    )(z, z, z, z, z, z, z, z, o_f, o_b, conv_w, conv_norm, gla_norm)


def _mix_bwd(z, o_f, o_b, dy, conv_w, conv_norm, gla_norm, *, name):
    s = z.shape[0]
    ts = _rows(s)
    halos = [_halo_specs(s, ts, CONV_WIDTH, c) for c in (0, 1, 2)]
    dprev, dnext = _halo_specs(s, ts, CONV_WIDTH, 0)

    def body(cb_ref, cc_ref, cu_ref, cbp_ref, cbn_ref, ccp_ref, ccn_ref, cup_ref, cun_ref, g_ref, of_ref, ob_ref,
             dyc_ref, dyg_ref, dyp_ref, dyn_ref, cw_ref, cn_ref, gn_ref, dza_ref, do_ref, dcw_ref, dcn_ref, dgn_ref):
        n = ts + 2 * HALO
        valid = _valid_rows(ts, s)
        cw = cw_ref[...]
        cn = cn_ref[...]
        ones = _group_ones()
        cb = _ext(cbp_ref, cb_ref, cbn_ref)
        cc = _ext(ccp_ref, cc_ref, ccn_ref)
        cu = _ext(cup_ref, cu_ref, cun_ref)
        dy = _ext(dyp_ref, dyc_ref, dyn_ref)
        hc, hc_prev, hc_next, conv = _conv_terms(cc, cu, cw, valid)
        yc = cb * conv
        r = lax.rsqrt(_dot_split(yc * yc, ones) * (1.0 / CONV_GROUP) + EPS)
        yh = yc * r
        dyh = dy * cn
        dyc = r * (dyh - yh * (_dot_split(dyh * yh, ones) * (1.0 / CONV_GROUP)))
        dconv = jnp.where(valid, dyc * cb, 0.0)
        dhc = cw[0:1] * pltpu.roll(dconv, n - 1, 0) + cw[1:2] * dconv + cw[2:3] * pltpu.roll(dconv, 1, 0)
        mid = lambda a: a[HALO : HALO + ts]
        dza_ref[:, 0 : 3 * CONV_WIDTH] = jnp.concatenate([mid(dyc * conv), mid(dhc * cu), mid(dhc * cc)], axis=1).astype(dza_ref.dtype)
        dconv_m = mid(dconv)
        colsum = lambda a: jnp.sum(a, axis=0, keepdims=True)
        dcw = jnp.concatenate([colsum(dconv_m * mid(hc_prev)), colsum(dconv_m * mid(hc)), colsum(dconv_m * mid(hc_next))], axis=0)
        dcn = colsum(mid(dy * yh))

        gate = g_ref[...]
        sg = _sigmoid(gate)
        silu = gate * sg
        gn = gn_ref[...]
        dyg = dyg_ref[...]
        don = dyg * silu
        heads = _head_norm(of_ref[...] + ob_ref[...], gn)
        on = jnp.concatenate([oh * gn for oh, _ in heads], axis=1)
        dza_ref[:, 3 * CONV_WIDTH : ZA_COLS] = (dyg * on * (sg * (1.0 + gate * (1.0 - sg)))).astype(dza_ref.dtype)
        dgn = jnp.zeros((1, GLA_DV), F32)
        dos = []
        for h, (oh, rh) in enumerate(heads):
            donh = don[:, GLA_DV * h : GLA_DV * (h + 1)]
            dgn = dgn + colsum(donh * oh)
            doh = donh * gn
            dos.append(rh * (doh - oh * jnp.mean(doh * oh, axis=-1, keepdims=True)))
        do_ref[...] = jnp.concatenate(dos, axis=1)

        first = pl.program_id(0) == 0

        @pl.when(first)
        def _():
            dcw_ref[...] = dcw
            dcn_ref[...] = dcn
            dgn_ref[...] = dgn

        @pl.when(jnp.logical_not(first))
        def _():
            dcw_ref[...] += dcw
            dcn_ref[...] += dcn
            dgn_ref[...] += dgn

    col = lambda c, w=CONV_WIDTH: pl.BlockSpec((ts, w), lambda i: (i, c))
    cw_spec = pl.BlockSpec((CONV_K, CONV_WIDTH), lambda i: (0, 0))
    cn_spec = pl.BlockSpec((1, CONV_WIDTH), lambda i: (0, 0))
    gn_spec = pl.BlockSpec((1, GLA_DV), lambda i: (0, 0))
    return pl.pallas_call(
        body,
        name=name,
        grid=(s // ts,),
        in_specs=[col(0), col(1), col(2), halos[0][0], halos[0][1], halos[1][0], halos[1][1], halos[2][0], halos[2][1],
                  col(3), col(0), col(0), col(0), col(1), dprev, dnext, cw_spec, cn_spec, gn_spec],
        out_specs=[pl.BlockSpec((ts, ZA_COLS), lambda i: (i, 0)), col(0), cw_spec, cn_spec, gn_spec],
        out_shape=[
            jax.ShapeDtypeStruct((s, ZA_COLS), _CD),
            jax.ShapeDtypeStruct((s, GLA_V_TOTAL), F32),
            jax.ShapeDtypeStruct((CONV_K, CONV_WIDTH), F32),
            jax.ShapeDtypeStruct((1, CONV_WIDTH), F32),
            jax.ShapeDtypeStruct((1, GLA_DV), F32),
        ],
        compiler_params=_cp(("arbitrary",)),
    )(z, z, z, z, z, z, z, z, z, z, o_f, o_b, dy, dy, dy, dy, conv_w, conv_norm, gla_norm)


def _xa_probs(q_ref, kv_ref, h):
    qh = q_ref[:, XA_HEAD_DIM * h : XA_HEAD_DIM * (h + 1)]
    kh = kv_ref[:, XA_HEAD_DIM * h : XA_HEAD_DIM * (h + 1)]
    vh = kv_ref[:, D_MODEL + XA_HEAD_DIM * h : D_MODEL + XA_HEAD_DIM * (h + 1)]
    sc = _dot_nt(qh, kh) * (XA_HEAD_DIM**-0.5)
    e = jnp.exp(sc - jnp.max(sc, axis=-1, keepdims=True))
    return qh, kh, vh, e / jnp.sum(e, axis=-1, keepdims=True)


def _xattn_fwd(qx, kv, *, name):
    s = qx.shape[0]
    ts = _rows(s)

    def body(q_ref, kv_ref, o_ref):
        outs = []
        for h in range(XA_HEADS):
            _, _, vh, p = _xa_probs(q_ref, kv_ref, h)
            outs.append(_dot(p, vh))
        o_ref[...] = jnp.concatenate(outs, axis=1).astype(o_ref.dtype)

    return pl.pallas_call(
        body,
        name=name,
        grid=(s // ts,),
        in_specs=[pl.BlockSpec((ts, D_MODEL), lambda i: (i, 0)), pl.BlockSpec((N_MEM, 2 * D_MODEL), lambda i: (0, 0))],
        out_specs=pl.BlockSpec((ts, D_MODEL), lambda i: (i, 0)),
        out_shape=jax.ShapeDtypeStruct((s, D_MODEL), _CD),
        compiler_params=_cp(("parallel",)),
    )(qx, kv)


def _xattn_bwd(qx, kv, dox, *, name):
    s = qx.shape[0]
    ts = _rows(s)

    def body(q_ref, kv_ref, do_ref, dq_ref, dkv_ref):
        dqs, dks, dvs = [], [], []
        for h in range(XA_HEADS):
            qh, kh, vh, p = _xa_probs(q_ref, kv_ref, h)
            doh = do_ref[:, XA_HEAD_DIM * h : XA_HEAD_DIM * (h + 1)]
            dp = _dot_nt(doh, vh)
            ds = p * (dp - jnp.sum(dp * p, axis=-1, keepdims=True)) * (XA_HEAD_DIM**-0.5)
            dqs.append(_dot(ds, kh))
            dks.append(_dot_tn(ds, qh))
            dvs.append(_dot_tn(p, doh))
        dq_ref[...] = jnp.concatenate(dqs, axis=1).astype(dq_ref.dtype)
        dkv = jnp.concatenate(dks + dvs, axis=1)

        @pl.when(pl.program_id(0) == 0)
        def _():
            dkv_ref[...] = dkv

        @pl.when(pl.program_id(0) > 0)
        def _():
            dkv_ref[...] += dkv

    tile = pl.BlockSpec((ts, D_MODEL), lambda i: (i, 0))
    kv_spec = pl.BlockSpec((N_MEM, 2 * D_MODEL), lambda i: (0, 0))
    return pl.pallas_call(
        body,
        name=name,
        grid=(s // ts,),
        in_specs=[tile, kv_spec, tile],
        out_specs=[tile, kv_spec],
        out_shape=[jax.ShapeDtypeStruct((s, D_MODEL), _CD), jax.ShapeDtypeStruct((N_MEM, 2 * D_MODEL), F32)],
        compiler_params=_cp(("arbitrary",)),
    )(qx, kv, dox)


def _adamw_math(w, g, m, v):
    m = ADAM_B1 * m + (1.0 - ADAM_B1) * g
    v = ADAM_B2 * v + (1.0 - ADAM_B2) * (g * g)
    m_hat = m / (1.0 - ADAM_B1**ADAM_STEP)
    v_hat = v / (1.0 - ADAM_B2**ADAM_STEP)
    delta = -ADAM_LR * (m_hat / (jnp.sqrt(v_hat) + ADAM_EPS) + ADAM_WD * w)
    return delta, m, v


def _adamw(w, g, m, v, *, name):
    r, c = w.shape
    tr = _pick(r, (256, 128, 64, 32, 16, 8))

    def body(w_ref, g_ref, m_ref, v_ref, d_ref, nm_ref, nv_ref):
        d_ref[...], nm_ref[...], nv_ref[...] = _adamw_math(w_ref[...], g_ref[...], m_ref[...], v_ref[...])

    tile = pl.BlockSpec((tr, c), lambda i: (i, 0))
    return pl.pallas_call(
        body,
        name=name,
        grid=(r // tr,),
        in_specs=[tile] * 4,
        out_specs=[tile] * 3,
        out_shape=[jax.ShapeDtypeStruct((r, c), F32)] * 3,
        compiler_params=_cp(("parallel",)),
    )(w, g, m, v)


def _adamw_small(groups, *, name):
    n = len(groups)

    def body(*refs):
        ins, outs = refs[: 4 * n], refs[4 * n :]
        for i in range(n):
            w_ref, g_ref, m_ref, v_ref = ins[4 * i : 4 * i + 4]
            outs[3 * i][...], outs[3 * i + 1][...], outs[3 * i + 2][...] = _adamw_math(w_ref[...], g_ref[...], m_ref[...], v_ref[...])

    flat = [a for grp in groups for a in grp]
    vm = pl.BlockSpec(memory_space=pltpu.VMEM)
    res = pl.pallas_call(
        body,
        name=name,
        in_specs=[vm] * (4 * n),
        out_specs=[vm] * (3 * n),
        out_shape=[jax.ShapeDtypeStruct(grp[0].shape, F32) for grp in groups for _ in range(3)],
        compiler_params=_cp(),
    )(*flat)
    return [tuple(res[3 * i : 3 * i + 3]) for i in range(n)]


def _place():
    return lax.axis_index("x"), lax.axis_index("y"), lax.axis_index("c")


def _rel_chip(x, y, k):
    return (1 - x if k & 2 else x), (1 - y if k & 1 else y)


def _half(c, rh):
    return pl.ds(pl.multiple_of(c * rh, 16), rh)


def _gather_weights(pack):
    r, w = pack.shape
    rh = r // 2

    def body(p_ref, q_ref, send_sems, recv_sems):
        x, y, c = _place()
        j = 2 * x + y
        rows = _half(c, rh)

        def to_chip(k):
            cx, cy = _rel_chip(x, y, k)
            return pltpu.make_async_remote_copy(
                src_ref=p_ref.at[rows], dst_ref=q_ref.at[j, rows], send_sem=send_sems.at[k - 1], recv_sem=recv_sems.at[k - 1],
                device_id=(cx, cy, c), device_id_type=MESH)

        def to_sibling(k):
            cx, cy = _rel_chip(x, y, k)
            slot = q_ref.at[2 * cx + cy, rows]
            return pltpu.make_async_remote_copy(
                src_ref=slot, dst_ref=slot, send_sem=send_sems.at[2 + k], recv_sem=recv_sems.at[2 + k],
                device_id=(x, y, 1 - c), device_id_type=MESH)

        first = [to_chip(k) for k in range(1, N_CHIPS)]
        passed = [to_sibling(k) for k in range(1, N_CHIPS)]
        own = pltpu.make_async_remote_copy(
            src_ref=p_ref, dst_ref=q_ref.at[j], send_sem=send_sems.at[6], recv_sem=recv_sems.at[6],
            device_id=(x, y, 1 - c), device_id_type=MESH)
        for cp in first:
            cp.start()
        own.start()
        for cp, fw in zip(first, passed):
            cp.wait_recv()
            fw.start()
        for fw in passed:
            fw.wait_recv()
        own.wait_recv()
        for cp in first + passed + [own]:
            cp.wait_send()

    return pl.pallas_call(
        body,
        name="gather_weights",
        in_specs=[ANY],
        out_specs=ANY,
        out_shape=jax.ShapeDtypeStruct((N_CHIPS, r, w), pack.dtype),
        scratch_shapes=[pltpu.SemaphoreType.DMA((7,)), pltpu.SemaphoreType.DMA((7,))],
        compiler_params=pltpu.CompilerParams(has_side_effects=True),
    )(pack)


def _swap_halves(g):
    n, r, w = g.shape
    rh = r // 2

    def body(g_ref, o_ref, send_sem, recv_sem):
        x, y, c = _place()
        cp = pltpu.make_async_remote_copy(
            src_ref=g_ref.at[:, _half(1 - c, rh)], dst_ref=o_ref, send_sem=send_sem, recv_sem=recv_sem,
            device_id=(x, y, 1 - c), device_id_type=MESH)
        cp.start()
        cp.wait()

    return pl.pallas_call(
        body,
        name="grads_to_sibling",
        in_specs=[ANY],
        out_specs=ANY,
        out_shape=jax.ShapeDtypeStruct((n, rh, w), g.dtype),
        scratch_shapes=[pltpu.SemaphoreType.DMA, pltpu.SemaphoreType.DMA],
        compiler_params=pltpu.CompilerParams(has_side_effects=True),
    )(g)


def _chip_sums(g, got, where):
    n, r, w = g.shape
    rh = r // 2
    nt = rh // PACK_TILE

    def body(where_ref, g_ref, got_ref, o_ref):
        o_ref[...] = (g_ref[...] + got_ref[...]).astype(o_ref.dtype)

    return pl.pallas_call(
        body,
        name="chip_sums",
        grid_spec=pltpu.PrefetchScalarGridSpec(
            num_scalar_prefetch=1,
            grid=(n, nt),
            in_specs=[pl.BlockSpec((1, PACK_TILE, w), lambda a, i, wh: (a, wh[0] * nt + i, 0)),
                      pl.BlockSpec((1, PACK_TILE, w), lambda a, i, wh: (a, i, 0))],
            out_specs=pl.BlockSpec((1, PACK_TILE, w), lambda a, i, wh: (a, i, 0)),
        ),
        out_shape=jax.ShapeDtypeStruct((n, rh, w), _TD),
        compiler_params=_cp(("parallel", "parallel")),
    )(where, g, got)


def _exchange_chip_sums(h):
    n, rh, w = h.shape

    def body(h_ref, o_ref, send_sems, recv_sems):
        x, y, c = _place()
        j = 2 * x + y
        copies = []
        for k in range(1, N_CHIPS):
            cx, cy = _rel_chip(x, y, k)
            copies.append(pltpu.make_async_remote_copy(
                src_ref=h_ref.at[2 * cx + cy], dst_ref=o_ref.at[k - 1], send_sem=send_sems.at[k - 1], recv_sem=recv_sems.at[k - 1],
                device_id=(cx, cy, c), device_id_type=MESH))
        for cp in copies:
            cp.start()
        for cp in copies:
            cp.wait()

    return pl.pallas_call(
        body,
        name="chip_sums_exchange",
        in_specs=[ANY],
        out_specs=ANY,
        out_shape=jax.ShapeDtypeStruct((N_CHIPS - 1, rh, w), h.dtype),
        scratch_shapes=[pltpu.SemaphoreType.DMA((3,)), pltpu.SemaphoreType.DMA((3,))],
        compiler_params=pltpu.CompilerParams(has_side_effects=True),
    )(h)


def _shard_sum(g, got, others, where):
    n, r, w = g.shape
    rh = r // 2
    nt = rh // PACK_TILE

    def body(where_ref, g_ref, got_ref, oth_ref, o_ref):
        acc = g_ref[0] + got_ref[0]
        for k in range(N_CHIPS - 1):
            acc = acc + oth_ref[k].astype(F32)
        o_ref[...] = acc

    return pl.pallas_call(
        body,
        name="shard_sum",
        grid_spec=pltpu.PrefetchScalarGridSpec(
            num_scalar_prefetch=1,
            grid=(nt,),
            in_specs=[pl.BlockSpec((1, PACK_TILE, w), lambda i, wh: (wh[1], wh[0] * nt + i, 0)),
                      pl.BlockSpec((1, PACK_TILE, w), lambda i, wh: (wh[1], i, 0)),
                      pl.BlockSpec((N_CHIPS - 1, PACK_TILE, w), lambda i, wh: (0, i, 0))],
            out_specs=pl.BlockSpec((PACK_TILE, w), lambda i, wh: (i, 0)),
        ),
        out_shape=jax.ShapeDtypeStruct((rh, w), F32),
        compiler_params=_cp(("parallel",)),
    )(where, g, got, others)


def _join_halves(e):
    rh, w = e.shape

    def body(e_ref, o_ref, send_sem, recv_sem, local_sem):
        x, y, c = _place()
        rows = _half(c, rh)
        mine = pltpu.make_async_copy(e_ref, o_ref.at[rows], local_sem)
        mine.start()
        cp = pltpu.make_async_remote_copy(
            src_ref=e_ref, dst_ref=o_ref.at[rows], send_sem=send_sem, recv_sem=recv_sem, device_id=(x, y, 1 - c), device_id_type=MESH)
        cp.start()
        cp.wait()
        mine.wait()

    return pl.pallas_call(
        body,
        name="shard_to_sibling",
        in_specs=[ANY],
        out_specs=ANY,
        out_shape=jax.ShapeDtypeStruct((2 * rh, w), e.dtype),
        scratch_shapes=[pltpu.SemaphoreType.DMA, pltpu.SemaphoreType.DMA, pltpu.SemaphoreType.DMA],
        compiler_params=pltpu.CompilerParams(has_side_effects=True),
    )(e)


HBM = pl.BlockSpec(memory_space=pltpu.HBM)
SEM = pl.BlockSpec(memory_space=pltpu.SEMAPHORE)
EFFECT = pltpu.SideEffectType.DATAFLOW_SIDE_EFFECTING


def _in_hbm(a):
    return pltpu.with_memory_space_constraint(a, pltpu.HBM)


def _gather_copies(p_ref, land_ref, send_sems, recv_sems):
    rh = p_ref.shape[0] // 2
    x, y, c = _place()
    rows = _half(c, rh)
    copies = []
    for k in range(1, N_CHIPS):
        cx, cy = _rel_chip(x, y, k)
        copies.append(pltpu.make_async_remote_copy(
            src_ref=p_ref.at[rows], dst_ref=land_ref.at[2 * x + y, rows], send_sem=send_sems.at[k - 1], recv_sem=recv_sems.at[k - 1],
            device_id=(cx, cy, c), device_id_type=MESH))
    copies.append(pltpu.make_async_remote_copy(
        src_ref=p_ref, dst_ref=land_ref.at[2 * x + y], send_sem=send_sems.at[N_CHIPS - 1], recv_sem=recv_sems.at[N_CHIPS - 1],
        device_id=(x, y, 1 - c), device_id_type=MESH))
    return copies


def _gather_start(pack, after, *, name):
    r, w = pack.shape

    def body(p_ref, land_ref, after_ref, send_sems, recv_sems, p_thru, land_thru, token):
        for cp in _gather_copies(p_ref, land_ref, send_sems, recv_sems):
            cp.start()
        token[...] = jnp.zeros_like(token)

    return pl.pallas_call(
        body,
        name=name,
        out_shape=(pltpu.SemaphoreType.DMA((N_CHIPS,)), pltpu.SemaphoreType.DMA((N_CHIPS,)), pltpu.HBM((r, w), pack.dtype),
                   pltpu.HBM((N_CHIPS, r, w), pack.dtype), jax.ShapeDtypeStruct((8, 128), F32)),
        in_specs=(HBM, HBM, ANY),
        out_specs=(SEM, SEM, HBM, HBM, pl.BlockSpec(memory_space=pltpu.VMEM)),
        input_output_aliases={0: 2, 1: 3},
        compiler_params=pltpu.CompilerParams(has_side_effects=EFFECT),
    )(_in_hbm(pack), _in_hbm(lax.empty((N_CHIPS, r, w), pack.dtype)), after)


def _gather_wait(send_sems, recv_sems, pack, land, after, *, name):
    def body(p_ref, land_ref, send_sems, recv_sems, after_ref, p_out, land_out):
        for cp in _gather_copies(p_ref, land_ref, send_sems, recv_sems):
            cp.wait_send()
            cp.wait_recv()

    return pl.pallas_call(
        body,
        name=name,
        out_shape=(pltpu.HBM(pack.shape, pack.dtype), pltpu.HBM(land.shape, land.dtype)),
        in_specs=(HBM, HBM, SEM, SEM, ANY),
        out_specs=(HBM, HBM),
        input_output_aliases={0: 0, 1: 1},
        compiler_params=pltpu.CompilerParams(has_side_effects=EFFECT),
    )(pack, land, send_sems, recv_sems, after)


def _gather_spread(land, *, name):
    n, r, w = land.shape
    rh = r // 2

    def body(land_ref, o_ref, send_sems, recv_sems):
        x, y, c = _place()
        rows = _half(c, rh)
        copies = []
        for k in range(1, N_CHIPS):
            cx, cy = _rel_chip(x, y, k)
            copies.append(pltpu.make_async_remote_copy(
                src_ref=land_ref.at[2 * cx + cy, rows], dst_ref=o_ref.at[2 * cx + cy, rows], send_sem=send_sems.at[k - 1],
                recv_sem=recv_sems.at[k - 1], device_id=(x, y, 1 - c), device_id_type=MESH))
        for cp in copies:
            cp.start()
        for cp in copies:
            cp.wait()

    return pl.pallas_call(
        body,
        name=name,
        in_specs=[ANY],
        out_specs=ANY,
        out_shape=jax.ShapeDtypeStruct(land.shape, land.dtype),
        input_output_aliases={0: 0},
        scratch_shapes=[pltpu.SemaphoreType.DMA((N_CHIPS - 1,)), pltpu.SemaphoreType.DMA((N_CHIPS - 1,))],
        compiler_params=pltpu.CompilerParams(has_side_effects=True),
    )(land)


N_PARTS = 2 * (N_CHIPS - 1)


def _scatter_copies(lo_ref, g_ref, land_lo_ref, land_f_ref, send_sems, recv_sems, starting):
    rh = g_ref.shape[1] // 2
    x, y, c = _place()
    copies = []
    for k in range(1, N_CHIPS):
        cx, cy = _rel_chip(x, y, k)
        for i in range(2):
            part = 2 * (k - 1) + (c if starting else i)
            copies.append(pltpu.make_async_remote_copy(
                src_ref=lo_ref.at[2 * cx + cy, pl.ds(i * rh, rh)], dst_ref=land_lo_ref.at[part],
                send_sem=send_sems.at[2 * (k - 1) + i], recv_sem=recv_sems.at[part], device_id=(cx, cy, i), device_id_type=MESH))
    copies.append(pltpu.make_async_remote_copy(
        src_ref=g_ref.at[2 * x + y, _half(1 - c, rh)], dst_ref=land_f_ref, send_sem=send_sems.at[N_PARTS], recv_sem=recv_sems.at[N_PARTS],
        device_id=(x, y, 1 - c), device_id_type=MESH))
    return copies


def _scatter_start(g_lo, g, *, name):
    n, r, w = g.shape
    rh = r // 2

    def body(lo_ref, g_ref, land_lo_ref, land_f_ref, send_sems, recv_sems, lo_thru, g_thru, land_lo_thru, land_f_thru, token):
        for cp in _scatter_copies(lo_ref, g_ref, land_lo_ref, land_f_ref, send_sems, recv_sems, True):
            cp.start()
        token[...] = jnp.zeros_like(token)

    return pl.pallas_call(
        body,
        name=name,
        out_shape=(pltpu.SemaphoreType.DMA((N_PARTS + 1,)), pltpu.SemaphoreType.DMA((N_PARTS + 1,)), pltpu.HBM(g_lo.shape, g_lo.dtype),
                   pltpu.HBM(g.shape, g.dtype), pltpu.HBM((N_PARTS, rh, w), g_lo.dtype), pltpu.HBM((rh, w), g.dtype),
                   jax.ShapeDtypeStruct((8, 128), F32)),
        in_specs=(HBM, HBM, HBM, HBM),
        out_specs=(SEM, SEM, HBM, HBM, HBM, HBM, pl.BlockSpec(memory_space=pltpu.VMEM)),
        input_output_aliases={0: 2, 1: 3, 2: 4, 3: 5},
        compiler_params=pltpu.CompilerParams(has_side_effects=EFFECT),
    )(_in_hbm(g_lo), _in_hbm(g), _in_hbm(lax.empty((N_PARTS, rh, w), g_lo.dtype)), _in_hbm(lax.empty((rh, w), g.dtype)))


def _scatter_wait(send_sems, recv_sems, g_lo, g, land_lo, land_f, after, *, name):
    def body(lo_ref, g_ref, land_lo_ref, land_f_ref, send_sems, recv_sems, after_ref, o0, o1, o2, o3):
        for cp in _scatter_copies(lo_ref, g_ref, land_lo_ref, land_f_ref, send_sems, recv_sems, False):
            cp.wait_send()
            cp.wait_recv()

    arrays = (g_lo, g, land_lo, land_f)
    return pl.pallas_call(
        body,
        name=name,
        out_shape=tuple(pltpu.HBM(a.shape, a.dtype) for a in arrays),
        in_specs=(HBM, HBM, HBM, HBM, SEM, SEM, ANY),
        out_specs=(HBM, HBM, HBM, HBM),
        input_output_aliases={0: 0, 1: 1, 2: 2, 3: 3},
        compiler_params=pltpu.CompilerParams(has_side_effects=EFFECT),
    )(*arrays, send_sems, recv_sems, after)


def _scatter_sum(g, land_lo, land_f, where, *, name):
    n, r, w = g.shape
    rh = r // 2
    tr = _pick(rh, (256, 160, 80))
    nt = rh // tr

    def body(where_ref, g_ref, f_ref, lo_ref, o_ref):
        acc = g_ref[0] + f_ref[...]
        for part in range(N_PARTS):
            acc = acc + lo_ref[part].astype(F32)
        o_ref[...] = acc

    return pl.pallas_call(
        body,
        name=name,
        grid_spec=pltpu.PrefetchScalarGridSpec(
            num_scalar_prefetch=1,
            grid=(nt,),
            in_specs=[pl.BlockSpec((1, tr, w), lambda i, wh: (wh[1], wh[0] * nt + i, 0)),
                      pl.BlockSpec((tr, w), lambda i, wh: (i, 0)),
                      pl.BlockSpec((N_PARTS, tr, w), lambda i, wh: (0, i, 0))],
            out_specs=pl.BlockSpec((tr, w), lambda i, wh: (wh[0] * nt + i, 0)),
        ),
        out_shape=jax.ShapeDtypeStruct((r, w), F32),
        compiler_params=_cp(("parallel",)),
    )(where, g, land_f, land_lo)


def _swap_all(shards, *, name):
    n = len(shards)

    def body(*refs):
        ins, outs = refs[:n], refs[n : 2 * n]
        send_sems, recv_sems = refs[2 * n :]
        x, y, c = _place()
        copies = []
        for i, (e_ref, o_ref) in enumerate(zip(ins, outs)):
            rows = _half(c, e_ref.shape[0] // 2)
            copies.append(pltpu.make_async_remote_copy(src_ref=e_ref.at[rows], dst_ref=o_ref.at[rows], send_sem=send_sems.at[i],
                                                       recv_sem=recv_sems.at[i], device_id=(x, y, 1 - c), device_id_type=MESH))
        for cp in copies:
            cp.start()
        for cp in copies:
            cp.wait()

    return pl.pallas_call(
        body,
        name=name,
        in_specs=[ANY] * n,
        out_specs=[ANY] * n,
        out_shape=[jax.ShapeDtypeStruct(e.shape, e.dtype) for e in shards],
        input_output_aliases={i: i for i in range(n)},
        scratch_shapes=[pltpu.SemaphoreType.DMA((n,)), pltpu.SemaphoreType.DMA((n,))],
        compiler_params=pltpu.CompilerParams(has_side_effects=True),
    )(*shards)


def _sum_small(small, after):
    n_dev = 8

    def body(s_ref, after_ref, o_ref, all_ref, send_sems, recv_sems):
        x, y, c = _place()
        me = 4 * x + 2 * y + c
        all_ref[me] = s_ref[...]
        copies = []
        for k in range(1, n_dev):
            cx, cy = _rel_chip(x, y, k >> 1)
            cc = 1 - c if k & 1 else c
            copies.append(pltpu.make_async_remote_copy(
                src_ref=s_ref, dst_ref=all_ref.at[me], send_sem=send_sems.at[k - 1], recv_sem=recv_sems.at[k - 1],
                device_id=(cx, cy, cc), device_id_type=MESH))
        for cp in copies:
            cp.start()
        for cp in copies:
            cp.wait()
        acc = all_ref[0]
        for a in range(1, n_dev):
            acc = acc + all_ref[a]
        o_ref[...] = acc

    vm = pl.BlockSpec(memory_space=pltpu.VMEM)
    return pl.pallas_call(
        body,
        name="sum_small",
        in_specs=[vm, ANY],
        out_specs=vm,
        out_shape=jax.ShapeDtypeStruct(small.shape, F32),
        scratch_shapes=[pltpu.VMEM((n_dev,) + small.shape, F32), pltpu.SemaphoreType.DMA((n_dev - 1,)), pltpu.SemaphoreType.DMA((n_dev - 1,))],
        compiler_params=pltpu.CompilerParams(has_side_effects=True),
    )(small, after)


MATS = {"w_in": (776, True), "w_out": (256, False), "w_xq": (256, False), "w_xkv": (512, True), "w_xo": (256, False),
        "w_up": (1024, True), "w_down": (1024, False)}
GATHER_FIRST = ("w_in",)
GATHER_REST = ("w_out", "w_xq", "w_xkv", "w_xo", "w_up", "w_down")
GRAD_GROUPS = (("w_up", "w_down"), ("w_out", "w_xq", "w_xkv", "w_xo"), ("w_in",))


def _group_rows(names):
    n = sum(MATS[name][0] for name in names)
    return n + (-n) % 32


def _pack(pieces, rows):
    p = jnp.concatenate(pieces, axis=0) if len(pieces) > 1 else pieces[0]
    return jnp.pad(p, ((0, rows - p.shape[0]), (0, 0))) if rows > p.shape[0] else p


def _unpack(rows, names):
    out, off = {}, 0
    for name in names:
        out[name] = rows[off : off + MATS[name][0]]
        off += MATS[name][0]
    return out


SMALL = (
    ("mix_norm", 1024), ("conv_norm", 512), ("b_af", 256), ("b_ab", 256), ("gla_norm", 128), ("xa_norm", 1024), ("mem_norm", 1024),
    ("mlp_norm", 1024), ("final_norm", 1024), ("conv_w", 1536), ("w_af", 4096), ("w_ab", 4096), ("loss", 128),
)


def kernel(x, mem, mix_norm, w_in, conv_w, conv_norm, w_af, b_af, w_ab, b_ab, gla_norm, w_out, xa_norm, mem_norm, w_xq, w_xkv, w_xo, mlp_norm, w_up, w_down, final_norm, loss_target, m_mix_norm, m_w_in, m_conv_w, m_conv_norm, m_w_af, m_b_af, m_w_ab, m_b_ab, m_gla_norm, m_w_out, m_xa_norm, m_mem_norm, m_w_xq, m_w_xkv, m_w_xo, m_mlp_norm, m_w_up, m_w_down, m_final_norm, v_mix_norm, v_w_in, v_conv_w, v_conv_norm, v_w_af, v_b_af, v_w_ab, v_b_ab, v_gla_norm, v_w_out, v_xa_norm, v_mem_norm, v_w_xq, v_w_xkv, v_w_xo, v_mlp_norm, v_w_up, v_w_down, v_final_norm):
    given = dict(locals())
    xi, yi, ci = _place()
    chip = 2 * xi + yi
    where = jnp.stack([ci, chip]).astype(jnp.int32)

    lo = {name: (given[name][0].T if MATS[name][1] else given[name][0]).astype(_CD) for name in MATS}
    pack_rest = _pack([lo[name] for name in GATHER_REST], _group_rows(GATHER_REST))
    pack_first = _pack([lo[name] for name in GATHER_FIRST], _group_rows(GATHER_FIRST))
    got_first = _gather_weights(pack_first)

    def whole(got, off, rows):
        return got[:, off : off + rows].reshape(N_CHIPS * rows, D_MODEL)

    w_in_t = whole(got_first, 0, MATS["w_in"][0])
    w_za = jnp.concatenate([w_in_t[0:1536], w_in_t[2560:3072]], axis=0)
    w_zb = jnp.concatenate([w_in_t[1536:2560], w_in_t[3072:W_IN_COLS], jnp.zeros((ZB_COLS - 1056, D_MODEL), _CD)], axis=0)

    def placed(shard, full_shape, col):
        return lax.dynamic_update_slice(jnp.zeros(full_shape, F32), shard, (0, col)).reshape(-1, 128)

    sw = jnp.concatenate([
        placed(conv_w[0], (CONV_K, CONV_WIDTH), 128 * chip),
        placed(w_af[0], (GLA_LOWRANK, GLA_K_TOTAL), 64 * chip),
        placed(w_ab[0], (GLA_LOWRANK, GLA_K_TOTAL), 64 * chip),
    ], axis=0)
    sw = jnp.pad(sw, ((0, SMALL_ROWS - sw.shape[0]), (0, 0))) * (ci == 0).astype(F32)
    sw = _sum_small(sw, got_first)
    rest_send, rest_recv, pack_rest, land_rest, rest_token = _gather_start(pack_rest, sw, name="gather_rest_start")
    conv_w_full = sw[0:12].reshape(CONV_K, CONV_WIDTH)
    w_af_full = sw[12:44].reshape(GLA_LOWRANK, GLA_K_TOTAL)
    w_ab_full = sw[44:76].reshape(GLA_LOWRANK, GLA_K_TOTAL)
    waf_p = jnp.pad(w_af_full, ((0, 128 - GLA_LOWRANK), (0, 0))).astype(_CD)
    wab_p = jnp.pad(w_ab_full, ((GLA_LOWRANK, 128 - 2 * GLA_LOWRANK), (0, 0))).astype(_CD)

    xs, mems, tgt = x[0], mem[0], loss_target[0]
    add_res = lambda acc, res: (acc + res,)
    behind = lambda gain, token: gain + token[0, 0]

    h1 = _rms_fwd(xs, behind(mix_norm, rest_token), name="norm_mix")
    z_b = _mm(h1, w_zb, mode="nt", name="proj_in_b", tn=ZB_COLS)
    z_a = _mm(h1, w_za, mode="nt", name="proj_in_a", tm=512, tn=ZA_COLS)
    b_f, b_b = _gate_fwd(z_b, waf_p, wab_p, b_af, b_ab, name="gates")
    o_f, st_f, o_b, st_b = _gla_fwd(z_b, b_f, b_b, name="gla_scan")
    y = _mix_fwd(z_a, o_f, o_b, conv_w_full, conv_norm, gla_norm, name="mix_out")
    pack_rest, land_rest = _gather_wait(rest_send, rest_recv, pack_rest, land_rest, y, name="gather_rest_wait")
    gathered = _gather_spread(land_rest, name="gather_rest_spread")
    wt, off = {}, 0
    for name in GATHER_REST:
        wt[name] = whole(gathered, off, MATS[name][0])
        off += MATS[name][0]
    x1, hx = _mm_rows(y, wt["w_out"], mode="nn", name="proj_out", rows=(xs,), vecs=(xa_norm,), out_rows=(F32, _CD), epilogue=_ep_residual_norm)
    qx = _mm(hx, wt["w_xq"], mode="nn", name="proj_xq", out_dtypes=(_CD,))
    hmem = _rms_fwd(mems, mem_norm, name="norm_mem")
    kv = _mm(hmem, wt["w_xkv"], mode="nt", name="proj_xkv", out_dtypes=(_CD,))
    ox = _xattn_fwd(qx, kv, name="xattn")
    x2, hm = _mm_rows(ox, wt["w_xo"], mode="nn", name="proj_xo", rows=(x1,), vecs=(mlp_norm,), out_rows=(F32, _CD), epilogue=_ep_residual_norm)
    act = _mm(hm, wt["w_up"], mode="nt", name="mlp_up", out_dtypes=(_CD,), epilogue=lambda acc: (jnp.square(jnp.maximum(acc, 0.0)),))
    dx3, dx3_lo, loss_part, g_final_norm = _mm_rows(
        act, wt["w_down"], mode="nn", name="mlp_down", rows=(x2, tgt), vecs=(final_norm.reshape(1, D_MODEL),),
        out_rows=(F32, _CD), out_vecs=(128, D_MODEL), epilogue=_ep_loss)

    grads_t = {}

    def start_group(names, tag):
        rows = _group_rows(names)
        g = jnp.stack([_pack([grads_t[name][a * MATS[name][0] : (a + 1) * MATS[name][0]] for name in names], rows) for a in range(N_CHIPS)])
        return _scatter_start(g.astype(_TD), g, name="grads_" + tag + "_start")

    def finish_group(state, after, tag):
        send_sems, recv_sems, g_lo, g, land_lo, land_f, _ = state
        g_lo, g, land_lo, land_f = _scatter_wait(send_sems, recv_sems, g_lo, g, land_lo, land_f, after, name="grads_" + tag + "_wait")
        return _scatter_sum(g, land_lo, land_f, where, name="grads_" + tag + "_sum")

    def new_packs(names):
        shape = (N_CHIPS, _group_rows(names), D_MODEL)
        return lax.empty(shape, F32), lax.empty(shape, _TD)

    def grad_into(packs, names, which, a, b, name):
        off = sum(MATS[other][0] for other in names[: names.index(which)])
        return _mm_tn_into(a, b, packs, rows=MATS[which][0], off=off, name=name)

    du = _mm(dx3_lo, wt["w_down"], mode="nt", name="mlp_down_dx", out_dtypes=(_CD,), extras=(act,),
             epilogue=lambda acc, aa: (acc * (2.0 * jnp.sqrt(aa.astype(F32))),))
    packs = new_packs(GRAD_GROUPS[0])
    packs = grad_into(packs, GRAD_GROUPS[0], "w_down", act, dx3_lo, "mlp_down_dw")
    packs = grad_into(packs, GRAD_GROUPS[0], "w_up", du, hm, "mlp_up_dw")
    mlp_state = _scatter_start(packs[1], packs[0], name="grads_mlp_start")
    dx2, dx2_lo, g_mlp_norm = _mm_rows(
        du, wt["w_up"], mode="nn", name="mlp_up_dx", rows=(x2, dx3), vecs=(behind(mlp_norm, mlp_state[-1]),),
        out_rows=(F32, _CD), out_vecs=(D_MODEL,), epilogue=_ep_norm_bwd)
    dox = _mm(dx2_lo, wt["w_xo"], mode="nt", name="proj_xo_dx", out_dtypes=(_CD,))
    packs = new_packs(GRAD_GROUPS[1])
    packs = grad_into(packs, GRAD_GROUPS[1], "w_xo", ox, dx2_lo, "proj_xo_dw")
    dqx, dkv = _xattn_bwd(qx, kv, dox, name="xattn_bwd")
    packs = grad_into(packs, GRAD_GROUPS[1], "w_xq", hx, dqx, "proj_xq_dw")
    dx1, dx1_lo, g_xa_norm = _mm_rows(
        dqx, wt["w_xq"], mode="nt", name="proj_xq_dx", rows=(x1, dx2), vecs=(xa_norm,),
        out_rows=(F32, _CD), out_vecs=(D_MODEL,), epilogue=_ep_norm_bwd)
    dkv_lo = dkv.astype(_CD)
    packs = grad_into(packs, GRAD_GROUPS[1], "w_xkv", dkv_lo, hmem, "proj_xkv_dw")
    dhmem = _mm(dkv_lo, wt["w_xkv"], mode="nn", name="proj_xkv_dx")
    (g_mem_norm,) = _rms_bwd(mems, mem_norm, dhmem, name="norm_mem_bwd", want_dx=False, want_lo=False)
    dy = _mm(dx1_lo, wt["w_out"], mode="nt", name="proj_out_dx")
    packs = grad_into(packs, GRAD_GROUPS[1], "w_out", y, dx1_lo, "proj_out_dw")
    attn_state = _scatter_start(packs[1], packs[0], name="grads_attn_start")
    dz_a, do, g_conv_w, g_conv_norm, g_gla_norm = _mix_bwd(z_a, o_f, o_b, dy, conv_w_full, behind(conv_norm, attn_state[-1]), gla_norm, name="mix_out_bwd")
    dqkv_f, db_f, dqkv_b, db_b = _gla_bwd(z_b, b_f, b_b, do, st_f, st_b, name="gla_scan_bwd")
    dz_b, g_waf_p, g_wab_p, g_b_af, g_b_ab = _gate_bwd(z_b, waf_p, wab_p, b_af, b_ab, db_f, db_b, dqkv_f, dqkv_b, name="gates_bwd")
    g_za = _mm_tn(dz_a, h1, name="proj_in_a_dw")
    g_zb = _mm_tn(dz_b, h1, name="proj_in_b_dw")
    grads_t["w_in"] = jnp.concatenate([g_za[0:1536], g_zb[0:1024], g_za[1536:2048], g_zb[1024:1056]], axis=0)
    in_state = start_group(GRAD_GROUPS[2], "in")
    dh1_a = _mm(dz_a, w_za, mode="nn", name="proj_in_a_dx", tm=512, tk=ZA_COLS)
    grad_x, g_mix_norm = _mm_rows(
        dz_b, w_zb, mode="nn", name="proj_in_b_dx", rows=(xs, dx1, dh1_a), vecs=(behind(mix_norm, in_state[-1]),),
        out_rows=(F32,), out_vecs=(D_MODEL,), epilogue=_ep_norm_bwd)

    half_mlp = finish_group(mlp_state, grad_x, "mlp")
    half_attn = finish_group(attn_state, half_mlp, "attn")
    half_in = finish_group(in_state, half_attn, "in")
    g_shard = {}
    for names, rows in zip(GRAD_GROUPS, _swap_all([half_mlp, half_attn, half_in], name="shards_to_sibling")):
        for name, g in _unpack(rows, names).items():
            g_shard[name] = g.T if MATS[name][1] else g

    small_vals = dict(mix_norm=g_mix_norm, conv_norm=g_conv_norm, b_af=g_b_af, b_ab=g_b_ab, gla_norm=g_gla_norm, xa_norm=g_xa_norm,
                      mem_norm=g_mem_norm, mlp_norm=g_mlp_norm, final_norm=g_final_norm, conv_w=g_conv_w,
                      w_af=g_waf_p[0:GLA_LOWRANK], w_ab=g_wab_p[GLA_LOWRANK : 2 * GLA_LOWRANK], loss=loss_part)
    small = jnp.concatenate([small_vals[name].reshape(-1, 128) for name, _ in SMALL], axis=0)
    small = _sum_small(jnp.pad(small, ((0, SMALL_ROWS - small.shape[0]), (0, 0))), loss_part)
    g_small, off = {}, 0
    for name, n in SMALL:
        g_small[name] = small[off : off + n // 128]
        off += n // 128
    loss = g_small["loss"][0, 0]
    g_small["conv_w"] = lax.dynamic_slice(g_small["conv_w"].reshape(CONV_K, CONV_WIDTH), (0, 128 * chip), (CONV_K, 128))
    g_small["w_af"] = lax.dynamic_slice(g_small["w_af"].reshape(GLA_LOWRANK, GLA_K_TOTAL), (0, 64 * chip), (GLA_LOWRANK, 64))
    g_small["w_ab"] = lax.dynamic_slice(g_small["w_ab"].reshape(GLA_LOWRANK, GLA_K_TOTAL), (0, 64 * chip), (GLA_LOWRANK, 64))

    names = ["mix_norm", "w_in", "conv_w", "conv_norm", "w_af", "b_af", "w_ab", "b_ab", "gla_norm", "w_out", "xa_norm", "mem_norm",
             "w_xq", "w_xkv", "w_xo", "mlp_norm", "w_up", "w_down", "final_norm"]
    big_names = list(MATS)
    as2d = lambda a: a.reshape(1, -1) if a.ndim == 1 else a.reshape(a.shape[-2:])
    grads, deltas, new_m, new_v = {}, {}, {}, {}
    for name in big_names:
        grads[name] = g_shard[name]
        deltas[name], new_m[name], new_v[name] = _adamw(as2d(given[name]), g_shard[name], as2d(given["m_" + name]),
                                                         as2d(given["v_" + name]), name="adamw_" + name)
    small_names = [name for name in names if name not in big_names]
    groups = []
    for name in small_names:
        grads[name] = g_small[name].reshape(as2d(given[name]).shape)
        groups.append((as2d(given[name]), grads[name], as2d(given["m_" + name]), as2d(given["v_" + name])))
    for name, res in zip(small_names, _adamw_small(groups, name="adamw_small")):
        deltas[name], new_m[name], new_v[name] = res

    like = lambda name, a: a.reshape(given[name].shape)
    return (loss, grad_x[None], *[like(n, grads[n]) for n in names], *[like(n, deltas[n]) for n in names],
            *[like(n, new_m[n]) for n in names], *[like(n, new_v[n]) for n in names])
```

```python
import functools

import jax
import jax.numpy as jnp
from jax import lax
from jax.experimental import pallas as pl
from jax.experimental.pallas import tpu as pltpu

F32 = jnp.float32
BF16 = jnp.bfloat16
_CD = jnp.bfloat16
_TD = jnp.bfloat16

D_MODEL = 1024
N_MEM = 256
CONV_WIDTH = 512
CONV_GROUP = 64
CONV_K = 3
GLA_HEADS = 4
GLA_DK = 64
GLA_DV = 128
GLA_K_TOTAL = 256
GLA_V_TOTAL = 512
GLA_LOWRANK = 16
GLA_GATE_SCALE = 1.0 / 16.0
GLA_CHUNK = 64
XA_HEADS = 4
XA_HEAD_DIM = 256
D_FF = 4096
EPS = 1e-6
W_IN_COLS = 3104
ZA_COLS = 2048
ZB_COLS = 1152
LR_COL = 1024

ADAM_LR = 0.001
ADAM_B1 = 0.9
ADAM_B2 = 0.999
ADAM_EPS = 1e-08
ADAM_WD = 0.01
ADAM_STEP = 10

N_CHIPS = 4
PACK_W = 1024
PACK_ROWS = 4160
PACK_TILE = 160
SMALL_ROWS = 128

_TS = 512
_VMEM = 44 * 1024 * 1024
MESH = pl.DeviceIdType.MESH
ANY = pl.BlockSpec(memory_space=pl.ANY)


def _cp(sem=None, **kw):
    return pltpu.CompilerParams(dimension_semantics=sem, vmem_limit_bytes=_VMEM, **kw)


def _dot(a, b):
    return jnp.dot(a.astype(_CD), b.astype(_CD), preferred_element_type=F32)


def _dot_nt(a, b):
    return lax.dot_general(a.astype(_CD), b.astype(_CD), (((1,), (1,)), ((), ())), preferred_element_type=F32)


def _dot_tn(a, b):
    return lax.dot_general(a.astype(_CD), b.astype(_CD), (((0,), (0,)), ((), ())), preferred_element_type=F32)


def _dot_split(x, ones):
    hi = x.astype(BF16)
    r = x - hi.astype(F32)
    mid = r.astype(BF16)
    lo = (r - mid.astype(F32)).astype(BF16)
    d = lambda p: jnp.dot(p, ones, preferred_element_type=F32)
    return d(hi) + d(mid) + d(lo)


def _pick(n, cands=(1024, 640, 512, 256, 128)):
    for t in cands:
        if n % t == 0:
            return t
    return n


def _rows(s):
    return min(_TS, s)


def _sigmoid(v):
    e = jnp.exp(-jnp.abs(v))
    return jnp.where(v >= 0, 1.0 / (1.0 + e), e / (1.0 + e))


def _mm(a, b, *, mode, name, out_dtypes=(F32,), extras=(), epilogue=None, tm=None, tn=None, tk=None):
    m, k = a.shape
    n = b.shape[1] if mode == "nn" else b.shape[0]
    tm = min(m, tm or 1024)
    tn = tn or _pick(n)
    tk = tk or _pick(k)
    nk = k // tk
    n_ex, n_out = len(extras), len(out_dtypes)

    def body(*refs):
        a_ref, b_ref = refs[:2]
        ex = refs[2 : 2 + n_ex]
        outs = refs[2 + n_ex : 2 + n_ex + n_out]
        part = _dot(a_ref[...], b_ref[...]) if mode == "nn" else _dot_nt(a_ref[...], b_ref[...])

        def finish(acc):
            res = epilogue(acc, *[e[...] for e in ex]) if epilogue else (acc,)
            for o, r in zip(outs, res):
                o[...] = r.astype(o.dtype)

        if nk == 1:
            finish(part)
        else:
            acc_ref = refs[-1]
            kk = pl.program_id(2)

            @pl.when(kk == 0)
            def _():
                acc_ref[...] = part

            @pl.when(kk > 0)
            def _():
                acc_ref[...] += part

            @pl.when(kk == nk - 1)
            def _():
                finish(acc_ref[...])

    b_spec = pl.BlockSpec((tk, tn), lambda i, j, kk: (kk, j)) if mode == "nn" else pl.BlockSpec((tn, tk), lambda i, j, kk: (j, kk))
    tile = pl.BlockSpec((tm, tn), lambda i, j, kk: (i, j))
    out = pl.pallas_call(
        body,
        name=name,
        grid=(m // tm, n // tn, nk),
        in_specs=[pl.BlockSpec((tm, tk), lambda i, j, kk: (i, kk)), b_spec] + [tile] * n_ex,
        out_specs=[tile] * n_out,
        out_shape=[jax.ShapeDtypeStruct((m, n), dt) for dt in out_dtypes],
        scratch_shapes=[pltpu.VMEM((tm, tn), F32)] if nk > 1 else [],
        compiler_params=_cp(("parallel", "parallel", "arbitrary")),
    )(a, b, *extras)
    return out[0] if n_out == 1 else out


def _mm_tn(a, b, *, name):
    s, m = a.shape
    n = b.shape[1]
    cap = max(128, (1 << 20) // n)
    tm = _pick(m, tuple(t for t in (512, 640, 384, 256, 128) if t <= max(cap, 128)))
    ts = min(s, 1 << (((1 << 22) // n).bit_length() - 1))
    ns = s // ts

    def body(a_ref, b_ref, o_ref):
        part = _dot_tn(a_ref[...], b_ref[...])
        if ns == 1:
            o_ref[...] = part
        else:
            ss = pl.program_id(1)

            @pl.when(ss == 0)
            def _():
                o_ref[...] = part

            @pl.when(ss > 0)
            def _():
                o_ref[...] += part

    return pl.pallas_call(
        body,
        name=name,
        grid=(m // tm, ns),
        in_specs=[pl.BlockSpec((ts, tm), lambda i, ss: (ss, i)), pl.BlockSpec((ts, n), lambda i, ss: (ss, 0))],
        out_specs=pl.BlockSpec((tm, n), lambda i, ss: (i, 0)),
        out_shape=jax.ShapeDtypeStruct((m, n), F32),
        compiler_params=_cp(("parallel", "arbitrary")),
    )(a, b)


def _mm_tn_into(a, b, packs, *, rows, off, name):
    s, m = a.shape
    n = b.shape[1]
    tm = 512
    tr = min(tm, rows)
    per, chips = rows // tr, tm // tr
    ts = min(s, 1 << (((1 << 22) // n).bit_length() - 1))
    ns = s // ts

    def body(a_ref, b_ref, f_in, lo_in, f_ref, lo_ref):
        part = _dot_tn(a_ref[...], b_ref[...])
        pieces = [part[c * tr : (c + 1) * tr] for c in range(chips)]
        if ns == 1:
            for c, p in enumerate(pieces):
                f_ref[c] = p
                lo_ref[c] = p.astype(lo_ref.dtype)
        else:
            ss = pl.program_id(1)

            @pl.when(ss == 0)
            def _():
                for c, p in enumerate(pieces):
                    f_ref[c] = p

            @pl.when(ss > 0)
            def _():
                for c, p in enumerate(pieces):
                    f_ref[c] += p

            @pl.when(ss == ns - 1)
            def _():
                lo_ref[...] = f_ref[...].astype(lo_ref.dtype)

    spec = pl.BlockSpec((chips, tr, n), lambda i, ss: (i // per, off // tr + i % per, 0))
    return pl.pallas_call(
        body,
        name=name,
        grid=(m // tm, ns),
        in_specs=[pl.BlockSpec((ts, tm), lambda i, ss: (ss, i)), pl.BlockSpec((ts, n), lambda i, ss: (ss, 0)), ANY, ANY],
        out_specs=[spec, spec],
        out_shape=[jax.ShapeDtypeStruct(p.shape, p.dtype) for p in packs],
        input_output_aliases={2: 0, 3: 1},
        compiler_params=_cp(("parallel", "arbitrary")),
    )(a, b, *packs)


def _mm_rows(a, b, *, mode, name, rows=(), vecs=(), out_rows=(), out_vecs=(), epilogue, tm=512):
    m, k = a.shape
    n = b.shape[1] if mode == "nn" else b.shape[0]
    tm = min(m, tm)
    n_r, n_v, n_or, n_ov = len(rows), len(vecs), len(out_rows), len(out_vecs)

    def body(*refs):
        a_ref, b_ref = refs[:2]
        r_refs = refs[2 : 2 + n_r]
        v_refs = refs[2 + n_r : 2 + n_r + n_v]
        or_refs = refs[2 + n_r + n_v : 2 + n_r + n_v + n_or]
        ov_refs = refs[2 + n_r + n_v + n_or :]
        acc = _dot(a_ref[...], b_ref[...]) if mode == "nn" else _dot_nt(a_ref[...], b_ref[...])
        res_rows, res_vecs = epilogue(acc, [r[...] for r in r_refs], [v[...] for v in v_refs])
        for o, r in zip(or_refs, res_rows):
            o[...] = r.astype(o.dtype)
        if n_ov:
            first = pl.program_id(0) == 0

            @pl.when(first)
            def _():
                for o, r in zip(ov_refs, res_vecs):
                    o[...] = r

            @pl.when(jnp.logical_not(first))
            def _():
                for o, r in zip(ov_refs, res_vecs):
                    o[...] += r

    tile = pl.BlockSpec((tm, n), lambda i: (i, 0))
    whole = lambda arr: pl.BlockSpec(arr.shape, lambda i: (0, 0))
    vec = lambda w: pl.BlockSpec((1, w), lambda i: (0, 0))
    out = pl.pallas_call(
        body,
        name=name,
        grid=(m // tm,),
        in_specs=[pl.BlockSpec((tm, k), lambda i: (i, 0)), whole(b)] + [tile] * n_r + [vec(v.shape[1]) for v in vecs],
        out_specs=[tile] * n_or + [vec(w) for w in out_vecs],
        out_shape=[jax.ShapeDtypeStruct((m, n), dt) for dt in out_rows] + [jax.ShapeDtypeStruct((1, w), F32) for w in out_vecs],
        compiler_params=_cp(("arbitrary",) if n_ov else ("parallel",)),
    )(a, b, *rows, *vecs)
    return out


def _ep_residual_norm(acc, rows, vecs):
    x = acc + rows[0]
    r = lax.rsqrt(jnp.mean(x * x, axis=-1, keepdims=True) + EPS)
    return [x, x * r * vecs[0]], []


def _ep_norm_bwd(acc, rows, vecs):
    dy = acc
    for extra in rows[2:]:
        dy = dy + extra
    x, dres = rows[0], rows[1]
    r = lax.rsqrt(jnp.mean(x * x, axis=-1, keepdims=True) + EPS)
    xh = x * r
    dxh = dy * vecs[0]
    dx = r * (dxh - xh * jnp.mean(dxh * xh, axis=-1, keepdims=True)) + dres
    return [dx, dx], [jnp.sum(dy * xh, axis=0, keepdims=True)]


def _ep_loss(acc, rows, vecs):
    x = acc + rows[0]
    d = x.shape[-1]
    r = lax.rsqrt(jnp.mean(x * x, axis=-1, keepdims=True) + EPS)
    xh = x * r
    err = xh * vecs[0] - rows[1]
    loss = jnp.zeros((1, 128), F32) + 0.5 * jnp.sum(jnp.mean(err * err, axis=-1, keepdims=True))
    dy = err * (1.0 / d)
    dxh = dy * vecs[0]
    dx = r * (dxh - xh * jnp.mean(dxh * xh, axis=-1, keepdims=True))
    return [dx, dx], [loss, jnp.sum(dy * xh, axis=0, keepdims=True)]


def _rms_fwd(x, g, *, name):
    s, d = x.shape
    ts = _rows(s)

    def body(x_ref, g_ref, o_ref):
        xf = x_ref[...]
        r = lax.rsqrt(jnp.mean(xf * xf, axis=-1, keepdims=True) + EPS)
        o_ref[...] = (xf * r * g_ref[...]).astype(o_ref.dtype)

    return pl.pallas_call(
        body,
        name=name,
        grid=(s // ts,),
        in_specs=[pl.BlockSpec((ts, d), lambda i: (i, 0)), pl.BlockSpec((1, d), lambda i: (0, 0))],
        out_specs=pl.BlockSpec((ts, d), lambda i: (i, 0)),
        out_shape=jax.ShapeDtypeStruct((s, d), _CD),
        compiler_params=_cp(("parallel",)),
    )(x, g)


def _rms_bwd(x, g, dy, dres=None, *, name, want_dx=True, want_lo=True):
    s, d = x.shape
    ts = _rows(s)
    has_res = dres is not None

    def body(*refs):
        x_ref, g_ref, dy_ref = refs[:3]
        pos = 3
        dres_ref = refs[pos] if has_res else None
        pos += has_res
        dx_ref = refs[pos] if want_dx else None
        pos += want_dx
        lo_ref = refs[pos] if want_lo else None
        pos += want_lo
        dg_ref = refs[pos]
        xf = x_ref[...]
        r = lax.rsqrt(jnp.mean(xf * xf, axis=-1, keepdims=True) + EPS)
        xh = xf * r
        dyf = dy_ref[...]
        part = jnp.sum(dyf * xh, axis=0, keepdims=True)

        @pl.when(pl.program_id(0) == 0)
        def _():
            dg_ref[...] = part

        @pl.when(pl.program_id(0) > 0)
        def _():
            dg_ref[...] += part

        if want_dx or want_lo:
            dxh = dyf * g_ref[...]
            dx = r * (dxh - xh * jnp.mean(dxh * xh, axis=-1, keepdims=True))
            if has_res:
                dx = dx + dres_ref[...]
            if want_dx:
                dx_ref[...] = dx
            if want_lo:
                lo_ref[...] = dx.astype(lo_ref.dtype)

    tile = pl.BlockSpec((ts, d), lambda i: (i, 0))
    vec = pl.BlockSpec((1, d), lambda i: (0, 0))
    out_specs, out_shape = [], []
    if want_dx:
        out_specs.append(tile)
        out_shape.append(jax.ShapeDtypeStruct((s, d), F32))
    if want_lo:
        out_specs.append(tile)
        out_shape.append(jax.ShapeDtypeStruct((s, d), _CD))
    out_specs.append(vec)
    out_shape.append(jax.ShapeDtypeStruct((1, d), F32))
    return pl.pallas_call(
        body,
        name=name,
        grid=(s // ts,),
        in_specs=[tile, vec, tile] + ([tile] if has_res else []),
        out_specs=out_specs,
        out_shape=out_shape,
        compiler_params=_cp(("arbitrary",)),
    )(x, g, dy, *([dres] if has_res else []))


def _final_loss(x3, g, tgt, *, name):
    s, d = x3.shape
    ts = _rows(s)

    def body(x_ref, g_ref, t_ref, dx_ref, lo_ref, loss_ref, dg_ref):
        xf = x_ref[...]
        r = lax.rsqrt(jnp.mean(xf * xf, axis=-1, keepdims=True) + EPS)
        xh = xf * r
        gg = g_ref[...]
        err = xh * gg - t_ref[...]
        lpart = jnp.zeros((1, 128), F32) + 0.5 * jnp.sum(jnp.mean(err * err, axis=-1, keepdims=True))
        dy = err * (1.0 / d)
        gpart = jnp.sum(dy * xh, axis=0, keepdims=True)

        @pl.when(pl.program_id(0) == 0)
        def _():
            loss_ref[...] = lpart
            dg_ref[...] = gpart

        @pl.when(pl.program_id(0) > 0)
        def _():
            loss_ref[...] += lpart
            dg_ref[...] += gpart

        dxh = dy * gg
        dx = r * (dxh - xh * jnp.mean(dxh * xh, axis=-1, keepdims=True))
        dx_ref[...] = dx
        lo_ref[...] = dx.astype(lo_ref.dtype)

    tile = pl.BlockSpec((ts, d), lambda i: (i, 0))
    vec = pl.BlockSpec((1, d), lambda i: (0, 0))
    return pl.pallas_call(
        body,
        name=name,
        grid=(s // ts,),
        in_specs=[tile, vec, tile],
        out_specs=[tile, tile, pl.BlockSpec((1, 128), lambda i: (0, 0)), vec],
        out_shape=[
            jax.ShapeDtypeStruct((s, d), F32),
            jax.ShapeDtypeStruct((s, d), _CD),
            jax.ShapeDtypeStruct((1, 128), F32),
            jax.ShapeDtypeStruct((1, d), F32),
        ],
        compiler_params=_cp(("arbitrary",)),
    )(x3, g, tgt)


def _chunk_scan(v, row_in_chunk, suffix):
    t = v.shape[0]
    step = 1
    while step < GLA_CHUNK:
        if suffix:
            v = v + jnp.where(row_in_chunk < GLA_CHUNK - step, pltpu.roll(v, t - step, 0), 0.0)
        else:
            v = v + jnp.where(row_in_chunk >= step, pltpu.roll(v, step, 0), 0.0)
        step *= 2
    return v


def _gate_pre(lr, w_ref, b_ref):
    return _dot(lr, w_ref[...]) + b_ref[...]


def _gate_fwd(z, waf, wab, baf, bab, *, name):
    s = z.shape[0]
    ts = _rows(s)

    def body(lr_ref, waf_ref, wab_ref, baf_ref, bab_ref, bf_ref, bb_ref):
        lr = lr_ref[...]
        ric = lax.broadcasted_iota(jnp.int32, (ts, GLA_K_TOTAL), 0) & (GLA_CHUNK - 1)
        for w_ref, b_ref, o_ref, suffix in ((waf_ref, baf_ref, bf_ref, False), (wab_ref, bab_ref, bb_ref, True)):
            pre = _gate_pre(lr, w_ref, b_ref)
            la = (jnp.minimum(pre, 0.0) - jnp.log(1.0 + jnp.exp(-jnp.abs(pre)))) * GLA_GATE_SCALE
            o_ref[...] = _chunk_scan(la, ric, suffix)

    wspec = pl.BlockSpec((128, GLA_K_TOTAL), lambda i: (0, 0))
    bspec = pl.BlockSpec((1, GLA_K_TOTAL), lambda i: (0, 0))
    tile = pl.BlockSpec((ts, GLA_K_TOTAL), lambda i: (i, 0))
    return pl.pallas_call(
        body,
        name=name,
        grid=(s // ts,),
        in_specs=[pl.BlockSpec((ts, 128), lambda i: (i, LR_COL // 128)), wspec, wspec, bspec, bspec],
        out_specs=[tile, tile],
        out_shape=[jax.ShapeDtypeStruct((s, GLA_K_TOTAL), F32)] * 2,
        compiler_params=_cp(("parallel",)),
    )(z, waf, wab, baf, bab)


def _gate_bwd(z, waf, wab, baf, bab, dbf, dbb, dqkv_f, dqkv_b, *, name):
    s = z.shape[0]
    ts = _rows(s)

    def body(lr_ref, waf_ref, wab_ref, baf_ref, bab_ref, dbf_ref, dbb_ref, gf_ref, gb_ref, dzb_ref, dwf_ref, dwb_ref, dbaf_ref, dbab_ref):
        lr = lr_ref[...]
        ric = lax.broadcasted_iota(jnp.int32, (ts, GLA_K_TOTAL), 0) & (GLA_CHUNK - 1)
        first = pl.program_id(0) == 0
        dlr = None
        for w_ref, b_ref, db_ref, dw_ref, dbias_ref, suffix in (
            (waf_ref, baf_ref, dbf_ref, dwf_ref, dbaf_ref, True),
            (wab_ref, bab_ref, dbb_ref, dwb_ref, dbab_ref, False),
        ):
            pre = _gate_pre(lr, w_ref, b_ref)
            dla = _chunk_scan(db_ref[...], ric, suffix)
            dpre = dla * GLA_GATE_SCALE * _sigmoid(-pre)
            part = _dot_nt(dpre, w_ref[...])
            dlr = part if dlr is None else dlr + part
            dw = _dot_tn(lr, dpre)
            dbias = jnp.sum(dpre, axis=0, keepdims=True)

            @pl.when(first)
            def _():
                dw_ref[...] = dw
                dbias_ref[...] = dbias

            @pl.when(jnp.logical_not(first))
            def _():
                dw_ref[...] += dw
                dbias_ref[...] += dbias

        dzb_ref[...] = jnp.concatenate([gf_ref[...] + gb_ref[...], dlr], axis=1).astype(dzb_ref.dtype)

    wspec = pl.BlockSpec((128, GLA_K_TOTAL), lambda i: (0, 0))
    bspec = pl.BlockSpec((1, GLA_K_TOTAL), lambda i: (0, 0))
    tile = pl.BlockSpec((ts, GLA_K_TOTAL), lambda i: (i, 0))
    wide = pl.BlockSpec((ts, 2 * GLA_K_TOTAL + GLA_V_TOTAL), lambda i: (i, 0))
    return pl.pallas_call(
        body,
        name=name,
        grid=(s // ts,),
        in_specs=[pl.BlockSpec((ts, 128), lambda i: (i, LR_COL // 128)), wspec, wspec, bspec, bspec, tile, tile, wide, wide],
        out_specs=[pl.BlockSpec((ts, ZB_COLS), lambda i: (i, 0)), wspec, wspec, bspec, bspec],
        out_shape=[
            jax.ShapeDtypeStruct((s, ZB_COLS), _CD),
            jax.ShapeDtypeStruct((128, GLA_K_TOTAL), F32),
            jax.ShapeDtypeStruct((128, GLA_K_TOTAL), F32),
            jax.ShapeDtypeStruct((1, GLA_K_TOTAL), F32),
            jax.ShapeDtypeStruct((1, GLA_K_TOTAL), F32),
        ],
        compiler_params=_cp(("arbitrary",)),
    )(z, waf, wab, baf, bab, dbf, dbb, dqkv_f, dqkv_b)


def _gla_masks(rev):
    lane_head = lax.broadcasted_iota(jnp.int32, (1, GLA_K_TOTAL), 1) >> 6
    head_masks = [lane_head == h for h in range(GLA_HEADS)]
    t = lax.broadcasted_iota(jnp.int32, (GLA_HEADS * GLA_CHUNK, GLA_CHUNK), 0) & (GLA_CHUNK - 1)
    u = lax.broadcasted_iota(jnp.int32, (GLA_HEADS * GLA_CHUNK, GLA_CHUNK), 1)
    tri = (u > t) if rev else (u <= t)
    row = lax.broadcasted_iota(jnp.int32, (GLA_CHUNK, GLA_K_TOTAL), 0)
    total_row = row == (0 if rev else GLA_CHUNK - 1)
    return head_masks, tri, total_row


def _spread(a, head_masks):
    return jnp.concatenate([jnp.where(m, a, 0.0) for m in head_masks], axis=0)


def _stack(a):
    return jnp.concatenate([a[:, GLA_DV * h : GLA_DV * (h + 1)] for h in range(GLA_HEADS)], axis=0)


def _unstack(a):
    return jnp.concatenate([a[GLA_CHUNK * h : GLA_CHUNK * (h + 1)] for h in range(GLA_HEADS)], axis=1)


def _collect(a, head_masks):
    out = None
    for h, m in enumerate(head_masks):
        part = jnp.where(m, a[GLA_CHUNK * h : GLA_CHUNK * (h + 1)], 0.0)
        out = part if out is None else out + part
    return out


def _gla_chunk_terms(q_ref, k_ref, v_ref, b_ref, rows, head_masks, tri, total_row):
    q = q_ref[rows, :] * (GLA_DK**-0.5)
    k = k_ref[rows, :]
    v = v_ref[rows, :]
    b = b_ref[rows, :]
    eb = jnp.exp(b)
    enb = jnp.exp(-b)
    g = jnp.sum(jnp.where(total_row, b, 0.0), axis=0, keepdims=True)
    egb = jnp.exp(g - b)
    qt = q * eb
    kt = k * enb
    kh = k * egb
    q_heads = _spread(qt, head_masks)
    attn = jnp.where(tri, _dot_nt(q_heads, kt), 0.0)
    return v, eb, enb, egb, jnp.exp(g), qt, kt, kh, q_heads, attn


def _gla_specs(s, tb, rev_blocks):
    nb = s // tb
    rb = (lambda i: nb - 1 - i) if rev_blocks else (lambda i: i)
    q_spec = pl.BlockSpec((tb, GLA_K_TOTAL), lambda i: (rb(i), 0))
    k_spec = pl.BlockSpec((tb, GLA_K_TOTAL), lambda i: (rb(i), 1))
    v_spec = pl.BlockSpec((tb, GLA_V_TOTAL), lambda i: (rb(i), 1))
    b_spec = pl.BlockSpec((tb, GLA_K_TOTAL), lambda i: (rb(i), 0))
    o_spec = pl.BlockSpec((tb, GLA_V_TOTAL), lambda i: (rb(i), 0))
    st_spec = pl.BlockSpec((tb // GLA_CHUNK, GLA_DV, GLA_K_TOTAL), lambda i: (rb(i), 0, 0))
    return nb, q_spec, k_spec, v_spec, b_spec, o_spec, st_spec


def _gla_fwd_chunk(cidx, q_ref, k_ref, v_ref, b_ref, o_ref, sv_ref, st_ref, masks):
    head_masks, tri, total_row = masks
    rows = pl.ds(pl.multiple_of(cidx * GLA_CHUNK, GLA_CHUNK), GLA_CHUNK)
    v, _, _, _, eg, _, _, kh, q_heads, attn = _gla_chunk_terms(q_ref, k_ref, v_ref, b_ref, rows, head_masks, tri, total_row)
    o = jnp.concatenate(
        [_dot(attn[GLA_CHUNK * h : GLA_CHUNK * (h + 1)], v[:, GLA_DV * h : GLA_DV * (h + 1)]) for h in range(GLA_HEADS)], axis=1
    )
    st = st_ref[...]
    o_ref[rows, :] = o + _unstack(_dot_nt(q_heads, st))
    sv_ref[cidx] = st
    st_ref[...] = st * eg + _dot_tn(_stack(v), _spread(kh, head_masks))


def _gla_fwd(z, b_f, b_b, *, name):
    s = z.shape[0]
    tb = _rows(s)
    cpb = tb // GLA_CHUNK
    nb, qf, kf, vf, bf, of, sf = _gla_specs(s, tb, False)
    _, qr, kr, vr, br, orr, sr = _gla_specs(s, tb, True)

    def body(qf_ref, kf_ref, vf_ref, bf_ref, qr_ref, kr_ref, vr_ref, br_ref, of_ref, svf_ref, or_ref, svr_ref, stf_ref, str_ref):
        masks_f, masks_r = _gla_masks(False), _gla_masks(True)

        @pl.when(pl.program_id(0) == 0)
        def _():
            stf_ref[...] = jnp.zeros_like(stf_ref)
            str_ref[...] = jnp.zeros_like(str_ref)

        def chunk(ci, carry):
            _gla_fwd_chunk(ci, qf_ref, kf_ref, vf_ref, bf_ref, of_ref, svf_ref, stf_ref, masks_f)
            _gla_fwd_chunk(cpb - 1 - ci, qr_ref, kr_ref, vr_ref, br_ref, or_ref, svr_ref, str_ref, masks_r)
            return carry

        lax.fori_loop(0, cpb, chunk, 0)

    o_shape = jax.ShapeDtypeStruct((s, GLA_V_TOTAL), F32)
    st_shape = jax.ShapeDtypeStruct((s // GLA_CHUNK, GLA_DV, GLA_K_TOTAL), F32)
    return pl.pallas_call(
        body,
        name=name,
        grid=(nb,),
        in_specs=[qf, kf, vf, bf, qr, kr, vr, br],
        out_specs=[of, sf, orr, sr],
        out_shape=[o_shape, st_shape, o_shape, st_shape],
        scratch_shapes=[pltpu.VMEM((GLA_DV, GLA_K_TOTAL), F32)] * 2,
        compiler_params=_cp(("arbitrary",)),
    )(z, z, z, b_f, z, z, z, b_b)


def _gla_bwd_chunk(cidx, q_ref, k_ref, v_ref, b_ref, do_ref, sv_ref, dqkv_ref, db_ref, dst_ref, masks):
    head_masks, tri, total_row = masks
    rows = pl.ds(pl.multiple_of(cidx * GLA_CHUNK, GLA_CHUNK), GLA_CHUNK)
    v, eb, enb, egb, eg, qt, kt, kh, q_heads, attn = _gla_chunk_terms(q_ref, k_ref, v_ref, b_ref, rows, head_masks, tri, total_row)
    do_c = do_ref[rows, :]
    st = sv_ref[cidx]
    dst = dst_ref[...]
    do_s, v_s = _stack(do_c), _stack(v)
    hs = lambda a, h: a[GLA_CHUNK * h : GLA_CHUNK * (h + 1)]
    vs = lambda a, h: a[:, GLA_DV * h : GLA_DV * (h + 1)]
    dattn = jnp.concatenate([_dot_nt(vs(do_c, h), vs(v, h)) for h in range(GLA_HEADS)], axis=0)
    dattn = jnp.where(tri, dattn, 0.0)
    dv = jnp.concatenate([_dot_tn(hs(attn, h), vs(do_c, h)) for h in range(GLA_HEADS)], axis=1)
    dv = dv + _unstack(_dot_nt(_spread(kh, head_masks), dst))
    dqt = _collect(_dot(do_s, st), head_masks)
    dkt = jnp.zeros_like(dqt)
    for h in range(GLA_HEADS):
        dqt = dqt + jnp.where(head_masks[h], _dot(hs(dattn, h), kt), 0.0)
        dkt = dkt + jnp.where(head_masks[h], _dot_tn(hs(dattn, h), qt), 0.0)
    dkh = _collect(_dot(v_s, dst), head_masks)
    dg = jnp.sum(dkh * kh, axis=0, keepdims=True) + jnp.sum(dst * st, axis=0, keepdims=True) * eg
    db = dqt * qt - dkt * kt - dkh * kh + jnp.where(total_row, dg, 0.0)
    dq = dqt * eb * (GLA_DK**-0.5)
    dk = dkt * enb + dkh * egb
    dqkv_ref[rows, :] = jnp.concatenate([dq, dk, dv], axis=1)
    db_ref[rows, :] = db
    dst_ref[...] = dst * eg + _dot_tn(do_s, q_heads)


def _gla_bwd(z, b_f, b_b, do, st_f, st_b, *, name):
    s = z.shape[0]
    tb = _rows(s)
    cpb = tb // GLA_CHUNK
    wide = 2 * GLA_K_TOTAL + GLA_V_TOTAL
    nb, qf, kf, vf, bf, of, sf = _gla_specs(s, tb, True)
    _, qr, kr, vr, br, orr, sr = _gla_specs(s, tb, False)
    gf = pl.BlockSpec((tb, wide), lambda i: (nb - 1 - i, 0))
    gr = pl.BlockSpec((tb, wide), lambda i: (i, 0))

    def body(qf_ref, kf_ref, vf_ref, bf_ref, dof_ref, svf_ref, qr_ref, kr_ref, vr_ref, br_ref, dor_ref, svr_ref,
             gf_ref, dbf_ref, gr_ref, dbr_ref, dstf_ref, dstr_ref):
        masks_f, masks_r = _gla_masks(False), _gla_masks(True)

        @pl.when(pl.program_id(0) == 0)
        def _():
            dstf_ref[...] = jnp.zeros_like(dstf_ref)
            dstr_ref[...] = jnp.zeros_like(dstr_ref)

        def chunk(ci, carry):
            _gla_bwd_chunk(cpb - 1 - ci, qf_ref, kf_ref, vf_ref, bf_ref, dof_ref, svf_ref, gf_ref, dbf_ref, dstf_ref, masks_f)
            _gla_bwd_chunk(ci, qr_ref, kr_ref, vr_ref, br_ref, dor_ref, svr_ref, gr_ref, dbr_ref, dstr_ref, masks_r)
            return carry

        lax.fori_loop(0, cpb, chunk, 0)

    g_shape = jax.ShapeDtypeStruct((s, wide), F32)
    db_shape = jax.ShapeDtypeStruct((s, GLA_K_TOTAL), F32)
    return pl.pallas_call(
        body,
        name=name,
        grid=(nb,),
        in_specs=[qf, kf, vf, bf, of, sf, qr, kr, vr, br, orr, sr],
        out_specs=[gf, bf, gr, br],
        out_shape=[g_shape, db_shape, g_shape, db_shape],
        scratch_shapes=[pltpu.VMEM((GLA_DV, GLA_K_TOTAL), F32)] * 2,
        compiler_params=_cp(("arbitrary",)),
    )(z, z, z, b_f, do, st_f, z, z, z, b_b, do, st_b)


HALO = 8


def _halo_specs(s, ts, width, col):
    last = s // HALO - 1
    per = ts // HALO
    prev = pl.BlockSpec((HALO, width), lambda i: (jnp.maximum(i * per - 1, 0), col))
    nxt = pl.BlockSpec((HALO, width), lambda i: (jnp.minimum((i + 1) * per, last), col))
    return prev, nxt


def _group_ones():
    r = lax.broadcasted_iota(jnp.int32, (CONV_WIDTH, CONV_WIDTH), 0) >> 6
    c = lax.broadcasted_iota(jnp.int32, (CONV_WIDTH, CONV_WIDTH), 1) >> 6
    return (r == c).astype(BF16)


def _conv_terms(cc_ext, cu_ext, cw, valid):
    n = cc_ext.shape[0]
    hc = jnp.where(valid, cc_ext * cu_ext, 0.0)
    hc_prev = pltpu.roll(hc, 1, 0)
    hc_next = pltpu.roll(hc, n - 1, 0)
    conv = cw[0:1] * hc_prev + cw[1:2] * hc + cw[2:3] * hc_next
    return hc, hc_prev, hc_next, conv


def _ext(prev_ref, cur_ref, next_ref):
    return jnp.concatenate([prev_ref[...], cur_ref[...], next_ref[...]], axis=0)


def _valid_rows(ts, s):
    row = lax.broadcasted_iota(jnp.int32, (ts + 2 * HALO, 1), 0) + (pl.program_id(0) * ts - HALO)
    return (row >= 0) & (row < s)


def _head_norm(o, gn):
    out = []
    for h in range(GLA_HEADS):
        oh = o[:, GLA_DV * h : GLA_DV * (h + 1)]
        r = lax.rsqrt(jnp.mean(oh * oh, axis=-1, keepdims=True) + EPS)
        out.append((oh * r, r))
    return out


def _mix_fwd(z, o_f, o_b, conv_w, conv_norm, gla_norm, *, name):
    s = z.shape[0]
    ts = _rows(s)
    cprev, cnext = _halo_specs(s, ts, CONV_WIDTH, 1)
    uprev, unext = _halo_specs(s, ts, CONV_WIDTH, 2)

    def body(cb_ref, cc_ref, cu_ref, ccp_ref, ccn_ref, cup_ref, cun_ref, g_ref, of_ref, ob_ref, cw_ref, cn_ref, gn_ref, y_ref):
        valid = _valid_rows(ts, s)
        _, _, _, conv = _conv_terms(_ext(ccp_ref, cc_ref, ccn_ref), _ext(cup_ref, cu_ref, cun_ref), cw_ref[...], valid)
        yc = cb_ref[...] * conv[HALO : HALO + ts]
        ms = _dot_split(yc * yc, _group_ones()) * (1.0 / CONV_GROUP)
        y_conv = yc * lax.rsqrt(ms + EPS) * cn_ref[...]
        gate = g_ref[...]
        silu = gate * _sigmoid(gate)
        gn = gn_ref[...]
        y_gla = jnp.concatenate([oh * gn for oh, _ in _head_norm(of_ref[...] + ob_ref[...], gn)], axis=1) * silu
        y_ref[...] = jnp.concatenate([y_conv, y_gla], axis=1).astype(y_ref.dtype)

    col = lambda c, w=CONV_WIDTH: pl.BlockSpec((ts, w), lambda i: (i, c))
    return pl.pallas_call(
        body,
        name=name,
        grid=(s // ts,),
        in_specs=[col(0), col(1), col(2), cprev, cnext, uprev, unext, col(3), col(0), col(0),
                  pl.BlockSpec((CONV_K, CONV_WIDTH), lambda i: (0, 0)), pl.BlockSpec((1, CONV_WIDTH), lambda i: (0, 0)),
                  pl.BlockSpec((1, GLA_DV), lambda i: (0, 0))],
        out_specs=pl.BlockSpec((ts, D_MODEL), lambda i: (i, 0)),
        out_shape=jax.ShapeDtypeStruct((s, D_MODEL), _CD),
        compiler_params=_cp(("parallel",)),
    )(z, z, z, z, z, z, z, z, o_f, o_b, conv_w, conv_norm, gla_norm)


def _mix_bwd(z, o_f, o_b, dy, conv_w, conv_norm, gla_norm, *, name):
    s = z.shape[0]
    ts = _rows(s)
    halos = [_halo_specs(s, ts, CONV_WIDTH, c) for c in (0, 1, 2)]
    dprev, dnext = _halo_specs(s, ts, CONV_WIDTH, 0)

    def body(cb_ref, cc_ref, cu_ref, cbp_ref, cbn_ref, ccp_ref, ccn_ref, cup_ref, cun_ref, g_ref, of_ref, ob_ref,
             dyc_ref, dyg_ref, dyp_ref, dyn_ref, cw_ref, cn_ref, gn_ref, dza_ref, do_ref, dcw_ref, dcn_ref, dgn_ref):
        n = ts + 2 * HALO
        valid = _valid_rows(ts, s)
        cw = cw_ref[...]
        cn = cn_ref[...]
        ones = _group_ones()
        cb = _ext(cbp_ref, cb_ref, cbn_ref)
        cc = _ext(ccp_ref, cc_ref, ccn_ref)
        cu = _ext(cup_ref, cu_ref, cun_ref)
        dy = _ext(dyp_ref, dyc_ref, dyn_ref)
        hc, hc_prev, hc_next, conv = _conv_terms(cc, cu, cw, valid)
        yc = cb * conv
        r = lax.rsqrt(_dot_split(yc * yc, ones) * (1.0 / CONV_GROUP) + EPS)
        yh = yc * r
        dyh = dy * cn
        dyc = r * (dyh - yh * (_dot_split(dyh * yh, ones) * (1.0 / CONV_GROUP)))
        dconv = jnp.where(valid, dyc * cb, 0.0)
        dhc = cw[0:1] * pltpu.roll(dconv, n - 1, 0) + cw[1:2] * dconv + cw[2:3] * pltpu.roll(dconv, 1, 0)
        mid = lambda a: a[HALO : HALO + ts]
        dza_ref[:, 0 : 3 * CONV_WIDTH] = jnp.concatenate([mid(dyc * conv), mid(dhc * cu), mid(dhc * cc)], axis=1).astype(dza_ref.dtype)
        dconv_m = mid(dconv)
        colsum = lambda a: jnp.sum(a, axis=0, keepdims=True)
        dcw = jnp.concatenate([colsum(dconv_m * mid(hc_prev)), colsum(dconv_m * mid(hc)), colsum(dconv_m * mid(hc_next))], axis=0)
        dcn = colsum(mid(dy * yh))

        gate = g_ref[...]
        sg = _sigmoid(gate)
        silu = gate * sg
        gn = gn_ref[...]
        dyg = dyg_ref[...]
        don = dyg * silu
        heads = _head_norm(of_ref[...] + ob_ref[...], gn)
        on = jnp.concatenate([oh * gn for oh, _ in heads], axis=1)
        dza_ref[:, 3 * CONV_WIDTH : ZA_COLS] = (dyg * on * (sg * (1.0 + gate * (1.0 - sg)))).astype(dza_ref.dtype)
        dgn = jnp.zeros((1, GLA_DV), F32)
        dos = []
        for h, (oh, rh) in enumerate(heads):
            donh = don[:, GLA_DV * h : GLA_DV * (h + 1)]
            dgn = dgn + colsum(donh * oh)
            doh = donh * gn
            dos.append(rh * (doh - oh * jnp.mean(doh * oh, axis=-1, keepdims=True)))
        do_ref[...] = jnp.concatenate(dos, axis=1)

        first = pl.program_id(0) == 0

        @pl.when(first)
        def _():
            dcw_ref[...] = dcw
            dcn_ref[...] = dcn
            dgn_ref[...] = dgn

        @pl.when(jnp.logical_not(first))
        def _():
            dcw_ref[...] += dcw
            dcn_ref[...] += dcn
            dgn_ref[...] += dgn

    col = lambda c, w=CONV_WIDTH: pl.BlockSpec((ts, w), lambda i: (i, c))
    cw_spec = pl.BlockSpec((CONV_K, CONV_WIDTH), lambda i: (0, 0))
    cn_spec = pl.BlockSpec((1, CONV_WIDTH), lambda i: (0, 0))
    gn_spec = pl.BlockSpec((1, GLA_DV), lambda i: (0, 0))
    return pl.pallas_call(
        body,
        name=name,
        grid=(s // ts,),
        in_specs=[col(0), col(1), col(2), halos[0][0], halos[0][1], halos[1][0], halos[1][1], halos[2][0], halos[2][1],
                  col(3), col(0), col(0), col(0), col(1), dprev, dnext, cw_spec, cn_spec, gn_spec],
        out_specs=[pl.BlockSpec((ts, ZA_COLS), lambda i: (i, 0)), col(0), cw_spec, cn_spec, gn_spec],
        out_shape=[
            jax.ShapeDtypeStruct((s, ZA_COLS), _CD),
            jax.ShapeDtypeStruct((s, GLA_V_TOTAL), F32),
            jax.ShapeDtypeStruct((CONV_K, CONV_WIDTH), F32),
            jax.ShapeDtypeStruct((1, CONV_WIDTH), F32),
            jax.ShapeDtypeStruct((1, GLA_DV), F32),
        ],
        compiler_params=_cp(("arbitrary",)),
    )(z, z, z, z, z, z, z, z, z, z, o_f, o_b, dy, dy, dy, dy, conv_w, conv_norm, gla_norm)


def _xa_probs(q_ref, kv_ref, h):
    qh = q_ref[:, XA_HEAD_DIM * h : XA_HEAD_DIM * (h + 1)]
    kh = kv_ref[:, XA_HEAD_DIM * h : XA_HEAD_DIM * (h + 1)]
    vh = kv_ref[:, D_MODEL + XA_HEAD_DIM * h : D_MODEL + XA_HEAD_DIM * (h + 1)]
    sc = _dot_nt(qh, kh) * (XA_HEAD_DIM**-0.5)
    e = jnp.exp(sc - jnp.max(sc, axis=-1, keepdims=True))
    return qh, kh, vh, e / jnp.sum(e, axis=-1, keepdims=True)


def _xattn_fwd(qx, kv, *, name):
    s = qx.shape[0]
    ts = _rows(s)

    def body(q_ref, kv_ref, o_ref):
        outs = []
        for h in range(XA_HEADS):
            _, _, vh, p = _xa_probs(q_ref, kv_ref, h)
            outs.append(_dot(p, vh))
        o_ref[...] = jnp.concatenate(outs, axis=1).astype(o_ref.dtype)

    return pl.pallas_call(
        body,
        name=name,
        grid=(s // ts,),
        in_specs=[pl.BlockSpec((ts, D_MODEL), lambda i: (i, 0)), pl.BlockSpec((N_MEM, 2 * D_MODEL), lambda i: (0, 0))],
        out_specs=pl.BlockSpec((ts, D_MODEL), lambda i: (i, 0)),
        out_shape=jax.ShapeDtypeStruct((s, D_MODEL), _CD),
        compiler_params=_cp(("parallel",)),
    )(qx, kv)


def _xattn_bwd(qx, kv, dox, *, name):
    s = qx.shape[0]
    ts = _rows(s)

    def body(q_ref, kv_ref, do_ref, dq_ref, dkv_ref):
        dqs, dks, dvs = [], [], []
        for h in range(XA_HEADS):
            qh, kh, vh, p = _xa_probs(q_ref, kv_ref, h)
            doh = do_ref[:, XA_HEAD_DIM * h : XA_HEAD_DIM * (h + 1)]
            dp = _dot_nt(doh, vh)
            ds = p * (dp - jnp.sum(dp * p, axis=-1, keepdims=True)) * (XA_HEAD_DIM**-0.5)
            dqs.append(_dot(ds, kh))
            dks.append(_dot_tn(ds, qh))
            dvs.append(_dot_tn(p, doh))
        dq_ref[...] = jnp.concatenate(dqs, axis=1).astype(dq_ref.dtype)
        dkv = jnp.concatenate(dks + dvs, axis=1)

        @pl.when(pl.program_id(0) == 0)
        def _():
            dkv_ref[...] = dkv

        @pl.when(pl.program_id(0) > 0)
        def _():
            dkv_ref[...] += dkv

    tile = pl.BlockSpec((ts, D_MODEL), lambda i: (i, 0))
    kv_spec = pl.BlockSpec((N_MEM, 2 * D_MODEL), lambda i: (0, 0))
    return pl.pallas_call(
        body,
        name=name,
        grid=(s // ts,),
        in_specs=[tile, kv_spec, tile],
        out_specs=[tile, kv_spec],
        out_shape=[jax.ShapeDtypeStruct((s, D_MODEL), _CD), jax.ShapeDtypeStruct((N_MEM, 2 * D_MODEL), F32)],
        compiler_params=_cp(("arbitrary",)),
    )(qx, kv, dox)


def _adamw_math(w, g, m, v):
    m = ADAM_B1 * m + (1.0 - ADAM_B1) * g
    v = ADAM_B2 * v + (1.0 - ADAM_B2) * (g * g)
    m_hat = m / (1.0 - ADAM_B1**ADAM_STEP)
    v_hat = v / (1.0 - ADAM_B2**ADAM_STEP)
    delta = -ADAM_LR * (m_hat / (jnp.sqrt(v_hat) + ADAM_EPS) + ADAM_WD * w)
    return delta, m, v


def _adamw(w, g, m, v, *, name):
    r, c = w.shape
    tr = _pick(r, (256, 128, 64, 32, 16, 8))

    def body(w_ref, g_ref, m_ref, v_ref, d_ref, nm_ref, nv_ref):
        d_ref[...], nm_ref[...], nv_ref[...] = _adamw_math(w_ref[...], g_ref[...], m_ref[...], v_ref[...])

    tile = pl.BlockSpec((tr, c), lambda i: (i, 0))
    return pl.pallas_call(
        body,
        name=name,
        grid=(r // tr,),
        in_specs=[tile] * 4,
        out_specs=[tile] * 3,
        out_shape=[jax.ShapeDtypeStruct((r, c), F32)] * 3,
        compiler_params=_cp(("parallel",)),
    )(w, g, m, v)


def _adamw_small(groups, *, name):
    n = len(groups)

    def body(*refs):
        ins, outs = refs[: 4 * n], refs[4 * n :]
        for i in range(n):
            w_ref, g_ref, m_ref, v_ref = ins[4 * i : 4 * i + 4]
            outs[3 * i][...], outs[3 * i + 1][...], outs[3 * i + 2][...] = _adamw_math(w_ref[...], g_ref[...], m_ref[...], v_ref[...])

    flat = [a for grp in groups for a in grp]
    vm = pl.BlockSpec(memory_space=pltpu.VMEM)
    res = pl.pallas_call(
        body,
        name=name,
        in_specs=[vm] * (4 * n),
        out_specs=[vm] * (3 * n),
        out_shape=[jax.ShapeDtypeStruct(grp[0].shape, F32) for grp in groups for _ in range(3)],
        compiler_params=_cp(),
    )(*flat)
    return [tuple(res[3 * i : 3 * i + 3]) for i in range(n)]


def _place():
    return lax.axis_index("x"), lax.axis_index("y"), lax.axis_index("c")


def _rel_chip(x, y, k):
    return (1 - x if k & 2 else x), (1 - y if k & 1 else y)


def _half(c, rh):
    return pl.ds(pl.multiple_of(c * rh, 16), rh)


def _gather_weights(pack):
    r, w = pack.shape
    rh = r // 2

    def body(p_ref, q_ref, send_sems, recv_sems):
        x, y, c = _place()
        j = 2 * x + y
        rows = _half(c, rh)

        def to_chip(k):
            cx, cy = _rel_chip(x, y, k)
            return pltpu.make_async_remote_copy(
                src_ref=p_ref.at[rows], dst_ref=q_ref.at[j, rows], send_sem=send_sems.at[k - 1], recv_sem=recv_sems.at[k - 1],
                device_id=(cx, cy, c), device_id_type=MESH)

        def to_sibling(k):
            cx, cy = _rel_chip(x, y, k)
            slot = q_ref.at[2 * cx + cy, rows]
            return pltpu.make_async_remote_copy(
                src_ref=slot, dst_ref=slot, send_sem=send_sems.at[2 + k], recv_sem=recv_sems.at[2 + k],
                device_id=(x, y, 1 - c), device_id_type=MESH)

        first = [to_chip(k) for k in range(1, N_CHIPS)]
        passed = [to_sibling(k) for k in range(1, N_CHIPS)]
        own = pltpu.make_async_remote_copy(
            src_ref=p_ref, dst_ref=q_ref.at[j], send_sem=send_sems.at[6], recv_sem=recv_sems.at[6],
            device_id=(x, y, 1 - c), device_id_type=MESH)
        for cp in first:
            cp.start()
        own.start()
        for cp, fw in zip(first, passed):
            cp.wait_recv()
            fw.start()
        for fw in passed:
            fw.wait_recv()
        own.wait_recv()
        for cp in first + passed + [own]:
            cp.wait_send()

    return pl.pallas_call(
        body,
        name="gather_weights",
        in_specs=[ANY],
        out_specs=ANY,
        out_shape=jax.ShapeDtypeStruct((N_CHIPS, r, w), pack.dtype),
        scratch_shapes=[pltpu.SemaphoreType.DMA((7,)), pltpu.SemaphoreType.DMA((7,))],
        compiler_params=pltpu.CompilerParams(has_side_effects=True),
    )(pack)


def _swap_halves(g):
    n, r, w = g.shape
    rh = r // 2

    def body(g_ref, o_ref, send_sem, recv_sem):
        x, y, c = _place()
        cp = pltpu.make_async_remote_copy(
            src_ref=g_ref.at[:, _half(1 - c, rh)], dst_ref=o_ref, send_sem=send_sem, recv_sem=recv_sem,
            device_id=(x, y, 1 - c), device_id_type=MESH)
        cp.start()
        cp.wait()

    return pl.pallas_call(
        body,
        name="grads_to_sibling",
        in_specs=[ANY],
        out_specs=ANY,
        out_shape=jax.ShapeDtypeStruct((n, rh, w), g.dtype),
        scratch_shapes=[pltpu.SemaphoreType.DMA, pltpu.SemaphoreType.DMA],
        compiler_params=pltpu.CompilerParams(has_side_effects=True),
    )(g)


def _chip_sums(g, got, where):
    n, r, w = g.shape
    rh = r // 2
    nt = rh // PACK_TILE

    def body(where_ref, g_ref, got_ref, o_ref):
        o_ref[...] = (g_ref[...] + got_ref[...]).astype(o_ref.dtype)

    return pl.pallas_call(
        body,
        name="chip_sums",
        grid_spec=pltpu.PrefetchScalarGridSpec(
            num_scalar_prefetch=1,
            grid=(n, nt),
            in_specs=[pl.BlockSpec((1, PACK_TILE, w), lambda a, i, wh: (a, wh[0] * nt + i, 0)),
                      pl.BlockSpec((1, PACK_TILE, w), lambda a, i, wh: (a, i, 0))],
            out_specs=pl.BlockSpec((1, PACK_TILE, w), lambda a, i, wh: (a, i, 0)),
        ),
        out_shape=jax.ShapeDtypeStruct((n, rh, w), _TD),
        compiler_params=_cp(("parallel", "parallel")),
    )(where, g, got)


def _exchange_chip_sums(h):
    n, rh, w = h.shape

    def body(h_ref, o_ref, send_sems, recv_sems):
        x, y, c = _place()
        j = 2 * x + y
        copies = []
        for k in range(1, N_CHIPS):
            cx, cy = _rel_chip(x, y, k)
            copies.append(pltpu.make_async_remote_copy(
                src_ref=h_ref.at[2 * cx + cy], dst_ref=o_ref.at[k - 1], send_sem=send_sems.at[k - 1], recv_sem=recv_sems.at[k - 1],
                device_id=(cx, cy, c), device_id_type=MESH))
        for cp in copies:
            cp.start()
        for cp in copies:
            cp.wait()

    return pl.pallas_call(
        body,
        name="chip_sums_exchange",
        in_specs=[ANY],
        out_specs=ANY,
        out_shape=jax.ShapeDtypeStruct((N_CHIPS - 1, rh, w), h.dtype),
        scratch_shapes=[pltpu.SemaphoreType.DMA((3,)), pltpu.SemaphoreType.DMA((3,))],
        compiler_params=pltpu.CompilerParams(has_side_effects=True),
    )(h)


def _shard_sum(g, got, others, where):
    n, r, w = g.shape
    rh = r // 2
    nt = rh // PACK_TILE

    def body(where_ref, g_ref, got_ref, oth_ref, o_ref):
        acc = g_ref[0] + got_ref[0]
        for k in range(N_CHIPS - 1):
            acc = acc + oth_ref[k].astype(F32)
        o_ref[...] = acc

    return pl.pallas_call(
        body,
        name="shard_sum",
        grid_spec=pltpu.PrefetchScalarGridSpec(
            num_scalar_prefetch=1,
            grid=(nt,),
            in_specs=[pl.BlockSpec((1, PACK_TILE, w), lambda i, wh: (wh[1], wh[0] * nt + i, 0)),
                      pl.BlockSpec((1, PACK_TILE, w), lambda i, wh: (wh[1], i, 0)),
                      pl.BlockSpec((N_CHIPS - 1, PACK_TILE, w), lambda i, wh: (0, i, 0))],
            out_specs=pl.BlockSpec((PACK_TILE, w), lambda i, wh: (i, 0)),
        ),
        out_shape=jax.ShapeDtypeStruct((rh, w), F32),
        compiler_params=_cp(("parallel",)),
    )(where, g, got, others)


def _join_halves(e):
    rh, w = e.shape

    def body(e_ref, o_ref, send_sem, recv_sem, local_sem):
        x, y, c = _place()
        rows = _half(c, rh)
        mine = pltpu.make_async_copy(e_ref, o_ref.at[rows], local_sem)
        mine.start()
        cp = pltpu.make_async_remote_copy(
            src_ref=e_ref, dst_ref=o_ref.at[rows], send_sem=send_sem, recv_sem=recv_sem, device_id=(x, y, 1 - c), device_id_type=MESH)
        cp.start()
        cp.wait()
        mine.wait()

    return pl.pallas_call(
        body,
        name="shard_to_sibling",
        in_specs=[ANY],
        out_specs=ANY,
        out_shape=jax.ShapeDtypeStruct((2 * rh, w), e.dtype),
        scratch_shapes=[pltpu.SemaphoreType.DMA, pltpu.SemaphoreType.DMA, pltpu.SemaphoreType.DMA],
        compiler_params=pltpu.CompilerParams(has_side_effects=True),
    )(e)


HBM = pl.BlockSpec(memory_space=pltpu.HBM)
SEM = pl.BlockSpec(memory_space=pltpu.SEMAPHORE)
EFFECT = pltpu.SideEffectType.DATAFLOW_SIDE_EFFECTING


def _in_hbm(a):
    return pltpu.with_memory_space_constraint(a, pltpu.HBM)


def _gather_copies(p_ref, land_ref, send_sems, recv_sems):
    rh = p_ref.shape[0] // 2
    x, y, c = _place()
    rows = _half(c, rh)
    copies = []
    for k in range(1, N_CHIPS):
        cx, cy = _rel_chip(x, y, k)
        copies.append(pltpu.make_async_remote_copy(
            src_ref=p_ref.at[rows], dst_ref=land_ref.at[2 * x + y, rows], send_sem=send_sems.at[k - 1], recv_sem=recv_sems.at[k - 1],
            device_id=(cx, cy, c), device_id_type=MESH))
    copies.append(pltpu.make_async_remote_copy(
        src_ref=p_ref, dst_ref=land_ref.at[2 * x + y], send_sem=send_sems.at[N_CHIPS - 1], recv_sem=recv_sems.at[N_CHIPS - 1],
        device_id=(x, y, 1 - c), device_id_type=MESH))
    return copies


def _gather_start(pack, after, *, name):
    r, w = pack.shape

    def body(p_ref, land_ref, after_ref, send_sems, recv_sems, p_thru, land_thru, token):
        for cp in _gather_copies(p_ref, land_ref, send_sems, recv_sems):
            cp.start()
        token[...] = jnp.zeros_like(token)

    return pl.pallas_call(
        body,
        name=name,
        out_shape=(pltpu.SemaphoreType.DMA((N_CHIPS,)), pltpu.SemaphoreType.DMA((N_CHIPS,)), pltpu.HBM((r, w), pack.dtype),
                   pltpu.HBM((N_CHIPS, r, w), pack.dtype), jax.ShapeDtypeStruct((8, 128), F32)),
        in_specs=(HBM, HBM, ANY),
        out_specs=(SEM, SEM, HBM, HBM, pl.BlockSpec(memory_space=pltpu.VMEM)),
        input_output_aliases={0: 2, 1: 3},
        compiler_params=pltpu.CompilerParams(has_side_effects=EFFECT),
    )(_in_hbm(pack), _in_hbm(lax.empty((N_CHIPS, r, w), pack.dtype)), after)


def _gather_wait(send_sems, recv_sems, pack, land, after, *, name):
    def body(p_ref, land_ref, send_sems, recv_sems, after_ref, p_out, land_out):
        for cp in _gather_copies(p_ref, land_ref, send_sems, recv_sems):
            cp.wait_send()
            cp.wait_recv()

    return pl.pallas_call(
        body,
        name=name,
        out_shape=(pltpu.HBM(pack.shape, pack.dtype), pltpu.HBM(land.shape, land.dtype)),
        in_specs=(HBM, HBM, SEM, SEM, ANY),
        out_specs=(HBM, HBM),
        input_output_aliases={0: 0, 1: 1},
        compiler_params=pltpu.CompilerParams(has_side_effects=EFFECT),
    )(pack, land, send_sems, recv_sems, after)


def _gather_spread(land, *, name):
    n, r, w = land.shape
    rh = r // 2

    def body(land_ref, o_ref, send_sems, recv_sems):
        x, y, c = _place()
        rows = _half(c, rh)
        copies = []
        for k in range(1, N_CHIPS):
            cx, cy = _rel_chip(x, y, k)
            copies.append(pltpu.make_async_remote_copy(
                src_ref=land_ref.at[2 * cx + cy, rows], dst_ref=o_ref.at[2 * cx + cy, rows], send_sem=send_sems.at[k - 1],
                recv_sem=recv_sems.at[k - 1], device_id=(x, y, 1 - c), device_id_type=MESH))
        for cp in copies:
            cp.start()
        for cp in copies:
            cp.wait()

    return pl.pallas_call(
        body,
        name=name,
        in_specs=[ANY],
        out_specs=ANY,
        out_shape=jax.ShapeDtypeStruct(land.shape, land.dtype),
        input_output_aliases={0: 0},
        scratch_shapes=[pltpu.SemaphoreType.DMA((N_CHIPS - 1,)), pltpu.SemaphoreType.DMA((N_CHIPS - 1,))],
        compiler_params=pltpu.CompilerParams(has_side_effects=True),
    )(land)


N_PARTS = 2 * (N_CHIPS - 1)


def _scatter_copies(lo_ref, g_ref, land_lo_ref, land_f_ref, send_sems, recv_sems, starting):
    rh = g_ref.shape[1] // 2
    x, y, c = _place()
    copies = []
    for k in range(1, N_CHIPS):
        cx, cy = _rel_chip(x, y, k)
        for i in range(2):
            part = 2 * (k - 1) + (c if starting else i)
            copies.append(pltpu.make_async_remote_copy(
                src_ref=lo_ref.at[2 * cx + cy, pl.ds(i * rh, rh)], dst_ref=land_lo_ref.at[part],
                send_sem=send_sems.at[2 * (k - 1) + i], recv_sem=recv_sems.at[part], device_id=(cx, cy, i), device_id_type=MESH))
    copies.append(pltpu.make_async_remote_copy(
        src_ref=g_ref.at[2 * x + y, _half(1 - c, rh)], dst_ref=land_f_ref, send_sem=send_sems.at[N_PARTS], recv_sem=recv_sems.at[N_PARTS],
        device_id=(x, y, 1 - c), device_id_type=MESH))
    return copies


def _scatter_start(g_lo, g, *, name):
    n, r, w = g.shape
    rh = r // 2

    def body(lo_ref, g_ref, land_lo_ref, land_f_ref, send_sems, recv_sems, lo_thru, g_thru, land_lo_thru, land_f_thru, token):
        for cp in _scatter_copies(lo_ref, g_ref, land_lo_ref, land_f_ref, send_sems, recv_sems, True):
            cp.start()
        token[...] = jnp.zeros_like(token)

    return pl.pallas_call(
        body,
        name=name,
        out_shape=(pltpu.SemaphoreType.DMA((N_PARTS + 1,)), pltpu.SemaphoreType.DMA((N_PARTS + 1,)), pltpu.HBM(g_lo.shape, g_lo.dtype),
                   pltpu.HBM(g.shape, g.dtype), pltpu.HBM((N_PARTS, rh, w), g_lo.dtype), pltpu.HBM((rh, w), g.dtype),
                   jax.ShapeDtypeStruct((8, 128), F32)),
        in_specs=(HBM, HBM, HBM, HBM),
        out_specs=(SEM, SEM, HBM, HBM, HBM, HBM, pl.BlockSpec(memory_space=pltpu.VMEM)),
        input_output_aliases={0: 2, 1: 3, 2: 4, 3: 5},
        compiler_params=pltpu.CompilerParams(has_side_effects=EFFECT),
    )(_in_hbm(g_lo), _in_hbm(g), _in_hbm(lax.empty((N_PARTS, rh, w), g_lo.dtype)), _in_hbm(lax.empty((rh, w), g.dtype)))


def _scatter_wait(send_sems, recv_sems, g_lo, g, land_lo, land_f, after, *, name):
    def body(lo_ref, g_ref, land_lo_ref, land_f_ref, send_sems, recv_sems, after_ref, o0, o1, o2, o3):
        for cp in _scatter_copies(lo_ref, g_ref, land_lo_ref, land_f_ref, send_sems, recv_sems, False):
            cp.wait_send()
            cp.wait_recv()

    arrays = (g_lo, g, land_lo, land_f)
    return pl.pallas_call(
        body,
        name=name,
        out_shape=tuple(pltpu.HBM(a.shape, a.dtype) for a in arrays),
        in_specs=(HBM, HBM, HBM, HBM, SEM, SEM, ANY),
        out_specs=(HBM, HBM, HBM, HBM),
        input_output_aliases={0: 0, 1: 1, 2: 2, 3: 3},
        compiler_params=pltpu.CompilerParams(has_side_effects=EFFECT),
    )(*arrays, send_sems, recv_sems, after)


def _scatter_sum(g, land_lo, land_f, where, *, name):
    n, r, w = g.shape
    rh = r // 2
    tr = _pick(rh, (256, 160, 80))
    nt = rh // tr

    def body(where_ref, g_ref, f_ref, lo_ref, o_ref):
        acc = g_ref[0] + f_ref[...]
        for part in range(N_PARTS):
            acc = acc + lo_ref[part].astype(F32)
        o_ref[...] = acc

    return pl.pallas_call(
        body,
        name=name,
        grid_spec=pltpu.PrefetchScalarGridSpec(
            num_scalar_prefetch=1,
            grid=(nt,),
            in_specs=[pl.BlockSpec((1, tr, w), lambda i, wh: (wh[1], wh[0] * nt + i, 0)),
                      pl.BlockSpec((tr, w), lambda i, wh: (i, 0)),
                      pl.BlockSpec((N_PARTS, tr, w), lambda i, wh: (0, i, 0))],
            out_specs=pl.BlockSpec((tr, w), lambda i, wh: (wh[0] * nt + i, 0)),
        ),
        out_shape=jax.ShapeDtypeStruct((r, w), F32),
        compiler_params=_cp(("parallel",)),
    )(where, g, land_f, land_lo)


def _swap_all(shards, *, name):
    n = len(shards)

    def body(*refs):
        ins, outs = refs[:n], refs[n : 2 * n]
        send_sems, recv_sems = refs[2 * n :]
        x, y, c = _place()
        copies = []
        for i, (e_ref, o_ref) in enumerate(zip(ins, outs)):
            rows = _half(c, e_ref.shape[0] // 2)
            copies.append(pltpu.make_async_remote_copy(src_ref=e_ref.at[rows], dst_ref=o_ref.at[rows], send_sem=send_sems.at[i],
                                                       recv_sem=recv_sems.at[i], device_id=(x, y, 1 - c), device_id_type=MESH))
        for cp in copies:
            cp.start()
        for cp in copies:
            cp.wait()

    return pl.pallas_call(
        body,
        name=name,
        in_specs=[ANY] * n,
        out_specs=[ANY] * n,
        out_shape=[jax.ShapeDtypeStruct(e.shape, e.dtype) for e in shards],
        input_output_aliases={i: i for i in range(n)},
        scratch_shapes=[pltpu.SemaphoreType.DMA((n,)), pltpu.SemaphoreType.DMA((n,))],
        compiler_params=pltpu.CompilerParams(has_side_effects=True),
    )(*shards)


def _sum_small(small, after):
    n_dev = 8

    def body(s_ref, after_ref, o_ref, all_ref, send_sems, recv_sems):
        x, y, c = _place()
        me = 4 * x + 2 * y + c
        all_ref[me] = s_ref[...]
        copies = []
        for k in range(1, n_dev):
            cx, cy = _rel_chip(x, y, k >> 1)
            cc = 1 - c if k & 1 else c
            copies.append(pltpu.make_async_remote_copy(
                src_ref=s_ref, dst_ref=all_ref.at[me], send_sem=send_sems.at[k - 1], recv_sem=recv_sems.at[k - 1],
                device_id=(cx, cy, cc), device_id_type=MESH))
        for cp in copies:
            cp.start()
        for cp in copies:
            cp.wait()
        acc = all_ref[0]
        for a in range(1, n_dev):
            acc = acc + all_ref[a]
        o_ref[...] = acc

    vm = pl.BlockSpec(memory_space=pltpu.VMEM)
    return pl.pallas_call(
        body,
        name="sum_small",
        in_specs=[vm, ANY],
        out_specs=vm,
        out_shape=jax.ShapeDtypeStruct(small.shape, F32),
        scratch_shapes=[pltpu.VMEM((n_dev,) + small.shape, F32), pltpu.SemaphoreType.DMA((n_dev - 1,)), pltpu.SemaphoreType.DMA((n_dev - 1,))],
        compiler_params=pltpu.CompilerParams(has_side_effects=True),
    )(small, after)


MATS = {"w_in": (776, True), "w_out": (256, False), "w_xq": (256, False), "w_xkv": (512, True), "w_xo": (256, False),
        "w_up": (1024, True), "w_down": (1024, False)}
GATHER_FIRST = ("w_in",)
GATHER_REST = ("w_out", "w_xq", "w_xkv", "w_xo", "w_up", "w_down")
GRAD_GROUPS = (("w_up", "w_down"), ("w_out", "w_xq", "w_xkv", "w_xo"), ("w_in",))


def _group_rows(names):
    n = sum(MATS[name][0] for name in names)
    return n + (-n) % 32


def _pack(pieces, rows):
    p = jnp.concatenate(pieces, axis=0) if len(pieces) > 1 else pieces[0]
    return jnp.pad(p, ((0, rows - p.shape[0]), (0, 0))) if rows > p.shape[0] else p


def _unpack(rows, names):
    out, off = {}, 0
    for name in names:
        out[name] = rows[off : off + MATS[name][0]]
        off += MATS[name][0]
    return out


SMALL = (
    ("mix_norm", 1024), ("conv_norm", 512), ("b_af", 256), ("b_ab", 256), ("gla_norm", 128), ("xa_norm", 1024), ("mem_norm", 1024),
    ("mlp_norm", 1024), ("final_norm", 1024), ("conv_w", 1536), ("w_af", 4096), ("w_ab", 4096), ("loss", 128),
)


def kernel(x, mem, mix_norm, w_in, conv_w, conv_norm, w_af, b_af, w_ab, b_ab, gla_norm, w_out, xa_norm, mem_norm, w_xq, w_xkv, w_xo, mlp_norm, w_up, w_down, final_norm, loss_target, m_mix_norm, m_w_in, m_conv_w, m_conv_norm, m_w_af, m_b_af, m_w_ab, m_b_ab, m_gla_norm, m_w_out, m_xa_norm, m_mem_norm, m_w_xq, m_w_xkv, m_w_xo, m_mlp_norm, m_w_up, m_w_down, m_final_norm, v_mix_norm, v_w_in, v_conv_w, v_conv_norm, v_w_af, v_b_af, v_w_ab, v_b_ab, v_gla_norm, v_w_out, v_xa_norm, v_mem_norm, v_w_xq, v_w_xkv, v_w_xo, v_mlp_norm, v_w_up, v_w_down, v_final_norm):
    given = dict(locals())
    xi, yi, ci = _place()
    chip = 2 * xi + yi
    where = jnp.stack([ci, chip]).astype(jnp.int32)

    lo = {name: (given[name][0].T if MATS[name][1] else given[name][0]).astype(_CD) for name in MATS}
    pack_rest = _pack([lo[name] for name in GATHER_REST], _group_rows(GATHER_REST))
    pack_first = _pack([lo[name] for name in GATHER_FIRST], _group_rows(GATHER_FIRST))
    got_first = _gather_weights(pack_first)

    def whole(got, off, rows):
        return got[:, off : off + rows].reshape(N_CHIPS * rows, D_MODEL)

    w_in_t = whole(got_first, 0, MATS["w_in"][0])
    w_za = jnp.concatenate([w_in_t[0:1536], w_in_t[2560:3072]], axis=0)
    w_zb = jnp.concatenate([w_in_t[1536:2560], w_in_t[3072:W_IN_COLS], jnp.zeros((ZB_COLS - 1056, D_MODEL), _CD)], axis=0)

    def placed(shard, full_shape, col):
        return lax.dynamic_update_slice(jnp.zeros(full_shape, F32), shard, (0, col)).reshape(-1, 128)

    sw = jnp.concatenate([
        placed(conv_w[0], (CONV_K, CONV_WIDTH), 128 * chip),
        placed(w_af[0], (GLA_LOWRANK, GLA_K_TOTAL), 64 * chip),
        placed(w_ab[0], (GLA_LOWRANK, GLA_K_TOTAL), 64 * chip),
    ], axis=0)
    sw = jnp.pad(sw, ((0, SMALL_ROWS - sw.shape[0]), (0, 0))) * (ci == 0).astype(F32)
    sw = _sum_small(sw, got_first)
    rest_send, rest_recv, pack_rest, land_rest, rest_token = _gather_start(pack_rest, sw, name="gather_rest_start")
    conv_w_full = sw[0:12].reshape(CONV_K, CONV_WIDTH)
    w_af_full = sw[12:44].reshape(GLA_LOWRANK, GLA_K_TOTAL)
    w_ab_full = sw[44:76].reshape(GLA_LOWRANK, GLA_K_TOTAL)
    waf_p = jnp.pad(w_af_full, ((0, 128 - GLA_LOWRANK), (0, 0))).astype(_CD)
    wab_p = jnp.pad(w_ab_full, ((GLA_LOWRANK, 128 - 2 * GLA_LOWRANK), (0, 0))).astype(_CD)

    xs, mems, tgt = x[0], mem[0], loss_target[0]
    add_res = lambda acc, res: (acc + res,)
    behind = lambda gain, token: gain + token[0, 0]

    h1 = _rms_fwd(xs, behind(mix_norm, rest_token), name="norm_mix")
    z_b = _mm(h1, w_zb, mode="nt", name="proj_in_b", tn=ZB_COLS)
    z_a = _mm(h1, w_za, mode="nt", name="proj_in_a", tm=512, tn=ZA_COLS)
    b_f, b_b = _gate_fwd(z_b, waf_p, wab_p, b_af, b_ab, name="gates")
    o_f, st_f, o_b, st_b = _gla_fwd(z_b, b_f, b_b, name="gla_scan")
    y = _mix_fwd(z_a, o_f, o_b, conv_w_full, conv_norm, gla_norm, name="mix_out")
    pack_rest, land_rest = _gather_wait(rest_send, rest_recv, pack_rest, land_rest, y, name="gather_rest_wait")
    gathered = _gather_spread(land_rest, name="gather_rest_spread")
    wt, off = {}, 0
    for name in GATHER_REST:
        wt[name] = whole(gathered, off, MATS[name][0])
        off += MATS[name][0]
    x1, hx = _mm_rows(y, wt["w_out"], mode="nn", name="proj_out", rows=(xs,), vecs=(xa_norm,), out_rows=(F32, _CD), epilogue=_ep_residual_norm)
    qx = _mm(hx, wt["w_xq"], mode="nn", name="proj_xq", out_dtypes=(_CD,))
    hmem = _rms_fwd(mems, mem_norm, name="norm_mem")
    kv = _mm(hmem, wt["w_xkv"], mode="nt", name="proj_xkv", out_dtypes=(_CD,))
    ox = _xattn_fwd(qx, kv, name="xattn")
    x2, hm = _mm_rows(ox, wt["w_xo"], mode="nn", name="proj_xo", rows=(x1,), vecs=(mlp_norm,), out_rows=(F32, _CD), epilogue=_ep_residual_norm)
    act, relu_u = _mm(hm, wt["w_up"], mode="nt", name="mlp_up", out_dtypes=(_CD, _CD),
                      epilogue=lambda acc: (jnp.square(jnp.maximum(acc, 0.0)), jnp.maximum(acc, 0.0)))
    dx3, dx3_lo, loss_part, g_final_norm = _mm_rows(
        act, wt["w_down"], mode="nn", name="mlp_down", rows=(x2, tgt), vecs=(final_norm.reshape(1, D_MODEL),),
        out_rows=(F32, _CD), out_vecs=(128, D_MODEL), epilogue=_ep_loss)

    grads_t = {}

    def start_group(names, tag):
        rows = _group_rows(names)
        g = jnp.stack([_pack([grads_t[name][a * MATS[name][0] : (a + 1) * MATS[name][0]] for name in names], rows) for a in range(N_CHIPS)])
        return _scatter_start(g.astype(_TD), g, name="grads_" + tag + "_start")

    def finish_group(state, after, tag):
        send_sems, recv_sems, g_lo, g, land_lo, land_f, _ = state
        g_lo, g, land_lo, land_f = _scatter_wait(send_sems, recv_sems, g_lo, g, land_lo, land_f, after, name="grads_" + tag + "_wait")
        return _scatter_sum(g, land_lo, land_f, where, name="grads_" + tag + "_sum")

    def new_packs(names):
        shape = (N_CHIPS, _group_rows(names), D_MODEL)
        return lax.empty(shape, F32), lax.empty(shape, _TD)

    def grad_into(packs, names, which, a, b, name):
        off = sum(MATS[other][0] for other in names[: names.index(which)])
        return _mm_tn_into(a, b, packs, rows=MATS[which][0], off=off, name=name)

    du = _mm(dx3_lo, wt["w_down"], mode="nt", name="mlp_down_dx", out_dtypes=(_CD,), extras=(relu_u,),
             epilogue=lambda acc, rr: (acc * (2.0 * rr.astype(F32)),))
    packs = new_packs(GRAD_GROUPS[0])
    packs = grad_into(packs, GRAD_GROUPS[0], "w_down", act, dx3_lo, "mlp_down_dw")
    packs = grad_into(packs, GRAD_GROUPS[0], "w_up", du, hm, "mlp_up_dw")
    mlp_state = _scatter_start(packs[1], packs[0], name="grads_mlp_start")
    dx2, dx2_lo, g_mlp_norm = _mm_rows(
        du, wt["w_up"], mode="nn", name="mlp_up_dx", rows=(x2, dx3), vecs=(behind(mlp_norm, mlp_state[-1]),),
        out_rows=(F32, _CD), out_vecs=(D_MODEL,), epilogue=_ep_norm_bwd)
    dox = _mm(dx2_lo, wt["w_xo"], mode="nt", name="proj_xo_dx", out_dtypes=(_CD,))
    packs = new_packs(GRAD_GROUPS[1])
    packs = grad_into(packs, GRAD_GROUPS[1], "w_xo", ox, dx2_lo, "proj_xo_dw")
    dqx, dkv = _xattn_bwd(qx, kv, dox, name="xattn_bwd")
    packs = grad_into(packs, GRAD_GROUPS[1], "w_xq", hx, dqx, "proj_xq_dw")
    dx1, dx1_lo, g_xa_norm = _mm_rows(
        dqx, wt["w_xq"], mode="nt", name="proj_xq_dx", rows=(x1, dx2), vecs=(xa_norm,),
        out_rows=(F32, _CD), out_vecs=(D_MODEL,), epilogue=_ep_norm_bwd)
    dkv_lo = dkv.astype(_CD)
    packs = grad_into(packs, GRAD_GROUPS[1], "w_xkv", dkv_lo, hmem, "proj_xkv_dw")
    dhmem = _mm(dkv_lo, wt["w_xkv"], mode="nn", name="proj_xkv_dx")
    (g_mem_norm,) = _rms_bwd(mems, mem_norm, dhmem, name="norm_mem_bwd", want_dx=False, want_lo=False)
    dy = _mm(dx1_lo, wt["w_out"], mode="nt", name="proj_out_dx")
    packs = grad_into(packs, GRAD_GROUPS[1], "w_out", y, dx1_lo, "proj_out_dw")
    attn_state = _scatter_start(packs[1], packs[0], name="grads_attn_start")
    dz_a, do, g_conv_w, g_conv_norm, g_gla_norm = _mix_bwd(z_a, o_f, o_b, dy, conv_w_full, behind(conv_norm, attn_state[-1]), gla_norm, name="mix_out_bwd")
    dqkv_f, db_f, dqkv_b, db_b = _gla_bwd(z_b, b_f, b_b, do, st_f, st_b, name="gla_scan_bwd")
    dz_b, g_waf_p, g_wab_p, g_b_af, g_b_ab = _gate_bwd(z_b, waf_p, wab_p, b_af, b_ab, db_f, db_b, dqkv_f, dqkv_b, name="gates_bwd")
    g_za = _mm_tn(dz_a, h1, name="proj_in_a_dw")
    g_zb = _mm_tn(dz_b, h1, name="proj_in_b_dw")
    grads_t["w_in"] = jnp.concatenate([g_za[0:1536], g_zb[0:1024], g_za[1536:2048], g_zb[1024:1056]], axis=0)
    in_state = start_group(GRAD_GROUPS[2], "in")
    dh1_a = _mm(dz_a, w_za, mode="nn", name="proj_in_a_dx", tm=512, tk=ZA_COLS)
    grad_x, g_mix_norm = _mm_rows(
        dz_b, w_zb, mode="nn", name="proj_in_b_dx", rows=(xs, dx1, dh1_a), vecs=(behind(mix_norm, in_state[-1]),),
        out_rows=(F32,), out_vecs=(D_MODEL,), epilogue=_ep_norm_bwd)

    half_mlp = finish_group(mlp_state, grad_x, "mlp")
    half_attn = finish_group(attn_state, half_mlp, "attn")
    half_in = finish_group(in_state, half_attn, "in")
    g_shard = {}
    for names, rows in zip(GRAD_GROUPS, _swap_all([half_mlp, half_attn, half_in], name="shards_to_sibling")):
        for name, g in _unpack(rows, names).items():
            g_shard[name] = g.T if MATS[name][1] else g

    small_vals = dict(mix_norm=g_mix_norm, conv_norm=g_conv_norm, b_af=g_b_af, b_ab=g_b_ab, gla_norm=g_gla_norm, xa_norm=g_xa_norm,
                      mem_norm=g_mem_norm, mlp_norm=g_mlp_norm, final_norm=g_final_norm, conv_w=g_conv_w,
                      w_af=g_waf_p[0:GLA_LOWRANK], w_ab=g_wab_p[GLA_LOWRANK : 2 * GLA_LOWRANK], loss=loss_part)
    small = jnp.concatenate([small_vals[name].reshape(-1, 128) for name, _ in SMALL], axis=0)
    small = _sum_small(jnp.pad(small, ((0, SMALL_ROWS - small.shape[0]), (0, 0))), loss_part)
    g_small, off = {}, 0
    for name, n in SMALL:
        g_small[name] = small[off : off + n // 128]
        off += n // 128
    loss = g_small["loss"][0, 0]
    g_small["conv_w"] = lax.dynamic_slice(g_small["conv_w"].reshape(CONV_K, CONV_WIDTH), (0, 128 * chip), (CONV_K, 128))
    g_small["w_af"] = lax.dynamic_slice(g_small["w_af"].reshape(GLA_LOWRANK, GLA_K_TOTAL), (0, 64 * chip), (GLA_LOWRANK, 64))
    g_small["w_ab"] = lax.dynamic_slice(g_small["w_ab"].reshape(GLA_LOWRANK, GLA_K_TOTAL), (0, 64 * chip), (GLA_LOWRANK, 64))

    names = ["mix_norm", "w_in", "conv_w", "conv_norm", "w_af", "b_af", "w_ab", "b_ab", "gla_norm", "w_out", "xa_norm", "mem_norm",
             "w_xq", "w_xkv", "w_xo", "mlp_norm", "w_up", "w_down", "final_norm"]
    big_names = list(MATS)
    as2d = lambda a: a.reshape(1, -1) if a.ndim == 1 else a.reshape(a.shape[-2:])
    grads, deltas, new_m, new_v = {}, {}, {}, {}
    for name in big_names:
        grads[name] = g_shard[name]
        deltas[name], new_m[name], new_v[name] = _adamw(as2d(given[name]), g_shard[name], as2d(given["m_" + name]),
                                                         as2d(given["v_" + name]), name="adamw_" + name)
    small_names = [name for name in names if name not in big_names]
    groups = []
    for name in small_names:
        grads[name] = g_small[name].reshape(as2d(given[name]).shape)
        groups.append((as2d(given[name]), grads[name], as2d(given["m_" + name]), as2d(given["v_" + name])))
    for name, res in zip(small_names, _adamw_small(groups, name="adamw_small")):
        deltas[name], new_m[name], new_v[name] = res

    like = lambda name, a: a.reshape(given[name].shape)
    return (loss, grad_x[None], *[like(n, grads[n]) for n in names], *[like(n, deltas[n]) for n in names],
            *[like(n, new_m[n]) for n in names], *[like(n, new_v[n]) for n in names])
```

```python
import functools

import jax
import jax.numpy as jnp
from jax import lax
from jax.experimental import pallas as pl
from jax.experimental.pallas import tpu as pltpu

F32 = jnp.float32
BF16 = jnp.bfloat16
_CD = jnp.bfloat16
_TD = jnp.bfloat16

D_MODEL = 1024
N_MEM = 256
CONV_WIDTH = 512
CONV_GROUP = 64
CONV_K = 3
GLA_HEADS = 4
GLA_DK = 64
GLA_DV = 128
GLA_K_TOTAL = 256
GLA_V_TOTAL = 512
GLA_LOWRANK = 16
GLA_GATE_SCALE = 1.0 / 16.0
GLA_CHUNK = 64
XA_HEADS = 4
XA_HEAD_DIM = 256
D_FF = 4096
EPS = 1e-6
W_IN_COLS = 3104
ZA_COLS = 2048
ZB_COLS = 1152
LR_COL = 1024

ADAM_LR = 0.001
ADAM_B1 = 0.9
ADAM_B2 = 0.999
ADAM_EPS = 1e-08
ADAM_WD = 0.01
ADAM_STEP = 10

N_CHIPS = 4
PACK_W = 1024
PACK_ROWS = 4160
PACK_TILE = 160
SMALL_ROWS = 128

_TS = 512
_VMEM = 44 * 1024 * 1024
MESH = pl.DeviceIdType.MESH
ANY = pl.BlockSpec(memory_space=pl.ANY)


def _cp(sem=None, **kw):
    return pltpu.CompilerParams(dimension_semantics=sem, vmem_limit_bytes=_VMEM, **kw)


def _dot(a, b):
    return jnp.dot(a.astype(_CD), b.astype(_CD), preferred_element_type=F32)


def _dot_nt(a, b):
    return lax.dot_general(a.astype(_CD), b.astype(_CD), (((1,), (1,)), ((), ())), preferred_element_type=F32)


def _dot_tn(a, b):
    return lax.dot_general(a.astype(_CD), b.astype(_CD), (((0,), (0,)), ((), ())), preferred_element_type=F32)


def _dot_split(x, ones):
    hi = x.astype(BF16)
    r = x - hi.astype(F32)
    mid = r.astype(BF16)
    lo = (r - mid.astype(F32)).astype(BF16)
    d = lambda p: jnp.dot(p, ones, preferred_element_type=F32)
    return d(hi) + d(mid) + d(lo)


def _pick(n, cands=(1024, 640, 512, 256, 128)):
    for t in cands:
        if n % t == 0:
            return t
    return n


def _rows(s):
    return min(_TS, s)


def _sigmoid(v):
    e = jnp.exp(-jnp.abs(v))
    return jnp.where(v >= 0, 1.0 / (1.0 + e), e / (1.0 + e))


def _mm(a, b, *, mode, name, out_dtypes=(F32,), extras=(), epilogue=None, tm=None, tn=None, tk=None):
    m, k = a.shape
    n = b.shape[1] if mode == "nn" else b.shape[0]
    tm = min(m, tm or 1024)
    tn = tn or _pick(n)
    tk = tk or _pick(k)
    nk = k // tk
    n_ex, n_out = len(extras), len(out_dtypes)

    def body(*refs):
        a_ref, b_ref = refs[:2]
        ex = refs[2 : 2 + n_ex]
        outs = refs[2 + n_ex : 2 + n_ex + n_out]
        part = _dot(a_ref[...], b_ref[...]) if mode == "nn" else _dot_nt(a_ref[...], b_ref[...])

        def finish(acc):
            res = epilogue(acc, *[e[...] for e in ex]) if epilogue else (acc,)
            for o, r in zip(outs, res):
                o[...] = r.astype(o.dtype)

        if nk == 1:
            finish(part)
        else:
            acc_ref = refs[-1]
            kk = pl.program_id(2)

            @pl.when(kk == 0)
            def _():
                acc_ref[...] = part

            @pl.when(kk > 0)
            def _():
                acc_ref[...] += part

            @pl.when(kk == nk - 1)
            def _():
                finish(acc_ref[...])

    b_spec = pl.BlockSpec((tk, tn), lambda i, j, kk: (kk, j)) if mode == "nn" else pl.BlockSpec((tn, tk), lambda i, j, kk: (j, kk))
    tile = pl.BlockSpec((tm, tn), lambda i, j, kk: (i, j))
    out = pl.pallas_call(
        body,
        name=name,
        grid=(m // tm, n // tn, nk),
        in_specs=[pl.BlockSpec((tm, tk), lambda i, j, kk: (i, kk)), b_spec] + [tile] * n_ex,
        out_specs=[tile] * n_out,
        out_shape=[jax.ShapeDtypeStruct((m, n), dt) for dt in out_dtypes],
        scratch_shapes=[pltpu.VMEM((tm, tn), F32)] if nk > 1 else [],
        compiler_params=_cp(("parallel", "parallel", "arbitrary")),
    )(a, b, *extras)
    return out[0] if n_out == 1 else out


def _mm_tn(a, b, *, name):
    s, m = a.shape
    n = b.shape[1]
    cap = max(128, (1 << 20) // n)
    tm = _pick(m, tuple(t for t in (512, 640, 384, 256, 128) if t <= max(cap, 128)))
    ts = min(s, 1 << (((1 << 22) // n).bit_length() - 1))
    ns = s // ts

    def body(a_ref, b_ref, o_ref):
        part = _dot_tn(a_ref[...], b_ref[...])
        if ns == 1:
            o_ref[...] = part
        else:
            ss = pl.program_id(1)

            @pl.when(ss == 0)
            def _():
                o_ref[...] = part

            @pl.when(ss > 0)
            def _():
                o_ref[...] += part

    return pl.pallas_call(
        body,
        name=name,
        grid=(m // tm, ns),
        in_specs=[pl.BlockSpec((ts, tm), lambda i, ss: (ss, i)), pl.BlockSpec((ts, n), lambda i, ss: (ss, 0))],
        out_specs=pl.BlockSpec((tm, n), lambda i, ss: (i, 0)),
        out_shape=jax.ShapeDtypeStruct((m, n), F32),
        compiler_params=_cp(("parallel", "arbitrary")),
    )(a, b)


def _mm_tn_into(a, b, packs, *, rows, off, name):
    s, m = a.shape
    n = b.shape[1]
    tm = 512
    tr = min(tm, rows)
    per, chips = rows // tr, tm // tr
    ts = min(s, 1 << (((1 << 22) // n).bit_length() - 1))
    ns = s // ts

    def body(a_ref, b_ref, f_in, lo_in, f_ref, lo_ref):
        part = _dot_tn(a_ref[...], b_ref[...])
        pieces = [part[c * tr : (c + 1) * tr] for c in range(chips)]
        if ns == 1:
            for c, p in enumerate(pieces):
                f_ref[c] = p
                lo_ref[c] = p.astype(lo_ref.dtype)
        else:
            ss = pl.program_id(1)

            @pl.when(ss == 0)
            def _():
                for c, p in enumerate(pieces):
                    f_ref[c] = p

            @pl.when(ss > 0)
            def _():
                for c, p in enumerate(pieces):
                    f_ref[c] += p

            @pl.when(ss == ns - 1)
            def _():
                lo_ref[...] = f_ref[...].astype(lo_ref.dtype)

    spec = pl.BlockSpec((chips, tr, n), lambda i, ss: (i // per, off // tr + i % per, 0))
    return pl.pallas_call(
        body,
        name=name,
        grid=(m // tm, ns),
        in_specs=[pl.BlockSpec((ts, tm), lambda i, ss: (ss, i)), pl.BlockSpec((ts, n), lambda i, ss: (ss, 0)), ANY, ANY],
        out_specs=[spec, spec],
        out_shape=[jax.ShapeDtypeStruct(p.shape, p.dtype) for p in packs],
        input_output_aliases={2: 0, 3: 1},
        compiler_params=_cp(("parallel", "arbitrary")),
    )(a, b, *packs)


def _mm_rows(a, b, *, mode, name, rows=(), vecs=(), out_rows=(), out_vecs=(), epilogue, tm=512):
    m, k = a.shape
    n = b.shape[1] if mode == "nn" else b.shape[0]
    tm = min(m, tm)
    parts = 2 if tm % 256 == 0 else 1
    n_r, n_v, n_or, n_ov = len(rows), len(vecs), len(out_rows), len(out_vecs)

    def body(*refs):
        a_ref, b_ref = refs[:2]
        r_refs = refs[2 : 2 + n_r]
        v_refs = refs[2 + n_r : 2 + n_r + n_v]
        or_refs = refs[2 + n_r + n_v : 2 + n_r + n_v + n_or]
        ov_refs = refs[2 + n_r + n_v + n_or :]
        res_vecs = None
        for p in range(parts):
            rs = slice(p * tm // parts, (p + 1) * tm // parts)
            acc = _dot(a_ref[rs, :], b_ref[...]) if mode == "nn" else _dot_nt(a_ref[rs, :], b_ref[...])
            res_rows, part_vecs = epilogue(acc, [r[rs, :] for r in r_refs], [v[...] for v in v_refs])
            for o, r in zip(or_refs, res_rows):
                o[rs, :] = r.astype(o.dtype)
            res_vecs = part_vecs if res_vecs is None else [s + t for s, t in zip(res_vecs, part_vecs)]
        if n_ov:
            first = pl.program_id(0) == 0

            @pl.when(first)
            def _():
                for o, r in zip(ov_refs, res_vecs):
                    o[...] = r

            @pl.when(jnp.logical_not(first))
            def _():
                for o, r in zip(ov_refs, res_vecs):
                    o[...] += r

    tile = pl.BlockSpec((tm, n), lambda i: (i, 0))
    whole = lambda arr: pl.BlockSpec(arr.shape, lambda i: (0, 0))
    vec = lambda w: pl.BlockSpec((1, w), lambda i: (0, 0))
    out = pl.pallas_call(
        body,
        name=name,
        grid=(m // tm,),
        in_specs=[pl.BlockSpec((tm, k), lambda i: (i, 0)), whole(b)] + [tile] * n_r + [vec(v.shape[1]) for v in vecs],
        out_specs=[tile] * n_or + [vec(w) for w in out_vecs],
        out_shape=[jax.ShapeDtypeStruct((m, n), dt) for dt in out_rows] + [jax.ShapeDtypeStruct((1, w), F32) for w in out_vecs],
        compiler_params=_cp(("arbitrary",) if n_ov else ("parallel",)),
    )(a, b, *rows, *vecs)
    return out


def _ep_residual_norm(acc, rows, vecs):
    x = acc + rows[0]
    r = lax.rsqrt(jnp.mean(x * x, axis=-1, keepdims=True) + EPS)
    return [x, x * r * vecs[0]], []


def _ep_norm_bwd(acc, rows, vecs):
    dy = acc
    for extra in rows[2:]:
        dy = dy + extra
    x, dres = rows[0], rows[1]
    r = lax.rsqrt(jnp.mean(x * x, axis=-1, keepdims=True) + EPS)
    xh = x * r
    dxh = dy * vecs[0]
    dx = r * (dxh - xh * jnp.mean(dxh * xh, axis=-1, keepdims=True)) + dres
    return [dx, dx], [jnp.sum(dy * xh, axis=0, keepdims=True)]


def _ep_loss(acc, rows, vecs):
    x = acc + rows[0]
    d = x.shape[-1]
    r = lax.rsqrt(jnp.mean(x * x, axis=-1, keepdims=True) + EPS)
    xh = x * r
    err = xh * vecs[0] - rows[1]
    loss = jnp.zeros((1, 128), F32) + 0.5 * jnp.sum(jnp.mean(err * err, axis=-1, keepdims=True))
    dy = err * (1.0 / d)
    dxh = dy * vecs[0]
    dx = r * (dxh - xh * jnp.mean(dxh * xh, axis=-1, keepdims=True))
    return [dx, dx], [loss, jnp.sum(dy * xh, axis=0, keepdims=True)]


def _rms_fwd(x, g, *, name):
    s, d = x.shape
    ts = _rows(s)

    def body(x_ref, g_ref, o_ref):
        xf = x_ref[...]
        r = lax.rsqrt(jnp.mean(xf * xf, axis=-1, keepdims=True) + EPS)
        o_ref[...] = (xf * r * g_ref[...]).astype(o_ref.dtype)

    return pl.pallas_call(
        body,
        name=name,
        grid=(s // ts,),
        in_specs=[pl.BlockSpec((ts, d), lambda i: (i, 0)), pl.BlockSpec((1, d), lambda i: (0, 0))],
        out_specs=pl.BlockSpec((ts, d), lambda i: (i, 0)),
        out_shape=jax.ShapeDtypeStruct((s, d), _CD),
        compiler_params=_cp(("parallel",)),
    )(x, g)


def _rms_bwd(x, g, dy, dres=None, *, name, want_dx=True, want_lo=True):
    s, d = x.shape
    ts = _rows(s)
    has_res = dres is not None

    def body(*refs):
        x_ref, g_ref, dy_ref = refs[:3]
        pos = 3
        dres_ref = refs[pos] if has_res else None
        pos += has_res
        dx_ref = refs[pos] if want_dx else None
        pos += want_dx
        lo_ref = refs[pos] if want_lo else None
        pos += want_lo
        dg_ref = refs[pos]
        xf = x_ref[...]
        r = lax.rsqrt(jnp.mean(xf * xf, axis=-1, keepdims=True) + EPS)
        xh = xf * r
        dyf = dy_ref[...]
        part = jnp.sum(dyf * xh, axis=0, keepdims=True)

        @pl.when(pl.program_id(0) == 0)
        def _():
            dg_ref[...] = part

        @pl.when(pl.program_id(0) > 0)
        def _():
            dg_ref[...] += part

        if want_dx or want_lo:
            dxh = dyf * g_ref[...]
            dx = r * (dxh - xh * jnp.mean(dxh * xh, axis=-1, keepdims=True))
            if has_res:
                dx = dx + dres_ref[...]
            if want_dx:
                dx_ref[...] = dx
            if want_lo:
                lo_ref[...] = dx.astype(lo_ref.dtype)

    tile = pl.BlockSpec((ts, d), lambda i: (i, 0))
    vec = pl.BlockSpec((1, d), lambda i: (0, 0))
    out_specs, out_shape = [], []
    if want_dx:
        out_specs.append(tile)
        out_shape.append(jax.ShapeDtypeStruct((s, d), F32))
    if want_lo:
        out_specs.append(tile)
        out_shape.append(jax.ShapeDtypeStruct((s, d), _CD))
    out_specs.append(vec)
    out_shape.append(jax.ShapeDtypeStruct((1, d), F32))
    return pl.pallas_call(
        body,
        name=name,
        grid=(s // ts,),
        in_specs=[tile, vec, tile] + ([tile] if has_res else []),
        out_specs=out_specs,
        out_shape=out_shape,
        compiler_params=_cp(("arbitrary",)),
    )(x, g, dy, *([dres] if has_res else []))


def _final_loss(x3, g, tgt, *, name):
    s, d = x3.shape
    ts = _rows(s)

    def body(x_ref, g_ref, t_ref, dx_ref, lo_ref, loss_ref, dg_ref):
        xf = x_ref[...]
        r = lax.rsqrt(jnp.mean(xf * xf, axis=-1, keepdims=True) + EPS)
        xh = xf * r
        gg = g_ref[...]
        err = xh * gg - t_ref[...]
        lpart = jnp.zeros((1, 128), F32) + 0.5 * jnp.sum(jnp.mean(err * err, axis=-1, keepdims=True))
        dy = err * (1.0 / d)
        gpart = jnp.sum(dy * xh, axis=0, keepdims=True)

        @pl.when(pl.program_id(0) == 0)
        def _():
            loss_ref[...] = lpart
            dg_ref[...] = gpart

        @pl.when(pl.program_id(0) > 0)
        def _():
            loss_ref[...] += lpart
            dg_ref[...] += gpart

        dxh = dy * gg
        dx = r * (dxh - xh * jnp.mean(dxh * xh, axis=-1, keepdims=True))
        dx_ref[...] = dx
        lo_ref[...] = dx.astype(lo_ref.dtype)

    tile = pl.BlockSpec((ts, d), lambda i: (i, 0))
    vec = pl.BlockSpec((1, d), lambda i: (0, 0))
    return pl.pallas_call(
        body,
        name=name,
        grid=(s // ts,),
        in_specs=[tile, vec, tile],
        out_specs=[tile, tile, pl.BlockSpec((1, 128), lambda i: (0, 0)), vec],
        out_shape=[
            jax.ShapeDtypeStruct((s, d), F32),
            jax.ShapeDtypeStruct((s, d), _CD),
            jax.ShapeDtypeStruct((1, 128), F32),
            jax.ShapeDtypeStruct((1, d), F32),
        ],
        compiler_params=_cp(("arbitrary",)),
    )(x3, g, tgt)


def _chunk_scan(v, row_in_chunk, suffix):
    t = v.shape[0]
    step = 1
    while step < GLA_CHUNK:
        if suffix:
            v = v + jnp.where(row_in_chunk < GLA_CHUNK - step, pltpu.roll(v, t - step, 0), 0.0)
        else:
            v = v + jnp.where(row_in_chunk >= step, pltpu.roll(v, step, 0), 0.0)
        step *= 2
    return v


def _gate_pre(lr, w_ref, b_ref):
    return _dot(lr, w_ref[...]) + b_ref[...]


def _gate_fwd(z, waf, wab, baf, bab, *, name):
    s = z.shape[0]
    ts = _rows(s)

    def body(lr_ref, waf_ref, wab_ref, baf_ref, bab_ref, bf_ref, bb_ref):
        lr = lr_ref[...]
        ric = lax.broadcasted_iota(jnp.int32, (ts, GLA_K_TOTAL), 0) & (GLA_CHUNK - 1)
        for w_ref, b_ref, o_ref, suffix in ((waf_ref, baf_ref, bf_ref, False), (wab_ref, bab_ref, bb_ref, True)):
            pre = _gate_pre(lr, w_ref, b_ref)
            la = (jnp.minimum(pre, 0.0) - jnp.log(1.0 + jnp.exp(-jnp.abs(pre)))) * GLA_GATE_SCALE
            o_ref[...] = _chunk_scan(la, ric, suffix)

    wspec = pl.BlockSpec((128, GLA_K_TOTAL), lambda i: (0, 0))
    bspec = pl.BlockSpec((1, GLA_K_TOTAL), lambda i: (0, 0))
    tile = pl.BlockSpec((ts, GLA_K_TOTAL), lambda i: (i, 0))
    return pl.pallas_call(
        body,
        name=name,
        grid=(s // ts,),
        in_specs=[pl.BlockSpec((ts, 128), lambda i: (i, LR_COL // 128)), wspec, wspec, bspec, bspec],
        out_specs=[tile, tile],
        out_shape=[jax.ShapeDtypeStruct((s, GLA_K_TOTAL), F32)] * 2,
        compiler_params=_cp(("parallel",)),
    )(z, waf, wab, baf, bab)


def _gate_bwd(z, waf, wab, baf, bab, dbf, dbb, dqkv_f, dqkv_b, *, name):
    s = z.shape[0]
    ts = _rows(s)

    def body(lr_ref, waf_ref, wab_ref, baf_ref, bab_ref, dbf_ref, dbb_ref, gf_ref, gb_ref, dzb_ref, dwf_ref, dwb_ref, dbaf_ref, dbab_ref):
        lr = lr_ref[...]
        ric = lax.broadcasted_iota(jnp.int32, (ts, GLA_K_TOTAL), 0) & (GLA_CHUNK - 1)
        first = pl.program_id(0) == 0
        dlr = None
        for w_ref, b_ref, db_ref, dw_ref, dbias_ref, suffix in (
            (waf_ref, baf_ref, dbf_ref, dwf_ref, dbaf_ref, True),
            (wab_ref, bab_ref, dbb_ref, dwb_ref, dbab_ref, False),
        ):
            pre = _gate_pre(lr, w_ref, b_ref)
            dla = _chunk_scan(db_ref[...], ric, suffix)
            dpre = dla * GLA_GATE_SCALE * _sigmoid(-pre)
            part = _dot_nt(dpre, w_ref[...])
            dlr = part if dlr is None else dlr + part
            dw = _dot_tn(lr, dpre)
            dbias = jnp.sum(dpre, axis=0, keepdims=True)

            @pl.when(first)
            def _():
                dw_ref[...] = dw
                dbias_ref[...] = dbias

            @pl.when(jnp.logical_not(first))
            def _():
                dw_ref[...] += dw
                dbias_ref[...] += dbias

        dzb_ref[...] = jnp.concatenate([gf_ref[...] + gb_ref[...], dlr], axis=1).astype(dzb_ref.dtype)

    wspec = pl.BlockSpec((128, GLA_K_TOTAL), lambda i: (0, 0))
    bspec = pl.BlockSpec((1, GLA_K_TOTAL), lambda i: (0, 0))
    tile = pl.BlockSpec((ts, GLA_K_TOTAL), lambda i: (i, 0))
    wide = pl.BlockSpec((ts, 2 * GLA_K_TOTAL + GLA_V_TOTAL), lambda i: (i, 0))
    return pl.pallas_call(
        body,
        name=name,
        grid=(s // ts,),
        in_specs=[pl.BlockSpec((ts, 128), lambda i: (i, LR_COL // 128)), wspec, wspec, bspec, bspec, tile, tile, wide, wide],
        out_specs=[pl.BlockSpec((ts, ZB_COLS), lambda i: (i, 0)), wspec, wspec, bspec, bspec],
        out_shape=[
            jax.ShapeDtypeStruct((s, ZB_COLS), _CD),
            jax.ShapeDtypeStruct((128, GLA_K_TOTAL), F32),
            jax.ShapeDtypeStruct((128, GLA_K_TOTAL), F32),
            jax.ShapeDtypeStruct((1, GLA_K_TOTAL), F32),
            jax.ShapeDtypeStruct((1, GLA_K_TOTAL), F32),
        ],
        compiler_params=_cp(("arbitrary",)),
    )(z, waf, wab, baf, bab, dbf, dbb, dqkv_f, dqkv_b)


def _gla_masks(rev):
    lane_head = lax.broadcasted_iota(jnp.int32, (1, GLA_K_TOTAL), 1) >> 6
    head_masks = [lane_head == h for h in range(GLA_HEADS)]
    t = lax.broadcasted_iota(jnp.int32, (GLA_HEADS * GLA_CHUNK, GLA_CHUNK), 0) & (GLA_CHUNK - 1)
    u = lax.broadcasted_iota(jnp.int32, (GLA_HEADS * GLA_CHUNK, GLA_CHUNK), 1)
    tri = (u > t) if rev else (u <= t)
    row = lax.broadcasted_iota(jnp.int32, (GLA_CHUNK, GLA_K_TOTAL), 0)
    total_row = row == (0 if rev else GLA_CHUNK - 1)
    return head_masks, tri, total_row


def _spread(a, head_masks):
    return jnp.concatenate([jnp.where(m, a, 0.0) for m in head_masks], axis=0)


def _stack(a):
    return jnp.concatenate([a[:, GLA_DV * h : GLA_DV * (h + 1)] for h in range(GLA_HEADS)], axis=0)


def _unstack(a):
    return jnp.concatenate([a[GLA_CHUNK * h : GLA_CHUNK * (h + 1)] for h in range(GLA_HEADS)], axis=1)


def _collect(a, head_masks):
    out = None
    for h, m in enumerate(head_masks):
        part = jnp.where(m, a[GLA_CHUNK * h : GLA_CHUNK * (h + 1)], 0.0)
        out = part if out is None else out + part
    return out


def _gla_chunk_terms(q_ref, k_ref, v_ref, b_ref, rows, head_masks, tri, total_row):
    q = q_ref[rows, :] * (GLA_DK**-0.5)
    k = k_ref[rows, :]
    v = v_ref[rows, :]
    b = b_ref[rows, :]
    eb = jnp.exp(b)
    enb = jnp.exp(-b)
    g = jnp.sum(jnp.where(total_row, b, 0.0), axis=0, keepdims=True)
    egb = jnp.exp(g - b)
    qt = q * eb
    kt = k * enb
    kh = k * egb
    q_heads = _spread(qt, head_masks)
    attn = jnp.where(tri, _dot_nt(q_heads, kt), 0.0)
    return v, eb, enb, egb, jnp.exp(g), qt, kt, kh, q_heads, attn


def _gla_specs(s, tb, rev_blocks):
    nb = s // tb
    rb = (lambda i: nb - 1 - i) if rev_blocks else (lambda i: i)
    q_spec = pl.BlockSpec((tb, GLA_K_TOTAL), lambda i: (rb(i), 0))
    k_spec = pl.BlockSpec((tb, GLA_K_TOTAL), lambda i: (rb(i), 1))
    v_spec = pl.BlockSpec((tb, GLA_V_TOTAL), lambda i: (rb(i), 1))
    b_spec = pl.BlockSpec((tb, GLA_K_TOTAL), lambda i: (rb(i), 0))
    o_spec = pl.BlockSpec((tb, GLA_V_TOTAL), lambda i: (rb(i), 0))
    st_spec = pl.BlockSpec((tb // GLA_CHUNK, GLA_DV, GLA_K_TOTAL), lambda i: (rb(i), 0, 0))
    return nb, q_spec, k_spec, v_spec, b_spec, o_spec, st_spec


def _gla_fwd_chunk(cidx, q_ref, k_ref, v_ref, b_ref, o_ref, sv_ref, st_ref, masks):
    head_masks, tri, total_row = masks
    rows = pl.ds(pl.multiple_of(cidx * GLA_CHUNK, GLA_CHUNK), GLA_CHUNK)
    v, _, _, _, eg, _, _, kh, q_heads, attn = _gla_chunk_terms(q_ref, k_ref, v_ref, b_ref, rows, head_masks, tri, total_row)
    o = jnp.concatenate(
        [_dot(attn[GLA_CHUNK * h : GLA_CHUNK * (h + 1)], v[:, GLA_DV * h : GLA_DV * (h + 1)]) for h in range(GLA_HEADS)], axis=1
    )
    st = st_ref[...]
    o_ref[rows, :] = o + _unstack(_dot_nt(q_heads, st))
    sv_ref[cidx] = st
    st_ref[...] = st * eg + _dot_tn(_stack(v), _spread(kh, head_masks))


def _gla_fwd(z, b_f, b_b, *, name):
    s = z.shape[0]
    tb = _rows(s)
    cpb = tb // GLA_CHUNK
    nb, qf, kf, vf, bf, of, sf = _gla_specs(s, tb, False)
    _, qr, kr, vr, br, orr, sr = _gla_specs(s, tb, True)

    def body(qf_ref, kf_ref, vf_ref, bf_ref, qr_ref, kr_ref, vr_ref, br_ref, of_ref, svf_ref, or_ref, svr_ref, stf_ref, str_ref):
        masks_f, masks_r = _gla_masks(False), _gla_masks(True)

        @pl.when(pl.program_id(0) == 0)
        def _():
            stf_ref[...] = jnp.zeros_like(stf_ref)
            str_ref[...] = jnp.zeros_like(str_ref)

        def chunk(ci, carry):
            _gla_fwd_chunk(ci, qf_ref, kf_ref, vf_ref, bf_ref, of_ref, svf_ref, stf_ref, masks_f)
            _gla_fwd_chunk(cpb - 1 - ci, qr_ref, kr_ref, vr_ref, br_ref, or_ref, svr_ref, str_ref, masks_r)
            return carry

        lax.fori_loop(0, cpb, chunk, 0)

    o_shape = jax.ShapeDtypeStruct((s, GLA_V_TOTAL), F32)
    st_shape = jax.ShapeDtypeStruct((s // GLA_CHUNK, GLA_DV, GLA_K_TOTAL), F32)
    return pl.pallas_call(
        body,
        name=name,
        grid=(nb,),
        in_specs=[qf, kf, vf, bf, qr, kr, vr, br],
        out_specs=[of, sf, orr, sr],
        out_shape=[o_shape, st_shape, o_shape, st_shape],
        scratch_shapes=[pltpu.VMEM((GLA_DV, GLA_K_TOTAL), F32)] * 2,
        compiler_params=_cp(("arbitrary",)),
    )(z, z, z, b_f, z, z, z, b_b)


def _gla_bwd_chunk(cidx, q_ref, k_ref, v_ref, b_ref, do_ref, sv_ref, dqkv_ref, db_ref, dst_ref, masks):
    head_masks, tri, total_row = masks
    rows = pl.ds(pl.multiple_of(cidx * GLA_CHUNK, GLA_CHUNK), GLA_CHUNK)
    v, eb, enb, egb, eg, qt, kt, kh, q_heads, attn = _gla_chunk_terms(q_ref, k_ref, v_ref, b_ref, rows, head_masks, tri, total_row)
    do_c = do_ref[rows, :]
    st = sv_ref[cidx]
    dst = dst_ref[...]
    do_s, v_s = _stack(do_c), _stack(v)
    hs = lambda a, h: a[GLA_CHUNK * h : GLA_CHUNK * (h + 1)]
    vs = lambda a, h: a[:, GLA_DV * h : GLA_DV * (h + 1)]
    dattn = jnp.concatenate([_dot_nt(vs(do_c, h), vs(v, h)) for h in range(GLA_HEADS)], axis=0)
    dattn = jnp.where(tri, dattn, 0.0)
    dv = jnp.concatenate([_dot_tn(hs(attn, h), vs(do_c, h)) for h in range(GLA_HEADS)], axis=1)
    dv = dv + _unstack(_dot_nt(_spread(kh, head_masks), dst))
    dqt = _collect(_dot(do_s, st), head_masks)
    dkt = jnp.zeros_like(dqt)
    for h in range(GLA_HEADS):
        dqt = dqt + jnp.where(head_masks[h], _dot(hs(dattn, h), kt), 0.0)
        dkt = dkt + jnp.where(head_masks[h], _dot_tn(hs(dattn, h), qt), 0.0)
    dkh = _collect(_dot(v_s, dst), head_masks)
    dg = jnp.sum(dkh * kh, axis=0, keepdims=True) + jnp.sum(dst * st, axis=0, keepdims=True) * eg
    db = dqt * qt - dkt * kt - dkh * kh + jnp.where(total_row, dg, 0.0)
    dq = dqt * eb * (GLA_DK**-0.5)
    dk = dkt * enb + dkh * egb
    dqkv_ref[rows, :] = jnp.concatenate([dq, dk, dv], axis=1)
    db_ref[rows, :] = db
    dst_ref[...] = dst * eg + _dot_tn(do_s, q_heads)


def _gla_bwd(z, b_f, b_b, do, st_f, st_b, *, name):
    s = z.shape[0]
    tb = _rows(s)
    cpb = tb // GLA_CHUNK
    wide = 2 * GLA_K_TOTAL + GLA_V_TOTAL
    nb, qf, kf, vf, bf, of, sf = _gla_specs(s, tb, True)
    _, qr, kr, vr, br, orr, sr = _gla_specs(s, tb, False)
    gf = pl.BlockSpec((tb, wide), lambda i: (nb - 1 - i, 0))
    gr = pl.BlockSpec((tb, wide), lambda i: (i, 0))

    def body(qf_ref, kf_ref, vf_ref, bf_ref, dof_ref, svf_ref, qr_ref, kr_ref, vr_ref, br_ref, dor_ref, svr_ref,
             gf_ref, dbf_ref, gr_ref, dbr_ref, dstf_ref, dstr_ref):
        masks_f, masks_r = _gla_masks(False), _gla_masks(True)

        @pl.when(pl.program_id(0) == 0)
        def _():
            dstf_ref[...] = jnp.zeros_like(dstf_ref)
            dstr_ref[...] = jnp.zeros_like(dstr_ref)

        def chunk(ci, carry):
            _gla_bwd_chunk(cpb - 1 - ci, qf_ref, kf_ref, vf_ref, bf_ref, dof_ref, svf_ref, gf_ref, dbf_ref, dstf_ref, masks_f)
            _gla_bwd_chunk(ci, qr_ref, kr_ref, vr_ref, br_ref, dor_ref, svr_ref, gr_ref, dbr_ref, dstr_ref, masks_r)
            return carry

        lax.fori_loop(0, cpb, chunk, 0)

    g_shape = jax.ShapeDtypeStruct((s, wide), F32)
    db_shape = jax.ShapeDtypeStruct((s, GLA_K_TOTAL), F32)
    return pl.pallas_call(
        body,
        name=name,
        grid=(nb,),
        in_specs=[qf, kf, vf, bf, of, sf, qr, kr, vr, br, orr, sr],
        out_specs=[gf, bf, gr, br],
        out_shape=[g_shape, db_shape, g_shape, db_shape],
        scratch_shapes=[pltpu.VMEM((GLA_DV, GLA_K_TOTAL), F32)] * 2,
        compiler_params=_cp(("arbitrary",)),
    )(z, z, z, b_f, do, st_f, z, z, z, b_b, do, st_b)


HALO = 8


def _halo_specs(s, ts, width, col):
    last = s // HALO - 1
    per = ts // HALO
    prev = pl.BlockSpec((HALO, width), lambda i: (jnp.maximum(i * per - 1, 0), col))
    nxt = pl.BlockSpec((HALO, width), lambda i: (jnp.minimum((i + 1) * per, last), col))
    return prev, nxt


def _group_ones():
    r = lax.broadcasted_iota(jnp.int32, (CONV_WIDTH, CONV_WIDTH), 0) >> 6
    c = lax.broadcasted_iota(jnp.int32, (CONV_WIDTH, CONV_WIDTH), 1) >> 6
    return (r == c).astype(BF16)


def _conv_terms(cc_ext, cu_ext, cw, valid):
    n = cc_ext.shape[0]
    hc = jnp.where(valid, cc_ext * cu_ext, 0.0)
    hc_prev = pltpu.roll(hc, 1, 0)
    hc_next = pltpu.roll(hc, n - 1, 0)
    conv = cw[0:1] * hc_prev + cw[1:2] * hc + cw[2:3] * hc_next
    return hc, hc_prev, hc_next, conv


def _ext(prev_ref, cur_ref, next_ref):
    return jnp.concatenate([prev_ref[...], cur_ref[...], next_ref[...]], axis=0)


def _valid_rows(ts, s):
    row = lax.broadcasted_iota(jnp.int32, (ts + 2 * HALO, 1), 0) + (pl.program_id(0) * ts - HALO)
    return (row >= 0) & (row < s)


def _head_norm(o, gn):
    out = []
    for h in range(GLA_HEADS):
        oh = o[:, GLA_DV * h : GLA_DV * (h + 1)]
        r = lax.rsqrt(jnp.mean(oh * oh, axis=-1, keepdims=True) + EPS)
        out.append((oh * r, r))
    return out


def _mix_fwd(z, o_f, o_b, conv_w, conv_norm, gla_norm, *, name):
    s = z.shape[0]
    ts = _rows(s)
    cprev, cnext = _halo_specs(s, ts, CONV_WIDTH, 1)
    uprev, unext = _halo_specs(s, ts, CONV_WIDTH, 2)

    def body(cb_ref, cc_ref, cu_ref, ccp_ref, ccn_ref, cup_ref, cun_ref, g_ref, of_ref, ob_ref, cw_ref, cn_ref, gn_ref, y_ref):
        valid = _valid_rows(ts, s)
        _, _, _, conv = _conv_terms(_ext(ccp_ref, cc_ref, ccn_ref), _ext(cup_ref, cu_ref, cun_ref), cw_ref[...], valid)
        yc = cb_ref[...] * conv[HALO : HALO + ts]
        ms = _dot_split(yc * yc, _group_ones()) * (1.0 / CONV_GROUP)
        y_conv = yc * lax.rsqrt(ms + EPS) * cn_ref[...]
        gate = g_ref[...]
        silu = gate * _sigmoid(gate)
        gn = gn_ref[...]
        y_gla = jnp.concatenate([oh * gn for oh, _ in _head_norm(of_ref[...] + ob_ref[...], gn)], axis=1) * silu
        y_ref[...] = jnp.concatenate([y_conv, y_gla], axis=1).astype(y_ref.dtype)

    col = lambda c, w=CONV_WIDTH: pl.BlockSpec((ts, w), lambda i: (i, c))
    return pl.pallas_call(
        body,
        name=name,
        grid=(s // ts,),
        in_specs=[col(0), col(1), col(2), cprev, cnext, uprev, unext, col(3), col(0), col(0),
                  pl.BlockSpec((CONV_K, CONV_WIDTH), lambda i: (0, 0)), pl.BlockSpec((1, CONV_WIDTH), lambda i: (0, 0)),
                  pl.BlockSpec((1, GLA_DV), lambda i: (0, 0))],
        out_specs=pl.BlockSpec((ts, D_MODEL), lambda i: (i, 0)),
        out_shape=jax.ShapeDtypeStruct((s, D_MODEL), _CD),
        compiler_params=_cp(("parallel",)),
    )(z, z, z, z, z, z, z, z, o_f, o_b, conv_w, conv_norm, gla_norm)


def _mix_bwd(z, o_f, o_b, dy, conv_w, conv_norm, gla_norm, *, name):
    s = z.shape[0]
    ts = _rows(s)
    halos = [_halo_specs(s, ts, CONV_WIDTH, c) for c in (0, 1, 2)]
    dprev, dnext = _halo_specs(s, ts, CONV_WIDTH, 0)

    def body(cb_ref, cc_ref, cu_ref, cbp_ref, cbn_ref, ccp_ref, ccn_ref, cup_ref, cun_ref, g_ref, of_ref, ob_ref,
             dyc_ref, dyg_ref, dyp_ref, dyn_ref, cw_ref, cn_ref, gn_ref, dza_ref, do_ref, dcw_ref, dcn_ref, dgn_ref):
        n = ts + 2 * HALO
        valid = _valid_rows(ts, s)
        cw = cw_ref[...]
        cn = cn_ref[...]
        ones = _group_ones()
        cb = _ext(cbp_ref, cb_ref, cbn_ref)
        cc = _ext(ccp_ref, cc_ref, ccn_ref)
        cu = _ext(cup_ref, cu_ref, cun_ref)
        dy = _ext(dyp_ref, dyc_ref, dyn_ref)
        hc, hc_prev, hc_next, conv = _conv_terms(cc, cu, cw, valid)
        yc = cb * conv
        r = lax.rsqrt(_dot_split(yc * yc, ones) * (1.0 / CONV_GROUP) + EPS)
        yh = yc * r
        dyh = dy * cn
        dyc = r * (dyh - yh * (_dot_split(dyh * yh, ones) * (1.0 / CONV_GROUP)))
        dconv = jnp.where(valid, dyc * cb, 0.0)
        dhc = cw[0:1] * pltpu.roll(dconv, n - 1, 0) + cw[1:2] * dconv + cw[2:3] * pltpu.roll(dconv, 1, 0)
        mid = lambda a: a[HALO : HALO + ts]
        dza_ref[:, 0 : 3 * CONV_WIDTH] = jnp.concatenate([mid(dyc * conv), mid(dhc * cu), mid(dhc * cc)], axis=1).astype(dza_ref.dtype)
        dconv_m = mid(dconv)
        colsum = lambda a: jnp.sum(a, axis=0, keepdims=True)
        dcw = jnp.concatenate([colsum(dconv_m * mid(hc_prev)), colsum(dconv_m * mid(hc)), colsum(dconv_m * mid(hc_next))], axis=0)
        dcn = colsum(mid(dy * yh))

        gate = g_ref[...]
        sg = _sigmoid(gate)
        silu = gate * sg
        gn = gn_ref[...]
        dyg = dyg_ref[...]
        don = dyg * silu
        heads = _head_norm(of_ref[...] + ob_ref[...], gn)
        on = jnp.concatenate([oh * gn for oh, _ in heads], axis=1)
        dza_ref[:, 3 * CONV_WIDTH : ZA_COLS] = (dyg * on * (sg * (1.0 + gate * (1.0 - sg)))).astype(dza_ref.dtype)
        dgn = jnp.zeros((1, GLA_DV), F32)
        dos = []
        for h, (oh, rh) in enumerate(heads):
            donh = don[:, GLA_DV * h : GLA_DV * (h + 1)]
            dgn = dgn + colsum(donh * oh)
            doh = donh * gn
            dos.append(rh * (doh - oh * jnp.mean(doh * oh, axis=-1, keepdims=True)))
        do_ref[...] = jnp.concatenate(dos, axis=1)

        first = pl.program_id(0) == 0

        @pl.when(first)
        def _():
            dcw_ref[...] = dcw
            dcn_ref[...] = dcn
            dgn_ref[...] = dgn

        @pl.when(jnp.logical_not(first))
        def _():
            dcw_ref[...] += dcw
            dcn_ref[...] += dcn
            dgn_ref[...] += dgn

    col = lambda c, w=CONV_WIDTH: pl.BlockSpec((ts, w), lambda i: (i, c))
    cw_spec = pl.BlockSpec((CONV_K, CONV_WIDTH), lambda i: (0, 0))
    cn_spec = pl.BlockSpec((1, CONV_WIDTH), lambda i: (0, 0))
    gn_spec = pl.BlockSpec((1, GLA_DV), lambda i: (0, 0))
    return pl.pallas_call(
        body,
        name=name,
        grid=(s // ts,),
        in_specs=[col(0), col(1), col(2), halos[0][0], halos[0][1], halos[1][0], halos[1][1], halos[2][0], halos[2][1],
                  col(3), col(0), col(0), col(0), col(1), dprev, dnext, cw_spec, cn_spec, gn_spec],
        out_specs=[pl.BlockSpec((ts, ZA_COLS), lambda i: (i, 0)), col(0), cw_spec, cn_spec, gn_spec],
        out_shape=[
            jax.ShapeDtypeStruct((s, ZA_COLS), _CD),
            jax.ShapeDtypeStruct((s, GLA_V_TOTAL), F32),
            jax.ShapeDtypeStruct((CONV_K, CONV_WIDTH), F32),
            jax.ShapeDtypeStruct((1, CONV_WIDTH), F32),
            jax.ShapeDtypeStruct((1, GLA_DV), F32),
        ],
        compiler_params=_cp(("arbitrary",)),
    )(z, z, z, z, z, z, z, z, z, z, o_f, o_b, dy, dy, dy, dy, conv_w, conv_norm, gla_norm)


def _xa_probs(q_ref, kv_ref, h):
    qh = q_ref[:, XA_HEAD_DIM * h : XA_HEAD_DIM * (h + 1)]
    kh = kv_ref[:, XA_HEAD_DIM * h : XA_HEAD_DIM * (h + 1)]
    vh = kv_ref[:, D_MODEL + XA_HEAD_DIM * h : D_MODEL + XA_HEAD_DIM * (h + 1)]
    sc = _dot_nt(qh, kh) * (XA_HEAD_DIM**-0.5)
    e = jnp.exp(sc - jnp.max(sc, axis=-1, keepdims=True))
    return qh, kh, vh, e / jnp.sum(e, axis=-1, keepdims=True)


def _xattn_fwd(qx, kv, *, name):
    s = qx.shape[0]
    ts = _rows(s)

    def body(q_ref, kv_ref, o_ref):
        outs = []
        for h in range(XA_HEADS):
            _, _, vh, p = _xa_probs(q_ref, kv_ref, h)
            outs.append(_dot(p, vh))
        o_ref[...] = jnp.concatenate(outs, axis=1).astype(o_ref.dtype)

    return pl.pallas_call(
        body,
        name=name,
        grid=(s // ts,),
        in_specs=[pl.BlockSpec((ts, D_MODEL), lambda i: (i, 0)), pl.BlockSpec((N_MEM, 2 * D_MODEL), lambda i: (0, 0))],
        out_specs=pl.BlockSpec((ts, D_MODEL), lambda i: (i, 0)),
        out_shape=jax.ShapeDtypeStruct((s, D_MODEL), _CD),
        compiler_params=_cp(("parallel",)),
    )(qx, kv)


def _xattn_bwd(qx, kv, dox, *, name):
    s = qx.shape[0]
    ts = _rows(s)

    def body(q_ref, kv_ref, do_ref, dq_ref, dkv_ref):
        dqs, dks, dvs = [], [], []
        for h in range(XA_HEADS):
            qh, kh, vh, p = _xa_probs(q_ref, kv_ref, h)
            doh = do_ref[:, XA_HEAD_DIM * h : XA_HEAD_DIM * (h + 1)]
            dp = _dot_nt(doh, vh)
            ds = p * (dp - jnp.sum(dp * p, axis=-1, keepdims=True)) * (XA_HEAD_DIM**-0.5)
            dqs.append(_dot(ds, kh))
            dks.append(_dot_tn(ds, qh))
            dvs.append(_dot_tn(p, doh))
        dq_ref[...] = jnp.concatenate(dqs, axis=1).astype(dq_ref.dtype)
        dkv = jnp.concatenate(dks + dvs, axis=1)

        @pl.when(pl.program_id(0) == 0)
        def _():
            dkv_ref[...] = dkv

        @pl.when(pl.program_id(0) > 0)
        def _():
            dkv_ref[...] += dkv

    tile = pl.BlockSpec((ts, D_MODEL), lambda i: (i, 0))
    kv_spec = pl.BlockSpec((N_MEM, 2 * D_MODEL), lambda i: (0, 0))
    return pl.pallas_call(
        body,
        name=name,
        grid=(s // ts,),
        in_specs=[tile, kv_spec, tile],
        out_specs=[tile, kv_spec],
        out_shape=[jax.ShapeDtypeStruct((s, D_MODEL), _CD), jax.ShapeDtypeStruct((N_MEM, 2 * D_MODEL), F32)],
        compiler_params=_cp(("arbitrary",)),
    )(qx, kv, dox)


def _adamw_math(w, g, m, v):
    m = ADAM_B1 * m + (1.0 - ADAM_B1) * g
    v = ADAM_B2 * v + (1.0 - ADAM_B2) * (g * g)
    m_hat = m / (1.0 - ADAM_B1**ADAM_STEP)
    v_hat = v / (1.0 - ADAM_B2**ADAM_STEP)
    delta = -ADAM_LR * (m_hat / (jnp.sqrt(v_hat) + ADAM_EPS) + ADAM_WD * w)
    return delta, m, v


def _adamw(w, m, v, shard_rows, off, *, transposed, name):
    r, c = w.shape
    tr = min(r, 256)
    g_spec = pl.BlockSpec((c, tr), lambda i: (off // c, i)) if transposed else pl.BlockSpec((tr, c), lambda i: (off // tr + i, 0))

    def body(w_ref, g_ref, m_ref, v_ref, go_ref, d_ref, nm_ref, nv_ref):
        g = g_ref[...].T if transposed else g_ref[...]
        go_ref[...] = g
        d_ref[...], nm_ref[...], nv_ref[...] = _adamw_math(w_ref[...], g, m_ref[...], v_ref[...])

    tile = pl.BlockSpec((tr, c), lambda i: (i, 0))
    return pl.pallas_call(
        body,
        name=name,
        grid=(r // tr,),
        in_specs=[tile, g_spec, tile, tile],
        out_specs=[tile] * 4,
        out_shape=[jax.ShapeDtypeStruct((r, c), F32)] * 4,
        compiler_params=_cp(("parallel",)),
    )(w, shard_rows, m, v)


def _adamw_small(groups, *, name):
    n = len(groups)

    def body(*refs):
        ins, outs = refs[: 4 * n], refs[4 * n :]
        for i in range(n):
            w_ref, g_ref, m_ref, v_ref = ins[4 * i : 4 * i + 4]
            outs[3 * i][...], outs[3 * i + 1][...], outs[3 * i + 2][...] = _adamw_math(w_ref[...], g_ref[...], m_ref[...], v_ref[...])

    flat = [a for grp in groups for a in grp]
    vm = pl.BlockSpec(memory_space=pltpu.VMEM)
    res = pl.pallas_call(
        body,
        name=name,
        in_specs=[vm] * (4 * n),
        out_specs=[vm] * (3 * n),
        out_shape=[jax.ShapeDtypeStruct(grp[0].shape, F32) for grp in groups for _ in range(3)],
        compiler_params=_cp(),
    )(*flat)
    return [tuple(res[3 * i : 3 * i + 3]) for i in range(n)]


def _place():
    return lax.axis_index("x"), lax.axis_index("y"), lax.axis_index("c")


def _rel_chip(x, y, k):
    return (1 - x if k & 2 else x), (1 - y if k & 1 else y)


def _half(c, rh):
    return pl.ds(pl.multiple_of(c * rh, 16), rh)


def _gather_weights(pack):
    r, w = pack.shape
    rh = r // 2

    def body(p_ref, q_ref, send_sems, recv_sems):
        x, y, c = _place()
        j = 2 * x + y
        rows = _half(c, rh)

        def to_chip(k):
            cx, cy = _rel_chip(x, y, k)
            return pltpu.make_async_remote_copy(
                src_ref=p_ref.at[rows], dst_ref=q_ref.at[j, rows], send_sem=send_sems.at[k - 1], recv_sem=recv_sems.at[k - 1],
                device_id=(cx, cy, c), device_id_type=MESH)

        def to_sibling(k):
            cx, cy = _rel_chip(x, y, k)
            slot = q_ref.at[2 * cx + cy, rows]
            return pltpu.make_async_remote_copy(
                src_ref=slot, dst_ref=slot, send_sem=send_sems.at[2 + k], recv_sem=recv_sems.at[2 + k],
                device_id=(x, y, 1 - c), device_id_type=MESH)

        first = [to_chip(k) for k in range(1, N_CHIPS)]
        passed = [to_sibling(k) for k in range(1, N_CHIPS)]
        own = pltpu.make_async_remote_copy(
            src_ref=p_ref, dst_ref=q_ref.at[j], send_sem=send_sems.at[6], recv_sem=recv_sems.at[6],
            device_id=(x, y, 1 - c), device_id_type=MESH)
        for cp in first:
            cp.start()
        own.start()
        for cp, fw in zip(first, passed):
            cp.wait_recv()
            fw.start()
        for fw in passed:
            fw.wait_recv()
        own.wait_recv()
        for cp in first + passed + [own]:
            cp.wait_send()

    return pl.pallas_call(
        body,
        name="gather_weights",
        in_specs=[ANY],
        out_specs=ANY,
        out_shape=jax.ShapeDtypeStruct((N_CHIPS, r, w), pack.dtype),
        scratch_shapes=[pltpu.SemaphoreType.DMA((7,)), pltpu.SemaphoreType.DMA((7,))],
        compiler_params=pltpu.CompilerParams(has_side_effects=True),
    )(pack)


def _swap_halves(g):
    n, r, w = g.shape
    rh = r // 2

    def body(g_ref, o_ref, send_sem, recv_sem):
        x, y, c = _place()
        cp = pltpu.make_async_remote_copy(
            src_ref=g_ref.at[:, _half(1 - c, rh)], dst_ref=o_ref, send_sem=send_sem, recv_sem=recv_sem,
            device_id=(x, y, 1 - c), device_id_type=MESH)
        cp.start()
        cp.wait()

    return pl.pallas_call(
        body,
        name="grads_to_sibling",
        in_specs=[ANY],
        out_specs=ANY,
        out_shape=jax.ShapeDtypeStruct((n, rh, w), g.dtype),
        scratch_shapes=[pltpu.SemaphoreType.DMA, pltpu.SemaphoreType.DMA],
        compiler_params=pltpu.CompilerParams(has_side_effects=True),
    )(g)


def _chip_sums(g, got, where):
    n, r, w = g.shape
    rh = r // 2
    nt = rh // PACK_TILE

    def body(where_ref, g_ref, got_ref, o_ref):
        o_ref[...] = (g_ref[...] + got_ref[...]).astype(o_ref.dtype)

    return pl.pallas_call(
        body,
        name="chip_sums",
        grid_spec=pltpu.PrefetchScalarGridSpec(
            num_scalar_prefetch=1,
            grid=(n, nt),
            in_specs=[pl.BlockSpec((1, PACK_TILE, w), lambda a, i, wh: (a, wh[0] * nt + i, 0)),
                      pl.BlockSpec((1, PACK_TILE, w), lambda a, i, wh: (a, i, 0))],
            out_specs=pl.BlockSpec((1, PACK_TILE, w), lambda a, i, wh: (a, i, 0)),
        ),
        out_shape=jax.ShapeDtypeStruct((n, rh, w), _TD),
        compiler_params=_cp(("parallel", "parallel")),
    )(where, g, got)


def _exchange_chip_sums(h):
    n, rh, w = h.shape

    def body(h_ref, o_ref, send_sems, recv_sems):
        x, y, c = _place()
        j = 2 * x + y
        copies = []
        for k in range(1, N_CHIPS):
            cx, cy = _rel_chip(x, y, k)
            copies.append(pltpu.make_async_remote_copy(
                src_ref=h_ref.at[2 * cx + cy], dst_ref=o_ref.at[k - 1], send_sem=send_sems.at[k - 1], recv_sem=recv_sems.at[k - 1],
                device_id=(cx, cy, c), device_id_type=MESH))
        for cp in copies:
            cp.start()
        for cp in copies:
            cp.wait()

    return pl.pallas_call(
        body,
        name="chip_sums_exchange",
        in_specs=[ANY],
        out_specs=ANY,
        out_shape=jax.ShapeDtypeStruct((N_CHIPS - 1, rh, w), h.dtype),
        scratch_shapes=[pltpu.SemaphoreType.DMA((3,)), pltpu.SemaphoreType.DMA((3,))],
        compiler_params=pltpu.CompilerParams(has_side_effects=True),
    )(h)


def _shard_sum(g, got, others, where):
    n, r, w = g.shape
    rh = r // 2
    nt = rh // PACK_TILE

    def body(where_ref, g_ref, got_ref, oth_ref, o_ref):
        acc = g_ref[0] + got_ref[0]
        for k in range(N_CHIPS - 1):
            acc = acc + oth_ref[k].astype(F32)
        o_ref[...] = acc

    return pl.pallas_call(
        body,
        name="shard_sum",
        grid_spec=pltpu.PrefetchScalarGridSpec(
            num_scalar_prefetch=1,
            grid=(nt,),
            in_specs=[pl.BlockSpec((1, PACK_TILE, w), lambda i, wh: (wh[1], wh[0] * nt + i, 0)),
                      pl.BlockSpec((1, PACK_TILE, w), lambda i, wh: (wh[1], i, 0)),
                      pl.BlockSpec((N_CHIPS - 1, PACK_TILE, w), lambda i, wh: (0, i, 0))],
            out_specs=pl.BlockSpec((PACK_TILE, w), lambda i, wh: (i, 0)),
        ),
        out_shape=jax.ShapeDtypeStruct((rh, w), F32),
        compiler_params=_cp(("parallel",)),
    )(where, g, got, others)


def _join_halves(e):
    rh, w = e.shape

    def body(e_ref, o_ref, send_sem, recv_sem, local_sem):
        x, y, c = _place()
        rows = _half(c, rh)
        mine = pltpu.make_async_copy(e_ref, o_ref.at[rows], local_sem)
        mine.start()
        cp = pltpu.make_async_remote_copy(
            src_ref=e_ref, dst_ref=o_ref.at[rows], send_sem=send_sem, recv_sem=recv_sem, device_id=(x, y, 1 - c), device_id_type=MESH)
        cp.start()
        cp.wait()
        mine.wait()

    return pl.pallas_call(
        body,
        name="shard_to_sibling",
        in_specs=[ANY],
        out_specs=ANY,
        out_shape=jax.ShapeDtypeStruct((2 * rh, w), e.dtype),
        scratch_shapes=[pltpu.SemaphoreType.DMA, pltpu.SemaphoreType.DMA, pltpu.SemaphoreType.DMA],
        compiler_params=pltpu.CompilerParams(has_side_effects=True),
    )(e)


HBM = pl.BlockSpec(memory_space=pltpu.HBM)
SEM = pl.BlockSpec(memory_space=pltpu.SEMAPHORE)
EFFECT = pltpu.SideEffectType.DATAFLOW_SIDE_EFFECTING


def _in_hbm(a):
    return pltpu.with_memory_space_constraint(a, pltpu.HBM)


def _gather_copies(p_ref, land_ref, send_sems, recv_sems):
    rh = p_ref.shape[0] // 2
    x, y, c = _place()
    rows = _half(c, rh)
    copies = []
    for k in range(1, N_CHIPS):
        cx, cy = _rel_chip(x, y, k)
        copies.append(pltpu.make_async_remote_copy(
            src_ref=p_ref.at[rows], dst_ref=land_ref.at[2 * x + y, rows], send_sem=send_sems.at[k - 1], recv_sem=recv_sems.at[k - 1],
            device_id=(cx, cy, c), device_id_type=MESH))
    copies.append(pltpu.make_async_remote_copy(
        src_ref=p_ref, dst_ref=land_ref.at[2 * x + y], send_sem=send_sems.at[N_CHIPS - 1], recv_sem=recv_sems.at[N_CHIPS - 1],
        device_id=(x, y, 1 - c), device_id_type=MESH))
    return copies


def _gather_start(pack, after, *, name):
    r, w = pack.shape

    def body(p_ref, land_ref, after_ref, send_sems, recv_sems, p_thru, land_thru, token):
        for cp in _gather_copies(p_ref, land_ref, send_sems, recv_sems):
            cp.start()
        token[...] = jnp.zeros_like(token)

    return pl.pallas_call(
        body,
        name=name,
        out_shape=(pltpu.SemaphoreType.DMA((N_CHIPS,)), pltpu.SemaphoreType.DMA((N_CHIPS,)), pltpu.HBM((r, w), pack.dtype),
                   pltpu.HBM((N_CHIPS, r, w), pack.dtype), jax.ShapeDtypeStruct((8, 128), F32)),
        in_specs=(HBM, HBM, ANY),
        out_specs=(SEM, SEM, HBM, HBM, pl.BlockSpec(memory_space=pltpu.VMEM)),
        input_output_aliases={0: 2, 1: 3},
        compiler_params=pltpu.CompilerParams(has_side_effects=EFFECT),
    )(_in_hbm(pack), _in_hbm(lax.empty((N_CHIPS, r, w), pack.dtype)), after)


def _gather_wait(send_sems, recv_sems, pack, land, after, *, name):
    def body(p_ref, land_ref, send_sems, recv_sems, after_ref, p_out, land_out):
        for cp in _gather_copies(p_ref, land_ref, send_sems, recv_sems):
            cp.wait_send()
            cp.wait_recv()

    return pl.pallas_call(
        body,
        name=name,
        out_shape=(pltpu.HBM(pack.shape, pack.dtype), pltpu.HBM(land.shape, land.dtype)),
        in_specs=(HBM, HBM, SEM, SEM, ANY),
        out_specs=(HBM, HBM),
        input_output_aliases={0: 0, 1: 1},
        compiler_params=pltpu.CompilerParams(has_side_effects=EFFECT),
    )(pack, land, send_sems, recv_sems, after)


def _gather_spread(land, *, name):
    n, r, w = land.shape
    rh = r // 2

    def body(land_ref, o_ref, send_sems, recv_sems):
        x, y, c = _place()
        rows = _half(c, rh)
        copies = []
        for k in range(1, N_CHIPS):
            cx, cy = _rel_chip(x, y, k)
            copies.append(pltpu.make_async_remote_copy(
                src_ref=land_ref.at[2 * cx + cy, rows], dst_ref=o_ref.at[2 * cx + cy, rows], send_sem=send_sems.at[k - 1],
                recv_sem=recv_sems.at[k - 1], device_id=(x, y, 1 - c), device_id_type=MESH))
        for cp in copies:
            cp.start()
        for cp in copies:
            cp.wait()

    return pl.pallas_call(
        body,
        name=name,
        in_specs=[ANY],
        out_specs=ANY,
        out_shape=jax.ShapeDtypeStruct(land.shape, land.dtype),
        input_output_aliases={0: 0},
        scratch_shapes=[pltpu.SemaphoreType.DMA((N_CHIPS - 1,)), pltpu.SemaphoreType.DMA((N_CHIPS - 1,))],
        compiler_params=pltpu.CompilerParams(has_side_effects=True),
    )(land)


N_PARTS = 2 * (N_CHIPS - 1)


def _scatter_copies(lo_ref, g_ref, land_lo_ref, land_f_ref, send_sems, recv_sems, starting):
    rh = g_ref.shape[1] // 2
    x, y, c = _place()
    copies = []
    for k in range(1, N_CHIPS):
        cx, cy = _rel_chip(x, y, k)
        for i in range(2):
            part = 2 * (k - 1) + (c if starting else i)
            copies.append(pltpu.make_async_remote_copy(
                src_ref=lo_ref.at[2 * cx + cy, pl.ds(i * rh, rh)], dst_ref=land_lo_ref.at[part],
                send_sem=send_sems.at[2 * (k - 1) + i], recv_sem=recv_sems.at[part], device_id=(cx, cy, i), device_id_type=MESH))
    copies.append(pltpu.make_async_remote_copy(
        src_ref=g_ref.at[2 * x + y, _half(1 - c, rh)], dst_ref=land_f_ref, send_sem=send_sems.at[N_PARTS], recv_sem=recv_sems.at[N_PARTS],
        device_id=(x, y, 1 - c), device_id_type=MESH))
    return copies


def _scatter_start(g_lo, g, *, name):
    n, r, w = g.shape
    rh = r // 2

    def body(lo_ref, g_ref, land_lo_ref, land_f_ref, send_sems, recv_sems, lo_thru, g_thru, land_lo_thru, land_f_thru, token):
        for cp in _scatter_copies(lo_ref, g_ref, land_lo_ref, land_f_ref, send_sems, recv_sems, True):
            cp.start()
        token[...] = jnp.zeros_like(token)

    return pl.pallas_call(
        body,
        name=name,
        out_shape=(pltpu.SemaphoreType.DMA((N_PARTS + 1,)), pltpu.SemaphoreType.DMA((N_PARTS + 1,)), pltpu.HBM(g_lo.shape, g_lo.dtype),
                   pltpu.HBM(g.shape, g.dtype), pltpu.HBM((N_PARTS, rh, w), g_lo.dtype), pltpu.HBM((rh, w), g.dtype),
                   jax.ShapeDtypeStruct((8, 128), F32)),
        in_specs=(HBM, HBM, HBM, HBM),
        out_specs=(SEM, SEM, HBM, HBM, HBM, HBM, pl.BlockSpec(memory_space=pltpu.VMEM)),
        input_output_aliases={0: 2, 1: 3, 2: 4, 3: 5},
        compiler_params=pltpu.CompilerParams(has_side_effects=EFFECT),
    )(_in_hbm(g_lo), _in_hbm(g), _in_hbm(lax.empty((N_PARTS, rh, w), g_lo.dtype)), _in_hbm(lax.empty((rh, w), g.dtype)))


def _scatter_wait(send_sems, recv_sems, g_lo, g, land_lo, land_f, after, *, name):
    def body(lo_ref, g_ref, land_lo_ref, land_f_ref, send_sems, recv_sems, after_ref, o0, o1, o2, o3):
        for cp in _scatter_copies(lo_ref, g_ref, land_lo_ref, land_f_ref, send_sems, recv_sems, False):
            cp.wait_send()
            cp.wait_recv()

    arrays = (g_lo, g, land_lo, land_f)
    return pl.pallas_call(
        body,
        name=name,
        out_shape=tuple(pltpu.HBM(a.shape, a.dtype) for a in arrays),
        in_specs=(HBM, HBM, HBM, HBM, SEM, SEM, ANY),
        out_specs=(HBM, HBM, HBM, HBM),
        input_output_aliases={0: 0, 1: 1, 2: 2, 3: 3},
        compiler_params=pltpu.CompilerParams(has_side_effects=EFFECT),
    )(*arrays, send_sems, recv_sems, after)


def _scatter_sum(g, land_lo, land_f, where, *, name):
    n, r, w = g.shape
    rh = r // 2
    tr = _pick(rh, (256, 160, 80))
    nt = rh // tr

    def body(where_ref, g_ref, f_ref, lo_ref, o_ref):
        acc = g_ref[0] + f_ref[...]
        for part in range(N_PARTS):
            acc = acc + lo_ref[part].astype(F32)
        o_ref[...] = acc

    return pl.pallas_call(
        body,
        name=name,
        grid_spec=pltpu.PrefetchScalarGridSpec(
            num_scalar_prefetch=1,
            grid=(nt,),
            in_specs=[pl.BlockSpec((1, tr, w), lambda i, wh: (wh[1], wh[0] * nt + i, 0)),
                      pl.BlockSpec((tr, w), lambda i, wh: (i, 0)),
                      pl.BlockSpec((N_PARTS, tr, w), lambda i, wh: (0, i, 0))],
            out_specs=pl.BlockSpec((tr, w), lambda i, wh: (wh[0] * nt + i, 0)),
        ),
        out_shape=jax.ShapeDtypeStruct((r, w), F32),
        compiler_params=_cp(("parallel",)),
    )(where, g, land_f, land_lo)


def _swap_all(shards, *, name):
    n = len(shards)

    def body(*refs):
        ins, outs = refs[:n], refs[n : 2 * n]
        send_sems, recv_sems = refs[2 * n :]
        x, y, c = _place()
        copies = []
        for i, (e_ref, o_ref) in enumerate(zip(ins, outs)):
            rows = _half(c, e_ref.shape[0] // 2)
            copies.append(pltpu.make_async_remote_copy(src_ref=e_ref.at[rows], dst_ref=o_ref.at[rows], send_sem=send_sems.at[i],
                                                       recv_sem=recv_sems.at[i], device_id=(x, y, 1 - c), device_id_type=MESH))
        for cp in copies:
            cp.start()
        for cp in copies:
            cp.wait()

    return pl.pallas_call(
        body,
        name=name,
        in_specs=[ANY] * n,
        out_specs=[ANY] * n,
        out_shape=[jax.ShapeDtypeStruct(e.shape, e.dtype) for e in shards],
        input_output_aliases={i: i for i in range(n)},
        scratch_shapes=[pltpu.SemaphoreType.DMA((n,)), pltpu.SemaphoreType.DMA((n,))],
        compiler_params=pltpu.CompilerParams(has_side_effects=True),
    )(*shards)


def _sum_small(small, after):
    n_dev = 8

    def body(s_ref, after_ref, o_ref, all_ref, send_sems, recv_sems):
        x, y, c = _place()
        me = 4 * x + 2 * y + c
        all_ref[me] = s_ref[...]
        copies = []
        for k in range(1, n_dev):
            cx, cy = _rel_chip(x, y, k >> 1)
            cc = 1 - c if k & 1 else c
            copies.append(pltpu.make_async_remote_copy(
                src_ref=s_ref, dst_ref=all_ref.at[me], send_sem=send_sems.at[k - 1], recv_sem=recv_sems.at[k - 1],
                device_id=(cx, cy, cc), device_id_type=MESH))
        for cp in copies:
            cp.start()
        for cp in copies:
            cp.wait()
        acc = all_ref[0]
        for a in range(1, n_dev):
            acc = acc + all_ref[a]
        o_ref[...] = acc

    vm = pl.BlockSpec(memory_space=pltpu.VMEM)
    return pl.pallas_call(
        body,
        name="sum_small",
        in_specs=[vm, ANY],
        out_specs=vm,
        out_shape=jax.ShapeDtypeStruct(small.shape, F32),
        scratch_shapes=[pltpu.VMEM((n_dev,) + small.shape, F32), pltpu.SemaphoreType.DMA((n_dev - 1,)), pltpu.SemaphoreType.DMA((n_dev - 1,))],
        compiler_params=pltpu.CompilerParams(has_side_effects=True),
    )(small, after)


MATS = {"w_in": (776, True), "w_out": (256, False), "w_xq": (256, False), "w_xkv": (512, True), "w_xo": (256, False),
        "w_up": (1024, True), "w_down": (1024, False)}
GATHER_FIRST = ("w_in",)
GATHER_REST = ("w_out", "w_xq", "w_xkv", "w_xo", "w_up", "w_down")
GRAD_GROUPS = (("w_up", "w_down"), ("w_out", "w_xq", "w_xkv", "w_xo"), ("w_in",))


def _group_rows(names):
    n = sum(MATS[name][0] for name in names)
    return n + (-n) % 32


def _pack(pieces, rows):
    p = jnp.concatenate(pieces, axis=0) if len(pieces) > 1 else pieces[0]
    return jnp.pad(p, ((0, rows - p.shape[0]), (0, 0))) if rows > p.shape[0] else p


def _unpack(rows, names):
    out, off = {}, 0
    for name in names:
        out[name] = rows[off : off + MATS[name][0]]
        off += MATS[name][0]
    return out


SMALL = (
    ("mix_norm", 1024), ("conv_norm", 512), ("b_af", 256), ("b_ab", 256), ("gla_norm", 128), ("xa_norm", 1024), ("mem_norm", 1024),
    ("mlp_norm", 1024), ("final_norm", 1024), ("conv_w", 1536), ("w_af", 4096), ("w_ab", 4096), ("loss", 128),
)


def kernel(x, mem, mix_norm, w_in, conv_w, conv_norm, w_af, b_af, w_ab, b_ab, gla_norm, w_out, xa_norm, mem_norm, w_xq, w_xkv, w_xo, mlp_norm, w_up, w_down, final_norm, loss_target, m_mix_norm, m_w_in, m_conv_w, m_conv_norm, m_w_af, m_b_af, m_w_ab, m_b_ab, m_gla_norm, m_w_out, m_xa_norm, m_mem_norm, m_w_xq, m_w_xkv, m_w_xo, m_mlp_norm, m_w_up, m_w_down, m_final_norm, v_mix_norm, v_w_in, v_conv_w, v_conv_norm, v_w_af, v_b_af, v_w_ab, v_b_ab, v_gla_norm, v_w_out, v_xa_norm, v_mem_norm, v_w_xq, v_w_xkv, v_w_xo, v_mlp_norm, v_w_up, v_w_down, v_final_norm):
    given = dict(locals())
    xi, yi, ci = _place()
    chip = 2 * xi + yi
    where = jnp.stack([ci, chip]).astype(jnp.int32)

    lo = {name: (given[name][0].T if MATS[name][1] else given[name][0]).astype(_CD) for name in MATS}
    pack_rest = _pack([lo[name] for name in GATHER_REST], _group_rows(GATHER_REST))
    pack_first = _pack([lo[name] for name in GATHER_FIRST], _group_rows(GATHER_FIRST))
    got_first = _gather_weights(pack_first)

    def whole(got, off, rows):
        return got[:, off : off + rows].reshape(N_CHIPS * rows, D_MODEL)

    w_in_t = whole(got_first, 0, MATS["w_in"][0])
    w_za = jnp.concatenate([w_in_t[0:1536], w_in_t[2560:3072]], axis=0)
    w_zb = jnp.concatenate([w_in_t[1536:2560], w_in_t[3072:W_IN_COLS], jnp.zeros((ZB_COLS - 1056, D_MODEL), _CD)], axis=0)

    def placed(shard, full_shape, col):
        return lax.dynamic_update_slice(jnp.zeros(full_shape, F32), shard, (0, col)).reshape(-1, 128)

    sw = jnp.concatenate([
        placed(conv_w[0], (CONV_K, CONV_WIDTH), 128 * chip),
        placed(w_af[0], (GLA_LOWRANK, GLA_K_TOTAL), 64 * chip),
        placed(w_ab[0], (GLA_LOWRANK, GLA_K_TOTAL), 64 * chip),
    ], axis=0)
    sw = jnp.pad(sw, ((0, SMALL_ROWS - sw.shape[0]), (0, 0))) * (ci == 0).astype(F32)
    sw = _sum_small(sw, got_first)
    rest_send, rest_recv, pack_rest, land_rest, rest_token = _gather_start(pack_rest, sw, name="gather_rest_start")
    conv_w_full = sw[0:12].reshape(CONV_K, CONV_WIDTH)
    w_af_full = sw[12:44].reshape(GLA_LOWRANK, GLA_K_TOTAL)
    w_ab_full = sw[44:76].reshape(GLA_LOWRANK, GLA_K_TOTAL)
    waf_p = jnp.pad(w_af_full, ((0, 128 - GLA_LOWRANK), (0, 0))).astype(_CD)
    wab_p = jnp.pad(w_ab_full, ((GLA_LOWRANK, 128 - 2 * GLA_LOWRANK), (0, 0))).astype(_CD)

    xs, mems, tgt = x[0], mem[0], loss_target[0]
    add_res = lambda acc, res: (acc + res,)
    behind = lambda gain, token: gain + token[0, 0]

    h1 = _rms_fwd(xs, behind(mix_norm, rest_token), name="norm_mix")
    z_b = _mm(h1, w_zb, mode="nt", name="proj_in_b", tn=ZB_COLS)
    z_a = _mm(h1, w_za, mode="nt", name="proj_in_a", tm=512, tn=ZA_COLS)
    b_f, b_b = _gate_fwd(z_b, waf_p, wab_p, b_af, b_ab, name="gates")
    o_f, st_f, o_b, st_b = _gla_fwd(z_b, b_f, b_b, name="gla_scan")
    y = _mix_fwd(z_a, o_f, o_b, conv_w_full, conv_norm, gla_norm, name="mix_out")
    pack_rest, land_rest = _gather_wait(rest_send, rest_recv, pack_rest, land_rest, y, name="gather_rest_wait")
    gathered = _gather_spread(land_rest, name="gather_rest_spread")
    wt, off = {}, 0
    for name in GATHER_REST:
        wt[name] = whole(gathered, off, MATS[name][0])
        off += MATS[name][0]
    x1, hx = _mm_rows(y, wt["w_out"], mode="nn", name="proj_out", rows=(xs,), vecs=(xa_norm,), out_rows=(F32, _CD), epilogue=_ep_residual_norm)
    qx = _mm(hx, wt["w_xq"], mode="nn", name="proj_xq", out_dtypes=(_CD,))
    hmem = _rms_fwd(mems, mem_norm, name="norm_mem")
    kv = _mm(hmem, wt["w_xkv"], mode="nt", name="proj_xkv", out_dtypes=(_CD,))
    ox = _xattn_fwd(qx, kv, name="xattn")
    x2, hm = _mm_rows(ox, wt["w_xo"], mode="nn", name="proj_xo", rows=(x1,), vecs=(mlp_norm,), out_rows=(F32, _CD), epilogue=_ep_residual_norm)
    act, relu_u = _mm(hm, wt["w_up"], mode="nt", name="mlp_up", out_dtypes=(_CD, _CD),
                      epilogue=lambda acc: (jnp.square(jnp.maximum(acc, 0.0)), jnp.maximum(acc, 0.0)))
    dx3, dx3_lo, loss_part, g_final_norm = _mm_rows(
        act, wt["w_down"], mode="nn", name="mlp_down", rows=(x2, tgt), vecs=(final_norm.reshape(1, D_MODEL),),
        out_rows=(F32, _CD), out_vecs=(128, D_MODEL), epilogue=_ep_loss)

    grads_t = {}

    def start_group(names, tag):
        rows = _group_rows(names)
        g = jnp.stack([_pack([grads_t[name][a * MATS[name][0] : (a + 1) * MATS[name][0]] for name in names], rows) for a in range(N_CHIPS)])
        return _scatter_start(g.astype(_TD), g, name="grads_" + tag + "_start")

    def finish_group(state, after, tag):
        send_sems, recv_sems, g_lo, g, land_lo, land_f, _ = state
        g_lo, g, land_lo, land_f = _scatter_wait(send_sems, recv_sems, g_lo, g, land_lo, land_f, after, name="grads_" + tag + "_wait")
        return _scatter_sum(g, land_lo, land_f, where, name="grads_" + tag + "_sum")

    def new_packs(names):
        shape = (N_CHIPS, _group_rows(names), D_MODEL)
        return lax.empty(shape, F32), lax.empty(shape, _TD)

    def grad_into(packs, names, which, a, b, name):
        off = sum(MATS[other][0] for other in names[: names.index(which)])
        return _mm_tn_into(a, b, packs, rows=MATS[which][0], off=off, name=name)

    du = _mm(dx3_lo, wt["w_down"], mode="nt", name="mlp_down_dx", out_dtypes=(_CD,), extras=(relu_u,),
             epilogue=lambda acc, rr: (acc * (2.0 * rr.astype(F32)),))
    packs = new_packs(GRAD_GROUPS[0])
    packs = grad_into(packs, GRAD_GROUPS[0], "w_down", act, dx3_lo, "mlp_down_dw")
    packs = grad_into(packs, GRAD_GROUPS[0], "w_up", du, hm, "mlp_up_dw")
    mlp_state = _scatter_start(packs[1], packs[0], name="grads_mlp_start")
    dx2, dx2_lo, g_mlp_norm = _mm_rows(
        du, wt["w_up"], mode="nn", name="mlp_up_dx", rows=(x2, dx3), vecs=(behind(mlp_norm, mlp_state[-1]),),
        out_rows=(F32, _CD), out_vecs=(D_MODEL,), epilogue=_ep_norm_bwd)
    dox = _mm(dx2_lo, wt["w_xo"], mode="nt", name="proj_xo_dx", out_dtypes=(_CD,))
    packs = new_packs(GRAD_GROUPS[1])
    packs = grad_into(packs, GRAD_GROUPS[1], "w_xo", ox, dx2_lo, "proj_xo_dw")
    dqx, dkv = _xattn_bwd(qx, kv, dox, name="xattn_bwd")
    packs = grad_into(packs, GRAD_GROUPS[1], "w_xq", hx, dqx, "proj_xq_dw")
    dx1, dx1_lo, g_xa_norm = _mm_rows(
        dqx, wt["w_xq"], mode="nt", name="proj_xq_dx", rows=(x1, dx2), vecs=(xa_norm,),
        out_rows=(F32, _CD), out_vecs=(D_MODEL,), epilogue=_ep_norm_bwd)
    dkv_lo = dkv.astype(_CD)
    packs = grad_into(packs, GRAD_GROUPS[1], "w_xkv", dkv_lo, hmem, "proj_xkv_dw")
    dhmem = _mm(dkv_lo, wt["w_xkv"], mode="nn", name="proj_xkv_dx")
    (g_mem_norm,) = _rms_bwd(mems, mem_norm, dhmem, name="norm_mem_bwd", want_dx=False, want_lo=False)
    dy = _mm(dx1_lo, wt["w_out"], mode="nt", name="proj_out_dx")
    packs = grad_into(packs, GRAD_GROUPS[1], "w_out", y, dx1_lo, "proj_out_dw")
    attn_state = _scatter_start(packs[1], packs[0], name="grads_attn_start")
    dz_a, do, g_conv_w, g_conv_norm, g_gla_norm = _mix_bwd(z_a, o_f, o_b, dy, conv_w_full, behind(conv_norm, attn_state[-1]), gla_norm, name="mix_out_bwd")
    dqkv_f, db_f, dqkv_b, db_b = _gla_bwd(z_b, b_f, b_b, do, st_f, st_b, name="gla_scan_bwd")
    dz_b, g_waf_p, g_wab_p, g_b_af, g_b_ab = _gate_bwd(z_b, waf_p, wab_p, b_af, b_ab, db_f, db_b, dqkv_f, dqkv_b, name="gates_bwd")
    g_za = _mm_tn(dz_a, h1, name="proj_in_a_dw")
    g_zb = _mm_tn(dz_b, h1, name="proj_in_b_dw")
    grads_t["w_in"] = jnp.concatenate([g_za[0:1536], g_zb[0:1024], g_za[1536:2048], g_zb[1024:1056]], axis=0)
    in_state = start_group(GRAD_GROUPS[2], "in")
    dh1_a = _mm(dz_a, w_za, mode="nn", name="proj_in_a_dx", tm=512, tk=ZA_COLS)
    grad_x, g_mix_norm = _mm_rows(
        dz_b, w_zb, mode="nn", name="proj_in_b_dx", rows=(xs, dx1, dh1_a), vecs=(behind(mix_norm, in_state[-1]),),
        out_rows=(F32,), out_vecs=(D_MODEL,), epilogue=_ep_norm_bwd)

    half_mlp = finish_group(mlp_state, grad_x, "mlp")
    half_attn = finish_group(attn_state, half_mlp, "attn")
    half_in = finish_group(in_state, half_attn, "in")
    shard_rows = {}
    for names, rows in zip(GRAD_GROUPS, _swap_all([half_mlp, half_attn, half_in], name="shards_to_sibling")):
        off = 0
        for name in names:
            shard_rows[name] = (rows, off)
            off += MATS[name][0]

    small_vals = dict(mix_norm=g_mix_norm, conv_norm=g_conv_norm, b_af=g_b_af, b_ab=g_b_ab, gla_norm=g_gla_norm, xa_norm=g_xa_norm,
                      mem_norm=g_mem_norm, mlp_norm=g_mlp_norm, final_norm=g_final_norm, conv_w=g_conv_w,
                      w_af=g_waf_p[0:GLA_LOWRANK], w_ab=g_wab_p[GLA_LOWRANK : 2 * GLA_LOWRANK], loss=loss_part)
    small = jnp.concatenate([small_vals[name].reshape(-1, 128) for name, _ in SMALL], axis=0)
    small = _sum_small(jnp.pad(small, ((0, SMALL_ROWS - small.shape[0]), (0, 0))), loss_part)
    g_small, off = {}, 0
    for name, n in SMALL:
        g_small[name] = small[off : off + n // 128]
        off += n // 128
    loss = g_small["loss"][0, 0]
    g_small["conv_w"] = lax.dynamic_slice(g_small["conv_w"].reshape(CONV_K, CONV_WIDTH), (0, 128 * chip), (CONV_K, 128))
    g_small["w_af"] = lax.dynamic_slice(g_small["w_af"].reshape(GLA_LOWRANK, GLA_K_TOTAL), (0, 64 * chip), (GLA_LOWRANK, 64))
    g_small["w_ab"] = lax.dynamic_slice(g_small["w_ab"].reshape(GLA_LOWRANK, GLA_K_TOTAL), (0, 64 * chip), (GLA_LOWRANK, 64))

    names = ["mix_norm", "w_in", "conv_w", "conv_norm", "w_af", "b_af", "w_ab", "b_ab", "gla_norm", "w_out", "xa_norm", "mem_norm",
             "w_xq", "w_xkv", "w_xo", "mlp_norm", "w_up", "w_down", "final_norm"]
    big_names = list(MATS)
    as2d = lambda a: a.reshape(1, -1) if a.ndim == 1 else a.reshape(a.shape[-2:])
    grads, deltas, new_m, new_v = {}, {}, {}, {}
    for name in big_names:
        rows, off = shard_rows[name]
        grads[name], deltas[name], new_m[name], new_v[name] = _adamw(
            as2d(given[name]), as2d(given["m_" + name]), as2d(given["v_" + name]), rows, off, transposed=MATS[name][1],
            name="adamw_" + name)
    small_names = [name for name in names if name not in big_names]
    groups = []
    for name in small_names:
        grads[name] = g_small[name].reshape(as2d(given[name]).shape)
        groups.append((as2d(given[name]), grads[name], as2d(given["m_" + name]), as2d(given["v_" + name])))
    for name, res in zip(small_names, _adamw_small(groups, name="adamw_small")):
        deltas[name], new_m[name], new_v[name] = res

    like = lambda name, a: a.reshape(given[name].shape)
    return (loss, grad_x[None], *[like(n, grads[n]) for n in names], *[like(n, deltas[n]) for n in names],
            *[like(n, new_m[n]) for n in names], *[like(n, new_v[n]) for n in names])
```

```python
import functools

import jax
import jax.numpy as jnp
from jax import lax
from jax.experimental import pallas as pl
from jax.experimental.pallas import tpu as pltpu

F32 = jnp.float32
BF16 = jnp.bfloat16
_CD = jnp.bfloat16
_TD = jnp.bfloat16

D_MODEL = 1024
N_MEM = 256
CONV_WIDTH = 512
CONV_GROUP = 64
CONV_K = 3
GLA_HEADS = 4
GLA_DK = 64
GLA_DV = 128
GLA_K_TOTAL = 256
GLA_V_TOTAL = 512
GLA_LOWRANK = 16
GLA_GATE_SCALE = 1.0 / 16.0
GLA_CHUNK = 64
XA_HEADS = 4
XA_HEAD_DIM = 256
D_FF = 4096
EPS = 1e-6
W_IN_COLS = 3104
ZA_COLS = 2048
ZB_COLS = 1152
LR_COL = 1024

ADAM_LR = 0.001
ADAM_B1 = 0.9
ADAM_B2 = 0.999
ADAM_EPS = 1e-08
ADAM_WD = 0.01
ADAM_STEP = 10

N_CHIPS = 4
PACK_W = 1024
PACK_ROWS = 4160
PACK_TILE = 160
SMALL_ROWS = 128

_TS = 512
_VMEM = 44 * 1024 * 1024
MESH = pl.DeviceIdType.MESH
ANY = pl.BlockSpec(memory_space=pl.ANY)


def _cp(sem=None, **kw):
    return pltpu.CompilerParams(dimension_semantics=sem, vmem_limit_bytes=_VMEM, **kw)


def _dot(a, b):
    return jnp.dot(a.astype(_CD), b.astype(_CD), preferred_element_type=F32)


def _dot_nt(a, b):
    return lax.dot_general(a.astype(_CD), b.astype(_CD), (((1,), (1,)), ((), ())), preferred_element_type=F32)


def _dot_tn(a, b):
    return lax.dot_general(a.astype(_CD), b.astype(_CD), (((0,), (0,)), ((), ())), preferred_element_type=F32)


def _dot_split(x, ones):
    hi = x.astype(BF16)
    r = x - hi.astype(F32)
    mid = r.astype(BF16)
    lo = (r - mid.astype(F32)).astype(BF16)
    d = lambda p: jnp.dot(p, ones, preferred_element_type=F32)
    return d(hi) + d(mid) + d(lo)


def _pick(n, cands=(1024, 640, 512, 256, 128)):
    for t in cands:
        if n % t == 0:
            return t
    return n


def _rows(s):
    return min(_TS, s)


def _sigmoid(v):
    e = jnp.exp(-jnp.abs(v))
    return jnp.where(v >= 0, 1.0 / (1.0 + e), e / (1.0 + e))


def _mm(a, b, *, mode, name, out_dtypes=(F32,), extras=(), epilogue=None, tm=None, tn=None, tk=None):
    m, k = a.shape
    n = b.shape[1] if mode == "nn" else b.shape[0]
    tm = min(m, tm or 1024)
    tn = tn or _pick(n)
    tk = tk or _pick(k)
    nk = k // tk
    n_ex, n_out = len(extras), len(out_dtypes)

    def body(*refs):
        a_ref, b_ref = refs[:2]
        ex = refs[2 : 2 + n_ex]
        outs = refs[2 + n_ex : 2 + n_ex + n_out]
        part = _dot(a_ref[...], b_ref[...]) if mode == "nn" else _dot_nt(a_ref[...], b_ref[...])

        def finish(acc):
            res = epilogue(acc, *[e[...] for e in ex]) if epilogue else (acc,)
            for o, r in zip(outs, res):
                o[...] = r.astype(o.dtype)

        if nk == 1:
            finish(part)
        else:
            acc_ref = refs[-1]
            kk = pl.program_id(2)

            @pl.when(kk == 0)
            def _():
                acc_ref[...] = part

            @pl.when(kk > 0)
            def _():
                acc_ref[...] += part

            @pl.when(kk == nk - 1)
            def _():
                finish(acc_ref[...])

    b_spec = pl.BlockSpec((tk, tn), lambda i, j, kk: (kk, j)) if mode == "nn" else pl.BlockSpec((tn, tk), lambda i, j, kk: (j, kk))
    tile = pl.BlockSpec((tm, tn), lambda i, j, kk: (i, j))
    out = pl.pallas_call(
        body,
        name=name,
        grid=(m // tm, n // tn, nk),
        in_specs=[pl.BlockSpec((tm, tk), lambda i, j, kk: (i, kk)), b_spec] + [tile] * n_ex,
        out_specs=[tile] * n_out,
        out_shape=[jax.ShapeDtypeStruct((m, n), dt) for dt in out_dtypes],
        scratch_shapes=[pltpu.VMEM((tm, tn), F32)] if nk > 1 else [],
        compiler_params=_cp(("parallel", "parallel", "arbitrary")),
    )(a, b, *extras)
    return out[0] if n_out == 1 else out


def _mm_tn(a, b, *, name):
    s, m = a.shape
    n = b.shape[1]
    cap = max(128, (1 << 20) // n)
    tm = _pick(m, tuple(t for t in (512, 640, 384, 256, 128) if t <= max(cap, 128)))
    ts = min(s, 1 << (((1 << 22) // n).bit_length() - 1))
    ns = s // ts

    def body(a_ref, b_ref, o_ref):
        part = _dot_tn(a_ref[...], b_ref[...])
        if ns == 1:
            o_ref[...] = part
        else:
            ss = pl.program_id(1)

            @pl.when(ss == 0)
            def _():
                o_ref[...] = part

            @pl.when(ss > 0)
            def _():
                o_ref[...] += part

    return pl.pallas_call(
        body,
        name=name,
        grid=(m // tm, ns),
        in_specs=[pl.BlockSpec((ts, tm), lambda i, ss: (ss, i)), pl.BlockSpec((ts, n), lambda i, ss: (ss, 0))],
        out_specs=pl.BlockSpec((tm, n), lambda i, ss: (i, 0)),
        out_shape=jax.ShapeDtypeStruct((m, n), F32),
        compiler_params=_cp(("parallel", "arbitrary")),
    )(a, b)


def _mm_tn_into(a, b, packs, *, rows, off, name):
    s, m = a.shape
    n = b.shape[1]
    tm = 512
    tr = min(tm, rows)
    per, chips = rows // tr, tm // tr
    ts = min(s, 1 << (((1 << 22) // n).bit_length() - 1))
    ns = s // ts

    def body(a_ref, b_ref, f_in, lo_in, f_ref, lo_ref):
        part = _dot_tn(a_ref[...], b_ref[...])
        pieces = [part[c * tr : (c + 1) * tr] for c in range(chips)]
        if ns == 1:
            for c, p in enumerate(pieces):
                f_ref[c] = p
                lo_ref[c] = p.astype(lo_ref.dtype)
        else:
            ss = pl.program_id(1)

            @pl.when(ss == 0)
            def _():
                for c, p in enumerate(pieces):
                    f_ref[c] = p

            @pl.when(ss > 0)
            def _():
                for c, p in enumerate(pieces):
                    f_ref[c] += p

            @pl.when(ss == ns - 1)
            def _():
                lo_ref[...] = f_ref[...].astype(lo_ref.dtype)

    spec = pl.BlockSpec((chips, tr, n), lambda i, ss: (i // per, off // tr + i % per, 0))
    return pl.pallas_call(
        body,
        name=name,
        grid=(m // tm, ns),
        in_specs=[pl.BlockSpec((ts, tm), lambda i, ss: (ss, i)), pl.BlockSpec((ts, n), lambda i, ss: (ss, 0)), ANY, ANY],
        out_specs=[spec, spec],
        out_shape=[jax.ShapeDtypeStruct(p.shape, p.dtype) for p in packs],
        input_output_aliases={2: 0, 3: 1},
        compiler_params=_cp(("parallel", "arbitrary")),
    )(a, b, *packs)


def _mm_rows(a, b, *, mode, name, rows=(), vecs=(), out_rows=(), out_vecs=(), epilogue, tm=512):
    m, k = a.shape
    n = b.shape[1] if mode == "nn" else b.shape[0]
    tm = min(m, tm)
    parts = 2 if tm % 256 == 0 else 1
    n_r, n_v, n_or, n_ov = len(rows), len(vecs), len(out_rows), len(out_vecs)

    def body(*refs):
        a_ref, b_ref = refs[:2]
        r_refs = refs[2 : 2 + n_r]
        v_refs = refs[2 + n_r : 2 + n_r + n_v]
        or_refs = refs[2 + n_r + n_v : 2 + n_r + n_v + n_or]
        ov_refs = refs[2 + n_r + n_v + n_or :]
        res_vecs = None
        for p in range(parts):
            rs = slice(p * tm // parts, (p + 1) * tm // parts)
            acc = _dot(a_ref[rs, :], b_ref[...]) if mode == "nn" else _dot_nt(a_ref[rs, :], b_ref[...])
            res_rows, part_vecs = epilogue(acc, [r[rs, :] for r in r_refs], [v[...] for v in v_refs])
            for o, r in zip(or_refs, res_rows):
                o[rs, :] = r.astype(o.dtype)
            res_vecs = part_vecs if res_vecs is None else [s + t for s, t in zip(res_vecs, part_vecs)]
        if n_ov:
            first = pl.program_id(0) == 0

            @pl.when(first)
            def _():
                for o, r in zip(ov_refs, res_vecs):
                    o[...] = r

            @pl.when(jnp.logical_not(first))
            def _():
                for o, r in zip(ov_refs, res_vecs):
                    o[...] += r

    tile = pl.BlockSpec((tm, n), lambda i: (i, 0))
    whole = lambda arr: pl.BlockSpec(arr.shape, lambda i: (0, 0))
    vec = lambda w: pl.BlockSpec((1, w), lambda i: (0, 0))
    out = pl.pallas_call(
        body,
        name=name,
        grid=(m // tm,),
        in_specs=[pl.BlockSpec((tm, k), lambda i: (i, 0)), whole(b)] + [tile] * n_r + [vec(v.shape[1]) for v in vecs],
        out_specs=[tile] * n_or + [vec(w) for w in out_vecs],
        out_shape=[jax.ShapeDtypeStruct((m, n), dt) for dt in out_rows] + [jax.ShapeDtypeStruct((1, w), F32) for w in out_vecs],
        compiler_params=_cp(("arbitrary",) if n_ov else ("parallel",)),
    )(a, b, *rows, *vecs)
    return out


def _ep_residual_norm(acc, rows, vecs):
    x = acc + rows[0]
    r = lax.rsqrt(jnp.mean(x * x, axis=-1, keepdims=True) + EPS)
    return [x, x * r * vecs[0]], []


def _ep_norm_bwd(acc, rows, vecs):
    dy = acc
    for extra in rows[2:]:
        dy = dy + extra
    x, dres = rows[0], rows[1]
    r = lax.rsqrt(jnp.mean(x * x, axis=-1, keepdims=True) + EPS)
    xh = x * r
    dxh = dy * vecs[0]
    dx = r * (dxh - xh * jnp.mean(dxh * xh, axis=-1, keepdims=True)) + dres
    return [dx, dx], [jnp.sum(dy * xh, axis=0, keepdims=True)]


def _ep_loss(acc, rows, vecs):
    x = acc + rows[0]
    d = x.shape[-1]
    r = lax.rsqrt(jnp.mean(x * x, axis=-1, keepdims=True) + EPS)
    xh = x * r
    err = xh * vecs[0] - rows[1]
    loss = jnp.zeros((1, 128), F32) + 0.5 * jnp.sum(jnp.mean(err * err, axis=-1, keepdims=True))
    dy = err * (1.0 / d)
    dxh = dy * vecs[0]
    dx = r * (dxh - xh * jnp.mean(dxh * xh, axis=-1, keepdims=True))
    return [dx, dx], [loss, jnp.sum(dy * xh, axis=0, keepdims=True)]


def _rms_fwd(x, g, *, name):
    s, d = x.shape
    ts = _rows(s)

    def body(x_ref, g_ref, o_ref):
        xf = x_ref[...]
        r = lax.rsqrt(jnp.mean(xf * xf, axis=-1, keepdims=True) + EPS)
        o_ref[...] = (xf * r * g_ref[...]).astype(o_ref.dtype)

    return pl.pallas_call(
        body,
        name=name,
        grid=(s // ts,),
        in_specs=[pl.BlockSpec((ts, d), lambda i: (i, 0)), pl.BlockSpec((1, d), lambda i: (0, 0))],
        out_specs=pl.BlockSpec((ts, d), lambda i: (i, 0)),
        out_shape=jax.ShapeDtypeStruct((s, d), _CD),
        compiler_params=_cp(("parallel",)),
    )(x, g)


def _rms_bwd(x, g, dy, dres=None, *, name, want_dx=True, want_lo=True):
    s, d = x.shape
    ts = _rows(s)
    has_res = dres is not None

    def body(*refs):
        x_ref, g_ref, dy_ref = refs[:3]
        pos = 3
        dres_ref = refs[pos] if has_res else None
        pos += has_res
        dx_ref = refs[pos] if want_dx else None
        pos += want_dx
        lo_ref = refs[pos] if want_lo else None
        pos += want_lo
        dg_ref = refs[pos]
        xf = x_ref[...]
        r = lax.rsqrt(jnp.mean(xf * xf, axis=-1, keepdims=True) + EPS)
        xh = xf * r
        dyf = dy_ref[...]
        part = jnp.sum(dyf * xh, axis=0, keepdims=True)

        @pl.when(pl.program_id(0) == 0)
        def _():
            dg_ref[...] = part

        @pl.when(pl.program_id(0) > 0)
        def _():
            dg_ref[...] += part

        if want_dx or want_lo:
            dxh = dyf * g_ref[...]
            dx = r * (dxh - xh * jnp.mean(dxh * xh, axis=-1, keepdims=True))
            if has_res:
                dx = dx + dres_ref[...]
            if want_dx:
                dx_ref[...] = dx
            if want_lo:
                lo_ref[...] = dx.astype(lo_ref.dtype)

    tile = pl.BlockSpec((ts, d), lambda i: (i, 0))
    vec = pl.BlockSpec((1, d), lambda i: (0, 0))
    out_specs, out_shape = [], []
    if want_dx:
        out_specs.append(tile)
        out_shape.append(jax.ShapeDtypeStruct((s, d), F32))
    if want_lo:
        out_specs.append(tile)
        out_shape.append(jax.ShapeDtypeStruct((s, d), _CD))
    out_specs.append(vec)
    out_shape.append(jax.ShapeDtypeStruct((1, d), F32))
    return pl.pallas_call(
        body,
        name=name,
        grid=(s // ts,),
        in_specs=[tile, vec, tile] + ([tile] if has_res else []),
        out_specs=out_specs,
        out_shape=out_shape,
        compiler_params=_cp(("arbitrary",)),
    )(x, g, dy, *([dres] if has_res else []))


def _final_loss(x3, g, tgt, *, name):
    s, d = x3.shape
    ts = _rows(s)

    def body(x_ref, g_ref, t_ref, dx_ref, lo_ref, loss_ref, dg_ref):
        xf = x_ref[...]
        r = lax.rsqrt(jnp.mean(xf * xf, axis=-1, keepdims=True) + EPS)
        xh = xf * r
        gg = g_ref[...]
        err = xh * gg - t_ref[...]
        lpart = jnp.zeros((1, 128), F32) + 0.5 * jnp.sum(jnp.mean(err * err, axis=-1, keepdims=True))
        dy = err * (1.0 / d)
        gpart = jnp.sum(dy * xh, axis=0, keepdims=True)

        @pl.when(pl.program_id(0) == 0)
        def _():
            loss_ref[...] = lpart
            dg_ref[...] = gpart

        @pl.when(pl.program_id(0) > 0)
        def _():
            loss_ref[...] += lpart
            dg_ref[...] += gpart

        dxh = dy * gg
        dx = r * (dxh - xh * jnp.mean(dxh * xh, axis=-1, keepdims=True))
        dx_ref[...] = dx
        lo_ref[...] = dx.astype(lo_ref.dtype)

    tile = pl.BlockSpec((ts, d), lambda i: (i, 0))
    vec = pl.BlockSpec((1, d), lambda i: (0, 0))
    return pl.pallas_call(
        body,
        name=name,
        grid=(s // ts,),
        in_specs=[tile, vec, tile],
        out_specs=[tile, tile, pl.BlockSpec((1, 128), lambda i: (0, 0)), vec],
        out_shape=[
            jax.ShapeDtypeStruct((s, d), F32),
            jax.ShapeDtypeStruct((s, d), _CD),
            jax.ShapeDtypeStruct((1, 128), F32),
            jax.ShapeDtypeStruct((1, d), F32),
        ],
        compiler_params=_cp(("arbitrary",)),
    )(x3, g, tgt)


def _chunk_scan(v, row_in_chunk, suffix):
    t = v.shape[0]
    step = 1
    while step < GLA_CHUNK:
        if suffix:
            v = v + jnp.where(row_in_chunk < GLA_CHUNK - step, pltpu.roll(v, t - step, 0), 0.0)
        else:
            v = v + jnp.where(row_in_chunk >= step, pltpu.roll(v, step, 0), 0.0)
        step *= 2
    return v


def _gate_pre(lr, w_ref, b_ref):
    return _dot(lr, w_ref[...]) + b_ref[...]


def _gate_fwd(z, waf, wab, baf, bab, *, name):
    s = z.shape[0]
    ts = _rows(s)

    def body(lr_ref, waf_ref, wab_ref, baf_ref, bab_ref, bf_ref, bb_ref):
        lr = lr_ref[...]
        ric = lax.broadcasted_iota(jnp.int32, (ts, GLA_K_TOTAL), 0) & (GLA_CHUNK - 1)
        for w_ref, b_ref, o_ref, suffix in ((waf_ref, baf_ref, bf_ref, False), (wab_ref, bab_ref, bb_ref, True)):
            pre = _gate_pre(lr, w_ref, b_ref)
            la = (jnp.minimum(pre, 0.0) - jnp.log(1.0 + jnp.exp(-jnp.abs(pre)))) * GLA_GATE_SCALE
            o_ref[...] = _chunk_scan(la, ric, suffix)

    wspec = pl.BlockSpec((128, GLA_K_TOTAL), lambda i: (0, 0))
    bspec = pl.BlockSpec((1, GLA_K_TOTAL), lambda i: (0, 0))
    tile = pl.BlockSpec((ts, GLA_K_TOTAL), lambda i: (i, 0))
    return pl.pallas_call(
        body,
        name=name,
        grid=(s // ts,),
        in_specs=[pl.BlockSpec((ts, 128), lambda i: (i, LR_COL // 128)), wspec, wspec, bspec, bspec],
        out_specs=[tile, tile],
        out_shape=[jax.ShapeDtypeStruct((s, GLA_K_TOTAL), F32)] * 2,
        compiler_params=_cp(("parallel",)),
    )(z, waf, wab, baf, bab)


def _gate_bwd(z, waf, wab, baf, bab, dbf, dbb, dqkv_f, dqkv_b, *, name):
    s = z.shape[0]
    ts = _rows(s)

    def body(lr_ref, waf_ref, wab_ref, baf_ref, bab_ref, dbf_ref, dbb_ref, gf_ref, gb_ref, dzb_ref, dwf_ref, dwb_ref, dbaf_ref, dbab_ref):
        lr = lr_ref[...]
        ric = lax.broadcasted_iota(jnp.int32, (ts, GLA_K_TOTAL), 0) & (GLA_CHUNK - 1)
        first = pl.program_id(0) == 0
        dlr = None
        for w_ref, b_ref, db_ref, dw_ref, dbias_ref, suffix in (
            (waf_ref, baf_ref, dbf_ref, dwf_ref, dbaf_ref, True),
            (wab_ref, bab_ref, dbb_ref, dwb_ref, dbab_ref, False),
        ):
            pre = _gate_pre(lr, w_ref, b_ref)
            dla = _chunk_scan(db_ref[...], ric, suffix)
            dpre = dla * GLA_GATE_SCALE * _sigmoid(-pre)
            part = _dot_nt(dpre, w_ref[...])
            dlr = part if dlr is None else dlr + part
            dw = _dot_tn(lr, dpre)
            dbias = jnp.sum(dpre, axis=0, keepdims=True)

            @pl.when(first)
            def _():
                dw_ref[...] = dw
                dbias_ref[...] = dbias

            @pl.when(jnp.logical_not(first))
            def _():
                dw_ref[...] += dw
                dbias_ref[...] += dbias

        dzb_ref[...] = jnp.concatenate([gf_ref[...] + gb_ref[...], dlr], axis=1).astype(dzb_ref.dtype)

    wspec = pl.BlockSpec((128, GLA_K_TOTAL), lambda i: (0, 0))
    bspec = pl.BlockSpec((1, GLA_K_TOTAL), lambda i: (0, 0))
    tile = pl.BlockSpec((ts, GLA_K_TOTAL), lambda i: (i, 0))
    wide = pl.BlockSpec((ts, 2 * GLA_K_TOTAL + GLA_V_TOTAL), lambda i: (i, 0))
    return pl.pallas_call(
        body,
        name=name,
        grid=(s // ts,),
        in_specs=[pl.BlockSpec((ts, 128), lambda i: (i, LR_COL // 128)), wspec, wspec, bspec, bspec, tile, tile, wide, wide],
        out_specs=[pl.BlockSpec((ts, ZB_COLS), lambda i: (i, 0)), wspec, wspec, bspec, bspec],
        out_shape=[
            jax.ShapeDtypeStruct((s, ZB_COLS), _CD),
            jax.ShapeDtypeStruct((128, GLA_K_TOTAL), F32),
            jax.ShapeDtypeStruct((128, GLA_K_TOTAL), F32),
            jax.ShapeDtypeStruct((1, GLA_K_TOTAL), F32),
            jax.ShapeDtypeStruct((1, GLA_K_TOTAL), F32),
        ],
        compiler_params=_cp(("arbitrary",)),
    )(z, waf, wab, baf, bab, dbf, dbb, dqkv_f, dqkv_b)


def _gla_masks(rev):
    lane_head = lax.broadcasted_iota(jnp.int32, (1, GLA_K_TOTAL), 1) >> 6
    head_masks = [lane_head == h for h in range(GLA_HEADS)]
    t = lax.broadcasted_iota(jnp.int32, (GLA_HEADS * GLA_CHUNK, GLA_CHUNK), 0) & (GLA_CHUNK - 1)
    u = lax.broadcasted_iota(jnp.int32, (GLA_HEADS * GLA_CHUNK, GLA_CHUNK), 1)
    tri = (u > t) if rev else (u <= t)
    row = lax.broadcasted_iota(jnp.int32, (GLA_CHUNK, GLA_K_TOTAL), 0)
    total_row = row == (0 if rev else GLA_CHUNK - 1)
    return head_masks, tri, total_row


def _spread(a, head_masks):
    return jnp.concatenate([jnp.where(m, a, 0.0) for m in head_masks], axis=0)


def _stack(a):
    return jnp.concatenate([a[:, GLA_DV * h : GLA_DV * (h + 1)] for h in range(GLA_HEADS)], axis=0)


def _unstack(a):
    return jnp.concatenate([a[GLA_CHUNK * h : GLA_CHUNK * (h + 1)] for h in range(GLA_HEADS)], axis=1)


def _collect(a, head_masks):
    out = None
    for h, m in enumerate(head_masks):
        part = jnp.where(m, a[GLA_CHUNK * h : GLA_CHUNK * (h + 1)], 0.0)
        out = part if out is None else out + part
    return out


def _gla_chunk_terms(q_ref, k_ref, v_ref, b_ref, rows, head_masks, tri, total_row):
    q = q_ref[rows, :] * (GLA_DK**-0.5)
    k = k_ref[rows, :]
    v = v_ref[rows, :]
    b = b_ref[rows, :]
    eb = jnp.exp(b)
    enb = jnp.exp(-b)
    g = jnp.sum(jnp.where(total_row, b, 0.0), axis=0, keepdims=True)
    egb = jnp.exp(g - b)
    qt = q * eb
    kt = k * enb
    kh = k * egb
    q_heads = _spread(qt, head_masks)
    attn = jnp.where(tri, _dot_nt(q_heads, kt), 0.0)
    return v, eb, enb, egb, jnp.exp(g), qt, kt, kh, q_heads, attn


def _gla_specs(s, tb, rev_blocks):
    nb = s // tb
    rb = (lambda i: nb - 1 - i) if rev_blocks else (lambda i: i)
    q_spec = pl.BlockSpec((tb, GLA_K_TOTAL), lambda i: (rb(i), 0))
    k_spec = pl.BlockSpec((tb, GLA_K_TOTAL), lambda i: (rb(i), 1))
    v_spec = pl.BlockSpec((tb, GLA_V_TOTAL), lambda i: (rb(i), 1))
    b_spec = pl.BlockSpec((tb, GLA_K_TOTAL), lambda i: (rb(i), 0))
    o_spec = pl.BlockSpec((tb, GLA_V_TOTAL), lambda i: (rb(i), 0))
    st_spec = pl.BlockSpec((tb // GLA_CHUNK, GLA_DV, GLA_K_TOTAL), lambda i: (rb(i), 0, 0))
    return nb, q_spec, k_spec, v_spec, b_spec, o_spec, st_spec


def _gla_fwd_chunk(cidx, q_ref, k_ref, v_ref, b_ref, o_ref, sv_ref, st_ref, masks):
    head_masks, tri, total_row = masks
    rows = pl.ds(pl.multiple_of(cidx * GLA_CHUNK, GLA_CHUNK), GLA_CHUNK)
    v, _, _, _, eg, _, _, kh, q_heads, attn = _gla_chunk_terms(q_ref, k_ref, v_ref, b_ref, rows, head_masks, tri, total_row)
    o = jnp.concatenate(
        [_dot(attn[GLA_CHUNK * h : GLA_CHUNK * (h + 1)], v[:, GLA_DV * h : GLA_DV * (h + 1)]) for h in range(GLA_HEADS)], axis=1
    )
    st = st_ref[...]
    o_ref[rows, :] = o + _unstack(_dot_nt(q_heads, st))
    sv_ref[cidx] = st
    st_ref[...] = st * eg + _dot_tn(_stack(v), _spread(kh, head_masks))


def _gla_fwd(z, b_f, b_b, *, name):
    s = z.shape[0]
    tb = _rows(s)
    cpb = tb // GLA_CHUNK
    nb, qf, kf, vf, bf, of, sf = _gla_specs(s, tb, False)
    _, qr, kr, vr, br, orr, sr = _gla_specs(s, tb, True)

    def body(qf_ref, kf_ref, vf_ref, bf_ref, qr_ref, kr_ref, vr_ref, br_ref, of_ref, svf_ref, or_ref, svr_ref, stf_ref, str_ref):
        masks_f, masks_r = _gla_masks(False), _gla_masks(True)

        @pl.when(pl.program_id(0) == 0)
        def _():
            stf_ref[...] = jnp.zeros_like(stf_ref)
            str_ref[...] = jnp.zeros_like(str_ref)

        def chunk(ci, carry):
            _gla_fwd_chunk(ci, qf_ref, kf_ref, vf_ref, bf_ref, of_ref, svf_ref, stf_ref, masks_f)
            _gla_fwd_chunk(cpb - 1 - ci, qr_ref, kr_ref, vr_ref, br_ref, or_ref, svr_ref, str_ref, masks_r)
            return carry

        lax.fori_loop(0, cpb, chunk, 0)

    o_shape = jax.ShapeDtypeStruct((s, GLA_V_TOTAL), F32)
    st_shape = jax.ShapeDtypeStruct((s // GLA_CHUNK, GLA_DV, GLA_K_TOTAL), F32)
    return pl.pallas_call(
        body,
        name=name,
        grid=(nb,),
        in_specs=[qf, kf, vf, bf, qr, kr, vr, br],
        out_specs=[of, sf, orr, sr],
        out_shape=[o_shape, st_shape, o_shape, st_shape],
        scratch_shapes=[pltpu.VMEM((GLA_DV, GLA_K_TOTAL), F32)] * 2,
        compiler_params=_cp(("arbitrary",)),
    )(z, z, z, b_f, z, z, z, b_b)


def _gla_bwd_chunk(cidx, q_ref, k_ref, v_ref, b_ref, do_ref, sv_ref, dqkv_ref, db_ref, dst_ref, masks):
    head_masks, tri, total_row = masks
    rows = pl.ds(pl.multiple_of(cidx * GLA_CHUNK, GLA_CHUNK), GLA_CHUNK)
    v, eb, enb, egb, eg, qt, kt, kh, q_heads, attn = _gla_chunk_terms(q_ref, k_ref, v_ref, b_ref, rows, head_masks, tri, total_row)
    do_c = do_ref[rows, :]
    st = sv_ref[cidx]
    dst = dst_ref[...]
    do_s, v_s = _stack(do_c), _stack(v)
    hs = lambda a, h: a[GLA_CHUNK * h : GLA_CHUNK * (h + 1)]
    vs = lambda a, h: a[:, GLA_DV * h : GLA_DV * (h + 1)]
    dattn = jnp.concatenate([_dot_nt(vs(do_c, h), vs(v, h)) for h in range(GLA_HEADS)], axis=0)
    dattn = jnp.where(tri, dattn, 0.0)
    dv = jnp.concatenate([_dot_tn(hs(attn, h), vs(do_c, h)) for h in range(GLA_HEADS)], axis=1)
    dv = dv + _unstack(_dot_nt(_spread(kh, head_masks), dst))
    dqt = _collect(_dot(do_s, st), head_masks)
    dkt = jnp.zeros_like(dqt)
    for h in range(GLA_HEADS):
        dqt = dqt + jnp.where(head_masks[h], _dot(hs(dattn, h), kt), 0.0)
        dkt = dkt + jnp.where(head_masks[h], _dot_tn(hs(dattn, h), qt), 0.0)
    dkh = _collect(_dot(v_s, dst), head_masks)
    dg = jnp.sum(dkh * kh, axis=0, keepdims=True) + jnp.sum(dst * st, axis=0, keepdims=True) * eg
    db = dqt * qt - dkt * kt - dkh * kh + jnp.where(total_row, dg, 0.0)
    dq = dqt * eb * (GLA_DK**-0.5)
    dk = dkt * enb + dkh * egb
    dqkv_ref[rows, :] = jnp.concatenate([dq, dk, dv], axis=1)
    db_ref[rows, :] = db
    dst_ref[...] = dst * eg + _dot_tn(do_s, q_heads)


def _gla_bwd(z, b_f, b_b, do, st_f, st_b, *, name):
    s = z.shape[0]
    tb = _rows(s)
    cpb = tb // GLA_CHUNK
    wide = 2 * GLA_K_TOTAL + GLA_V_TOTAL
    nb, qf, kf, vf, bf, of, sf = _gla_specs(s, tb, True)
    _, qr, kr, vr, br, orr, sr = _gla_specs(s, tb, False)
    gf = pl.BlockSpec((tb, wide), lambda i: (nb - 1 - i, 0))
    gr = pl.BlockSpec((tb, wide), lambda i: (i, 0))

    def body(qf_ref, kf_ref, vf_ref, bf_ref, dof_ref, svf_ref, qr_ref, kr_ref, vr_ref, br_ref, dor_ref, svr_ref,
             gf_ref, dbf_ref, gr_ref, dbr_ref, dstf_ref, dstr_ref):
        masks_f, masks_r = _gla_masks(False), _gla_masks(True)

        @pl.when(pl.program_id(0) == 0)
        def _():
            dstf_ref[...] = jnp.zeros_like(dstf_ref)
            dstr_ref[...] = jnp.zeros_like(dstr_ref)

        def chunk(ci, carry):
            _gla_bwd_chunk(cpb - 1 - ci, qf_ref, kf_ref, vf_ref, bf_ref, dof_ref, svf_ref, gf_ref, dbf_ref, dstf_ref, masks_f)
            _gla_bwd_chunk(ci, qr_ref, kr_ref, vr_ref, br_ref, dor_ref, svr_ref, gr_ref, dbr_ref, dstr_ref, masks_r)
            return carry

        lax.fori_loop(0, cpb, chunk, 0)

    g_shape = jax.ShapeDtypeStruct((s, wide), F32)
    db_shape = jax.ShapeDtypeStruct((s, GLA_K_TOTAL), F32)
    return pl.pallas_call(
        body,
        name=name,
        grid=(nb,),
        in_specs=[qf, kf, vf, bf, of, sf, qr, kr, vr, br, orr, sr],
        out_specs=[gf, bf, gr, br],
        out_shape=[g_shape, db_shape, g_shape, db_shape],
        scratch_shapes=[pltpu.VMEM((GLA_DV, GLA_K_TOTAL), F32)] * 2,
        compiler_params=_cp(("arbitrary",)),
    )(z, z, z, b_f, do, st_f, z, z, z, b_b, do, st_b)


HALO = 8


def _halo_specs(s, ts, width, col):
    last = s // HALO - 1
    per = ts // HALO
    prev = pl.BlockSpec((HALO, width), lambda i: (jnp.maximum(i * per - 1, 0), col))
    nxt = pl.BlockSpec((HALO, width), lambda i: (jnp.minimum((i + 1) * per, last), col))
    return prev, nxt


def _group_ones():
    r = lax.broadcasted_iota(jnp.int32, (CONV_WIDTH, CONV_WIDTH), 0) >> 6
    c = lax.broadcasted_iota(jnp.int32, (CONV_WIDTH, CONV_WIDTH), 1) >> 6
    return (r == c).astype(BF16)


def _conv_terms(cc_ext, cu_ext, cw, valid):
    n = cc_ext.shape[0]
    hc = jnp.where(valid, cc_ext * cu_ext, 0.0)
    hc_prev = pltpu.roll(hc, 1, 0)
    hc_next = pltpu.roll(hc, n - 1, 0)
    conv = cw[0:1] * hc_prev + cw[1:2] * hc + cw[2:3] * hc_next
    return hc, hc_prev, hc_next, conv


def _ext(prev_ref, cur_ref, next_ref):
    return jnp.concatenate([prev_ref[...], cur_ref[...], next_ref[...]], axis=0)


def _valid_rows(ts, s):
    row = lax.broadcasted_iota(jnp.int32, (ts + 2 * HALO, 1), 0) + (pl.program_id(0) * ts - HALO)
    return (row >= 0) & (row < s)


def _head_norm(o, gn):
    out = []
    for h in range(GLA_HEADS):
        oh = o[:, GLA_DV * h : GLA_DV * (h + 1)]
        r = lax.rsqrt(jnp.mean(oh * oh, axis=-1, keepdims=True) + EPS)
        out.append((oh * r, r))
    return out


def _mix_fwd(z, o_f, o_b, conv_w, conv_norm, gla_norm, *, name):
    s = z.shape[0]
    ts = _rows(s)
    cprev, cnext = _halo_specs(s, ts, CONV_WIDTH, 1)
    uprev, unext = _halo_specs(s, ts, CONV_WIDTH, 2)

    def body(cb_ref, cc_ref, cu_ref, ccp_ref, ccn_ref, cup_ref, cun_ref, g_ref, of_ref, ob_ref, cw_ref, cn_ref, gn_ref, y_ref):
        valid = _valid_rows(ts, s)
        _, _, _, conv = _conv_terms(_ext(ccp_ref, cc_ref, ccn_ref), _ext(cup_ref, cu_ref, cun_ref), cw_ref[...], valid)
        yc = cb_ref[...] * conv[HALO : HALO + ts]
        ms = _dot_split(yc * yc, _group_ones()) * (1.0 / CONV_GROUP)
        y_conv = yc * lax.rsqrt(ms + EPS) * cn_ref[...]
        gate = g_ref[...]
        silu = gate * _sigmoid(gate)
        gn = gn_ref[...]
        y_gla = jnp.concatenate([oh * gn for oh, _ in _head_norm(of_ref[...] + ob_ref[...], gn)], axis=1) * silu
        y_ref[...] = jnp.concatenate([y_conv, y_gla], axis=1).astype(y_ref.dtype)

    col = lambda c, w=CONV_WIDTH: pl.BlockSpec((ts, w), lambda i: (i, c))
    return pl.pallas_call(
        body,
        name=name,
        grid=(s // ts,),
        in_specs=[col(0), col(1), col(2), cprev, cnext, uprev, unext, col(3), col(0), col(0),
                  pl.BlockSpec((CONV_K, CONV_WIDTH), lambda i: (0, 0)), pl.BlockSpec((1, CONV_WIDTH), lambda i: (0, 0)),
                  pl.BlockSpec((1, GLA_DV), lambda i: (0, 0))],
        out_specs=pl.BlockSpec((ts, D_MODEL), lambda i: (i, 0)),
        out_shape=jax.ShapeDtypeStruct((s, D_MODEL), _CD),
        compiler_params=_cp(("parallel",)),
    )(z, z, z, z, z, z, z, z, o_f, o_b, conv_w, conv_norm, gla_norm)


def _mix_bwd(z, o_f, o_b, dy, conv_w, conv_norm, gla_norm, *, name):
    s = z.shape[0]
    ts = _rows(s)
    halos = [_halo_specs(s, ts, CONV_WIDTH, c) for c in (0, 1, 2)]
    dprev, dnext = _halo_specs(s, ts, CONV_WIDTH, 0)

    def body(cb_ref, cc_ref, cu_ref, cbp_ref, cbn_ref, ccp_ref, ccn_ref, cup_ref, cun_ref, g_ref, of_ref, ob_ref,
             dyc_ref, dyg_ref, dyp_ref, dyn_ref, cw_ref, cn_ref, gn_ref, dza_ref, do_ref, dcw_ref, dcn_ref, dgn_ref):
        n = ts + 2 * HALO
        valid = _valid_rows(ts, s)
        cw = cw_ref[...]
        cn = cn_ref[...]
        ones = _group_ones()
        cb = _ext(cbp_ref, cb_ref, cbn_ref)
        cc = _ext(ccp_ref, cc_ref, ccn_ref)
        cu = _ext(cup_ref, cu_ref, cun_ref)
        dy = _ext(dyp_ref, dyc_ref, dyn_ref)
        hc, hc_prev, hc_next, conv = _conv_terms(cc, cu, cw, valid)
        yc = cb * conv
        r = lax.rsqrt(_dot_split(yc * yc, ones) * (1.0 / CONV_GROUP) + EPS)
        yh = yc * r
        dyh = dy * cn
        dyc = r * (dyh - yh * (_dot_split(dyh * yh, ones) * (1.0 / CONV_GROUP)))
        dconv = jnp.where(valid, dyc * cb, 0.0)
        dhc = cw[0:1] * pltpu.roll(dconv, n - 1, 0) + cw[1:2] * dconv + cw[2:3] * pltpu.roll(dconv, 1, 0)
        mid = lambda a: a[HALO : HALO + ts]
        dza_ref[:, 0 : 3 * CONV_WIDTH] = jnp.concatenate([mid(dyc * conv), mid(dhc * cu), mid(dhc * cc)], axis=1).astype(dza_ref.dtype)
        dconv_m = mid(dconv)
        colsum = lambda a: jnp.sum(a, axis=0, keepdims=True)
        dcw = jnp.concatenate([colsum(dconv_m * mid(hc_prev)), colsum(dconv_m * mid(hc)), colsum(dconv_m * mid(hc_next))], axis=0)
        dcn = colsum(mid(dy * yh))

        gate = g_ref[...]
        sg = _sigmoid(gate)
        silu = gate * sg
        gn = gn_ref[...]
        dyg = dyg_ref[...]
        don = dyg * silu
        heads = _head_norm(of_ref[...] + ob_ref[...], gn)
        on = jnp.concatenate([oh * gn for oh, _ in heads], axis=1)
        dza_ref[:, 3 * CONV_WIDTH : ZA_COLS] = (dyg * on * (sg * (1.0 + gate * (1.0 - sg)))).astype(dza_ref.dtype)
        dgn = jnp.zeros((1, GLA_DV), F32)
        dos = []
        for h, (oh, rh) in enumerate(heads):
            donh = don[:, GLA_DV * h : GLA_DV * (h + 1)]
            dgn = dgn + colsum(donh * oh)
            doh = donh * gn
            dos.append(rh * (doh - oh * jnp.mean(doh * oh, axis=-1, keepdims=True)))
        do_ref[...] = jnp.concatenate(dos, axis=1)

        first = pl.program_id(0) == 0

        @pl.when(first)
        def _():
            dcw_ref[...] = dcw
            dcn_ref[...] = dcn
            dgn_ref[...] = dgn

        @pl.when(jnp.logical_not(first))
        def _():
            dcw_ref[...] += dcw
            dcn_ref[...] += dcn
            dgn_ref[...] += dgn

    col = lambda c, w=CONV_WIDTH: pl.BlockSpec((ts, w), lambda i: (i, c))
    cw_spec = pl.BlockSpec((CONV_K, CONV_WIDTH), lambda i: (0, 0))
    cn_spec = pl.BlockSpec((1, CONV_WIDTH), lambda i: (0, 0))
    gn_spec = pl.BlockSpec((1, GLA_DV), lambda i: (0, 0))
    return pl.pallas_call(
        body,
        name=name,
        grid=(s // ts,),
        in_specs=[col(0), col(1), col(2), halos[0][0], halos[0][1], halos[1][0], halos[1][1], halos[2][0], halos[2][1],
                  col(3), col(0), col(0), col(0), col(1), dprev, dnext, cw_spec, cn_spec, gn_spec],
        out_specs=[pl.BlockSpec((ts, ZA_COLS), lambda i: (i, 0)), col(0), cw_spec, cn_spec, gn_spec],
        out_shape=[
            jax.ShapeDtypeStruct((s, ZA_COLS), _CD),
            jax.ShapeDtypeStruct((s, GLA_V_TOTAL), F32),
            jax.ShapeDtypeStruct((CONV_K, CONV_WIDTH), F32),
            jax.ShapeDtypeStruct((1, CONV_WIDTH), F32),
            jax.ShapeDtypeStruct((1, GLA_DV), F32),
        ],
        compiler_params=_cp(("arbitrary",)),
    )(z, z, z, z, z, z, z, z, z, z, o_f, o_b, dy, dy, dy, dy, conv_w, conv_norm, gla_norm)


def _xa_probs(q_ref, kv_ref, h):
    qh = q_ref[:, XA_HEAD_DIM * h : XA_HEAD_DIM * (h + 1)]
    kh = kv_ref[:, XA_HEAD_DIM * h : XA_HEAD_DIM * (h + 1)]
    vh = kv_ref[:, D_MODEL + XA_HEAD_DIM * h : D_MODEL + XA_HEAD_DIM * (h + 1)]
    sc = _dot_nt(qh, kh) * (XA_HEAD_DIM**-0.5)
    e = jnp.exp(sc - jnp.max(sc, axis=-1, keepdims=True))
    return qh, kh, vh, e / jnp.sum(e, axis=-1, keepdims=True)


def _xattn_fwd(qx, kv, *, name):
    s = qx.shape[0]
    ts = _rows(s)

    def body(q_ref, kv_ref, o_ref):
        outs = []
        for h in range(XA_HEADS):
            _, _, vh, p = _xa_probs(q_ref, kv_ref, h)
            outs.append(_dot(p, vh))
        o_ref[...] = jnp.concatenate(outs, axis=1).astype(o_ref.dtype)

    return pl.pallas_call(
        body,
        name=name,
        grid=(s // ts,),
        in_specs=[pl.BlockSpec((ts, D_MODEL), lambda i: (i, 0)), pl.BlockSpec((N_MEM, 2 * D_MODEL), lambda i: (0, 0))],
        out_specs=pl.BlockSpec((ts, D_MODEL), lambda i: (i, 0)),
        out_shape=jax.ShapeDtypeStruct((s, D_MODEL), _CD),
        compiler_params=_cp(("parallel",)),
    )(qx, kv)


def _xattn_bwd(qx, kv, dox, *, name):
    s = qx.shape[0]
    ts = _rows(s)

    def body(q_ref, kv_ref, do_ref, dq_ref, dkv_ref):
        dqs, dks, dvs = [], [], []
        for h in range(XA_HEADS):
            qh, kh, vh, p = _xa_probs(q_ref, kv_ref, h)
            doh = do_ref[:, XA_HEAD_DIM * h : XA_HEAD_DIM * (h + 1)]
            dp = _dot_nt(doh, vh)
            ds = p * (dp - jnp.sum(dp * p, axis=-1, keepdims=True)) * (XA_HEAD_DIM**-0.5)
            dqs.append(_dot(ds, kh))
            dks.append(_dot_tn(ds, qh))
            dvs.append(_dot_tn(p, doh))
        dq_ref[...] = jnp.concatenate(dqs, axis=1).astype(dq_ref.dtype)
        dkv = jnp.concatenate(dks + dvs, axis=1)

        @pl.when(pl.program_id(0) == 0)
        def _():
            dkv_ref[...] = dkv

        @pl.when(pl.program_id(0) > 0)
        def _():
            dkv_ref[...] += dkv

    tile = pl.BlockSpec((ts, D_MODEL), lambda i: (i, 0))
    kv_spec = pl.BlockSpec((N_MEM, 2 * D_MODEL), lambda i: (0, 0))
    return pl.pallas_call(
        body,
        name=name,
        grid=(s // ts,),
        in_specs=[tile, kv_spec, tile],
        out_specs=[tile, kv_spec],
        out_shape=[jax.ShapeDtypeStruct((s, D_MODEL), _CD), jax.ShapeDtypeStruct((N_MEM, 2 * D_MODEL), F32)],
        compiler_params=_cp(("arbitrary",)),
    )(qx, kv, dox)


def _adamw_math(w, g, m, v):
    m = ADAM_B1 * m + (1.0 - ADAM_B1) * g
    v = ADAM_B2 * v + (1.0 - ADAM_B2) * (g * g)
    m_hat = m / (1.0 - ADAM_B1**ADAM_STEP)
    v_hat = v / (1.0 - ADAM_B2**ADAM_STEP)
    delta = -ADAM_LR * (m_hat / (jnp.sqrt(v_hat) + ADAM_EPS) + ADAM_WD * w)
    return delta, m, v


def _adamw(w, m, v, shard_rows, off, *, transposed, name):
    r, c = w.shape
    by_columns = r % 256 != 0
    tr = 256
    if by_columns:
        assert not transposed and off == 0
        g_spec = tile = pl.BlockSpec((r, tr), lambda i: (0, i))
    else:
        g_spec = pl.BlockSpec((c, tr), lambda i: (off // c, i)) if transposed else pl.BlockSpec((tr, c), lambda i: (off // tr + i, 0))
        tile = pl.BlockSpec((tr, c), lambda i: (i, 0))

    def body(w_ref, g_ref, m_ref, v_ref, go_ref, d_ref, nm_ref, nv_ref):
        g = g_ref[...].T if transposed else g_ref[...]
        go_ref[...] = g
        d_ref[...], nm_ref[...], nv_ref[...] = _adamw_math(w_ref[...], g, m_ref[...], v_ref[...])

    return pl.pallas_call(
        body,
        name=name,
        grid=((c if by_columns else r) // tr,),
        in_specs=[tile, g_spec, tile, tile],
        out_specs=[tile] * 4,
        out_shape=[jax.ShapeDtypeStruct((r, c), F32)] * 4,
        compiler_params=_cp(("parallel",)),
    )(w, shard_rows, m, v)


def _adamw_small(groups, *, name):
    n = len(groups)

    def body(*refs):
        ins, outs = refs[: 4 * n], refs[4 * n :]
        for i in range(n):
            w_ref, g_ref, m_ref, v_ref = ins[4 * i : 4 * i + 4]
            outs[3 * i][...], outs[3 * i + 1][...], outs[3 * i + 2][...] = _adamw_math(w_ref[...], g_ref[...], m_ref[...], v_ref[...])

    flat = [a for grp in groups for a in grp]
    vm = pl.BlockSpec(memory_space=pltpu.VMEM)
    res = pl.pallas_call(
        body,
        name=name,
        in_specs=[vm] * (4 * n),
        out_specs=[vm] * (3 * n),
        out_shape=[jax.ShapeDtypeStruct(grp[0].shape, F32) for grp in groups for _ in range(3)],
        compiler_params=_cp(),
    )(*flat)
    return [tuple(res[3 * i : 3 * i + 3]) for i in range(n)]


def _place():
    return lax.axis_index("x"), lax.axis_index("y"), lax.axis_index("c")


def _rel_chip(x, y, k):
    return (1 - x if k & 2 else x), (1 - y if k & 1 else y)


def _half(c, rh):
    return pl.ds(pl.multiple_of(c * rh, 16), rh)


def _gather_weights(pack):
    r, w = pack.shape
    rh = r // 2

    def body(p_ref, q_ref, send_sems, recv_sems):
        x, y, c = _place()
        j = 2 * x + y
        rows = _half(c, rh)

        def to_chip(k):
            cx, cy = _rel_chip(x, y, k)
            return pltpu.make_async_remote_copy(
                src_ref=p_ref.at[rows], dst_ref=q_ref.at[j, rows], send_sem=send_sems.at[k - 1], recv_sem=recv_sems.at[k - 1],
                device_id=(cx, cy, c), device_id_type=MESH)

        def to_sibling(k):
            cx, cy = _rel_chip(x, y, k)
            slot = q_ref.at[2 * cx + cy, rows]
            return pltpu.make_async_remote_copy(
                src_ref=slot, dst_ref=slot, send_sem=send_sems.at[2 + k], recv_sem=recv_sems.at[2 + k],
                device_id=(x, y, 1 - c), device_id_type=MESH)

        first = [to_chip(k) for k in range(1, N_CHIPS)]
        passed = [to_sibling(k) for k in range(1, N_CHIPS)]
        own = pltpu.make_async_remote_copy(
            src_ref=p_ref, dst_ref=q_ref.at[j], send_sem=send_sems.at[6], recv_sem=recv_sems.at[6],
            device_id=(x, y, 1 - c), device_id_type=MESH)
        for cp in first:
            cp.start()
        own.start()
        for cp, fw in zip(first, passed):
            cp.wait_recv()
            fw.start()
        for fw in passed:
            fw.wait_recv()
        own.wait_recv()
        for cp in first + passed + [own]:
            cp.wait_send()

    return pl.pallas_call(
        body,
        name="gather_weights",
        in_specs=[ANY],
        out_specs=ANY,
        out_shape=jax.ShapeDtypeStruct((N_CHIPS, r, w), pack.dtype),
        scratch_shapes=[pltpu.SemaphoreType.DMA((7,)), pltpu.SemaphoreType.DMA((7,))],
        compiler_params=pltpu.CompilerParams(has_side_effects=True),
    )(pack)


def _swap_halves(g):
    n, r, w = g.shape
    rh = r // 2

    def body(g_ref, o_ref, send_sem, recv_sem):
        x, y, c = _place()
        cp = pltpu.make_async_remote_copy(
            src_ref=g_ref.at[:, _half(1 - c, rh)], dst_ref=o_ref, send_sem=send_sem, recv_sem=recv_sem,
            device_id=(x, y, 1 - c), device_id_type=MESH)
        cp.start()
        cp.wait()

    return pl.pallas_call(
        body,
        name="grads_to_sibling",
        in_specs=[ANY],
        out_specs=ANY,
        out_shape=jax.ShapeDtypeStruct((n, rh, w), g.dtype),
        scratch_shapes=[pltpu.SemaphoreType.DMA, pltpu.SemaphoreType.DMA],
        compiler_params=pltpu.CompilerParams(has_side_effects=True),
    )(g)


def _chip_sums(g, got, where):
    n, r, w = g.shape
    rh = r // 2
    nt = rh // PACK_TILE

    def body(where_ref, g_ref, got_ref, o_ref):
        o_ref[...] = (g_ref[...] + got_ref[...]).astype(o_ref.dtype)

    return pl.pallas_call(
        body,
        name="chip_sums",
        grid_spec=pltpu.PrefetchScalarGridSpec(
            num_scalar_prefetch=1,
            grid=(n, nt),
            in_specs=[pl.BlockSpec((1, PACK_TILE, w), lambda a, i, wh: (a, wh[0] * nt + i, 0)),
                      pl.BlockSpec((1, PACK_TILE, w), lambda a, i, wh: (a, i, 0))],
            out_specs=pl.BlockSpec((1, PACK_TILE, w), lambda a, i, wh: (a, i, 0)),
        ),
        out_shape=jax.ShapeDtypeStruct((n, rh, w), _TD),
        compiler_params=_cp(("parallel", "parallel")),
    )(where, g, got)


def _exchange_chip_sums(h):
    n, rh, w = h.shape

    def body(h_ref, o_ref, send_sems, recv_sems):
        x, y, c = _place()
        j = 2 * x + y
        copies = []
        for k in range(1, N_CHIPS):
            cx, cy = _rel_chip(x, y, k)
            copies.append(pltpu.make_async_remote_copy(
                src_ref=h_ref.at[2 * cx + cy], dst_ref=o_ref.at[k - 1], send_sem=send_sems.at[k - 1], recv_sem=recv_sems.at[k - 1],
                device_id=(cx, cy, c), device_id_type=MESH))
        for cp in copies:
            cp.start()
        for cp in copies:
            cp.wait()

    return pl.pallas_call(
        body,
        name="chip_sums_exchange",
        in_specs=[ANY],
        out_specs=ANY,
        out_shape=jax.ShapeDtypeStruct((N_CHIPS - 1, rh, w), h.dtype),
        scratch_shapes=[pltpu.SemaphoreType.DMA((3,)), pltpu.SemaphoreType.DMA((3,))],
        compiler_params=pltpu.CompilerParams(has_side_effects=True),
    )(h)


def _shard_sum(g, got, others, where):
    n, r, w = g.shape
    rh = r // 2
    nt = rh // PACK_TILE

    def body(where_ref, g_ref, got_ref, oth_ref, o_ref):
        acc = g_ref[0] + got_ref[0]
        for k in range(N_CHIPS - 1):
            acc = acc + oth_ref[k].astype(F32)
        o_ref[...] = acc

    return pl.pallas_call(
        body,
        name="shard_sum",
        grid_spec=pltpu.PrefetchScalarGridSpec(
            num_scalar_prefetch=1,
            grid=(nt,),
            in_specs=[pl.BlockSpec((1, PACK_TILE, w), lambda i, wh: (wh[1], wh[0] * nt + i, 0)),
                      pl.BlockSpec((1, PACK_TILE, w), lambda i, wh: (wh[1], i, 0)),
                      pl.BlockSpec((N_CHIPS - 1, PACK_TILE, w), lambda i, wh: (0, i, 0))],
            out_specs=pl.BlockSpec((PACK_TILE, w), lambda i, wh: (i, 0)),
        ),
        out_shape=jax.ShapeDtypeStruct((rh, w), F32),
        compiler_params=_cp(("parallel",)),
    )(where, g, got, others)


def _join_halves(e):
    rh, w = e.shape

    def body(e_ref, o_ref, send_sem, recv_sem, local_sem):
        x, y, c = _place()
        rows = _half(c, rh)
        mine = pltpu.make_async_copy(e_ref, o_ref.at[rows], local_sem)
        mine.start()
        cp = pltpu.make_async_remote_copy(
            src_ref=e_ref, dst_ref=o_ref.at[rows], send_sem=send_sem, recv_sem=recv_sem, device_id=(x, y, 1 - c), device_id_type=MESH)
        cp.start()
        cp.wait()
        mine.wait()

    return pl.pallas_call(
        body,
        name="shard_to_sibling",
        in_specs=[ANY],
        out_specs=ANY,
        out_shape=jax.ShapeDtypeStruct((2 * rh, w), e.dtype),
        scratch_shapes=[pltpu.SemaphoreType.DMA, pltpu.SemaphoreType.DMA, pltpu.SemaphoreType.DMA],
        compiler_params=pltpu.CompilerParams(has_side_effects=True),
    )(e)


HBM = pl.BlockSpec(memory_space=pltpu.HBM)
SEM = pl.BlockSpec(memory_space=pltpu.SEMAPHORE)
EFFECT = pltpu.SideEffectType.DATAFLOW_SIDE_EFFECTING


def _in_hbm(a):
    return pltpu.with_memory_space_constraint(a, pltpu.HBM)


def _gather_copies(p_ref, land_ref, send_sems, recv_sems):
    rh = p_ref.shape[0] // 2
    x, y, c = _place()
    rows = _half(c, rh)
    copies = []
    for k in range(1, N_CHIPS):
        cx, cy = _rel_chip(x, y, k)
        copies.append(pltpu.make_async_remote_copy(
            src_ref=p_ref.at[rows], dst_ref=land_ref.at[2 * x + y, rows], send_sem=send_sems.at[k - 1], recv_sem=recv_sems.at[k - 1],
            device_id=(cx, cy, c), device_id_type=MESH))
    copies.append(pltpu.make_async_remote_copy(
        src_ref=p_ref, dst_ref=land_ref.at[2 * x + y], send_sem=send_sems.at[N_CHIPS - 1], recv_sem=recv_sems.at[N_CHIPS - 1],
        device_id=(x, y, 1 - c), device_id_type=MESH))
    return copies


def _gather_start(pack, after, *, name):
    r, w = pack.shape

    def body(p_ref, land_ref, after_ref, send_sems, recv_sems, p_thru, land_thru, token):
        for cp in _gather_copies(p_ref, land_ref, send_sems, recv_sems):
            cp.start()
        token[...] = jnp.zeros_like(token)

    return pl.pallas_call(
        body,
        name=name,
        out_shape=(pltpu.SemaphoreType.DMA((N_CHIPS,)), pltpu.SemaphoreType.DMA((N_CHIPS,)), pltpu.HBM((r, w), pack.dtype),
                   pltpu.HBM((N_CHIPS, r, w), pack.dtype), jax.ShapeDtypeStruct((8, 128), F32)),
        in_specs=(HBM, HBM, ANY),
        out_specs=(SEM, SEM, HBM, HBM, pl.BlockSpec(memory_space=pltpu.VMEM)),
        input_output_aliases={0: 2, 1: 3},
        compiler_params=pltpu.CompilerParams(has_side_effects=EFFECT),
    )(_in_hbm(pack), _in_hbm(lax.empty((N_CHIPS, r, w), pack.dtype)), after)


def _gather_wait(send_sems, recv_sems, pack, land, after, *, name):
    def body(p_ref, land_ref, send_sems, recv_sems, after_ref, p_out, land_out):
        for cp in _gather_copies(p_ref, land_ref, send_sems, recv_sems):
            cp.wait_send()
            cp.wait_recv()

    return pl.pallas_call(
        body,
        name=name,
        out_shape=(pltpu.HBM(pack.shape, pack.dtype), pltpu.HBM(land.shape, land.dtype)),
        in_specs=(HBM, HBM, SEM, SEM, ANY),
        out_specs=(HBM, HBM),
        input_output_aliases={0: 0, 1: 1},
        compiler_params=pltpu.CompilerParams(has_side_effects=EFFECT),
    )(pack, land, send_sems, recv_sems, after)


def _gather_spread(land, *, name):
    n, r, w = land.shape
    rh = r // 2

    def body(land_ref, o_ref, send_sems, recv_sems):
        x, y, c = _place()
        rows = _half(c, rh)
        copies = []
        for k in range(1, N_CHIPS):
            cx, cy = _rel_chip(x, y, k)
            copies.append(pltpu.make_async_remote_copy(
                src_ref=land_ref.at[2 * cx + cy, rows], dst_ref=o_ref.at[2 * cx + cy, rows], send_sem=send_sems.at[k - 1],
                recv_sem=recv_sems.at[k - 1], device_id=(x, y, 1 - c), device_id_type=MESH))
        for cp in copies:
            cp.start()
        for cp in copies:
            cp.wait()

    return pl.pallas_call(
        body,
        name=name,
        in_specs=[ANY],
        out_specs=ANY,
        out_shape=jax.ShapeDtypeStruct(land.shape, land.dtype),
        input_output_aliases={0: 0},
        scratch_shapes=[pltpu.SemaphoreType.DMA((N_CHIPS - 1,)), pltpu.SemaphoreType.DMA((N_CHIPS - 1,))],
        compiler_params=pltpu.CompilerParams(has_side_effects=True),
    )(land)


N_PARTS = 2 * (N_CHIPS - 1)


def _scatter_copies(lo_ref, g_ref, land_lo_ref, land_f_ref, send_sems, recv_sems, starting):
    rh = g_ref.shape[1] // 2
    x, y, c = _place()
    copies = []
    for k in range(1, N_CHIPS):
        cx, cy = _rel_chip(x, y, k)
        for i in range(2):
            part = 2 * (k - 1) + (c if starting else i)
            copies.append(pltpu.make_async_remote_copy(
                src_ref=lo_ref.at[2 * cx + cy, pl.ds(i * rh, rh)], dst_ref=land_lo_ref.at[part],
                send_sem=send_sems.at[2 * (k - 1) + i], recv_sem=recv_sems.at[part], device_id=(cx, cy, i), device_id_type=MESH))
    copies.append(pltpu.make_async_remote_copy(
        src_ref=g_ref.at[2 * x + y, _half(1 - c, rh)], dst_ref=land_f_ref, send_sem=send_sems.at[N_PARTS], recv_sem=recv_sems.at[N_PARTS],
        device_id=(x, y, 1 - c), device_id_type=MESH))
    return copies


def _scatter_start(g_lo, g, *, name):
    n, r, w = g.shape
    rh = r // 2

    def body(lo_ref, g_ref, land_lo_ref, land_f_ref, send_sems, recv_sems, lo_thru, g_thru, land_lo_thru, land_f_thru, token):
        for cp in _scatter_copies(lo_ref, g_ref, land_lo_ref, land_f_ref, send_sems, recv_sems, True):
            cp.start()
        token[...] = jnp.zeros_like(token)

    return pl.pallas_call(
        body,
        name=name,
        out_shape=(pltpu.SemaphoreType.DMA((N_PARTS + 1,)), pltpu.SemaphoreType.DMA((N_PARTS + 1,)), pltpu.HBM(g_lo.shape, g_lo.dtype),
                   pltpu.HBM(g.shape, g.dtype), pltpu.HBM((N_PARTS, rh, w), g_lo.dtype), pltpu.HBM((rh, w), g.dtype),
                   jax.ShapeDtypeStruct((8, 128), F32)),
        in_specs=(HBM, HBM, HBM, HBM),
        out_specs=(SEM, SEM, HBM, HBM, HBM, HBM, pl.BlockSpec(memory_space=pltpu.VMEM)),
        input_output_aliases={0: 2, 1: 3, 2: 4, 3: 5},
        compiler_params=pltpu.CompilerParams(has_side_effects=EFFECT),
    )(_in_hbm(g_lo), _in_hbm(g), _in_hbm(lax.empty((N_PARTS, rh, w), g_lo.dtype)), _in_hbm(lax.empty((rh, w), g.dtype)))


def _scatter_wait(send_sems, recv_sems, g_lo, g, land_lo, land_f, after, *, name):
    def body(lo_ref, g_ref, land_lo_ref, land_f_ref, send_sems, recv_sems, after_ref, o0, o1, o2, o3):
        for cp in _scatter_copies(lo_ref, g_ref, land_lo_ref, land_f_ref, send_sems, recv_sems, False):
            cp.wait_send()
            cp.wait_recv()

    arrays = (g_lo, g, land_lo, land_f)
    return pl.pallas_call(
        body,
        name=name,
        out_shape=tuple(pltpu.HBM(a.shape, a.dtype) for a in arrays),
        in_specs=(HBM, HBM, HBM, HBM, SEM, SEM, ANY),
        out_specs=(HBM, HBM, HBM, HBM),
        input_output_aliases={0: 0, 1: 1, 2: 2, 3: 3},
        compiler_params=pltpu.CompilerParams(has_side_effects=EFFECT),
    )(*arrays, send_sems, recv_sems, after)


def _scatter_sum(g, land_lo, land_f, where, *, name):
    n, r, w = g.shape
    rh = r // 2
    tr = _pick(rh, (256, 160, 80))
    nt = rh // tr

    def body(where_ref, g_ref, f_ref, lo_ref, o_ref):
        acc = g_ref[0] + f_ref[...]
        for part in range(N_PARTS):
            acc = acc + lo_ref[part].astype(F32)
        o_ref[...] = acc

    return pl.pallas_call(
        body,
        name=name,
        grid_spec=pltpu.PrefetchScalarGridSpec(
            num_scalar_prefetch=1,
            grid=(nt,),
            in_specs=[pl.BlockSpec((1, tr, w), lambda i, wh: (wh[1], wh[0] * nt + i, 0)),
                      pl.BlockSpec((tr, w), lambda i, wh: (i, 0)),
                      pl.BlockSpec((N_PARTS, tr, w), lambda i, wh: (0, i, 0))],
            out_specs=pl.BlockSpec((tr, w), lambda i, wh: (wh[0] * nt + i, 0)),
        ),
        out_shape=jax.ShapeDtypeStruct((r, w), F32),
        compiler_params=_cp(("parallel",)),
    )(where, g, land_f, land_lo)


def _swap_all(shards, *, name):
    n = len(shards)

    def body(*refs):
        ins, outs = refs[:n], refs[n : 2 * n]
        send_sems, recv_sems = refs[2 * n :]
        x, y, c = _place()
        copies = []
        for i, (e_ref, o_ref) in enumerate(zip(ins, outs)):
            rows = _half(c, e_ref.shape[0] // 2)
            copies.append(pltpu.make_async_remote_copy(src_ref=e_ref.at[rows], dst_ref=o_ref.at[rows], send_sem=send_sems.at[i],
                                                       recv_sem=recv_sems.at[i], device_id=(x, y, 1 - c), device_id_type=MESH))
        for cp in copies:
            cp.start()
        for cp in copies:
            cp.wait()

    return pl.pallas_call(
        body,
        name=name,
        in_specs=[ANY] * n,
        out_specs=[ANY] * n,
        out_shape=[jax.ShapeDtypeStruct(e.shape, e.dtype) for e in shards],
        input_output_aliases={i: i for i in range(n)},
        scratch_shapes=[pltpu.SemaphoreType.DMA((n,)), pltpu.SemaphoreType.DMA((n,))],
        compiler_params=pltpu.CompilerParams(has_side_effects=True),
    )(*shards)


def _sum_small(small, after):
    n_dev = 8

    def body(s_ref, after_ref, o_ref, all_ref, send_sems, recv_sems):
        x, y, c = _place()
        me = 4 * x + 2 * y + c
        all_ref[me] = s_ref[...]
        copies = []
        for k in range(1, n_dev):
            cx, cy = _rel_chip(x, y, k >> 1)
            cc = 1 - c if k & 1 else c
            copies.append(pltpu.make_async_remote_copy(
                src_ref=s_ref, dst_ref=all_ref.at[me], send_sem=send_sems.at[k - 1], recv_sem=recv_sems.at[k - 1],
                device_id=(cx, cy, cc), device_id_type=MESH))
        for cp in copies:
            cp.start()
        for cp in copies:
            cp.wait()
        acc = all_ref[0]
        for a in range(1, n_dev):
            acc = acc + all_ref[a]
        o_ref[...] = acc

    vm = pl.BlockSpec(memory_space=pltpu.VMEM)
    return pl.pallas_call(
        body,
        name="sum_small",
        in_specs=[vm, ANY],
        out_specs=vm,
        out_shape=jax.ShapeDtypeStruct(small.shape, F32),
        scratch_shapes=[pltpu.VMEM((n_dev,) + small.shape, F32), pltpu.SemaphoreType.DMA((n_dev - 1,)), pltpu.SemaphoreType.DMA((n_dev - 1,))],
        compiler_params=pltpu.CompilerParams(has_side_effects=True),
    )(small, after)


MATS = {"w_in": (776, True), "w_out": (256, False), "w_xq": (256, False), "w_xkv": (512, True), "w_xo": (256, False),
        "w_up": (1024, True), "w_down": (1024, False)}
GATHER_FIRST = ("w_in",)
GATHER_REST = ("w_out", "w_xq", "w_xkv", "w_xo", "w_up", "w_down")
GRAD_GROUPS = (("w_up", "w_down"), ("w_out", "w_xq", "w_xkv", "w_xo"), ("w_in",))


def _group_rows(names):
    n = sum(MATS[name][0] for name in names)
    return n + (-n) % 32


def _pack(pieces, rows):
    p = jnp.concatenate(pieces, axis=0) if len(pieces) > 1 else pieces[0]
    return jnp.pad(p, ((0, rows - p.shape[0]), (0, 0))) if rows > p.shape[0] else p


def _unpack(rows, names):
    out, off = {}, 0
    for name in names:
        out[name] = rows[off : off + MATS[name][0]]
        off += MATS[name][0]
    return out


SMALL = (
    ("mix_norm", 1024), ("conv_norm", 512), ("b_af", 256), ("b_ab", 256), ("gla_norm", 128), ("xa_norm", 1024), ("mem_norm", 1024),
    ("mlp_norm", 1024), ("final_norm", 1024), ("conv_w", 1536), ("w_af", 4096), ("w_ab", 4096), ("loss", 128),
)


def kernel(x, mem, mix_norm, w_in, conv_w, conv_norm, w_af, b_af, w_ab, b_ab, gla_norm, w_out, xa_norm, mem_norm, w_xq, w_xkv, w_xo, mlp_norm, w_up, w_down, final_norm, loss_target, m_mix_norm, m_w_in, m_conv_w, m_conv_norm, m_w_af, m_b_af, m_w_ab, m_b_ab, m_gla_norm, m_w_out, m_xa_norm, m_mem_norm, m_w_xq, m_w_xkv, m_w_xo, m_mlp_norm, m_w_up, m_w_down, m_final_norm, v_mix_norm, v_w_in, v_conv_w, v_conv_norm, v_w_af, v_b_af, v_w_ab, v_b_ab, v_gla_norm, v_w_out, v_xa_norm, v_mem_norm, v_w_xq, v_w_xkv, v_w_xo, v_mlp_norm, v_w_up, v_w_down, v_final_norm):
    given = dict(locals())
    xi, yi, ci = _place()
    chip = 2 * xi + yi
    where = jnp.stack([ci, chip]).astype(jnp.int32)

    lo = {name: (given[name][0].T if MATS[name][1] else given[name][0]).astype(_CD) for name in MATS}
    pack_rest = _pack([lo[name] for name in GATHER_REST], _group_rows(GATHER_REST))
    pack_first = _pack([lo[name] for name in GATHER_FIRST], _group_rows(GATHER_FIRST))
    xs, mems, tgt = x[0], mem[0], loss_target[0]
    add_res = lambda acc, res: (acc + res,)
    behind = lambda gain, token: gain + token[0, 0]
    first_send, first_recv, pack_first, land_first, first_token = _gather_start(pack_first, mix_norm, name="gather_first_start")
    rest_send, rest_recv, pack_rest, land_rest, rest_token = _gather_start(pack_rest, first_token, name="gather_rest_start")
    h1 = _rms_fwd(xs, behind(mix_norm, rest_token), name="norm_mix")
    pack_first, land_first = _gather_wait(first_send, first_recv, pack_first, land_first, h1, name="gather_first_wait")
    got_first = _gather_spread(land_first, name="gather_first_spread")

    def whole(got, off, rows):
        return got[:, off : off + rows].reshape(N_CHIPS * rows, D_MODEL)

    w_in_t = whole(got_first, 0, MATS["w_in"][0])
    w_za = jnp.concatenate([w_in_t[0:1536], w_in_t[2560:3072]], axis=0)
    w_zb = jnp.concatenate([w_in_t[1536:2560], w_in_t[3072:W_IN_COLS], jnp.zeros((ZB_COLS - 1056, D_MODEL), _CD)], axis=0)

    def placed(shard, full_shape, col):
        return lax.dynamic_update_slice(jnp.zeros(full_shape, F32), shard, (0, col)).reshape(-1, 128)

    sw = jnp.concatenate([
        placed(conv_w[0], (CONV_K, CONV_WIDTH), 128 * chip),
        placed(w_af[0], (GLA_LOWRANK, GLA_K_TOTAL), 64 * chip),
        placed(w_ab[0], (GLA_LOWRANK, GLA_K_TOTAL), 64 * chip),
    ], axis=0)
    sw = jnp.pad(sw, ((0, SMALL_ROWS - sw.shape[0]), (0, 0))) * (ci == 0).astype(F32)
    sw = _sum_small(sw, got_first)
    conv_w_full = sw[0:12].reshape(CONV_K, CONV_WIDTH)
    w_af_full = sw[12:44].reshape(GLA_LOWRANK, GLA_K_TOTAL)
    w_ab_full = sw[44:76].reshape(GLA_LOWRANK, GLA_K_TOTAL)
    waf_p = jnp.pad(w_af_full, ((0, 128 - GLA_LOWRANK), (0, 0))).astype(_CD)
    wab_p = jnp.pad(w_ab_full, ((GLA_LOWRANK, 128 - 2 * GLA_LOWRANK), (0, 0))).astype(_CD)

    z_b = _mm(h1, w_zb, mode="nt", name="proj_in_b", tn=ZB_COLS)
    z_a = _mm(h1, w_za, mode="nt", name="proj_in_a", tm=512, tn=ZA_COLS)
    b_f, b_b = _gate_fwd(z_b, waf_p, wab_p, b_af, b_ab, name="gates")
    o_f, st_f, o_b, st_b = _gla_fwd(z_b, b_f, b_b, name="gla_scan")
    y = _mix_fwd(z_a, o_f, o_b, conv_w_full, conv_norm, gla_norm, name="mix_out")
    pack_rest, land_rest = _gather_wait(rest_send, rest_recv, pack_rest, land_rest, y, name="gather_rest_wait")
    gathered = _gather_spread(land_rest, name="gather_rest_spread")
    wt, off = {}, 0
    for name in GATHER_REST:
        wt[name] = whole(gathered, off, MATS[name][0])
        off += MATS[name][0]
    x1, hx = _mm_rows(y, wt["w_out"], mode="nn", name="proj_out", rows=(xs,), vecs=(xa_norm,), out_rows=(F32, _CD), epilogue=_ep_residual_norm)
    qx = _mm(hx, wt["w_xq"], mode="nn", name="proj_xq", out_dtypes=(_CD,))
    hmem = _rms_fwd(mems, mem_norm, name="norm_mem")
    kv = _mm(hmem, wt["w_xkv"], mode="nt", name="proj_xkv", out_dtypes=(_CD,))
    ox = _xattn_fwd(qx, kv, name="xattn")
    x2, hm = _mm_rows(ox, wt["w_xo"], mode="nn", name="proj_xo", rows=(x1,), vecs=(mlp_norm,), out_rows=(F32, _CD), epilogue=_ep_residual_norm)
    act, relu_u = _mm(hm, wt["w_up"], mode="nt", name="mlp_up", out_dtypes=(_CD, _CD),
                      epilogue=lambda acc: (jnp.square(jnp.maximum(acc, 0.0)), jnp.maximum(acc, 0.0)))
    dx3, dx3_lo, loss_part, g_final_norm = _mm_rows(
        act, wt["w_down"], mode="nn", name="mlp_down", rows=(x2, tgt), vecs=(final_norm.reshape(1, D_MODEL),),
        out_rows=(F32, _CD), out_vecs=(128, D_MODEL), epilogue=_ep_loss)

    grads_t = {}

    def start_group(names, tag):
        rows = _group_rows(names)
        g = jnp.stack([_pack([grads_t[name][a * MATS[name][0] : (a + 1) * MATS[name][0]] for name in names], rows) for a in range(N_CHIPS)])
        return _scatter_start(g.astype(_TD), g, name="grads_" + tag + "_start")

    def finish_group(state, after, tag):
        send_sems, recv_sems, g_lo, g, land_lo, land_f, _ = state
        g_lo, g, land_lo, land_f = _scatter_wait(send_sems, recv_sems, g_lo, g, land_lo, land_f, after, name="grads_" + tag + "_wait")
        return _scatter_sum(g, land_lo, land_f, where, name="grads_" + tag + "_sum")

    def new_packs(names):
        shape = (N_CHIPS, _group_rows(names), D_MODEL)
        return lax.empty(shape, F32), lax.empty(shape, _TD)

    def grad_into(packs, names, which, a, b, name):
        off = sum(MATS[other][0] for other in names[: names.index(which)])
        return _mm_tn_into(a, b, packs, rows=MATS[which][0], off=off, name=name)

    du = _mm(dx3_lo, wt["w_down"], mode="nt", name="mlp_down_dx", out_dtypes=(_CD,), extras=(relu_u,),
             epilogue=lambda acc, rr: (acc * (2.0 * rr.astype(F32)),))
    packs = new_packs(GRAD_GROUPS[0])
    packs = grad_into(packs, GRAD_GROUPS[0], "w_down", act, dx3_lo, "mlp_down_dw")
    packs = grad_into(packs, GRAD_GROUPS[0], "w_up", du, hm, "mlp_up_dw")
    mlp_state = _scatter_start(packs[1], packs[0], name="grads_mlp_start")
    dx2, dx2_lo, g_mlp_norm = _mm_rows(
        du, wt["w_up"], mode="nn", name="mlp_up_dx", rows=(x2, dx3), vecs=(behind(mlp_norm, mlp_state[-1]),),
        out_rows=(F32, _CD), out_vecs=(D_MODEL,), epilogue=_ep_norm_bwd)
    dox = _mm(dx2_lo, wt["w_xo"], mode="nt", name="proj_xo_dx", out_dtypes=(_CD,))
    packs = new_packs(GRAD_GROUPS[1])
    packs = grad_into(packs, GRAD_GROUPS[1], "w_xo", ox, dx2_lo, "proj_xo_dw")
    dqx, dkv = _xattn_bwd(qx, kv, dox, name="xattn_bwd")
    packs = grad_into(packs, GRAD_GROUPS[1], "w_xq", hx, dqx, "proj_xq_dw")
    dx1, dx1_lo, g_xa_norm = _mm_rows(
        dqx, wt["w_xq"], mode="nt", name="proj_xq_dx", rows=(x1, dx2), vecs=(xa_norm,),
        out_rows=(F32, _CD), out_vecs=(D_MODEL,), epilogue=_ep_norm_bwd)
    dkv_lo = dkv.astype(_CD)
    packs = grad_into(packs, GRAD_GROUPS[1], "w_xkv", dkv_lo, hmem, "proj_xkv_dw")
    dhmem = _mm(dkv_lo, wt["w_xkv"], mode="nn", name="proj_xkv_dx")
    (g_mem_norm,) = _rms_bwd(mems, mem_norm, dhmem, name="norm_mem_bwd", want_dx=False, want_lo=False)
    dy = _mm(dx1_lo, wt["w_out"], mode="nt", name="proj_out_dx")
    packs = grad_into(packs, GRAD_GROUPS[1], "w_out", y, dx1_lo, "proj_out_dw")
    attn_state = _scatter_start(packs[1], packs[0], name="grads_attn_start")
    dz_a, do, g_conv_w, g_conv_norm, g_gla_norm = _mix_bwd(z_a, o_f, o_b, dy, conv_w_full, behind(conv_norm, attn_state[-1]), gla_norm, name="mix_out_bwd")
    dqkv_f, db_f, dqkv_b, db_b = _gla_bwd(z_b, b_f, b_b, do, st_f, st_b, name="gla_scan_bwd")
    dz_b, g_waf_p, g_wab_p, g_b_af, g_b_ab = _gate_bwd(z_b, waf_p, wab_p, b_af, b_ab, db_f, db_b, dqkv_f, dqkv_b, name="gates_bwd")
    g_za = _mm_tn(dz_a, h1, name="proj_in_a_dw")
    g_zb = _mm_tn(dz_b, h1, name="proj_in_b_dw")
    grads_t["w_in"] = jnp.concatenate([g_za[0:1536], g_zb[0:1024], g_za[1536:2048], g_zb[1024:1056]], axis=0)
    in_state = start_group(GRAD_GROUPS[2], "in")
    dh1_a = _mm(dz_a, w_za, mode="nn", name="proj_in_a_dx", tm=512, tk=ZA_COLS)
    grad_x, g_mix_norm = _mm_rows(
        dz_b, w_zb, mode="nn", name="proj_in_b_dx", rows=(xs, dx1, dh1_a), vecs=(behind(mix_norm, in_state[-1]),),
        out_rows=(F32,), out_vecs=(D_MODEL,), epilogue=_ep_norm_bwd)

    half_mlp = finish_group(mlp_state, grad_x, "mlp")
    half_attn = finish_group(attn_state, half_mlp, "attn")
    half_in = finish_group(in_state, half_attn, "in")
    shard_rows = {}
    for names, rows in zip(GRAD_GROUPS, _swap_all([half_mlp, half_attn, half_in], name="shards_to_sibling")):
        off = 0
        for name in names:
            shard_rows[name] = (rows, off)
            off += MATS[name][0]

    small_vals = dict(mix_norm=g_mix_norm, conv_norm=g_conv_norm, b_af=g_b_af, b_ab=g_b_ab, gla_norm=g_gla_norm, xa_norm=g_xa_norm,
                      mem_norm=g_mem_norm, mlp_norm=g_mlp_norm, final_norm=g_final_norm, conv_w=g_conv_w,
                      w_af=g_waf_p[0:GLA_LOWRANK], w_ab=g_wab_p[GLA_LOWRANK : 2 * GLA_LOWRANK], loss=loss_part)
    small = jnp.concatenate([small_vals[name].reshape(-1, 128) for name, _ in SMALL], axis=0)
    small = _sum_small(jnp.pad(small, ((0, SMALL_ROWS - small.shape[0]), (0, 0))), loss_part)
    g_small, off = {}, 0
    for name, n in SMALL:
        g_small[name] = small[off : off + n // 128]
        off += n // 128
    loss = g_small["loss"][0, 0]
    g_small["conv_w"] = lax.dynamic_slice(g_small["conv_w"].reshape(CONV_K, CONV_WIDTH), (0, 128 * chip), (CONV_K, 128))
    g_small["w_af"] = lax.dynamic_slice(g_small["w_af"].reshape(GLA_LOWRANK, GLA_K_TOTAL), (0, 64 * chip), (GLA_LOWRANK, 64))
    g_small["w_ab"] = lax.dynamic_slice(g_small["w_ab"].reshape(GLA_LOWRANK, GLA_K_TOTAL), (0, 64 * chip), (GLA_LOWRANK, 64))

    names = ["mix_norm", "w_in", "conv_w", "conv_norm", "w_af", "b_af", "w_ab", "b_ab", "gla_norm", "w_out", "xa_norm", "mem_norm",
             "w_xq", "w_xkv", "w_xo", "mlp_norm", "w_up", "w_down", "final_norm"]
    big_names = list(MATS)
    as2d = lambda a: a.reshape(1, -1) if a.ndim == 1 else a.reshape(a.shape[-2:])
    grads, deltas, new_m, new_v = {}, {}, {}, {}
    for name in big_names:
        rows, off = shard_rows[name]
        wmv = [as2d(given[name]), as2d(given["m_" + name]), as2d(given["v_" + name])]
        as_stored = name == "w_in"
        if as_stored:
            wmv = [a.T for a in wmv]
        res = _adamw(*wmv, rows, off, transposed=MATS[name][1] and not as_stored, name="adamw_" + name)
        grads[name], deltas[name], new_m[name], new_v[name] = [a.T for a in res] if as_stored else res
    small_names = [name for name in names if name not in big_names]
    groups = []
    for name in small_names:
        grads[name] = g_small[name].reshape(as2d(given[name]).shape)
        groups.append((as2d(given[name]), grads[name], as2d(given["m_" + name]), as2d(given["v_" + name])))
    for name, res in zip(small_names, _adamw_small(groups, name="adamw_small")):
        deltas[name], new_m[name], new_v[name] = res

    like = lambda name, a: a.reshape(given[name].shape)
    return (loss, grad_x[None], *[like(n, grads[n]) for n in names], *[like(n, deltas[n]) for n in names],
            *[like(n, new_m[n]) for n in names], *[like(n, new_v[n]) for n in names])
```

```python
import functools

import jax
import jax.numpy as jnp
from jax import lax
from jax.experimental import pallas as pl
from jax.experimental.pallas import tpu as pltpu

F32 = jnp.float32
BF16 = jnp.bfloat16
_CD = jnp.bfloat16
_TD = jnp.bfloat16

D_MODEL = 1024
N_MEM = 256
CONV_WIDTH = 512
CONV_GROUP = 64
CONV_K = 3
GLA_HEADS = 4
GLA_DK = 64
GLA_DV = 128
GLA_K_TOTAL = 256
GLA_V_TOTAL = 512
GLA_LOWRANK = 16
GLA_GATE_SCALE = 1.0 / 16.0
GLA_CHUNK = 64
XA_HEADS = 4
XA_HEAD_DIM = 256
D_FF = 4096
EPS = 1e-6
W_IN_COLS = 3104
ZA_COLS = 2048
ZB_COLS = 1152
LR_COL = 1024

ADAM_LR = 0.001
ADAM_B1 = 0.9
ADAM_B2 = 0.999
ADAM_EPS = 1e-08
ADAM_WD = 0.01
ADAM_STEP = 10

N_CHIPS = 4
PACK_W = 1024
PACK_ROWS = 4160
PACK_TILE = 160
SMALL_ROWS = 128

_TS = 512
_VMEM = 44 * 1024 * 1024
MESH = pl.DeviceIdType.MESH
ANY = pl.BlockSpec(memory_space=pl.ANY)


def _cp(sem=None, **kw):
    return pltpu.CompilerParams(dimension_semantics=sem, vmem_limit_bytes=_VMEM, **kw)


def _dot(a, b):
    return jnp.dot(a.astype(_CD), b.astype(_CD), preferred_element_type=F32)


def _dot_nt(a, b):
    return lax.dot_general(a.astype(_CD), b.astype(_CD), (((1,), (1,)), ((), ())), preferred_element_type=F32)


def _dot_tn(a, b):
    return lax.dot_general(a.astype(_CD), b.astype(_CD), (((0,), (0,)), ((), ())), preferred_element_type=F32)


def _dot_split(x, ones):
    hi = x.astype(BF16)
    r = x - hi.astype(F32)
    mid = r.astype(BF16)
    lo = (r - mid.astype(F32)).astype(BF16)
    d = lambda p: jnp.dot(p, ones, preferred_element_type=F32)
    return d(hi) + d(mid) + d(lo)


def _pick(n, cands=(1024, 640, 512, 256, 128)):
    for t in cands:
        if n % t == 0:
            return t
    return n


def _rows(s):
    return min(_TS, s)


def _sigmoid(v):
    e = jnp.exp(-jnp.abs(v))
    return jnp.where(v >= 0, 1.0 / (1.0 + e), e / (1.0 + e))


def _mm(a, b, *, mode, name, out_dtypes=(F32,), extras=(), epilogue=None, tm=None, tn=None, tk=None):
    m, k = a.shape
    n = b.shape[1] if mode == "nn" else b.shape[0]
    tm = min(m, tm or 1024)
    tn = tn or _pick(n)
    tk = tk or _pick(k)
    nk = k // tk
    n_ex, n_out = len(extras), len(out_dtypes)

    def body(*refs):
        a_ref, b_ref = refs[:2]
        ex = refs[2 : 2 + n_ex]
        outs = refs[2 + n_ex : 2 + n_ex + n_out]
        part = _dot(a_ref[...], b_ref[...]) if mode == "nn" else _dot_nt(a_ref[...], b_ref[...])

        def finish(acc):
            res = epilogue(acc, *[e[...] for e in ex]) if epilogue else (acc,)
            for o, r in zip(outs, res):
                o[...] = r.astype(o.dtype)

        if nk == 1:
            finish(part)
        else:
            acc_ref = refs[-1]
            kk = pl.program_id(2)

            @pl.when(kk == 0)
            def _():
                acc_ref[...] = part

            @pl.when(kk > 0)
            def _():
                acc_ref[...] += part

            @pl.when(kk == nk - 1)
            def _():
                finish(acc_ref[...])

    b_spec = pl.BlockSpec((tk, tn), lambda i, j, kk: (kk, j)) if mode == "nn" else pl.BlockSpec((tn, tk), lambda i, j, kk: (j, kk))
    tile = pl.BlockSpec((tm, tn), lambda i, j, kk: (i, j))
    out = pl.pallas_call(
        body,
        name=name,
        grid=(m // tm, n // tn, nk),
        in_specs=[pl.BlockSpec((tm, tk), lambda i, j, kk: (i, kk)), b_spec] + [tile] * n_ex,
        out_specs=[tile] * n_out,
        out_shape=[jax.ShapeDtypeStruct((m, n), dt) for dt in out_dtypes],
        scratch_shapes=[pltpu.VMEM((tm, tn), F32)] if nk > 1 else [],
        compiler_params=_cp(("parallel", "parallel", "arbitrary")),
    )(a, b, *extras)
    return out[0] if n_out == 1 else out


def _mm_tn(a, b, *, name):
    s, m = a.shape
    n = b.shape[1]
    cap = max(128, (1 << 20) // n)
    tm = _pick(m, tuple(t for t in (512, 640, 384, 256, 128) if t <= max(cap, 128)))
    ts = min(s, 1 << (((1 << 22) // n).bit_length() - 1))
    ns = s // ts

    def body(a_ref, b_ref, o_ref):
        part = _dot_tn(a_ref[...], b_ref[...])
        if ns == 1:
            o_ref[...] = part
        else:
            ss = pl.program_id(1)

            @pl.when(ss == 0)
            def _():
                o_ref[...] = part

            @pl.when(ss > 0)
            def _():
                o_ref[...] += part

    return pl.pallas_call(
        body,
        name=name,
        grid=(m // tm, ns),
        in_specs=[pl.BlockSpec((ts, tm), lambda i, ss: (ss, i)), pl.BlockSpec((ts, n), lambda i, ss: (ss, 0))],
        out_specs=pl.BlockSpec((tm, n), lambda i, ss: (i, 0)),
        out_shape=jax.ShapeDtypeStruct((m, n), F32),
        compiler_params=_cp(("parallel", "arbitrary")),
    )(a, b)


def _mm_tn_into(a, b, packs, *, rows, off, name):
    s, m = a.shape
    n = b.shape[1]
    tm = 512
    tr = min(tm, rows)
    per, chips = rows // tr, tm // tr
    ts = min(s, 1 << (((1 << 22) // n).bit_length() - 1))
    ns = s // ts

    def body(a_ref, b_ref, f_in, lo_in, f_ref, lo_ref):
        part = _dot_tn(a_ref[...], b_ref[...])
        pieces = [part[c * tr : (c + 1) * tr] for c in range(chips)]
        if ns == 1:
            for c, p in enumerate(pieces):
                f_ref[c] = p
                lo_ref[c] = p.astype(lo_ref.dtype)
        else:
            ss = pl.program_id(1)

            @pl.when(ss == 0)
            def _():
                for c, p in enumerate(pieces):
                    f_ref[c] = p

            @pl.when(ss > 0)
            def _():
                for c, p in enumerate(pieces):
                    f_ref[c] += p

            @pl.when(ss == ns - 1)
            def _():
                lo_ref[...] = f_ref[...].astype(lo_ref.dtype)

    spec = pl.BlockSpec((chips, tr, n), lambda i, ss: (i // per, off // tr + i % per, 0))
    return pl.pallas_call(
        body,
        name=name,
        grid=(m // tm, ns),
        in_specs=[pl.BlockSpec((ts, tm), lambda i, ss: (ss, i)), pl.BlockSpec((ts, n), lambda i, ss: (ss, 0)), ANY, ANY],
        out_specs=[spec, spec],
        out_shape=[jax.ShapeDtypeStruct(p.shape, p.dtype) for p in packs],
        input_output_aliases={2: 0, 3: 1},
        compiler_params=_cp(("parallel", "arbitrary")),
    )(a, b, *packs)


def _mm_rows(a, b, *, mode, name, rows=(), vecs=(), out_rows=(), out_vecs=(), epilogue, tm=512):
    m, k = a.shape
    n = b.shape[1] if mode == "nn" else b.shape[0]
    tm = min(m, tm)
    parts = 2 if tm % 256 == 0 else 1
    n_r, n_v, n_or, n_ov = len(rows), len(vecs), len(out_rows), len(out_vecs)

    def body(*refs):
        a_ref, b_ref = refs[:2]
        r_refs = refs[2 : 2 + n_r]
        v_refs = refs[2 + n_r : 2 + n_r + n_v]
        or_refs = refs[2 + n_r + n_v : 2 + n_r + n_v + n_or]
        ov_refs = refs[2 + n_r + n_v + n_or :]
        res_vecs = None
        for p in range(parts):
            rs = slice(p * tm // parts, (p + 1) * tm // parts)
            acc = _dot(a_ref[rs, :], b_ref[...]) if mode == "nn" else _dot_nt(a_ref[rs, :], b_ref[...])
            res_rows, part_vecs = epilogue(acc, [r[rs, :] for r in r_refs], [v[...] for v in v_refs])
            for o, r in zip(or_refs, res_rows):
                o[rs, :] = r.astype(o.dtype)
            res_vecs = part_vecs if res_vecs is None else [s + t for s, t in zip(res_vecs, part_vecs)]
        if n_ov:
            first = pl.program_id(0) == 0

            @pl.when(first)
            def _():
                for o, r in zip(ov_refs, res_vecs):
                    o[...] = r

            @pl.when(jnp.logical_not(first))
            def _():
                for o, r in zip(ov_refs, res_vecs):
                    o[...] += r

    tile = pl.BlockSpec((tm, n), lambda i: (i, 0))
    whole = lambda arr: pl.BlockSpec(arr.shape, lambda i: (0, 0))
    vec = lambda w: pl.BlockSpec((1, w), lambda i: (0, 0))
    out = pl.pallas_call(
        body,
        name=name,
        grid=(m // tm,),
        in_specs=[pl.BlockSpec((tm, k), lambda i: (i, 0)), whole(b)] + [tile] * n_r + [vec(v.shape[1]) for v in vecs],
        out_specs=[tile] * n_or + [vec(w) for w in out_vecs],
        out_shape=[jax.ShapeDtypeStruct((m, n), dt) for dt in out_rows] + [jax.ShapeDtypeStruct((1, w), F32) for w in out_vecs],
        compiler_params=_cp(("arbitrary",) if n_ov else ("parallel",)),
    )(a, b, *rows, *vecs)
    return out


def _ep_residual_norm(acc, rows, vecs):
    x = acc + rows[0]
    r = lax.rsqrt(jnp.mean(x * x, axis=-1, keepdims=True) + EPS)
    return [x, x * r * vecs[0]], []


def _ep_norm_bwd(acc, rows, vecs):
    dy = acc
    for extra in rows[2:]:
        dy = dy + extra
    x, dres = rows[0], rows[1]
    r = lax.rsqrt(jnp.mean(x * x, axis=-1, keepdims=True) + EPS)
    xh = x * r
    dxh = dy * vecs[0]
    dx = r * (dxh - xh * jnp.mean(dxh * xh, axis=-1, keepdims=True)) + dres
    return [dx, dx], [jnp.sum(dy * xh, axis=0, keepdims=True)]


def _ep_loss(acc, rows, vecs):
    x = acc + rows[0]
    d = x.shape[-1]
    r = lax.rsqrt(jnp.mean(x * x, axis=-1, keepdims=True) + EPS)
    xh = x * r
    err = xh * vecs[0] - rows[1]
    loss = jnp.zeros((1, 128), F32) + 0.5 * jnp.sum(jnp.mean(err * err, axis=-1, keepdims=True))
    dy = err * (1.0 / d)
    dxh = dy * vecs[0]
    dx = r * (dxh - xh * jnp.mean(dxh * xh, axis=-1, keepdims=True))
    return [dx, dx], [loss, jnp.sum(dy * xh, axis=0, keepdims=True)]


def _rms_fwd(x, g, *, name):
    s, d = x.shape
    ts = _rows(s)

    def body(x_ref, g_ref, o_ref):
        xf = x_ref[...]
        r = lax.rsqrt(jnp.mean(xf * xf, axis=-1, keepdims=True) + EPS)
        o_ref[...] = (xf * r * g_ref[...]).astype(o_ref.dtype)

    return pl.pallas_call(
        body,
        name=name,
        grid=(s // ts,),
        in_specs=[pl.BlockSpec((ts, d), lambda i: (i, 0)), pl.BlockSpec((1, d), lambda i: (0, 0))],
        out_specs=pl.BlockSpec((ts, d), lambda i: (i, 0)),
        out_shape=jax.ShapeDtypeStruct((s, d), _CD),
        compiler_params=_cp(("parallel",)),
    )(x, g)


def _rms_bwd(x, g, dy, dres=None, *, name, want_dx=True, want_lo=True):
    s, d = x.shape
    ts = _rows(s)
    has_res = dres is not None

    def body(*refs):
        x_ref, g_ref, dy_ref = refs[:3]
        pos = 3
        dres_ref = refs[pos] if has_res else None
        pos += has_res
        dx_ref = refs[pos] if want_dx else None
        pos += want_dx
        lo_ref = refs[pos] if want_lo else None
        pos += want_lo
        dg_ref = refs[pos]
        xf = x_ref[...]
        r = lax.rsqrt(jnp.mean(xf * xf, axis=-1, keepdims=True) + EPS)
        xh = xf * r
        dyf = dy_ref[...]
        part = jnp.sum(dyf * xh, axis=0, keepdims=True)

        @pl.when(pl.program_id(0) == 0)
        def _():
            dg_ref[...] = part

        @pl.when(pl.program_id(0) > 0)
        def _():
            dg_ref[...] += part

        if want_dx or want_lo:
            dxh = dyf * g_ref[...]
            dx = r * (dxh - xh * jnp.mean(dxh * xh, axis=-1, keepdims=True))
            if has_res:
                dx = dx + dres_ref[...]
            if want_dx:
                dx_ref[...] = dx
            if want_lo:
                lo_ref[...] = dx.astype(lo_ref.dtype)

    tile = pl.BlockSpec((ts, d), lambda i: (i, 0))
    vec = pl.BlockSpec((1, d), lambda i: (0, 0))
    out_specs, out_shape = [], []
    if want_dx:
        out_specs.append(tile)
        out_shape.append(jax.ShapeDtypeStruct((s, d), F32))
    if want_lo:
        out_specs.append(tile)
        out_shape.append(jax.ShapeDtypeStruct((s, d), _CD))
    out_specs.append(vec)
    out_shape.append(jax.ShapeDtypeStruct((1, d), F32))
    return pl.pallas_call(
        body,
        name=name,
        grid=(s // ts,),
        in_specs=[tile, vec, tile] + ([tile] if has_res else []),
        out_specs=out_specs,
        out_shape=out_shape,
        compiler_params=_cp(("arbitrary",)),
    )(x, g, dy, *([dres] if has_res else []))


def _final_loss(x3, g, tgt, *, name):
    s, d = x3.shape
    ts = _rows(s)

    def body(x_ref, g_ref, t_ref, dx_ref, lo_ref, loss_ref, dg_ref):
        xf = x_ref[...]
        r = lax.rsqrt(jnp.mean(xf * xf, axis=-1, keepdims=True) + EPS)
        xh = xf * r
        gg = g_ref[...]
        err = xh * gg - t_ref[...]
        lpart = jnp.zeros((1, 128), F32) + 0.5 * jnp.sum(jnp.mean(err * err, axis=-1, keepdims=True))
        dy = err * (1.0 / d)
        gpart = jnp.sum(dy * xh, axis=0, keepdims=True)

        @pl.when(pl.program_id(0) == 0)
        def _():
            loss_ref[...] = lpart
            dg_ref[...] = gpart

        @pl.when(pl.program_id(0) > 0)
        def _():
            loss_ref[...] += lpart
            dg_ref[...] += gpart

        dxh = dy * gg
        dx = r * (dxh - xh * jnp.mean(dxh * xh, axis=-1, keepdims=True))
        dx_ref[...] = dx
        lo_ref[...] = dx.astype(lo_ref.dtype)

    tile = pl.BlockSpec((ts, d), lambda i: (i, 0))
    vec = pl.BlockSpec((1, d), lambda i: (0, 0))
    return pl.pallas_call(
        body,
        name=name,
        grid=(s // ts,),
        in_specs=[tile, vec, tile],
        out_specs=[tile, tile, pl.BlockSpec((1, 128), lambda i: (0, 0)), vec],
        out_shape=[
            jax.ShapeDtypeStruct((s, d), F32),
            jax.ShapeDtypeStruct((s, d), _CD),
            jax.ShapeDtypeStruct((1, 128), F32),
            jax.ShapeDtypeStruct((1, d), F32),
        ],
        compiler_params=_cp(("arbitrary",)),
    )(x3, g, tgt)


def _chunk_scan(v, row_in_chunk, suffix):
    t = v.shape[0]
    step = 1
    while step < GLA_CHUNK:
        if suffix:
            v = v + jnp.where(row_in_chunk < GLA_CHUNK - step, pltpu.roll(v, t - step, 0), 0.0)
        else:
            v = v + jnp.where(row_in_chunk >= step, pltpu.roll(v, step, 0), 0.0)
        step *= 2
    return v


def _gate_pre(lr, w_ref, b_ref):
    return _dot(lr, w_ref[...]) + b_ref[...]


def _gate_fwd(z, waf, wab, baf, bab, *, name):
    s = z.shape[0]
    ts = _rows(s)

    def body(lr_ref, waf_ref, wab_ref, baf_ref, bab_ref, bf_ref, bb_ref):
        lr = lr_ref[...]
        ric = lax.broadcasted_iota(jnp.int32, (ts, GLA_K_TOTAL), 0) & (GLA_CHUNK - 1)
        for w_ref, b_ref, o_ref, suffix in ((waf_ref, baf_ref, bf_ref, False), (wab_ref, bab_ref, bb_ref, True)):
            pre = _gate_pre(lr, w_ref, b_ref)
            la = (jnp.minimum(pre, 0.0) - jnp.log(1.0 + jnp.exp(-jnp.abs(pre)))) * GLA_GATE_SCALE
            o_ref[...] = _chunk_scan(la, ric, suffix)

    wspec = pl.BlockSpec((128, GLA_K_TOTAL), lambda i: (0, 0))
    bspec = pl.BlockSpec((1, GLA_K_TOTAL), lambda i: (0, 0))
    tile = pl.BlockSpec((ts, GLA_K_TOTAL), lambda i: (i, 0))
    return pl.pallas_call(
        body,
        name=name,
        grid=(s // ts,),
        in_specs=[pl.BlockSpec((ts, 128), lambda i: (i, LR_COL // 128)), wspec, wspec, bspec, bspec],
        out_specs=[tile, tile],
        out_shape=[jax.ShapeDtypeStruct((s, GLA_K_TOTAL), F32)] * 2,
        compiler_params=_cp(("parallel",)),
    )(z, waf, wab, baf, bab)


def _gate_bwd(z, waf, wab, baf, bab, dbf, dbb, dqkv_f, dqkv_b, *, name):
    s = z.shape[0]
    ts = _rows(s)

    def body(lr_ref, waf_ref, wab_ref, baf_ref, bab_ref, dbf_ref, dbb_ref, gf_ref, gb_ref, dzb_ref, dwf_ref, dwb_ref, dbaf_ref, dbab_ref):
        lr = lr_ref[...]
        ric = lax.broadcasted_iota(jnp.int32, (ts, GLA_K_TOTAL), 0) & (GLA_CHUNK - 1)
        first = pl.program_id(0) == 0
        dlr = None
        for w_ref, b_ref, db_ref, dw_ref, dbias_ref, suffix in (
            (waf_ref, baf_ref, dbf_ref, dwf_ref, dbaf_ref, True),
            (wab_ref, bab_ref, dbb_ref, dwb_ref, dbab_ref, False),
        ):
            pre = _gate_pre(lr, w_ref, b_ref)
            dla = _chunk_scan(db_ref[...], ric, suffix)
            dpre = dla * GLA_GATE_SCALE * _sigmoid(-pre)
            part = _dot_nt(dpre, w_ref[...])
            dlr = part if dlr is None else dlr + part
            dw = _dot_tn(lr, dpre)
            dbias = jnp.sum(dpre, axis=0, keepdims=True)

            @pl.when(first)
            def _():
                dw_ref[...] = dw
                dbias_ref[...] = dbias

            @pl.when(jnp.logical_not(first))
            def _():
                dw_ref[...] += dw
                dbias_ref[...] += dbias

        dzb_ref[...] = jnp.concatenate([gf_ref[...] + gb_ref[...], dlr], axis=1).astype(dzb_ref.dtype)

    wspec = pl.BlockSpec((128, GLA_K_TOTAL), lambda i: (0, 0))
    bspec = pl.BlockSpec((1, GLA_K_TOTAL), lambda i: (0, 0))
    tile = pl.BlockSpec((ts, GLA_K_TOTAL), lambda i: (i, 0))
    wide = pl.BlockSpec((ts, 2 * GLA_K_TOTAL + GLA_V_TOTAL), lambda i: (i, 0))
    return pl.pallas_call(
        body,
        name=name,
        grid=(s // ts,),
        in_specs=[pl.BlockSpec((ts, 128), lambda i: (i, LR_COL // 128)), wspec, wspec, bspec, bspec, tile, tile, wide, wide],
        out_specs=[pl.BlockSpec((ts, ZB_COLS), lambda i: (i, 0)), wspec, wspec, bspec, bspec],
        out_shape=[
            jax.ShapeDtypeStruct((s, ZB_COLS), _CD),
            jax.ShapeDtypeStruct((128, GLA_K_TOTAL), F32),
            jax.ShapeDtypeStruct((128, GLA_K_TOTAL), F32),
            jax.ShapeDtypeStruct((1, GLA_K_TOTAL), F32),
            jax.ShapeDtypeStruct((1, GLA_K_TOTAL), F32),
        ],
        compiler_params=_cp(("arbitrary",)),
    )(z, waf, wab, baf, bab, dbf, dbb, dqkv_f, dqkv_b)


def _gla_masks(rev):
    lane_head = lax.broadcasted_iota(jnp.int32, (1, GLA_K_TOTAL), 1) >> 6
    head_masks = [lane_head == h for h in range(GLA_HEADS)]
    t = lax.broadcasted_iota(jnp.int32, (GLA_HEADS * GLA_CHUNK, GLA_CHUNK), 0) & (GLA_CHUNK - 1)
    u = lax.broadcasted_iota(jnp.int32, (GLA_HEADS * GLA_CHUNK, GLA_CHUNK), 1)
    tri = (u > t) if rev else (u <= t)
    row = lax.broadcasted_iota(jnp.int32, (GLA_CHUNK, GLA_K_TOTAL), 0)
    total_row = row == (0 if rev else GLA_CHUNK - 1)
    return head_masks, tri, total_row


def _spread(a, head_masks):
    return jnp.concatenate([jnp.where(m, a, 0.0) for m in head_masks], axis=0)


def _stack(a):
    return jnp.concatenate([a[:, GLA_DV * h : GLA_DV * (h + 1)] for h in range(GLA_HEADS)], axis=0)


def _unstack(a):
    return jnp.concatenate([a[GLA_CHUNK * h : GLA_CHUNK * (h + 1)] for h in range(GLA_HEADS)], axis=1)


def _collect(a, head_masks):
    out = None
    for h, m in enumerate(head_masks):
        part = jnp.where(m, a[GLA_CHUNK * h : GLA_CHUNK * (h + 1)], 0.0)
        out = part if out is None else out + part
    return out


def _gla_chunk_terms(q_ref, k_ref, v_ref, b_ref, rows, head_masks, tri, total_row):
    q = q_ref[rows, :] * (GLA_DK**-0.5)
    k = k_ref[rows, :]
    v = v_ref[rows, :]
    b = b_ref[rows, :]
    eb = jnp.exp(b)
    enb = jnp.exp(-b)
    g = jnp.sum(jnp.where(total_row, b, 0.0), axis=0, keepdims=True)
    egb = jnp.exp(g - b)
    qt = q * eb
    kt = k * enb
    kh = k * egb
    q_heads = _spread(qt, head_masks)
    attn = jnp.where(tri, _dot_nt(q_heads, kt), 0.0)
    return v, eb, enb, egb, jnp.exp(g), qt, kt, kh, q_heads, attn


def _gla_specs(s, tb, rev_blocks):
    nb = s // tb
    rb = (lambda i: nb - 1 - i) if rev_blocks else (lambda i: i)
    q_spec = pl.BlockSpec((tb, GLA_K_TOTAL), lambda i: (rb(i), 0))
    k_spec = pl.BlockSpec((tb, GLA_K_TOTAL), lambda i: (rb(i), 1))
    v_spec = pl.BlockSpec((tb, GLA_V_TOTAL), lambda i: (rb(i), 1))
    b_spec = pl.BlockSpec((tb, GLA_K_TOTAL), lambda i: (rb(i), 0))
    o_spec = pl.BlockSpec((tb, GLA_V_TOTAL), lambda i: (rb(i), 0))
    st_spec = pl.BlockSpec((tb // GLA_CHUNK, GLA_DV, GLA_K_TOTAL), lambda i: (rb(i), 0, 0))
    return nb, q_spec, k_spec, v_spec, b_spec, o_spec, st_spec


def _gla_fwd_chunk(cidx, q_ref, k_ref, v_ref, b_ref, o_ref, sv_ref, st_ref, masks):
    head_masks, tri, total_row = masks
    rows = pl.ds(pl.multiple_of(cidx * GLA_CHUNK, GLA_CHUNK), GLA_CHUNK)
    v, _, _, _, eg, _, _, kh, q_heads, attn = _gla_chunk_terms(q_ref, k_ref, v_ref, b_ref, rows, head_masks, tri, total_row)
    o = jnp.concatenate(
        [_dot(attn[GLA_CHUNK * h : GLA_CHUNK * (h + 1)], v[:, GLA_DV * h : GLA_DV * (h + 1)]) for h in range(GLA_HEADS)], axis=1
    )
    st = st_ref[...]
    o_ref[rows, :] = o + _unstack(_dot_nt(q_heads, st))
    sv_ref[cidx] = st
    st_ref[...] = st * eg + _dot_tn(_stack(v), _spread(kh, head_masks))


def _gla_fwd(z, b_f, b_b, *, name):
    s = z.shape[0]
    tb = _rows(s)
    cpb = tb // GLA_CHUNK
    nb, qf, kf, vf, bf, of, sf = _gla_specs(s, tb, False)
    _, qr, kr, vr, br, orr, sr = _gla_specs(s, tb, True)

    def body(qf_ref, kf_ref, vf_ref, bf_ref, qr_ref, kr_ref, vr_ref, br_ref, of_ref, svf_ref, or_ref, svr_ref, stf_ref, str_ref):
        masks_f, masks_r = _gla_masks(False), _gla_masks(True)

        @pl.when(pl.program_id(0) == 0)
        def _():
            stf_ref[...] = jnp.zeros_like(stf_ref)
            str_ref[...] = jnp.zeros_like(str_ref)

        def chunk(ci, carry):
            _gla_fwd_chunk(ci, qf_ref, kf_ref, vf_ref, bf_ref, of_ref, svf_ref, stf_ref, masks_f)
            _gla_fwd_chunk(cpb - 1 - ci, qr_ref, kr_ref, vr_ref, br_ref, or_ref, svr_ref, str_ref, masks_r)
            return carry

        lax.fori_loop(0, cpb, chunk, 0)

    o_shape = jax.ShapeDtypeStruct((s, GLA_V_TOTAL), F32)
    st_shape = jax.ShapeDtypeStruct((s // GLA_CHUNK, GLA_DV, GLA_K_TOTAL), F32)
    return pl.pallas_call(
        body,
        name=name,
        grid=(nb,),
        in_specs=[qf, kf, vf, bf, qr, kr, vr, br],
        out_specs=[of, sf, orr, sr],
        out_shape=[o_shape, st_shape, o_shape, st_shape],
        scratch_shapes=[pltpu.VMEM((GLA_DV, GLA_K_TOTAL), F32)] * 2,
        compiler_params=_cp(("arbitrary",)),
    )(z, z, z, b_f, z, z, z, b_b)


def _gla_bwd_chunk(cidx, q_ref, k_ref, v_ref, b_ref, do_ref, sv_ref, dqkv_ref, db_ref, dst_ref, masks):
    head_masks, tri, total_row = masks
    rows = pl.ds(pl.multiple_of(cidx * GLA_CHUNK, GLA_CHUNK), GLA_CHUNK)
    v, eb, enb, egb, eg, qt, kt, kh, q_heads, attn = _gla_chunk_terms(q_ref, k_ref, v_ref, b_ref, rows, head_masks, tri, total_row)
    do_c = do_ref[rows, :]
    st = sv_ref[cidx]
    dst = dst_ref[...]
    do_s, v_s = _stack(do_c), _stack(v)
    hs = lambda a, h: a[GLA_CHUNK * h : GLA_CHUNK * (h + 1)]
    vs = lambda a, h: a[:, GLA_DV * h : GLA_DV * (h + 1)]
    dattn = jnp.concatenate([_dot_nt(vs(do_c, h), vs(v, h)) for h in range(GLA_HEADS)], axis=0)
    dattn = jnp.where(tri, dattn, 0.0)
    dv = jnp.concatenate([_dot_tn(hs(attn, h), vs(do_c, h)) for h in range(GLA_HEADS)], axis=1)
    dv = dv + _unstack(_dot_nt(_spread(kh, head_masks), dst))
    dqt = _collect(_dot(do_s, st), head_masks)
    dkt = jnp.zeros_like(dqt)
    for h in range(GLA_HEADS):
        dqt = dqt + jnp.where(head_masks[h], _dot(hs(dattn, h), kt), 0.0)
        dkt = dkt + jnp.where(head_masks[h], _dot_tn(hs(dattn, h), qt), 0.0)
    dkh = _collect(_dot(v_s, dst), head_masks)
    dg = jnp.sum(dkh * kh, axis=0, keepdims=True) + jnp.sum(dst * st, axis=0, keepdims=True) * eg
    db = dqt * qt - dkt * kt - dkh * kh + jnp.where(total_row, dg, 0.0)
    dq = dqt * eb * (GLA_DK**-0.5)
    dk = dkt * enb + dkh * egb
    dqkv_ref[rows, :] = jnp.concatenate([dq, dk, dv], axis=1)
    db_ref[rows, :] = db
    dst_ref[...] = dst * eg + _dot_tn(do_s, q_heads)


def _gla_bwd(z, b_f, b_b, do, st_f, st_b, *, name):
    s = z.shape[0]
    tb = _rows(s)
    cpb = tb // GLA_CHUNK
    wide = 2 * GLA_K_TOTAL + GLA_V_TOTAL
    nb, qf, kf, vf, bf, of, sf = _gla_specs(s, tb, True)
    _, qr, kr, vr, br, orr, sr = _gla_specs(s, tb, False)
    gf = pl.BlockSpec((tb, wide), lambda i: (nb - 1 - i, 0))
    gr = pl.BlockSpec((tb, wide), lambda i: (i, 0))

    def body(qf_ref, kf_ref, vf_ref, bf_ref, dof_ref, svf_ref, qr_ref, kr_ref, vr_ref, br_ref, dor_ref, svr_ref,
             gf_ref, dbf_ref, gr_ref, dbr_ref, dstf_ref, dstr_ref):
        masks_f, masks_r = _gla_masks(False), _gla_masks(True)

        @pl.when(pl.program_id(0) == 0)
        def _():
            dstf_ref[...] = jnp.zeros_like(dstf_ref)
            dstr_ref[...] = jnp.zeros_like(dstr_ref)

        def chunk(ci, carry):
            _gla_bwd_chunk(cpb - 1 - ci, qf_ref, kf_ref, vf_ref, bf_ref, dof_ref, svf_ref, gf_ref, dbf_ref, dstf_ref, masks_f)
            _gla_bwd_chunk(ci, qr_ref, kr_ref, vr_ref, br_ref, dor_ref, svr_ref, gr_ref, dbr_ref, dstr_ref, masks_r)
            return carry

        lax.fori_loop(0, cpb, chunk, 0)

    g_shape = jax.ShapeDtypeStruct((s, wide), F32)
    db_shape = jax.ShapeDtypeStruct((s, GLA_K_TOTAL), F32)
    return pl.pallas_call(
        body,
        name=name,
        grid=(nb,),
        in_specs=[qf, kf, vf, bf, of, sf, qr, kr, vr, br, orr, sr],
        out_specs=[gf, bf, gr, br],
        out_shape=[g_shape, db_shape, g_shape, db_shape],
        scratch_shapes=[pltpu.VMEM((GLA_DV, GLA_K_TOTAL), F32)] * 2,
        compiler_params=_cp(("arbitrary",)),
    )(z, z, z, b_f, do, st_f, z, z, z, b_b, do, st_b)


HALO = 8


def _halo_specs(s, ts, width, col):
    last = s // HALO - 1
    per = ts // HALO
    prev = pl.BlockSpec((HALO, width), lambda i: (jnp.maximum(i * per - 1, 0), col))
    nxt = pl.BlockSpec((HALO, width), lambda i: (jnp.minimum((i + 1) * per, last), col))
    return prev, nxt


def _group_ones():
    r = lax.broadcasted_iota(jnp.int32, (CONV_WIDTH, CONV_WIDTH), 0) >> 6
    c = lax.broadcasted_iota(jnp.int32, (CONV_WIDTH, CONV_WIDTH), 1) >> 6
    return (r == c).astype(BF16)


def _conv_terms(cc_ext, cu_ext, cw, valid):
    n = cc_ext.shape[0]
    hc = jnp.where(valid, cc_ext * cu_ext, 0.0)
    hc_prev = pltpu.roll(hc, 1, 0)
    hc_next = pltpu.roll(hc, n - 1, 0)
    conv = cw[0:1] * hc_prev + cw[1:2] * hc + cw[2:3] * hc_next
    return hc, hc_prev, hc_next, conv


def _ext(prev_ref, cur_ref, next_ref):
    return jnp.concatenate([prev_ref[...], cur_ref[...], next_ref[...]], axis=0)


def _valid_rows(ts, s):
    row = lax.broadcasted_iota(jnp.int32, (ts + 2 * HALO, 1), 0) + (pl.program_id(0) * ts - HALO)
    return (row >= 0) & (row < s)


def _head_norm(o, gn):
    out = []
    for h in range(GLA_HEADS):
        oh = o[:, GLA_DV * h : GLA_DV * (h + 1)]
        r = lax.rsqrt(jnp.mean(oh * oh, axis=-1, keepdims=True) + EPS)
        out.append((oh * r, r))
    return out


def _mix_fwd(z, o_f, o_b, conv_w, conv_norm, gla_norm, *, name):
    s = z.shape[0]
    ts = _rows(s)
    cprev, cnext = _halo_specs(s, ts, CONV_WIDTH, 1)
    uprev, unext = _halo_specs(s, ts, CONV_WIDTH, 2)

    def body(cb_ref, cc_ref, cu_ref, ccp_ref, ccn_ref, cup_ref, cun_ref, g_ref, of_ref, ob_ref, cw_ref, cn_ref, gn_ref, y_ref):
        valid = _valid_rows(ts, s)
        _, _, _, conv = _conv_terms(_ext(ccp_ref, cc_ref, ccn_ref), _ext(cup_ref, cu_ref, cun_ref), cw_ref[...], valid)
        yc = cb_ref[...] * conv[HALO : HALO + ts]
        ms = _dot_split(yc * yc, _group_ones()) * (1.0 / CONV_GROUP)
        y_conv = yc * lax.rsqrt(ms + EPS) * cn_ref[...]
        gate = g_ref[...]
        silu = gate * _sigmoid(gate)
        gn = gn_ref[...]
        y_gla = jnp.concatenate([oh * gn for oh, _ in _head_norm(of_ref[...] + ob_ref[...], gn)], axis=1) * silu
        y_ref[...] = jnp.concatenate([y_conv, y_gla], axis=1).astype(y_ref.dtype)

    col = lambda c, w=CONV_WIDTH: pl.BlockSpec((ts, w), lambda i: (i, c))
    return pl.pallas_call(
        body,
        name=name,
        grid=(s // ts,),
        in_specs=[col(0), col(1), col(2), cprev, cnext, uprev, unext, col(3), col(0), col(0),
                  pl.BlockSpec((CONV_K, CONV_WIDTH), lambda i: (0, 0)), pl.BlockSpec((1, CONV_WIDTH), lambda i: (0, 0)),
                  pl.BlockSpec((1, GLA_DV), lambda i: (0, 0))],
        out_specs=pl.BlockSpec((ts, D_MODEL), lambda i: (i, 0)),
        out_shape=jax.ShapeDtypeStruct((s, D_MODEL), _CD),
        compiler_params=_cp(("parallel",)),
    )(z, z, z, z, z, z, z, z, o_f, o_b, conv_w, conv_norm, gla_norm)


def _mix_bwd(z, o_f, o_b, dy, conv_w, conv_norm, gla_norm, *, name):
    s = z.shape[0]
    ts = _rows(s)
    halos = [_halo_specs(s, ts, CONV_WIDTH, c) for c in (0, 1, 2)]
    dprev, dnext = _halo_specs(s, ts, CONV_WIDTH, 0)

    def body(cb_ref, cc_ref, cu_ref, cbp_ref, cbn_ref, ccp_ref, ccn_ref, cup_ref, cun_ref, g_ref, of_ref, ob_ref,
             dyc_ref, dyg_ref, dyp_ref, dyn_ref, cw_ref, cn_ref, gn_ref, dza_ref, do_ref, dcw_ref, dcn_ref, dgn_ref):
        n = ts + 2 * HALO
        valid = _valid_rows(ts, s)
        cw = cw_ref[...]
        cn = cn_ref[...]
        ones = _group_ones()
        cb = _ext(cbp_ref, cb_ref, cbn_ref)
        cc = _ext(ccp_ref, cc_ref, ccn_ref)
        cu = _ext(cup_ref, cu_ref, cun_ref)
        dy = _ext(dyp_ref, dyc_ref, dyn_ref)
        hc, hc_prev, hc_next, conv = _conv_terms(cc, cu, cw, valid)
        yc = cb * conv
        r = lax.rsqrt(_dot_split(yc * yc, ones) * (1.0 / CONV_GROUP) + EPS)
        yh = yc * r
        dyh = dy * cn
        dyc = r * (dyh - yh * (_dot_split(dyh * yh, ones) * (1.0 / CONV_GROUP)))
        dconv = jnp.where(valid, dyc * cb, 0.0)
        dhc = cw[0:1] * pltpu.roll(dconv, n - 1, 0) + cw[1:2] * dconv + cw[2:3] * pltpu.roll(dconv, 1, 0)
        mid = lambda a: a[HALO : HALO + ts]
        dza_ref[:, 0 : 3 * CONV_WIDTH] = jnp.concatenate([mid(dyc * conv), mid(dhc * cu), mid(dhc * cc)], axis=1).astype(dza_ref.dtype)
        dconv_m = mid(dconv)
        colsum = lambda a: jnp.sum(a, axis=0, keepdims=True)
        dcw = jnp.concatenate([colsum(dconv_m * mid(hc_prev)), colsum(dconv_m * mid(hc)), colsum(dconv_m * mid(hc_next))], axis=0)
        dcn = colsum(mid(dy * yh))

        gate = g_ref[...]
        sg = _sigmoid(gate)
        silu = gate * sg
        gn = gn_ref[...]
        dyg = dyg_ref[...]
        don = dyg * silu
        heads = _head_norm(of_ref[...] + ob_ref[...], gn)
        on = jnp.concatenate([oh * gn for oh, _ in heads], axis=1)
        dza_ref[:, 3 * CONV_WIDTH : ZA_COLS] = (dyg * on * (sg * (1.0 + gate * (1.0 - sg)))).astype(dza_ref.dtype)
        dgn = jnp.zeros((1, GLA_DV), F32)
        dos = []
        for h, (oh, rh) in enumerate(heads):
            donh = don[:, GLA_DV * h : GLA_DV * (h + 1)]
            dgn = dgn + colsum(donh * oh)
            doh = donh * gn
            dos.append(rh * (doh - oh * jnp.mean(doh * oh, axis=-1, keepdims=True)))
        do_ref[...] = jnp.concatenate(dos, axis=1)

        first = pl.program_id(0) == 0

        @pl.when(first)
        def _():
            dcw_ref[...] = dcw
            dcn_ref[...] = dcn
            dgn_ref[...] = dgn

        @pl.when(jnp.logical_not(first))
        def _():
            dcw_ref[...] += dcw
            dcn_ref[...] += dcn
            dgn_ref[...] += dgn

    col = lambda c, w=CONV_WIDTH: pl.BlockSpec((ts, w), lambda i: (i, c))
    cw_spec = pl.BlockSpec((CONV_K, CONV_WIDTH), lambda i: (0, 0))
    cn_spec = pl.BlockSpec((1, CONV_WIDTH), lambda i: (0, 0))
    gn_spec = pl.BlockSpec((1, GLA_DV), lambda i: (0, 0))
    return pl.pallas_call(
        body,
        name=name,
        grid=(s // ts,),
        in_specs=[col(0), col(1), col(2), halos[0][0], halos[0][1], halos[1][0], halos[1][1], halos[2][0], halos[2][1],
                  col(3), col(0), col(0), col(0), col(1), dprev, dnext, cw_spec, cn_spec, gn_spec],
        out_specs=[pl.BlockSpec((ts, ZA_COLS), lambda i: (i, 0)), col(0), cw_spec, cn_spec, gn_spec],
        out_shape=[
            jax.ShapeDtypeStruct((s, ZA_COLS), _CD),
            jax.ShapeDtypeStruct((s, GLA_V_TOTAL), F32),
            jax.ShapeDtypeStruct((CONV_K, CONV_WIDTH), F32),
            jax.ShapeDtypeStruct((1, CONV_WIDTH), F32),
            jax.ShapeDtypeStruct((1, GLA_DV), F32),
        ],
        compiler_params=_cp(("arbitrary",)),
    )(z, z, z, z, z, z, z, z, z, z, o_f, o_b, dy, dy, dy, dy, conv_w, conv_norm, gla_norm)


def _xa_probs(q_ref, kv_ref, h):
    qh = q_ref[:, XA_HEAD_DIM * h : XA_HEAD_DIM * (h + 1)]
    kh = kv_ref[:, XA_HEAD_DIM * h : XA_HEAD_DIM * (h + 1)]
    vh = kv_ref[:, D_MODEL + XA_HEAD_DIM * h : D_MODEL + XA_HEAD_DIM * (h + 1)]
    sc = _dot_nt(qh, kh) * (XA_HEAD_DIM**-0.5)
    e = jnp.exp(sc - jnp.max(sc, axis=-1, keepdims=True))
    return qh, kh, vh, e / jnp.sum(e, axis=-1, keepdims=True)


def _xattn_fwd(qx, kv, *, name):
    s = qx.shape[0]
    ts = _rows(s)

    def body(q_ref, kv_ref, o_ref):
        outs = []
        for h in range(XA_HEADS):
            _, _, vh, p = _xa_probs(q_ref, kv_ref, h)
            outs.append(_dot(p, vh))
        o_ref[...] = jnp.concatenate(outs, axis=1).astype(o_ref.dtype)

    return pl.pallas_call(
        body,
        name=name,
        grid=(s // ts,),
        in_specs=[pl.BlockSpec((ts, D_MODEL), lambda i: (i, 0)), pl.BlockSpec((N_MEM, 2 * D_MODEL), lambda i: (0, 0))],
        out_specs=pl.BlockSpec((ts, D_MODEL), lambda i: (i, 0)),
        out_shape=jax.ShapeDtypeStruct((s, D_MODEL), _CD),
        compiler_params=_cp(("parallel",)),
    )(qx, kv)


def _xattn_bwd(qx, kv, dox, *, name):
    s = qx.shape[0]
    ts = _rows(s)

    def body(q_ref, kv_ref, do_ref, dq_ref, dkv_ref):
        dqs, dks, dvs = [], [], []
        for h in range(XA_HEADS):
            qh, kh, vh, p = _xa_probs(q_ref, kv_ref, h)
            doh = do_ref[:, XA_HEAD_DIM * h : XA_HEAD_DIM * (h + 1)]
            dp = _dot_nt(doh, vh)
            ds = p * (dp - jnp.sum(dp * p, axis=-1, keepdims=True)) * (XA_HEAD_DIM**-0.5)
            dqs.append(_dot(ds, kh))
            dks.append(_dot_tn(ds, qh))
            dvs.append(_dot_tn(p, doh))
        dq_ref[...] = jnp.concatenate(dqs, axis=1).astype(dq_ref.dtype)
        dkv = jnp.concatenate(dks + dvs, axis=1)

        @pl.when(pl.program_id(0) == 0)
        def _():
            dkv_ref[...] = dkv

        @pl.when(pl.program_id(0) > 0)
        def _():
            dkv_ref[...] += dkv

    tile = pl.BlockSpec((ts, D_MODEL), lambda i: (i, 0))
    kv_spec = pl.BlockSpec((N_MEM, 2 * D_MODEL), lambda i: (0, 0))
    return pl.pallas_call(
        body,
        name=name,
        grid=(s // ts,),
        in_specs=[tile, kv_spec, tile],
        out_specs=[tile, kv_spec],
        out_shape=[jax.ShapeDtypeStruct((s, D_MODEL), _CD), jax.ShapeDtypeStruct((N_MEM, 2 * D_MODEL), F32)],
        compiler_params=_cp(("arbitrary",)),
    )(qx, kv, dox)


def _adamw_math(w, g, m, v):
    m = ADAM_B1 * m + (1.0 - ADAM_B1) * g
    v = ADAM_B2 * v + (1.0 - ADAM_B2) * (g * g)
    m_hat = m / (1.0 - ADAM_B1**ADAM_STEP)
    v_hat = v / (1.0 - ADAM_B2**ADAM_STEP)
    delta = -ADAM_LR * (m_hat / (jnp.sqrt(v_hat) + ADAM_EPS) + ADAM_WD * w)
    return delta, m, v


def _adamw(w, m, v, shard_rows, off, *, transposed, name):
    r, c = w.shape
    by_columns = r % 256 != 0
    tr = 256
    if by_columns:
        assert not transposed and off == 0
        g_spec = tile = pl.BlockSpec((r, tr), lambda i: (0, i))
    else:
        g_spec = pl.BlockSpec((c, tr), lambda i: (off // c, i)) if transposed else pl.BlockSpec((tr, c), lambda i: (off // tr + i, 0))
        tile = pl.BlockSpec((tr, c), lambda i: (i, 0))

    def body(w_ref, g_ref, m_ref, v_ref, go_ref, d_ref, nm_ref, nv_ref):
        g = g_ref[...].T if transposed else g_ref[...]
        go_ref[...] = g
        d_ref[...], nm_ref[...], nv_ref[...] = _adamw_math(w_ref[...], g, m_ref[...], v_ref[...])

    return pl.pallas_call(
        body,
        name=name,
        grid=((c if by_columns else r) // tr,),
        in_specs=[tile, g_spec, tile, tile],
        out_specs=[tile] * 4,
        out_shape=[jax.ShapeDtypeStruct((r, c), F32)] * 4,
        compiler_params=_cp(("parallel",)),
    )(w, shard_rows, m, v)


def _adamw_small(groups, *, name):
    n = len(groups)

    def body(*refs):
        ins, outs = refs[: 4 * n], refs[4 * n :]
        for i in range(n):
            w_ref, g_ref, m_ref, v_ref = ins[4 * i : 4 * i + 4]
            outs[3 * i][...], outs[3 * i + 1][...], outs[3 * i + 2][...] = _adamw_math(w_ref[...], g_ref[...], m_ref[...], v_ref[...])

    flat = [a for grp in groups for a in grp]
    vm = pl.BlockSpec(memory_space=pltpu.VMEM)
    res = pl.pallas_call(
        body,
        name=name,
        in_specs=[vm] * (4 * n),
        out_specs=[vm] * (3 * n),
        out_shape=[jax.ShapeDtypeStruct(grp[0].shape, F32) for grp in groups for _ in range(3)],
        compiler_params=_cp(),
    )(*flat)
    return [tuple(res[3 * i : 3 * i + 3]) for i in range(n)]


def _place():
    return lax.axis_index("x"), lax.axis_index("y"), lax.axis_index("c")


def _rel_chip(x, y, k):
    return (1 - x if k & 2 else x), (1 - y if k & 1 else y)


def _half(c, rh):
    return pl.ds(pl.multiple_of(c * rh, 16), rh)


def _gather_weights(pack):
    r, w = pack.shape
    rh = r // 2

    def body(p_ref, q_ref, send_sems, recv_sems):
        x, y, c = _place()
        j = 2 * x + y
        rows = _half(c, rh)

        def to_chip(k):
            cx, cy = _rel_chip(x, y, k)
            return pltpu.make_async_remote_copy(
                src_ref=p_ref.at[rows], dst_ref=q_ref.at[j, rows], send_sem=send_sems.at[k - 1], recv_sem=recv_sems.at[k - 1],
                device_id=(cx, cy, c), device_id_type=MESH)

        def to_sibling(k):
            cx, cy = _rel_chip(x, y, k)
            slot = q_ref.at[2 * cx + cy, rows]
            return pltpu.make_async_remote_copy(
                src_ref=slot, dst_ref=slot, send_sem=send_sems.at[2 + k], recv_sem=recv_sems.at[2 + k],
                device_id=(x, y, 1 - c), device_id_type=MESH)

        first = [to_chip(k) for k in range(1, N_CHIPS)]
        passed = [to_sibling(k) for k in range(1, N_CHIPS)]
        own = pltpu.make_async_remote_copy(
            src_ref=p_ref, dst_ref=q_ref.at[j], send_sem=send_sems.at[6], recv_sem=recv_sems.at[6],
            device_id=(x, y, 1 - c), device_id_type=MESH)
        for cp in first:
            cp.start()
        own.start()
        for cp, fw in zip(first, passed):
            cp.wait_recv()
            fw.start()
        for fw in passed:
            fw.wait_recv()
        own.wait_recv()
        for cp in first + passed + [own]:
            cp.wait_send()

    return pl.pallas_call(
        body,
        name="gather_weights",
        in_specs=[ANY],
        out_specs=ANY,
        out_shape=jax.ShapeDtypeStruct((N_CHIPS, r, w), pack.dtype),
        scratch_shapes=[pltpu.SemaphoreType.DMA((7,)), pltpu.SemaphoreType.DMA((7,))],
        compiler_params=pltpu.CompilerParams(has_side_effects=True),
    )(pack)


def _swap_halves(g):
    n, r, w = g.shape
    rh = r // 2

    def body(g_ref, o_ref, send_sem, recv_sem):
        x, y, c = _place()
        cp = pltpu.make_async_remote_copy(
            src_ref=g_ref.at[:, _half(1 - c, rh)], dst_ref=o_ref, send_sem=send_sem, recv_sem=recv_sem,
            device_id=(x, y, 1 - c), device_id_type=MESH)
        cp.start()
        cp.wait()

    return pl.pallas_call(
        body,
        name="grads_to_sibling",
        in_specs=[ANY],
        out_specs=ANY,
        out_shape=jax.ShapeDtypeStruct((n, rh, w), g.dtype),
        scratch_shapes=[pltpu.SemaphoreType.DMA, pltpu.SemaphoreType.DMA],
        compiler_params=pltpu.CompilerParams(has_side_effects=True),
    )(g)


def _chip_sums(g, got, where):
    n, r, w = g.shape
    rh = r // 2
    nt = rh // PACK_TILE

    def body(where_ref, g_ref, got_ref, o_ref):
        o_ref[...] = (g_ref[...] + got_ref[...]).astype(o_ref.dtype)

    return pl.pallas_call(
        body,
        name="chip_sums",
        grid_spec=pltpu.PrefetchScalarGridSpec(
            num_scalar_prefetch=1,
            grid=(n, nt),
            in_specs=[pl.BlockSpec((1, PACK_TILE, w), lambda a, i, wh: (a, wh[0] * nt + i, 0)),
                      pl.BlockSpec((1, PACK_TILE, w), lambda a, i, wh: (a, i, 0))],
            out_specs=pl.BlockSpec((1, PACK_TILE, w), lambda a, i, wh: (a, i, 0)),
        ),
        out_shape=jax.ShapeDtypeStruct((n, rh, w), _TD),
        compiler_params=_cp(("parallel", "parallel")),
    )(where, g, got)


def _exchange_chip_sums(h):
    n, rh, w = h.shape

    def body(h_ref, o_ref, send_sems, recv_sems):
        x, y, c = _place()
        j = 2 * x + y
        copies = []
        for k in range(1, N_CHIPS):
            cx, cy = _rel_chip(x, y, k)
            copies.append(pltpu.make_async_remote_copy(
                src_ref=h_ref.at[2 * cx + cy], dst_ref=o_ref.at[k - 1], send_sem=send_sems.at[k - 1], recv_sem=recv_sems.at[k - 1],
                device_id=(cx, cy, c), device_id_type=MESH))
        for cp in copies:
            cp.start()
        for cp in copies:
            cp.wait()

    return pl.pallas_call(
        body,
        name="chip_sums_exchange",
        in_specs=[ANY],
        out_specs=ANY,
        out_shape=jax.ShapeDtypeStruct((N_CHIPS - 1, rh, w), h.dtype),
        scratch_shapes=[pltpu.SemaphoreType.DMA((3,)), pltpu.SemaphoreType.DMA((3,))],
        compiler_params=pltpu.CompilerParams(has_side_effects=True),
    )(h)


def _shard_sum(g, got, others, where):
    n, r, w = g.shape
    rh = r // 2
    nt = rh // PACK_TILE

    def body(where_ref, g_ref, got_ref, oth_ref, o_ref):
        acc = g_ref[0] + got_ref[0]
        for k in range(N_CHIPS - 1):
            acc = acc + oth_ref[k].astype(F32)
        o_ref[...] = acc

    return pl.pallas_call(
        body,
        name="shard_sum",
        grid_spec=pltpu.PrefetchScalarGridSpec(
            num_scalar_prefetch=1,
            grid=(nt,),
            in_specs=[pl.BlockSpec((1, PACK_TILE, w), lambda i, wh: (wh[1], wh[0] * nt + i, 0)),
                      pl.BlockSpec((1, PACK_TILE, w), lambda i, wh: (wh[1], i, 0)),
                      pl.BlockSpec((N_CHIPS - 1, PACK_TILE, w), lambda i, wh: (0, i, 0))],
            out_specs=pl.BlockSpec((PACK_TILE, w), lambda i, wh: (i, 0)),
        ),
        out_shape=jax.ShapeDtypeStruct((rh, w), F32),
        compiler_params=_cp(("parallel",)),
    )(where, g, got, others)


def _join_halves(e):
    rh, w = e.shape

    def body(e_ref, o_ref, send_sem, recv_sem, local_sem):
        x, y, c = _place()
        rows = _half(c, rh)
        mine = pltpu.make_async_copy(e_ref, o_ref.at[rows], local_sem)
        mine.start()
        cp = pltpu.make_async_remote_copy(
            src_ref=e_ref, dst_ref=o_ref.at[rows], send_sem=send_sem, recv_sem=recv_sem, device_id=(x, y, 1 - c), device_id_type=MESH)
        cp.start()
        cp.wait()
        mine.wait()

    return pl.pallas_call(
        body,
        name="shard_to_sibling",
        in_specs=[ANY],
        out_specs=ANY,
        out_shape=jax.ShapeDtypeStruct((2 * rh, w), e.dtype),
        scratch_shapes=[pltpu.SemaphoreType.DMA, pltpu.SemaphoreType.DMA, pltpu.SemaphoreType.DMA],
        compiler_params=pltpu.CompilerParams(has_side_effects=True),
    )(e)


HBM = pl.BlockSpec(memory_space=pltpu.HBM)
SEM = pl.BlockSpec(memory_space=pltpu.SEMAPHORE)
EFFECT = pltpu.SideEffectType.DATAFLOW_SIDE_EFFECTING


def _in_hbm(a):
    return pltpu.with_memory_space_constraint(a, pltpu.HBM)


def _gather_copies(p_ref, land_ref, send_sems, recv_sems):
    rh = p_ref.shape[0] // 2
    x, y, c = _place()
    rows = _half(c, rh)
    copies = []
    for k in range(1, N_CHIPS):
        cx, cy = _rel_chip(x, y, k)
        copies.append(pltpu.make_async_remote_copy(
            src_ref=p_ref.at[rows], dst_ref=land_ref.at[2 * x + y, rows], send_sem=send_sems.at[k - 1], recv_sem=recv_sems.at[k - 1],
            device_id=(cx, cy, c), device_id_type=MESH))
    copies.append(pltpu.make_async_remote_copy(
        src_ref=p_ref, dst_ref=land_ref.at[2 * x + y], send_sem=send_sems.at[N_CHIPS - 1], recv_sem=recv_sems.at[N_CHIPS - 1],
        device_id=(x, y, 1 - c), device_id_type=MESH))
    return copies


def _gather_start(pack, after, *, name):
    r, w = pack.shape

    def body(p_ref, land_ref, after_ref, send_sems, recv_sems, p_thru, land_thru, token):
        for cp in _gather_copies(p_ref, land_ref, send_sems, recv_sems):
            cp.start()
        token[...] = jnp.zeros_like(token)

    return pl.pallas_call(
        body,
        name=name,
        out_shape=(pltpu.SemaphoreType.DMA((N_CHIPS,)), pltpu.SemaphoreType.DMA((N_CHIPS,)), pltpu.HBM((r, w), pack.dtype),
                   pltpu.HBM((N_CHIPS, r, w), pack.dtype), jax.ShapeDtypeStruct((8, 128), F32)),
        in_specs=(HBM, HBM, ANY),
        out_specs=(SEM, SEM, HBM, HBM, pl.BlockSpec(memory_space=pltpu.VMEM)),
        input_output_aliases={0: 2, 1: 3},
        compiler_params=pltpu.CompilerParams(has_side_effects=EFFECT),
    )(_in_hbm(pack), _in_hbm(lax.empty((N_CHIPS, r, w), pack.dtype)), after)


def _gather_wait(send_sems, recv_sems, pack, land, after, *, name):
    def body(p_ref, land_ref, send_sems, recv_sems, after_ref, p_out, land_out):
        for cp in _gather_copies(p_ref, land_ref, send_sems, recv_sems):
            cp.wait_send()
            cp.wait_recv()

    return pl.pallas_call(
        body,
        name=name,
        out_shape=(pltpu.HBM(pack.shape, pack.dtype), pltpu.HBM(land.shape, land.dtype)),
        in_specs=(HBM, HBM, SEM, SEM, ANY),
        out_specs=(HBM, HBM),
        input_output_aliases={0: 0, 1: 1},
        compiler_params=pltpu.CompilerParams(has_side_effects=EFFECT),
    )(pack, land, send_sems, recv_sems, after)


def _gather_spread(land, *, name):
    n, r, w = land.shape
    rh = r // 2

    def body(land_ref, o_ref, send_sems, recv_sems):
        x, y, c = _place()
        rows = _half(c, rh)
        copies = []
        for k in range(1, N_CHIPS):
            cx, cy = _rel_chip(x, y, k)
            copies.append(pltpu.make_async_remote_copy(
                src_ref=land_ref.at[2 * cx + cy, rows], dst_ref=o_ref.at[2 * cx + cy, rows], send_sem=send_sems.at[k - 1],
                recv_sem=recv_sems.at[k - 1], device_id=(x, y, 1 - c), device_id_type=MESH))
        for cp in copies:
            cp.start()
        for cp in copies:
            cp.wait()

    return pl.pallas_call(
        body,
        name=name,
        in_specs=[ANY],
        out_specs=ANY,
        out_shape=jax.ShapeDtypeStruct(land.shape, land.dtype),
        input_output_aliases={0: 0},
        scratch_shapes=[pltpu.SemaphoreType.DMA((N_CHIPS - 1,)), pltpu.SemaphoreType.DMA((N_CHIPS - 1,))],
        compiler_params=pltpu.CompilerParams(has_side_effects=True),
    )(land)


N_PARTS = 2 * (N_CHIPS - 1)


def _scatter_copies(lo_ref, g_ref, land_lo_ref, land_f_ref, send_sems, recv_sems, starting):
    rh = g_ref.shape[1] // 2
    x, y, c = _place()
    copies = []
    for k in range(1, N_CHIPS):
        cx, cy = _rel_chip(x, y, k)
        for i in range(2):
            part = 2 * (k - 1) + (c if starting else i)
            copies.append(pltpu.make_async_remote_copy(
                src_ref=lo_ref.at[2 * cx + cy, pl.ds(i * rh, rh)], dst_ref=land_lo_ref.at[part],
                send_sem=send_sems.at[2 * (k - 1) + i], recv_sem=recv_sems.at[part], device_id=(cx, cy, i), device_id_type=MESH))
    copies.append(pltpu.make_async_remote_copy(
        src_ref=g_ref.at[2 * x + y, _half(1 - c, rh)], dst_ref=land_f_ref, send_sem=send_sems.at[N_PARTS], recv_sem=recv_sems.at[N_PARTS],
        device_id=(x, y, 1 - c), device_id_type=MESH))
    return copies


def _scatter_start(g_lo, g, *, name):
    n, r, w = g.shape
    rh = r // 2

    def body(lo_ref, g_ref, land_lo_ref, land_f_ref, send_sems, recv_sems, lo_thru, g_thru, land_lo_thru, land_f_thru, token):
        for cp in _scatter_copies(lo_ref, g_ref, land_lo_ref, land_f_ref, send_sems, recv_sems, True):
            cp.start()
        token[...] = jnp.zeros_like(token)

    return pl.pallas_call(
        body,
        name=name,
        out_shape=(pltpu.SemaphoreType.DMA((N_PARTS + 1,)), pltpu.SemaphoreType.DMA((N_PARTS + 1,)), pltpu.HBM(g_lo.shape, g_lo.dtype),
                   pltpu.HBM(g.shape, g.dtype), pltpu.HBM((N_PARTS, rh, w), g_lo.dtype), pltpu.HBM((rh, w), g.dtype),
                   jax.ShapeDtypeStruct((8, 128), F32)),
        in_specs=(HBM, HBM, HBM, HBM),
        out_specs=(SEM, SEM, HBM, HBM, HBM, HBM, pl.BlockSpec(memory_space=pltpu.VMEM)),
        input_output_aliases={0: 2, 1: 3, 2: 4, 3: 5},
        compiler_params=pltpu.CompilerParams(has_side_effects=EFFECT),
    )(_in_hbm(g_lo), _in_hbm(g), _in_hbm(lax.empty((N_PARTS, rh, w), g_lo.dtype)), _in_hbm(lax.empty((rh, w), g.dtype)))


def _scatter_wait(send_sems, recv_sems, g_lo, g, land_lo, land_f, after, *, name):
    def body(lo_ref, g_ref, land_lo_ref, land_f_ref, send_sems, recv_sems, after_ref, o0, o1, o2, o3):
        for cp in _scatter_copies(lo_ref, g_ref, land_lo_ref, land_f_ref, send_sems, recv_sems, False):
            cp.wait_send()
            cp.wait_recv()

    arrays = (g_lo, g, land_lo, land_f)
    return pl.pallas_call(
        body,
        name=name,
        out_shape=tuple(pltpu.HBM(a.shape, a.dtype) for a in arrays),
        in_specs=(HBM, HBM, HBM, HBM, SEM, SEM, ANY),
        out_specs=(HBM, HBM, HBM, HBM),
        input_output_aliases={0: 0, 1: 1, 2: 2, 3: 3},
        compiler_params=pltpu.CompilerParams(has_side_effects=EFFECT),
    )(*arrays, send_sems, recv_sems, after)


def _scatter_sum(g, land_lo, land_f, where, *, name):
    n, r, w = g.shape
    rh = r // 2
    tr = _pick(rh, (256, 160, 80))
    nt = rh // tr

    def body(where_ref, g_ref, f_ref, lo_ref, o_ref):
        acc = g_ref[0] + f_ref[...]
        for part in range(N_PARTS):
            acc = acc + lo_ref[part].astype(F32)
        o_ref[...] = acc

    return pl.pallas_call(
        body,
        name=name,
        grid_spec=pltpu.PrefetchScalarGridSpec(
            num_scalar_prefetch=1,
            grid=(nt,),
            in_specs=[pl.BlockSpec((1, tr, w), lambda i, wh: (wh[1], wh[0] * nt + i, 0)),
                      pl.BlockSpec((tr, w), lambda i, wh: (i, 0)),
                      pl.BlockSpec((N_PARTS, tr, w), lambda i, wh: (0, i, 0))],
            out_specs=pl.BlockSpec((tr, w), lambda i, wh: (wh[0] * nt + i, 0)),
        ),
        out_shape=jax.ShapeDtypeStruct((r, w), F32),
        compiler_params=_cp(("parallel",)),
    )(where, g, land_f, land_lo)


def _swap_all(shards, *, name):
    n = len(shards)

    def body(*refs):
        ins, outs = refs[:n], refs[n : 2 * n]
        send_sems, recv_sems = refs[2 * n :]
        x, y, c = _place()
        copies = []
        for i, (e_ref, o_ref) in enumerate(zip(ins, outs)):
            rows = _half(c, e_ref.shape[0] // 2)
            copies.append(pltpu.make_async_remote_copy(src_ref=e_ref.at[rows], dst_ref=o_ref.at[rows], send_sem=send_sems.at[i],
                                                       recv_sem=recv_sems.at[i], device_id=(x, y, 1 - c), device_id_type=MESH))
        for cp in copies:
            cp.start()
        for cp in copies:
            cp.wait()

    return pl.pallas_call(
        body,
        name=name,
        in_specs=[ANY] * n,
        out_specs=[ANY] * n,
        out_shape=[jax.ShapeDtypeStruct(e.shape, e.dtype) for e in shards],
        input_output_aliases={i: i for i in range(n)},
        scratch_shapes=[pltpu.SemaphoreType.DMA((n,)), pltpu.SemaphoreType.DMA((n,))],
        compiler_params=pltpu.CompilerParams(has_side_effects=True),
    )(*shards)


def _sum_small(small, after):
    n_dev = 8

    def body(s_ref, after_ref, o_ref, all_ref, send_sems, recv_sems):
        x, y, c = _place()
        me = 4 * x + 2 * y + c
        all_ref[me] = s_ref[...]
        copies = []
        for k in range(1, n_dev):
            cx, cy = _rel_chip(x, y, k >> 1)
            cc = 1 - c if k & 1 else c
            copies.append(pltpu.make_async_remote_copy(
                src_ref=s_ref, dst_ref=all_ref.at[me], send_sem=send_sems.at[k - 1], recv_sem=recv_sems.at[k - 1],
                device_id=(cx, cy, cc), device_id_type=MESH))
        for cp in copies:
            cp.start()
        for cp in copies:
            cp.wait()
        acc = all_ref[0]
        for a in range(1, n_dev):
            acc = acc + all_ref[a]
        o_ref[...] = acc

    vm = pl.BlockSpec(memory_space=pltpu.VMEM)
    return pl.pallas_call(
        body,
        name="sum_small",
        in_specs=[vm, ANY],
        out_specs=vm,
        out_shape=jax.ShapeDtypeStruct(small.shape, F32),
        scratch_shapes=[pltpu.VMEM((n_dev,) + small.shape, F32), pltpu.SemaphoreType.DMA((n_dev - 1,)), pltpu.SemaphoreType.DMA((n_dev - 1,))],
        compiler_params=pltpu.CompilerParams(has_side_effects=True),
    )(small, after)


MATS = {"w_in": (776, True), "w_out": (256, False), "w_xq": (256, False), "w_xkv": (512, True), "w_xo": (256, False),
        "w_up": (1024, True), "w_down": (1024, False)}
GATHER_FIRST = ("w_in",)
GATHER_REST = ("w_out", "w_xq", "w_xkv", "w_xo", "w_up", "w_down")
GRAD_GROUPS = (("w_up", "w_down"), ("w_out", "w_xq", "w_xkv", "w_xo"), ("w_in",))


def _group_rows(names):
    n = sum(MATS[name][0] for name in names)
    return n + (-n) % 32


def _pack(pieces, rows):
    p = jnp.concatenate(pieces, axis=0) if len(pieces) > 1 else pieces[0]
    return jnp.pad(p, ((0, rows - p.shape[0]), (0, 0))) if rows > p.shape[0] else p


def _unpack(rows, names):
    out, off = {}, 0
    for name in names:
        out[name] = rows[off : off + MATS[name][0]]
        off += MATS[name][0]
    return out


SMALL = (
    ("mix_norm", 1024), ("conv_norm", 512), ("b_af", 256), ("b_ab", 256), ("gla_norm", 128), ("xa_norm", 1024), ("mem_norm", 1024),
    ("mlp_norm", 1024), ("final_norm", 1024), ("conv_w", 1536), ("w_af", 4096), ("w_ab", 4096), ("loss", 128),
)


def kernel(x, mem, mix_norm, w_in, conv_w, conv_norm, w_af, b_af, w_ab, b_ab, gla_norm, w_out, xa_norm, mem_norm, w_xq, w_xkv, w_xo, mlp_norm, w_up, w_down, final_norm, loss_target, m_mix_norm, m_w_in, m_conv_w, m_conv_norm, m_w_af, m_b_af, m_w_ab, m_b_ab, m_gla_norm, m_w_out, m_xa_norm, m_mem_norm, m_w_xq, m_w_xkv, m_w_xo, m_mlp_norm, m_w_up, m_w_down, m_final_norm, v_mix_norm, v_w_in, v_conv_w, v_conv_norm, v_w_af, v_b_af, v_w_ab, v_b_ab, v_gla_norm, v_w_out, v_xa_norm, v_mem_norm, v_w_xq, v_w_xkv, v_w_xo, v_mlp_norm, v_w_up, v_w_down, v_final_norm):
    given = dict(locals())
    xi, yi, ci = _place()
    chip = 2 * xi + yi
    where = jnp.stack([ci, chip]).astype(jnp.int32)

    lo = {name: (given[name][0].T if MATS[name][1] else given[name][0]).astype(_CD) for name in MATS}
    pack_rest = _pack([lo[name] for name in GATHER_REST], _group_rows(GATHER_REST))
    pack_first = _pack([lo[name] for name in GATHER_FIRST], _group_rows(GATHER_FIRST))
    xs, mems, tgt = x[0], mem[0], loss_target[0]
    add_res = lambda acc, res: (acc + res,)
    behind = lambda gain, token: gain + token[0, 0]

    def placed(shard, full_shape, col):
        return lax.dynamic_update_slice(jnp.zeros(full_shape, F32), shard, (0, col)).reshape(-1, 128)

    sw = jnp.concatenate([
        placed(conv_w[0], (CONV_K, CONV_WIDTH), 128 * chip),
        placed(w_af[0], (GLA_LOWRANK, GLA_K_TOTAL), 64 * chip),
        placed(w_ab[0], (GLA_LOWRANK, GLA_K_TOTAL), 64 * chip),
    ], axis=0)
    sw = jnp.pad(sw, ((0, SMALL_ROWS - sw.shape[0]), (0, 0))) * (ci == 0).astype(F32)
    sw = _sum_small(sw, mix_norm)

    first_send, first_recv, pack_first, land_first, first_token = _gather_start(pack_first, sw, name="gather_first_start")
    rest_send, rest_recv, pack_rest, land_rest, rest_token = _gather_start(pack_rest, first_token, name="gather_rest_start")
    h1 = _rms_fwd(xs, behind(mix_norm, rest_token), name="norm_mix")
    pack_first, land_first = _gather_wait(first_send, first_recv, pack_first, land_first, h1, name="gather_first_wait")
    got_first = _gather_spread(land_first, name="gather_first_spread")

    def whole(got, off, rows):
        return got[:, off : off + rows].reshape(N_CHIPS * rows, D_MODEL)

    w_in_t = whole(got_first, 0, MATS["w_in"][0])
    w_za = jnp.concatenate([w_in_t[0:1536], w_in_t[2560:3072]], axis=0)
    w_zb = jnp.concatenate([w_in_t[1536:2560], w_in_t[3072:W_IN_COLS], jnp.zeros((ZB_COLS - 1056, D_MODEL), _CD)], axis=0)
    conv_w_full = sw[0:12].reshape(CONV_K, CONV_WIDTH)
    w_af_full = sw[12:44].reshape(GLA_LOWRANK, GLA_K_TOTAL)
    w_ab_full = sw[44:76].reshape(GLA_LOWRANK, GLA_K_TOTAL)
    waf_p = jnp.pad(w_af_full, ((0, 128 - GLA_LOWRANK), (0, 0))).astype(_CD)
    wab_p = jnp.pad(w_ab_full, ((GLA_LOWRANK, 128 - 2 * GLA_LOWRANK), (0, 0))).astype(_CD)

    z_b = _mm(h1, w_zb, mode="nt", name="proj_in_b", tn=ZB_COLS)
    z_a = _mm(h1, w_za, mode="nt", name="proj_in_a", tm=512, tn=ZA_COLS)
    b_f, b_b = _gate_fwd(z_b, waf_p, wab_p, b_af, b_ab, name="gates")
    o_f, st_f, o_b, st_b = _gla_fwd(z_b, b_f, b_b, name="gla_scan")
    y = _mix_fwd(z_a, o_f, o_b, conv_w_full, conv_norm, gla_norm, name="mix_out")
    pack_rest, land_rest = _gather_wait(rest_send, rest_recv, pack_rest, land_rest, y, name="gather_rest_wait")
    gathered = _gather_spread(land_rest, name="gather_rest_spread")
    wt, off = {}, 0
    for name in GATHER_REST:
        wt[name] = whole(gathered, off, MATS[name][0])
        off += MATS[name][0]
    x1, hx = _mm_rows(y, wt["w_out"], mode="nn", name="proj_out", rows=(xs,), vecs=(xa_norm,), out_rows=(F32, _CD), epilogue=_ep_residual_norm)
    qx = _mm(hx, wt["w_xq"], mode="nn", name="proj_xq", out_dtypes=(_CD,))
    hmem = _rms_fwd(mems, mem_norm, name="norm_mem")
    kv = _mm(hmem, wt["w_xkv"], mode="nt", name="proj_xkv", out_dtypes=(_CD,))
    ox = _xattn_fwd(qx, kv, name="xattn")
    x2, hm = _mm_rows(ox, wt["w_xo"], mode="nn", name="proj_xo", rows=(x1,), vecs=(mlp_norm,), out_rows=(F32, _CD), epilogue=_ep_residual_norm)
    act, relu_u = _mm(hm, wt["w_up"], mode="nt", name="mlp_up", out_dtypes=(_CD, _CD),
                      epilogue=lambda acc: (jnp.square(jnp.maximum(acc, 0.0)), jnp.maximum(acc, 0.0)))
    dx3, dx3_lo, loss_part, g_final_norm = _mm_rows(
        act, wt["w_down"], mode="nn", name="mlp_down", rows=(x2, tgt), vecs=(final_norm.reshape(1, D_MODEL),),
        out_rows=(F32, _CD), out_vecs=(128, D_MODEL), epilogue=_ep_loss)

    grads_t = {}

    def start_group(names, tag):
        rows = _group_rows(names)
        g = jnp.stack([_pack([grads_t[name][a * MATS[name][0] : (a + 1) * MATS[name][0]] for name in names], rows) for a in range(N_CHIPS)])
        return _scatter_start(g.astype(_TD), g, name="grads_" + tag + "_start")

    def finish_group(state, after, tag):
        send_sems, recv_sems, g_lo, g, land_lo, land_f, _ = state
        g_lo, g, land_lo, land_f = _scatter_wait(send_sems, recv_sems, g_lo, g, land_lo, land_f, after, name="grads_" + tag + "_wait")
        return _scatter_sum(g, land_lo, land_f, where, name="grads_" + tag + "_sum")

    def new_packs(names):
        shape = (N_CHIPS, _group_rows(names), D_MODEL)
        return lax.empty(shape, F32), lax.empty(shape, _TD)

    def grad_into(packs, names, which, a, b, name):
        off = sum(MATS[other][0] for other in names[: names.index(which)])
        return _mm_tn_into(a, b, packs, rows=MATS[which][0], off=off, name=name)

    du = _mm(dx3_lo, wt["w_down"], mode="nt", name="mlp_down_dx", out_dtypes=(_CD,), extras=(relu_u,),
             epilogue=lambda acc, rr: (acc * (2.0 * rr.astype(F32)),))
    packs = new_packs(GRAD_GROUPS[0])
    packs = grad_into(packs, GRAD_GROUPS[0], "w_down", act, dx3_lo, "mlp_down_dw")
    packs = grad_into(packs, GRAD_GROUPS[0], "w_up", du, hm, "mlp_up_dw")
    mlp_state = _scatter_start(packs[1], packs[0], name="grads_mlp_start")
    dx2, dx2_lo, g_mlp_norm = _mm_rows(
        du, wt["w_up"], mode="nn", name="mlp_up_dx", rows=(x2, dx3), vecs=(behind(mlp_norm, mlp_state[-1]),),
        out_rows=(F32, _CD), out_vecs=(D_MODEL,), epilogue=_ep_norm_bwd)
    dox = _mm(dx2_lo, wt["w_xo"], mode="nt", name="proj_xo_dx", out_dtypes=(_CD,))
    packs = new_packs(GRAD_GROUPS[1])
    packs = grad_into(packs, GRAD_GROUPS[1], "w_xo", ox, dx2_lo, "proj_xo_dw")
    dqx, dkv = _xattn_bwd(qx, kv, dox, name="xattn_bwd")
    packs = grad_into(packs, GRAD_GROUPS[1], "w_xq", hx, dqx, "proj_xq_dw")
    dx1, dx1_lo, g_xa_norm = _mm_rows(
        dqx, wt["w_xq"], mode="nt", name="proj_xq_dx", rows=(x1, dx2), vecs=(xa_norm,),
        out_rows=(F32, _CD), out_vecs=(D_MODEL,), epilogue=_ep_norm_bwd)
    dkv_lo = dkv.astype(_CD)
    packs = grad_into(packs, GRAD_GROUPS[1], "w_xkv", dkv_lo, hmem, "proj_xkv_dw")
    dhmem = _mm(dkv_lo, wt["w_xkv"], mode="nn", name="proj_xkv_dx")
    (g_mem_norm,) = _rms_bwd(mems, mem_norm, dhmem, name="norm_mem_bwd", want_dx=False, want_lo=False)
    dy = _mm(dx1_lo, wt["w_out"], mode="nt", name="proj_out_dx")
    packs = grad_into(packs, GRAD_GROUPS[1], "w_out", y, dx1_lo, "proj_out_dw")
    attn_state = _scatter_start(packs[1], packs[0], name="grads_attn_start")
    dz_a, do, g_conv_w, g_conv_norm, g_gla_norm = _mix_bwd(z_a, o_f, o_b, dy, conv_w_full, behind(conv_norm, attn_state[-1]), gla_norm, name="mix_out_bwd")
    dqkv_f, db_f, dqkv_b, db_b = _gla_bwd(z_b, b_f, b_b, do, st_f, st_b, name="gla_scan_bwd")
    dz_b, g_waf_p, g_wab_p, g_b_af, g_b_ab = _gate_bwd(z_b, waf_p, wab_p, b_af, b_ab, db_f, db_b, dqkv_f, dqkv_b, name="gates_bwd")
    g_za = _mm_tn(dz_a, h1, name="proj_in_a_dw")
    g_zb = _mm_tn(dz_b, h1, name="proj_in_b_dw")
    grads_t["w_in"] = jnp.concatenate([g_za[0:1536], g_zb[0:1024], g_za[1536:2048], g_zb[1024:1056]], axis=0)
    in_state = start_group(GRAD_GROUPS[2], "in")
    dh1_a = _mm(dz_a, w_za, mode="nn", name="proj_in_a_dx", tm=512, tk=ZA_COLS)
    grad_x, g_mix_norm = _mm_rows(
        dz_b, w_zb, mode="nn", name="proj_in_b_dx", rows=(xs, dx1, dh1_a), vecs=(behind(mix_norm, in_state[-1]),),
        out_rows=(F32,), out_vecs=(D_MODEL,), epilogue=_ep_norm_bwd)

    half_mlp = finish_group(mlp_state, grad_x, "mlp")
    half_attn = finish_group(attn_state, half_mlp, "attn")
    half_in = finish_group(in_state, half_attn, "in")
    shard_rows = {}
    for names, rows in zip(GRAD_GROUPS, _swap_all([half_mlp, half_attn, half_in], name="shards_to_sibling")):
        off = 0
        for name in names:
            shard_rows[name] = (rows, off)
            off += MATS[name][0]

    small_vals = dict(mix_norm=g_mix_norm, conv_norm=g_conv_norm, b_af=g_b_af, b_ab=g_b_ab, gla_norm=g_gla_norm, xa_norm=g_xa_norm,
                      mem_norm=g_mem_norm, mlp_norm=g_mlp_norm, final_norm=g_final_norm, conv_w=g_conv_w,
                      w_af=g_waf_p[0:GLA_LOWRANK], w_ab=g_wab_p[GLA_LOWRANK : 2 * GLA_LOWRANK], loss=loss_part)
    small = jnp.concatenate([small_vals[name].reshape(-1, 128) for name, _ in SMALL], axis=0)
    small = _sum_small(jnp.pad(small, ((0, SMALL_ROWS - small.shape[0]), (0, 0))), loss_part)
    g_small, off = {}, 0
    for name, n in SMALL:
        g_small[name] = small[off : off + n // 128]
        off += n // 128
    loss = g_small["loss"][0, 0]
    g_small["conv_w"] = lax.dynamic_slice(g_small["conv_w"].reshape(CONV_K, CONV_WIDTH), (0, 128 * chip), (CONV_K, 128))
    g_small["w_af"] = lax.dynamic_slice(g_small["w_af"].reshape(GLA_LOWRANK, GLA_K_TOTAL), (0, 64 * chip), (GLA_LOWRANK, 64))
    g_small["w_ab"] = lax.dynamic_slice(g_small["w_ab"].reshape(GLA_LOWRANK, GLA_K_TOTAL), (0, 64 * chip), (GLA_LOWRANK, 64))

    names = ["mix_norm", "w_in", "conv_w", "conv_norm", "w_af", "b_af", "w_ab", "b_ab", "gla_norm", "w_out", "xa_norm", "mem_norm",
             "w_xq", "w_xkv", "w_xo", "mlp_norm", "w_up", "w_down", "final_norm"]
    big_names = list(MATS)
    as2d = lambda a: a.reshape(1, -1) if a.ndim == 1 else a.reshape(a.shape[-2:])
    grads, deltas, new_m, new_v = {}, {}, {}, {}
    for name in big_names:
        rows, off = shard_rows[name]
        wmv = [as2d(given[name]), as2d(given["m_" + name]), as2d(given["v_" + name])]
        as_stored = name == "w_in"
        if as_stored:
            wmv = [a.T for a in wmv]
        res = _adamw(*wmv, rows, off, transposed=MATS[name][1] and not as_stored, name="adamw_" + name)
        grads[name], deltas[name], new_m[name], new_v[name] = [a.T for a in res] if as_stored else res
    small_names = [name for name in names if name not in big_names]
    groups = []
    for name in small_names:
        grads[name] = g_small[name].reshape(as2d(given[name]).shape)
        groups.append((as2d(given[name]), grads[name], as2d(given["m_" + name]), as2d(given["v_" + name])))
    for name, res in zip(small_names, _adamw_small(groups, name="adamw_small")):
        deltas[name], new_m[name], new_v[name] = res

    like = lambda name, a: a.reshape(given[name].shape)
    return (loss, grad_x[None], *[like(n, grads[n]) for n in names], *[like(n, deltas[n]) for n in names],
            *[like(n, new_m[n]) for n in names], *[like(n, new_v[n]) for n in names])
```

```python
import functools

import jax
import jax.numpy as jnp
from jax import lax
from jax.experimental import pallas as pl
from jax.experimental.pallas import tpu as pltpu

F32 = jnp.float32
BF16 = jnp.bfloat16
_CD = jnp.bfloat16
_TD = jnp.bfloat16

D_MODEL = 1024
N_MEM = 256
CONV_WIDTH = 512
CONV_GROUP = 64
CONV_K = 3
GLA_HEADS = 4
GLA_DK = 64
GLA_DV = 128
GLA_K_TOTAL = 256
GLA_V_TOTAL = 512
GLA_LOWRANK = 16
GLA_GATE_SCALE = 1.0 / 16.0
GLA_CHUNK = 64
XA_HEADS = 4
XA_HEAD_DIM = 256
D_FF = 4096
EPS = 1e-6
W_IN_COLS = 3104
ZA_COLS = 2048
ZB_COLS = 1152
LR_COL = 1024

ADAM_LR = 0.001
ADAM_B1 = 0.9
ADAM_B2 = 0.999
ADAM_EPS = 1e-08
ADAM_WD = 0.01
ADAM_STEP = 10

N_CHIPS = 4
PACK_W = 1024
PACK_ROWS = 4160
PACK_TILE = 160
SMALL_ROWS = 128

_TS = 512
_VMEM = 44 * 1024 * 1024
MESH = pl.DeviceIdType.MESH
ANY = pl.BlockSpec(memory_space=pl.ANY)


def _cp(sem=None, **kw):
    return pltpu.CompilerParams(dimension_semantics=sem, vmem_limit_bytes=_VMEM, **kw)


def _dot(a, b):
    return jnp.dot(a.astype(_CD), b.astype(_CD), preferred_element_type=F32)


def _dot_nt(a, b):
    return lax.dot_general(a.astype(_CD), b.astype(_CD), (((1,), (1,)), ((), ())), preferred_element_type=F32)


def _dot_tn(a, b):
    return lax.dot_general(a.astype(_CD), b.astype(_CD), (((0,), (0,)), ((), ())), preferred_element_type=F32)


def _dot_split(x, ones):
    hi = x.astype(BF16)
    r = x - hi.astype(F32)
    mid = r.astype(BF16)
    lo = (r - mid.astype(F32)).astype(BF16)
    d = lambda p: jnp.dot(p, ones, preferred_element_type=F32)
    return d(hi) + d(mid) + d(lo)


def _pick(n, cands=(1024, 640, 512, 256, 128)):
    for t in cands:
        if n % t == 0:
            return t
    return n


def _rows(s):
    return min(_TS, s)


def _sigmoid(v):
    e = jnp.exp(-jnp.abs(v))
    return jnp.where(v >= 0, 1.0 / (1.0 + e), e / (1.0 + e))


def _mm(a, b, *, mode, name, out_dtypes=(F32,), extras=(), epilogue=None, tm=None, tn=None, tk=None):
    m, k = a.shape
    n = b.shape[1] if mode == "nn" else b.shape[0]
    tm = min(m, tm or 1024)
    tn = tn or _pick(n)
    tk = tk or _pick(k)
    nk = k // tk
    n_ex, n_out = len(extras), len(out_dtypes)

    def body(*refs):
        a_ref, b_ref = refs[:2]
        ex = refs[2 : 2 + n_ex]
        outs = refs[2 + n_ex : 2 + n_ex + n_out]
        part = _dot(a_ref[...], b_ref[...]) if mode == "nn" else _dot_nt(a_ref[...], b_ref[...])

        def finish(acc):
            res = epilogue(acc, *[e[...] for e in ex]) if epilogue else (acc,)
            for o, r in zip(outs, res):
                o[...] = r.astype(o.dtype)

        if nk == 1:
            finish(part)
        else:
            acc_ref = refs[-1]
            kk = pl.program_id(2)

            @pl.when(kk == 0)
            def _():
                acc_ref[...] = part

            @pl.when(kk > 0)
            def _():
                acc_ref[...] += part

            @pl.when(kk == nk - 1)
            def _():
                finish(acc_ref[...])

    b_spec = pl.BlockSpec((tk, tn), lambda i, j, kk: (kk, j)) if mode == "nn" else pl.BlockSpec((tn, tk), lambda i, j, kk: (j, kk))
    tile = pl.BlockSpec((tm, tn), lambda i, j, kk: (i, j))
    out = pl.pallas_call(
        body,
        name=name,
        grid=(m // tm, n // tn, nk),
        in_specs=[pl.BlockSpec((tm, tk), lambda i, j, kk: (i, kk)), b_spec] + [tile] * n_ex,
        out_specs=[tile] * n_out,
        out_shape=[jax.ShapeDtypeStruct((m, n), dt) for dt in out_dtypes],
        scratch_shapes=[pltpu.VMEM((tm, tn), F32)] if nk > 1 else [],
        compiler_params=_cp(("parallel", "parallel", "arbitrary")),
    )(a, b, *extras)
    return out[0] if n_out == 1 else out


def _mm_tn(a, b, *, name):
    s, m = a.shape
    n = b.shape[1]
    cap = max(128, (1 << 20) // n)
    tm = _pick(m, tuple(t for t in (512, 640, 384, 256, 128) if t <= max(cap, 128)))
    ts = min(s, 1 << (((1 << 22) // n).bit_length() - 1))
    ns = s // ts

    def body(a_ref, b_ref, o_ref):
        part = _dot_tn(a_ref[...], b_ref[...])
        if ns == 1:
            o_ref[...] = part
        else:
            ss = pl.program_id(1)

            @pl.when(ss == 0)
            def _():
                o_ref[...] = part

            @pl.when(ss > 0)
            def _():
                o_ref[...] += part

    return pl.pallas_call(
        body,
        name=name,
        grid=(m // tm, ns),
        in_specs=[pl.BlockSpec((ts, tm), lambda i, ss: (ss, i)), pl.BlockSpec((ts, n), lambda i, ss: (ss, 0))],
        out_specs=pl.BlockSpec((tm, n), lambda i, ss: (i, 0)),
        out_shape=jax.ShapeDtypeStruct((m, n), F32),
        compiler_params=_cp(("parallel", "arbitrary")),
    )(a, b)


def _mm_tn_into(a, b, packs, *, rows, off, name):
    s, m = a.shape
    n = b.shape[1]
    tm = 1024 if rows % 1024 == 0 and s >= 4096 else 512
    tr = min(tm, rows)
    per, chips = rows // tr, tm // tr
    ts = min(s, (1 << (((1 << 22) // n).bit_length() - 1)) * 512 // tm)
    ns = s // ts

    def body(a_ref, b_ref, f_in, lo_in, f_ref, lo_ref):
        part = _dot_tn(a_ref[...], b_ref[...])
        pieces = [part[c * tr : (c + 1) * tr] for c in range(chips)]
        if ns == 1:
            for c, p in enumerate(pieces):
                f_ref[c] = p
                lo_ref[c] = p.astype(lo_ref.dtype)
        else:
            ss = pl.program_id(1)

            @pl.when(ss == 0)
            def _():
                for c, p in enumerate(pieces):
                    f_ref[c] = p

            @pl.when(ss > 0)
            def _():
                for c, p in enumerate(pieces):
                    f_ref[c] += p

            @pl.when(ss == ns - 1)
            def _():
                lo_ref[...] = f_ref[...].astype(lo_ref.dtype)

    spec = pl.BlockSpec((chips, tr, n), lambda i, ss: (i // per, off // tr + i % per, 0))
    return pl.pallas_call(
        body,
        name=name,
        grid=(m // tm, ns),
        in_specs=[pl.BlockSpec((ts, tm), lambda i, ss: (ss, i)), pl.BlockSpec((ts, n), lambda i, ss: (ss, 0)), ANY, ANY],
        out_specs=[spec, spec],
        out_shape=[jax.ShapeDtypeStruct(p.shape, p.dtype) for p in packs],
        input_output_aliases={2: 0, 3: 1},
        compiler_params=_cp(("parallel", "arbitrary")),
    )(a, b, *packs)


def _mm_rows(a, b, *, mode, name, rows=(), vecs=(), out_rows=(), out_vecs=(), epilogue, tm=512):
    m, k = a.shape
    n = b.shape[1] if mode == "nn" else b.shape[0]
    tm = min(m, tm)
    parts = 2 if tm % 256 == 0 else 1
    n_r, n_v, n_or, n_ov = len(rows), len(vecs), len(out_rows), len(out_vecs)

    def body(*refs):
        a_ref, b_ref = refs[:2]
        r_refs = refs[2 : 2 + n_r]
        v_refs = refs[2 + n_r : 2 + n_r + n_v]
        or_refs = refs[2 + n_r + n_v : 2 + n_r + n_v + n_or]
        ov_refs = refs[2 + n_r + n_v + n_or :]
        res_vecs = None
        for p in range(parts):
            rs = slice(p * tm // parts, (p + 1) * tm // parts)
            acc = _dot(a_ref[rs, :], b_ref[...]) if mode == "nn" else _dot_nt(a_ref[rs, :], b_ref[...])
            res_rows, part_vecs = epilogue(acc, [r[rs, :] for r in r_refs], [v[...] for v in v_refs])
            for o, r in zip(or_refs, res_rows):
                o[rs, :] = r.astype(o.dtype)
            res_vecs = part_vecs if res_vecs is None else [s + t for s, t in zip(res_vecs, part_vecs)]
        if n_ov:
            first = pl.program_id(0) == 0

            @pl.when(first)
            def _():
                for o, r in zip(ov_refs, res_vecs):
                    o[...] = r

            @pl.when(jnp.logical_not(first))
            def _():
                for o, r in zip(ov_refs, res_vecs):
                    o[...] += r

    tile = pl.BlockSpec((tm, n), lambda i: (i, 0))
    whole = lambda arr: pl.BlockSpec(arr.shape, lambda i: (0, 0))
    vec = lambda w: pl.BlockSpec((1, w), lambda i: (0, 0))
    out = pl.pallas_call(
        body,
        name=name,
        grid=(m // tm,),
        in_specs=[pl.BlockSpec((tm, k), lambda i: (i, 0)), whole(b)] + [tile] * n_r + [vec(v.shape[1]) for v in vecs],
        out_specs=[tile] * n_or + [vec(w) for w in out_vecs],
        out_shape=[jax.ShapeDtypeStruct((m, n), dt) for dt in out_rows] + [jax.ShapeDtypeStruct((1, w), F32) for w in out_vecs],
        compiler_params=_cp(("arbitrary",) if n_ov else ("parallel",)),
    )(a, b, *rows, *vecs)
    return out


def _ep_residual_norm(acc, rows, vecs):
    x = acc + rows[0]
    r = lax.rsqrt(jnp.mean(x * x, axis=-1, keepdims=True) + EPS)
    return [x, x * r * vecs[0]], []


def _ep_norm_bwd(acc, rows, vecs):
    dy = acc
    for extra in rows[2:]:
        dy = dy + extra
    x, dres = rows[0], rows[1]
    r = lax.rsqrt(jnp.mean(x * x, axis=-1, keepdims=True) + EPS)
    xh = x * r
    dxh = dy * vecs[0]
    dx = r * (dxh - xh * jnp.mean(dxh * xh, axis=-1, keepdims=True)) + dres
    return [dx, dx], [jnp.sum(dy * xh, axis=0, keepdims=True)]


def _ep_loss(acc, rows, vecs):
    x = acc + rows[0]
    d = x.shape[-1]
    r = lax.rsqrt(jnp.mean(x * x, axis=-1, keepdims=True) + EPS)
    xh = x * r
    err = xh * vecs[0] - rows[1]
    loss = jnp.zeros((1, 128), F32) + 0.5 * jnp.sum(jnp.mean(err * err, axis=-1, keepdims=True))
    dy = err * (1.0 / d)
    dxh = dy * vecs[0]
    dx = r * (dxh - xh * jnp.mean(dxh * xh, axis=-1, keepdims=True))
    return [dx, dx], [loss, jnp.sum(dy * xh, axis=0, keepdims=True)]


def _rms_fwd(x, g, *, name):
    s, d = x.shape
    ts = _rows(s)

    def body(x_ref, g_ref, o_ref):
        xf = x_ref[...]
        r = lax.rsqrt(jnp.mean(xf * xf, axis=-1, keepdims=True) + EPS)
        o_ref[...] = (xf * r * g_ref[...]).astype(o_ref.dtype)

    return pl.pallas_call(
        body,
        name=name,
        grid=(s // ts,),
        in_specs=[pl.BlockSpec((ts, d), lambda i: (i, 0)), pl.BlockSpec((1, d), lambda i: (0, 0))],
        out_specs=pl.BlockSpec((ts, d), lambda i: (i, 0)),
        out_shape=jax.ShapeDtypeStruct((s, d), _CD),
        compiler_params=_cp(("parallel",)),
    )(x, g)


def _rms_bwd(x, g, dy, dres=None, *, name, want_dx=True, want_lo=True):
    s, d = x.shape
    ts = _rows(s)
    has_res = dres is not None

    def body(*refs):
        x_ref, g_ref, dy_ref = refs[:3]
        pos = 3
        dres_ref = refs[pos] if has_res else None
        pos += has_res
        dx_ref = refs[pos] if want_dx else None
        pos += want_dx
        lo_ref = refs[pos] if want_lo else None
        pos += want_lo
        dg_ref = refs[pos]
        xf = x_ref[...]
        r = lax.rsqrt(jnp.mean(xf * xf, axis=-1, keepdims=True) + EPS)
        xh = xf * r
        dyf = dy_ref[...]
        part = jnp.sum(dyf * xh, axis=0, keepdims=True)

        @pl.when(pl.program_id(0) == 0)
        def _():
            dg_ref[...] = part

        @pl.when(pl.program_id(0) > 0)
        def _():
            dg_ref[...] += part

        if want_dx or want_lo:
            dxh = dyf * g_ref[...]
            dx = r * (dxh - xh * jnp.mean(dxh * xh, axis=-1, keepdims=True))
            if has_res:
                dx = dx + dres_ref[...]
            if want_dx:
                dx_ref[...] = dx
            if want_lo:
                lo_ref[...] = dx.astype(lo_ref.dtype)

    tile = pl.BlockSpec((ts, d), lambda i: (i, 0))
    vec = pl.BlockSpec((1, d), lambda i: (0, 0))
    out_specs, out_shape = [], []
    if want_dx:
        out_specs.append(tile)
        out_shape.append(jax.ShapeDtypeStruct((s, d), F32))
    if want_lo:
        out_specs.append(tile)
        out_shape.append(jax.ShapeDtypeStruct((s, d), _CD))
    out_specs.append(vec)
    out_shape.append(jax.ShapeDtypeStruct((1, d), F32))
    return pl.pallas_call(
        body,
        name=name,
        grid=(s // ts,),
        in_specs=[tile, vec, tile] + ([tile] if has_res else []),
        out_specs=out_specs,
        out_shape=out_shape,
        compiler_params=_cp(("arbitrary",)),
    )(x, g, dy, *([dres] if has_res else []))


def _final_loss(x3, g, tgt, *, name):
    s, d = x3.shape
    ts = _rows(s)

    def body(x_ref, g_ref, t_ref, dx_ref, lo_ref, loss_ref, dg_ref):
        xf = x_ref[...]
        r = lax.rsqrt(jnp.mean(xf * xf, axis=-1, keepdims=True) + EPS)
        xh = xf * r
        gg = g_ref[...]
        err = xh * gg - t_ref[...]
        lpart = jnp.zeros((1, 128), F32) + 0.5 * jnp.sum(jnp.mean(err * err, axis=-1, keepdims=True))
        dy = err * (1.0 / d)
        gpart = jnp.sum(dy * xh, axis=0, keepdims=True)

        @pl.when(pl.program_id(0) == 0)
        def _():
            loss_ref[...] = lpart
            dg_ref[...] = gpart

        @pl.when(pl.program_id(0) > 0)
        def _():
            loss_ref[...] += lpart
            dg_ref[...] += gpart

        dxh = dy * gg
        dx = r * (dxh - xh * jnp.mean(dxh * xh, axis=-1, keepdims=True))
        dx_ref[...] = dx
        lo_ref[...] = dx.astype(lo_ref.dtype)

    tile = pl.BlockSpec((ts, d), lambda i: (i, 0))
    vec = pl.BlockSpec((1, d), lambda i: (0, 0))
    return pl.pallas_call(
        body,
        name=name,
        grid=(s // ts,),
        in_specs=[tile, vec, tile],
        out_specs=[tile, tile, pl.BlockSpec((1, 128), lambda i: (0, 0)), vec],
        out_shape=[
            jax.ShapeDtypeStruct((s, d), F32),
            jax.ShapeDtypeStruct((s, d), _CD),
            jax.ShapeDtypeStruct((1, 128), F32),
            jax.ShapeDtypeStruct((1, d), F32),
        ],
        compiler_params=_cp(("arbitrary",)),
    )(x3, g, tgt)


def _chunk_scan(v, row_in_chunk, suffix):
    t = v.shape[0]
    step = 1
    while step < GLA_CHUNK:
        if suffix:
            v = v + jnp.where(row_in_chunk < GLA_CHUNK - step, pltpu.roll(v, t - step, 0), 0.0)
        else:
            v = v + jnp.where(row_in_chunk >= step, pltpu.roll(v, step, 0), 0.0)
        step *= 2
    return v


def _gate_pre(lr, w_ref, b_ref):
    return _dot(lr, w_ref[...]) + b_ref[...]


def _gate_fwd(z, waf, wab, baf, bab, *, name):
    s = z.shape[0]
    ts = _rows(s)

    def body(lr_ref, waf_ref, wab_ref, baf_ref, bab_ref, bf_ref, bb_ref):
        lr = lr_ref[...]
        ric = lax.broadcasted_iota(jnp.int32, (ts, GLA_K_TOTAL), 0) & (GLA_CHUNK - 1)
        for w_ref, b_ref, o_ref, suffix in ((waf_ref, baf_ref, bf_ref, False), (wab_ref, bab_ref, bb_ref, True)):
            pre = _gate_pre(lr, w_ref, b_ref)
            la = (jnp.minimum(pre, 0.0) - jnp.log(1.0 + jnp.exp(-jnp.abs(pre)))) * GLA_GATE_SCALE
            o_ref[...] = _chunk_scan(la, ric, suffix)

    wspec = pl.BlockSpec((128, GLA_K_TOTAL), lambda i: (0, 0))
    bspec = pl.BlockSpec((1, GLA_K_TOTAL), lambda i: (0, 0))
    tile = pl.BlockSpec((ts, GLA_K_TOTAL), lambda i: (i, 0))
    return pl.pallas_call(
        body,
        name=name,
        grid=(s // ts,),
        in_specs=[pl.BlockSpec((ts, 128), lambda i: (i, LR_COL // 128)), wspec, wspec, bspec, bspec],
        out_specs=[tile, tile],
        out_shape=[jax.ShapeDtypeStruct((s, GLA_K_TOTAL), F32)] * 2,
        compiler_params=_cp(("parallel",)),
    )(z, waf, wab, baf, bab)


def _gate_bwd(z, waf, wab, baf, bab, dbf, dbb, dqkv_f, dqkv_b, *, name):
    s = z.shape[0]
    ts = _rows(s)

    def body(lr_ref, waf_ref, wab_ref, baf_ref, bab_ref, dbf_ref, dbb_ref, gf_ref, gb_ref, dzb_ref, dwf_ref, dwb_ref, dbaf_ref, dbab_ref):
        lr = lr_ref[...]
        ric = lax.broadcasted_iota(jnp.int32, (ts, GLA_K_TOTAL), 0) & (GLA_CHUNK - 1)
        first = pl.program_id(0) == 0
        dlr = None
        for w_ref, b_ref, db_ref, dw_ref, dbias_ref, suffix in (
            (waf_ref, baf_ref, dbf_ref, dwf_ref, dbaf_ref, True),
            (wab_ref, bab_ref, dbb_ref, dwb_ref, dbab_ref, False),
        ):
            pre = _gate_pre(lr, w_ref, b_ref)
            dla = _chunk_scan(db_ref[...], ric, suffix)
            dpre = dla * GLA_GATE_SCALE * _sigmoid(-pre)
            part = _dot_nt(dpre, w_ref[...])
            dlr = part if dlr is None else dlr + part
            dw = _dot_tn(lr, dpre)
            dbias = jnp.sum(dpre, axis=0, keepdims=True)

            @pl.when(first)
            def _():
                dw_ref[...] = dw
                dbias_ref[...] = dbias

            @pl.when(jnp.logical_not(first))
            def _():
                dw_ref[...] += dw
                dbias_ref[...] += dbias

        dzb_ref[...] = jnp.concatenate([gf_ref[...] + gb_ref[...], dlr], axis=1).astype(dzb_ref.dtype)

    wspec = pl.BlockSpec((128, GLA_K_TOTAL), lambda i: (0, 0))
    bspec = pl.BlockSpec((1, GLA_K_TOTAL), lambda i: (0, 0))
    tile = pl.BlockSpec((ts, GLA_K_TOTAL), lambda i: (i, 0))
    wide = pl.BlockSpec((ts, 2 * GLA_K_TOTAL + GLA_V_TOTAL), lambda i: (i, 0))
    return pl.pallas_call(
        body,
        name=name,
        grid=(s // ts,),
        in_specs=[pl.BlockSpec((ts, 128), lambda i: (i, LR_COL // 128)), wspec, wspec, bspec, bspec, tile, tile, wide, wide],
        out_specs=[pl.BlockSpec((ts, ZB_COLS), lambda i: (i, 0)), wspec, wspec, bspec, bspec],
        out_shape=[
            jax.ShapeDtypeStruct((s, ZB_COLS), _CD),
            jax.ShapeDtypeStruct((128, GLA_K_TOTAL), F32),
            jax.ShapeDtypeStruct((128, GLA_K_TOTAL), F32),
            jax.ShapeDtypeStruct((1, GLA_K_TOTAL), F32),
            jax.ShapeDtypeStruct((1, GLA_K_TOTAL), F32),
        ],
        compiler_params=_cp(("arbitrary",)),
    )(z, waf, wab, baf, bab, dbf, dbb, dqkv_f, dqkv_b)


def _gla_masks(rev):
    lane_head = lax.broadcasted_iota(jnp.int32, (1, GLA_K_TOTAL), 1) >> 6
    head_masks = [lane_head == h for h in range(GLA_HEADS)]
    t = lax.broadcasted_iota(jnp.int32, (GLA_HEADS * GLA_CHUNK, GLA_CHUNK), 0) & (GLA_CHUNK - 1)
    u = lax.broadcasted_iota(jnp.int32, (GLA_HEADS * GLA_CHUNK, GLA_CHUNK), 1)
    tri = (u > t) if rev else (u <= t)
    row = lax.broadcasted_iota(jnp.int32, (GLA_CHUNK, GLA_K_TOTAL), 0)
    total_row = row == (0 if rev else GLA_CHUNK - 1)
    return head_masks, tri, total_row


def _spread(a, head_masks):
    return jnp.concatenate([jnp.where(m, a, 0.0) for m in head_masks], axis=0)


def _stack(a):
    return jnp.concatenate([a[:, GLA_DV * h : GLA_DV * (h + 1)] for h in range(GLA_HEADS)], axis=0)


def _unstack(a):
    return jnp.concatenate([a[GLA_CHUNK * h : GLA_CHUNK * (h + 1)] for h in range(GLA_HEADS)], axis=1)


def _collect(a, head_masks):
    out = None
    for h, m in enumerate(head_masks):
        part = jnp.where(m, a[GLA_CHUNK * h : GLA_CHUNK * (h + 1)], 0.0)
        out = part if out is None else out + part
    return out


def _gla_chunk_terms(q_ref, k_ref, v_ref, b_ref, rows, head_masks, tri, total_row):
    q = q_ref[rows, :] * (GLA_DK**-0.5)
    k = k_ref[rows, :]
    v = v_ref[rows, :]
    b = b_ref[rows, :]
    eb = jnp.exp(b)
    enb = jnp.exp(-b)
    g = jnp.sum(jnp.where(total_row, b, 0.0), axis=0, keepdims=True)
    egb = jnp.exp(g - b)
    qt = q * eb
    kt = k * enb
    kh = k * egb
    q_heads = _spread(qt, head_masks)
    attn = jnp.where(tri, _dot_nt(q_heads, kt), 0.0)
    return v, eb, enb, egb, jnp.exp(g), qt, kt, kh, q_heads, attn


def _gla_specs(s, tb, rev_blocks):
    nb = s // tb
    rb = (lambda i: nb - 1 - i) if rev_blocks else (lambda i: i)
    q_spec = pl.BlockSpec((tb, GLA_K_TOTAL), lambda i: (rb(i), 0))
    k_spec = pl.BlockSpec((tb, GLA_K_TOTAL), lambda i: (rb(i), 1))
    v_spec = pl.BlockSpec((tb, GLA_V_TOTAL), lambda i: (rb(i), 1))
    b_spec = pl.BlockSpec((tb, GLA_K_TOTAL), lambda i: (rb(i), 0))
    o_spec = pl.BlockSpec((tb, GLA_V_TOTAL), lambda i: (rb(i), 0))
    st_spec = pl.BlockSpec((tb // GLA_CHUNK, GLA_DV, GLA_K_TOTAL), lambda i: (rb(i), 0, 0))
    return nb, q_spec, k_spec, v_spec, b_spec, o_spec, st_spec


def _gla_fwd_chunk(cidx, q_ref, k_ref, v_ref, b_ref, o_ref, sv_ref, st_ref, masks):
    head_masks, tri, total_row = masks
    rows = pl.ds(pl.multiple_of(cidx * GLA_CHUNK, GLA_CHUNK), GLA_CHUNK)
    v, _, _, _, eg, _, _, kh, q_heads, attn = _gla_chunk_terms(q_ref, k_ref, v_ref, b_ref, rows, head_masks, tri, total_row)
    o = jnp.concatenate(
        [_dot(attn[GLA_CHUNK * h : GLA_CHUNK * (h + 1)], v[:, GLA_DV * h : GLA_DV * (h + 1)]) for h in range(GLA_HEADS)], axis=1
    )
    st = st_ref[...]
    o_ref[rows, :] = o + _unstack(_dot_nt(q_heads, st))
    sv_ref[cidx] = st
    st_ref[...] = st * eg + _dot_tn(_stack(v), _spread(kh, head_masks))


def _gla_fwd(z, b_f, b_b, *, name):
    s = z.shape[0]
    tb = _rows(s)
    cpb = tb // GLA_CHUNK
    nb, qf, kf, vf, bf, of, sf = _gla_specs(s, tb, False)
    _, qr, kr, vr, br, orr, sr = _gla_specs(s, tb, True)

    def body(qf_ref, kf_ref, vf_ref, bf_ref, qr_ref, kr_ref, vr_ref, br_ref, of_ref, svf_ref, or_ref, svr_ref, stf_ref, str_ref):
        masks_f, masks_r = _gla_masks(False), _gla_masks(True)

        @pl.when(pl.program_id(0) == 0)
        def _():
            stf_ref[...] = jnp.zeros_like(stf_ref)
            str_ref[...] = jnp.zeros_like(str_ref)

        def chunk(ci, carry):
            _gla_fwd_chunk(ci, qf_ref, kf_ref, vf_ref, bf_ref, of_ref, svf_ref, stf_ref, masks_f)
            _gla_fwd_chunk(cpb - 1 - ci, qr_ref, kr_ref, vr_ref, br_ref, or_ref, svr_ref, str_ref, masks_r)
            return carry

        lax.fori_loop(0, cpb, chunk, 0)

    o_shape = jax.ShapeDtypeStruct((s, GLA_V_TOTAL), F32)
    st_shape = jax.ShapeDtypeStruct((s // GLA_CHUNK, GLA_DV, GLA_K_TOTAL), F32)
    return pl.pallas_call(
        body,
        name=name,
        grid=(nb,),
        in_specs=[qf, kf, vf, bf, qr, kr, vr, br],
        out_specs=[of, sf, orr, sr],
        out_shape=[o_shape, st_shape, o_shape, st_shape],
        scratch_shapes=[pltpu.VMEM((GLA_DV, GLA_K_TOTAL), F32)] * 2,
        compiler_params=_cp(("arbitrary",)),
    )(z, z, z, b_f, z, z, z, b_b)


def _gla_bwd_chunk(cidx, q_ref, k_ref, v_ref, b_ref, do_ref, sv_ref, dqkv_ref, db_ref, dst_ref, masks):
    head_masks, tri, total_row = masks
    rows = pl.ds(pl.multiple_of(cidx * GLA_CHUNK, GLA_CHUNK), GLA_CHUNK)
    v, eb, enb, egb, eg, qt, kt, kh, q_heads, attn = _gla_chunk_terms(q_ref, k_ref, v_ref, b_ref, rows, head_masks, tri, total_row)
    do_c = do_ref[rows, :]
    st = sv_ref[cidx]
    dst = dst_ref[...]
    do_s, v_s = _stack(do_c), _stack(v)
    hs = lambda a, h: a[GLA_CHUNK * h : GLA_CHUNK * (h + 1)]
    vs = lambda a, h: a[:, GLA_DV * h : GLA_DV * (h + 1)]
    dattn = jnp.concatenate([_dot_nt(vs(do_c, h), vs(v, h)) for h in range(GLA_HEADS)], axis=0)
    dattn = jnp.where(tri, dattn, 0.0)
    dv = jnp.concatenate([_dot_tn(hs(attn, h), vs(do_c, h)) for h in range(GLA_HEADS)], axis=1)
    dv = dv + _unstack(_dot_nt(_spread(kh, head_masks), dst))
    dqt = _collect(_dot(do_s, st), head_masks)
    dkt = jnp.zeros_like(dqt)
    for h in range(GLA_HEADS):
        dqt = dqt + jnp.where(head_masks[h], _dot(hs(dattn, h), kt), 0.0)
        dkt = dkt + jnp.where(head_masks[h], _dot_tn(hs(dattn, h), qt), 0.0)
    dkh = _collect(_dot(v_s, dst), head_masks)
    dg = jnp.sum(dkh * kh, axis=0, keepdims=True) + jnp.sum(dst * st, axis=0, keepdims=True) * eg
    db = dqt * qt - dkt * kt - dkh * kh + jnp.where(total_row, dg, 0.0)
    dq = dqt * eb * (GLA_DK**-0.5)
    dk = dkt * enb + dkh * egb
    dqkv_ref[rows, :] = jnp.concatenate([dq, dk, dv], axis=1)
    db_ref[rows, :] = db
    dst_ref[...] = dst * eg + _dot_tn(do_s, q_heads)


def _gla_bwd(z, b_f, b_b, do, st_f, st_b, *, name):
    s = z.shape[0]
    tb = _rows(s)
    cpb = tb // GLA_CHUNK
    wide = 2 * GLA_K_TOTAL + GLA_V_TOTAL
    nb, qf, kf, vf, bf, of, sf = _gla_specs(s, tb, True)
    _, qr, kr, vr, br, orr, sr = _gla_specs(s, tb, False)
    gf = pl.BlockSpec((tb, wide), lambda i: (nb - 1 - i, 0))
    gr = pl.BlockSpec((tb, wide), lambda i: (i, 0))

    def body(qf_ref, kf_ref, vf_ref, bf_ref, dof_ref, svf_ref, qr_ref, kr_ref, vr_ref, br_ref, dor_ref, svr_ref,
             gf_ref, dbf_ref, gr_ref, dbr_ref, dstf_ref, dstr_ref):
        masks_f, masks_r = _gla_masks(False), _gla_masks(True)

        @pl.when(pl.program_id(0) == 0)
        def _():
            dstf_ref[...] = jnp.zeros_like(dstf_ref)
            dstr_ref[...] = jnp.zeros_like(dstr_ref)

        def chunk(ci, carry):
            _gla_bwd_chunk(cpb - 1 - ci, qf_ref, kf_ref, vf_ref, bf_ref, dof_ref, svf_ref, gf_ref, dbf_ref, dstf_ref, masks_f)
            _gla_bwd_chunk(ci, qr_ref, kr_ref, vr_ref, br_ref, dor_ref, svr_ref, gr_ref, dbr_ref, dstr_ref, masks_r)
            return carry

        lax.fori_loop(0, cpb, chunk, 0)

    g_shape = jax.ShapeDtypeStruct((s, wide), F32)
    db_shape = jax.ShapeDtypeStruct((s, GLA_K_TOTAL), F32)
    return pl.pallas_call(
        body,
        name=name,
        grid=(nb,),
        in_specs=[qf, kf, vf, bf, of, sf, qr, kr, vr, br, orr, sr],
        out_specs=[gf, bf, gr, br],
        out_shape=[g_shape, db_shape, g_shape, db_shape],
        scratch_shapes=[pltpu.VMEM((GLA_DV, GLA_K_TOTAL), F32)] * 2,
        compiler_params=_cp(("arbitrary",)),
    )(z, z, z, b_f, do, st_f, z, z, z, b_b, do, st_b)


HALO = 8


def _halo_specs(s, ts, width, col):
    last = s // HALO - 1
    per = ts // HALO
    prev = pl.BlockSpec((HALO, width), lambda i: (jnp.maximum(i * per - 1, 0), col))
    nxt = pl.BlockSpec((HALO, width), lambda i: (jnp.minimum((i + 1) * per, last), col))
    return prev, nxt


def _group_ones():
    r = lax.broadcasted_iota(jnp.int32, (CONV_WIDTH, CONV_WIDTH), 0) >> 6
    c = lax.broadcasted_iota(jnp.int32, (CONV_WIDTH, CONV_WIDTH), 1) >> 6
    return (r == c).astype(BF16)


def _conv_terms(cc_ext, cu_ext, cw, valid):
    n = cc_ext.shape[0]
    hc = jnp.where(valid, cc_ext * cu_ext, 0.0)
    hc_prev = pltpu.roll(hc, 1, 0)
    hc_next = pltpu.roll(hc, n - 1, 0)
    conv = cw[0:1] * hc_prev + cw[1:2] * hc + cw[2:3] * hc_next
    return hc, hc_prev, hc_next, conv


def _ext(prev_ref, cur_ref, next_ref):
    return jnp.concatenate([prev_ref[...], cur_ref[...], next_ref[...]], axis=0)


def _valid_rows(ts, s):
    row = lax.broadcasted_iota(jnp.int32, (ts + 2 * HALO, 1), 0) + (pl.program_id(0) * ts - HALO)
    return (row >= 0) & (row < s)


def _head_norm(o, gn):
    out = []
    for h in range(GLA_HEADS):
        oh = o[:, GLA_DV * h : GLA_DV * (h + 1)]
        r = lax.rsqrt(jnp.mean(oh * oh, axis=-1, keepdims=True) + EPS)
        out.append((oh * r, r))
    return out


def _mix_fwd(z, o_f, o_b, conv_w, conv_norm, gla_norm, *, name):
    s = z.shape[0]
    ts = _rows(s)
    cprev, cnext = _halo_specs(s, ts, CONV_WIDTH, 1)
    uprev, unext = _halo_specs(s, ts, CONV_WIDTH, 2)

    def body(cb_ref, cc_ref, cu_ref, ccp_ref, ccn_ref, cup_ref, cun_ref, g_ref, of_ref, ob_ref, cw_ref, cn_ref, gn_ref, y_ref):
        valid = _valid_rows(ts, s)
        _, _, _, conv = _conv_terms(_ext(ccp_ref, cc_ref, ccn_ref), _ext(cup_ref, cu_ref, cun_ref), cw_ref[...], valid)
        yc = cb_ref[...] * conv[HALO : HALO + ts]
        ms = _dot_split(yc * yc, _group_ones()) * (1.0 / CONV_GROUP)
        y_conv = yc * lax.rsqrt(ms + EPS) * cn_ref[...]
        gate = g_ref[...]
        silu = gate * _sigmoid(gate)
        gn = gn_ref[...]
        y_gla = jnp.concatenate([oh * gn for oh, _ in _head_norm(of_ref[...] + ob_ref[...], gn)], axis=1) * silu
        y_ref[...] = jnp.concatenate([y_conv, y_gla], axis=1).astype(y_ref.dtype)

    col = lambda c, w=CONV_WIDTH: pl.BlockSpec((ts, w), lambda i: (i, c))
    return pl.pallas_call(
        body,
        name=name,
        grid=(s // ts,),
        in_specs=[col(0), col(1), col(2), cprev, cnext, uprev, unext, col(3), col(0), col(0),
                  pl.BlockSpec((CONV_K, CONV_WIDTH), lambda i: (0, 0)), pl.BlockSpec((1, CONV_WIDTH), lambda i: (0, 0)),
                  pl.BlockSpec((1, GLA_DV), lambda i: (0, 0))],
        out_specs=pl.BlockSpec((ts, D_MODEL), lambda i: (i, 0)),
        out_shape=jax.ShapeDtypeStruct((s, D_MODEL), _CD),
        compiler_params=_cp(("parallel",)),
    )(z, z, z, z, z, z, z, z, o_f, o_b, conv_w, conv_norm, gla_norm)


def _mix_bwd(z, o_f, o_b, dy, conv_w, conv_norm, gla_norm, *, name):
    s = z.shape[0]
    ts = _rows(s)
    halos = [_halo_specs(s, ts, CONV_WIDTH, c) for c in (0, 1, 2)]
    dprev, dnext = _halo_specs(s, ts, CONV_WIDTH, 0)

    def body(cb_ref, cc_ref, cu_ref, cbp_ref, cbn_ref, ccp_ref, ccn_ref, cup_ref, cun_ref, g_ref, of_ref, ob_ref,
             dyc_ref, dyg_ref, dyp_ref, dyn_ref, cw_ref, cn_ref, gn_ref, dza_ref, do_ref, dcw_ref, dcn_ref, dgn_ref):
        n = ts + 2 * HALO
        valid = _valid_rows(ts, s)
        cw = cw_ref[...]
        cn = cn_ref[...]
        ones = _group_ones()
        cb = _ext(cbp_ref, cb_ref, cbn_ref)
        cc = _ext(ccp_ref, cc_ref, ccn_ref)
        cu = _ext(cup_ref, cu_ref, cun_ref)
        dy = _ext(dyp_ref, dyc_ref, dyn_ref)
        hc, hc_prev, hc_next, conv = _conv_terms(cc, cu, cw, valid)
        yc = cb * conv
        r = lax.rsqrt(_dot_split(yc * yc, ones) * (1.0 / CONV_GROUP) + EPS)
        yh = yc * r
        dyh = dy * cn
        dyc = r * (dyh - yh * (_dot_split(dyh * yh, ones) * (1.0 / CONV_GROUP)))
        dconv = jnp.where(valid, dyc * cb, 0.0)
        dhc = cw[0:1] * pltpu.roll(dconv, n - 1, 0) + cw[1:2] * dconv + cw[2:3] * pltpu.roll(dconv, 1, 0)
        mid = lambda a: a[HALO : HALO + ts]
        dza_ref[:, 0 : 3 * CONV_WIDTH] = jnp.concatenate([mid(dyc * conv), mid(dhc * cu), mid(dhc * cc)], axis=1).astype(dza_ref.dtype)
        dconv_m = mid(dconv)
        colsum = lambda a: jnp.sum(a, axis=0, keepdims=True)
        dcw = jnp.concatenate([colsum(dconv_m * mid(hc_prev)), colsum(dconv_m * mid(hc)), colsum(dconv_m * mid(hc_next))], axis=0)
        dcn = colsum(mid(dy * yh))

        gate = g_ref[...]
        sg = _sigmoid(gate)
        silu = gate * sg
        gn = gn_ref[...]
        dyg = dyg_ref[...]
        don = dyg * silu
        heads = _head_norm(of_ref[...] + ob_ref[...], gn)
        on = jnp.concatenate([oh * gn for oh, _ in heads], axis=1)
        dza_ref[:, 3 * CONV_WIDTH : ZA_COLS] = (dyg * on * (sg * (1.0 + gate * (1.0 - sg)))).astype(dza_ref.dtype)
        dgn = jnp.zeros((1, GLA_DV), F32)
        dos = []
        for h, (oh, rh) in enumerate(heads):
            donh = don[:, GLA_DV * h : GLA_DV * (h + 1)]
            dgn = dgn + colsum(donh * oh)
            doh = donh * gn
            dos.append(rh * (doh - oh * jnp.mean(doh * oh, axis=-1, keepdims=True)))
        do_ref[...] = jnp.concatenate(dos, axis=1)

        first = pl.program_id(0) == 0

        @pl.when(first)
        def _():
            dcw_ref[...] = dcw
            dcn_ref[...] = dcn
            dgn_ref[...] = dgn

        @pl.when(jnp.logical_not(first))
        def _():
            dcw_ref[...] += dcw
            dcn_ref[...] += dcn
            dgn_ref[...] += dgn

    col = lambda c, w=CONV_WIDTH: pl.BlockSpec((ts, w), lambda i: (i, c))
    cw_spec = pl.BlockSpec((CONV_K, CONV_WIDTH), lambda i: (0, 0))
    cn_spec = pl.BlockSpec((1, CONV_WIDTH), lambda i: (0, 0))
    gn_spec = pl.BlockSpec((1, GLA_DV), lambda i: (0, 0))
    return pl.pallas_call(
        body,
        name=name,
        grid=(s // ts,),
        in_specs=[col(0), col(1), col(2), halos[0][0], halos[0][1], halos[1][0], halos[1][1], halos[2][0], halos[2][1],
                  col(3), col(0), col(0), col(0), col(1), dprev, dnext, cw_spec, cn_spec, gn_spec],
        out_specs=[pl.BlockSpec((ts, ZA_COLS), lambda i: (i, 0)), col(0), cw_spec, cn_spec, gn_spec],
        out_shape=[
            jax.ShapeDtypeStruct((s, ZA_COLS), _CD),
            jax.ShapeDtypeStruct((s, GLA_V_TOTAL), F32),
            jax.ShapeDtypeStruct((CONV_K, CONV_WIDTH), F32),
            jax.ShapeDtypeStruct((1, CONV_WIDTH), F32),
            jax.ShapeDtypeStruct((1, GLA_DV), F32),
        ],
        compiler_params=_cp(("arbitrary",)),
    )(z, z, z, z, z, z, z, z, z, z, o_f, o_b, dy, dy, dy, dy, conv_w, conv_norm, gla_norm)


def _xa_probs(q_ref, kv_ref, h):
    qh = q_ref[:, XA_HEAD_DIM * h : XA_HEAD_DIM * (h + 1)]
    kh = kv_ref[:, XA_HEAD_DIM * h : XA_HEAD_DIM * (h + 1)]
    vh = kv_ref[:, D_MODEL + XA_HEAD_DIM * h : D_MODEL + XA_HEAD_DIM * (h + 1)]
    sc = _dot_nt(qh, kh) * (XA_HEAD_DIM**-0.5)
    e = jnp.exp(sc - jnp.max(sc, axis=-1, keepdims=True))
    return qh, kh, vh, e / jnp.sum(e, axis=-1, keepdims=True)


def _xattn_fwd(qx, kv, *, name):
    s = qx.shape[0]
    ts = _rows(s)

    def body(q_ref, kv_ref, o_ref):
        outs = []
        for h in range(XA_HEADS):
            _, _, vh, p = _xa_probs(q_ref, kv_ref, h)
            outs.append(_dot(p, vh))
        o_ref[...] = jnp.concatenate(outs, axis=1).astype(o_ref.dtype)

    return pl.pallas_call(
        body,
        name=name,
        grid=(s // ts,),
        in_specs=[pl.BlockSpec((ts, D_MODEL), lambda i: (i, 0)), pl.BlockSpec((N_MEM, 2 * D_MODEL), lambda i: (0, 0))],
        out_specs=pl.BlockSpec((ts, D_MODEL), lambda i: (i, 0)),
        out_shape=jax.ShapeDtypeStruct((s, D_MODEL), _CD),
        compiler_params=_cp(("parallel",)),
    )(qx, kv)


def _xattn_bwd(qx, kv, dox, *, name):
    s = qx.shape[0]
    ts = _rows(s)

    def body(q_ref, kv_ref, do_ref, dq_ref, dkv_ref):
        dqs, dks, dvs = [], [], []
        for h in range(XA_HEADS):
            qh, kh, vh, p = _xa_probs(q_ref, kv_ref, h)
            doh = do_ref[:, XA_HEAD_DIM * h : XA_HEAD_DIM * (h + 1)]
            dp = _dot_nt(doh, vh)
            ds = p * (dp - jnp.sum(dp * p, axis=-1, keepdims=True)) * (XA_HEAD_DIM**-0.5)
            dqs.append(_dot(ds, kh))
            dks.append(_dot_tn(ds, qh))
            dvs.append(_dot_tn(p, doh))
        dq_ref[...] = jnp.concatenate(dqs, axis=1).astype(dq_ref.dtype)
        dkv = jnp.concatenate(dks + dvs, axis=1)

        @pl.when(pl.program_id(0) == 0)
        def _():
            dkv_ref[...] = dkv

        @pl.when(pl.program_id(0) > 0)
        def _():
            dkv_ref[...] += dkv

    tile = pl.BlockSpec((ts, D_MODEL), lambda i: (i, 0))
    kv_spec = pl.BlockSpec((N_MEM, 2 * D_MODEL), lambda i: (0, 0))
    return pl.pallas_call(
        body,
        name=name,
        grid=(s // ts,),
        in_specs=[tile, kv_spec, tile],
        out_specs=[tile, kv_spec],
        out_shape=[jax.ShapeDtypeStruct((s, D_MODEL), _CD), jax.ShapeDtypeStruct((N_MEM, 2 * D_MODEL), F32)],
        compiler_params=_cp(("arbitrary",)),
    )(qx, kv, dox)


def _adamw_math(w, g, m, v):
    m = ADAM_B1 * m + (1.0 - ADAM_B1) * g
    v = ADAM_B2 * v + (1.0 - ADAM_B2) * (g * g)
    m_hat = m / (1.0 - ADAM_B1**ADAM_STEP)
    v_hat = v / (1.0 - ADAM_B2**ADAM_STEP)
    delta = -ADAM_LR * (m_hat / (jnp.sqrt(v_hat) + ADAM_EPS) + ADAM_WD * w)
    return delta, m, v


def _adamw(w, m, v, shard_rows, off, *, transposed, name):
    r, c = w.shape
    by_columns = r % 256 != 0
    tr = 256
    if by_columns:
        assert not transposed and off == 0
        g_spec = tile = pl.BlockSpec((r, tr), lambda i: (0, i))
    else:
        g_spec = pl.BlockSpec((c, tr), lambda i: (off // c, i)) if transposed else pl.BlockSpec((tr, c), lambda i: (off // tr + i, 0))
        tile = pl.BlockSpec((tr, c), lambda i: (i, 0))

    def body(w_ref, g_ref, m_ref, v_ref, go_ref, d_ref, nm_ref, nv_ref):
        g = g_ref[...].T if transposed else g_ref[...]
        go_ref[...] = g
        d_ref[...], nm_ref[...], nv_ref[...] = _adamw_math(w_ref[...], g, m_ref[...], v_ref[...])

    return pl.pallas_call(
        body,
        name=name,
        grid=((c if by_columns else r) // tr,),
        in_specs=[tile, g_spec, tile, tile],
        out_specs=[tile] * 4,
        out_shape=[jax.ShapeDtypeStruct((r, c), F32)] * 4,
        compiler_params=_cp(("parallel",)),
    )(w, shard_rows, m, v)


def _adamw_small(groups, *, name):
    n = len(groups)

    def body(*refs):
        ins, outs = refs[: 4 * n], refs[4 * n :]
        for i in range(n):
            w_ref, g_ref, m_ref, v_ref = ins[4 * i : 4 * i + 4]
            outs[3 * i][...], outs[3 * i + 1][...], outs[3 * i + 2][...] = _adamw_math(w_ref[...], g_ref[...], m_ref[...], v_ref[...])

    flat = [a for grp in groups for a in grp]
    vm = pl.BlockSpec(memory_space=pltpu.VMEM)
    res = pl.pallas_call(
        body,
        name=name,
        in_specs=[vm] * (4 * n),
        out_specs=[vm] * (3 * n),
        out_shape=[jax.ShapeDtypeStruct(grp[0].shape, F32) for grp in groups for _ in range(3)],
        compiler_params=_cp(),
    )(*flat)
    return [tuple(res[3 * i : 3 * i + 3]) for i in range(n)]


def _place():
    return lax.axis_index("x"), lax.axis_index("y"), lax.axis_index("c")


def _rel_chip(x, y, k):
    return (1 - x if k & 2 else x), (1 - y if k & 1 else y)


def _half(c, rh):
    return pl.ds(pl.multiple_of(c * rh, 16), rh)


def _gather_weights(pack):
    r, w = pack.shape
    rh = r // 2

    def body(p_ref, q_ref, send_sems, recv_sems):
        x, y, c = _place()
        j = 2 * x + y
        rows = _half(c, rh)

        def to_chip(k):
            cx, cy = _rel_chip(x, y, k)
            return pltpu.make_async_remote_copy(
                src_ref=p_ref.at[rows], dst_ref=q_ref.at[j, rows], send_sem=send_sems.at[k - 1], recv_sem=recv_sems.at[k - 1],
                device_id=(cx, cy, c), device_id_type=MESH)

        def to_sibling(k):
            cx, cy = _rel_chip(x, y, k)
            slot = q_ref.at[2 * cx + cy, rows]
            return pltpu.make_async_remote_copy(
                src_ref=slot, dst_ref=slot, send_sem=send_sems.at[2 + k], recv_sem=recv_sems.at[2 + k],
                device_id=(x, y, 1 - c), device_id_type=MESH)

        first = [to_chip(k) for k in range(1, N_CHIPS)]
        passed = [to_sibling(k) for k in range(1, N_CHIPS)]
        own = pltpu.make_async_remote_copy(
            src_ref=p_ref, dst_ref=q_ref.at[j], send_sem=send_sems.at[6], recv_sem=recv_sems.at[6],
            device_id=(x, y, 1 - c), device_id_type=MESH)
        for cp in first:
            cp.start()
        own.start()
        for cp, fw in zip(first, passed):
            cp.wait_recv()
            fw.start()
        for fw in passed:
            fw.wait_recv()
        own.wait_recv()
        for cp in first + passed + [own]:
            cp.wait_send()

    return pl.pallas_call(
        body,
        name="gather_weights",
        in_specs=[ANY],
        out_specs=ANY,
        out_shape=jax.ShapeDtypeStruct((N_CHIPS, r, w), pack.dtype),
        scratch_shapes=[pltpu.SemaphoreType.DMA((7,)), pltpu.SemaphoreType.DMA((7,))],
        compiler_params=pltpu.CompilerParams(has_side_effects=True),
    )(pack)


def _swap_halves(g):
    n, r, w = g.shape
    rh = r // 2

    def body(g_ref, o_ref, send_sem, recv_sem):
        x, y, c = _place()
        cp = pltpu.make_async_remote_copy(
            src_ref=g_ref.at[:, _half(1 - c, rh)], dst_ref=o_ref, send_sem=send_sem, recv_sem=recv_sem,
            device_id=(x, y, 1 - c), device_id_type=MESH)
        cp.start()
        cp.wait()

    return pl.pallas_call(
        body,
        name="grads_to_sibling",
        in_specs=[ANY],
        out_specs=ANY,
        out_shape=jax.ShapeDtypeStruct((n, rh, w), g.dtype),
        scratch_shapes=[pltpu.SemaphoreType.DMA, pltpu.SemaphoreType.DMA],
        compiler_params=pltpu.CompilerParams(has_side_effects=True),
    )(g)


def _chip_sums(g, got, where):
    n, r, w = g.shape
    rh = r // 2
    nt = rh // PACK_TILE

    def body(where_ref, g_ref, got_ref, o_ref):
        o_ref[...] = (g_ref[...] + got_ref[...]).astype(o_ref.dtype)

    return pl.pallas_call(
        body,
        name="chip_sums",
        grid_spec=pltpu.PrefetchScalarGridSpec(
            num_scalar_prefetch=1,
            grid=(n, nt),
            in_specs=[pl.BlockSpec((1, PACK_TILE, w), lambda a, i, wh: (a, wh[0] * nt + i, 0)),
                      pl.BlockSpec((1, PACK_TILE, w), lambda a, i, wh: (a, i, 0))],
            out_specs=pl.BlockSpec((1, PACK_TILE, w), lambda a, i, wh: (a, i, 0)),
        ),
        out_shape=jax.ShapeDtypeStruct((n, rh, w), _TD),
        compiler_params=_cp(("parallel", "parallel")),
    )(where, g, got)


def _exchange_chip_sums(h):
    n, rh, w = h.shape

    def body(h_ref, o_ref, send_sems, recv_sems):
        x, y, c = _place()
        j = 2 * x + y
        copies = []
        for k in range(1, N_CHIPS):
            cx, cy = _rel_chip(x, y, k)
            copies.append(pltpu.make_async_remote_copy(
                src_ref=h_ref.at[2 * cx + cy], dst_ref=o_ref.at[k - 1], send_sem=send_sems.at[k - 1], recv_sem=recv_sems.at[k - 1],
                device_id=(cx, cy, c), device_id_type=MESH))
        for cp in copies:
            cp.start()
        for cp in copies:
            cp.wait()

    return pl.pallas_call(
        body,
        name="chip_sums_exchange",
        in_specs=[ANY],
        out_specs=ANY,
        out_shape=jax.ShapeDtypeStruct((N_CHIPS - 1, rh, w), h.dtype),
        scratch_shapes=[pltpu.SemaphoreType.DMA((3,)), pltpu.SemaphoreType.DMA((3,))],
        compiler_params=pltpu.CompilerParams(has_side_effects=True),
    )(h)


def _shard_sum(g, got, others, where):
    n, r, w = g.shape
    rh = r // 2
    nt = rh // PACK_TILE

    def body(where_ref, g_ref, got_ref, oth_ref, o_ref):
        acc = g_ref[0] + got_ref[0]
        for k in range(N_CHIPS - 1):
            acc = acc + oth_ref[k].astype(F32)
        o_ref[...] = acc

    return pl.pallas_call(
        body,
        name="shard_sum",
        grid_spec=pltpu.PrefetchScalarGridSpec(
            num_scalar_prefetch=1,
            grid=(nt,),
            in_specs=[pl.BlockSpec((1, PACK_TILE, w), lambda i, wh: (wh[1], wh[0] * nt + i, 0)),
                      pl.BlockSpec((1, PACK_TILE, w), lambda i, wh: (wh[1], i, 0)),
                      pl.BlockSpec((N_CHIPS - 1, PACK_TILE, w), lambda i, wh: (0, i, 0))],
            out_specs=pl.BlockSpec((PACK_TILE, w), lambda i, wh: (i, 0)),
        ),
        out_shape=jax.ShapeDtypeStruct((rh, w), F32),
        compiler_params=_cp(("parallel",)),
    )(where, g, got, others)


def _join_halves(e):
    rh, w = e.shape

    def body(e_ref, o_ref, send_sem, recv_sem, local_sem):
        x, y, c = _place()
        rows = _half(c, rh)
        mine = pltpu.make_async_copy(e_ref, o_ref.at[rows], local_sem)
        mine.start()
        cp = pltpu.make_async_remote_copy(
            src_ref=e_ref, dst_ref=o_ref.at[rows], send_sem=send_sem, recv_sem=recv_sem, device_id=(x, y, 1 - c), device_id_type=MESH)
        cp.start()
        cp.wait()
        mine.wait()

    return pl.pallas_call(
        body,
        name="shard_to_sibling",
        in_specs=[ANY],
        out_specs=ANY,
        out_shape=jax.ShapeDtypeStruct((2 * rh, w), e.dtype),
        scratch_shapes=[pltpu.SemaphoreType.DMA, pltpu.SemaphoreType.DMA, pltpu.SemaphoreType.DMA],
        compiler_params=pltpu.CompilerParams(has_side_effects=True),
    )(e)


HBM = pl.BlockSpec(memory_space=pltpu.HBM)
SEM = pl.BlockSpec(memory_space=pltpu.SEMAPHORE)
EFFECT = pltpu.SideEffectType.DATAFLOW_SIDE_EFFECTING


def _in_hbm(a):
    return pltpu.with_memory_space_constraint(a, pltpu.HBM)


def _gather_copies(p_ref, land_ref, send_sems, recv_sems):
    rh = p_ref.shape[0] // 2
    x, y, c = _place()
    rows = _half(c, rh)
    copies = []
    for k in range(1, N_CHIPS):
        cx, cy = _rel_chip(x, y, k)
        copies.append(pltpu.make_async_remote_copy(
            src_ref=p_ref.at[rows], dst_ref=land_ref.at[2 * x + y, rows], send_sem=send_sems.at[k - 1], recv_sem=recv_sems.at[k - 1],
            device_id=(cx, cy, c), device_id_type=MESH))
    copies.append(pltpu.make_async_remote_copy(
        src_ref=p_ref, dst_ref=land_ref.at[2 * x + y], send_sem=send_sems.at[N_CHIPS - 1], recv_sem=recv_sems.at[N_CHIPS - 1],
        device_id=(x, y, 1 - c), device_id_type=MESH))
    return copies


def _gather_start(pack, after, *, name):
    r, w = pack.shape

    def body(p_ref, land_ref, after_ref, send_sems, recv_sems, p_thru, land_thru, token):
        for cp in _gather_copies(p_ref, land_ref, send_sems, recv_sems):
            cp.start()
        token[...] = jnp.zeros_like(token)

    return pl.pallas_call(
        body,
        name=name,
        out_shape=(pltpu.SemaphoreType.DMA((N_CHIPS,)), pltpu.SemaphoreType.DMA((N_CHIPS,)), pltpu.HBM((r, w), pack.dtype),
                   pltpu.HBM((N_CHIPS, r, w), pack.dtype), jax.ShapeDtypeStruct((8, 128), F32)),
        in_specs=(HBM, HBM, ANY),
        out_specs=(SEM, SEM, HBM, HBM, pl.BlockSpec(memory_space=pltpu.VMEM)),
        input_output_aliases={0: 2, 1: 3},
        compiler_params=pltpu.CompilerParams(has_side_effects=EFFECT),
    )(_in_hbm(pack), _in_hbm(lax.empty((N_CHIPS, r, w), pack.dtype)), after)


def _gather_wait(send_sems, recv_sems, pack, land, after, *, name):
    def body(p_ref, land_ref, send_sems, recv_sems, after_ref, p_out, land_out):
        for cp in _gather_copies(p_ref, land_ref, send_sems, recv_sems):
            cp.wait_send()
            cp.wait_recv()

    return pl.pallas_call(
        body,
        name=name,
        out_shape=(pltpu.HBM(pack.shape, pack.dtype), pltpu.HBM(land.shape, land.dtype)),
        in_specs=(HBM, HBM, SEM, SEM, ANY),
        out_specs=(HBM, HBM),
        input_output_aliases={0: 0, 1: 1},
        compiler_params=pltpu.CompilerParams(has_side_effects=EFFECT),
    )(pack, land, send_sems, recv_sems, after)


def _gather_spread(land, *, name):
    n, r, w = land.shape
    rh = r // 2

    def body(land_ref, o_ref, send_sems, recv_sems):
        x, y, c = _place()
        rows = _half(c, rh)
        copies = []
        for k in range(1, N_CHIPS):
            cx, cy = _rel_chip(x, y, k)
            copies.append(pltpu.make_async_remote_copy(
                src_ref=land_ref.at[2 * cx + cy, rows], dst_ref=o_ref.at[2 * cx + cy, rows], send_sem=send_sems.at[k - 1],
                recv_sem=recv_sems.at[k - 1], device_id=(x, y, 1 - c), device_id_type=MESH))
        for cp in copies:
            cp.start()
        for cp in copies:
            cp.wait()

    return pl.pallas_call(
        body,
        name=name,
        in_specs=[ANY],
        out_specs=ANY,
        out_shape=jax.ShapeDtypeStruct(land.shape, land.dtype),
        input_output_aliases={0: 0},
        scratch_shapes=[pltpu.SemaphoreType.DMA((N_CHIPS - 1,)), pltpu.SemaphoreType.DMA((N_CHIPS - 1,))],
        compiler_params=pltpu.CompilerParams(has_side_effects=True),
    )(land)


N_PARTS = 2 * (N_CHIPS - 1)


def _scatter_copies(lo_ref, g_ref, land_lo_ref, land_f_ref, send_sems, recv_sems, starting):
    rh = g_ref.shape[1] // 2
    x, y, c = _place()
    copies = []
    for k in range(1, N_CHIPS):
        cx, cy = _rel_chip(x, y, k)
        for i in range(2):
            part = 2 * (k - 1) + (c if starting else i)
            copies.append(pltpu.make_async_remote_copy(
                src_ref=lo_ref.at[2 * cx + cy, pl.ds(i * rh, rh)], dst_ref=land_lo_ref.at[part],
                send_sem=send_sems.at[2 * (k - 1) + i], recv_sem=recv_sems.at[part], device_id=(cx, cy, i), device_id_type=MESH))
    copies.append(pltpu.make_async_remote_copy(
        src_ref=g_ref.at[2 * x + y, _half(1 - c, rh)], dst_ref=land_f_ref, send_sem=send_sems.at[N_PARTS], recv_sem=recv_sems.at[N_PARTS],
        device_id=(x, y, 1 - c), device_id_type=MESH))
    return copies


def _scatter_start(g_lo, g, *, name):
    n, r, w = g.shape
    rh = r // 2

    def body(lo_ref, g_ref, land_lo_ref, land_f_ref, send_sems, recv_sems, lo_thru, g_thru, land_lo_thru, land_f_thru, token):
        for cp in _scatter_copies(lo_ref, g_ref, land_lo_ref, land_f_ref, send_sems, recv_sems, True):
            cp.start()
        token[...] = jnp.zeros_like(token)

    return pl.pallas_call(
        body,
        name=name,
        out_shape=(pltpu.SemaphoreType.DMA((N_PARTS + 1,)), pltpu.SemaphoreType.DMA((N_PARTS + 1,)), pltpu.HBM(g_lo.shape, g_lo.dtype),
                   pltpu.HBM(g.shape, g.dtype), pltpu.HBM((N_PARTS, rh, w), g_lo.dtype), pltpu.HBM((rh, w), g.dtype),
                   jax.ShapeDtypeStruct((8, 128), F32)),
        in_specs=(HBM, HBM, HBM, HBM),
        out_specs=(SEM, SEM, HBM, HBM, HBM, HBM, pl.BlockSpec(memory_space=pltpu.VMEM)),
        input_output_aliases={0: 2, 1: 3, 2: 4, 3: 5},
        compiler_params=pltpu.CompilerParams(has_side_effects=EFFECT),
    )(_in_hbm(g_lo), _in_hbm(g), _in_hbm(lax.empty((N_PARTS, rh, w), g_lo.dtype)), _in_hbm(lax.empty((rh, w), g.dtype)))


def _scatter_wait(send_sems, recv_sems, g_lo, g, land_lo, land_f, after, *, name):
    def body(lo_ref, g_ref, land_lo_ref, land_f_ref, send_sems, recv_sems, after_ref, o0, o1, o2, o3):
        for cp in _scatter_copies(lo_ref, g_ref, land_lo_ref, land_f_ref, send_sems, recv_sems, False):
            cp.wait_send()
            cp.wait_recv()

    arrays = (g_lo, g, land_lo, land_f)
    return pl.pallas_call(
        body,
        name=name,
        out_shape=tuple(pltpu.HBM(a.shape, a.dtype) for a in arrays),
        in_specs=(HBM, HBM, HBM, HBM, SEM, SEM, ANY),
        out_specs=(HBM, HBM, HBM, HBM),
        input_output_aliases={0: 0, 1: 1, 2: 2, 3: 3},
        compiler_params=pltpu.CompilerParams(has_side_effects=EFFECT),
    )(*arrays, send_sems, recv_sems, after)


def _scatter_sum(g, land_lo, land_f, where, *, name):
    n, r, w = g.shape
    rh = r // 2
    tr = _pick(rh, (256, 160, 80))
    nt = rh // tr

    def body(where_ref, g_ref, f_ref, lo_ref, o_ref):
        acc = g_ref[0] + f_ref[...]
        for part in range(N_PARTS):
            acc = acc + lo_ref[part].astype(F32)
        o_ref[...] = acc

    return pl.pallas_call(
        body,
        name=name,
        grid_spec=pltpu.PrefetchScalarGridSpec(
            num_scalar_prefetch=1,
            grid=(nt,),
            in_specs=[pl.BlockSpec((1, tr, w), lambda i, wh: (wh[1], wh[0] * nt + i, 0)),
                      pl.BlockSpec((tr, w), lambda i, wh: (i, 0)),
                      pl.BlockSpec((N_PARTS, tr, w), lambda i, wh: (0, i, 0))],
            out_specs=pl.BlockSpec((tr, w), lambda i, wh: (wh[0] * nt + i, 0)),
        ),
        out_shape=jax.ShapeDtypeStruct((r, w), F32),
        compiler_params=_cp(("parallel",)),
    )(where, g, land_f, land_lo)


def _swap_all(shards, *, name):
    n = len(shards)

    def body(*refs):
        ins, outs = refs[:n], refs[n : 2 * n]
        send_sems, recv_sems = refs[2 * n :]
        x, y, c = _place()
        copies = []
        for i, (e_ref, o_ref) in enumerate(zip(ins, outs)):
            rows = _half(c, e_ref.shape[0] // 2)
            copies.append(pltpu.make_async_remote_copy(src_ref=e_ref.at[rows], dst_ref=o_ref.at[rows], send_sem=send_sems.at[i],
                                                       recv_sem=recv_sems.at[i], device_id=(x, y, 1 - c), device_id_type=MESH))
        for cp in copies:
            cp.start()
        for cp in copies:
            cp.wait()

    return pl.pallas_call(
        body,
        name=name,
        in_specs=[ANY] * n,
        out_specs=[ANY] * n,
        out_shape=[jax.ShapeDtypeStruct(e.shape, e.dtype) for e in shards],
        input_output_aliases={i: i for i in range(n)},
        scratch_shapes=[pltpu.SemaphoreType.DMA((n,)), pltpu.SemaphoreType.DMA((n,))],
        compiler_params=pltpu.CompilerParams(has_side_effects=True),
    )(*shards)


def _sum_small(small, after):
    n_dev = 8

    def body(s_ref, after_ref, o_ref, all_ref, send_sems, recv_sems):
        x, y, c = _place()
        me = 4 * x + 2 * y + c
        all_ref[me] = s_ref[...]
        copies = []
        for k in range(1, n_dev):
            cx, cy = _rel_chip(x, y, k >> 1)
            cc = 1 - c if k & 1 else c
            copies.append(pltpu.make_async_remote_copy(
                src_ref=s_ref, dst_ref=all_ref.at[me], send_sem=send_sems.at[k - 1], recv_sem=recv_sems.at[k - 1],
                device_id=(cx, cy, cc), device_id_type=MESH))
        for cp in copies:
            cp.start()
        for cp in copies:
            cp.wait()
        acc = all_ref[0]
        for a in range(1, n_dev):
            acc = acc + all_ref[a]
        o_ref[...] = acc

    vm = pl.BlockSpec(memory_space=pltpu.VMEM)
    return pl.pallas_call(
        body,
        name="sum_small",
        in_specs=[vm, ANY],
        out_specs=vm,
        out_shape=jax.ShapeDtypeStruct(small.shape, F32),
        scratch_shapes=[pltpu.VMEM((n_dev,) + small.shape, F32), pltpu.SemaphoreType.DMA((n_dev - 1,)), pltpu.SemaphoreType.DMA((n_dev - 1,))],
        compiler_params=pltpu.CompilerParams(has_side_effects=True),
    )(small, after)


MATS = {"w_in": (776, True), "w_out": (256, False), "w_xq": (256, False), "w_xkv": (512, True), "w_xo": (256, False),
        "w_up": (1024, True), "w_down": (1024, False)}
GATHER_FIRST = ("w_in",)
GATHER_REST = ("w_out", "w_xq", "w_xkv", "w_xo", "w_up", "w_down")
GRAD_GROUPS = (("w_up", "w_down"), ("w_out", "w_xq", "w_xkv", "w_xo"), ("w_in",))


def _group_rows(names):
    n = sum(MATS[name][0] for name in names)
    return n + (-n) % 32


def _pack(pieces, rows):
    p = jnp.concatenate(pieces, axis=0) if len(pieces) > 1 else pieces[0]
    return jnp.pad(p, ((0, rows - p.shape[0]), (0, 0))) if rows > p.shape[0] else p


def _unpack(rows, names):
    out, off = {}, 0
    for name in names:
        out[name] = rows[off : off + MATS[name][0]]
        off += MATS[name][0]
    return out


SMALL = (
    ("mix_norm", 1024), ("conv_norm", 512), ("b_af", 256), ("b_ab", 256), ("gla_norm", 128), ("xa_norm", 1024), ("mem_norm", 1024),
    ("mlp_norm", 1024), ("final_norm", 1024), ("conv_w", 1536), ("w_af", 4096), ("w_ab", 4096), ("loss", 128),
)


def kernel(x, mem, mix_norm, w_in, conv_w, conv_norm, w_af, b_af, w_ab, b_ab, gla_norm, w_out, xa_norm, mem_norm, w_xq, w_xkv, w_xo, mlp_norm, w_up, w_down, final_norm, loss_target, m_mix_norm, m_w_in, m_conv_w, m_conv_norm, m_w_af, m_b_af, m_w_ab, m_b_ab, m_gla_norm, m_w_out, m_xa_norm, m_mem_norm, m_w_xq, m_w_xkv, m_w_xo, m_mlp_norm, m_w_up, m_w_down, m_final_norm, v_mix_norm, v_w_in, v_conv_w, v_conv_norm, v_w_af, v_b_af, v_w_ab, v_b_ab, v_gla_norm, v_w_out, v_xa_norm, v_mem_norm, v_w_xq, v_w_xkv, v_w_xo, v_mlp_norm, v_w_up, v_w_down, v_final_norm):
    given = dict(locals())
    xi, yi, ci = _place()
    chip = 2 * xi + yi
    where = jnp.stack([ci, chip]).astype(jnp.int32)

    lo = {name: (given[name][0].T if MATS[name][1] else given[name][0]).astype(_CD) for name in MATS}
    pack_rest = _pack([lo[name] for name in GATHER_REST], _group_rows(GATHER_REST))
    pack_first = _pack([lo[name] for name in GATHER_FIRST], _group_rows(GATHER_FIRST))
    xs, mems, tgt = x[0], mem[0], loss_target[0]
    add_res = lambda acc, res: (acc + res,)
    behind = lambda gain, token: gain + token[0, 0]

    def placed(shard, full_shape, col):
        return lax.dynamic_update_slice(jnp.zeros(full_shape, F32), shard, (0, col)).reshape(-1, 128)

    sw = jnp.concatenate([
        placed(conv_w[0], (CONV_K, CONV_WIDTH), 128 * chip),
        placed(w_af[0], (GLA_LOWRANK, GLA_K_TOTAL), 64 * chip),
        placed(w_ab[0], (GLA_LOWRANK, GLA_K_TOTAL), 64 * chip),
    ], axis=0)
    sw = jnp.pad(sw, ((0, SMALL_ROWS - sw.shape[0]), (0, 0))) * (ci == 0).astype(F32)
    sw = _sum_small(sw, mix_norm)

    first_send, first_recv, pack_first, land_first, first_token = _gather_start(pack_first, sw, name="gather_first_start")
    rest_send, rest_recv, pack_rest, land_rest, rest_token = _gather_start(pack_rest, first_token, name="gather_rest_start")
    h1 = _rms_fwd(xs, behind(mix_norm, rest_token), name="norm_mix")
    pack_first, land_first = _gather_wait(first_send, first_recv, pack_first, land_first, h1, name="gather_first_wait")
    got_first = _gather_spread(land_first, name="gather_first_spread")

    def whole(got, off, rows):
        return got[:, off : off + rows].reshape(N_CHIPS * rows, D_MODEL)

    w_in_t = whole(got_first, 0, MATS["w_in"][0])
    w_za = jnp.concatenate([w_in_t[0:1536], w_in_t[2560:3072]], axis=0)
    w_zb = jnp.concatenate([w_in_t[1536:2560], w_in_t[3072:W_IN_COLS], jnp.zeros((ZB_COLS - 1056, D_MODEL), _CD)], axis=0)
    conv_w_full = sw[0:12].reshape(CONV_K, CONV_WIDTH)
    w_af_full = sw[12:44].reshape(GLA_LOWRANK, GLA_K_TOTAL)
    w_ab_full = sw[44:76].reshape(GLA_LOWRANK, GLA_K_TOTAL)
    waf_p = jnp.pad(w_af_full, ((0, 128 - GLA_LOWRANK), (0, 0))).astype(_CD)
    wab_p = jnp.pad(w_ab_full, ((GLA_LOWRANK, 128 - 2 * GLA_LOWRANK), (0, 0))).astype(_CD)

    z_b = _mm(h1, w_zb, mode="nt", name="proj_in_b", tn=ZB_COLS)
    z_a = _mm(h1, w_za, mode="nt", name="proj_in_a", tm=512, tn=ZA_COLS)
    b_f, b_b = _gate_fwd(z_b, waf_p, wab_p, b_af, b_ab, name="gates")
    o_f, st_f, o_b, st_b = _gla_fwd(z_b, b_f, b_b, name="gla_scan")
    y = _mix_fwd(z_a, o_f, o_b, conv_w_full, conv_norm, gla_norm, name="mix_out")
    pack_rest, land_rest = _gather_wait(rest_send, rest_recv, pack_rest, land_rest, y, name="gather_rest_wait")
    gathered = _gather_spread(land_rest, name="gather_rest_spread")
    wt, off = {}, 0
    for name in GATHER_REST:
        wt[name] = whole(gathered, off, MATS[name][0])
        off += MATS[name][0]
    x1, hx = _mm_rows(y, wt["w_out"], mode="nn", name="proj_out", rows=(xs,), vecs=(xa_norm,), out_rows=(F32, _CD), epilogue=_ep_residual_norm)
    qx = _mm(hx, wt["w_xq"], mode="nn", name="proj_xq", out_dtypes=(_CD,))
    hmem = _rms_fwd(mems, mem_norm, name="norm_mem")
    kv = _mm(hmem, wt["w_xkv"], mode="nt", name="proj_xkv", out_dtypes=(_CD,))
    ox = _xattn_fwd(qx, kv, name="xattn")
    x2, hm = _mm_rows(ox, wt["w_xo"], mode="nn", name="proj_xo", rows=(x1,), vecs=(mlp_norm,), out_rows=(F32, _CD), epilogue=_ep_residual_norm)
    act, relu_u = _mm(hm, wt["w_up"], mode="nt", name="mlp_up", out_dtypes=(_CD, _CD), tm=2048,
                      epilogue=lambda acc: (jnp.square(jnp.maximum(acc, 0.0)), jnp.maximum(acc, 0.0)))
    dx3, dx3_lo, loss_part, g_final_norm = _mm_rows(
        act, wt["w_down"], mode="nn", name="mlp_down", rows=(x2, tgt), vecs=(final_norm.reshape(1, D_MODEL),),
        out_rows=(F32, _CD), out_vecs=(128, D_MODEL), epilogue=_ep_loss)

    grads_t = {}

    def start_group(names, tag):
        rows = _group_rows(names)
        g = jnp.stack([_pack([grads_t[name][a * MATS[name][0] : (a + 1) * MATS[name][0]] for name in names], rows) for a in range(N_CHIPS)])
        return _scatter_start(g.astype(_TD), g, name="grads_" + tag + "_start")

    def finish_group(state, after, tag):
        send_sems, recv_sems, g_lo, g, land_lo, land_f, _ = state
        g_lo, g, land_lo, land_f = _scatter_wait(send_sems, recv_sems, g_lo, g, land_lo, land_f, after, name="grads_" + tag + "_wait")
        return _scatter_sum(g, land_lo, land_f, where, name="grads_" + tag + "_sum")

    def new_packs(names):
        shape = (N_CHIPS, _group_rows(names), D_MODEL)
        return lax.empty(shape, F32), lax.empty(shape, _TD)

    def grad_into(packs, names, which, a, b, name):
        off = sum(MATS[other][0] for other in names[: names.index(which)])
        return _mm_tn_into(a, b, packs, rows=MATS[which][0], off=off, name=name)

    du = _mm(dx3_lo, wt["w_down"], mode="nt", name="mlp_down_dx", out_dtypes=(_CD,), extras=(relu_u,), tm=2048,
             epilogue=lambda acc, rr: (acc * (2.0 * rr.astype(F32)),))
    packs = new_packs(GRAD_GROUPS[0])
    packs = grad_into(packs, GRAD_GROUPS[0], "w_down", act, dx3_lo, "mlp_down_dw")
    packs = grad_into(packs, GRAD_GROUPS[0], "w_up", du, hm, "mlp_up_dw")
    mlp_state = _scatter_start(packs[1], packs[0], name="grads_mlp_start")
    dx2, dx2_lo, g_mlp_norm = _mm_rows(
        du, wt["w_up"], mode="nn", name="mlp_up_dx", rows=(x2, dx3), vecs=(behind(mlp_norm, mlp_state[-1]),),
        out_rows=(F32, _CD), out_vecs=(D_MODEL,), epilogue=_ep_norm_bwd)
    dox = _mm(dx2_lo, wt["w_xo"], mode="nt", name="proj_xo_dx", out_dtypes=(_CD,))
    packs = new_packs(GRAD_GROUPS[1])
    packs = grad_into(packs, GRAD_GROUPS[1], "w_xo", ox, dx2_lo, "proj_xo_dw")
    dqx, dkv = _xattn_bwd(qx, kv, dox, name="xattn_bwd")
    packs = grad_into(packs, GRAD_GROUPS[1], "w_xq", hx, dqx, "proj_xq_dw")
    dx1, dx1_lo, g_xa_norm = _mm_rows(
        dqx, wt["w_xq"], mode="nt", name="proj_xq_dx", rows=(x1, dx2), vecs=(xa_norm,),
        out_rows=(F32, _CD), out_vecs=(D_MODEL,), epilogue=_ep_norm_bwd)
    dkv_lo = dkv.astype(_CD)
    packs = grad_into(packs, GRAD_GROUPS[1], "w_xkv", dkv_lo, hmem, "proj_xkv_dw")
    dhmem = _mm(dkv_lo, wt["w_xkv"], mode="nn", name="proj_xkv_dx")
    (g_mem_norm,) = _rms_bwd(mems, mem_norm, dhmem, name="norm_mem_bwd", want_dx=False, want_lo=False)
    dy = _mm(dx1_lo, wt["w_out"], mode="nt", name="proj_out_dx")
    packs = grad_into(packs, GRAD_GROUPS[1], "w_out", y, dx1_lo, "proj_out_dw")
    attn_state = _scatter_start(packs[1], packs[0], name="grads_attn_start")
    dz_a, do, g_conv_w, g_conv_norm, g_gla_norm = _mix_bwd(z_a, o_f, o_b, dy, conv_w_full, behind(conv_norm, attn_state[-1]), gla_norm, name="mix_out_bwd")
    dqkv_f, db_f, dqkv_b, db_b = _gla_bwd(z_b, b_f, b_b, do, st_f, st_b, name="gla_scan_bwd")
    dz_b, g_waf_p, g_wab_p, g_b_af, g_b_ab = _gate_bwd(z_b, waf_p, wab_p, b_af, b_ab, db_f, db_b, dqkv_f, dqkv_b, name="gates_bwd")
    g_za = _mm_tn(dz_a, h1, name="proj_in_a_dw")
    g_zb = _mm_tn(dz_b, h1, name="proj_in_b_dw")
    grads_t["w_in"] = jnp.concatenate([g_za[0:1536], g_zb[0:1024], g_za[1536:2048], g_zb[1024:1056]], axis=0)
    in_state = start_group(GRAD_GROUPS[2], "in")
    dh1_a = _mm(dz_a, w_za, mode="nn", name="proj_in_a_dx", tm=512, tk=ZA_COLS)
    grad_x, g_mix_norm = _mm_rows(
        dz_b, w_zb, mode="nn", name="proj_in_b_dx", rows=(xs, dx1, dh1_a), vecs=(behind(mix_norm, in_state[-1]),),
        out_rows=(F32,), out_vecs=(D_MODEL,), epilogue=_ep_norm_bwd)

    half_mlp = finish_group(mlp_state, grad_x, "mlp")
    half_attn = finish_group(attn_state, half_mlp, "attn")
    half_in = finish_group(in_state, half_attn, "in")
    shard_rows = {}
    for names, rows in zip(GRAD_GROUPS, _swap_all([half_mlp, half_attn, half_in], name="shards_to_sibling")):
        off = 0
        for name in names:
            shard_rows[name] = (rows, off)
            off += MATS[name][0]

    small_vals = dict(mix_norm=g_mix_norm, conv_norm=g_conv_norm, b_af=g_b_af, b_ab=g_b_ab, gla_norm=g_gla_norm, xa_norm=g_xa_norm,
                      mem_norm=g_mem_norm, mlp_norm=g_mlp_norm, final_norm=g_final_norm, conv_w=g_conv_w,
                      w_af=g_waf_p[0:GLA_LOWRANK], w_ab=g_wab_p[GLA_LOWRANK : 2 * GLA_LOWRANK], loss=loss_part)
    small = jnp.concatenate([small_vals[name].reshape(-1, 128) for name, _ in SMALL], axis=0)
    small = _sum_small(jnp.pad(small, ((0, SMALL_ROWS - small.shape[0]), (0, 0))), loss_part)
    g_small, off = {}, 0
    for name, n in SMALL:
        g_small[name] = small[off : off + n // 128]
        off += n // 128
    loss = g_small["loss"][0, 0]
    g_small["conv_w"] = lax.dynamic_slice(g_small["conv_w"].reshape(CONV_K, CONV_WIDTH), (0, 128 * chip), (CONV_K, 128))
    g_small["w_af"] = lax.dynamic_slice(g_small["w_af"].reshape(GLA_LOWRANK, GLA_K_TOTAL), (0, 64 * chip), (GLA_LOWRANK, 64))
    g_small["w_ab"] = lax.dynamic_slice(g_small["w_ab"].reshape(GLA_LOWRANK, GLA_K_TOTAL), (0, 64 * chip), (GLA_LOWRANK, 64))

    names = ["mix_norm", "w_in", "conv_w", "conv_norm", "w_af", "b_af", "w_ab", "b_ab", "gla_norm", "w_out", "xa_norm", "mem_norm",
             "w_xq", "w_xkv", "w_xo", "mlp_norm", "w_up", "w_down", "final_norm"]
    big_names = list(MATS)
    as2d = lambda a: a.reshape(1, -1) if a.ndim == 1 else a.reshape(a.shape[-2:])
    grads, deltas, new_m, new_v = {}, {}, {}, {}
    for name in big_names:
        rows, off = shard_rows[name]
        wmv = [as2d(given[name]), as2d(given["m_" + name]), as2d(given["v_" + name])]
        as_stored = name == "w_in"
        if as_stored:
            wmv = [a.T for a in wmv]
        res = _adamw(*wmv, rows, off, transposed=MATS[name][1] and not as_stored, name="adamw_" + name)
        grads[name], deltas[name], new_m[name], new_v[name] = [a.T for a in res] if as_stored else res
    small_names = [name for name in names if name not in big_names]
    groups = []
    for name in small_names:
        grads[name] = g_small[name].reshape(as2d(given[name]).shape)
        groups.append((as2d(given[name]), grads[name], as2d(given["m_" + name]), as2d(given["v_" + name])))
    for name, res in zip(small_names, _adamw_small(groups, name="adamw_small")):
        deltas[name], new_m[name], new_v[name] = res

    like = lambda name, a: a.reshape(given[name].shape)
    return (loss, grad_x[None], *[like(n, grads[n]) for n in names], *[like(n, deltas[n]) for n in names],
            *[like(n, new_m[n]) for n in names], *[like(n, new_v[n]) for n in names])
```

```python
import jax
import jax.numpy as jnp
from jax import lax
from jax.experimental import pallas as pl
from jax.experimental.pallas import tpu as pltpu

F32 = jnp.float32
BF16 = jnp.bfloat16
_CD = jnp.bfloat16
_TD = jnp.bfloat16

D_MODEL = 1024
N_MEM = 256
CONV_WIDTH = 512
CONV_GROUP = 64
CONV_K = 3
GLA_HEADS = 4
GLA_DK = 64
GLA_DV = 128
GLA_K_TOTAL = 256
GLA_V_TOTAL = 512
GLA_LOWRANK = 16
GLA_GATE_SCALE = 1.0 / 16.0
GLA_CHUNK = 64
XA_HEADS = 4
XA_HEAD_DIM = 256
D_FF = 4096
EPS = 1e-6
W_IN_COLS = 3104
ZA_COLS = 2048
ZB_COLS = 1152
LR_COL = 1024

ADAM_LR = 0.001
ADAM_B1 = 0.9
ADAM_B2 = 0.999
ADAM_EPS = 1e-08
ADAM_WD = 0.01
ADAM_STEP = 10

N_CHIPS = 4
SMALL_ROWS = 128

_TS = 512
_VMEM = 44 * 1024 * 1024
MESH = pl.DeviceIdType.MESH
ANY = pl.BlockSpec(memory_space=pl.ANY)


def _cp(sem=None, **kw):
    return pltpu.CompilerParams(dimension_semantics=sem, vmem_limit_bytes=_VMEM, **kw)


def _dot(a, b):
    return jnp.dot(a.astype(_CD), b.astype(_CD), preferred_element_type=F32)


def _dot_nt(a, b):
    return lax.dot_general(a.astype(_CD), b.astype(_CD), (((1,), (1,)), ((), ())), preferred_element_type=F32)


def _dot_tn(a, b):
    return lax.dot_general(a.astype(_CD), b.astype(_CD), (((0,), (0,)), ((), ())), preferred_element_type=F32)


def _dot_split(x, ones):
    hi = x.astype(BF16)
    r = x - hi.astype(F32)
    mid = r.astype(BF16)
    lo = (r - mid.astype(F32)).astype(BF16)
    d = lambda p: jnp.dot(p, ones, preferred_element_type=F32)
    return d(hi) + d(mid) + d(lo)


def _pick(n, cands=(1024, 640, 512, 256, 128)):
    for t in cands:
        if n % t == 0:
            return t
    return n


def _rows(s):
    return min(_TS, s)


def _sigmoid(v):
    e = jnp.exp(-jnp.abs(v))
    return jnp.where(v >= 0, 1.0 / (1.0 + e), e / (1.0 + e))


def _mm(a, b, *, mode, name, out_dtypes=(F32,), extras=(), epilogue=None, tm=None, tn=None, tk=None):
    m, k = a.shape
    n = b.shape[1] if mode == "nn" else b.shape[0]
    tm = min(m, tm or 1024)
    tn = tn or _pick(n)
    tk = tk or _pick(k)
    nk = k // tk
    n_ex, n_out = len(extras), len(out_dtypes)

    def body(*refs):
        a_ref, b_ref = refs[:2]
        ex = refs[2 : 2 + n_ex]
        outs = refs[2 + n_ex : 2 + n_ex + n_out]
        part = _dot(a_ref[...], b_ref[...]) if mode == "nn" else _dot_nt(a_ref[...], b_ref[...])

        def finish(acc):
            res = epilogue(acc, *[e[...] for e in ex]) if epilogue else (acc,)
            for o, r in zip(outs, res):
                o[...] = r.astype(o.dtype)

        if nk == 1:
            finish(part)
        else:
            acc_ref = refs[-1]
            kk = pl.program_id(2)

            @pl.when(kk == 0)
            def _():
                acc_ref[...] = part

            @pl.when(kk > 0)
            def _():
                acc_ref[...] += part

            @pl.when(kk == nk - 1)
            def _():
                finish(acc_ref[...])

    b_spec = pl.BlockSpec((tk, tn), lambda i, j, kk: (kk, j)) if mode == "nn" else pl.BlockSpec((tn, tk), lambda i, j, kk: (j, kk))
    tile = pl.BlockSpec((tm, tn), lambda i, j, kk: (i, j))
    out = pl.pallas_call(
        body,
        name=name,
        grid=(m // tm, n // tn, nk),
        in_specs=[pl.BlockSpec((tm, tk), lambda i, j, kk: (i, kk)), b_spec] + [tile] * n_ex,
        out_specs=[tile] * n_out,
        out_shape=[jax.ShapeDtypeStruct((m, n), dt) for dt in out_dtypes],
        scratch_shapes=[pltpu.VMEM((tm, tn), F32)] if nk > 1 else [],
        compiler_params=_cp(("parallel", "parallel", "arbitrary")),
    )(a, b, *extras)
    return out[0] if n_out == 1 else out


def _mm_tn(a, b, *, name):
    s, m = a.shape
    n = b.shape[1]
    cap = max(128, (1 << 20) // n)
    tm = _pick(m, tuple(t for t in (512, 640, 384, 256, 128) if t <= max(cap, 128)))
    ts = min(s, 1 << (((1 << 22) // n).bit_length() - 1))
    ns = s // ts

    def body(a_ref, b_ref, o_ref):
        part = _dot_tn(a_ref[...], b_ref[...])
        if ns == 1:
            o_ref[...] = part
        else:
            ss = pl.program_id(1)

            @pl.when(ss == 0)
            def _():
                o_ref[...] = part

            @pl.when(ss > 0)
            def _():
                o_ref[...] += part

    return pl.pallas_call(
        body,
        name=name,
        grid=(m // tm, ns),
        in_specs=[pl.BlockSpec((ts, tm), lambda i, ss: (ss, i)), pl.BlockSpec((ts, n), lambda i, ss: (ss, 0))],
        out_specs=pl.BlockSpec((tm, n), lambda i, ss: (i, 0)),
        out_shape=jax.ShapeDtypeStruct((m, n), F32),
        compiler_params=_cp(("parallel", "arbitrary")),
    )(a, b)


def _mm_tn_into(a, b, packs, *, rows, off, name):
    s, m = a.shape
    n = b.shape[1]
    tm = 1024 if rows % 1024 == 0 and s >= 4096 else 512
    tr = min(tm, rows)
    per, chips = rows // tr, tm // tr
    ts = min(s, (1 << (((1 << 22) // n).bit_length() - 1)) * 512 // tm)
    ns = s // ts

    def body(a_ref, b_ref, f_in, lo_in, f_ref, lo_ref):
        part = _dot_tn(a_ref[...], b_ref[...])
        pieces = [part[c * tr : (c + 1) * tr] for c in range(chips)]
        if ns == 1:
            for c, p in enumerate(pieces):
                f_ref[c] = p
                lo_ref[c] = p.astype(lo_ref.dtype)
        else:
            ss = pl.program_id(1)

            @pl.when(ss == 0)
            def _():
                for c, p in enumerate(pieces):
                    f_ref[c] = p

            @pl.when(ss > 0)
            def _():
                for c, p in enumerate(pieces):
                    f_ref[c] += p

            @pl.when(ss == ns - 1)
            def _():
                lo_ref[...] = f_ref[...].astype(lo_ref.dtype)

    spec = pl.BlockSpec((chips, tr, n), lambda i, ss: (i // per, off // tr + i % per, 0))
    return pl.pallas_call(
        body,
        name=name,
        grid=(m // tm, ns),
        in_specs=[pl.BlockSpec((ts, tm), lambda i, ss: (ss, i)), pl.BlockSpec((ts, n), lambda i, ss: (ss, 0)), ANY, ANY],
        out_specs=[spec, spec],
        out_shape=[jax.ShapeDtypeStruct(p.shape, p.dtype) for p in packs],
        input_output_aliases={2: 0, 3: 1},
        compiler_params=_cp(("parallel", "arbitrary")),
    )(a, b, *packs)


def _mm_rows(a, b, *, mode, name, rows=(), vecs=(), out_rows=(), out_vecs=(), epilogue, tm=512):
    m, k = a.shape
    n = b.shape[1] if mode == "nn" else b.shape[0]
    tm = min(m, tm)
    parts = 2 if tm % 256 == 0 else 1
    n_r, n_v, n_or, n_ov = len(rows), len(vecs), len(out_rows), len(out_vecs)

    def body(*refs):
        a_ref, b_ref = refs[:2]
        r_refs = refs[2 : 2 + n_r]
        v_refs = refs[2 + n_r : 2 + n_r + n_v]
        or_refs = refs[2 + n_r + n_v : 2 + n_r + n_v + n_or]
        ov_refs = refs[2 + n_r + n_v + n_or :]
        res_vecs = None
        for p in range(parts):
            rs = slice(p * tm // parts, (p + 1) * tm // parts)
            acc = _dot(a_ref[rs, :], b_ref[...]) if mode == "nn" else _dot_nt(a_ref[rs, :], b_ref[...])
            res_rows, part_vecs = epilogue(acc, [r[rs, :] for r in r_refs], [v[...] for v in v_refs])
            for o, r in zip(or_refs, res_rows):
                o[rs, :] = r.astype(o.dtype)
            res_vecs = part_vecs if res_vecs is None else [s + t for s, t in zip(res_vecs, part_vecs)]
        if n_ov:
            first = pl.program_id(0) == 0

            @pl.when(first)
            def _():
                for o, r in zip(ov_refs, res_vecs):
                    o[...] = r

            @pl.when(jnp.logical_not(first))
            def _():
                for o, r in zip(ov_refs, res_vecs):
                    o[...] += r

    tile = pl.BlockSpec((tm, n), lambda i: (i, 0))
    whole = lambda arr: pl.BlockSpec(arr.shape, lambda i: (0, 0))
    vec = lambda w: pl.BlockSpec((1, w), lambda i: (0, 0))
    out = pl.pallas_call(
        body,
        name=name,
        grid=(m // tm,),
        in_specs=[pl.BlockSpec((tm, k), lambda i: (i, 0)), whole(b)] + [tile] * n_r + [vec(v.shape[1]) for v in vecs],
        out_specs=[tile] * n_or + [vec(w) for w in out_vecs],
        out_shape=[jax.ShapeDtypeStruct((m, n), dt) for dt in out_rows] + [jax.ShapeDtypeStruct((1, w), F32) for w in out_vecs],
        compiler_params=_cp(("arbitrary",) if n_ov else ("parallel",)),
    )(a, b, *rows, *vecs)
    return out


def _ep_residual_norm(acc, rows, vecs):
    x = acc + rows[0]
    r = lax.rsqrt(jnp.mean(x * x, axis=-1, keepdims=True) + EPS)
    return [x, x * r * vecs[0]], []


def _ep_norm_bwd(acc, rows, vecs):
    dy = acc
    for extra in rows[2:]:
        dy = dy + extra
    x, dres = rows[0], rows[1]
    r = lax.rsqrt(jnp.mean(x * x, axis=-1, keepdims=True) + EPS)
    xh = x * r
    dxh = dy * vecs[0]
    dx = r * (dxh - xh * jnp.mean(dxh * xh, axis=-1, keepdims=True)) + dres
    return [dx, dx], [jnp.sum(dy * xh, axis=0, keepdims=True)]


def _ep_loss(acc, rows, vecs):
    x = acc + rows[0]
    d = x.shape[-1]
    r = lax.rsqrt(jnp.mean(x * x, axis=-1, keepdims=True) + EPS)
    xh = x * r
    err = xh * vecs[0] - rows[1]
    loss = jnp.zeros((1, 128), F32) + 0.5 * jnp.sum(jnp.mean(err * err, axis=-1, keepdims=True))
    dy = err * (1.0 / d)
    dxh = dy * vecs[0]
    dx = r * (dxh - xh * jnp.mean(dxh * xh, axis=-1, keepdims=True))
    return [dx, dx], [loss, jnp.sum(dy * xh, axis=0, keepdims=True)]


def _rms_fwd(x, g, *, name):
    s, d = x.shape
    ts = _rows(s)

    def body(x_ref, g_ref, o_ref):
        xf = x_ref[...]
        r = lax.rsqrt(jnp.mean(xf * xf, axis=-1, keepdims=True) + EPS)
        o_ref[...] = (xf * r * g_ref[...]).astype(o_ref.dtype)

    return pl.pallas_call(
        body,
        name=name,
        grid=(s // ts,),
        in_specs=[pl.BlockSpec((ts, d), lambda i: (i, 0)), pl.BlockSpec((1, d), lambda i: (0, 0))],
        out_specs=pl.BlockSpec((ts, d), lambda i: (i, 0)),
        out_shape=jax.ShapeDtypeStruct((s, d), _CD),
        compiler_params=_cp(("parallel",)),
    )(x, g)


def _rms_gain_grad(x, dy, *, name):
    s, d = x.shape
    ts = _rows(s)

    def body(x_ref, dy_ref, dg_ref):
        xf = x_ref[...]
        r = lax.rsqrt(jnp.mean(xf * xf, axis=-1, keepdims=True) + EPS)
        part = jnp.sum(dy_ref[...] * (xf * r), axis=0, keepdims=True)

        @pl.when(pl.program_id(0) == 0)
        def _():
            dg_ref[...] = part

        @pl.when(pl.program_id(0) > 0)
        def _():
            dg_ref[...] += part

    tile = pl.BlockSpec((ts, d), lambda i: (i, 0))
    return pl.pallas_call(
        body,
        name=name,
        grid=(s // ts,),
        in_specs=[tile, tile],
        out_specs=pl.BlockSpec((1, d), lambda i: (0, 0)),
        out_shape=jax.ShapeDtypeStruct((1, d), F32),
        compiler_params=_cp(("arbitrary",)),
    )(x, dy)


def _chunk_scan(v, row_in_chunk, suffix):
    t = v.shape[0]
    step = 1
    while step < GLA_CHUNK:
        if suffix:
            v = v + jnp.where(row_in_chunk < GLA_CHUNK - step, pltpu.roll(v, t - step, 0), 0.0)
        else:
            v = v + jnp.where(row_in_chunk >= step, pltpu.roll(v, step, 0), 0.0)
        step *= 2
    return v


def _gate_pre(lr, w_ref, b_ref):
    return _dot(lr, w_ref[...]) + b_ref[...]


def _gate_fwd(z, waf, wab, baf, bab, *, name):
    s = z.shape[0]
    ts = _rows(s)

    def body(lr_ref, waf_ref, wab_ref, baf_ref, bab_ref, bf_ref, bb_ref):
        lr = lr_ref[...]
        ric = lax.broadcasted_iota(jnp.int32, (ts, GLA_K_TOTAL), 0) & (GLA_CHUNK - 1)
        for w_ref, b_ref, o_ref, suffix in ((waf_ref, baf_ref, bf_ref, False), (wab_ref, bab_ref, bb_ref, True)):
            pre = _gate_pre(lr, w_ref, b_ref)
            la = (jnp.minimum(pre, 0.0) - jnp.log(1.0 + jnp.exp(-jnp.abs(pre)))) * GLA_GATE_SCALE
            o_ref[...] = _chunk_scan(la, ric, suffix)

    wspec = pl.BlockSpec((128, GLA_K_TOTAL), lambda i: (0, 0))
    bspec = pl.BlockSpec((1, GLA_K_TOTAL), lambda i: (0, 0))
    tile = pl.BlockSpec((ts, GLA_K_TOTAL), lambda i: (i, 0))
    return pl.pallas_call(
        body,
        name=name,
        grid=(s // ts,),
        in_specs=[pl.BlockSpec((ts, 128), lambda i: (i, LR_COL // 128)), wspec, wspec, bspec, bspec],
        out_specs=[tile, tile],
        out_shape=[jax.ShapeDtypeStruct((s, GLA_K_TOTAL), F32)] * 2,
        compiler_params=_cp(("parallel",)),
    )(z, waf, wab, baf, bab)


def _gate_bwd(z, waf, wab, baf, bab, dbf, dbb, dqkv_f, dqkv_b, *, name):
    s = z.shape[0]
    ts = _rows(s)

    def body(lr_ref, waf_ref, wab_ref, baf_ref, bab_ref, dbf_ref, dbb_ref, gf_ref, gb_ref, dzb_ref, dwf_ref, dwb_ref, dbaf_ref, dbab_ref):
        lr = lr_ref[...]
        ric = lax.broadcasted_iota(jnp.int32, (ts, GLA_K_TOTAL), 0) & (GLA_CHUNK - 1)
        first = pl.program_id(0) == 0
        dlr = None
        for w_ref, b_ref, db_ref, dw_ref, dbias_ref, suffix in (
            (waf_ref, baf_ref, dbf_ref, dwf_ref, dbaf_ref, True),
            (wab_ref, bab_ref, dbb_ref, dwb_ref, dbab_ref, False),
        ):
            pre = _gate_pre(lr, w_ref, b_ref)
            dla = _chunk_scan(db_ref[...], ric, suffix)
            dpre = dla * GLA_GATE_SCALE * _sigmoid(-pre)
            part = _dot_nt(dpre, w_ref[...])
            dlr = part if dlr is None else dlr + part
            dw = _dot_tn(lr, dpre)
            dbias = jnp.sum(dpre, axis=0, keepdims=True)

            @pl.when(first)
            def _():
                dw_ref[...] = dw
                dbias_ref[...] = dbias

            @pl.when(jnp.logical_not(first))
            def _():
                dw_ref[...] += dw
                dbias_ref[...] += dbias

        dzb_ref[...] = jnp.concatenate([gf_ref[...] + gb_ref[...], dlr], axis=1).astype(dzb_ref.dtype)

    wspec = pl.BlockSpec((128, GLA_K_TOTAL), lambda i: (0, 0))
    bspec = pl.BlockSpec((1, GLA_K_TOTAL), lambda i: (0, 0))
    tile = pl.BlockSpec((ts, GLA_K_TOTAL), lambda i: (i, 0))
    wide = pl.BlockSpec((ts, 2 * GLA_K_TOTAL + GLA_V_TOTAL), lambda i: (i, 0))
    return pl.pallas_call(
        body,
        name=name,
        grid=(s // ts,),
        in_specs=[pl.BlockSpec((ts, 128), lambda i: (i, LR_COL // 128)), wspec, wspec, bspec, bspec, tile, tile, wide, wide],
        out_specs=[pl.BlockSpec((ts, ZB_COLS), lambda i: (i, 0)), wspec, wspec, bspec, bspec],
        out_shape=[
            jax.ShapeDtypeStruct((s, ZB_COLS), _CD),
            jax.ShapeDtypeStruct((128, GLA_K_TOTAL), F32),
            jax.ShapeDtypeStruct((128, GLA_K_TOTAL), F32),
            jax.ShapeDtypeStruct((1, GLA_K_TOTAL), F32),
            jax.ShapeDtypeStruct((1, GLA_K_TOTAL), F32),
        ],
        compiler_params=_cp(("arbitrary",)),
    )(z, waf, wab, baf, bab, dbf, dbb, dqkv_f, dqkv_b)


def _gla_masks(rev):
    lane_head = lax.broadcasted_iota(jnp.int32, (1, GLA_K_TOTAL), 1) >> 6
    head_masks = [lane_head == h for h in range(GLA_HEADS)]
    t = lax.broadcasted_iota(jnp.int32, (GLA_HEADS * GLA_CHUNK, GLA_CHUNK), 0) & (GLA_CHUNK - 1)
    u = lax.broadcasted_iota(jnp.int32, (GLA_HEADS * GLA_CHUNK, GLA_CHUNK), 1)
    tri = (u > t) if rev else (u <= t)
    row = lax.broadcasted_iota(jnp.int32, (GLA_CHUNK, GLA_K_TOTAL), 0)
    total_row = row == (0 if rev else GLA_CHUNK - 1)
    return head_masks, tri, total_row


def _spread(a, head_masks):
    return jnp.concatenate([jnp.where(m, a, 0.0) for m in head_masks], axis=0)


def _stack(a):
    return jnp.concatenate([a[:, GLA_DV * h : GLA_DV * (h + 1)] for h in range(GLA_HEADS)], axis=0)


def _unstack(a):
    return jnp.concatenate([a[GLA_CHUNK * h : GLA_CHUNK * (h + 1)] for h in range(GLA_HEADS)], axis=1)


def _collect(a, head_masks):
    out = None
    for h, m in enumerate(head_masks):
        part = jnp.where(m, a[GLA_CHUNK * h : GLA_CHUNK * (h + 1)], 0.0)
        out = part if out is None else out + part
    return out


def _gla_chunk_terms(q_ref, k_ref, v_ref, b_ref, rows, head_masks, tri, total_row):
    q = q_ref[rows, :] * (GLA_DK**-0.5)
    k = k_ref[rows, :]
    v = v_ref[rows, :]
    b = b_ref[rows, :]
    eb = jnp.exp(b)
    enb = jnp.exp(-b)
    g = jnp.sum(jnp.where(total_row, b, 0.0), axis=0, keepdims=True)
    egb = jnp.exp(g - b)
    qt = q * eb
    kt = k * enb
    kh = k * egb
    q_heads = _spread(qt, head_masks)
    attn = jnp.where(tri, _dot_nt(q_heads, kt), 0.0)
    return v, eb, enb, egb, jnp.exp(g), qt, kt, kh, q_heads, attn


def _gla_specs(s, tb, rev_blocks):
    nb = s // tb
    rb = (lambda i: nb - 1 - i) if rev_blocks else (lambda i: i)
    q_spec = pl.BlockSpec((tb, GLA_K_TOTAL), lambda i: (rb(i), 0))
    k_spec = pl.BlockSpec((tb, GLA_K_TOTAL), lambda i: (rb(i), 1))
    v_spec = pl.BlockSpec((tb, GLA_V_TOTAL), lambda i: (rb(i), 1))
    b_spec = pl.BlockSpec((tb, GLA_K_TOTAL), lambda i: (rb(i), 0))
    o_spec = pl.BlockSpec((tb, GLA_V_TOTAL), lambda i: (rb(i), 0))
    st_spec = pl.BlockSpec((tb // GLA_CHUNK, GLA_DV, GLA_K_TOTAL), lambda i: (rb(i), 0, 0))
    return nb, q_spec, k_spec, v_spec, b_spec, o_spec, st_spec


def _gla_fwd_chunk(cidx, q_ref, k_ref, v_ref, b_ref, o_ref, sv_ref, st_ref, masks):
    head_masks, tri, total_row = masks
    rows = pl.ds(pl.multiple_of(cidx * GLA_CHUNK, GLA_CHUNK), GLA_CHUNK)
    v, _, _, _, eg, _, _, kh, q_heads, attn = _gla_chunk_terms(q_ref, k_ref, v_ref, b_ref, rows, head_masks, tri, total_row)
    o = jnp.concatenate(
        [_dot(attn[GLA_CHUNK * h : GLA_CHUNK * (h + 1)], v[:, GLA_DV * h : GLA_DV * (h + 1)]) for h in range(GLA_HEADS)], axis=1
    )
    st = st_ref[...]
    o_ref[rows, :] = o + _unstack(_dot_nt(q_heads, st))
    sv_ref[cidx] = st
    st_ref[...] = st * eg + _dot_tn(_stack(v), _spread(kh, head_masks))


def _gla_fwd(z, b_f, b_b, *, name):
    s = z.shape[0]
    tb = _rows(s)
    cpb = tb // GLA_CHUNK
    nb, qf, kf, vf, bf, of, sf = _gla_specs(s, tb, False)
    _, qr, kr, vr, br, orr, sr = _gla_specs(s, tb, True)

    def body(qf_ref, kf_ref, vf_ref, bf_ref, qr_ref, kr_ref, vr_ref, br_ref, of_ref, svf_ref, or_ref, svr_ref, stf_ref, str_ref):
        masks_f, masks_r = _gla_masks(False), _gla_masks(True)

        @pl.when(pl.program_id(0) == 0)
        def _():
            stf_ref[...] = jnp.zeros_like(stf_ref)
            str_ref[...] = jnp.zeros_like(str_ref)

        def chunk(ci, carry):
            _gla_fwd_chunk(ci, qf_ref, kf_ref, vf_ref, bf_ref, of_ref, svf_ref, stf_ref, masks_f)
            _gla_fwd_chunk(cpb - 1 - ci, qr_ref, kr_ref, vr_ref, br_ref, or_ref, svr_ref, str_ref, masks_r)
            return carry

        lax.fori_loop(0, cpb, chunk, 0)

    o_shape = jax.ShapeDtypeStruct((s, GLA_V_TOTAL), F32)
    st_shape = jax.ShapeDtypeStruct((s // GLA_CHUNK, GLA_DV, GLA_K_TOTAL), F32)
    return pl.pallas_call(
        body,
        name=name,
        grid=(nb,),
        in_specs=[qf, kf, vf, bf, qr, kr, vr, br],
        out_specs=[of, sf, orr, sr],
        out_shape=[o_shape, st_shape, o_shape, st_shape],
        scratch_shapes=[pltpu.VMEM((GLA_DV, GLA_K_TOTAL), F32)] * 2,
        compiler_params=_cp(("arbitrary",)),
    )(z, z, z, b_f, z, z, z, b_b)


def _gla_bwd_chunk(cidx, q_ref, k_ref, v_ref, b_ref, do_ref, sv_ref, dqkv_ref, db_ref, dst_ref, masks):
    head_masks, tri, total_row = masks
    rows = pl.ds(pl.multiple_of(cidx * GLA_CHUNK, GLA_CHUNK), GLA_CHUNK)
    v, eb, enb, egb, eg, qt, kt, kh, q_heads, attn = _gla_chunk_terms(q_ref, k_ref, v_ref, b_ref, rows, head_masks, tri, total_row)
    do_c = do_ref[rows, :]
    st = sv_ref[cidx]
    dst = dst_ref[...]
    do_s, v_s = _stack(do_c), _stack(v)
    hs = lambda a, h: a[GLA_CHUNK * h : GLA_CHUNK * (h + 1)]
    vs = lambda a, h: a[:, GLA_DV * h : GLA_DV * (h + 1)]
    dattn = jnp.concatenate([_dot_nt(vs(do_c, h), vs(v, h)) for h in range(GLA_HEADS)], axis=0)
    dattn = jnp.where(tri, dattn, 0.0)
    dv = jnp.concatenate([_dot_tn(hs(attn, h), vs(do_c, h)) for h in range(GLA_HEADS)], axis=1)
    dv = dv + _unstack(_dot_nt(_spread(kh, head_masks), dst))
    dqt = _collect(_dot(do_s, st), head_masks)
    dkt = jnp.zeros_like(dqt)
    for h in range(GLA_HEADS):
        dqt = dqt + jnp.where(head_masks[h], _dot(hs(dattn, h), kt), 0.0)
        dkt = dkt + jnp.where(head_masks[h], _dot_tn(hs(dattn, h), qt), 0.0)
    dkh = _collect(_dot(v_s, dst), head_masks)
    dg = jnp.sum(dkh * kh, axis=0, keepdims=True) + jnp.sum(dst * st, axis=0, keepdims=True) * eg
    db = dqt * qt - dkt * kt - dkh * kh + jnp.where(total_row, dg, 0.0)
    dq = dqt * eb * (GLA_DK**-0.5)
    dk = dkt * enb + dkh * egb
    dqkv_ref[rows, :] = jnp.concatenate([dq, dk, dv], axis=1)
    db_ref[rows, :] = db
    dst_ref[...] = dst * eg + _dot_tn(do_s, q_heads)


def _gla_bwd(z, b_f, b_b, do, st_f, st_b, *, name):
    s = z.shape[0]
    tb = _rows(s)
    cpb = tb // GLA_CHUNK
    wide = 2 * GLA_K_TOTAL + GLA_V_TOTAL
    nb, qf, kf, vf, bf, of, sf = _gla_specs(s, tb, True)
    _, qr, kr, vr, br, orr, sr = _gla_specs(s, tb, False)
    gf = pl.BlockSpec((tb, wide), lambda i: (nb - 1 - i, 0))
    gr = pl.BlockSpec((tb, wide), lambda i: (i, 0))

    def body(qf_ref, kf_ref, vf_ref, bf_ref, dof_ref, svf_ref, qr_ref, kr_ref, vr_ref, br_ref, dor_ref, svr_ref,
             gf_ref, dbf_ref, gr_ref, dbr_ref, dstf_ref, dstr_ref):
        masks_f, masks_r = _gla_masks(False), _gla_masks(True)

        @pl.when(pl.program_id(0) == 0)
        def _():
            dstf_ref[...] = jnp.zeros_like(dstf_ref)
            dstr_ref[...] = jnp.zeros_like(dstr_ref)

        def chunk(ci, carry):
            _gla_bwd_chunk(cpb - 1 - ci, qf_ref, kf_ref, vf_ref, bf_ref, dof_ref, svf_ref, gf_ref, dbf_ref, dstf_ref, masks_f)
            _gla_bwd_chunk(ci, qr_ref, kr_ref, vr_ref, br_ref, dor_ref, svr_ref, gr_ref, dbr_ref, dstr_ref, masks_r)
            return carry

        lax.fori_loop(0, cpb, chunk, 0)

    g_shape = jax.ShapeDtypeStruct((s, wide), F32)
    db_shape = jax.ShapeDtypeStruct((s, GLA_K_TOTAL), F32)
    return pl.pallas_call(
        body,
        name=name,
        grid=(nb,),
        in_specs=[qf, kf, vf, bf, of, sf, qr, kr, vr, br, orr, sr],
        out_specs=[gf, bf, gr, br],
        out_shape=[g_shape, db_shape, g_shape, db_shape],
        scratch_shapes=[pltpu.VMEM((GLA_DV, GLA_K_TOTAL), F32)] * 2,
        compiler_params=_cp(("arbitrary",)),
    )(z, z, z, b_f, do, st_f, z, z, z, b_b, do, st_b)


HALO = 8


def _halo_specs(s, ts, width, col):
    last = s // HALO - 1
    per = ts // HALO
    prev = pl.BlockSpec((HALO, width), lambda i: (jnp.maximum(i * per - 1, 0), col))
    nxt = pl.BlockSpec((HALO, width), lambda i: (jnp.minimum((i + 1) * per, last), col))
    return prev, nxt


def _group_ones():
    group = jnp.arange(CONV_WIDTH, dtype=jnp.int32) // CONV_GROUP
    return (group[:, None] == group[None, :]).astype(BF16)


_ONES_SPEC = pl.BlockSpec((CONV_WIDTH, CONV_WIDTH), lambda i: (0, 0))


def _conv_terms(cc_ext, cu_ext, cw, valid):
    n = cc_ext.shape[0]
    hc = jnp.where(valid, cc_ext * cu_ext, 0.0)
    hc_prev = pltpu.roll(hc, 1, 0)
    hc_next = pltpu.roll(hc, n - 1, 0)
    conv = cw[0:1] * hc_prev + cw[1:2] * hc + cw[2:3] * hc_next
    return hc, hc_prev, hc_next, conv


def _ext(prev_ref, cur_ref, next_ref):
    return jnp.concatenate([prev_ref[...], cur_ref[...], next_ref[...]], axis=0)


def _valid_rows(ts, s):
    row = lax.broadcasted_iota(jnp.int32, (ts + 2 * HALO, 1), 0) + (pl.program_id(0) * ts - HALO)
    return (row >= 0) & (row < s)


def _head_norm(o, gn):
    out = []
    for h in range(GLA_HEADS):
        oh = o[:, GLA_DV * h : GLA_DV * (h + 1)]
        r = lax.rsqrt(jnp.mean(oh * oh, axis=-1, keepdims=True) + EPS)
        out.append((oh * r, r))
    return out


def _mix_fwd(z, o_f, o_b, conv_w, conv_norm, gla_norm, *, name):
    s = z.shape[0]
    ts = _rows(s)
    cprev, cnext = _halo_specs(s, ts, CONV_WIDTH, 1)
    uprev, unext = _halo_specs(s, ts, CONV_WIDTH, 2)

    def body(cb_ref, cc_ref, cu_ref, ccp_ref, ccn_ref, cup_ref, cun_ref, g_ref, of_ref, ob_ref, cw_ref, cn_ref, gn_ref, ones_ref, y_ref):
        valid = _valid_rows(ts, s)
        _, _, _, conv = _conv_terms(_ext(ccp_ref, cc_ref, ccn_ref), _ext(cup_ref, cu_ref, cun_ref), cw_ref[...], valid)
        yc = cb_ref[...] * conv[HALO : HALO + ts]
        ms = _dot_split(yc * yc, ones_ref[...]) * (1.0 / CONV_GROUP)
        y_conv = yc * lax.rsqrt(ms + EPS) * cn_ref[...]
        gate = g_ref[...]
        silu = gate * _sigmoid(gate)
        gn = gn_ref[...]
        y_gla = jnp.concatenate([oh * gn for oh, _ in _head_norm(of_ref[...] + ob_ref[...], gn)], axis=1) * silu
        y_ref[...] = jnp.concatenate([y_conv, y_gla], axis=1).astype(y_ref.dtype)

    col = lambda c, w=CONV_WIDTH: pl.BlockSpec((ts, w), lambda i: (i, c))
    return pl.pallas_call(
        body,
        name=name,
        grid=(s // ts,),
        in_specs=[col(0), col(1), col(2), cprev, cnext, uprev, unext, col(3), col(0), col(0),
                  pl.BlockSpec((CONV_K, CONV_WIDTH), lambda i: (0, 0)), pl.BlockSpec((1, CONV_WIDTH), lambda i: (0, 0)),
                  pl.BlockSpec((1, GLA_DV), lambda i: (0, 0)), _ONES_SPEC],
        out_specs=pl.BlockSpec((ts, D_MODEL), lambda i: (i, 0)),
        out_shape=jax.ShapeDtypeStruct((s, D_MODEL), _CD),
        compiler_params=_cp(("parallel",)),
    )(z, z, z, z, z, z, z, z, o_f, o_b, conv_w, conv_norm, gla_norm, _group_ones())


def _mix_bwd(z, o_f, o_b, dy, conv_w, conv_norm, gla_norm, *, name):
    s = z.shape[0]
    ts = _rows(s)
    halos = [_halo_specs(s, ts, CONV_WIDTH, c) for c in (0, 1, 2)]
    dprev, dnext = _halo_specs(s, ts, CONV_WIDTH, 0)

    def body(cb_ref, cc_ref, cu_ref, cbp_ref, cbn_ref, ccp_ref, ccn_ref, cup_ref, cun_ref, g_ref, of_ref, ob_ref,
             dyc_ref, dyg_ref, dyp_ref, dyn_ref, cw_ref, cn_ref, gn_ref, ones_ref, dza_ref, do_ref, dcw_ref, dcn_ref, dgn_ref):
        n = ts + 2 * HALO
        valid = _valid_rows(ts, s)
        cw = cw_ref[...]
        cn = cn_ref[...]
        ones = ones_ref[...]
        cb = _ext(cbp_ref, cb_ref, cbn_ref)
        cc = _ext(ccp_ref, cc_ref, ccn_ref)
        cu = _ext(cup_ref, cu_ref, cun_ref)
        dy = _ext(dyp_ref, dyc_ref, dyn_ref)
        hc, hc_prev, hc_next, conv = _conv_terms(cc, cu, cw, valid)
        yc = cb * conv
        r = lax.rsqrt(_dot_split(yc * yc, ones) * (1.0 / CONV_GROUP) + EPS)
        yh = yc * r
        dyh = dy * cn
        dyc = r * (dyh - yh * (_dot_split(dyh * yh, ones) * (1.0 / CONV_GROUP)))
        dconv = jnp.where(valid, dyc * cb, 0.0)
        dhc = cw[0:1] * pltpu.roll(dconv, n - 1, 0) + cw[1:2] * dconv + cw[2:3] * pltpu.roll(dconv, 1, 0)
        mid = lambda a: a[HALO : HALO + ts]
        dza_ref[:, 0 : 3 * CONV_WIDTH] = jnp.concatenate([mid(dyc * conv), mid(dhc * cu), mid(dhc * cc)], axis=1).astype(dza_ref.dtype)
        dconv_m = mid(dconv)
        colsum = lambda a: jnp.sum(a, axis=0, keepdims=True)
        dcw = jnp.concatenate([colsum(dconv_m * mid(hc_prev)), colsum(dconv_m * mid(hc)), colsum(dconv_m * mid(hc_next))], axis=0)
        dcn = colsum(mid(dy * yh))

        gate = g_ref[...]
        sg = _sigmoid(gate)
        silu = gate * sg
        gn = gn_ref[...]
        dyg = dyg_ref[...]
        don = dyg * silu
        heads = _head_norm(of_ref[...] + ob_ref[...], gn)
        on = jnp.concatenate([oh * gn for oh, _ in heads], axis=1)
        dza_ref[:, 3 * CONV_WIDTH : ZA_COLS] = (dyg * on * (sg * (1.0 + gate * (1.0 - sg)))).astype(dza_ref.dtype)
        dgn = jnp.zeros((1, GLA_DV), F32)
        dos = []
        for h, (oh, rh) in enumerate(heads):
            donh = don[:, GLA_DV * h : GLA_DV * (h + 1)]
            dgn = dgn + colsum(donh * oh)
            doh = donh * gn
            dos.append(rh * (doh - oh * jnp.mean(doh * oh, axis=-1, keepdims=True)))
        do_ref[...] = jnp.concatenate(dos, axis=1)

        first = pl.program_id(0) == 0

        @pl.when(first)
        def _():
            dcw_ref[...] = dcw
            dcn_ref[...] = dcn
            dgn_ref[...] = dgn

        @pl.when(jnp.logical_not(first))
        def _():
            dcw_ref[...] += dcw
            dcn_ref[...] += dcn
            dgn_ref[...] += dgn

    col = lambda c, w=CONV_WIDTH: pl.BlockSpec((ts, w), lambda i: (i, c))
    cw_spec = pl.BlockSpec((CONV_K, CONV_WIDTH), lambda i: (0, 0))
    cn_spec = pl.BlockSpec((1, CONV_WIDTH), lambda i: (0, 0))
    gn_spec = pl.BlockSpec((1, GLA_DV), lambda i: (0, 0))
    return pl.pallas_call(
        body,
        name=name,
        grid=(s // ts,),
        in_specs=[col(0), col(1), col(2), halos[0][0], halos[0][1], halos[1][0], halos[1][1], halos[2][0], halos[2][1],
                  col(3), col(0), col(0), col(0), col(1), dprev, dnext, cw_spec, cn_spec, gn_spec, _ONES_SPEC],
        out_specs=[pl.BlockSpec((ts, ZA_COLS), lambda i: (i, 0)), col(0), cw_spec, cn_spec, gn_spec],
        out_shape=[
            jax.ShapeDtypeStruct((s, ZA_COLS), _CD),
            jax.ShapeDtypeStruct((s, GLA_V_TOTAL), F32),
            jax.ShapeDtypeStruct((CONV_K, CONV_WIDTH), F32),
            jax.ShapeDtypeStruct((1, CONV_WIDTH), F32),
            jax.ShapeDtypeStruct((1, GLA_DV), F32),
        ],
        compiler_params=_cp(("arbitrary",)),
    )(z, z, z, z, z, z, z, z, z, z, o_f, o_b, dy, dy, dy, dy, conv_w, conv_norm, gla_norm, _group_ones())


def _xa_probs(q_ref, kv_ref, h):
    qh = q_ref[:, XA_HEAD_DIM * h : XA_HEAD_DIM * (h + 1)]
    kh = kv_ref[:, XA_HEAD_DIM * h : XA_HEAD_DIM * (h + 1)]
    vh = kv_ref[:, D_MODEL + XA_HEAD_DIM * h : D_MODEL + XA_HEAD_DIM * (h + 1)]
    sc = _dot_nt(qh, kh) * (XA_HEAD_DIM**-0.5)
    e = jnp.exp(sc - jnp.max(sc, axis=-1, keepdims=True))
    return qh, kh, vh, e / jnp.sum(e, axis=-1, keepdims=True)


def _xattn_fwd(qx, kv, *, name):
    s = qx.shape[0]
    ts = _rows(s)

    def body(q_ref, kv_ref, o_ref):
        outs = []
        for h in range(XA_HEADS):
            _, _, vh, p = _xa_probs(q_ref, kv_ref, h)
            outs.append(_dot(p, vh))
        o_ref[...] = jnp.concatenate(outs, axis=1).astype(o_ref.dtype)

    return pl.pallas_call(
        body,
        name=name,
        grid=(s // ts,),
        in_specs=[pl.BlockSpec((ts, D_MODEL), lambda i: (i, 0)), pl.BlockSpec((N_MEM, 2 * D_MODEL), lambda i: (0, 0))],
        out_specs=pl.BlockSpec((ts, D_MODEL), lambda i: (i, 0)),
        out_shape=jax.ShapeDtypeStruct((s, D_MODEL), _CD),
        compiler_params=_cp(("parallel",)),
    )(qx, kv)


def _xattn_bwd(qx, kv, dox, *, name):
    s = qx.shape[0]
    ts = _rows(s)

    def body(q_ref, kv_ref, do_ref, dq_ref, dkv_ref):
        dqs, dks, dvs = [], [], []
        for h in range(XA_HEADS):
            qh, kh, vh, p = _xa_probs(q_ref, kv_ref, h)
            doh = do_ref[:, XA_HEAD_DIM * h : XA_HEAD_DIM * (h + 1)]
            dp = _dot_nt(doh, vh)
            ds = p * (dp - jnp.sum(dp * p, axis=-1, keepdims=True)) * (XA_HEAD_DIM**-0.5)
            dqs.append(_dot(ds, kh))
            dks.append(_dot_tn(ds, qh))
            dvs.append(_dot_tn(p, doh))
        dq_ref[...] = jnp.concatenate(dqs, axis=1).astype(dq_ref.dtype)
        dkv = jnp.concatenate(dks + dvs, axis=1)

        @pl.when(pl.program_id(0) == 0)
        def _():
            dkv_ref[...] = dkv

        @pl.when(pl.program_id(0) > 0)
        def _():
            dkv_ref[...] += dkv

    tile = pl.BlockSpec((ts, D_MODEL), lambda i: (i, 0))
    kv_spec = pl.BlockSpec((N_MEM, 2 * D_MODEL), lambda i: (0, 0))
    return pl.pallas_call(
        body,
        name=name,
        grid=(s // ts,),
        in_specs=[tile, kv_spec, tile],
        out_specs=[tile, kv_spec],
        out_shape=[jax.ShapeDtypeStruct((s, D_MODEL), _CD), jax.ShapeDtypeStruct((N_MEM, 2 * D_MODEL), F32)],
        compiler_params=_cp(("arbitrary",)),
    )(qx, kv, dox)


def _adamw_math(w, g, m, v):
    m = ADAM_B1 * m + (1.0 - ADAM_B1) * g
    v = ADAM_B2 * v + (1.0 - ADAM_B2) * (g * g)
    m_hat = m / (1.0 - ADAM_B1**ADAM_STEP)
    v_hat = v / (1.0 - ADAM_B2**ADAM_STEP)
    delta = -ADAM_LR * (m_hat / (jnp.sqrt(v_hat) + ADAM_EPS) + ADAM_WD * w)
    return delta, m, v


def _adamw(w, m, v, shard_rows, off, *, transposed, name):
    r, c = w.shape
    by_columns = r % 256 != 0
    tr = 256
    if by_columns:
        assert not transposed and off == 0
        g_spec = tile = pl.BlockSpec((r, tr), lambda i: (0, i))
    else:
        g_spec = pl.BlockSpec((c, tr), lambda i: (off // c, i)) if transposed else pl.BlockSpec((tr, c), lambda i: (off // tr + i, 0))
        tile = pl.BlockSpec((tr, c), lambda i: (i, 0))

    def body(w_ref, g_ref, m_ref, v_ref, go_ref, d_ref, nm_ref, nv_ref):
        g = g_ref[...].T if transposed else g_ref[...]
        go_ref[...] = g
        d_ref[...], nm_ref[...], nv_ref[...] = _adamw_math(w_ref[...], g, m_ref[...], v_ref[...])

    return pl.pallas_call(
        body,
        name=name,
        grid=((c if by_columns else r) // tr,),
        in_specs=[tile, g_spec, tile, tile],
        out_specs=[tile] * 4,
        out_shape=[jax.ShapeDtypeStruct((r, c), F32)] * 4,
        compiler_params=_cp(("parallel",)),
    )(w, shard_rows, m, v)


def _adamw_small(groups, *, name):
    n = len(groups)

    def body(*refs):
        ins, outs = refs[: 4 * n], refs[4 * n :]
        for i in range(n):
            w_ref, g_ref, m_ref, v_ref = ins[4 * i : 4 * i + 4]
            outs[3 * i][...], outs[3 * i + 1][...], outs[3 * i + 2][...] = _adamw_math(w_ref[...], g_ref[...], m_ref[...], v_ref[...])

    flat = [a for grp in groups for a in grp]
    vm = pl.BlockSpec(memory_space=pltpu.VMEM)
    res = pl.pallas_call(
        body,
        name=name,
        in_specs=[vm] * (4 * n),
        out_specs=[vm] * (3 * n),
        out_shape=[jax.ShapeDtypeStruct(grp[0].shape, F32) for grp in groups for _ in range(3)],
        compiler_params=_cp(),
    )(*flat)
    return [tuple(res[3 * i : 3 * i + 3]) for i in range(n)]


def _place():
    return lax.axis_index("x"), lax.axis_index("y"), lax.axis_index("c")


def _rel_chip(x, y, k):
    return (1 - x if k & 2 else x), (1 - y if k & 1 else y)


def _half(c, rh):
    return pl.ds(pl.multiple_of(c * rh, 16), rh)


HBM = pl.BlockSpec(memory_space=pltpu.HBM)
SEM = pl.BlockSpec(memory_space=pltpu.SEMAPHORE)
EFFECT = pltpu.SideEffectType.DATAFLOW_SIDE_EFFECTING


def _in_hbm(a):
    return pltpu.with_memory_space_constraint(a, pltpu.HBM)


def _gather_copies(p_ref, land_ref, send_sems, recv_sems):
    rh = p_ref.shape[0] // 2
    x, y, c = _place()
    rows = _half(c, rh)
    copies = []
    for k in range(1, N_CHIPS):
        cx, cy = _rel_chip(x, y, k)
        copies.append(pltpu.make_async_remote_copy(
            src_ref=p_ref.at[rows], dst_ref=land_ref.at[2 * x + y, rows], send_sem=send_sems.at[k - 1], recv_sem=recv_sems.at[k - 1],
            device_id=(cx, cy, c), device_id_type=MESH))
    copies.append(pltpu.make_async_remote_copy(
        src_ref=p_ref, dst_ref=land_ref.at[2 * x + y], send_sem=send_sems.at[N_CHIPS - 1], recv_sem=recv_sems.at[N_CHIPS - 1],
        device_id=(x, y, 1 - c), device_id_type=MESH))
    return copies


def _gather_start(pack, after, *, name):
    r, w = pack.shape

    def body(p_ref, land_ref, after_ref, send_sems, recv_sems, p_thru, land_thru, token):
        for cp in _gather_copies(p_ref, land_ref, send_sems, recv_sems):
            cp.start()
        token[...] = jnp.zeros_like(token)

    return pl.pallas_call(
        body,
        name=name,
        out_shape=(pltpu.SemaphoreType.DMA((N_CHIPS,)), pltpu.SemaphoreType.DMA((N_CHIPS,)), pltpu.HBM((r, w), pack.dtype),
                   pltpu.HBM((N_CHIPS, r, w), pack.dtype), jax.ShapeDtypeStruct((8, 128), F32)),
        in_specs=(HBM, HBM, ANY),
        out_specs=(SEM, SEM, HBM, HBM, pl.BlockSpec(memory_space=pltpu.VMEM)),
        input_output_aliases={0: 2, 1: 3},
        compiler_params=pltpu.CompilerParams(has_side_effects=EFFECT),
    )(_in_hbm(pack), _in_hbm(lax.empty((N_CHIPS, r, w), pack.dtype)), after)


def _gather_wait(send_sems, recv_sems, pack, land, after, *, name):
    def body(p_ref, land_ref, send_sems, recv_sems, after_ref, p_out, land_out):
        for cp in _gather_copies(p_ref, land_ref, send_sems, recv_sems):
            cp.wait_send()
            cp.wait_recv()

    return pl.pallas_call(
        body,
        name=name,
        out_shape=(pltpu.HBM(pack.shape, pack.dtype), pltpu.HBM(land.shape, land.dtype)),
        in_specs=(HBM, HBM, SEM, SEM, ANY),
        out_specs=(HBM, HBM),
        input_output_aliases={0: 0, 1: 1},
        compiler_params=pltpu.CompilerParams(has_side_effects=EFFECT),
    )(pack, land, send_sems, recv_sems, after)


def _gather_spread(land, *, name):
    n, r, w = land.shape
    rh = r // 2

    def body(land_ref, o_ref, send_sems, recv_sems):
        x, y, c = _place()
        rows = _half(c, rh)
        copies = []
        for k in range(1, N_CHIPS):
            cx, cy = _rel_chip(x, y, k)
            copies.append(pltpu.make_async_remote_copy(
                src_ref=land_ref.at[2 * cx + cy, rows], dst_ref=o_ref.at[2 * cx + cy, rows], send_sem=send_sems.at[k - 1],
                recv_sem=recv_sems.at[k - 1], device_id=(x, y, 1 - c), device_id_type=MESH))
        for cp in copies:
            cp.start()
        for cp in copies:
            cp.wait()

    return pl.pallas_call(
        body,
        name=name,
        in_specs=[ANY],
        out_specs=ANY,
        out_shape=jax.ShapeDtypeStruct(land.shape, land.dtype),
        input_output_aliases={0: 0},
        scratch_shapes=[pltpu.SemaphoreType.DMA((N_CHIPS - 1,)), pltpu.SemaphoreType.DMA((N_CHIPS - 1,))],
        compiler_params=pltpu.CompilerParams(has_side_effects=True),
    )(land)


N_PARTS = 2 * (N_CHIPS - 1)


def _scatter_copies(lo_ref, g_ref, land_lo_ref, land_f_ref, send_sems, recv_sems, starting):
    rh = g_ref.shape[1] // 2
    x, y, c = _place()
    copies = []
    for k in range(1, N_CHIPS):
        cx, cy = _rel_chip(x, y, k)
        for i in range(2):
            part = 2 * (k - 1) + (c if starting else i)
            copies.append(pltpu.make_async_remote_copy(
                src_ref=lo_ref.at[2 * cx + cy, pl.ds(i * rh, rh)], dst_ref=land_lo_ref.at[part],
                send_sem=send_sems.at[2 * (k - 1) + i], recv_sem=recv_sems.at[part], device_id=(cx, cy, i), device_id_type=MESH))
    copies.append(pltpu.make_async_remote_copy(
        src_ref=g_ref.at[2 * x + y, _half(1 - c, rh)], dst_ref=land_f_ref, send_sem=send_sems.at[N_PARTS], recv_sem=recv_sems.at[N_PARTS],
        device_id=(x, y, 1 - c), device_id_type=MESH))
    return copies


def _scatter_start(g_lo, g, *, name):
    n, r, w = g.shape
    rh = r // 2

    def body(lo_ref, g_ref, land_lo_ref, land_f_ref, send_sems, recv_sems, lo_thru, g_thru, land_lo_thru, land_f_thru, token):
        for cp in _scatter_copies(lo_ref, g_ref, land_lo_ref, land_f_ref, send_sems, recv_sems, True):
            cp.start()
        token[...] = jnp.zeros_like(token)

    return pl.pallas_call(
        body,
        name=name,
        out_shape=(pltpu.SemaphoreType.DMA((N_PARTS + 1,)), pltpu.SemaphoreType.DMA((N_PARTS + 1,)), pltpu.HBM(g_lo.shape, g_lo.dtype),
                   pltpu.HBM(g.shape, g.dtype), pltpu.HBM((N_PARTS, rh, w), g_lo.dtype), pltpu.HBM((rh, w), g.dtype),
                   jax.ShapeDtypeStruct((8, 128), F32)),
        in_specs=(HBM, HBM, HBM, HBM),
        out_specs=(SEM, SEM, HBM, HBM, HBM, HBM, pl.BlockSpec(memory_space=pltpu.VMEM)),
        input_output_aliases={0: 2, 1: 3, 2: 4, 3: 5},
        compiler_params=pltpu.CompilerParams(has_side_effects=EFFECT),
    )(_in_hbm(g_lo), _in_hbm(g), _in_hbm(lax.empty((N_PARTS, rh, w), g_lo.dtype)), _in_hbm(lax.empty((rh, w), g.dtype)))


def _scatter_wait(send_sems, recv_sems, g_lo, g, land_lo, land_f, after, *, name):
    def body(lo_ref, g_ref, land_lo_ref, land_f_ref, send_sems, recv_sems, after_ref, o0, o1, o2, o3):
        for cp in _scatter_copies(lo_ref, g_ref, land_lo_ref, land_f_ref, send_sems, recv_sems, False):
            cp.wait_send()
            cp.wait_recv()

    arrays = (g_lo, g, land_lo, land_f)
    return pl.pallas_call(
        body,
        name=name,
        out_shape=tuple(pltpu.HBM(a.shape, a.dtype) for a in arrays),
        in_specs=(HBM, HBM, HBM, HBM, SEM, SEM, ANY),
        out_specs=(HBM, HBM, HBM, HBM),
        input_output_aliases={0: 0, 1: 1, 2: 2, 3: 3},
        compiler_params=pltpu.CompilerParams(has_side_effects=EFFECT),
    )(*arrays, send_sems, recv_sems, after)


def _scatter_sum(g, land_lo, land_f, where, *, name):
    n, r, w = g.shape
    rh = r // 2
    tr = _pick(rh, (256, 160, 80))
    nt = rh // tr

    def body(where_ref, g_ref, f_ref, lo_ref, o_ref):
        acc = g_ref[0] + f_ref[...]
        for part in range(N_PARTS):
            acc = acc + lo_ref[part].astype(F32)
        o_ref[...] = acc

    return pl.pallas_call(
        body,
        name=name,
        grid_spec=pltpu.PrefetchScalarGridSpec(
            num_scalar_prefetch=1,
            grid=(nt,),
            in_specs=[pl.BlockSpec((1, tr, w), lambda i, wh: (wh[1], wh[0] * nt + i, 0)),
                      pl.BlockSpec((tr, w), lambda i, wh: (i, 0)),
                      pl.BlockSpec((N_PARTS, tr, w), lambda i, wh: (0, i, 0))],
            out_specs=pl.BlockSpec((tr, w), lambda i, wh: (wh[0] * nt + i, 0)),
        ),
        out_shape=jax.ShapeDtypeStruct((r, w), F32),
        compiler_params=_cp(("parallel",)),
    )(where, g, land_f, land_lo)


def _swap_all(shards, *, name):
    n = len(shards)

    def body(*refs):
        ins, outs = refs[:n], refs[n : 2 * n]
        send_sems, recv_sems = refs[2 * n :]
        x, y, c = _place()
        copies = []
        for i, (e_ref, o_ref) in enumerate(zip(ins, outs)):
            rows = _half(c, e_ref.shape[0] // 2)
            copies.append(pltpu.make_async_remote_copy(src_ref=e_ref.at[rows], dst_ref=o_ref.at[rows], send_sem=send_sems.at[i],
                                                       recv_sem=recv_sems.at[i], device_id=(x, y, 1 - c), device_id_type=MESH))
        for cp in copies:
            cp.start()
        for cp in copies:
            cp.wait()

    return pl.pallas_call(
        body,
        name=name,
        in_specs=[ANY] * n,
        out_specs=[ANY] * n,
        out_shape=[jax.ShapeDtypeStruct(e.shape, e.dtype) for e in shards],
        input_output_aliases={i: i for i in range(n)},
        scratch_shapes=[pltpu.SemaphoreType.DMA((n,)), pltpu.SemaphoreType.DMA((n,))],
        compiler_params=pltpu.CompilerParams(has_side_effects=True),
    )(*shards)


def _sum_small(small, after):
    n_dev = 8

    def body(s_ref, after_ref, o_ref, all_ref, send_sems, recv_sems):
        x, y, c = _place()
        me = 4 * x + 2 * y + c
        all_ref[me] = s_ref[...]
        copies = []
        for k in range(1, n_dev):
            cx, cy = _rel_chip(x, y, k >> 1)
            cc = 1 - c if k & 1 else c
            copies.append(pltpu.make_async_remote_copy(
                src_ref=s_ref, dst_ref=all_ref.at[me], send_sem=send_sems.at[k - 1], recv_sem=recv_sems.at[k - 1],
                device_id=(cx, cy, cc), device_id_type=MESH))
        for cp in copies:
            cp.start()
        for cp in copies:
            cp.wait()
        acc = all_ref[0]
        for a in range(1, n_dev):
            acc = acc + all_ref[a]
        o_ref[...] = acc

    vm = pl.BlockSpec(memory_space=pltpu.VMEM)
    return pl.pallas_call(
        body,
        name="sum_small",
        in_specs=[vm, ANY],
        out_specs=vm,
        out_shape=jax.ShapeDtypeStruct(small.shape, F32),
        scratch_shapes=[pltpu.VMEM((n_dev,) + small.shape, F32), pltpu.SemaphoreType.DMA((n_dev - 1,)), pltpu.SemaphoreType.DMA((n_dev - 1,))],
        compiler_params=pltpu.CompilerParams(has_side_effects=True),
    )(small, after)


MATS = {"w_in": (776, True), "w_out": (256, False), "w_xq": (256, False), "w_xkv": (512, True), "w_xo": (256, False),
        "w_up": (1024, True), "w_down": (1024, False)}
GATHER_FIRST = ("w_in",)
GATHER_REST = ("w_out", "w_xq", "w_xkv", "w_xo", "w_up", "w_down")
GRAD_GROUPS = (("w_up", "w_down"), ("w_out", "w_xq", "w_xkv", "w_xo"), ("w_in",))


def _group_rows(names):
    n = sum(MATS[name][0] for name in names)
    return n + (-n) % 32


def _pack(pieces, rows):
    p = jnp.concatenate(pieces, axis=0) if len(pieces) > 1 else pieces[0]
    return jnp.pad(p, ((0, rows - p.shape[0]), (0, 0))) if rows > p.shape[0] else p


SMALL = (
    ("mix_norm", 1024), ("conv_norm", 512), ("b_af", 256), ("b_ab", 256), ("gla_norm", 128), ("xa_norm", 1024), ("mem_norm", 1024),
    ("mlp_norm", 1024), ("final_norm", 1024), ("conv_w", 1536), ("w_af", 4096), ("w_ab", 4096), ("loss", 128),
)


def kernel(x, mem, mix_norm, w_in, conv_w, conv_norm, w_af, b_af, w_ab, b_ab, gla_norm, w_out, xa_norm, mem_norm, w_xq, w_xkv, w_xo, mlp_norm, w_up, w_down, final_norm, loss_target, m_mix_norm, m_w_in, m_conv_w, m_conv_norm, m_w_af, m_b_af, m_w_ab, m_b_ab, m_gla_norm, m_w_out, m_xa_norm, m_mem_norm, m_w_xq, m_w_xkv, m_w_xo, m_mlp_norm, m_w_up, m_w_down, m_final_norm, v_mix_norm, v_w_in, v_conv_w, v_conv_norm, v_w_af, v_b_af, v_w_ab, v_b_ab, v_gla_norm, v_w_out, v_xa_norm, v_mem_norm, v_w_xq, v_w_xkv, v_w_xo, v_mlp_norm, v_w_up, v_w_down, v_final_norm):
    given = dict(locals())
    xi, yi, ci = _place()
    chip = 2 * xi + yi
    where = jnp.stack([ci, chip]).astype(jnp.int32)

    lo = {name: (given[name][0].T if MATS[name][1] else given[name][0]).astype(_CD) for name in MATS}
    pack_rest = _pack([lo[name] for name in GATHER_REST], _group_rows(GATHER_REST))
    pack_first = _pack([lo[name] for name in GATHER_FIRST], _group_rows(GATHER_FIRST))
    xs, mems, tgt = x[0], mem[0], loss_target[0]
    behind = lambda gain, token: gain + token[0, 0]

    def placed(shard, full_shape, col):
        return lax.dynamic_update_slice(jnp.zeros(full_shape, F32), shard, (0, col)).reshape(-1, 128)

    sw = jnp.concatenate([
        placed(conv_w[0], (CONV_K, CONV_WIDTH), 128 * chip),
        placed(w_af[0], (GLA_LOWRANK, GLA_K_TOTAL), 64 * chip),
        placed(w_ab[0], (GLA_LOWRANK, GLA_K_TOTAL), 64 * chip),
    ], axis=0)
    sw = jnp.pad(sw, ((0, SMALL_ROWS - sw.shape[0]), (0, 0))) * (ci == 0).astype(F32)
    sw = _sum_small(sw, mix_norm)

    first_send, first_recv, pack_first, land_first, first_token = _gather_start(pack_first, sw, name="gather_first_start")
    rest_send, rest_recv, pack_rest, land_rest, rest_token = _gather_start(pack_rest, first_token, name="gather_rest_start")
    h1 = _rms_fwd(xs, behind(mix_norm, rest_token), name="norm_mix")
    pack_first, land_first = _gather_wait(first_send, first_recv, pack_first, land_first, h1, name="gather_first_wait")
    got_first = _gather_spread(land_first, name="gather_first_spread")

    def whole(got, off, rows):
        return got[:, off : off + rows].reshape(N_CHIPS * rows, D_MODEL)

    w_in_t = whole(got_first, 0, MATS["w_in"][0])
    w_za = jnp.concatenate([w_in_t[0:1536], w_in_t[2560:3072]], axis=0)
    w_zb = jnp.concatenate([w_in_t[1536:2560], w_in_t[3072:W_IN_COLS], jnp.zeros((ZB_COLS - 1056, D_MODEL), _CD)], axis=0)
    conv_w_full = sw[0:12].reshape(CONV_K, CONV_WIDTH)
    w_af_full = sw[12:44].reshape(GLA_LOWRANK, GLA_K_TOTAL)
    w_ab_full = sw[44:76].reshape(GLA_LOWRANK, GLA_K_TOTAL)
    waf_p = jnp.pad(w_af_full, ((0, 128 - GLA_LOWRANK), (0, 0))).astype(_CD)
    wab_p = jnp.pad(w_ab_full, ((GLA_LOWRANK, 128 - 2 * GLA_LOWRANK), (0, 0))).astype(_CD)

    z_b = _mm(h1, w_zb, mode="nt", name="proj_in_b", tn=ZB_COLS)
    z_a = _mm(h1, w_za, mode="nt", name="proj_in_a", tm=512, tn=ZA_COLS)
    b_f, b_b = _gate_fwd(z_b, waf_p, wab_p, b_af, b_ab, name="gates")
    o_f, st_f, o_b, st_b = _gla_fwd(z_b, b_f, b_b, name="gla_scan")
    y = _mix_fwd(z_a, o_f, o_b, conv_w_full, conv_norm, gla_norm, name="mix_out")
    pack_rest, land_rest = _gather_wait(rest_send, rest_recv, pack_rest, land_rest, y, name="gather_rest_wait")
    gathered = _gather_spread(land_rest, name="gather_rest_spread")
    wt, off = {}, 0
    for name in GATHER_REST:
        wt[name] = whole(gathered, off, MATS[name][0])
        off += MATS[name][0]
    x1, hx = _mm_rows(y, wt["w_out"], mode="nn", name="proj_out", rows=(xs,), vecs=(xa_norm,), out_rows=(F32, _CD), epilogue=_ep_residual_norm)
    qx = _mm(hx, wt["w_xq"], mode="nn", name="proj_xq", out_dtypes=(_CD,))
    hmem = _rms_fwd(mems, mem_norm, name="norm_mem")
    kv = _mm(hmem, wt["w_xkv"], mode="nt", name="proj_xkv", out_dtypes=(_CD,))
    ox = _xattn_fwd(qx, kv, name="xattn")
    x2, hm = _mm_rows(ox, wt["w_xo"], mode="nn", name="proj_xo", rows=(x1,), vecs=(mlp_norm,), out_rows=(F32, _CD), epilogue=_ep_residual_norm)
    act, relu_u = _mm(hm, wt["w_up"], mode="nt", name="mlp_up", out_dtypes=(_CD, _CD), tm=2048,
                      epilogue=lambda acc: (jnp.square(jnp.maximum(acc, 0.0)), jnp.maximum(acc, 0.0)))
    dx3, dx3_lo, loss_part, g_final_norm = _mm_rows(
        act, wt["w_down"], mode="nn", name="mlp_down", rows=(x2, tgt), vecs=(final_norm.reshape(1, D_MODEL),),
        out_rows=(F32, _CD), out_vecs=(128, D_MODEL), epilogue=_ep_loss)

    grads_t = {}

    def start_group(names, tag):
        rows = _group_rows(names)
        g = jnp.stack([_pack([grads_t[name][a * MATS[name][0] : (a + 1) * MATS[name][0]] for name in names], rows) for a in range(N_CHIPS)])
        return _scatter_start(g.astype(_TD), g, name="grads_" + tag + "_start")

    def finish_group(state, after, tag):
        send_sems, recv_sems, g_lo, g, land_lo, land_f, _ = state
        g_lo, g, land_lo, land_f = _scatter_wait(send_sems, recv_sems, g_lo, g, land_lo, land_f, after, name="grads_" + tag + "_wait")
        return _scatter_sum(g, land_lo, land_f, where, name="grads_" + tag + "_sum")

    def new_packs(names):
        shape = (N_CHIPS, _group_rows(names), D_MODEL)
        return lax.empty(shape, F32), lax.empty(shape, _TD)

    def grad_into(packs, names, which, a, b, name):
        off = sum(MATS[other][0] for other in names[: names.index(which)])
        return _mm_tn_into(a, b, packs, rows=MATS[which][0], off=off, name=name)

    du = _mm(dx3_lo, wt["w_down"], mode="nt", name="mlp_down_dx", out_dtypes=(_CD,), extras=(relu_u,), tm=2048,
             epilogue=lambda acc, rr: (acc * (2.0 * rr.astype(F32)),))
    packs = new_packs(GRAD_GROUPS[0])
    packs = grad_into(packs, GRAD_GROUPS[0], "w_down", act, dx3_lo, "mlp_down_dw")
    packs = grad_into(packs, GRAD_GROUPS[0], "w_up", du, hm, "mlp_up_dw")
    mlp_state = _scatter_start(packs[1], packs[0], name="grads_mlp_start")
    dx2, dx2_lo, g_mlp_norm = _mm_rows(
        du, wt["w_up"], mode="nn", name="mlp_up_dx", rows=(x2, dx3), vecs=(behind(mlp_norm, mlp_state[-1]),),
        out_rows=(F32, _CD), out_vecs=(D_MODEL,), epilogue=_ep_norm_bwd)
    dox = _mm(dx2_lo, wt["w_xo"], mode="nt", name="proj_xo_dx", out_dtypes=(_CD,))
    packs = new_packs(GRAD_GROUPS[1])
    packs = grad_into(packs, GRAD_GROUPS[1], "w_xo", ox, dx2_lo, "proj_xo_dw")
    dqx, dkv = _xattn_bwd(qx, kv, dox, name="xattn_bwd")
    packs = grad_into(packs, GRAD_GROUPS[1], "w_xq", hx, dqx, "proj_xq_dw")
    dx1, dx1_lo, g_xa_norm = _mm_rows(
        dqx, wt["w_xq"], mode="nt", name="proj_xq_dx", rows=(x1, dx2), vecs=(xa_norm,),
        out_rows=(F32, _CD), out_vecs=(D_MODEL,), epilogue=_ep_norm_bwd)
    dkv_lo = dkv.astype(_CD)
    packs = grad_into(packs, GRAD_GROUPS[1], "w_xkv", dkv_lo, hmem, "proj_xkv_dw")
    dhmem = _mm(dkv_lo, wt["w_xkv"], mode="nn", name="proj_xkv_dx")
    g_mem_norm = _rms_gain_grad(mems, dhmem, name="norm_mem_bwd")
    dy = _mm(dx1_lo, wt["w_out"], mode="nt", name="proj_out_dx")
    packs = grad_into(packs, GRAD_GROUPS[1], "w_out", y, dx1_lo, "proj_out_dw")
    attn_state = _scatter_start(packs[1], packs[0], name="grads_attn_start")
    dz_a, do, g_conv_w, g_conv_norm, g_gla_norm = _mix_bwd(z_a, o_f, o_b, dy, conv_w_full, behind(conv_norm, attn_state[-1]), gla_norm, name="mix_out_bwd")
    dqkv_f, db_f, dqkv_b, db_b = _gla_bwd(z_b, b_f, b_b, do, st_f, st_b, name="gla_scan_bwd")
    dz_b, g_waf_p, g_wab_p, g_b_af, g_b_ab = _gate_bwd(z_b, waf_p, wab_p, b_af, b_ab, db_f, db_b, dqkv_f, dqkv_b, name="gates_bwd")
    g_za = _mm_tn(dz_a, h1, name="proj_in_a_dw")
    g_zb = _mm_tn(dz_b, h1, name="proj_in_b_dw")
    grads_t["w_in"] = jnp.concatenate([g_za[0:1536], g_zb[0:1024], g_za[1536:2048], g_zb[1024:1056]], axis=0)
    in_state = start_group(GRAD_GROUPS[2], "in")
    dh1_a = _mm(dz_a, w_za, mode="nn", name="proj_in_a_dx", tm=512, tk=ZA_COLS)
    grad_x, g_mix_norm = _mm_rows(
        dz_b, w_zb, mode="nn", name="proj_in_b_dx", rows=(xs, dx1, dh1_a), vecs=(behind(mix_norm, in_state[-1]),),
        out_rows=(F32,), out_vecs=(D_MODEL,), epilogue=_ep_norm_bwd)

    half_mlp = finish_group(mlp_state, grad_x, "mlp")
    half_attn = finish_group(attn_state, half_mlp, "attn")
    half_in = finish_group(in_state, half_attn, "in")
    shard_rows = {}
    for names, rows in zip(GRAD_GROUPS, _swap_all([half_mlp, half_attn, half_in], name="shards_to_sibling")):
        off = 0
        for name in names:
            shard_rows[name] = (rows, off)
            off += MATS[name][0]

    small_vals = dict(mix_norm=g_mix_norm, conv_norm=g_conv_norm, b_af=g_b_af, b_ab=g_b_ab, gla_norm=g_gla_norm, xa_norm=g_xa_norm,
                      mem_norm=g_mem_norm, mlp_norm=g_mlp_norm, final_norm=g_final_norm, conv_w=g_conv_w,
                      w_af=g_waf_p[0:GLA_LOWRANK], w_ab=g_wab_p[GLA_LOWRANK : 2 * GLA_LOWRANK], loss=loss_part)
    small = jnp.concatenate([small_vals[name].reshape(-1, 128) for name, _ in SMALL], axis=0)
    small = _sum_small(jnp.pad(small, ((0, SMALL_ROWS - small.shape[0]), (0, 0))), loss_part)
    g_small, off = {}, 0
    for name, n in SMALL:
        g_small[name] = small[off : off + n // 128]
        off += n // 128
    loss = g_small["loss"][0, 0]
    g_small["conv_w"] = lax.dynamic_slice(g_small["conv_w"].reshape(CONV_K, CONV_WIDTH), (0, 128 * chip), (CONV_K, 128))
    g_small["w_af"] = lax.dynamic_slice(g_small["w_af"].reshape(GLA_LOWRANK, GLA_K_TOTAL), (0, 64 * chip), (GLA_LOWRANK, 64))
    g_small["w_ab"] = lax.dynamic_slice(g_small["w_ab"].reshape(GLA_LOWRANK, GLA_K_TOTAL), (0, 64 * chip), (GLA_LOWRANK, 64))

    names = ["mix_norm", "w_in", "conv_w", "conv_norm", "w_af", "b_af", "w_ab", "b_ab", "gla_norm", "w_out", "xa_norm", "mem_norm",
             "w_xq", "w_xkv", "w_xo", "mlp_norm", "w_up", "w_down", "final_norm"]
    big_names = list(MATS)
    as2d = lambda a: a.reshape(1, -1) if a.ndim == 1 else a.reshape(a.shape[-2:])
    grads, deltas, new_m, new_v = {}, {}, {}, {}
    for name in big_names:
        rows, off = shard_rows[name]
        wmv = [as2d(given[name]), as2d(given["m_" + name]), as2d(given["v_" + name])]
        as_stored = name == "w_in"
        if as_stored:
            wmv = [a.T for a in wmv]
        res = _adamw(*wmv, rows, off, transposed=MATS[name][1] and not as_stored, name="adamw_" + name)
        grads[name], deltas[name], new_m[name], new_v[name] = [a.T for a in res] if as_stored else res
    small_names = [name for name in names if name not in big_names]
    groups = []
    for name in small_names:
        grads[name] = g_small[name].reshape(as2d(given[name]).shape)
        groups.append((as2d(given[name]), grads[name], as2d(given["m_" + name]), as2d(given["v_" + name])))
    for name, res in zip(small_names, _adamw_small(groups, name="adamw_small")):
        deltas[name], new_m[name], new_v[name] = res

    like = lambda name, a: a.reshape(given[name].shape)
    return (loss, grad_x[None], *[like(n, grads[n]) for n in names], *[like(n, deltas[n]) for n in names],
            *[like(n, new_m[n]) for n in names], *[like(n, new_v[n]) for n in names])
```

```python
import jax
import jax.numpy as jnp
from jax import lax
from jax.experimental import pallas as pl
from jax.experimental.pallas import tpu as pltpu

F32 = jnp.float32
BF16 = jnp.bfloat16
_CD = jnp.bfloat16
_TD = jnp.bfloat16

D_MODEL = 1024
N_MEM = 256
CONV_WIDTH = 512
CONV_GROUP = 64
CONV_K = 3
GLA_HEADS = 4
GLA_DK = 64
GLA_DV = 128
GLA_K_TOTAL = 256
GLA_V_TOTAL = 512
GLA_LOWRANK = 16
GLA_GATE_SCALE = 1.0 / 16.0
GLA_CHUNK = 64
XA_HEADS = 4
XA_HEAD_DIM = 256
D_FF = 4096
EPS = 1e-6
W_IN_COLS = 3104
ZA_COLS = 2048
ZB_COLS = 1152
LR_COL = 1024

ADAM_LR = 0.001
ADAM_B1 = 0.9
ADAM_B2 = 0.999
ADAM_EPS = 1e-08
ADAM_WD = 0.01
ADAM_STEP = 10

N_CHIPS = 4
SMALL_ROWS = 128

_TS = 512
_VMEM = 44 * 1024 * 1024
MESH = pl.DeviceIdType.MESH
ANY = pl.BlockSpec(memory_space=pl.ANY)


def _cp(sem=None, **kw):
    return pltpu.CompilerParams(dimension_semantics=sem, vmem_limit_bytes=_VMEM, **kw)


def _dot(a, b):
    return jnp.dot(a.astype(_CD), b.astype(_CD), preferred_element_type=F32)


def _dot_nt(a, b):
    return lax.dot_general(a.astype(_CD), b.astype(_CD), (((1,), (1,)), ((), ())), preferred_element_type=F32)


def _dot_tn(a, b):
    return lax.dot_general(a.astype(_CD), b.astype(_CD), (((0,), (0,)), ((), ())), preferred_element_type=F32)


def _dot_split(x, ones):
    hi = x.astype(BF16)
    r = x - hi.astype(F32)
    mid = r.astype(BF16)
    lo = (r - mid.astype(F32)).astype(BF16)
    d = lambda p: jnp.dot(p, ones, preferred_element_type=F32)
    return d(hi) + d(mid) + d(lo)


def _pick(n, cands=(1024, 640, 512, 256, 128)):
    for t in cands:
        if n % t == 0:
            return t
    return n


def _rows(s, light=False):
    return min(2 * _TS if light else _TS, s)


def _sigmoid(v):
    e = jnp.exp(-jnp.abs(v))
    return jnp.where(v >= 0, 1.0 / (1.0 + e), e / (1.0 + e))


def _mm(a, b, *, mode, name, out_dtypes=(F32,), extras=(), epilogue=None, tm=None, tn=None, tk=None):
    m, k = a.shape
    n = b.shape[1] if mode == "nn" else b.shape[0]
    tm = min(m, tm or 1024)
    tn = tn or _pick(n)
    tk = tk or _pick(k)
    nk = k // tk
    n_ex, n_out = len(extras), len(out_dtypes)

    def body(*refs):
        a_ref, b_ref = refs[:2]
        ex = refs[2 : 2 + n_ex]
        outs = refs[2 + n_ex : 2 + n_ex + n_out]
        part = _dot(a_ref[...], b_ref[...]) if mode == "nn" else _dot_nt(a_ref[...], b_ref[...])

        def finish(acc):
            res = epilogue(acc, *[e[...] for e in ex]) if epilogue else (acc,)
            for o, r in zip(outs, res):
                o[...] = r.astype(o.dtype)

        if nk == 1:
            finish(part)
        else:
            acc_ref = refs[-1]
            kk = pl.program_id(2)

            @pl.when(kk == 0)
            def _():
                acc_ref[...] = part

            @pl.when(kk > 0)
            def _():
                acc_ref[...] += part

            @pl.when(kk == nk - 1)
            def _():
                finish(acc_ref[...])

    b_spec = pl.BlockSpec((tk, tn), lambda i, j, kk: (kk, j)) if mode == "nn" else pl.BlockSpec((tn, tk), lambda i, j, kk: (j, kk))
    tile = pl.BlockSpec((tm, tn), lambda i, j, kk: (i, j))
    out = pl.pallas_call(
        body,
        name=name,
        grid=(m // tm, n // tn, nk),
        in_specs=[pl.BlockSpec((tm, tk), lambda i, j, kk: (i, kk)), b_spec] + [tile] * n_ex,
        out_specs=[tile] * n_out,
        out_shape=[jax.ShapeDtypeStruct((m, n), dt) for dt in out_dtypes],
        scratch_shapes=[pltpu.VMEM((tm, tn), F32)] if nk > 1 else [],
        compiler_params=_cp(("parallel", "parallel", "arbitrary")),
    )(a, b, *extras)
    return out[0] if n_out == 1 else out


def _mm_tn(a, b, *, name):
    s, m = a.shape
    n = b.shape[1]
    cap = max(128, (1 << 20) // n)
    tm = _pick(m, tuple(t for t in (512, 640, 384, 256, 128) if t <= max(cap, 128)))
    ts = min(s, 1 << (((1 << 22) // n).bit_length() - 1))
    ns = s // ts

    def body(a_ref, b_ref, o_ref):
        part = _dot_tn(a_ref[...], b_ref[...])
        if ns == 1:
            o_ref[...] = part
        else:
            ss = pl.program_id(1)

            @pl.when(ss == 0)
            def _():
                o_ref[...] = part

            @pl.when(ss > 0)
            def _():
                o_ref[...] += part

    return pl.pallas_call(
        body,
        name=name,
        grid=(m // tm, ns),
        in_specs=[pl.BlockSpec((ts, tm), lambda i, ss: (ss, i)), pl.BlockSpec((ts, n), lambda i, ss: (ss, 0))],
        out_specs=pl.BlockSpec((tm, n), lambda i, ss: (i, 0)),
        out_shape=jax.ShapeDtypeStruct((m, n), F32),
        compiler_params=_cp(("parallel", "arbitrary")),
    )(a, b)


def _mm_tn_into(a, b, packs, *, rows, off, name):
    s, m = a.shape
    n = b.shape[1]
    tm = 1024 if rows % 1024 == 0 and s >= 4096 else 512
    tr = min(tm, rows)
    per, chips = rows // tr, tm // tr
    ts = min(s, (1 << (((1 << 22) // n).bit_length() - 1)) * 512 // tm)
    ns = s // ts

    def body(a_ref, b_ref, f_in, lo_in, f_ref, lo_ref):
        part = _dot_tn(a_ref[...], b_ref[...])
        pieces = [part[c * tr : (c + 1) * tr] for c in range(chips)]
        if ns == 1:
            for c, p in enumerate(pieces):
                f_ref[c] = p
                lo_ref[c] = p.astype(lo_ref.dtype)
        else:
            ss = pl.program_id(1)

            @pl.when(ss == 0)
            def _():
                for c, p in enumerate(pieces):
                    f_ref[c] = p

            @pl.when(ss > 0)
            def _():
                for c, p in enumerate(pieces):
                    f_ref[c] += p

            @pl.when(ss == ns - 1)
            def _():
                lo_ref[...] = f_ref[...].astype(lo_ref.dtype)

    spec = pl.BlockSpec((chips, tr, n), lambda i, ss: (i // per, off // tr + i % per, 0))
    return pl.pallas_call(
        body,
        name=name,
        grid=(m // tm, ns),
        in_specs=[pl.BlockSpec((ts, tm), lambda i, ss: (ss, i)), pl.BlockSpec((ts, n), lambda i, ss: (ss, 0)), ANY, ANY],
        out_specs=[spec, spec],
        out_shape=[jax.ShapeDtypeStruct(p.shape, p.dtype) for p in packs],
        input_output_aliases={2: 0, 3: 1},
        compiler_params=_cp(("parallel", "arbitrary")),
    )(a, b, *packs)


def _mm_rows(a, b, *, mode, name, rows=(), vecs=(), out_rows=(), out_vecs=(), epilogue, tm=512):
    m, k = a.shape
    n = b.shape[1] if mode == "nn" else b.shape[0]
    tm = min(m, tm)
    parts = 2 if tm % 256 == 0 else 1
    n_r, n_v, n_or, n_ov = len(rows), len(vecs), len(out_rows), len(out_vecs)

    def body(*refs):
        a_ref, b_ref = refs[:2]
        r_refs = refs[2 : 2 + n_r]
        v_refs = refs[2 + n_r : 2 + n_r + n_v]
        or_refs = refs[2 + n_r + n_v : 2 + n_r + n_v + n_or]
        ov_refs = refs[2 + n_r + n_v + n_or :]
        res_vecs = None
        for p in range(parts):
            rs = slice(p * tm // parts, (p + 1) * tm // parts)
            acc = _dot(a_ref[rs, :], b_ref[...]) if mode == "nn" else _dot_nt(a_ref[rs, :], b_ref[...])
            res_rows, part_vecs = epilogue(acc, [r[rs, :] for r in r_refs], [v[...] for v in v_refs])
            for o, r in zip(or_refs, res_rows):
                o[rs, :] = r.astype(o.dtype)
            res_vecs = part_vecs if res_vecs is None else [s + t for s, t in zip(res_vecs, part_vecs)]
        if n_ov:
            first = pl.program_id(0) == 0

            @pl.when(first)
            def _():
                for o, r in zip(ov_refs, res_vecs):
                    o[...] = r

            @pl.when(jnp.logical_not(first))
            def _():
                for o, r in zip(ov_refs, res_vecs):
                    o[...] += r

    tile = pl.BlockSpec((tm, n), lambda i: (i, 0))
    whole = lambda arr: pl.BlockSpec(arr.shape, lambda i: (0, 0))
    vec = lambda w: pl.BlockSpec((1, w), lambda i: (0, 0))
    out = pl.pallas_call(
        body,
        name=name,
        grid=(m // tm,),
        in_specs=[pl.BlockSpec((tm, k), lambda i: (i, 0)), whole(b)] + [tile] * n_r + [vec(v.shape[1]) for v in vecs],
        out_specs=[tile] * n_or + [vec(w) for w in out_vecs],
        out_shape=[jax.ShapeDtypeStruct((m, n), dt) for dt in out_rows] + [jax.ShapeDtypeStruct((1, w), F32) for w in out_vecs],
        compiler_params=_cp(("arbitrary",) if n_ov else ("parallel",)),
    )(a, b, *rows, *vecs)
    return out


def _ep_residual_norm(acc, rows, vecs):
    x = acc + rows[0]
    r = lax.rsqrt(jnp.mean(x * x, axis=-1, keepdims=True) + EPS)
    return [x, x * r * vecs[0]], []


def _ep_norm_bwd(acc, rows, vecs):
    dy = acc
    for extra in rows[2:]:
        dy = dy + extra
    x, dres = rows[0], rows[1]
    r = lax.rsqrt(jnp.mean(x * x, axis=-1, keepdims=True) + EPS)
    xh = x * r
    dxh = dy * vecs[0]
    dx = r * (dxh - xh * jnp.mean(dxh * xh, axis=-1, keepdims=True)) + dres
    return [dx, dx], [jnp.sum(dy * xh, axis=0, keepdims=True)]


def _ep_loss(acc, rows, vecs):
    x = acc + rows[0]
    d = x.shape[-1]
    r = lax.rsqrt(jnp.mean(x * x, axis=-1, keepdims=True) + EPS)
    xh = x * r
    err = xh * vecs[0] - rows[1]
    loss = jnp.zeros((1, 128), F32) + 0.5 * jnp.sum(jnp.mean(err * err, axis=-1, keepdims=True))
    dy = err * (1.0 / d)
    dxh = dy * vecs[0]
    dx = r * (dxh - xh * jnp.mean(dxh * xh, axis=-1, keepdims=True))
    return [dx, dx], [loss, jnp.sum(dy * xh, axis=0, keepdims=True)]


def _rms_fwd(x, g, *, name):
    s, d = x.shape
    ts = _rows(s, light=True)

    def body(x_ref, g_ref, o_ref):
        xf = x_ref[...]
        r = lax.rsqrt(jnp.mean(xf * xf, axis=-1, keepdims=True) + EPS)
        o_ref[...] = (xf * r * g_ref[...]).astype(o_ref.dtype)

    return pl.pallas_call(
        body,
        name=name,
        grid=(s // ts,),
        in_specs=[pl.BlockSpec((ts, d), lambda i: (i, 0)), pl.BlockSpec((1, d), lambda i: (0, 0))],
        out_specs=pl.BlockSpec((ts, d), lambda i: (i, 0)),
        out_shape=jax.ShapeDtypeStruct((s, d), _CD),
        compiler_params=_cp(("parallel",)),
    )(x, g)


def _rms_gain_grad(x, dy, *, name):
    s, d = x.shape
    ts = _rows(s)

    def body(x_ref, dy_ref, dg_ref):
        xf = x_ref[...]
        r = lax.rsqrt(jnp.mean(xf * xf, axis=-1, keepdims=True) + EPS)
        part = jnp.sum(dy_ref[...] * (xf * r), axis=0, keepdims=True)

        @pl.when(pl.program_id(0) == 0)
        def _():
            dg_ref[...] = part

        @pl.when(pl.program_id(0) > 0)
        def _():
            dg_ref[...] += part

    tile = pl.BlockSpec((ts, d), lambda i: (i, 0))
    return pl.pallas_call(
        body,
        name=name,
        grid=(s // ts,),
        in_specs=[tile, tile],
        out_specs=pl.BlockSpec((1, d), lambda i: (0, 0)),
        out_shape=jax.ShapeDtypeStruct((1, d), F32),
        compiler_params=_cp(("arbitrary",)),
    )(x, dy)


def _chunk_scan(v, row_in_chunk, suffix):
    t = v.shape[0]
    step = 1
    while step < GLA_CHUNK:
        if suffix:
            v = v + jnp.where(row_in_chunk < GLA_CHUNK - step, pltpu.roll(v, t - step, 0), 0.0)
        else:
            v = v + jnp.where(row_in_chunk >= step, pltpu.roll(v, step, 0), 0.0)
        step *= 2
    return v


def _gate_pre(lr, w_ref, b_ref):
    return _dot(lr, w_ref[...]) + b_ref[...]


def _gate_fwd(z, waf, wab, baf, bab, *, name):
    s = z.shape[0]
    ts = _rows(s, light=True)

    def body(lr_ref, waf_ref, wab_ref, baf_ref, bab_ref, bf_ref, bb_ref):
        lr = lr_ref[...]
        ric = lax.broadcasted_iota(jnp.int32, (ts, GLA_K_TOTAL), 0) & (GLA_CHUNK - 1)
        for w_ref, b_ref, o_ref, suffix in ((waf_ref, baf_ref, bf_ref, False), (wab_ref, bab_ref, bb_ref, True)):
            pre = _gate_pre(lr, w_ref, b_ref)
            la = (jnp.minimum(pre, 0.0) - jnp.log(1.0 + jnp.exp(-jnp.abs(pre)))) * GLA_GATE_SCALE
            o_ref[...] = _chunk_scan(la, ric, suffix)

    wspec = pl.BlockSpec((128, GLA_K_TOTAL), lambda i: (0, 0))
    bspec = pl.BlockSpec((1, GLA_K_TOTAL), lambda i: (0, 0))
    tile = pl.BlockSpec((ts, GLA_K_TOTAL), lambda i: (i, 0))
    return pl.pallas_call(
        body,
        name=name,
        grid=(s // ts,),
        in_specs=[pl.BlockSpec((ts, 128), lambda i: (i, LR_COL // 128)), wspec, wspec, bspec, bspec],
        out_specs=[tile, tile],
        out_shape=[jax.ShapeDtypeStruct((s, GLA_K_TOTAL), F32)] * 2,
        compiler_params=_cp(("parallel",)),
    )(z, waf, wab, baf, bab)


def _gate_bwd(z, waf, wab, baf, bab, dbf, dbb, dqkv_f, dqkv_b, *, name):
    s = z.shape[0]
    ts = _rows(s, light=True)

    def body(lr_ref, waf_ref, wab_ref, baf_ref, bab_ref, dbf_ref, dbb_ref, gf_ref, gb_ref, dzb_ref, dwf_ref, dwb_ref, dbaf_ref, dbab_ref):
        lr = lr_ref[...]
        ric = lax.broadcasted_iota(jnp.int32, (ts, GLA_K_TOTAL), 0) & (GLA_CHUNK - 1)
        first = pl.program_id(0) == 0
        dlr = None
        for w_ref, b_ref, db_ref, dw_ref, dbias_ref, suffix in (
            (waf_ref, baf_ref, dbf_ref, dwf_ref, dbaf_ref, True),
            (wab_ref, bab_ref, dbb_ref, dwb_ref, dbab_ref, False),
        ):
            pre = _gate_pre(lr, w_ref, b_ref)
            dla = _chunk_scan(db_ref[...], ric, suffix)
            dpre = dla * GLA_GATE_SCALE * _sigmoid(-pre)
            part = _dot_nt(dpre, w_ref[...])
            dlr = part if dlr is None else dlr + part
            dw = _dot_tn(lr, dpre)
            dbias = jnp.sum(dpre, axis=0, keepdims=True)

            @pl.when(first)
            def _():
                dw_ref[...] = dw
                dbias_ref[...] = dbias

            @pl.when(jnp.logical_not(first))
            def _():
                dw_ref[...] += dw
                dbias_ref[...] += dbias

        dzb_ref[...] = jnp.concatenate([gf_ref[...] + gb_ref[...], dlr], axis=1).astype(dzb_ref.dtype)

    wspec = pl.BlockSpec((128, GLA_K_TOTAL), lambda i: (0, 0))
    bspec = pl.BlockSpec((1, GLA_K_TOTAL), lambda i: (0, 0))
    tile = pl.BlockSpec((ts, GLA_K_TOTAL), lambda i: (i, 0))
    wide = pl.BlockSpec((ts, 2 * GLA_K_TOTAL + GLA_V_TOTAL), lambda i: (i, 0))
    return pl.pallas_call(
        body,
        name=name,
        grid=(s // ts,),
        in_specs=[pl.BlockSpec((ts, 128), lambda i: (i, LR_COL // 128)), wspec, wspec, bspec, bspec, tile, tile, wide, wide],
        out_specs=[pl.BlockSpec((ts, ZB_COLS), lambda i: (i, 0)), wspec, wspec, bspec, bspec],
        out_shape=[
            jax.ShapeDtypeStruct((s, ZB_COLS), _CD),
            jax.ShapeDtypeStruct((128, GLA_K_TOTAL), F32),
            jax.ShapeDtypeStruct((128, GLA_K_TOTAL), F32),
            jax.ShapeDtypeStruct((1, GLA_K_TOTAL), F32),
            jax.ShapeDtypeStruct((1, GLA_K_TOTAL), F32),
        ],
        compiler_params=_cp(("arbitrary",)),
    )(z, waf, wab, baf, bab, dbf, dbb, dqkv_f, dqkv_b)


def _gla_masks(rev):
    lane_head = lax.broadcasted_iota(jnp.int32, (1, GLA_K_TOTAL), 1) >> 6
    head_masks = [lane_head == h for h in range(GLA_HEADS)]
    t = lax.broadcasted_iota(jnp.int32, (GLA_HEADS * GLA_CHUNK, GLA_CHUNK), 0) & (GLA_CHUNK - 1)
    u = lax.broadcasted_iota(jnp.int32, (GLA_HEADS * GLA_CHUNK, GLA_CHUNK), 1)
    tri = (u > t) if rev else (u <= t)
    row = lax.broadcasted_iota(jnp.int32, (GLA_CHUNK, GLA_K_TOTAL), 0)
    total_row = row == (0 if rev else GLA_CHUNK - 1)
    return head_masks, tri, total_row


def _spread(a, head_masks):
    return jnp.concatenate([jnp.where(m, a, 0.0) for m in head_masks], axis=0)


def _stack(a):
    return jnp.concatenate([a[:, GLA_DV * h : GLA_DV * (h + 1)] for h in range(GLA_HEADS)], axis=0)


def _unstack(a):
    return jnp.concatenate([a[GLA_CHUNK * h : GLA_CHUNK * (h + 1)] for h in range(GLA_HEADS)], axis=1)


def _collect(a, head_masks):
    out = None
    for h, m in enumerate(head_masks):
        part = jnp.where(m, a[GLA_CHUNK * h : GLA_CHUNK * (h + 1)], 0.0)
        out = part if out is None else out + part
    return out


def _gla_chunk_terms(q_ref, k_ref, v_ref, b_ref, rows, head_masks, tri, total_row):
    q = q_ref[rows, :] * (GLA_DK**-0.5)
    k = k_ref[rows, :]
    v = v_ref[rows, :]
    b = b_ref[rows, :]
    eb = jnp.exp(b)
    enb = jnp.exp(-b)
    g = jnp.sum(jnp.where(total_row, b, 0.0), axis=0, keepdims=True)
    egb = jnp.exp(g - b)
    qt = q * eb
    kt = k * enb
    kh = k * egb
    q_heads = _spread(qt, head_masks)
    attn = jnp.where(tri, _dot_nt(q_heads, kt), 0.0)
    return v, eb, enb, egb, jnp.exp(g), qt, kt, kh, q_heads, attn


def _gla_specs(s, tb, rev_blocks):
    nb = s // tb
    rb = (lambda i: nb - 1 - i) if rev_blocks else (lambda i: i)
    q_spec = pl.BlockSpec((tb, GLA_K_TOTAL), lambda i: (rb(i), 0))
    k_spec = pl.BlockSpec((tb, GLA_K_TOTAL), lambda i: (rb(i), 1))
    v_spec = pl.BlockSpec((tb, GLA_V_TOTAL), lambda i: (rb(i), 1))
    b_spec = pl.BlockSpec((tb, GLA_K_TOTAL), lambda i: (rb(i), 0))
    o_spec = pl.BlockSpec((tb, GLA_V_TOTAL), lambda i: (rb(i), 0))
    st_spec = pl.BlockSpec((tb // GLA_CHUNK, GLA_DV, GLA_K_TOTAL), lambda i: (rb(i), 0, 0))
    return nb, q_spec, k_spec, v_spec, b_spec, o_spec, st_spec


def _gla_fwd_chunk(cidx, q_ref, k_ref, v_ref, b_ref, o_ref, sv_ref, st_ref, masks):
    head_masks, tri, total_row = masks
    rows = pl.ds(pl.multiple_of(cidx * GLA_CHUNK, GLA_CHUNK), GLA_CHUNK)
    v, _, _, _, eg, _, _, kh, q_heads, attn = _gla_chunk_terms(q_ref, k_ref, v_ref, b_ref, rows, head_masks, tri, total_row)
    o = jnp.concatenate(
        [_dot(attn[GLA_CHUNK * h : GLA_CHUNK * (h + 1)], v[:, GLA_DV * h : GLA_DV * (h + 1)]) for h in range(GLA_HEADS)], axis=1
    )
    st = st_ref[...]
    o_ref[rows, :] = o + _unstack(_dot_nt(q_heads, st))
    sv_ref[cidx] = st
    st_ref[...] = st * eg + _dot_tn(_stack(v), _spread(kh, head_masks))


def _gla_fwd(z, b_f, b_b, *, name):
    s = z.shape[0]
    tb = _rows(s)
    cpb = tb // GLA_CHUNK
    nb, qf, kf, vf, bf, of, sf = _gla_specs(s, tb, False)
    _, qr, kr, vr, br, orr, sr = _gla_specs(s, tb, True)

    def body(qf_ref, kf_ref, vf_ref, bf_ref, qr_ref, kr_ref, vr_ref, br_ref, of_ref, svf_ref, or_ref, svr_ref, stf_ref, str_ref):
        masks_f, masks_r = _gla_masks(False), _gla_masks(True)

        @pl.when(pl.program_id(0) == 0)
        def _():
            stf_ref[...] = jnp.zeros_like(stf_ref)
            str_ref[...] = jnp.zeros_like(str_ref)

        def chunk(ci, carry):
            _gla_fwd_chunk(ci, qf_ref, kf_ref, vf_ref, bf_ref, of_ref, svf_ref, stf_ref, masks_f)
            _gla_fwd_chunk(cpb - 1 - ci, qr_ref, kr_ref, vr_ref, br_ref, or_ref, svr_ref, str_ref, masks_r)
            return carry

        lax.fori_loop(0, cpb, chunk, 0)

    o_shape = jax.ShapeDtypeStruct((s, GLA_V_TOTAL), F32)
    st_shape = jax.ShapeDtypeStruct((s // GLA_CHUNK, GLA_DV, GLA_K_TOTAL), F32)
    return pl.pallas_call(
        body,
        name=name,
        grid=(nb,),
        in_specs=[qf, kf, vf, bf, qr, kr, vr, br],
        out_specs=[of, sf, orr, sr],
        out_shape=[o_shape, st_shape, o_shape, st_shape],
        scratch_shapes=[pltpu.VMEM((GLA_DV, GLA_K_TOTAL), F32)] * 2,
        compiler_params=_cp(("arbitrary",)),
    )(z, z, z, b_f, z, z, z, b_b)


def _gla_bwd_chunk(cidx, q_ref, k_ref, v_ref, b_ref, do_ref, sv_ref, dqkv_ref, db_ref, dst_ref, masks):
    head_masks, tri, total_row = masks
    rows = pl.ds(pl.multiple_of(cidx * GLA_CHUNK, GLA_CHUNK), GLA_CHUNK)
    v, eb, enb, egb, eg, qt, kt, kh, q_heads, attn = _gla_chunk_terms(q_ref, k_ref, v_ref, b_ref, rows, head_masks, tri, total_row)
    do_c = do_ref[rows, :]
    st = sv_ref[cidx]
    dst = dst_ref[...]
    do_s, v_s = _stack(do_c), _stack(v)
    hs = lambda a, h: a[GLA_CHUNK * h : GLA_CHUNK * (h + 1)]
    vs = lambda a, h: a[:, GLA_DV * h : GLA_DV * (h + 1)]
    dattn = jnp.concatenate([_dot_nt(vs(do_c, h), vs(v, h)) for h in range(GLA_HEADS)], axis=0)
    dattn = jnp.where(tri, dattn, 0.0)
    dv = jnp.concatenate([_dot_tn(hs(attn, h), vs(do_c, h)) for h in range(GLA_HEADS)], axis=1)
    dv = dv + _unstack(_dot_nt(_spread(kh, head_masks), dst))
    dqt = _collect(_dot(do_s, st) + _dot(dattn, kt), head_masks)
    dkt = _dot_tn(dattn, q_heads)
    dkh = _collect(_dot(v_s, dst), head_masks)
    dg = jnp.sum(dkh * kh, axis=0, keepdims=True) + jnp.sum(dst * st, axis=0, keepdims=True) * eg
    db = dqt * qt - dkt * kt - dkh * kh + jnp.where(total_row, dg, 0.0)
    dq = dqt * eb * (GLA_DK**-0.5)
    dk = dkt * enb + dkh * egb
    dqkv_ref[rows, :] = jnp.concatenate([dq, dk, dv], axis=1)
    db_ref[rows, :] = db
    dst_ref[...] = dst * eg + _dot_tn(do_s, q_heads)


def _gla_bwd(z, b_f, b_b, do, st_f, st_b, *, name):
    s = z.shape[0]
    tb = _rows(s)
    cpb = tb // GLA_CHUNK
    wide = 2 * GLA_K_TOTAL + GLA_V_TOTAL
    nb, qf, kf, vf, bf, of, sf = _gla_specs(s, tb, True)
    _, qr, kr, vr, br, orr, sr = _gla_specs(s, tb, False)
    gf = pl.BlockSpec((tb, wide), lambda i: (nb - 1 - i, 0))
    gr = pl.BlockSpec((tb, wide), lambda i: (i, 0))

    def body(qf_ref, kf_ref, vf_ref, bf_ref, dof_ref, svf_ref, qr_ref, kr_ref, vr_ref, br_ref, dor_ref, svr_ref,
             gf_ref, dbf_ref, gr_ref, dbr_ref, dstf_ref, dstr_ref):
        masks_f, masks_r = _gla_masks(False), _gla_masks(True)

        @pl.when(pl.program_id(0) == 0)
        def _():
            dstf_ref[...] = jnp.zeros_like(dstf_ref)
            dstr_ref[...] = jnp.zeros_like(dstr_ref)

        def chunk(ci, carry):
            _gla_bwd_chunk(cpb - 1 - ci, qf_ref, kf_ref, vf_ref, bf_ref, dof_ref, svf_ref, gf_ref, dbf_ref, dstf_ref, masks_f)
            _gla_bwd_chunk(ci, qr_ref, kr_ref, vr_ref, br_ref, dor_ref, svr_ref, gr_ref, dbr_ref, dstr_ref, masks_r)
            return carry

        lax.fori_loop(0, cpb, chunk, 0)

    g_shape = jax.ShapeDtypeStruct((s, wide), F32)
    db_shape = jax.ShapeDtypeStruct((s, GLA_K_TOTAL), F32)
    return pl.pallas_call(
        body,
        name=name,
        grid=(nb,),
        in_specs=[qf, kf, vf, bf, of, sf, qr, kr, vr, br, orr, sr],
        out_specs=[gf, bf, gr, br],
        out_shape=[g_shape, db_shape, g_shape, db_shape],
        scratch_shapes=[pltpu.VMEM((GLA_DV, GLA_K_TOTAL), F32)] * 2,
        compiler_params=_cp(("arbitrary",)),
    )(z, z, z, b_f, do, st_f, z, z, z, b_b, do, st_b)


HALO = 8


def _halo_specs(s, ts, width, col):
    last = s // HALO - 1
    per = ts // HALO
    prev = pl.BlockSpec((HALO, width), lambda i: (jnp.maximum(i * per - 1, 0), col))
    nxt = pl.BlockSpec((HALO, width), lambda i: (jnp.minimum((i + 1) * per, last), col))
    return prev, nxt


def _group_ones():
    group = jnp.arange(CONV_WIDTH, dtype=jnp.int32) // CONV_GROUP
    return (group[:, None] == group[None, :]).astype(BF16)


_ONES_SPEC = pl.BlockSpec((CONV_WIDTH, CONV_WIDTH), lambda i: (0, 0))


def _conv_terms(cc_ext, cu_ext, cw, valid):
    n = cc_ext.shape[0]
    hc = jnp.where(valid, cc_ext * cu_ext, 0.0)
    hc_prev = pltpu.roll(hc, 1, 0)
    hc_next = pltpu.roll(hc, n - 1, 0)
    conv = cw[0:1] * hc_prev + cw[1:2] * hc + cw[2:3] * hc_next
    return hc, hc_prev, hc_next, conv


def _ext(prev_ref, cur_ref, next_ref):
    return jnp.concatenate([prev_ref[...], cur_ref[...], next_ref[...]], axis=0)


def _valid_rows(ts, s):
    row = lax.broadcasted_iota(jnp.int32, (ts + 2 * HALO, 1), 0) + (pl.program_id(0) * ts - HALO)
    return (row >= 0) & (row < s)


def _head_norm(o, gn):
    out = []
    for h in range(GLA_HEADS):
        oh = o[:, GLA_DV * h : GLA_DV * (h + 1)]
        r = lax.rsqrt(jnp.mean(oh * oh, axis=-1, keepdims=True) + EPS)
        out.append((oh * r, r))
    return out


def _mix_fwd(z, o_f, o_b, conv_w, conv_norm, gla_norm, *, name):
    s = z.shape[0]
    ts = _rows(s)
    cprev, cnext = _halo_specs(s, ts, CONV_WIDTH, 1)
    uprev, unext = _halo_specs(s, ts, CONV_WIDTH, 2)

    def body(cb_ref, cc_ref, cu_ref, ccp_ref, ccn_ref, cup_ref, cun_ref, g_ref, of_ref, ob_ref, cw_ref, cn_ref, gn_ref, ones_ref, y_ref):
        valid = _valid_rows(ts, s)
        _, _, _, conv = _conv_terms(_ext(ccp_ref, cc_ref, ccn_ref), _ext(cup_ref, cu_ref, cun_ref), cw_ref[...], valid)
        yc = cb_ref[...] * conv[HALO : HALO + ts]
        ms = _dot_split(yc * yc, ones_ref[...]) * (1.0 / CONV_GROUP)
        y_conv = yc * lax.rsqrt(ms + EPS) * cn_ref[...]
        gate = g_ref[...]
        silu = gate * _sigmoid(gate)
        gn = gn_ref[...]
        y_gla = jnp.concatenate([oh * gn for oh, _ in _head_norm(of_ref[...] + ob_ref[...], gn)], axis=1) * silu
        y_ref[...] = jnp.concatenate([y_conv, y_gla], axis=1).astype(y_ref.dtype)

    col = lambda c, w=CONV_WIDTH: pl.BlockSpec((ts, w), lambda i: (i, c))
    return pl.pallas_call(
        body,
        name=name,
        grid=(s // ts,),
        in_specs=[col(0), col(1), col(2), cprev, cnext, uprev, unext, col(3), col(0), col(0),
                  pl.BlockSpec((CONV_K, CONV_WIDTH), lambda i: (0, 0)), pl.BlockSpec((1, CONV_WIDTH), lambda i: (0, 0)),
                  pl.BlockSpec((1, GLA_DV), lambda i: (0, 0)), _ONES_SPEC],
        out_specs=pl.BlockSpec((ts, D_MODEL), lambda i: (i, 0)),
        out_shape=jax.ShapeDtypeStruct((s, D_MODEL), _CD),
        compiler_params=_cp(("parallel",)),
    )(z, z, z, z, z, z, z, z, o_f, o_b, conv_w, conv_norm, gla_norm, _group_ones())


def _mix_bwd(z, o_f, o_b, dy, conv_w, conv_norm, gla_norm, *, name):
    s = z.shape[0]
    ts = _rows(s)
    halos = [_halo_specs(s, ts, CONV_WIDTH, c) for c in (0, 1, 2)]
    dprev, dnext = _halo_specs(s, ts, CONV_WIDTH, 0)

    def body(cb_ref, cc_ref, cu_ref, cbp_ref, cbn_ref, ccp_ref, ccn_ref, cup_ref, cun_ref, g_ref, of_ref, ob_ref,
             dyc_ref, dyg_ref, dyp_ref, dyn_ref, cw_ref, cn_ref, gn_ref, ones_ref, dza_ref, do_ref, dcw_ref, dcn_ref, dgn_ref):
        n = ts + 2 * HALO
        valid = _valid_rows(ts, s)
        cw = cw_ref[...]
        cn = cn_ref[...]
        ones = ones_ref[...]
        cb = _ext(cbp_ref, cb_ref, cbn_ref)
        cc = _ext(ccp_ref, cc_ref, ccn_ref)
        cu = _ext(cup_ref, cu_ref, cun_ref)
        dy = _ext(dyp_ref, dyc_ref, dyn_ref)
        hc, hc_prev, hc_next, conv = _conv_terms(cc, cu, cw, valid)
        yc = cb * conv
        r = lax.rsqrt(_dot_split(yc * yc, ones) * (1.0 / CONV_GROUP) + EPS)
        yh = yc * r
        dyh = dy * cn
        dyc = r * (dyh - yh * (_dot_split(dyh * yh, ones) * (1.0 / CONV_GROUP)))
        dconv = jnp.where(valid, dyc * cb, 0.0)
        dhc = cw[0:1] * pltpu.roll(dconv, n - 1, 0) + cw[1:2] * dconv + cw[2:3] * pltpu.roll(dconv, 1, 0)
        mid = lambda a: a[HALO : HALO + ts]
        dza_ref[:, 0 : 3 * CONV_WIDTH] = jnp.concatenate([mid(dyc * conv), mid(dhc * cu), mid(dhc * cc)], axis=1).astype(dza_ref.dtype)
        dconv_m = mid(dconv)
        colsum = lambda a: jnp.sum(a, axis=0, keepdims=True)
        dcw = jnp.concatenate([colsum(dconv_m * mid(hc_prev)), colsum(dconv_m * mid(hc)), colsum(dconv_m * mid(hc_next))], axis=0)
        dcn = colsum(mid(dy * yh))

        gate = g_ref[...]
        sg = _sigmoid(gate)
        silu = gate * sg
        gn = gn_ref[...]
        dyg = dyg_ref[...]
        don = dyg * silu
        heads = _head_norm(of_ref[...] + ob_ref[...], gn)
        on = jnp.concatenate([oh * gn for oh, _ in heads], axis=1)
        dza_ref[:, 3 * CONV_WIDTH : ZA_COLS] = (dyg * on * (sg * (1.0 + gate * (1.0 - sg)))).astype(dza_ref.dtype)
        dgn = jnp.zeros((1, GLA_DV), F32)
        dos = []
        for h, (oh, rh) in enumerate(heads):
            donh = don[:, GLA_DV * h : GLA_DV * (h + 1)]
            dgn = dgn + colsum(donh * oh)
            doh = donh * gn
            dos.append(rh * (doh - oh * jnp.mean(doh * oh, axis=-1, keepdims=True)))
        do_ref[...] = jnp.concatenate(dos, axis=1)

        first = pl.program_id(0) == 0

        @pl.when(first)
        def _():
            dcw_ref[...] = dcw
            dcn_ref[...] = dcn
            dgn_ref[...] = dgn

        @pl.when(jnp.logical_not(first))
        def _():
            dcw_ref[...] += dcw
            dcn_ref[...] += dcn
            dgn_ref[...] += dgn

    col = lambda c, w=CONV_WIDTH: pl.BlockSpec((ts, w), lambda i: (i, c))
    cw_spec = pl.BlockSpec((CONV_K, CONV_WIDTH), lambda i: (0, 0))
    cn_spec = pl.BlockSpec((1, CONV_WIDTH), lambda i: (0, 0))
    gn_spec = pl.BlockSpec((1, GLA_DV), lambda i: (0, 0))
    return pl.pallas_call(
        body,
        name=name,
        grid=(s // ts,),
        in_specs=[col(0), col(1), col(2), halos[0][0], halos[0][1], halos[1][0], halos[1][1], halos[2][0], halos[2][1],
                  col(3), col(0), col(0), col(0), col(1), dprev, dnext, cw_spec, cn_spec, gn_spec, _ONES_SPEC],
        out_specs=[pl.BlockSpec((ts, ZA_COLS), lambda i: (i, 0)), col(0), cw_spec, cn_spec, gn_spec],
        out_shape=[
            jax.ShapeDtypeStruct((s, ZA_COLS), _CD),
            jax.ShapeDtypeStruct((s, GLA_V_TOTAL), F32),
            jax.ShapeDtypeStruct((CONV_K, CONV_WIDTH), F32),
            jax.ShapeDtypeStruct((1, CONV_WIDTH), F32),
            jax.ShapeDtypeStruct((1, GLA_DV), F32),
        ],
        compiler_params=_cp(("arbitrary",)),
    )(z, z, z, z, z, z, z, z, z, z, o_f, o_b, dy, dy, dy, dy, conv_w, conv_norm, gla_norm, _group_ones())


def _xa_probs(q_ref, kv_ref, h):
    qh = q_ref[:, XA_HEAD_DIM * h : XA_HEAD_DIM * (h + 1)]
    kh = kv_ref[:, XA_HEAD_DIM * h : XA_HEAD_DIM * (h + 1)]
    vh = kv_ref[:, D_MODEL + XA_HEAD_DIM * h : D_MODEL + XA_HEAD_DIM * (h + 1)]
    sc = _dot_nt(qh, kh) * (XA_HEAD_DIM**-0.5)
    e = jnp.exp(sc - jnp.max(sc, axis=-1, keepdims=True))
    return qh, kh, vh, e / jnp.sum(e, axis=-1, keepdims=True)


def _xattn_fwd(qx, kv, *, name):
    s = qx.shape[0]
    ts = _rows(s, light=True)

    def body(q_ref, kv_ref, o_ref):
        outs = []
        for h in range(XA_HEADS):
            _, _, vh, p = _xa_probs(q_ref, kv_ref, h)
            outs.append(_dot(p, vh))
        o_ref[...] = jnp.concatenate(outs, axis=1).astype(o_ref.dtype)

    return pl.pallas_call(
        body,
        name=name,
        grid=(s // ts,),
        in_specs=[pl.BlockSpec((ts, D_MODEL), lambda i: (i, 0)), pl.BlockSpec((N_MEM, 2 * D_MODEL), lambda i: (0, 0))],
        out_specs=pl.BlockSpec((ts, D_MODEL), lambda i: (i, 0)),
        out_shape=jax.ShapeDtypeStruct((s, D_MODEL), _CD),
        compiler_params=_cp(("parallel",)),
    )(qx, kv)


def _xattn_bwd(qx, kv, dox, *, name):
    s = qx.shape[0]
    ts = _rows(s)

    def body(q_ref, kv_ref, do_ref, dq_ref, dkv_ref):
        dqs, dks, dvs = [], [], []
        for h in range(XA_HEADS):
            qh, kh, vh, p = _xa_probs(q_ref, kv_ref, h)
            doh = do_ref[:, XA_HEAD_DIM * h : XA_HEAD_DIM * (h + 1)]
            dp = _dot_nt(doh, vh)
            ds = p * (dp - jnp.sum(dp * p, axis=-1, keepdims=True)) * (XA_HEAD_DIM**-0.5)
            dqs.append(_dot(ds, kh))
            dks.append(_dot_tn(ds, qh))
            dvs.append(_dot_tn(p, doh))
        dq_ref[...] = jnp.concatenate(dqs, axis=1).astype(dq_ref.dtype)
        dkv = jnp.concatenate(dks + dvs, axis=1)

        @pl.when(pl.program_id(0) == 0)
        def _():
            dkv_ref[...] = dkv

        @pl.when(pl.program_id(0) > 0)
        def _():
            dkv_ref[...] += dkv

    tile = pl.BlockSpec((ts, D_MODEL), lambda i: (i, 0))
    kv_spec = pl.BlockSpec((N_MEM, 2 * D_MODEL), lambda i: (0, 0))
    return pl.pallas_call(
        body,
        name=name,
        grid=(s // ts,),
        in_specs=[tile, kv_spec, tile],
        out_specs=[tile, kv_spec],
        out_shape=[jax.ShapeDtypeStruct((s, D_MODEL), _CD), jax.ShapeDtypeStruct((N_MEM, 2 * D_MODEL), F32)],
        compiler_params=_cp(("arbitrary",)),
    )(qx, kv, dox)


def _adamw_math(w, g, m, v):
    m = ADAM_B1 * m + (1.0 - ADAM_B1) * g
    v = ADAM_B2 * v + (1.0 - ADAM_B2) * (g * g)
    m_hat = m / (1.0 - ADAM_B1**ADAM_STEP)
    v_hat = v / (1.0 - ADAM_B2**ADAM_STEP)
    delta = -ADAM_LR * (m_hat / (jnp.sqrt(v_hat) + ADAM_EPS) + ADAM_WD * w)
    return delta, m, v


def _adamw(w, m, v, shard_rows, off, *, transposed, name):
    r, c = w.shape
    by_columns = r % 256 != 0
    tr = 256
    if by_columns:
        assert not transposed and off == 0
        g_spec = tile = pl.BlockSpec((r, tr), lambda i: (0, i))
    else:
        g_spec = pl.BlockSpec((c, tr), lambda i: (off // c, i)) if transposed else pl.BlockSpec((tr, c), lambda i: (off // tr + i, 0))
        tile = pl.BlockSpec((tr, c), lambda i: (i, 0))

    def body(w_ref, g_ref, m_ref, v_ref, go_ref, d_ref, nm_ref, nv_ref):
        g = g_ref[...].T if transposed else g_ref[...]
        go_ref[...] = g
        d_ref[...], nm_ref[...], nv_ref[...] = _adamw_math(w_ref[...], g, m_ref[...], v_ref[...])

    return pl.pallas_call(
        body,
        name=name,
        grid=((c if by_columns else r) // tr,),
        in_specs=[tile, g_spec, tile, tile],
        out_specs=[tile] * 4,
        out_shape=[jax.ShapeDtypeStruct((r, c), F32)] * 4,
        compiler_params=_cp(("parallel",)),
    )(w, shard_rows, m, v)


def _adamw_small(groups, *, name):
    n = len(groups)

    def body(*refs):
        ins, outs = refs[: 4 * n], refs[4 * n :]
        for i in range(n):
            w_ref, g_ref, m_ref, v_ref = ins[4 * i : 4 * i + 4]
            outs[3 * i][...], outs[3 * i + 1][...], outs[3 * i + 2][...] = _adamw_math(w_ref[...], g_ref[...], m_ref[...], v_ref[...])

    flat = [a for grp in groups for a in grp]
    vm = pl.BlockSpec(memory_space=pltpu.VMEM)
    res = pl.pallas_call(
        body,
        name=name,
        in_specs=[vm] * (4 * n),
        out_specs=[vm] * (3 * n),
        out_shape=[jax.ShapeDtypeStruct(grp[0].shape, F32) for grp in groups for _ in range(3)],
        compiler_params=_cp(),
    )(*flat)
    return [tuple(res[3 * i : 3 * i + 3]) for i in range(n)]


def _place():
    return lax.axis_index("x"), lax.axis_index("y"), lax.axis_index("c")


def _rel_chip(x, y, k):
    return (1 - x if k & 2 else x), (1 - y if k & 1 else y)


def _half(c, rh):
    return pl.ds(pl.multiple_of(c * rh, 16), rh)


HBM = pl.BlockSpec(memory_space=pltpu.HBM)
SEM = pl.BlockSpec(memory_space=pltpu.SEMAPHORE)
EFFECT = pltpu.SideEffectType.DATAFLOW_SIDE_EFFECTING


def _in_hbm(a):
    return pltpu.with_memory_space_constraint(a, pltpu.HBM)


def _gather_copies(p_ref, land_ref, send_sems, recv_sems):
    rh = p_ref.shape[0] // 2
    x, y, c = _place()
    rows = _half(c, rh)
    copies = []
    for k in range(1, N_CHIPS):
        cx, cy = _rel_chip(x, y, k)
        copies.append(pltpu.make_async_remote_copy(
            src_ref=p_ref.at[rows], dst_ref=land_ref.at[2 * x + y, rows], send_sem=send_sems.at[k - 1], recv_sem=recv_sems.at[k - 1],
            device_id=(cx, cy, c), device_id_type=MESH))
    copies.append(pltpu.make_async_remote_copy(
        src_ref=p_ref, dst_ref=land_ref.at[2 * x + y], send_sem=send_sems.at[N_CHIPS - 1], recv_sem=recv_sems.at[N_CHIPS - 1],
        device_id=(x, y, 1 - c), device_id_type=MESH))
    return copies


def _gather_start(pack, after, *, name):
    r, w = pack.shape

    def body(p_ref, land_ref, after_ref, send_sems, recv_sems, p_thru, land_thru, token):
        for cp in _gather_copies(p_ref, land_ref, send_sems, recv_sems):
            cp.start()
        token[...] = jnp.zeros_like(token)

    return pl.pallas_call(
        body,
        name=name,
        out_shape=(pltpu.SemaphoreType.DMA((N_CHIPS,)), pltpu.SemaphoreType.DMA((N_CHIPS,)), pltpu.HBM((r, w), pack.dtype),
                   pltpu.HBM((N_CHIPS, r, w), pack.dtype), jax.ShapeDtypeStruct((8, 128), F32)),
        in_specs=(HBM, HBM, ANY),
        out_specs=(SEM, SEM, HBM, HBM, pl.BlockSpec(memory_space=pltpu.VMEM)),
        input_output_aliases={0: 2, 1: 3},
        compiler_params=pltpu.CompilerParams(has_side_effects=EFFECT),
    )(_in_hbm(pack), _in_hbm(lax.empty((N_CHIPS, r, w), pack.dtype)), after)


def _gather_wait(send_sems, recv_sems, pack, land, after, *, name):
    def body(p_ref, land_ref, send_sems, recv_sems, after_ref, p_out, land_out):
        for cp in _gather_copies(p_ref, land_ref, send_sems, recv_sems):
            cp.wait_send()
            cp.wait_recv()

    return pl.pallas_call(
        body,
        name=name,
        out_shape=(pltpu.HBM(pack.shape, pack.dtype), pltpu.HBM(land.shape, land.dtype)),
        in_specs=(HBM, HBM, SEM, SEM, ANY),
        out_specs=(HBM, HBM),
        input_output_aliases={0: 0, 1: 1},
        compiler_params=pltpu.CompilerParams(has_side_effects=EFFECT),
    )(pack, land, send_sems, recv_sems, after)


def _gather_spread(land, *, name):
    n, r, w = land.shape
    rh = r // 2

    def body(land_ref, o_ref, send_sems, recv_sems):
        x, y, c = _place()
        rows = _half(c, rh)
        copies = []
        for k in range(1, N_CHIPS):
            cx, cy = _rel_chip(x, y, k)
            copies.append(pltpu.make_async_remote_copy(
                src_ref=land_ref.at[2 * cx + cy, rows], dst_ref=o_ref.at[2 * cx + cy, rows], send_sem=send_sems.at[k - 1],
                recv_sem=recv_sems.at[k - 1], device_id=(x, y, 1 - c), device_id_type=MESH))
        for cp in copies:
            cp.start()
        for cp in copies:
            cp.wait()

    return pl.pallas_call(
        body,
        name=name,
        in_specs=[ANY],
        out_specs=ANY,
        out_shape=jax.ShapeDtypeStruct(land.shape, land.dtype),
        input_output_aliases={0: 0},
        scratch_shapes=[pltpu.SemaphoreType.DMA((N_CHIPS - 1,)), pltpu.SemaphoreType.DMA((N_CHIPS - 1,))],
        compiler_params=pltpu.CompilerParams(has_side_effects=True),
    )(land)


N_PARTS = 2 * (N_CHIPS - 1)


def _scatter_copies(lo_ref, g_ref, land_lo_ref, land_f_ref, send_sems, recv_sems, starting):
    rh = g_ref.shape[1] // 2
    x, y, c = _place()
    copies = []
    for k in range(1, N_CHIPS):
        cx, cy = _rel_chip(x, y, k)
        for i in range(2):
            part = 2 * (k - 1) + (c if starting else i)
            copies.append(pltpu.make_async_remote_copy(
                src_ref=lo_ref.at[2 * cx + cy, pl.ds(i * rh, rh)], dst_ref=land_lo_ref.at[part],
                send_sem=send_sems.at[2 * (k - 1) + i], recv_sem=recv_sems.at[part], device_id=(cx, cy, i), device_id_type=MESH))
    copies.append(pltpu.make_async_remote_copy(
        src_ref=g_ref.at[2 * x + y, _half(1 - c, rh)], dst_ref=land_f_ref, send_sem=send_sems.at[N_PARTS], recv_sem=recv_sems.at[N_PARTS],
        device_id=(x, y, 1 - c), device_id_type=MESH))
    return copies


def _scatter_start(g_lo, g, *, name):
    n, r, w = g.shape
    rh = r // 2

    def body(lo_ref, g_ref, land_lo_ref, land_f_ref, send_sems, recv_sems, lo_thru, g_thru, land_lo_thru, land_f_thru, token):
        for cp in _scatter_copies(lo_ref, g_ref, land_lo_ref, land_f_ref, send_sems, recv_sems, True):
            cp.start()
        token[...] = jnp.zeros_like(token)

    return pl.pallas_call(
        body,
        name=name,
        out_shape=(pltpu.SemaphoreType.DMA((N_PARTS + 1,)), pltpu.SemaphoreType.DMA((N_PARTS + 1,)), pltpu.HBM(g_lo.shape, g_lo.dtype),
                   pltpu.HBM(g.shape, g.dtype), pltpu.HBM((N_PARTS, rh, w), g_lo.dtype), pltpu.HBM((rh, w), g.dtype),
                   jax.ShapeDtypeStruct((8, 128), F32)),
        in_specs=(HBM, HBM, HBM, HBM),
        out_specs=(SEM, SEM, HBM, HBM, HBM, HBM, pl.BlockSpec(memory_space=pltpu.VMEM)),
        input_output_aliases={0: 2, 1: 3, 2: 4, 3: 5},
        compiler_params=pltpu.CompilerParams(has_side_effects=EFFECT),
    )(_in_hbm(g_lo), _in_hbm(g), _in_hbm(lax.empty((N_PARTS, rh, w), g_lo.dtype)), _in_hbm(lax.empty((rh, w), g.dtype)))


def _scatter_wait(send_sems, recv_sems, g_lo, g, land_lo, land_f, after, *, name):
    def body(lo_ref, g_ref, land_lo_ref, land_f_ref, send_sems, recv_sems, after_ref, o0, o1, o2, o3):
        for cp in _scatter_copies(lo_ref, g_ref, land_lo_ref, land_f_ref, send_sems, recv_sems, False):
            cp.wait_send()
            cp.wait_recv()

    arrays = (g_lo, g, land_lo, land_f)
    return pl.pallas_call(
        body,
        name=name,
        out_shape=tuple(pltpu.HBM(a.shape, a.dtype) for a in arrays),
        in_specs=(HBM, HBM, HBM, HBM, SEM, SEM, ANY),
        out_specs=(HBM, HBM, HBM, HBM),
        input_output_aliases={0: 0, 1: 1, 2: 2, 3: 3},
        compiler_params=pltpu.CompilerParams(has_side_effects=EFFECT),
    )(*arrays, send_sems, recv_sems, after)


def _scatter_sum(g, land_lo, land_f, where, *, name):
    n, r, w = g.shape
    rh = r // 2
    tr = _pick(rh, (256, 160, 80))
    nt = rh // tr

    def body(where_ref, g_ref, f_ref, lo_ref, o_ref):
        acc = g_ref[0] + f_ref[...]
        for part in range(N_PARTS):
            acc = acc + lo_ref[part].astype(F32)
        o_ref[...] = acc

    return pl.pallas_call(
        body,
        name=name,
        grid_spec=pltpu.PrefetchScalarGridSpec(
            num_scalar_prefetch=1,
            grid=(nt,),
            in_specs=[pl.BlockSpec((1, tr, w), lambda i, wh: (wh[1], wh[0] * nt + i, 0)),
                      pl.BlockSpec((tr, w), lambda i, wh: (i, 0)),
                      pl.BlockSpec((N_PARTS, tr, w), lambda i, wh: (0, i, 0))],
            out_specs=pl.BlockSpec((tr, w), lambda i, wh: (wh[0] * nt + i, 0)),
        ),
        out_shape=jax.ShapeDtypeStruct((r, w), F32),
        compiler_params=_cp(("parallel",)),
    )(where, g, land_f, land_lo)


def _swap_all(shards, *, name):
    n = len(shards)

    def body(*refs):
        ins, outs = refs[:n], refs[n : 2 * n]
        send_sems, recv_sems = refs[2 * n :]
        x, y, c = _place()
        copies = []
        for i, (e_ref, o_ref) in enumerate(zip(ins, outs)):
            rows = _half(c, e_ref.shape[0] // 2)
            copies.append(pltpu.make_async_remote_copy(src_ref=e_ref.at[rows], dst_ref=o_ref.at[rows], send_sem=send_sems.at[i],
                                                       recv_sem=recv_sems.at[i], device_id=(x, y, 1 - c), device_id_type=MESH))
        for cp in copies:
            cp.start()
        for cp in copies:
            cp.wait()

    return pl.pallas_call(
        body,
        name=name,
        in_specs=[ANY] * n,
        out_specs=[ANY] * n,
        out_shape=[jax.ShapeDtypeStruct(e.shape, e.dtype) for e in shards],
        input_output_aliases={i: i for i in range(n)},
        scratch_shapes=[pltpu.SemaphoreType.DMA((n,)), pltpu.SemaphoreType.DMA((n,))],
        compiler_params=pltpu.CompilerParams(has_side_effects=True),
    )(*shards)


def _sum_small(small, after):
    n_dev = 8

    def body(s_ref, after_ref, o_ref, all_ref, send_sems, recv_sems):
        x, y, c = _place()
        me = 4 * x + 2 * y + c
        all_ref[me] = s_ref[...]
        copies = []
        for k in range(1, n_dev):
            cx, cy = _rel_chip(x, y, k >> 1)
            cc = 1 - c if k & 1 else c
            copies.append(pltpu.make_async_remote_copy(
                src_ref=s_ref, dst_ref=all_ref.at[me], send_sem=send_sems.at[k - 1], recv_sem=recv_sems.at[k - 1],
                device_id=(cx, cy, cc), device_id_type=MESH))
        for cp in copies:
            cp.start()
        for cp in copies:
            cp.wait()
        acc = all_ref[0]
        for a in range(1, n_dev):
            acc = acc + all_ref[a]
        o_ref[...] = acc

    vm = pl.BlockSpec(memory_space=pltpu.VMEM)
    return pl.pallas_call(
        body,
        name="sum_small",
        in_specs=[vm, ANY],
        out_specs=vm,
        out_shape=jax.ShapeDtypeStruct(small.shape, F32),
        scratch_shapes=[pltpu.VMEM((n_dev,) + small.shape, F32), pltpu.SemaphoreType.DMA((n_dev - 1,)), pltpu.SemaphoreType.DMA((n_dev - 1,))],
        compiler_params=pltpu.CompilerParams(has_side_effects=True),
    )(small, after)


MATS = {"w_in": (776, True), "w_out": (256, False), "w_xq": (256, False), "w_xkv": (512, True), "w_xo": (256, False),
        "w_up": (1024, True), "w_down": (1024, False)}
GATHER_FIRST = ("w_in",)
GATHER_REST = ("w_out", "w_xq", "w_xkv", "w_xo", "w_up", "w_down")
GRAD_GROUPS = (("w_up", "w_down"), ("w_out", "w_xq", "w_xkv", "w_xo"), ("w_in",))


def _group_rows(names):
    n = sum(MATS[name][0] for name in names)
    return n + (-n) % 32


def _pack(pieces, rows):
    p = jnp.concatenate(pieces, axis=0) if len(pieces) > 1 else pieces[0]
    return jnp.pad(p, ((0, rows - p.shape[0]), (0, 0))) if rows > p.shape[0] else p


SMALL = (
    ("mix_norm", 1024), ("conv_norm", 512), ("b_af", 256), ("b_ab", 256), ("gla_norm", 128), ("xa_norm", 1024), ("mem_norm", 1024),
    ("mlp_norm", 1024), ("final_norm", 1024), ("conv_w", 1536), ("w_af", 4096), ("w_ab", 4096), ("loss", 128),
)


def kernel(x, mem, mix_norm, w_in, conv_w, conv_norm, w_af, b_af, w_ab, b_ab, gla_norm, w_out, xa_norm, mem_norm, w_xq, w_xkv, w_xo, mlp_norm, w_up, w_down, final_norm, loss_target, m_mix_norm, m_w_in, m_conv_w, m_conv_norm, m_w_af, m_b_af, m_w_ab, m_b_ab, m_gla_norm, m_w_out, m_xa_norm, m_mem_norm, m_w_xq, m_w_xkv, m_w_xo, m_mlp_norm, m_w_up, m_w_down, m_final_norm, v_mix_norm, v_w_in, v_conv_w, v_conv_norm, v_w_af, v_b_af, v_w_ab, v_b_ab, v_gla_norm, v_w_out, v_xa_norm, v_mem_norm, v_w_xq, v_w_xkv, v_w_xo, v_mlp_norm, v_w_up, v_w_down, v_final_norm):
    given = dict(locals())
    xi, yi, ci = _place()
    chip = 2 * xi + yi
    where = jnp.stack([ci, chip]).astype(jnp.int32)

    lo = {name: (given[name][0].T if MATS[name][1] else given[name][0]).astype(_CD) for name in MATS}
    pack_rest = _pack([lo[name] for name in GATHER_REST], _group_rows(GATHER_REST))
    pack_first = _pack([lo[name] for name in GATHER_FIRST], _group_rows(GATHER_FIRST))
    xs, mems, tgt = x[0], mem[0], loss_target[0]
    behind = lambda gain, token: gain + token[0, 0]

    def placed(shard, full_shape, col):
        return lax.dynamic_update_slice(jnp.zeros(full_shape, F32), shard, (0, col)).reshape(-1, 128)

    sw = jnp.concatenate([
        placed(conv_w[0], (CONV_K, CONV_WIDTH), 128 * chip),
        placed(w_af[0], (GLA_LOWRANK, GLA_K_TOTAL), 64 * chip),
        placed(w_ab[0], (GLA_LOWRANK, GLA_K_TOTAL), 64 * chip),
    ], axis=0)
    sw = jnp.pad(sw, ((0, SMALL_ROWS - sw.shape[0]), (0, 0))) * (ci == 0).astype(F32)
    sw = _sum_small(sw, mix_norm)

    first_send, first_recv, pack_first, land_first, first_token = _gather_start(pack_first, sw, name="gather_first_start")
    rest_send, rest_recv, pack_rest, land_rest, rest_token = _gather_start(pack_rest, first_token, name="gather_rest_start")
    h1 = _rms_fwd(xs, behind(mix_norm, rest_token), name="norm_mix")
    pack_first, land_first = _gather_wait(first_send, first_recv, pack_first, land_first, h1, name="gather_first_wait")
    got_first = _gather_spread(land_first, name="gather_first_spread")

    def whole(got, off, rows):
        return got[:, off : off + rows].reshape(N_CHIPS * rows, D_MODEL)

    w_in_t = whole(got_first, 0, MATS["w_in"][0])
    w_za = jnp.concatenate([w_in_t[0:1536], w_in_t[2560:3072]], axis=0)
    w_zb = jnp.concatenate([w_in_t[1536:2560], w_in_t[3072:W_IN_COLS], jnp.zeros((ZB_COLS - 1056, D_MODEL), _CD)], axis=0)
    conv_w_full = sw[0:12].reshape(CONV_K, CONV_WIDTH)
    w_af_full = sw[12:44].reshape(GLA_LOWRANK, GLA_K_TOTAL)
    w_ab_full = sw[44:76].reshape(GLA_LOWRANK, GLA_K_TOTAL)
    waf_p = jnp.pad(w_af_full, ((0, 128 - GLA_LOWRANK), (0, 0))).astype(_CD)
    wab_p = jnp.pad(w_ab_full, ((GLA_LOWRANK, 128 - 2 * GLA_LOWRANK), (0, 0))).astype(_CD)

    z_b = _mm(h1, w_zb, mode="nt", name="proj_in_b", tn=ZB_COLS)
    z_a = _mm(h1, w_za, mode="nt", name="proj_in_a", tm=512, tn=ZA_COLS)
    b_f, b_b = _gate_fwd(z_b, waf_p, wab_p, b_af, b_ab, name="gates")
    o_f, st_f, o_b, st_b = _gla_fwd(z_b, b_f, b_b, name="gla_scan")
    y = _mix_fwd(z_a, o_f, o_b, conv_w_full, conv_norm, gla_norm, name="mix_out")
    pack_rest, land_rest = _gather_wait(rest_send, rest_recv, pack_rest, land_rest, y, name="gather_rest_wait")
    gathered = _gather_spread(land_rest, name="gather_rest_spread")
    wt, off = {}, 0
    for name in GATHER_REST:
        wt[name] = whole(gathered, off, MATS[name][0])
        off += MATS[name][0]
    x1, hx = _mm_rows(y, wt["w_out"], mode="nn", name="proj_out", rows=(xs,), vecs=(xa_norm,), out_rows=(F32, _CD), epilogue=_ep_residual_norm)
    qx = _mm(hx, wt["w_xq"], mode="nn", name="proj_xq", out_dtypes=(_CD,))
    hmem = _rms_fwd(mems, mem_norm, name="norm_mem")
    kv = _mm(hmem, wt["w_xkv"], mode="nt", name="proj_xkv", out_dtypes=(_CD,))
    ox = _xattn_fwd(qx, kv, name="xattn")
    x2, hm = _mm_rows(ox, wt["w_xo"], mode="nn", name="proj_xo", rows=(x1,), vecs=(mlp_norm,), out_rows=(F32, _CD), epilogue=_ep_residual_norm)
    act, relu_u = _mm(hm, wt["w_up"], mode="nt", name="mlp_up", out_dtypes=(_CD, _CD), tm=2048,
                      epilogue=lambda acc: (jnp.square(jnp.maximum(acc, 0.0)), jnp.maximum(acc, 0.0)))
    dx3, dx3_lo, loss_part, g_final_norm = _mm_rows(
        act, wt["w_down"], mode="nn", name="mlp_down", rows=(x2, tgt), vecs=(final_norm.reshape(1, D_MODEL),),
        out_rows=(F32, _CD), out_vecs=(128, D_MODEL), epilogue=_ep_loss)

    grads_t = {}

    def start_group(names, tag):
        rows = _group_rows(names)
        g = jnp.stack([_pack([grads_t[name][a * MATS[name][0] : (a + 1) * MATS[name][0]] for name in names], rows) for a in range(N_CHIPS)])
        return _scatter_start(g.astype(_TD), g, name="grads_" + tag + "_start")

    def finish_group(state, after, tag):
        send_sems, recv_sems, g_lo, g, land_lo, land_f, _ = state
        g_lo, g, land_lo, land_f = _scatter_wait(send_sems, recv_sems, g_lo, g, land_lo, land_f, after, name="grads_" + tag + "_wait")
        return _scatter_sum(g, land_lo, land_f, where, name="grads_" + tag + "_sum")

    def new_packs(names):
        shape = (N_CHIPS, _group_rows(names), D_MODEL)
        return lax.empty(shape, F32), lax.empty(shape, _TD)

    def grad_into(packs, names, which, a, b, name):
        off = sum(MATS[other][0] for other in names[: names.index(which)])
        return _mm_tn_into(a, b, packs, rows=MATS[which][0], off=off, name=name)

    du = _mm(dx3_lo, wt["w_down"], mode="nt", name="mlp_down_dx", out_dtypes=(_CD,), extras=(relu_u,), tm=2048,
             epilogue=lambda acc, rr: (acc * (2.0 * rr.astype(F32)),))
    packs = new_packs(GRAD_GROUPS[0])
    packs = grad_into(packs, GRAD_GROUPS[0], "w_down", act, dx3_lo, "mlp_down_dw")
    packs = grad_into(packs, GRAD_GROUPS[0], "w_up", du, hm, "mlp_up_dw")
    mlp_state = _scatter_start(packs[1], packs[0], name="grads_mlp_start")
    dx2, dx2_lo, g_mlp_norm = _mm_rows(
        du, wt["w_up"], mode="nn", name="mlp_up_dx", rows=(x2, dx3), vecs=(behind(mlp_norm, mlp_state[-1]),),
        out_rows=(F32, _CD), out_vecs=(D_MODEL,), epilogue=_ep_norm_bwd)
    dox = _mm(dx2_lo, wt["w_xo"], mode="nt", name="proj_xo_dx", out_dtypes=(_CD,))
    packs = new_packs(GRAD_GROUPS[1])
    packs = grad_into(packs, GRAD_GROUPS[1], "w_xo", ox, dx2_lo, "proj_xo_dw")
    dqx, dkv = _xattn_bwd(qx, kv, dox, name="xattn_bwd")
    packs = grad_into(packs, GRAD_GROUPS[1], "w_xq", hx, dqx, "proj_xq_dw")
    dx1, dx1_lo, g_xa_norm = _mm_rows(
        dqx, wt["w_xq"], mode="nt", name="proj_xq_dx", rows=(x1, dx2), vecs=(xa_norm,),
        out_rows=(F32, _CD), out_vecs=(D_MODEL,), epilogue=_ep_norm_bwd)
    dkv_lo = dkv.astype(_CD)
    packs = grad_into(packs, GRAD_GROUPS[1], "w_xkv", dkv_lo, hmem, "proj_xkv_dw")
    dhmem = _mm(dkv_lo, wt["w_xkv"], mode="nn", name="proj_xkv_dx")
    g_mem_norm = _rms_gain_grad(mems, dhmem, name="norm_mem_bwd")
    dy = _mm(dx1_lo, wt["w_out"], mode="nt", name="proj_out_dx")
    packs = grad_into(packs, GRAD_GROUPS[1], "w_out", y, dx1_lo, "proj_out_dw")
    attn_state = _scatter_start(packs[1], packs[0], name="grads_attn_start")
    dz_a, do, g_conv_w, g_conv_norm, g_gla_norm = _mix_bwd(z_a, o_f, o_b, dy, conv_w_full, behind(conv_norm, attn_state[-1]), gla_norm, name="mix_out_bwd")
    dqkv_f, db_f, dqkv_b, db_b = _gla_bwd(z_b, b_f, b_b, do, st_f, st_b, name="gla_scan_bwd")
    dz_b, g_waf_p, g_wab_p, g_b_af, g_b_ab = _gate_bwd(z_b, waf_p, wab_p, b_af, b_ab, db_f, db_b, dqkv_f, dqkv_b, name="gates_bwd")
    g_za = _mm_tn(dz_a, h1, name="proj_in_a_dw")
    g_zb = _mm_tn(dz_b, h1, name="proj_in_b_dw")
    grads_t["w_in"] = jnp.concatenate([g_za[0:1536], g_zb[0:1024], g_za[1536:2048], g_zb[1024:1056]], axis=0)
    in_state = start_group(GRAD_GROUPS[2], "in")
    dh1_a = _mm(dz_a, w_za, mode="nn", name="proj_in_a_dx", tm=512, tk=ZA_COLS)
    grad_x, g_mix_norm = _mm_rows(
        dz_b, w_zb, mode="nn", name="proj_in_b_dx", rows=(xs, dx1, dh1_a), vecs=(behind(mix_norm, in_state[-1]),),
        out_rows=(F32,), out_vecs=(D_MODEL,), epilogue=_ep_norm_bwd)

    half_mlp = finish_group(mlp_state, grad_x, "mlp")
    half_attn = finish_group(attn_state, half_mlp, "attn")
    half_in = finish_group(in_state, half_attn, "in")
    shard_rows = {}
    for names, rows in zip(GRAD_GROUPS, _swap_all([half_mlp, half_attn, half_in], name="shards_to_sibling")):
        off = 0
        for name in names:
            shard_rows[name] = (rows, off)
            off += MATS[name][0]

    small_vals = dict(mix_norm=g_mix_norm, conv_norm=g_conv_norm, b_af=g_b_af, b_ab=g_b_ab, gla_norm=g_gla_norm, xa_norm=g_xa_norm,
                      mem_norm=g_mem_norm, mlp_norm=g_mlp_norm, final_norm=g_final_norm, conv_w=g_conv_w,
                      w_af=g_waf_p[0:GLA_LOWRANK], w_ab=g_wab_p[GLA_LOWRANK : 2 * GLA_LOWRANK], loss=loss_part)
    small = jnp.concatenate([small_vals[name].reshape(-1, 128) for name, _ in SMALL], axis=0)
    small = _sum_small(jnp.pad(small, ((0, SMALL_ROWS - small.shape[0]), (0, 0))), loss_part)
    g_small, off = {}, 0
    for name, n in SMALL:
        g_small[name] = small[off : off + n // 128]
        off += n // 128
    loss = g_small["loss"][0, 0]
    g_small["conv_w"] = lax.dynamic_slice(g_small["conv_w"].reshape(CONV_K, CONV_WIDTH), (0, 128 * chip), (CONV_K, 128))
    g_small["w_af"] = lax.dynamic_slice(g_small["w_af"].reshape(GLA_LOWRANK, GLA_K_TOTAL), (0, 64 * chip), (GLA_LOWRANK, 64))
    g_small["w_ab"] = lax.dynamic_slice(g_small["w_ab"].reshape(GLA_LOWRANK, GLA_K_TOTAL), (0, 64 * chip), (GLA_LOWRANK, 64))

    names = ["mix_norm", "w_in", "conv_w", "conv_norm", "w_af", "b_af", "w_ab", "b_ab", "gla_norm", "w_out", "xa_norm", "mem_norm",
             "w_xq", "w_xkv", "w_xo", "mlp_norm", "w_up", "w_down", "final_norm"]
    big_names = list(MATS)
    as2d = lambda a: a.reshape(1, -1) if a.ndim == 1 else a.reshape(a.shape[-2:])
    grads, deltas, new_m, new_v = {}, {}, {}, {}
    for name in big_names:
        rows, off = shard_rows[name]
        wmv = [as2d(given[name]), as2d(given["m_" + name]), as2d(given["v_" + name])]
        as_stored = name == "w_in"
        if as_stored:
            wmv = [a.T for a in wmv]
        res = _adamw(*wmv, rows, off, transposed=MATS[name][1] and not as_stored, name="adamw_" + name)
        grads[name], deltas[name], new_m[name], new_v[name] = [a.T for a in res] if as_stored else res
    small_names = [name for name in names if name not in big_names]
    groups = []
    for name in small_names:
        grads[name] = g_small[name].reshape(as2d(given[name]).shape)
        groups.append((as2d(given[name]), grads[name], as2d(given["m_" + name]), as2d(given["v_" + name])))
    for name, res in zip(small_names, _adamw_small(groups, name="adamw_small")):
        deltas[name], new_m[name], new_v[name] = res

    like = lambda name, a: a.reshape(given[name].shape)
    return (loss, grad_x[None], *[like(n, grads[n]) for n in names], *[like(n, deltas[n]) for n in names],
            *[like(n, new_m[n]) for n in names], *[like(n, new_v[n]) for n in names])
```

```python
import jax
import jax.numpy as jnp
from jax import lax
from jax.experimental import pallas as pl
from jax.experimental.pallas import tpu as pltpu

F32 = jnp.float32
BF16 = jnp.bfloat16
_CD = jnp.bfloat16
_TD = jnp.bfloat16

D_MODEL = 1024
N_MEM = 256
CONV_WIDTH = 512
CONV_GROUP = 64
CONV_K = 3
GLA_HEADS = 4
GLA_DK = 64
GLA_DV = 128
GLA_K_TOTAL = 256
GLA_V_TOTAL = 512
GLA_LOWRANK = 16
GLA_GATE_SCALE = 1.0 / 16.0
GLA_CHUNK = 64
XA_HEADS = 4
XA_HEAD_DIM = 256
D_FF = 4096
EPS = 1e-6
W_IN_COLS = 3104
ZA_COLS = 2048
ZB_COLS = 1152
LR_COL = 1024

ADAM_LR = 0.001
ADAM_B1 = 0.9
ADAM_B2 = 0.999
ADAM_EPS = 1e-08
ADAM_WD = 0.01
ADAM_STEP = 10

N_CHIPS = 4
SMALL_ROWS = 128

_TS = 512
_VMEM = 44 * 1024 * 1024
MESH = pl.DeviceIdType.MESH
ANY = pl.BlockSpec(memory_space=pl.ANY)


def _cp(sem=None, **kw):
    return pltpu.CompilerParams(dimension_semantics=sem, vmem_limit_bytes=_VMEM, **kw)


def _dot(a, b):
    return jnp.dot(a.astype(_CD), b.astype(_CD), preferred_element_type=F32)


def _dot_nt(a, b):
    return lax.dot_general(a.astype(_CD), b.astype(_CD), (((1,), (1,)), ((), ())), preferred_element_type=F32)


def _dot_tn(a, b):
    return lax.dot_general(a.astype(_CD), b.astype(_CD), (((0,), (0,)), ((), ())), preferred_element_type=F32)


def _dot_split(x, ones):
    hi = x.astype(BF16)
    r = x - hi.astype(F32)
    mid = r.astype(BF16)
    lo = (r - mid.astype(F32)).astype(BF16)
    d = lambda p: jnp.dot(p, ones, preferred_element_type=F32)
    return d(hi) + d(mid) + d(lo)


def _pick(n, cands=(1024, 640, 512, 256, 128)):
    for t in cands:
        if n % t == 0:
            return t
    return n


def _rows(s, light=False):
    return min(2 * _TS if light else _TS, s)


def _sigmoid(v):
    e = jnp.exp(-jnp.abs(v))
    return jnp.where(v >= 0, 1.0 / (1.0 + e), e / (1.0 + e))


def _mm(a, b, *, mode, name, out_dtypes=(F32,), extras=(), epilogue=None, tm=None, tn=None, tk=None):
    m, k = a.shape
    n = b.shape[1] if mode == "nn" else b.shape[0]
    tm = min(m, tm or 1024)
    tn = tn or _pick(n)
    tk = tk or _pick(k)
    nk = k // tk
    n_ex, n_out = len(extras), len(out_dtypes)

    def body(*refs):
        a_ref, b_ref = refs[:2]
        ex = refs[2 : 2 + n_ex]
        outs = refs[2 + n_ex : 2 + n_ex + n_out]
        part = _dot(a_ref[...], b_ref[...]) if mode == "nn" else _dot_nt(a_ref[...], b_ref[...])

        def finish(acc):
            res = epilogue(acc, *[e[...] for e in ex]) if epilogue else (acc,)
            for o, r in zip(outs, res):
                o[...] = r.astype(o.dtype)

        if nk == 1:
            finish(part)
        else:
            acc_ref = refs[-1]
            kk = pl.program_id(2)

            @pl.when(kk == 0)
            def _():
                acc_ref[...] = part

            @pl.when(kk > 0)
            def _():
                acc_ref[...] += part

            @pl.when(kk == nk - 1)
            def _():
                finish(acc_ref[...])

    b_spec = pl.BlockSpec((tk, tn), lambda i, j, kk: (kk, j)) if mode == "nn" else pl.BlockSpec((tn, tk), lambda i, j, kk: (j, kk))
    tile = pl.BlockSpec((tm, tn), lambda i, j, kk: (i, j))
    out = pl.pallas_call(
        body,
        name=name,
        grid=(m // tm, n // tn, nk),
        in_specs=[pl.BlockSpec((tm, tk), lambda i, j, kk: (i, kk)), b_spec] + [tile] * n_ex,
        out_specs=[tile] * n_out,
        out_shape=[jax.ShapeDtypeStruct((m, n), dt) for dt in out_dtypes],
        scratch_shapes=[pltpu.VMEM((tm, tn), F32)] if nk > 1 else [],
        compiler_params=_cp(("parallel", "parallel", "arbitrary")),
    )(a, b, *extras)
    return out[0] if n_out == 1 else out


def _mm_tn(a, b, *, name):
    s, m = a.shape
    n = b.shape[1]
    cap = max(128, (1 << 20) // n)
    tm = _pick(m, tuple(t for t in (512, 640, 384, 256, 128) if t <= max(cap, 128)))
    ts = min(s, 1 << (((1 << 22) // n).bit_length() - 1))
    ns = s // ts

    def body(a_ref, b_ref, o_ref):
        part = _dot_tn(a_ref[...], b_ref[...])
        if ns == 1:
            o_ref[...] = part
        else:
            ss = pl.program_id(1)

            @pl.when(ss == 0)
            def _():
                o_ref[...] = part

            @pl.when(ss > 0)
            def _():
                o_ref[...] += part

    return pl.pallas_call(
        body,
        name=name,
        grid=(m // tm, ns),
        in_specs=[pl.BlockSpec((ts, tm), lambda i, ss: (ss, i)), pl.BlockSpec((ts, n), lambda i, ss: (ss, 0))],
        out_specs=pl.BlockSpec((tm, n), lambda i, ss: (i, 0)),
        out_shape=jax.ShapeDtypeStruct((m, n), F32),
        compiler_params=_cp(("parallel", "arbitrary")),
    )(a, b)


def _mm_tn_into(a, b, packs, *, rows, off, name):
    s, m = a.shape
    n = b.shape[1]
    tm = 1024 if rows % 1024 == 0 and s >= 4096 else 512
    tr = min(tm, rows)
    per, chips = rows // tr, tm // tr
    ts = min(s, (1 << (((1 << 22) // n).bit_length() - 1)) * 512 // tm)
    ns = s // ts

    def body(a_ref, b_ref, f_in, lo_in, f_ref, lo_ref):
        part = _dot_tn(a_ref[...], b_ref[...])
        pieces = [part[c * tr : (c + 1) * tr] for c in range(chips)]
        if ns == 1:
            for c, p in enumerate(pieces):
                f_ref[c] = p
                lo_ref[c] = p.astype(lo_ref.dtype)
        else:
            ss = pl.program_id(1)

            @pl.when(ss == 0)
            def _():
                for c, p in enumerate(pieces):
                    f_ref[c] = p

            @pl.when(ss > 0)
            def _():
                for c, p in enumerate(pieces):
                    f_ref[c] += p

            @pl.when(ss == ns - 1)
            def _():
                lo_ref[...] = f_ref[...].astype(lo_ref.dtype)

    spec = pl.BlockSpec((chips, tr, n), lambda i, ss: (i // per, off // tr + i % per, 0))
    return pl.pallas_call(
        body,
        name=name,
        grid=(m // tm, ns),
        in_specs=[pl.BlockSpec((ts, tm), lambda i, ss: (ss, i)), pl.BlockSpec((ts, n), lambda i, ss: (ss, 0)), ANY, ANY],
        out_specs=[spec, spec],
        out_shape=[jax.ShapeDtypeStruct(p.shape, p.dtype) for p in packs],
        input_output_aliases={2: 0, 3: 1},
        compiler_params=_cp(("parallel", "arbitrary")),
    )(a, b, *packs)


def _mm_rows(a, b, *, mode, name, rows=(), vecs=(), out_rows=(), out_vecs=(), epilogue, tm=512):
    m, k = a.shape
    n = b.shape[1] if mode == "nn" else b.shape[0]
    tm = min(m, tm)
    parts = 2 if tm % 256 == 0 else 1
    n_r, n_v, n_or, n_ov = len(rows), len(vecs), len(out_rows), len(out_vecs)

    def body(*refs):
        a_ref, b_ref = refs[:2]
        r_refs = refs[2 : 2 + n_r]
        v_refs = refs[2 + n_r : 2 + n_r + n_v]
        or_refs = refs[2 + n_r + n_v : 2 + n_r + n_v + n_or]
        ov_refs = refs[2 + n_r + n_v + n_or :]
        res_vecs = None
        for p in range(parts):
            rs = slice(p * tm // parts, (p + 1) * tm // parts)
            acc = _dot(a_ref[rs, :], b_ref[...]) if mode == "nn" else _dot_nt(a_ref[rs, :], b_ref[...])
            res_rows, part_vecs = epilogue(acc, [r[rs, :] for r in r_refs], [v[...] for v in v_refs])
            for o, r in zip(or_refs, res_rows):
                o[rs, :] = r.astype(o.dtype)
            res_vecs = part_vecs if res_vecs is None else [s + t for s, t in zip(res_vecs, part_vecs)]
        if n_ov:
            first = pl.program_id(0) == 0

            @pl.when(first)
            def _():
                for o, r in zip(ov_refs, res_vecs):
                    o[...] = r

            @pl.when(jnp.logical_not(first))
            def _():
                for o, r in zip(ov_refs, res_vecs):
                    o[...] += r

    tile = pl.BlockSpec((tm, n), lambda i: (i, 0))
    whole = lambda arr: pl.BlockSpec(arr.shape, lambda i: (0, 0))
    vec = lambda w: pl.BlockSpec((1, w), lambda i: (0, 0))
    out = pl.pallas_call(
        body,
        name=name,
        grid=(m // tm,),
        in_specs=[pl.BlockSpec((tm, k), lambda i: (i, 0)), whole(b)] + [tile] * n_r + [vec(v.shape[1]) for v in vecs],
        out_specs=[tile] * n_or + [vec(w) for w in out_vecs],
        out_shape=[jax.ShapeDtypeStruct((m, n), dt) for dt in out_rows] + [jax.ShapeDtypeStruct((1, w), F32) for w in out_vecs],
        compiler_params=_cp(("arbitrary",) if n_ov else ("parallel",)),
    )(a, b, *rows, *vecs)
    return out


def _ep_residual_norm(acc, rows, vecs):
    x = acc + rows[0]
    r = lax.rsqrt(jnp.mean(x * x, axis=-1, keepdims=True) + EPS)
    return [x, x * r * vecs[0]], []


def _ep_norm_bwd(acc, rows, vecs):
    dy = acc
    for extra in rows[2:]:
        dy = dy + extra
    x, dres = rows[0], rows[1]
    r = lax.rsqrt(jnp.mean(x * x, axis=-1, keepdims=True) + EPS)
    xh = x * r
    dxh = dy * vecs[0]
    dx = r * (dxh - xh * jnp.mean(dxh * xh, axis=-1, keepdims=True)) + dres
    return [dx, dx], [jnp.sum(dy * xh, axis=0, keepdims=True)]


def _ep_loss(acc, rows, vecs):
    x = acc + rows[0]
    d = x.shape[-1]
    r = lax.rsqrt(jnp.mean(x * x, axis=-1, keepdims=True) + EPS)
    xh = x * r
    err = xh * vecs[0] - rows[1]
    loss = jnp.zeros((1, 128), F32) + 0.5 * jnp.sum(jnp.mean(err * err, axis=-1, keepdims=True))
    dy = err * (1.0 / d)
    dxh = dy * vecs[0]
    dx = r * (dxh - xh * jnp.mean(dxh * xh, axis=-1, keepdims=True))
    return [dx, dx], [loss, jnp.sum(dy * xh, axis=0, keepdims=True)]


def _rms_fwd(x, g, *, name):
    s, d = x.shape
    ts = _rows(s, light=True)

    def body(x_ref, g_ref, o_ref):
        xf = x_ref[...]
        r = lax.rsqrt(jnp.mean(xf * xf, axis=-1, keepdims=True) + EPS)
        o_ref[...] = (xf * r * g_ref[...]).astype(o_ref.dtype)

    return pl.pallas_call(
        body,
        name=name,
        grid=(s // ts,),
        in_specs=[pl.BlockSpec((ts, d), lambda i: (i, 0)), pl.BlockSpec((1, d), lambda i: (0, 0))],
        out_specs=pl.BlockSpec((ts, d), lambda i: (i, 0)),
        out_shape=jax.ShapeDtypeStruct((s, d), _CD),
        compiler_params=_cp(("parallel",)),
    )(x, g)


def _rms_gain_grad(x, dy, *, name):
    s, d = x.shape
    ts = _rows(s)

    def body(x_ref, dy_ref, dg_ref):
        xf = x_ref[...]
        r = lax.rsqrt(jnp.mean(xf * xf, axis=-1, keepdims=True) + EPS)
        part = jnp.sum(dy_ref[...] * (xf * r), axis=0, keepdims=True)

        @pl.when(pl.program_id(0) == 0)
        def _():
            dg_ref[...] = part

        @pl.when(pl.program_id(0) > 0)
        def _():
            dg_ref[...] += part

    tile = pl.BlockSpec((ts, d), lambda i: (i, 0))
    return pl.pallas_call(
        body,
        name=name,
        grid=(s // ts,),
        in_specs=[tile, tile],
        out_specs=pl.BlockSpec((1, d), lambda i: (0, 0)),
        out_shape=jax.ShapeDtypeStruct((1, d), F32),
        compiler_params=_cp(("arbitrary",)),
    )(x, dy)


def _chunk_scan(v, row_in_chunk, suffix):
    t = v.shape[0]
    step = 1
    while step < GLA_CHUNK:
        if suffix:
            v = v + jnp.where(row_in_chunk < GLA_CHUNK - step, pltpu.roll(v, t - step, 0), 0.0)
        else:
            v = v + jnp.where(row_in_chunk >= step, pltpu.roll(v, step, 0), 0.0)
        step *= 2
    return v


def _gate_pre(lr, w_ref, b_ref):
    return _dot(lr, w_ref[...]) + b_ref[...]


def _gate_fwd(z, waf, wab, baf, bab, *, name):
    s = z.shape[0]
    ts = _rows(s, light=True)

    def body(lr_ref, waf_ref, wab_ref, baf_ref, bab_ref, bf_ref, bb_ref):
        lr = lr_ref[...]
        ric = lax.broadcasted_iota(jnp.int32, (ts, GLA_K_TOTAL), 0) & (GLA_CHUNK - 1)
        for w_ref, b_ref, o_ref, suffix in ((waf_ref, baf_ref, bf_ref, False), (wab_ref, bab_ref, bb_ref, True)):
            pre = _gate_pre(lr, w_ref, b_ref)
            la = (jnp.minimum(pre, 0.0) - jnp.log(1.0 + jnp.exp(-jnp.abs(pre)))) * GLA_GATE_SCALE
            o_ref[...] = _chunk_scan(la, ric, suffix)

    wspec = pl.BlockSpec((128, GLA_K_TOTAL), lambda i: (0, 0))
    bspec = pl.BlockSpec((1, GLA_K_TOTAL), lambda i: (0, 0))
    tile = pl.BlockSpec((ts, GLA_K_TOTAL), lambda i: (i, 0))
    return pl.pallas_call(
        body,
        name=name,
        grid=(s // ts,),
        in_specs=[pl.BlockSpec((ts, 128), lambda i: (i, LR_COL // 128)), wspec, wspec, bspec, bspec],
        out_specs=[tile, tile],
        out_shape=[jax.ShapeDtypeStruct((s, GLA_K_TOTAL), F32)] * 2,
        compiler_params=_cp(("parallel",)),
    )(z, waf, wab, baf, bab)


def _gate_bwd(z, waf, wab, baf, bab, dbf, dbb, dqkv_f, dqkv_b, *, name):
    s = z.shape[0]
    ts = _rows(s, light=True)

    def body(lr_ref, waf_ref, wab_ref, baf_ref, bab_ref, dbf_ref, dbb_ref, gf_ref, gb_ref, dzb_ref, dwf_ref, dwb_ref, dbaf_ref, dbab_ref):
        lr = lr_ref[...]
        ric = lax.broadcasted_iota(jnp.int32, (ts, GLA_K_TOTAL), 0) & (GLA_CHUNK - 1)
        first = pl.program_id(0) == 0
        dlr = None
        for w_ref, b_ref, db_ref, dw_ref, dbias_ref, suffix in (
            (waf_ref, baf_ref, dbf_ref, dwf_ref, dbaf_ref, True),
            (wab_ref, bab_ref, dbb_ref, dwb_ref, dbab_ref, False),
        ):
            pre = _gate_pre(lr, w_ref, b_ref)
            dla = _chunk_scan(db_ref[...], ric, suffix)
            dpre = dla * GLA_GATE_SCALE * _sigmoid(-pre)
            part = _dot_nt(dpre, w_ref[...])
            dlr = part if dlr is None else dlr + part
            dw = _dot_tn(lr, dpre)
            dbias = jnp.sum(dpre, axis=0, keepdims=True)

            @pl.when(first)
            def _():
                dw_ref[...] = dw
                dbias_ref[...] = dbias

            @pl.when(jnp.logical_not(first))
            def _():
                dw_ref[...] += dw
                dbias_ref[...] += dbias

        dzb_ref[...] = jnp.concatenate([gf_ref[...] + gb_ref[...], dlr], axis=1).astype(dzb_ref.dtype)

    wspec = pl.BlockSpec((128, GLA_K_TOTAL), lambda i: (0, 0))
    bspec = pl.BlockSpec((1, GLA_K_TOTAL), lambda i: (0, 0))
    tile = pl.BlockSpec((ts, GLA_K_TOTAL), lambda i: (i, 0))
    wide = pl.BlockSpec((ts, 2 * GLA_K_TOTAL + GLA_V_TOTAL), lambda i: (i, 0))
    return pl.pallas_call(
        body,
        name=name,
        grid=(s // ts,),
        in_specs=[pl.BlockSpec((ts, 128), lambda i: (i, LR_COL // 128)), wspec, wspec, bspec, bspec, tile, tile, wide, wide],
        out_specs=[pl.BlockSpec((ts, ZB_COLS), lambda i: (i, 0)), wspec, wspec, bspec, bspec],
        out_shape=[
            jax.ShapeDtypeStruct((s, ZB_COLS), _CD),
            jax.ShapeDtypeStruct((128, GLA_K_TOTAL), F32),
            jax.ShapeDtypeStruct((128, GLA_K_TOTAL), F32),
            jax.ShapeDtypeStruct((1, GLA_K_TOTAL), F32),
            jax.ShapeDtypeStruct((1, GLA_K_TOTAL), F32),
        ],
        compiler_params=_cp(("arbitrary",)),
    )(z, waf, wab, baf, bab, dbf, dbb, dqkv_f, dqkv_b)


def _gla_masks(rev):
    lane_head = lax.broadcasted_iota(jnp.int32, (1, GLA_K_TOTAL), 1) >> 6
    head_masks = [lane_head == h for h in range(GLA_HEADS)]
    t = lax.broadcasted_iota(jnp.int32, (GLA_HEADS * GLA_CHUNK, GLA_CHUNK), 0) & (GLA_CHUNK - 1)
    u = lax.broadcasted_iota(jnp.int32, (GLA_HEADS * GLA_CHUNK, GLA_CHUNK), 1)
    tri = (u > t) if rev else (u <= t)
    row = lax.broadcasted_iota(jnp.int32, (GLA_CHUNK, GLA_K_TOTAL), 0)
    total_row = row == (0 if rev else GLA_CHUNK - 1)
    return head_masks, tri, total_row


def _spread(a, head_masks):
    return jnp.concatenate([jnp.where(m, a, 0.0) for m in head_masks], axis=0)


def _stack(a):
    return jnp.concatenate([a[:, GLA_DV * h : GLA_DV * (h + 1)] for h in range(GLA_HEADS)], axis=0)


def _unstack(a):
    return jnp.concatenate([a[GLA_CHUNK * h : GLA_CHUNK * (h + 1)] for h in range(GLA_HEADS)], axis=1)


def _collect(a, head_masks):
    out = None
    for h, m in enumerate(head_masks):
        part = jnp.where(m, a[GLA_CHUNK * h : GLA_CHUNK * (h + 1)], 0.0)
        out = part if out is None else out + part
    return out


def _gla_chunk_terms(q_ref, k_ref, v_ref, b_ref, rows, head_masks, tri, total_row):
    q = q_ref[rows, :] * (GLA_DK**-0.5)
    k = k_ref[rows, :]
    v = v_ref[rows, :]
    b = b_ref[rows, :]
    eb = jnp.exp(b)
    enb = jnp.exp(-b)
    g = jnp.sum(jnp.where(total_row, b, 0.0), axis=0, keepdims=True)
    egb = jnp.exp(g - b)
    qt = q * eb
    kt = k * enb
    kh = k * egb
    q_heads = _spread(qt, head_masks)
    attn = jnp.where(tri, _dot_nt(q_heads, kt), 0.0)
    return v, eb, enb, egb, jnp.exp(g), qt, kt, kh, q_heads, attn


def _gla_specs(s, tb, rev_blocks):
    nb = s // tb
    rb = (lambda i: nb - 1 - i) if rev_blocks else (lambda i: i)
    q_spec = pl.BlockSpec((tb, GLA_K_TOTAL), lambda i: (rb(i), 0))
    k_spec = pl.BlockSpec((tb, GLA_K_TOTAL), lambda i: (rb(i), 1))
    v_spec = pl.BlockSpec((tb, GLA_V_TOTAL), lambda i: (rb(i), 1))
    b_spec = pl.BlockSpec((tb, GLA_K_TOTAL), lambda i: (rb(i), 0))
    o_spec = pl.BlockSpec((tb, GLA_V_TOTAL), lambda i: (rb(i), 0))
    st_spec = pl.BlockSpec((tb // GLA_CHUNK, GLA_DV, GLA_K_TOTAL), lambda i: (rb(i), 0, 0))
    return nb, q_spec, k_spec, v_spec, b_spec, o_spec, st_spec


def _gla_fwd_chunk(cidx, q_ref, k_ref, v_ref, b_ref, o_ref, sv_ref, st_ref, masks):
    head_masks, tri, total_row = masks
    rows = pl.ds(pl.multiple_of(cidx * GLA_CHUNK, GLA_CHUNK), GLA_CHUNK)
    v, _, _, _, eg, _, _, kh, q_heads, attn = _gla_chunk_terms(q_ref, k_ref, v_ref, b_ref, rows, head_masks, tri, total_row)
    o = jnp.concatenate(
        [_dot(attn[GLA_CHUNK * h : GLA_CHUNK * (h + 1)], v[:, GLA_DV * h : GLA_DV * (h + 1)]) for h in range(GLA_HEADS)], axis=1
    )
    st = st_ref[...]
    o_ref[rows, :] = o + _unstack(_dot_nt(q_heads, st))
    sv_ref[cidx] = st
    st_ref[...] = st * eg + _dot_tn(_stack(v), _spread(kh, head_masks))


def _gla_fwd(z, b_f, b_b, *, name):
    s = z.shape[0]
    tb = _rows(s)
    cpb = tb // GLA_CHUNK
    nb, qf, kf, vf, bf, of, sf = _gla_specs(s, tb, False)
    _, qr, kr, vr, br, orr, sr = _gla_specs(s, tb, True)

    def body(qf_ref, kf_ref, vf_ref, bf_ref, qr_ref, kr_ref, vr_ref, br_ref, of_ref, svf_ref, or_ref, svr_ref, stf_ref, str_ref):
        masks_f, masks_r = _gla_masks(False), _gla_masks(True)

        @pl.when(pl.program_id(0) == 0)
        def _():
            stf_ref[...] = jnp.zeros_like(stf_ref)
            str_ref[...] = jnp.zeros_like(str_ref)

        def chunk(ci, carry):
            _gla_fwd_chunk(ci, qf_ref, kf_ref, vf_ref, bf_ref, of_ref, svf_ref, stf_ref, masks_f)
            _gla_fwd_chunk(cpb - 1 - ci, qr_ref, kr_ref, vr_ref, br_ref, or_ref, svr_ref, str_ref, masks_r)
            return carry

        lax.fori_loop(0, cpb, chunk, 0)

    o_shape = jax.ShapeDtypeStruct((s, GLA_V_TOTAL), F32)
    st_shape = jax.ShapeDtypeStruct((s // GLA_CHUNK, GLA_DV, GLA_K_TOTAL), F32)
    return pl.pallas_call(
        body,
        name=name,
        grid=(nb,),
        in_specs=[qf, kf, vf, bf, qr, kr, vr, br],
        out_specs=[of, sf, orr, sr],
        out_shape=[o_shape, st_shape, o_shape, st_shape],
        scratch_shapes=[pltpu.VMEM((GLA_DV, GLA_K_TOTAL), F32)] * 2,
        compiler_params=_cp(("arbitrary",)),
    )(z, z, z, b_f, z, z, z, b_b)


def _gla_bwd_chunk(cidx, q_ref, k_ref, v_ref, b_ref, do_ref, sv_ref, dqkv_ref, db_ref, dst_ref, masks):
    head_masks, tri, total_row = masks
    rows = pl.ds(pl.multiple_of(cidx * GLA_CHUNK, GLA_CHUNK), GLA_CHUNK)
    v, eb, enb, egb, eg, qt, kt, kh, q_heads, attn = _gla_chunk_terms(q_ref, k_ref, v_ref, b_ref, rows, head_masks, tri, total_row)
    do_c = do_ref[rows, :]
    st = sv_ref[cidx]
    dst = dst_ref[...]
    do_s, v_s = _stack(do_c), _stack(v)
    hs = lambda a, h: a[GLA_CHUNK * h : GLA_CHUNK * (h + 1)]
    vs = lambda a, h: a[:, GLA_DV * h : GLA_DV * (h + 1)]
    dattn = jnp.concatenate([_dot_nt(vs(do_c, h), vs(v, h)) for h in range(GLA_HEADS)], axis=0)
    dattn = jnp.where(tri, dattn, 0.0)
    dv = jnp.concatenate([_dot_tn(hs(attn, h), vs(do_c, h)) for h in range(GLA_HEADS)], axis=1)
    dv = dv + _unstack(_dot_nt(_spread(kh, head_masks), dst))
    dqt = _collect(_dot(do_s, st) + _dot(dattn, kt), head_masks)
    dkt = _dot_tn(dattn, q_heads)
    dkh = _collect(_dot(v_s, dst), head_masks)
    dg = jnp.sum(dkh * kh, axis=0, keepdims=True) + jnp.sum(dst * st, axis=0, keepdims=True) * eg
    db = dqt * qt - dkt * kt - dkh * kh + jnp.where(total_row, dg, 0.0)
    dq = dqt * eb * (GLA_DK**-0.5)
    dk = dkt * enb + dkh * egb
    dqkv_ref[rows, :] = jnp.concatenate([dq, dk, dv], axis=1)
    db_ref[rows, :] = db
    dst_ref[...] = dst * eg + _dot_tn(do_s, q_heads)


def _gla_bwd(z, b_f, b_b, do, st_f, st_b, *, name):
    s = z.shape[0]
    tb = _rows(s)
    cpb = tb // GLA_CHUNK
    wide = 2 * GLA_K_TOTAL + GLA_V_TOTAL
    nb, qf, kf, vf, bf, of, sf = _gla_specs(s, tb, True)
    _, qr, kr, vr, br, orr, sr = _gla_specs(s, tb, False)
    gf = pl.BlockSpec((tb, wide), lambda i: (nb - 1 - i, 0))
    gr = pl.BlockSpec((tb, wide), lambda i: (i, 0))

    def body(qf_ref, kf_ref, vf_ref, bf_ref, dof_ref, svf_ref, qr_ref, kr_ref, vr_ref, br_ref, dor_ref, svr_ref,
             gf_ref, dbf_ref, gr_ref, dbr_ref, dstf_ref, dstr_ref):
        masks_f, masks_r = _gla_masks(False), _gla_masks(True)

        @pl.when(pl.program_id(0) == 0)
        def _():
            dstf_ref[...] = jnp.zeros_like(dstf_ref)
            dstr_ref[...] = jnp.zeros_like(dstr_ref)

        def chunk(ci, carry):
            _gla_bwd_chunk(cpb - 1 - ci, qf_ref, kf_ref, vf_ref, bf_ref, dof_ref, svf_ref, gf_ref, dbf_ref, dstf_ref, masks_f)
            _gla_bwd_chunk(ci, qr_ref, kr_ref, vr_ref, br_ref, dor_ref, svr_ref, gr_ref, dbr_ref, dstr_ref, masks_r)
            return carry

        lax.fori_loop(0, cpb, chunk, 0)

    g_shape = jax.ShapeDtypeStruct((s, wide), F32)
    db_shape = jax.ShapeDtypeStruct((s, GLA_K_TOTAL), F32)
    return pl.pallas_call(
        body,
        name=name,
        grid=(nb,),
        in_specs=[qf, kf, vf, bf, of, sf, qr, kr, vr, br, orr, sr],
        out_specs=[gf, bf, gr, br],
        out_shape=[g_shape, db_shape, g_shape, db_shape],
        scratch_shapes=[pltpu.VMEM((GLA_DV, GLA_K_TOTAL), F32)] * 2,
        compiler_params=_cp(("arbitrary",)),
    )(z, z, z, b_f, do, st_f, z, z, z, b_b, do, st_b)


HALO = 8


def _halo_specs(s, ts, width, col):
    last = s // HALO - 1
    per = ts // HALO
    prev = pl.BlockSpec((HALO, width), lambda i: (jnp.maximum(i * per - 1, 0), col))
    nxt = pl.BlockSpec((HALO, width), lambda i: (jnp.minimum((i + 1) * per, last), col))
    return prev, nxt


def _group_ones():
    group = jnp.arange(CONV_WIDTH, dtype=jnp.int32) // CONV_GROUP
    return (group[:, None] == group[None, :]).astype(BF16)


_ONES_SPEC = pl.BlockSpec((CONV_WIDTH, CONV_WIDTH), lambda i: (0, 0))


def _conv_terms(cc_ext, cu_ext, cw, valid):
    n = cc_ext.shape[0]
    hc = jnp.where(valid, cc_ext * cu_ext, 0.0)
    hc_prev = pltpu.roll(hc, 1, 0)
    hc_next = pltpu.roll(hc, n - 1, 0)
    conv = cw[0:1] * hc_prev + cw[1:2] * hc + cw[2:3] * hc_next
    return hc, hc_prev, hc_next, conv


def _ext(prev_ref, cur_ref, next_ref):
    return jnp.concatenate([prev_ref[...], cur_ref[...], next_ref[...]], axis=0)


def _valid_rows(ts, s):
    row = lax.broadcasted_iota(jnp.int32, (ts + 2 * HALO, 1), 0) + (pl.program_id(0) * ts - HALO)
    return (row >= 0) & (row < s)


def _head_norm(o, gn):
    out = []
    for h in range(GLA_HEADS):
        oh = o[:, GLA_DV * h : GLA_DV * (h + 1)]
        r = lax.rsqrt(jnp.mean(oh * oh, axis=-1, keepdims=True) + EPS)
        out.append((oh * r, r))
    return out


def _mix_fwd(z, o_f, o_b, conv_w, conv_norm, gla_norm, *, name):
    s = z.shape[0]
    ts = _rows(s, light=True)
    cprev, cnext = _halo_specs(s, ts, CONV_WIDTH, 1)
    uprev, unext = _halo_specs(s, ts, CONV_WIDTH, 2)

    def body(cb_ref, cc_ref, cu_ref, ccp_ref, ccn_ref, cup_ref, cun_ref, g_ref, of_ref, ob_ref, cw_ref, cn_ref, gn_ref, ones_ref, y_ref):
        valid = _valid_rows(ts, s)
        _, _, _, conv = _conv_terms(_ext(ccp_ref, cc_ref, ccn_ref), _ext(cup_ref, cu_ref, cun_ref), cw_ref[...], valid)
        yc = cb_ref[...] * conv[HALO : HALO + ts]
        ms = _dot_split(yc * yc, ones_ref[...]) * (1.0 / CONV_GROUP)
        y_conv = yc * lax.rsqrt(ms + EPS) * cn_ref[...]
        gate = g_ref[...]
        silu = gate * _sigmoid(gate)
        gn = gn_ref[...]
        y_gla = jnp.concatenate([oh * gn for oh, _ in _head_norm(of_ref[...] + ob_ref[...], gn)], axis=1) * silu
        y_ref[...] = jnp.concatenate([y_conv, y_gla], axis=1).astype(y_ref.dtype)

    col = lambda c, w=CONV_WIDTH: pl.BlockSpec((ts, w), lambda i: (i, c))
    return pl.pallas_call(
        body,
        name=name,
        grid=(s // ts,),
        in_specs=[col(0), col(1), col(2), cprev, cnext, uprev, unext, col(3), col(0), col(0),
                  pl.BlockSpec((CONV_K, CONV_WIDTH), lambda i: (0, 0)), pl.BlockSpec((1, CONV_WIDTH), lambda i: (0, 0)),
                  pl.BlockSpec((1, GLA_DV), lambda i: (0, 0)), _ONES_SPEC],
        out_specs=pl.BlockSpec((ts, D_MODEL), lambda i: (i, 0)),
        out_shape=jax.ShapeDtypeStruct((s, D_MODEL), _CD),
        compiler_params=_cp(("parallel",)),
    )(z, z, z, z, z, z, z, z, o_f, o_b, conv_w, conv_norm, gla_norm, _group_ones())


def _mix_bwd(z, o_f, o_b, dy, conv_w, conv_norm, gla_norm, *, name):
    s = z.shape[0]
    ts = _rows(s)
    halos = [_halo_specs(s, ts, CONV_WIDTH, c) for c in (0, 1, 2)]
    dprev, dnext = _halo_specs(s, ts, CONV_WIDTH, 0)

    def body(cb_ref, cc_ref, cu_ref, cbp_ref, cbn_ref, ccp_ref, ccn_ref, cup_ref, cun_ref, g_ref, of_ref, ob_ref,
             dyc_ref, dyg_ref, dyp_ref, dyn_ref, cw_ref, cn_ref, gn_ref, ones_ref, dza_ref, do_ref, dcw_ref, dcn_ref, dgn_ref):
        n = ts + 2 * HALO
        valid = _valid_rows(ts, s)
        cw = cw_ref[...]
        cn = cn_ref[...]
        ones = ones_ref[...]
        cb = _ext(cbp_ref, cb_ref, cbn_ref)
        cc = _ext(ccp_ref, cc_ref, ccn_ref)
        cu = _ext(cup_ref, cu_ref, cun_ref)
        dy = _ext(dyp_ref, dyc_ref, dyn_ref)
        hc, hc_prev, hc_next, conv = _conv_terms(cc, cu, cw, valid)
        yc = cb * conv
        r = lax.rsqrt(_dot_split(yc * yc, ones) * (1.0 / CONV_GROUP) + EPS)
        yh = yc * r
        dyh = dy * cn
        dyc = r * (dyh - yh * (_dot_split(dyh * yh, ones) * (1.0 / CONV_GROUP)))
        dconv = jnp.where(valid, dyc * cb, 0.0)
        dhc = cw[0:1] * pltpu.roll(dconv, n - 1, 0) + cw[1:2] * dconv + cw[2:3] * pltpu.roll(dconv, 1, 0)
        mid = lambda a: a[HALO : HALO + ts]
        dza_ref[:, 0 : 3 * CONV_WIDTH] = jnp.concatenate([mid(dyc * conv), mid(dhc * cu), mid(dhc * cc)], axis=1).astype(dza_ref.dtype)
        dconv_m = mid(dconv)
        colsum = lambda a: jnp.sum(a, axis=0, keepdims=True)
        dcw = jnp.concatenate([colsum(dconv_m * mid(hc_prev)), colsum(dconv_m * mid(hc)), colsum(dconv_m * mid(hc_next))], axis=0)
        dcn = colsum(mid(dy * yh))

        gate = g_ref[...]
        sg = _sigmoid(gate)
        silu = gate * sg
        gn = gn_ref[...]
        dyg = dyg_ref[...]
        don = dyg * silu
        heads = _head_norm(of_ref[...] + ob_ref[...], gn)
        on = jnp.concatenate([oh * gn for oh, _ in heads], axis=1)
        dza_ref[:, 3 * CONV_WIDTH : ZA_COLS] = (dyg * on * (sg * (1.0 + gate * (1.0 - sg)))).astype(dza_ref.dtype)
        dgn = jnp.zeros((1, GLA_DV), F32)
        dos = []
        for h, (oh, rh) in enumerate(heads):
            donh = don[:, GLA_DV * h : GLA_DV * (h + 1)]
            dgn = dgn + colsum(donh * oh)
            doh = donh * gn
            dos.append(rh * (doh - oh * jnp.mean(doh * oh, axis=-1, keepdims=True)))
        do_ref[...] = jnp.concatenate(dos, axis=1)

        first = pl.program_id(0) == 0

        @pl.when(first)
        def _():
            dcw_ref[...] = dcw
            dcn_ref[...] = dcn
            dgn_ref[...] = dgn

        @pl.when(jnp.logical_not(first))
        def _():
            dcw_ref[...] += dcw
            dcn_ref[...] += dcn
            dgn_ref[...] += dgn

    col = lambda c, w=CONV_WIDTH: pl.BlockSpec((ts, w), lambda i: (i, c))
    cw_spec = pl.BlockSpec((CONV_K, CONV_WIDTH), lambda i: (0, 0))
    cn_spec = pl.BlockSpec((1, CONV_WIDTH), lambda i: (0, 0))
    gn_spec = pl.BlockSpec((1, GLA_DV), lambda i: (0, 0))
    return pl.pallas_call(
        body,
        name=name,
        grid=(s // ts,),
        in_specs=[col(0), col(1), col(2), halos[0][0], halos[0][1], halos[1][0], halos[1][1], halos[2][0], halos[2][1],
                  col(3), col(0), col(0), col(0), col(1), dprev, dnext, cw_spec, cn_spec, gn_spec, _ONES_SPEC],
        out_specs=[pl.BlockSpec((ts, ZA_COLS), lambda i: (i, 0)), col(0), cw_spec, cn_spec, gn_spec],
        out_shape=[
            jax.ShapeDtypeStruct((s, ZA_COLS), _CD),
            jax.ShapeDtypeStruct((s, GLA_V_TOTAL), F32),
            jax.ShapeDtypeStruct((CONV_K, CONV_WIDTH), F32),
            jax.ShapeDtypeStruct((1, CONV_WIDTH), F32),
            jax.ShapeDtypeStruct((1, GLA_DV), F32),
        ],
        compiler_params=_cp(("arbitrary",)),
    )(z, z, z, z, z, z, z, z, z, z, o_f, o_b, dy, dy, dy, dy, conv_w, conv_norm, gla_norm, _group_ones())


def _xa_probs(q_ref, kv_ref, h):
    qh = q_ref[:, XA_HEAD_DIM * h : XA_HEAD_DIM * (h + 1)]
    kh = kv_ref[:, XA_HEAD_DIM * h : XA_HEAD_DIM * (h + 1)]
    vh = kv_ref[:, D_MODEL + XA_HEAD_DIM * h : D_MODEL + XA_HEAD_DIM * (h + 1)]
    sc = _dot_nt(qh, kh) * (XA_HEAD_DIM**-0.5)
    e = jnp.exp(sc - jnp.max(sc, axis=-1, keepdims=True))
    return qh, kh, vh, e / jnp.sum(e, axis=-1, keepdims=True)


def _xattn_fwd(qx, kv, *, name):
    s = qx.shape[0]
    ts = _rows(s, light=True)

    def body(q_ref, kv_ref, o_ref):
        outs = []
        for h in range(XA_HEADS):
            _, _, vh, p = _xa_probs(q_ref, kv_ref, h)
            outs.append(_dot(p, vh))
        o_ref[...] = jnp.concatenate(outs, axis=1).astype(o_ref.dtype)

    return pl.pallas_call(
        body,
        name=name,
        grid=(s // ts,),
        in_specs=[pl.BlockSpec((ts, D_MODEL), lambda i: (i, 0)), pl.BlockSpec((N_MEM, 2 * D_MODEL), lambda i: (0, 0))],
        out_specs=pl.BlockSpec((ts, D_MODEL), lambda i: (i, 0)),
        out_shape=jax.ShapeDtypeStruct((s, D_MODEL), _CD),
        compiler_params=_cp(("parallel",)),
    )(qx, kv)


def _xattn_bwd(qx, kv, dox, *, name):
    s = qx.shape[0]
    ts = _rows(s, light=True)

    def body(q_ref, kv_ref, do_ref, dq_ref, dkv_ref):
        dqs, dks, dvs = [], [], []
        for h in range(XA_HEADS):
            qh, kh, vh, p = _xa_probs(q_ref, kv_ref, h)
            doh = do_ref[:, XA_HEAD_DIM * h : XA_HEAD_DIM * (h + 1)]
            dp = _dot_nt(doh, vh)
            ds = p * (dp - jnp.sum(dp * p, axis=-1, keepdims=True)) * (XA_HEAD_DIM**-0.5)
            dqs.append(_dot(ds, kh))
            dks.append(_dot_tn(ds, qh))
            dvs.append(_dot_tn(p, doh))
        dq_ref[...] = jnp.concatenate(dqs, axis=1).astype(dq_ref.dtype)
        dkv = jnp.concatenate(dks + dvs, axis=1)

        @pl.when(pl.program_id(0) == 0)
        def _():
            dkv_ref[...] = dkv

        @pl.when(pl.program_id(0) > 0)
        def _():
            dkv_ref[...] += dkv

    tile = pl.BlockSpec((ts, D_MODEL), lambda i: (i, 0))
    kv_spec = pl.BlockSpec((N_MEM, 2 * D_MODEL), lambda i: (0, 0))
    return pl.pallas_call(
        body,
        name=name,
        grid=(s // ts,),
        in_specs=[tile, kv_spec, tile],
        out_specs=[tile, kv_spec],
        out_shape=[jax.ShapeDtypeStruct((s, D_MODEL), _CD), jax.ShapeDtypeStruct((N_MEM, 2 * D_MODEL), F32)],
        compiler_params=_cp(("arbitrary",)),
    )(qx, kv, dox)


def _adamw_math(w, g, m, v):
    m = ADAM_B1 * m + (1.0 - ADAM_B1) * g
    v = ADAM_B2 * v + (1.0 - ADAM_B2) * (g * g)
    m_hat = m / (1.0 - ADAM_B1**ADAM_STEP)
    v_hat = v / (1.0 - ADAM_B2**ADAM_STEP)
    delta = -ADAM_LR * (m_hat / (jnp.sqrt(v_hat) + ADAM_EPS) + ADAM_WD * w)
    return delta, m, v


def _adamw(w, m, v, shard_rows, off, *, transposed, name):
    r, c = w.shape
    by_columns = r % 256 != 0
    tr = 256
    if by_columns:
        assert not transposed and off == 0
        g_spec = tile = pl.BlockSpec((r, tr), lambda i: (0, i))
    else:
        g_spec = pl.BlockSpec((c, tr), lambda i: (off // c, i)) if transposed else pl.BlockSpec((tr, c), lambda i: (off // tr + i, 0))
        tile = pl.BlockSpec((tr, c), lambda i: (i, 0))

    def body(w_ref, g_ref, m_ref, v_ref, go_ref, d_ref, nm_ref, nv_ref):
        g = g_ref[...].T if transposed else g_ref[...]
        go_ref[...] = g
        d_ref[...], nm_ref[...], nv_ref[...] = _adamw_math(w_ref[...], g, m_ref[...], v_ref[...])

    return pl.pallas_call(
        body,
        name=name,
        grid=((c if by_columns else r) // tr,),
        in_specs=[tile, g_spec, tile, tile],
        out_specs=[tile] * 4,
        out_shape=[jax.ShapeDtypeStruct((r, c), F32)] * 4,
        compiler_params=_cp(("parallel",)),
    )(w, shard_rows, m, v)


def _adamw_small(groups, *, name):
    n = len(groups)

    def body(*refs):
        ins, outs = refs[: 4 * n], refs[4 * n :]
        for i in range(n):
            w_ref, g_ref, m_ref, v_ref = ins[4 * i : 4 * i + 4]
            outs[3 * i][...], outs[3 * i + 1][...], outs[3 * i + 2][...] = _adamw_math(w_ref[...], g_ref[...], m_ref[...], v_ref[...])

    flat = [a for grp in groups for a in grp]
    vm = pl.BlockSpec(memory_space=pltpu.VMEM)
    res = pl.pallas_call(
        body,
        name=name,
        in_specs=[vm] * (4 * n),
        out_specs=[vm] * (3 * n),
        out_shape=[jax.ShapeDtypeStruct(grp[0].shape, F32) for grp in groups for _ in range(3)],
        compiler_params=_cp(),
    )(*flat)
    return [tuple(res[3 * i : 3 * i + 3]) for i in range(n)]


def _place():
    return lax.axis_index("x"), lax.axis_index("y"), lax.axis_index("c")


def _rel_chip(x, y, k):
    return (1 - x if k & 2 else x), (1 - y if k & 1 else y)


def _half(c, rh):
    return pl.ds(pl.multiple_of(c * rh, 16), rh)


HBM = pl.BlockSpec(memory_space=pltpu.HBM)
SEM = pl.BlockSpec(memory_space=pltpu.SEMAPHORE)
EFFECT = pltpu.SideEffectType.DATAFLOW_SIDE_EFFECTING


def _in_hbm(a):
    return pltpu.with_memory_space_constraint(a, pltpu.HBM)


def _gather_copies(p_ref, land_ref, send_sems, recv_sems):
    rh = p_ref.shape[0] // 2
    x, y, c = _place()
    rows = _half(c, rh)
    copies = []
    for k in range(1, N_CHIPS):
        cx, cy = _rel_chip(x, y, k)
        copies.append(pltpu.make_async_remote_copy(
            src_ref=p_ref.at[rows], dst_ref=land_ref.at[2 * x + y, rows], send_sem=send_sems.at[k - 1], recv_sem=recv_sems.at[k - 1],
            device_id=(cx, cy, c), device_id_type=MESH))
    copies.append(pltpu.make_async_remote_copy(
        src_ref=p_ref, dst_ref=land_ref.at[2 * x + y], send_sem=send_sems.at[N_CHIPS - 1], recv_sem=recv_sems.at[N_CHIPS - 1],
        device_id=(x, y, 1 - c), device_id_type=MESH))
    return copies


def _gather_start(pack, after, *, name):
    r, w = pack.shape

    def body(p_ref, land_ref, after_ref, send_sems, recv_sems, p_thru, land_thru, token):
        for cp in _gather_copies(p_ref, land_ref, send_sems, recv_sems):
            cp.start()
        token[...] = jnp.zeros_like(token)

    return pl.pallas_call(
        body,
        name=name,
        out_shape=(pltpu.SemaphoreType.DMA((N_CHIPS,)), pltpu.SemaphoreType.DMA((N_CHIPS,)), pltpu.HBM((r, w), pack.dtype),
                   pltpu.HBM((N_CHIPS, r, w), pack.dtype), jax.ShapeDtypeStruct((8, 128), F32)),
        in_specs=(HBM, HBM, ANY),
        out_specs=(SEM, SEM, HBM, HBM, pl.BlockSpec(memory_space=pltpu.VMEM)),
        input_output_aliases={0: 2, 1: 3},
        compiler_params=pltpu.CompilerParams(has_side_effects=EFFECT),
    )(_in_hbm(pack), _in_hbm(lax.empty((N_CHIPS, r, w), pack.dtype)), after)


def _gather_wait(send_sems, recv_sems, pack, land, after, *, name):
    def body(p_ref, land_ref, send_sems, recv_sems, after_ref, p_out, land_out):
        for cp in _gather_copies(p_ref, land_ref, send_sems, recv_sems):
            cp.wait_send()
            cp.wait_recv()

    return pl.pallas_call(
        body,
        name=name,
        out_shape=(pltpu.HBM(pack.shape, pack.dtype), pltpu.HBM(land.shape, land.dtype)),
        in_specs=(HBM, HBM, SEM, SEM, ANY),
        out_specs=(HBM, HBM),
        input_output_aliases={0: 0, 1: 1},
        compiler_params=pltpu.CompilerParams(has_side_effects=EFFECT),
    )(pack, land, send_sems, recv_sems, after)


def _gather_spread(land, *, name):
    n, r, w = land.shape
    rh = r // 2

    def body(land_ref, o_ref, send_sems, recv_sems):
        x, y, c = _place()
        rows = _half(c, rh)
        copies = []
        for k in range(1, N_CHIPS):
            cx, cy = _rel_chip(x, y, k)
            copies.append(pltpu.make_async_remote_copy(
                src_ref=land_ref.at[2 * cx + cy, rows], dst_ref=o_ref.at[2 * cx + cy, rows], send_sem=send_sems.at[k - 1],
                recv_sem=recv_sems.at[k - 1], device_id=(x, y, 1 - c), device_id_type=MESH))
        for cp in copies:
            cp.start()
        for cp in copies:
            cp.wait()

    return pl.pallas_call(
        body,
        name=name,
        in_specs=[ANY],
        out_specs=ANY,
        out_shape=jax.ShapeDtypeStruct(land.shape, land.dtype),
        input_output_aliases={0: 0},
        scratch_shapes=[pltpu.SemaphoreType.DMA((N_CHIPS - 1,)), pltpu.SemaphoreType.DMA((N_CHIPS - 1,))],
        compiler_params=pltpu.CompilerParams(has_side_effects=True),
    )(land)


N_PARTS = 2 * (N_CHIPS - 1)


def _scatter_copies(lo_ref, g_ref, land_lo_ref, land_f_ref, send_sems, recv_sems, starting):
    rh = g_ref.shape[1] // 2
    x, y, c = _place()
    copies = []
    for k in range(1, N_CHIPS):
        cx, cy = _rel_chip(x, y, k)
        for i in range(2):
            part = 2 * (k - 1) + (c if starting else i)
            copies.append(pltpu.make_async_remote_copy(
                src_ref=lo_ref.at[2 * cx + cy, pl.ds(i * rh, rh)], dst_ref=land_lo_ref.at[part],
                send_sem=send_sems.at[2 * (k - 1) + i], recv_sem=recv_sems.at[part], device_id=(cx, cy, i), device_id_type=MESH))
    copies.append(pltpu.make_async_remote_copy(
        src_ref=g_ref.at[2 * x + y, _half(1 - c, rh)], dst_ref=land_f_ref, send_sem=send_sems.at[N_PARTS], recv_sem=recv_sems.at[N_PARTS],
        device_id=(x, y, 1 - c), device_id_type=MESH))
    return copies


def _scatter_start(g_lo, g, *, name):
    n, r, w = g.shape
    rh = r // 2

    def body(lo_ref, g_ref, land_lo_ref, land_f_ref, send_sems, recv_sems, lo_thru, g_thru, land_lo_thru, land_f_thru, token):
        for cp in _scatter_copies(lo_ref, g_ref, land_lo_ref, land_f_ref, send_sems, recv_sems, True):
            cp.start()
        token[...] = jnp.zeros_like(token)

    return pl.pallas_call(
        body,
        name=name,
        out_shape=(pltpu.SemaphoreType.DMA((N_PARTS + 1,)), pltpu.SemaphoreType.DMA((N_PARTS + 1,)), pltpu.HBM(g_lo.shape, g_lo.dtype),
                   pltpu.HBM(g.shape, g.dtype), pltpu.HBM((N_PARTS, rh, w), g_lo.dtype), pltpu.HBM((rh, w), g.dtype),
                   jax.ShapeDtypeStruct((8, 128), F32)),
        in_specs=(HBM, HBM, HBM, HBM),
        out_specs=(SEM, SEM, HBM, HBM, HBM, HBM, pl.BlockSpec(memory_space=pltpu.VMEM)),
        input_output_aliases={0: 2, 1: 3, 2: 4, 3: 5},
        compiler_params=pltpu.CompilerParams(has_side_effects=EFFECT),
    )(_in_hbm(g_lo), _in_hbm(g), _in_hbm(lax.empty((N_PARTS, rh, w), g_lo.dtype)), _in_hbm(lax.empty((rh, w), g.dtype)))


def _scatter_wait(send_sems, recv_sems, g_lo, g, land_lo, land_f, after, *, name):
    def body(lo_ref, g_ref, land_lo_ref, land_f_ref, send_sems, recv_sems, after_ref, o0, o1, o2, o3):
        for cp in _scatter_copies(lo_ref, g_ref, land_lo_ref, land_f_ref, send_sems, recv_sems, False):
            cp.wait_send()
            cp.wait_recv()

    arrays = (g_lo, g, land_lo, land_f)
    return pl.pallas_call(
        body,
        name=name,
        out_shape=tuple(pltpu.HBM(a.shape, a.dtype) for a in arrays),
        in_specs=(HBM, HBM, HBM, HBM, SEM, SEM, ANY),
        out_specs=(HBM, HBM, HBM, HBM),
        input_output_aliases={0: 0, 1: 1, 2: 2, 3: 3},
        compiler_params=pltpu.CompilerParams(has_side_effects=EFFECT),
    )(*arrays, send_sems, recv_sems, after)


def _scatter_sum(g, land_lo, land_f, where, *, name):
    n, r, w = g.shape
    rh = r // 2
    tr = _pick(rh, (256, 160, 80))
    nt = rh // tr

    def body(where_ref, g_ref, f_ref, lo_ref, o_ref):
        acc = g_ref[0] + f_ref[...]
        for part in range(N_PARTS):
            acc = acc + lo_ref[part].astype(F32)
        o_ref[...] = acc

    return pl.pallas_call(
        body,
        name=name,
        grid_spec=pltpu.PrefetchScalarGridSpec(
            num_scalar_prefetch=1,
            grid=(nt,),
            in_specs=[pl.BlockSpec((1, tr, w), lambda i, wh: (wh[1], wh[0] * nt + i, 0)),
                      pl.BlockSpec((tr, w), lambda i, wh: (i, 0)),
                      pl.BlockSpec((N_PARTS, tr, w), lambda i, wh: (0, i, 0))],
            out_specs=pl.BlockSpec((tr, w), lambda i, wh: (wh[0] * nt + i, 0)),
        ),
        out_shape=jax.ShapeDtypeStruct((r, w), F32),
        compiler_params=_cp(("parallel",)),
    )(where, g, land_f, land_lo)


def _swap_all(shards, *, name):
    n = len(shards)

    def body(*refs):
        ins, outs = refs[:n], refs[n : 2 * n]
        send_sems, recv_sems = refs[2 * n :]
        x, y, c = _place()
        copies = []
        for i, (e_ref, o_ref) in enumerate(zip(ins, outs)):
            rows = _half(c, e_ref.shape[0] // 2)
            copies.append(pltpu.make_async_remote_copy(src_ref=e_ref.at[rows], dst_ref=o_ref.at[rows], send_sem=send_sems.at[i],
                                                       recv_sem=recv_sems.at[i], device_id=(x, y, 1 - c), device_id_type=MESH))
        for cp in copies:
            cp.start()
        for cp in copies:
            cp.wait()

    return pl.pallas_call(
        body,
        name=name,
        in_specs=[ANY] * n,
        out_specs=[ANY] * n,
        out_shape=[jax.ShapeDtypeStruct(e.shape, e.dtype) for e in shards],
        input_output_aliases={i: i for i in range(n)},
        scratch_shapes=[pltpu.SemaphoreType.DMA((n,)), pltpu.SemaphoreType.DMA((n,))],
        compiler_params=pltpu.CompilerParams(has_side_effects=True),
    )(*shards)


def _sum_small(small, after):
    n_dev = 8

    def body(s_ref, after_ref, o_ref, all_ref, send_sems, recv_sems):
        x, y, c = _place()
        me = 4 * x + 2 * y + c
        all_ref[me] = s_ref[...]
        copies = []
        for k in range(1, n_dev):
            cx, cy = _rel_chip(x, y, k >> 1)
            cc = 1 - c if k & 1 else c
            copies.append(pltpu.make_async_remote_copy(
                src_ref=s_ref, dst_ref=all_ref.at[me], send_sem=send_sems.at[k - 1], recv_sem=recv_sems.at[k - 1],
                device_id=(cx, cy, cc), device_id_type=MESH))
        for cp in copies:
            cp.start()
        for cp in copies:
            cp.wait()
        acc = all_ref[0]
        for a in range(1, n_dev):
            acc = acc + all_ref[a]
        o_ref[...] = acc

    vm = pl.BlockSpec(memory_space=pltpu.VMEM)
    return pl.pallas_call(
        body,
        name="sum_small",
        in_specs=[vm, ANY],
        out_specs=vm,
        out_shape=jax.ShapeDtypeStruct(small.shape, F32),
        scratch_shapes=[pltpu.VMEM((n_dev,) + small.shape, F32), pltpu.SemaphoreType.DMA((n_dev - 1,)), pltpu.SemaphoreType.DMA((n_dev - 1,))],
        compiler_params=pltpu.CompilerParams(has_side_effects=True),
    )(small, after)


MATS = {"w_in": (776, True), "w_out": (256, False), "w_xq": (256, False), "w_xkv": (512, True), "w_xo": (256, False),
        "w_up": (1024, True), "w_down": (1024, False)}
GATHER_FIRST = ("w_in",)
GATHER_REST = ("w_out", "w_xq", "w_xkv", "w_xo", "w_up", "w_down")
GRAD_GROUPS = (("w_up", "w_down"), ("w_out", "w_xq", "w_xkv", "w_xo"), ("w_in",))


def _group_rows(names):
    n = sum(MATS[name][0] for name in names)
    return n + (-n) % 32


def _pack(pieces, rows):
    p = jnp.concatenate(pieces, axis=0) if len(pieces) > 1 else pieces[0]
    return jnp.pad(p, ((0, rows - p.shape[0]), (0, 0))) if rows > p.shape[0] else p


SMALL = (
    ("mix_norm", 1024), ("conv_norm", 512), ("b_af", 256), ("b_ab", 256), ("gla_norm", 128), ("xa_norm", 1024), ("mem_norm", 1024),
    ("mlp_norm", 1024), ("final_norm", 1024), ("conv_w", 1536), ("w_af", 4096), ("w_ab", 4096), ("loss", 128),
)


def kernel(x, mem, mix_norm, w_in, conv_w, conv_norm, w_af, b_af, w_ab, b_ab, gla_norm, w_out, xa_norm, mem_norm, w_xq, w_xkv, w_xo, mlp_norm, w_up, w_down, final_norm, loss_target, m_mix_norm, m_w_in, m_conv_w, m_conv_norm, m_w_af, m_b_af, m_w_ab, m_b_ab, m_gla_norm, m_w_out, m_xa_norm, m_mem_norm, m_w_xq, m_w_xkv, m_w_xo, m_mlp_norm, m_w_up, m_w_down, m_final_norm, v_mix_norm, v_w_in, v_conv_w, v_conv_norm, v_w_af, v_b_af, v_w_ab, v_b_ab, v_gla_norm, v_w_out, v_xa_norm, v_mem_norm, v_w_xq, v_w_xkv, v_w_xo, v_mlp_norm, v_w_up, v_w_down, v_final_norm):
    given = dict(locals())
    xi, yi, ci = _place()
    chip = 2 * xi + yi
    where = jnp.stack([ci, chip]).astype(jnp.int32)

    lo = {name: (given[name][0].T if MATS[name][1] else given[name][0]).astype(_CD) for name in MATS}
    pack_rest = _pack([lo[name] for name in GATHER_REST], _group_rows(GATHER_REST))
    pack_first = _pack([lo[name] for name in GATHER_FIRST], _group_rows(GATHER_FIRST))
    xs, mems, tgt = x[0], mem[0], loss_target[0]
    behind = lambda gain, token: gain + token[0, 0]

    def placed(shard, full_shape, col):
        return lax.dynamic_update_slice(jnp.zeros(full_shape, F32), shard, (0, col)).reshape(-1, 128)

    sw = jnp.concatenate([
        placed(conv_w[0], (CONV_K, CONV_WIDTH), 128 * chip),
        placed(w_af[0], (GLA_LOWRANK, GLA_K_TOTAL), 64 * chip),
        placed(w_ab[0], (GLA_LOWRANK, GLA_K_TOTAL), 64 * chip),
    ], axis=0)
    sw = jnp.pad(sw, ((0, SMALL_ROWS - sw.shape[0]), (0, 0))) * (ci == 0).astype(F32)
    sw = _sum_small(sw, mix_norm)

    first_send, first_recv, pack_first, land_first, first_token = _gather_start(pack_first, sw, name="gather_first_start")
    rest_send, rest_recv, pack_rest, land_rest, rest_token = _gather_start(pack_rest, first_token, name="gather_rest_start")
    h1 = _rms_fwd(xs, behind(mix_norm, rest_token), name="norm_mix")
    pack_first, land_first = _gather_wait(first_send, first_recv, pack_first, land_first, h1, name="gather_first_wait")
    got_first = _gather_spread(land_first, name="gather_first_spread")

    def whole(got, off, rows):
        return got[:, off : off + rows].reshape(N_CHIPS * rows, D_MODEL)

    w_in_t = whole(got_first, 0, MATS["w_in"][0])
    w_za = jnp.concatenate([w_in_t[0:1536], w_in_t[2560:3072]], axis=0)
    w_zb = jnp.concatenate([w_in_t[1536:2560], w_in_t[3072:W_IN_COLS], jnp.zeros((ZB_COLS - 1056, D_MODEL), _CD)], axis=0)
    conv_w_full = sw[0:12].reshape(CONV_K, CONV_WIDTH)
    w_af_full = sw[12:44].reshape(GLA_LOWRANK, GLA_K_TOTAL)
    w_ab_full = sw[44:76].reshape(GLA_LOWRANK, GLA_K_TOTAL)
    waf_p = jnp.pad(w_af_full, ((0, 128 - GLA_LOWRANK), (0, 0))).astype(_CD)
    wab_p = jnp.pad(w_ab_full, ((GLA_LOWRANK, 128 - 2 * GLA_LOWRANK), (0, 0))).astype(_CD)

    z_b = _mm(h1, w_zb, mode="nt", name="proj_in_b", tn=ZB_COLS)
    z_a = _mm(h1, w_za, mode="nt", name="proj_in_a", tm=512, tn=ZA_COLS)
    b_f, b_b = _gate_fwd(z_b, waf_p, wab_p, b_af, b_ab, name="gates")
    o_f, st_f, o_b, st_b = _gla_fwd(z_b, b_f, b_b, name="gla_scan")
    y = _mix_fwd(z_a, o_f, o_b, conv_w_full, conv_norm, gla_norm, name="mix_out")
    pack_rest, land_rest = _gather_wait(rest_send, rest_recv, pack_rest, land_rest, y, name="gather_rest_wait")
    gathered = _gather_spread(land_rest, name="gather_rest_spread")
    wt, off = {}, 0
    for name in GATHER_REST:
        wt[name] = whole(gathered, off, MATS[name][0])
        off += MATS[name][0]
    x1, hx = _mm_rows(y, wt["w_out"], mode="nn", name="proj_out", rows=(xs,), vecs=(xa_norm,), out_rows=(F32, _CD), epilogue=_ep_residual_norm)
    qx = _mm(hx, wt["w_xq"], mode="nn", name="proj_xq", out_dtypes=(_CD,))
    hmem = _rms_fwd(mems, mem_norm, name="norm_mem")
    kv = _mm(hmem, wt["w_xkv"], mode="nt", name="proj_xkv", out_dtypes=(_CD,))
    ox = _xattn_fwd(qx, kv, name="xattn")
    x2, hm = _mm_rows(ox, wt["w_xo"], mode="nn", name="proj_xo", rows=(x1,), vecs=(mlp_norm,), out_rows=(F32, _CD), epilogue=_ep_residual_norm)
    act, relu_u = _mm(hm, wt["w_up"], mode="nt", name="mlp_up", out_dtypes=(_CD, _CD), tm=2048,
                      epilogue=lambda acc: (jnp.square(jnp.maximum(acc, 0.0)), jnp.maximum(acc, 0.0)))
    dx3, dx3_lo, loss_part, g_final_norm = _mm_rows(
        act, wt["w_down"], mode="nn", name="mlp_down", rows=(x2, tgt), vecs=(final_norm.reshape(1, D_MODEL),),
        out_rows=(F32, _CD), out_vecs=(128, D_MODEL), epilogue=_ep_loss)

    grads_t = {}

    def start_group(names, tag):
        rows = _group_rows(names)
        g = jnp.stack([_pack([grads_t[name][a * MATS[name][0] : (a + 1) * MATS[name][0]] for name in names], rows) for a in range(N_CHIPS)])
        return _scatter_start(g.astype(_TD), g, name="grads_" + tag + "_start")

    def finish_group(state, after, tag):
        send_sems, recv_sems, g_lo, g, land_lo, land_f, _ = state
        g_lo, g, land_lo, land_f = _scatter_wait(send_sems, recv_sems, g_lo, g, land_lo, land_f, after, name="grads_" + tag + "_wait")
        return _scatter_sum(g, land_lo, land_f, where, name="grads_" + tag + "_sum")

    def new_packs(names):
        shape = (N_CHIPS, _group_rows(names), D_MODEL)
        return lax.empty(shape, F32), lax.empty(shape, _TD)

    def grad_into(packs, names, which, a, b, name):
        off = sum(MATS[other][0] for other in names[: names.index(which)])
        return _mm_tn_into(a, b, packs, rows=MATS[which][0], off=off, name=name)

    du = _mm(dx3_lo, wt["w_down"], mode="nt", name="mlp_down_dx", out_dtypes=(_CD,), extras=(relu_u,), tm=2048,
             epilogue=lambda acc, rr: (acc * (2.0 * rr.astype(F32)),))
    packs = new_packs(GRAD_GROUPS[0])
    packs = grad_into(packs, GRAD_GROUPS[0], "w_down", act, dx3_lo, "mlp_down_dw")
    packs = grad_into(packs, GRAD_GROUPS[0], "w_up", du, hm, "mlp_up_dw")
    mlp_state = _scatter_start(packs[1], packs[0], name="grads_mlp_start")
    dx2, dx2_lo, g_mlp_norm = _mm_rows(
        du, wt["w_up"], mode="nn", name="mlp_up_dx", rows=(x2, dx3), vecs=(behind(mlp_norm, mlp_state[-1]),),
        out_rows=(F32, _CD), out_vecs=(D_MODEL,), epilogue=_ep_norm_bwd)
    dox = _mm(dx2_lo, wt["w_xo"], mode="nt", name="proj_xo_dx", out_dtypes=(_CD,))
    packs = new_packs(GRAD_GROUPS[1])
    packs = grad_into(packs, GRAD_GROUPS[1], "w_xo", ox, dx2_lo, "proj_xo_dw")
    dqx, dkv = _xattn_bwd(qx, kv, dox, name="xattn_bwd")
    packs = grad_into(packs, GRAD_GROUPS[1], "w_xq", hx, dqx, "proj_xq_dw")
    dx1, dx1_lo, g_xa_norm = _mm_rows(
        dqx, wt["w_xq"], mode="nt", name="proj_xq_dx", rows=(x1, dx2), vecs=(xa_norm,),
        out_rows=(F32, _CD), out_vecs=(D_MODEL,), epilogue=_ep_norm_bwd)
    dkv_lo = dkv.astype(_CD)
    packs = grad_into(packs, GRAD_GROUPS[1], "w_xkv", dkv_lo, hmem, "proj_xkv_dw")
    dhmem = _mm(dkv_lo, wt["w_xkv"], mode="nn", name="proj_xkv_dx")
    g_mem_norm = _rms_gain_grad(mems, dhmem, name="norm_mem_bwd")
    dy = _mm(dx1_lo, wt["w_out"], mode="nt", name="proj_out_dx")
    packs = grad_into(packs, GRAD_GROUPS[1], "w_out", y, dx1_lo, "proj_out_dw")
    attn_state = _scatter_start(packs[1], packs[0], name="grads_attn_start")
    dz_a, do, g_conv_w, g_conv_norm, g_gla_norm = _mix_bwd(z_a, o_f, o_b, dy, conv_w_full, behind(conv_norm, attn_state[-1]), gla_norm, name="mix_out_bwd")
    dqkv_f, db_f, dqkv_b, db_b = _gla_bwd(z_b, b_f, b_b, do, st_f, st_b, name="gla_scan_bwd")
    dz_b, g_waf_p, g_wab_p, g_b_af, g_b_ab = _gate_bwd(z_b, waf_p, wab_p, b_af, b_ab, db_f, db_b, dqkv_f, dqkv_b, name="gates_bwd")
    g_za = _mm_tn(dz_a, h1, name="proj_in_a_dw")
    g_zb = _mm_tn(dz_b, h1, name="proj_in_b_dw")
    grads_t["w_in"] = jnp.concatenate([g_za[0:1536], g_zb[0:1024], g_za[1536:2048], g_zb[1024:1056]], axis=0)
    in_state = start_group(GRAD_GROUPS[2], "in")
    dh1_a = _mm(dz_a, w_za, mode="nn", name="proj_in_a_dx", tm=512, tk=ZA_COLS)
    grad_x, g_mix_norm = _mm_rows(
        dz_b, w_zb, mode="nn", name="proj_in_b_dx", rows=(xs, dx1, dh1_a), vecs=(behind(mix_norm, in_state[-1]),),
        out_rows=(F32,), out_vecs=(D_MODEL,), epilogue=_ep_norm_bwd)

    half_mlp = finish_group(mlp_state, grad_x, "mlp")
    half_attn = finish_group(attn_state, half_mlp, "attn")
    half_in = finish_group(in_state, half_attn, "in")
    shard_rows = {}
    for names, rows in zip(GRAD_GROUPS, _swap_all([half_mlp, half_attn, half_in], name="shards_to_sibling")):
        off = 0
        for name in names:
            shard_rows[name] = (rows, off)
            off += MATS[name][0]

    small_vals = dict(mix_norm=g_mix_norm, conv_norm=g_conv_norm, b_af=g_b_af, b_ab=g_b_ab, gla_norm=g_gla_norm, xa_norm=g_xa_norm,
                      mem_norm=g_mem_norm, mlp_norm=g_mlp_norm, final_norm=g_final_norm, conv_w=g_conv_w,
                      w_af=g_waf_p[0:GLA_LOWRANK], w_ab=g_wab_p[GLA_LOWRANK : 2 * GLA_LOWRANK], loss=loss_part)
    small = jnp.concatenate([small_vals[name].reshape(-1, 128) for name, _ in SMALL], axis=0)
    small = _sum_small(jnp.pad(small, ((0, SMALL_ROWS - small.shape[0]), (0, 0))), loss_part)
    g_small, off = {}, 0
    for name, n in SMALL:
        g_small[name] = small[off : off + n // 128]
        off += n // 128
    loss = g_small["loss"][0, 0]
    g_small["conv_w"] = lax.dynamic_slice(g_small["conv_w"].reshape(CONV_K, CONV_WIDTH), (0, 128 * chip), (CONV_K, 128))
    g_small["w_af"] = lax.dynamic_slice(g_small["w_af"].reshape(GLA_LOWRANK, GLA_K_TOTAL), (0, 64 * chip), (GLA_LOWRANK, 64))
    g_small["w_ab"] = lax.dynamic_slice(g_small["w_ab"].reshape(GLA_LOWRANK, GLA_K_TOTAL), (0, 64 * chip), (GLA_LOWRANK, 64))

    names = ["mix_norm", "w_in", "conv_w", "conv_norm", "w_af", "b_af", "w_ab", "b_ab", "gla_norm", "w_out", "xa_norm", "mem_norm",
             "w_xq", "w_xkv", "w_xo", "mlp_norm", "w_up", "w_down", "final_norm"]
    big_names = list(MATS)
    as2d = lambda a: a.reshape(1, -1) if a.ndim == 1 else a.reshape(a.shape[-2:])
    grads, deltas, new_m, new_v = {}, {}, {}, {}
    for name in big_names:
        rows, off = shard_rows[name]
        wmv = [as2d(given[name]), as2d(given["m_" + name]), as2d(given["v_" + name])]
        as_stored = name == "w_in"
        if as_stored:
            wmv = [a.T for a in wmv]
        res = _adamw(*wmv, rows, off, transposed=MATS[name][1] and not as_stored, name="adamw_" + name)
        grads[name], deltas[name], new_m[name], new_v[name] = [a.T for a in res] if as_stored else res
    small_names = [name for name in names if name not in big_names]
    groups = []
    for name in small_names:
        grads[name] = g_small[name].reshape(as2d(given[name]).shape)
        groups.append((as2d(given[name]), grads[name], as2d(given["m_" + name]), as2d(given["v_" + name])))
    for name, res in zip(small_names, _adamw_small(groups, name="adamw_small")):
        deltas[name], new_m[name], new_v[name] = res

    like = lambda name, a: a.reshape(given[name].shape)
    return (loss, grad_x[None], *[like(n, grads[n]) for n in names], *[like(n, deltas[n]) for n in names],
            *[like(n, new_m[n]) for n in names], *[like(n, new_v[n]) for n in names])
```

```python
import jax
import jax.numpy as jnp
from jax import lax
from jax.experimental import pallas as pl
from jax.experimental.pallas import tpu as pltpu

F32 = jnp.float32
BF16 = jnp.bfloat16
_CD = jnp.bfloat16
_TD = jnp.bfloat16

D_MODEL = 1024
N_MEM = 256
CONV_WIDTH = 512
CONV_GROUP = 64
CONV_K = 3
GLA_HEADS = 4
GLA_DK = 64
GLA_DV = 128
GLA_K_TOTAL = 256
GLA_V_TOTAL = 512
GLA_LOWRANK = 16
GLA_GATE_SCALE = 1.0 / 16.0
GLA_CHUNK = 64
XA_HEADS = 4
XA_HEAD_DIM = 256
D_FF = 4096
EPS = 1e-6
W_IN_COLS = 3104
ZA_COLS = 2048
ZB_COLS = 1152
LR_COL = 1024

ADAM_LR = 0.001
ADAM_B1 = 0.9
ADAM_B2 = 0.999
ADAM_EPS = 1e-08
ADAM_WD = 0.01
ADAM_STEP = 10

N_CHIPS = 4
SMALL_ROWS = 128

_TS = 512
_VMEM = 44 * 1024 * 1024
MESH = pl.DeviceIdType.MESH
ANY = pl.BlockSpec(memory_space=pl.ANY)


def _cp(sem=None, **kw):
    return pltpu.CompilerParams(dimension_semantics=sem, vmem_limit_bytes=_VMEM, **kw)


def _dot(a, b):
    return jnp.dot(a.astype(_CD), b.astype(_CD), preferred_element_type=F32)


def _dot_nt(a, b):
    return lax.dot_general(a.astype(_CD), b.astype(_CD), (((1,), (1,)), ((), ())), preferred_element_type=F32)


def _dot_tn(a, b):
    return lax.dot_general(a.astype(_CD), b.astype(_CD), (((0,), (0,)), ((), ())), preferred_element_type=F32)


def _dot_split(x, ones):
    hi = x.astype(BF16)
    r = x - hi.astype(F32)
    mid = r.astype(BF16)
    lo = (r - mid.astype(F32)).astype(BF16)
    d = lambda p: jnp.dot(p, ones, preferred_element_type=F32)
    return d(hi) + d(mid) + d(lo)


def _pick(n, cands=(1024, 640, 512, 256, 128)):
    for t in cands:
        if n % t == 0:
            return t
    return n


def _rows(s, light=False):
    return min(2 * _TS if light else _TS, s)


def _sigmoid(v):
    e = jnp.exp(-jnp.abs(v))
    return jnp.where(v >= 0, 1.0 / (1.0 + e), e / (1.0 + e))


def _mm(a, b, *, mode, name, out_dtypes=(F32,), extras=(), epilogue=None, tm=None, tn=None, tk=None):
    m, k = a.shape
    n = b.shape[1] if mode == "nn" else b.shape[0]
    tm = min(m, tm or 1024)
    tn = tn or _pick(n)
    tk = tk or _pick(k)
    nk = k // tk
    n_ex, n_out = len(extras), len(out_dtypes)

    def body(*refs):
        a_ref, b_ref = refs[:2]
        ex = refs[2 : 2 + n_ex]
        outs = refs[2 + n_ex : 2 + n_ex + n_out]
        part = _dot(a_ref[...], b_ref[...]) if mode == "nn" else _dot_nt(a_ref[...], b_ref[...])

        def finish(acc):
            res = epilogue(acc, *[e[...] for e in ex]) if epilogue else (acc,)
            for o, r in zip(outs, res):
                o[...] = r.astype(o.dtype)

        if nk == 1:
            finish(part)
        else:
            acc_ref = refs[-1]
            kk = pl.program_id(2)

            @pl.when(kk == 0)
            def _():
                acc_ref[...] = part

            @pl.when(kk > 0)
            def _():
                acc_ref[...] += part

            @pl.when(kk == nk - 1)
            def _():
                finish(acc_ref[...])

    b_spec = pl.BlockSpec((tk, tn), lambda i, j, kk: (kk, j)) if mode == "nn" else pl.BlockSpec((tn, tk), lambda i, j, kk: (j, kk))
    tile = pl.BlockSpec((tm, tn), lambda i, j, kk: (i, j))
    out = pl.pallas_call(
        body,
        name=name,
        grid=(m // tm, n // tn, nk),
        in_specs=[pl.BlockSpec((tm, tk), lambda i, j, kk: (i, kk)), b_spec] + [tile] * n_ex,
        out_specs=[tile] * n_out,
        out_shape=[jax.ShapeDtypeStruct((m, n), dt) for dt in out_dtypes],
        scratch_shapes=[pltpu.VMEM((tm, tn), F32)] if nk > 1 else [],
        compiler_params=_cp(("parallel", "parallel", "arbitrary")),
    )(a, b, *extras)
    return out[0] if n_out == 1 else out


def _mm_tn(a, b, *, name):
    s, m = a.shape
    n = b.shape[1]
    cap = max(128, (1 << 20) // n)
    tm = _pick(m, tuple(t for t in (512, 640, 384, 256, 128) if t <= max(cap, 128)))
    ts = min(s, 1 << (((1 << 22) // n).bit_length() - 1))
    ns = s // ts

    def body(a_ref, b_ref, o_ref):
        part = _dot_tn(a_ref[...], b_ref[...])
        if ns == 1:
            o_ref[...] = part
        else:
            ss = pl.program_id(1)

            @pl.when(ss == 0)
            def _():
                o_ref[...] = part

            @pl.when(ss > 0)
            def _():
                o_ref[...] += part

    return pl.pallas_call(
        body,
        name=name,
        grid=(m // tm, ns),
        in_specs=[pl.BlockSpec((ts, tm), lambda i, ss: (ss, i)), pl.BlockSpec((ts, n), lambda i, ss: (ss, 0))],
        out_specs=pl.BlockSpec((tm, n), lambda i, ss: (i, 0)),
        out_shape=jax.ShapeDtypeStruct((m, n), F32),
        compiler_params=_cp(("parallel", "arbitrary")),
    )(a, b)


def _mm_tn_into(a, b, packs, *, rows, off, name):
    s, m = a.shape
    n = b.shape[1]
    tm = 1024 if rows % 1024 == 0 and s >= 4096 else 512
    tr = min(tm, rows)
    per, chips = rows // tr, tm // tr
    ts = min(s, (1 << (((1 << 22) // n).bit_length() - 1)) * 512 // tm)
    ns = s // ts

    def body(a_ref, b_ref, f_in, lo_in, f_ref, lo_ref):
        part = _dot_tn(a_ref[...], b_ref[...])
        pieces = [part[c * tr : (c + 1) * tr] for c in range(chips)]
        if ns == 1:
            for c, p in enumerate(pieces):
                f_ref[c] = p
                lo_ref[c] = p.astype(lo_ref.dtype)
        else:
            ss = pl.program_id(1)

            @pl.when(ss == 0)
            def _():
                for c, p in enumerate(pieces):
                    f_ref[c] = p

            @pl.when(ss > 0)
            def _():
                for c, p in enumerate(pieces):
                    f_ref[c] += p

            @pl.when(ss == ns - 1)
            def _():
                lo_ref[...] = f_ref[...].astype(lo_ref.dtype)

    spec = pl.BlockSpec((chips, tr, n), lambda i, ss: (i // per, off // tr + i % per, 0))
    return pl.pallas_call(
        body,
        name=name,
        grid=(m // tm, ns),
        in_specs=[pl.BlockSpec((ts, tm), lambda i, ss: (ss, i)), pl.BlockSpec((ts, n), lambda i, ss: (ss, 0)), ANY, ANY],
        out_specs=[spec, spec],
        out_shape=[jax.ShapeDtypeStruct(p.shape, p.dtype) for p in packs],
        input_output_aliases={2: 0, 3: 1},
        compiler_params=_cp(("parallel", "arbitrary")),
    )(a, b, *packs)


def _mm_rows(a, b, *, mode, name, rows=(), vecs=(), out_rows=(), out_vecs=(), epilogue, tm=512):
    m, k = a.shape
    n = b.shape[1] if mode == "nn" else b.shape[0]
    tm = min(m, tm)
    parts = 2 if tm % 256 == 0 else 1
    n_r, n_v, n_or, n_ov = len(rows), len(vecs), len(out_rows), len(out_vecs)

    def body(*refs):
        a_ref, b_ref = refs[:2]
        r_refs = refs[2 : 2 + n_r]
        v_refs = refs[2 + n_r : 2 + n_r + n_v]
        or_refs = refs[2 + n_r + n_v : 2 + n_r + n_v + n_or]
        ov_refs = refs[2 + n_r + n_v + n_or :]
        res_vecs = None
        for p in range(parts):
            rs = slice(p * tm // parts, (p + 1) * tm // parts)
            acc = _dot(a_ref[rs, :], b_ref[...]) if mode == "nn" else _dot_nt(a_ref[rs, :], b_ref[...])
            res_rows, part_vecs = epilogue(acc, [r[rs, :] for r in r_refs], [v[...] for v in v_refs])
            for o, r in zip(or_refs, res_rows):
                o[rs, :] = r.astype(o.dtype)
            res_vecs = part_vecs if res_vecs is None else [s + t for s, t in zip(res_vecs, part_vecs)]
        if n_ov:
            first = pl.program_id(0) == 0

            @pl.when(first)
            def _():
                for o, r in zip(ov_refs, res_vecs):
                    o[...] = r

            @pl.when(jnp.logical_not(first))
            def _():
                for o, r in zip(ov_refs, res_vecs):
                    o[...] += r

    tile = pl.BlockSpec((tm, n), lambda i: (i, 0))
    whole = lambda arr: pl.BlockSpec(arr.shape, lambda i: (0, 0))
    vec = lambda w: pl.BlockSpec((1, w), lambda i: (0, 0))
    out = pl.pallas_call(
        body,
        name=name,
        grid=(m // tm,),
        in_specs=[pl.BlockSpec((tm, k), lambda i: (i, 0)), whole(b)] + [tile] * n_r + [vec(v.shape[1]) for v in vecs],
        out_specs=[tile] * n_or + [vec(w) for w in out_vecs],
        out_shape=[jax.ShapeDtypeStruct((m, n), dt) for dt in out_rows] + [jax.ShapeDtypeStruct((1, w), F32) for w in out_vecs],
        compiler_params=_cp(("arbitrary",) if n_ov else ("parallel",)),
    )(a, b, *rows, *vecs)
    return out


def _ep_residual_norm(acc, rows, vecs):
    x = acc + rows[0]
    r = lax.rsqrt(jnp.mean(x * x, axis=-1, keepdims=True) + EPS)
    return [x, x * r * vecs[0]], []


def _ep_norm_bwd(acc, rows, vecs):
    dy = acc
    for extra in rows[2:]:
        dy = dy + extra
    x, dres = rows[0], rows[1]
    r = lax.rsqrt(jnp.mean(x * x, axis=-1, keepdims=True) + EPS)
    xh = x * r
    dxh = dy * vecs[0]
    dx = r * (dxh - xh * jnp.mean(dxh * xh, axis=-1, keepdims=True)) + dres
    return [dx, dx], [jnp.sum(dy * xh, axis=0, keepdims=True)]


def _ep_loss(acc, rows, vecs):
    x = acc + rows[0]
    d = x.shape[-1]
    r = lax.rsqrt(jnp.mean(x * x, axis=-1, keepdims=True) + EPS)
    xh = x * r
    err = xh * vecs[0] - rows[1]
    loss = jnp.zeros((1, 128), F32) + 0.5 * jnp.sum(jnp.mean(err * err, axis=-1, keepdims=True))
    dy = err * (1.0 / d)
    dxh = dy * vecs[0]
    dx = r * (dxh - xh * jnp.mean(dxh * xh, axis=-1, keepdims=True))
    return [dx, dx], [loss, jnp.sum(dy * xh, axis=0, keepdims=True)]


def _rms_fwd(x, g, *, name):
    s, d = x.shape
    ts = _rows(s, light=True)

    def body(x_ref, g_ref, o_ref):
        xf = x_ref[...]
        r = lax.rsqrt(jnp.mean(xf * xf, axis=-1, keepdims=True) + EPS)
        o_ref[...] = (xf * r * g_ref[...]).astype(o_ref.dtype)

    return pl.pallas_call(
        body,
        name=name,
        grid=(s // ts,),
        in_specs=[pl.BlockSpec((ts, d), lambda i: (i, 0)), pl.BlockSpec((1, d), lambda i: (0, 0))],
        out_specs=pl.BlockSpec((ts, d), lambda i: (i, 0)),
        out_shape=jax.ShapeDtypeStruct((s, d), _CD),
        compiler_params=_cp(("parallel",)),
    )(x, g)


def _rms_gain_grad(x, dy, *, name):
    s, d = x.shape
    ts = _rows(s)

    def body(x_ref, dy_ref, dg_ref):
        xf = x_ref[...]
        r = lax.rsqrt(jnp.mean(xf * xf, axis=-1, keepdims=True) + EPS)
        part = jnp.sum(dy_ref[...] * (xf * r), axis=0, keepdims=True)

        @pl.when(pl.program_id(0) == 0)
        def _():
            dg_ref[...] = part

        @pl.when(pl.program_id(0) > 0)
        def _():
            dg_ref[...] += part

    tile = pl.BlockSpec((ts, d), lambda i: (i, 0))
    return pl.pallas_call(
        body,
        name=name,
        grid=(s // ts,),
        in_specs=[tile, tile],
        out_specs=pl.BlockSpec((1, d), lambda i: (0, 0)),
        out_shape=jax.ShapeDtypeStruct((1, d), F32),
        compiler_params=_cp(("arbitrary",)),
    )(x, dy)


def _chunk_scan(v, row_in_chunk, suffix):
    t = v.shape[0]
    step = 1
    while step < GLA_CHUNK:
        if suffix:
            v = v + jnp.where(row_in_chunk < GLA_CHUNK - step, pltpu.roll(v, t - step, 0), 0.0)
        else:
            v = v + jnp.where(row_in_chunk >= step, pltpu.roll(v, step, 0), 0.0)
        step *= 2
    return v


def _gate_pre(lr, w_ref, b_ref):
    return _dot(lr, w_ref[...]) + b_ref[...]


def _gate_fwd(z, waf, wab, baf, bab, *, name):
    s = z.shape[0]
    ts = _rows(s, light=True)

    def body(lr_ref, waf_ref, wab_ref, baf_ref, bab_ref, bf_ref, bb_ref):
        lr = lr_ref[...]
        ric = lax.broadcasted_iota(jnp.int32, (ts, GLA_K_TOTAL), 0) & (GLA_CHUNK - 1)
        for w_ref, b_ref, o_ref, suffix in ((waf_ref, baf_ref, bf_ref, False), (wab_ref, bab_ref, bb_ref, True)):
            pre = _gate_pre(lr, w_ref, b_ref)
            la = (jnp.minimum(pre, 0.0) - jnp.log(1.0 + jnp.exp(-jnp.abs(pre)))) * GLA_GATE_SCALE
            o_ref[...] = _chunk_scan(la, ric, suffix)

    wspec = pl.BlockSpec((128, GLA_K_TOTAL), lambda i: (0, 0))
    bspec = pl.BlockSpec((1, GLA_K_TOTAL), lambda i: (0, 0))
    tile = pl.BlockSpec((ts, GLA_K_TOTAL), lambda i: (i, 0))
    return pl.pallas_call(
        body,
        name=name,
        grid=(s // ts,),
        in_specs=[pl.BlockSpec((ts, 128), lambda i: (i, LR_COL // 128)), wspec, wspec, bspec, bspec],
        out_specs=[tile, tile],
        out_shape=[jax.ShapeDtypeStruct((s, GLA_K_TOTAL), F32)] * 2,
        compiler_params=_cp(("parallel",)),
    )(z, waf, wab, baf, bab)


def _gate_bwd(z, waf, wab, baf, bab, dbf, dbb, dqkv_f, dqkv_b, *, name):
    s = z.shape[0]
    ts = _rows(s, light=True)

    def body(lr_ref, waf_ref, wab_ref, baf_ref, bab_ref, dbf_ref, dbb_ref, gf_ref, gb_ref, dzb_ref, dwf_ref, dwb_ref, dbaf_ref, dbab_ref):
        lr = lr_ref[...]
        ric = lax.broadcasted_iota(jnp.int32, (ts, GLA_K_TOTAL), 0) & (GLA_CHUNK - 1)
        first = pl.program_id(0) == 0
        dlr = None
        for w_ref, b_ref, db_ref, dw_ref, dbias_ref, suffix in (
            (waf_ref, baf_ref, dbf_ref, dwf_ref, dbaf_ref, True),
            (wab_ref, bab_ref, dbb_ref, dwb_ref, dbab_ref, False),
        ):
            pre = _gate_pre(lr, w_ref, b_ref)
            dla = _chunk_scan(db_ref[...], ric, suffix)
            dpre = dla * GLA_GATE_SCALE * _sigmoid(-pre)
            part = _dot_nt(dpre, w_ref[...])
            dlr = part if dlr is None else dlr + part
            dw = _dot_tn(lr, dpre)
            dbias = jnp.sum(dpre, axis=0, keepdims=True)

            @pl.when(first)
            def _():
                dw_ref[...] = dw
                dbias_ref[...] = dbias

            @pl.when(jnp.logical_not(first))
            def _():
                dw_ref[...] += dw
                dbias_ref[...] += dbias

        dzb_ref[...] = jnp.concatenate([gf_ref[...] + gb_ref[...], dlr], axis=1).astype(dzb_ref.dtype)

    wspec = pl.BlockSpec((128, GLA_K_TOTAL), lambda i: (0, 0))
    bspec = pl.BlockSpec((1, GLA_K_TOTAL), lambda i: (0, 0))
    tile = pl.BlockSpec((ts, GLA_K_TOTAL), lambda i: (i, 0))
    wide = pl.BlockSpec((ts, 2 * GLA_K_TOTAL + GLA_V_TOTAL), lambda i: (i, 0))
    return pl.pallas_call(
        body,
        name=name,
        grid=(s // ts,),
        in_specs=[pl.BlockSpec((ts, 128), lambda i: (i, LR_COL // 128)), wspec, wspec, bspec, bspec, tile, tile, wide, wide],
        out_specs=[pl.BlockSpec((ts, ZB_COLS), lambda i: (i, 0)), wspec, wspec, bspec, bspec],
        out_shape=[
            jax.ShapeDtypeStruct((s, ZB_COLS), _CD),
            jax.ShapeDtypeStruct((128, GLA_K_TOTAL), F32),
            jax.ShapeDtypeStruct((128, GLA_K_TOTAL), F32),
            jax.ShapeDtypeStruct((1, GLA_K_TOTAL), F32),
            jax.ShapeDtypeStruct((1, GLA_K_TOTAL), F32),
        ],
        compiler_params=_cp(("arbitrary",)),
    )(z, waf, wab, baf, bab, dbf, dbb, dqkv_f, dqkv_b)


def _gla_masks(rev):
    lane_head = lax.broadcasted_iota(jnp.int32, (1, GLA_K_TOTAL), 1) >> 6
    head_masks = [lane_head == h for h in range(GLA_HEADS)]
    t = lax.broadcasted_iota(jnp.int32, (GLA_HEADS * GLA_CHUNK, GLA_CHUNK), 0) & (GLA_CHUNK - 1)
    u = lax.broadcasted_iota(jnp.int32, (GLA_HEADS * GLA_CHUNK, GLA_CHUNK), 1)
    tri = (u > t) if rev else (u <= t)
    row = lax.broadcasted_iota(jnp.int32, (GLA_CHUNK, GLA_K_TOTAL), 0)
    total_row = row == (0 if rev else GLA_CHUNK - 1)
    return head_masks, tri, total_row


def _spread(a, head_masks):
    return jnp.concatenate([jnp.where(m, a, 0.0) for m in head_masks], axis=0)


def _stack(a):
    return jnp.concatenate([a[:, GLA_DV * h : GLA_DV * (h + 1)] for h in range(GLA_HEADS)], axis=0)


def _unstack(a):
    return jnp.concatenate([a[GLA_CHUNK * h : GLA_CHUNK * (h + 1)] for h in range(GLA_HEADS)], axis=1)


def _collect(a, head_masks):
    out = None
    for h, m in enumerate(head_masks):
        part = jnp.where(m, a[GLA_CHUNK * h : GLA_CHUNK * (h + 1)], 0.0)
        out = part if out is None else out + part
    return out


def _gla_chunk_terms(q_ref, k_ref, v_ref, b_ref, rows, head_masks, tri, total_row):
    q = q_ref[rows, :] * (GLA_DK**-0.5)
    k = k_ref[rows, :]
    v = v_ref[rows, :]
    b = b_ref[rows, :]
    eb = jnp.exp(b)
    enb = jnp.exp(-b)
    g = jnp.sum(jnp.where(total_row, b, 0.0), axis=0, keepdims=True)
    egb = jnp.exp(g - b)
    qt = q * eb
    kt = k * enb
    kh = k * egb
    q_heads = _spread(qt, head_masks)
    attn = jnp.where(tri, _dot_nt(q_heads, kt), 0.0)
    return v, eb, enb, egb, jnp.exp(g), qt, kt, kh, q_heads, attn


def _gla_specs(s, tb, rev_blocks):
    nb = s // tb
    rb = (lambda i: nb - 1 - i) if rev_blocks else (lambda i: i)
    q_spec = pl.BlockSpec((tb, GLA_K_TOTAL), lambda i: (rb(i), 0))
    k_spec = pl.BlockSpec((tb, GLA_K_TOTAL), lambda i: (rb(i), 1))
    v_spec = pl.BlockSpec((tb, GLA_V_TOTAL), lambda i: (rb(i), 1))
    b_spec = pl.BlockSpec((tb, GLA_K_TOTAL), lambda i: (rb(i), 0))
    o_spec = pl.BlockSpec((tb, GLA_V_TOTAL), lambda i: (rb(i), 0))
    st_spec = pl.BlockSpec((tb // GLA_CHUNK, GLA_DV, GLA_K_TOTAL), lambda i: (rb(i), 0, 0))
    return nb, q_spec, k_spec, v_spec, b_spec, o_spec, st_spec


def _gla_fwd_chunk(cidx, q_ref, k_ref, v_ref, b_ref, o_ref, sv_ref, st_ref, masks):
    head_masks, tri, total_row = masks
    rows = pl.ds(pl.multiple_of(cidx * GLA_CHUNK, GLA_CHUNK), GLA_CHUNK)
    v, _, _, _, eg, _, _, kh, q_heads, attn = _gla_chunk_terms(q_ref, k_ref, v_ref, b_ref, rows, head_masks, tri, total_row)
    o = jnp.concatenate(
        [_dot(attn[GLA_CHUNK * h : GLA_CHUNK * (h + 1)], v[:, GLA_DV * h : GLA_DV * (h + 1)]) for h in range(GLA_HEADS)], axis=1
    )
    st = st_ref[...]
    o_ref[rows, :] = o + _unstack(_dot_nt(q_heads, st))
    sv_ref[cidx] = st
    st_ref[...] = st * eg + _dot_tn(_stack(v), _spread(kh, head_masks))


def _gla_fwd(z, b_f, b_b, *, name):
    s = z.shape[0]
    tb = _rows(s)
    cpb = tb // GLA_CHUNK
    nb, qf, kf, vf, bf, of, sf = _gla_specs(s, tb, False)
    _, qr, kr, vr, br, orr, sr = _gla_specs(s, tb, True)

    def body(qf_ref, kf_ref, vf_ref, bf_ref, qr_ref, kr_ref, vr_ref, br_ref, of_ref, svf_ref, or_ref, svr_ref, stf_ref, str_ref):
        masks_f, masks_r = _gla_masks(False), _gla_masks(True)

        @pl.when(pl.program_id(0) == 0)
        def _():
            stf_ref[...] = jnp.zeros_like(stf_ref)
            str_ref[...] = jnp.zeros_like(str_ref)

        def chunk(ci, carry):
            _gla_fwd_chunk(ci, qf_ref, kf_ref, vf_ref, bf_ref, of_ref, svf_ref, stf_ref, masks_f)
            _gla_fwd_chunk(cpb - 1 - ci, qr_ref, kr_ref, vr_ref, br_ref, or_ref, svr_ref, str_ref, masks_r)
            return carry

        lax.fori_loop(0, cpb, chunk, 0)

    o_shape = jax.ShapeDtypeStruct((s, GLA_V_TOTAL), F32)
    st_shape = jax.ShapeDtypeStruct((s // GLA_CHUNK, GLA_DV, GLA_K_TOTAL), F32)
    return pl.pallas_call(
        body,
        name=name,
        grid=(nb,),
        in_specs=[qf, kf, vf, bf, qr, kr, vr, br],
        out_specs=[of, sf, orr, sr],
        out_shape=[o_shape, st_shape, o_shape, st_shape],
        scratch_shapes=[pltpu.VMEM((GLA_DV, GLA_K_TOTAL), F32)] * 2,
        compiler_params=_cp(("arbitrary",)),
    )(z, z, z, b_f, z, z, z, b_b)


def _gla_bwd_chunk(cidx, q_ref, k_ref, v_ref, b_ref, do_ref, sv_ref, dqkv_ref, db_ref, dst_ref, masks):
    head_masks, tri, total_row = masks
    rows = pl.ds(pl.multiple_of(cidx * GLA_CHUNK, GLA_CHUNK), GLA_CHUNK)
    v, eb, enb, egb, eg, qt, kt, kh, q_heads, attn = _gla_chunk_terms(q_ref, k_ref, v_ref, b_ref, rows, head_masks, tri, total_row)
    do_c = do_ref[rows, :]
    st = sv_ref[cidx]
    dst = dst_ref[...]
    do_s, v_s = _stack(do_c), _stack(v)
    hs = lambda a, h: a[GLA_CHUNK * h : GLA_CHUNK * (h + 1)]
    vs = lambda a, h: a[:, GLA_DV * h : GLA_DV * (h + 1)]
    dattn = jnp.concatenate([_dot_nt(vs(do_c, h), vs(v, h)) for h in range(GLA_HEADS)], axis=0)
    dattn = jnp.where(tri, dattn, 0.0)
    dv = jnp.concatenate([_dot_tn(hs(attn, h), vs(do_c, h)) for h in range(GLA_HEADS)], axis=1)
    dv = dv + _unstack(_dot_nt(_spread(kh, head_masks), dst))
    dqt = _collect(_dot(do_s, st) + _dot(dattn, kt), head_masks)
    dkt = _dot_tn(dattn, q_heads)
    dkh = _collect(_dot(v_s, dst), head_masks)
    dg = jnp.sum(dkh * kh, axis=0, keepdims=True) + jnp.sum(dst * st, axis=0, keepdims=True) * eg
    db = dqt * qt - dkt * kt - dkh * kh + jnp.where(total_row, dg, 0.0)
    dq = dqt * eb * (GLA_DK**-0.5)
    dk = dkt * enb + dkh * egb
    dqkv_ref[rows, :] = jnp.concatenate([dq, dk, dv], axis=1)
    db_ref[rows, :] = db
    dst_ref[...] = dst * eg + _dot_tn(do_s, q_heads)


def _gla_bwd(z, b_f, b_b, do, st_f, st_b, *, name):
    s = z.shape[0]
    tb = _rows(s)
    cpb = tb // GLA_CHUNK
    wide = 2 * GLA_K_TOTAL + GLA_V_TOTAL
    nb, qf, kf, vf, bf, of, sf = _gla_specs(s, tb, True)
    _, qr, kr, vr, br, orr, sr = _gla_specs(s, tb, False)
    gf = pl.BlockSpec((tb, wide), lambda i: (nb - 1 - i, 0))
    gr = pl.BlockSpec((tb, wide), lambda i: (i, 0))

    def body(qf_ref, kf_ref, vf_ref, bf_ref, dof_ref, svf_ref, qr_ref, kr_ref, vr_ref, br_ref, dor_ref, svr_ref,
             gf_ref, dbf_ref, gr_ref, dbr_ref, dstf_ref, dstr_ref):
        masks_f, masks_r = _gla_masks(False), _gla_masks(True)

        @pl.when(pl.program_id(0) == 0)
        def _():
            dstf_ref[...] = jnp.zeros_like(dstf_ref)
            dstr_ref[...] = jnp.zeros_like(dstr_ref)

        def chunk(ci, carry):
            _gla_bwd_chunk(cpb - 1 - ci, qf_ref, kf_ref, vf_ref, bf_ref, dof_ref, svf_ref, gf_ref, dbf_ref, dstf_ref, masks_f)
            _gla_bwd_chunk(ci, qr_ref, kr_ref, vr_ref, br_ref, dor_ref, svr_ref, gr_ref, dbr_ref, dstr_ref, masks_r)
            return carry

        lax.fori_loop(0, cpb, chunk, 0)

    g_shape = jax.ShapeDtypeStruct((s, wide), F32)
    db_shape = jax.ShapeDtypeStruct((s, GLA_K_TOTAL), F32)
    return pl.pallas_call(
        body,
        name=name,
        grid=(nb,),
        in_specs=[qf, kf, vf, bf, of, sf, qr, kr, vr, br, orr, sr],
        out_specs=[gf, bf, gr, br],
        out_shape=[g_shape, db_shape, g_shape, db_shape],
        scratch_shapes=[pltpu.VMEM((GLA_DV, GLA_K_TOTAL), F32)] * 2,
        compiler_params=_cp(("arbitrary",)),
    )(z, z, z, b_f, do, st_f, z, z, z, b_b, do, st_b)


HALO = 8


def _halo_specs(s, ts, width, col):
    last = s // HALO - 1
    per = ts // HALO
    prev = pl.BlockSpec((HALO, width), lambda i: (jnp.maximum(i * per - 1, 0), col))
    nxt = pl.BlockSpec((HALO, width), lambda i: (jnp.minimum((i + 1) * per, last), col))
    return prev, nxt


def _group_ones():
    group = jnp.arange(CONV_WIDTH, dtype=jnp.int32) // CONV_GROUP
    return (group[:, None] == group[None, :]).astype(BF16)


_ONES_SPEC = pl.BlockSpec((CONV_WIDTH, CONV_WIDTH), lambda i: (0, 0))


def _conv_terms(cc_ext, cu_ext, cw, valid):
    n = cc_ext.shape[0]
    hc = jnp.where(valid, cc_ext * cu_ext, 0.0)
    hc_prev = pltpu.roll(hc, 1, 0)
    hc_next = pltpu.roll(hc, n - 1, 0)
    conv = cw[0:1] * hc_prev + cw[1:2] * hc + cw[2:3] * hc_next
    return hc, hc_prev, hc_next, conv


def _ext(prev_ref, cur_ref, next_ref):
    return jnp.concatenate([prev_ref[...], cur_ref[...], next_ref[...]], axis=0)


def _valid_rows(ts, s):
    row = lax.broadcasted_iota(jnp.int32, (ts + 2 * HALO, 1), 0) + (pl.program_id(0) * ts - HALO)
    return (row >= 0) & (row < s)


def _head_norm(o, gn):
    out = []
    for h in range(GLA_HEADS):
        oh = o[:, GLA_DV * h : GLA_DV * (h + 1)]
        r = lax.rsqrt(jnp.mean(oh * oh, axis=-1, keepdims=True) + EPS)
        out.append((oh * r, r))
    return out


def _mix_fwd(z, o_f, o_b, conv_w, conv_norm, gla_norm, *, name):
    s = z.shape[0]
    ts = _rows(s, light=True)
    cprev, cnext = _halo_specs(s, ts, CONV_WIDTH, 1)
    uprev, unext = _halo_specs(s, ts, CONV_WIDTH, 2)

    def body(cb_ref, cc_ref, cu_ref, ccp_ref, ccn_ref, cup_ref, cun_ref, g_ref, of_ref, ob_ref, cw_ref, cn_ref, gn_ref, ones_ref, y_ref):
        valid = _valid_rows(ts, s)
        _, _, _, conv = _conv_terms(_ext(ccp_ref, cc_ref, ccn_ref), _ext(cup_ref, cu_ref, cun_ref), cw_ref[...], valid)
        yc = cb_ref[...] * conv[HALO : HALO + ts]
        ms = _dot_split(yc * yc, ones_ref[...]) * (1.0 / CONV_GROUP)
        y_conv = yc * lax.rsqrt(ms + EPS) * cn_ref[...]
        gate = g_ref[...]
        silu = gate * _sigmoid(gate)
        gn = gn_ref[...]
        y_gla = jnp.concatenate([oh * gn for oh, _ in _head_norm(of_ref[...] + ob_ref[...], gn)], axis=1) * silu
        y_ref[...] = jnp.concatenate([y_conv, y_gla], axis=1).astype(y_ref.dtype)

    col = lambda c, w=CONV_WIDTH: pl.BlockSpec((ts, w), lambda i: (i, c))
    return pl.pallas_call(
        body,
        name=name,
        grid=(s // ts,),
        in_specs=[col(0), col(1), col(2), cprev, cnext, uprev, unext, col(3), col(0), col(0),
                  pl.BlockSpec((CONV_K, CONV_WIDTH), lambda i: (0, 0)), pl.BlockSpec((1, CONV_WIDTH), lambda i: (0, 0)),
                  pl.BlockSpec((1, GLA_DV), lambda i: (0, 0)), _ONES_SPEC],
        out_specs=pl.BlockSpec((ts, D_MODEL), lambda i: (i, 0)),
        out_shape=jax.ShapeDtypeStruct((s, D_MODEL), _CD),
        compiler_params=_cp(("parallel",)),
    )(z, z, z, z, z, z, z, z, o_f, o_b, conv_w, conv_norm, gla_norm, _group_ones())


def _mix_bwd(z, o_f, o_b, dy, conv_w, conv_norm, gla_norm, *, name):
    s = z.shape[0]
    ts = _rows(s)
    halos = [_halo_specs(s, ts, CONV_WIDTH, c) for c in (0, 1, 2)]
    dprev, dnext = _halo_specs(s, ts, CONV_WIDTH, 0)

    def body(cb_ref, cc_ref, cu_ref, cbp_ref, cbn_ref, ccp_ref, ccn_ref, cup_ref, cun_ref, g_ref, of_ref, ob_ref,
             dyc_ref, dyg_ref, dyp_ref, dyn_ref, cw_ref, cn_ref, gn_ref, ones_ref, dza_ref, do_ref, dcw_ref, dcn_ref, dgn_ref):
        n = ts + 2 * HALO
        valid = _valid_rows(ts, s)
        cw = cw_ref[...]
        cn = cn_ref[...]
        ones = ones_ref[...]
        cb = _ext(cbp_ref, cb_ref, cbn_ref)
        cc = _ext(ccp_ref, cc_ref, ccn_ref)
        cu = _ext(cup_ref, cu_ref, cun_ref)
        dy = _ext(dyp_ref, dyc_ref, dyn_ref)
        hc, hc_prev, hc_next, conv = _conv_terms(cc, cu, cw, valid)
        yc = cb * conv
        r = lax.rsqrt(_dot_split(yc * yc, ones) * (1.0 / CONV_GROUP) + EPS)
        yh = yc * r
        dyh = dy * cn
        dyc = r * (dyh - yh * (_dot_split(dyh * yh, ones) * (1.0 / CONV_GROUP)))
        dconv = jnp.where(valid, dyc * cb, 0.0)
        dhc = cw[0:1] * pltpu.roll(dconv, n - 1, 0) + cw[1:2] * dconv + cw[2:3] * pltpu.roll(dconv, 1, 0)
        mid = lambda a: a[HALO : HALO + ts]
        dza_ref[:, 0 : 3 * CONV_WIDTH] = jnp.concatenate([mid(dyc * conv), mid(dhc * cu), mid(dhc * cc)], axis=1).astype(dza_ref.dtype)
        dconv_m = mid(dconv)
        colsum = lambda a: jnp.sum(a, axis=0, keepdims=True)
        dcw = jnp.concatenate([colsum(dconv_m * mid(hc_prev)), colsum(dconv_m * mid(hc)), colsum(dconv_m * mid(hc_next))], axis=0)
        dcn = colsum(mid(dy * yh))

        gate = g_ref[...]
        sg = _sigmoid(gate)
        silu = gate * sg
        gn = gn_ref[...]
        dyg = dyg_ref[...]
        don = dyg * silu
        heads = _head_norm(of_ref[...] + ob_ref[...], gn)
        on = jnp.concatenate([oh * gn for oh, _ in heads], axis=1)
        dza_ref[:, 3 * CONV_WIDTH : ZA_COLS] = (dyg * on * (sg * (1.0 + gate * (1.0 - sg)))).astype(dza_ref.dtype)
        dgn = jnp.zeros((1, GLA_DV), F32)
        dos = []
        for h, (oh, rh) in enumerate(heads):
            donh = don[:, GLA_DV * h : GLA_DV * (h + 1)]
            dgn = dgn + colsum(donh * oh)
            doh = donh * gn
            dos.append(rh * (doh - oh * jnp.mean(doh * oh, axis=-1, keepdims=True)))
        do_ref[...] = jnp.concatenate(dos, axis=1)

        first = pl.program_id(0) == 0

        @pl.when(first)
        def _():
            dcw_ref[...] = dcw
            dcn_ref[...] = dcn
            dgn_ref[...] = dgn

        @pl.when(jnp.logical_not(first))
        def _():
            dcw_ref[...] += dcw
            dcn_ref[...] += dcn
            dgn_ref[...] += dgn

    col = lambda c, w=CONV_WIDTH: pl.BlockSpec((ts, w), lambda i: (i, c))
    cw_spec = pl.BlockSpec((CONV_K, CONV_WIDTH), lambda i: (0, 0))
    cn_spec = pl.BlockSpec((1, CONV_WIDTH), lambda i: (0, 0))
    gn_spec = pl.BlockSpec((1, GLA_DV), lambda i: (0, 0))
    return pl.pallas_call(
        body,
        name=name,
        grid=(s // ts,),
        in_specs=[col(0), col(1), col(2), halos[0][0], halos[0][1], halos[1][0], halos[1][1], halos[2][0], halos[2][1],
                  col(3), col(0), col(0), col(0), col(1), dprev, dnext, cw_spec, cn_spec, gn_spec, _ONES_SPEC],
        out_specs=[pl.BlockSpec((ts, ZA_COLS), lambda i: (i, 0)), col(0), cw_spec, cn_spec, gn_spec],
        out_shape=[
            jax.ShapeDtypeStruct((s, ZA_COLS), _CD),
            jax.ShapeDtypeStruct((s, GLA_V_TOTAL), F32),
            jax.ShapeDtypeStruct((CONV_K, CONV_WIDTH), F32),
            jax.ShapeDtypeStruct((1, CONV_WIDTH), F32),
            jax.ShapeDtypeStruct((1, GLA_DV), F32),
        ],
        compiler_params=_cp(("arbitrary",)),
    )(z, z, z, z, z, z, z, z, z, z, o_f, o_b, dy, dy, dy, dy, conv_w, conv_norm, gla_norm, _group_ones())


def _xa_probs(q_ref, kv_ref, h):
    qh = q_ref[:, XA_HEAD_DIM * h : XA_HEAD_DIM * (h + 1)]
    kh = kv_ref[:, XA_HEAD_DIM * h : XA_HEAD_DIM * (h + 1)]
    vh = kv_ref[:, D_MODEL + XA_HEAD_DIM * h : D_MODEL + XA_HEAD_DIM * (h + 1)]
    sc = _dot_nt(qh, kh) * (XA_HEAD_DIM**-0.5)
    e = jnp.exp(sc - jnp.max(sc, axis=-1, keepdims=True))
    return qh, kh, vh, e / jnp.sum(e, axis=-1, keepdims=True)


def _xattn_fwd(qx, kv, *, name):
    s = qx.shape[0]
    ts = _rows(s, light=True)

    def body(q_ref, kv_ref, o_ref):
        outs = []
        for h in range(XA_HEADS):
            _, _, vh, p = _xa_probs(q_ref, kv_ref, h)
            outs.append(_dot(p, vh))
        o_ref[...] = jnp.concatenate(outs, axis=1).astype(o_ref.dtype)

    return pl.pallas_call(
        body,
        name=name,
        grid=(s // ts,),
        in_specs=[pl.BlockSpec((ts, D_MODEL), lambda i: (i, 0)), pl.BlockSpec((N_MEM, 2 * D_MODEL), lambda i: (0, 0))],
        out_specs=pl.BlockSpec((ts, D_MODEL), lambda i: (i, 0)),
        out_shape=jax.ShapeDtypeStruct((s, D_MODEL), _CD),
        compiler_params=_cp(("parallel",)),
    )(qx, kv)


def _xattn_bwd(qx, kv, dox, *, name):
    s = qx.shape[0]
    ts = _rows(s, light=True)

    def body(q_ref, kv_ref, do_ref, dq_ref, dkv_ref):
        dqs, dks, dvs = [], [], []
        for h in range(XA_HEADS):
            qh, kh, vh, p = _xa_probs(q_ref, kv_ref, h)
            doh = do_ref[:, XA_HEAD_DIM * h : XA_HEAD_DIM * (h + 1)]
            dp = _dot_nt(doh, vh)
            ds = p * (dp - jnp.sum(dp * p, axis=-1, keepdims=True)) * (XA_HEAD_DIM**-0.5)
            dqs.append(_dot(ds, kh))
            dks.append(_dot_tn(ds, qh))
            dvs.append(_dot_tn(p, doh))
        dq_ref[...] = jnp.concatenate(dqs, axis=1).astype(dq_ref.dtype)
        dkv = jnp.concatenate(dks + dvs, axis=1)

        @pl.when(pl.program_id(0) == 0)
        def _():
            dkv_ref[...] = dkv

        @pl.when(pl.program_id(0) > 0)
        def _():
            dkv_ref[...] += dkv

    tile = pl.BlockSpec((ts, D_MODEL), lambda i: (i, 0))
    kv_spec = pl.BlockSpec((N_MEM, 2 * D_MODEL), lambda i: (0, 0))
    return pl.pallas_call(
        body,
        name=name,
        grid=(s // ts,),
        in_specs=[tile, kv_spec, tile],
        out_specs=[tile, kv_spec],
        out_shape=[jax.ShapeDtypeStruct((s, D_MODEL), _CD), jax.ShapeDtypeStruct((N_MEM, 2 * D_MODEL), F32)],
        compiler_params=_cp(("arbitrary",)),
    )(qx, kv, dox)


def _adamw_math(w, g, m, v):
    m = ADAM_B1 * m + (1.0 - ADAM_B1) * g
    v = ADAM_B2 * v + (1.0 - ADAM_B2) * (g * g)
    m_hat = m / (1.0 - ADAM_B1**ADAM_STEP)
    v_hat = v / (1.0 - ADAM_B2**ADAM_STEP)
    delta = -ADAM_LR * (m_hat / (jnp.sqrt(v_hat) + ADAM_EPS) + ADAM_WD * w)
    return delta, m, v


def _adamw(w, m, v, shard_rows, off, *, transposed, name):
    r, c = w.shape
    by_columns = r % 256 != 0
    tr = 512 if (c if by_columns else r) % 512 == 0 and off % 512 == 0 else 256
    if by_columns:
        assert not transposed and off == 0
        g_spec = tile = pl.BlockSpec((r, tr), lambda i: (0, i))
    else:
        g_spec = pl.BlockSpec((c, tr), lambda i: (off // c, i)) if transposed else pl.BlockSpec((tr, c), lambda i: (off // tr + i, 0))
        tile = pl.BlockSpec((tr, c), lambda i: (i, 0))

    def body(w_ref, g_ref, m_ref, v_ref, go_ref, d_ref, nm_ref, nv_ref):
        g = g_ref[...].T if transposed else g_ref[...]
        go_ref[...] = g
        d_ref[...], nm_ref[...], nv_ref[...] = _adamw_math(w_ref[...], g, m_ref[...], v_ref[...])

    return pl.pallas_call(
        body,
        name=name,
        grid=((c if by_columns else r) // tr,),
        in_specs=[tile, g_spec, tile, tile],
        out_specs=[tile] * 4,
        out_shape=[jax.ShapeDtypeStruct((r, c), F32)] * 4,
        compiler_params=_cp(("parallel",)),
    )(w, shard_rows, m, v)


def _adamw_small(groups, *, name):
    n = len(groups)

    def body(*refs):
        ins, outs = refs[: 4 * n], refs[4 * n :]
        for i in range(n):
            w_ref, g_ref, m_ref, v_ref = ins[4 * i : 4 * i + 4]
            outs[3 * i][...], outs[3 * i + 1][...], outs[3 * i + 2][...] = _adamw_math(w_ref[...], g_ref[...], m_ref[...], v_ref[...])

    flat = [a for grp in groups for a in grp]
    vm = pl.BlockSpec(memory_space=pltpu.VMEM)
    res = pl.pallas_call(
        body,
        name=name,
        in_specs=[vm] * (4 * n),
        out_specs=[vm] * (3 * n),
        out_shape=[jax.ShapeDtypeStruct(grp[0].shape, F32) for grp in groups for _ in range(3)],
        compiler_params=_cp(),
    )(*flat)
    return [tuple(res[3 * i : 3 * i + 3]) for i in range(n)]


def _place():
    return lax.axis_index("x"), lax.axis_index("y"), lax.axis_index("c")


def _rel_chip(x, y, k):
    return (1 - x if k & 2 else x), (1 - y if k & 1 else y)


def _half(c, rh):
    return pl.ds(pl.multiple_of(c * rh, 16), rh)


HBM = pl.BlockSpec(memory_space=pltpu.HBM)
SEM = pl.BlockSpec(memory_space=pltpu.SEMAPHORE)
EFFECT = pltpu.SideEffectType.DATAFLOW_SIDE_EFFECTING


def _in_hbm(a):
    return pltpu.with_memory_space_constraint(a, pltpu.HBM)


def _gather_copies(p_ref, land_ref, send_sems, recv_sems):
    rh = p_ref.shape[0] // 2
    x, y, c = _place()
    rows = _half(c, rh)
    copies = []
    for k in range(1, N_CHIPS):
        cx, cy = _rel_chip(x, y, k)
        copies.append(pltpu.make_async_remote_copy(
            src_ref=p_ref.at[rows], dst_ref=land_ref.at[2 * x + y, rows], send_sem=send_sems.at[k - 1], recv_sem=recv_sems.at[k - 1],
            device_id=(cx, cy, c), device_id_type=MESH))
    copies.append(pltpu.make_async_remote_copy(
        src_ref=p_ref, dst_ref=land_ref.at[2 * x + y], send_sem=send_sems.at[N_CHIPS - 1], recv_sem=recv_sems.at[N_CHIPS - 1],
        device_id=(x, y, 1 - c), device_id_type=MESH))
    return copies


def _gather_start(pack, after, *, name):
    r, w = pack.shape

    def body(p_ref, land_ref, after_ref, send_sems, recv_sems, p_thru, land_thru, token):
        for cp in _gather_copies(p_ref, land_ref, send_sems, recv_sems):
            cp.start()
        token[...] = jnp.zeros_like(token)

    return pl.pallas_call(
        body,
        name=name,
        out_shape=(pltpu.SemaphoreType.DMA((N_CHIPS,)), pltpu.SemaphoreType.DMA((N_CHIPS,)), pltpu.HBM((r, w), pack.dtype),
                   pltpu.HBM((N_CHIPS, r, w), pack.dtype), jax.ShapeDtypeStruct((8, 128), F32)),
        in_specs=(HBM, HBM, ANY),
        out_specs=(SEM, SEM, HBM, HBM, pl.BlockSpec(memory_space=pltpu.VMEM)),
        input_output_aliases={0: 2, 1: 3},
        compiler_params=pltpu.CompilerParams(has_side_effects=EFFECT),
    )(_in_hbm(pack), _in_hbm(lax.empty((N_CHIPS, r, w), pack.dtype)), after)


def _gather_wait(send_sems, recv_sems, pack, land, after, *, name):
    def body(p_ref, land_ref, send_sems, recv_sems, after_ref, p_out, land_out):
        for cp in _gather_copies(p_ref, land_ref, send_sems, recv_sems):
            cp.wait_send()
            cp.wait_recv()

    return pl.pallas_call(
        body,
        name=name,
        out_shape=(pltpu.HBM(pack.shape, pack.dtype), pltpu.HBM(land.shape, land.dtype)),
        in_specs=(HBM, HBM, SEM, SEM, ANY),
        out_specs=(HBM, HBM),
        input_output_aliases={0: 0, 1: 1},
        compiler_params=pltpu.CompilerParams(has_side_effects=EFFECT),
    )(pack, land, send_sems, recv_sems, after)


def _gather_spread(land, *, name):
    n, r, w = land.shape
    rh = r // 2

    def body(land_ref, o_ref, send_sems, recv_sems):
        x, y, c = _place()
        rows = _half(c, rh)
        copies = []
        for k in range(1, N_CHIPS):
            cx, cy = _rel_chip(x, y, k)
            copies.append(pltpu.make_async_remote_copy(
                src_ref=land_ref.at[2 * cx + cy, rows], dst_ref=o_ref.at[2 * cx + cy, rows], send_sem=send_sems.at[k - 1],
                recv_sem=recv_sems.at[k - 1], device_id=(x, y, 1 - c), device_id_type=MESH))
        for cp in copies:
            cp.start()
        for cp in copies:
            cp.wait()

    return pl.pallas_call(
        body,
        name=name,
        in_specs=[ANY],
        out_specs=ANY,
        out_shape=jax.ShapeDtypeStruct(land.shape, land.dtype),
        input_output_aliases={0: 0},
        scratch_shapes=[pltpu.SemaphoreType.DMA((N_CHIPS - 1,)), pltpu.SemaphoreType.DMA((N_CHIPS - 1,))],
        compiler_params=pltpu.CompilerParams(has_side_effects=True),
    )(land)


N_PARTS = 2 * (N_CHIPS - 1)


def _scatter_copies(lo_ref, g_ref, land_lo_ref, land_f_ref, send_sems, recv_sems, starting):
    rh = g_ref.shape[1] // 2
    x, y, c = _place()
    copies = []
    for k in range(1, N_CHIPS):
        cx, cy = _rel_chip(x, y, k)
        for i in range(2):
            part = 2 * (k - 1) + (c if starting else i)
            copies.append(pltpu.make_async_remote_copy(
                src_ref=lo_ref.at[2 * cx + cy, pl.ds(i * rh, rh)], dst_ref=land_lo_ref.at[part],
                send_sem=send_sems.at[2 * (k - 1) + i], recv_sem=recv_sems.at[part], device_id=(cx, cy, i), device_id_type=MESH))
    copies.append(pltpu.make_async_remote_copy(
        src_ref=g_ref.at[2 * x + y, _half(1 - c, rh)], dst_ref=land_f_ref, send_sem=send_sems.at[N_PARTS], recv_sem=recv_sems.at[N_PARTS],
        device_id=(x, y, 1 - c), device_id_type=MESH))
    return copies


def _scatter_start(g_lo, g, *, name):
    n, r, w = g.shape
    rh = r // 2

    def body(lo_ref, g_ref, land_lo_ref, land_f_ref, send_sems, recv_sems, lo_thru, g_thru, land_lo_thru, land_f_thru, token):
        for cp in _scatter_copies(lo_ref, g_ref, land_lo_ref, land_f_ref, send_sems, recv_sems, True):
            cp.start()
        token[...] = jnp.zeros_like(token)

    return pl.pallas_call(
        body,
        name=name,
        out_shape=(pltpu.SemaphoreType.DMA((N_PARTS + 1,)), pltpu.SemaphoreType.DMA((N_PARTS + 1,)), pltpu.HBM(g_lo.shape, g_lo.dtype),
                   pltpu.HBM(g.shape, g.dtype), pltpu.HBM((N_PARTS, rh, w), g_lo.dtype), pltpu.HBM((rh, w), g.dtype),
                   jax.ShapeDtypeStruct((8, 128), F32)),
        in_specs=(HBM, HBM, HBM, HBM),
        out_specs=(SEM, SEM, HBM, HBM, HBM, HBM, pl.BlockSpec(memory_space=pltpu.VMEM)),
        input_output_aliases={0: 2, 1: 3, 2: 4, 3: 5},
        compiler_params=pltpu.CompilerParams(has_side_effects=EFFECT),
    )(_in_hbm(g_lo), _in_hbm(g), _in_hbm(lax.empty((N_PARTS, rh, w), g_lo.dtype)), _in_hbm(lax.empty((rh, w), g.dtype)))


def _scatter_wait(send_sems, recv_sems, g_lo, g, land_lo, land_f, after, *, name):
    def body(lo_ref, g_ref, land_lo_ref, land_f_ref, send_sems, recv_sems, after_ref, o0, o1, o2, o3):
        for cp in _scatter_copies(lo_ref, g_ref, land_lo_ref, land_f_ref, send_sems, recv_sems, False):
            cp.wait_send()
            cp.wait_recv()

    arrays = (g_lo, g, land_lo, land_f)
    return pl.pallas_call(
        body,
        name=name,
        out_shape=tuple(pltpu.HBM(a.shape, a.dtype) for a in arrays),
        in_specs=(HBM, HBM, HBM, HBM, SEM, SEM, ANY),
        out_specs=(HBM, HBM, HBM, HBM),
        input_output_aliases={0: 0, 1: 1, 2: 2, 3: 3},
        compiler_params=pltpu.CompilerParams(has_side_effects=EFFECT),
    )(*arrays, send_sems, recv_sems, after)


def _scatter_sum(g, land_lo, land_f, where, *, name):
    n, r, w = g.shape
    rh = r // 2
    tr = _pick(rh, (256, 160, 80))
    nt = rh // tr

    def body(where_ref, g_ref, f_ref, lo_ref, o_ref):
        acc = g_ref[0] + f_ref[...]
        for part in range(N_PARTS):
            acc = acc + lo_ref[part].astype(F32)
        o_ref[...] = acc

    return pl.pallas_call(
        body,
        name=name,
        grid_spec=pltpu.PrefetchScalarGridSpec(
            num_scalar_prefetch=1,
            grid=(nt,),
            in_specs=[pl.BlockSpec((1, tr, w), lambda i, wh: (wh[1], wh[0] * nt + i, 0)),
                      pl.BlockSpec((tr, w), lambda i, wh: (i, 0)),
                      pl.BlockSpec((N_PARTS, tr, w), lambda i, wh: (0, i, 0))],
            out_specs=pl.BlockSpec((tr, w), lambda i, wh: (wh[0] * nt + i, 0)),
        ),
        out_shape=jax.ShapeDtypeStruct((r, w), F32),
        compiler_params=_cp(("parallel",)),
    )(where, g, land_f, land_lo)


def _swap_all(shards, *, name):
    n = len(shards)

    def body(*refs):
        ins, outs = refs[:n], refs[n : 2 * n]
        send_sems, recv_sems = refs[2 * n :]
        x, y, c = _place()
        copies = []
        for i, (e_ref, o_ref) in enumerate(zip(ins, outs)):
            rows = _half(c, e_ref.shape[0] // 2)
            copies.append(pltpu.make_async_remote_copy(src_ref=e_ref.at[rows], dst_ref=o_ref.at[rows], send_sem=send_sems.at[i],
                                                       recv_sem=recv_sems.at[i], device_id=(x, y, 1 - c), device_id_type=MESH))
        for cp in copies:
            cp.start()
        for cp in copies:
            cp.wait()

    return pl.pallas_call(
        body,
        name=name,
        in_specs=[ANY] * n,
        out_specs=[ANY] * n,
        out_shape=[jax.ShapeDtypeStruct(e.shape, e.dtype) for e in shards],
        input_output_aliases={i: i for i in range(n)},
        scratch_shapes=[pltpu.SemaphoreType.DMA((n,)), pltpu.SemaphoreType.DMA((n,))],
        compiler_params=pltpu.CompilerParams(has_side_effects=True),
    )(*shards)


def _sum_small(small, after):
    n_dev = 8

    def body(s_ref, after_ref, o_ref, all_ref, send_sems, recv_sems):
        x, y, c = _place()
        me = 4 * x + 2 * y + c
        all_ref[me] = s_ref[...]
        copies = []
        for k in range(1, n_dev):
            cx, cy = _rel_chip(x, y, k >> 1)
            cc = 1 - c if k & 1 else c
            copies.append(pltpu.make_async_remote_copy(
                src_ref=s_ref, dst_ref=all_ref.at[me], send_sem=send_sems.at[k - 1], recv_sem=recv_sems.at[k - 1],
                device_id=(cx, cy, cc), device_id_type=MESH))
        for cp in copies:
            cp.start()
        for cp in copies:
            cp.wait()
        acc = all_ref[0]
        for a in range(1, n_dev):
            acc = acc + all_ref[a]
        o_ref[...] = acc

    vm = pl.BlockSpec(memory_space=pltpu.VMEM)
    return pl.pallas_call(
        body,
        name="sum_small",
        in_specs=[vm, ANY],
        out_specs=vm,
        out_shape=jax.ShapeDtypeStruct(small.shape, F32),
        scratch_shapes=[pltpu.VMEM((n_dev,) + small.shape, F32), pltpu.SemaphoreType.DMA((n_dev - 1,)), pltpu.SemaphoreType.DMA((n_dev - 1,))],
        compiler_params=pltpu.CompilerParams(has_side_effects=True),
    )(small, after)


MATS = {"w_in": (776, True), "w_out": (256, False), "w_xq": (256, False), "w_xkv": (512, True), "w_xo": (256, False),
        "w_up": (1024, True), "w_down": (1024, False)}
GATHER_FIRST = ("w_in",)
GATHER_REST = ("w_out", "w_xq", "w_xkv", "w_xo", "w_up", "w_down")
GRAD_GROUPS = (("w_up", "w_down"), ("w_out", "w_xq", "w_xkv", "w_xo"), ("w_in",))


def _group_rows(names):
    n = sum(MATS[name][0] for name in names)
    return n + (-n) % 32


def _pack(pieces, rows):
    p = jnp.concatenate(pieces, axis=0) if len(pieces) > 1 else pieces[0]
    return jnp.pad(p, ((0, rows - p.shape[0]), (0, 0))) if rows > p.shape[0] else p


SMALL = (
    ("mix_norm", 1024), ("conv_norm", 512), ("b_af", 256), ("b_ab", 256), ("gla_norm", 128), ("xa_norm", 1024), ("mem_norm", 1024),
    ("mlp_norm", 1024), ("final_norm", 1024), ("conv_w", 1536), ("w_af", 4096), ("w_ab", 4096), ("loss", 128),
)


def kernel(x, mem, mix_norm, w_in, conv_w, conv_norm, w_af, b_af, w_ab, b_ab, gla_norm, w_out, xa_norm, mem_norm, w_xq, w_xkv, w_xo, mlp_norm, w_up, w_down, final_norm, loss_target, m_mix_norm, m_w_in, m_conv_w, m_conv_norm, m_w_af, m_b_af, m_w_ab, m_b_ab, m_gla_norm, m_w_out, m_xa_norm, m_mem_norm, m_w_xq, m_w_xkv, m_w_xo, m_mlp_norm, m_w_up, m_w_down, m_final_norm, v_mix_norm, v_w_in, v_conv_w, v_conv_norm, v_w_af, v_b_af, v_w_ab, v_b_ab, v_gla_norm, v_w_out, v_xa_norm, v_mem_norm, v_w_xq, v_w_xkv, v_w_xo, v_mlp_norm, v_w_up, v_w_down, v_final_norm):
    given = dict(locals())
    xi, yi, ci = _place()
    chip = 2 * xi + yi
    where = jnp.stack([ci, chip]).astype(jnp.int32)

    lo = {name: (given[name][0].T if MATS[name][1] else given[name][0]).astype(_CD) for name in MATS}
    pack_rest = _pack([lo[name] for name in GATHER_REST], _group_rows(GATHER_REST))
    pack_first = _pack([lo[name] for name in GATHER_FIRST], _group_rows(GATHER_FIRST))
    xs, mems, tgt = x[0], mem[0], loss_target[0]
    behind = lambda gain, token: gain + token[0, 0]

    def placed(shard, full_shape, col):
        return lax.dynamic_update_slice(jnp.zeros(full_shape, F32), shard, (0, col)).reshape(-1, 128)

    sw = jnp.concatenate([
        placed(conv_w[0], (CONV_K, CONV_WIDTH), 128 * chip),
        placed(w_af[0], (GLA_LOWRANK, GLA_K_TOTAL), 64 * chip),
        placed(w_ab[0], (GLA_LOWRANK, GLA_K_TOTAL), 64 * chip),
    ], axis=0)
    sw = jnp.pad(sw, ((0, SMALL_ROWS - sw.shape[0]), (0, 0))) * (ci == 0).astype(F32)
    sw = _sum_small(sw, mix_norm)

    first_send, first_recv, pack_first, land_first, first_token = _gather_start(pack_first, sw, name="gather_first_start")
    rest_send, rest_recv, pack_rest, land_rest, rest_token = _gather_start(pack_rest, first_token, name="gather_rest_start")
    h1 = _rms_fwd(xs, behind(mix_norm, rest_token), name="norm_mix")
    pack_first, land_first = _gather_wait(first_send, first_recv, pack_first, land_first, h1, name="gather_first_wait")
    got_first = _gather_spread(land_first, name="gather_first_spread")

    def whole(got, off, rows):
        return got[:, off : off + rows].reshape(N_CHIPS * rows, D_MODEL)

    w_in_t = whole(got_first, 0, MATS["w_in"][0])
    w_za = jnp.concatenate([w_in_t[0:1536], w_in_t[2560:3072]], axis=0)
    w_zb = jnp.concatenate([w_in_t[1536:2560], w_in_t[3072:W_IN_COLS], jnp.zeros((ZB_COLS - 1056, D_MODEL), _CD)], axis=0)
    conv_w_full = sw[0:12].reshape(CONV_K, CONV_WIDTH)
    w_af_full = sw[12:44].reshape(GLA_LOWRANK, GLA_K_TOTAL)
    w_ab_full = sw[44:76].reshape(GLA_LOWRANK, GLA_K_TOTAL)
    waf_p = jnp.pad(w_af_full, ((0, 128 - GLA_LOWRANK), (0, 0))).astype(_CD)
    wab_p = jnp.pad(w_ab_full, ((GLA_LOWRANK, 128 - 2 * GLA_LOWRANK), (0, 0))).astype(_CD)

    z_b = _mm(h1, w_zb, mode="nt", name="proj_in_b", tn=ZB_COLS)
    z_a = _mm(h1, w_za, mode="nt", name="proj_in_a", tm=512, tn=ZA_COLS)
    b_f, b_b = _gate_fwd(z_b, waf_p, wab_p, b_af, b_ab, name="gates")
    o_f, st_f, o_b, st_b = _gla_fwd(z_b, b_f, b_b, name="gla_scan")
    y = _mix_fwd(z_a, o_f, o_b, conv_w_full, conv_norm, gla_norm, name="mix_out")
    pack_rest, land_rest = _gather_wait(rest_send, rest_recv, pack_rest, land_rest, y, name="gather_rest_wait")
    gathered = _gather_spread(land_rest, name="gather_rest_spread")
    wt, off = {}, 0
    for name in GATHER_REST:
        wt[name] = whole(gathered, off, MATS[name][0])
        off += MATS[name][0]
    x1, hx = _mm_rows(y, wt["w_out"], mode="nn", name="proj_out", rows=(xs,), vecs=(xa_norm,), out_rows=(F32, _CD),
                      epilogue=_ep_residual_norm, tm=1024)
    qx = _mm(hx, wt["w_xq"], mode="nn", name="proj_xq", out_dtypes=(_CD,))
    hmem = _rms_fwd(mems, mem_norm, name="norm_mem")
    kv = _mm(hmem, wt["w_xkv"], mode="nt", name="proj_xkv", out_dtypes=(_CD,))
    ox = _xattn_fwd(qx, kv, name="xattn")
    x2, hm = _mm_rows(ox, wt["w_xo"], mode="nn", name="proj_xo", rows=(x1,), vecs=(mlp_norm,), out_rows=(F32, _CD),
                      epilogue=_ep_residual_norm, tm=1024)
    act, relu_u = _mm(hm, wt["w_up"], mode="nt", name="mlp_up", out_dtypes=(_CD, _CD), tm=2048,
                      epilogue=lambda acc: (jnp.square(jnp.maximum(acc, 0.0)), jnp.maximum(acc, 0.0)))
    dx3, dx3_lo, loss_part, g_final_norm = _mm_rows(
        act, wt["w_down"], mode="nn", name="mlp_down", rows=(x2, tgt), vecs=(final_norm.reshape(1, D_MODEL),),
        out_rows=(F32, _CD), out_vecs=(128, D_MODEL), epilogue=_ep_loss)

    grads_t = {}

    def start_group(names, tag):
        rows = _group_rows(names)
        g = jnp.stack([_pack([grads_t[name][a * MATS[name][0] : (a + 1) * MATS[name][0]] for name in names], rows) for a in range(N_CHIPS)])
        return _scatter_start(g.astype(_TD), g, name="grads_" + tag + "_start")

    def finish_group(state, after, tag):
        send_sems, recv_sems, g_lo, g, land_lo, land_f, _ = state
        g_lo, g, land_lo, land_f = _scatter_wait(send_sems, recv_sems, g_lo, g, land_lo, land_f, after, name="grads_" + tag + "_wait")
        return _scatter_sum(g, land_lo, land_f, where, name="grads_" + tag + "_sum")

    def new_packs(names):
        shape = (N_CHIPS, _group_rows(names), D_MODEL)
        return lax.empty(shape, F32), lax.empty(shape, _TD)

    def grad_into(packs, names, which, a, b, name):
        off = sum(MATS[other][0] for other in names[: names.index(which)])
        return _mm_tn_into(a, b, packs, rows=MATS[which][0], off=off, name=name)

    du = _mm(dx3_lo, wt["w_down"], mode="nt", name="mlp_down_dx", out_dtypes=(_CD,), extras=(relu_u,), tm=2048,
             epilogue=lambda acc, rr: (acc * (2.0 * rr.astype(F32)),))
    packs = new_packs(GRAD_GROUPS[0])
    packs = grad_into(packs, GRAD_GROUPS[0], "w_down", act, dx3_lo, "mlp_down_dw")
    packs = grad_into(packs, GRAD_GROUPS[0], "w_up", du, hm, "mlp_up_dw")
    mlp_state = _scatter_start(packs[1], packs[0], name="grads_mlp_start")
    dx2, dx2_lo, g_mlp_norm = _mm_rows(
        du, wt["w_up"], mode="nn", name="mlp_up_dx", rows=(x2, dx3), vecs=(behind(mlp_norm, mlp_state[-1]),),
        out_rows=(F32, _CD), out_vecs=(D_MODEL,), epilogue=_ep_norm_bwd)
    dox = _mm(dx2_lo, wt["w_xo"], mode="nt", name="proj_xo_dx", out_dtypes=(_CD,))
    packs = new_packs(GRAD_GROUPS[1])
    packs = grad_into(packs, GRAD_GROUPS[1], "w_xo", ox, dx2_lo, "proj_xo_dw")
    dqx, dkv = _xattn_bwd(qx, kv, dox, name="xattn_bwd")
    packs = grad_into(packs, GRAD_GROUPS[1], "w_xq", hx, dqx, "proj_xq_dw")
    dx1, dx1_lo, g_xa_norm = _mm_rows(
        dqx, wt["w_xq"], mode="nt", name="proj_xq_dx", rows=(x1, dx2), vecs=(xa_norm,),
        out_rows=(F32, _CD), out_vecs=(D_MODEL,), epilogue=_ep_norm_bwd, tm=1024)
    dkv_lo = dkv.astype(_CD)
    packs = grad_into(packs, GRAD_GROUPS[1], "w_xkv", dkv_lo, hmem, "proj_xkv_dw")
    dhmem = _mm(dkv_lo, wt["w_xkv"], mode="nn", name="proj_xkv_dx")
    g_mem_norm = _rms_gain_grad(mems, dhmem, name="norm_mem_bwd")
    dy = _mm(dx1_lo, wt["w_out"], mode="nt", name="proj_out_dx")
    packs = grad_into(packs, GRAD_GROUPS[1], "w_out", y, dx1_lo, "proj_out_dw")
    attn_state = _scatter_start(packs[1], packs[0], name="grads_attn_start")
    dz_a, do, g_conv_w, g_conv_norm, g_gla_norm = _mix_bwd(z_a, o_f, o_b, dy, conv_w_full, behind(conv_norm, attn_state[-1]), gla_norm, name="mix_out_bwd")
    dqkv_f, db_f, dqkv_b, db_b = _gla_bwd(z_b, b_f, b_b, do, st_f, st_b, name="gla_scan_bwd")
    dz_b, g_waf_p, g_wab_p, g_b_af, g_b_ab = _gate_bwd(z_b, waf_p, wab_p, b_af, b_ab, db_f, db_b, dqkv_f, dqkv_b, name="gates_bwd")
    g_za = _mm_tn(dz_a, h1, name="proj_in_a_dw")
    g_zb = _mm_tn(dz_b, h1, name="proj_in_b_dw")
    grads_t["w_in"] = jnp.concatenate([g_za[0:1536], g_zb[0:1024], g_za[1536:2048], g_zb[1024:1056]], axis=0)
    in_state = start_group(GRAD_GROUPS[2], "in")
    dh1_a = _mm(dz_a, w_za, mode="nn", name="proj_in_a_dx", tm=512, tk=ZA_COLS)
    grad_x, g_mix_norm = _mm_rows(
        dz_b, w_zb, mode="nn", name="proj_in_b_dx", rows=(xs, dx1, dh1_a), vecs=(behind(mix_norm, in_state[-1]),),
        out_rows=(F32,), out_vecs=(D_MODEL,), epilogue=_ep_norm_bwd)

    half_mlp = finish_group(mlp_state, grad_x, "mlp")
    half_attn = finish_group(attn_state, half_mlp, "attn")
    half_in = finish_group(in_state, half_attn, "in")
    shard_rows = {}
    for names, rows in zip(GRAD_GROUPS, _swap_all([half_mlp, half_attn, half_in], name="shards_to_sibling")):
        off = 0
        for name in names:
            shard_rows[name] = (rows, off)
            off += MATS[name][0]

    small_vals = dict(mix_norm=g_mix_norm, conv_norm=g_conv_norm, b_af=g_b_af, b_ab=g_b_ab, gla_norm=g_gla_norm, xa_norm=g_xa_norm,
                      mem_norm=g_mem_norm, mlp_norm=g_mlp_norm, final_norm=g_final_norm, conv_w=g_conv_w,
                      w_af=g_waf_p[0:GLA_LOWRANK], w_ab=g_wab_p[GLA_LOWRANK : 2 * GLA_LOWRANK], loss=loss_part)
    small = jnp.concatenate([small_vals[name].reshape(-1, 128) for name, _ in SMALL], axis=0)
    small = _sum_small(jnp.pad(small, ((0, SMALL_ROWS - small.shape[0]), (0, 0))), loss_part)
    g_small, off = {}, 0
    for name, n in SMALL:
        g_small[name] = small[off : off + n // 128]
        off += n // 128
    loss = g_small["loss"][0, 0]
    g_small["conv_w"] = lax.dynamic_slice(g_small["conv_w"].reshape(CONV_K, CONV_WIDTH), (0, 128 * chip), (CONV_K, 128))
    g_small["w_af"] = lax.dynamic_slice(g_small["w_af"].reshape(GLA_LOWRANK, GLA_K_TOTAL), (0, 64 * chip), (GLA_LOWRANK, 64))
    g_small["w_ab"] = lax.dynamic_slice(g_small["w_ab"].reshape(GLA_LOWRANK, GLA_K_TOTAL), (0, 64 * chip), (GLA_LOWRANK, 64))

    names = ["mix_norm", "w_in", "conv_w", "conv_norm", "w_af", "b_af", "w_ab", "b_ab", "gla_norm", "w_out", "xa_norm", "mem_norm",
             "w_xq", "w_xkv", "w_xo", "mlp_norm", "w_up", "w_down", "final_norm"]
    big_names = list(MATS)
    as2d = lambda a: a.reshape(1, -1) if a.ndim == 1 else a.reshape(a.shape[-2:])
    grads, deltas, new_m, new_v = {}, {}, {}, {}
    for name in big_names:
        rows, off = shard_rows[name]
        wmv = [as2d(given[name]), as2d(given["m_" + name]), as2d(given["v_" + name])]
        as_stored = name == "w_in"
        if as_stored:
            wmv = [a.T for a in wmv]
        res = _adamw(*wmv, rows, off, transposed=MATS[name][1] and not as_stored, name="adamw_" + name)
        grads[name], deltas[name], new_m[name], new_v[name] = [a.T for a in res] if as_stored else res
    small_names = [name for name in names if name not in big_names]
    groups = []
    for name in small_names:
        grads[name] = g_small[name].reshape(as2d(given[name]).shape)
        groups.append((as2d(given[name]), grads[name], as2d(given["m_" + name]), as2d(given["v_" + name])))
    for name, res in zip(small_names, _adamw_small(groups, name="adamw_small")):
        deltas[name], new_m[name], new_v[name] = res

    like = lambda name, a: a.reshape(given[name].shape)
    return (loss, grad_x[None], *[like(n, grads[n]) for n in names], *[like(n, deltas[n]) for n in names],
            *[like(n, new_m[n]) for n in names], *[like(n, new_v[n]) for n in names])
```

```python
import jax
import jax.numpy as jnp
from jax import lax
from jax.experimental import pallas as pl
from jax.experimental.pallas import tpu as pltpu

F32 = jnp.float32
BF16 = jnp.bfloat16
_CD = jnp.bfloat16
_TD = jnp.bfloat16

D_MODEL = 1024
N_MEM = 256
CONV_WIDTH = 512
CONV_GROUP = 64
CONV_K = 3
GLA_HEADS = 4
GLA_DK = 64
GLA_DV = 128
GLA_K_TOTAL = 256
GLA_V_TOTAL = 512
GLA_LOWRANK = 16
GLA_GATE_SCALE = 1.0 / 16.0
GLA_CHUNK = 64
XA_HEADS = 4
XA_HEAD_DIM = 256
D_FF = 4096
EPS = 1e-6
W_IN_COLS = 3104
ZA_COLS = 2048
ZB_COLS = 1152
LR_COL = 1024

ADAM_LR = 0.001
ADAM_B1 = 0.9
ADAM_B2 = 0.999
ADAM_EPS = 1e-08
ADAM_WD = 0.01
ADAM_STEP = 10

N_CHIPS = 4
SMALL_ROWS = 128
SMALL_W_WORDS = 4096

_TS = 512
_VMEM = 44 * 1024 * 1024
MESH = pl.DeviceIdType.MESH
ANY = pl.BlockSpec(memory_space=pl.ANY)


def _cp(sem=None, **kw):
    return pltpu.CompilerParams(dimension_semantics=sem, vmem_limit_bytes=_VMEM, **kw)


def _dot(a, b):
    return jnp.dot(a.astype(_CD), b.astype(_CD), preferred_element_type=F32)


def _dot_nt(a, b):
    return lax.dot_general(a.astype(_CD), b.astype(_CD), (((1,), (1,)), ((), ())), preferred_element_type=F32)


def _dot_tn(a, b):
    return lax.dot_general(a.astype(_CD), b.astype(_CD), (((0,), (0,)), ((), ())), preferred_element_type=F32)


def _dot_split(x, ones):
    hi = x.astype(BF16)
    r = x - hi.astype(F32)
    mid = r.astype(BF16)
    lo = (r - mid.astype(F32)).astype(BF16)
    d = lambda p: jnp.dot(p, ones, preferred_element_type=F32)
    return d(hi) + d(mid) + d(lo)


def _pick(n, cands=(1024, 640, 512, 256, 128)):
    for t in cands:
        if n % t == 0:
            return t
    return n


def _rows(s, light=False):
    return min(2 * _TS if light else _TS, s)


def _sigmoid(v):
    e = jnp.exp(-jnp.abs(v))
    return jnp.where(v >= 0, 1.0 / (1.0 + e), e / (1.0 + e))


def _mm(a, b, *, mode, name, out_dtypes=(F32,), extras=(), epilogue=None, tm=None, tn=None, tk=None):
    m, k = a.shape
    n = b.shape[1] if mode == "nn" else b.shape[0]
    tm = min(m, tm or 1024)
    tn = tn or _pick(n)
    tk = tk or _pick(k)
    nk = k // tk
    n_ex, n_out = len(extras), len(out_dtypes)

    def body(*refs):
        a_ref, b_ref = refs[:2]
        ex = refs[2 : 2 + n_ex]
        outs = refs[2 + n_ex : 2 + n_ex + n_out]
        part = _dot(a_ref[...], b_ref[...]) if mode == "nn" else _dot_nt(a_ref[...], b_ref[...])

        def finish(acc):
            res = epilogue(acc, *[e[...] for e in ex]) if epilogue else (acc,)
            for o, r in zip(outs, res):
                o[...] = r.astype(o.dtype)

        if nk == 1:
            finish(part)
        else:
            acc_ref = refs[-1]
            kk = pl.program_id(2)

            @pl.when(kk == 0)
            def _():
                acc_ref[...] = part

            @pl.when(kk > 0)
            def _():
                acc_ref[...] += part

            @pl.when(kk == nk - 1)
            def _():
                finish(acc_ref[...])

    b_spec = pl.BlockSpec((tk, tn), lambda i, j, kk: (kk, j)) if mode == "nn" else pl.BlockSpec((tn, tk), lambda i, j, kk: (j, kk))
    tile = pl.BlockSpec((tm, tn), lambda i, j, kk: (i, j))
    out = pl.pallas_call(
        body,
        name=name,
        grid=(m // tm, n // tn, nk),
        in_specs=[pl.BlockSpec((tm, tk), lambda i, j, kk: (i, kk)), b_spec] + [tile] * n_ex,
        out_specs=[tile] * n_out,
        out_shape=[jax.ShapeDtypeStruct((m, n), dt) for dt in out_dtypes],
        scratch_shapes=[pltpu.VMEM((tm, tn), F32)] if nk > 1 else [],
        compiler_params=_cp(("parallel", "parallel", "arbitrary")),
    )(a, b, *extras)
    return out[0] if n_out == 1 else out


def _mm_tn(a, b, *, name):
    s, m = a.shape
    n = b.shape[1]
    cap = max(128, (1 << 20) // n)
    tm = _pick(m, tuple(t for t in (512, 640, 384, 256, 128) if t <= max(cap, 128)))
    ts = min(s, 1 << (((1 << 22) // n).bit_length() - 1))
    ns = s // ts

    def body(a_ref, b_ref, o_ref):
        part = _dot_tn(a_ref[...], b_ref[...])
        if ns == 1:
            o_ref[...] = part
        else:
            ss = pl.program_id(1)

            @pl.when(ss == 0)
            def _():
                o_ref[...] = part

            @pl.when(ss > 0)
            def _():
                o_ref[...] += part

    return pl.pallas_call(
        body,
        name=name,
        grid=(m // tm, ns),
        in_specs=[pl.BlockSpec((ts, tm), lambda i, ss: (ss, i)), pl.BlockSpec((ts, n), lambda i, ss: (ss, 0))],
        out_specs=pl.BlockSpec((tm, n), lambda i, ss: (i, 0)),
        out_shape=jax.ShapeDtypeStruct((m, n), F32),
        compiler_params=_cp(("parallel", "arbitrary")),
    )(a, b)


def _mm_tn_into(a, b, packs, *, rows, off, name):
    s, m = a.shape
    n = b.shape[1]
    tm = 1024 if rows % 1024 == 0 and s >= 4096 else 512
    tr = min(tm, rows)
    per, chips = rows // tr, tm // tr
    ts = min(s, (1 << (((1 << 22) // n).bit_length() - 1)) * 512 // tm)
    ns = s // ts

    def body(a_ref, b_ref, f_in, lo_in, f_ref, lo_ref):
        part = _dot_tn(a_ref[...], b_ref[...])
        pieces = [part[c * tr : (c + 1) * tr] for c in range(chips)]
        if ns == 1:
            for c, p in enumerate(pieces):
                f_ref[c] = p
                lo_ref[c] = p.astype(lo_ref.dtype)
        else:
            ss = pl.program_id(1)

            @pl.when(ss == 0)
            def _():
                for c, p in enumerate(pieces):
                    f_ref[c] = p

            @pl.when(ss > 0)
            def _():
                for c, p in enumerate(pieces):
                    f_ref[c] += p

            @pl.when(ss == ns - 1)
            def _():
                lo_ref[...] = f_ref[...].astype(lo_ref.dtype)

    spec = pl.BlockSpec((chips, tr, n), lambda i, ss: (i // per, off // tr + i % per, 0))
    return pl.pallas_call(
        body,
        name=name,
        grid=(m // tm, ns),
        in_specs=[pl.BlockSpec((ts, tm), lambda i, ss: (ss, i)), pl.BlockSpec((ts, n), lambda i, ss: (ss, 0)), ANY, ANY],
        out_specs=[spec, spec],
        out_shape=[jax.ShapeDtypeStruct(p.shape, p.dtype) for p in packs],
        input_output_aliases={2: 0, 3: 1},
        compiler_params=_cp(("parallel", "arbitrary")),
    )(a, b, *packs)


def _mm_rows(a, b, *, mode, name, rows=(), vecs=(), out_rows=(), out_vecs=(), epilogue, tm=512):
    m, k = a.shape
    n = b.shape[1] if mode == "nn" else b.shape[0]
    tm = min(m, tm)
    parts = 2 if tm % 256 == 0 else 1
    n_r, n_v, n_or, n_ov = len(rows), len(vecs), len(out_rows), len(out_vecs)

    def body(*refs):
        a_ref, b_ref = refs[:2]
        r_refs = refs[2 : 2 + n_r]
        v_refs = refs[2 + n_r : 2 + n_r + n_v]
        or_refs = refs[2 + n_r + n_v : 2 + n_r + n_v + n_or]
        ov_refs = refs[2 + n_r + n_v + n_or :]
        res_vecs = None
        for p in range(parts):
            rs = slice(p * tm // parts, (p + 1) * tm // parts)
            acc = _dot(a_ref[rs, :], b_ref[...]) if mode == "nn" else _dot_nt(a_ref[rs, :], b_ref[...])
            res_rows, part_vecs = epilogue(acc, [r[rs, :] for r in r_refs], [v[...] for v in v_refs])
            for o, r in zip(or_refs, res_rows):
                o[rs, :] = r.astype(o.dtype)
            res_vecs = part_vecs if res_vecs is None else [s + t for s, t in zip(res_vecs, part_vecs)]
        if n_ov:
            first = pl.program_id(0) == 0

            @pl.when(first)
            def _():
                for o, r in zip(ov_refs, res_vecs):
                    o[...] = r

            @pl.when(jnp.logical_not(first))
            def _():
                for o, r in zip(ov_refs, res_vecs):
                    o[...] += r

    tile = pl.BlockSpec((tm, n), lambda i: (i, 0))
    whole = lambda arr: pl.BlockSpec(arr.shape, lambda i: (0, 0))
    vec = lambda w: pl.BlockSpec((1, w), lambda i: (0, 0))
    out = pl.pallas_call(
        body,
        name=name,
        grid=(m // tm,),
        in_specs=[pl.BlockSpec((tm, k), lambda i: (i, 0)), whole(b)] + [tile] * n_r + [vec(v.shape[1]) for v in vecs],
        out_specs=[tile] * n_or + [vec(w) for w in out_vecs],
        out_shape=[jax.ShapeDtypeStruct((m, n), dt) for dt in out_rows] + [jax.ShapeDtypeStruct((1, w), F32) for w in out_vecs],
        compiler_params=_cp(("arbitrary",) if n_ov else ("parallel",)),
    )(a, b, *rows, *vecs)
    return out


def _ep_residual_norm(acc, rows, vecs):
    x = acc + rows[0]
    r = lax.rsqrt(jnp.mean(x * x, axis=-1, keepdims=True) + EPS)
    return [x, x * r * vecs[0]], []


def _ep_norm_bwd(acc, rows, vecs):
    dy = acc
    for extra in rows[2:]:
        dy = dy + extra
    x, dres = rows[0], rows[1]
    r = lax.rsqrt(jnp.mean(x * x, axis=-1, keepdims=True) + EPS)
    xh = x * r
    dxh = dy * vecs[0]
    dx = r * (dxh - xh * jnp.mean(dxh * xh, axis=-1, keepdims=True)) + dres
    return [dx, dx], [jnp.sum(dy * xh, axis=0, keepdims=True)]


def _ep_loss(acc, rows, vecs):
    x = acc + rows[0]
    d = x.shape[-1]
    r = lax.rsqrt(jnp.mean(x * x, axis=-1, keepdims=True) + EPS)
    xh = x * r
    err = xh * vecs[0] - rows[1]
    loss = jnp.zeros((1, 128), F32) + 0.5 * jnp.sum(jnp.mean(err * err, axis=-1, keepdims=True))
    dy = err * (1.0 / d)
    dxh = dy * vecs[0]
    dx = r * (dxh - xh * jnp.mean(dxh * xh, axis=-1, keepdims=True))
    return [dx, dx], [loss, jnp.sum(dy * xh, axis=0, keepdims=True)]


def _rms_fwd(x, g, *, name):
    s, d = x.shape
    ts = _rows(s, light=True)

    def body(x_ref, g_ref, o_ref):
        xf = x_ref[...]
        r = lax.rsqrt(jnp.mean(xf * xf, axis=-1, keepdims=True) + EPS)
        o_ref[...] = (xf * r * g_ref[...]).astype(o_ref.dtype)

    return pl.pallas_call(
        body,
        name=name,
        grid=(s // ts,),
        in_specs=[pl.BlockSpec((ts, d), lambda i: (i, 0)), pl.BlockSpec((1, d), lambda i: (0, 0))],
        out_specs=pl.BlockSpec((ts, d), lambda i: (i, 0)),
        out_shape=jax.ShapeDtypeStruct((s, d), _CD),
        compiler_params=_cp(("parallel",)),
    )(x, g)


def _rms_gain_grad(x, dy, *, name):
    s, d = x.shape
    ts = _rows(s)

    def body(x_ref, dy_ref, dg_ref):
        xf = x_ref[...]
        r = lax.rsqrt(jnp.mean(xf * xf, axis=-1, keepdims=True) + EPS)
        part = jnp.sum(dy_ref[...] * (xf * r), axis=0, keepdims=True)

        @pl.when(pl.program_id(0) == 0)
        def _():
            dg_ref[...] = part

        @pl.when(pl.program_id(0) > 0)
        def _():
            dg_ref[...] += part

    tile = pl.BlockSpec((ts, d), lambda i: (i, 0))
    return pl.pallas_call(
        body,
        name=name,
        grid=(s // ts,),
        in_specs=[tile, tile],
        out_specs=pl.BlockSpec((1, d), lambda i: (0, 0)),
        out_shape=jax.ShapeDtypeStruct((1, d), F32),
        compiler_params=_cp(("arbitrary",)),
    )(x, dy)


def _chunk_scan(v, row_in_chunk, suffix):
    t = v.shape[0]
    step = 1
    while step < GLA_CHUNK:
        if suffix:
            v = v + jnp.where(row_in_chunk < GLA_CHUNK - step, pltpu.roll(v, t - step, 0), 0.0)
        else:
            v = v + jnp.where(row_in_chunk >= step, pltpu.roll(v, step, 0), 0.0)
        step *= 2
    return v


def _gate_pre(lr, w_ref, b_ref):
    return _dot(lr, w_ref[...]) + b_ref[...]


def _gate_fwd(z, waf, wab, baf, bab, *, name):
    s = z.shape[0]
    ts = _rows(s, light=True)

    def body(lr_ref, waf_ref, wab_ref, baf_ref, bab_ref, bf_ref, bb_ref):
        lr = lr_ref[...]
        ric = lax.broadcasted_iota(jnp.int32, (ts, GLA_K_TOTAL), 0) & (GLA_CHUNK - 1)
        for w_ref, b_ref, o_ref, suffix in ((waf_ref, baf_ref, bf_ref, False), (wab_ref, bab_ref, bb_ref, True)):
            pre = _gate_pre(lr, w_ref, b_ref)
            la = (jnp.minimum(pre, 0.0) - jnp.log(1.0 + jnp.exp(-jnp.abs(pre)))) * GLA_GATE_SCALE
            o_ref[...] = _chunk_scan(la, ric, suffix)

    wspec = pl.BlockSpec((128, GLA_K_TOTAL), lambda i: (0, 0))
    bspec = pl.BlockSpec((1, GLA_K_TOTAL), lambda i: (0, 0))
    tile = pl.BlockSpec((ts, GLA_K_TOTAL), lambda i: (i, 0))
    return pl.pallas_call(
        body,
        name=name,
        grid=(s // ts,),
        in_specs=[pl.BlockSpec((ts, 128), lambda i: (i, LR_COL // 128)), wspec, wspec, bspec, bspec],
        out_specs=[tile, tile],
        out_shape=[jax.ShapeDtypeStruct((s, GLA_K_TOTAL), F32)] * 2,
        compiler_params=_cp(("parallel",)),
    )(z, waf, wab, baf, bab)


def _gate_bwd(z, waf, wab, baf, bab, dbf, dbb, dqkv_f, dqkv_b, *, name):
    s = z.shape[0]
    ts = _rows(s, light=True)

    def body(lr_ref, waf_ref, wab_ref, baf_ref, bab_ref, dbf_ref, dbb_ref, gf_ref, gb_ref, dzb_ref, dwf_ref, dwb_ref, dbaf_ref, dbab_ref):
        lr = lr_ref[...]
        ric = lax.broadcasted_iota(jnp.int32, (ts, GLA_K_TOTAL), 0) & (GLA_CHUNK - 1)
        first = pl.program_id(0) == 0
        dlr = None
        for w_ref, b_ref, db_ref, dw_ref, dbias_ref, suffix in (
            (waf_ref, baf_ref, dbf_ref, dwf_ref, dbaf_ref, True),
            (wab_ref, bab_ref, dbb_ref, dwb_ref, dbab_ref, False),
        ):
            pre = _gate_pre(lr, w_ref, b_ref)
            dla = _chunk_scan(db_ref[...], ric, suffix)
            dpre = dla * GLA_GATE_SCALE * _sigmoid(-pre)
            part = _dot_nt(dpre, w_ref[...])
            dlr = part if dlr is None else dlr + part
            dw = _dot_tn(lr, dpre)
            dbias = jnp.sum(dpre, axis=0, keepdims=True)

            @pl.when(first)
            def _():
                dw_ref[...] = dw
                dbias_ref[...] = dbias

            @pl.when(jnp.logical_not(first))
            def _():
                dw_ref[...] += dw
                dbias_ref[...] += dbias

        dzb_ref[...] = jnp.concatenate([gf_ref[...] + gb_ref[...], dlr], axis=1).astype(dzb_ref.dtype)

    wspec = pl.BlockSpec((128, GLA_K_TOTAL), lambda i: (0, 0))
    bspec = pl.BlockSpec((1, GLA_K_TOTAL), lambda i: (0, 0))
    tile = pl.BlockSpec((ts, GLA_K_TOTAL), lambda i: (i, 0))
    wide = pl.BlockSpec((ts, 2 * GLA_K_TOTAL + GLA_V_TOTAL), lambda i: (i, 0))
    return pl.pallas_call(
        body,
        name=name,
        grid=(s // ts,),
        in_specs=[pl.BlockSpec((ts, 128), lambda i: (i, LR_COL // 128)), wspec, wspec, bspec, bspec, tile, tile, wide, wide],
        out_specs=[pl.BlockSpec((ts, ZB_COLS), lambda i: (i, 0)), wspec, wspec, bspec, bspec],
        out_shape=[
            jax.ShapeDtypeStruct((s, ZB_COLS), _CD),
            jax.ShapeDtypeStruct((128, GLA_K_TOTAL), F32),
            jax.ShapeDtypeStruct((128, GLA_K_TOTAL), F32),
            jax.ShapeDtypeStruct((1, GLA_K_TOTAL), F32),
            jax.ShapeDtypeStruct((1, GLA_K_TOTAL), F32),
        ],
        compiler_params=_cp(("arbitrary",)),
    )(z, waf, wab, baf, bab, dbf, dbb, dqkv_f, dqkv_b)


def _gla_masks(rev):
    lane_head = lax.broadcasted_iota(jnp.int32, (1, GLA_K_TOTAL), 1) >> 6
    head_masks = [lane_head == h for h in range(GLA_HEADS)]
    t = lax.broadcasted_iota(jnp.int32, (GLA_HEADS * GLA_CHUNK, GLA_CHUNK), 0) & (GLA_CHUNK - 1)
    u = lax.broadcasted_iota(jnp.int32, (GLA_HEADS * GLA_CHUNK, GLA_CHUNK), 1)
    tri = (u > t) if rev else (u <= t)
    row = lax.broadcasted_iota(jnp.int32, (GLA_CHUNK, GLA_K_TOTAL), 0)
    total_row = row == (0 if rev else GLA_CHUNK - 1)
    return head_masks, tri, total_row


def _spread(a, head_masks):
    return jnp.concatenate([jnp.where(m, a, 0.0) for m in head_masks], axis=0)


def _stack(a):
    return jnp.concatenate([a[:, GLA_DV * h : GLA_DV * (h + 1)] for h in range(GLA_HEADS)], axis=0)


def _unstack(a):
    return jnp.concatenate([a[GLA_CHUNK * h : GLA_CHUNK * (h + 1)] for h in range(GLA_HEADS)], axis=1)


def _collect(a, head_masks):
    out = None
    for h, m in enumerate(head_masks):
        part = jnp.where(m, a[GLA_CHUNK * h : GLA_CHUNK * (h + 1)], 0.0)
        out = part if out is None else out + part
    return out


def _gla_chunk_terms(q_ref, k_ref, v_ref, b_ref, rows, head_masks, tri, total_row):
    q = q_ref[rows, :] * (GLA_DK**-0.5)
    k = k_ref[rows, :]
    v = v_ref[rows, :]
    b = b_ref[rows, :]
    eb = jnp.exp(b)
    enb = jnp.exp(-b)
    g = jnp.sum(jnp.where(total_row, b, 0.0), axis=0, keepdims=True)
    egb = jnp.exp(g - b)
    qt = q * eb
    kt = k * enb
    kh = k * egb
    q_heads = _spread(qt, head_masks)
    attn = jnp.where(tri, _dot_nt(q_heads, kt), 0.0)
    return v, eb, enb, egb, jnp.exp(g), qt, kt, kh, q_heads, attn


def _gla_specs(s, tb, rev_blocks):
    nb = s // tb
    rb = (lambda i: nb - 1 - i) if rev_blocks else (lambda i: i)
    q_spec = pl.BlockSpec((tb, GLA_K_TOTAL), lambda i: (rb(i), 0))
    k_spec = pl.BlockSpec((tb, GLA_K_TOTAL), lambda i: (rb(i), 1))
    v_spec = pl.BlockSpec((tb, GLA_V_TOTAL), lambda i: (rb(i), 1))
    b_spec = pl.BlockSpec((tb, GLA_K_TOTAL), lambda i: (rb(i), 0))
    o_spec = pl.BlockSpec((tb, GLA_V_TOTAL), lambda i: (rb(i), 0))
    st_spec = pl.BlockSpec((tb // GLA_CHUNK, GLA_DV, GLA_K_TOTAL), lambda i: (rb(i), 0, 0))
    return nb, q_spec, k_spec, v_spec, b_spec, o_spec, st_spec


def _gla_fwd_chunk(cidx, q_ref, k_ref, v_ref, b_ref, o_ref, sv_ref, st_ref, masks):
    head_masks, tri, total_row = masks
    rows = pl.ds(pl.multiple_of(cidx * GLA_CHUNK, GLA_CHUNK), GLA_CHUNK)
    v, _, _, _, eg, _, _, kh, q_heads, attn = _gla_chunk_terms(q_ref, k_ref, v_ref, b_ref, rows, head_masks, tri, total_row)
    o = jnp.concatenate(
        [_dot(attn[GLA_CHUNK * h : GLA_CHUNK * (h + 1)], v[:, GLA_DV * h : GLA_DV * (h + 1)]) for h in range(GLA_HEADS)], axis=1
    )
    st = st_ref[...]
    o_ref[rows, :] = o + _unstack(_dot_nt(q_heads, st))
    sv_ref[cidx] = st
    st_ref[...] = st * eg + _dot_tn(_stack(v), _spread(kh, head_masks))


def _gla_fwd(z, b_f, b_b, *, name):
    s = z.shape[0]
    tb = _rows(s)
    cpb = tb // GLA_CHUNK
    nb, qf, kf, vf, bf, of, sf = _gla_specs(s, tb, False)
    _, qr, kr, vr, br, orr, sr = _gla_specs(s, tb, True)

    def body(qf_ref, kf_ref, vf_ref, bf_ref, qr_ref, kr_ref, vr_ref, br_ref, of_ref, svf_ref, or_ref, svr_ref, stf_ref, str_ref):
        masks_f, masks_r = _gla_masks(False), _gla_masks(True)

        @pl.when(pl.program_id(0) == 0)
        def _():
            stf_ref[...] = jnp.zeros_like(stf_ref)
            str_ref[...] = jnp.zeros_like(str_ref)

        def chunk(ci, carry):
            _gla_fwd_chunk(ci, qf_ref, kf_ref, vf_ref, bf_ref, of_ref, svf_ref, stf_ref, masks_f)
            _gla_fwd_chunk(cpb - 1 - ci, qr_ref, kr_ref, vr_ref, br_ref, or_ref, svr_ref, str_ref, masks_r)
            return carry

        lax.fori_loop(0, cpb, chunk, 0)

    o_shape = jax.ShapeDtypeStruct((s, GLA_V_TOTAL), F32)
    st_shape = jax.ShapeDtypeStruct((s // GLA_CHUNK, GLA_DV, GLA_K_TOTAL), F32)
    return pl.pallas_call(
        body,
        name=name,
        grid=(nb,),
        in_specs=[qf, kf, vf, bf, qr, kr, vr, br],
        out_specs=[of, sf, orr, sr],
        out_shape=[o_shape, st_shape, o_shape, st_shape],
        scratch_shapes=[pltpu.VMEM((GLA_DV, GLA_K_TOTAL), F32)] * 2,
        compiler_params=_cp(("arbitrary",)),
    )(z, z, z, b_f, z, z, z, b_b)


def _gla_bwd_chunk(cidx, q_ref, k_ref, v_ref, b_ref, do_ref, sv_ref, dqkv_ref, db_ref, dst_ref, masks):
    head_masks, tri, total_row = masks
    rows = pl.ds(pl.multiple_of(cidx * GLA_CHUNK, GLA_CHUNK), GLA_CHUNK)
    v, eb, enb, egb, eg, qt, kt, kh, q_heads, attn = _gla_chunk_terms(q_ref, k_ref, v_ref, b_ref, rows, head_masks, tri, total_row)
    do_c = do_ref[rows, :]
    st = sv_ref[cidx]
    dst = dst_ref[...]
    do_s, v_s = _stack(do_c), _stack(v)
    hs = lambda a, h: a[GLA_CHUNK * h : GLA_CHUNK * (h + 1)]
    vs = lambda a, h: a[:, GLA_DV * h : GLA_DV * (h + 1)]
    dattn = jnp.concatenate([_dot_nt(vs(do_c, h), vs(v, h)) for h in range(GLA_HEADS)], axis=0)
    dattn = jnp.where(tri, dattn, 0.0)
    dv = jnp.concatenate([_dot_tn(hs(attn, h), vs(do_c, h)) for h in range(GLA_HEADS)], axis=1)
    dv = dv + _unstack(_dot_nt(_spread(kh, head_masks), dst))
    dqt = _collect(_dot(do_s, st) + _dot(dattn, kt), head_masks)
    dkt = _dot_tn(dattn, q_heads)
    dkh = _collect(_dot(v_s, dst), head_masks)
    dg = jnp.sum(dkh * kh, axis=0, keepdims=True) + jnp.sum(dst * st, axis=0, keepdims=True) * eg
    db = dqt * qt - dkt * kt - dkh * kh + jnp.where(total_row, dg, 0.0)
    dq = dqt * eb * (GLA_DK**-0.5)
    dk = dkt * enb + dkh * egb
    dqkv_ref[rows, :] = jnp.concatenate([dq, dk, dv], axis=1)
    db_ref[rows, :] = db
    dst_ref[...] = dst * eg + _dot_tn(do_s, q_heads)


def _gla_bwd(z, b_f, b_b, do, st_f, st_b, *, name):
    s = z.shape[0]
    tb = _rows(s)
    cpb = tb // GLA_CHUNK
    wide = 2 * GLA_K_TOTAL + GLA_V_TOTAL
    nb, qf, kf, vf, bf, of, sf = _gla_specs(s, tb, True)
    _, qr, kr, vr, br, orr, sr = _gla_specs(s, tb, False)
    gf = pl.BlockSpec((tb, wide), lambda i: (nb - 1 - i, 0))
    gr = pl.BlockSpec((tb, wide), lambda i: (i, 0))

    def body(qf_ref, kf_ref, vf_ref, bf_ref, dof_ref, svf_ref, qr_ref, kr_ref, vr_ref, br_ref, dor_ref, svr_ref,
             gf_ref, dbf_ref, gr_ref, dbr_ref, dstf_ref, dstr_ref):
        masks_f, masks_r = _gla_masks(False), _gla_masks(True)

        @pl.when(pl.program_id(0) == 0)
        def _():
            dstf_ref[...] = jnp.zeros_like(dstf_ref)
            dstr_ref[...] = jnp.zeros_like(dstr_ref)

        def chunk(ci, carry):
            _gla_bwd_chunk(cpb - 1 - ci, qf_ref, kf_ref, vf_ref, bf_ref, dof_ref, svf_ref, gf_ref, dbf_ref, dstf_ref, masks_f)
            _gla_bwd_chunk(ci, qr_ref, kr_ref, vr_ref, br_ref, dor_ref, svr_ref, gr_ref, dbr_ref, dstr_ref, masks_r)
            return carry

        lax.fori_loop(0, cpb, chunk, 0)

    g_shape = jax.ShapeDtypeStruct((s, wide), F32)
    db_shape = jax.ShapeDtypeStruct((s, GLA_K_TOTAL), F32)
    return pl.pallas_call(
        body,
        name=name,
        grid=(nb,),
        in_specs=[qf, kf, vf, bf, of, sf, qr, kr, vr, br, orr, sr],
        out_specs=[gf, bf, gr, br],
        out_shape=[g_shape, db_shape, g_shape, db_shape],
        scratch_shapes=[pltpu.VMEM((GLA_DV, GLA_K_TOTAL), F32)] * 2,
        compiler_params=_cp(("arbitrary",)),
    )(z, z, z, b_f, do, st_f, z, z, z, b_b, do, st_b)


HALO = 8


def _halo_specs(s, ts, width, col):
    last = s // HALO - 1
    per = ts // HALO
    prev = pl.BlockSpec((HALO, width), lambda i: (jnp.maximum(i * per - 1, 0), col))
    nxt = pl.BlockSpec((HALO, width), lambda i: (jnp.minimum((i + 1) * per, last), col))
    return prev, nxt


def _group_ones():
    group = jnp.arange(CONV_WIDTH, dtype=jnp.int32) // CONV_GROUP
    return (group[:, None] == group[None, :]).astype(BF16)


_ONES_SPEC = pl.BlockSpec((CONV_WIDTH, CONV_WIDTH), lambda i: (0, 0))


def _conv_terms(cc_ext, cu_ext, cw, valid):
    n = cc_ext.shape[0]
    hc = jnp.where(valid, cc_ext * cu_ext, 0.0)
    hc_prev = pltpu.roll(hc, 1, 0)
    hc_next = pltpu.roll(hc, n - 1, 0)
    conv = cw[0:1] * hc_prev + cw[1:2] * hc + cw[2:3] * hc_next
    return hc, hc_prev, hc_next, conv


def _ext(prev_ref, cur_ref, next_ref):
    return jnp.concatenate([prev_ref[...], cur_ref[...], next_ref[...]], axis=0)


def _valid_rows(ts, s):
    row = lax.broadcasted_iota(jnp.int32, (ts + 2 * HALO, 1), 0) + (pl.program_id(0) * ts - HALO)
    return (row >= 0) & (row < s)


def _head_norm(o, gn):
    out = []
    for h in range(GLA_HEADS):
        oh = o[:, GLA_DV * h : GLA_DV * (h + 1)]
        r = lax.rsqrt(jnp.mean(oh * oh, axis=-1, keepdims=True) + EPS)
        out.append((oh * r, r))
    return out


def _mix_fwd(z, o_f, o_b, conv_w, conv_norm, gla_norm, *, name):
    s = z.shape[0]
    ts = _rows(s, light=True)
    cprev, cnext = _halo_specs(s, ts, CONV_WIDTH, 1)
    uprev, unext = _halo_specs(s, ts, CONV_WIDTH, 2)

    def body(cb_ref, cc_ref, cu_ref, ccp_ref, ccn_ref, cup_ref, cun_ref, g_ref, of_ref, ob_ref, cw_ref, cn_ref, gn_ref, ones_ref, y_ref):
        valid = _valid_rows(ts, s)
        _, _, _, conv = _conv_terms(_ext(ccp_ref, cc_ref, ccn_ref), _ext(cup_ref, cu_ref, cun_ref), cw_ref[...], valid)
        yc = cb_ref[...] * conv[HALO : HALO + ts]
        ms = _dot_split(yc * yc, ones_ref[...]) * (1.0 / CONV_GROUP)
        y_conv = yc * lax.rsqrt(ms + EPS) * cn_ref[...]
        gate = g_ref[...]
        silu = gate * _sigmoid(gate)
        gn = gn_ref[...]
        y_gla = jnp.concatenate([oh * gn for oh, _ in _head_norm(of_ref[...] + ob_ref[...], gn)], axis=1) * silu
        y_ref[...] = jnp.concatenate([y_conv, y_gla], axis=1).astype(y_ref.dtype)

    col = lambda c, w=CONV_WIDTH: pl.BlockSpec((ts, w), lambda i: (i, c))
    return pl.pallas_call(
        body,
        name=name,
        grid=(s // ts,),
        in_specs=[col(0), col(1), col(2), cprev, cnext, uprev, unext, col(3), col(0), col(0),
                  pl.BlockSpec((CONV_K, CONV_WIDTH), lambda i: (0, 0)), pl.BlockSpec((1, CONV_WIDTH), lambda i: (0, 0)),
                  pl.BlockSpec((1, GLA_DV), lambda i: (0, 0)), _ONES_SPEC],
        out_specs=pl.BlockSpec((ts, D_MODEL), lambda i: (i, 0)),
        out_shape=jax.ShapeDtypeStruct((s, D_MODEL), _CD),
        compiler_params=_cp(("parallel",)),
    )(z, z, z, z, z, z, z, z, o_f, o_b, conv_w, conv_norm, gla_norm, _group_ones())


def _mix_bwd(z, o_f, o_b, dy, conv_w, conv_norm, gla_norm, *, name):
    s = z.shape[0]
    ts = _rows(s)
    halos = [_halo_specs(s, ts, CONV_WIDTH, c) for c in (0, 1, 2)]
    dprev, dnext = _halo_specs(s, ts, CONV_WIDTH, 0)

    def body(cb_ref, cc_ref, cu_ref, cbp_ref, cbn_ref, ccp_ref, ccn_ref, cup_ref, cun_ref, g_ref, of_ref, ob_ref,
             dyc_ref, dyg_ref, dyp_ref, dyn_ref, cw_ref, cn_ref, gn_ref, ones_ref, dza_ref, do_ref, dcw_ref, dcn_ref, dgn_ref):
        n = ts + 2 * HALO
        valid = _valid_rows(ts, s)
        cw = cw_ref[...]
        cn = cn_ref[...]
        ones = ones_ref[...]
        cb = _ext(cbp_ref, cb_ref, cbn_ref)
        cc = _ext(ccp_ref, cc_ref, ccn_ref)
        cu = _ext(cup_ref, cu_ref, cun_ref)
        dy = _ext(dyp_ref, dyc_ref, dyn_ref)
        hc, hc_prev, hc_next, conv = _conv_terms(cc, cu, cw, valid)
        yc = cb * conv
        r = lax.rsqrt(_dot_split(yc * yc, ones) * (1.0 / CONV_GROUP) + EPS)
        yh = yc * r
        dyh = dy * cn
        dyc = r * (dyh - yh * (_dot_split(dyh * yh, ones) * (1.0 / CONV_GROUP)))
        dconv = jnp.where(valid, dyc * cb, 0.0)
        dhc = cw[0:1] * pltpu.roll(dconv, n - 1, 0) + cw[1:2] * dconv + cw[2:3] * pltpu.roll(dconv, 1, 0)
        mid = lambda a: a[HALO : HALO + ts]
        dza_ref[:, 0 : 3 * CONV_WIDTH] = jnp.concatenate([mid(dyc * conv), mid(dhc * cu), mid(dhc * cc)], axis=1).astype(dza_ref.dtype)
        dconv_m = mid(dconv)
        colsum = lambda a: jnp.sum(a, axis=0, keepdims=True)
        dcw = jnp.concatenate([colsum(dconv_m * mid(hc_prev)), colsum(dconv_m * mid(hc)), colsum(dconv_m * mid(hc_next))], axis=0)
        dcn = colsum(mid(dy * yh))

        gate = g_ref[...]
        sg = _sigmoid(gate)
        silu = gate * sg
        gn = gn_ref[...]
        dyg = dyg_ref[...]
        don = dyg * silu
        heads = _head_norm(of_ref[...] + ob_ref[...], gn)
        on = jnp.concatenate([oh * gn for oh, _ in heads], axis=1)
        dza_ref[:, 3 * CONV_WIDTH : ZA_COLS] = (dyg * on * (sg * (1.0 + gate * (1.0 - sg)))).astype(dza_ref.dtype)
        dgn = jnp.zeros((1, GLA_DV), F32)
        dos = []
        for h, (oh, rh) in enumerate(heads):
            donh = don[:, GLA_DV * h : GLA_DV * (h + 1)]
            dgn = dgn + colsum(donh * oh)
            doh = donh * gn
            dos.append(rh * (doh - oh * jnp.mean(doh * oh, axis=-1, keepdims=True)))
        do_ref[...] = jnp.concatenate(dos, axis=1)

        first = pl.program_id(0) == 0

        @pl.when(first)
        def _():
            dcw_ref[...] = dcw
            dcn_ref[...] = dcn
            dgn_ref[...] = dgn

        @pl.when(jnp.logical_not(first))
        def _():
            dcw_ref[...] += dcw
            dcn_ref[...] += dcn
            dgn_ref[...] += dgn

    col = lambda c, w=CONV_WIDTH: pl.BlockSpec((ts, w), lambda i: (i, c))
    cw_spec = pl.BlockSpec((CONV_K, CONV_WIDTH), lambda i: (0, 0))
    cn_spec = pl.BlockSpec((1, CONV_WIDTH), lambda i: (0, 0))
    gn_spec = pl.BlockSpec((1, GLA_DV), lambda i: (0, 0))
    return pl.pallas_call(
        body,
        name=name,
        grid=(s // ts,),
        in_specs=[col(0), col(1), col(2), halos[0][0], halos[0][1], halos[1][0], halos[1][1], halos[2][0], halos[2][1],
                  col(3), col(0), col(0), col(0), col(1), dprev, dnext, cw_spec, cn_spec, gn_spec, _ONES_SPEC],
        out_specs=[pl.BlockSpec((ts, ZA_COLS), lambda i: (i, 0)), col(0), cw_spec, cn_spec, gn_spec],
        out_shape=[
            jax.ShapeDtypeStruct((s, ZA_COLS), _CD),
            jax.ShapeDtypeStruct((s, GLA_V_TOTAL), F32),
            jax.ShapeDtypeStruct((CONV_K, CONV_WIDTH), F32),
            jax.ShapeDtypeStruct((1, CONV_WIDTH), F32),
            jax.ShapeDtypeStruct((1, GLA_DV), F32),
        ],
        compiler_params=_cp(("arbitrary",)),
    )(z, z, z, z, z, z, z, z, z, z, o_f, o_b, dy, dy, dy, dy, conv_w, conv_norm, gla_norm, _group_ones())


def _xa_probs(q_ref, kv_ref, h):
    qh = q_ref[:, XA_HEAD_DIM * h : XA_HEAD_DIM * (h + 1)]
    kh = kv_ref[:, XA_HEAD_DIM * h : XA_HEAD_DIM * (h + 1)]
    vh = kv_ref[:, D_MODEL + XA_HEAD_DIM * h : D_MODEL + XA_HEAD_DIM * (h + 1)]
    sc = _dot_nt(qh, kh) * (XA_HEAD_DIM**-0.5)
    e = jnp.exp(sc - jnp.max(sc, axis=-1, keepdims=True))
    return qh, kh, vh, e / jnp.sum(e, axis=-1, keepdims=True)


def _xattn_fwd(qx, kv, *, name):
    s = qx.shape[0]
    ts = _rows(s, light=True)

    def body(q_ref, kv_ref, o_ref):
        outs = []
        for h in range(XA_HEADS):
            _, _, vh, p = _xa_probs(q_ref, kv_ref, h)
            outs.append(_dot(p, vh))
        o_ref[...] = jnp.concatenate(outs, axis=1).astype(o_ref.dtype)

    return pl.pallas_call(
        body,
        name=name,
        grid=(s // ts,),
        in_specs=[pl.BlockSpec((ts, D_MODEL), lambda i: (i, 0)), pl.BlockSpec((N_MEM, 2 * D_MODEL), lambda i: (0, 0))],
        out_specs=pl.BlockSpec((ts, D_MODEL), lambda i: (i, 0)),
        out_shape=jax.ShapeDtypeStruct((s, D_MODEL), _CD),
        compiler_params=_cp(("parallel",)),
    )(qx, kv)


def _xattn_bwd(qx, kv, dox, *, name):
    s = qx.shape[0]
    ts = _rows(s, light=True)

    def body(q_ref, kv_ref, do_ref, dq_ref, dkv_ref):
        dqs, dks, dvs = [], [], []
        for h in range(XA_HEADS):
            qh, kh, vh, p = _xa_probs(q_ref, kv_ref, h)
            doh = do_ref[:, XA_HEAD_DIM * h : XA_HEAD_DIM * (h + 1)]
            dp = _dot_nt(doh, vh)
            ds = p * (dp - jnp.sum(dp * p, axis=-1, keepdims=True)) * (XA_HEAD_DIM**-0.5)
            dqs.append(_dot(ds, kh))
            dks.append(_dot_tn(ds, qh))
            dvs.append(_dot_tn(p, doh))
        dq_ref[...] = jnp.concatenate(dqs, axis=1).astype(dq_ref.dtype)
        dkv = jnp.concatenate(dks + dvs, axis=1)

        @pl.when(pl.program_id(0) == 0)
        def _():
            dkv_ref[...] = dkv

        @pl.when(pl.program_id(0) > 0)
        def _():
            dkv_ref[...] += dkv

    tile = pl.BlockSpec((ts, D_MODEL), lambda i: (i, 0))
    kv_spec = pl.BlockSpec((N_MEM, 2 * D_MODEL), lambda i: (0, 0))
    return pl.pallas_call(
        body,
        name=name,
        grid=(s // ts,),
        in_specs=[tile, kv_spec, tile],
        out_specs=[tile, kv_spec],
        out_shape=[jax.ShapeDtypeStruct((s, D_MODEL), _CD), jax.ShapeDtypeStruct((N_MEM, 2 * D_MODEL), F32)],
        compiler_params=_cp(("arbitrary",)),
    )(qx, kv, dox)


def _adamw_math(w, g, m, v):
    m = ADAM_B1 * m + (1.0 - ADAM_B1) * g
    v = ADAM_B2 * v + (1.0 - ADAM_B2) * (g * g)
    m_hat = m / (1.0 - ADAM_B1**ADAM_STEP)
    v_hat = v / (1.0 - ADAM_B2**ADAM_STEP)
    delta = -ADAM_LR * (m_hat / (jnp.sqrt(v_hat) + ADAM_EPS) + ADAM_WD * w)
    return delta, m, v


def _adamw(w, m, v, shard_rows, off, *, transposed, name):
    r, c = w.shape
    by_columns = r % 256 != 0
    tr = 512 if (c if by_columns else r) % 512 == 0 and off % 512 == 0 else 256
    if by_columns:
        assert not transposed and off == 0
        g_spec = tile = pl.BlockSpec((r, tr), lambda i: (0, i))
    else:
        g_spec = pl.BlockSpec((c, tr), lambda i: (off // c, i)) if transposed else pl.BlockSpec((tr, c), lambda i: (off // tr + i, 0))
        tile = pl.BlockSpec((tr, c), lambda i: (i, 0))

    def body(w_ref, g_ref, m_ref, v_ref, go_ref, d_ref, nm_ref, nv_ref):
        g = g_ref[...].T if transposed else g_ref[...]
        go_ref[...] = g
        d_ref[...], nm_ref[...], nv_ref[...] = _adamw_math(w_ref[...], g, m_ref[...], v_ref[...])

    return pl.pallas_call(
        body,
        name=name,
        grid=((c if by_columns else r) // tr,),
        in_specs=[tile, g_spec, tile, tile],
        out_specs=[tile] * 4,
        out_shape=[jax.ShapeDtypeStruct((r, c), F32)] * 4,
        compiler_params=_cp(("parallel",)),
    )(w, shard_rows, m, v)


def _adamw_small(groups, *, name):
    n = len(groups)

    def body(*refs):
        ins, outs = refs[: 4 * n], refs[4 * n :]
        for i in range(n):
            w_ref, g_ref, m_ref, v_ref = ins[4 * i : 4 * i + 4]
            outs[3 * i][...], outs[3 * i + 1][...], outs[3 * i + 2][...] = _adamw_math(w_ref[...], g_ref[...], m_ref[...], v_ref[...])

    flat = [a for grp in groups for a in grp]
    vm = pl.BlockSpec(memory_space=pltpu.VMEM)
    res = pl.pallas_call(
        body,
        name=name,
        in_specs=[vm] * (4 * n),
        out_specs=[vm] * (3 * n),
        out_shape=[jax.ShapeDtypeStruct(grp[0].shape, F32) for grp in groups for _ in range(3)],
        compiler_params=_cp(),
    )(*flat)
    return [tuple(res[3 * i : 3 * i + 3]) for i in range(n)]


def _place():
    return lax.axis_index("x"), lax.axis_index("y"), lax.axis_index("c")


def _rel_chip(x, y, k):
    return (1 - x if k & 2 else x), (1 - y if k & 1 else y)


def _half(c, rh):
    return pl.ds(pl.multiple_of(c * rh, 16), rh)


HBM = pl.BlockSpec(memory_space=pltpu.HBM)
SEM = pl.BlockSpec(memory_space=pltpu.SEMAPHORE)
EFFECT = pltpu.SideEffectType.DATAFLOW_SIDE_EFFECTING


def _in_hbm(a):
    return pltpu.with_memory_space_constraint(a, pltpu.HBM)


def _gather_copies(p_ref, land_ref, send_sems, recv_sems):
    rh = p_ref.shape[0] // 2
    x, y, c = _place()
    rows = _half(c, rh)
    copies = []
    for k in range(1, N_CHIPS):
        cx, cy = _rel_chip(x, y, k)
        copies.append(pltpu.make_async_remote_copy(
            src_ref=p_ref.at[rows], dst_ref=land_ref.at[2 * x + y, rows], send_sem=send_sems.at[k - 1], recv_sem=recv_sems.at[k - 1],
            device_id=(cx, cy, c), device_id_type=MESH))
    copies.append(pltpu.make_async_remote_copy(
        src_ref=p_ref, dst_ref=land_ref.at[2 * x + y], send_sem=send_sems.at[N_CHIPS - 1], recv_sem=recv_sems.at[N_CHIPS - 1],
        device_id=(x, y, 1 - c), device_id_type=MESH))
    return copies


def _gather_start(pack, after, *, name):
    r, w = pack.shape

    def body(p_ref, land_ref, after_ref, send_sems, recv_sems, p_thru, land_thru, token):
        for cp in _gather_copies(p_ref, land_ref, send_sems, recv_sems):
            cp.start()
        token[...] = jnp.zeros_like(token)

    return pl.pallas_call(
        body,
        name=name,
        out_shape=(pltpu.SemaphoreType.DMA((N_CHIPS,)), pltpu.SemaphoreType.DMA((N_CHIPS,)), pltpu.HBM((r, w), pack.dtype),
                   pltpu.HBM((N_CHIPS, r, w), pack.dtype), jax.ShapeDtypeStruct((8, 128), F32)),
        in_specs=(HBM, HBM, ANY),
        out_specs=(SEM, SEM, HBM, HBM, pl.BlockSpec(memory_space=pltpu.VMEM)),
        input_output_aliases={0: 2, 1: 3},
        compiler_params=pltpu.CompilerParams(has_side_effects=EFFECT),
    )(_in_hbm(pack), _in_hbm(lax.empty((N_CHIPS, r, w), pack.dtype)), after)


def _gather_wait(send_sems, recv_sems, pack, land, after, *, name):
    def body(p_ref, land_ref, send_sems, recv_sems, after_ref, p_out, land_out):
        for cp in _gather_copies(p_ref, land_ref, send_sems, recv_sems):
            cp.wait_send()
            cp.wait_recv()

    return pl.pallas_call(
        body,
        name=name,
        out_shape=(pltpu.HBM(pack.shape, pack.dtype), pltpu.HBM(land.shape, land.dtype)),
        in_specs=(HBM, HBM, SEM, SEM, ANY),
        out_specs=(HBM, HBM),
        input_output_aliases={0: 0, 1: 1},
        compiler_params=pltpu.CompilerParams(has_side_effects=EFFECT),
    )(pack, land, send_sems, recv_sems, after)


def _gather_spread(land, *, name):
    n, r, w = land.shape
    rh = r // 2

    def body(land_ref, o_ref, send_sems, recv_sems):
        x, y, c = _place()
        rows = _half(c, rh)
        copies = []
        for k in range(1, N_CHIPS):
            cx, cy = _rel_chip(x, y, k)
            copies.append(pltpu.make_async_remote_copy(
                src_ref=land_ref.at[2 * cx + cy, rows], dst_ref=o_ref.at[2 * cx + cy, rows], send_sem=send_sems.at[k - 1],
                recv_sem=recv_sems.at[k - 1], device_id=(x, y, 1 - c), device_id_type=MESH))
        for cp in copies:
            cp.start()
        for cp in copies:
            cp.wait()

    return pl.pallas_call(
        body,
        name=name,
        in_specs=[ANY],
        out_specs=ANY,
        out_shape=jax.ShapeDtypeStruct(land.shape, land.dtype),
        input_output_aliases={0: 0},
        scratch_shapes=[pltpu.SemaphoreType.DMA((N_CHIPS - 1,)), pltpu.SemaphoreType.DMA((N_CHIPS - 1,))],
        compiler_params=pltpu.CompilerParams(has_side_effects=True),
    )(land)


N_PARTS = 2 * (N_CHIPS - 1)


def _scatter_copies(lo_ref, g_ref, land_lo_ref, land_f_ref, send_sems, recv_sems, starting):
    rh = g_ref.shape[1] // 2
    x, y, c = _place()
    copies = []
    for k in range(1, N_CHIPS):
        cx, cy = _rel_chip(x, y, k)
        for i in range(2):
            part = 2 * (k - 1) + (c if starting else i)
            copies.append(pltpu.make_async_remote_copy(
                src_ref=lo_ref.at[2 * cx + cy, pl.ds(i * rh, rh)], dst_ref=land_lo_ref.at[part],
                send_sem=send_sems.at[2 * (k - 1) + i], recv_sem=recv_sems.at[part], device_id=(cx, cy, i), device_id_type=MESH))
    copies.append(pltpu.make_async_remote_copy(
        src_ref=g_ref.at[2 * x + y, _half(1 - c, rh)], dst_ref=land_f_ref, send_sem=send_sems.at[N_PARTS], recv_sem=recv_sems.at[N_PARTS],
        device_id=(x, y, 1 - c), device_id_type=MESH))
    return copies


def _scatter_start(g_lo, g, *, name):
    n, r, w = g.shape
    rh = r // 2

    def body(lo_ref, g_ref, land_lo_ref, land_f_ref, send_sems, recv_sems, lo_thru, g_thru, land_lo_thru, land_f_thru, token):
        for cp in _scatter_copies(lo_ref, g_ref, land_lo_ref, land_f_ref, send_sems, recv_sems, True):
            cp.start()
        token[...] = jnp.zeros_like(token)

    return pl.pallas_call(
        body,
        name=name,
        out_shape=(pltpu.SemaphoreType.DMA((N_PARTS + 1,)), pltpu.SemaphoreType.DMA((N_PARTS + 1,)), pltpu.HBM(g_lo.shape, g_lo.dtype),
                   pltpu.HBM(g.shape, g.dtype), pltpu.HBM((N_PARTS, rh, w), g_lo.dtype), pltpu.HBM((rh, w), g.dtype),
                   jax.ShapeDtypeStruct((8, 128), F32)),
        in_specs=(HBM, HBM, HBM, HBM),
        out_specs=(SEM, SEM, HBM, HBM, HBM, HBM, pl.BlockSpec(memory_space=pltpu.VMEM)),
        input_output_aliases={0: 2, 1: 3, 2: 4, 3: 5},
        compiler_params=pltpu.CompilerParams(has_side_effects=EFFECT),
    )(_in_hbm(g_lo), _in_hbm(g), _in_hbm(lax.empty((N_PARTS, rh, w), g_lo.dtype)), _in_hbm(lax.empty((rh, w), g.dtype)))


def _scatter_wait(send_sems, recv_sems, g_lo, g, land_lo, land_f, after, *, name):
    def body(lo_ref, g_ref, land_lo_ref, land_f_ref, send_sems, recv_sems, after_ref, o0, o1, o2, o3):
        for cp in _scatter_copies(lo_ref, g_ref, land_lo_ref, land_f_ref, send_sems, recv_sems, False):
            cp.wait_send()
            cp.wait_recv()

    arrays = (g_lo, g, land_lo, land_f)
    return pl.pallas_call(
        body,
        name=name,
        out_shape=tuple(pltpu.HBM(a.shape, a.dtype) for a in arrays),
        in_specs=(HBM, HBM, HBM, HBM, SEM, SEM, ANY),
        out_specs=(HBM, HBM, HBM, HBM),
        input_output_aliases={0: 0, 1: 1, 2: 2, 3: 3},
        compiler_params=pltpu.CompilerParams(has_side_effects=EFFECT),
    )(*arrays, send_sems, recv_sems, after)


def _scatter_sum(g, land_lo, land_f, where, *, name):
    n, r, w = g.shape
    rh = r // 2
    tr = _pick(rh, (256, 160, 80))
    nt = rh // tr

    def body(where_ref, g_ref, f_ref, lo_ref, o_ref):
        acc = g_ref[0] + f_ref[...]
        for part in range(N_PARTS):
            acc = acc + lo_ref[part].astype(F32)
        o_ref[...] = acc

    return pl.pallas_call(
        body,
        name=name,
        grid_spec=pltpu.PrefetchScalarGridSpec(
            num_scalar_prefetch=1,
            grid=(nt,),
            in_specs=[pl.BlockSpec((1, tr, w), lambda i, wh: (wh[1], wh[0] * nt + i, 0)),
                      pl.BlockSpec((tr, w), lambda i, wh: (i, 0)),
                      pl.BlockSpec((N_PARTS, tr, w), lambda i, wh: (0, i, 0))],
            out_specs=pl.BlockSpec((tr, w), lambda i, wh: (wh[0] * nt + i, 0)),
        ),
        out_shape=jax.ShapeDtypeStruct((r, w), F32),
        compiler_params=_cp(("parallel",)),
    )(where, g, land_f, land_lo)


def _swap_all(shards, *, name):
    n = len(shards)

    def body(*refs):
        ins, outs = refs[:n], refs[n : 2 * n]
        send_sems, recv_sems = refs[2 * n :]
        x, y, c = _place()
        copies = []
        for i, (e_ref, o_ref) in enumerate(zip(ins, outs)):
            rows = _half(c, e_ref.shape[0] // 2)
            copies.append(pltpu.make_async_remote_copy(src_ref=e_ref.at[rows], dst_ref=o_ref.at[rows], send_sem=send_sems.at[i],
                                                       recv_sem=recv_sems.at[i], device_id=(x, y, 1 - c), device_id_type=MESH))
        for cp in copies:
            cp.start()
        for cp in copies:
            cp.wait()

    return pl.pallas_call(
        body,
        name=name,
        in_specs=[ANY] * n,
        out_specs=[ANY] * n,
        out_shape=[jax.ShapeDtypeStruct(e.shape, e.dtype) for e in shards],
        input_output_aliases={i: i for i in range(n)},
        scratch_shapes=[pltpu.SemaphoreType.DMA((n,)), pltpu.SemaphoreType.DMA((n,))],
        compiler_params=pltpu.CompilerParams(has_side_effects=True),
    )(*shards)


def _sum_small(small, after):
    n_dev = 8

    def body(s_ref, after_ref, o_ref, all_ref, send_sems, recv_sems):
        x, y, c = _place()
        me = 4 * x + 2 * y + c
        all_ref[me] = s_ref[...]
        copies = []
        for k in range(1, n_dev):
            cx, cy = _rel_chip(x, y, k >> 1)
            cc = 1 - c if k & 1 else c
            copies.append(pltpu.make_async_remote_copy(
                src_ref=s_ref, dst_ref=all_ref.at[me], send_sem=send_sems.at[k - 1], recv_sem=recv_sems.at[k - 1],
                device_id=(cx, cy, cc), device_id_type=MESH))
        for cp in copies:
            cp.start()
        for cp in copies:
            cp.wait()
        acc = all_ref[0]
        for a in range(1, n_dev):
            acc = acc + all_ref[a]
        o_ref[...] = acc

    vm = pl.BlockSpec(memory_space=pltpu.VMEM)
    return pl.pallas_call(
        body,
        name="sum_small",
        in_specs=[vm, ANY],
        out_specs=vm,
        out_shape=jax.ShapeDtypeStruct(small.shape, F32),
        scratch_shapes=[pltpu.VMEM((n_dev,) + small.shape, F32), pltpu.SemaphoreType.DMA((n_dev - 1,)), pltpu.SemaphoreType.DMA((n_dev - 1,))],
        compiler_params=pltpu.CompilerParams(has_side_effects=True),
    )(small, after)


MATS = {"w_in": (776, True), "w_out": (256, False), "w_xq": (256, False), "w_xkv": (512, True), "w_xo": (256, False),
        "w_up": (1024, True), "w_down": (1024, False)}
GATHER_FIRST = ("w_in",)
GATHER_REST = ("w_out", "w_xq", "w_xkv", "w_xo", "w_up", "w_down")
GRAD_GROUPS = (("w_up", "w_down"), ("w_out", "w_xq", "w_xkv", "w_xo"), ("w_in",))


def _group_rows(names):
    n = sum(MATS[name][0] for name in names)
    return n + (-n) % 32


def _pack(pieces, rows):
    p = jnp.concatenate(pieces, axis=0) if len(pieces) > 1 else pieces[0]
    return jnp.pad(p, ((0, rows - p.shape[0]), (0, 0))) if rows > p.shape[0] else p


SMALL = (
    ("mix_norm", 1024), ("conv_norm", 512), ("b_af", 256), ("b_ab", 256), ("gla_norm", 128), ("xa_norm", 1024), ("mem_norm", 1024),
    ("mlp_norm", 1024), ("final_norm", 1024), ("conv_w", 1536), ("w_af", 4096), ("w_ab", 4096), ("loss", 128),
)


def kernel(x, mem, mix_norm, w_in, conv_w, conv_norm, w_af, b_af, w_ab, b_ab, gla_norm, w_out, xa_norm, mem_norm, w_xq, w_xkv, w_xo, mlp_norm, w_up, w_down, final_norm, loss_target, m_mix_norm, m_w_in, m_conv_w, m_conv_norm, m_w_af, m_b_af, m_w_ab, m_b_ab, m_gla_norm, m_w_out, m_xa_norm, m_mem_norm, m_w_xq, m_w_xkv, m_w_xo, m_mlp_norm, m_w_up, m_w_down, m_final_norm, v_mix_norm, v_w_in, v_conv_w, v_conv_norm, v_w_af, v_b_af, v_w_ab, v_b_ab, v_gla_norm, v_w_out, v_xa_norm, v_mem_norm, v_w_xq, v_w_xkv, v_w_xo, v_mlp_norm, v_w_up, v_w_down, v_final_norm):
    given = dict(locals())
    xi, yi, ci = _place()
    chip = 2 * xi + yi
    where = jnp.stack([ci, chip]).astype(jnp.int32)

    lo = {name: (given[name][0].T if MATS[name][1] else given[name][0]).astype(_CD) for name in MATS}
    pack_rest = _pack([lo[name] for name in GATHER_REST], _group_rows(GATHER_REST))
    small_shards = (conv_w[0], w_af[0], w_ab[0])
    small_flat = jnp.concatenate([a.reshape(-1) for a in small_shards])
    small_flat = jnp.pad(small_flat, (0, SMALL_W_WORDS - small_flat.shape[0]))
    small_bits = lax.bitcast_convert_type(small_flat, _CD).reshape(-1, D_MODEL)
    pack_first = _pack([lo[name] for name in GATHER_FIRST] + [small_bits], _group_rows(GATHER_FIRST))
    xs, mems, tgt = x[0], mem[0], loss_target[0]
    behind = lambda gain, token: gain + token[0, 0]

    first_send, first_recv, pack_first, land_first, first_token = _gather_start(pack_first, mix_norm, name="gather_first_start")
    rest_send, rest_recv, pack_rest, land_rest, rest_token = _gather_start(pack_rest, first_token, name="gather_rest_start")
    h1 = _rms_fwd(xs, behind(mix_norm, rest_token), name="norm_mix")
    pack_first, land_first = _gather_wait(first_send, first_recv, pack_first, land_first, h1, name="gather_first_wait")
    got_first = _gather_spread(land_first, name="gather_first_spread")

    def whole(got, off, rows):
        return got[:, off : off + rows].reshape(N_CHIPS * rows, D_MODEL)

    w_in_t = whole(got_first, 0, MATS["w_in"][0])
    w_za = jnp.concatenate([w_in_t[0:1536], w_in_t[2560:3072]], axis=0)
    w_zb = jnp.concatenate([w_in_t[1536:2560], w_in_t[3072:W_IN_COLS], jnp.zeros((ZB_COLS - 1056, D_MODEL), _CD)], axis=0)
    first_rows = sum(MATS[name][0] for name in GATHER_FIRST)
    got_bits = got_first[:, first_rows : first_rows + small_bits.shape[0]].reshape(N_CHIPS, SMALL_W_WORDS, -1)
    small_all = lax.bitcast_convert_type(got_bits if got_bits.shape[-1] > 1 else got_bits[..., 0], F32)
    small_full, off = [], 0
    for a in small_shards:
        part = small_all[:, off : off + a.size].reshape((N_CHIPS,) + a.shape)
        small_full.append(jnp.concatenate([part[c] for c in range(N_CHIPS)], axis=1))
        off += a.size
    conv_w_full, w_af_full, w_ab_full = small_full
    waf_p = jnp.pad(w_af_full, ((0, 128 - GLA_LOWRANK), (0, 0))).astype(_CD)
    wab_p = jnp.pad(w_ab_full, ((GLA_LOWRANK, 128 - 2 * GLA_LOWRANK), (0, 0))).astype(_CD)

    z_b = _mm(h1, w_zb, mode="nt", name="proj_in_b", tn=ZB_COLS)
    z_a = _mm(h1, w_za, mode="nt", name="proj_in_a", tm=512, tn=ZA_COLS)
    b_f, b_b = _gate_fwd(z_b, waf_p, wab_p, b_af, b_ab, name="gates")
    o_f, st_f, o_b, st_b = _gla_fwd(z_b, b_f, b_b, name="gla_scan")
    y = _mix_fwd(z_a, o_f, o_b, conv_w_full, conv_norm, gla_norm, name="mix_out")
    pack_rest, land_rest = _gather_wait(rest_send, rest_recv, pack_rest, land_rest, y, name="gather_rest_wait")
    gathered = _gather_spread(land_rest, name="gather_rest_spread")
    wt, off = {}, 0
    for name in GATHER_REST:
        wt[name] = whole(gathered, off, MATS[name][0])
        off += MATS[name][0]
    x1, hx = _mm_rows(y, wt["w_out"], mode="nn", name="proj_out", rows=(xs,), vecs=(xa_norm,), out_rows=(F32, _CD),
                      epilogue=_ep_residual_norm, tm=1024)
    qx = _mm(hx, wt["w_xq"], mode="nn", name="proj_xq", out_dtypes=(_CD,))
    hmem = _rms_fwd(mems, mem_norm, name="norm_mem")
    kv = _mm(hmem, wt["w_xkv"], mode="nt", name="proj_xkv", out_dtypes=(_CD,))
    ox = _xattn_fwd(qx, kv, name="xattn")
    x2, hm = _mm_rows(ox, wt["w_xo"], mode="nn", name="proj_xo", rows=(x1,), vecs=(mlp_norm,), out_rows=(F32, _CD),
                      epilogue=_ep_residual_norm, tm=1024)
    act, relu_u = _mm(hm, wt["w_up"], mode="nt", name="mlp_up", out_dtypes=(_CD, _CD), tm=2048,
                      epilogue=lambda acc: (jnp.square(jnp.maximum(acc, 0.0)), jnp.maximum(acc, 0.0)))
    dx3, dx3_lo, loss_part, g_final_norm = _mm_rows(
        act, wt["w_down"], mode="nn", name="mlp_down", rows=(x2, tgt), vecs=(final_norm.reshape(1, D_MODEL),),
        out_rows=(F32, _CD), out_vecs=(128, D_MODEL), epilogue=_ep_loss)

    grads_t = {}

    def start_group(names, tag):
        rows = _group_rows(names)
        g = jnp.stack([_pack([grads_t[name][a * MATS[name][0] : (a + 1) * MATS[name][0]] for name in names], rows) for a in range(N_CHIPS)])
        return _scatter_start(g.astype(_TD), g, name="grads_" + tag + "_start")

    def finish_group(state, after, tag):
        send_sems, recv_sems, g_lo, g, land_lo, land_f, _ = state
        g_lo, g, land_lo, land_f = _scatter_wait(send_sems, recv_sems, g_lo, g, land_lo, land_f, after, name="grads_" + tag + "_wait")
        return _scatter_sum(g, land_lo, land_f, where, name="grads_" + tag + "_sum")

    def new_packs(names):
        shape = (N_CHIPS, _group_rows(names), D_MODEL)
        return lax.empty(shape, F32), lax.empty(shape, _TD)

    def grad_into(packs, names, which, a, b, name):
        off = sum(MATS[other][0] for other in names[: names.index(which)])
        return _mm_tn_into(a, b, packs, rows=MATS[which][0], off=off, name=name)

    du = _mm(dx3_lo, wt["w_down"], mode="nt", name="mlp_down_dx", out_dtypes=(_CD,), extras=(relu_u,), tm=2048,
             epilogue=lambda acc, rr: (acc * (2.0 * rr.astype(F32)),))
    packs = new_packs(GRAD_GROUPS[0])
    packs = grad_into(packs, GRAD_GROUPS[0], "w_down", act, dx3_lo, "mlp_down_dw")
    packs = grad_into(packs, GRAD_GROUPS[0], "w_up", du, hm, "mlp_up_dw")
    mlp_state = _scatter_start(packs[1], packs[0], name="grads_mlp_start")
    dx2, dx2_lo, g_mlp_norm = _mm_rows(
        du, wt["w_up"], mode="nn", name="mlp_up_dx", rows=(x2, dx3), vecs=(behind(mlp_norm, mlp_state[-1]),),
        out_rows=(F32, _CD), out_vecs=(D_MODEL,), epilogue=_ep_norm_bwd)
    dox = _mm(dx2_lo, wt["w_xo"], mode="nt", name="proj_xo_dx", out_dtypes=(_CD,))
    packs = new_packs(GRAD_GROUPS[1])
    packs = grad_into(packs, GRAD_GROUPS[1], "w_xo", ox, dx2_lo, "proj_xo_dw")
    dqx, dkv = _xattn_bwd(qx, kv, dox, name="xattn_bwd")
    packs = grad_into(packs, GRAD_GROUPS[1], "w_xq", hx, dqx, "proj_xq_dw")
    dx1, dx1_lo, g_xa_norm = _mm_rows(
        dqx, wt["w_xq"], mode="nt", name="proj_xq_dx", rows=(x1, dx2), vecs=(xa_norm,),
        out_rows=(F32, _CD), out_vecs=(D_MODEL,), epilogue=_ep_norm_bwd, tm=1024)
    dkv_lo = dkv.astype(_CD)
    packs = grad_into(packs, GRAD_GROUPS[1], "w_xkv", dkv_lo, hmem, "proj_xkv_dw")
    dhmem = _mm(dkv_lo, wt["w_xkv"], mode="nn", name="proj_xkv_dx")
    g_mem_norm = _rms_gain_grad(mems, dhmem, name="norm_mem_bwd")
    dy = _mm(dx1_lo, wt["w_out"], mode="nt", name="proj_out_dx")
    packs = grad_into(packs, GRAD_GROUPS[1], "w_out", y, dx1_lo, "proj_out_dw")
    attn_state = _scatter_start(packs[1], packs[0], name="grads_attn_start")
    dz_a, do, g_conv_w, g_conv_norm, g_gla_norm = _mix_bwd(z_a, o_f, o_b, dy, conv_w_full, behind(conv_norm, attn_state[-1]), gla_norm, name="mix_out_bwd")
    dqkv_f, db_f, dqkv_b, db_b = _gla_bwd(z_b, b_f, b_b, do, st_f, st_b, name="gla_scan_bwd")
    dz_b, g_waf_p, g_wab_p, g_b_af, g_b_ab = _gate_bwd(z_b, waf_p, wab_p, b_af, b_ab, db_f, db_b, dqkv_f, dqkv_b, name="gates_bwd")
    g_za = _mm_tn(dz_a, h1, name="proj_in_a_dw")
    g_zb = _mm_tn(dz_b, h1, name="proj_in_b_dw")
    grads_t["w_in"] = jnp.concatenate([g_za[0:1536], g_zb[0:1024], g_za[1536:2048], g_zb[1024:1056]], axis=0)
    in_state = start_group(GRAD_GROUPS[2], "in")
    dh1_a = _mm(dz_a, w_za, mode="nn", name="proj_in_a_dx", tm=512, tk=ZA_COLS)
    grad_x, g_mix_norm = _mm_rows(
        dz_b, w_zb, mode="nn", name="proj_in_b_dx", rows=(xs, dx1, dh1_a), vecs=(behind(mix_norm, in_state[-1]),),
        out_rows=(F32,), out_vecs=(D_MODEL,), epilogue=_ep_norm_bwd)

    half_mlp = finish_group(mlp_state, grad_x, "mlp")
    half_attn = finish_group(attn_state, half_mlp, "attn")
    half_in = finish_group(in_state, half_attn, "in")
    shard_rows = {}
    for names, rows in zip(GRAD_GROUPS, _swap_all([half_mlp, half_attn, half_in], name="shards_to_sibling")):
        off = 0
        for name in names:
            shard_rows[name] = (rows, off)
            off += MATS[name][0]

    small_vals = dict(mix_norm=g_mix_norm, conv_norm=g_conv_norm, b_af=g_b_af, b_ab=g_b_ab, gla_norm=g_gla_norm, xa_norm=g_xa_norm,
                      mem_norm=g_mem_norm, mlp_norm=g_mlp_norm, final_norm=g_final_norm, conv_w=g_conv_w,
                      w_af=g_waf_p[0:GLA_LOWRANK], w_ab=g_wab_p[GLA_LOWRANK : 2 * GLA_LOWRANK], loss=loss_part)
    small = jnp.concatenate([small_vals[name].reshape(-1, 128) for name, _ in SMALL], axis=0)
    small = _sum_small(jnp.pad(small, ((0, SMALL_ROWS - small.shape[0]), (0, 0))), loss_part)
    g_small, off = {}, 0
    for name, n in SMALL:
        g_small[name] = small[off : off + n // 128]
        off += n // 128
    loss = g_small["loss"][0, 0]
    g_small["conv_w"] = lax.dynamic_slice(g_small["conv_w"].reshape(CONV_K, CONV_WIDTH), (0, 128 * chip), (CONV_K, 128))
    g_small["w_af"] = lax.dynamic_slice(g_small["w_af"].reshape(GLA_LOWRANK, GLA_K_TOTAL), (0, 64 * chip), (GLA_LOWRANK, 64))
    g_small["w_ab"] = lax.dynamic_slice(g_small["w_ab"].reshape(GLA_LOWRANK, GLA_K_TOTAL), (0, 64 * chip), (GLA_LOWRANK, 64))

    names = ["mix_norm", "w_in", "conv_w", "conv_norm", "w_af", "b_af", "w_ab", "b_ab", "gla_norm", "w_out", "xa_norm", "mem_norm",
             "w_xq", "w_xkv", "w_xo", "mlp_norm", "w_up", "w_down", "final_norm"]
    big_names = list(MATS)
    as2d = lambda a: a.reshape(1, -1) if a.ndim == 1 else a.reshape(a.shape[-2:])
    grads, deltas, new_m, new_v = {}, {}, {}, {}
    for name in big_names:
        rows, off = shard_rows[name]
        wmv = [as2d(given[name]), as2d(given["m_" + name]), as2d(given["v_" + name])]
        as_stored = name == "w_in"
        if as_stored:
            wmv = [a.T for a in wmv]
        res = _adamw(*wmv, rows, off, transposed=MATS[name][1] and not as_stored, name="adamw_" + name)
        grads[name], deltas[name], new_m[name], new_v[name] = [a.T for a in res] if as_stored else res
    small_names = [name for name in names if name not in big_names]
    groups = []
    for name in small_names:
        grads[name] = g_small[name].reshape(as2d(given[name]).shape)
        groups.append((as2d(given[name]), grads[name], as2d(given["m_" + name]), as2d(given["v_" + name])))
    for name, res in zip(small_names, _adamw_small(groups, name="adamw_small")):
        deltas[name], new_m[name], new_v[name] = res

    like = lambda name, a: a.reshape(given[name].shape)
    return (loss, grad_x[None], *[like(n, grads[n]) for n in names], *[like(n, deltas[n]) for n in names],
            *[like(n, new_m[n]) for n in names], *[like(n, new_v[n]) for n in names])
```

```python
import jax
import jax.numpy as jnp
from jax import lax
from jax.experimental import pallas as pl
from jax.experimental.pallas import tpu as pltpu

F32 = jnp.float32
BF16 = jnp.bfloat16
_CD = jnp.bfloat16
_TD = jnp.bfloat16

D_MODEL = 1024
N_MEM = 256
CONV_WIDTH = 512
CONV_GROUP = 64
CONV_K = 3
GLA_HEADS = 4
GLA_DK = 64
GLA_DV = 128
GLA_K_TOTAL = 256
GLA_V_TOTAL = 512
GLA_LOWRANK = 16
GLA_GATE_SCALE = 1.0 / 16.0
GLA_CHUNK = 64
XA_HEADS = 4
XA_HEAD_DIM = 256
D_FF = 4096
EPS = 1e-6
W_IN_COLS = 3104
ZA_COLS = 2048
ZB_COLS = 1152
LR_COL = 1024

ADAM_LR = 0.001
ADAM_B1 = 0.9
ADAM_B2 = 0.999
ADAM_EPS = 1e-08
ADAM_WD = 0.01
ADAM_STEP = 10

N_CHIPS = 4
SMALL_ROWS = 128

_TS = 512
_VMEM = 44 * 1024 * 1024
MESH = pl.DeviceIdType.MESH
ANY = pl.BlockSpec(memory_space=pl.ANY)


def _cp(sem=None, **kw):
    return pltpu.CompilerParams(dimension_semantics=sem, vmem_limit_bytes=_VMEM, **kw)


def _dot(a, b):
    return jnp.dot(a.astype(_CD), b.astype(_CD), preferred_element_type=F32)


def _dot_nt(a, b):
    return lax.dot_general(a.astype(_CD), b.astype(_CD), (((1,), (1,)), ((), ())), preferred_element_type=F32)


def _dot_tn(a, b):
    return lax.dot_general(a.astype(_CD), b.astype(_CD), (((0,), (0,)), ((), ())), preferred_element_type=F32)


def _dot_split(x, ones):
    hi = x.astype(BF16)
    r = x - hi.astype(F32)
    mid = r.astype(BF16)
    lo = (r - mid.astype(F32)).astype(BF16)
    d = lambda p: jnp.dot(p, ones, preferred_element_type=F32)
    return d(hi) + d(mid) + d(lo)


def _pick(n, cands=(1024, 640, 512, 256, 128)):
    for t in cands:
        if n % t == 0:
            return t
    return n


def _rows(s, light=False, times=2):
    return min(times * _TS if light else _TS, s)


def _sigmoid(v):
    e = jnp.exp(-jnp.abs(v))
    return jnp.where(v >= 0, 1.0 / (1.0 + e), e / (1.0 + e))


def _mm(a, b, *, mode, name, out_dtypes=(F32,), extras=(), epilogue=None, tm=None, tn=None, tk=None):
    m, k = a.shape
    n = b.shape[1] if mode == "nn" else b.shape[0]
    tm = min(m, tm or 1024)
    tn = tn or _pick(n)
    tk = tk or _pick(k)
    nk = k // tk
    n_ex, n_out = len(extras), len(out_dtypes)

    def body(*refs):
        a_ref, b_ref = refs[:2]
        ex = refs[2 : 2 + n_ex]
        outs = refs[2 + n_ex : 2 + n_ex + n_out]
        part = _dot(a_ref[...], b_ref[...]) if mode == "nn" else _dot_nt(a_ref[...], b_ref[...])

        def finish(acc):
            res = epilogue(acc, *[e[...] for e in ex]) if epilogue else (acc,)
            for o, r in zip(outs, res):
                o[...] = r.astype(o.dtype)

        if nk == 1:
            finish(part)
        else:
            acc_ref = refs[-1]
            kk = pl.program_id(2)

            @pl.when(kk == 0)
            def _():
                acc_ref[...] = part

            @pl.when(kk > 0)
            def _():
                acc_ref[...] += part

            @pl.when(kk == nk - 1)
            def _():
                finish(acc_ref[...])

    b_spec = pl.BlockSpec((tk, tn), lambda i, j, kk: (kk, j)) if mode == "nn" else pl.BlockSpec((tn, tk), lambda i, j, kk: (j, kk))
    tile = pl.BlockSpec((tm, tn), lambda i, j, kk: (i, j))
    out = pl.pallas_call(
        body,
        name=name,
        grid=(m // tm, n // tn, nk),
        in_specs=[pl.BlockSpec((tm, tk), lambda i, j, kk: (i, kk)), b_spec] + [tile] * n_ex,
        out_specs=[tile] * n_out,
        out_shape=[jax.ShapeDtypeStruct((m, n), dt) for dt in out_dtypes],
        scratch_shapes=[pltpu.VMEM((tm, tn), F32)] if nk > 1 else [],
        compiler_params=_cp(("parallel", "parallel", "arbitrary")),
    )(a, b, *extras)
    return out[0] if n_out == 1 else out


def _mm_tn(a, b, *, name):
    s, m = a.shape
    n = b.shape[1]
    cap = max(128, (1 << 20) // n)
    tm = _pick(m, tuple(t for t in (512, 640, 384, 256, 128) if t <= max(cap, 128)))
    ts = min(s, 1 << (((1 << 22) // n).bit_length() - 1))
    ns = s // ts

    def body(a_ref, b_ref, o_ref):
        part = _dot_tn(a_ref[...], b_ref[...])
        if ns == 1:
            o_ref[...] = part
        else:
            ss = pl.program_id(1)

            @pl.when(ss == 0)
            def _():
                o_ref[...] = part

            @pl.when(ss > 0)
            def _():
                o_ref[...] += part

    return pl.pallas_call(
        body,
        name=name,
        grid=(m // tm, ns),
        in_specs=[pl.BlockSpec((ts, tm), lambda i, ss: (ss, i)), pl.BlockSpec((ts, n), lambda i, ss: (ss, 0))],
        out_specs=pl.BlockSpec((tm, n), lambda i, ss: (i, 0)),
        out_shape=jax.ShapeDtypeStruct((m, n), F32),
        compiler_params=_cp(("parallel", "arbitrary")),
    )(a, b)


def _mm_tn_into(a, b, packs, *, rows, off, name):
    s, m = a.shape
    n = b.shape[1]
    tm = 1024 if rows % 1024 == 0 and s >= 4096 else 512
    tr = min(tm, rows)
    per, chips = rows // tr, tm // tr
    ts = min(s, (1 << (((1 << 22) // n).bit_length() - 1)) * 512 // tm)
    ns = s // ts

    def body(a_ref, b_ref, f_in, lo_in, f_ref, lo_ref):
        part = _dot_tn(a_ref[...], b_ref[...])
        pieces = [part[c * tr : (c + 1) * tr] for c in range(chips)]
        if ns == 1:
            for c, p in enumerate(pieces):
                f_ref[c] = p
                lo_ref[c] = p.astype(lo_ref.dtype)
        else:
            ss = pl.program_id(1)

            @pl.when(ss == 0)
            def _():
                for c, p in enumerate(pieces):
                    f_ref[c] = p

            @pl.when(ss > 0)
            def _():
                for c, p in enumerate(pieces):
                    f_ref[c] += p

            @pl.when(ss == ns - 1)
            def _():
                lo_ref[...] = f_ref[...].astype(lo_ref.dtype)

    spec = pl.BlockSpec((chips, tr, n), lambda i, ss: (i // per, off // tr + i % per, 0))
    return pl.pallas_call(
        body,
        name=name,
        grid=(m // tm, ns),
        in_specs=[pl.BlockSpec((ts, tm), lambda i, ss: (ss, i)), pl.BlockSpec((ts, n), lambda i, ss: (ss, 0)), ANY, ANY],
        out_specs=[spec, spec],
        out_shape=[jax.ShapeDtypeStruct(p.shape, p.dtype) for p in packs],
        input_output_aliases={2: 0, 3: 1},
        compiler_params=_cp(("parallel", "arbitrary")),
    )(a, b, *packs)


def _mm_rows(a, b, *, mode, name, rows=(), vecs=(), out_rows=(), out_vecs=(), epilogue, tm=512):
    m, k = a.shape
    n = b.shape[1] if mode == "nn" else b.shape[0]
    tm = min(m, tm)
    parts = 2 if tm % 256 == 0 else 1
    n_r, n_v, n_or, n_ov = len(rows), len(vecs), len(out_rows), len(out_vecs)

    def body(*refs):
        a_ref, b_ref = refs[:2]
        r_refs = refs[2 : 2 + n_r]
        v_refs = refs[2 + n_r : 2 + n_r + n_v]
        or_refs = refs[2 + n_r + n_v : 2 + n_r + n_v + n_or]
        ov_refs = refs[2 + n_r + n_v + n_or :]
        res_vecs = None
        for p in range(parts):
            rs = slice(p * tm // parts, (p + 1) * tm // parts)
            acc = _dot(a_ref[rs, :], b_ref[...]) if mode == "nn" else _dot_nt(a_ref[rs, :], b_ref[...])
            res_rows, part_vecs = epilogue(acc, [r[rs, :] for r in r_refs], [v[...] for v in v_refs])
            for o, r in zip(or_refs, res_rows):
                o[rs, :] = r.astype(o.dtype)
            res_vecs = part_vecs if res_vecs is None else [s + t for s, t in zip(res_vecs, part_vecs)]
        if n_ov:
            first = pl.program_id(0) == 0

            @pl.when(first)
            def _():
                for o, r in zip(ov_refs, res_vecs):
                    o[...] = r

            @pl.when(jnp.logical_not(first))
            def _():
                for o, r in zip(ov_refs, res_vecs):
                    o[...] += r

    tile = pl.BlockSpec((tm, n), lambda i: (i, 0))
    whole = lambda arr: pl.BlockSpec(arr.shape, lambda i: (0, 0))
    vec = lambda w: pl.BlockSpec((1, w), lambda i: (0, 0))
    out = pl.pallas_call(
        body,
        name=name,
        grid=(m // tm,),
        in_specs=[pl.BlockSpec((tm, k), lambda i: (i, 0)), whole(b)] + [tile] * n_r + [vec(v.shape[1]) for v in vecs],
        out_specs=[tile] * n_or + [vec(w) for w in out_vecs],
        out_shape=[jax.ShapeDtypeStruct((m, n), dt) for dt in out_rows] + [jax.ShapeDtypeStruct((1, w), F32) for w in out_vecs],
        compiler_params=_cp(("arbitrary",) if n_ov else ("parallel",)),
    )(a, b, *rows, *vecs)
    return out


def _ep_residual_norm(acc, rows, vecs):
    x = acc + rows[0]
    r = lax.rsqrt(jnp.mean(x * x, axis=-1, keepdims=True) + EPS)
    return [x, x * r * vecs[0]], []


def _ep_norm_bwd(acc, rows, vecs):
    dy = acc
    for extra in rows[2:]:
        dy = dy + extra
    x, dres = rows[0], rows[1]
    r = lax.rsqrt(jnp.mean(x * x, axis=-1, keepdims=True) + EPS)
    xh = x * r
    dxh = dy * vecs[0]
    dx = r * (dxh - xh * jnp.mean(dxh * xh, axis=-1, keepdims=True)) + dres
    return [dx, dx], [jnp.sum(dy * xh, axis=0, keepdims=True)]


def _ep_loss(acc, rows, vecs):
    x = acc + rows[0]
    d = x.shape[-1]
    r = lax.rsqrt(jnp.mean(x * x, axis=-1, keepdims=True) + EPS)
    xh = x * r
    err = xh * vecs[0] - rows[1]
    loss = jnp.zeros((1, 128), F32) + 0.5 * jnp.sum(jnp.mean(err * err, axis=-1, keepdims=True))
    dy = err * (1.0 / d)
    dxh = dy * vecs[0]
    dx = r * (dxh - xh * jnp.mean(dxh * xh, axis=-1, keepdims=True))
    return [dx, dx], [loss, jnp.sum(dy * xh, axis=0, keepdims=True)]


def _rms_fwd(x, g, *, name):
    s, d = x.shape
    ts = _rows(s, light=True, times=4)

    def body(x_ref, g_ref, o_ref):
        xf = x_ref[...]
        r = lax.rsqrt(jnp.mean(xf * xf, axis=-1, keepdims=True) + EPS)
        o_ref[...] = (xf * r * g_ref[...]).astype(o_ref.dtype)

    return pl.pallas_call(
        body,
        name=name,
        grid=(s // ts,),
        in_specs=[pl.BlockSpec((ts, d), lambda i: (i, 0)), pl.BlockSpec((1, d), lambda i: (0, 0))],
        out_specs=pl.BlockSpec((ts, d), lambda i: (i, 0)),
        out_shape=jax.ShapeDtypeStruct((s, d), _CD),
        compiler_params=_cp(("parallel",)),
    )(x, g)


def _rms_gain_grad(x, dy, *, name):
    s, d = x.shape
    ts = _rows(s)

    def body(x_ref, dy_ref, dg_ref):
        xf = x_ref[...]
        r = lax.rsqrt(jnp.mean(xf * xf, axis=-1, keepdims=True) + EPS)
        part = jnp.sum(dy_ref[...] * (xf * r), axis=0, keepdims=True)

        @pl.when(pl.program_id(0) == 0)
        def _():
            dg_ref[...] = part

        @pl.when(pl.program_id(0) > 0)
        def _():
            dg_ref[...] += part

    tile = pl.BlockSpec((ts, d), lambda i: (i, 0))
    return pl.pallas_call(
        body,
        name=name,
        grid=(s // ts,),
        in_specs=[tile, tile],
        out_specs=pl.BlockSpec((1, d), lambda i: (0, 0)),
        out_shape=jax.ShapeDtypeStruct((1, d), F32),
        compiler_params=_cp(("arbitrary",)),
    )(x, dy)


def _chunk_scan(v, row_in_chunk, suffix):
    t = v.shape[0]
    step = 1
    while step < GLA_CHUNK:
        if suffix:
            v = v + jnp.where(row_in_chunk < GLA_CHUNK - step, pltpu.roll(v, t - step, 0), 0.0)
        else:
            v = v + jnp.where(row_in_chunk >= step, pltpu.roll(v, step, 0), 0.0)
        step *= 2
    return v


def _gate_pre(lr, w_ref, b_ref):
    return _dot(lr, w_ref[...]) + b_ref[...]


def _gate_fwd(z, waf, wab, baf, bab, *, name):
    s = z.shape[0]
    ts = _rows(s, light=True, times=4)

    def body(lr_ref, waf_ref, wab_ref, baf_ref, bab_ref, bf_ref, bb_ref):
        lr = lr_ref[...]
        ric = lax.broadcasted_iota(jnp.int32, (ts, GLA_K_TOTAL), 0) & (GLA_CHUNK - 1)
        for w_ref, b_ref, o_ref, suffix in ((waf_ref, baf_ref, bf_ref, False), (wab_ref, bab_ref, bb_ref, True)):
            pre = _gate_pre(lr, w_ref, b_ref)
            la = (jnp.minimum(pre, 0.0) - jnp.log(1.0 + jnp.exp(-jnp.abs(pre)))) * GLA_GATE_SCALE
            o_ref[...] = _chunk_scan(la, ric, suffix)

    wspec = pl.BlockSpec((128, GLA_K_TOTAL), lambda i: (0, 0))
    bspec = pl.BlockSpec((1, GLA_K_TOTAL), lambda i: (0, 0))
    tile = pl.BlockSpec((ts, GLA_K_TOTAL), lambda i: (i, 0))
    return pl.pallas_call(
        body,
        name=name,
        grid=(s // ts,),
        in_specs=[pl.BlockSpec((ts, 128), lambda i: (i, LR_COL // 128)), wspec, wspec, bspec, bspec],
        out_specs=[tile, tile],
        out_shape=[jax.ShapeDtypeStruct((s, GLA_K_TOTAL), F32)] * 2,
        compiler_params=_cp(("parallel",)),
    )(z, waf, wab, baf, bab)


def _gate_bwd(z, waf, wab, baf, bab, dbf, dbb, dqkv_f, dqkv_b, *, name):
    s = z.shape[0]
    ts = _rows(s, light=True)

    def body(lr_ref, waf_ref, wab_ref, baf_ref, bab_ref, dbf_ref, dbb_ref, gf_ref, gb_ref, dzb_ref, dwf_ref, dwb_ref, dbaf_ref, dbab_ref):
        lr = lr_ref[...]
        ric = lax.broadcasted_iota(jnp.int32, (ts, GLA_K_TOTAL), 0) & (GLA_CHUNK - 1)
        first = pl.program_id(0) == 0
        dlr = None
        for w_ref, b_ref, db_ref, dw_ref, dbias_ref, suffix in (
            (waf_ref, baf_ref, dbf_ref, dwf_ref, dbaf_ref, True),
            (wab_ref, bab_ref, dbb_ref, dwb_ref, dbab_ref, False),
        ):
            pre = _gate_pre(lr, w_ref, b_ref)
            dla = _chunk_scan(db_ref[...], ric, suffix)
            dpre = dla * GLA_GATE_SCALE * _sigmoid(-pre)
            part = _dot_nt(dpre, w_ref[...])
            dlr = part if dlr is None else dlr + part
            dw = _dot_tn(lr, dpre)
            dbias = jnp.sum(dpre, axis=0, keepdims=True)

            @pl.when(first)
            def _():
                dw_ref[...] = dw
                dbias_ref[...] = dbias

            @pl.when(jnp.logical_not(first))
            def _():
                dw_ref[...] += dw
                dbias_ref[...] += dbias

        dzb_ref[...] = jnp.concatenate([gf_ref[...] + gb_ref[...], dlr], axis=1).astype(dzb_ref.dtype)

    wspec = pl.BlockSpec((128, GLA_K_TOTAL), lambda i: (0, 0))
    bspec = pl.BlockSpec((1, GLA_K_TOTAL), lambda i: (0, 0))
    tile = pl.BlockSpec((ts, GLA_K_TOTAL), lambda i: (i, 0))
    wide = pl.BlockSpec((ts, 2 * GLA_K_TOTAL + GLA_V_TOTAL), lambda i: (i, 0))
    return pl.pallas_call(
        body,
        name=name,
        grid=(s // ts,),
        in_specs=[pl.BlockSpec((ts, 128), lambda i: (i, LR_COL // 128)), wspec, wspec, bspec, bspec, tile, tile, wide, wide],
        out_specs=[pl.BlockSpec((ts, ZB_COLS), lambda i: (i, 0)), wspec, wspec, bspec, bspec],
        out_shape=[
            jax.ShapeDtypeStruct((s, ZB_COLS), _CD),
            jax.ShapeDtypeStruct((128, GLA_K_TOTAL), F32),
            jax.ShapeDtypeStruct((128, GLA_K_TOTAL), F32),
            jax.ShapeDtypeStruct((1, GLA_K_TOTAL), F32),
            jax.ShapeDtypeStruct((1, GLA_K_TOTAL), F32),
        ],
        compiler_params=_cp(("arbitrary",)),
    )(z, waf, wab, baf, bab, dbf, dbb, dqkv_f, dqkv_b)


def _gla_masks(rev):
    lane_head = lax.broadcasted_iota(jnp.int32, (1, GLA_K_TOTAL), 1) >> 6
    head_masks = [lane_head == h for h in range(GLA_HEADS)]
    t = lax.broadcasted_iota(jnp.int32, (GLA_HEADS * GLA_CHUNK, GLA_CHUNK), 0) & (GLA_CHUNK - 1)
    u = lax.broadcasted_iota(jnp.int32, (GLA_HEADS * GLA_CHUNK, GLA_CHUNK), 1)
    tri = (u > t) if rev else (u <= t)
    row = lax.broadcasted_iota(jnp.int32, (GLA_CHUNK, GLA_K_TOTAL), 0)
    total_row = row == (0 if rev else GLA_CHUNK - 1)
    return head_masks, tri, total_row


def _spread(a, head_masks):
    return jnp.concatenate([jnp.where(m, a, 0.0) for m in head_masks], axis=0)


def _stack(a):
    return jnp.concatenate([a[:, GLA_DV * h : GLA_DV * (h + 1)] for h in range(GLA_HEADS)], axis=0)


def _unstack(a):
    return jnp.concatenate([a[GLA_CHUNK * h : GLA_CHUNK * (h + 1)] for h in range(GLA_HEADS)], axis=1)


def _collect(a, head_masks):
    out = None
    for h, m in enumerate(head_masks):
        part = jnp.where(m, a[GLA_CHUNK * h : GLA_CHUNK * (h + 1)], 0.0)
        out = part if out is None else out + part
    return out


def _gla_chunk_terms(q_ref, k_ref, v_ref, b_ref, rows, head_masks, tri, total_row):
    q = q_ref[rows, :] * (GLA_DK**-0.5)
    k = k_ref[rows, :]
    v = v_ref[rows, :]
    b = b_ref[rows, :]
    eb = jnp.exp(b)
    enb = jnp.exp(-b)
    g = jnp.sum(jnp.where(total_row, b, 0.0), axis=0, keepdims=True)
    egb = jnp.exp(g - b)
    qt = q * eb
    kt = k * enb
    kh = k * egb
    q_heads = _spread(qt, head_masks)
    attn = jnp.where(tri, _dot_nt(q_heads, kt), 0.0)
    return v, eb, enb, egb, jnp.exp(g), qt, kt, kh, q_heads, attn


def _gla_specs(s, tb, rev_blocks):
    nb = s // tb
    rb = (lambda i: nb - 1 - i) if rev_blocks else (lambda i: i)
    q_spec = pl.BlockSpec((tb, GLA_K_TOTAL), lambda i: (rb(i), 0))
    k_spec = pl.BlockSpec((tb, GLA_K_TOTAL), lambda i: (rb(i), 1))
    v_spec = pl.BlockSpec((tb, GLA_V_TOTAL), lambda i: (rb(i), 1))
    b_spec = pl.BlockSpec((tb, GLA_K_TOTAL), lambda i: (rb(i), 0))
    o_spec = pl.BlockSpec((tb, GLA_V_TOTAL), lambda i: (rb(i), 0))
    st_spec = pl.BlockSpec((tb // GLA_CHUNK, GLA_DV, GLA_K_TOTAL), lambda i: (rb(i), 0, 0))
    return nb, q_spec, k_spec, v_spec, b_spec, o_spec, st_spec


def _gla_fwd_chunk(cidx, q_ref, k_ref, v_ref, b_ref, o_ref, sv_ref, st_ref, masks):
    head_masks, tri, total_row = masks
    rows = pl.ds(pl.multiple_of(cidx * GLA_CHUNK, GLA_CHUNK), GLA_CHUNK)
    v, _, _, _, eg, _, _, kh, q_heads, attn = _gla_chunk_terms(q_ref, k_ref, v_ref, b_ref, rows, head_masks, tri, total_row)
    o = jnp.concatenate(
        [_dot(attn[GLA_CHUNK * h : GLA_CHUNK * (h + 1)], v[:, GLA_DV * h : GLA_DV * (h + 1)]) for h in range(GLA_HEADS)], axis=1
    )
    st = st_ref[...]
    o_ref[rows, :] = o + _unstack(_dot_nt(q_heads, st))
    sv_ref[cidx] = st
    st_ref[...] = st * eg + _dot_tn(_stack(v), _spread(kh, head_masks))


def _gla_fwd(z, b_f, b_b, *, name):
    s = z.shape[0]
    tb = _rows(s)
    cpb = tb // GLA_CHUNK
    nb, qf, kf, vf, bf, of, sf = _gla_specs(s, tb, False)
    _, qr, kr, vr, br, orr, sr = _gla_specs(s, tb, True)

    def body(qf_ref, kf_ref, vf_ref, bf_ref, qr_ref, kr_ref, vr_ref, br_ref, of_ref, svf_ref, or_ref, svr_ref, stf_ref, str_ref):
        masks_f, masks_r = _gla_masks(False), _gla_masks(True)

        @pl.when(pl.program_id(0) == 0)
        def _():
            stf_ref[...] = jnp.zeros_like(stf_ref)
            str_ref[...] = jnp.zeros_like(str_ref)

        def chunk(ci, carry):
            _gla_fwd_chunk(ci, qf_ref, kf_ref, vf_ref, bf_ref, of_ref, svf_ref, stf_ref, masks_f)
            _gla_fwd_chunk(cpb - 1 - ci, qr_ref, kr_ref, vr_ref, br_ref, or_ref, svr_ref, str_ref, masks_r)
            return carry

        lax.fori_loop(0, cpb, chunk, 0)

    o_shape = jax.ShapeDtypeStruct((s, GLA_V_TOTAL), F32)
    st_shape = jax.ShapeDtypeStruct((s // GLA_CHUNK, GLA_DV, GLA_K_TOTAL), F32)
    return pl.pallas_call(
        body,
        name=name,
        grid=(nb,),
        in_specs=[qf, kf, vf, bf, qr, kr, vr, br],
        out_specs=[of, sf, orr, sr],
        out_shape=[o_shape, st_shape, o_shape, st_shape],
        scratch_shapes=[pltpu.VMEM((GLA_DV, GLA_K_TOTAL), F32)] * 2,
        compiler_params=_cp(("arbitrary",)),
    )(z, z, z, b_f, z, z, z, b_b)


def _gla_bwd_chunk(cidx, q_ref, k_ref, v_ref, b_ref, do_ref, sv_ref, dqkv_ref, db_ref, dst_ref, masks):
    head_masks, tri, total_row = masks
    rows = pl.ds(pl.multiple_of(cidx * GLA_CHUNK, GLA_CHUNK), GLA_CHUNK)
    v, eb, enb, egb, eg, qt, kt, kh, q_heads, attn = _gla_chunk_terms(q_ref, k_ref, v_ref, b_ref, rows, head_masks, tri, total_row)
    do_c = do_ref[rows, :]
    st = sv_ref[cidx]
    dst = dst_ref[...]
    do_s, v_s = _stack(do_c), _stack(v)
    hs = lambda a, h: a[GLA_CHUNK * h : GLA_CHUNK * (h + 1)]
    vs = lambda a, h: a[:, GLA_DV * h : GLA_DV * (h + 1)]
    dattn = jnp.concatenate([_dot_nt(vs(do_c, h), vs(v, h)) for h in range(GLA_HEADS)], axis=0)
    dattn = jnp.where(tri, dattn, 0.0)
    dv = jnp.concatenate([_dot_tn(hs(attn, h), vs(do_c, h)) for h in range(GLA_HEADS)], axis=1)
    dv = dv + _unstack(_dot_nt(_spread(kh, head_masks), dst))
    dqt = _collect(_dot(do_s, st) + _dot(dattn, kt), head_masks)
    dkt = _dot_tn(dattn, q_heads)
    dkh = _collect(_dot(v_s, dst), head_masks)
    dg = jnp.sum(dkh * kh, axis=0, keepdims=True) + jnp.sum(dst * st, axis=0, keepdims=True) * eg
    db = dqt * qt - dkt * kt - dkh * kh + jnp.where(total_row, dg, 0.0)
    dq = dqt * eb * (GLA_DK**-0.5)
    dk = dkt * enb + dkh * egb
    dqkv_ref[rows, :] = jnp.concatenate([dq, dk, dv], axis=1)
    db_ref[rows, :] = db
    dst_ref[...] = dst * eg + _dot_tn(do_s, q_heads)


def _gla_bwd(z, b_f, b_b, do, st_f, st_b, *, name):
    s = z.shape[0]
    tb = _rows(s)
    cpb = tb // GLA_CHUNK
    wide = 2 * GLA_K_TOTAL + GLA_V_TOTAL
    nb, qf, kf, vf, bf, of, sf = _gla_specs(s, tb, True)
    _, qr, kr, vr, br, orr, sr = _gla_specs(s, tb, False)
    gf = pl.BlockSpec((tb, wide), lambda i: (nb - 1 - i, 0))
    gr = pl.BlockSpec((tb, wide), lambda i: (i, 0))

    def body(qf_ref, kf_ref, vf_ref, bf_ref, dof_ref, svf_ref, qr_ref, kr_ref, vr_ref, br_ref, dor_ref, svr_ref,
             gf_ref, dbf_ref, gr_ref, dbr_ref, dstf_ref, dstr_ref):
        masks_f, masks_r = _gla_masks(False), _gla_masks(True)

        @pl.when(pl.program_id(0) == 0)
        def _():
            dstf_ref[...] = jnp.zeros_like(dstf_ref)
            dstr_ref[...] = jnp.zeros_like(dstr_ref)

        def chunk(ci, carry):
            _gla_bwd_chunk(cpb - 1 - ci, qf_ref, kf_ref, vf_ref, bf_ref, dof_ref, svf_ref, gf_ref, dbf_ref, dstf_ref, masks_f)
            _gla_bwd_chunk(ci, qr_ref, kr_ref, vr_ref, br_ref, dor_ref, svr_ref, gr_ref, dbr_ref, dstr_ref, masks_r)
            return carry

        lax.fori_loop(0, cpb, chunk, 0)

    g_shape = jax.ShapeDtypeStruct((s, wide), F32)
    db_shape = jax.ShapeDtypeStruct((s, GLA_K_TOTAL), F32)
    return pl.pallas_call(
        body,
        name=name,
        grid=(nb,),
        in_specs=[qf, kf, vf, bf, of, sf, qr, kr, vr, br, orr, sr],
        out_specs=[gf, bf, gr, br],
        out_shape=[g_shape, db_shape, g_shape, db_shape],
        scratch_shapes=[pltpu.VMEM((GLA_DV, GLA_K_TOTAL), F32)] * 2,
        compiler_params=_cp(("arbitrary",)),
    )(z, z, z, b_f, do, st_f, z, z, z, b_b, do, st_b)


HALO = 8


def _halo_specs(s, ts, width, col):
    last = s // HALO - 1
    per = ts // HALO
    prev = pl.BlockSpec((HALO, width), lambda i: (jnp.maximum(i * per - 1, 0), col))
    nxt = pl.BlockSpec((HALO, width), lambda i: (jnp.minimum((i + 1) * per, last), col))
    return prev, nxt


def _group_ones():
    group = jnp.arange(CONV_WIDTH, dtype=jnp.int32) // CONV_GROUP
    return (group[:, None] == group[None, :]).astype(BF16)


_ONES_SPEC = pl.BlockSpec((CONV_WIDTH, CONV_WIDTH), lambda i: (0, 0))


def _conv_terms(cc_ext, cu_ext, cw, valid):
    n = cc_ext.shape[0]
    hc = jnp.where(valid, cc_ext * cu_ext, 0.0)
    hc_prev = pltpu.roll(hc, 1, 0)
    hc_next = pltpu.roll(hc, n - 1, 0)
    conv = cw[0:1] * hc_prev + cw[1:2] * hc + cw[2:3] * hc_next
    return hc, hc_prev, hc_next, conv


def _ext(prev_ref, cur_ref, next_ref):
    return jnp.concatenate([prev_ref[...], cur_ref[...], next_ref[...]], axis=0)


def _valid_rows(ts, s):
    row = lax.broadcasted_iota(jnp.int32, (ts + 2 * HALO, 1), 0) + (pl.program_id(0) * ts - HALO)
    return (row >= 0) & (row < s)


def _head_norm(o, gn):
    out = []
    for h in range(GLA_HEADS):
        oh = o[:, GLA_DV * h : GLA_DV * (h + 1)]
        r = lax.rsqrt(jnp.mean(oh * oh, axis=-1, keepdims=True) + EPS)
        out.append((oh * r, r))
    return out


def _mix_fwd(z, o_f, o_b, conv_w, conv_norm, gla_norm, *, name):
    s = z.shape[0]
    ts = _rows(s, light=True)
    cprev, cnext = _halo_specs(s, ts, CONV_WIDTH, 1)
    uprev, unext = _halo_specs(s, ts, CONV_WIDTH, 2)

    def body(cb_ref, cc_ref, cu_ref, ccp_ref, ccn_ref, cup_ref, cun_ref, g_ref, of_ref, ob_ref, cw_ref, cn_ref, gn_ref, ones_ref, y_ref):
        valid = _valid_rows(ts, s)
        _, _, _, conv = _conv_terms(_ext(ccp_ref, cc_ref, ccn_ref), _ext(cup_ref, cu_ref, cun_ref), cw_ref[...], valid)
        yc = cb_ref[...] * conv[HALO : HALO + ts]
        ms = _dot_split(yc * yc, ones_ref[...]) * (1.0 / CONV_GROUP)
        y_conv = yc * lax.rsqrt(ms + EPS) * cn_ref[...]
        gate = g_ref[...]
        silu = gate * _sigmoid(gate)
        gn = gn_ref[...]
        y_gla = jnp.concatenate([oh * gn for oh, _ in _head_norm(of_ref[...] + ob_ref[...], gn)], axis=1) * silu
        y_ref[...] = jnp.concatenate([y_conv, y_gla], axis=1).astype(y_ref.dtype)

    col = lambda c, w=CONV_WIDTH: pl.BlockSpec((ts, w), lambda i: (i, c))
    return pl.pallas_call(
        body,
        name=name,
        grid=(s // ts,),
        in_specs=[col(0), col(1), col(2), cprev, cnext, uprev, unext, col(3), col(0), col(0),
                  pl.BlockSpec((CONV_K, CONV_WIDTH), lambda i: (0, 0)), pl.BlockSpec((1, CONV_WIDTH), lambda i: (0, 0)),
                  pl.BlockSpec((1, GLA_DV), lambda i: (0, 0)), _ONES_SPEC],
        out_specs=pl.BlockSpec((ts, D_MODEL), lambda i: (i, 0)),
        out_shape=jax.ShapeDtypeStruct((s, D_MODEL), _CD),
        compiler_params=_cp(("parallel",)),
    )(z, z, z, z, z, z, z, z, o_f, o_b, conv_w, conv_norm, gla_norm, _group_ones())


def _mix_bwd(z, o_f, o_b, dy, conv_w, conv_norm, gla_norm, *, name):
    s = z.shape[0]
    ts = _rows(s)
    halos = [_halo_specs(s, ts, CONV_WIDTH, c) for c in (0, 1, 2)]
    dprev, dnext = _halo_specs(s, ts, CONV_WIDTH, 0)

    def body(cb_ref, cc_ref, cu_ref, cbp_ref, cbn_ref, ccp_ref, ccn_ref, cup_ref, cun_ref, g_ref, of_ref, ob_ref,
             dyc_ref, dyg_ref, dyp_ref, dyn_ref, cw_ref, cn_ref, gn_ref, ones_ref, dza_ref, do_ref, dcw_ref, dcn_ref, dgn_ref):
        n = ts + 2 * HALO
        valid = _valid_rows(ts, s)
        cw = cw_ref[...]
        cn = cn_ref[...]
        ones = ones_ref[...]
        cb = _ext(cbp_ref, cb_ref, cbn_ref)
        cc = _ext(ccp_ref, cc_ref, ccn_ref)
        cu = _ext(cup_ref, cu_ref, cun_ref)
        dy = _ext(dyp_ref, dyc_ref, dyn_ref)
        hc, hc_prev, hc_next, conv = _conv_terms(cc, cu, cw, valid)
        yc = cb * conv
        r = lax.rsqrt(_dot_split(yc * yc, ones) * (1.0 / CONV_GROUP) + EPS)
        yh = yc * r
        dyh = dy * cn
        dyc = r * (dyh - yh * (_dot_split(dyh * yh, ones) * (1.0 / CONV_GROUP)))
        dconv = jnp.where(valid, dyc * cb, 0.0)
        dhc = cw[0:1] * pltpu.roll(dconv, n - 1, 0) + cw[1:2] * dconv + cw[2:3] * pltpu.roll(dconv, 1, 0)
        mid = lambda a: a[HALO : HALO + ts]
        dza_ref[:, 0 : 3 * CONV_WIDTH] = jnp.concatenate([mid(dyc * conv), mid(dhc * cu), mid(dhc * cc)], axis=1).astype(dza_ref.dtype)
        dconv_m = mid(dconv)
        colsum = lambda a: jnp.sum(a, axis=0, keepdims=True)
        dcw = jnp.concatenate([colsum(dconv_m * mid(hc_prev)), colsum(dconv_m * mid(hc)), colsum(dconv_m * mid(hc_next))], axis=0)
        dcn = colsum(mid(dy * yh))

        gate = g_ref[...]
        sg = _sigmoid(gate)
        silu = gate * sg
        gn = gn_ref[...]
        dyg = dyg_ref[...]
        don = dyg * silu
        heads = _head_norm(of_ref[...] + ob_ref[...], gn)
        on = jnp.concatenate([oh * gn for oh, _ in heads], axis=1)
        dza_ref[:, 3 * CONV_WIDTH : ZA_COLS] = (dyg * on * (sg * (1.0 + gate * (1.0 - sg)))).astype(dza_ref.dtype)
        dgn = jnp.zeros((1, GLA_DV), F32)
        dos = []
        for h, (oh, rh) in enumerate(heads):
            donh = don[:, GLA_DV * h : GLA_DV * (h + 1)]
            dgn = dgn + colsum(donh * oh)
            doh = donh * gn
            dos.append(rh * (doh - oh * jnp.mean(doh * oh, axis=-1, keepdims=True)))
        do_ref[...] = jnp.concatenate(dos, axis=1)

        first = pl.program_id(0) == 0

        @pl.when(first)
        def _():
            dcw_ref[...] = dcw
            dcn_ref[...] = dcn
            dgn_ref[...] = dgn

        @pl.when(jnp.logical_not(first))
        def _():
            dcw_ref[...] += dcw
            dcn_ref[...] += dcn
            dgn_ref[...] += dgn

    col = lambda c, w=CONV_WIDTH: pl.BlockSpec((ts, w), lambda i: (i, c))
    cw_spec = pl.BlockSpec((CONV_K, CONV_WIDTH), lambda i: (0, 0))
    cn_spec = pl.BlockSpec((1, CONV_WIDTH), lambda i: (0, 0))
    gn_spec = pl.BlockSpec((1, GLA_DV), lambda i: (0, 0))
    return pl.pallas_call(
        body,
        name=name,
        grid=(s // ts,),
        in_specs=[col(0), col(1), col(2), halos[0][0], halos[0][1], halos[1][0], halos[1][1], halos[2][0], halos[2][1],
                  col(3), col(0), col(0), col(0), col(1), dprev, dnext, cw_spec, cn_spec, gn_spec, _ONES_SPEC],
        out_specs=[pl.BlockSpec((ts, ZA_COLS), lambda i: (i, 0)), col(0), cw_spec, cn_spec, gn_spec],
        out_shape=[
            jax.ShapeDtypeStruct((s, ZA_COLS), _CD),
            jax.ShapeDtypeStruct((s, GLA_V_TOTAL), F32),
            jax.ShapeDtypeStruct((CONV_K, CONV_WIDTH), F32),
            jax.ShapeDtypeStruct((1, CONV_WIDTH), F32),
            jax.ShapeDtypeStruct((1, GLA_DV), F32),
        ],
        compiler_params=_cp(("arbitrary",)),
    )(z, z, z, z, z, z, z, z, z, z, o_f, o_b, dy, dy, dy, dy, conv_w, conv_norm, gla_norm, _group_ones())


def _xa_probs(q_ref, kv_ref, h):
    qh = q_ref[:, XA_HEAD_DIM * h : XA_HEAD_DIM * (h + 1)]
    kh = kv_ref[:, XA_HEAD_DIM * h : XA_HEAD_DIM * (h + 1)]
    vh = kv_ref[:, D_MODEL + XA_HEAD_DIM * h : D_MODEL + XA_HEAD_DIM * (h + 1)]
    sc = _dot_nt(qh, kh) * (XA_HEAD_DIM**-0.5)
    e = jnp.exp(sc - jnp.max(sc, axis=-1, keepdims=True))
    return qh, kh, vh, e / jnp.sum(e, axis=-1, keepdims=True)


def _xattn_fwd(qx, kv, *, name):
    s = qx.shape[0]
    ts = _rows(s, light=True, times=4)

    def body(q_ref, kv_ref, o_ref):
        outs = []
        for h in range(XA_HEADS):
            _, _, vh, p = _xa_probs(q_ref, kv_ref, h)
            outs.append(_dot(p, vh))
        o_ref[...] = jnp.concatenate(outs, axis=1).astype(o_ref.dtype)

    return pl.pallas_call(
        body,
        name=name,
        grid=(s // ts,),
        in_specs=[pl.BlockSpec((ts, D_MODEL), lambda i: (i, 0)), pl.BlockSpec((N_MEM, 2 * D_MODEL), lambda i: (0, 0))],
        out_specs=pl.BlockSpec((ts, D_MODEL), lambda i: (i, 0)),
        out_shape=jax.ShapeDtypeStruct((s, D_MODEL), _CD),
        compiler_params=_cp(("parallel",)),
    )(qx, kv)


def _xattn_bwd(qx, kv, dox, *, name):
    s = qx.shape[0]
    ts = _rows(s, light=True, times=4)

    def body(q_ref, kv_ref, do_ref, dq_ref, dkv_ref):
        dqs, dks, dvs = [], [], []
        for h in range(XA_HEADS):
            qh, kh, vh, p = _xa_probs(q_ref, kv_ref, h)
            doh = do_ref[:, XA_HEAD_DIM * h : XA_HEAD_DIM * (h + 1)]
            dp = _dot_nt(doh, vh)
            ds = p * (dp - jnp.sum(dp * p, axis=-1, keepdims=True)) * (XA_HEAD_DIM**-0.5)
            dqs.append(_dot(ds, kh))
            dks.append(_dot_tn(ds, qh))
            dvs.append(_dot_tn(p, doh))
        dq_ref[...] = jnp.concatenate(dqs, axis=1).astype(dq_ref.dtype)
        dkv = jnp.concatenate(dks + dvs, axis=1)

        @pl.when(pl.program_id(0) == 0)
        def _():
            dkv_ref[...] = dkv

        @pl.when(pl.program_id(0) > 0)
        def _():
            dkv_ref[...] += dkv

    tile = pl.BlockSpec((ts, D_MODEL), lambda i: (i, 0))
    kv_spec = pl.BlockSpec((N_MEM, 2 * D_MODEL), lambda i: (0, 0))
    return pl.pallas_call(
        body,
        name=name,
        grid=(s // ts,),
        in_specs=[tile, kv_spec, tile],
        out_specs=[tile, kv_spec],
        out_shape=[jax.ShapeDtypeStruct((s, D_MODEL), _CD), jax.ShapeDtypeStruct((N_MEM, 2 * D_MODEL), F32)],
        compiler_params=_cp(("arbitrary",)),
    )(qx, kv, dox)


def _adamw_math(w, g, m, v):
    m = ADAM_B1 * m + (1.0 - ADAM_B1) * g
    v = ADAM_B2 * v + (1.0 - ADAM_B2) * (g * g)
    m_hat = m / (1.0 - ADAM_B1**ADAM_STEP)
    v_hat = v / (1.0 - ADAM_B2**ADAM_STEP)
    delta = -ADAM_LR * (m_hat / (jnp.sqrt(v_hat) + ADAM_EPS) + ADAM_WD * w)
    return delta, m, v


def _adamw(w, m, v, shard_rows, off, *, transposed, name):
    r, c = w.shape
    by_columns = r % 256 != 0
    tr = 512 if (c if by_columns else r) % 512 == 0 and off % 512 == 0 else 256
    if by_columns:
        assert not transposed and off == 0
        g_spec = tile = pl.BlockSpec((r, tr), lambda i: (0, i))
    else:
        g_spec = pl.BlockSpec((c, tr), lambda i: (off // c, i)) if transposed else pl.BlockSpec((tr, c), lambda i: (off // tr + i, 0))
        tile = pl.BlockSpec((tr, c), lambda i: (i, 0))

    def body(w_ref, g_ref, m_ref, v_ref, go_ref, d_ref, nm_ref, nv_ref):
        g = g_ref[...].T if transposed else g_ref[...]
        go_ref[...] = g
        d_ref[...], nm_ref[...], nv_ref[...] = _adamw_math(w_ref[...], g, m_ref[...], v_ref[...])

    return pl.pallas_call(
        body,
        name=name,
        grid=((c if by_columns else r) // tr,),
        in_specs=[tile, g_spec, tile, tile],
        out_specs=[tile] * 4,
        out_shape=[jax.ShapeDtypeStruct((r, c), F32)] * 4,
        compiler_params=_cp(("parallel",)),
    )(w, shard_rows, m, v)


def _adamw_small(groups, *, name):
    n = len(groups)

    def body(*refs):
        ins, outs = refs[: 4 * n], refs[4 * n :]
        for i in range(n):
            w_ref, g_ref, m_ref, v_ref = ins[4 * i : 4 * i + 4]
            outs[3 * i][...], outs[3 * i + 1][...], outs[3 * i + 2][...] = _adamw_math(w_ref[...], g_ref[...], m_ref[...], v_ref[...])

    flat = [a for grp in groups for a in grp]
    vm = pl.BlockSpec(memory_space=pltpu.VMEM)
    res = pl.pallas_call(
        body,
        name=name,
        in_specs=[vm] * (4 * n),
        out_specs=[vm] * (3 * n),
        out_shape=[jax.ShapeDtypeStruct(grp[0].shape, F32) for grp in groups for _ in range(3)],
        compiler_params=_cp(),
    )(*flat)
    return [tuple(res[3 * i : 3 * i + 3]) for i in range(n)]


def _place():
    return lax.axis_index("x"), lax.axis_index("y"), lax.axis_index("c")


def _rel_chip(x, y, k):
    return (1 - x if k & 2 else x), (1 - y if k & 1 else y)


def _half(c, rh):
    return pl.ds(pl.multiple_of(c * rh, 16), rh)


HBM = pl.BlockSpec(memory_space=pltpu.HBM)
SEM = pl.BlockSpec(memory_space=pltpu.SEMAPHORE)
EFFECT = pltpu.SideEffectType.DATAFLOW_SIDE_EFFECTING


def _in_hbm(a):
    return pltpu.with_memory_space_constraint(a, pltpu.HBM)


def _gather_copies(p_ref, land_ref, send_sems, recv_sems):
    rh = p_ref.shape[0] // 2
    x, y, c = _place()
    rows = _half(c, rh)
    copies = []
    for k in range(1, N_CHIPS):
        cx, cy = _rel_chip(x, y, k)
        copies.append(pltpu.make_async_remote_copy(
            src_ref=p_ref.at[rows], dst_ref=land_ref.at[2 * x + y, rows], send_sem=send_sems.at[k - 1], recv_sem=recv_sems.at[k - 1],
            device_id=(cx, cy, c), device_id_type=MESH))
    copies.append(pltpu.make_async_remote_copy(
        src_ref=p_ref, dst_ref=land_ref.at[2 * x + y], send_sem=send_sems.at[N_CHIPS - 1], recv_sem=recv_sems.at[N_CHIPS - 1],
        device_id=(x, y, 1 - c), device_id_type=MESH))
    return copies


def _gather_start(pack, after, *, name):
    r, w = pack.shape

    def body(p_ref, land_ref, after_ref, send_sems, recv_sems, p_thru, land_thru, token):
        for cp in _gather_copies(p_ref, land_ref, send_sems, recv_sems):
            cp.start()
        token[...] = jnp.zeros_like(token)

    return pl.pallas_call(
        body,
        name=name,
        out_shape=(pltpu.SemaphoreType.DMA((N_CHIPS,)), pltpu.SemaphoreType.DMA((N_CHIPS,)), pltpu.HBM((r, w), pack.dtype),
                   pltpu.HBM((N_CHIPS, r, w), pack.dtype), jax.ShapeDtypeStruct((8, 128), F32)),
        in_specs=(HBM, HBM, ANY),
        out_specs=(SEM, SEM, HBM, HBM, pl.BlockSpec(memory_space=pltpu.VMEM)),
        input_output_aliases={0: 2, 1: 3},
        compiler_params=pltpu.CompilerParams(has_side_effects=EFFECT),
    )(_in_hbm(pack), _in_hbm(lax.empty((N_CHIPS, r, w), pack.dtype)), after)


def _gather_wait(send_sems, recv_sems, pack, land, after, *, name):
    def body(p_ref, land_ref, send_sems, recv_sems, after_ref, p_out, land_out):
        for cp in _gather_copies(p_ref, land_ref, send_sems, recv_sems):
            cp.wait_send()
            cp.wait_recv()

    return pl.pallas_call(
        body,
        name=name,
        out_shape=(pltpu.HBM(pack.shape, pack.dtype), pltpu.HBM(land.shape, land.dtype)),
        in_specs=(HBM, HBM, SEM, SEM, ANY),
        out_specs=(HBM, HBM),
        input_output_aliases={0: 0, 1: 1},
        compiler_params=pltpu.CompilerParams(has_side_effects=EFFECT),
    )(pack, land, send_sems, recv_sems, after)


def _gather_spread(land, *, name):
    n, r, w = land.shape
    rh = r // 2

    def body(land_ref, o_ref, send_sems, recv_sems):
        x, y, c = _place()
        rows = _half(c, rh)
        copies = []
        for k in range(1, N_CHIPS):
            cx, cy = _rel_chip(x, y, k)
            copies.append(pltpu.make_async_remote_copy(
                src_ref=land_ref.at[2 * cx + cy, rows], dst_ref=o_ref.at[2 * cx + cy, rows], send_sem=send_sems.at[k - 1],
                recv_sem=recv_sems.at[k - 1], device_id=(x, y, 1 - c), device_id_type=MESH))
        for cp in copies:
            cp.start()
        for cp in copies:
            cp.wait()

    return pl.pallas_call(
        body,
        name=name,
        in_specs=[ANY],
        out_specs=ANY,
        out_shape=jax.ShapeDtypeStruct(land.shape, land.dtype),
        input_output_aliases={0: 0},
        scratch_shapes=[pltpu.SemaphoreType.DMA((N_CHIPS - 1,)), pltpu.SemaphoreType.DMA((N_CHIPS - 1,))],
        compiler_params=pltpu.CompilerParams(has_side_effects=True),
    )(land)


N_PARTS = 2 * (N_CHIPS - 1)


def _scatter_copies(lo_ref, g_ref, land_lo_ref, land_f_ref, send_sems, recv_sems, starting):
    rh = g_ref.shape[1] // 2
    x, y, c = _place()
    copies = []
    for k in range(1, N_CHIPS):
        cx, cy = _rel_chip(x, y, k)
        for i in range(2):
            part = 2 * (k - 1) + (c if starting else i)
            copies.append(pltpu.make_async_remote_copy(
                src_ref=lo_ref.at[2 * cx + cy, pl.ds(i * rh, rh)], dst_ref=land_lo_ref.at[part],
                send_sem=send_sems.at[2 * (k - 1) + i], recv_sem=recv_sems.at[part], device_id=(cx, cy, i), device_id_type=MESH))
    copies.append(pltpu.make_async_remote_copy(
        src_ref=g_ref.at[2 * x + y, _half(1 - c, rh)], dst_ref=land_f_ref, send_sem=send_sems.at[N_PARTS], recv_sem=recv_sems.at[N_PARTS],
        device_id=(x, y, 1 - c), device_id_type=MESH))
    return copies


def _scatter_start(g_lo, g, *, name):
    n, r, w = g.shape
    rh = r // 2

    def body(lo_ref, g_ref, land_lo_ref, land_f_ref, send_sems, recv_sems, lo_thru, g_thru, land_lo_thru, land_f_thru, token):
        for cp in _scatter_copies(lo_ref, g_ref, land_lo_ref, land_f_ref, send_sems, recv_sems, True):
            cp.start()
        token[...] = jnp.zeros_like(token)

    return pl.pallas_call(
        body,
        name=name,
        out_shape=(pltpu.SemaphoreType.DMA((N_PARTS + 1,)), pltpu.SemaphoreType.DMA((N_PARTS + 1,)), pltpu.HBM(g_lo.shape, g_lo.dtype),
                   pltpu.HBM(g.shape, g.dtype), pltpu.HBM((N_PARTS, rh, w), g_lo.dtype), pltpu.HBM((rh, w), g.dtype),
                   jax.ShapeDtypeStruct((8, 128), F32)),
        in_specs=(HBM, HBM, HBM, HBM),
        out_specs=(SEM, SEM, HBM, HBM, HBM, HBM, pl.BlockSpec(memory_space=pltpu.VMEM)),
        input_output_aliases={0: 2, 1: 3, 2: 4, 3: 5},
        compiler_params=pltpu.CompilerParams(has_side_effects=EFFECT),
    )(_in_hbm(g_lo), _in_hbm(g), _in_hbm(lax.empty((N_PARTS, rh, w), g_lo.dtype)), _in_hbm(lax.empty((rh, w), g.dtype)))


def _scatter_wait(send_sems, recv_sems, g_lo, g, land_lo, land_f, after, *, name):
    def body(lo_ref, g_ref, land_lo_ref, land_f_ref, send_sems, recv_sems, after_ref, o0, o1, o2, o3):
        for cp in _scatter_copies(lo_ref, g_ref, land_lo_ref, land_f_ref, send_sems, recv_sems, False):
            cp.wait_send()
            cp.wait_recv()

    arrays = (g_lo, g, land_lo, land_f)
    return pl.pallas_call(
        body,
        name=name,
        out_shape=tuple(pltpu.HBM(a.shape, a.dtype) for a in arrays),
        in_specs=(HBM, HBM, HBM, HBM, SEM, SEM, ANY),
        out_specs=(HBM, HBM, HBM, HBM),
        input_output_aliases={0: 0, 1: 1, 2: 2, 3: 3},
        compiler_params=pltpu.CompilerParams(has_side_effects=EFFECT),
    )(*arrays, send_sems, recv_sems, after)


def _scatter_sum(g, land_lo, land_f, where, *, name):
    n, r, w = g.shape
    rh = r // 2
    tr = _pick(rh, (256, 160, 80))
    nt = rh // tr

    def body(where_ref, g_ref, f_ref, lo_ref, o_ref):
        acc = g_ref[0] + f_ref[...]
        for part in range(N_PARTS):
            acc = acc + lo_ref[part].astype(F32)
        o_ref[...] = acc

    return pl.pallas_call(
        body,
        name=name,
        grid_spec=pltpu.PrefetchScalarGridSpec(
            num_scalar_prefetch=1,
            grid=(nt,),
            in_specs=[pl.BlockSpec((1, tr, w), lambda i, wh: (wh[1], wh[0] * nt + i, 0)),
                      pl.BlockSpec((tr, w), lambda i, wh: (i, 0)),
                      pl.BlockSpec((N_PARTS, tr, w), lambda i, wh: (0, i, 0))],
            out_specs=pl.BlockSpec((tr, w), lambda i, wh: (wh[0] * nt + i, 0)),
        ),
        out_shape=jax.ShapeDtypeStruct((r, w), F32),
        compiler_params=_cp(("parallel",)),
    )(where, g, land_f, land_lo)


def _swap_all(shards, *, name):
    n = len(shards)

    def body(*refs):
        ins, outs = refs[:n], refs[n : 2 * n]
        send_sems, recv_sems = refs[2 * n :]
        x, y, c = _place()
        copies = []
        for i, (e_ref, o_ref) in enumerate(zip(ins, outs)):
            rows = _half(c, e_ref.shape[0] // 2)
            copies.append(pltpu.make_async_remote_copy(src_ref=e_ref.at[rows], dst_ref=o_ref.at[rows], send_sem=send_sems.at[i],
                                                       recv_sem=recv_sems.at[i], device_id=(x, y, 1 - c), device_id_type=MESH))
        for cp in copies:
            cp.start()
        for cp in copies:
            cp.wait()

    return pl.pallas_call(
        body,
        name=name,
        in_specs=[ANY] * n,
        out_specs=[ANY] * n,
        out_shape=[jax.ShapeDtypeStruct(e.shape, e.dtype) for e in shards],
        input_output_aliases={i: i for i in range(n)},
        scratch_shapes=[pltpu.SemaphoreType.DMA((n,)), pltpu.SemaphoreType.DMA((n,))],
        compiler_params=pltpu.CompilerParams(has_side_effects=True),
    )(*shards)


def _sum_small(small, after):
    n_dev = 8

    def body(s_ref, after_ref, o_ref, all_ref, send_sems, recv_sems):
        x, y, c = _place()
        me = 4 * x + 2 * y + c
        all_ref[me] = s_ref[...]
        copies = []
        for k in range(1, n_dev):
            cx, cy = _rel_chip(x, y, k >> 1)
            cc = 1 - c if k & 1 else c
            copies.append(pltpu.make_async_remote_copy(
                src_ref=s_ref, dst_ref=all_ref.at[me], send_sem=send_sems.at[k - 1], recv_sem=recv_sems.at[k - 1],
                device_id=(cx, cy, cc), device_id_type=MESH))
        for cp in copies:
            cp.start()
        for cp in copies:
            cp.wait()
        acc = all_ref[0]
        for a in range(1, n_dev):
            acc = acc + all_ref[a]
        o_ref[...] = acc

    vm = pl.BlockSpec(memory_space=pltpu.VMEM)
    return pl.pallas_call(
        body,
        name="sum_small",
        in_specs=[vm, ANY],
        out_specs=vm,
        out_shape=jax.ShapeDtypeStruct(small.shape, F32),
        scratch_shapes=[pltpu.VMEM((n_dev,) + small.shape, F32), pltpu.SemaphoreType.DMA((n_dev - 1,)), pltpu.SemaphoreType.DMA((n_dev - 1,))],
        compiler_params=pltpu.CompilerParams(has_side_effects=True),
    )(small, after)


MATS = {"w_in": (776, True), "w_out": (256, False), "w_xq": (256, False), "w_xkv": (512, True), "w_xo": (256, False),
        "w_up": (1024, True), "w_down": (1024, False)}
GATHER_FIRST = ("w_in",)
GATHER_REST = ("w_out", "w_xq", "w_xkv", "w_xo", "w_up", "w_down")
GRAD_GROUPS = (("w_up", "w_down"), ("w_out", "w_xq", "w_xkv", "w_xo"), ("w_in",))


def _group_rows(names):
    n = sum(MATS[name][0] for name in names)
    return n + (-n) % 32


def _pack(pieces, rows):
    p = jnp.concatenate(pieces, axis=0) if len(pieces) > 1 else pieces[0]
    return jnp.pad(p, ((0, rows - p.shape[0]), (0, 0))) if rows > p.shape[0] else p


SMALL = (
    ("mix_norm", 1024), ("conv_norm", 512), ("b_af", 256), ("b_ab", 256), ("gla_norm", 128), ("xa_norm", 1024), ("mem_norm", 1024),
    ("mlp_norm", 1024), ("final_norm", 1024), ("conv_w", 1536), ("w_af", 4096), ("w_ab", 4096), ("loss", 128),
)


def kernel(x, mem, mix_norm, w_in, conv_w, conv_norm, w_af, b_af, w_ab, b_ab, gla_norm, w_out, xa_norm, mem_norm, w_xq, w_xkv, w_xo, mlp_norm, w_up, w_down, final_norm, loss_target, m_mix_norm, m_w_in, m_conv_w, m_conv_norm, m_w_af, m_b_af, m_w_ab, m_b_ab, m_gla_norm, m_w_out, m_xa_norm, m_mem_norm, m_w_xq, m_w_xkv, m_w_xo, m_mlp_norm, m_w_up, m_w_down, m_final_norm, v_mix_norm, v_w_in, v_conv_w, v_conv_norm, v_w_af, v_b_af, v_w_ab, v_b_ab, v_gla_norm, v_w_out, v_xa_norm, v_mem_norm, v_w_xq, v_w_xkv, v_w_xo, v_mlp_norm, v_w_up, v_w_down, v_final_norm):
    given = dict(locals())
    xi, yi, ci = _place()
    chip = 2 * xi + yi
    where = jnp.stack([ci, chip]).astype(jnp.int32)

    lo = {name: (given[name][0].T if MATS[name][1] else given[name][0]).astype(_CD) for name in MATS}
    pack_rest = _pack([lo[name] for name in GATHER_REST], _group_rows(GATHER_REST))
    pack_first = _pack([lo[name] for name in GATHER_FIRST], _group_rows(GATHER_FIRST))
    xs, mems, tgt = x[0], mem[0], loss_target[0]
    behind = lambda gain, token: gain + token[0, 0]

    def placed(shard, full_shape, col):
        return lax.dynamic_update_slice(jnp.zeros(full_shape, F32), shard, (0, col)).reshape(-1, 128)

    sw = jnp.concatenate([
        placed(conv_w[0], (CONV_K, CONV_WIDTH), 128 * chip),
        placed(w_af[0], (GLA_LOWRANK, GLA_K_TOTAL), 64 * chip),
        placed(w_ab[0], (GLA_LOWRANK, GLA_K_TOTAL), 64 * chip),
    ], axis=0)
    sw = jnp.pad(sw, ((0, SMALL_ROWS - sw.shape[0]), (0, 0))) * (ci == 0).astype(F32)
    sw = _sum_small(sw, mix_norm)

    first_send, first_recv, pack_first, land_first, first_token = _gather_start(pack_first, sw, name="gather_first_start")
    rest_send, rest_recv, pack_rest, land_rest, rest_token = _gather_start(pack_rest, first_token, name="gather_rest_start")
    h1 = _rms_fwd(xs, behind(mix_norm, rest_token), name="norm_mix")
    pack_first, land_first = _gather_wait(first_send, first_recv, pack_first, land_first, h1, name="gather_first_wait")
    got_first = _gather_spread(land_first, name="gather_first_spread")

    def whole(got, off, rows):
        return got[:, off : off + rows].reshape(N_CHIPS * rows, D_MODEL)

    w_in_t = whole(got_first, 0, MATS["w_in"][0])
    w_za = jnp.concatenate([w_in_t[0:1536], w_in_t[2560:3072]], axis=0)
    w_zb = jnp.concatenate([w_in_t[1536:2560], w_in_t[3072:W_IN_COLS], jnp.zeros((ZB_COLS - 1056, D_MODEL), _CD)], axis=0)
    conv_w_full = sw[0:12].reshape(CONV_K, CONV_WIDTH)
    w_af_full = sw[12:44].reshape(GLA_LOWRANK, GLA_K_TOTAL)
    w_ab_full = sw[44:76].reshape(GLA_LOWRANK, GLA_K_TOTAL)
    waf_p = jnp.pad(w_af_full, ((0, 128 - GLA_LOWRANK), (0, 0))).astype(_CD)
    wab_p = jnp.pad(w_ab_full, ((GLA_LOWRANK, 128 - 2 * GLA_LOWRANK), (0, 0))).astype(_CD)

    z_b = _mm(h1, w_zb, mode="nt", name="proj_in_b", tn=ZB_COLS)
    z_a = _mm(h1, w_za, mode="nt", name="proj_in_a", tm=512, tn=ZA_COLS)
    b_f, b_b = _gate_fwd(z_b, waf_p, wab_p, b_af, b_ab, name="gates")
    o_f, st_f, o_b, st_b = _gla_fwd(z_b, b_f, b_b, name="gla_scan")
    y = _mix_fwd(z_a, o_f, o_b, conv_w_full, conv_norm, gla_norm, name="mix_out")
    pack_rest, land_rest = _gather_wait(rest_send, rest_recv, pack_rest, land_rest, y, name="gather_rest_wait")
    gathered = _gather_spread(land_rest, name="gather_rest_spread")
    wt, off = {}, 0
    for name in GATHER_REST:
        wt[name] = whole(gathered, off, MATS[name][0])
        off += MATS[name][0]
    x1, hx = _mm_rows(y, wt["w_out"], mode="nn", name="proj_out", rows=(xs,), vecs=(xa_norm,), out_rows=(F32, _CD),
                      epilogue=_ep_residual_norm, tm=1024)
    qx = _mm(hx, wt["w_xq"], mode="nn", name="proj_xq", out_dtypes=(_CD,))
    hmem = _rms_fwd(mems, mem_norm, name="norm_mem")
    kv = _mm(hmem, wt["w_xkv"], mode="nt", name="proj_xkv", out_dtypes=(_CD,))
    ox = _xattn_fwd(qx, kv, name="xattn")
    x2, hm = _mm_rows(ox, wt["w_xo"], mode="nn", name="proj_xo", rows=(x1,), vecs=(mlp_norm,), out_rows=(F32, _CD),
                      epilogue=_ep_residual_norm, tm=1024)
    act, relu_u = _mm(hm, wt["w_up"], mode="nt", name="mlp_up", out_dtypes=(_CD, _CD), tm=2048,
                      epilogue=lambda acc: (jnp.square(jnp.maximum(acc, 0.0)), jnp.maximum(acc, 0.0)))
    dx3, dx3_lo, loss_part, g_final_norm = _mm_rows(
        act, wt["w_down"], mode="nn", name="mlp_down", rows=(x2, tgt), vecs=(final_norm.reshape(1, D_MODEL),),
        out_rows=(F32, _CD), out_vecs=(128, D_MODEL), epilogue=_ep_loss)

    grads_t = {}

    def start_group(names, tag):
        rows = _group_rows(names)
        g = jnp.stack([_pack([grads_t[name][a * MATS[name][0] : (a + 1) * MATS[name][0]] for name in names], rows) for a in range(N_CHIPS)])
        return _scatter_start(g.astype(_TD), g, name="grads_" + tag + "_start")

    def finish_group(state, after, tag):
        send_sems, recv_sems, g_lo, g, land_lo, land_f, _ = state
        g_lo, g, land_lo, land_f = _scatter_wait(send_sems, recv_sems, g_lo, g, land_lo, land_f, after, name="grads_" + tag + "_wait")
        return _scatter_sum(g, land_lo, land_f, where, name="grads_" + tag + "_sum")

    def new_packs(names):
        shape = (N_CHIPS, _group_rows(names), D_MODEL)
        return lax.empty(shape, F32), lax.empty(shape, _TD)

    def grad_into(packs, names, which, a, b, name):
        off = sum(MATS[other][0] for other in names[: names.index(which)])
        return _mm_tn_into(a, b, packs, rows=MATS[which][0], off=off, name=name)

    du = _mm(dx3_lo, wt["w_down"], mode="nt", name="mlp_down_dx", out_dtypes=(_CD,), extras=(relu_u,), tm=2048,
             epilogue=lambda acc, rr: (acc * (2.0 * rr.astype(F32)),))
    packs = new_packs(GRAD_GROUPS[0])
    packs = grad_into(packs, GRAD_GROUPS[0], "w_down", act, dx3_lo, "mlp_down_dw")
    packs = grad_into(packs, GRAD_GROUPS[0], "w_up", du, hm, "mlp_up_dw")
    mlp_state = _scatter_start(packs[1], packs[0], name="grads_mlp_start")
    dx2, dx2_lo, g_mlp_norm = _mm_rows(
        du, wt["w_up"], mode="nn", name="mlp_up_dx", rows=(x2, dx3), vecs=(behind(mlp_norm, mlp_state[-1]),),
        out_rows=(F32, _CD), out_vecs=(D_MODEL,), epilogue=_ep_norm_bwd)
    dox = _mm(dx2_lo, wt["w_xo"], mode="nt", name="proj_xo_dx", out_dtypes=(_CD,))
    packs = new_packs(GRAD_GROUPS[1])
    packs = grad_into(packs, GRAD_GROUPS[1], "w_xo", ox, dx2_lo, "proj_xo_dw")
    dqx, dkv = _xattn_bwd(qx, kv, dox, name="xattn_bwd")
    packs = grad_into(packs, GRAD_GROUPS[1], "w_xq", hx, dqx, "proj_xq_dw")
    dx1, dx1_lo, g_xa_norm = _mm_rows(
        dqx, wt["w_xq"], mode="nt", name="proj_xq_dx", rows=(x1, dx2), vecs=(xa_norm,),
        out_rows=(F32, _CD), out_vecs=(D_MODEL,), epilogue=_ep_norm_bwd, tm=1024)
    dkv_lo = dkv.astype(_CD)
    packs = grad_into(packs, GRAD_GROUPS[1], "w_xkv", dkv_lo, hmem, "proj_xkv_dw")
    dhmem = _mm(dkv_lo, wt["w_xkv"], mode="nn", name="proj_xkv_dx")
    g_mem_norm = _rms_gain_grad(mems, dhmem, name="norm_mem_bwd")
    dy = _mm(dx1_lo, wt["w_out"], mode="nt", name="proj_out_dx")
    packs = grad_into(packs, GRAD_GROUPS[1], "w_out", y, dx1_lo, "proj_out_dw")
    attn_state = _scatter_start(packs[1], packs[0], name="grads_attn_start")
    dz_a, do, g_conv_w, g_conv_norm, g_gla_norm = _mix_bwd(z_a, o_f, o_b, dy, conv_w_full, behind(conv_norm, attn_state[-1]), gla_norm, name="mix_out_bwd")
    dqkv_f, db_f, dqkv_b, db_b = _gla_bwd(z_b, b_f, b_b, do, st_f, st_b, name="gla_scan_bwd")
    dz_b, g_waf_p, g_wab_p, g_b_af, g_b_ab = _gate_bwd(z_b, waf_p, wab_p, b_af, b_ab, db_f, db_b, dqkv_f, dqkv_b, name="gates_bwd")
    g_za = _mm_tn(dz_a, h1, name="proj_in_a_dw")
    g_zb = _mm_tn(dz_b, h1, name="proj_in_b_dw")
    grads_t["w_in"] = jnp.concatenate([g_za[0:1536], g_zb[0:1024], g_za[1536:2048], g_zb[1024:1056]], axis=0)
    in_state = start_group(GRAD_GROUPS[2], "in")
    dh1_a = _mm(dz_a, w_za, mode="nn", name="proj_in_a_dx", tm=512, tk=ZA_COLS)
    grad_x, g_mix_norm = _mm_rows(
        dz_b, w_zb, mode="nn", name="proj_in_b_dx", rows=(xs, dx1, dh1_a), vecs=(behind(mix_norm, in_state[-1]),),
        out_rows=(F32,), out_vecs=(D_MODEL,), epilogue=_ep_norm_bwd)

    half_mlp = finish_group(mlp_state, grad_x, "mlp")
    half_attn = finish_group(attn_state, half_mlp, "attn")
    half_in = finish_group(in_state, half_attn, "in")
    shard_rows = {}
    for names, rows in zip(GRAD_GROUPS, _swap_all([half_mlp, half_attn, half_in], name="shards_to_sibling")):
        off = 0
        for name in names:
            shard_rows[name] = (rows, off)
            off += MATS[name][0]

    small_vals = dict(mix_norm=g_mix_norm, conv_norm=g_conv_norm, b_af=g_b_af, b_ab=g_b_ab, gla_norm=g_gla_norm, xa_norm=g_xa_norm,
                      mem_norm=g_mem_norm, mlp_norm=g_mlp_norm, final_norm=g_final_norm, conv_w=g_conv_w,
                      w_af=g_waf_p[0:GLA_LOWRANK], w_ab=g_wab_p[GLA_LOWRANK : 2 * GLA_LOWRANK], loss=loss_part)
    small = jnp.concatenate([small_vals[name].reshape(-1, 128) for name, _ in SMALL], axis=0)
    small = _sum_small(jnp.pad(small, ((0, SMALL_ROWS - small.shape[0]), (0, 0))), loss_part)
    g_small, off = {}, 0
    for name, n in SMALL:
        g_small[name] = small[off : off + n // 128]
        off += n // 128
    loss = g_small["loss"][0, 0]
    g_small["conv_w"] = lax.dynamic_slice(g_small["conv_w"].reshape(CONV_K, CONV_WIDTH), (0, 128 * chip), (CONV_K, 128))
    g_small["w_af"] = lax.dynamic_slice(g_small["w_af"].reshape(GLA_LOWRANK, GLA_K_TOTAL), (0, 64 * chip), (GLA_LOWRANK, 64))
    g_small["w_ab"] = lax.dynamic_slice(g_small["w_ab"].reshape(GLA_LOWRANK, GLA_K_TOTAL), (0, 64 * chip), (GLA_LOWRANK, 64))

    names = ["mix_norm", "w_in", "conv_w", "conv_norm", "w_af", "b_af", "w_ab", "b_ab", "gla_norm", "w_out", "xa_norm", "mem_norm",
             "w_xq", "w_xkv", "w_xo", "mlp_norm", "w_up", "w_down", "final_norm"]
    big_names = list(MATS)
    as2d = lambda a: a.reshape(1, -1) if a.ndim == 1 else a.reshape(a.shape[-2:])
    grads, deltas, new_m, new_v = {}, {}, {}, {}
    for name in big_names:
        rows, off = shard_rows[name]
        wmv = [as2d(given[name]), as2d(given["m_" + name]), as2d(given["v_" + name])]
        as_stored = name == "w_in"
        if as_stored:
            wmv = [a.T for a in wmv]
        res = _adamw(*wmv, rows, off, transposed=MATS[name][1] and not as_stored, name="adamw_" + name)
        grads[name], deltas[name], new_m[name], new_v[name] = [a.T for a in res] if as_stored else res
    small_names = [name for name in names if name not in big_names]
    groups = []
    for name in small_names:
        grads[name] = g_small[name].reshape(as2d(given[name]).shape)
        groups.append((as2d(given[name]), grads[name], as2d(given["m_" + name]), as2d(given["v_" + name])))
    for name, res in zip(small_names, _adamw_small(groups, name="adamw_small")):
        deltas[name], new_m[name], new_v[name] = res

    like = lambda name, a: a.reshape(given[name].shape)
    return (loss, grad_x[None], *[like(n, grads[n]) for n in names], *[like(n, deltas[n]) for n in names],
            *[like(n, new_m[n]) for n in names], *[like(n, new_v[n]) for n in names])
```

```python
import jax
import jax.numpy as jnp
from jax import lax
from jax.experimental import pallas as pl
from jax.experimental.pallas import tpu as pltpu

F32 = jnp.float32
BF16 = jnp.bfloat16
_CD = jnp.bfloat16
_TD = jnp.bfloat16

D_MODEL = 1024
N_MEM = 256
CONV_WIDTH = 512
CONV_GROUP = 64
CONV_K = 3
GLA_HEADS = 4
GLA_DK = 64
GLA_DV = 128
GLA_K_TOTAL = 256
GLA_V_TOTAL = 512
GLA_LOWRANK = 16
GLA_GATE_SCALE = 1.0 / 16.0
GLA_CHUNK = 64
XA_HEADS = 4
XA_HEAD_DIM = 256
D_FF = 4096
EPS = 1e-6
W_IN_COLS = 3104
ZA_COLS = 2048
ZB_COLS = 1152
LR_COL = 1024

ADAM_LR = 0.001
ADAM_B1 = 0.9
ADAM_B2 = 0.999
ADAM_EPS = 1e-08
ADAM_WD = 0.01
ADAM_STEP = 10

N_CHIPS = 4
SMALL_ROWS = 128

_TS = 512
_VMEM = 44 * 1024 * 1024
MESH = pl.DeviceIdType.MESH
ANY = pl.BlockSpec(memory_space=pl.ANY)


def _cp(sem=None, **kw):
    return pltpu.CompilerParams(dimension_semantics=sem, vmem_limit_bytes=_VMEM, **kw)


def _dot(a, b):
    return jnp.dot(a.astype(_CD), b.astype(_CD), preferred_element_type=F32)


def _dot_nt(a, b):
    return lax.dot_general(a.astype(_CD), b.astype(_CD), (((1,), (1,)), ((), ())), preferred_element_type=F32)


def _dot_tn(a, b):
    return lax.dot_general(a.astype(_CD), b.astype(_CD), (((0,), (0,)), ((), ())), preferred_element_type=F32)


def _dot_split(x, ones):
    hi = x.astype(BF16)
    r = x - hi.astype(F32)
    mid = r.astype(BF16)
    lo = (r - mid.astype(F32)).astype(BF16)
    d = lambda p: jnp.dot(p, ones, preferred_element_type=F32)
    return d(hi) + d(mid) + d(lo)


def _pick(n, cands=(1024, 640, 512, 256, 128)):
    for t in cands:
        if n % t == 0:
            return t
    return n


def _rows(s, light=False, times=2):
    return min(times * _TS if light else _TS, s)


def _sigmoid(v):
    e = jnp.exp(-jnp.abs(v))
    return jnp.where(v >= 0, 1.0 / (1.0 + e), e / (1.0 + e))


def _mm(a, b, *, mode, name, out_dtypes=(F32,), extras=(), epilogue=None, tm=None, tn=None, tk=None):
    m, k = a.shape
    n = b.shape[1] if mode == "nn" else b.shape[0]
    tm = min(m, tm or 1024)
    tn = tn or _pick(n)
    tk = tk or _pick(k)
    nk = k // tk
    n_ex, n_out = len(extras), len(out_dtypes)

    def body(*refs):
        a_ref, b_ref = refs[:2]
        ex = refs[2 : 2 + n_ex]
        outs = refs[2 + n_ex : 2 + n_ex + n_out]
        part = _dot(a_ref[...], b_ref[...]) if mode == "nn" else _dot_nt(a_ref[...], b_ref[...])

        def finish(acc):
            res = epilogue(acc, *[e[...] for e in ex]) if epilogue else (acc,)
            for o, r in zip(outs, res):
                o[...] = r.astype(o.dtype)

        if nk == 1:
            finish(part)
        else:
            acc_ref = refs[-1]
            kk = pl.program_id(2)

            @pl.when(kk == 0)
            def _():
                acc_ref[...] = part

            @pl.when(kk > 0)
            def _():
                acc_ref[...] += part

            @pl.when(kk == nk - 1)
            def _():
                finish(acc_ref[...])

    b_spec = pl.BlockSpec((tk, tn), lambda i, j, kk: (kk, j)) if mode == "nn" else pl.BlockSpec((tn, tk), lambda i, j, kk: (j, kk))
    tile = pl.BlockSpec((tm, tn), lambda i, j, kk: (i, j))
    out = pl.pallas_call(
        body,
        name=name,
        grid=(m // tm, n // tn, nk),
        in_specs=[pl.BlockSpec((tm, tk), lambda i, j, kk: (i, kk)), b_spec] + [tile] * n_ex,
        out_specs=[tile] * n_out,
        out_shape=[jax.ShapeDtypeStruct((m, n), dt) for dt in out_dtypes],
        scratch_shapes=[pltpu.VMEM((tm, tn), F32)] if nk > 1 else [],
        compiler_params=_cp(("parallel", "parallel", "arbitrary")),
    )(a, b, *extras)
    return out[0] if n_out == 1 else out


def _mm_tn(a, b, *, name):
    s, m = a.shape
    n = b.shape[1]
    cap = max(128, (1 << 20) // n)
    tm = _pick(m, tuple(t for t in (512, 640, 384, 256, 128) if t <= max(cap, 128)))
    ts = min(s, 1 << (((1 << 22) // n).bit_length() - 1))
    ns = s // ts

    def body(a_ref, b_ref, o_ref):
        part = _dot_tn(a_ref[...], b_ref[...])
        if ns == 1:
            o_ref[...] = part
        else:
            ss = pl.program_id(1)

            @pl.when(ss == 0)
            def _():
                o_ref[...] = part

            @pl.when(ss > 0)
            def _():
                o_ref[...] += part

    return pl.pallas_call(
        body,
        name=name,
        grid=(m // tm, ns),
        in_specs=[pl.BlockSpec((ts, tm), lambda i, ss: (ss, i)), pl.BlockSpec((ts, n), lambda i, ss: (ss, 0))],
        out_specs=pl.BlockSpec((tm, n), lambda i, ss: (i, 0)),
        out_shape=jax.ShapeDtypeStruct((m, n), F32),
        compiler_params=_cp(("parallel", "arbitrary")),
    )(a, b)


def _mm_tn_into(a, b, packs, *, rows, off, name):
    s, m = a.shape
    n = b.shape[1]
    tm = 1024 if rows % 1024 == 0 and s >= 4096 else 512
    tr = min(tm, rows)
    per, chips = rows // tr, tm // tr
    ts = min(s, (1 << (((1 << 22) // n).bit_length() - 1)) * 512 // tm)
    ns = s // ts

    def body(a_ref, b_ref, f_in, lo_in, f_ref, lo_ref):
        part = _dot_tn(a_ref[...], b_ref[...])
        pieces = [part[c * tr : (c + 1) * tr] for c in range(chips)]
        if ns == 1:
            for c, p in enumerate(pieces):
                f_ref[c] = p
                lo_ref[c] = p.astype(lo_ref.dtype)
        else:
            ss = pl.program_id(1)

            @pl.when(ss == 0)
            def _():
                for c, p in enumerate(pieces):
                    f_ref[c] = p

            @pl.when(ss > 0)
            def _():
                for c, p in enumerate(pieces):
                    f_ref[c] += p

            @pl.when(ss == ns - 1)
            def _():
                lo_ref[...] = f_ref[...].astype(lo_ref.dtype)

    spec = pl.BlockSpec((chips, tr, n), lambda i, ss: (i // per, off // tr + i % per, 0))
    return pl.pallas_call(
        body,
        name=name,
        grid=(m // tm, ns),
        in_specs=[pl.BlockSpec((ts, tm), lambda i, ss: (ss, i)), pl.BlockSpec((ts, n), lambda i, ss: (ss, 0)), ANY, ANY],
        out_specs=[spec, spec],
        out_shape=[jax.ShapeDtypeStruct(p.shape, p.dtype) for p in packs],
        input_output_aliases={2: 0, 3: 1},
        compiler_params=_cp(("parallel", "arbitrary")),
    )(a, b, *packs)


def _mm_rows(a, b, *, mode, name, more=(), rows=(), vecs=(), out_rows=(), out_vecs=(), epilogue, tm=512):
    m, k = a.shape
    n = b.shape[1] if mode == "nn" else b.shape[0]
    tm = min(m, tm)
    parts = 2 if tm % 256 == 0 else 1
    n_m, n_r, n_v, n_or, n_ov = 2 * len(more), len(rows), len(vecs), len(out_rows), len(out_vecs)

    def body(*refs):
        a_ref, b_ref = refs[:2]
        m_refs = refs[2 : 2 + n_m]
        rest = refs[2 + n_m :]
        r_refs = rest[:n_r]
        v_refs = rest[n_r : n_r + n_v]
        or_refs = rest[n_r + n_v : n_r + n_v + n_or]
        ov_refs = rest[n_r + n_v + n_or :]
        res_vecs = None
        for p in range(parts):
            rs = slice(p * tm // parts, (p + 1) * tm // parts)
            acc = _dot(a_ref[rs, :], b_ref[...]) if mode == "nn" else _dot_nt(a_ref[rs, :], b_ref[...])
            for a2_ref, b2_ref in zip(m_refs[0::2], m_refs[1::2]):
                acc = acc + _dot(a2_ref[rs, :], b2_ref[...])
            res_rows, part_vecs = epilogue(acc, [r[rs, :] for r in r_refs], [v[...] for v in v_refs])
            for o, r in zip(or_refs, res_rows):
                o[rs, :] = r.astype(o.dtype)
            res_vecs = part_vecs if res_vecs is None else [s + t for s, t in zip(res_vecs, part_vecs)]
        if n_ov:
            first = pl.program_id(0) == 0

            @pl.when(first)
            def _():
                for o, r in zip(ov_refs, res_vecs):
                    o[...] = r

            @pl.when(jnp.logical_not(first))
            def _():
                for o, r in zip(ov_refs, res_vecs):
                    o[...] += r

    tile = pl.BlockSpec((tm, n), lambda i: (i, 0))
    whole = lambda arr: pl.BlockSpec(arr.shape, lambda i: (0, 0))
    vec = lambda w: pl.BlockSpec((1, w), lambda i: (0, 0))
    out = pl.pallas_call(
        body,
        name=name,
        grid=(m // tm,),
        in_specs=[pl.BlockSpec((tm, k), lambda i: (i, 0)), whole(b)]
        + [spec for a2, b2 in more for spec in (pl.BlockSpec((tm, a2.shape[1]), lambda i: (i, 0)), whole(b2))]
        + [tile] * n_r + [vec(v.shape[1]) for v in vecs],
        out_specs=[tile] * n_or + [vec(w) for w in out_vecs],
        out_shape=[jax.ShapeDtypeStruct((m, n), dt) for dt in out_rows] + [jax.ShapeDtypeStruct((1, w), F32) for w in out_vecs],
        compiler_params=_cp(("arbitrary",) if n_ov else ("parallel",)),
    )(a, b, *[x for pair in more for x in pair], *rows, *vecs)
    return out


def _ep_residual_norm(acc, rows, vecs):
    x = acc + rows[0]
    r = lax.rsqrt(jnp.mean(x * x, axis=-1, keepdims=True) + EPS)
    return [x, x * r * vecs[0]], []


def _ep_norm_bwd(acc, rows, vecs):
    dy = acc
    for extra in rows[2:]:
        dy = dy + extra
    x, dres = rows[0], rows[1]
    r = lax.rsqrt(jnp.mean(x * x, axis=-1, keepdims=True) + EPS)
    xh = x * r
    dxh = dy * vecs[0]
    dx = r * (dxh - xh * jnp.mean(dxh * xh, axis=-1, keepdims=True)) + dres
    return [dx, dx], [jnp.sum(dy * xh, axis=0, keepdims=True)]


def _ep_loss(acc, rows, vecs):
    x = acc + rows[0]
    d = x.shape[-1]
    r = lax.rsqrt(jnp.mean(x * x, axis=-1, keepdims=True) + EPS)
    xh = x * r
    err = xh * vecs[0] - rows[1]
    loss = jnp.zeros((1, 128), F32) + 0.5 * jnp.sum(jnp.mean(err * err, axis=-1, keepdims=True))
    dy = err * (1.0 / d)
    dxh = dy * vecs[0]
    dx = r * (dxh - xh * jnp.mean(dxh * xh, axis=-1, keepdims=True))
    return [dx, dx], [loss, jnp.sum(dy * xh, axis=0, keepdims=True)]


def _rms_fwd(x, g, *, name):
    s, d = x.shape
    ts = _rows(s, light=True, times=4)

    def body(x_ref, g_ref, o_ref):
        xf = x_ref[...]
        r = lax.rsqrt(jnp.mean(xf * xf, axis=-1, keepdims=True) + EPS)
        o_ref[...] = (xf * r * g_ref[...]).astype(o_ref.dtype)

    return pl.pallas_call(
        body,
        name=name,
        grid=(s // ts,),
        in_specs=[pl.BlockSpec((ts, d), lambda i: (i, 0)), pl.BlockSpec((1, d), lambda i: (0, 0))],
        out_specs=pl.BlockSpec((ts, d), lambda i: (i, 0)),
        out_shape=jax.ShapeDtypeStruct((s, d), _CD),
        compiler_params=_cp(("parallel",)),
    )(x, g)


def _rms_gain_grad(x, dy, *, name):
    s, d = x.shape
    ts = _rows(s)

    def body(x_ref, dy_ref, dg_ref):
        xf = x_ref[...]
        r = lax.rsqrt(jnp.mean(xf * xf, axis=-1, keepdims=True) + EPS)
        part = jnp.sum(dy_ref[...] * (xf * r), axis=0, keepdims=True)

        @pl.when(pl.program_id(0) == 0)
        def _():
            dg_ref[...] = part

        @pl.when(pl.program_id(0) > 0)
        def _():
            dg_ref[...] += part

    tile = pl.BlockSpec((ts, d), lambda i: (i, 0))
    return pl.pallas_call(
        body,
        name=name,
        grid=(s // ts,),
        in_specs=[tile, tile],
        out_specs=pl.BlockSpec((1, d), lambda i: (0, 0)),
        out_shape=jax.ShapeDtypeStruct((1, d), F32),
        compiler_params=_cp(("arbitrary",)),
    )(x, dy)


def _chunk_scan(v, row_in_chunk, suffix):
    t = v.shape[0]
    step = 1
    while step < GLA_CHUNK:
        if suffix:
            v = v + jnp.where(row_in_chunk < GLA_CHUNK - step, pltpu.roll(v, t - step, 0), 0.0)
        else:
            v = v + jnp.where(row_in_chunk >= step, pltpu.roll(v, step, 0), 0.0)
        step *= 2
    return v


def _gate_pre(lr, w_ref, b_ref):
    return _dot(lr, w_ref[...]) + b_ref[...]


def _gate_fwd(z, waf, wab, baf, bab, *, name):
    s = z.shape[0]
    ts = _rows(s, light=True, times=4)

    def body(lr_ref, waf_ref, wab_ref, baf_ref, bab_ref, bf_ref, bb_ref):
        lr = lr_ref[...]
        ric = lax.broadcasted_iota(jnp.int32, (ts, GLA_K_TOTAL), 0) & (GLA_CHUNK - 1)
        for w_ref, b_ref, o_ref, suffix in ((waf_ref, baf_ref, bf_ref, False), (wab_ref, bab_ref, bb_ref, True)):
            pre = _gate_pre(lr, w_ref, b_ref)
            la = (jnp.minimum(pre, 0.0) - jnp.log(1.0 + jnp.exp(-jnp.abs(pre)))) * GLA_GATE_SCALE
            o_ref[...] = _chunk_scan(la, ric, suffix)

    wspec = pl.BlockSpec((128, GLA_K_TOTAL), lambda i: (0, 0))
    bspec = pl.BlockSpec((1, GLA_K_TOTAL), lambda i: (0, 0))
    tile = pl.BlockSpec((ts, GLA_K_TOTAL), lambda i: (i, 0))
    return pl.pallas_call(
        body,
        name=name,
        grid=(s // ts,),
        in_specs=[pl.BlockSpec((ts, 128), lambda i: (i, LR_COL // 128)), wspec, wspec, bspec, bspec],
        out_specs=[tile, tile],
        out_shape=[jax.ShapeDtypeStruct((s, GLA_K_TOTAL), F32)] * 2,
        compiler_params=_cp(("parallel",)),
    )(z, waf, wab, baf, bab)


def _gate_bwd(z, waf, wab, baf, bab, dbf, dbb, dqkv_f, dqkv_b, *, name):
    s = z.shape[0]
    ts = _rows(s, light=True)

    def body(lr_ref, waf_ref, wab_ref, baf_ref, bab_ref, dbf_ref, dbb_ref, gf_ref, gb_ref, dzb_ref, dwf_ref, dwb_ref, dbaf_ref, dbab_ref):
        lr = lr_ref[...]
        ric = lax.broadcasted_iota(jnp.int32, (ts, GLA_K_TOTAL), 0) & (GLA_CHUNK - 1)
        first = pl.program_id(0) == 0
        dlr = None
        for w_ref, b_ref, db_ref, dw_ref, dbias_ref, suffix in (
            (waf_ref, baf_ref, dbf_ref, dwf_ref, dbaf_ref, True),
            (wab_ref, bab_ref, dbb_ref, dwb_ref, dbab_ref, False),
        ):
            pre = _gate_pre(lr, w_ref, b_ref)
            dla = _chunk_scan(db_ref[...], ric, suffix)
            dpre = dla * GLA_GATE_SCALE * _sigmoid(-pre)
            part = _dot_nt(dpre, w_ref[...])
            dlr = part if dlr is None else dlr + part
            dw = _dot_tn(lr, dpre)
            dbias = jnp.sum(dpre, axis=0, keepdims=True)

            @pl.when(first)
            def _():
                dw_ref[...] = dw
                dbias_ref[...] = dbias

            @pl.when(jnp.logical_not(first))
            def _():
                dw_ref[...] += dw
                dbias_ref[...] += dbias

        dzb_ref[...] = jnp.concatenate([gf_ref[...] + gb_ref[...], dlr], axis=1).astype(dzb_ref.dtype)

    wspec = pl.BlockSpec((128, GLA_K_TOTAL), lambda i: (0, 0))
    bspec = pl.BlockSpec((1, GLA_K_TOTAL), lambda i: (0, 0))
    tile = pl.BlockSpec((ts, GLA_K_TOTAL), lambda i: (i, 0))
    wide = pl.BlockSpec((ts, 2 * GLA_K_TOTAL + GLA_V_TOTAL), lambda i: (i, 0))
    return pl.pallas_call(
        body,
        name=name,
        grid=(s // ts,),
        in_specs=[pl.BlockSpec((ts, 128), lambda i: (i, LR_COL // 128)), wspec, wspec, bspec, bspec, tile, tile, wide, wide],
        out_specs=[pl.BlockSpec((ts, ZB_COLS), lambda i: (i, 0)), wspec, wspec, bspec, bspec],
        out_shape=[
            jax.ShapeDtypeStruct((s, ZB_COLS), _CD),
            jax.ShapeDtypeStruct((128, GLA_K_TOTAL), F32),
            jax.ShapeDtypeStruct((128, GLA_K_TOTAL), F32),
            jax.ShapeDtypeStruct((1, GLA_K_TOTAL), F32),
            jax.ShapeDtypeStruct((1, GLA_K_TOTAL), F32),
        ],
        compiler_params=_cp(("arbitrary",)),
    )(z, waf, wab, baf, bab, dbf, dbb, dqkv_f, dqkv_b)


def _gla_masks(rev):
    lane_head = lax.broadcasted_iota(jnp.int32, (1, GLA_K_TOTAL), 1) >> 6
    head_masks = [lane_head == h for h in range(GLA_HEADS)]
    t = lax.broadcasted_iota(jnp.int32, (GLA_HEADS * GLA_CHUNK, GLA_CHUNK), 0) & (GLA_CHUNK - 1)
    u = lax.broadcasted_iota(jnp.int32, (GLA_HEADS * GLA_CHUNK, GLA_CHUNK), 1)
    tri = (u > t) if rev else (u <= t)
    row = lax.broadcasted_iota(jnp.int32, (GLA_CHUNK, GLA_K_TOTAL), 0)
    total_row = row == (0 if rev else GLA_CHUNK - 1)
    return head_masks, tri, total_row


def _spread(a, head_masks):
    return jnp.concatenate([jnp.where(m, a, 0.0) for m in head_masks], axis=0)


def _stack(a):
    return jnp.concatenate([a[:, GLA_DV * h : GLA_DV * (h + 1)] for h in range(GLA_HEADS)], axis=0)


def _unstack(a):
    return jnp.concatenate([a[GLA_CHUNK * h : GLA_CHUNK * (h + 1)] for h in range(GLA_HEADS)], axis=1)


def _collect(a, head_masks):
    out = None
    for h, m in enumerate(head_masks):
        part = jnp.where(m, a[GLA_CHUNK * h : GLA_CHUNK * (h + 1)], 0.0)
        out = part if out is None else out + part
    return out


def _gla_chunk_terms(q_ref, k_ref, v_ref, b_ref, rows, head_masks, tri, total_row):
    q = q_ref[rows, :] * (GLA_DK**-0.5)
    k = k_ref[rows, :]
    v = v_ref[rows, :]
    b = b_ref[rows, :]
    eb = jnp.exp(b)
    enb = jnp.exp(-b)
    g = jnp.sum(jnp.where(total_row, b, 0.0), axis=0, keepdims=True)
    egb = jnp.exp(g - b)
    qt = q * eb
    kt = k * enb
    kh = k * egb
    q_heads = _spread(qt, head_masks)
    attn = jnp.where(tri, _dot_nt(q_heads, kt), 0.0)
    return v, eb, enb, egb, jnp.exp(g), qt, kt, kh, q_heads, attn


def _gla_specs(s, tb, rev_blocks):
    nb = s // tb
    rb = (lambda i: nb - 1 - i) if rev_blocks else (lambda i: i)
    q_spec = pl.BlockSpec((tb, GLA_K_TOTAL), lambda i: (rb(i), 0))
    k_spec = pl.BlockSpec((tb, GLA_K_TOTAL), lambda i: (rb(i), 1))
    v_spec = pl.BlockSpec((tb, GLA_V_TOTAL), lambda i: (rb(i), 1))
    b_spec = pl.BlockSpec((tb, GLA_K_TOTAL), lambda i: (rb(i), 0))
    o_spec = pl.BlockSpec((tb, GLA_V_TOTAL), lambda i: (rb(i), 0))
    st_spec = pl.BlockSpec((tb // GLA_CHUNK, GLA_DV, GLA_K_TOTAL), lambda i: (rb(i), 0, 0))
    return nb, q_spec, k_spec, v_spec, b_spec, o_spec, st_spec


def _gla_fwd_chunk(cidx, q_ref, k_ref, v_ref, b_ref, o_ref, sv_ref, st_ref, masks):
    head_masks, tri, total_row = masks
    rows = pl.ds(pl.multiple_of(cidx * GLA_CHUNK, GLA_CHUNK), GLA_CHUNK)
    v, _, _, _, eg, _, _, kh, q_heads, attn = _gla_chunk_terms(q_ref, k_ref, v_ref, b_ref, rows, head_masks, tri, total_row)
    o = jnp.concatenate(
        [_dot(attn[GLA_CHUNK * h : GLA_CHUNK * (h + 1)], v[:, GLA_DV * h : GLA_DV * (h + 1)]) for h in range(GLA_HEADS)], axis=1
    )
    st = st_ref[...]
    o_ref[rows, :] = o + _unstack(_dot_nt(q_heads, st))
    sv_ref[cidx] = st
    st_ref[...] = st * eg + _dot_tn(_stack(v), _spread(kh, head_masks))


def _gla_fwd(z, b_f, b_b, *, name):
    s = z.shape[0]
    tb = _rows(s)
    cpb = tb // GLA_CHUNK
    nb, qf, kf, vf, bf, of, sf = _gla_specs(s, tb, False)
    _, qr, kr, vr, br, orr, sr = _gla_specs(s, tb, True)

    def body(qf_ref, kf_ref, vf_ref, bf_ref, qr_ref, kr_ref, vr_ref, br_ref, of_ref, svf_ref, or_ref, svr_ref, stf_ref, str_ref):
        masks_f, masks_r = _gla_masks(False), _gla_masks(True)

        @pl.when(pl.program_id(0) == 0)
        def _():
            stf_ref[...] = jnp.zeros_like(stf_ref)
            str_ref[...] = jnp.zeros_like(str_ref)

        def chunk(ci, carry):
            _gla_fwd_chunk(ci, qf_ref, kf_ref, vf_ref, bf_ref, of_ref, svf_ref, stf_ref, masks_f)
            _gla_fwd_chunk(cpb - 1 - ci, qr_ref, kr_ref, vr_ref, br_ref, or_ref, svr_ref, str_ref, masks_r)
            return carry

        lax.fori_loop(0, cpb, chunk, 0)

    o_shape = jax.ShapeDtypeStruct((s, GLA_V_TOTAL), F32)
    st_shape = jax.ShapeDtypeStruct((s // GLA_CHUNK, GLA_DV, GLA_K_TOTAL), F32)
    return pl.pallas_call(
        body,
        name=name,
        grid=(nb,),
        in_specs=[qf, kf, vf, bf, qr, kr, vr, br],
        out_specs=[of, sf, orr, sr],
        out_shape=[o_shape, st_shape, o_shape, st_shape],
        scratch_shapes=[pltpu.VMEM((GLA_DV, GLA_K_TOTAL), F32)] * 2,
        compiler_params=_cp(("arbitrary",)),
    )(z, z, z, b_f, z, z, z, b_b)


def _gla_bwd_chunk(cidx, q_ref, k_ref, v_ref, b_ref, do_ref, sv_ref, dqkv_ref, db_ref, dst_ref, masks):
    head_masks, tri, total_row = masks
    rows = pl.ds(pl.multiple_of(cidx * GLA_CHUNK, GLA_CHUNK), GLA_CHUNK)
    v, eb, enb, egb, eg, qt, kt, kh, q_heads, attn = _gla_chunk_terms(q_ref, k_ref, v_ref, b_ref, rows, head_masks, tri, total_row)
    do_c = do_ref[rows, :]
    st = sv_ref[cidx]
    dst = dst_ref[...]
    do_s, v_s = _stack(do_c), _stack(v)
    hs = lambda a, h: a[GLA_CHUNK * h : GLA_CHUNK * (h + 1)]
    vs = lambda a, h: a[:, GLA_DV * h : GLA_DV * (h + 1)]
    dattn = jnp.concatenate([_dot_nt(vs(do_c, h), vs(v, h)) for h in range(GLA_HEADS)], axis=0)
    dattn = jnp.where(tri, dattn, 0.0)
    dv = jnp.concatenate([_dot_tn(hs(attn, h), vs(do_c, h)) for h in range(GLA_HEADS)], axis=1)
    dv = dv + _unstack(_dot_nt(_spread(kh, head_masks), dst))
    dqt = _collect(_dot(do_s, st) + _dot(dattn, kt), head_masks)
    dkt = _dot_tn(dattn, q_heads)
    dkh = _collect(_dot(v_s, dst), head_masks)
    dg = jnp.sum(dkh * kh, axis=0, keepdims=True) + jnp.sum(dst * st, axis=0, keepdims=True) * eg
    db = dqt * qt - dkt * kt - dkh * kh + jnp.where(total_row, dg, 0.0)
    dq = dqt * eb * (GLA_DK**-0.5)
    dk = dkt * enb + dkh * egb
    dqkv_ref[rows, :] = jnp.concatenate([dq, dk, dv], axis=1)
    db_ref[rows, :] = db
    dst_ref[...] = dst * eg + _dot_tn(do_s, q_heads)


def _gla_bwd(z, b_f, b_b, do, st_f, st_b, *, name):
    s = z.shape[0]
    tb = _rows(s)
    cpb = tb // GLA_CHUNK
    wide = 2 * GLA_K_TOTAL + GLA_V_TOTAL
    nb, qf, kf, vf, bf, of, sf = _gla_specs(s, tb, True)
    _, qr, kr, vr, br, orr, sr = _gla_specs(s, tb, False)
    gf = pl.BlockSpec((tb, wide), lambda i: (nb - 1 - i, 0))
    gr = pl.BlockSpec((tb, wide), lambda i: (i, 0))

    def body(qf_ref, kf_ref, vf_ref, bf_ref, dof_ref, svf_ref, qr_ref, kr_ref, vr_ref, br_ref, dor_ref, svr_ref,
             gf_ref, dbf_ref, gr_ref, dbr_ref, dstf_ref, dstr_ref):
        masks_f, masks_r = _gla_masks(False), _gla_masks(True)

        @pl.when(pl.program_id(0) == 0)
        def _():
            dstf_ref[...] = jnp.zeros_like(dstf_ref)
            dstr_ref[...] = jnp.zeros_like(dstr_ref)

        def chunk(ci, carry):
            _gla_bwd_chunk(cpb - 1 - ci, qf_ref, kf_ref, vf_ref, bf_ref, dof_ref, svf_ref, gf_ref, dbf_ref, dstf_ref, masks_f)
            _gla_bwd_chunk(ci, qr_ref, kr_ref, vr_ref, br_ref, dor_ref, svr_ref, gr_ref, dbr_ref, dstr_ref, masks_r)
            return carry

        lax.fori_loop(0, cpb, chunk, 0)

    g_shape = jax.ShapeDtypeStruct((s, wide), F32)
    db_shape = jax.ShapeDtypeStruct((s, GLA_K_TOTAL), F32)
    return pl.pallas_call(
        body,
        name=name,
        grid=(nb,),
        in_specs=[qf, kf, vf, bf, of, sf, qr, kr, vr, br, orr, sr],
        out_specs=[gf, bf, gr, br],
        out_shape=[g_shape, db_shape, g_shape, db_shape],
        scratch_shapes=[pltpu.VMEM((GLA_DV, GLA_K_TOTAL), F32)] * 2,
        compiler_params=_cp(("arbitrary",)),
    )(z, z, z, b_f, do, st_f, z, z, z, b_b, do, st_b)


HALO = 8


def _halo_specs(s, ts, width, col):
    last = s // HALO - 1
    per = ts // HALO
    prev = pl.BlockSpec((HALO, width), lambda i: (jnp.maximum(i * per - 1, 0), col))
    nxt = pl.BlockSpec((HALO, width), lambda i: (jnp.minimum((i + 1) * per, last), col))
    return prev, nxt


def _group_ones():
    group = jnp.arange(CONV_WIDTH, dtype=jnp.int32) // CONV_GROUP
    return (group[:, None] == group[None, :]).astype(BF16)


_ONES_SPEC = pl.BlockSpec((CONV_WIDTH, CONV_WIDTH), lambda i: (0, 0))


def _conv_terms(cc_ext, cu_ext, cw, valid):
    n = cc_ext.shape[0]
    hc = jnp.where(valid, cc_ext * cu_ext, 0.0)
    hc_prev = pltpu.roll(hc, 1, 0)
    hc_next = pltpu.roll(hc, n - 1, 0)
    conv = cw[0:1] * hc_prev + cw[1:2] * hc + cw[2:3] * hc_next
    return hc, hc_prev, hc_next, conv


def _ext(prev_ref, cur_ref, next_ref):
    return jnp.concatenate([prev_ref[...], cur_ref[...], next_ref[...]], axis=0)


def _valid_rows(ts, s):
    row = lax.broadcasted_iota(jnp.int32, (ts + 2 * HALO, 1), 0) + (pl.program_id(0) * ts - HALO)
    return (row >= 0) & (row < s)


def _head_norm(o, gn):
    out = []
    for h in range(GLA_HEADS):
        oh = o[:, GLA_DV * h : GLA_DV * (h + 1)]
        r = lax.rsqrt(jnp.mean(oh * oh, axis=-1, keepdims=True) + EPS)
        out.append((oh * r, r))
    return out


def _mix_fwd(z, o_f, o_b, conv_w, conv_norm, gla_norm, *, name):
    s = z.shape[0]
    ts = _rows(s, light=True)
    cprev, cnext = _halo_specs(s, ts, CONV_WIDTH, 1)
    uprev, unext = _halo_specs(s, ts, CONV_WIDTH, 2)

    def body(cb_ref, cc_ref, cu_ref, ccp_ref, ccn_ref, cup_ref, cun_ref, g_ref, of_ref, ob_ref, cw_ref, cn_ref, gn_ref, ones_ref, y_ref):
        valid = _valid_rows(ts, s)
        _, _, _, conv = _conv_terms(_ext(ccp_ref, cc_ref, ccn_ref), _ext(cup_ref, cu_ref, cun_ref), cw_ref[...], valid)
        yc = cb_ref[...] * conv[HALO : HALO + ts]
        ms = _dot_split(yc * yc, ones_ref[...]) * (1.0 / CONV_GROUP)
        y_conv = yc * lax.rsqrt(ms + EPS) * cn_ref[...]
        gate = g_ref[...]
        silu = gate * _sigmoid(gate)
        gn = gn_ref[...]
        y_gla = jnp.concatenate([oh * gn for oh, _ in _head_norm(of_ref[...] + ob_ref[...], gn)], axis=1) * silu
        y_ref[...] = jnp.concatenate([y_conv, y_gla], axis=1).astype(y_ref.dtype)

    col = lambda c, w=CONV_WIDTH: pl.BlockSpec((ts, w), lambda i: (i, c))
    return pl.pallas_call(
        body,
        name=name,
        grid=(s // ts,),
        in_specs=[col(0), col(1), col(2), cprev, cnext, uprev, unext, col(3), col(0), col(0),
                  pl.BlockSpec((CONV_K, CONV_WIDTH), lambda i: (0, 0)), pl.BlockSpec((1, CONV_WIDTH), lambda i: (0, 0)),
                  pl.BlockSpec((1, GLA_DV), lambda i: (0, 0)), _ONES_SPEC],
        out_specs=pl.BlockSpec((ts, D_MODEL), lambda i: (i, 0)),
        out_shape=jax.ShapeDtypeStruct((s, D_MODEL), _CD),
        compiler_params=_cp(("parallel",)),
    )(z, z, z, z, z, z, z, z, o_f, o_b, conv_w, conv_norm, gla_norm, _group_ones())


def _mix_bwd(z, o_f, o_b, dy, conv_w, conv_norm, gla_norm, *, name):
    s = z.shape[0]
    ts = _rows(s)
    halos = [_halo_specs(s, ts, CONV_WIDTH, c) for c in (0, 1, 2)]
    dprev, dnext = _halo_specs(s, ts, CONV_WIDTH, 0)

    def body(cb_ref, cc_ref, cu_ref, cbp_ref, cbn_ref, ccp_ref, ccn_ref, cup_ref, cun_ref, g_ref, of_ref, ob_ref,
             dyc_ref, dyg_ref, dyp_ref, dyn_ref, cw_ref, cn_ref, gn_ref, ones_ref, dza_ref, do_ref, dcw_ref, dcn_ref, dgn_ref):
        n = ts + 2 * HALO
        valid = _valid_rows(ts, s)
        cw = cw_ref[...]
        cn = cn_ref[...]
        ones = ones_ref[...]
        cb = _ext(cbp_ref, cb_ref, cbn_ref)
        cc = _ext(ccp_ref, cc_ref, ccn_ref)
        cu = _ext(cup_ref, cu_ref, cun_ref)
        dy = _ext(dyp_ref, dyc_ref, dyn_ref)
        hc, hc_prev, hc_next, conv = _conv_terms(cc, cu, cw, valid)
        yc = cb * conv
        r = lax.rsqrt(_dot_split(yc * yc, ones) * (1.0 / CONV_GROUP) + EPS)
        yh = yc * r
        dyh = dy * cn
        dyc = r * (dyh - yh * (_dot_split(dyh * yh, ones) * (1.0 / CONV_GROUP)))
        dconv = jnp.where(valid, dyc * cb, 0.0)
        dhc = cw[0:1] * pltpu.roll(dconv, n - 1, 0) + cw[1:2] * dconv + cw[2:3] * pltpu.roll(dconv, 1, 0)
        mid = lambda a: a[HALO : HALO + ts]
        dza_ref[:, 0 : 3 * CONV_WIDTH] = jnp.concatenate([mid(dyc * conv), mid(dhc * cu), mid(dhc * cc)], axis=1).astype(dza_ref.dtype)
        dconv_m = mid(dconv)
        colsum = lambda a: jnp.sum(a, axis=0, keepdims=True)
        dcw = jnp.concatenate([colsum(dconv_m * mid(hc_prev)), colsum(dconv_m * mid(hc)), colsum(dconv_m * mid(hc_next))], axis=0)
        dcn = colsum(mid(dy * yh))

        gate = g_ref[...]
        sg = _sigmoid(gate)
        silu = gate * sg
        gn = gn_ref[...]
        dyg = dyg_ref[...]
        don = dyg * silu
        heads = _head_norm(of_ref[...] + ob_ref[...], gn)
        on = jnp.concatenate([oh * gn for oh, _ in heads], axis=1)
        dza_ref[:, 3 * CONV_WIDTH : ZA_COLS] = (dyg * on * (sg * (1.0 + gate * (1.0 - sg)))).astype(dza_ref.dtype)
        dgn = jnp.zeros((1, GLA_DV), F32)
        dos = []
        for h, (oh, rh) in enumerate(heads):
            donh = don[:, GLA_DV * h : GLA_DV * (h + 1)]
            dgn = dgn + colsum(donh * oh)
            doh = donh * gn
            dos.append(rh * (doh - oh * jnp.mean(doh * oh, axis=-1, keepdims=True)))
        do_ref[...] = jnp.concatenate(dos, axis=1)

        first = pl.program_id(0) == 0

        @pl.when(first)
        def _():
            dcw_ref[...] = dcw
            dcn_ref[...] = dcn
            dgn_ref[...] = dgn

        @pl.when(jnp.logical_not(first))
        def _():
            dcw_ref[...] += dcw
            dcn_ref[...] += dcn
            dgn_ref[...] += dgn

    col = lambda c, w=CONV_WIDTH: pl.BlockSpec((ts, w), lambda i: (i, c))
    cw_spec = pl.BlockSpec((CONV_K, CONV_WIDTH), lambda i: (0, 0))
    cn_spec = pl.BlockSpec((1, CONV_WIDTH), lambda i: (0, 0))
    gn_spec = pl.BlockSpec((1, GLA_DV), lambda i: (0, 0))
    return pl.pallas_call(
        body,
        name=name,
        grid=(s // ts,),
        in_specs=[col(0), col(1), col(2), halos[0][0], halos[0][1], halos[1][0], halos[1][1], halos[2][0], halos[2][1],
                  col(3), col(0), col(0), col(0), col(1), dprev, dnext, cw_spec, cn_spec, gn_spec, _ONES_SPEC],
        out_specs=[pl.BlockSpec((ts, ZA_COLS), lambda i: (i, 0)), col(0), cw_spec, cn_spec, gn_spec],
        out_shape=[
            jax.ShapeDtypeStruct((s, ZA_COLS), _CD),
            jax.ShapeDtypeStruct((s, GLA_V_TOTAL), F32),
            jax.ShapeDtypeStruct((CONV_K, CONV_WIDTH), F32),
            jax.ShapeDtypeStruct((1, CONV_WIDTH), F32),
            jax.ShapeDtypeStruct((1, GLA_DV), F32),
        ],
        compiler_params=_cp(("arbitrary",)),
    )(z, z, z, z, z, z, z, z, z, z, o_f, o_b, dy, dy, dy, dy, conv_w, conv_norm, gla_norm, _group_ones())


def _xa_probs(q_ref, kv_ref, h):
    qh = q_ref[:, XA_HEAD_DIM * h : XA_HEAD_DIM * (h + 1)]
    kh = kv_ref[:, XA_HEAD_DIM * h : XA_HEAD_DIM * (h + 1)]
    vh = kv_ref[:, D_MODEL + XA_HEAD_DIM * h : D_MODEL + XA_HEAD_DIM * (h + 1)]
    sc = _dot_nt(qh, kh) * (XA_HEAD_DIM**-0.5)
    e = jnp.exp(sc - jnp.max(sc, axis=-1, keepdims=True))
    return qh, kh, vh, e / jnp.sum(e, axis=-1, keepdims=True)


def _xattn_fwd(qx, kv, *, name):
    s = qx.shape[0]
    ts = _rows(s, light=True, times=4)

    def body(q_ref, kv_ref, o_ref):
        outs = []
        for h in range(XA_HEADS):
            _, _, vh, p = _xa_probs(q_ref, kv_ref, h)
            outs.append(_dot(p, vh))
        o_ref[...] = jnp.concatenate(outs, axis=1).astype(o_ref.dtype)

    return pl.pallas_call(
        body,
        name=name,
        grid=(s // ts,),
        in_specs=[pl.BlockSpec((ts, D_MODEL), lambda i: (i, 0)), pl.BlockSpec((N_MEM, 2 * D_MODEL), lambda i: (0, 0))],
        out_specs=pl.BlockSpec((ts, D_MODEL), lambda i: (i, 0)),
        out_shape=jax.ShapeDtypeStruct((s, D_MODEL), _CD),
        compiler_params=_cp(("parallel",)),
    )(qx, kv)


def _xattn_bwd(qx, kv, dox, *, name):
    s = qx.shape[0]
    ts = _rows(s, light=True, times=4)

    def body(q_ref, kv_ref, do_ref, dq_ref, dkv_ref):
        dqs, dks, dvs = [], [], []
        for h in range(XA_HEADS):
            qh, kh, vh, p = _xa_probs(q_ref, kv_ref, h)
            doh = do_ref[:, XA_HEAD_DIM * h : XA_HEAD_DIM * (h + 1)]
            dp = _dot_nt(doh, vh)
            ds = p * (dp - jnp.sum(dp * p, axis=-1, keepdims=True)) * (XA_HEAD_DIM**-0.5)
            dqs.append(_dot(ds, kh))
            dks.append(_dot_tn(ds, qh))
            dvs.append(_dot_tn(p, doh))
        dq_ref[...] = jnp.concatenate(dqs, axis=1).astype(dq_ref.dtype)
        dkv = jnp.concatenate(dks + dvs, axis=1)

        @pl.when(pl.program_id(0) == 0)
        def _():
            dkv_ref[...] = dkv

        @pl.when(pl.program_id(0) > 0)
        def _():
            dkv_ref[...] += dkv

    tile = pl.BlockSpec((ts, D_MODEL), lambda i: (i, 0))
    kv_spec = pl.BlockSpec((N_MEM, 2 * D_MODEL), lambda i: (0, 0))
    return pl.pallas_call(
        body,
        name=name,
        grid=(s // ts,),
        in_specs=[tile, kv_spec, tile],
        out_specs=[tile, kv_spec],
        out_shape=[jax.ShapeDtypeStruct((s, D_MODEL), _CD), jax.ShapeDtypeStruct((N_MEM, 2 * D_MODEL), F32)],
        compiler_params=_cp(("arbitrary",)),
    )(qx, kv, dox)


def _adamw_math(w, g, m, v):
    m = ADAM_B1 * m + (1.0 - ADAM_B1) * g
    v = ADAM_B2 * v + (1.0 - ADAM_B2) * (g * g)
    m_hat = m / (1.0 - ADAM_B1**ADAM_STEP)
    v_hat = v / (1.0 - ADAM_B2**ADAM_STEP)
    delta = -ADAM_LR * (m_hat / (jnp.sqrt(v_hat) + ADAM_EPS) + ADAM_WD * w)
    return delta, m, v


def _adamw(w, m, v, shard_rows, off, *, transposed, name):
    r, c = w.shape
    by_columns = r % 256 != 0
    tr = 512 if (c if by_columns else r) % 512 == 0 and off % 512 == 0 else 256
    if by_columns:
        assert not transposed and off == 0
        g_spec = tile = pl.BlockSpec((r, tr), lambda i: (0, i))
    else:
        g_spec = pl.BlockSpec((c, tr), lambda i: (off // c, i)) if transposed else pl.BlockSpec((tr, c), lambda i: (off // tr + i, 0))
        tile = pl.BlockSpec((tr, c), lambda i: (i, 0))

    def body(w_ref, g_ref, m_ref, v_ref, go_ref, d_ref, nm_ref, nv_ref):
        g = g_ref[...].T if transposed else g_ref[...]
        go_ref[...] = g
        d_ref[...], nm_ref[...], nv_ref[...] = _adamw_math(w_ref[...], g, m_ref[...], v_ref[...])

    return pl.pallas_call(
        body,
        name=name,
        grid=((c if by_columns else r) // tr,),
        in_specs=[tile, g_spec, tile, tile],
        out_specs=[tile] * 4,
        out_shape=[jax.ShapeDtypeStruct((r, c), F32)] * 4,
        compiler_params=_cp(("parallel",)),
    )(w, shard_rows, m, v)


def _adamw_small(groups, *, name):
    n = len(groups)

    def body(*refs):
        ins, outs = refs[: 4 * n], refs[4 * n :]
        for i in range(n):
            w_ref, g_ref, m_ref, v_ref = ins[4 * i : 4 * i + 4]
            outs[3 * i][...], outs[3 * i + 1][...], outs[3 * i + 2][...] = _adamw_math(w_ref[...], g_ref[...], m_ref[...], v_ref[...])

    flat = [a for grp in groups for a in grp]
    vm = pl.BlockSpec(memory_space=pltpu.VMEM)
    res = pl.pallas_call(
        body,
        name=name,
        in_specs=[vm] * (4 * n),
        out_specs=[vm] * (3 * n),
        out_shape=[jax.ShapeDtypeStruct(grp[0].shape, F32) for grp in groups for _ in range(3)],
        compiler_params=_cp(),
    )(*flat)
    return [tuple(res[3 * i : 3 * i + 3]) for i in range(n)]


def _place():
    return lax.axis_index("x"), lax.axis_index("y"), lax.axis_index("c")


def _rel_chip(x, y, k):
    return (1 - x if k & 2 else x), (1 - y if k & 1 else y)


def _half(c, rh):
    return pl.ds(pl.multiple_of(c * rh, 16), rh)


HBM = pl.BlockSpec(memory_space=pltpu.HBM)
SEM = pl.BlockSpec(memory_space=pltpu.SEMAPHORE)
EFFECT = pltpu.SideEffectType.DATAFLOW_SIDE_EFFECTING


def _in_hbm(a):
    return pltpu.with_memory_space_constraint(a, pltpu.HBM)


def _gather_copies(p_ref, land_ref, send_sems, recv_sems):
    rh = p_ref.shape[0] // 2
    x, y, c = _place()
    rows = _half(c, rh)
    copies = []
    for k in range(1, N_CHIPS):
        cx, cy = _rel_chip(x, y, k)
        copies.append(pltpu.make_async_remote_copy(
            src_ref=p_ref.at[rows], dst_ref=land_ref.at[2 * x + y, rows], send_sem=send_sems.at[k - 1], recv_sem=recv_sems.at[k - 1],
            device_id=(cx, cy, c), device_id_type=MESH))
    copies.append(pltpu.make_async_remote_copy(
        src_ref=p_ref, dst_ref=land_ref.at[2 * x + y], send_sem=send_sems.at[N_CHIPS - 1], recv_sem=recv_sems.at[N_CHIPS - 1],
        device_id=(x, y, 1 - c), device_id_type=MESH))
    return copies


def _gather_start(pack, after, *, name):
    r, w = pack.shape

    def body(p_ref, land_ref, after_ref, send_sems, recv_sems, p_thru, land_thru, token):
        for cp in _gather_copies(p_ref, land_ref, send_sems, recv_sems):
            cp.start()
        token[...] = jnp.zeros_like(token)

    return pl.pallas_call(
        body,
        name=name,
        out_shape=(pltpu.SemaphoreType.DMA((N_CHIPS,)), pltpu.SemaphoreType.DMA((N_CHIPS,)), pltpu.HBM((r, w), pack.dtype),
                   pltpu.HBM((N_CHIPS, r, w), pack.dtype), jax.ShapeDtypeStruct((8, 128), F32)),
        in_specs=(HBM, HBM, ANY),
        out_specs=(SEM, SEM, HBM, HBM, pl.BlockSpec(memory_space=pltpu.VMEM)),
        input_output_aliases={0: 2, 1: 3},
        compiler_params=pltpu.CompilerParams(has_side_effects=EFFECT),
    )(_in_hbm(pack), _in_hbm(lax.empty((N_CHIPS, r, w), pack.dtype)), after)


def _gather_wait(send_sems, recv_sems, pack, land, after, *, name):
    def body(p_ref, land_ref, send_sems, recv_sems, after_ref, p_out, land_out):
        for cp in _gather_copies(p_ref, land_ref, send_sems, recv_sems):
            cp.wait_send()
            cp.wait_recv()

    return pl.pallas_call(
        body,
        name=name,
        out_shape=(pltpu.HBM(pack.shape, pack.dtype), pltpu.HBM(land.shape, land.dtype)),
        in_specs=(HBM, HBM, SEM, SEM, ANY),
        out_specs=(HBM, HBM),
        input_output_aliases={0: 0, 1: 1},
        compiler_params=pltpu.CompilerParams(has_side_effects=EFFECT),
    )(pack, land, send_sems, recv_sems, after)


def _gather_spread(land, *, name):
    n, r, w = land.shape
    rh = r // 2

    def body(land_ref, o_ref, send_sems, recv_sems):
        x, y, c = _place()
        rows = _half(c, rh)
        copies = []
        for k in range(1, N_CHIPS):
            cx, cy = _rel_chip(x, y, k)
            copies.append(pltpu.make_async_remote_copy(
                src_ref=land_ref.at[2 * cx + cy, rows], dst_ref=o_ref.at[2 * cx + cy, rows], send_sem=send_sems.at[k - 1],
                recv_sem=recv_sems.at[k - 1], device_id=(x, y, 1 - c), device_id_type=MESH))
        for cp in copies:
            cp.start()
        for cp in copies:
            cp.wait()

    return pl.pallas_call(
        body,
        name=name,
        in_specs=[ANY],
        out_specs=ANY,
        out_shape=jax.ShapeDtypeStruct(land.shape, land.dtype),
        input_output_aliases={0: 0},
        scratch_shapes=[pltpu.SemaphoreType.DMA((N_CHIPS - 1,)), pltpu.SemaphoreType.DMA((N_CHIPS - 1,))],
        compiler_params=pltpu.CompilerParams(has_side_effects=True),
    )(land)


N_PARTS = 2 * (N_CHIPS - 1)


def _scatter_copies(lo_ref, g_ref, land_lo_ref, land_f_ref, send_sems, recv_sems, starting):
    rh = g_ref.shape[1] // 2
    x, y, c = _place()
    copies = []
    for k in range(1, N_CHIPS):
        cx, cy = _rel_chip(x, y, k)
        for i in range(2):
            part = 2 * (k - 1) + (c if starting else i)
            copies.append(pltpu.make_async_remote_copy(
                src_ref=lo_ref.at[2 * cx + cy, pl.ds(i * rh, rh)], dst_ref=land_lo_ref.at[part],
                send_sem=send_sems.at[2 * (k - 1) + i], recv_sem=recv_sems.at[part], device_id=(cx, cy, i), device_id_type=MESH))
    copies.append(pltpu.make_async_remote_copy(
        src_ref=g_ref.at[2 * x + y, _half(1 - c, rh)], dst_ref=land_f_ref, send_sem=send_sems.at[N_PARTS], recv_sem=recv_sems.at[N_PARTS],
        device_id=(x, y, 1 - c), device_id_type=MESH))
    return copies


def _scatter_start(g_lo, g, *, name):
    n, r, w = g.shape
    rh = r // 2

    def body(lo_ref, g_ref, land_lo_ref, land_f_ref, send_sems, recv_sems, lo_thru, g_thru, land_lo_thru, land_f_thru, token):
        for cp in _scatter_copies(lo_ref, g_ref, land_lo_ref, land_f_ref, send_sems, recv_sems, True):
            cp.start()
        token[...] = jnp.zeros_like(token)

    return pl.pallas_call(
        body,
        name=name,
        out_shape=(pltpu.SemaphoreType.DMA((N_PARTS + 1,)), pltpu.SemaphoreType.DMA((N_PARTS + 1,)), pltpu.HBM(g_lo.shape, g_lo.dtype),
                   pltpu.HBM(g.shape, g.dtype), pltpu.HBM((N_PARTS, rh, w), g_lo.dtype), pltpu.HBM((rh, w), g.dtype),
                   jax.ShapeDtypeStruct((8, 128), F32)),
        in_specs=(HBM, HBM, HBM, HBM),
        out_specs=(SEM, SEM, HBM, HBM, HBM, HBM, pl.BlockSpec(memory_space=pltpu.VMEM)),
        input_output_aliases={0: 2, 1: 3, 2: 4, 3: 5},
        compiler_params=pltpu.CompilerParams(has_side_effects=EFFECT),
    )(_in_hbm(g_lo), _in_hbm(g), _in_hbm(lax.empty((N_PARTS, rh, w), g_lo.dtype)), _in_hbm(lax.empty((rh, w), g.dtype)))


def _scatter_wait(send_sems, recv_sems, g_lo, g, land_lo, land_f, after, *, name):
    def body(lo_ref, g_ref, land_lo_ref, land_f_ref, send_sems, recv_sems, after_ref, o0, o1, o2, o3):
        for cp in _scatter_copies(lo_ref, g_ref, land_lo_ref, land_f_ref, send_sems, recv_sems, False):
            cp.wait_send()
            cp.wait_recv()

    arrays = (g_lo, g, land_lo, land_f)
    return pl.pallas_call(
        body,
        name=name,
        out_shape=tuple(pltpu.HBM(a.shape, a.dtype) for a in arrays),
        in_specs=(HBM, HBM, HBM, HBM, SEM, SEM, ANY),
        out_specs=(HBM, HBM, HBM, HBM),
        input_output_aliases={0: 0, 1: 1, 2: 2, 3: 3},
        compiler_params=pltpu.CompilerParams(has_side_effects=EFFECT),
    )(*arrays, send_sems, recv_sems, after)


def _scatter_sum(g, land_lo, land_f, where, *, name):
    n, r, w = g.shape
    rh = r // 2
    tr = _pick(rh, (256, 160, 80))
    nt = rh // tr

    def body(where_ref, g_ref, f_ref, lo_ref, o_ref):
        acc = g_ref[0] + f_ref[...]
        for part in range(N_PARTS):
            acc = acc + lo_ref[part].astype(F32)
        o_ref[...] = acc

    return pl.pallas_call(
        body,
        name=name,
        grid_spec=pltpu.PrefetchScalarGridSpec(
            num_scalar_prefetch=1,
            grid=(nt,),
            in_specs=[pl.BlockSpec((1, tr, w), lambda i, wh: (wh[1], wh[0] * nt + i, 0)),
                      pl.BlockSpec((tr, w), lambda i, wh: (i, 0)),
                      pl.BlockSpec((N_PARTS, tr, w), lambda i, wh: (0, i, 0))],
            out_specs=pl.BlockSpec((tr, w), lambda i, wh: (wh[0] * nt + i, 0)),
        ),
        out_shape=jax.ShapeDtypeStruct((r, w), F32),
        compiler_params=_cp(("parallel",)),
    )(where, g, land_f, land_lo)


def _swap_all(shards, *, name):
    n = len(shards)

    def body(*refs):
        ins, outs = refs[:n], refs[n : 2 * n]
        send_sems, recv_sems = refs[2 * n :]
        x, y, c = _place()
        copies = []
        for i, (e_ref, o_ref) in enumerate(zip(ins, outs)):
            rows = _half(c, e_ref.shape[0] // 2)
            copies.append(pltpu.make_async_remote_copy(src_ref=e_ref.at[rows], dst_ref=o_ref.at[rows], send_sem=send_sems.at[i],
                                                       recv_sem=recv_sems.at[i], device_id=(x, y, 1 - c), device_id_type=MESH))
        for cp in copies:
            cp.start()
        for cp in copies:
            cp.wait()

    return pl.pallas_call(
        body,
        name=name,
        in_specs=[ANY] * n,
        out_specs=[ANY] * n,
        out_shape=[jax.ShapeDtypeStruct(e.shape, e.dtype) for e in shards],
        input_output_aliases={i: i for i in range(n)},
        scratch_shapes=[pltpu.SemaphoreType.DMA((n,)), pltpu.SemaphoreType.DMA((n,))],
        compiler_params=pltpu.CompilerParams(has_side_effects=True),
    )(*shards)


def _sum_small(small, after):
    n_dev = 8

    def body(s_ref, after_ref, o_ref, all_ref, send_sems, recv_sems):
        x, y, c = _place()
        me = 4 * x + 2 * y + c
        all_ref[me] = s_ref[...]
        copies = []
        for k in range(1, n_dev):
            cx, cy = _rel_chip(x, y, k >> 1)
            cc = 1 - c if k & 1 else c
            copies.append(pltpu.make_async_remote_copy(
                src_ref=s_ref, dst_ref=all_ref.at[me], send_sem=send_sems.at[k - 1], recv_sem=recv_sems.at[k - 1],
                device_id=(cx, cy, cc), device_id_type=MESH))
        for cp in copies:
            cp.start()
        for cp in copies:
            cp.wait()
        acc = all_ref[0]
        for a in range(1, n_dev):
            acc = acc + all_ref[a]
        o_ref[...] = acc

    vm = pl.BlockSpec(memory_space=pltpu.VMEM)
    return pl.pallas_call(
        body,
        name="sum_small",
        in_specs=[vm, ANY],
        out_specs=vm,
        out_shape=jax.ShapeDtypeStruct(small.shape, F32),
        scratch_shapes=[pltpu.VMEM((n_dev,) + small.shape, F32), pltpu.SemaphoreType.DMA((n_dev - 1,)), pltpu.SemaphoreType.DMA((n_dev - 1,))],
        compiler_params=pltpu.CompilerParams(has_side_effects=True),
    )(small, after)


MATS = {"w_in": (776, True), "w_out": (256, False), "w_xq": (256, False), "w_xkv": (512, True), "w_xo": (256, False),
        "w_up": (1024, True), "w_down": (1024, False)}
GATHER_FIRST = ("w_in",)
GATHER_REST = ("w_out", "w_xq", "w_xkv", "w_xo", "w_up", "w_down")
GRAD_GROUPS = (("w_up", "w_down"), ("w_out", "w_xq", "w_xkv", "w_xo"), ("w_in",))


def _group_rows(names):
    n = sum(MATS[name][0] for name in names)
    return n + (-n) % 32


def _pack(pieces, rows):
    p = jnp.concatenate(pieces, axis=0) if len(pieces) > 1 else pieces[0]
    return jnp.pad(p, ((0, rows - p.shape[0]), (0, 0))) if rows > p.shape[0] else p


SMALL = (
    ("mix_norm", 1024), ("conv_norm", 512), ("b_af", 256), ("b_ab", 256), ("gla_norm", 128), ("xa_norm", 1024), ("mem_norm", 1024),
    ("mlp_norm", 1024), ("final_norm", 1024), ("conv_w", 1536), ("w_af", 4096), ("w_ab", 4096), ("loss", 128),
)


def kernel(x, mem, mix_norm, w_in, conv_w, conv_norm, w_af, b_af, w_ab, b_ab, gla_norm, w_out, xa_norm, mem_norm, w_xq, w_xkv, w_xo, mlp_norm, w_up, w_down, final_norm, loss_target, m_mix_norm, m_w_in, m_conv_w, m_conv_norm, m_w_af, m_b_af, m_w_ab, m_b_ab, m_gla_norm, m_w_out, m_xa_norm, m_mem_norm, m_w_xq, m_w_xkv, m_w_xo, m_mlp_norm, m_w_up, m_w_down, m_final_norm, v_mix_norm, v_w_in, v_conv_w, v_conv_norm, v_w_af, v_b_af, v_w_ab, v_b_ab, v_gla_norm, v_w_out, v_xa_norm, v_mem_norm, v_w_xq, v_w_xkv, v_w_xo, v_mlp_norm, v_w_up, v_w_down, v_final_norm):
    given = dict(locals())
    xi, yi, ci = _place()
    chip = 2 * xi + yi
    where = jnp.stack([ci, chip]).astype(jnp.int32)

    lo = {name: (given[name][0].T if MATS[name][1] else given[name][0]).astype(_CD) for name in MATS}
    pack_rest = _pack([lo[name] for name in GATHER_REST], _group_rows(GATHER_REST))
    pack_first = _pack([lo[name] for name in GATHER_FIRST], _group_rows(GATHER_FIRST))
    xs, mems, tgt = x[0], mem[0], loss_target[0]
    behind = lambda gain, token: gain + token[0, 0]

    def placed(shard, full_shape, col):
        return lax.dynamic_update_slice(jnp.zeros(full_shape, F32), shard, (0, col)).reshape(-1, 128)

    sw = jnp.concatenate([
        placed(conv_w[0], (CONV_K, CONV_WIDTH), 128 * chip),
        placed(w_af[0], (GLA_LOWRANK, GLA_K_TOTAL), 64 * chip),
        placed(w_ab[0], (GLA_LOWRANK, GLA_K_TOTAL), 64 * chip),
    ], axis=0)
    sw = jnp.pad(sw, ((0, SMALL_ROWS - sw.shape[0]), (0, 0))) * (ci == 0).astype(F32)
    sw = _sum_small(sw, mix_norm)

    first_send, first_recv, pack_first, land_first, first_token = _gather_start(pack_first, sw, name="gather_first_start")
    rest_send, rest_recv, pack_rest, land_rest, rest_token = _gather_start(pack_rest, first_token, name="gather_rest_start")
    h1 = _rms_fwd(xs, behind(mix_norm, rest_token), name="norm_mix")
    pack_first, land_first = _gather_wait(first_send, first_recv, pack_first, land_first, h1, name="gather_first_wait")
    got_first = _gather_spread(land_first, name="gather_first_spread")

    def whole(got, off, rows):
        return got[:, off : off + rows].reshape(N_CHIPS * rows, D_MODEL)

    w_in_t = whole(got_first, 0, MATS["w_in"][0])
    w_za = jnp.concatenate([w_in_t[0:1536], w_in_t[2560:3072]], axis=0)
    w_zb = jnp.concatenate([w_in_t[1536:2560], w_in_t[3072:W_IN_COLS], jnp.zeros((ZB_COLS - 1056, D_MODEL), _CD)], axis=0)
    conv_w_full = sw[0:12].reshape(CONV_K, CONV_WIDTH)
    w_af_full = sw[12:44].reshape(GLA_LOWRANK, GLA_K_TOTAL)
    w_ab_full = sw[44:76].reshape(GLA_LOWRANK, GLA_K_TOTAL)
    waf_p = jnp.pad(w_af_full, ((0, 128 - GLA_LOWRANK), (0, 0))).astype(_CD)
    wab_p = jnp.pad(w_ab_full, ((GLA_LOWRANK, 128 - 2 * GLA_LOWRANK), (0, 0))).astype(_CD)

    z_b = _mm(h1, w_zb, mode="nt", name="proj_in_b", tn=ZB_COLS)
    z_a = _mm(h1, w_za, mode="nt", name="proj_in_a", tm=512, tn=ZA_COLS)
    b_f, b_b = _gate_fwd(z_b, waf_p, wab_p, b_af, b_ab, name="gates")
    o_f, st_f, o_b, st_b = _gla_fwd(z_b, b_f, b_b, name="gla_scan")
    y = _mix_fwd(z_a, o_f, o_b, conv_w_full, conv_norm, gla_norm, name="mix_out")
    pack_rest, land_rest = _gather_wait(rest_send, rest_recv, pack_rest, land_rest, y, name="gather_rest_wait")
    gathered = _gather_spread(land_rest, name="gather_rest_spread")
    wt, off = {}, 0
    for name in GATHER_REST:
        wt[name] = whole(gathered, off, MATS[name][0])
        off += MATS[name][0]
    x1, hx = _mm_rows(y, wt["w_out"], mode="nn", name="proj_out", rows=(xs,), vecs=(xa_norm,), out_rows=(F32, _CD),
                      epilogue=_ep_residual_norm, tm=1024)
    qx = _mm(hx, wt["w_xq"], mode="nn", name="proj_xq", out_dtypes=(_CD,))
    hmem = _rms_fwd(mems, mem_norm, name="norm_mem")
    kv = _mm(hmem, wt["w_xkv"], mode="nt", name="proj_xkv", out_dtypes=(_CD,))
    ox = _xattn_fwd(qx, kv, name="xattn")
    x2, hm = _mm_rows(ox, wt["w_xo"], mode="nn", name="proj_xo", rows=(x1,), vecs=(mlp_norm,), out_rows=(F32, _CD),
                      epilogue=_ep_residual_norm, tm=1024)
    act, relu_u = _mm(hm, wt["w_up"], mode="nt", name="mlp_up", out_dtypes=(_CD, _CD), tm=2048,
                      epilogue=lambda acc: (jnp.square(jnp.maximum(acc, 0.0)), jnp.maximum(acc, 0.0)))
    dx3, dx3_lo, loss_part, g_final_norm = _mm_rows(
        act, wt["w_down"], mode="nn", name="mlp_down", rows=(x2, tgt), vecs=(final_norm.reshape(1, D_MODEL),),
        out_rows=(F32, _CD), out_vecs=(128, D_MODEL), epilogue=_ep_loss)

    grads_t = {}

    def start_group(names, tag):
        rows = _group_rows(names)
        g = jnp.stack([_pack([grads_t[name][a * MATS[name][0] : (a + 1) * MATS[name][0]] for name in names], rows) for a in range(N_CHIPS)])
        return _scatter_start(g.astype(_TD), g, name="grads_" + tag + "_start")

    def finish_group(state, after, tag):
        send_sems, recv_sems, g_lo, g, land_lo, land_f, _ = state
        g_lo, g, land_lo, land_f = _scatter_wait(send_sems, recv_sems, g_lo, g, land_lo, land_f, after, name="grads_" + tag + "_wait")
        return _scatter_sum(g, land_lo, land_f, where, name="grads_" + tag + "_sum")

    def new_packs(names):
        shape = (N_CHIPS, _group_rows(names), D_MODEL)
        return lax.empty(shape, F32), lax.empty(shape, _TD)

    def grad_into(packs, names, which, a, b, name):
        off = sum(MATS[other][0] for other in names[: names.index(which)])
        return _mm_tn_into(a, b, packs, rows=MATS[which][0], off=off, name=name)

    du = _mm(dx3_lo, wt["w_down"], mode="nt", name="mlp_down_dx", out_dtypes=(_CD,), extras=(relu_u,), tm=2048,
             epilogue=lambda acc, rr: (acc * (2.0 * rr.astype(F32)),))
    packs = new_packs(GRAD_GROUPS[0])
    packs = grad_into(packs, GRAD_GROUPS[0], "w_down", act, dx3_lo, "mlp_down_dw")
    packs = grad_into(packs, GRAD_GROUPS[0], "w_up", du, hm, "mlp_up_dw")
    mlp_state = _scatter_start(packs[1], packs[0], name="grads_mlp_start")
    dx2, dx2_lo, g_mlp_norm = _mm_rows(
        du, wt["w_up"], mode="nn", name="mlp_up_dx", rows=(x2, dx3), vecs=(behind(mlp_norm, mlp_state[-1]),),
        out_rows=(F32, _CD), out_vecs=(D_MODEL,), epilogue=_ep_norm_bwd)
    dox = _mm(dx2_lo, wt["w_xo"], mode="nt", name="proj_xo_dx", out_dtypes=(_CD,))
    packs = new_packs(GRAD_GROUPS[1])
    packs = grad_into(packs, GRAD_GROUPS[1], "w_xo", ox, dx2_lo, "proj_xo_dw")
    dqx, dkv = _xattn_bwd(qx, kv, dox, name="xattn_bwd")
    packs = grad_into(packs, GRAD_GROUPS[1], "w_xq", hx, dqx, "proj_xq_dw")
    dx1, dx1_lo, g_xa_norm = _mm_rows(
        dqx, wt["w_xq"], mode="nt", name="proj_xq_dx", rows=(x1, dx2), vecs=(xa_norm,),
        out_rows=(F32, _CD), out_vecs=(D_MODEL,), epilogue=_ep_norm_bwd, tm=1024)
    dkv_lo = dkv.astype(_CD)
    packs = grad_into(packs, GRAD_GROUPS[1], "w_xkv", dkv_lo, hmem, "proj_xkv_dw")
    dhmem = _mm(dkv_lo, wt["w_xkv"], mode="nn", name="proj_xkv_dx")
    g_mem_norm = _rms_gain_grad(mems, dhmem, name="norm_mem_bwd")
    dy = _mm(dx1_lo, wt["w_out"], mode="nt", name="proj_out_dx")
    packs = grad_into(packs, GRAD_GROUPS[1], "w_out", y, dx1_lo, "proj_out_dw")
    attn_state = _scatter_start(packs[1], packs[0], name="grads_attn_start")
    dz_a, do, g_conv_w, g_conv_norm, g_gla_norm = _mix_bwd(z_a, o_f, o_b, dy, conv_w_full, behind(conv_norm, attn_state[-1]), gla_norm, name="mix_out_bwd")
    dqkv_f, db_f, dqkv_b, db_b = _gla_bwd(z_b, b_f, b_b, do, st_f, st_b, name="gla_scan_bwd")
    dz_b, g_waf_p, g_wab_p, g_b_af, g_b_ab = _gate_bwd(z_b, waf_p, wab_p, b_af, b_ab, db_f, db_b, dqkv_f, dqkv_b, name="gates_bwd")
    g_za = _mm_tn(dz_a, h1, name="proj_in_a_dw")
    g_zb = _mm_tn(dz_b, h1, name="proj_in_b_dw")
    grads_t["w_in"] = jnp.concatenate([g_za[0:1536], g_zb[0:1024], g_za[1536:2048], g_zb[1024:1056]], axis=0)
    in_state = start_group(GRAD_GROUPS[2], "in")
    grad_x, g_mix_norm = _mm_rows(
        dz_a, w_za, mode="nn", name="proj_in_dx", more=((dz_b, w_zb),), rows=(xs, dx1), vecs=(behind(mix_norm, in_state[-1]),),
        out_rows=(F32,), out_vecs=(D_MODEL,), epilogue=_ep_norm_bwd)

    half_mlp = finish_group(mlp_state, grad_x, "mlp")
    half_attn = finish_group(attn_state, half_mlp, "attn")
    half_in = finish_group(in_state, half_attn, "in")
    shard_rows = {}
    for names, rows in zip(GRAD_GROUPS, _swap_all([half_mlp, half_attn, half_in], name="shards_to_sibling")):
        off = 0
        for name in names:
            shard_rows[name] = (rows, off)
            off += MATS[name][0]

    small_vals = dict(mix_norm=g_mix_norm, conv_norm=g_conv_norm, b_af=g_b_af, b_ab=g_b_ab, gla_norm=g_gla_norm, xa_norm=g_xa_norm,
                      mem_norm=g_mem_norm, mlp_norm=g_mlp_norm, final_norm=g_final_norm, conv_w=g_conv_w,
                      w_af=g_waf_p[0:GLA_LOWRANK], w_ab=g_wab_p[GLA_LOWRANK : 2 * GLA_LOWRANK], loss=loss_part)
    small = jnp.concatenate([small_vals[name].reshape(-1, 128) for name, _ in SMALL], axis=0)
    small = _sum_small(jnp.pad(small, ((0, SMALL_ROWS - small.shape[0]), (0, 0))), loss_part)
    g_small, off = {}, 0
    for name, n in SMALL:
        g_small[name] = small[off : off + n // 128]
        off += n // 128
    loss = g_small["loss"][0, 0]
    g_small["conv_w"] = lax.dynamic_slice(g_small["conv_w"].reshape(CONV_K, CONV_WIDTH), (0, 128 * chip), (CONV_K, 128))
    g_small["w_af"] = lax.dynamic_slice(g_small["w_af"].reshape(GLA_LOWRANK, GLA_K_TOTAL), (0, 64 * chip), (GLA_LOWRANK, 64))
    g_small["w_ab"] = lax.dynamic_slice(g_small["w_ab"].reshape(GLA_LOWRANK, GLA_K_TOTAL), (0, 64 * chip), (GLA_LOWRANK, 64))

    names = ["mix_norm", "w_in", "conv_w", "conv_norm", "w_af", "b_af", "w_ab", "b_ab", "gla_norm", "w_out", "xa_norm", "mem_norm",
             "w_xq", "w_xkv", "w_xo", "mlp_norm", "w_up", "w_down", "final_norm"]
    big_names = list(MATS)
    as2d = lambda a: a.reshape(1, -1) if a.ndim == 1 else a.reshape(a.shape[-2:])
    grads, deltas, new_m, new_v = {}, {}, {}, {}
    for name in big_names:
        rows, off = shard_rows[name]
        wmv = [as2d(given[name]), as2d(given["m_" + name]), as2d(given["v_" + name])]
        as_stored = name == "w_in"
        if as_stored:
            wmv = [a.T for a in wmv]
        res = _adamw(*wmv, rows, off, transposed=MATS[name][1] and not as_stored, name="adamw_" + name)
        grads[name], deltas[name], new_m[name], new_v[name] = [a.T for a in res] if as_stored else res
    small_names = [name for name in names if name not in big_names]
    groups = []
    for name in small_names:
        grads[name] = g_small[name].reshape(as2d(given[name]).shape)
        groups.append((as2d(given[name]), grads[name], as2d(given["m_" + name]), as2d(given["v_" + name])))
    for name, res in zip(small_names, _adamw_small(groups, name="adamw_small")):
        deltas[name], new_m[name], new_v[name] = res

    like = lambda name, a: a.reshape(given[name].shape)
    return (loss, grad_x[None], *[like(n, grads[n]) for n in names], *[like(n, deltas[n]) for n in names],
            *[like(n, new_m[n]) for n in names], *[like(n, new_v[n]) for n in names])
```

```python
import jax
import jax.numpy as jnp
from jax import lax
from jax.experimental import pallas as pl
from jax.experimental.pallas import tpu as pltpu

F32 = jnp.float32
BF16 = jnp.bfloat16
_CD = jnp.bfloat16
_TD = jnp.bfloat16

D_MODEL = 1024
N_MEM = 256
CONV_WIDTH = 512
CONV_GROUP = 64
CONV_K = 3
GLA_HEADS = 4
GLA_DK = 64
GLA_DV = 128
GLA_K_TOTAL = 256
GLA_V_TOTAL = 512
GLA_LOWRANK = 16
GLA_GATE_SCALE = 1.0 / 16.0
GLA_CHUNK = 64
XA_HEADS = 4
XA_HEAD_DIM = 256
D_FF = 4096
EPS = 1e-6
W_IN_COLS = 3104
ZA_COLS = 2048
ZB_COLS = 1152
LR_COL = 1024

ADAM_LR = 0.001
ADAM_B1 = 0.9
ADAM_B2 = 0.999
ADAM_EPS = 1e-08
ADAM_WD = 0.01
ADAM_STEP = 10

N_CHIPS = 4
SMALL_ROWS = 128

_TS = 512
_VMEM = 44 * 1024 * 1024
MESH = pl.DeviceIdType.MESH
ANY = pl.BlockSpec(memory_space=pl.ANY)


def _cp(sem=None, **kw):
    return pltpu.CompilerParams(dimension_semantics=sem, vmem_limit_bytes=_VMEM, **kw)


def _dot(a, b):
    return jnp.dot(a.astype(_CD), b.astype(_CD), preferred_element_type=F32)


def _dot_nt(a, b):
    return lax.dot_general(a.astype(_CD), b.astype(_CD), (((1,), (1,)), ((), ())), preferred_element_type=F32)


def _dot_tn(a, b):
    return lax.dot_general(a.astype(_CD), b.astype(_CD), (((0,), (0,)), ((), ())), preferred_element_type=F32)


def _dot_split(x, ones):
    hi = x.astype(BF16)
    r = x - hi.astype(F32)
    mid = r.astype(BF16)
    lo = (r - mid.astype(F32)).astype(BF16)
    d = lambda p: jnp.dot(p, ones, preferred_element_type=F32)
    return d(hi) + d(mid) + d(lo)


def _pick(n, cands=(1024, 640, 512, 256, 128)):
    for t in cands:
        if n % t == 0:
            return t
    return n


def _rows(s, light=False, times=2):
    return min(times * _TS if light else _TS, s)


def _sigmoid(v):
    e = jnp.exp(-jnp.abs(v))
    return jnp.where(v >= 0, 1.0 / (1.0 + e), e / (1.0 + e))


def _mm(a, b, *, mode, name, out_dtypes=(F32,), extras=(), epilogue=None, tm=None, tn=None, tk=None):
    m, k = a.shape
    n = b.shape[1] if mode == "nn" else b.shape[0]
    tm = min(m, tm or 1024)
    tn = tn or _pick(n)
    tk = tk or _pick(k)
    nk = k // tk
    n_ex, n_out = len(extras), len(out_dtypes)

    def body(*refs):
        a_ref, b_ref = refs[:2]
        ex = refs[2 : 2 + n_ex]
        outs = refs[2 + n_ex : 2 + n_ex + n_out]
        part = _dot(a_ref[...], b_ref[...]) if mode == "nn" else _dot_nt(a_ref[...], b_ref[...])

        def finish(acc):
            res = epilogue(acc, *[e[...] for e in ex]) if epilogue else (acc,)
            for o, r in zip(outs, res):
                o[...] = r.astype(o.dtype)

        if nk == 1:
            finish(part)
        else:
            acc_ref = refs[-1]
            kk = pl.program_id(2)

            @pl.when(kk == 0)
            def _():
                acc_ref[...] = part

            @pl.when(kk > 0)
            def _():
                acc_ref[...] += part

            @pl.when(kk == nk - 1)
            def _():
                finish(acc_ref[...])

    b_spec = pl.BlockSpec((tk, tn), lambda i, j, kk: (kk, j)) if mode == "nn" else pl.BlockSpec((tn, tk), lambda i, j, kk: (j, kk))
    tile = pl.BlockSpec((tm, tn), lambda i, j, kk: (i, j))
    out = pl.pallas_call(
        body,
        name=name,
        grid=(m // tm, n // tn, nk),
        in_specs=[pl.BlockSpec((tm, tk), lambda i, j, kk: (i, kk)), b_spec] + [tile] * n_ex,
        out_specs=[tile] * n_out,
        out_shape=[jax.ShapeDtypeStruct((m, n), dt) for dt in out_dtypes],
        scratch_shapes=[pltpu.VMEM((tm, tn), F32)] if nk > 1 else [],
        compiler_params=_cp(("parallel", "parallel", "arbitrary")),
    )(a, b, *extras)
    return out[0] if n_out == 1 else out


def _mm_two(a, b1, b2, *, name, tm=512):
    m, k = a.shape
    tm = min(m, tm)

    def body(a_ref, b1_ref, b2_ref, o1_ref, o2_ref):
        av = a_ref[...]
        o1_ref[...] = _dot_nt(av, b1_ref[...])
        o2_ref[...] = _dot_nt(av, b2_ref[...])

    whole = lambda arr: pl.BlockSpec(arr.shape, lambda i: (0, 0))
    rows = lambda n: pl.BlockSpec((tm, n), lambda i: (i, 0))
    return pl.pallas_call(
        body,
        name=name,
        grid=(m // tm,),
        in_specs=[rows(k), whole(b1), whole(b2)],
        out_specs=[rows(b1.shape[0]), rows(b2.shape[0])],
        out_shape=[jax.ShapeDtypeStruct((m, b1.shape[0]), F32), jax.ShapeDtypeStruct((m, b2.shape[0]), F32)],
        compiler_params=_cp(("parallel",)),
    )(a, b1, b2)


def _mm_tn(a, b, *, name):
    s, m = a.shape
    n = b.shape[1]
    cap = max(128, (1 << 20) // n)
    tm = _pick(m, tuple(t for t in (512, 640, 384, 256, 128) if t <= max(cap, 128)))
    ts = min(s, 1 << (((1 << 22) // n).bit_length() - 1))
    ns = s // ts

    def body(a_ref, b_ref, o_ref):
        part = _dot_tn(a_ref[...], b_ref[...])
        if ns == 1:
            o_ref[...] = part
        else:
            ss = pl.program_id(1)

            @pl.when(ss == 0)
            def _():
                o_ref[...] = part

            @pl.when(ss > 0)
            def _():
                o_ref[...] += part

    return pl.pallas_call(
        body,
        name=name,
        grid=(m // tm, ns),
        in_specs=[pl.BlockSpec((ts, tm), lambda i, ss: (ss, i)), pl.BlockSpec((ts, n), lambda i, ss: (ss, 0))],
        out_specs=pl.BlockSpec((tm, n), lambda i, ss: (i, 0)),
        out_shape=jax.ShapeDtypeStruct((m, n), F32),
        compiler_params=_cp(("parallel", "arbitrary")),
    )(a, b)


def _mm_tn_into(a, b, packs, *, rows, off, name):
    s, m = a.shape
    n = b.shape[1]
    tm = 1024 if rows % 1024 == 0 and s >= 4096 else 512
    tr = min(tm, rows)
    per, chips = rows // tr, tm // tr
    ts = min(s, (1 << (((1 << 22) // n).bit_length() - 1)) * 512 // tm)
    ns = s // ts

    def body(a_ref, b_ref, f_in, lo_in, f_ref, lo_ref):
        part = _dot_tn(a_ref[...], b_ref[...])
        pieces = [part[c * tr : (c + 1) * tr] for c in range(chips)]
        if ns == 1:
            for c, p in enumerate(pieces):
                f_ref[c] = p
                lo_ref[c] = p.astype(lo_ref.dtype)
        else:
            ss = pl.program_id(1)

            @pl.when(ss == 0)
            def _():
                for c, p in enumerate(pieces):
                    f_ref[c] = p

            @pl.when(ss > 0)
            def _():
                for c, p in enumerate(pieces):
                    f_ref[c] += p

            @pl.when(ss == ns - 1)
            def _():
                lo_ref[...] = f_ref[...].astype(lo_ref.dtype)

    spec = pl.BlockSpec((chips, tr, n), lambda i, ss: (i // per, off // tr + i % per, 0))
    return pl.pallas_call(
        body,
        name=name,
        grid=(m // tm, ns),
        in_specs=[pl.BlockSpec((ts, tm), lambda i, ss: (ss, i)), pl.BlockSpec((ts, n), lambda i, ss: (ss, 0)), ANY, ANY],
        out_specs=[spec, spec],
        out_shape=[jax.ShapeDtypeStruct(p.shape, p.dtype) for p in packs],
        input_output_aliases={2: 0, 3: 1},
        compiler_params=_cp(("parallel", "arbitrary")),
    )(a, b, *packs)


def _mm_rows(a, b, *, mode, name, more=(), rows=(), vecs=(), out_rows=(), out_vecs=(), epilogue, tm=512):
    m, k = a.shape
    n = b.shape[1] if mode == "nn" else b.shape[0]
    tm = min(m, tm)
    parts = 2 if tm % 256 == 0 else 1
    n_m, n_r, n_v, n_or, n_ov = 2 * len(more), len(rows), len(vecs), len(out_rows), len(out_vecs)

    def body(*refs):
        a_ref, b_ref = refs[:2]
        m_refs = refs[2 : 2 + n_m]
        rest = refs[2 + n_m :]
        r_refs = rest[:n_r]
        v_refs = rest[n_r : n_r + n_v]
        or_refs = rest[n_r + n_v : n_r + n_v + n_or]
        ov_refs = rest[n_r + n_v + n_or :]
        res_vecs = None
        for p in range(parts):
            rs = slice(p * tm // parts, (p + 1) * tm // parts)
            acc = _dot(a_ref[rs, :], b_ref[...]) if mode == "nn" else _dot_nt(a_ref[rs, :], b_ref[...])
            for a2_ref, b2_ref in zip(m_refs[0::2], m_refs[1::2]):
                acc = acc + _dot(a2_ref[rs, :], b2_ref[...])
            res_rows, part_vecs = epilogue(acc, [r[rs, :] for r in r_refs], [v[...] for v in v_refs])
            for o, r in zip(or_refs, res_rows):
                o[rs, :] = r.astype(o.dtype)
            res_vecs = part_vecs if res_vecs is None else [s + t for s, t in zip(res_vecs, part_vecs)]
        if n_ov:
            first = pl.program_id(0) == 0

            @pl.when(first)
            def _():
                for o, r in zip(ov_refs, res_vecs):
                    o[...] = r

            @pl.when(jnp.logical_not(first))
            def _():
                for o, r in zip(ov_refs, res_vecs):
                    o[...] += r

    tile = pl.BlockSpec((tm, n), lambda i: (i, 0))
    whole = lambda arr: pl.BlockSpec(arr.shape, lambda i: (0, 0))
    vec = lambda w: pl.BlockSpec((1, w), lambda i: (0, 0))
    out = pl.pallas_call(
        body,
        name=name,
        grid=(m // tm,),
        in_specs=[pl.BlockSpec((tm, k), lambda i: (i, 0)), whole(b)]
        + [spec for a2, b2 in more for spec in (pl.BlockSpec((tm, a2.shape[1]), lambda i: (i, 0)), whole(b2))]
        + [tile] * n_r + [vec(v.shape[1]) for v in vecs],
        out_specs=[tile] * n_or + [vec(w) for w in out_vecs],
        out_shape=[jax.ShapeDtypeStruct((m, n), dt) for dt in out_rows] + [jax.ShapeDtypeStruct((1, w), F32) for w in out_vecs],
        compiler_params=_cp(("arbitrary",) if n_ov else ("parallel",)),
    )(a, b, *[x for pair in more for x in pair], *rows, *vecs)
    return out


def _ep_residual_norm(acc, rows, vecs):
    x = acc + rows[0]
    r = lax.rsqrt(jnp.mean(x * x, axis=-1, keepdims=True) + EPS)
    return [x, x * r * vecs[0]], []


def _ep_norm_bwd(acc, rows, vecs):
    dy = acc
    for extra in rows[2:]:
        dy = dy + extra
    x, dres = rows[0], rows[1]
    r = lax.rsqrt(jnp.mean(x * x, axis=-1, keepdims=True) + EPS)
    xh = x * r
    dxh = dy * vecs[0]
    dx = r * (dxh - xh * jnp.mean(dxh * xh, axis=-1, keepdims=True)) + dres
    return [dx, dx], [jnp.sum(dy * xh, axis=0, keepdims=True)]


def _ep_loss(acc, rows, vecs):
    x = acc + rows[0]
    d = x.shape[-1]
    r = lax.rsqrt(jnp.mean(x * x, axis=-1, keepdims=True) + EPS)
    xh = x * r
    err = xh * vecs[0] - rows[1]
    loss = jnp.zeros((1, 128), F32) + 0.5 * jnp.sum(jnp.mean(err * err, axis=-1, keepdims=True))
    dy = err * (1.0 / d)
    dxh = dy * vecs[0]
    dx = r * (dxh - xh * jnp.mean(dxh * xh, axis=-1, keepdims=True))
    return [dx, dx], [loss, jnp.sum(dy * xh, axis=0, keepdims=True)]


def _rms_fwd(x, g, *, name):
    s, d = x.shape
    ts = _rows(s, light=True, times=4)

    def body(x_ref, g_ref, o_ref):
        xf = x_ref[...]
        r = lax.rsqrt(jnp.mean(xf * xf, axis=-1, keepdims=True) + EPS)
        o_ref[...] = (xf * r * g_ref[...]).astype(o_ref.dtype)

    return pl.pallas_call(
        body,
        name=name,
        grid=(s // ts,),
        in_specs=[pl.BlockSpec((ts, d), lambda i: (i, 0)), pl.BlockSpec((1, d), lambda i: (0, 0))],
        out_specs=pl.BlockSpec((ts, d), lambda i: (i, 0)),
        out_shape=jax.ShapeDtypeStruct((s, d), _CD),
        compiler_params=_cp(("parallel",)),
    )(x, g)


def _rms_gain_grad(x, dy, *, name):
    s, d = x.shape
    ts = _rows(s)

    def body(x_ref, dy_ref, dg_ref):
        xf = x_ref[...]
        r = lax.rsqrt(jnp.mean(xf * xf, axis=-1, keepdims=True) + EPS)
        part = jnp.sum(dy_ref[...] * (xf * r), axis=0, keepdims=True)

        @pl.when(pl.program_id(0) == 0)
        def _():
            dg_ref[...] = part

        @pl.when(pl.program_id(0) > 0)
        def _():
            dg_ref[...] += part

    tile = pl.BlockSpec((ts, d), lambda i: (i, 0))
    return pl.pallas_call(
        body,
        name=name,
        grid=(s // ts,),
        in_specs=[tile, tile],
        out_specs=pl.BlockSpec((1, d), lambda i: (0, 0)),
        out_shape=jax.ShapeDtypeStruct((1, d), F32),
        compiler_params=_cp(("arbitrary",)),
    )(x, dy)


def _chunk_scan(v, row_in_chunk, suffix):
    t = v.shape[0]
    step = 1
    while step < GLA_CHUNK:
        if suffix:
            v = v + jnp.where(row_in_chunk < GLA_CHUNK - step, pltpu.roll(v, t - step, 0), 0.0)
        else:
            v = v + jnp.where(row_in_chunk >= step, pltpu.roll(v, step, 0), 0.0)
        step *= 2
    return v


def _gate_pre(lr, w_ref, b_ref):
    return _dot(lr, w_ref[...]) + b_ref[...]


def _gate_fwd(z, waf, wab, baf, bab, *, name):
    s = z.shape[0]
    ts = _rows(s, light=True, times=4)

    def body(lr_ref, waf_ref, wab_ref, baf_ref, bab_ref, bf_ref, bb_ref):
        lr = lr_ref[...]
        ric = lax.broadcasted_iota(jnp.int32, (ts, GLA_K_TOTAL), 0) & (GLA_CHUNK - 1)
        for w_ref, b_ref, o_ref, suffix in ((waf_ref, baf_ref, bf_ref, False), (wab_ref, bab_ref, bb_ref, True)):
            pre = _gate_pre(lr, w_ref, b_ref)
            la = (jnp.minimum(pre, 0.0) - jnp.log(1.0 + jnp.exp(-jnp.abs(pre)))) * GLA_GATE_SCALE
            o_ref[...] = _chunk_scan(la, ric, suffix)

    wspec = pl.BlockSpec((128, GLA_K_TOTAL), lambda i: (0, 0))
    bspec = pl.BlockSpec((1, GLA_K_TOTAL), lambda i: (0, 0))
    tile = pl.BlockSpec((ts, GLA_K_TOTAL), lambda i: (i, 0))
    return pl.pallas_call(
        body,
        name=name,
        grid=(s // ts,),
        in_specs=[pl.BlockSpec((ts, 128), lambda i: (i, LR_COL // 128)), wspec, wspec, bspec, bspec],
        out_specs=[tile, tile],
        out_shape=[jax.ShapeDtypeStruct((s, GLA_K_TOTAL), F32)] * 2,
        compiler_params=_cp(("parallel",)),
    )(z, waf, wab, baf, bab)


def _gate_bwd(z, waf, wab, baf, bab, dbf, dbb, dqkv_f, dqkv_b, *, name):
    s = z.shape[0]
    ts = _rows(s, light=True)

    def body(lr_ref, waf_ref, wab_ref, baf_ref, bab_ref, dbf_ref, dbb_ref, gf_ref, gb_ref, dzb_ref, dwf_ref, dwb_ref, dbaf_ref, dbab_ref):
        lr = lr_ref[...]
        ric = lax.broadcasted_iota(jnp.int32, (ts, GLA_K_TOTAL), 0) & (GLA_CHUNK - 1)
        first = pl.program_id(0) == 0
        dlr = None
        for w_ref, b_ref, db_ref, dw_ref, dbias_ref, suffix in (
            (waf_ref, baf_ref, dbf_ref, dwf_ref, dbaf_ref, True),
            (wab_ref, bab_ref, dbb_ref, dwb_ref, dbab_ref, False),
        ):
            pre = _gate_pre(lr, w_ref, b_ref)
            dla = _chunk_scan(db_ref[...], ric, suffix)
            dpre = dla * GLA_GATE_SCALE * _sigmoid(-pre)
            part = _dot_nt(dpre, w_ref[...])
            dlr = part if dlr is None else dlr + part
            dw = _dot_tn(lr, dpre)
            dbias = jnp.sum(dpre, axis=0, keepdims=True)

            @pl.when(first)
            def _():
                dw_ref[...] = dw
                dbias_ref[...] = dbias

            @pl.when(jnp.logical_not(first))
            def _():
                dw_ref[...] += dw
                dbias_ref[...] += dbias

        dqkv = gf_ref[...].astype(F32) + gb_ref[...].astype(F32)
        dzb_ref[...] = jnp.concatenate([dqkv, dlr], axis=1).astype(dzb_ref.dtype)

    wspec = pl.BlockSpec((128, GLA_K_TOTAL), lambda i: (0, 0))
    bspec = pl.BlockSpec((1, GLA_K_TOTAL), lambda i: (0, 0))
    tile = pl.BlockSpec((ts, GLA_K_TOTAL), lambda i: (i, 0))
    wide = pl.BlockSpec((ts, 2 * GLA_K_TOTAL + GLA_V_TOTAL), lambda i: (i, 0))
    return pl.pallas_call(
        body,
        name=name,
        grid=(s // ts,),
        in_specs=[pl.BlockSpec((ts, 128), lambda i: (i, LR_COL // 128)), wspec, wspec, bspec, bspec, tile, tile, wide, wide],
        out_specs=[pl.BlockSpec((ts, ZB_COLS), lambda i: (i, 0)), wspec, wspec, bspec, bspec],
        out_shape=[
            jax.ShapeDtypeStruct((s, ZB_COLS), _CD),
            jax.ShapeDtypeStruct((128, GLA_K_TOTAL), F32),
            jax.ShapeDtypeStruct((128, GLA_K_TOTAL), F32),
            jax.ShapeDtypeStruct((1, GLA_K_TOTAL), F32),
            jax.ShapeDtypeStruct((1, GLA_K_TOTAL), F32),
        ],
        compiler_params=_cp(("arbitrary",)),
    )(z, waf, wab, baf, bab, dbf, dbb, dqkv_f, dqkv_b)


def _gla_masks(rev):
    lane_head = lax.broadcasted_iota(jnp.int32, (1, GLA_K_TOTAL), 1) >> 6
    head_masks = [lane_head == h for h in range(GLA_HEADS)]
    t = lax.broadcasted_iota(jnp.int32, (GLA_HEADS * GLA_CHUNK, GLA_CHUNK), 0) & (GLA_CHUNK - 1)
    u = lax.broadcasted_iota(jnp.int32, (GLA_HEADS * GLA_CHUNK, GLA_CHUNK), 1)
    tri = (u > t) if rev else (u <= t)
    row = lax.broadcasted_iota(jnp.int32, (GLA_CHUNK, GLA_K_TOTAL), 0)
    total_row = row == (0 if rev else GLA_CHUNK - 1)
    return head_masks, tri, total_row


def _spread(a, head_masks):
    return jnp.concatenate([jnp.where(m, a, 0.0) for m in head_masks], axis=0)


def _stack(a):
    return jnp.concatenate([a[:, GLA_DV * h : GLA_DV * (h + 1)] for h in range(GLA_HEADS)], axis=0)


def _unstack(a):
    return jnp.concatenate([a[GLA_CHUNK * h : GLA_CHUNK * (h + 1)] for h in range(GLA_HEADS)], axis=1)


def _collect(a, head_masks):
    out = None
    for h, m in enumerate(head_masks):
        part = jnp.where(m, a[GLA_CHUNK * h : GLA_CHUNK * (h + 1)], 0.0)
        out = part if out is None else out + part
    return out


def _gla_chunk_terms(q_ref, k_ref, v_ref, b_ref, rows, head_masks, tri, total_row):
    q = q_ref[rows, :] * (GLA_DK**-0.5)
    k = k_ref[rows, :]
    v = v_ref[rows, :]
    b = b_ref[rows, :]
    eb = jnp.exp(b)
    enb = jnp.exp(-b)
    g = jnp.sum(jnp.where(total_row, b, 0.0), axis=0, keepdims=True)
    egb = jnp.exp(g - b)
    qt = q * eb
    kt = k * enb
    kh = k * egb
    q_heads = _spread(qt, head_masks)
    attn = jnp.where(tri, _dot_nt(q_heads, kt), 0.0)
    return v, eb, enb, egb, jnp.exp(g), qt, kt, kh, q_heads, attn


def _gla_specs(s, tb, rev_blocks):
    nb = s // tb
    rb = (lambda i: nb - 1 - i) if rev_blocks else (lambda i: i)
    q_spec = pl.BlockSpec((tb, GLA_K_TOTAL), lambda i: (rb(i), 0))
    k_spec = pl.BlockSpec((tb, GLA_K_TOTAL), lambda i: (rb(i), 1))
    v_spec = pl.BlockSpec((tb, GLA_V_TOTAL), lambda i: (rb(i), 1))
    b_spec = pl.BlockSpec((tb, GLA_K_TOTAL), lambda i: (rb(i), 0))
    o_spec = pl.BlockSpec((tb, GLA_V_TOTAL), lambda i: (rb(i), 0))
    st_spec = pl.BlockSpec((tb // GLA_CHUNK, GLA_DV, GLA_K_TOTAL), lambda i: (rb(i), 0, 0))
    return nb, q_spec, k_spec, v_spec, b_spec, o_spec, st_spec


def _gla_fwd_chunk(cidx, q_ref, k_ref, v_ref, b_ref, o_ref, sv_ref, st_ref, masks):
    head_masks, tri, total_row = masks
    rows = pl.ds(pl.multiple_of(cidx * GLA_CHUNK, GLA_CHUNK), GLA_CHUNK)
    v, _, _, _, eg, _, _, kh, q_heads, attn = _gla_chunk_terms(q_ref, k_ref, v_ref, b_ref, rows, head_masks, tri, total_row)
    o = jnp.concatenate(
        [_dot(attn[GLA_CHUNK * h : GLA_CHUNK * (h + 1)], v[:, GLA_DV * h : GLA_DV * (h + 1)]) for h in range(GLA_HEADS)], axis=1
    )
    st = st_ref[...]
    o_ref[rows, :] = o + _unstack(_dot_nt(q_heads, st))
    sv_ref[cidx] = st
    st_ref[...] = st * eg + _dot_tn(_stack(v), _spread(kh, head_masks))


def _gla_fwd(z, b_f, b_b, *, name):
    s = z.shape[0]
    tb = _rows(s)
    cpb = tb // GLA_CHUNK
    nb, qf, kf, vf, bf, of, sf = _gla_specs(s, tb, False)
    _, qr, kr, vr, br, orr, sr = _gla_specs(s, tb, True)

    def body(qf_ref, kf_ref, vf_ref, bf_ref, qr_ref, kr_ref, vr_ref, br_ref, of_ref, svf_ref, or_ref, svr_ref, stf_ref, str_ref):
        masks_f, masks_r = _gla_masks(False), _gla_masks(True)

        @pl.when(pl.program_id(0) == 0)
        def _():
            stf_ref[...] = jnp.zeros_like(stf_ref)
            str_ref[...] = jnp.zeros_like(str_ref)

        def chunk(ci, carry):
            _gla_fwd_chunk(ci, qf_ref, kf_ref, vf_ref, bf_ref, of_ref, svf_ref, stf_ref, masks_f)
            _gla_fwd_chunk(cpb - 1 - ci, qr_ref, kr_ref, vr_ref, br_ref, or_ref, svr_ref, str_ref, masks_r)
            return carry

        lax.fori_loop(0, cpb, chunk, 0)

    o_shape = jax.ShapeDtypeStruct((s, GLA_V_TOTAL), F32)
    st_shape = jax.ShapeDtypeStruct((s // GLA_CHUNK, GLA_DV, GLA_K_TOTAL), F32)
    return pl.pallas_call(
        body,
        name=name,
        grid=(nb,),
        in_specs=[qf, kf, vf, bf, qr, kr, vr, br],
        out_specs=[of, sf, orr, sr],
        out_shape=[o_shape, st_shape, o_shape, st_shape],
        scratch_shapes=[pltpu.VMEM((GLA_DV, GLA_K_TOTAL), F32)] * 2,
        compiler_params=_cp(("arbitrary",)),
    )(z, z, z, b_f, z, z, z, b_b)


def _gla_bwd_chunk(cidx, q_ref, k_ref, v_ref, b_ref, do_ref, sv_ref, dqkv_ref, db_ref, dst_ref, masks):
    head_masks, tri, total_row = masks
    rows = pl.ds(pl.multiple_of(cidx * GLA_CHUNK, GLA_CHUNK), GLA_CHUNK)
    v, eb, enb, egb, eg, qt, kt, kh, q_heads, attn = _gla_chunk_terms(q_ref, k_ref, v_ref, b_ref, rows, head_masks, tri, total_row)
    do_c = do_ref[rows, :]
    st = sv_ref[cidx]
    dst = dst_ref[...]
    do_s, v_s = _stack(do_c), _stack(v)
    hs = lambda a, h: a[GLA_CHUNK * h : GLA_CHUNK * (h + 1)]
    vs = lambda a, h: a[:, GLA_DV * h : GLA_DV * (h + 1)]
    dattn = jnp.concatenate([_dot_nt(vs(do_c, h), vs(v, h)) for h in range(GLA_HEADS)], axis=0)
    dattn = jnp.where(tri, dattn, 0.0)
    dv = jnp.concatenate([_dot_tn(hs(attn, h), vs(do_c, h)) for h in range(GLA_HEADS)], axis=1)
    dv = dv + _unstack(_dot_nt(_spread(kh, head_masks), dst))
    dqt = _collect(_dot(do_s, st) + _dot(dattn, kt), head_masks)
    dkt = _dot_tn(dattn, q_heads)
    dkh = _collect(_dot(v_s, dst), head_masks)
    dg = jnp.sum(dkh * kh, axis=0, keepdims=True) + jnp.sum(dst * st, axis=0, keepdims=True) * eg
    db = dqt * qt - dkt * kt - dkh * kh + jnp.where(total_row, dg, 0.0)
    dq = dqt * eb * (GLA_DK**-0.5)
    dk = dkt * enb + dkh * egb
    dqkv_ref[rows, :] = jnp.concatenate([dq, dk, dv], axis=1).astype(dqkv_ref.dtype)
    db_ref[rows, :] = db
    dst_ref[...] = dst * eg + _dot_tn(do_s, q_heads)


def _gla_bwd(z, b_f, b_b, do, st_f, st_b, *, name):
    s = z.shape[0]
    tb = _rows(s)
    cpb = tb // GLA_CHUNK
    wide = 2 * GLA_K_TOTAL + GLA_V_TOTAL
    nb, qf, kf, vf, bf, of, sf = _gla_specs(s, tb, True)
    _, qr, kr, vr, br, orr, sr = _gla_specs(s, tb, False)
    gf = pl.BlockSpec((tb, wide), lambda i: (nb - 1 - i, 0))
    gr = pl.BlockSpec((tb, wide), lambda i: (i, 0))

    def body(qf_ref, kf_ref, vf_ref, bf_ref, dof_ref, svf_ref, qr_ref, kr_ref, vr_ref, br_ref, dor_ref, svr_ref,
             gf_ref, dbf_ref, gr_ref, dbr_ref, dstf_ref, dstr_ref):
        masks_f, masks_r = _gla_masks(False), _gla_masks(True)

        @pl.when(pl.program_id(0) == 0)
        def _():
            dstf_ref[...] = jnp.zeros_like(dstf_ref)
            dstr_ref[...] = jnp.zeros_like(dstr_ref)

        def chunk(ci, carry):
            _gla_bwd_chunk(cpb - 1 - ci, qf_ref, kf_ref, vf_ref, bf_ref, dof_ref, svf_ref, gf_ref, dbf_ref, dstf_ref, masks_f)
            _gla_bwd_chunk(ci, qr_ref, kr_ref, vr_ref, br_ref, dor_ref, svr_ref, gr_ref, dbr_ref, dstr_ref, masks_r)
            return carry

        lax.fori_loop(0, cpb, chunk, 0)

    g_shape = jax.ShapeDtypeStruct((s, wide), _CD)
    db_shape = jax.ShapeDtypeStruct((s, GLA_K_TOTAL), F32)
    return pl.pallas_call(
        body,
        name=name,
        grid=(nb,),
        in_specs=[qf, kf, vf, bf, of, sf, qr, kr, vr, br, orr, sr],
        out_specs=[gf, bf, gr, br],
        out_shape=[g_shape, db_shape, g_shape, db_shape],
        scratch_shapes=[pltpu.VMEM((GLA_DV, GLA_K_TOTAL), F32)] * 2,
        compiler_params=_cp(("arbitrary",)),
    )(z, z, z, b_f, do, st_f, z, z, z, b_b, do, st_b)


HALO = 8


def _halo_specs(s, ts, width, col):
    last = s // HALO - 1
    per = ts // HALO
    prev = pl.BlockSpec((HALO, width), lambda i: (jnp.maximum(i * per - 1, 0), col))
    nxt = pl.BlockSpec((HALO, width), lambda i: (jnp.minimum((i + 1) * per, last), col))
    return prev, nxt


def _group_ones():
    group = jnp.arange(CONV_WIDTH, dtype=jnp.int32) // CONV_GROUP
    return (group[:, None] == group[None, :]).astype(BF16)


_ONES_SPEC = pl.BlockSpec((CONV_WIDTH, CONV_WIDTH), lambda i: (0, 0))


def _conv_terms(cc_ext, cu_ext, cw, valid):
    n = cc_ext.shape[0]
    hc = jnp.where(valid, cc_ext * cu_ext, 0.0)
    hc_prev = pltpu.roll(hc, 1, 0)
    hc_next = pltpu.roll(hc, n - 1, 0)
    conv = cw[0:1] * hc_prev + cw[1:2] * hc + cw[2:3] * hc_next
    return hc, hc_prev, hc_next, conv


def _ext(prev_ref, cur_ref, next_ref):
    return jnp.concatenate([prev_ref[...], cur_ref[...], next_ref[...]], axis=0)


def _valid_rows(ts, s):
    row = lax.broadcasted_iota(jnp.int32, (ts + 2 * HALO, 1), 0) + (pl.program_id(0) * ts - HALO)
    return (row >= 0) & (row < s)


def _head_norm(o, gn):
    out = []
    for h in range(GLA_HEADS):
        oh = o[:, GLA_DV * h : GLA_DV * (h + 1)]
        r = lax.rsqrt(jnp.mean(oh * oh, axis=-1, keepdims=True) + EPS)
        out.append((oh * r, r))
    return out


def _mix_fwd(z, o_f, o_b, conv_w, conv_norm, gla_norm, *, name):
    s = z.shape[0]
    ts = _rows(s, light=True)
    cprev, cnext = _halo_specs(s, ts, CONV_WIDTH, 1)
    uprev, unext = _halo_specs(s, ts, CONV_WIDTH, 2)

    def body(cb_ref, cc_ref, cu_ref, ccp_ref, ccn_ref, cup_ref, cun_ref, g_ref, of_ref, ob_ref, cw_ref, cn_ref, gn_ref, ones_ref, y_ref):
        valid = _valid_rows(ts, s)
        _, _, _, conv = _conv_terms(_ext(ccp_ref, cc_ref, ccn_ref), _ext(cup_ref, cu_ref, cun_ref), cw_ref[...], valid)
        yc = cb_ref[...] * conv[HALO : HALO + ts]
        ms = _dot_split(yc * yc, ones_ref[...]) * (1.0 / CONV_GROUP)
        y_conv = yc * lax.rsqrt(ms + EPS) * cn_ref[...]
        gate = g_ref[...]
        silu = gate * _sigmoid(gate)
        gn = gn_ref[...]
        y_gla = jnp.concatenate([oh * gn for oh, _ in _head_norm(of_ref[...] + ob_ref[...], gn)], axis=1) * silu
        y_ref[...] = jnp.concatenate([y_conv, y_gla], axis=1).astype(y_ref.dtype)

    col = lambda c, w=CONV_WIDTH: pl.BlockSpec((ts, w), lambda i: (i, c))
    return pl.pallas_call(
        body,
        name=name,
        grid=(s // ts,),
        in_specs=[col(0), col(1), col(2), cprev, cnext, uprev, unext, col(3), col(0), col(0),
                  pl.BlockSpec((CONV_K, CONV_WIDTH), lambda i: (0, 0)), pl.BlockSpec((1, CONV_WIDTH), lambda i: (0, 0)),
                  pl.BlockSpec((1, GLA_DV), lambda i: (0, 0)), _ONES_SPEC],
        out_specs=pl.BlockSpec((ts, D_MODEL), lambda i: (i, 0)),
        out_shape=jax.ShapeDtypeStruct((s, D_MODEL), _CD),
        compiler_params=_cp(("parallel",)),
    )(z, z, z, z, z, z, z, z, o_f, o_b, conv_w, conv_norm, gla_norm, _group_ones())


def _mix_bwd(z, o_f, o_b, dy, conv_w, conv_norm, gla_norm, *, name):
    s = z.shape[0]
    ts = _rows(s)
    halos = [_halo_specs(s, ts, CONV_WIDTH, c) for c in (0, 1, 2)]
    dprev, dnext = _halo_specs(s, ts, CONV_WIDTH, 0)

    def body(cb_ref, cc_ref, cu_ref, cbp_ref, cbn_ref, ccp_ref, ccn_ref, cup_ref, cun_ref, g_ref, of_ref, ob_ref,
             dyc_ref, dyg_ref, dyp_ref, dyn_ref, cw_ref, cn_ref, gn_ref, ones_ref, dza_ref, do_ref, dcw_ref, dcn_ref, dgn_ref):
        n = ts + 2 * HALO
        valid = _valid_rows(ts, s)
        cw = cw_ref[...]
        cn = cn_ref[...]
        ones = ones_ref[...]
        cb = _ext(cbp_ref, cb_ref, cbn_ref)
        cc = _ext(ccp_ref, cc_ref, ccn_ref)
        cu = _ext(cup_ref, cu_ref, cun_ref)
        dy = _ext(dyp_ref, dyc_ref, dyn_ref)
        hc, hc_prev, hc_next, conv = _conv_terms(cc, cu, cw, valid)
        yc = cb * conv
        r = lax.rsqrt(_dot_split(yc * yc, ones) * (1.0 / CONV_GROUP) + EPS)
        yh = yc * r
        dyh = dy * cn
        dyc = r * (dyh - yh * (_dot_split(dyh * yh, ones) * (1.0 / CONV_GROUP)))
        dconv = jnp.where(valid, dyc * cb, 0.0)
        dhc = cw[0:1] * pltpu.roll(dconv, n - 1, 0) + cw[1:2] * dconv + cw[2:3] * pltpu.roll(dconv, 1, 0)
        mid = lambda a: a[HALO : HALO + ts]
        dza_ref[:, 0 : 3 * CONV_WIDTH] = jnp.concatenate([mid(dyc * conv), mid(dhc * cu), mid(dhc * cc)], axis=1).astype(dza_ref.dtype)
        dconv_m = mid(dconv)
        colsum = lambda a: jnp.sum(a, axis=0, keepdims=True)
        dcw = jnp.concatenate([colsum(dconv_m * mid(hc_prev)), colsum(dconv_m * mid(hc)), colsum(dconv_m * mid(hc_next))], axis=0)
        dcn = colsum(mid(dy * yh))

        gate = g_ref[...]
        sg = _sigmoid(gate)
        silu = gate * sg
        gn = gn_ref[...]
        dyg = dyg_ref[...]
        don = dyg * silu
        heads = _head_norm(of_ref[...] + ob_ref[...], gn)
        on = jnp.concatenate([oh * gn for oh, _ in heads], axis=1)
        dza_ref[:, 3 * CONV_WIDTH : ZA_COLS] = (dyg * on * (sg * (1.0 + gate * (1.0 - sg)))).astype(dza_ref.dtype)
        dgn = jnp.zeros((1, GLA_DV), F32)
        dos = []
        for h, (oh, rh) in enumerate(heads):
            donh = don[:, GLA_DV * h : GLA_DV * (h + 1)]
            dgn = dgn + colsum(donh * oh)
            doh = donh * gn
            dos.append(rh * (doh - oh * jnp.mean(doh * oh, axis=-1, keepdims=True)))
        do_ref[...] = jnp.concatenate(dos, axis=1)

        first = pl.program_id(0) == 0

        @pl.when(first)
        def _():
            dcw_ref[...] = dcw
            dcn_ref[...] = dcn
            dgn_ref[...] = dgn

        @pl.when(jnp.logical_not(first))
        def _():
            dcw_ref[...] += dcw
            dcn_ref[...] += dcn
            dgn_ref[...] += dgn

    col = lambda c, w=CONV_WIDTH: pl.BlockSpec((ts, w), lambda i: (i, c))
    cw_spec = pl.BlockSpec((CONV_K, CONV_WIDTH), lambda i: (0, 0))
    cn_spec = pl.BlockSpec((1, CONV_WIDTH), lambda i: (0, 0))
    gn_spec = pl.BlockSpec((1, GLA_DV), lambda i: (0, 0))
    return pl.pallas_call(
        body,
        name=name,
        grid=(s // ts,),
        in_specs=[col(0), col(1), col(2), halos[0][0], halos[0][1], halos[1][0], halos[1][1], halos[2][0], halos[2][1],
                  col(3), col(0), col(0), col(0), col(1), dprev, dnext, cw_spec, cn_spec, gn_spec, _ONES_SPEC],
        out_specs=[pl.BlockSpec((ts, ZA_COLS), lambda i: (i, 0)), col(0), cw_spec, cn_spec, gn_spec],
        out_shape=[
            jax.ShapeDtypeStruct((s, ZA_COLS), _CD),
            jax.ShapeDtypeStruct((s, GLA_V_TOTAL), F32),
            jax.ShapeDtypeStruct((CONV_K, CONV_WIDTH), F32),
            jax.ShapeDtypeStruct((1, CONV_WIDTH), F32),
            jax.ShapeDtypeStruct((1, GLA_DV), F32),
        ],
        compiler_params=_cp(("arbitrary",)),
    )(z, z, z, z, z, z, z, z, z, z, o_f, o_b, dy, dy, dy, dy, conv_w, conv_norm, gla_norm, _group_ones())


def _xa_probs(q_ref, kv_ref, h):
    qh = q_ref[:, XA_HEAD_DIM * h : XA_HEAD_DIM * (h + 1)]
    kh = kv_ref[:, XA_HEAD_DIM * h : XA_HEAD_DIM * (h + 1)]
    vh = kv_ref[:, D_MODEL + XA_HEAD_DIM * h : D_MODEL + XA_HEAD_DIM * (h + 1)]
    sc = _dot_nt(qh, kh) * (XA_HEAD_DIM**-0.5)
    e = jnp.exp(sc - jnp.max(sc, axis=-1, keepdims=True))
    return qh, kh, vh, e / jnp.sum(e, axis=-1, keepdims=True)


def _xattn_fwd(qx, kv, *, name):
    s = qx.shape[0]
    ts = _rows(s, light=True, times=4)

    def body(q_ref, kv_ref, o_ref):
        outs = []
        for h in range(XA_HEADS):
            _, _, vh, p = _xa_probs(q_ref, kv_ref, h)
            outs.append(_dot(p, vh))
        o_ref[...] = jnp.concatenate(outs, axis=1).astype(o_ref.dtype)

    return pl.pallas_call(
        body,
        name=name,
        grid=(s // ts,),
        in_specs=[pl.BlockSpec((ts, D_MODEL), lambda i: (i, 0)), pl.BlockSpec((N_MEM, 2 * D_MODEL), lambda i: (0, 0))],
        out_specs=pl.BlockSpec((ts, D_MODEL), lambda i: (i, 0)),
        out_shape=jax.ShapeDtypeStruct((s, D_MODEL), _CD),
        compiler_params=_cp(("parallel",)),
    )(qx, kv)


def _xattn_bwd(qx, kv, dx, w_xo, *, name):
    s = qx.shape[0]
    ts = _rows(s, light=True)

    def body(q_ref, kv_ref, dx_ref, w_ref, dq_ref, dkv_ref):
        do = _dot_nt(dx_ref[...], w_ref[...]).astype(_CD)
        dqs, dks, dvs = [], [], []
        for h in range(XA_HEADS):
            qh, kh, vh, p = _xa_probs(q_ref, kv_ref, h)
            doh = do[:, XA_HEAD_DIM * h : XA_HEAD_DIM * (h + 1)]
            dp = _dot_nt(doh, vh)
            ds = p * (dp - jnp.sum(dp * p, axis=-1, keepdims=True)) * (XA_HEAD_DIM**-0.5)
            dqs.append(_dot(ds, kh))
            dks.append(_dot_tn(ds, qh))
            dvs.append(_dot_tn(p, doh))
        dq_ref[...] = jnp.concatenate(dqs, axis=1).astype(dq_ref.dtype)
        dkv = jnp.concatenate(dks + dvs, axis=1)

        @pl.when(pl.program_id(0) == 0)
        def _():
            dkv_ref[...] = dkv

        @pl.when(pl.program_id(0) > 0)
        def _():
            dkv_ref[...] += dkv

    tile = pl.BlockSpec((ts, D_MODEL), lambda i: (i, 0))
    kv_spec = pl.BlockSpec((N_MEM, 2 * D_MODEL), lambda i: (0, 0))
    return pl.pallas_call(
        body,
        name=name,
        grid=(s // ts,),
        in_specs=[tile, kv_spec, tile, pl.BlockSpec((D_MODEL, D_MODEL), lambda i: (0, 0))],
        out_specs=[tile, kv_spec],
        out_shape=[jax.ShapeDtypeStruct((s, D_MODEL), _CD), jax.ShapeDtypeStruct((N_MEM, 2 * D_MODEL), F32)],
        compiler_params=_cp(("arbitrary",)),
    )(qx, kv, dx, w_xo)


def _adamw_math(w, g, m, v):
    m = ADAM_B1 * m + (1.0 - ADAM_B1) * g
    v = ADAM_B2 * v + (1.0 - ADAM_B2) * (g * g)
    m_hat = m / (1.0 - ADAM_B1**ADAM_STEP)
    v_hat = v / (1.0 - ADAM_B2**ADAM_STEP)
    delta = -ADAM_LR * (m_hat / (jnp.sqrt(v_hat) + ADAM_EPS) + ADAM_WD * w)
    return delta, m, v


def _adamw(w, m, v, shard_rows, off, *, transposed, name):
    r, c = w.shape
    by_columns = r % 256 != 0
    tr = 512 if (c if by_columns else r) % 512 == 0 and off % 512 == 0 else 256
    if by_columns:
        assert not transposed and off == 0
        g_spec = tile = pl.BlockSpec((r, tr), lambda i: (0, i))
    else:
        g_spec = pl.BlockSpec((c, tr), lambda i: (off // c, i)) if transposed else pl.BlockSpec((tr, c), lambda i: (off // tr + i, 0))
        tile = pl.BlockSpec((tr, c), lambda i: (i, 0))

    def body(w_ref, g_ref, m_ref, v_ref, go_ref, d_ref, nm_ref, nv_ref):
        g = g_ref[...].T if transposed else g_ref[...]
        go_ref[...] = g
        d_ref[...], nm_ref[...], nv_ref[...] = _adamw_math(w_ref[...], g, m_ref[...], v_ref[...])

    return pl.pallas_call(
        body,
        name=name,
        grid=((c if by_columns else r) // tr,),
        in_specs=[tile, g_spec, tile, tile],
        out_specs=[tile] * 4,
        out_shape=[jax.ShapeDtypeStruct((r, c), F32)] * 4,
        compiler_params=_cp(("parallel",)),
    )(w, shard_rows, m, v)


def _adamw_small(groups, *, name):
    n = len(groups)

    def body(*refs):
        ins, outs = refs[: 4 * n], refs[4 * n :]
        for i in range(n):
            w_ref, g_ref, m_ref, v_ref = ins[4 * i : 4 * i + 4]
            outs[3 * i][...], outs[3 * i + 1][...], outs[3 * i + 2][...] = _adamw_math(w_ref[...], g_ref[...], m_ref[...], v_ref[...])

    flat = [a for grp in groups for a in grp]
    vm = pl.BlockSpec(memory_space=pltpu.VMEM)
    res = pl.pallas_call(
        body,
        name=name,
        in_specs=[vm] * (4 * n),
        out_specs=[vm] * (3 * n),
        out_shape=[jax.ShapeDtypeStruct(grp[0].shape, F32) for grp in groups for _ in range(3)],
        compiler_params=_cp(),
    )(*flat)
    return [tuple(res[3 * i : 3 * i + 3]) for i in range(n)]


def _place():
    return lax.axis_index("x"), lax.axis_index("y"), lax.axis_index("c")


def _rel_chip(x, y, k):
    return (1 - x if k & 2 else x), (1 - y if k & 1 else y)


def _half(c, rh):
    return pl.ds(pl.multiple_of(c * rh, 16), rh)


HBM = pl.BlockSpec(memory_space=pltpu.HBM)
SEM = pl.BlockSpec(memory_space=pltpu.SEMAPHORE)
EFFECT = pltpu.SideEffectType.DATAFLOW_SIDE_EFFECTING


def _in_hbm(a):
    return pltpu.with_memory_space_constraint(a, pltpu.HBM)


def _gather_copies(p_ref, land_ref, send_sems, recv_sems):
    rh = p_ref.shape[0] // 2
    x, y, c = _place()
    rows = _half(c, rh)
    copies = []
    for k in range(1, N_CHIPS):
        cx, cy = _rel_chip(x, y, k)
        copies.append(pltpu.make_async_remote_copy(
            src_ref=p_ref.at[rows], dst_ref=land_ref.at[2 * x + y, rows], send_sem=send_sems.at[k - 1], recv_sem=recv_sems.at[k - 1],
            device_id=(cx, cy, c), device_id_type=MESH))
    copies.append(pltpu.make_async_remote_copy(
        src_ref=p_ref, dst_ref=land_ref.at[2 * x + y], send_sem=send_sems.at[N_CHIPS - 1], recv_sem=recv_sems.at[N_CHIPS - 1],
        device_id=(x, y, 1 - c), device_id_type=MESH))
    return copies


def _gather_start(pack, after, *, name):
    r, w = pack.shape

    def body(p_ref, land_ref, after_ref, send_sems, recv_sems, p_thru, land_thru, token):
        for cp in _gather_copies(p_ref, land_ref, send_sems, recv_sems):
            cp.start()
        token[...] = jnp.zeros_like(token)

    return pl.pallas_call(
        body,
        name=name,
        out_shape=(pltpu.SemaphoreType.DMA((N_CHIPS,)), pltpu.SemaphoreType.DMA((N_CHIPS,)), pltpu.HBM((r, w), pack.dtype),
                   pltpu.HBM((N_CHIPS, r, w), pack.dtype), jax.ShapeDtypeStruct((8, 128), F32)),
        in_specs=(HBM, HBM, ANY),
        out_specs=(SEM, SEM, HBM, HBM, pl.BlockSpec(memory_space=pltpu.VMEM)),
        input_output_aliases={0: 2, 1: 3},
        compiler_params=pltpu.CompilerParams(has_side_effects=EFFECT),
    )(_in_hbm(pack), _in_hbm(lax.empty((N_CHIPS, r, w), pack.dtype)), after)


def _gather_wait(send_sems, recv_sems, pack, land, after, *, name):
    def body(p_ref, land_ref, send_sems, recv_sems, after_ref, p_out, land_out):
        for cp in _gather_copies(p_ref, land_ref, send_sems, recv_sems):
            cp.wait_send()
            cp.wait_recv()

    return pl.pallas_call(
        body,
        name=name,
        out_shape=(pltpu.HBM(pack.shape, pack.dtype), pltpu.HBM(land.shape, land.dtype)),
        in_specs=(HBM, HBM, SEM, SEM, ANY),
        out_specs=(HBM, HBM),
        input_output_aliases={0: 0, 1: 1},
        compiler_params=pltpu.CompilerParams(has_side_effects=EFFECT),
    )(pack, land, send_sems, recv_sems, after)


def _gather_spread(land, *, name):
    n, r, w = land.shape
    rh = r // 2

    def body(land_ref, o_ref, send_sems, recv_sems):
        x, y, c = _place()
        rows = _half(c, rh)
        copies = []
        for k in range(1, N_CHIPS):
            cx, cy = _rel_chip(x, y, k)
            copies.append(pltpu.make_async_remote_copy(
                src_ref=land_ref.at[2 * cx + cy, rows], dst_ref=o_ref.at[2 * cx + cy, rows], send_sem=send_sems.at[k - 1],
                recv_sem=recv_sems.at[k - 1], device_id=(x, y, 1 - c), device_id_type=MESH))
        for cp in copies:
            cp.start()
        for cp in copies:
            cp.wait()

    return pl.pallas_call(
        body,
        name=name,
        in_specs=[ANY],
        out_specs=ANY,
        out_shape=jax.ShapeDtypeStruct(land.shape, land.dtype),
        input_output_aliases={0: 0},
        scratch_shapes=[pltpu.SemaphoreType.DMA((N_CHIPS - 1,)), pltpu.SemaphoreType.DMA((N_CHIPS - 1,))],
        compiler_params=pltpu.CompilerParams(has_side_effects=True),
    )(land)


N_PARTS = 2 * (N_CHIPS - 1)


def _scatter_copies(lo_ref, g_ref, land_lo_ref, land_f_ref, send_sems, recv_sems, starting):
    rh = g_ref.shape[1] // 2
    x, y, c = _place()
    copies = []
    for k in range(1, N_CHIPS):
        cx, cy = _rel_chip(x, y, k)
        for i in range(2):
            part = 2 * (k - 1) + (c if starting else i)
            copies.append(pltpu.make_async_remote_copy(
                src_ref=lo_ref.at[2 * cx + cy, pl.ds(i * rh, rh)], dst_ref=land_lo_ref.at[part],
                send_sem=send_sems.at[2 * (k - 1) + i], recv_sem=recv_sems.at[part], device_id=(cx, cy, i), device_id_type=MESH))
    copies.append(pltpu.make_async_remote_copy(
        src_ref=g_ref.at[2 * x + y, _half(1 - c, rh)], dst_ref=land_f_ref, send_sem=send_sems.at[N_PARTS], recv_sem=recv_sems.at[N_PARTS],
        device_id=(x, y, 1 - c), device_id_type=MESH))
    return copies


def _scatter_start(g_lo, g, *, name):
    n, r, w = g.shape
    rh = r // 2

    def body(lo_ref, g_ref, land_lo_ref, land_f_ref, send_sems, recv_sems, lo_thru, g_thru, land_lo_thru, land_f_thru, token):
        for cp in _scatter_copies(lo_ref, g_ref, land_lo_ref, land_f_ref, send_sems, recv_sems, True):
            cp.start()
        token[...] = jnp.zeros_like(token)

    return pl.pallas_call(
        body,
        name=name,
        out_shape=(pltpu.SemaphoreType.DMA((N_PARTS + 1,)), pltpu.SemaphoreType.DMA((N_PARTS + 1,)), pltpu.HBM(g_lo.shape, g_lo.dtype),
                   pltpu.HBM(g.shape, g.dtype), pltpu.HBM((N_PARTS, rh, w), g_lo.dtype), pltpu.HBM((rh, w), g.dtype),
                   jax.ShapeDtypeStruct((8, 128), F32)),
        in_specs=(HBM, HBM, HBM, HBM),
        out_specs=(SEM, SEM, HBM, HBM, HBM, HBM, pl.BlockSpec(memory_space=pltpu.VMEM)),
        input_output_aliases={0: 2, 1: 3, 2: 4, 3: 5},
        compiler_params=pltpu.CompilerParams(has_side_effects=EFFECT),
    )(_in_hbm(g_lo), _in_hbm(g), _in_hbm(lax.empty((N_PARTS, rh, w), g_lo.dtype)), _in_hbm(lax.empty((rh, w), g.dtype)))


def _scatter_wait(send_sems, recv_sems, g_lo, g, land_lo, land_f, after, *, name):
    def body(lo_ref, g_ref, land_lo_ref, land_f_ref, send_sems, recv_sems, after_ref, o0, o1, o2, o3):
        for cp in _scatter_copies(lo_ref, g_ref, land_lo_ref, land_f_ref, send_sems, recv_sems, False):
            cp.wait_send()
            cp.wait_recv()

    arrays = (g_lo, g, land_lo, land_f)
    return pl.pallas_call(
        body,
        name=name,
        out_shape=tuple(pltpu.HBM(a.shape, a.dtype) for a in arrays),
        in_specs=(HBM, HBM, HBM, HBM, SEM, SEM, ANY),
        out_specs=(HBM, HBM, HBM, HBM),
        input_output_aliases={0: 0, 1: 1, 2: 2, 3: 3},
        compiler_params=pltpu.CompilerParams(has_side_effects=EFFECT),
    )(*arrays, send_sems, recv_sems, after)


def _scatter_sum(g, land_lo, land_f, where, *, name):
    n, r, w = g.shape
    rh = r // 2
    tr = _pick(rh, (256, 160, 80))
    nt = rh // tr

    def body(where_ref, g_ref, f_ref, lo_ref, o_ref):
        acc = g_ref[0] + f_ref[...]
        for part in range(N_PARTS):
            acc = acc + lo_ref[part].astype(F32)
        o_ref[...] = acc

    return pl.pallas_call(
        body,
        name=name,
        grid_spec=pltpu.PrefetchScalarGridSpec(
            num_scalar_prefetch=1,
            grid=(nt,),
            in_specs=[pl.BlockSpec((1, tr, w), lambda i, wh: (wh[1], wh[0] * nt + i, 0)),
                      pl.BlockSpec((tr, w), lambda i, wh: (i, 0)),
                      pl.BlockSpec((N_PARTS, tr, w), lambda i, wh: (0, i, 0))],
            out_specs=pl.BlockSpec((tr, w), lambda i, wh: (wh[0] * nt + i, 0)),
        ),
        out_shape=jax.ShapeDtypeStruct((r, w), F32),
        compiler_params=_cp(("parallel",)),
    )(where, g, land_f, land_lo)


def _swap_all(shards, *, name):
    n = len(shards)

    def body(*refs):
        ins, outs = refs[:n], refs[n : 2 * n]
        send_sems, recv_sems = refs[2 * n :]
        x, y, c = _place()
        copies = []
        for i, (e_ref, o_ref) in enumerate(zip(ins, outs)):
            rows = _half(c, e_ref.shape[0] // 2)
            copies.append(pltpu.make_async_remote_copy(src_ref=e_ref.at[rows], dst_ref=o_ref.at[rows], send_sem=send_sems.at[i],
                                                       recv_sem=recv_sems.at[i], device_id=(x, y, 1 - c), device_id_type=MESH))
        for cp in copies:
            cp.start()
        for cp in copies:
            cp.wait()

    return pl.pallas_call(
        body,
        name=name,
        in_specs=[ANY] * n,
        out_specs=[ANY] * n,
        out_shape=[jax.ShapeDtypeStruct(e.shape, e.dtype) for e in shards],
        input_output_aliases={i: i for i in range(n)},
        scratch_shapes=[pltpu.SemaphoreType.DMA((n,)), pltpu.SemaphoreType.DMA((n,))],
        compiler_params=pltpu.CompilerParams(has_side_effects=True),
    )(*shards)


def _sum_small(small, after):
    n_dev = 8

    def body(s_ref, after_ref, o_ref, all_ref, send_sems, recv_sems):
        x, y, c = _place()
        me = 4 * x + 2 * y + c
        all_ref[me] = s_ref[...]
        copies = []
        for k in range(1, n_dev):
            cx, cy = _rel_chip(x, y, k >> 1)
            cc = 1 - c if k & 1 else c
            copies.append(pltpu.make_async_remote_copy(
                src_ref=s_ref, dst_ref=all_ref.at[me], send_sem=send_sems.at[k - 1], recv_sem=recv_sems.at[k - 1],
                device_id=(cx, cy, cc), device_id_type=MESH))
        for cp in copies:
            cp.start()
        for cp in copies:
            cp.wait()
        acc = all_ref[0]
        for a in range(1, n_dev):
            acc = acc + all_ref[a]
        o_ref[...] = acc

    vm = pl.BlockSpec(memory_space=pltpu.VMEM)
    return pl.pallas_call(
        body,
        name="sum_small",
        in_specs=[vm, ANY],
        out_specs=vm,
        out_shape=jax.ShapeDtypeStruct(small.shape, F32),
        scratch_shapes=[pltpu.VMEM((n_dev,) + small.shape, F32), pltpu.SemaphoreType.DMA((n_dev - 1,)), pltpu.SemaphoreType.DMA((n_dev - 1,))],
        compiler_params=pltpu.CompilerParams(has_side_effects=True),
    )(small, after)


MATS = {"w_in": (776, True), "w_out": (256, False), "w_xq": (256, False), "w_xkv": (512, True), "w_xo": (256, False),
        "w_up": (1024, True), "w_down": (1024, False)}
GATHER_FIRST = ("w_in",)
GATHER_REST = ("w_out", "w_xq", "w_xkv", "w_xo", "w_up", "w_down")
GRAD_GROUPS = (("w_up", "w_down"), ("w_out", "w_xq", "w_xkv", "w_xo"), ("w_in",))


def _group_rows(names):
    n = sum(MATS[name][0] for name in names)
    return n + (-n) % 32


def _pack(pieces, rows):
    p = jnp.concatenate(pieces, axis=0) if len(pieces) > 1 else pieces[0]
    return jnp.pad(p, ((0, rows - p.shape[0]), (0, 0))) if rows > p.shape[0] else p


SMALL = (
    ("mix_norm", 1024), ("conv_norm", 512), ("b_af", 256), ("b_ab", 256), ("gla_norm", 128), ("xa_norm", 1024), ("mem_norm", 1024),
    ("mlp_norm", 1024), ("final_norm", 1024), ("conv_w", 1536), ("w_af", 4096), ("w_ab", 4096), ("loss", 128),
)


def kernel(x, mem, mix_norm, w_in, conv_w, conv_norm, w_af, b_af, w_ab, b_ab, gla_norm, w_out, xa_norm, mem_norm, w_xq, w_xkv, w_xo, mlp_norm, w_up, w_down, final_norm, loss_target, m_mix_norm, m_w_in, m_conv_w, m_conv_norm, m_w_af, m_b_af, m_w_ab, m_b_ab, m_gla_norm, m_w_out, m_xa_norm, m_mem_norm, m_w_xq, m_w_xkv, m_w_xo, m_mlp_norm, m_w_up, m_w_down, m_final_norm, v_mix_norm, v_w_in, v_conv_w, v_conv_norm, v_w_af, v_b_af, v_w_ab, v_b_ab, v_gla_norm, v_w_out, v_xa_norm, v_mem_norm, v_w_xq, v_w_xkv, v_w_xo, v_mlp_norm, v_w_up, v_w_down, v_final_norm):
    given = dict(locals())
    xi, yi, ci = _place()
    chip = 2 * xi + yi
    where = jnp.stack([ci, chip]).astype(jnp.int32)

    lo = {name: (given[name][0].T if MATS[name][1] else given[name][0]).astype(_CD) for name in MATS}
    pack_rest = _pack([lo[name] for name in GATHER_REST], _group_rows(GATHER_REST))
    pack_first = _pack([lo[name] for name in GATHER_FIRST], _group_rows(GATHER_FIRST))
    xs, mems, tgt = x[0], mem[0], loss_target[0]
    behind = lambda gain, token: gain + token[0, 0]

    def placed(shard, full_shape, col):
        return lax.dynamic_update_slice(jnp.zeros(full_shape, F32), shard, (0, col)).reshape(-1, 128)

    sw = jnp.concatenate([
        placed(conv_w[0], (CONV_K, CONV_WIDTH), 128 * chip),
        placed(w_af[0], (GLA_LOWRANK, GLA_K_TOTAL), 64 * chip),
        placed(w_ab[0], (GLA_LOWRANK, GLA_K_TOTAL), 64 * chip),
    ], axis=0)
    sw = jnp.pad(sw, ((0, SMALL_ROWS - sw.shape[0]), (0, 0))) * (ci == 0).astype(F32)
    sw = _sum_small(sw, mix_norm)

    first_send, first_recv, pack_first, land_first, first_token = _gather_start(pack_first, sw, name="gather_first_start")
    rest_send, rest_recv, pack_rest, land_rest, rest_token = _gather_start(pack_rest, first_token, name="gather_rest_start")
    h1 = _rms_fwd(xs, behind(mix_norm, rest_token), name="norm_mix")
    pack_first, land_first = _gather_wait(first_send, first_recv, pack_first, land_first, h1, name="gather_first_wait")
    got_first = _gather_spread(land_first, name="gather_first_spread")

    def whole(got, off, rows):
        return got[:, off : off + rows].reshape(N_CHIPS * rows, D_MODEL)

    w_in_t = whole(got_first, 0, MATS["w_in"][0])
    w_za = jnp.concatenate([w_in_t[0:1536], w_in_t[2560:3072]], axis=0)
    w_zb = jnp.concatenate([w_in_t[1536:2560], w_in_t[3072:W_IN_COLS], jnp.zeros((ZB_COLS - 1056, D_MODEL), _CD)], axis=0)
    conv_w_full = sw[0:12].reshape(CONV_K, CONV_WIDTH)
    w_af_full = sw[12:44].reshape(GLA_LOWRANK, GLA_K_TOTAL)
    w_ab_full = sw[44:76].reshape(GLA_LOWRANK, GLA_K_TOTAL)
    waf_p = jnp.pad(w_af_full, ((0, 128 - GLA_LOWRANK), (0, 0))).astype(_CD)
    wab_p = jnp.pad(w_ab_full, ((GLA_LOWRANK, 128 - 2 * GLA_LOWRANK), (0, 0))).astype(_CD)

    z_a, z_b = _mm_two(h1, w_za, w_zb, name="proj_in")
    b_f, b_b = _gate_fwd(z_b, waf_p, wab_p, b_af, b_ab, name="gates")
    o_f, st_f, o_b, st_b = _gla_fwd(z_b, b_f, b_b, name="gla_scan")
    y = _mix_fwd(z_a, o_f, o_b, conv_w_full, conv_norm, gla_norm, name="mix_out")
    pack_rest, land_rest = _gather_wait(rest_send, rest_recv, pack_rest, land_rest, y, name="gather_rest_wait")
    gathered = _gather_spread(land_rest, name="gather_rest_spread")
    wt, off = {}, 0
    for name in GATHER_REST:
        wt[name] = whole(gathered, off, MATS[name][0])
        off += MATS[name][0]
    x1, hx = _mm_rows(y, wt["w_out"], mode="nn", name="proj_out", rows=(xs,), vecs=(xa_norm,), out_rows=(F32, _CD),
                      epilogue=_ep_residual_norm, tm=1024)
    qx = _mm(hx, wt["w_xq"], mode="nn", name="proj_xq", out_dtypes=(_CD,))
    hmem = _rms_fwd(mems, mem_norm, name="norm_mem")
    kv = _mm(hmem, wt["w_xkv"], mode="nt", name="proj_xkv", out_dtypes=(_CD,))
    ox = _xattn_fwd(qx, kv, name="xattn")
    x2, hm = _mm_rows(ox, wt["w_xo"], mode="nn", name="proj_xo", rows=(x1,), vecs=(mlp_norm,), out_rows=(F32, _CD),
                      epilogue=_ep_residual_norm, tm=1024)
    act, relu_u = _mm(hm, wt["w_up"], mode="nt", name="mlp_up", out_dtypes=(_CD, _CD), tm=2048,
                      epilogue=lambda acc: (jnp.square(jnp.maximum(acc, 0.0)), jnp.maximum(acc, 0.0)))
    dx3, dx3_lo, loss_part, g_final_norm = _mm_rows(
        act, wt["w_down"], mode="nn", name="mlp_down", rows=(x2, tgt), vecs=(final_norm.reshape(1, D_MODEL),),
        out_rows=(F32, _CD), out_vecs=(128, D_MODEL), epilogue=_ep_loss)

    grads_t = {}

    def start_group(names, tag):
        rows = _group_rows(names)
        g = jnp.stack([_pack([grads_t[name][a * MATS[name][0] : (a + 1) * MATS[name][0]] for name in names], rows) for a in range(N_CHIPS)])
        return _scatter_start(g.astype(_TD), g, name="grads_" + tag + "_start")

    def finish_group(state, after, tag):
        send_sems, recv_sems, g_lo, g, land_lo, land_f, _ = state
        g_lo, g, land_lo, land_f = _scatter_wait(send_sems, recv_sems, g_lo, g, land_lo, land_f, after, name="grads_" + tag + "_wait")
        return _scatter_sum(g, land_lo, land_f, where, name="grads_" + tag + "_sum")

    def new_packs(names):
        shape = (N_CHIPS, _group_rows(names), D_MODEL)
        return lax.empty(shape, F32), lax.empty(shape, _TD)

    def grad_into(packs, names, which, a, b, name):
        off = sum(MATS[other][0] for other in names[: names.index(which)])
        return _mm_tn_into(a, b, packs, rows=MATS[which][0], off=off, name=name)

    du = _mm(dx3_lo, wt["w_down"], mode="nt", name="mlp_down_dx", out_dtypes=(_CD,), extras=(relu_u,), tm=2048,
             epilogue=lambda acc, rr: (acc * (2.0 * rr.astype(F32)),))
    packs = new_packs(GRAD_GROUPS[0])
    packs = grad_into(packs, GRAD_GROUPS[0], "w_down", act, dx3_lo, "mlp_down_dw")
    packs = grad_into(packs, GRAD_GROUPS[0], "w_up", du, hm, "mlp_up_dw")
    mlp_state = _scatter_start(packs[1], packs[0], name="grads_mlp_start")
    dx2, dx2_lo, g_mlp_norm = _mm_rows(
        du, wt["w_up"], mode="nn", name="mlp_up_dx", rows=(x2, dx3), vecs=(behind(mlp_norm, mlp_state[-1]),),
        out_rows=(F32, _CD), out_vecs=(D_MODEL,), epilogue=_ep_norm_bwd)
    packs = new_packs(GRAD_GROUPS[1])
    packs = grad_into(packs, GRAD_GROUPS[1], "w_xo", ox, dx2_lo, "proj_xo_dw")
    dqx, dkv = _xattn_bwd(qx, kv, dx2_lo, wt["w_xo"], name="xattn_bwd")
    packs = grad_into(packs, GRAD_GROUPS[1], "w_xq", hx, dqx, "proj_xq_dw")
    dx1, dx1_lo, g_xa_norm = _mm_rows(
        dqx, wt["w_xq"], mode="nt", name="proj_xq_dx", rows=(x1, dx2), vecs=(xa_norm,),
        out_rows=(F32, _CD), out_vecs=(D_MODEL,), epilogue=_ep_norm_bwd, tm=1024)
    dkv_lo = dkv.astype(_CD)
    packs = grad_into(packs, GRAD_GROUPS[1], "w_xkv", dkv_lo, hmem, "proj_xkv_dw")
    dhmem = _mm(dkv_lo, wt["w_xkv"], mode="nn", name="proj_xkv_dx")
    g_mem_norm = _rms_gain_grad(mems, dhmem, name="norm_mem_bwd")
    dy = _mm(dx1_lo, wt["w_out"], mode="nt", name="proj_out_dx")
    packs = grad_into(packs, GRAD_GROUPS[1], "w_out", y, dx1_lo, "proj_out_dw")
    attn_state = _scatter_start(packs[1], packs[0], name="grads_attn_start")
    dz_a, do, g_conv_w, g_conv_norm, g_gla_norm = _mix_bwd(z_a, o_f, o_b, dy, conv_w_full, behind(conv_norm, attn_state[-1]), gla_norm, name="mix_out_bwd")
    dqkv_f, db_f, dqkv_b, db_b = _gla_bwd(z_b, b_f, b_b, do, st_f, st_b, name="gla_scan_bwd")
    dz_b, g_waf_p, g_wab_p, g_b_af, g_b_ab = _gate_bwd(z_b, waf_p, wab_p, b_af, b_ab, db_f, db_b, dqkv_f, dqkv_b, name="gates_bwd")
    g_za = _mm_tn(dz_a, h1, name="proj_in_a_dw")
    g_zb = _mm_tn(dz_b, h1, name="proj_in_b_dw")
    grads_t["w_in"] = jnp.concatenate([g_za[0:1536], g_zb[0:1024], g_za[1536:2048], g_zb[1024:1056]], axis=0)
    in_state = start_group(GRAD_GROUPS[2], "in")
    grad_x, g_mix_norm = _mm_rows(
        dz_a, w_za, mode="nn", name="proj_in_dx", more=((dz_b, w_zb),), rows=(xs, dx1), vecs=(behind(mix_norm, in_state[-1]),),
        out_rows=(F32,), out_vecs=(D_MODEL,), epilogue=_ep_norm_bwd)

    half_mlp = finish_group(mlp_state, grad_x, "mlp")
    half_attn = finish_group(attn_state, half_mlp, "attn")
    half_in = finish_group(in_state, half_attn, "in")
    shard_rows = {}
    for names, rows in zip(GRAD_GROUPS, _swap_all([half_mlp, half_attn, half_in], name="shards_to_sibling")):
        off = 0
        for name in names:
            shard_rows[name] = (rows, off)
            off += MATS[name][0]

    small_vals = dict(mix_norm=g_mix_norm, conv_norm=g_conv_norm, b_af=g_b_af, b_ab=g_b_ab, gla_norm=g_gla_norm, xa_norm=g_xa_norm,
                      mem_norm=g_mem_norm, mlp_norm=g_mlp_norm, final_norm=g_final_norm, conv_w=g_conv_w,
                      w_af=g_waf_p[0:GLA_LOWRANK], w_ab=g_wab_p[GLA_LOWRANK : 2 * GLA_LOWRANK], loss=loss_part)
    small = jnp.concatenate([small_vals[name].reshape(-1, 128) for name, _ in SMALL], axis=0)
    small = _sum_small(jnp.pad(small, ((0, SMALL_ROWS - small.shape[0]), (0, 0))), loss_part)
    g_small, off = {}, 0
    for name, n in SMALL:
        g_small[name] = small[off : off + n // 128]
        off += n // 128
    loss = g_small["loss"][0, 0]
    g_small["conv_w"] = lax.dynamic_slice(g_small["conv_w"].reshape(CONV_K, CONV_WIDTH), (0, 128 * chip), (CONV_K, 128))
    g_small["w_af"] = lax.dynamic_slice(g_small["w_af"].reshape(GLA_LOWRANK, GLA_K_TOTAL), (0, 64 * chip), (GLA_LOWRANK, 64))
    g_small["w_ab"] = lax.dynamic_slice(g_small["w_ab"].reshape(GLA_LOWRANK, GLA_K_TOTAL), (0, 64 * chip), (GLA_LOWRANK, 64))

    names = ["mix_norm", "w_in", "conv_w", "conv_norm", "w_af", "b_af", "w_ab", "b_ab", "gla_norm", "w_out", "xa_norm", "mem_norm",
             "w_xq", "w_xkv", "w_xo", "mlp_norm", "w_up", "w_down", "final_norm"]
    big_names = list(MATS)
    as2d = lambda a: a.reshape(1, -1) if a.ndim == 1 else a.reshape(a.shape[-2:])
    grads, deltas, new_m, new_v = {}, {}, {}, {}
    for name in big_names:
        rows, off = shard_rows[name]
        wmv = [as2d(given[name]), as2d(given["m_" + name]), as2d(given["v_" + name])]
        as_stored = name == "w_in"
        if as_stored:
            wmv = [a.T for a in wmv]
        res = _adamw(*wmv, rows, off, transposed=MATS[name][1] and not as_stored, name="adamw_" + name)
        grads[name], deltas[name], new_m[name], new_v[name] = [a.T for a in res] if as_stored else res
    small_names = [name for name in names if name not in big_names]
    groups = []
    for name in small_names:
        grads[name] = g_small[name].reshape(as2d(given[name]).shape)
        groups.append((as2d(given[name]), grads[name], as2d(given["m_" + name]), as2d(given["v_" + name])))
    for name, res in zip(small_names, _adamw_small(groups, name="adamw_small")):
        deltas[name], new_m[name], new_v[name] = res

    like = lambda name, a: a.reshape(given[name].shape)
    return (loss, grad_x[None], *[like(n, grads[n]) for n in names], *[like(n, deltas[n]) for n in names],
            *[like(n, new_m[n]) for n in names], *[like(n, new_v[n]) for n in names])
```

```python
import jax
import jax.numpy as jnp
from jax import lax
from jax.experimental import pallas as pl
from jax.experimental.pallas import tpu as pltpu

F32 = jnp.float32
BF16 = jnp.bfloat16
_CD = jnp.bfloat16
_TD = jnp.bfloat16

D_MODEL = 1024
N_MEM = 256
CONV_WIDTH = 512
CONV_GROUP = 64
CONV_K = 3
GLA_HEADS = 4
GLA_DK = 64
GLA_DV = 128
GLA_K_TOTAL = 256
GLA_V_TOTAL = 512
GLA_LOWRANK = 16
GLA_GATE_SCALE = 1.0 / 16.0
GLA_CHUNK = 64
XA_HEADS = 4
XA_HEAD_DIM = 256
D_FF = 4096
EPS = 1e-6
W_IN_COLS = 3104
ZA_COLS = 2048
ZB_COLS = 1152
LR_COL = 1024

ADAM_LR = 0.001
ADAM_B1 = 0.9
ADAM_B2 = 0.999
ADAM_EPS = 1e-08
ADAM_WD = 0.01
ADAM_STEP = 10

N_CHIPS = 4
SMALL_ROWS = 128

_TS = 512
_VMEM = 44 * 1024 * 1024
MESH = pl.DeviceIdType.MESH
ANY = pl.BlockSpec(memory_space=pl.ANY)


def _cp(sem=None, **kw):
    return pltpu.CompilerParams(dimension_semantics=sem, vmem_limit_bytes=_VMEM, **kw)


def _dot(a, b):
    return jnp.dot(a.astype(_CD), b.astype(_CD), preferred_element_type=F32)


def _dot_nt(a, b):
    return lax.dot_general(a.astype(_CD), b.astype(_CD), (((1,), (1,)), ((), ())), preferred_element_type=F32)


def _dot_tn(a, b):
    return lax.dot_general(a.astype(_CD), b.astype(_CD), (((0,), (0,)), ((), ())), preferred_element_type=F32)


def _dot_split(x, ones):
    hi = x.astype(BF16)
    r = x - hi.astype(F32)
    mid = r.astype(BF16)
    lo = (r - mid.astype(F32)).astype(BF16)
    d = lambda p: jnp.dot(p, ones, preferred_element_type=F32)
    return d(hi) + d(mid) + d(lo)


def _pick(n, cands=(1024, 640, 512, 256, 128)):
    for t in cands:
        if n % t == 0:
            return t
    return n


def _rows(s, light=False, times=2):
    return min(times * _TS if light else _TS, s)


def _sigmoid(v):
    e = jnp.exp(-jnp.abs(v))
    return jnp.where(v >= 0, 1.0 / (1.0 + e), e / (1.0 + e))


def _mm(a, b, *, mode, name, out_dtypes=(F32,), extras=(), epilogue=None, tm=None, tn=None, tk=None):
    m, k = a.shape
    n = b.shape[1] if mode == "nn" else b.shape[0]
    tm = min(m, tm or 1024)
    tn = tn or _pick(n)
    tk = tk or _pick(k)
    nk = k // tk
    n_ex, n_out = len(extras), len(out_dtypes)

    def body(*refs):
        a_ref, b_ref = refs[:2]
        ex = refs[2 : 2 + n_ex]
        outs = refs[2 + n_ex : 2 + n_ex + n_out]
        part = _dot(a_ref[...], b_ref[...]) if mode == "nn" else _dot_nt(a_ref[...], b_ref[...])

        def finish(acc):
            res = epilogue(acc, *[e[...] for e in ex]) if epilogue else (acc,)
            for o, r in zip(outs, res):
                o[...] = r.astype(o.dtype)

        if nk == 1:
            finish(part)
        else:
            acc_ref = refs[-1]
            kk = pl.program_id(2)

            @pl.when(kk == 0)
            def _():
                acc_ref[...] = part

            @pl.when(kk > 0)
            def _():
                acc_ref[...] += part

            @pl.when(kk == nk - 1)
            def _():
                finish(acc_ref[...])

    b_spec = pl.BlockSpec((tk, tn), lambda i, j, kk: (kk, j)) if mode == "nn" else pl.BlockSpec((tn, tk), lambda i, j, kk: (j, kk))
    tile = pl.BlockSpec((tm, tn), lambda i, j, kk: (i, j))
    out = pl.pallas_call(
        body,
        name=name,
        grid=(m // tm, n // tn, nk),
        in_specs=[pl.BlockSpec((tm, tk), lambda i, j, kk: (i, kk)), b_spec] + [tile] * n_ex,
        out_specs=[tile] * n_out,
        out_shape=[jax.ShapeDtypeStruct((m, n), dt) for dt in out_dtypes],
        scratch_shapes=[pltpu.VMEM((tm, tn), F32)] if nk > 1 else [],
        compiler_params=_cp(("parallel", "parallel", "arbitrary")),
    )(a, b, *extras)
    return out[0] if n_out == 1 else out


def _mm_two(a, b1, b2, *, name, tm=512):
    m, k = a.shape
    tm = min(m, tm)

    def body(a_ref, b1_ref, b2_ref, o1_ref, o2_ref):
        av = a_ref[...]
        o1_ref[...] = _dot_nt(av, b1_ref[...])
        o2_ref[...] = _dot_nt(av, b2_ref[...])

    whole = lambda arr: pl.BlockSpec(arr.shape, lambda i: (0, 0))
    rows = lambda n: pl.BlockSpec((tm, n), lambda i: (i, 0))
    return pl.pallas_call(
        body,
        name=name,
        grid=(m // tm,),
        in_specs=[rows(k), whole(b1), whole(b2)],
        out_specs=[rows(b1.shape[0]), rows(b2.shape[0])],
        out_shape=[jax.ShapeDtypeStruct((m, b1.shape[0]), F32), jax.ShapeDtypeStruct((m, b2.shape[0]), F32)],
        compiler_params=_cp(("parallel",)),
    )(a, b1, b2)


def _mm_tn(a, b, *, name):
    s, m = a.shape
    n = b.shape[1]
    cap = max(128, (1 << 20) // n)
    tm = _pick(m, tuple(t for t in (512, 640, 384, 256, 128) if t <= max(cap, 128)))
    ts = min(s, 1 << (((1 << 22) // n).bit_length() - 1))
    ns = s // ts

    def body(a_ref, b_ref, o_ref):
        part = _dot_tn(a_ref[...], b_ref[...])
        if ns == 1:
            o_ref[...] = part
        else:
            ss = pl.program_id(1)

            @pl.when(ss == 0)
            def _():
                o_ref[...] = part

            @pl.when(ss > 0)
            def _():
                o_ref[...] += part

    return pl.pallas_call(
        body,
        name=name,
        grid=(m // tm, ns),
        in_specs=[pl.BlockSpec((ts, tm), lambda i, ss: (ss, i)), pl.BlockSpec((ts, n), lambda i, ss: (ss, 0))],
        out_specs=pl.BlockSpec((tm, n), lambda i, ss: (i, 0)),
        out_shape=jax.ShapeDtypeStruct((m, n), F32),
        compiler_params=_cp(("parallel", "arbitrary")),
    )(a, b)


def _mm_tn_into(a, b, packs, *, rows, off, name):
    s, m = a.shape
    n = b.shape[1]
    tm = 1024 if rows % 1024 == 0 and s >= 4096 else 512
    tr = min(tm, rows)
    per, chips = rows // tr, tm // tr
    ts = min(s, (1 << (((1 << 22) // n).bit_length() - 1)) * 512 // tm)
    ns = s // ts

    def body(a_ref, b_ref, f_in, lo_in, f_ref, lo_ref):
        part = _dot_tn(a_ref[...], b_ref[...])
        pieces = [part[c * tr : (c + 1) * tr] for c in range(chips)]
        if ns == 1:
            for c, p in enumerate(pieces):
                f_ref[c] = p
                lo_ref[c] = p.astype(lo_ref.dtype)
        else:
            ss = pl.program_id(1)

            @pl.when(ss == 0)
            def _():
                for c, p in enumerate(pieces):
                    f_ref[c] = p

            @pl.when(ss > 0)
            def _():
                for c, p in enumerate(pieces):
                    f_ref[c] += p

            @pl.when(ss == ns - 1)
            def _():
                lo_ref[...] = f_ref[...].astype(lo_ref.dtype)

    spec = pl.BlockSpec((chips, tr, n), lambda i, ss: (i // per, off // tr + i % per, 0))
    return pl.pallas_call(
        body,
        name=name,
        grid=(m // tm, ns),
        in_specs=[pl.BlockSpec((ts, tm), lambda i, ss: (ss, i)), pl.BlockSpec((ts, n), lambda i, ss: (ss, 0)), ANY, ANY],
        out_specs=[spec, spec],
        out_shape=[jax.ShapeDtypeStruct(p.shape, p.dtype) for p in packs],
        input_output_aliases={2: 0, 3: 1},
        compiler_params=_cp(("parallel", "arbitrary")),
    )(a, b, *packs)


def _mm_rows(a, b, *, mode, name, more=(), rows=(), vecs=(), out_rows=(), out_vecs=(), epilogue, tm=512):
    m, k = a.shape
    n = b.shape[1] if mode == "nn" else b.shape[0]
    tm = min(m, tm)
    parts = 2 if tm % 256 == 0 else 1
    n_m, n_r, n_v, n_or, n_ov = 2 * len(more), len(rows), len(vecs), len(out_rows), len(out_vecs)

    def body(*refs):
        a_ref, b_ref = refs[:2]
        m_refs = refs[2 : 2 + n_m]
        rest = refs[2 + n_m :]
        r_refs = rest[:n_r]
        v_refs = rest[n_r : n_r + n_v]
        or_refs = rest[n_r + n_v : n_r + n_v + n_or]
        ov_refs = rest[n_r + n_v + n_or :]
        res_vecs = None
        for p in range(parts):
            rs = slice(p * tm // parts, (p + 1) * tm // parts)
            acc = _dot(a_ref[rs, :], b_ref[...]) if mode == "nn" else _dot_nt(a_ref[rs, :], b_ref[...])
            for a2_ref, b2_ref in zip(m_refs[0::2], m_refs[1::2]):
                acc = acc + _dot(a2_ref[rs, :], b2_ref[...])
            res_rows, part_vecs = epilogue(acc, [r[rs, :] for r in r_refs], [v[...] for v in v_refs])
            for o, r in zip(or_refs, res_rows):
                o[rs, :] = r.astype(o.dtype)
            res_vecs = part_vecs if res_vecs is None else [s + t for s, t in zip(res_vecs, part_vecs)]
        if n_ov:
            first = pl.program_id(0) == 0

            @pl.when(first)
            def _():
                for o, r in zip(ov_refs, res_vecs):
                    o[...] = r

            @pl.when(jnp.logical_not(first))
            def _():
                for o, r in zip(ov_refs, res_vecs):
                    o[...] += r

    tile = pl.BlockSpec((tm, n), lambda i: (i, 0))
    whole = lambda arr: pl.BlockSpec(arr.shape, lambda i: (0, 0))
    vec = lambda w: pl.BlockSpec((1, w), lambda i: (0, 0))
    out = pl.pallas_call(
        body,
        name=name,
        grid=(m // tm,),
        in_specs=[pl.BlockSpec((tm, k), lambda i: (i, 0)), whole(b)]
        + [spec for a2, b2 in more for spec in (pl.BlockSpec((tm, a2.shape[1]), lambda i: (i, 0)), whole(b2))]
        + [tile] * n_r + [vec(v.shape[1]) for v in vecs],
        out_specs=[tile] * n_or + [vec(w) for w in out_vecs],
        out_shape=[jax.ShapeDtypeStruct((m, n), dt) for dt in out_rows] + [jax.ShapeDtypeStruct((1, w), F32) for w in out_vecs],
        compiler_params=_cp(("arbitrary",) if n_ov else ("parallel",)),
    )(a, b, *[x for pair in more for x in pair], *rows, *vecs)
    return out


def _ep_residual_norm(acc, rows, vecs):
    x = acc + rows[0]
    r = lax.rsqrt(jnp.mean(x * x, axis=-1, keepdims=True) + EPS)
    return [x, x * r * vecs[0]], []


def _ep_norm_bwd(acc, rows, vecs):
    dy = acc
    for extra in rows[2:]:
        dy = dy + extra
    x, dres = rows[0], rows[1]
    r = lax.rsqrt(jnp.mean(x * x, axis=-1, keepdims=True) + EPS)
    xh = x * r
    dxh = dy * vecs[0]
    dx = r * (dxh - xh * jnp.mean(dxh * xh, axis=-1, keepdims=True)) + dres
    return [dx, dx], [jnp.sum(dy * xh, axis=0, keepdims=True)]


def _ep_loss(acc, rows, vecs):
    x = acc + rows[0]
    d = x.shape[-1]
    r = lax.rsqrt(jnp.mean(x * x, axis=-1, keepdims=True) + EPS)
    xh = x * r
    err = xh * vecs[0] - rows[1]
    loss = jnp.zeros((1, 128), F32) + 0.5 * jnp.sum(jnp.mean(err * err, axis=-1, keepdims=True))
    dy = err * (1.0 / d)
    dxh = dy * vecs[0]
    dx = r * (dxh - xh * jnp.mean(dxh * xh, axis=-1, keepdims=True))
    return [dx, dx], [loss, jnp.sum(dy * xh, axis=0, keepdims=True)]


def _rms_fwd(x, g, *, name):
    s, d = x.shape
    ts = _rows(s, light=True, times=4)

    def body(x_ref, g_ref, o_ref):
        xf = x_ref[...]
        r = lax.rsqrt(jnp.mean(xf * xf, axis=-1, keepdims=True) + EPS)
        o_ref[...] = (xf * r * g_ref[...]).astype(o_ref.dtype)

    return pl.pallas_call(
        body,
        name=name,
        grid=(s // ts,),
        in_specs=[pl.BlockSpec((ts, d), lambda i: (i, 0)), pl.BlockSpec((1, d), lambda i: (0, 0))],
        out_specs=pl.BlockSpec((ts, d), lambda i: (i, 0)),
        out_shape=jax.ShapeDtypeStruct((s, d), _CD),
        compiler_params=_cp(("parallel",)),
    )(x, g)


def _rms_gain_grad(x, dy, *, name):
    s, d = x.shape
    ts = _rows(s)

    def body(x_ref, dy_ref, dg_ref):
        xf = x_ref[...]
        r = lax.rsqrt(jnp.mean(xf * xf, axis=-1, keepdims=True) + EPS)
        part = jnp.sum(dy_ref[...] * (xf * r), axis=0, keepdims=True)

        @pl.when(pl.program_id(0) == 0)
        def _():
            dg_ref[...] = part

        @pl.when(pl.program_id(0) > 0)
        def _():
            dg_ref[...] += part

    tile = pl.BlockSpec((ts, d), lambda i: (i, 0))
    return pl.pallas_call(
        body,
        name=name,
        grid=(s // ts,),
        in_specs=[tile, tile],
        out_specs=pl.BlockSpec((1, d), lambda i: (0, 0)),
        out_shape=jax.ShapeDtypeStruct((1, d), F32),
        compiler_params=_cp(("arbitrary",)),
    )(x, dy)


def _chunk_scan(v, row_in_chunk, suffix):
    t = v.shape[0]
    step = 1
    while step < GLA_CHUNK:
        if suffix:
            v = v + jnp.where(row_in_chunk < GLA_CHUNK - step, pltpu.roll(v, t - step, 0), 0.0)
        else:
            v = v + jnp.where(row_in_chunk >= step, pltpu.roll(v, step, 0), 0.0)
        step *= 2
    return v


def _gate_pre(lr, w_ref, b_ref):
    return _dot(lr, w_ref[...]) + b_ref[...]


def _gate_fwd(z, waf, wab, baf, bab, *, name):
    s = z.shape[0]
    ts = _rows(s, light=True, times=4)

    def body(lr_ref, waf_ref, wab_ref, baf_ref, bab_ref, bf_ref, bb_ref):
        lr = lr_ref[...]
        ric = lax.broadcasted_iota(jnp.int32, (ts, GLA_K_TOTAL), 0) & (GLA_CHUNK - 1)
        for w_ref, b_ref, o_ref, suffix in ((waf_ref, baf_ref, bf_ref, False), (wab_ref, bab_ref, bb_ref, True)):
            pre = _gate_pre(lr, w_ref, b_ref)
            la = (jnp.minimum(pre, 0.0) - jnp.log(1.0 + jnp.exp(-jnp.abs(pre)))) * GLA_GATE_SCALE
            o_ref[...] = _chunk_scan(la, ric, suffix)

    wspec = pl.BlockSpec((128, GLA_K_TOTAL), lambda i: (0, 0))
    bspec = pl.BlockSpec((1, GLA_K_TOTAL), lambda i: (0, 0))
    tile = pl.BlockSpec((ts, GLA_K_TOTAL), lambda i: (i, 0))
    return pl.pallas_call(
        body,
        name=name,
        grid=(s // ts,),
        in_specs=[pl.BlockSpec((ts, 128), lambda i: (i, LR_COL // 128)), wspec, wspec, bspec, bspec],
        out_specs=[tile, tile],
        out_shape=[jax.ShapeDtypeStruct((s, GLA_K_TOTAL), F32)] * 2,
        compiler_params=_cp(("parallel",)),
    )(z, waf, wab, baf, bab)


def _gate_bwd(z, waf, wab, baf, bab, dbf, dbb, dqkv_f, dqkv_b, *, name):
    s = z.shape[0]
    ts = _rows(s, light=True)

    def body(lr_ref, waf_ref, wab_ref, baf_ref, bab_ref, dbf_ref, dbb_ref, gf_ref, gb_ref, dzb_ref, dwf_ref, dwb_ref, dbaf_ref, dbab_ref):
        lr = lr_ref[...]
        ric = lax.broadcasted_iota(jnp.int32, (ts, GLA_K_TOTAL), 0) & (GLA_CHUNK - 1)
        first = pl.program_id(0) == 0
        dlr = None
        for w_ref, b_ref, db_ref, dw_ref, dbias_ref, suffix in (
            (waf_ref, baf_ref, dbf_ref, dwf_ref, dbaf_ref, True),
            (wab_ref, bab_ref, dbb_ref, dwb_ref, dbab_ref, False),
        ):
            pre = _gate_pre(lr, w_ref, b_ref)
            dla = _chunk_scan(db_ref[...], ric, suffix)
            dpre = dla * GLA_GATE_SCALE * _sigmoid(-pre)
            part = _dot_nt(dpre, w_ref[...])
            dlr = part if dlr is None else dlr + part
            dw = _dot_tn(lr, dpre)
            dbias = jnp.sum(dpre, axis=0, keepdims=True)

            @pl.when(first)
            def _():
                dw_ref[...] = dw
                dbias_ref[...] = dbias

            @pl.when(jnp.logical_not(first))
            def _():
                dw_ref[...] += dw
                dbias_ref[...] += dbias

        dqkv = gf_ref[...].astype(F32) + gb_ref[...].astype(F32)
        dzb_ref[...] = jnp.concatenate([dqkv, dlr], axis=1).astype(dzb_ref.dtype)

    wspec = pl.BlockSpec((128, GLA_K_TOTAL), lambda i: (0, 0))
    bspec = pl.BlockSpec((1, GLA_K_TOTAL), lambda i: (0, 0))
    tile = pl.BlockSpec((ts, GLA_K_TOTAL), lambda i: (i, 0))
    wide = pl.BlockSpec((ts, 2 * GLA_K_TOTAL + GLA_V_TOTAL), lambda i: (i, 0))
    return pl.pallas_call(
        body,
        name=name,
        grid=(s // ts,),
        in_specs=[pl.BlockSpec((ts, 128), lambda i: (i, LR_COL // 128)), wspec, wspec, bspec, bspec, tile, tile, wide, wide],
        out_specs=[pl.BlockSpec((ts, ZB_COLS), lambda i: (i, 0)), wspec, wspec, bspec, bspec],
        out_shape=[
            jax.ShapeDtypeStruct((s, ZB_COLS), _CD),
            jax.ShapeDtypeStruct((128, GLA_K_TOTAL), F32),
            jax.ShapeDtypeStruct((128, GLA_K_TOTAL), F32),
            jax.ShapeDtypeStruct((1, GLA_K_TOTAL), F32),
            jax.ShapeDtypeStruct((1, GLA_K_TOTAL), F32),
        ],
        compiler_params=_cp(("arbitrary",)),
    )(z, waf, wab, baf, bab, dbf, dbb, dqkv_f, dqkv_b)


def _gla_masks(rev):
    lane_head = lax.broadcasted_iota(jnp.int32, (1, GLA_K_TOTAL), 1) >> 6
    head_masks = [lane_head == h for h in range(GLA_HEADS)]
    t = lax.broadcasted_iota(jnp.int32, (GLA_HEADS * GLA_CHUNK, GLA_CHUNK), 0) & (GLA_CHUNK - 1)
    u = lax.broadcasted_iota(jnp.int32, (GLA_HEADS * GLA_CHUNK, GLA_CHUNK), 1)
    tri = (u > t) if rev else (u <= t)
    row = lax.broadcasted_iota(jnp.int32, (GLA_CHUNK, GLA_K_TOTAL), 0)
    total_row = row == (0 if rev else GLA_CHUNK - 1)
    return head_masks, tri, total_row


def _spread(a, head_masks):
    return jnp.concatenate([jnp.where(m, a, 0.0) for m in head_masks], axis=0)


def _stack(a):
    return jnp.concatenate([a[:, GLA_DV * h : GLA_DV * (h + 1)] for h in range(GLA_HEADS)], axis=0)


def _unstack(a):
    return jnp.concatenate([a[GLA_CHUNK * h : GLA_CHUNK * (h + 1)] for h in range(GLA_HEADS)], axis=1)


def _collect(a, head_masks):
    out = None
    for h, m in enumerate(head_masks):
        part = jnp.where(m, a[GLA_CHUNK * h : GLA_CHUNK * (h + 1)], 0.0)
        out = part if out is None else out + part
    return out


def _gla_chunk_terms(q_ref, k_ref, v_ref, b_ref, rows, head_masks, tri, total_row):
    q = q_ref[rows, :] * (GLA_DK**-0.5)
    k = k_ref[rows, :]
    v = v_ref[rows, :]
    b = b_ref[rows, :]
    eb = jnp.exp(b)
    enb = jnp.exp(-b)
    g = jnp.sum(jnp.where(total_row, b, 0.0), axis=0, keepdims=True)
    egb = jnp.exp(g - b)
    qt = q * eb
    kt = k * enb
    kh = k * egb
    q_heads = _spread(qt, head_masks)
    attn = jnp.where(tri, _dot_nt(q_heads, kt), 0.0)
    return v, eb, enb, egb, jnp.exp(g), qt, kt, kh, q_heads, attn


def _gla_specs(s, tb, rev_blocks):
    nb = s // tb
    rb = (lambda i: nb - 1 - i) if rev_blocks else (lambda i: i)
    q_spec = pl.BlockSpec((tb, GLA_K_TOTAL), lambda i: (rb(i), 0))
    k_spec = pl.BlockSpec((tb, GLA_K_TOTAL), lambda i: (rb(i), 1))
    v_spec = pl.BlockSpec((tb, GLA_V_TOTAL), lambda i: (rb(i), 1))
    b_spec = pl.BlockSpec((tb, GLA_K_TOTAL), lambda i: (rb(i), 0))
    o_spec = pl.BlockSpec((tb, GLA_V_TOTAL), lambda i: (rb(i), 0))
    st_spec = pl.BlockSpec((tb // GLA_CHUNK, GLA_DV, GLA_K_TOTAL), lambda i: (rb(i), 0, 0))
    return nb, q_spec, k_spec, v_spec, b_spec, o_spec, st_spec


def _gla_fwd_chunk(cidx, q_ref, k_ref, v_ref, b_ref, o_ref, sv_ref, st_ref, masks):
    head_masks, tri, total_row = masks
    rows = pl.ds(pl.multiple_of(cidx * GLA_CHUNK, GLA_CHUNK), GLA_CHUNK)
    v, _, _, _, eg, _, _, kh, q_heads, attn = _gla_chunk_terms(q_ref, k_ref, v_ref, b_ref, rows, head_masks, tri, total_row)
    o = jnp.concatenate(
        [_dot(attn[GLA_CHUNK * h : GLA_CHUNK * (h + 1)], v[:, GLA_DV * h : GLA_DV * (h + 1)]) for h in range(GLA_HEADS)], axis=1
    )
    st = st_ref[...]
    o_ref[rows, :] = o + _unstack(_dot_nt(q_heads, st))
    sv_ref[cidx] = st
    st_ref[...] = st * eg + _dot_tn(_stack(v), _spread(kh, head_masks))


def _gla_fwd(z, b_f, b_b, *, name):
    s = z.shape[0]
    tb = _rows(s)
    cpb = tb // GLA_CHUNK
    nb, qf, kf, vf, bf, of, sf = _gla_specs(s, tb, False)
    _, qr, kr, vr, br, orr, sr = _gla_specs(s, tb, True)

    def body(qf_ref, kf_ref, vf_ref, bf_ref, qr_ref, kr_ref, vr_ref, br_ref, of_ref, svf_ref, or_ref, svr_ref, stf_ref, str_ref):
        masks_f, masks_r = _gla_masks(False), _gla_masks(True)

        @pl.when(pl.program_id(0) == 0)
        def _():
            stf_ref[...] = jnp.zeros_like(stf_ref)
            str_ref[...] = jnp.zeros_like(str_ref)

        def chunk(ci, carry):
            _gla_fwd_chunk(ci, qf_ref, kf_ref, vf_ref, bf_ref, of_ref, svf_ref, stf_ref, masks_f)
            _gla_fwd_chunk(cpb - 1 - ci, qr_ref, kr_ref, vr_ref, br_ref, or_ref, svr_ref, str_ref, masks_r)
            return carry

        lax.fori_loop(0, cpb, chunk, 0)

    o_shape = jax.ShapeDtypeStruct((s, GLA_V_TOTAL), F32)
    st_shape = jax.ShapeDtypeStruct((s // GLA_CHUNK, GLA_DV, GLA_K_TOTAL), F32)
    return pl.pallas_call(
        body,
        name=name,
        grid=(nb,),
        in_specs=[qf, kf, vf, bf, qr, kr, vr, br],
        out_specs=[of, sf, orr, sr],
        out_shape=[o_shape, st_shape, o_shape, st_shape],
        scratch_shapes=[pltpu.VMEM((GLA_DV, GLA_K_TOTAL), F32)] * 2,
        compiler_params=_cp(("arbitrary",)),
    )(z, z, z, b_f, z, z, z, b_b)


def _gla_bwd_chunk(cidx, q_ref, k_ref, v_ref, b_ref, do_ref, sv_ref, dqkv_ref, db_ref, dst_ref, masks):
    head_masks, tri, total_row = masks
    rows = pl.ds(pl.multiple_of(cidx * GLA_CHUNK, GLA_CHUNK), GLA_CHUNK)
    v, eb, enb, egb, eg, qt, kt, kh, q_heads, attn = _gla_chunk_terms(q_ref, k_ref, v_ref, b_ref, rows, head_masks, tri, total_row)
    do_c = do_ref[rows, :]
    st = sv_ref[cidx]
    dst = dst_ref[...]
    do_s, v_s = _stack(do_c), _stack(v)
    hs = lambda a, h: a[GLA_CHUNK * h : GLA_CHUNK * (h + 1)]
    vs = lambda a, h: a[:, GLA_DV * h : GLA_DV * (h + 1)]
    dattn = jnp.concatenate([_dot_nt(vs(do_c, h), vs(v, h)) for h in range(GLA_HEADS)], axis=0)
    dattn = jnp.where(tri, dattn, 0.0)
    dv = jnp.concatenate([_dot_tn(hs(attn, h), vs(do_c, h)) for h in range(GLA_HEADS)], axis=1)
    dv = dv + _unstack(_dot_nt(_spread(kh, head_masks), dst))
    dqt = _collect(_dot(do_s, st) + _dot(dattn, kt), head_masks)
    dkt = _dot_tn(dattn, q_heads)
    dkh = _collect(_dot(v_s, dst), head_masks)
    dg = jnp.sum(dkh * kh, axis=0, keepdims=True) + jnp.sum(dst * st, axis=0, keepdims=True) * eg
    db = dqt * qt - dkt * kt - dkh * kh + jnp.where(total_row, dg, 0.0)
    dq = dqt * eb * (GLA_DK**-0.5)
    dk = dkt * enb + dkh * egb
    dqkv_ref[rows, :] = jnp.concatenate([dq, dk, dv], axis=1).astype(dqkv_ref.dtype)
    db_ref[rows, :] = db
    dst_ref[...] = dst * eg + _dot_tn(do_s, q_heads)


def _gla_bwd(z, b_f, b_b, do, st_f, st_b, *, name):
    s = z.shape[0]
    tb = _rows(s)
    cpb = tb // GLA_CHUNK
    wide = 2 * GLA_K_TOTAL + GLA_V_TOTAL
    nb, qf, kf, vf, bf, of, sf = _gla_specs(s, tb, True)
    _, qr, kr, vr, br, orr, sr = _gla_specs(s, tb, False)
    gf = pl.BlockSpec((tb, wide), lambda i: (nb - 1 - i, 0))
    gr = pl.BlockSpec((tb, wide), lambda i: (i, 0))

    def body(qf_ref, kf_ref, vf_ref, bf_ref, dof_ref, svf_ref, qr_ref, kr_ref, vr_ref, br_ref, dor_ref, svr_ref,
             gf_ref, dbf_ref, gr_ref, dbr_ref, dstf_ref, dstr_ref):
        masks_f, masks_r = _gla_masks(False), _gla_masks(True)

        @pl.when(pl.program_id(0) == 0)
        def _():
            dstf_ref[...] = jnp.zeros_like(dstf_ref)
            dstr_ref[...] = jnp.zeros_like(dstr_ref)

        def chunk(ci, carry):
            _gla_bwd_chunk(cpb - 1 - ci, qf_ref, kf_ref, vf_ref, bf_ref, dof_ref, svf_ref, gf_ref, dbf_ref, dstf_ref, masks_f)
            _gla_bwd_chunk(ci, qr_ref, kr_ref, vr_ref, br_ref, dor_ref, svr_ref, gr_ref, dbr_ref, dstr_ref, masks_r)
            return carry

        lax.fori_loop(0, cpb, chunk, 0)

    g_shape = jax.ShapeDtypeStruct((s, wide), _CD)
    db_shape = jax.ShapeDtypeStruct((s, GLA_K_TOTAL), F32)
    return pl.pallas_call(
        body,
        name=name,
        grid=(nb,),
        in_specs=[qf, kf, vf, bf, of, sf, qr, kr, vr, br, orr, sr],
        out_specs=[gf, bf, gr, br],
        out_shape=[g_shape, db_shape, g_shape, db_shape],
        scratch_shapes=[pltpu.VMEM((GLA_DV, GLA_K_TOTAL), F32)] * 2,
        compiler_params=_cp(("arbitrary",)),
    )(z, z, z, b_f, do, st_f, z, z, z, b_b, do, st_b)


HALO = 8


def _halo_specs(s, ts, width, col):
    last = s // HALO - 1
    per = ts // HALO
    prev = pl.BlockSpec((HALO, width), lambda i: (jnp.maximum(i * per - 1, 0), col))
    nxt = pl.BlockSpec((HALO, width), lambda i: (jnp.minimum((i + 1) * per, last), col))
    return prev, nxt


def _group_ones():
    group = jnp.arange(CONV_WIDTH, dtype=jnp.int32) // CONV_GROUP
    return (group[:, None] == group[None, :]).astype(BF16)


_ONES_SPEC = pl.BlockSpec((CONV_WIDTH, CONV_WIDTH), lambda i: (0, 0))


def _conv_terms(cc_ext, cu_ext, cw, valid):
    n = cc_ext.shape[0]
    hc = jnp.where(valid, cc_ext * cu_ext, 0.0)
    hc_prev = pltpu.roll(hc, 1, 0)
    hc_next = pltpu.roll(hc, n - 1, 0)
    conv = cw[0:1] * hc_prev + cw[1:2] * hc + cw[2:3] * hc_next
    return hc, hc_prev, hc_next, conv


def _ext(prev_ref, cur_ref, next_ref):
    return jnp.concatenate([prev_ref[...], cur_ref[...], next_ref[...]], axis=0)


def _valid_rows(ts, s):
    row = lax.broadcasted_iota(jnp.int32, (ts + 2 * HALO, 1), 0) + (pl.program_id(0) * ts - HALO)
    return (row >= 0) & (row < s)


def _head_norm(o, gn):
    out = []
    for h in range(GLA_HEADS):
        oh = o[:, GLA_DV * h : GLA_DV * (h + 1)]
        r = lax.rsqrt(jnp.mean(oh * oh, axis=-1, keepdims=True) + EPS)
        out.append((oh * r, r))
    return out


def _mix_fwd(z, o_f, o_b, conv_w, conv_norm, gla_norm, *, name):
    s = z.shape[0]
    ts = _rows(s, light=True)
    cprev, cnext = _halo_specs(s, ts, CONV_WIDTH, 1)
    uprev, unext = _halo_specs(s, ts, CONV_WIDTH, 2)

    def body(cb_ref, cc_ref, cu_ref, ccp_ref, ccn_ref, cup_ref, cun_ref, g_ref, of_ref, ob_ref, cw_ref, cn_ref, gn_ref, ones_ref, y_ref):
        valid = _valid_rows(ts, s)
        _, _, _, conv = _conv_terms(_ext(ccp_ref, cc_ref, ccn_ref), _ext(cup_ref, cu_ref, cun_ref), cw_ref[...], valid)
        yc = cb_ref[...] * conv[HALO : HALO + ts]
        ms = _dot_split(yc * yc, ones_ref[...]) * (1.0 / CONV_GROUP)
        y_conv = yc * lax.rsqrt(ms + EPS) * cn_ref[...]
        gate = g_ref[...]
        silu = gate * _sigmoid(gate)
        gn = gn_ref[...]
        y_gla = jnp.concatenate([oh * gn for oh, _ in _head_norm(of_ref[...] + ob_ref[...], gn)], axis=1) * silu
        y_ref[...] = jnp.concatenate([y_conv, y_gla], axis=1).astype(y_ref.dtype)

    col = lambda c, w=CONV_WIDTH: pl.BlockSpec((ts, w), lambda i: (i, c))
    return pl.pallas_call(
        body,
        name=name,
        grid=(s // ts,),
        in_specs=[col(0), col(1), col(2), cprev, cnext, uprev, unext, col(3), col(0), col(0),
                  pl.BlockSpec((CONV_K, CONV_WIDTH), lambda i: (0, 0)), pl.BlockSpec((1, CONV_WIDTH), lambda i: (0, 0)),
                  pl.BlockSpec((1, GLA_DV), lambda i: (0, 0)), _ONES_SPEC],
        out_specs=pl.BlockSpec((ts, D_MODEL), lambda i: (i, 0)),
        out_shape=jax.ShapeDtypeStruct((s, D_MODEL), _CD),
        compiler_params=_cp(("parallel",)),
    )(z, z, z, z, z, z, z, z, o_f, o_b, conv_w, conv_norm, gla_norm, _group_ones())


def _mix_bwd(z, o_f, o_b, dy, conv_w, conv_norm, gla_norm, *, name):
    s = z.shape[0]
    ts = _rows(s)
    halos = [_halo_specs(s, ts, CONV_WIDTH, c) for c in (0, 1, 2)]
    dprev, dnext = _halo_specs(s, ts, CONV_WIDTH, 0)

    def body(cb_ref, cc_ref, cu_ref, cbp_ref, cbn_ref, ccp_ref, ccn_ref, cup_ref, cun_ref, g_ref, of_ref, ob_ref,
             dyc_ref, dyg_ref, dyp_ref, dyn_ref, cw_ref, cn_ref, gn_ref, ones_ref, dza_ref, do_ref, dcw_ref, dcn_ref, dgn_ref):
        n = ts + 2 * HALO
        valid = _valid_rows(ts, s)
        cw = cw_ref[...]
        cn = cn_ref[...]
        ones = ones_ref[...]
        cb = _ext(cbp_ref, cb_ref, cbn_ref)
        cc = _ext(ccp_ref, cc_ref, ccn_ref)
        cu = _ext(cup_ref, cu_ref, cun_ref)
        dy = _ext(dyp_ref, dyc_ref, dyn_ref)
        hc, hc_prev, hc_next, conv = _conv_terms(cc, cu, cw, valid)
        yc = cb * conv
        r = lax.rsqrt(_dot_split(yc * yc, ones) * (1.0 / CONV_GROUP) + EPS)
        yh = yc * r
        dyh = dy * cn
        dyc = r * (dyh - yh * (_dot_split(dyh * yh, ones) * (1.0 / CONV_GROUP)))
        dconv = jnp.where(valid, dyc * cb, 0.0)
        dhc = cw[0:1] * pltpu.roll(dconv, n - 1, 0) + cw[1:2] * dconv + cw[2:3] * pltpu.roll(dconv, 1, 0)
        mid = lambda a: a[HALO : HALO + ts]
        dza_ref[:, 0 : 3 * CONV_WIDTH] = jnp.concatenate([mid(dyc * conv), mid(dhc * cu), mid(dhc * cc)], axis=1).astype(dza_ref.dtype)
        dconv_m = mid(dconv)
        colsum = lambda a: jnp.sum(a, axis=0, keepdims=True)
        dcw = jnp.concatenate([colsum(dconv_m * mid(hc_prev)), colsum(dconv_m * mid(hc)), colsum(dconv_m * mid(hc_next))], axis=0)
        dcn = colsum(mid(dy * yh))

        gate = g_ref[...]
        sg = _sigmoid(gate)
        silu = gate * sg
        gn = gn_ref[...]
        dyg = dyg_ref[...]
        don = dyg * silu
        heads = _head_norm(of_ref[...] + ob_ref[...], gn)
        on = jnp.concatenate([oh * gn for oh, _ in heads], axis=1)
        dza_ref[:, 3 * CONV_WIDTH : ZA_COLS] = (dyg * on * (sg * (1.0 + gate * (1.0 - sg)))).astype(dza_ref.dtype)
        dgn = jnp.zeros((1, GLA_DV), F32)
        dos = []
        for h, (oh, rh) in enumerate(heads):
            donh = don[:, GLA_DV * h : GLA_DV * (h + 1)]
            dgn = dgn + colsum(donh * oh)
            doh = donh * gn
            dos.append(rh * (doh - oh * jnp.mean(doh * oh, axis=-1, keepdims=True)))
        do_ref[...] = jnp.concatenate(dos, axis=1)

        first = pl.program_id(0) == 0

        @pl.when(first)
        def _():
            dcw_ref[...] = dcw
            dcn_ref[...] = dcn
            dgn_ref[...] = dgn

        @pl.when(jnp.logical_not(first))
        def _():
            dcw_ref[...] += dcw
            dcn_ref[...] += dcn
            dgn_ref[...] += dgn

    col = lambda c, w=CONV_WIDTH: pl.BlockSpec((ts, w), lambda i: (i, c))
    cw_spec = pl.BlockSpec((CONV_K, CONV_WIDTH), lambda i: (0, 0))
    cn_spec = pl.BlockSpec((1, CONV_WIDTH), lambda i: (0, 0))
    gn_spec = pl.BlockSpec((1, GLA_DV), lambda i: (0, 0))
    return pl.pallas_call(
        body,
        name=name,
        grid=(s // ts,),
        in_specs=[col(0), col(1), col(2), halos[0][0], halos[0][1], halos[1][0], halos[1][1], halos[2][0], halos[2][1],
                  col(3), col(0), col(0), col(0), col(1), dprev, dnext, cw_spec, cn_spec, gn_spec, _ONES_SPEC],
        out_specs=[pl.BlockSpec((ts, ZA_COLS), lambda i: (i, 0)), col(0), cw_spec, cn_spec, gn_spec],
        out_shape=[
            jax.ShapeDtypeStruct((s, ZA_COLS), _CD),
            jax.ShapeDtypeStruct((s, GLA_V_TOTAL), F32),
            jax.ShapeDtypeStruct((CONV_K, CONV_WIDTH), F32),
            jax.ShapeDtypeStruct((1, CONV_WIDTH), F32),
            jax.ShapeDtypeStruct((1, GLA_DV), F32),
        ],
        compiler_params=_cp(("arbitrary",)),
    )(z, z, z, z, z, z, z, z, z, z, o_f, o_b, dy, dy, dy, dy, conv_w, conv_norm, gla_norm, _group_ones())


def _xa_probs(q_ref, kv_ref, h):
    qh = q_ref[:, XA_HEAD_DIM * h : XA_HEAD_DIM * (h + 1)]
    kh = kv_ref[:, XA_HEAD_DIM * h : XA_HEAD_DIM * (h + 1)]
    vh = kv_ref[:, D_MODEL + XA_HEAD_DIM * h : D_MODEL + XA_HEAD_DIM * (h + 1)]
    sc = _dot_nt(qh, kh) * (XA_HEAD_DIM**-0.5)
    e = jnp.exp(sc - jnp.max(sc, axis=-1, keepdims=True))
    return qh, kh, vh, e / jnp.sum(e, axis=-1, keepdims=True)


def _xattn_fwd(qx, kv, *, name):
    s = qx.shape[0]
    ts = _rows(s, light=True, times=4)

    def body(q_ref, kv_ref, o_ref):
        outs = []
        for h in range(XA_HEADS):
            _, _, vh, p = _xa_probs(q_ref, kv_ref, h)
            outs.append(_dot(p, vh))
        o_ref[...] = jnp.concatenate(outs, axis=1).astype(o_ref.dtype)

    return pl.pallas_call(
        body,
        name=name,
        grid=(s // ts,),
        in_specs=[pl.BlockSpec((ts, D_MODEL), lambda i: (i, 0)), pl.BlockSpec((N_MEM, 2 * D_MODEL), lambda i: (0, 0))],
        out_specs=pl.BlockSpec((ts, D_MODEL), lambda i: (i, 0)),
        out_shape=jax.ShapeDtypeStruct((s, D_MODEL), _CD),
        compiler_params=_cp(("parallel",)),
    )(qx, kv)


def _xattn_block(hx, w_xq, kv, w_xo, x1, gain, *, name):
    s, d = hx.shape
    ts = _rows(s)

    def body(h_ref, wq_ref, kv_ref, wo_ref, x_ref, g_ref, q_out, o_out, hm_out, x_out):
        q = _dot(h_ref[...], wq_ref[...]).astype(_CD)
        q_out[...] = q
        heads = []
        for h in range(XA_HEADS):
            _, _, vh, p = _xa_probs(q, kv_ref, h)
            heads.append(_dot(p, vh))
        o = jnp.concatenate(heads, axis=1).astype(_CD)
        o_out[...] = o
        x = _dot(o, wo_ref[...]) + x_ref[...]
        r = lax.rsqrt(jnp.mean(x * x, axis=-1, keepdims=True) + EPS)
        x_out[...] = x
        hm_out[...] = (x * r * g_ref[...]).astype(hm_out.dtype)

    tile = pl.BlockSpec((ts, d), lambda i: (i, 0))
    whole = lambda arr: pl.BlockSpec(arr.shape, lambda i: (0, 0))
    lo = jax.ShapeDtypeStruct((s, d), _CD)
    return pl.pallas_call(
        body,
        name=name,
        grid=(s // ts,),
        in_specs=[tile, whole(w_xq), whole(kv), whole(w_xo), tile, whole(gain)],
        out_specs=[tile] * 4,
        out_shape=[lo, lo, lo, jax.ShapeDtypeStruct((s, d), F32)],
        compiler_params=_cp(("parallel",)),
    )(hx, w_xq, kv, w_xo, x1, gain)


def _xattn_bwd(qx, kv, dx, w_xo, *, name):
    s = qx.shape[0]
    ts = _rows(s, light=True)

    def body(q_ref, kv_ref, dx_ref, w_ref, dq_ref, dkv_ref):
        do = _dot_nt(dx_ref[...], w_ref[...]).astype(_CD)
        dqs, dks, dvs = [], [], []
        for h in range(XA_HEADS):
            qh, kh, vh, p = _xa_probs(q_ref, kv_ref, h)
            doh = do[:, XA_HEAD_DIM * h : XA_HEAD_DIM * (h + 1)]
            dp = _dot_nt(doh, vh)
            ds = p * (dp - jnp.sum(dp * p, axis=-1, keepdims=True)) * (XA_HEAD_DIM**-0.5)
            dqs.append(_dot(ds, kh))
            dks.append(_dot_tn(ds, qh))
            dvs.append(_dot_tn(p, doh))
        dq_ref[...] = jnp.concatenate(dqs, axis=1).astype(dq_ref.dtype)
        dkv = jnp.concatenate(dks + dvs, axis=1)

        @pl.when(pl.program_id(0) == 0)
        def _():
            dkv_ref[...] = dkv

        @pl.when(pl.program_id(0) > 0)
        def _():
            dkv_ref[...] += dkv

    tile = pl.BlockSpec((ts, D_MODEL), lambda i: (i, 0))
    kv_spec = pl.BlockSpec((N_MEM, 2 * D_MODEL), lambda i: (0, 0))
    return pl.pallas_call(
        body,
        name=name,
        grid=(s // ts,),
        in_specs=[tile, kv_spec, tile, pl.BlockSpec((D_MODEL, D_MODEL), lambda i: (0, 0))],
        out_specs=[tile, kv_spec],
        out_shape=[jax.ShapeDtypeStruct((s, D_MODEL), _CD), jax.ShapeDtypeStruct((N_MEM, 2 * D_MODEL), F32)],
        compiler_params=_cp(("arbitrary",)),
    )(qx, kv, dx, w_xo)


def _adamw_math(w, g, m, v):
    m = ADAM_B1 * m + (1.0 - ADAM_B1) * g
    v = ADAM_B2 * v + (1.0 - ADAM_B2) * (g * g)
    m_hat = m / (1.0 - ADAM_B1**ADAM_STEP)
    v_hat = v / (1.0 - ADAM_B2**ADAM_STEP)
    delta = -ADAM_LR * (m_hat / (jnp.sqrt(v_hat) + ADAM_EPS) + ADAM_WD * w)
    return delta, m, v


def _adamw(w, m, v, shard_rows, off, *, transposed, name):
    r, c = w.shape
    by_columns = r % 256 != 0
    tr = 512 if (c if by_columns else r) % 512 == 0 and off % 512 == 0 else 256
    if by_columns:
        assert not transposed and off == 0
        g_spec = tile = pl.BlockSpec((r, tr), lambda i: (0, i))
    else:
        g_spec = pl.BlockSpec((c, tr), lambda i: (off // c, i)) if transposed else pl.BlockSpec((tr, c), lambda i: (off // tr + i, 0))
        tile = pl.BlockSpec((tr, c), lambda i: (i, 0))

    def body(w_ref, g_ref, m_ref, v_ref, go_ref, d_ref, nm_ref, nv_ref):
        g = g_ref[...].T if transposed else g_ref[...]
        go_ref[...] = g
        d_ref[...], nm_ref[...], nv_ref[...] = _adamw_math(w_ref[...], g, m_ref[...], v_ref[...])

    return pl.pallas_call(
        body,
        name=name,
        grid=((c if by_columns else r) // tr,),
        in_specs=[tile, g_spec, tile, tile],
        out_specs=[tile] * 4,
        out_shape=[jax.ShapeDtypeStruct((r, c), F32)] * 4,
        compiler_params=_cp(("parallel",)),
    )(w, shard_rows, m, v)


def _adamw_small(groups, *, name):
    n = len(groups)

    def body(*refs):
        ins, outs = refs[: 4 * n], refs[4 * n :]
        for i in range(n):
            w_ref, g_ref, m_ref, v_ref = ins[4 * i : 4 * i + 4]
            outs[3 * i][...], outs[3 * i + 1][...], outs[3 * i + 2][...] = _adamw_math(w_ref[...], g_ref[...], m_ref[...], v_ref[...])

    flat = [a for grp in groups for a in grp]
    vm = pl.BlockSpec(memory_space=pltpu.VMEM)
    res = pl.pallas_call(
        body,
        name=name,
        in_specs=[vm] * (4 * n),
        out_specs=[vm] * (3 * n),
        out_shape=[jax.ShapeDtypeStruct(grp[0].shape, F32) for grp in groups for _ in range(3)],
        compiler_params=_cp(),
    )(*flat)
    return [tuple(res[3 * i : 3 * i + 3]) for i in range(n)]


def _place():
    return lax.axis_index("x"), lax.axis_index("y"), lax.axis_index("c")


def _rel_chip(x, y, k):
    return (1 - x if k & 2 else x), (1 - y if k & 1 else y)


def _half(c, rh):
    return pl.ds(pl.multiple_of(c * rh, 16), rh)


HBM = pl.BlockSpec(memory_space=pltpu.HBM)
SEM = pl.BlockSpec(memory_space=pltpu.SEMAPHORE)
EFFECT = pltpu.SideEffectType.DATAFLOW_SIDE_EFFECTING


def _in_hbm(a):
    return pltpu.with_memory_space_constraint(a, pltpu.HBM)


def _gather_copies(p_ref, land_ref, send_sems, recv_sems):
    rh = p_ref.shape[0] // 2
    x, y, c = _place()
    rows = _half(c, rh)
    copies = []
    for k in range(1, N_CHIPS):
        cx, cy = _rel_chip(x, y, k)
        copies.append(pltpu.make_async_remote_copy(
            src_ref=p_ref.at[rows], dst_ref=land_ref.at[2 * x + y, rows], send_sem=send_sems.at[k - 1], recv_sem=recv_sems.at[k - 1],
            device_id=(cx, cy, c), device_id_type=MESH))
    copies.append(pltpu.make_async_remote_copy(
        src_ref=p_ref, dst_ref=land_ref.at[2 * x + y], send_sem=send_sems.at[N_CHIPS - 1], recv_sem=recv_sems.at[N_CHIPS - 1],
        device_id=(x, y, 1 - c), device_id_type=MESH))
    return copies


def _gather_start(pack, after, *, name):
    r, w = pack.shape

    def body(p_ref, land_ref, after_ref, send_sems, recv_sems, p_thru, land_thru, token):
        for cp in _gather_copies(p_ref, land_ref, send_sems, recv_sems):
            cp.start()
        token[...] = jnp.zeros_like(token)

    return pl.pallas_call(
        body,
        name=name,
        out_shape=(pltpu.SemaphoreType.DMA((N_CHIPS,)), pltpu.SemaphoreType.DMA((N_CHIPS,)), pltpu.HBM((r, w), pack.dtype),
                   pltpu.HBM((N_CHIPS, r, w), pack.dtype), jax.ShapeDtypeStruct((8, 128), F32)),
        in_specs=(HBM, HBM, ANY),
        out_specs=(SEM, SEM, HBM, HBM, pl.BlockSpec(memory_space=pltpu.VMEM)),
        input_output_aliases={0: 2, 1: 3},
        compiler_params=pltpu.CompilerParams(has_side_effects=EFFECT),
    )(_in_hbm(pack), _in_hbm(lax.empty((N_CHIPS, r, w), pack.dtype)), after)


def _gather_wait(send_sems, recv_sems, pack, land, after, *, name):
    def body(p_ref, land_ref, send_sems, recv_sems, after_ref, p_out, land_out):
        for cp in _gather_copies(p_ref, land_ref, send_sems, recv_sems):
            cp.wait_send()
            cp.wait_recv()

    return pl.pallas_call(
        body,
        name=name,
        out_shape=(pltpu.HBM(pack.shape, pack.dtype), pltpu.HBM(land.shape, land.dtype)),
        in_specs=(HBM, HBM, SEM, SEM, ANY),
        out_specs=(HBM, HBM),
        input_output_aliases={0: 0, 1: 1},
        compiler_params=pltpu.CompilerParams(has_side_effects=EFFECT),
    )(pack, land, send_sems, recv_sems, after)


def _gather_spread(land, *, name):
    n, r, w = land.shape
    rh = r // 2

    def body(land_ref, o_ref, send_sems, recv_sems):
        x, y, c = _place()
        rows = _half(c, rh)
        copies = []
        for k in range(1, N_CHIPS):
            cx, cy = _rel_chip(x, y, k)
            copies.append(pltpu.make_async_remote_copy(
                src_ref=land_ref.at[2 * cx + cy, rows], dst_ref=o_ref.at[2 * cx + cy, rows], send_sem=send_sems.at[k - 1],
                recv_sem=recv_sems.at[k - 1], device_id=(x, y, 1 - c), device_id_type=MESH))
        for cp in copies:
            cp.start()
        for cp in copies:
            cp.wait()

    return pl.pallas_call(
        body,
        name=name,
        in_specs=[ANY],
        out_specs=ANY,
        out_shape=jax.ShapeDtypeStruct(land.shape, land.dtype),
        input_output_aliases={0: 0},
        scratch_shapes=[pltpu.SemaphoreType.DMA((N_CHIPS - 1,)), pltpu.SemaphoreType.DMA((N_CHIPS - 1,))],
        compiler_params=pltpu.CompilerParams(has_side_effects=True),
    )(land)


N_PARTS = 2 * (N_CHIPS - 1)


def _scatter_copies(lo_ref, g_ref, land_lo_ref, land_f_ref, send_sems, recv_sems, starting):
    rh = g_ref.shape[1] // 2
    x, y, c = _place()
    copies = []
    for k in range(1, N_CHIPS):
        cx, cy = _rel_chip(x, y, k)
        for i in range(2):
            part = 2 * (k - 1) + (c if starting else i)
            copies.append(pltpu.make_async_remote_copy(
                src_ref=lo_ref.at[2 * cx + cy, pl.ds(i * rh, rh)], dst_ref=land_lo_ref.at[part],
                send_sem=send_sems.at[2 * (k - 1) + i], recv_sem=recv_sems.at[part], device_id=(cx, cy, i), device_id_type=MESH))
    copies.append(pltpu.make_async_remote_copy(
        src_ref=g_ref.at[2 * x + y, _half(1 - c, rh)], dst_ref=land_f_ref, send_sem=send_sems.at[N_PARTS], recv_sem=recv_sems.at[N_PARTS],
        device_id=(x, y, 1 - c), device_id_type=MESH))
    return copies


def _scatter_start(g_lo, g, *, name):
    n, r, w = g.shape
    rh = r // 2

    def body(lo_ref, g_ref, land_lo_ref, land_f_ref, send_sems, recv_sems, lo_thru, g_thru, land_lo_thru, land_f_thru, token):
        for cp in _scatter_copies(lo_ref, g_ref, land_lo_ref, land_f_ref, send_sems, recv_sems, True):
            cp.start()
        token[...] = jnp.zeros_like(token)

    return pl.pallas_call(
        body,
        name=name,
        out_shape=(pltpu.SemaphoreType.DMA((N_PARTS + 1,)), pltpu.SemaphoreType.DMA((N_PARTS + 1,)), pltpu.HBM(g_lo.shape, g_lo.dtype),
                   pltpu.HBM(g.shape, g.dtype), pltpu.HBM((N_PARTS, rh, w), g_lo.dtype), pltpu.HBM((rh, w), g.dtype),
                   jax.ShapeDtypeStruct((8, 128), F32)),
        in_specs=(HBM, HBM, HBM, HBM),
        out_specs=(SEM, SEM, HBM, HBM, HBM, HBM, pl.BlockSpec(memory_space=pltpu.VMEM)),
        input_output_aliases={0: 2, 1: 3, 2: 4, 3: 5},
        compiler_params=pltpu.CompilerParams(has_side_effects=EFFECT),
    )(_in_hbm(g_lo), _in_hbm(g), _in_hbm(lax.empty((N_PARTS, rh, w), g_lo.dtype)), _in_hbm(lax.empty((rh, w), g.dtype)))


def _scatter_wait(send_sems, recv_sems, g_lo, g, land_lo, land_f, after, *, name):
    def body(lo_ref, g_ref, land_lo_ref, land_f_ref, send_sems, recv_sems, after_ref, o0, o1, o2, o3):
        for cp in _scatter_copies(lo_ref, g_ref, land_lo_ref, land_f_ref, send_sems, recv_sems, False):
            cp.wait_send()
            cp.wait_recv()

    arrays = (g_lo, g, land_lo, land_f)
    return pl.pallas_call(
        body,
        name=name,
        out_shape=tuple(pltpu.HBM(a.shape, a.dtype) for a in arrays),
        in_specs=(HBM, HBM, HBM, HBM, SEM, SEM, ANY),
        out_specs=(HBM, HBM, HBM, HBM),
        input_output_aliases={0: 0, 1: 1, 2: 2, 3: 3},
        compiler_params=pltpu.CompilerParams(has_side_effects=EFFECT),
    )(*arrays, send_sems, recv_sems, after)


def _scatter_sum(g, land_lo, land_f, where, *, name):
    n, r, w = g.shape
    rh = r // 2
    tr = _pick(rh, (256, 160, 80))
    nt = rh // tr

    def body(where_ref, g_ref, f_ref, lo_ref, o_ref):
        acc = g_ref[0] + f_ref[...]
        for part in range(N_PARTS):
            acc = acc + lo_ref[part].astype(F32)
        o_ref[...] = acc

    return pl.pallas_call(
        body,
        name=name,
        grid_spec=pltpu.PrefetchScalarGridSpec(
            num_scalar_prefetch=1,
            grid=(nt,),
            in_specs=[pl.BlockSpec((1, tr, w), lambda i, wh: (wh[1], wh[0] * nt + i, 0)),
                      pl.BlockSpec((tr, w), lambda i, wh: (i, 0)),
                      pl.BlockSpec((N_PARTS, tr, w), lambda i, wh: (0, i, 0))],
            out_specs=pl.BlockSpec((tr, w), lambda i, wh: (wh[0] * nt + i, 0)),
        ),
        out_shape=jax.ShapeDtypeStruct((r, w), F32),
        compiler_params=_cp(("parallel",)),
    )(where, g, land_f, land_lo)


def _swap_all(shards, *, name):
    n = len(shards)

    def body(*refs):
        ins, outs = refs[:n], refs[n : 2 * n]
        send_sems, recv_sems = refs[2 * n :]
        x, y, c = _place()
        copies = []
        for i, (e_ref, o_ref) in enumerate(zip(ins, outs)):
            rows = _half(c, e_ref.shape[0] // 2)
            copies.append(pltpu.make_async_remote_copy(src_ref=e_ref.at[rows], dst_ref=o_ref.at[rows], send_sem=send_sems.at[i],
                                                       recv_sem=recv_sems.at[i], device_id=(x, y, 1 - c), device_id_type=MESH))
        for cp in copies:
            cp.start()
        for cp in copies:
            cp.wait()

    return pl.pallas_call(
        body,
        name=name,
        in_specs=[ANY] * n,
        out_specs=[ANY] * n,
        out_shape=[jax.ShapeDtypeStruct(e.shape, e.dtype) for e in shards],
        input_output_aliases={i: i for i in range(n)},
        scratch_shapes=[pltpu.SemaphoreType.DMA((n,)), pltpu.SemaphoreType.DMA((n,))],
        compiler_params=pltpu.CompilerParams(has_side_effects=True),
    )(*shards)


def _sum_small(small, after):
    n_dev = 8

    def body(s_ref, after_ref, o_ref, all_ref, send_sems, recv_sems):
        x, y, c = _place()
        me = 4 * x + 2 * y + c
        all_ref[me] = s_ref[...]
        copies = []
        for k in range(1, n_dev):
            cx, cy = _rel_chip(x, y, k >> 1)
            cc = 1 - c if k & 1 else c
            copies.append(pltpu.make_async_remote_copy(
                src_ref=s_ref, dst_ref=all_ref.at[me], send_sem=send_sems.at[k - 1], recv_sem=recv_sems.at[k - 1],
                device_id=(cx, cy, cc), device_id_type=MESH))
        for cp in copies:
            cp.start()
        for cp in copies:
            cp.wait()
        acc = all_ref[0]
        for a in range(1, n_dev):
            acc = acc + all_ref[a]
        o_ref[...] = acc

    vm = pl.BlockSpec(memory_space=pltpu.VMEM)
    return pl.pallas_call(
        body,
        name="sum_small",
        in_specs=[vm, ANY],
        out_specs=vm,
        out_shape=jax.ShapeDtypeStruct(small.shape, F32),
        scratch_shapes=[pltpu.VMEM((n_dev,) + small.shape, F32), pltpu.SemaphoreType.DMA((n_dev - 1,)), pltpu.SemaphoreType.DMA((n_dev - 1,))],
        compiler_params=pltpu.CompilerParams(has_side_effects=True),
    )(small, after)


MATS = {"w_in": (776, True), "w_out": (256, False), "w_xq": (256, False), "w_xkv": (512, True), "w_xo": (256, False),
        "w_up": (1024, True), "w_down": (1024, False)}
GATHER_FIRST = ("w_in",)
GATHER_REST = ("w_out", "w_xq", "w_xkv", "w_xo", "w_up", "w_down")
GRAD_GROUPS = (("w_up", "w_down"), ("w_out", "w_xq", "w_xkv", "w_xo"), ("w_in",))


def _group_rows(names):
    n = sum(MATS[name][0] for name in names)
    return n + (-n) % 32


def _pack(pieces, rows):
    p = jnp.concatenate(pieces, axis=0) if len(pieces) > 1 else pieces[0]
    return jnp.pad(p, ((0, rows - p.shape[0]), (0, 0))) if rows > p.shape[0] else p


SMALL = (
    ("mix_norm", 1024), ("conv_norm", 512), ("b_af", 256), ("b_ab", 256), ("gla_norm", 128), ("xa_norm", 1024), ("mem_norm", 1024),
    ("mlp_norm", 1024), ("final_norm", 1024), ("conv_w", 1536), ("w_af", 4096), ("w_ab", 4096), ("loss", 128),
)


def kernel(x, mem, mix_norm, w_in, conv_w, conv_norm, w_af, b_af, w_ab, b_ab, gla_norm, w_out, xa_norm, mem_norm, w_xq, w_xkv, w_xo, mlp_norm, w_up, w_down, final_norm, loss_target, m_mix_norm, m_w_in, m_conv_w, m_conv_norm, m_w_af, m_b_af, m_w_ab, m_b_ab, m_gla_norm, m_w_out, m_xa_norm, m_mem_norm, m_w_xq, m_w_xkv, m_w_xo, m_mlp_norm, m_w_up, m_w_down, m_final_norm, v_mix_norm, v_w_in, v_conv_w, v_conv_norm, v_w_af, v_b_af, v_w_ab, v_b_ab, v_gla_norm, v_w_out, v_xa_norm, v_mem_norm, v_w_xq, v_w_xkv, v_w_xo, v_mlp_norm, v_w_up, v_w_down, v_final_norm):
    given = dict(locals())
    xi, yi, ci = _place()
    chip = 2 * xi + yi
    where = jnp.stack([ci, chip]).astype(jnp.int32)

    lo = {name: (given[name][0].T if MATS[name][1] else given[name][0]).astype(_CD) for name in MATS}
    pack_rest = _pack([lo[name] for name in GATHER_REST], _group_rows(GATHER_REST))
    pack_first = _pack([lo[name] for name in GATHER_FIRST], _group_rows(GATHER_FIRST))
    xs, mems, tgt = x[0], mem[0], loss_target[0]
    behind = lambda gain, token: gain + token[0, 0]

    def placed(shard, full_shape, col):
        return lax.dynamic_update_slice(jnp.zeros(full_shape, F32), shard, (0, col)).reshape(-1, 128)

    sw = jnp.concatenate([
        placed(conv_w[0], (CONV_K, CONV_WIDTH), 128 * chip),
        placed(w_af[0], (GLA_LOWRANK, GLA_K_TOTAL), 64 * chip),
        placed(w_ab[0], (GLA_LOWRANK, GLA_K_TOTAL), 64 * chip),
    ], axis=0)
    sw = jnp.pad(sw, ((0, SMALL_ROWS - sw.shape[0]), (0, 0))) * (ci == 0).astype(F32)
    sw = _sum_small(sw, mix_norm)

    first_send, first_recv, pack_first, land_first, first_token = _gather_start(pack_first, sw, name="gather_first_start")
    rest_send, rest_recv, pack_rest, land_rest, rest_token = _gather_start(pack_rest, first_token, name="gather_rest_start")
    h1 = _rms_fwd(xs, behind(mix_norm, rest_token), name="norm_mix")
    pack_first, land_first = _gather_wait(first_send, first_recv, pack_first, land_first, h1, name="gather_first_wait")
    got_first = _gather_spread(land_first, name="gather_first_spread")

    def whole(got, off, rows):
        return got[:, off : off + rows].reshape(N_CHIPS * rows, D_MODEL)

    w_in_t = whole(got_first, 0, MATS["w_in"][0])
    w_za = jnp.concatenate([w_in_t[0:1536], w_in_t[2560:3072]], axis=0)
    w_zb = jnp.concatenate([w_in_t[1536:2560], w_in_t[3072:W_IN_COLS], jnp.zeros((ZB_COLS - 1056, D_MODEL), _CD)], axis=0)
    conv_w_full = sw[0:12].reshape(CONV_K, CONV_WIDTH)
    w_af_full = sw[12:44].reshape(GLA_LOWRANK, GLA_K_TOTAL)
    w_ab_full = sw[44:76].reshape(GLA_LOWRANK, GLA_K_TOTAL)
    waf_p = jnp.pad(w_af_full, ((0, 128 - GLA_LOWRANK), (0, 0))).astype(_CD)
    wab_p = jnp.pad(w_ab_full, ((GLA_LOWRANK, 128 - 2 * GLA_LOWRANK), (0, 0))).astype(_CD)

    z_a, z_b = _mm_two(h1, w_za, w_zb, name="proj_in")
    b_f, b_b = _gate_fwd(z_b, waf_p, wab_p, b_af, b_ab, name="gates")
    o_f, st_f, o_b, st_b = _gla_fwd(z_b, b_f, b_b, name="gla_scan")
    y = _mix_fwd(z_a, o_f, o_b, conv_w_full, conv_norm, gla_norm, name="mix_out")
    pack_rest, land_rest = _gather_wait(rest_send, rest_recv, pack_rest, land_rest, y, name="gather_rest_wait")
    gathered = _gather_spread(land_rest, name="gather_rest_spread")
    wt, off = {}, 0
    for name in GATHER_REST:
        wt[name] = whole(gathered, off, MATS[name][0])
        off += MATS[name][0]
    x1, hx = _mm_rows(y, wt["w_out"], mode="nn", name="proj_out", rows=(xs,), vecs=(xa_norm,), out_rows=(F32, _CD),
                      epilogue=_ep_residual_norm, tm=1024)
    hmem = _rms_fwd(mems, mem_norm, name="norm_mem")
    kv = _mm(hmem, wt["w_xkv"], mode="nt", name="proj_xkv", out_dtypes=(_CD,))
    qx, ox, hm, x2 = _xattn_block(hx, wt["w_xq"], kv, wt["w_xo"], x1, mlp_norm, name="xattn_block")
    act, relu_u = _mm(hm, wt["w_up"], mode="nt", name="mlp_up", out_dtypes=(_CD, _CD), tm=2048,
                      epilogue=lambda acc: (jnp.square(jnp.maximum(acc, 0.0)), jnp.maximum(acc, 0.0)))
    dx3, dx3_lo, loss_part, g_final_norm = _mm_rows(
        act, wt["w_down"], mode="nn", name="mlp_down", rows=(x2, tgt), vecs=(final_norm.reshape(1, D_MODEL),),
        out_rows=(F32, _CD), out_vecs=(128, D_MODEL), epilogue=_ep_loss)

    grads_t = {}

    def start_group(names, tag):
        rows = _group_rows(names)
        g = jnp.stack([_pack([grads_t[name][a * MATS[name][0] : (a + 1) * MATS[name][0]] for name in names], rows) for a in range(N_CHIPS)])
        return _scatter_start(g.astype(_TD), g, name="grads_" + tag + "_start")

    def finish_group(state, after, tag):
        send_sems, recv_sems, g_lo, g, land_lo, land_f, _ = state
        g_lo, g, land_lo, land_f = _scatter_wait(send_sems, recv_sems, g_lo, g, land_lo, land_f, after, name="grads_" + tag + "_wait")
        return _scatter_sum(g, land_lo, land_f, where, name="grads_" + tag + "_sum")

    def new_packs(names):
        shape = (N_CHIPS, _group_rows(names), D_MODEL)
        return lax.empty(shape, F32), lax.empty(shape, _TD)

    def grad_into(packs, names, which, a, b, name):
        off = sum(MATS[other][0] for other in names[: names.index(which)])
        return _mm_tn_into(a, b, packs, rows=MATS[which][0], off=off, name=name)

    du = _mm(dx3_lo, wt["w_down"], mode="nt", name="mlp_down_dx", out_dtypes=(_CD,), extras=(relu_u,), tm=2048,
             epilogue=lambda acc, rr: (acc * (2.0 * rr.astype(F32)),))
    packs = new_packs(GRAD_GROUPS[0])
    packs = grad_into(packs, GRAD_GROUPS[0], "w_down", act, dx3_lo, "mlp_down_dw")
    packs = grad_into(packs, GRAD_GROUPS[0], "w_up", du, hm, "mlp_up_dw")
    mlp_state = _scatter_start(packs[1], packs[0], name="grads_mlp_start")
    dx2, dx2_lo, g_mlp_norm = _mm_rows(
        du, wt["w_up"], mode="nn", name="mlp_up_dx", rows=(x2, dx3), vecs=(behind(mlp_norm, mlp_state[-1]),),
        out_rows=(F32, _CD), out_vecs=(D_MODEL,), epilogue=_ep_norm_bwd)
    packs = new_packs(GRAD_GROUPS[1])
    packs = grad_into(packs, GRAD_GROUPS[1], "w_xo", ox, dx2_lo, "proj_xo_dw")
    dqx, dkv = _xattn_bwd(qx, kv, dx2_lo, wt["w_xo"], name="xattn_bwd")
    packs = grad_into(packs, GRAD_GROUPS[1], "w_xq", hx, dqx, "proj_xq_dw")
    dx1, dx1_lo, g_xa_norm = _mm_rows(
        dqx, wt["w_xq"], mode="nt", name="proj_xq_dx", rows=(x1, dx2), vecs=(xa_norm,),
        out_rows=(F32, _CD), out_vecs=(D_MODEL,), epilogue=_ep_norm_bwd, tm=1024)
    dkv_lo = dkv.astype(_CD)
    packs = grad_into(packs, GRAD_GROUPS[1], "w_xkv", dkv_lo, hmem, "proj_xkv_dw")
    dhmem = _mm(dkv_lo, wt["w_xkv"], mode="nn", name="proj_xkv_dx")
    g_mem_norm = _rms_gain_grad(mems, dhmem, name="norm_mem_bwd")
    dy = _mm(dx1_lo, wt["w_out"], mode="nt", name="proj_out_dx")
    packs = grad_into(packs, GRAD_GROUPS[1], "w_out", y, dx1_lo, "proj_out_dw")
    attn_state = _scatter_start(packs[1], packs[0], name="grads_attn_start")
    dz_a, do, g_conv_w, g_conv_norm, g_gla_norm = _mix_bwd(z_a, o_f, o_b, dy, conv_w_full, behind(conv_norm, attn_state[-1]), gla_norm, name="mix_out_bwd")
    dqkv_f, db_f, dqkv_b, db_b = _gla_bwd(z_b, b_f, b_b, do, st_f, st_b, name="gla_scan_bwd")
    dz_b, g_waf_p, g_wab_p, g_b_af, g_b_ab = _gate_bwd(z_b, waf_p, wab_p, b_af, b_ab, db_f, db_b, dqkv_f, dqkv_b, name="gates_bwd")
    g_za = _mm_tn(dz_a, h1, name="proj_in_a_dw")
    g_zb = _mm_tn(dz_b, h1, name="proj_in_b_dw")
    grads_t["w_in"] = jnp.concatenate([g_za[0:1536], g_zb[0:1024], g_za[1536:2048], g_zb[1024:1056]], axis=0)
    in_state = start_group(GRAD_GROUPS[2], "in")
    grad_x, g_mix_norm = _mm_rows(
        dz_a, w_za, mode="nn", name="proj_in_dx", more=((dz_b, w_zb),), rows=(xs, dx1), vecs=(behind(mix_norm, in_state[-1]),),
        out_rows=(F32,), out_vecs=(D_MODEL,), epilogue=_ep_norm_bwd)

    half_mlp = finish_group(mlp_state, grad_x, "mlp")
    half_attn = finish_group(attn_state, half_mlp, "attn")
    half_in = finish_group(in_state, half_attn, "in")
    shard_rows = {}
    for names, rows in zip(GRAD_GROUPS, _swap_all([half_mlp, half_attn, half_in], name="shards_to_sibling")):
        off = 0
        for name in names:
            shard_rows[name] = (rows, off)
            off += MATS[name][0]

    small_vals = dict(mix_norm=g_mix_norm, conv_norm=g_conv_norm, b_af=g_b_af, b_ab=g_b_ab, gla_norm=g_gla_norm, xa_norm=g_xa_norm,
                      mem_norm=g_mem_norm, mlp_norm=g_mlp_norm, final_norm=g_final_norm, conv_w=g_conv_w,
                      w_af=g_waf_p[0:GLA_LOWRANK], w_ab=g_wab_p[GLA_LOWRANK : 2 * GLA_LOWRANK], loss=loss_part)
    small = jnp.concatenate([small_vals[name].reshape(-1, 128) for name, _ in SMALL], axis=0)
    small = _sum_small(jnp.pad(small, ((0, SMALL_ROWS - small.shape[0]), (0, 0))), loss_part)
    g_small, off = {}, 0
    for name, n in SMALL:
        g_small[name] = small[off : off + n // 128]
        off += n // 128
    loss = g_small["loss"][0, 0]
    g_small["conv_w"] = lax.dynamic_slice(g_small["conv_w"].reshape(CONV_K, CONV_WIDTH), (0, 128 * chip), (CONV_K, 128))
    g_small["w_af"] = lax.dynamic_slice(g_small["w_af"].reshape(GLA_LOWRANK, GLA_K_TOTAL), (0, 64 * chip), (GLA_LOWRANK, 64))
    g_small["w_ab"] = lax.dynamic_slice(g_small["w_ab"].reshape(GLA_LOWRANK, GLA_K_TOTAL), (0, 64 * chip), (GLA_LOWRANK, 64))

    names = ["mix_norm", "w_in", "conv_w", "conv_norm", "w_af", "b_af", "w_ab", "b_ab", "gla_norm", "w_out", "xa_norm", "mem_norm",
             "w_xq", "w_xkv", "w_xo", "mlp_norm", "w_up", "w_down", "final_norm"]
    big_names = list(MATS)
    as2d = lambda a: a.reshape(1, -1) if a.ndim == 1 else a.reshape(a.shape[-2:])
    grads, deltas, new_m, new_v = {}, {}, {}, {}
    for name in big_names:
        rows, off = shard_rows[name]
        wmv = [as2d(given[name]), as2d(given["m_" + name]), as2d(given["v_" + name])]
        as_stored = name == "w_in"
        if as_stored:
            wmv = [a.T for a in wmv]
        res = _adamw(*wmv, rows, off, transposed=MATS[name][1] and not as_stored, name="adamw_" + name)
        grads[name], deltas[name], new_m[name], new_v[name] = [a.T for a in res] if as_stored else res
    small_names = [name for name in names if name not in big_names]
    groups = []
    for name in small_names:
        grads[name] = g_small[name].reshape(as2d(given[name]).shape)
        groups.append((as2d(given[name]), grads[name], as2d(given["m_" + name]), as2d(given["v_" + name])))
    for name, res in zip(small_names, _adamw_small(groups, name="adamw_small")):
        deltas[name], new_m[name], new_v[name] = res

    like = lambda name, a: a.reshape(given[name].shape)
    return (loss, grad_x[None], *[like(n, grads[n]) for n in names], *[like(n, deltas[n]) for n in names],
            *[like(n, new_m[n]) for n in names], *[like(n, new_v[n]) for n in names])
```

```python
import jax
import jax.numpy as jnp
from jax import lax
from jax.experimental import pallas as pl
from jax.experimental.pallas import tpu as pltpu

F32 = jnp.float32
BF16 = jnp.bfloat16
_CD = jnp.bfloat16
_TD = jnp.bfloat16

D_MODEL = 1024
N_MEM = 256
CONV_WIDTH = 512
CONV_GROUP = 64
CONV_K = 3
GLA_HEADS = 4
GLA_DK = 64
GLA_DV = 128
GLA_K_TOTAL = 256
GLA_V_TOTAL = 512
GLA_LOWRANK = 16
GLA_GATE_SCALE = 1.0 / 16.0
GLA_CHUNK = 64
XA_HEADS = 4
XA_HEAD_DIM = 256
D_FF = 4096
EPS = 1e-6
W_IN_COLS = 3104
ZA_COLS = 2048
ZB_COLS = 1152
LR_COL = 1024

ADAM_LR = 0.001
ADAM_B1 = 0.9
ADAM_B2 = 0.999
ADAM_EPS = 1e-08
ADAM_WD = 0.01
ADAM_STEP = 10

N_CHIPS = 4
SMALL_ROWS = 128

_TS = 512
_VMEM = 44 * 1024 * 1024
MESH = pl.DeviceIdType.MESH
ANY = pl.BlockSpec(memory_space=pl.ANY)


def _cp(sem=None, **kw):
    return pltpu.CompilerParams(dimension_semantics=sem, vmem_limit_bytes=_VMEM, **kw)


def _dot(a, b):
    return jnp.dot(a.astype(_CD), b.astype(_CD), preferred_element_type=F32)


def _dot_nt(a, b):
    return lax.dot_general(a.astype(_CD), b.astype(_CD), (((1,), (1,)), ((), ())), preferred_element_type=F32)


def _dot_tn(a, b):
    return lax.dot_general(a.astype(_CD), b.astype(_CD), (((0,), (0,)), ((), ())), preferred_element_type=F32)


def _dot_split(x, ones):
    hi = x.astype(BF16)
    r = x - hi.astype(F32)
    mid = r.astype(BF16)
    lo = (r - mid.astype(F32)).astype(BF16)
    d = lambda p: jnp.dot(p, ones, preferred_element_type=F32)
    return d(hi) + d(mid) + d(lo)


def _pick(n, cands=(1024, 640, 512, 256, 128)):
    for t in cands:
        if n % t == 0:
            return t
    return n


def _rows(s, light=False, times=2):
    return min(times * _TS if light else _TS, s)


def _sigmoid(v):
    e = jnp.exp(-jnp.abs(v))
    return jnp.where(v >= 0, 1.0 / (1.0 + e), e / (1.0 + e))


def _mm(a, b, *, mode, name, out_dtypes=(F32,), extras=(), epilogue=None, tm=None, tn=None, tk=None):
    m, k = a.shape
    n = b.shape[1] if mode == "nn" else b.shape[0]
    tm = min(m, tm or 1024)
    tn = tn or _pick(n)
    tk = tk or _pick(k)
    nk = k // tk
    n_ex, n_out = len(extras), len(out_dtypes)

    def body(*refs):
        a_ref, b_ref = refs[:2]
        ex = refs[2 : 2 + n_ex]
        outs = refs[2 + n_ex : 2 + n_ex + n_out]
        part = _dot(a_ref[...], b_ref[...]) if mode == "nn" else _dot_nt(a_ref[...], b_ref[...])

        def finish(acc):
            res = epilogue(acc, *[e[...] for e in ex]) if epilogue else (acc,)
            for o, r in zip(outs, res):
                o[...] = r.astype(o.dtype)

        if nk == 1:
            finish(part)
        else:
            acc_ref = refs[-1]
            kk = pl.program_id(2)

            @pl.when(kk == 0)
            def _():
                acc_ref[...] = part

            @pl.when(kk > 0)
            def _():
                acc_ref[...] += part

            @pl.when(kk == nk - 1)
            def _():
                finish(acc_ref[...])

    b_spec = pl.BlockSpec((tk, tn), lambda i, j, kk: (kk, j)) if mode == "nn" else pl.BlockSpec((tn, tk), lambda i, j, kk: (j, kk))
    tile = pl.BlockSpec((tm, tn), lambda i, j, kk: (i, j))
    out = pl.pallas_call(
        body,
        name=name,
        grid=(m // tm, n // tn, nk),
        in_specs=[pl.BlockSpec((tm, tk), lambda i, j, kk: (i, kk)), b_spec] + [tile] * n_ex,
        out_specs=[tile] * n_out,
        out_shape=[jax.ShapeDtypeStruct((m, n), dt) for dt in out_dtypes],
        scratch_shapes=[pltpu.VMEM((tm, tn), F32)] if nk > 1 else [],
        compiler_params=_cp(("parallel", "parallel", "arbitrary")),
    )(a, b, *extras)
    return out[0] if n_out == 1 else out


def _mm_two(a, b1, b2, *, name, tm=512):
    m, k = a.shape
    tm = min(m, tm)

    def body(a_ref, b1_ref, b2_ref, o1_ref, o2_ref):
        av = a_ref[...]
        o1_ref[...] = _dot_nt(av, b1_ref[...])
        o2_ref[...] = _dot_nt(av, b2_ref[...])

    whole = lambda arr: pl.BlockSpec(arr.shape, lambda i: (0, 0))
    rows = lambda n: pl.BlockSpec((tm, n), lambda i: (i, 0))
    return pl.pallas_call(
        body,
        name=name,
        grid=(m // tm,),
        in_specs=[rows(k), whole(b1), whole(b2)],
        out_specs=[rows(b1.shape[0]), rows(b2.shape[0])],
        out_shape=[jax.ShapeDtypeStruct((m, b1.shape[0]), F32), jax.ShapeDtypeStruct((m, b2.shape[0]), F32)],
        compiler_params=_cp(("parallel",)),
    )(a, b1, b2)


def _mm_tn(a, b, *, name):
    s, m = a.shape
    n = b.shape[1]
    cap = max(128, (1 << 20) // n)
    tm = _pick(m, tuple(t for t in (512, 640, 384, 256, 128) if t <= max(cap, 128)))
    ts = min(s, 1 << (((1 << 22) // n).bit_length() - 1))
    ns = s // ts

    def body(a_ref, b_ref, o_ref):
        part = _dot_tn(a_ref[...], b_ref[...])
        if ns == 1:
            o_ref[...] = part
        else:
            ss = pl.program_id(1)

            @pl.when(ss == 0)
            def _():
                o_ref[...] = part

            @pl.when(ss > 0)
            def _():
                o_ref[...] += part

    return pl.pallas_call(
        body,
        name=name,
        grid=(m // tm, ns),
        in_specs=[pl.BlockSpec((ts, tm), lambda i, ss: (ss, i)), pl.BlockSpec((ts, n), lambda i, ss: (ss, 0))],
        out_specs=pl.BlockSpec((tm, n), lambda i, ss: (i, 0)),
        out_shape=jax.ShapeDtypeStruct((m, n), F32),
        compiler_params=_cp(("parallel", "arbitrary")),
    )(a, b)


def _mm_tn_into(a, b, packs, *, rows, off, name):
    s, m = a.shape
    n = b.shape[1]
    tm = 1024 if rows % 1024 == 0 and s >= 4096 else 512
    tr = min(tm, rows)
    per, chips = rows // tr, tm // tr
    ts = min(s, (1 << (((1 << 22) // n).bit_length() - 1)) * 512 // tm)
    ns = s // ts

    def body(a_ref, b_ref, f_in, lo_in, f_ref, lo_ref):
        part = _dot_tn(a_ref[...], b_ref[...])
        pieces = [part[c * tr : (c + 1) * tr] for c in range(chips)]
        if ns == 1:
            for c, p in enumerate(pieces):
                f_ref[c] = p
                lo_ref[c] = p.astype(lo_ref.dtype)
        else:
            ss = pl.program_id(1)

            @pl.when(ss == 0)
            def _():
                for c, p in enumerate(pieces):
                    f_ref[c] = p

            @pl.when(ss > 0)
            def _():
                for c, p in enumerate(pieces):
                    f_ref[c] += p

            @pl.when(ss == ns - 1)
            def _():
                lo_ref[...] = f_ref[...].astype(lo_ref.dtype)

    spec = pl.BlockSpec((chips, tr, n), lambda i, ss: (i // per, off // tr + i % per, 0))
    return pl.pallas_call(
        body,
        name=name,
        grid=(m // tm, ns),
        in_specs=[pl.BlockSpec((ts, tm), lambda i, ss: (ss, i)), pl.BlockSpec((ts, n), lambda i, ss: (ss, 0)), ANY, ANY],
        out_specs=[spec, spec],
        out_shape=[jax.ShapeDtypeStruct(p.shape, p.dtype) for p in packs],
        input_output_aliases={2: 0, 3: 1},
        compiler_params=_cp(("parallel", "arbitrary")),
    )(a, b, *packs)


def _mm_rows(a, b, *, mode, name, more=(), rows=(), vecs=(), out_rows=(), out_vecs=(), epilogue, tm=512):
    m, k = a.shape
    n = b.shape[1] if mode == "nn" else b.shape[0]
    tm = min(m, tm)
    parts = 2 if tm % 256 == 0 else 1
    n_m, n_r, n_v, n_or, n_ov = 2 * len(more), len(rows), len(vecs), len(out_rows), len(out_vecs)

    def body(*refs):
        a_ref, b_ref = refs[:2]
        m_refs = refs[2 : 2 + n_m]
        rest = refs[2 + n_m :]
        r_refs = rest[:n_r]
        v_refs = rest[n_r : n_r + n_v]
        or_refs = rest[n_r + n_v : n_r + n_v + n_or]
        ov_refs = rest[n_r + n_v + n_or :]
        res_vecs = None
        for p in range(parts):
            rs = slice(p * tm // parts, (p + 1) * tm // parts)
            acc = _dot(a_ref[rs, :], b_ref[...]) if mode == "nn" else _dot_nt(a_ref[rs, :], b_ref[...])
            for a2_ref, b2_ref in zip(m_refs[0::2], m_refs[1::2]):
                acc = acc + _dot(a2_ref[rs, :], b2_ref[...])
            res_rows, part_vecs = epilogue(acc, [r[rs, :] for r in r_refs], [v[...] for v in v_refs])
            for o, r in zip(or_refs, res_rows):
                o[rs, :] = r.astype(o.dtype)
            res_vecs = part_vecs if res_vecs is None else [s + t for s, t in zip(res_vecs, part_vecs)]
        if n_ov:
            first = pl.program_id(0) == 0

            @pl.when(first)
            def _():
                for o, r in zip(ov_refs, res_vecs):
                    o[...] = r

            @pl.when(jnp.logical_not(first))
            def _():
                for o, r in zip(ov_refs, res_vecs):
                    o[...] += r

    tile = pl.BlockSpec((tm, n), lambda i: (i, 0))
    whole = lambda arr: pl.BlockSpec(arr.shape, lambda i: (0, 0))
    vec = lambda w: pl.BlockSpec((1, w), lambda i: (0, 0))
    out = pl.pallas_call(
        body,
        name=name,
        grid=(m // tm,),
        in_specs=[pl.BlockSpec((tm, k), lambda i: (i, 0)), whole(b)]
        + [spec for a2, b2 in more for spec in (pl.BlockSpec((tm, a2.shape[1]), lambda i: (i, 0)), whole(b2))]
        + [tile] * n_r + [vec(v.shape[1]) for v in vecs],
        out_specs=[tile] * n_or + [vec(w) for w in out_vecs],
        out_shape=[jax.ShapeDtypeStruct((m, n), dt) for dt in out_rows] + [jax.ShapeDtypeStruct((1, w), F32) for w in out_vecs],
        compiler_params=_cp(("arbitrary",) if n_ov else ("parallel",)),
    )(a, b, *[x for pair in more for x in pair], *rows, *vecs)
    return out


def _ep_residual_norm(acc, rows, vecs):
    x = acc + rows[0]
    r = lax.rsqrt(jnp.mean(x * x, axis=-1, keepdims=True) + EPS)
    return [x, x * r * vecs[0]], []


def _ep_norm_bwd(acc, rows, vecs):
    dy = acc
    for extra in rows[2:]:
        dy = dy + extra
    x, dres = rows[0], rows[1]
    r = lax.rsqrt(jnp.mean(x * x, axis=-1, keepdims=True) + EPS)
    xh = x * r
    dxh = dy * vecs[0]
    dx = r * (dxh - xh * jnp.mean(dxh * xh, axis=-1, keepdims=True)) + dres
    return [dx, dx], [jnp.sum(dy * xh, axis=0, keepdims=True)]


def _ep_loss(acc, rows, vecs):
    x = acc + rows[0]
    d = x.shape[-1]
    r = lax.rsqrt(jnp.mean(x * x, axis=-1, keepdims=True) + EPS)
    xh = x * r
    err = xh * vecs[0] - rows[1]
    loss = jnp.zeros((1, 128), F32) + 0.5 * jnp.sum(jnp.mean(err * err, axis=-1, keepdims=True))
    dy = err * (1.0 / d)
    dxh = dy * vecs[0]
    dx = r * (dxh - xh * jnp.mean(dxh * xh, axis=-1, keepdims=True))
    return [dx, dx], [loss, jnp.sum(dy * xh, axis=0, keepdims=True)]


def _rms_fwd(x, g, *, name):
    s, d = x.shape
    ts = _rows(s, light=True, times=4)

    def body(x_ref, g_ref, o_ref):
        xf = x_ref[...]
        r = lax.rsqrt(jnp.mean(xf * xf, axis=-1, keepdims=True) + EPS)
        o_ref[...] = (xf * r * g_ref[...]).astype(o_ref.dtype)

    return pl.pallas_call(
        body,
        name=name,
        grid=(s // ts,),
        in_specs=[pl.BlockSpec((ts, d), lambda i: (i, 0)), pl.BlockSpec((1, d), lambda i: (0, 0))],
        out_specs=pl.BlockSpec((ts, d), lambda i: (i, 0)),
        out_shape=jax.ShapeDtypeStruct((s, d), _CD),
        compiler_params=_cp(("parallel",)),
    )(x, g)


def _rms_gain_grad(x, dy, *, name):
    s, d = x.shape
    ts = _rows(s)

    def body(x_ref, dy_ref, dg_ref):
        xf = x_ref[...]
        r = lax.rsqrt(jnp.mean(xf * xf, axis=-1, keepdims=True) + EPS)
        part = jnp.sum(dy_ref[...] * (xf * r), axis=0, keepdims=True)

        @pl.when(pl.program_id(0) == 0)
        def _():
            dg_ref[...] = part

        @pl.when(pl.program_id(0) > 0)
        def _():
            dg_ref[...] += part

    tile = pl.BlockSpec((ts, d), lambda i: (i, 0))
    return pl.pallas_call(
        body,
        name=name,
        grid=(s // ts,),
        in_specs=[tile, tile],
        out_specs=pl.BlockSpec((1, d), lambda i: (0, 0)),
        out_shape=jax.ShapeDtypeStruct((1, d), F32),
        compiler_params=_cp(("arbitrary",)),
    )(x, dy)


def _chunk_scan(v, row_in_chunk, suffix):
    t = v.shape[0]
    step = 1
    while step < GLA_CHUNK:
        if suffix:
            v = v + jnp.where(row_in_chunk < GLA_CHUNK - step, pltpu.roll(v, t - step, 0), 0.0)
        else:
            v = v + jnp.where(row_in_chunk >= step, pltpu.roll(v, step, 0), 0.0)
        step *= 2
    return v


def _gate_pre(lr, w_ref, b_ref):
    return _dot(lr, w_ref[...]) + b_ref[...]


def _gate_fwd(z, waf, wab, baf, bab, *, name):
    s = z.shape[0]
    ts = _rows(s, light=True, times=4)

    def body(lr_ref, waf_ref, wab_ref, baf_ref, bab_ref, bf_ref, bb_ref):
        lr = lr_ref[...]
        ric = lax.broadcasted_iota(jnp.int32, (ts, GLA_K_TOTAL), 0) & (GLA_CHUNK - 1)
        for w_ref, b_ref, o_ref, suffix in ((waf_ref, baf_ref, bf_ref, False), (wab_ref, bab_ref, bb_ref, True)):
            pre = _gate_pre(lr, w_ref, b_ref)
            la = (jnp.minimum(pre, 0.0) - jnp.log(1.0 + jnp.exp(-jnp.abs(pre)))) * GLA_GATE_SCALE
            o_ref[...] = _chunk_scan(la, ric, suffix)

    wspec = pl.BlockSpec((128, GLA_K_TOTAL), lambda i: (0, 0))
    bspec = pl.BlockSpec((1, GLA_K_TOTAL), lambda i: (0, 0))
    tile = pl.BlockSpec((ts, GLA_K_TOTAL), lambda i: (i, 0))
    return pl.pallas_call(
        body,
        name=name,
        grid=(s // ts,),
        in_specs=[pl.BlockSpec((ts, 128), lambda i: (i, LR_COL // 128)), wspec, wspec, bspec, bspec],
        out_specs=[tile, tile],
        out_shape=[jax.ShapeDtypeStruct((s, GLA_K_TOTAL), F32)] * 2,
        compiler_params=_cp(("parallel",)),
    )(z, waf, wab, baf, bab)


def _gate_bwd(z, waf, wab, baf, bab, dbf, dbb, dqkv_f, dqkv_b, *, name):
    s = z.shape[0]
    ts = _rows(s, light=True)

    def body(lr_ref, waf_ref, wab_ref, baf_ref, bab_ref, dbf_ref, dbb_ref, gf_ref, gb_ref, dzb_ref, dwf_ref, dwb_ref, dbaf_ref, dbab_ref):
        lr = lr_ref[...]
        ric = lax.broadcasted_iota(jnp.int32, (ts, GLA_K_TOTAL), 0) & (GLA_CHUNK - 1)
        first = pl.program_id(0) == 0
        dlr = None
        for w_ref, b_ref, db_ref, dw_ref, dbias_ref, suffix in (
            (waf_ref, baf_ref, dbf_ref, dwf_ref, dbaf_ref, True),
            (wab_ref, bab_ref, dbb_ref, dwb_ref, dbab_ref, False),
        ):
            pre = _gate_pre(lr, w_ref, b_ref)
            dla = _chunk_scan(db_ref[...], ric, suffix)
            dpre = dla * GLA_GATE_SCALE * _sigmoid(-pre)
            part = _dot_nt(dpre, w_ref[...])
            dlr = part if dlr is None else dlr + part
            dw = _dot_tn(lr, dpre)
            dbias = jnp.sum(dpre, axis=0, keepdims=True)

            @pl.when(first)
            def _():
                dw_ref[...] = dw
                dbias_ref[...] = dbias

            @pl.when(jnp.logical_not(first))
            def _():
                dw_ref[...] += dw
                dbias_ref[...] += dbias

        dqkv = gf_ref[...].astype(F32) + gb_ref[...].astype(F32)
        dzb_ref[...] = jnp.concatenate([dqkv, dlr], axis=1).astype(dzb_ref.dtype)

    wspec = pl.BlockSpec((128, GLA_K_TOTAL), lambda i: (0, 0))
    bspec = pl.BlockSpec((1, GLA_K_TOTAL), lambda i: (0, 0))
    tile = pl.BlockSpec((ts, GLA_K_TOTAL), lambda i: (i, 0))
    wide = pl.BlockSpec((ts, 2 * GLA_K_TOTAL + GLA_V_TOTAL), lambda i: (i, 0))
    return pl.pallas_call(
        body,
        name=name,
        grid=(s // ts,),
        in_specs=[pl.BlockSpec((ts, 128), lambda i: (i, LR_COL // 128)), wspec, wspec, bspec, bspec, tile, tile, wide, wide],
        out_specs=[pl.BlockSpec((ts, ZB_COLS), lambda i: (i, 0)), wspec, wspec, bspec, bspec],
        out_shape=[
            jax.ShapeDtypeStruct((s, ZB_COLS), _CD),
            jax.ShapeDtypeStruct((128, GLA_K_TOTAL), F32),
            jax.ShapeDtypeStruct((128, GLA_K_TOTAL), F32),
            jax.ShapeDtypeStruct((1, GLA_K_TOTAL), F32),
            jax.ShapeDtypeStruct((1, GLA_K_TOTAL), F32),
        ],
        compiler_params=_cp(("arbitrary",)),
    )(z, waf, wab, baf, bab, dbf, dbb, dqkv_f, dqkv_b)


def _gla_masks(rev):
    lane_head = lax.broadcasted_iota(jnp.int32, (1, GLA_K_TOTAL), 1) >> 6
    head_masks = [lane_head == h for h in range(GLA_HEADS)]
    t = lax.broadcasted_iota(jnp.int32, (GLA_HEADS * GLA_CHUNK, GLA_CHUNK), 0) & (GLA_CHUNK - 1)
    u = lax.broadcasted_iota(jnp.int32, (GLA_HEADS * GLA_CHUNK, GLA_CHUNK), 1)
    tri = (u > t) if rev else (u <= t)
    row = lax.broadcasted_iota(jnp.int32, (GLA_CHUNK, GLA_K_TOTAL), 0)
    total_row = row == (0 if rev else GLA_CHUNK - 1)
    return head_masks, tri, total_row


def _spread(a, head_masks):
    return jnp.concatenate([jnp.where(m, a, 0.0) for m in head_masks], axis=0)


def _stack(a):
    return jnp.concatenate([a[:, GLA_DV * h : GLA_DV * (h + 1)] for h in range(GLA_HEADS)], axis=0)


def _unstack(a):
    return jnp.concatenate([a[GLA_CHUNK * h : GLA_CHUNK * (h + 1)] for h in range(GLA_HEADS)], axis=1)


def _collect(a, head_masks):
    out = None
    for h, m in enumerate(head_masks):
        part = jnp.where(m, a[GLA_CHUNK * h : GLA_CHUNK * (h + 1)], 0.0)
        out = part if out is None else out + part
    return out


def _gla_chunk_terms(q_ref, k_ref, v_ref, b_ref, rows, head_masks, tri, total_row):
    q = q_ref[rows, :] * (GLA_DK**-0.5)
    k = k_ref[rows, :]
    v = v_ref[rows, :]
    b = b_ref[rows, :]
    eb = jnp.exp(b)
    enb = jnp.exp(-b)
    g = jnp.sum(jnp.where(total_row, b, 0.0), axis=0, keepdims=True)
    egb = jnp.exp(g - b)
    qt = q * eb
    kt = k * enb
    kh = k * egb
    q_heads = _spread(qt, head_masks)
    attn = jnp.where(tri, _dot_nt(q_heads, kt), 0.0)
    return v, eb, enb, egb, jnp.exp(g), qt, kt, kh, q_heads, attn


def _gla_specs(s, tb, rev_blocks):
    nb = s // tb
    rb = (lambda i: nb - 1 - i) if rev_blocks else (lambda i: i)
    q_spec = pl.BlockSpec((tb, GLA_K_TOTAL), lambda i: (rb(i), 0))
    k_spec = pl.BlockSpec((tb, GLA_K_TOTAL), lambda i: (rb(i), 1))
    v_spec = pl.BlockSpec((tb, GLA_V_TOTAL), lambda i: (rb(i), 1))
    b_spec = pl.BlockSpec((tb, GLA_K_TOTAL), lambda i: (rb(i), 0))
    o_spec = pl.BlockSpec((tb, GLA_V_TOTAL), lambda i: (rb(i), 0))
    st_spec = pl.BlockSpec((tb // GLA_CHUNK, GLA_DV, GLA_K_TOTAL), lambda i: (rb(i), 0, 0))
    return nb, q_spec, k_spec, v_spec, b_spec, o_spec, st_spec


def _gla_fwd_chunk(cidx, q_ref, k_ref, v_ref, b_ref, o_ref, sv_ref, st_ref, masks):
    head_masks, tri, total_row = masks
    rows = pl.ds(pl.multiple_of(cidx * GLA_CHUNK, GLA_CHUNK), GLA_CHUNK)
    v, _, _, _, eg, _, _, kh, q_heads, attn = _gla_chunk_terms(q_ref, k_ref, v_ref, b_ref, rows, head_masks, tri, total_row)
    o = jnp.concatenate(
        [_dot(attn[GLA_CHUNK * h : GLA_CHUNK * (h + 1)], v[:, GLA_DV * h : GLA_DV * (h + 1)]) for h in range(GLA_HEADS)], axis=1
    )
    st = st_ref[...]
    o_ref[rows, :] = o + _unstack(_dot_nt(q_heads, st))
    sv_ref[cidx] = st
    st_ref[...] = st * eg + _dot_tn(_stack(v), _spread(kh, head_masks))


def _gla_fwd(z, b_f, b_b, *, name):
    s = z.shape[0]
    tb = _rows(s)
    cpb = tb // GLA_CHUNK
    nb, qf, kf, vf, bf, of, sf = _gla_specs(s, tb, False)
    _, qr, kr, vr, br, orr, sr = _gla_specs(s, tb, True)

    def body(qf_ref, kf_ref, vf_ref, bf_ref, qr_ref, kr_ref, vr_ref, br_ref, of_ref, svf_ref, or_ref, svr_ref, stf_ref, str_ref):
        masks_f, masks_r = _gla_masks(False), _gla_masks(True)

        @pl.when(pl.program_id(0) == 0)
        def _():
            stf_ref[...] = jnp.zeros_like(stf_ref)
            str_ref[...] = jnp.zeros_like(str_ref)

        def chunk(ci, carry):
            _gla_fwd_chunk(ci, qf_ref, kf_ref, vf_ref, bf_ref, of_ref, svf_ref, stf_ref, masks_f)
            _gla_fwd_chunk(cpb - 1 - ci, qr_ref, kr_ref, vr_ref, br_ref, or_ref, svr_ref, str_ref, masks_r)
            return carry

        lax.fori_loop(0, cpb, chunk, 0)

    o_shape = jax.ShapeDtypeStruct((s, GLA_V_TOTAL), F32)
    st_shape = jax.ShapeDtypeStruct((s // GLA_CHUNK, GLA_DV, GLA_K_TOTAL), F32)
    return pl.pallas_call(
        body,
        name=name,
        grid=(nb,),
        in_specs=[qf, kf, vf, bf, qr, kr, vr, br],
        out_specs=[of, sf, orr, sr],
        out_shape=[o_shape, st_shape, o_shape, st_shape],
        scratch_shapes=[pltpu.VMEM((GLA_DV, GLA_K_TOTAL), F32)] * 2,
        compiler_params=_cp(("arbitrary",)),
    )(z, z, z, b_f, z, z, z, b_b)


def _gla_bwd_chunk(cidx, q_ref, k_ref, v_ref, b_ref, do_ref, sv_ref, dqkv_ref, db_ref, dst_ref, masks):
    head_masks, tri, total_row = masks
    rows = pl.ds(pl.multiple_of(cidx * GLA_CHUNK, GLA_CHUNK), GLA_CHUNK)
    v, eb, enb, egb, eg, qt, kt, kh, q_heads, attn = _gla_chunk_terms(q_ref, k_ref, v_ref, b_ref, rows, head_masks, tri, total_row)
    do_c = do_ref[rows, :]
    st = sv_ref[cidx]
    dst = dst_ref[...]
    do_s, v_s = _stack(do_c), _stack(v)
    hs = lambda a, h: a[GLA_CHUNK * h : GLA_CHUNK * (h + 1)]
    vs = lambda a, h: a[:, GLA_DV * h : GLA_DV * (h + 1)]
    dattn = jnp.concatenate([_dot_nt(vs(do_c, h), vs(v, h)) for h in range(GLA_HEADS)], axis=0)
    dattn = jnp.where(tri, dattn, 0.0)
    dv = jnp.concatenate([_dot_tn(hs(attn, h), vs(do_c, h)) for h in range(GLA_HEADS)], axis=1)
    dv = dv + _unstack(_dot_nt(_spread(kh, head_masks), dst))
    dqt = _collect(_dot(do_s, st) + _dot(dattn, kt), head_masks)
    dkt = _dot_tn(dattn, q_heads)
    dkh = _collect(_dot(v_s, dst), head_masks)
    dg = jnp.sum(dkh * kh, axis=0, keepdims=True) + jnp.sum(dst * st, axis=0, keepdims=True) * eg
    db = dqt * qt - dkt * kt - dkh * kh + jnp.where(total_row, dg, 0.0)
    dq = dqt * eb * (GLA_DK**-0.5)
    dk = dkt * enb + dkh * egb
    dqkv_ref[rows, :] = jnp.concatenate([dq, dk, dv], axis=1).astype(dqkv_ref.dtype)
    db_ref[rows, :] = db
    dst_ref[...] = dst * eg + _dot_tn(do_s, q_heads)


def _gla_bwd(z, b_f, b_b, do, st_f, st_b, *, name):
    s = z.shape[0]
    tb = _rows(s)
    cpb = tb // GLA_CHUNK
    wide = 2 * GLA_K_TOTAL + GLA_V_TOTAL
    nb, qf, kf, vf, bf, of, sf = _gla_specs(s, tb, True)
    _, qr, kr, vr, br, orr, sr = _gla_specs(s, tb, False)
    gf = pl.BlockSpec((tb, wide), lambda i: (nb - 1 - i, 0))
    gr = pl.BlockSpec((tb, wide), lambda i: (i, 0))

    def body(qf_ref, kf_ref, vf_ref, bf_ref, dof_ref, svf_ref, qr_ref, kr_ref, vr_ref, br_ref, dor_ref, svr_ref,
             gf_ref, dbf_ref, gr_ref, dbr_ref, dstf_ref, dstr_ref):
        masks_f, masks_r = _gla_masks(False), _gla_masks(True)

        @pl.when(pl.program_id(0) == 0)
        def _():
            dstf_ref[...] = jnp.zeros_like(dstf_ref)
            dstr_ref[...] = jnp.zeros_like(dstr_ref)

        def chunk(ci, carry):
            _gla_bwd_chunk(cpb - 1 - ci, qf_ref, kf_ref, vf_ref, bf_ref, dof_ref, svf_ref, gf_ref, dbf_ref, dstf_ref, masks_f)
            _gla_bwd_chunk(ci, qr_ref, kr_ref, vr_ref, br_ref, dor_ref, svr_ref, gr_ref, dbr_ref, dstr_ref, masks_r)
            return carry

        lax.fori_loop(0, cpb, chunk, 0)

    g_shape = jax.ShapeDtypeStruct((s, wide), _CD)
    db_shape = jax.ShapeDtypeStruct((s, GLA_K_TOTAL), F32)
    return pl.pallas_call(
        body,
        name=name,
        grid=(nb,),
        in_specs=[qf, kf, vf, bf, of, sf, qr, kr, vr, br, orr, sr],
        out_specs=[gf, bf, gr, br],
        out_shape=[g_shape, db_shape, g_shape, db_shape],
        scratch_shapes=[pltpu.VMEM((GLA_DV, GLA_K_TOTAL), F32)] * 2,
        compiler_params=_cp(("arbitrary",)),
    )(z, z, z, b_f, do, st_f, z, z, z, b_b, do, st_b)


HALO = 8


def _halo_specs(s, ts, width, col):
    last = s // HALO - 1
    per = ts // HALO
    prev = pl.BlockSpec((HALO, width), lambda i: (jnp.maximum(i * per - 1, 0), col))
    nxt = pl.BlockSpec((HALO, width), lambda i: (jnp.minimum((i + 1) * per, last), col))
    return prev, nxt


def _group_ones():
    group = jnp.arange(CONV_WIDTH, dtype=jnp.int32) // CONV_GROUP
    return (group[:, None] == group[None, :]).astype(BF16)


_ONES_SPEC = pl.BlockSpec((CONV_WIDTH, CONV_WIDTH), lambda i: (0, 0))


def _conv_terms(cc_ext, cu_ext, cw, valid):
    n = cc_ext.shape[0]
    hc = jnp.where(valid, cc_ext * cu_ext, 0.0)
    hc_prev = pltpu.roll(hc, 1, 0)
    hc_next = pltpu.roll(hc, n - 1, 0)
    conv = cw[0:1] * hc_prev + cw[1:2] * hc + cw[2:3] * hc_next
    return hc, hc_prev, hc_next, conv


def _ext(prev_ref, cur_ref, next_ref):
    return jnp.concatenate([prev_ref[...], cur_ref[...], next_ref[...]], axis=0)


def _valid_rows(ts, s):
    row = lax.broadcasted_iota(jnp.int32, (ts + 2 * HALO, 1), 0) + (pl.program_id(0) * ts - HALO)
    return (row >= 0) & (row < s)


def _head_norm(o, gn):
    out = []
    for h in range(GLA_HEADS):
        oh = o[:, GLA_DV * h : GLA_DV * (h + 1)]
        r = lax.rsqrt(jnp.mean(oh * oh, axis=-1, keepdims=True) + EPS)
        out.append((oh * r, r))
    return out


def _mix_fwd(z, o_f, o_b, conv_w, conv_norm, gla_norm, *, name):
    s = z.shape[0]
    ts = _rows(s, light=True)
    cprev, cnext = _halo_specs(s, ts, CONV_WIDTH, 1)
    uprev, unext = _halo_specs(s, ts, CONV_WIDTH, 2)

    def body(cb_ref, cc_ref, cu_ref, ccp_ref, ccn_ref, cup_ref, cun_ref, g_ref, of_ref, ob_ref, cw_ref, cn_ref, gn_ref, ones_ref, y_ref):
        valid = _valid_rows(ts, s)
        _, _, _, conv = _conv_terms(_ext(ccp_ref, cc_ref, ccn_ref), _ext(cup_ref, cu_ref, cun_ref), cw_ref[...], valid)
        yc = cb_ref[...] * conv[HALO : HALO + ts]
        ms = _dot_split(yc * yc, ones_ref[...]) * (1.0 / CONV_GROUP)
        y_conv = yc * lax.rsqrt(ms + EPS) * cn_ref[...]
        gate = g_ref[...]
        silu = gate * _sigmoid(gate)
        gn = gn_ref[...]
        y_gla = jnp.concatenate([oh * gn for oh, _ in _head_norm(of_ref[...] + ob_ref[...], gn)], axis=1) * silu
        y_ref[...] = jnp.concatenate([y_conv, y_gla], axis=1).astype(y_ref.dtype)

    col = lambda c, w=CONV_WIDTH: pl.BlockSpec((ts, w), lambda i: (i, c))
    return pl.pallas_call(
        body,
        name=name,
        grid=(s // ts,),
        in_specs=[col(0), col(1), col(2), cprev, cnext, uprev, unext, col(3), col(0), col(0),
                  pl.BlockSpec((CONV_K, CONV_WIDTH), lambda i: (0, 0)), pl.BlockSpec((1, CONV_WIDTH), lambda i: (0, 0)),
                  pl.BlockSpec((1, GLA_DV), lambda i: (0, 0)), _ONES_SPEC],
        out_specs=pl.BlockSpec((ts, D_MODEL), lambda i: (i, 0)),
        out_shape=jax.ShapeDtypeStruct((s, D_MODEL), _CD),
        compiler_params=_cp(("parallel",)),
    )(z, z, z, z, z, z, z, z, o_f, o_b, conv_w, conv_norm, gla_norm, _group_ones())


def _mix_bwd(z, o_f, o_b, dx, w_out, conv_w, conv_norm, gla_norm, *, name):
    s = z.shape[0]
    ts = _rows(s)
    halos = [_halo_specs(s, ts, CONV_WIDTH, c) for c in (0, 1, 2)]
    wide = 2 * HALO
    last = s // wide - 1
    dprev = pl.BlockSpec((wide, D_MODEL), lambda i: (jnp.maximum(i * (ts // wide) - 1, 0), 0))
    dnext = pl.BlockSpec((wide, D_MODEL), lambda i: (jnp.minimum((i + 1) * (ts // wide), last), 0))

    def body(cb_ref, cc_ref, cu_ref, cbp_ref, cbn_ref, ccp_ref, ccn_ref, cup_ref, cun_ref, g_ref, of_ref, ob_ref,
             dx_ref, dxp_ref, dxn_ref, w_ref, cw_ref, cn_ref, gn_ref, ones_ref, dza_ref, do_ref, dcw_ref, dcn_ref, dgn_ref):
        n = ts + 2 * HALO
        valid = _valid_rows(ts, s)
        cw = cw_ref[...]
        cn = cn_ref[...]
        ones = ones_ref[...]
        cb = _ext(cbp_ref, cb_ref, cbn_ref)
        cc = _ext(ccp_ref, cc_ref, ccn_ref)
        cu = _ext(cup_ref, cu_ref, cun_ref)
        dy_all = _dot_nt(dx_ref[...], w_ref[...])
        w_conv = w_ref[0:CONV_WIDTH, :]
        dy = jnp.concatenate([_dot_nt(dxp_ref[...], w_conv)[HALO:], dy_all[:, 0:CONV_WIDTH], _dot_nt(dxn_ref[...], w_conv)[:HALO]], axis=0)
        hc, hc_prev, hc_next, conv = _conv_terms(cc, cu, cw, valid)
        yc = cb * conv
        r = lax.rsqrt(_dot_split(yc * yc, ones) * (1.0 / CONV_GROUP) + EPS)
        yh = yc * r
        dyh = dy * cn
        dyc = r * (dyh - yh * (_dot_split(dyh * yh, ones) * (1.0 / CONV_GROUP)))
        dconv = jnp.where(valid, dyc * cb, 0.0)
        dhc = cw[0:1] * pltpu.roll(dconv, n - 1, 0) + cw[1:2] * dconv + cw[2:3] * pltpu.roll(dconv, 1, 0)
        mid = lambda a: a[HALO : HALO + ts]
        dza_ref[:, 0 : 3 * CONV_WIDTH] = jnp.concatenate([mid(dyc * conv), mid(dhc * cu), mid(dhc * cc)], axis=1).astype(dza_ref.dtype)
        dconv_m = mid(dconv)
        colsum = lambda a: jnp.sum(a, axis=0, keepdims=True)
        dcw = jnp.concatenate([colsum(dconv_m * mid(hc_prev)), colsum(dconv_m * mid(hc)), colsum(dconv_m * mid(hc_next))], axis=0)
        dcn = colsum(mid(dy * yh))

        gate = g_ref[...]
        sg = _sigmoid(gate)
        silu = gate * sg
        gn = gn_ref[...]
        dyg = dy_all[:, CONV_WIDTH:]
        don = dyg * silu
        heads = _head_norm(of_ref[...] + ob_ref[...], gn)
        on = jnp.concatenate([oh * gn for oh, _ in heads], axis=1)
        dza_ref[:, 3 * CONV_WIDTH : ZA_COLS] = (dyg * on * (sg * (1.0 + gate * (1.0 - sg)))).astype(dza_ref.dtype)
        dgn = jnp.zeros((1, GLA_DV), F32)
        dos = []
        for h, (oh, rh) in enumerate(heads):
            donh = don[:, GLA_DV * h : GLA_DV * (h + 1)]
            dgn = dgn + colsum(donh * oh)
            doh = donh * gn
            dos.append(rh * (doh - oh * jnp.mean(doh * oh, axis=-1, keepdims=True)))
        do_ref[...] = jnp.concatenate(dos, axis=1)

        first = pl.program_id(0) == 0

        @pl.when(first)
        def _():
            dcw_ref[...] = dcw
            dcn_ref[...] = dcn
            dgn_ref[...] = dgn

        @pl.when(jnp.logical_not(first))
        def _():
            dcw_ref[...] += dcw
            dcn_ref[...] += dcn
            dgn_ref[...] += dgn

    col = lambda c, w=CONV_WIDTH: pl.BlockSpec((ts, w), lambda i: (i, c))
    cw_spec = pl.BlockSpec((CONV_K, CONV_WIDTH), lambda i: (0, 0))
    cn_spec = pl.BlockSpec((1, CONV_WIDTH), lambda i: (0, 0))
    gn_spec = pl.BlockSpec((1, GLA_DV), lambda i: (0, 0))
    return pl.pallas_call(
        body,
        name=name,
        grid=(s // ts,),
        in_specs=[col(0), col(1), col(2), halos[0][0], halos[0][1], halos[1][0], halos[1][1], halos[2][0], halos[2][1],
                  col(3), col(0), col(0), pl.BlockSpec((ts, D_MODEL), lambda i: (i, 0)), dprev, dnext,
                  pl.BlockSpec((D_MODEL, D_MODEL), lambda i: (0, 0)), cw_spec, cn_spec, gn_spec, _ONES_SPEC],
        out_specs=[pl.BlockSpec((ts, ZA_COLS), lambda i: (i, 0)), col(0), cw_spec, cn_spec, gn_spec],
        out_shape=[
            jax.ShapeDtypeStruct((s, ZA_COLS), _CD),
            jax.ShapeDtypeStruct((s, GLA_V_TOTAL), F32),
            jax.ShapeDtypeStruct((CONV_K, CONV_WIDTH), F32),
            jax.ShapeDtypeStruct((1, CONV_WIDTH), F32),
            jax.ShapeDtypeStruct((1, GLA_DV), F32),
        ],
        compiler_params=_cp(("arbitrary",)),
    )(z, z, z, z, z, z, z, z, z, z, o_f, o_b, dx, dx, dx, w_out, conv_w, conv_norm, gla_norm, _group_ones())


def _xa_probs(q_ref, kv_ref, h):
    qh = q_ref[:, XA_HEAD_DIM * h : XA_HEAD_DIM * (h + 1)]
    kh = kv_ref[:, XA_HEAD_DIM * h : XA_HEAD_DIM * (h + 1)]
    vh = kv_ref[:, D_MODEL + XA_HEAD_DIM * h : D_MODEL + XA_HEAD_DIM * (h + 1)]
    sc = _dot_nt(qh, kh) * (XA_HEAD_DIM**-0.5)
    e = jnp.exp(sc - jnp.max(sc, axis=-1, keepdims=True))
    return qh, kh, vh, e / jnp.sum(e, axis=-1, keepdims=True)


def _xattn_fwd(qx, kv, *, name):
    s = qx.shape[0]
    ts = _rows(s, light=True, times=4)

    def body(q_ref, kv_ref, o_ref):
        outs = []
        for h in range(XA_HEADS):
            _, _, vh, p = _xa_probs(q_ref, kv_ref, h)
            outs.append(_dot(p, vh))
        o_ref[...] = jnp.concatenate(outs, axis=1).astype(o_ref.dtype)

    return pl.pallas_call(
        body,
        name=name,
        grid=(s // ts,),
        in_specs=[pl.BlockSpec((ts, D_MODEL), lambda i: (i, 0)), pl.BlockSpec((N_MEM, 2 * D_MODEL), lambda i: (0, 0))],
        out_specs=pl.BlockSpec((ts, D_MODEL), lambda i: (i, 0)),
        out_shape=jax.ShapeDtypeStruct((s, D_MODEL), _CD),
        compiler_params=_cp(("parallel",)),
    )(qx, kv)


def _xattn_block(hx, w_xq, kv, w_xo, x1, gain, *, name):
    s, d = hx.shape
    ts = _rows(s)

    def body(h_ref, wq_ref, kv_ref, wo_ref, x_ref, g_ref, q_out, o_out, hm_out, x_out):
        q = _dot(h_ref[...], wq_ref[...]).astype(_CD)
        q_out[...] = q
        heads = []
        for h in range(XA_HEADS):
            _, _, vh, p = _xa_probs(q, kv_ref, h)
            heads.append(_dot(p, vh))
        o = jnp.concatenate(heads, axis=1).astype(_CD)
        o_out[...] = o
        x = _dot(o, wo_ref[...]) + x_ref[...]
        r = lax.rsqrt(jnp.mean(x * x, axis=-1, keepdims=True) + EPS)
        x_out[...] = x
        hm_out[...] = (x * r * g_ref[...]).astype(hm_out.dtype)

    tile = pl.BlockSpec((ts, d), lambda i: (i, 0))
    whole = lambda arr: pl.BlockSpec(arr.shape, lambda i: (0, 0))
    lo = jax.ShapeDtypeStruct((s, d), _CD)
    return pl.pallas_call(
        body,
        name=name,
        grid=(s // ts,),
        in_specs=[tile, whole(w_xq), whole(kv), whole(w_xo), tile, whole(gain)],
        out_specs=[tile] * 4,
        out_shape=[lo, lo, lo, jax.ShapeDtypeStruct((s, d), F32)],
        compiler_params=_cp(("parallel",)),
    )(hx, w_xq, kv, w_xo, x1, gain)


def _xattn_bwd(qx, kv, dx, w_xo, *, name):
    s = qx.shape[0]
    ts = _rows(s, light=True)

    def body(q_ref, kv_ref, dx_ref, w_ref, dq_ref, dkv_ref):
        do = _dot_nt(dx_ref[...], w_ref[...]).astype(_CD)
        dqs, dks, dvs = [], [], []
        for h in range(XA_HEADS):
            qh, kh, vh, p = _xa_probs(q_ref, kv_ref, h)
            doh = do[:, XA_HEAD_DIM * h : XA_HEAD_DIM * (h + 1)]
            dp = _dot_nt(doh, vh)
            ds = p * (dp - jnp.sum(dp * p, axis=-1, keepdims=True)) * (XA_HEAD_DIM**-0.5)
            dqs.append(_dot(ds, kh))
            dks.append(_dot_tn(ds, qh))
            dvs.append(_dot_tn(p, doh))
        dq_ref[...] = jnp.concatenate(dqs, axis=1).astype(dq_ref.dtype)
        dkv = jnp.concatenate(dks + dvs, axis=1)

        @pl.when(pl.program_id(0) == 0)
        def _():
            dkv_ref[...] = dkv

        @pl.when(pl.program_id(0) > 0)
        def _():
            dkv_ref[...] += dkv

    tile = pl.BlockSpec((ts, D_MODEL), lambda i: (i, 0))
    kv_spec = pl.BlockSpec((N_MEM, 2 * D_MODEL), lambda i: (0, 0))
    return pl.pallas_call(
        body,
        name=name,
        grid=(s // ts,),
        in_specs=[tile, kv_spec, tile, pl.BlockSpec((D_MODEL, D_MODEL), lambda i: (0, 0))],
        out_specs=[tile, kv_spec],
        out_shape=[jax.ShapeDtypeStruct((s, D_MODEL), _CD), jax.ShapeDtypeStruct((N_MEM, 2 * D_MODEL), F32)],
        compiler_params=_cp(("arbitrary",)),
    )(qx, kv, dx, w_xo)


def _adamw_math(w, g, m, v):
    m = ADAM_B1 * m + (1.0 - ADAM_B1) * g
    v = ADAM_B2 * v + (1.0 - ADAM_B2) * (g * g)
    m_hat = m / (1.0 - ADAM_B1**ADAM_STEP)
    v_hat = v / (1.0 - ADAM_B2**ADAM_STEP)
    delta = -ADAM_LR * (m_hat / (jnp.sqrt(v_hat) + ADAM_EPS) + ADAM_WD * w)
    return delta, m, v


def _adamw(w, m, v, shard_rows, off, *, transposed, name):
    r, c = w.shape
    by_columns = r % 256 != 0
    tr = 512 if (c if by_columns else r) % 512 == 0 and off % 512 == 0 else 256
    if by_columns:
        assert not transposed and off == 0
        g_spec = tile = pl.BlockSpec((r, tr), lambda i: (0, i))
    else:
        g_spec = pl.BlockSpec((c, tr), lambda i: (off // c, i)) if transposed else pl.BlockSpec((tr, c), lambda i: (off // tr + i, 0))
        tile = pl.BlockSpec((tr, c), lambda i: (i, 0))

    def body(w_ref, g_ref, m_ref, v_ref, go_ref, d_ref, nm_ref, nv_ref):
        g = g_ref[...].T if transposed else g_ref[...]
        go_ref[...] = g
        d_ref[...], nm_ref[...], nv_ref[...] = _adamw_math(w_ref[...], g, m_ref[...], v_ref[...])

    return pl.pallas_call(
        body,
        name=name,
        grid=((c if by_columns else r) // tr,),
        in_specs=[tile, g_spec, tile, tile],
        out_specs=[tile] * 4,
        out_shape=[jax.ShapeDtypeStruct((r, c), F32)] * 4,
        compiler_params=_cp(("parallel",)),
    )(w, shard_rows, m, v)


def _adamw_small(groups, *, name):
    n = len(groups)

    def body(*refs):
        ins, outs = refs[: 4 * n], refs[4 * n :]
        for i in range(n):
            w_ref, g_ref, m_ref, v_ref = ins[4 * i : 4 * i + 4]
            outs[3 * i][...], outs[3 * i + 1][...], outs[3 * i + 2][...] = _adamw_math(w_ref[...], g_ref[...], m_ref[...], v_ref[...])

    flat = [a for grp in groups for a in grp]
    vm = pl.BlockSpec(memory_space=pltpu.VMEM)
    res = pl.pallas_call(
        body,
        name=name,
        in_specs=[vm] * (4 * n),
        out_specs=[vm] * (3 * n),
        out_shape=[jax.ShapeDtypeStruct(grp[0].shape, F32) for grp in groups for _ in range(3)],
        compiler_params=_cp(),
    )(*flat)
    return [tuple(res[3 * i : 3 * i + 3]) for i in range(n)]


def _place():
    return lax.axis_index("x"), lax.axis_index("y"), lax.axis_index("c")


def _rel_chip(x, y, k):
    return (1 - x if k & 2 else x), (1 - y if k & 1 else y)


def _half(c, rh):
    return pl.ds(pl.multiple_of(c * rh, 16), rh)


HBM = pl.BlockSpec(memory_space=pltpu.HBM)
SEM = pl.BlockSpec(memory_space=pltpu.SEMAPHORE)
EFFECT = pltpu.SideEffectType.DATAFLOW_SIDE_EFFECTING


def _in_hbm(a):
    return pltpu.with_memory_space_constraint(a, pltpu.HBM)


def _gather_copies(p_ref, land_ref, send_sems, recv_sems):
    rh = p_ref.shape[0] // 2
    x, y, c = _place()
    rows = _half(c, rh)
    copies = []
    for k in range(1, N_CHIPS):
        cx, cy = _rel_chip(x, y, k)
        copies.append(pltpu.make_async_remote_copy(
            src_ref=p_ref.at[rows], dst_ref=land_ref.at[2 * x + y, rows], send_sem=send_sems.at[k - 1], recv_sem=recv_sems.at[k - 1],
            device_id=(cx, cy, c), device_id_type=MESH))
    copies.append(pltpu.make_async_remote_copy(
        src_ref=p_ref, dst_ref=land_ref.at[2 * x + y], send_sem=send_sems.at[N_CHIPS - 1], recv_sem=recv_sems.at[N_CHIPS - 1],
        device_id=(x, y, 1 - c), device_id_type=MESH))
    return copies


def _gather_start(pack, after, *, name):
    r, w = pack.shape

    def body(p_ref, land_ref, after_ref, send_sems, recv_sems, p_thru, land_thru, token):
        for cp in _gather_copies(p_ref, land_ref, send_sems, recv_sems):
            cp.start()
        token[...] = jnp.zeros_like(token)

    return pl.pallas_call(
        body,
        name=name,
        out_shape=(pltpu.SemaphoreType.DMA((N_CHIPS,)), pltpu.SemaphoreType.DMA((N_CHIPS,)), pltpu.HBM((r, w), pack.dtype),
                   pltpu.HBM((N_CHIPS, r, w), pack.dtype), jax.ShapeDtypeStruct((8, 128), F32)),
        in_specs=(HBM, HBM, ANY),
        out_specs=(SEM, SEM, HBM, HBM, pl.BlockSpec(memory_space=pltpu.VMEM)),
        input_output_aliases={0: 2, 1: 3},
        compiler_params=pltpu.CompilerParams(has_side_effects=EFFECT),
    )(_in_hbm(pack), _in_hbm(lax.empty((N_CHIPS, r, w), pack.dtype)), after)


def _gather_wait(send_sems, recv_sems, pack, land, after, *, name):
    def body(p_ref, land_ref, send_sems, recv_sems, after_ref, p_out, land_out):
        for cp in _gather_copies(p_ref, land_ref, send_sems, recv_sems):
            cp.wait_send()
            cp.wait_recv()

    return pl.pallas_call(
        body,
        name=name,
        out_shape=(pltpu.HBM(pack.shape, pack.dtype), pltpu.HBM(land.shape, land.dtype)),
        in_specs=(HBM, HBM, SEM, SEM, ANY),
        out_specs=(HBM, HBM),
        input_output_aliases={0: 0, 1: 1},
        compiler_params=pltpu.CompilerParams(has_side_effects=EFFECT),
    )(pack, land, send_sems, recv_sems, after)


def _gather_spread(land, *, name):
    n, r, w = land.shape
    rh = r // 2

    def body(land_ref, o_ref, send_sems, recv_sems):
        x, y, c = _place()
        rows = _half(c, rh)
        copies = []
        for k in range(1, N_CHIPS):
            cx, cy = _rel_chip(x, y, k)
            copies.append(pltpu.make_async_remote_copy(
                src_ref=land_ref.at[2 * cx + cy, rows], dst_ref=o_ref.at[2 * cx + cy, rows], send_sem=send_sems.at[k - 1],
                recv_sem=recv_sems.at[k - 1], device_id=(x, y, 1 - c), device_id_type=MESH))
        for cp in copies:
            cp.start()
        for cp in copies:
            cp.wait()

    return pl.pallas_call(
        body,
        name=name,
        in_specs=[ANY],
        out_specs=ANY,
        out_shape=jax.ShapeDtypeStruct(land.shape, land.dtype),
        input_output_aliases={0: 0},
        scratch_shapes=[pltpu.SemaphoreType.DMA((N_CHIPS - 1,)), pltpu.SemaphoreType.DMA((N_CHIPS - 1,))],
        compiler_params=pltpu.CompilerParams(has_side_effects=True),
    )(land)


N_PARTS = 2 * (N_CHIPS - 1)


def _scatter_copies(lo_ref, g_ref, land_lo_ref, land_f_ref, send_sems, recv_sems, starting):
    rh = g_ref.shape[1] // 2
    x, y, c = _place()
    copies = []
    for k in range(1, N_CHIPS):
        cx, cy = _rel_chip(x, y, k)
        for i in range(2):
            part = 2 * (k - 1) + (c if starting else i)
            copies.append(pltpu.make_async_remote_copy(
                src_ref=lo_ref.at[2 * cx + cy, pl.ds(i * rh, rh)], dst_ref=land_lo_ref.at[part],
                send_sem=send_sems.at[2 * (k - 1) + i], recv_sem=recv_sems.at[part], device_id=(cx, cy, i), device_id_type=MESH))
    copies.append(pltpu.make_async_remote_copy(
        src_ref=g_ref.at[2 * x + y, _half(1 - c, rh)], dst_ref=land_f_ref, send_sem=send_sems.at[N_PARTS], recv_sem=recv_sems.at[N_PARTS],
        device_id=(x, y, 1 - c), device_id_type=MESH))
    return copies


def _scatter_start(g_lo, g, *, name):
    n, r, w = g.shape
    rh = r // 2

    def body(lo_ref, g_ref, land_lo_ref, land_f_ref, send_sems, recv_sems, lo_thru, g_thru, land_lo_thru, land_f_thru, token):
        for cp in _scatter_copies(lo_ref, g_ref, land_lo_ref, land_f_ref, send_sems, recv_sems, True):
            cp.start()
        token[...] = jnp.zeros_like(token)

    return pl.pallas_call(
        body,
        name=name,
        out_shape=(pltpu.SemaphoreType.DMA((N_PARTS + 1,)), pltpu.SemaphoreType.DMA((N_PARTS + 1,)), pltpu.HBM(g_lo.shape, g_lo.dtype),
                   pltpu.HBM(g.shape, g.dtype), pltpu.HBM((N_PARTS, rh, w), g_lo.dtype), pltpu.HBM((rh, w), g.dtype),
                   jax.ShapeDtypeStruct((8, 128), F32)),
        in_specs=(HBM, HBM, HBM, HBM),
        out_specs=(SEM, SEM, HBM, HBM, HBM, HBM, pl.BlockSpec(memory_space=pltpu.VMEM)),
        input_output_aliases={0: 2, 1: 3, 2: 4, 3: 5},
        compiler_params=pltpu.CompilerParams(has_side_effects=EFFECT),
    )(_in_hbm(g_lo), _in_hbm(g), _in_hbm(lax.empty((N_PARTS, rh, w), g_lo.dtype)), _in_hbm(lax.empty((rh, w), g.dtype)))


def _scatter_wait(send_sems, recv_sems, g_lo, g, land_lo, land_f, after, *, name):
    def body(lo_ref, g_ref, land_lo_ref, land_f_ref, send_sems, recv_sems, after_ref, o0, o1, o2, o3):
        for cp in _scatter_copies(lo_ref, g_ref, land_lo_ref, land_f_ref, send_sems, recv_sems, False):
            cp.wait_send()
            cp.wait_recv()

    arrays = (g_lo, g, land_lo, land_f)
    return pl.pallas_call(
        body,
        name=name,
        out_shape=tuple(pltpu.HBM(a.shape, a.dtype) for a in arrays),
        in_specs=(HBM, HBM, HBM, HBM, SEM, SEM, ANY),
        out_specs=(HBM, HBM, HBM, HBM),
        input_output_aliases={0: 0, 1: 1, 2: 2, 3: 3},
        compiler_params=pltpu.CompilerParams(has_side_effects=EFFECT),
    )(*arrays, send_sems, recv_sems, after)


def _scatter_sum(g, land_lo, land_f, where, *, name):
    n, r, w = g.shape
    rh = r // 2
    tr = _pick(rh, (256, 160, 80))
    nt = rh // tr

    def body(where_ref, g_ref, f_ref, lo_ref, o_ref):
        acc = g_ref[0] + f_ref[...]
        for part in range(N_PARTS):
            acc = acc + lo_ref[part].astype(F32)
        o_ref[...] = acc

    return pl.pallas_call(
        body,
        name=name,
        grid_spec=pltpu.PrefetchScalarGridSpec(
            num_scalar_prefetch=1,
            grid=(nt,),
            in_specs=[pl.BlockSpec((1, tr, w), lambda i, wh: (wh[1], wh[0] * nt + i, 0)),
                      pl.BlockSpec((tr, w), lambda i, wh: (i, 0)),
                      pl.BlockSpec((N_PARTS, tr, w), lambda i, wh: (0, i, 0))],
            out_specs=pl.BlockSpec((tr, w), lambda i, wh: (wh[0] * nt + i, 0)),
        ),
        out_shape=jax.ShapeDtypeStruct((r, w), F32),
        compiler_params=_cp(("parallel",)),
    )(where, g, land_f, land_lo)


def _swap_all(shards, *, name):
    n = len(shards)

    def body(*refs):
        ins, outs = refs[:n], refs[n : 2 * n]
        send_sems, recv_sems = refs[2 * n :]
        x, y, c = _place()
        copies = []
        for i, (e_ref, o_ref) in enumerate(zip(ins, outs)):
            rows = _half(c, e_ref.shape[0] // 2)
            copies.append(pltpu.make_async_remote_copy(src_ref=e_ref.at[rows], dst_ref=o_ref.at[rows], send_sem=send_sems.at[i],
                                                       recv_sem=recv_sems.at[i], device_id=(x, y, 1 - c), device_id_type=MESH))
        for cp in copies:
            cp.start()
        for cp in copies:
            cp.wait()

    return pl.pallas_call(
        body,
        name=name,
        in_specs=[ANY] * n,
        out_specs=[ANY] * n,
        out_shape=[jax.ShapeDtypeStruct(e.shape, e.dtype) for e in shards],
        input_output_aliases={i: i for i in range(n)},
        scratch_shapes=[pltpu.SemaphoreType.DMA((n,)), pltpu.SemaphoreType.DMA((n,))],
        compiler_params=pltpu.CompilerParams(has_side_effects=True),
    )(*shards)


def _sum_small(small, after):
    n_dev = 8

    def body(s_ref, after_ref, o_ref, all_ref, send_sems, recv_sems):
        x, y, c = _place()
        me = 4 * x + 2 * y + c
        all_ref[me] = s_ref[...]
        copies = []
        for k in range(1, n_dev):
            cx, cy = _rel_chip(x, y, k >> 1)
            cc = 1 - c if k & 1 else c
            copies.append(pltpu.make_async_remote_copy(
                src_ref=s_ref, dst_ref=all_ref.at[me], send_sem=send_sems.at[k - 1], recv_sem=recv_sems.at[k - 1],
                device_id=(cx, cy, cc), device_id_type=MESH))
        for cp in copies:
            cp.start()
        for cp in copies:
            cp.wait()
        acc = all_ref[0]
        for a in range(1, n_dev):
            acc = acc + all_ref[a]
        o_ref[...] = acc

    vm = pl.BlockSpec(memory_space=pltpu.VMEM)
    return pl.pallas_call(
        body,
        name="sum_small",
        in_specs=[vm, ANY],
        out_specs=vm,
        out_shape=jax.ShapeDtypeStruct(small.shape, F32),
        scratch_shapes=[pltpu.VMEM((n_dev,) + small.shape, F32), pltpu.SemaphoreType.DMA((n_dev - 1,)), pltpu.SemaphoreType.DMA((n_dev - 1,))],
        compiler_params=pltpu.CompilerParams(has_side_effects=True),
    )(small, after)


MATS = {"w_in": (776, True), "w_out": (256, False), "w_xq": (256, False), "w_xkv": (512, True), "w_xo": (256, False),
        "w_up": (1024, True), "w_down": (1024, False)}
GATHER_FIRST = ("w_in",)
GATHER_REST = ("w_out", "w_xq", "w_xkv", "w_xo", "w_up", "w_down")
GRAD_GROUPS = (("w_up", "w_down"), ("w_out", "w_xq", "w_xkv", "w_xo"), ("w_in",))


def _group_rows(names):
    n = sum(MATS[name][0] for name in names)
    return n + (-n) % 32


def _pack(pieces, rows):
    p = jnp.concatenate(pieces, axis=0) if len(pieces) > 1 else pieces[0]
    return jnp.pad(p, ((0, rows - p.shape[0]), (0, 0))) if rows > p.shape[0] else p


SMALL = (
    ("mix_norm", 1024), ("conv_norm", 512), ("b_af", 256), ("b_ab", 256), ("gla_norm", 128), ("xa_norm", 1024), ("mem_norm", 1024),
    ("mlp_norm", 1024), ("final_norm", 1024), ("conv_w", 1536), ("w_af", 4096), ("w_ab", 4096), ("loss", 128),
)


def kernel(x, mem, mix_norm, w_in, conv_w, conv_norm, w_af, b_af, w_ab, b_ab, gla_norm, w_out, xa_norm, mem_norm, w_xq, w_xkv, w_xo, mlp_norm, w_up, w_down, final_norm, loss_target, m_mix_norm, m_w_in, m_conv_w, m_conv_norm, m_w_af, m_b_af, m_w_ab, m_b_ab, m_gla_norm, m_w_out, m_xa_norm, m_mem_norm, m_w_xq, m_w_xkv, m_w_xo, m_mlp_norm, m_w_up, m_w_down, m_final_norm, v_mix_norm, v_w_in, v_conv_w, v_conv_norm, v_w_af, v_b_af, v_w_ab, v_b_ab, v_gla_norm, v_w_out, v_xa_norm, v_mem_norm, v_w_xq, v_w_xkv, v_w_xo, v_mlp_norm, v_w_up, v_w_down, v_final_norm):
    given = dict(locals())
    xi, yi, ci = _place()
    chip = 2 * xi + yi
    where = jnp.stack([ci, chip]).astype(jnp.int32)

    lo = {name: (given[name][0].T if MATS[name][1] else given[name][0]).astype(_CD) for name in MATS}
    pack_rest = _pack([lo[name] for name in GATHER_REST], _group_rows(GATHER_REST))
    pack_first = _pack([lo[name] for name in GATHER_FIRST], _group_rows(GATHER_FIRST))
    xs, mems, tgt = x[0], mem[0], loss_target[0]
    behind = lambda gain, token: gain + token[0, 0]

    def placed(shard, full_shape, col):
        return lax.dynamic_update_slice(jnp.zeros(full_shape, F32), shard, (0, col)).reshape(-1, 128)

    sw = jnp.concatenate([
        placed(conv_w[0], (CONV_K, CONV_WIDTH), 128 * chip),
        placed(w_af[0], (GLA_LOWRANK, GLA_K_TOTAL), 64 * chip),
        placed(w_ab[0], (GLA_LOWRANK, GLA_K_TOTAL), 64 * chip),
    ], axis=0)
    sw = jnp.pad(sw, ((0, SMALL_ROWS - sw.shape[0]), (0, 0))) * (ci == 0).astype(F32)
    sw = _sum_small(sw, mix_norm)

    first_send, first_recv, pack_first, land_first, first_token = _gather_start(pack_first, sw, name="gather_first_start")
    rest_send, rest_recv, pack_rest, land_rest, rest_token = _gather_start(pack_rest, first_token, name="gather_rest_start")
    h1 = _rms_fwd(xs, behind(mix_norm, rest_token), name="norm_mix")
    pack_first, land_first = _gather_wait(first_send, first_recv, pack_first, land_first, h1, name="gather_first_wait")
    got_first = _gather_spread(land_first, name="gather_first_spread")

    def whole(got, off, rows):
        return got[:, off : off + rows].reshape(N_CHIPS * rows, D_MODEL)

    w_in_t = whole(got_first, 0, MATS["w_in"][0])
    w_za = jnp.concatenate([w_in_t[0:1536], w_in_t[2560:3072]], axis=0)
    w_zb = jnp.concatenate([w_in_t[1536:2560], w_in_t[3072:W_IN_COLS], jnp.zeros((ZB_COLS - 1056, D_MODEL), _CD)], axis=0)
    conv_w_full = sw[0:12].reshape(CONV_K, CONV_WIDTH)
    w_af_full = sw[12:44].reshape(GLA_LOWRANK, GLA_K_TOTAL)
    w_ab_full = sw[44:76].reshape(GLA_LOWRANK, GLA_K_TOTAL)
    waf_p = jnp.pad(w_af_full, ((0, 128 - GLA_LOWRANK), (0, 0))).astype(_CD)
    wab_p = jnp.pad(w_ab_full, ((GLA_LOWRANK, 128 - 2 * GLA_LOWRANK), (0, 0))).astype(_CD)

    z_a, z_b = _mm_two(h1, w_za, w_zb, name="proj_in")
    b_f, b_b = _gate_fwd(z_b, waf_p, wab_p, b_af, b_ab, name="gates")
    o_f, st_f, o_b, st_b = _gla_fwd(z_b, b_f, b_b, name="gla_scan")
    y = _mix_fwd(z_a, o_f, o_b, conv_w_full, conv_norm, gla_norm, name="mix_out")
    pack_rest, land_rest = _gather_wait(rest_send, rest_recv, pack_rest, land_rest, y, name="gather_rest_wait")
    gathered = _gather_spread(land_rest, name="gather_rest_spread")
    wt, off = {}, 0
    for name in GATHER_REST:
        wt[name] = whole(gathered, off, MATS[name][0])
        off += MATS[name][0]
    x1, hx = _mm_rows(y, wt["w_out"], mode="nn", name="proj_out", rows=(xs,), vecs=(xa_norm,), out_rows=(F32, _CD),
                      epilogue=_ep_residual_norm, tm=1024)
    hmem = _rms_fwd(mems, mem_norm, name="norm_mem")
    kv = _mm(hmem, wt["w_xkv"], mode="nt", name="proj_xkv", out_dtypes=(_CD,))
    qx, ox, hm, x2 = _xattn_block(hx, wt["w_xq"], kv, wt["w_xo"], x1, mlp_norm, name="xattn_block")
    act, relu_u = _mm(hm, wt["w_up"], mode="nt", name="mlp_up", out_dtypes=(_CD, _CD), tm=2048,
                      epilogue=lambda acc: (jnp.square(jnp.maximum(acc, 0.0)), jnp.maximum(acc, 0.0)))
    dx3, dx3_lo, loss_part, g_final_norm = _mm_rows(
        act, wt["w_down"], mode="nn", name="mlp_down", rows=(x2, tgt), vecs=(final_norm.reshape(1, D_MODEL),),
        out_rows=(F32, _CD), out_vecs=(128, D_MODEL), epilogue=_ep_loss)

    grads_t = {}

    def start_group(names, tag):
        rows = _group_rows(names)
        g = jnp.stack([_pack([grads_t[name][a * MATS[name][0] : (a + 1) * MATS[name][0]] for name in names], rows) for a in range(N_CHIPS)])
        return _scatter_start(g.astype(_TD), g, name="grads_" + tag + "_start")

    def finish_group(state, after, tag):
        send_sems, recv_sems, g_lo, g, land_lo, land_f, _ = state
        g_lo, g, land_lo, land_f = _scatter_wait(send_sems, recv_sems, g_lo, g, land_lo, land_f, after, name="grads_" + tag + "_wait")
        return _scatter_sum(g, land_lo, land_f, where, name="grads_" + tag + "_sum")

    def new_packs(names):
        shape = (N_CHIPS, _group_rows(names), D_MODEL)
        return lax.empty(shape, F32), lax.empty(shape, _TD)

    def grad_into(packs, names, which, a, b, name):
        off = sum(MATS[other][0] for other in names[: names.index(which)])
        return _mm_tn_into(a, b, packs, rows=MATS[which][0], off=off, name=name)

    du = _mm(dx3_lo, wt["w_down"], mode="nt", name="mlp_down_dx", out_dtypes=(_CD,), extras=(relu_u,), tm=2048,
             epilogue=lambda acc, rr: (acc * (2.0 * rr.astype(F32)),))
    packs = new_packs(GRAD_GROUPS[0])
    packs = grad_into(packs, GRAD_GROUPS[0], "w_down", act, dx3_lo, "mlp_down_dw")
    packs = grad_into(packs, GRAD_GROUPS[0], "w_up", du, hm, "mlp_up_dw")
    mlp_state = _scatter_start(packs[1], packs[0], name="grads_mlp_start")
    dx2, dx2_lo, g_mlp_norm = _mm_rows(
        du, wt["w_up"], mode="nn", name="mlp_up_dx", rows=(x2, dx3), vecs=(behind(mlp_norm, mlp_state[-1]),),
        out_rows=(F32, _CD), out_vecs=(D_MODEL,), epilogue=_ep_norm_bwd)
    packs = new_packs(GRAD_GROUPS[1])
    packs = grad_into(packs, GRAD_GROUPS[1], "w_xo", ox, dx2_lo, "proj_xo_dw")
    dqx, dkv = _xattn_bwd(qx, kv, dx2_lo, wt["w_xo"], name="xattn_bwd")
    packs = grad_into(packs, GRAD_GROUPS[1], "w_xq", hx, dqx, "proj_xq_dw")
    dx1, dx1_lo, g_xa_norm = _mm_rows(
        dqx, wt["w_xq"], mode="nt", name="proj_xq_dx", rows=(x1, dx2), vecs=(xa_norm,),
        out_rows=(F32, _CD), out_vecs=(D_MODEL,), epilogue=_ep_norm_bwd, tm=1024)
    dkv_lo = dkv.astype(_CD)
    packs = grad_into(packs, GRAD_GROUPS[1], "w_xkv", dkv_lo, hmem, "proj_xkv_dw")
    dhmem = _mm(dkv_lo, wt["w_xkv"], mode="nn", name="proj_xkv_dx")
    g_mem_norm = _rms_gain_grad(mems, dhmem, name="norm_mem_bwd")
    packs = grad_into(packs, GRAD_GROUPS[1], "w_out", y, dx1_lo, "proj_out_dw")
    attn_state = _scatter_start(packs[1], packs[0], name="grads_attn_start")
    dz_a, do, g_conv_w, g_conv_norm, g_gla_norm = _mix_bwd(
        z_a, o_f, o_b, dx1_lo, wt["w_out"], conv_w_full, behind(conv_norm, attn_state[-1]), gla_norm, name="mix_out_bwd")
    dqkv_f, db_f, dqkv_b, db_b = _gla_bwd(z_b, b_f, b_b, do, st_f, st_b, name="gla_scan_bwd")
    dz_b, g_waf_p, g_wab_p, g_b_af, g_b_ab = _gate_bwd(z_b, waf_p, wab_p, b_af, b_ab, db_f, db_b, dqkv_f, dqkv_b, name="gates_bwd")
    g_za = _mm_tn(dz_a, h1, name="proj_in_a_dw")
    g_zb = _mm_tn(dz_b, h1, name="proj_in_b_dw")
    grads_t["w_in"] = jnp.concatenate([g_za[0:1536], g_zb[0:1024], g_za[1536:2048], g_zb[1024:1056]], axis=0)
    in_state = start_group(GRAD_GROUPS[2], "in")
    grad_x, g_mix_norm = _mm_rows(
        dz_a, w_za, mode="nn", name="proj_in_dx", more=((dz_b, w_zb),), rows=(xs, dx1), vecs=(behind(mix_norm, in_state[-1]),),
        out_rows=(F32,), out_vecs=(D_MODEL,), epilogue=_ep_norm_bwd)

    half_mlp = finish_group(mlp_state, grad_x, "mlp")
    half_attn = finish_group(attn_state, half_mlp, "attn")
    half_in = finish_group(in_state, half_attn, "in")
    shard_rows = {}
    for names, rows in zip(GRAD_GROUPS, _swap_all([half_mlp, half_attn, half_in], name="shards_to_sibling")):
        off = 0
        for name in names:
            shard_rows[name] = (rows, off)
            off += MATS[name][0]

    small_vals = dict(mix_norm=g_mix_norm, conv_norm=g_conv_norm, b_af=g_b_af, b_ab=g_b_ab, gla_norm=g_gla_norm, xa_norm=g_xa_norm,
                      mem_norm=g_mem_norm, mlp_norm=g_mlp_norm, final_norm=g_final_norm, conv_w=g_conv_w,
                      w_af=g_waf_p[0:GLA_LOWRANK], w_ab=g_wab_p[GLA_LOWRANK : 2 * GLA_LOWRANK], loss=loss_part)
    small = jnp.concatenate([small_vals[name].reshape(-1, 128) for name, _ in SMALL], axis=0)
    small = _sum_small(jnp.pad(small, ((0, SMALL_ROWS - small.shape[0]), (0, 0))), loss_part)
    g_small, off = {}, 0
    for name, n in SMALL:
        g_small[name] = small[off : off + n // 128]
        off += n // 128
    loss = g_small["loss"][0, 0]
    g_small["conv_w"] = lax.dynamic_slice(g_small["conv_w"].reshape(CONV_K, CONV_WIDTH), (0, 128 * chip), (CONV_K, 128))
    g_small["w_af"] = lax.dynamic_slice(g_small["w_af"].reshape(GLA_LOWRANK, GLA_K_TOTAL), (0, 64 * chip), (GLA_LOWRANK, 64))
    g_small["w_ab"] = lax.dynamic_slice(g_small["w_ab"].reshape(GLA_LOWRANK, GLA_K_TOTAL), (0, 64 * chip), (GLA_LOWRANK, 64))

    names = ["mix_norm", "w_in", "conv_w", "conv_norm", "w_af", "b_af", "w_ab", "b_ab", "gla_norm", "w_out", "xa_norm", "mem_norm",
             "w_xq", "w_xkv", "w_xo", "mlp_norm", "w_up", "w_down", "final_norm"]
    big_names = list(MATS)
    as2d = lambda a: a.reshape(1, -1) if a.ndim == 1 else a.reshape(a.shape[-2:])
    grads, deltas, new_m, new_v = {}, {}, {}, {}
    for name in big_names:
        rows, off = shard_rows[name]
        wmv = [as2d(given[name]), as2d(given["m_" + name]), as2d(given["v_" + name])]
        as_stored = name == "w_in"
        if as_stored:
            wmv = [a.T for a in wmv]
        res = _adamw(*wmv, rows, off, transposed=MATS[name][1] and not as_stored, name="adamw_" + name)
        grads[name], deltas[name], new_m[name], new_v[name] = [a.T for a in res] if as_stored else res
    small_names = [name for name in names if name not in big_names]
    groups = []
    for name in small_names:
        grads[name] = g_small[name].reshape(as2d(given[name]).shape)
        groups.append((as2d(given[name]), grads[name], as2d(given["m_" + name]), as2d(given["v_" + name])))
    for name, res in zip(small_names, _adamw_small(groups, name="adamw_small")):
        deltas[name], new_m[name], new_v[name] = res

    like = lambda name, a: a.reshape(given[name].shape)
    return (loss, grad_x[None], *[like(n, grads[n]) for n in names], *[like(n, deltas[n]) for n in names],
            *[like(n, new_m[n]) for n in names], *[like(n, new_v[n]) for n in names])
```

```python
import jax
import jax.numpy as jnp
from jax import lax
from jax.experimental import pallas as pl
from jax.experimental.pallas import tpu as pltpu

F32 = jnp.float32
BF16 = jnp.bfloat16
_CD = jnp.bfloat16
_TD = jnp.bfloat16

D_MODEL = 1024
N_MEM = 256
CONV_WIDTH = 512
CONV_GROUP = 64
CONV_K = 3
GLA_HEADS = 4
GLA_DK = 64
GLA_DV = 128
GLA_K_TOTAL = 256
GLA_V_TOTAL = 512
GLA_LOWRANK = 16
GLA_GATE_SCALE = 1.0 / 16.0
GLA_CHUNK = 64
XA_HEADS = 4
XA_HEAD_DIM = 256
D_FF = 4096
EPS = 1e-6
W_IN_COLS = 3104
ZA_COLS = 2048
ZB_COLS = 1152
LR_COL = 1024

ADAM_LR = 0.001
ADAM_B1 = 0.9
ADAM_B2 = 0.999
ADAM_EPS = 1e-08
ADAM_WD = 0.01
ADAM_STEP = 10

N_CHIPS = 4
SMALL_ROWS = 128

_TS = 512
_VMEM = 44 * 1024 * 1024
MESH = pl.DeviceIdType.MESH
ANY = pl.BlockSpec(memory_space=pl.ANY)


def _cp(sem=None, **kw):
    return pltpu.CompilerParams(dimension_semantics=sem, vmem_limit_bytes=_VMEM, **kw)


def _dot(a, b):
    return jnp.dot(a.astype(_CD), b.astype(_CD), preferred_element_type=F32)


def _dot_nt(a, b):
    return lax.dot_general(a.astype(_CD), b.astype(_CD), (((1,), (1,)), ((), ())), preferred_element_type=F32)


def _dot_tn(a, b):
    return lax.dot_general(a.astype(_CD), b.astype(_CD), (((0,), (0,)), ((), ())), preferred_element_type=F32)


def _dot_split(x, ones):
    hi = x.astype(BF16)
    r = x - hi.astype(F32)
    mid = r.astype(BF16)
    lo = (r - mid.astype(F32)).astype(BF16)
    d = lambda p: jnp.dot(p, ones, preferred_element_type=F32)
    return d(hi) + d(mid) + d(lo)


def _pick(n, cands=(1024, 640, 512, 256, 128)):
    for t in cands:
        if n % t == 0:
            return t
    return n


def _rows(s, light=False, times=2):
    return min(times * _TS if light else _TS, s)


def _sigmoid(v):
    e = jnp.exp(-jnp.abs(v))
    return jnp.where(v >= 0, 1.0 / (1.0 + e), e / (1.0 + e))


def _mm(a, b, *, mode, name, out_dtypes=(F32,), extras=(), epilogue=None, tm=None, tn=None, tk=None):
    m, k = a.shape
    placed = isinstance(b, tuple)
    if placed:
        b, b_off, tn = b
        assert mode == "nt" and b_off % tn == 0
        n = N_CHIPS * tn
    else:
        n = b.shape[1] if mode == "nn" else b.shape[0]
    tm = min(m, tm or 1024)
    tn = tn or _pick(n)
    tk = tk or _pick(k)
    nk = k // tk
    n_ex, n_out = len(extras), len(out_dtypes)

    def body(*refs):
        a_ref, b_ref = refs[:2]
        ex = refs[2 : 2 + n_ex]
        outs = refs[2 + n_ex : 2 + n_ex + n_out]
        part = _dot(a_ref[...], b_ref[...]) if mode == "nn" else _dot_nt(a_ref[...], b_ref[...])

        def finish(acc):
            res = epilogue(acc, *[e[...] for e in ex]) if epilogue else (acc,)
            for o, r in zip(outs, res):
                o[...] = r.astype(o.dtype)

        if nk == 1:
            finish(part)
        else:
            acc_ref = refs[-1]
            kk = pl.program_id(2)

            @pl.when(kk == 0)
            def _():
                acc_ref[...] = part

            @pl.when(kk > 0)
            def _():
                acc_ref[...] += part

            @pl.when(kk == nk - 1)
            def _():
                finish(acc_ref[...])

    if placed:
        b_spec = pl.BlockSpec((None, tn, tk), lambda i, j, kk: (j, b_off // tn, kk))
    else:
        b_spec = pl.BlockSpec((tk, tn), lambda i, j, kk: (kk, j)) if mode == "nn" else pl.BlockSpec((tn, tk), lambda i, j, kk: (j, kk))
    tile = pl.BlockSpec((tm, tn), lambda i, j, kk: (i, j))
    out = pl.pallas_call(
        body,
        name=name,
        grid=(m // tm, n // tn, nk),
        in_specs=[pl.BlockSpec((tm, tk), lambda i, j, kk: (i, kk)), b_spec] + [tile] * n_ex,
        out_specs=[tile] * n_out,
        out_shape=[jax.ShapeDtypeStruct((m, n), dt) for dt in out_dtypes],
        scratch_shapes=[pltpu.VMEM((tm, tn), F32)] if nk > 1 else [],
        compiler_params=_cp(("parallel", "parallel", "arbitrary")),
    )(a, b, *extras)
    return out[0] if n_out == 1 else out


def _mm_two(a, b1, b2, *, name, tm=512):
    m, k = a.shape
    tm = min(m, tm)

    def body(a_ref, b1_ref, b2_ref, o1_ref, o2_ref):
        av = a_ref[...]
        o1_ref[...] = _dot_nt(av, b1_ref[...])
        o2_ref[...] = _dot_nt(av, b2_ref[...])

    whole = lambda arr: pl.BlockSpec(arr.shape, lambda i: (0, 0))
    rows = lambda n: pl.BlockSpec((tm, n), lambda i: (i, 0))
    return pl.pallas_call(
        body,
        name=name,
        grid=(m // tm,),
        in_specs=[rows(k), whole(b1), whole(b2)],
        out_specs=[rows(b1.shape[0]), rows(b2.shape[0])],
        out_shape=[jax.ShapeDtypeStruct((m, b1.shape[0]), F32), jax.ShapeDtypeStruct((m, b2.shape[0]), F32)],
        compiler_params=_cp(("parallel",)),
    )(a, b1, b2)


def _mm_tn(a, b, *, name):
    s, m = a.shape
    n = b.shape[1]
    cap = max(128, (1 << 20) // n)
    tm = _pick(m, tuple(t for t in (512, 640, 384, 256, 128) if t <= max(cap, 128)))
    ts = min(s, 1 << (((1 << 22) // n).bit_length() - 1))
    ns = s // ts

    def body(a_ref, b_ref, o_ref):
        part = _dot_tn(a_ref[...], b_ref[...])
        if ns == 1:
            o_ref[...] = part
        else:
            ss = pl.program_id(1)

            @pl.when(ss == 0)
            def _():
                o_ref[...] = part

            @pl.when(ss > 0)
            def _():
                o_ref[...] += part

    return pl.pallas_call(
        body,
        name=name,
        grid=(m // tm, ns),
        in_specs=[pl.BlockSpec((ts, tm), lambda i, ss: (ss, i)), pl.BlockSpec((ts, n), lambda i, ss: (ss, 0))],
        out_specs=pl.BlockSpec((tm, n), lambda i, ss: (i, 0)),
        out_shape=jax.ShapeDtypeStruct((m, n), F32),
        compiler_params=_cp(("parallel", "arbitrary")),
    )(a, b)


def _mm_tn_into(a, b, packs, *, rows, off, name):
    s, m = a.shape
    n = b.shape[1]
    tm = 1024 if rows % 1024 == 0 and s >= 4096 else 512
    tr = min(tm, rows)
    per, chips = rows // tr, tm // tr
    ts = min(s, (1 << (((1 << 22) // n).bit_length() - 1)) * 512 // tm)
    ns = s // ts

    def body(a_ref, b_ref, f_in, lo_in, f_ref, lo_ref):
        part = _dot_tn(a_ref[...], b_ref[...])
        pieces = [part[c * tr : (c + 1) * tr] for c in range(chips)]
        if ns == 1:
            for c, p in enumerate(pieces):
                f_ref[c] = p
                lo_ref[c] = p.astype(lo_ref.dtype)
        else:
            ss = pl.program_id(1)

            @pl.when(ss == 0)
            def _():
                for c, p in enumerate(pieces):
                    f_ref[c] = p

            @pl.when(ss > 0)
            def _():
                for c, p in enumerate(pieces):
                    f_ref[c] += p

            @pl.when(ss == ns - 1)
            def _():
                lo_ref[...] = f_ref[...].astype(lo_ref.dtype)

    spec = pl.BlockSpec((chips, tr, n), lambda i, ss: (i // per, off // tr + i % per, 0))
    return pl.pallas_call(
        body,
        name=name,
        grid=(m // tm, ns),
        in_specs=[pl.BlockSpec((ts, tm), lambda i, ss: (ss, i)), pl.BlockSpec((ts, n), lambda i, ss: (ss, 0)), ANY, ANY],
        out_specs=[spec, spec],
        out_shape=[jax.ShapeDtypeStruct(p.shape, p.dtype) for p in packs],
        input_output_aliases={2: 0, 3: 1},
        compiler_params=_cp(("parallel", "arbitrary")),
    )(a, b, *packs)


def _mm_rows(a, b, *, mode, name, more=(), rows=(), vecs=(), out_rows=(), out_vecs=(), epilogue, tm=512):
    m, k = a.shape
    placed = isinstance(b, tuple)
    if placed:
        b, b_off, b_rows = b
        assert mode == "nn" and b_off % b_rows == 0 and k == N_CHIPS * b_rows
        n = b.shape[2]
        b_spec = pl.BlockSpec((N_CHIPS, b_rows, n), lambda i: (0, b_off // b_rows, 0))
    else:
        n = b.shape[1] if mode == "nn" else b.shape[0]
        b_spec = pl.BlockSpec(b.shape, lambda i: (0, 0))
    tm = min(m, tm)
    parts = 2 if tm % 256 == 0 else 1
    n_m, n_r, n_v, n_or, n_ov = 2 * len(more), len(rows), len(vecs), len(out_rows), len(out_vecs)

    def body(*refs):
        a_ref, b_ref = refs[:2]
        m_refs = refs[2 : 2 + n_m]
        rest = refs[2 + n_m :]
        r_refs = rest[:n_r]
        v_refs = rest[n_r : n_r + n_v]
        or_refs = rest[n_r + n_v : n_r + n_v + n_or]
        ov_refs = rest[n_r + n_v + n_or :]
        res_vecs = None
        bv = b_ref[...].reshape(k, n) if placed else b_ref[...]
        for p in range(parts):
            rs = slice(p * tm // parts, (p + 1) * tm // parts)
            acc = _dot(a_ref[rs, :], bv) if mode == "nn" else _dot_nt(a_ref[rs, :], bv)
            for a2_ref, b2_ref in zip(m_refs[0::2], m_refs[1::2]):
                acc = acc + _dot(a2_ref[rs, :], b2_ref[...])
            res_rows, part_vecs = epilogue(acc, [r[rs, :] for r in r_refs], [v[...] for v in v_refs])
            for o, r in zip(or_refs, res_rows):
                o[rs, :] = r.astype(o.dtype)
            res_vecs = part_vecs if res_vecs is None else [s + t for s, t in zip(res_vecs, part_vecs)]
        if n_ov:
            first = pl.program_id(0) == 0

            @pl.when(first)
            def _():
                for o, r in zip(ov_refs, res_vecs):
                    o[...] = r

            @pl.when(jnp.logical_not(first))
            def _():
                for o, r in zip(ov_refs, res_vecs):
                    o[...] += r

    tile = pl.BlockSpec((tm, n), lambda i: (i, 0))
    whole = lambda arr: pl.BlockSpec(arr.shape, lambda i: (0, 0))
    vec = lambda w: pl.BlockSpec((1, w), lambda i: (0, 0))
    out = pl.pallas_call(
        body,
        name=name,
        grid=(m // tm,),
        in_specs=[pl.BlockSpec((tm, k), lambda i: (i, 0)), b_spec]
        + [spec for a2, b2 in more for spec in (pl.BlockSpec((tm, a2.shape[1]), lambda i: (i, 0)), whole(b2))]
        + [tile] * n_r + [vec(v.shape[1]) for v in vecs],
        out_specs=[tile] * n_or + [vec(w) for w in out_vecs],
        out_shape=[jax.ShapeDtypeStruct((m, n), dt) for dt in out_rows] + [jax.ShapeDtypeStruct((1, w), F32) for w in out_vecs],
        compiler_params=_cp(("arbitrary",) if n_ov else ("parallel",)),
    )(a, b, *[x for pair in more for x in pair], *rows, *vecs)
    return out


def _ep_residual_norm(acc, rows, vecs):
    x = acc + rows[0]
    r = lax.rsqrt(jnp.mean(x * x, axis=-1, keepdims=True) + EPS)
    return [x, x * r * vecs[0]], []


def _ep_norm_bwd(acc, rows, vecs):
    dy = acc
    for extra in rows[2:]:
        dy = dy + extra
    x, dres = rows[0], rows[1]
    r = lax.rsqrt(jnp.mean(x * x, axis=-1, keepdims=True) + EPS)
    xh = x * r
    dxh = dy * vecs[0]
    dx = r * (dxh - xh * jnp.mean(dxh * xh, axis=-1, keepdims=True)) + dres
    return [dx, dx], [jnp.sum(dy * xh, axis=0, keepdims=True)]


def _ep_loss(acc, rows, vecs):
    x = acc + rows[0]
    d = x.shape[-1]
    r = lax.rsqrt(jnp.mean(x * x, axis=-1, keepdims=True) + EPS)
    xh = x * r
    err = xh * vecs[0] - rows[1]
    loss = jnp.zeros((1, 128), F32) + 0.5 * jnp.sum(jnp.mean(err * err, axis=-1, keepdims=True))
    dy = err * (1.0 / d)
    dxh = dy * vecs[0]
    dx = r * (dxh - xh * jnp.mean(dxh * xh, axis=-1, keepdims=True))
    return [dx, dx], [loss, jnp.sum(dy * xh, axis=0, keepdims=True)]


def _rms_fwd(x, g, *, name):
    s, d = x.shape
    ts = _rows(s, light=True, times=4)

    def body(x_ref, g_ref, o_ref):
        xf = x_ref[...]
        r = lax.rsqrt(jnp.mean(xf * xf, axis=-1, keepdims=True) + EPS)
        o_ref[...] = (xf * r * g_ref[...]).astype(o_ref.dtype)

    return pl.pallas_call(
        body,
        name=name,
        grid=(s // ts,),
        in_specs=[pl.BlockSpec((ts, d), lambda i: (i, 0)), pl.BlockSpec((1, d), lambda i: (0, 0))],
        out_specs=pl.BlockSpec((ts, d), lambda i: (i, 0)),
        out_shape=jax.ShapeDtypeStruct((s, d), _CD),
        compiler_params=_cp(("parallel",)),
    )(x, g)


def _rms_gain_grad(x, dy, *, name):
    s, d = x.shape
    ts = _rows(s)

    def body(x_ref, dy_ref, dg_ref):
        xf = x_ref[...]
        r = lax.rsqrt(jnp.mean(xf * xf, axis=-1, keepdims=True) + EPS)
        part = jnp.sum(dy_ref[...] * (xf * r), axis=0, keepdims=True)

        @pl.when(pl.program_id(0) == 0)
        def _():
            dg_ref[...] = part

        @pl.when(pl.program_id(0) > 0)
        def _():
            dg_ref[...] += part

    tile = pl.BlockSpec((ts, d), lambda i: (i, 0))
    return pl.pallas_call(
        body,
        name=name,
        grid=(s // ts,),
        in_specs=[tile, tile],
        out_specs=pl.BlockSpec((1, d), lambda i: (0, 0)),
        out_shape=jax.ShapeDtypeStruct((1, d), F32),
        compiler_params=_cp(("arbitrary",)),
    )(x, dy)


def _chunk_scan(v, row_in_chunk, suffix):
    t = v.shape[0]
    step = 1
    while step < GLA_CHUNK:
        if suffix:
            v = v + jnp.where(row_in_chunk < GLA_CHUNK - step, pltpu.roll(v, t - step, 0), 0.0)
        else:
            v = v + jnp.where(row_in_chunk >= step, pltpu.roll(v, step, 0), 0.0)
        step *= 2
    return v


def _gate_pre(lr, w_ref, b_ref):
    return _dot(lr, w_ref[...]) + b_ref[...]


def _gate_fwd(z, waf, wab, baf, bab, *, name):
    s = z.shape[0]
    ts = _rows(s, light=True, times=4)

    def body(lr_ref, waf_ref, wab_ref, baf_ref, bab_ref, bf_ref, bb_ref):
        lr = lr_ref[...]
        ric = lax.broadcasted_iota(jnp.int32, (ts, GLA_K_TOTAL), 0) & (GLA_CHUNK - 1)
        for w_ref, b_ref, o_ref, suffix in ((waf_ref, baf_ref, bf_ref, False), (wab_ref, bab_ref, bb_ref, True)):
            pre = _gate_pre(lr, w_ref, b_ref)
            la = (jnp.minimum(pre, 0.0) - jnp.log(1.0 + jnp.exp(-jnp.abs(pre)))) * GLA_GATE_SCALE
            o_ref[...] = _chunk_scan(la, ric, suffix)

    wspec = pl.BlockSpec((128, GLA_K_TOTAL), lambda i: (0, 0))
    bspec = pl.BlockSpec((1, GLA_K_TOTAL), lambda i: (0, 0))
    tile = pl.BlockSpec((ts, GLA_K_TOTAL), lambda i: (i, 0))
    return pl.pallas_call(
        body,
        name=name,
        grid=(s // ts,),
        in_specs=[pl.BlockSpec((ts, 128), lambda i: (i, LR_COL // 128)), wspec, wspec, bspec, bspec],
        out_specs=[tile, tile],
        out_shape=[jax.ShapeDtypeStruct((s, GLA_K_TOTAL), F32)] * 2,
        compiler_params=_cp(("parallel",)),
    )(z, waf, wab, baf, bab)


def _gate_bwd(z, waf, wab, baf, bab, dbf, dbb, dqkv_f, dqkv_b, *, name):
    s = z.shape[0]
    ts = _rows(s, light=True)

    def body(lr_ref, waf_ref, wab_ref, baf_ref, bab_ref, dbf_ref, dbb_ref, gf_ref, gb_ref, dzb_ref, dwf_ref, dwb_ref, dbaf_ref, dbab_ref):
        lr = lr_ref[...]
        ric = lax.broadcasted_iota(jnp.int32, (ts, GLA_K_TOTAL), 0) & (GLA_CHUNK - 1)
        first = pl.program_id(0) == 0
        dlr = None
        for w_ref, b_ref, db_ref, dw_ref, dbias_ref, suffix in (
            (waf_ref, baf_ref, dbf_ref, dwf_ref, dbaf_ref, True),
            (wab_ref, bab_ref, dbb_ref, dwb_ref, dbab_ref, False),
        ):
            pre = _gate_pre(lr, w_ref, b_ref)
            dla = _chunk_scan(db_ref[...], ric, suffix)
            dpre = dla * GLA_GATE_SCALE * _sigmoid(-pre)
            part = _dot_nt(dpre, w_ref[...])
            dlr = part if dlr is None else dlr + part
            dw = _dot_tn(lr, dpre)
            dbias = jnp.sum(dpre, axis=0, keepdims=True)

            @pl.when(first)
            def _():
                dw_ref[...] = dw
                dbias_ref[...] = dbias

            @pl.when(jnp.logical_not(first))
            def _():
                dw_ref[...] += dw
                dbias_ref[...] += dbias

        dqkv = gf_ref[...].astype(F32) + gb_ref[...].astype(F32)
        dzb_ref[...] = jnp.concatenate([dqkv, dlr], axis=1).astype(dzb_ref.dtype)

    wspec = pl.BlockSpec((128, GLA_K_TOTAL), lambda i: (0, 0))
    bspec = pl.BlockSpec((1, GLA_K_TOTAL), lambda i: (0, 0))
    tile = pl.BlockSpec((ts, GLA_K_TOTAL), lambda i: (i, 0))
    wide = pl.BlockSpec((ts, 2 * GLA_K_TOTAL + GLA_V_TOTAL), lambda i: (i, 0))
    return pl.pallas_call(
        body,
        name=name,
        grid=(s // ts,),
        in_specs=[pl.BlockSpec((ts, 128), lambda i: (i, LR_COL // 128)), wspec, wspec, bspec, bspec, tile, tile, wide, wide],
        out_specs=[pl.BlockSpec((ts, ZB_COLS), lambda i: (i, 0)), wspec, wspec, bspec, bspec],
        out_shape=[
            jax.ShapeDtypeStruct((s, ZB_COLS), _CD),
            jax.ShapeDtypeStruct((128, GLA_K_TOTAL), F32),
            jax.ShapeDtypeStruct((128, GLA_K_TOTAL), F32),
            jax.ShapeDtypeStruct((1, GLA_K_TOTAL), F32),
            jax.ShapeDtypeStruct((1, GLA_K_TOTAL), F32),
        ],
        compiler_params=_cp(("arbitrary",)),
    )(z, waf, wab, baf, bab, dbf, dbb, dqkv_f, dqkv_b)


def _gla_masks(rev):
    lane_head = lax.broadcasted_iota(jnp.int32, (1, GLA_K_TOTAL), 1) >> 6
    head_masks = [lane_head == h for h in range(GLA_HEADS)]
    t = lax.broadcasted_iota(jnp.int32, (GLA_HEADS * GLA_CHUNK, GLA_CHUNK), 0) & (GLA_CHUNK - 1)
    u = lax.broadcasted_iota(jnp.int32, (GLA_HEADS * GLA_CHUNK, GLA_CHUNK), 1)
    tri = (u > t) if rev else (u <= t)
    row = lax.broadcasted_iota(jnp.int32, (GLA_CHUNK, GLA_K_TOTAL), 0)
    total_row = row == (0 if rev else GLA_CHUNK - 1)
    return head_masks, tri, total_row


def _spread(a, head_masks):
    return jnp.concatenate([jnp.where(m, a, 0.0) for m in head_masks], axis=0)


def _stack(a):
    return jnp.concatenate([a[:, GLA_DV * h : GLA_DV * (h + 1)] for h in range(GLA_HEADS)], axis=0)


def _unstack(a):
    return jnp.concatenate([a[GLA_CHUNK * h : GLA_CHUNK * (h + 1)] for h in range(GLA_HEADS)], axis=1)


def _collect(a, head_masks):
    out = None
    for h, m in enumerate(head_masks):
        part = jnp.where(m, a[GLA_CHUNK * h : GLA_CHUNK * (h + 1)], 0.0)
        out = part if out is None else out + part
    return out


def _gla_chunk_terms(q_ref, k_ref, v_ref, b_ref, rows, head_masks, tri, total_row):
    q = q_ref[rows, :] * (GLA_DK**-0.5)
    k = k_ref[rows, :]
    v = v_ref[rows, :]
    b = b_ref[rows, :]
    eb = jnp.exp(b)
    enb = jnp.exp(-b)
    g = jnp.sum(jnp.where(total_row, b, 0.0), axis=0, keepdims=True)
    egb = jnp.exp(g - b)
    qt = q * eb
    kt = k * enb
    kh = k * egb
    q_heads = _spread(qt, head_masks)
    attn = jnp.where(tri, _dot_nt(q_heads, kt), 0.0)
    return v, eb, enb, egb, jnp.exp(g), qt, kt, kh, q_heads, attn


def _gla_specs(s, tb, rev_blocks):
    nb = s // tb
    rb = (lambda i: nb - 1 - i) if rev_blocks else (lambda i: i)
    q_spec = pl.BlockSpec((tb, GLA_K_TOTAL), lambda i: (rb(i), 0))
    k_spec = pl.BlockSpec((tb, GLA_K_TOTAL), lambda i: (rb(i), 1))
    v_spec = pl.BlockSpec((tb, GLA_V_TOTAL), lambda i: (rb(i), 1))
    b_spec = pl.BlockSpec((tb, GLA_K_TOTAL), lambda i: (rb(i), 0))
    o_spec = pl.BlockSpec((tb, GLA_V_TOTAL), lambda i: (rb(i), 0))
    st_spec = pl.BlockSpec((tb // GLA_CHUNK, GLA_DV, GLA_K_TOTAL), lambda i: (rb(i), 0, 0))
    return nb, q_spec, k_spec, v_spec, b_spec, o_spec, st_spec


def _gla_fwd_chunk(cidx, q_ref, k_ref, v_ref, b_ref, o_ref, sv_ref, st_ref, masks):
    head_masks, tri, total_row = masks
    rows = pl.ds(pl.multiple_of(cidx * GLA_CHUNK, GLA_CHUNK), GLA_CHUNK)
    v, _, _, _, eg, _, _, kh, q_heads, attn = _gla_chunk_terms(q_ref, k_ref, v_ref, b_ref, rows, head_masks, tri, total_row)
    o = jnp.concatenate(
        [_dot(attn[GLA_CHUNK * h : GLA_CHUNK * (h + 1)], v[:, GLA_DV * h : GLA_DV * (h + 1)]) for h in range(GLA_HEADS)], axis=1
    )
    st = st_ref[...]
    o_ref[rows, :] = o + _unstack(_dot_nt(q_heads, st))
    sv_ref[cidx] = st
    st_ref[...] = st * eg + _dot_tn(_stack(v), _spread(kh, head_masks))


def _gla_fwd(z, b_f, b_b, *, name):
    s = z.shape[0]
    tb = _rows(s)
    cpb = tb // GLA_CHUNK
    nb, qf, kf, vf, bf, of, sf = _gla_specs(s, tb, False)
    _, qr, kr, vr, br, orr, sr = _gla_specs(s, tb, True)

    def body(qf_ref, kf_ref, vf_ref, bf_ref, qr_ref, kr_ref, vr_ref, br_ref, of_ref, svf_ref, or_ref, svr_ref, stf_ref, str_ref):
        masks_f, masks_r = _gla_masks(False), _gla_masks(True)

        @pl.when(pl.program_id(0) == 0)
        def _():
            stf_ref[...] = jnp.zeros_like(stf_ref)
            str_ref[...] = jnp.zeros_like(str_ref)

        def chunk(ci, carry):
            _gla_fwd_chunk(ci, qf_ref, kf_ref, vf_ref, bf_ref, of_ref, svf_ref, stf_ref, masks_f)
            _gla_fwd_chunk(cpb - 1 - ci, qr_ref, kr_ref, vr_ref, br_ref, or_ref, svr_ref, str_ref, masks_r)
            return carry

        lax.fori_loop(0, cpb, chunk, 0)

    o_shape = jax.ShapeDtypeStruct((s, GLA_V_TOTAL), F32)
    st_shape = jax.ShapeDtypeStruct((s // GLA_CHUNK, GLA_DV, GLA_K_TOTAL), F32)
    return pl.pallas_call(
        body,
        name=name,
        grid=(nb,),
        in_specs=[qf, kf, vf, bf, qr, kr, vr, br],
        out_specs=[of, sf, orr, sr],
        out_shape=[o_shape, st_shape, o_shape, st_shape],
        scratch_shapes=[pltpu.VMEM((GLA_DV, GLA_K_TOTAL), F32)] * 2,
        compiler_params=_cp(("arbitrary",)),
    )(z, z, z, b_f, z, z, z, b_b)


def _gla_bwd_chunk(cidx, q_ref, k_ref, v_ref, b_ref, do_ref, sv_ref, dqkv_ref, db_ref, dst_ref, masks):
    head_masks, tri, total_row = masks
    rows = pl.ds(pl.multiple_of(cidx * GLA_CHUNK, GLA_CHUNK), GLA_CHUNK)
    v, eb, enb, egb, eg, qt, kt, kh, q_heads, attn = _gla_chunk_terms(q_ref, k_ref, v_ref, b_ref, rows, head_masks, tri, total_row)
    do_c = do_ref[rows, :]
    st = sv_ref[cidx]
    dst = dst_ref[...]
    do_s, v_s = _stack(do_c), _stack(v)
    hs = lambda a, h: a[GLA_CHUNK * h : GLA_CHUNK * (h + 1)]
    vs = lambda a, h: a[:, GLA_DV * h : GLA_DV * (h + 1)]
    dattn = jnp.concatenate([_dot_nt(vs(do_c, h), vs(v, h)) for h in range(GLA_HEADS)], axis=0)
    dattn = jnp.where(tri, dattn, 0.0)
    dv = jnp.concatenate([_dot_tn(hs(attn, h), vs(do_c, h)) for h in range(GLA_HEADS)], axis=1)
    dv = dv + _unstack(_dot_nt(_spread(kh, head_masks), dst))
    dqt = _collect(_dot(do_s, st) + _dot(dattn, kt), head_masks)
    dkt = _dot_tn(dattn, q_heads)
    dkh = _collect(_dot(v_s, dst), head_masks)
    dg = jnp.sum(dkh * kh, axis=0, keepdims=True) + jnp.sum(dst * st, axis=0, keepdims=True) * eg
    db = dqt * qt - dkt * kt - dkh * kh + jnp.where(total_row, dg, 0.0)
    dq = dqt * eb * (GLA_DK**-0.5)
    dk = dkt * enb + dkh * egb
    dqkv_ref[rows, :] = jnp.concatenate([dq, dk, dv], axis=1).astype(dqkv_ref.dtype)
    db_ref[rows, :] = db
    dst_ref[...] = dst * eg + _dot_tn(do_s, q_heads)


def _gla_bwd(z, b_f, b_b, do, st_f, st_b, *, name):
    s = z.shape[0]
    tb = _rows(s)
    cpb = tb // GLA_CHUNK
    wide = 2 * GLA_K_TOTAL + GLA_V_TOTAL
    nb, qf, kf, vf, bf, of, sf = _gla_specs(s, tb, True)
    _, qr, kr, vr, br, orr, sr = _gla_specs(s, tb, False)
    gf = pl.BlockSpec((tb, wide), lambda i: (nb - 1 - i, 0))
    gr = pl.BlockSpec((tb, wide), lambda i: (i, 0))

    def body(qf_ref, kf_ref, vf_ref, bf_ref, dof_ref, svf_ref, qr_ref, kr_ref, vr_ref, br_ref, dor_ref, svr_ref,
             gf_ref, dbf_ref, gr_ref, dbr_ref, dstf_ref, dstr_ref):
        masks_f, masks_r = _gla_masks(False), _gla_masks(True)

        @pl.when(pl.program_id(0) == 0)
        def _():
            dstf_ref[...] = jnp.zeros_like(dstf_ref)
            dstr_ref[...] = jnp.zeros_like(dstr_ref)

        def chunk(ci, carry):
            _gla_bwd_chunk(cpb - 1 - ci, qf_ref, kf_ref, vf_ref, bf_ref, dof_ref, svf_ref, gf_ref, dbf_ref, dstf_ref, masks_f)
            _gla_bwd_chunk(ci, qr_ref, kr_ref, vr_ref, br_ref, dor_ref, svr_ref, gr_ref, dbr_ref, dstr_ref, masks_r)
            return carry

        lax.fori_loop(0, cpb, chunk, 0)

    g_shape = jax.ShapeDtypeStruct((s, wide), _CD)
    db_shape = jax.ShapeDtypeStruct((s, GLA_K_TOTAL), F32)
    return pl.pallas_call(
        body,
        name=name,
        grid=(nb,),
        in_specs=[qf, kf, vf, bf, of, sf, qr, kr, vr, br, orr, sr],
        out_specs=[gf, bf, gr, br],
        out_shape=[g_shape, db_shape, g_shape, db_shape],
        scratch_shapes=[pltpu.VMEM((GLA_DV, GLA_K_TOTAL), F32)] * 2,
        compiler_params=_cp(("arbitrary",)),
    )(z, z, z, b_f, do, st_f, z, z, z, b_b, do, st_b)


HALO = 8


def _halo_specs(s, ts, width, col):
    last = s // HALO - 1
    per = ts // HALO
    prev = pl.BlockSpec((HALO, width), lambda i: (jnp.maximum(i * per - 1, 0), col))
    nxt = pl.BlockSpec((HALO, width), lambda i: (jnp.minimum((i + 1) * per, last), col))
    return prev, nxt


def _group_ones():
    group = jnp.arange(CONV_WIDTH, dtype=jnp.int32) // CONV_GROUP
    return (group[:, None] == group[None, :]).astype(BF16)


_ONES_SPEC = pl.BlockSpec((CONV_WIDTH, CONV_WIDTH), lambda i: (0, 0))


def _conv_terms(cc_ext, cu_ext, cw, valid):
    n = cc_ext.shape[0]
    hc = jnp.where(valid, cc_ext * cu_ext, 0.0)
    hc_prev = pltpu.roll(hc, 1, 0)
    hc_next = pltpu.roll(hc, n - 1, 0)
    conv = cw[0:1] * hc_prev + cw[1:2] * hc + cw[2:3] * hc_next
    return hc, hc_prev, hc_next, conv


def _ext(prev_ref, cur_ref, next_ref):
    return jnp.concatenate([prev_ref[...], cur_ref[...], next_ref[...]], axis=0)


def _valid_rows(ts, s):
    row = lax.broadcasted_iota(jnp.int32, (ts + 2 * HALO, 1), 0) + (pl.program_id(0) * ts - HALO)
    return (row >= 0) & (row < s)


def _head_norm(o, gn):
    out = []
    for h in range(GLA_HEADS):
        oh = o[:, GLA_DV * h : GLA_DV * (h + 1)]
        r = lax.rsqrt(jnp.mean(oh * oh, axis=-1, keepdims=True) + EPS)
        out.append((oh * r, r))
    return out


def _mix_fwd(z, o_f, o_b, conv_w, conv_norm, gla_norm, *, name):
    s = z.shape[0]
    ts = _rows(s, light=True)
    cprev, cnext = _halo_specs(s, ts, CONV_WIDTH, 1)
    uprev, unext = _halo_specs(s, ts, CONV_WIDTH, 2)

    def body(cb_ref, cc_ref, cu_ref, ccp_ref, ccn_ref, cup_ref, cun_ref, g_ref, of_ref, ob_ref, cw_ref, cn_ref, gn_ref, ones_ref, y_ref):
        valid = _valid_rows(ts, s)
        _, _, _, conv = _conv_terms(_ext(ccp_ref, cc_ref, ccn_ref), _ext(cup_ref, cu_ref, cun_ref), cw_ref[...], valid)
        yc = cb_ref[...] * conv[HALO : HALO + ts]
        ms = _dot_split(yc * yc, ones_ref[...]) * (1.0 / CONV_GROUP)
        y_conv = yc * lax.rsqrt(ms + EPS) * cn_ref[...]
        gate = g_ref[...]
        silu = gate * _sigmoid(gate)
        gn = gn_ref[...]
        y_gla = jnp.concatenate([oh * gn for oh, _ in _head_norm(of_ref[...] + ob_ref[...], gn)], axis=1) * silu
        y_ref[...] = jnp.concatenate([y_conv, y_gla], axis=1).astype(y_ref.dtype)

    col = lambda c, w=CONV_WIDTH: pl.BlockSpec((ts, w), lambda i: (i, c))
    return pl.pallas_call(
        body,
        name=name,
        grid=(s // ts,),
        in_specs=[col(0), col(1), col(2), cprev, cnext, uprev, unext, col(3), col(0), col(0),
                  pl.BlockSpec((CONV_K, CONV_WIDTH), lambda i: (0, 0)), pl.BlockSpec((1, CONV_WIDTH), lambda i: (0, 0)),
                  pl.BlockSpec((1, GLA_DV), lambda i: (0, 0)), _ONES_SPEC],
        out_specs=pl.BlockSpec((ts, D_MODEL), lambda i: (i, 0)),
        out_shape=jax.ShapeDtypeStruct((s, D_MODEL), _CD),
        compiler_params=_cp(("parallel",)),
    )(z, z, z, z, z, z, z, z, o_f, o_b, conv_w, conv_norm, gla_norm, _group_ones())


def _mix_bwd(z, o_f, o_b, dy, conv_w, conv_norm, gla_norm, *, name):
    s = z.shape[0]
    ts = _rows(s)
    halos = [_halo_specs(s, ts, CONV_WIDTH, c) for c in (0, 1, 2)]
    dprev, dnext = _halo_specs(s, ts, CONV_WIDTH, 0)

    def body(cb_ref, cc_ref, cu_ref, cbp_ref, cbn_ref, ccp_ref, ccn_ref, cup_ref, cun_ref, g_ref, of_ref, ob_ref,
             dyc_ref, dyg_ref, dyp_ref, dyn_ref, cw_ref, cn_ref, gn_ref, ones_ref, dza_ref, do_ref, dcw_ref, dcn_ref, dgn_ref):
        n = ts + 2 * HALO
        valid = _valid_rows(ts, s)
        cw = cw_ref[...]
        cn = cn_ref[...]
        ones = ones_ref[...]
        cb = _ext(cbp_ref, cb_ref, cbn_ref)
        cc = _ext(ccp_ref, cc_ref, ccn_ref)
        cu = _ext(cup_ref, cu_ref, cun_ref)
        dy = _ext(dyp_ref, dyc_ref, dyn_ref)
        hc, hc_prev, hc_next, conv = _conv_terms(cc, cu, cw, valid)
        yc = cb * conv
        r = lax.rsqrt(_dot_split(yc * yc, ones) * (1.0 / CONV_GROUP) + EPS)
        yh = yc * r
        dyh = dy * cn
        dyc = r * (dyh - yh * (_dot_split(dyh * yh, ones) * (1.0 / CONV_GROUP)))
        dconv = jnp.where(valid, dyc * cb, 0.0)
        dhc = cw[0:1] * pltpu.roll(dconv, n - 1, 0) + cw[1:2] * dconv + cw[2:3] * pltpu.roll(dconv, 1, 0)
        mid = lambda a: a[HALO : HALO + ts]
        dza_ref[:, 0 : 3 * CONV_WIDTH] = jnp.concatenate([mid(dyc * conv), mid(dhc * cu), mid(dhc * cc)], axis=1).astype(dza_ref.dtype)
        dconv_m = mid(dconv)
        colsum = lambda a: jnp.sum(a, axis=0, keepdims=True)
        dcw = jnp.concatenate([colsum(dconv_m * mid(hc_prev)), colsum(dconv_m * mid(hc)), colsum(dconv_m * mid(hc_next))], axis=0)
        dcn = colsum(mid(dy * yh))

        gate = g_ref[...]
        sg = _sigmoid(gate)
        silu = gate * sg
        gn = gn_ref[...]
        dyg = dyg_ref[...]
        don = dyg * silu
        heads = _head_norm(of_ref[...] + ob_ref[...], gn)
        on = jnp.concatenate([oh * gn for oh, _ in heads], axis=1)
        dza_ref[:, 3 * CONV_WIDTH : ZA_COLS] = (dyg * on * (sg * (1.0 + gate * (1.0 - sg)))).astype(dza_ref.dtype)
        dgn = jnp.zeros((1, GLA_DV), F32)
        dos = []
        for h, (oh, rh) in enumerate(heads):
            donh = don[:, GLA_DV * h : GLA_DV * (h + 1)]
            dgn = dgn + colsum(donh * oh)
            doh = donh * gn
            dos.append(rh * (doh - oh * jnp.mean(doh * oh, axis=-1, keepdims=True)))
        do_ref[...] = jnp.concatenate(dos, axis=1)

        first = pl.program_id(0) == 0

        @pl.when(first)
        def _():
            dcw_ref[...] = dcw
            dcn_ref[...] = dcn
            dgn_ref[...] = dgn

        @pl.when(jnp.logical_not(first))
        def _():
            dcw_ref[...] += dcw
            dcn_ref[...] += dcn
            dgn_ref[...] += dgn

    col = lambda c, w=CONV_WIDTH: pl.BlockSpec((ts, w), lambda i: (i, c))
    cw_spec = pl.BlockSpec((CONV_K, CONV_WIDTH), lambda i: (0, 0))
    cn_spec = pl.BlockSpec((1, CONV_WIDTH), lambda i: (0, 0))
    gn_spec = pl.BlockSpec((1, GLA_DV), lambda i: (0, 0))
    return pl.pallas_call(
        body,
        name=name,
        grid=(s // ts,),
        in_specs=[col(0), col(1), col(2), halos[0][0], halos[0][1], halos[1][0], halos[1][1], halos[2][0], halos[2][1],
                  col(3), col(0), col(0), col(0), col(1), dprev, dnext, cw_spec, cn_spec, gn_spec, _ONES_SPEC],
        out_specs=[pl.BlockSpec((ts, ZA_COLS), lambda i: (i, 0)), col(0), cw_spec, cn_spec, gn_spec],
        out_shape=[
            jax.ShapeDtypeStruct((s, ZA_COLS), _CD),
            jax.ShapeDtypeStruct((s, GLA_V_TOTAL), F32),
            jax.ShapeDtypeStruct((CONV_K, CONV_WIDTH), F32),
            jax.ShapeDtypeStruct((1, CONV_WIDTH), F32),
            jax.ShapeDtypeStruct((1, GLA_DV), F32),
        ],
        compiler_params=_cp(("arbitrary",)),
    )(z, z, z, z, z, z, z, z, z, z, o_f, o_b, dy, dy, dy, dy, conv_w, conv_norm, gla_norm, _group_ones())


def _xa_probs(q_ref, kv_ref, h):
    qh = q_ref[:, XA_HEAD_DIM * h : XA_HEAD_DIM * (h + 1)]
    kh = kv_ref[:, XA_HEAD_DIM * h : XA_HEAD_DIM * (h + 1)]
    vh = kv_ref[:, D_MODEL + XA_HEAD_DIM * h : D_MODEL + XA_HEAD_DIM * (h + 1)]
    sc = _dot_nt(qh, kh) * (XA_HEAD_DIM**-0.5)
    e = jnp.exp(sc - jnp.max(sc, axis=-1, keepdims=True))
    return qh, kh, vh, e / jnp.sum(e, axis=-1, keepdims=True)


def _xattn_block(hx, w_xq, kv, w_xo, x1, gain, *, name):
    s, d = hx.shape
    ts = _rows(s)

    def body(h_ref, wq_ref, kv_ref, wo_ref, x_ref, g_ref, q_out, o_out, hm_out, x_out):
        q = _dot(h_ref[...], wq_ref[...]).astype(_CD)
        q_out[...] = q
        heads = []
        for h in range(XA_HEADS):
            _, _, vh, p = _xa_probs(q, kv_ref, h)
            heads.append(_dot(p, vh))
        o = jnp.concatenate(heads, axis=1).astype(_CD)
        o_out[...] = o
        x = _dot(o, wo_ref[...]) + x_ref[...]
        r = lax.rsqrt(jnp.mean(x * x, axis=-1, keepdims=True) + EPS)
        x_out[...] = x
        hm_out[...] = (x * r * g_ref[...]).astype(hm_out.dtype)

    tile = pl.BlockSpec((ts, d), lambda i: (i, 0))
    whole = lambda arr: pl.BlockSpec(arr.shape, lambda i: (0, 0))
    lo = jax.ShapeDtypeStruct((s, d), _CD)
    return pl.pallas_call(
        body,
        name=name,
        grid=(s // ts,),
        in_specs=[tile, whole(w_xq), whole(kv), whole(w_xo), tile, whole(gain)],
        out_specs=[tile] * 4,
        out_shape=[lo, lo, lo, jax.ShapeDtypeStruct((s, d), F32)],
        compiler_params=_cp(("parallel",)),
    )(hx, w_xq, kv, w_xo, x1, gain)


def _xattn_bwd(qx, kv, dx, w_xo, *, name):
    s = qx.shape[0]
    ts = _rows(s, light=True)

    def body(q_ref, kv_ref, dx_ref, w_ref, dq_ref, dkv_ref):
        do = _dot_nt(dx_ref[...], w_ref[...]).astype(_CD)
        dqs, dks, dvs = [], [], []
        for h in range(XA_HEADS):
            qh, kh, vh, p = _xa_probs(q_ref, kv_ref, h)
            doh = do[:, XA_HEAD_DIM * h : XA_HEAD_DIM * (h + 1)]
            dp = _dot_nt(doh, vh)
            ds = p * (dp - jnp.sum(dp * p, axis=-1, keepdims=True)) * (XA_HEAD_DIM**-0.5)
            dqs.append(_dot(ds, kh))
            dks.append(_dot_tn(ds, qh))
            dvs.append(_dot_tn(p, doh))
        dq_ref[...] = jnp.concatenate(dqs, axis=1).astype(dq_ref.dtype)
        dkv = jnp.concatenate(dks + dvs, axis=1)

        @pl.when(pl.program_id(0) == 0)
        def _():
            dkv_ref[...] = dkv

        @pl.when(pl.program_id(0) > 0)
        def _():
            dkv_ref[...] += dkv

    tile = pl.BlockSpec((ts, D_MODEL), lambda i: (i, 0))
    kv_spec = pl.BlockSpec((N_MEM, 2 * D_MODEL), lambda i: (0, 0))
    return pl.pallas_call(
        body,
        name=name,
        grid=(s // ts,),
        in_specs=[tile, kv_spec, tile, pl.BlockSpec((D_MODEL, D_MODEL), lambda i: (0, 0))],
        out_specs=[tile, kv_spec],
        out_shape=[jax.ShapeDtypeStruct((s, D_MODEL), _CD), jax.ShapeDtypeStruct((N_MEM, 2 * D_MODEL), F32)],
        compiler_params=_cp(("arbitrary",)),
    )(qx, kv, dx, w_xo)


def _adamw_math(w, g, m, v):
    m = ADAM_B1 * m + (1.0 - ADAM_B1) * g
    v = ADAM_B2 * v + (1.0 - ADAM_B2) * (g * g)
    m_hat = m / (1.0 - ADAM_B1**ADAM_STEP)
    v_hat = v / (1.0 - ADAM_B2**ADAM_STEP)
    delta = -ADAM_LR * (m_hat / (jnp.sqrt(v_hat) + ADAM_EPS) + ADAM_WD * w)
    return delta, m, v


def _adamw(w, m, v, shard_rows, off, *, transposed, name):
    r, c = w.shape
    by_columns = r % 256 != 0
    tr = 512 if (c if by_columns else r) % 512 == 0 and off % 512 == 0 else 256
    if by_columns:
        assert not transposed and off == 0
        g_spec = tile = pl.BlockSpec((r, tr), lambda i: (0, i))
    else:
        g_spec = pl.BlockSpec((c, tr), lambda i: (off // c, i)) if transposed else pl.BlockSpec((tr, c), lambda i: (off // tr + i, 0))
        tile = pl.BlockSpec((tr, c), lambda i: (i, 0))

    def body(w_ref, g_ref, m_ref, v_ref, go_ref, d_ref, nm_ref, nv_ref):
        g = g_ref[...].T if transposed else g_ref[...]
        go_ref[...] = g
        d_ref[...], nm_ref[...], nv_ref[...] = _adamw_math(w_ref[...], g, m_ref[...], v_ref[...])

    return pl.pallas_call(
        body,
        name=name,
        grid=((c if by_columns else r) // tr,),
        in_specs=[tile, g_spec, tile, tile],
        out_specs=[tile] * 4,
        out_shape=[jax.ShapeDtypeStruct((r, c), F32)] * 4,
        compiler_params=_cp(("parallel",)),
    )(w, shard_rows, m, v)


def _adamw_small(groups, *, name):
    n = len(groups)

    def body(*refs):
        ins, outs = refs[: 4 * n], refs[4 * n :]
        for i in range(n):
            w_ref, g_ref, m_ref, v_ref = ins[4 * i : 4 * i + 4]
            outs[3 * i][...], outs[3 * i + 1][...], outs[3 * i + 2][...] = _adamw_math(w_ref[...], g_ref[...], m_ref[...], v_ref[...])

    flat = [a for grp in groups for a in grp]
    vm = pl.BlockSpec(memory_space=pltpu.VMEM)
    res = pl.pallas_call(
        body,
        name=name,
        in_specs=[vm] * (4 * n),
        out_specs=[vm] * (3 * n),
        out_shape=[jax.ShapeDtypeStruct(grp[0].shape, F32) for grp in groups for _ in range(3)],
        compiler_params=_cp(),
    )(*flat)
    return [tuple(res[3 * i : 3 * i + 3]) for i in range(n)]


def _place():
    return lax.axis_index("x"), lax.axis_index("y"), lax.axis_index("c")


def _rel_chip(x, y, k):
    return (1 - x if k & 2 else x), (1 - y if k & 1 else y)


def _half(c, rh):
    return pl.ds(pl.multiple_of(c * rh, 16), rh)


HBM = pl.BlockSpec(memory_space=pltpu.HBM)
SEM = pl.BlockSpec(memory_space=pltpu.SEMAPHORE)
EFFECT = pltpu.SideEffectType.DATAFLOW_SIDE_EFFECTING


def _in_hbm(a):
    return pltpu.with_memory_space_constraint(a, pltpu.HBM)


def _gather_copies(p_ref, land_ref, send_sems, recv_sems):
    rh = p_ref.shape[0] // 2
    x, y, c = _place()
    rows = _half(c, rh)
    copies = []
    for k in range(1, N_CHIPS):
        cx, cy = _rel_chip(x, y, k)
        copies.append(pltpu.make_async_remote_copy(
            src_ref=p_ref.at[rows], dst_ref=land_ref.at[2 * x + y, rows], send_sem=send_sems.at[k - 1], recv_sem=recv_sems.at[k - 1],
            device_id=(cx, cy, c), device_id_type=MESH))
    copies.append(pltpu.make_async_remote_copy(
        src_ref=p_ref, dst_ref=land_ref.at[2 * x + y], send_sem=send_sems.at[N_CHIPS - 1], recv_sem=recv_sems.at[N_CHIPS - 1],
        device_id=(x, y, 1 - c), device_id_type=MESH))
    return copies


def _gather_start(pack, after, *, name):
    r, w = pack.shape

    def body(p_ref, land_ref, after_ref, send_sems, recv_sems, p_thru, land_thru, token):
        for cp in _gather_copies(p_ref, land_ref, send_sems, recv_sems):
            cp.start()
        token[...] = jnp.zeros_like(token)

    return pl.pallas_call(
        body,
        name=name,
        out_shape=(pltpu.SemaphoreType.DMA((N_CHIPS,)), pltpu.SemaphoreType.DMA((N_CHIPS,)), pltpu.HBM((r, w), pack.dtype),
                   pltpu.HBM((N_CHIPS, r, w), pack.dtype), jax.ShapeDtypeStruct((8, 128), F32)),
        in_specs=(HBM, HBM, ANY),
        out_specs=(SEM, SEM, HBM, HBM, pl.BlockSpec(memory_space=pltpu.VMEM)),
        input_output_aliases={0: 2, 1: 3},
        compiler_params=pltpu.CompilerParams(has_side_effects=EFFECT),
    )(_in_hbm(pack), _in_hbm(lax.empty((N_CHIPS, r, w), pack.dtype)), after)


def _gather_wait(send_sems, recv_sems, pack, land, after, *, name):
    def body(p_ref, land_ref, send_sems, recv_sems, after_ref, p_out, land_out):
        for cp in _gather_copies(p_ref, land_ref, send_sems, recv_sems):
            cp.wait_send()
            cp.wait_recv()

    return pl.pallas_call(
        body,
        name=name,
        out_shape=(pltpu.HBM(pack.shape, pack.dtype), pltpu.HBM(land.shape, land.dtype)),
        in_specs=(HBM, HBM, SEM, SEM, ANY),
        out_specs=(HBM, HBM),
        input_output_aliases={0: 0, 1: 1},
        compiler_params=pltpu.CompilerParams(has_side_effects=EFFECT),
    )(pack, land, send_sems, recv_sems, after)


def _gather_spread(land, *, name):
    n, r, w = land.shape
    rh = r // 2

    def body(land_ref, o_ref, send_sems, recv_sems):
        x, y, c = _place()
        rows = _half(c, rh)
        copies = []
        for k in range(1, N_CHIPS):
            cx, cy = _rel_chip(x, y, k)
            copies.append(pltpu.make_async_remote_copy(
                src_ref=land_ref.at[2 * cx + cy, rows], dst_ref=o_ref.at[2 * cx + cy, rows], send_sem=send_sems.at[k - 1],
                recv_sem=recv_sems.at[k - 1], device_id=(x, y, 1 - c), device_id_type=MESH))
        for cp in copies:
            cp.start()
        for cp in copies:
            cp.wait()

    return pl.pallas_call(
        body,
        name=name,
        in_specs=[ANY],
        out_specs=ANY,
        out_shape=jax.ShapeDtypeStruct(land.shape, land.dtype),
        input_output_aliases={0: 0},
        scratch_shapes=[pltpu.SemaphoreType.DMA((N_CHIPS - 1,)), pltpu.SemaphoreType.DMA((N_CHIPS - 1,))],
        compiler_params=pltpu.CompilerParams(has_side_effects=True),
    )(land)


N_PARTS = 2 * (N_CHIPS - 1)


def _scatter_copies(lo_ref, g_ref, land_lo_ref, land_f_ref, send_sems, recv_sems, starting):
    rh = g_ref.shape[1] // 2
    x, y, c = _place()
    copies = []
    for k in range(1, N_CHIPS):
        cx, cy = _rel_chip(x, y, k)
        for i in range(2):
            part = 2 * (k - 1) + (c if starting else i)
            copies.append(pltpu.make_async_remote_copy(
                src_ref=lo_ref.at[2 * cx + cy, pl.ds(i * rh, rh)], dst_ref=land_lo_ref.at[part],
                send_sem=send_sems.at[2 * (k - 1) + i], recv_sem=recv_sems.at[part], device_id=(cx, cy, i), device_id_type=MESH))
    copies.append(pltpu.make_async_remote_copy(
        src_ref=g_ref.at[2 * x + y, _half(1 - c, rh)], dst_ref=land_f_ref, send_sem=send_sems.at[N_PARTS], recv_sem=recv_sems.at[N_PARTS],
        device_id=(x, y, 1 - c), device_id_type=MESH))
    return copies


def _scatter_start(g_lo, g, *, name):
    n, r, w = g.shape
    rh = r // 2

    def body(lo_ref, g_ref, land_lo_ref, land_f_ref, send_sems, recv_sems, lo_thru, g_thru, land_lo_thru, land_f_thru, token):
        for cp in _scatter_copies(lo_ref, g_ref, land_lo_ref, land_f_ref, send_sems, recv_sems, True):
            cp.start()
        token[...] = jnp.zeros_like(token)

    return pl.pallas_call(
        body,
        name=name,
        out_shape=(pltpu.SemaphoreType.DMA((N_PARTS + 1,)), pltpu.SemaphoreType.DMA((N_PARTS + 1,)), pltpu.HBM(g_lo.shape, g_lo.dtype),
                   pltpu.HBM(g.shape, g.dtype), pltpu.HBM((N_PARTS, rh, w), g_lo.dtype), pltpu.HBM((rh, w), g.dtype),
                   jax.ShapeDtypeStruct((8, 128), F32)),
        in_specs=(HBM, HBM, HBM, HBM),
        out_specs=(SEM, SEM, HBM, HBM, HBM, HBM, pl.BlockSpec(memory_space=pltpu.VMEM)),
        input_output_aliases={0: 2, 1: 3, 2: 4, 3: 5},
        compiler_params=pltpu.CompilerParams(has_side_effects=EFFECT),
    )(_in_hbm(g_lo), _in_hbm(g), _in_hbm(lax.empty((N_PARTS, rh, w), g_lo.dtype)), _in_hbm(lax.empty((rh, w), g.dtype)))


def _scatter_wait(send_sems, recv_sems, g_lo, g, land_lo, land_f, after, *, name):
    def body(lo_ref, g_ref, land_lo_ref, land_f_ref, send_sems, recv_sems, after_ref, o0, o1, o2, o3):
        for cp in _scatter_copies(lo_ref, g_ref, land_lo_ref, land_f_ref, send_sems, recv_sems, False):
            cp.wait_send()
            cp.wait_recv()

    arrays = (g_lo, g, land_lo, land_f)
    return pl.pallas_call(
        body,
        name=name,
        out_shape=tuple(pltpu.HBM(a.shape, a.dtype) for a in arrays),
        in_specs=(HBM, HBM, HBM, HBM, SEM, SEM, ANY),
        out_specs=(HBM, HBM, HBM, HBM),
        input_output_aliases={0: 0, 1: 1, 2: 2, 3: 3},
        compiler_params=pltpu.CompilerParams(has_side_effects=EFFECT),
    )(*arrays, send_sems, recv_sems, after)


def _scatter_sum(g, land_lo, land_f, where, *, name):
    n, r, w = g.shape
    rh = r // 2
    tr = _pick(rh, (256, 160, 80))
    nt = rh // tr

    def body(where_ref, g_ref, f_ref, lo_ref, o_ref):
        acc = g_ref[0] + f_ref[...]
        for part in range(N_PARTS):
            acc = acc + lo_ref[part].astype(F32)
        o_ref[...] = acc

    return pl.pallas_call(
        body,
        name=name,
        grid_spec=pltpu.PrefetchScalarGridSpec(
            num_scalar_prefetch=1,
            grid=(nt,),
            in_specs=[pl.BlockSpec((1, tr, w), lambda i, wh: (wh[1], wh[0] * nt + i, 0)),
                      pl.BlockSpec((tr, w), lambda i, wh: (i, 0)),
                      pl.BlockSpec((N_PARTS, tr, w), lambda i, wh: (0, i, 0))],
            out_specs=pl.BlockSpec((tr, w), lambda i, wh: (wh[0] * nt + i, 0)),
        ),
        out_shape=jax.ShapeDtypeStruct((r, w), F32),
        compiler_params=_cp(("parallel",)),
    )(where, g, land_f, land_lo)


def _swap_all(shards, *, name):
    n = len(shards)

    def body(*refs):
        ins, outs = refs[:n], refs[n : 2 * n]
        send_sems, recv_sems = refs[2 * n :]
        x, y, c = _place()
        copies = []
        for i, (e_ref, o_ref) in enumerate(zip(ins, outs)):
            rows = _half(c, e_ref.shape[0] // 2)
            copies.append(pltpu.make_async_remote_copy(src_ref=e_ref.at[rows], dst_ref=o_ref.at[rows], send_sem=send_sems.at[i],
                                                       recv_sem=recv_sems.at[i], device_id=(x, y, 1 - c), device_id_type=MESH))
        for cp in copies:
            cp.start()
        for cp in copies:
            cp.wait()

    return pl.pallas_call(
        body,
        name=name,
        in_specs=[ANY] * n,
        out_specs=[ANY] * n,
        out_shape=[jax.ShapeDtypeStruct(e.shape, e.dtype) for e in shards],
        input_output_aliases={i: i for i in range(n)},
        scratch_shapes=[pltpu.SemaphoreType.DMA((n,)), pltpu.SemaphoreType.DMA((n,))],
        compiler_params=pltpu.CompilerParams(has_side_effects=True),
    )(*shards)


def _sum_small(small, after):
    n_dev = 8

    def body(s_ref, after_ref, o_ref, all_ref, send_sems, recv_sems):
        x, y, c = _place()
        me = 4 * x + 2 * y + c
        all_ref[me] = s_ref[...]
        copies = []
        for k in range(1, n_dev):
            cx, cy = _rel_chip(x, y, k >> 1)
            cc = 1 - c if k & 1 else c
            copies.append(pltpu.make_async_remote_copy(
                src_ref=s_ref, dst_ref=all_ref.at[me], send_sem=send_sems.at[k - 1], recv_sem=recv_sems.at[k - 1],
                device_id=(cx, cy, cc), device_id_type=MESH))
        for cp in copies:
            cp.start()
        for cp in copies:
            cp.wait()
        acc = all_ref[0]
        for a in range(1, n_dev):
            acc = acc + all_ref[a]
        o_ref[...] = acc

    vm = pl.BlockSpec(memory_space=pltpu.VMEM)
    return pl.pallas_call(
        body,
        name="sum_small",
        in_specs=[vm, ANY],
        out_specs=vm,
        out_shape=jax.ShapeDtypeStruct(small.shape, F32),
        scratch_shapes=[pltpu.VMEM((n_dev,) + small.shape, F32), pltpu.SemaphoreType.DMA((n_dev - 1,)), pltpu.SemaphoreType.DMA((n_dev - 1,))],
        compiler_params=pltpu.CompilerParams(has_side_effects=True),
    )(small, after)


MATS = {"w_in": (776, True), "w_out": (256, False), "w_xq": (256, False), "w_xkv": (512, True), "w_xo": (256, False),
        "w_up": (1024, True), "w_down": (1024, False)}
GATHER_FIRST = ("w_in",)
GATHER_REST = ("w_up", "w_down", "w_out", "w_xq", "w_xkv", "w_xo")
IN_PLACE = ("w_up", "w_down")
GRAD_GROUPS = (("w_up", "w_down"), ("w_out", "w_xq", "w_xkv", "w_xo"), ("w_in",))


def _group_rows(names):
    n = sum(MATS[name][0] for name in names)
    return n + (-n) % 32


def _pack(pieces, rows):
    p = jnp.concatenate(pieces, axis=0) if len(pieces) > 1 else pieces[0]
    return jnp.pad(p, ((0, rows - p.shape[0]), (0, 0))) if rows > p.shape[0] else p


SMALL = (
    ("mix_norm", 1024), ("conv_norm", 512), ("b_af", 256), ("b_ab", 256), ("gla_norm", 128), ("xa_norm", 1024), ("mem_norm", 1024),
    ("mlp_norm", 1024), ("final_norm", 1024), ("conv_w", 1536), ("w_af", 4096), ("w_ab", 4096), ("loss", 128),
)


def kernel(x, mem, mix_norm, w_in, conv_w, conv_norm, w_af, b_af, w_ab, b_ab, gla_norm, w_out, xa_norm, mem_norm, w_xq, w_xkv, w_xo, mlp_norm, w_up, w_down, final_norm, loss_target, m_mix_norm, m_w_in, m_conv_w, m_conv_norm, m_w_af, m_b_af, m_w_ab, m_b_ab, m_gla_norm, m_w_out, m_xa_norm, m_mem_norm, m_w_xq, m_w_xkv, m_w_xo, m_mlp_norm, m_w_up, m_w_down, m_final_norm, v_mix_norm, v_w_in, v_conv_w, v_conv_norm, v_w_af, v_b_af, v_w_ab, v_b_ab, v_gla_norm, v_w_out, v_xa_norm, v_mem_norm, v_w_xq, v_w_xkv, v_w_xo, v_mlp_norm, v_w_up, v_w_down, v_final_norm):
    given = dict(locals())
    xi, yi, ci = _place()
    chip = 2 * xi + yi
    where = jnp.stack([ci, chip]).astype(jnp.int32)

    lo = {name: (given[name][0].T if MATS[name][1] else given[name][0]).astype(_CD) for name in MATS}
    pack_rest = _pack([lo[name] for name in GATHER_REST], _group_rows(GATHER_REST))
    pack_first = _pack([lo[name] for name in GATHER_FIRST], _group_rows(GATHER_FIRST))
    xs, mems, tgt = x[0], mem[0], loss_target[0]
    behind = lambda gain, token: gain + token[0, 0]

    def placed(shard, full_shape, col):
        return lax.dynamic_update_slice(jnp.zeros(full_shape, F32), shard, (0, col)).reshape(-1, 128)

    sw = jnp.concatenate([
        placed(conv_w[0], (CONV_K, CONV_WIDTH), 128 * chip),
        placed(w_af[0], (GLA_LOWRANK, GLA_K_TOTAL), 64 * chip),
        placed(w_ab[0], (GLA_LOWRANK, GLA_K_TOTAL), 64 * chip),
    ], axis=0)
    sw = jnp.pad(sw, ((0, SMALL_ROWS - sw.shape[0]), (0, 0))) * (ci == 0).astype(F32)
    sw = _sum_small(sw, mix_norm)

    first_send, first_recv, pack_first, land_first, first_token = _gather_start(pack_first, sw, name="gather_first_start")
    rest_send, rest_recv, pack_rest, land_rest, rest_token = _gather_start(pack_rest, first_token, name="gather_rest_start")
    h1 = _rms_fwd(xs, behind(mix_norm, rest_token), name="norm_mix")
    pack_first, land_first = _gather_wait(first_send, first_recv, pack_first, land_first, h1, name="gather_first_wait")
    got_first = _gather_spread(land_first, name="gather_first_spread")

    def whole(got, off, rows):
        return got[:, off : off + rows].reshape(N_CHIPS * rows, D_MODEL)

    w_in_t = whole(got_first, 0, MATS["w_in"][0])
    w_za = jnp.concatenate([w_in_t[0:1536], w_in_t[2560:3072]], axis=0)
    w_zb = jnp.concatenate([w_in_t[1536:2560], w_in_t[3072:W_IN_COLS], jnp.zeros((ZB_COLS - 1056, D_MODEL), _CD)], axis=0)
    conv_w_full = sw[0:12].reshape(CONV_K, CONV_WIDTH)
    w_af_full = sw[12:44].reshape(GLA_LOWRANK, GLA_K_TOTAL)
    w_ab_full = sw[44:76].reshape(GLA_LOWRANK, GLA_K_TOTAL)
    waf_p = jnp.pad(w_af_full, ((0, 128 - GLA_LOWRANK), (0, 0))).astype(_CD)
    wab_p = jnp.pad(w_ab_full, ((GLA_LOWRANK, 128 - 2 * GLA_LOWRANK), (0, 0))).astype(_CD)

    z_a, z_b = _mm_two(h1, w_za, w_zb, name="proj_in")
    b_f, b_b = _gate_fwd(z_b, waf_p, wab_p, b_af, b_ab, name="gates")
    o_f, st_f, o_b, st_b = _gla_fwd(z_b, b_f, b_b, name="gla_scan")
    y = _mix_fwd(z_a, o_f, o_b, conv_w_full, conv_norm, gla_norm, name="mix_out")
    pack_rest, land_rest = _gather_wait(rest_send, rest_recv, pack_rest, land_rest, y, name="gather_rest_wait")
    gathered = _gather_spread(land_rest, name="gather_rest_spread")
    wt, off = {}, 0
    for name in GATHER_REST:
        wt[name] = (gathered, off, MATS[name][0]) if name in IN_PLACE else whole(gathered, off, MATS[name][0])
        off += MATS[name][0]
    x1, hx = _mm_rows(y, wt["w_out"], mode="nn", name="proj_out", rows=(xs,), vecs=(xa_norm,), out_rows=(F32, _CD),
                      epilogue=_ep_residual_norm, tm=1024)
    hmem = _rms_fwd(mems, mem_norm, name="norm_mem")
    kv = _mm(hmem, wt["w_xkv"], mode="nt", name="proj_xkv", out_dtypes=(_CD,))
    qx, ox, hm, x2 = _xattn_block(hx, wt["w_xq"], kv, wt["w_xo"], x1, mlp_norm, name="xattn_block")
    act, relu_u = _mm(hm, wt["w_up"], mode="nt", name="mlp_up", out_dtypes=(_CD, _CD), tm=2048,
                      epilogue=lambda acc: (jnp.square(jnp.maximum(acc, 0.0)), jnp.maximum(acc, 0.0)))
    dx3, dx3_lo, loss_part, g_final_norm = _mm_rows(
        act, wt["w_down"], mode="nn", name="mlp_down", rows=(x2, tgt), vecs=(final_norm.reshape(1, D_MODEL),),
        out_rows=(F32, _CD), out_vecs=(128, D_MODEL), epilogue=_ep_loss)

    grads_t = {}

    def start_group(names, tag):
        rows = _group_rows(names)
        g = jnp.stack([_pack([grads_t[name][a * MATS[name][0] : (a + 1) * MATS[name][0]] for name in names], rows) for a in range(N_CHIPS)])
        return _scatter_start(g.astype(_TD), g, name="grads_" + tag + "_start")

    def finish_group(state, after, tag):
        send_sems, recv_sems, g_lo, g, land_lo, land_f, _ = state
        g_lo, g, land_lo, land_f = _scatter_wait(send_sems, recv_sems, g_lo, g, land_lo, land_f, after, name="grads_" + tag + "_wait")
        return _scatter_sum(g, land_lo, land_f, where, name="grads_" + tag + "_sum")

    def new_packs(names):
        shape = (N_CHIPS, _group_rows(names), D_MODEL)
        return lax.empty(shape, F32), lax.empty(shape, _TD)

    def grad_into(packs, names, which, a, b, name):
        off = sum(MATS[other][0] for other in names[: names.index(which)])
        return _mm_tn_into(a, b, packs, rows=MATS[which][0], off=off, name=name)

    du = _mm(dx3_lo, wt["w_down"], mode="nt", name="mlp_down_dx", out_dtypes=(_CD,), extras=(relu_u,), tm=2048,
             epilogue=lambda acc, rr: (acc * (2.0 * rr.astype(F32)),))
    packs = new_packs(GRAD_GROUPS[0])
    packs = grad_into(packs, GRAD_GROUPS[0], "w_down", act, dx3_lo, "mlp_down_dw")
    packs = grad_into(packs, GRAD_GROUPS[0], "w_up", du, hm, "mlp_up_dw")
    mlp_state = _scatter_start(packs[1], packs[0], name="grads_mlp_start")
    dx2, dx2_lo, g_mlp_norm = _mm_rows(
        du, wt["w_up"], mode="nn", name="mlp_up_dx", rows=(x2, dx3), vecs=(behind(mlp_norm, mlp_state[-1]),),
        out_rows=(F32, _CD), out_vecs=(D_MODEL,), epilogue=_ep_norm_bwd)
    packs = new_packs(GRAD_GROUPS[1])
    packs = grad_into(packs, GRAD_GROUPS[1], "w_xo", ox, dx2_lo, "proj_xo_dw")
    dqx, dkv = _xattn_bwd(qx, kv, dx2_lo, wt["w_xo"], name="xattn_bwd")
    packs = grad_into(packs, GRAD_GROUPS[1], "w_xq", hx, dqx, "proj_xq_dw")
    dx1, dx1_lo, g_xa_norm = _mm_rows(
        dqx, wt["w_xq"], mode="nt", name="proj_xq_dx", rows=(x1, dx2), vecs=(xa_norm,),
        out_rows=(F32, _CD), out_vecs=(D_MODEL,), epilogue=_ep_norm_bwd, tm=1024)
    dkv_lo = dkv.astype(_CD)
    packs = grad_into(packs, GRAD_GROUPS[1], "w_xkv", dkv_lo, hmem, "proj_xkv_dw")
    dhmem = _mm(dkv_lo, wt["w_xkv"], mode="nn", name="proj_xkv_dx")
    g_mem_norm = _rms_gain_grad(mems, dhmem, name="norm_mem_bwd")
    dy = _mm(dx1_lo, wt["w_out"], mode="nt", name="proj_out_dx")
    packs = grad_into(packs, GRAD_GROUPS[1], "w_out", y, dx1_lo, "proj_out_dw")
    attn_state = _scatter_start(packs[1], packs[0], name="grads_attn_start")
    dz_a, do, g_conv_w, g_conv_norm, g_gla_norm = _mix_bwd(z_a, o_f, o_b, dy, conv_w_full, behind(conv_norm, attn_state[-1]), gla_norm, name="mix_out_bwd")
    dqkv_f, db_f, dqkv_b, db_b = _gla_bwd(z_b, b_f, b_b, do, st_f, st_b, name="gla_scan_bwd")
    dz_b, g_waf_p, g_wab_p, g_b_af, g_b_ab = _gate_bwd(z_b, waf_p, wab_p, b_af, b_ab, db_f, db_b, dqkv_f, dqkv_b, name="gates_bwd")
    g_za = _mm_tn(dz_a, h1, name="proj_in_a_dw")
    g_zb = _mm_tn(dz_b, h1, name="proj_in_b_dw")
    grads_t["w_in"] = jnp.concatenate([g_za[0:1536], g_zb[0:1024], g_za[1536:2048], g_zb[1024:1056]], axis=0)
    in_state = start_group(GRAD_GROUPS[2], "in")
    grad_x, g_mix_norm = _mm_rows(
        dz_a, w_za, mode="nn", name="proj_in_dx", more=((dz_b, w_zb),), rows=(xs, dx1), vecs=(behind(mix_norm, in_state[-1]),),
        out_rows=(F32,), out_vecs=(D_MODEL,), epilogue=_ep_norm_bwd)

    half_mlp = finish_group(mlp_state, grad_x, "mlp")
    half_attn = finish_group(attn_state, half_mlp, "attn")
    half_in = finish_group(in_state, half_attn, "in")
    shard_rows = {}
    for names, rows in zip(GRAD_GROUPS, _swap_all([half_mlp, half_attn, half_in], name="shards_to_sibling")):
        off = 0
        for name in names:
            shard_rows[name] = (rows, off)
            off += MATS[name][0]

    small_vals = dict(mix_norm=g_mix_norm, conv_norm=g_conv_norm, b_af=g_b_af, b_ab=g_b_ab, gla_norm=g_gla_norm, xa_norm=g_xa_norm,
                      mem_norm=g_mem_norm, mlp_norm=g_mlp_norm, final_norm=g_final_norm, conv_w=g_conv_w,
                      w_af=g_waf_p[0:GLA_LOWRANK], w_ab=g_wab_p[GLA_LOWRANK : 2 * GLA_LOWRANK], loss=loss_part)
    small = jnp.concatenate([small_vals[name].reshape(-1, 128) for name, _ in SMALL], axis=0)
    small = _sum_small(jnp.pad(small, ((0, SMALL_ROWS - small.shape[0]), (0, 0))), loss_part)
    g_small, off = {}, 0
    for name, n in SMALL:
        g_small[name] = small[off : off + n // 128]
        off += n // 128
    loss = g_small["loss"][0, 0]
    g_small["conv_w"] = lax.dynamic_slice(g_small["conv_w"].reshape(CONV_K, CONV_WIDTH), (0, 128 * chip), (CONV_K, 128))
    g_small["w_af"] = lax.dynamic_slice(g_small["w_af"].reshape(GLA_LOWRANK, GLA_K_TOTAL), (0, 64 * chip), (GLA_LOWRANK, 64))
    g_small["w_ab"] = lax.dynamic_slice(g_small["w_ab"].reshape(GLA_LOWRANK, GLA_K_TOTAL), (0, 64 * chip), (GLA_LOWRANK, 64))

    names = ["mix_norm", "w_in", "conv_w", "conv_norm", "w_af", "b_af", "w_ab", "b_ab", "gla_norm", "w_out", "xa_norm", "mem_norm",
             "w_xq", "w_xkv", "w_xo", "mlp_norm", "w_up", "w_down", "final_norm"]
    big_names = list(MATS)
    as2d = lambda a: a.reshape(1, -1) if a.ndim == 1 else a.reshape(a.shape[-2:])
    grads, deltas, new_m, new_v = {}, {}, {}, {}
    for name in big_names:
        rows, off = shard_rows[name]
        wmv = [as2d(given[name]), as2d(given["m_" + name]), as2d(given["v_" + name])]
        as_stored = name == "w_in"
        if as_stored:
            wmv = [a.T for a in wmv]
        res = _adamw(*wmv, rows, off, transposed=MATS[name][1] and not as_stored, name="adamw_" + name)
        grads[name], deltas[name], new_m[name], new_v[name] = [a.T for a in res] if as_stored else res
    small_names = [name for name in names if name not in big_names]
    groups = []
    for name in small_names:
        grads[name] = g_small[name].reshape(as2d(given[name]).shape)
        groups.append((as2d(given[name]), grads[name], as2d(given["m_" + name]), as2d(given["v_" + name])))
    for name, res in zip(small_names, _adamw_small(groups, name="adamw_small")):
        deltas[name], new_m[name], new_v[name] = res

    like = lambda name, a: a.reshape(given[name].shape)
    return (loss, grad_x[None], *[like(n, grads[n]) for n in names], *[like(n, deltas[n]) for n in names],
            *[like(n, new_m[n]) for n in names], *[like(n, new_v[n]) for n in names])
```

```python
import jax
import jax.numpy as jnp
from jax import lax
from jax.experimental import pallas as pl
from jax.experimental.pallas import tpu as pltpu

F32 = jnp.float32
BF16 = jnp.bfloat16
_CD = jnp.bfloat16
_TD = jnp.bfloat16

D_MODEL = 1024
N_MEM = 256
CONV_WIDTH = 512
CONV_GROUP = 64
CONV_K = 3
GLA_HEADS = 4
GLA_DK = 64
GLA_DV = 128
GLA_K_TOTAL = 256
GLA_V_TOTAL = 512
GLA_LOWRANK = 16
GLA_GATE_SCALE = 1.0 / 16.0
GLA_CHUNK = 64
XA_HEADS = 4
XA_HEAD_DIM = 256
D_FF = 4096
EPS = 1e-6
W_IN_COLS = 3104
ZA_COLS = 2048
ZB_COLS = 1152
LR_COL = 1024

ADAM_LR = 0.001
ADAM_B1 = 0.9
ADAM_B2 = 0.999
ADAM_EPS = 1e-08
ADAM_WD = 0.01
ADAM_STEP = 10

N_CHIPS = 4
SMALL_ROWS = 128

_TS = 512
_VMEM = 44 * 1024 * 1024
_VMEM_STREAM = 63 * 1024 * 1024
MESH = pl.DeviceIdType.MESH
ANY = pl.BlockSpec(memory_space=pl.ANY)


def _cp(sem=None, vmem=_VMEM, **kw):
    return pltpu.CompilerParams(dimension_semantics=sem, vmem_limit_bytes=vmem, **kw)


def _dot(a, b):
    return jnp.dot(a.astype(_CD), b.astype(_CD), preferred_element_type=F32)


def _dot_nt(a, b):
    return lax.dot_general(a.astype(_CD), b.astype(_CD), (((1,), (1,)), ((), ())), preferred_element_type=F32)


def _dot_tn(a, b):
    return lax.dot_general(a.astype(_CD), b.astype(_CD), (((0,), (0,)), ((), ())), preferred_element_type=F32)


def _dot_split(x, ones):
    hi = x.astype(BF16)
    r = x - hi.astype(F32)
    mid = r.astype(BF16)
    lo = (r - mid.astype(F32)).astype(BF16)
    d = lambda p: jnp.dot(p, ones, preferred_element_type=F32)
    return d(hi) + d(mid) + d(lo)


def _pick(n, cands=(1024, 640, 512, 256, 128)):
    for t in cands:
        if n % t == 0:
            return t
    return n


def _rows(s, light=False, times=2):
    return min(times * _TS if light else _TS, s)


def _sigmoid(v):
    e = jnp.exp(-jnp.abs(v))
    return jnp.where(v >= 0, 1.0 / (1.0 + e), e / (1.0 + e))


def _mm(a, b, *, mode, name, out_dtypes=(F32,), extras=(), epilogue=None, tm=None, tn=None, tk=None):
    m, k = a.shape
    placed = isinstance(b, tuple)
    if placed:
        b, b_off, tn = b
        assert mode == "nt" and b_off % tn == 0
        n = N_CHIPS * tn
    else:
        n = b.shape[1] if mode == "nn" else b.shape[0]
    tm = min(m, tm or 1024)
    tn = tn or _pick(n)
    tk = tk or _pick(k)
    nk = k // tk
    n_ex, n_out = len(extras), len(out_dtypes)

    def body(*refs):
        a_ref, b_ref = refs[:2]
        ex = refs[2 : 2 + n_ex]
        outs = refs[2 + n_ex : 2 + n_ex + n_out]
        part = _dot(a_ref[...], b_ref[...]) if mode == "nn" else _dot_nt(a_ref[...], b_ref[...])

        def finish(acc):
            res = epilogue(acc, *[e[...] for e in ex]) if epilogue else (acc,)
            for o, r in zip(outs, res):
                o[...] = r.astype(o.dtype)

        if nk == 1:
            finish(part)
        else:
            acc_ref = refs[-1]
            kk = pl.program_id(2)

            @pl.when(kk == 0)
            def _():
                acc_ref[...] = part

            @pl.when(kk > 0)
            def _():
                acc_ref[...] += part

            @pl.when(kk == nk - 1)
            def _():
                finish(acc_ref[...])

    if placed:
        b_spec = pl.BlockSpec((None, tn, tk), lambda i, j, kk: (j, b_off // tn, kk))
    else:
        b_spec = pl.BlockSpec((tk, tn), lambda i, j, kk: (kk, j)) if mode == "nn" else pl.BlockSpec((tn, tk), lambda i, j, kk: (j, kk))
    tile = pl.BlockSpec((tm, tn), lambda i, j, kk: (i, j))
    out = pl.pallas_call(
        body,
        name=name,
        grid=(m // tm, n // tn, nk),
        in_specs=[pl.BlockSpec((tm, tk), lambda i, j, kk: (i, kk)), b_spec] + [tile] * n_ex,
        out_specs=[tile] * n_out,
        out_shape=[jax.ShapeDtypeStruct((m, n), dt) for dt in out_dtypes],
        scratch_shapes=[pltpu.VMEM((tm, tn), F32)] if nk > 1 else [],
        compiler_params=_cp(("parallel", "parallel", "arbitrary")),
    )(a, b, *extras)
    return out[0] if n_out == 1 else out


def _mm_two(a, b1, b2, *, name, tm=512):
    m, k = a.shape
    tm = min(m, tm)

    def body(a_ref, b1_ref, b2_ref, o1_ref, o2_ref):
        av = a_ref[...]
        o1_ref[...] = _dot_nt(av, b1_ref[...])
        o2_ref[...] = _dot_nt(av, b2_ref[...])

    whole = lambda arr: pl.BlockSpec(arr.shape, lambda i: (0, 0))
    rows = lambda n: pl.BlockSpec((tm, n), lambda i: (i, 0))
    return pl.pallas_call(
        body,
        name=name,
        grid=(m // tm,),
        in_specs=[rows(k), whole(b1), whole(b2)],
        out_specs=[rows(b1.shape[0]), rows(b2.shape[0])],
        out_shape=[jax.ShapeDtypeStruct((m, b1.shape[0]), F32), jax.ShapeDtypeStruct((m, b2.shape[0]), F32)],
        compiler_params=_cp(("parallel",)),
    )(a, b1, b2)


def _mm_tn(a, b, *, name):
    s, m = a.shape
    n = b.shape[1]
    cap = max(128, (1 << 20) // n)
    tm = _pick(m, tuple(t for t in (512, 640, 384, 256, 128) if t <= max(cap, 128)))
    ts = min(s, 1 << (((1 << 22) // n).bit_length() - 1))
    ns = s // ts

    def body(a_ref, b_ref, o_ref):
        part = _dot_tn(a_ref[...], b_ref[...])
        if ns == 1:
            o_ref[...] = part
        else:
            ss = pl.program_id(1)

            @pl.when(ss == 0)
            def _():
                o_ref[...] = part

            @pl.when(ss > 0)
            def _():
                o_ref[...] += part

    return pl.pallas_call(
        body,
        name=name,
        grid=(m // tm, ns),
        in_specs=[pl.BlockSpec((ts, tm), lambda i, ss: (ss, i)), pl.BlockSpec((ts, n), lambda i, ss: (ss, 0))],
        out_specs=pl.BlockSpec((tm, n), lambda i, ss: (i, 0)),
        out_shape=jax.ShapeDtypeStruct((m, n), F32),
        compiler_params=_cp(("parallel", "arbitrary")),
    )(a, b)


def _mm_tn_into(a, b, packs, *, rows, off, name):
    s, m = a.shape
    n = b.shape[1]
    tm = 1024 if rows % 1024 == 0 and s >= 4096 else 512
    tr = min(tm, rows)
    per, chips = rows // tr, tm // tr
    ts = min(s, (1 << (((1 << 22) // n).bit_length() - 1)) * 512 // tm)
    ns = s // ts

    def body(a_ref, b_ref, f_in, lo_in, f_ref, lo_ref):
        part = _dot_tn(a_ref[...], b_ref[...])
        pieces = [part[c * tr : (c + 1) * tr] for c in range(chips)]
        if ns == 1:
            for c, p in enumerate(pieces):
                f_ref[c] = p
                lo_ref[c] = p.astype(lo_ref.dtype)
        else:
            ss = pl.program_id(1)

            @pl.when(ss == 0)
            def _():
                for c, p in enumerate(pieces):
                    f_ref[c] = p

            @pl.when(ss > 0)
            def _():
                for c, p in enumerate(pieces):
                    f_ref[c] += p

            @pl.when(ss == ns - 1)
            def _():
                lo_ref[...] = f_ref[...].astype(lo_ref.dtype)

    spec = pl.BlockSpec((chips, tr, n), lambda i, ss: (i // per, off // tr + i % per, 0))
    return pl.pallas_call(
        body,
        name=name,
        grid=(m // tm, ns),
        in_specs=[pl.BlockSpec((ts, tm), lambda i, ss: (ss, i)), pl.BlockSpec((ts, n), lambda i, ss: (ss, 0)), ANY, ANY],
        out_specs=[spec, spec],
        out_shape=[jax.ShapeDtypeStruct(p.shape, p.dtype) for p in packs],
        input_output_aliases={2: 0, 3: 1},
        compiler_params=_cp(("parallel", "arbitrary")),
    )(a, b, *packs)


def _mm_rows(a, b, *, mode, name, more=(), rows=(), vecs=(), out_rows=(), out_vecs=(), epilogue, tm=512):
    m, k = a.shape
    placed = isinstance(b, tuple)
    if placed:
        b, b_off, b_rows = b
        assert mode == "nn" and b_off % b_rows == 0 and k == N_CHIPS * b_rows
        n = b.shape[2]
        b_spec = pl.BlockSpec((N_CHIPS, b_rows, n), lambda i: (0, b_off // b_rows, 0))
    else:
        n = b.shape[1] if mode == "nn" else b.shape[0]
        b_spec = pl.BlockSpec(b.shape, lambda i: (0, 0))
    tm = min(m, tm)
    parts = 2 if tm % 256 == 0 else 1
    n_m, n_r, n_v, n_or, n_ov = 2 * len(more), len(rows), len(vecs), len(out_rows), len(out_vecs)

    def body(*refs):
        a_ref, b_ref = refs[:2]
        m_refs = refs[2 : 2 + n_m]
        rest = refs[2 + n_m :]
        r_refs = rest[:n_r]
        v_refs = rest[n_r : n_r + n_v]
        or_refs = rest[n_r + n_v : n_r + n_v + n_or]
        ov_refs = rest[n_r + n_v + n_or :]
        res_vecs = None
        bv = b_ref[...].reshape(k, n) if placed else b_ref[...]
        for p in range(parts):
            rs = slice(p * tm // parts, (p + 1) * tm // parts)
            acc = _dot(a_ref[rs, :], bv) if mode == "nn" else _dot_nt(a_ref[rs, :], bv)
            for a2_ref, b2_ref in zip(m_refs[0::2], m_refs[1::2]):
                acc = acc + _dot(a2_ref[rs, :], b2_ref[...])
            res_rows, part_vecs = epilogue(acc, [r[rs, :] for r in r_refs], [v[...] for v in v_refs])
            for o, r in zip(or_refs, res_rows):
                o[rs, :] = r.astype(o.dtype)
            res_vecs = part_vecs if res_vecs is None else [s + t for s, t in zip(res_vecs, part_vecs)]
        if n_ov:
            first = pl.program_id(0) == 0

            @pl.when(first)
            def _():
                for o, r in zip(ov_refs, res_vecs):
                    o[...] = r

            @pl.when(jnp.logical_not(first))
            def _():
                for o, r in zip(ov_refs, res_vecs):
                    o[...] += r

    tile = pl.BlockSpec((tm, n), lambda i: (i, 0))
    whole = lambda arr: pl.BlockSpec(arr.shape, lambda i: (0, 0))
    vec = lambda w: pl.BlockSpec((1, w), lambda i: (0, 0))
    out = pl.pallas_call(
        body,
        name=name,
        grid=(m // tm,),
        in_specs=[pl.BlockSpec((tm, k), lambda i: (i, 0)), b_spec]
        + [spec for a2, b2 in more for spec in (pl.BlockSpec((tm, a2.shape[1]), lambda i: (i, 0)), whole(b2))]
        + [tile] * n_r + [vec(v.shape[1]) for v in vecs],
        out_specs=[tile] * n_or + [vec(w) for w in out_vecs],
        out_shape=[jax.ShapeDtypeStruct((m, n), dt) for dt in out_rows] + [jax.ShapeDtypeStruct((1, w), F32) for w in out_vecs],
        compiler_params=_cp(("arbitrary",) if n_ov else ("parallel",)),
    )(a, b, *[x for pair in more for x in pair], *rows, *vecs)
    return out


def _ep_residual_norm(acc, rows, vecs):
    x = acc + rows[0]
    r = lax.rsqrt(jnp.mean(x * x, axis=-1, keepdims=True) + EPS)
    return [x, x * r * vecs[0]], []


def _ep_norm_bwd(acc, rows, vecs):
    dy = acc
    for extra in rows[2:]:
        dy = dy + extra
    x, dres = rows[0], rows[1]
    r = lax.rsqrt(jnp.mean(x * x, axis=-1, keepdims=True) + EPS)
    xh = x * r
    dxh = dy * vecs[0]
    dx = r * (dxh - xh * jnp.mean(dxh * xh, axis=-1, keepdims=True)) + dres
    return [dx, dx], [jnp.sum(dy * xh, axis=0, keepdims=True)]


def _ep_loss(acc, rows, vecs):
    x = acc + rows[0]
    d = x.shape[-1]
    r = lax.rsqrt(jnp.mean(x * x, axis=-1, keepdims=True) + EPS)
    xh = x * r
    err = xh * vecs[0] - rows[1]
    loss = jnp.zeros((1, 128), F32) + 0.5 * jnp.sum(jnp.mean(err * err, axis=-1, keepdims=True))
    dy = err * (1.0 / d)
    dxh = dy * vecs[0]
    dx = r * (dxh - xh * jnp.mean(dxh * xh, axis=-1, keepdims=True))
    return [dx, dx], [loss, jnp.sum(dy * xh, axis=0, keepdims=True)]


def _rms_fwd(x, g, *, name):
    s, d = x.shape
    ts = _rows(s, light=True, times=4)

    def body(x_ref, g_ref, o_ref):
        xf = x_ref[...]
        r = lax.rsqrt(jnp.mean(xf * xf, axis=-1, keepdims=True) + EPS)
        o_ref[...] = (xf * r * g_ref[...]).astype(o_ref.dtype)

    return pl.pallas_call(
        body,
        name=name,
        grid=(s // ts,),
        in_specs=[pl.BlockSpec((ts, d), lambda i: (i, 0)), pl.BlockSpec((1, d), lambda i: (0, 0))],
        out_specs=pl.BlockSpec((ts, d), lambda i: (i, 0)),
        out_shape=jax.ShapeDtypeStruct((s, d), _CD),
        compiler_params=_cp(("parallel",)),
    )(x, g)


def _rms_gain_grad(x, dy, *, name):
    s, d = x.shape
    ts = _rows(s)

    def body(x_ref, dy_ref, dg_ref):
        xf = x_ref[...]
        r = lax.rsqrt(jnp.mean(xf * xf, axis=-1, keepdims=True) + EPS)
        part = jnp.sum(dy_ref[...] * (xf * r), axis=0, keepdims=True)

        @pl.when(pl.program_id(0) == 0)
        def _():
            dg_ref[...] = part

        @pl.when(pl.program_id(0) > 0)
        def _():
            dg_ref[...] += part

    tile = pl.BlockSpec((ts, d), lambda i: (i, 0))
    return pl.pallas_call(
        body,
        name=name,
        grid=(s // ts,),
        in_specs=[tile, tile],
        out_specs=pl.BlockSpec((1, d), lambda i: (0, 0)),
        out_shape=jax.ShapeDtypeStruct((1, d), F32),
        compiler_params=_cp(("arbitrary",)),
    )(x, dy)


def _chunk_scan(v, row_in_chunk, suffix):
    t = v.shape[0]
    step = 1
    while step < GLA_CHUNK:
        if suffix:
            v = v + jnp.where(row_in_chunk < GLA_CHUNK - step, pltpu.roll(v, t - step, 0), 0.0)
        else:
            v = v + jnp.where(row_in_chunk >= step, pltpu.roll(v, step, 0), 0.0)
        step *= 2
    return v


def _gate_pre(lr, w_ref, b_ref):
    return _dot(lr, w_ref[...]) + b_ref[...]


def _gate_fwd(z, waf, wab, baf, bab, *, name):
    s = z.shape[0]
    ts = _rows(s, light=True, times=4)

    def body(lr_ref, waf_ref, wab_ref, baf_ref, bab_ref, bf_ref, bb_ref):
        lr = lr_ref[...]
        ric = lax.broadcasted_iota(jnp.int32, (ts, GLA_K_TOTAL), 0) & (GLA_CHUNK - 1)
        for w_ref, b_ref, o_ref, suffix in ((waf_ref, baf_ref, bf_ref, False), (wab_ref, bab_ref, bb_ref, True)):
            pre = _gate_pre(lr, w_ref, b_ref)
            la = (jnp.minimum(pre, 0.0) - jnp.log(1.0 + jnp.exp(-jnp.abs(pre)))) * GLA_GATE_SCALE
            o_ref[...] = _chunk_scan(la, ric, suffix)

    wspec = pl.BlockSpec((128, GLA_K_TOTAL), lambda i: (0, 0))
    bspec = pl.BlockSpec((1, GLA_K_TOTAL), lambda i: (0, 0))
    tile = pl.BlockSpec((ts, GLA_K_TOTAL), lambda i: (i, 0))
    return pl.pallas_call(
        body,
        name=name,
        grid=(s // ts,),
        in_specs=[pl.BlockSpec((ts, 128), lambda i: (i, LR_COL // 128)), wspec, wspec, bspec, bspec],
        out_specs=[tile, tile],
        out_shape=[jax.ShapeDtypeStruct((s, GLA_K_TOTAL), F32)] * 2,
        compiler_params=_cp(("parallel",)),
    )(z, waf, wab, baf, bab)


def _gate_bwd(z, waf, wab, baf, bab, dbf, dbb, dqkv_f, dqkv_b, *, name):
    s = z.shape[0]
    ts = _rows(s, light=True)

    def body(lr_ref, waf_ref, wab_ref, baf_ref, bab_ref, dbf_ref, dbb_ref, gf_ref, gb_ref, dzb_ref, dwf_ref, dwb_ref, dbaf_ref, dbab_ref):
        lr = lr_ref[...]
        ric = lax.broadcasted_iota(jnp.int32, (ts, GLA_K_TOTAL), 0) & (GLA_CHUNK - 1)
        first = pl.program_id(0) == 0
        dlr = None
        for w_ref, b_ref, db_ref, dw_ref, dbias_ref, suffix in (
            (waf_ref, baf_ref, dbf_ref, dwf_ref, dbaf_ref, True),
            (wab_ref, bab_ref, dbb_ref, dwb_ref, dbab_ref, False),
        ):
            pre = _gate_pre(lr, w_ref, b_ref)
            dla = _chunk_scan(db_ref[...], ric, suffix)
            dpre = dla * GLA_GATE_SCALE * _sigmoid(-pre)
            part = _dot_nt(dpre, w_ref[...])
            dlr = part if dlr is None else dlr + part
            dw = _dot_tn(lr, dpre)
            dbias = jnp.sum(dpre, axis=0, keepdims=True)

            @pl.when(first)
            def _():
                dw_ref[...] = dw
                dbias_ref[...] = dbias

            @pl.when(jnp.logical_not(first))
            def _():
                dw_ref[...] += dw
                dbias_ref[...] += dbias

        dqkv = gf_ref[...].astype(F32) + gb_ref[...].astype(F32)
        dzb_ref[...] = jnp.concatenate([dqkv, dlr], axis=1).astype(dzb_ref.dtype)

    wspec = pl.BlockSpec((128, GLA_K_TOTAL), lambda i: (0, 0))
    bspec = pl.BlockSpec((1, GLA_K_TOTAL), lambda i: (0, 0))
    tile = pl.BlockSpec((ts, GLA_K_TOTAL), lambda i: (i, 0))
    wide = pl.BlockSpec((ts, 2 * GLA_K_TOTAL + GLA_V_TOTAL), lambda i: (i, 0))
    return pl.pallas_call(
        body,
        name=name,
        grid=(s // ts,),
        in_specs=[pl.BlockSpec((ts, 128), lambda i: (i, LR_COL // 128)), wspec, wspec, bspec, bspec, tile, tile, wide, wide],
        out_specs=[pl.BlockSpec((ts, ZB_COLS), lambda i: (i, 0)), wspec, wspec, bspec, bspec],
        out_shape=[
            jax.ShapeDtypeStruct((s, ZB_COLS), _CD),
            jax.ShapeDtypeStruct((128, GLA_K_TOTAL), F32),
            jax.ShapeDtypeStruct((128, GLA_K_TOTAL), F32),
            jax.ShapeDtypeStruct((1, GLA_K_TOTAL), F32),
            jax.ShapeDtypeStruct((1, GLA_K_TOTAL), F32),
        ],
        compiler_params=_cp(("arbitrary",)),
    )(z, waf, wab, baf, bab, dbf, dbb, dqkv_f, dqkv_b)


def _gla_masks(rev):
    lane_head = lax.broadcasted_iota(jnp.int32, (1, GLA_K_TOTAL), 1) >> 6
    head_masks = [lane_head == h for h in range(GLA_HEADS)]
    t = lax.broadcasted_iota(jnp.int32, (GLA_HEADS * GLA_CHUNK, GLA_CHUNK), 0) & (GLA_CHUNK - 1)
    u = lax.broadcasted_iota(jnp.int32, (GLA_HEADS * GLA_CHUNK, GLA_CHUNK), 1)
    tri = (u > t) if rev else (u <= t)
    row = lax.broadcasted_iota(jnp.int32, (GLA_CHUNK, GLA_K_TOTAL), 0)
    total_row = row == (0 if rev else GLA_CHUNK - 1)
    return head_masks, tri, total_row


def _spread(a, head_masks):
    return jnp.concatenate([jnp.where(m, a, 0.0) for m in head_masks], axis=0)


def _stack(a):
    return jnp.concatenate([a[:, GLA_DV * h : GLA_DV * (h + 1)] for h in range(GLA_HEADS)], axis=0)


def _unstack(a):
    return jnp.concatenate([a[GLA_CHUNK * h : GLA_CHUNK * (h + 1)] for h in range(GLA_HEADS)], axis=1)


def _collect(a, head_masks):
    out = None
    for h, m in enumerate(head_masks):
        part = jnp.where(m, a[GLA_CHUNK * h : GLA_CHUNK * (h + 1)], 0.0)
        out = part if out is None else out + part
    return out


def _gla_chunk_terms(q_ref, k_ref, v_ref, b_ref, rows, head_masks, tri, total_row):
    q = q_ref[rows, :] * (GLA_DK**-0.5)
    k = k_ref[rows, :]
    v = v_ref[rows, :]
    b = b_ref[rows, :]
    eb = jnp.exp(b)
    enb = jnp.exp(-b)
    g = jnp.sum(jnp.where(total_row, b, 0.0), axis=0, keepdims=True)
    egb = jnp.exp(g - b)
    qt = q * eb
    kt = k * enb
    kh = k * egb
    q_heads = _spread(qt, head_masks)
    attn = jnp.where(tri, _dot_nt(q_heads, kt), 0.0)
    return v, eb, enb, egb, jnp.exp(g), qt, kt, kh, q_heads, attn


def _gla_specs(s, tb, rev_blocks):
    nb = s // tb
    rb = (lambda i: nb - 1 - i) if rev_blocks else (lambda i: i)
    q_spec = pl.BlockSpec((tb, GLA_K_TOTAL), lambda i: (rb(i), 0))
    k_spec = pl.BlockSpec((tb, GLA_K_TOTAL), lambda i: (rb(i), 1))
    v_spec = pl.BlockSpec((tb, GLA_V_TOTAL), lambda i: (rb(i), 1))
    b_spec = pl.BlockSpec((tb, GLA_K_TOTAL), lambda i: (rb(i), 0))
    o_spec = pl.BlockSpec((tb, GLA_V_TOTAL), lambda i: (rb(i), 0))
    st_spec = pl.BlockSpec((tb // GLA_CHUNK, GLA_DV, GLA_K_TOTAL), lambda i: (rb(i), 0, 0))
    return nb, q_spec, k_spec, v_spec, b_spec, o_spec, st_spec


def _gla_fwd_chunk(cidx, q_ref, k_ref, v_ref, b_ref, o_ref, sv_ref, st_ref, masks):
    head_masks, tri, total_row = masks
    rows = pl.ds(pl.multiple_of(cidx * GLA_CHUNK, GLA_CHUNK), GLA_CHUNK)
    v, _, _, _, eg, _, _, kh, q_heads, attn = _gla_chunk_terms(q_ref, k_ref, v_ref, b_ref, rows, head_masks, tri, total_row)
    o = jnp.concatenate(
        [_dot(attn[GLA_CHUNK * h : GLA_CHUNK * (h + 1)], v[:, GLA_DV * h : GLA_DV * (h + 1)]) for h in range(GLA_HEADS)], axis=1
    )
    st = st_ref[...]
    o_ref[rows, :] = o + _unstack(_dot_nt(q_heads, st))
    sv_ref[cidx] = st
    st_ref[...] = st * eg + _dot_tn(_stack(v), _spread(kh, head_masks))


def _gla_fwd(z, b_f, b_b, *, name):
    s = z.shape[0]
    tb = _rows(s)
    cpb = tb // GLA_CHUNK
    nb, qf, kf, vf, bf, of, sf = _gla_specs(s, tb, False)
    _, qr, kr, vr, br, orr, sr = _gla_specs(s, tb, True)

    def body(qf_ref, kf_ref, vf_ref, bf_ref, qr_ref, kr_ref, vr_ref, br_ref, of_ref, svf_ref, or_ref, svr_ref, stf_ref, str_ref):
        masks_f, masks_r = _gla_masks(False), _gla_masks(True)

        @pl.when(pl.program_id(0) == 0)
        def _():
            stf_ref[...] = jnp.zeros_like(stf_ref)
            str_ref[...] = jnp.zeros_like(str_ref)

        def chunk(ci, carry):
            _gla_fwd_chunk(ci, qf_ref, kf_ref, vf_ref, bf_ref, of_ref, svf_ref, stf_ref, masks_f)
            _gla_fwd_chunk(cpb - 1 - ci, qr_ref, kr_ref, vr_ref, br_ref, or_ref, svr_ref, str_ref, masks_r)
            return carry

        lax.fori_loop(0, cpb, chunk, 0)

    o_shape = jax.ShapeDtypeStruct((s, GLA_V_TOTAL), F32)
    st_shape = jax.ShapeDtypeStruct((s // GLA_CHUNK, GLA_DV, GLA_K_TOTAL), F32)
    return pl.pallas_call(
        body,
        name=name,
        grid=(nb,),
        in_specs=[qf, kf, vf, bf, qr, kr, vr, br],
        out_specs=[of, sf, orr, sr],
        out_shape=[o_shape, st_shape, o_shape, st_shape],
        scratch_shapes=[pltpu.VMEM((GLA_DV, GLA_K_TOTAL), F32)] * 2,
        compiler_params=_cp(("arbitrary",)),
    )(z, z, z, b_f, z, z, z, b_b)


def _gla_bwd_chunk(cidx, q_ref, k_ref, v_ref, b_ref, do_ref, sv_ref, dqkv_ref, db_ref, dst_ref, masks):
    head_masks, tri, total_row = masks
    rows = pl.ds(pl.multiple_of(cidx * GLA_CHUNK, GLA_CHUNK), GLA_CHUNK)
    v, eb, enb, egb, eg, qt, kt, kh, q_heads, attn = _gla_chunk_terms(q_ref, k_ref, v_ref, b_ref, rows, head_masks, tri, total_row)
    do_c = do_ref[rows, :]
    st = sv_ref[cidx]
    dst = dst_ref[...]
    do_s, v_s = _stack(do_c), _stack(v)
    hs = lambda a, h: a[GLA_CHUNK * h : GLA_CHUNK * (h + 1)]
    vs = lambda a, h: a[:, GLA_DV * h : GLA_DV * (h + 1)]
    dattn = jnp.concatenate([_dot_nt(vs(do_c, h), vs(v, h)) for h in range(GLA_HEADS)], axis=0)
    dattn = jnp.where(tri, dattn, 0.0)
    dv = jnp.concatenate([_dot_tn(hs(attn, h), vs(do_c, h)) for h in range(GLA_HEADS)], axis=1)
    dv = dv + _unstack(_dot_nt(_spread(kh, head_masks), dst))
    dqt = _collect(_dot(do_s, st) + _dot(dattn, kt), head_masks)
    dkt = _dot_tn(dattn, q_heads)
    dkh = _collect(_dot(v_s, dst), head_masks)
    dg = jnp.sum(dkh * kh, axis=0, keepdims=True) + jnp.sum(dst * st, axis=0, keepdims=True) * eg
    db = dqt * qt - dkt * kt - dkh * kh + jnp.where(total_row, dg, 0.0)
    dq = dqt * eb * (GLA_DK**-0.5)
    dk = dkt * enb + dkh * egb
    dqkv_ref[rows, :] = jnp.concatenate([dq, dk, dv], axis=1).astype(dqkv_ref.dtype)
    db_ref[rows, :] = db
    dst_ref[...] = dst * eg + _dot_tn(do_s, q_heads)


def _gla_bwd(z, b_f, b_b, do, st_f, st_b, *, name):
    s = z.shape[0]
    tb = _rows(s)
    cpb = tb // GLA_CHUNK
    wide = 2 * GLA_K_TOTAL + GLA_V_TOTAL
    nb, qf, kf, vf, bf, of, sf = _gla_specs(s, tb, True)
    _, qr, kr, vr, br, orr, sr = _gla_specs(s, tb, False)
    gf = pl.BlockSpec((tb, wide), lambda i: (nb - 1 - i, 0))
    gr = pl.BlockSpec((tb, wide), lambda i: (i, 0))

    def body(qf_ref, kf_ref, vf_ref, bf_ref, dof_ref, svf_ref, qr_ref, kr_ref, vr_ref, br_ref, dor_ref, svr_ref,
             gf_ref, dbf_ref, gr_ref, dbr_ref, dstf_ref, dstr_ref):
        masks_f, masks_r = _gla_masks(False), _gla_masks(True)

        @pl.when(pl.program_id(0) == 0)
        def _():
            dstf_ref[...] = jnp.zeros_like(dstf_ref)
            dstr_ref[...] = jnp.zeros_like(dstr_ref)

        def chunk(ci, carry):
            _gla_bwd_chunk(cpb - 1 - ci, qf_ref, kf_ref, vf_ref, bf_ref, dof_ref, svf_ref, gf_ref, dbf_ref, dstf_ref, masks_f)
            _gla_bwd_chunk(ci, qr_ref, kr_ref, vr_ref, br_ref, dor_ref, svr_ref, gr_ref, dbr_ref, dstr_ref, masks_r)
            return carry

        lax.fori_loop(0, cpb, chunk, 0)

    g_shape = jax.ShapeDtypeStruct((s, wide), _CD)
    db_shape = jax.ShapeDtypeStruct((s, GLA_K_TOTAL), F32)
    return pl.pallas_call(
        body,
        name=name,
        grid=(nb,),
        in_specs=[qf, kf, vf, bf, of, sf, qr, kr, vr, br, orr, sr],
        out_specs=[gf, bf, gr, br],
        out_shape=[g_shape, db_shape, g_shape, db_shape],
        scratch_shapes=[pltpu.VMEM((GLA_DV, GLA_K_TOTAL), F32)] * 2,
        compiler_params=_cp(("arbitrary",)),
    )(z, z, z, b_f, do, st_f, z, z, z, b_b, do, st_b)


HALO = 8


def _halo_specs(s, ts, width, col):
    last = s // HALO - 1
    per = ts // HALO
    prev = pl.BlockSpec((HALO, width), lambda i: (jnp.maximum(i * per - 1, 0), col))
    nxt = pl.BlockSpec((HALO, width), lambda i: (jnp.minimum((i + 1) * per, last), col))
    return prev, nxt


def _group_ones():
    group = jnp.arange(CONV_WIDTH, dtype=jnp.int32) // CONV_GROUP
    return (group[:, None] == group[None, :]).astype(BF16)


_ONES_SPEC = pl.BlockSpec((CONV_WIDTH, CONV_WIDTH), lambda i: (0, 0))


def _conv_terms(cc_ext, cu_ext, cw, valid):
    n = cc_ext.shape[0]
    hc = jnp.where(valid, cc_ext * cu_ext, 0.0)
    hc_prev = pltpu.roll(hc, 1, 0)
    hc_next = pltpu.roll(hc, n - 1, 0)
    conv = cw[0:1] * hc_prev + cw[1:2] * hc + cw[2:3] * hc_next
    return hc, hc_prev, hc_next, conv


def _ext(prev_ref, cur_ref, next_ref):
    return jnp.concatenate([prev_ref[...], cur_ref[...], next_ref[...]], axis=0)


def _valid_rows(ts, s):
    row = lax.broadcasted_iota(jnp.int32, (ts + 2 * HALO, 1), 0) + (pl.program_id(0) * ts - HALO)
    return (row >= 0) & (row < s)


def _head_norm(o, gn):
    out = []
    for h in range(GLA_HEADS):
        oh = o[:, GLA_DV * h : GLA_DV * (h + 1)]
        r = lax.rsqrt(jnp.mean(oh * oh, axis=-1, keepdims=True) + EPS)
        out.append((oh * r, r))
    return out


def _mix_fwd(z, o_f, o_b, conv_w, conv_norm, gla_norm, *, name):
    s = z.shape[0]
    ts = _rows(s, light=True)
    cprev, cnext = _halo_specs(s, ts, CONV_WIDTH, 1)
    uprev, unext = _halo_specs(s, ts, CONV_WIDTH, 2)

    def body(cb_ref, cc_ref, cu_ref, ccp_ref, ccn_ref, cup_ref, cun_ref, g_ref, of_ref, ob_ref, cw_ref, cn_ref, gn_ref, ones_ref, y_ref):
        valid = _valid_rows(ts, s)
        _, _, _, conv = _conv_terms(_ext(ccp_ref, cc_ref, ccn_ref), _ext(cup_ref, cu_ref, cun_ref), cw_ref[...], valid)
        yc = cb_ref[...] * conv[HALO : HALO + ts]
        ms = _dot_split(yc * yc, ones_ref[...]) * (1.0 / CONV_GROUP)
        y_conv = yc * lax.rsqrt(ms + EPS) * cn_ref[...]
        gate = g_ref[...]
        silu = gate * _sigmoid(gate)
        gn = gn_ref[...]
        y_gla = jnp.concatenate([oh * gn for oh, _ in _head_norm(of_ref[...] + ob_ref[...], gn)], axis=1) * silu
        y_ref[...] = jnp.concatenate([y_conv, y_gla], axis=1).astype(y_ref.dtype)

    col = lambda c, w=CONV_WIDTH: pl.BlockSpec((ts, w), lambda i: (i, c))
    return pl.pallas_call(
        body,
        name=name,
        grid=(s // ts,),
        in_specs=[col(0), col(1), col(2), cprev, cnext, uprev, unext, col(3), col(0), col(0),
                  pl.BlockSpec((CONV_K, CONV_WIDTH), lambda i: (0, 0)), pl.BlockSpec((1, CONV_WIDTH), lambda i: (0, 0)),
                  pl.BlockSpec((1, GLA_DV), lambda i: (0, 0)), _ONES_SPEC],
        out_specs=pl.BlockSpec((ts, D_MODEL), lambda i: (i, 0)),
        out_shape=jax.ShapeDtypeStruct((s, D_MODEL), _CD),
        compiler_params=_cp(("parallel",)),
    )(z, z, z, z, z, z, z, z, o_f, o_b, conv_w, conv_norm, gla_norm, _group_ones())


def _mix_bwd(z, o_f, o_b, dy, conv_w, conv_norm, gla_norm, *, name):
    s = z.shape[0]
    ts = _rows(s)
    halos = [_halo_specs(s, ts, CONV_WIDTH, c) for c in (0, 1, 2)]
    dprev, dnext = _halo_specs(s, ts, CONV_WIDTH, 0)

    def body(cb_ref, cc_ref, cu_ref, cbp_ref, cbn_ref, ccp_ref, ccn_ref, cup_ref, cun_ref, g_ref, of_ref, ob_ref,
             dyc_ref, dyg_ref, dyp_ref, dyn_ref, cw_ref, cn_ref, gn_ref, ones_ref, dza_ref, do_ref, dcw_ref, dcn_ref, dgn_ref):
        n = ts + 2 * HALO
        valid = _valid_rows(ts, s)
        cw = cw_ref[...]
        cn = cn_ref[...]
        ones = ones_ref[...]
        cb = _ext(cbp_ref, cb_ref, cbn_ref)
        cc = _ext(ccp_ref, cc_ref, ccn_ref)
        cu = _ext(cup_ref, cu_ref, cun_ref)
        dy = _ext(dyp_ref, dyc_ref, dyn_ref)
        hc, hc_prev, hc_next, conv = _conv_terms(cc, cu, cw, valid)
        yc = cb * conv
        r = lax.rsqrt(_dot_split(yc * yc, ones) * (1.0 / CONV_GROUP) + EPS)
        yh = yc * r
        dyh = dy * cn
        dyc = r * (dyh - yh * (_dot_split(dyh * yh, ones) * (1.0 / CONV_GROUP)))
        dconv = jnp.where(valid, dyc * cb, 0.0)
        dhc = cw[0:1] * pltpu.roll(dconv, n - 1, 0) + cw[1:2] * dconv + cw[2:3] * pltpu.roll(dconv, 1, 0)
        mid = lambda a: a[HALO : HALO + ts]
        dza_ref[:, 0 : 3 * CONV_WIDTH] = jnp.concatenate([mid(dyc * conv), mid(dhc * cu), mid(dhc * cc)], axis=1).astype(dza_ref.dtype)
        dconv_m = mid(dconv)
        colsum = lambda a: jnp.sum(a, axis=0, keepdims=True)
        dcw = jnp.concatenate([colsum(dconv_m * mid(hc_prev)), colsum(dconv_m * mid(hc)), colsum(dconv_m * mid(hc_next))], axis=0)
        dcn = colsum(mid(dy * yh))

        gate = g_ref[...]
        sg = _sigmoid(gate)
        silu = gate * sg
        gn = gn_ref[...]
        dyg = dyg_ref[...]
        don = dyg * silu
        heads = _head_norm(of_ref[...] + ob_ref[...], gn)
        on = jnp.concatenate([oh * gn for oh, _ in heads], axis=1)
        dza_ref[:, 3 * CONV_WIDTH : ZA_COLS] = (dyg * on * (sg * (1.0 + gate * (1.0 - sg)))).astype(dza_ref.dtype)
        dgn = jnp.zeros((1, GLA_DV), F32)
        dos = []
        for h, (oh, rh) in enumerate(heads):
            donh = don[:, GLA_DV * h : GLA_DV * (h + 1)]
            dgn = dgn + colsum(donh * oh)
            doh = donh * gn
            dos.append(rh * (doh - oh * jnp.mean(doh * oh, axis=-1, keepdims=True)))
        do_ref[...] = jnp.concatenate(dos, axis=1)

        first = pl.program_id(0) == 0

        @pl.when(first)
        def _():
            dcw_ref[...] = dcw
            dcn_ref[...] = dcn
            dgn_ref[...] = dgn

        @pl.when(jnp.logical_not(first))
        def _():
            dcw_ref[...] += dcw
            dcn_ref[...] += dcn
            dgn_ref[...] += dgn

    col = lambda c, w=CONV_WIDTH: pl.BlockSpec((ts, w), lambda i: (i, c))
    cw_spec = pl.BlockSpec((CONV_K, CONV_WIDTH), lambda i: (0, 0))
    cn_spec = pl.BlockSpec((1, CONV_WIDTH), lambda i: (0, 0))
    gn_spec = pl.BlockSpec((1, GLA_DV), lambda i: (0, 0))
    return pl.pallas_call(
        body,
        name=name,
        grid=(s // ts,),
        in_specs=[col(0), col(1), col(2), halos[0][0], halos[0][1], halos[1][0], halos[1][1], halos[2][0], halos[2][1],
                  col(3), col(0), col(0), col(0), col(1), dprev, dnext, cw_spec, cn_spec, gn_spec, _ONES_SPEC],
        out_specs=[pl.BlockSpec((ts, ZA_COLS), lambda i: (i, 0)), col(0), cw_spec, cn_spec, gn_spec],
        out_shape=[
            jax.ShapeDtypeStruct((s, ZA_COLS), _CD),
            jax.ShapeDtypeStruct((s, GLA_V_TOTAL), F32),
            jax.ShapeDtypeStruct((CONV_K, CONV_WIDTH), F32),
            jax.ShapeDtypeStruct((1, CONV_WIDTH), F32),
            jax.ShapeDtypeStruct((1, GLA_DV), F32),
        ],
        compiler_params=_cp(("arbitrary",)),
    )(z, z, z, z, z, z, z, z, z, z, o_f, o_b, dy, dy, dy, dy, conv_w, conv_norm, gla_norm, _group_ones())


def _xa_probs(q_ref, kv_ref, h):
    qh = q_ref[:, XA_HEAD_DIM * h : XA_HEAD_DIM * (h + 1)]
    kh = kv_ref[:, XA_HEAD_DIM * h : XA_HEAD_DIM * (h + 1)]
    vh = kv_ref[:, D_MODEL + XA_HEAD_DIM * h : D_MODEL + XA_HEAD_DIM * (h + 1)]
    sc = _dot_nt(qh, kh) * (XA_HEAD_DIM**-0.5)
    e = jnp.exp(sc - jnp.max(sc, axis=-1, keepdims=True))
    return qh, kh, vh, e / jnp.sum(e, axis=-1, keepdims=True)


def _xattn_block(hx, w_xq, kv, w_xo, x1, gain, *, name):
    s, d = hx.shape
    ts = _rows(s)

    def body(h_ref, wq_ref, kv_ref, wo_ref, x_ref, g_ref, q_out, o_out, hm_out, x_out):
        q = _dot(h_ref[...], wq_ref[...]).astype(_CD)
        q_out[...] = q
        heads = []
        for h in range(XA_HEADS):
            _, _, vh, p = _xa_probs(q, kv_ref, h)
            heads.append(_dot(p, vh))
        o = jnp.concatenate(heads, axis=1).astype(_CD)
        o_out[...] = o
        x = _dot(o, wo_ref[...]) + x_ref[...]
        r = lax.rsqrt(jnp.mean(x * x, axis=-1, keepdims=True) + EPS)
        x_out[...] = x
        hm_out[...] = (x * r * g_ref[...]).astype(hm_out.dtype)

    tile = pl.BlockSpec((ts, d), lambda i: (i, 0))
    whole = lambda arr: pl.BlockSpec(arr.shape, lambda i: (0, 0))
    lo = jax.ShapeDtypeStruct((s, d), _CD)
    return pl.pallas_call(
        body,
        name=name,
        grid=(s // ts,),
        in_specs=[tile, whole(w_xq), whole(kv), whole(w_xo), tile, whole(gain)],
        out_specs=[tile] * 4,
        out_shape=[lo, lo, lo, jax.ShapeDtypeStruct((s, d), F32)],
        compiler_params=_cp(("parallel",)),
    )(hx, w_xq, kv, w_xo, x1, gain)


def _xattn_bwd(qx, kv, dx, w_xo, *, name):
    s = qx.shape[0]
    ts = _rows(s, light=True)

    def body(q_ref, kv_ref, dx_ref, w_ref, dq_ref, dkv_ref):
        do = _dot_nt(dx_ref[...], w_ref[...]).astype(_CD)
        dqs, dks, dvs = [], [], []
        for h in range(XA_HEADS):
            qh, kh, vh, p = _xa_probs(q_ref, kv_ref, h)
            doh = do[:, XA_HEAD_DIM * h : XA_HEAD_DIM * (h + 1)]
            dp = _dot_nt(doh, vh)
            ds = p * (dp - jnp.sum(dp * p, axis=-1, keepdims=True)) * (XA_HEAD_DIM**-0.5)
            dqs.append(_dot(ds, kh))
            dks.append(_dot_tn(ds, qh))
            dvs.append(_dot_tn(p, doh))
        dq_ref[...] = jnp.concatenate(dqs, axis=1).astype(dq_ref.dtype)
        dkv = jnp.concatenate(dks + dvs, axis=1)

        @pl.when(pl.program_id(0) == 0)
        def _():
            dkv_ref[...] = dkv

        @pl.when(pl.program_id(0) > 0)
        def _():
            dkv_ref[...] += dkv

    tile = pl.BlockSpec((ts, D_MODEL), lambda i: (i, 0))
    kv_spec = pl.BlockSpec((N_MEM, 2 * D_MODEL), lambda i: (0, 0))
    return pl.pallas_call(
        body,
        name=name,
        grid=(s // ts,),
        in_specs=[tile, kv_spec, tile, pl.BlockSpec((D_MODEL, D_MODEL), lambda i: (0, 0))],
        out_specs=[tile, kv_spec],
        out_shape=[jax.ShapeDtypeStruct((s, D_MODEL), _CD), jax.ShapeDtypeStruct((N_MEM, 2 * D_MODEL), F32)],
        compiler_params=_cp(("arbitrary",)),
    )(qx, kv, dx, w_xo)


def _adamw_math(w, g, m, v):
    m = ADAM_B1 * m + (1.0 - ADAM_B1) * g
    v = ADAM_B2 * v + (1.0 - ADAM_B2) * (g * g)
    m_hat = m / (1.0 - ADAM_B1**ADAM_STEP)
    v_hat = v / (1.0 - ADAM_B2**ADAM_STEP)
    delta = -ADAM_LR * (m_hat / (jnp.sqrt(v_hat) + ADAM_EPS) + ADAM_WD * w)
    return delta, m, v


def _adamw(w, m, v, shard_rows, off, *, transposed, name):
    r, c = w.shape
    by_columns = r % 256 != 0
    tr = 512 if (c if by_columns else r) % 512 == 0 and off % 512 == 0 else 256
    if by_columns:
        assert not transposed and off == 0
        g_spec = tile = pl.BlockSpec((r, tr), lambda i: (0, i))
    else:
        g_spec = pl.BlockSpec((c, tr), lambda i: (off // c, i)) if transposed else pl.BlockSpec((tr, c), lambda i: (off // tr + i, 0))
        tile = pl.BlockSpec((tr, c), lambda i: (i, 0))

    def body(w_ref, g_ref, m_ref, v_ref, go_ref, d_ref, nm_ref, nv_ref):
        g = g_ref[...].T if transposed else g_ref[...]
        go_ref[...] = g
        d_ref[...], nm_ref[...], nv_ref[...] = _adamw_math(w_ref[...], g, m_ref[...], v_ref[...])

    return pl.pallas_call(
        body,
        name=name,
        grid=((c if by_columns else r) // tr,),
        in_specs=[tile, g_spec, tile, tile],
        out_specs=[tile] * 4,
        out_shape=[jax.ShapeDtypeStruct((r, c), F32)] * 4,
        compiler_params=_cp(("parallel",), vmem=_VMEM_STREAM),
    )(w, shard_rows, m, v)


def _adamw_small(groups, *, name):
    n = len(groups)

    def body(*refs):
        ins, outs = refs[: 4 * n], refs[4 * n :]
        for i in range(n):
            w_ref, g_ref, m_ref, v_ref = ins[4 * i : 4 * i + 4]
            outs[3 * i][...], outs[3 * i + 1][...], outs[3 * i + 2][...] = _adamw_math(w_ref[...], g_ref[...], m_ref[...], v_ref[...])

    flat = [a for grp in groups for a in grp]
    vm = pl.BlockSpec(memory_space=pltpu.VMEM)
    res = pl.pallas_call(
        body,
        name=name,
        in_specs=[vm] * (4 * n),
        out_specs=[vm] * (3 * n),
        out_shape=[jax.ShapeDtypeStruct(grp[0].shape, F32) for grp in groups for _ in range(3)],
        compiler_params=_cp(),
    )(*flat)
    return [tuple(res[3 * i : 3 * i + 3]) for i in range(n)]


def _place():
    return lax.axis_index("x"), lax.axis_index("y"), lax.axis_index("c")


def _rel_chip(x, y, k):
    return (1 - x if k & 2 else x), (1 - y if k & 1 else y)


def _half(c, rh):
    return pl.ds(pl.multiple_of(c * rh, 16), rh)


HBM = pl.BlockSpec(memory_space=pltpu.HBM)
SEM = pl.BlockSpec(memory_space=pltpu.SEMAPHORE)
EFFECT = pltpu.SideEffectType.DATAFLOW_SIDE_EFFECTING


def _in_hbm(a):
    return pltpu.with_memory_space_constraint(a, pltpu.HBM)


def _gather_copies(p_ref, land_ref, send_sems, recv_sems):
    rh = p_ref.shape[0] // 2
    x, y, c = _place()
    rows = _half(c, rh)
    copies = []
    for k in range(1, N_CHIPS):
        cx, cy = _rel_chip(x, y, k)
        copies.append(pltpu.make_async_remote_copy(
            src_ref=p_ref.at[rows], dst_ref=land_ref.at[2 * x + y, rows], send_sem=send_sems.at[k - 1], recv_sem=recv_sems.at[k - 1],
            device_id=(cx, cy, c), device_id_type=MESH))
    copies.append(pltpu.make_async_remote_copy(
        src_ref=p_ref, dst_ref=land_ref.at[2 * x + y], send_sem=send_sems.at[N_CHIPS - 1], recv_sem=recv_sems.at[N_CHIPS - 1],
        device_id=(x, y, 1 - c), device_id_type=MESH))
    return copies


def _gather_start(pack, after, *, name):
    r, w = pack.shape

    def body(p_ref, land_ref, after_ref, send_sems, recv_sems, p_thru, land_thru, token):
        for cp in _gather_copies(p_ref, land_ref, send_sems, recv_sems):
            cp.start()
        token[...] = jnp.zeros_like(token)

    return pl.pallas_call(
        body,
        name=name,
        out_shape=(pltpu.SemaphoreType.DMA((N_CHIPS,)), pltpu.SemaphoreType.DMA((N_CHIPS,)), pltpu.HBM((r, w), pack.dtype),
                   pltpu.HBM((N_CHIPS, r, w), pack.dtype), jax.ShapeDtypeStruct((8, 128), F32)),
        in_specs=(HBM, HBM, ANY),
        out_specs=(SEM, SEM, HBM, HBM, pl.BlockSpec(memory_space=pltpu.VMEM)),
        input_output_aliases={0: 2, 1: 3},
        compiler_params=pltpu.CompilerParams(has_side_effects=EFFECT),
    )(_in_hbm(pack), _in_hbm(lax.empty((N_CHIPS, r, w), pack.dtype)), after)


def _gather_wait(send_sems, recv_sems, pack, land, after, *, name):
    def body(p_ref, land_ref, send_sems, recv_sems, after_ref, p_out, land_out):
        for cp in _gather_copies(p_ref, land_ref, send_sems, recv_sems):
            cp.wait_send()
            cp.wait_recv()

    return pl.pallas_call(
        body,
        name=name,
        out_shape=(pltpu.HBM(pack.shape, pack.dtype), pltpu.HBM(land.shape, land.dtype)),
        in_specs=(HBM, HBM, SEM, SEM, ANY),
        out_specs=(HBM, HBM),
        input_output_aliases={0: 0, 1: 1},
        compiler_params=pltpu.CompilerParams(has_side_effects=EFFECT),
    )(pack, land, send_sems, recv_sems, after)


def _gather_spread(land, *, name):
    n, r, w = land.shape
    rh = r // 2

    def body(land_ref, o_ref, send_sems, recv_sems):
        x, y, c = _place()
        rows = _half(c, rh)
        copies = []
        for k in range(1, N_CHIPS):
            cx, cy = _rel_chip(x, y, k)
            copies.append(pltpu.make_async_remote_copy(
                src_ref=land_ref.at[2 * cx + cy, rows], dst_ref=o_ref.at[2 * cx + cy, rows], send_sem=send_sems.at[k - 1],
                recv_sem=recv_sems.at[k - 1], device_id=(x, y, 1 - c), device_id_type=MESH))
        for cp in copies:
            cp.start()
        for cp in copies:
            cp.wait()

    return pl.pallas_call(
        body,
        name=name,
        in_specs=[ANY],
        out_specs=ANY,
        out_shape=jax.ShapeDtypeStruct(land.shape, land.dtype),
        input_output_aliases={0: 0},
        scratch_shapes=[pltpu.SemaphoreType.DMA((N_CHIPS - 1,)), pltpu.SemaphoreType.DMA((N_CHIPS - 1,))],
        compiler_params=pltpu.CompilerParams(has_side_effects=True),
    )(land)


N_PARTS = 2 * (N_CHIPS - 1)


def _scatter_copies(lo_ref, g_ref, land_lo_ref, land_f_ref, send_sems, recv_sems, starting):
    rh = g_ref.shape[1] // 2
    x, y, c = _place()
    copies = []
    for k in range(1, N_CHIPS):
        cx, cy = _rel_chip(x, y, k)
        for i in range(2):
            part = 2 * (k - 1) + (c if starting else i)
            copies.append(pltpu.make_async_remote_copy(
                src_ref=lo_ref.at[2 * cx + cy, pl.ds(i * rh, rh)], dst_ref=land_lo_ref.at[part],
                send_sem=send_sems.at[2 * (k - 1) + i], recv_sem=recv_sems.at[part], device_id=(cx, cy, i), device_id_type=MESH))
    copies.append(pltpu.make_async_remote_copy(
        src_ref=g_ref.at[2 * x + y, _half(1 - c, rh)], dst_ref=land_f_ref, send_sem=send_sems.at[N_PARTS], recv_sem=recv_sems.at[N_PARTS],
        device_id=(x, y, 1 - c), device_id_type=MESH))
    return copies


def _scatter_start(g_lo, g, *, name):
    n, r, w = g.shape
    rh = r // 2

    def body(lo_ref, g_ref, land_lo_ref, land_f_ref, send_sems, recv_sems, lo_thru, g_thru, land_lo_thru, land_f_thru, token):
        for cp in _scatter_copies(lo_ref, g_ref, land_lo_ref, land_f_ref, send_sems, recv_sems, True):
            cp.start()
        token[...] = jnp.zeros_like(token)

    return pl.pallas_call(
        body,
        name=name,
        out_shape=(pltpu.SemaphoreType.DMA((N_PARTS + 1,)), pltpu.SemaphoreType.DMA((N_PARTS + 1,)), pltpu.HBM(g_lo.shape, g_lo.dtype),
                   pltpu.HBM(g.shape, g.dtype), pltpu.HBM((N_PARTS, rh, w), g_lo.dtype), pltpu.HBM((rh, w), g.dtype),
                   jax.ShapeDtypeStruct((8, 128), F32)),
        in_specs=(HBM, HBM, HBM, HBM),
        out_specs=(SEM, SEM, HBM, HBM, HBM, HBM, pl.BlockSpec(memory_space=pltpu.VMEM)),
        input_output_aliases={0: 2, 1: 3, 2: 4, 3: 5},
        compiler_params=pltpu.CompilerParams(has_side_effects=EFFECT),
    )(_in_hbm(g_lo), _in_hbm(g), _in_hbm(lax.empty((N_PARTS, rh, w), g_lo.dtype)), _in_hbm(lax.empty((rh, w), g.dtype)))


def _scatter_wait(send_sems, recv_sems, g_lo, g, land_lo, land_f, after, *, name):
    def body(lo_ref, g_ref, land_lo_ref, land_f_ref, send_sems, recv_sems, after_ref, o0, o1, o2, o3):
        for cp in _scatter_copies(lo_ref, g_ref, land_lo_ref, land_f_ref, send_sems, recv_sems, False):
            cp.wait_send()
            cp.wait_recv()

    arrays = (g_lo, g, land_lo, land_f)
    return pl.pallas_call(
        body,
        name=name,
        out_shape=tuple(pltpu.HBM(a.shape, a.dtype) for a in arrays),
        in_specs=(HBM, HBM, HBM, HBM, SEM, SEM, ANY),
        out_specs=(HBM, HBM, HBM, HBM),
        input_output_aliases={0: 0, 1: 1, 2: 2, 3: 3},
        compiler_params=pltpu.CompilerParams(has_side_effects=EFFECT),
    )(*arrays, send_sems, recv_sems, after)


def _scatter_sum(g, land_lo, land_f, where, *, name):
    n, r, w = g.shape
    rh = r // 2
    tr = _pick(rh, (256, 160, 80))
    nt = rh // tr

    def body(where_ref, g_ref, f_ref, lo_ref, o_ref):
        acc = g_ref[0] + f_ref[...]
        for part in range(N_PARTS):
            acc = acc + lo_ref[part].astype(F32)
        o_ref[...] = acc

    return pl.pallas_call(
        body,
        name=name,
        grid_spec=pltpu.PrefetchScalarGridSpec(
            num_scalar_prefetch=1,
            grid=(nt,),
            in_specs=[pl.BlockSpec((1, tr, w), lambda i, wh: (wh[1], wh[0] * nt + i, 0)),
                      pl.BlockSpec((tr, w), lambda i, wh: (i, 0)),
                      pl.BlockSpec((N_PARTS, tr, w), lambda i, wh: (0, i, 0))],
            out_specs=pl.BlockSpec((tr, w), lambda i, wh: (wh[0] * nt + i, 0)),
        ),
        out_shape=jax.ShapeDtypeStruct((r, w), F32),
        compiler_params=_cp(("parallel",), vmem=_VMEM_STREAM),
    )(where, g, land_f, land_lo)


def _swap_all(shards, *, name):
    n = len(shards)

    def body(*refs):
        ins, outs = refs[:n], refs[n : 2 * n]
        send_sems, recv_sems = refs[2 * n :]
        x, y, c = _place()
        copies = []
        for i, (e_ref, o_ref) in enumerate(zip(ins, outs)):
            rows = _half(c, e_ref.shape[0] // 2)
            copies.append(pltpu.make_async_remote_copy(src_ref=e_ref.at[rows], dst_ref=o_ref.at[rows], send_sem=send_sems.at[i],
                                                       recv_sem=recv_sems.at[i], device_id=(x, y, 1 - c), device_id_type=MESH))
        for cp in copies:
            cp.start()
        for cp in copies:
            cp.wait()

    return pl.pallas_call(
        body,
        name=name,
        in_specs=[ANY] * n,
        out_specs=[ANY] * n,
        out_shape=[jax.ShapeDtypeStruct(e.shape, e.dtype) for e in shards],
        input_output_aliases={i: i for i in range(n)},
        scratch_shapes=[pltpu.SemaphoreType.DMA((n,)), pltpu.SemaphoreType.DMA((n,))],
        compiler_params=pltpu.CompilerParams(has_side_effects=True),
    )(*shards)


def _sum_small(small, after):
    n_dev = 8

    def body(s_ref, after_ref, o_ref, all_ref, send_sems, recv_sems):
        x, y, c = _place()
        me = 4 * x + 2 * y + c
        all_ref[me] = s_ref[...]
        copies = []
        for k in range(1, n_dev):
            cx, cy = _rel_chip(x, y, k >> 1)
            cc = 1 - c if k & 1 else c
            copies.append(pltpu.make_async_remote_copy(
                src_ref=s_ref, dst_ref=all_ref.at[me], send_sem=send_sems.at[k - 1], recv_sem=recv_sems.at[k - 1],
                device_id=(cx, cy, cc), device_id_type=MESH))
        for cp in copies:
            cp.start()
        for cp in copies:
            cp.wait()
        acc = all_ref[0]
        for a in range(1, n_dev):
            acc = acc + all_ref[a]
        o_ref[...] = acc

    vm = pl.BlockSpec(memory_space=pltpu.VMEM)
    return pl.pallas_call(
        body,
        name="sum_small",
        in_specs=[vm, ANY],
        out_specs=vm,
        out_shape=jax.ShapeDtypeStruct(small.shape, F32),
        scratch_shapes=[pltpu.VMEM((n_dev,) + small.shape, F32), pltpu.SemaphoreType.DMA((n_dev - 1,)), pltpu.SemaphoreType.DMA((n_dev - 1,))],
        compiler_params=pltpu.CompilerParams(has_side_effects=True),
    )(small, after)


MATS = {"w_in": (776, True), "w_out": (256, False), "w_xq": (256, False), "w_xkv": (512, True), "w_xo": (256, False),
        "w_up": (1024, True), "w_down": (1024, False)}
GATHER_FIRST = ("w_in",)
GATHER_REST = ("w_up", "w_down", "w_out", "w_xq", "w_xkv", "w_xo")
IN_PLACE = ("w_up", "w_down")
GRAD_GROUPS = (("w_up", "w_down"), ("w_out", "w_xq", "w_xkv", "w_xo"), ("w_in",))


def _group_rows(names):
    n = sum(MATS[name][0] for name in names)
    return n + (-n) % 32


def _pack(pieces, rows):
    p = jnp.concatenate(pieces, axis=0) if len(pieces) > 1 else pieces[0]
    return jnp.pad(p, ((0, rows - p.shape[0]), (0, 0))) if rows > p.shape[0] else p


SMALL = (
    ("mix_norm", 1024), ("conv_norm", 512), ("b_af", 256), ("b_ab", 256), ("gla_norm", 128), ("xa_norm", 1024), ("mem_norm", 1024),
    ("mlp_norm", 1024), ("final_norm", 1024), ("conv_w", 1536), ("w_af", 4096), ("w_ab", 4096), ("loss", 128),
)


def kernel(x, mem, mix_norm, w_in, conv_w, conv_norm, w_af, b_af, w_ab, b_ab, gla_norm, w_out, xa_norm, mem_norm, w_xq, w_xkv, w_xo, mlp_norm, w_up, w_down, final_norm, loss_target, m_mix_norm, m_w_in, m_conv_w, m_conv_norm, m_w_af, m_b_af, m_w_ab, m_b_ab, m_gla_norm, m_w_out, m_xa_norm, m_mem_norm, m_w_xq, m_w_xkv, m_w_xo, m_mlp_norm, m_w_up, m_w_down, m_final_norm, v_mix_norm, v_w_in, v_conv_w, v_conv_norm, v_w_af, v_b_af, v_w_ab, v_b_ab, v_gla_norm, v_w_out, v_xa_norm, v_mem_norm, v_w_xq, v_w_xkv, v_w_xo, v_mlp_norm, v_w_up, v_w_down, v_final_norm):
    given = dict(locals())
    xi, yi, ci = _place()
    chip = 2 * xi + yi
    where = jnp.stack([ci, chip]).astype(jnp.int32)

    lo = {name: (given[name][0].T if MATS[name][1] else given[name][0]).astype(_CD) for name in MATS}
    pack_rest = _pack([lo[name] for name in GATHER_REST], _group_rows(GATHER_REST))
    pack_first = _pack([lo[name] for name in GATHER_FIRST], _group_rows(GATHER_FIRST))
    xs, mems, tgt = x[0], mem[0], loss_target[0]
    behind = lambda gain, token: gain + token[0, 0]

    def placed(shard, full_shape, col):
        return lax.dynamic_update_slice(jnp.zeros(full_shape, F32), shard, (0, col)).reshape(-1, 128)

    sw = jnp.concatenate([
        placed(conv_w[0], (CONV_K, CONV_WIDTH), 128 * chip),
        placed(w_af[0], (GLA_LOWRANK, GLA_K_TOTAL), 64 * chip),
        placed(w_ab[0], (GLA_LOWRANK, GLA_K_TOTAL), 64 * chip),
    ], axis=0)
    sw = jnp.pad(sw, ((0, SMALL_ROWS - sw.shape[0]), (0, 0))) * (ci == 0).astype(F32)
    sw = _sum_small(sw, mix_norm)

    first_send, first_recv, pack_first, land_first, first_token = _gather_start(pack_first, sw, name="gather_first_start")
    rest_send, rest_recv, pack_rest, land_rest, rest_token = _gather_start(pack_rest, first_token, name="gather_rest_start")
    h1 = _rms_fwd(xs, behind(mix_norm, rest_token), name="norm_mix")
    pack_first, land_first = _gather_wait(first_send, first_recv, pack_first, land_first, h1, name="gather_first_wait")
    got_first = _gather_spread(land_first, name="gather_first_spread")

    def whole(got, off, rows):
        return got[:, off : off + rows].reshape(N_CHIPS * rows, D_MODEL)

    w_in_t = whole(got_first, 0, MATS["w_in"][0])
    w_za = jnp.concatenate([w_in_t[0:1536], w_in_t[2560:3072]], axis=0)
    w_zb = jnp.concatenate([w_in_t[1536:2560], w_in_t[3072:W_IN_COLS], jnp.zeros((ZB_COLS - 1056, D_MODEL), _CD)], axis=0)
    conv_w_full = sw[0:12].reshape(CONV_K, CONV_WIDTH)
    w_af_full = sw[12:44].reshape(GLA_LOWRANK, GLA_K_TOTAL)
    w_ab_full = sw[44:76].reshape(GLA_LOWRANK, GLA_K_TOTAL)
    waf_p = jnp.pad(w_af_full, ((0, 128 - GLA_LOWRANK), (0, 0))).astype(_CD)
    wab_p = jnp.pad(w_ab_full, ((GLA_LOWRANK, 128 - 2 * GLA_LOWRANK), (0, 0))).astype(_CD)

    z_a, z_b = _mm_two(h1, w_za, w_zb, name="proj_in")
    b_f, b_b = _gate_fwd(z_b, waf_p, wab_p, b_af, b_ab, name="gates")
    o_f, st_f, o_b, st_b = _gla_fwd(z_b, b_f, b_b, name="gla_scan")
    y = _mix_fwd(z_a, o_f, o_b, conv_w_full, conv_norm, gla_norm, name="mix_out")
    pack_rest, land_rest = _gather_wait(rest_send, rest_recv, pack_rest, land_rest, y, name="gather_rest_wait")
    gathered = _gather_spread(land_rest, name="gather_rest_spread")
    wt, off = {}, 0
    for name in GATHER_REST:
        wt[name] = (gathered, off, MATS[name][0]) if name in IN_PLACE else whole(gathered, off, MATS[name][0])
        off += MATS[name][0]
    x1, hx = _mm_rows(y, wt["w_out"], mode="nn", name="proj_out", rows=(xs,), vecs=(xa_norm,), out_rows=(F32, _CD),
                      epilogue=_ep_residual_norm, tm=1024)
    hmem = _rms_fwd(mems, mem_norm, name="norm_mem")
    kv = _mm(hmem, wt["w_xkv"], mode="nt", name="proj_xkv", out_dtypes=(_CD,))
    qx, ox, hm, x2 = _xattn_block(hx, wt["w_xq"], kv, wt["w_xo"], x1, mlp_norm, name="xattn_block")
    act, relu_u = _mm(hm, wt["w_up"], mode="nt", name="mlp_up", out_dtypes=(_CD, _CD), tm=2048,
                      epilogue=lambda acc: (jnp.square(jnp.maximum(acc, 0.0)), jnp.maximum(acc, 0.0)))
    dx3, dx3_lo, loss_part, g_final_norm = _mm_rows(
        act, wt["w_down"], mode="nn", name="mlp_down", rows=(x2, tgt), vecs=(final_norm.reshape(1, D_MODEL),),
        out_rows=(F32, _CD), out_vecs=(128, D_MODEL), epilogue=_ep_loss)

    grads_t = {}

    def start_group(names, tag):
        rows = _group_rows(names)
        g = jnp.stack([_pack([grads_t[name][a * MATS[name][0] : (a + 1) * MATS[name][0]] for name in names], rows) for a in range(N_CHIPS)])
        return _scatter_start(g.astype(_TD), g, name="grads_" + tag + "_start")

    def finish_group(state, after, tag):
        send_sems, recv_sems, g_lo, g, land_lo, land_f, _ = state
        g_lo, g, land_lo, land_f = _scatter_wait(send_sems, recv_sems, g_lo, g, land_lo, land_f, after, name="grads_" + tag + "_wait")
        return _scatter_sum(g, land_lo, land_f, where, name="grads_" + tag + "_sum")

    def new_packs(names):
        shape = (N_CHIPS, _group_rows(names), D_MODEL)
        return lax.empty(shape, F32), lax.empty(shape, _TD)

    def grad_into(packs, names, which, a, b, name):
        off = sum(MATS[other][0] for other in names[: names.index(which)])
        return _mm_tn_into(a, b, packs, rows=MATS[which][0], off=off, name=name)

    du = _mm(dx3_lo, wt["w_down"], mode="nt", name="mlp_down_dx", out_dtypes=(_CD,), extras=(relu_u,), tm=2048,
             epilogue=lambda acc, rr: (acc * (2.0 * rr.astype(F32)),))
    packs = new_packs(GRAD_GROUPS[0])
    packs = grad_into(packs, GRAD_GROUPS[0], "w_down", act, dx3_lo, "mlp_down_dw")
    packs = grad_into(packs, GRAD_GROUPS[0], "w_up", du, hm, "mlp_up_dw")
    mlp_state = _scatter_start(packs[1], packs[0], name="grads_mlp_start")
    dx2, dx2_lo, g_mlp_norm = _mm_rows(
        du, wt["w_up"], mode="nn", name="mlp_up_dx", rows=(x2, dx3), vecs=(behind(mlp_norm, mlp_state[-1]),),
        out_rows=(F32, _CD), out_vecs=(D_MODEL,), epilogue=_ep_norm_bwd)
    packs = new_packs(GRAD_GROUPS[1])
    packs = grad_into(packs, GRAD_GROUPS[1], "w_xo", ox, dx2_lo, "proj_xo_dw")
    dqx, dkv = _xattn_bwd(qx, kv, dx2_lo, wt["w_xo"], name="xattn_bwd")
    packs = grad_into(packs, GRAD_GROUPS[1], "w_xq", hx, dqx, "proj_xq_dw")
    dx1, dx1_lo, g_xa_norm = _mm_rows(
        dqx, wt["w_xq"], mode="nt", name="proj_xq_dx", rows=(x1, dx2), vecs=(xa_norm,),
        out_rows=(F32, _CD), out_vecs=(D_MODEL,), epilogue=_ep_norm_bwd, tm=1024)
    dkv_lo = dkv.astype(_CD)
    packs = grad_into(packs, GRAD_GROUPS[1], "w_xkv", dkv_lo, hmem, "proj_xkv_dw")
    dhmem = _mm(dkv_lo, wt["w_xkv"], mode="nn", name="proj_xkv_dx")
    g_mem_norm = _rms_gain_grad(mems, dhmem, name="norm_mem_bwd")
    dy = _mm(dx1_lo, wt["w_out"], mode="nt", name="proj_out_dx")
    packs = grad_into(packs, GRAD_GROUPS[1], "w_out", y, dx1_lo, "proj_out_dw")
    attn_state = _scatter_start(packs[1], packs[0], name="grads_attn_start")
    dz_a, do, g_conv_w, g_conv_norm, g_gla_norm = _mix_bwd(z_a, o_f, o_b, dy, conv_w_full, behind(conv_norm, attn_state[-1]), gla_norm, name="mix_out_bwd")
    dqkv_f, db_f, dqkv_b, db_b = _gla_bwd(z_b, b_f, b_b, do, st_f, st_b, name="gla_scan_bwd")
    dz_b, g_waf_p, g_wab_p, g_b_af, g_b_ab = _gate_bwd(z_b, waf_p, wab_p, b_af, b_ab, db_f, db_b, dqkv_f, dqkv_b, name="gates_bwd")
    g_za = _mm_tn(dz_a, h1, name="proj_in_a_dw")
    g_zb = _mm_tn(dz_b, h1, name="proj_in_b_dw")
    grads_t["w_in"] = jnp.concatenate([g_za[0:1536], g_zb[0:1024], g_za[1536:2048], g_zb[1024:1056]], axis=0)
    in_state = start_group(GRAD_GROUPS[2], "in")
    grad_x, g_mix_norm = _mm_rows(
        dz_a, w_za, mode="nn", name="proj_in_dx", more=((dz_b, w_zb),), rows=(xs, dx1), vecs=(behind(mix_norm, in_state[-1]),),
        out_rows=(F32,), out_vecs=(D_MODEL,), epilogue=_ep_norm_bwd)

    half_mlp = finish_group(mlp_state, grad_x, "mlp")
    half_attn = finish_group(attn_state, half_mlp, "attn")
    half_in = finish_group(in_state, half_attn, "in")
    shard_rows = {}
    for names, rows in zip(GRAD_GROUPS, _swap_all([half_mlp, half_attn, half_in], name="shards_to_sibling")):
        off = 0
        for name in names:
            shard_rows[name] = (rows, off)
            off += MATS[name][0]

    small_vals = dict(mix_norm=g_mix_norm, conv_norm=g_conv_norm, b_af=g_b_af, b_ab=g_b_ab, gla_norm=g_gla_norm, xa_norm=g_xa_norm,
                      mem_norm=g_mem_norm, mlp_norm=g_mlp_norm, final_norm=g_final_norm, conv_w=g_conv_w,
                      w_af=g_waf_p[0:GLA_LOWRANK], w_ab=g_wab_p[GLA_LOWRANK : 2 * GLA_LOWRANK], loss=loss_part)
    small = jnp.concatenate([small_vals[name].reshape(-1, 128) for name, _ in SMALL], axis=0)
    small = _sum_small(jnp.pad(small, ((0, SMALL_ROWS - small.shape[0]), (0, 0))), loss_part)
    g_small, off = {}, 0
    for name, n in SMALL:
        g_small[name] = small[off : off + n // 128]
        off += n // 128
    loss = g_small["loss"][0, 0]
    g_small["conv_w"] = lax.dynamic_slice(g_small["conv_w"].reshape(CONV_K, CONV_WIDTH), (0, 128 * chip), (CONV_K, 128))
    g_small["w_af"] = lax.dynamic_slice(g_small["w_af"].reshape(GLA_LOWRANK, GLA_K_TOTAL), (0, 64 * chip), (GLA_LOWRANK, 64))
    g_small["w_ab"] = lax.dynamic_slice(g_small["w_ab"].reshape(GLA_LOWRANK, GLA_K_TOTAL), (0, 64 * chip), (GLA_LOWRANK, 64))

    names = ["mix_norm", "w_in", "conv_w", "conv_norm", "w_af", "b_af", "w_ab", "b_ab", "gla_norm", "w_out", "xa_norm", "mem_norm",
             "w_xq", "w_xkv", "w_xo", "mlp_norm", "w_up", "w_down", "final_norm"]
    big_names = list(MATS)
    as2d = lambda a: a.reshape(1, -1) if a.ndim == 1 else a.reshape(a.shape[-2:])
    grads, deltas, new_m, new_v = {}, {}, {}, {}
    for name in big_names:
        rows, off = shard_rows[name]
        wmv = [as2d(given[name]), as2d(given["m_" + name]), as2d(given["v_" + name])]
        as_stored = name == "w_in"
        if as_stored:
            wmv = [a.T for a in wmv]
        res = _adamw(*wmv, rows, off, transposed=MATS[name][1] and not as_stored, name="adamw_" + name)
        grads[name], deltas[name], new_m[name], new_v[name] = [a.T for a in res] if as_stored else res
    small_names = [name for name in names if name not in big_names]
    groups = []
    for name in small_names:
        grads[name] = g_small[name].reshape(as2d(given[name]).shape)
        groups.append((as2d(given[name]), grads[name], as2d(given["m_" + name]), as2d(given["v_" + name])))
    for name, res in zip(small_names, _adamw_small(groups, name="adamw_small")):
        deltas[name], new_m[name], new_v[name] = res

    like = lambda name, a: a.reshape(given[name].shape)
    return (loss, grad_x[None], *[like(n, grads[n]) for n in names], *[like(n, deltas[n]) for n in names],
            *[like(n, new_m[n]) for n in names], *[like(n, new_v[n]) for n in names])
```

```python
import jax
import jax.numpy as jnp
from jax import lax
from jax.experimental import pallas as pl
from jax.experimental.pallas import tpu as pltpu

F32 = jnp.float32
BF16 = jnp.bfloat16
_CD = jnp.bfloat16
_TD = jnp.bfloat16

D_MODEL = 1024
N_MEM = 256
CONV_WIDTH = 512
CONV_GROUP = 64
CONV_K = 3
GLA_HEADS = 4
GLA_DK = 64
GLA_DV = 128
GLA_K_TOTAL = 256
GLA_V_TOTAL = 512
GLA_LOWRANK = 16
GLA_GATE_SCALE = 1.0 / 16.0
GLA_CHUNK = 64
XA_HEADS = 4
XA_HEAD_DIM = 256
D_FF = 4096
EPS = 1e-6
W_IN_COLS = 3104
ZA_COLS = 2048
ZB_COLS = 1152
LR_COL = 1024

ADAM_LR = 0.001
ADAM_B1 = 0.9
ADAM_B2 = 0.999
ADAM_EPS = 1e-08
ADAM_WD = 0.01
ADAM_STEP = 10

N_CHIPS = 4
SMALL_ROWS = 128

_TS = 512
_VMEM = 63 * 1024 * 1024
MESH = pl.DeviceIdType.MESH
ANY = pl.BlockSpec(memory_space=pl.ANY)


def _cp(sem=None, **kw):
    return pltpu.CompilerParams(dimension_semantics=sem, vmem_limit_bytes=_VMEM, **kw)


def _dot(a, b):
    return jnp.dot(a.astype(_CD), b.astype(_CD), preferred_element_type=F32)


def _dot_nt(a, b):
    return lax.dot_general(a.astype(_CD), b.astype(_CD), (((1,), (1,)), ((), ())), preferred_element_type=F32)


def _dot_tn(a, b):
    return lax.dot_general(a.astype(_CD), b.astype(_CD), (((0,), (0,)), ((), ())), preferred_element_type=F32)


def _dot_split(x, ones):
    hi = x.astype(BF16)
    r = x - hi.astype(F32)
    mid = r.astype(BF16)
    lo = (r - mid.astype(F32)).astype(BF16)
    d = lambda p: jnp.dot(p, ones, preferred_element_type=F32)
    return d(hi) + d(mid) + d(lo)


def _pick(n, cands=(1024, 640, 512, 256, 128)):
    for t in cands:
        if n % t == 0:
            return t
    return n


def _rows(s, light=False, times=2):
    return min(times * _TS if light else _TS, s)


def _sigmoid(v):
    e = jnp.exp(-jnp.abs(v))
    return jnp.where(v >= 0, 1.0 / (1.0 + e), e / (1.0 + e))


def _mm(a, b, *, mode, name, out_dtypes=(F32,), extras=(), epilogue=None, tm=None, tn=None, tk=None):
    m, k = a.shape
    placed = isinstance(b, tuple)
    if placed:
        b, b_off, tn = b
        assert mode == "nt" and b_off % tn == 0
        n = N_CHIPS * tn
    else:
        n = b.shape[1] if mode == "nn" else b.shape[0]
    tm = min(m, tm or 1024)
    tn = tn or _pick(n)
    tk = tk or _pick(k)
    nk = k // tk
    n_ex, n_out = len(extras), len(out_dtypes)

    def body(*refs):
        a_ref, b_ref = refs[:2]
        ex = refs[2 : 2 + n_ex]
        outs = refs[2 + n_ex : 2 + n_ex + n_out]
        part = _dot(a_ref[...], b_ref[...]) if mode == "nn" else _dot_nt(a_ref[...], b_ref[...])

        def finish(acc):
            res = epilogue(acc, *[e[...] for e in ex]) if epilogue else (acc,)
            for o, r in zip(outs, res):
                o[...] = r.astype(o.dtype)

        if nk == 1:
            finish(part)
        else:
            acc_ref = refs[-1]
            kk = pl.program_id(2)

            @pl.when(kk == 0)
            def _():
                acc_ref[...] = part

            @pl.when(kk > 0)
            def _():
                acc_ref[...] += part

            @pl.when(kk == nk - 1)
            def _():
                finish(acc_ref[...])

    if placed:
        b_spec = pl.BlockSpec((None, tn, tk), lambda i, j, kk: (j, b_off // tn, kk))
    else:
        b_spec = pl.BlockSpec((tk, tn), lambda i, j, kk: (kk, j)) if mode == "nn" else pl.BlockSpec((tn, tk), lambda i, j, kk: (j, kk))
    tile = pl.BlockSpec((tm, tn), lambda i, j, kk: (i, j))
    out = pl.pallas_call(
        body,
        name=name,
        grid=(m // tm, n // tn, nk),
        in_specs=[pl.BlockSpec((tm, tk), lambda i, j, kk: (i, kk)), b_spec] + [tile] * n_ex,
        out_specs=[tile] * n_out,
        out_shape=[jax.ShapeDtypeStruct((m, n), dt) for dt in out_dtypes],
        scratch_shapes=[pltpu.VMEM((tm, tn), F32)] if nk > 1 else [],
        compiler_params=_cp(("parallel", "parallel", "arbitrary")),
    )(a, b, *extras)
    return out[0] if n_out == 1 else out


def _mm_two(a, b1, b2, *, name, tm=512):
    m, k = a.shape
    tm = min(m, tm)

    def body(a_ref, b1_ref, b2_ref, o1_ref, o2_ref):
        av = a_ref[...]
        o1_ref[...] = _dot_nt(av, b1_ref[...])
        o2_ref[...] = _dot_nt(av, b2_ref[...])

    whole = lambda arr: pl.BlockSpec(arr.shape, lambda i: (0, 0))
    rows = lambda n: pl.BlockSpec((tm, n), lambda i: (i, 0))
    return pl.pallas_call(
        body,
        name=name,
        grid=(m // tm,),
        in_specs=[rows(k), whole(b1), whole(b2)],
        out_specs=[rows(b1.shape[0]), rows(b2.shape[0])],
        out_shape=[jax.ShapeDtypeStruct((m, b1.shape[0]), F32), jax.ShapeDtypeStruct((m, b2.shape[0]), F32)],
        compiler_params=_cp(("parallel",)),
    )(a, b1, b2)


def _mm_tn(a, b, *, name):
    s, m = a.shape
    n = b.shape[1]
    cap = max(128, (1 << 20) // n)
    tm = _pick(m, tuple(t for t in (512, 640, 384, 256, 128) if t <= max(cap, 128)))
    ts = min(s, 1 << (((1 << 22) // n).bit_length() - 1))
    ns = s // ts

    def body(a_ref, b_ref, o_ref):
        part = _dot_tn(a_ref[...], b_ref[...])
        if ns == 1:
            o_ref[...] = part
        else:
            ss = pl.program_id(1)

            @pl.when(ss == 0)
            def _():
                o_ref[...] = part

            @pl.when(ss > 0)
            def _():
                o_ref[...] += part

    return pl.pallas_call(
        body,
        name=name,
        grid=(m // tm, ns),
        in_specs=[pl.BlockSpec((ts, tm), lambda i, ss: (ss, i)), pl.BlockSpec((ts, n), lambda i, ss: (ss, 0))],
        out_specs=pl.BlockSpec((tm, n), lambda i, ss: (i, 0)),
        out_shape=jax.ShapeDtypeStruct((m, n), F32),
        compiler_params=_cp(("parallel", "arbitrary")),
    )(a, b)


def _mm_tn_into(a, b, packs, *, rows, off, name):
    s, m = a.shape
    n = b.shape[1]
    tm = 1024 if rows % 1024 == 0 and s >= 4096 else 512
    tr = min(tm, rows)
    per, chips = rows // tr, tm // tr
    ts = min(s, (1 << (((1 << 22) // n).bit_length() - 1)) * 512 // tm)
    ns = s // ts

    def body(a_ref, b_ref, f_in, lo_in, f_ref, lo_ref):
        part = _dot_tn(a_ref[...], b_ref[...])
        pieces = [part[c * tr : (c + 1) * tr] for c in range(chips)]
        if ns == 1:
            for c, p in enumerate(pieces):
                f_ref[c] = p
                lo_ref[c] = p.astype(lo_ref.dtype)
        else:
            ss = pl.program_id(1)

            @pl.when(ss == 0)
            def _():
                for c, p in enumerate(pieces):
                    f_ref[c] = p

            @pl.when(ss > 0)
            def _():
                for c, p in enumerate(pieces):
                    f_ref[c] += p

            @pl.when(ss == ns - 1)
            def _():
                lo_ref[...] = f_ref[...].astype(lo_ref.dtype)

    spec = pl.BlockSpec((chips, tr, n), lambda i, ss: (i // per, off // tr + i % per, 0))
    return pl.pallas_call(
        body,
        name=name,
        grid=(m // tm, ns),
        in_specs=[pl.BlockSpec((ts, tm), lambda i, ss: (ss, i)), pl.BlockSpec((ts, n), lambda i, ss: (ss, 0)), ANY, ANY],
        out_specs=[spec, spec],
        out_shape=[jax.ShapeDtypeStruct(p.shape, p.dtype) for p in packs],
        input_output_aliases={2: 0, 3: 1},
        compiler_params=_cp(("parallel", "arbitrary")),
    )(a, b, *packs)


def _mm_rows(a, b, *, mode, name, more=(), rows=(), vecs=(), out_rows=(), out_vecs=(), epilogue, tm=512):
    m, k = a.shape
    placed = isinstance(b, tuple)
    if placed:
        b, b_off, b_rows = b
        assert mode == "nn" and b_off % b_rows == 0 and k == N_CHIPS * b_rows
        n = b.shape[2]
        b_spec = pl.BlockSpec((N_CHIPS, b_rows, n), lambda i: (0, b_off // b_rows, 0))
    else:
        n = b.shape[1] if mode == "nn" else b.shape[0]
        b_spec = pl.BlockSpec(b.shape, lambda i: (0, 0))
    tm = min(m, tm)
    parts = 2 if tm % 256 == 0 else 1
    n_m, n_r, n_v, n_or, n_ov = 2 * len(more), len(rows), len(vecs), len(out_rows), len(out_vecs)

    def body(*refs):
        a_ref, b_ref = refs[:2]
        m_refs = refs[2 : 2 + n_m]
        rest = refs[2 + n_m :]
        r_refs = rest[:n_r]
        v_refs = rest[n_r : n_r + n_v]
        or_refs = rest[n_r + n_v : n_r + n_v + n_or]
        ov_refs = rest[n_r + n_v + n_or :]
        res_vecs = None
        bv = b_ref[...].reshape(k, n) if placed else b_ref[...]
        for p in range(parts):
            rs = slice(p * tm // parts, (p + 1) * tm // parts)
            acc = _dot(a_ref[rs, :], bv) if mode == "nn" else _dot_nt(a_ref[rs, :], bv)
            for a2_ref, b2_ref in zip(m_refs[0::2], m_refs[1::2]):
                acc = acc + _dot(a2_ref[rs, :], b2_ref[...])
            res_rows, part_vecs = epilogue(acc, [r[rs, :] for r in r_refs], [v[...] for v in v_refs])
            for o, r in zip(or_refs, res_rows):
                o[rs, :] = r.astype(o.dtype)
            res_vecs = part_vecs if res_vecs is None else [s + t for s, t in zip(res_vecs, part_vecs)]
        if n_ov:
            first = pl.program_id(0) == 0

            @pl.when(first)
            def _():
                for o, r in zip(ov_refs, res_vecs):
                    o[...] = r

            @pl.when(jnp.logical_not(first))
            def _():
                for o, r in zip(ov_refs, res_vecs):
                    o[...] += r

    tile = pl.BlockSpec((tm, n), lambda i: (i, 0))
    whole = lambda arr: pl.BlockSpec(arr.shape, lambda i: (0, 0))
    vec = lambda w: pl.BlockSpec((1, w), lambda i: (0, 0))
    out = pl.pallas_call(
        body,
        name=name,
        grid=(m // tm,),
        in_specs=[pl.BlockSpec((tm, k), lambda i: (i, 0)), b_spec]
        + [spec for a2, b2 in more for spec in (pl.BlockSpec((tm, a2.shape[1]), lambda i: (i, 0)), whole(b2))]
        + [tile] * n_r + [vec(v.shape[1]) for v in vecs],
        out_specs=[tile] * n_or + [vec(w) for w in out_vecs],
        out_shape=[jax.ShapeDtypeStruct((m, n), dt) for dt in out_rows] + [jax.ShapeDtypeStruct((1, w), F32) for w in out_vecs],
        compiler_params=_cp(("arbitrary",) if n_ov else ("parallel",)),
    )(a, b, *[x for pair in more for x in pair], *rows, *vecs)
    return out


def _ep_residual_norm(acc, rows, vecs):
    x = acc + rows[0]
    r = lax.rsqrt(jnp.mean(x * x, axis=-1, keepdims=True) + EPS)
    return [x, x * r * vecs[0]], []


def _ep_norm_bwd(acc, rows, vecs):
    dy = acc
    for extra in rows[2:]:
        dy = dy + extra
    x, dres = rows[0], rows[1]
    r = lax.rsqrt(jnp.mean(x * x, axis=-1, keepdims=True) + EPS)
    xh = x * r
    dxh = dy * vecs[0]
    dx = r * (dxh - xh * jnp.mean(dxh * xh, axis=-1, keepdims=True)) + dres
    return [dx, dx], [jnp.sum(dy * xh, axis=0, keepdims=True)]


def _ep_loss(acc, rows, vecs):
    x = acc + rows[0]
    d = x.shape[-1]
    r = lax.rsqrt(jnp.mean(x * x, axis=-1, keepdims=True) + EPS)
    xh = x * r
    err = xh * vecs[0] - rows[1]
    loss = jnp.zeros((1, 128), F32) + 0.5 * jnp.sum(jnp.mean(err * err, axis=-1, keepdims=True))
    dy = err * (1.0 / d)
    dxh = dy * vecs[0]
    dx = r * (dxh - xh * jnp.mean(dxh * xh, axis=-1, keepdims=True))
    return [dx, dx], [loss, jnp.sum(dy * xh, axis=0, keepdims=True)]


def _rms_fwd(x, g, *, name):
    s, d = x.shape
    ts = _rows(s, light=True, times=4)

    def body(x_ref, g_ref, o_ref):
        xf = x_ref[...]
        r = lax.rsqrt(jnp.mean(xf * xf, axis=-1, keepdims=True) + EPS)
        o_ref[...] = (xf * r * g_ref[...]).astype(o_ref.dtype)

    return pl.pallas_call(
        body,
        name=name,
        grid=(s // ts,),
        in_specs=[pl.BlockSpec((ts, d), lambda i: (i, 0)), pl.BlockSpec((1, d), lambda i: (0, 0))],
        out_specs=pl.BlockSpec((ts, d), lambda i: (i, 0)),
        out_shape=jax.ShapeDtypeStruct((s, d), _CD),
        compiler_params=_cp(("parallel",)),
    )(x, g)


def _rms_gain_grad(x, dy, *, name):
    s, d = x.shape
    ts = _rows(s)

    def body(x_ref, dy_ref, dg_ref):
        xf = x_ref[...]
        r = lax.rsqrt(jnp.mean(xf * xf, axis=-1, keepdims=True) + EPS)
        part = jnp.sum(dy_ref[...] * (xf * r), axis=0, keepdims=True)

        @pl.when(pl.program_id(0) == 0)
        def _():
            dg_ref[...] = part

        @pl.when(pl.program_id(0) > 0)
        def _():
            dg_ref[...] += part

    tile = pl.BlockSpec((ts, d), lambda i: (i, 0))
    return pl.pallas_call(
        body,
        name=name,
        grid=(s // ts,),
        in_specs=[tile, tile],
        out_specs=pl.BlockSpec((1, d), lambda i: (0, 0)),
        out_shape=jax.ShapeDtypeStruct((1, d), F32),
        compiler_params=_cp(("arbitrary",)),
    )(x, dy)


def _chunk_scan(v, row_in_chunk, suffix):
    t = v.shape[0]
    step = 1
    while step < GLA_CHUNK:
        if suffix:
            v = v + jnp.where(row_in_chunk < GLA_CHUNK - step, pltpu.roll(v, t - step, 0), 0.0)
        else:
            v = v + jnp.where(row_in_chunk >= step, pltpu.roll(v, step, 0), 0.0)
        step *= 2
    return v


def _gate_pre(lr, w_ref, b_ref):
    return _dot(lr, w_ref[...]) + b_ref[...]


def _gate_fwd(z, waf, wab, baf, bab, *, name):
    s = z.shape[0]
    ts = _rows(s, light=True, times=4)

    def body(lr_ref, waf_ref, wab_ref, baf_ref, bab_ref, bf_ref, bb_ref):
        lr = lr_ref[...]
        ric = lax.broadcasted_iota(jnp.int32, (ts, GLA_K_TOTAL), 0) & (GLA_CHUNK - 1)
        for w_ref, b_ref, o_ref, suffix in ((waf_ref, baf_ref, bf_ref, False), (wab_ref, bab_ref, bb_ref, True)):
            pre = _gate_pre(lr, w_ref, b_ref)
            la = (jnp.minimum(pre, 0.0) - jnp.log(1.0 + jnp.exp(-jnp.abs(pre)))) * GLA_GATE_SCALE
            o_ref[...] = _chunk_scan(la, ric, suffix)

    wspec = pl.BlockSpec((128, GLA_K_TOTAL), lambda i: (0, 0))
    bspec = pl.BlockSpec((1, GLA_K_TOTAL), lambda i: (0, 0))
    tile = pl.BlockSpec((ts, GLA_K_TOTAL), lambda i: (i, 0))
    return pl.pallas_call(
        body,
        name=name,
        grid=(s // ts,),
        in_specs=[pl.BlockSpec((ts, 128), lambda i: (i, LR_COL // 128)), wspec, wspec, bspec, bspec],
        out_specs=[tile, tile],
        out_shape=[jax.ShapeDtypeStruct((s, GLA_K_TOTAL), F32)] * 2,
        compiler_params=_cp(("parallel",)),
    )(z, waf, wab, baf, bab)


def _gate_bwd(z, waf, wab, baf, bab, dbf, dbb, dqkv_f, dqkv_b, *, name):
    s = z.shape[0]
    ts = _rows(s, light=True)

    def body(lr_ref, waf_ref, wab_ref, baf_ref, bab_ref, dbf_ref, dbb_ref, gf_ref, gb_ref, dzb_ref, dwf_ref, dwb_ref, dbaf_ref, dbab_ref):
        lr = lr_ref[...]
        ric = lax.broadcasted_iota(jnp.int32, (ts, GLA_K_TOTAL), 0) & (GLA_CHUNK - 1)
        first = pl.program_id(0) == 0
        dlr = None
        for w_ref, b_ref, db_ref, dw_ref, dbias_ref, suffix in (
            (waf_ref, baf_ref, dbf_ref, dwf_ref, dbaf_ref, True),
            (wab_ref, bab_ref, dbb_ref, dwb_ref, dbab_ref, False),
        ):
            pre = _gate_pre(lr, w_ref, b_ref)
            dla = _chunk_scan(db_ref[...], ric, suffix)
            dpre = dla * GLA_GATE_SCALE * _sigmoid(-pre)
            part = _dot_nt(dpre, w_ref[...])
            dlr = part if dlr is None else dlr + part
            dw = _dot_tn(lr, dpre)
            dbias = jnp.sum(dpre, axis=0, keepdims=True)

            @pl.when(first)
            def _():
                dw_ref[...] = dw
                dbias_ref[...] = dbias

            @pl.when(jnp.logical_not(first))
            def _():
                dw_ref[...] += dw
                dbias_ref[...] += dbias

        dqkv = gf_ref[...].astype(F32) + gb_ref[...].astype(F32)
        dzb_ref[...] = jnp.concatenate([dqkv, dlr], axis=1).astype(dzb_ref.dtype)

    wspec = pl.BlockSpec((128, GLA_K_TOTAL), lambda i: (0, 0))
    bspec = pl.BlockSpec((1, GLA_K_TOTAL), lambda i: (0, 0))
    tile = pl.BlockSpec((ts, GLA_K_TOTAL), lambda i: (i, 0))
    wide = pl.BlockSpec((ts, 2 * GLA_K_TOTAL + GLA_V_TOTAL), lambda i: (i, 0))
    return pl.pallas_call(
        body,
        name=name,
        grid=(s // ts,),
        in_specs=[pl.BlockSpec((ts, 128), lambda i: (i, LR_COL // 128)), wspec, wspec, bspec, bspec, tile, tile, wide, wide],
        out_specs=[pl.BlockSpec((ts, ZB_COLS), lambda i: (i, 0)), wspec, wspec, bspec, bspec],
        out_shape=[
            jax.ShapeDtypeStruct((s, ZB_COLS), _CD),
            jax.ShapeDtypeStruct((128, GLA_K_TOTAL), F32),
            jax.ShapeDtypeStruct((128, GLA_K_TOTAL), F32),
            jax.ShapeDtypeStruct((1, GLA_K_TOTAL), F32),
            jax.ShapeDtypeStruct((1, GLA_K_TOTAL), F32),
        ],
        compiler_params=_cp(("arbitrary",)),
    )(z, waf, wab, baf, bab, dbf, dbb, dqkv_f, dqkv_b)


def _gla_masks(rev):
    lane_head = lax.broadcasted_iota(jnp.int32, (1, GLA_K_TOTAL), 1) >> 6
    head_masks = [lane_head == h for h in range(GLA_HEADS)]
    t = lax.broadcasted_iota(jnp.int32, (GLA_HEADS * GLA_CHUNK, GLA_CHUNK), 0) & (GLA_CHUNK - 1)
    u = lax.broadcasted_iota(jnp.int32, (GLA_HEADS * GLA_CHUNK, GLA_CHUNK), 1)
    tri = (u > t) if rev else (u <= t)
    row = lax.broadcasted_iota(jnp.int32, (GLA_CHUNK, GLA_K_TOTAL), 0)
    total_row = row == (0 if rev else GLA_CHUNK - 1)
    return head_masks, tri, total_row


def _spread(a, head_masks):
    return jnp.concatenate([jnp.where(m, a, 0.0) for m in head_masks], axis=0)


def _stack(a):
    return jnp.concatenate([a[:, GLA_DV * h : GLA_DV * (h + 1)] for h in range(GLA_HEADS)], axis=0)


def _unstack(a):
    return jnp.concatenate([a[GLA_CHUNK * h : GLA_CHUNK * (h + 1)] for h in range(GLA_HEADS)], axis=1)


def _collect(a, head_masks):
    out = None
    for h, m in enumerate(head_masks):
        part = jnp.where(m, a[GLA_CHUNK * h : GLA_CHUNK * (h + 1)], 0.0)
        out = part if out is None else out + part
    return out


def _gla_chunk_terms(q_ref, k_ref, v_ref, b_ref, rows, head_masks, tri, total_row):
    q = q_ref[rows, :] * (GLA_DK**-0.5)
    k = k_ref[rows, :]
    v = v_ref[rows, :]
    b = b_ref[rows, :]
    eb = jnp.exp(b)
    enb = jnp.exp(-b)
    g = jnp.sum(jnp.where(total_row, b, 0.0), axis=0, keepdims=True)
    egb = jnp.exp(g - b)
    qt = q * eb
    kt = k * enb
    kh = k * egb
    q_heads = _spread(qt, head_masks)
    attn = jnp.where(tri, _dot_nt(q_heads, kt), 0.0)
    return v, eb, enb, egb, jnp.exp(g), qt, kt, kh, q_heads, attn


def _gla_specs(s, tb, rev_blocks):
    nb = s // tb
    rb = (lambda i: nb - 1 - i) if rev_blocks else (lambda i: i)
    q_spec = pl.BlockSpec((tb, GLA_K_TOTAL), lambda i: (rb(i), 0))
    k_spec = pl.BlockSpec((tb, GLA_K_TOTAL), lambda i: (rb(i), 1))
    v_spec = pl.BlockSpec((tb, GLA_V_TOTAL), lambda i: (rb(i), 1))
    b_spec = pl.BlockSpec((tb, GLA_K_TOTAL), lambda i: (rb(i), 0))
    o_spec = pl.BlockSpec((tb, GLA_V_TOTAL), lambda i: (rb(i), 0))
    st_spec = pl.BlockSpec((tb // GLA_CHUNK, GLA_DV, GLA_K_TOTAL), lambda i: (rb(i), 0, 0))
    return nb, q_spec, k_spec, v_spec, b_spec, o_spec, st_spec


def _gla_fwd_chunk(cidx, q_ref, k_ref, v_ref, b_ref, o_ref, sv_ref, st_ref, masks):
    head_masks, tri, total_row = masks
    rows = pl.ds(pl.multiple_of(cidx * GLA_CHUNK, GLA_CHUNK), GLA_CHUNK)
    v, _, _, _, eg, _, _, kh, q_heads, attn = _gla_chunk_terms(q_ref, k_ref, v_ref, b_ref, rows, head_masks, tri, total_row)
    o = jnp.concatenate(
        [_dot(attn[GLA_CHUNK * h : GLA_CHUNK * (h + 1)], v[:, GLA_DV * h : GLA_DV * (h + 1)]) for h in range(GLA_HEADS)], axis=1
    )
    st = st_ref[...]
    o_ref[rows, :] = o + _unstack(_dot_nt(q_heads, st))
    sv_ref[cidx] = st
    st_ref[...] = st * eg + _dot_tn(_stack(v), _spread(kh, head_masks))


def _gla_fwd(z, b_f, b_b, *, name):
    s = z.shape[0]
    tb = _rows(s)
    cpb = tb // GLA_CHUNK
    nb, qf, kf, vf, bf, of, sf = _gla_specs(s, tb, False)
    _, qr, kr, vr, br, orr, sr = _gla_specs(s, tb, True)

    def body(qf_ref, kf_ref, vf_ref, bf_ref, qr_ref, kr_ref, vr_ref, br_ref, of_ref, svf_ref, or_ref, svr_ref, stf_ref, str_ref):
        masks_f, masks_r = _gla_masks(False), _gla_masks(True)

        @pl.when(pl.program_id(0) == 0)
        def _():
            stf_ref[...] = jnp.zeros_like(stf_ref)
            str_ref[...] = jnp.zeros_like(str_ref)

        def chunk(ci, carry):
            _gla_fwd_chunk(ci, qf_ref, kf_ref, vf_ref, bf_ref, of_ref, svf_ref, stf_ref, masks_f)
            _gla_fwd_chunk(cpb - 1 - ci, qr_ref, kr_ref, vr_ref, br_ref, or_ref, svr_ref, str_ref, masks_r)
            return carry

        lax.fori_loop(0, cpb, chunk, 0)

    o_shape = jax.ShapeDtypeStruct((s, GLA_V_TOTAL), F32)
    st_shape = jax.ShapeDtypeStruct((s // GLA_CHUNK, GLA_DV, GLA_K_TOTAL), F32)
    return pl.pallas_call(
        body,
        name=name,
        grid=(nb,),
        in_specs=[qf, kf, vf, bf, qr, kr, vr, br],
        out_specs=[of, sf, orr, sr],
        out_shape=[o_shape, st_shape, o_shape, st_shape],
        scratch_shapes=[pltpu.VMEM((GLA_DV, GLA_K_TOTAL), F32)] * 2,
        compiler_params=_cp(("arbitrary",)),
    )(z, z, z, b_f, z, z, z, b_b)


def _gla_bwd_chunk(cidx, q_ref, k_ref, v_ref, b_ref, do_ref, sv_ref, dqkv_ref, db_ref, dst_ref, masks):
    head_masks, tri, total_row = masks
    rows = pl.ds(pl.multiple_of(cidx * GLA_CHUNK, GLA_CHUNK), GLA_CHUNK)
    v, eb, enb, egb, eg, qt, kt, kh, q_heads, attn = _gla_chunk_terms(q_ref, k_ref, v_ref, b_ref, rows, head_masks, tri, total_row)
    do_c = do_ref[rows, :]
    st = sv_ref[cidx]
    dst = dst_ref[...]
    do_s, v_s = _stack(do_c), _stack(v)
    hs = lambda a, h: a[GLA_CHUNK * h : GLA_CHUNK * (h + 1)]
    vs = lambda a, h: a[:, GLA_DV * h : GLA_DV * (h + 1)]
    dattn = jnp.concatenate([_dot_nt(vs(do_c, h), vs(v, h)) for h in range(GLA_HEADS)], axis=0)
    dattn = jnp.where(tri, dattn, 0.0)
    dv = jnp.concatenate([_dot_tn(hs(attn, h), vs(do_c, h)) for h in range(GLA_HEADS)], axis=1)
    dv = dv + _unstack(_dot_nt(_spread(kh, head_masks), dst))
    dqt = _collect(_dot(do_s, st) + _dot(dattn, kt), head_masks)
    dkt = _dot_tn(dattn, q_heads)
    dkh = _collect(_dot(v_s, dst), head_masks)
    dg = jnp.sum(dkh * kh, axis=0, keepdims=True) + jnp.sum(dst * st, axis=0, keepdims=True) * eg
    db = dqt * qt - dkt * kt - dkh * kh + jnp.where(total_row, dg, 0.0)
    dq = dqt * eb * (GLA_DK**-0.5)
    dk = dkt * enb + dkh * egb
    dqkv_ref[rows, :] = jnp.concatenate([dq, dk, dv], axis=1).astype(dqkv_ref.dtype)
    db_ref[rows, :] = db
    dst_ref[...] = dst * eg + _dot_tn(do_s, q_heads)


def _gla_bwd(z, b_f, b_b, do, st_f, st_b, *, name):
    s = z.shape[0]
    tb = _rows(s)
    cpb = tb // GLA_CHUNK
    wide = 2 * GLA_K_TOTAL + GLA_V_TOTAL
    nb, qf, kf, vf, bf, of, sf = _gla_specs(s, tb, True)
    _, qr, kr, vr, br, orr, sr = _gla_specs(s, tb, False)
    gf = pl.BlockSpec((tb, wide), lambda i: (nb - 1 - i, 0))
    gr = pl.BlockSpec((tb, wide), lambda i: (i, 0))

    def body(qf_ref, kf_ref, vf_ref, bf_ref, dof_ref, svf_ref, qr_ref, kr_ref, vr_ref, br_ref, dor_ref, svr_ref,
             gf_ref, dbf_ref, gr_ref, dbr_ref, dstf_ref, dstr_ref):
        masks_f, masks_r = _gla_masks(False), _gla_masks(True)

        @pl.when(pl.program_id(0) == 0)
        def _():
            dstf_ref[...] = jnp.zeros_like(dstf_ref)
            dstr_ref[...] = jnp.zeros_like(dstr_ref)

        def chunk(ci, carry):
            _gla_bwd_chunk(cpb - 1 - ci, qf_ref, kf_ref, vf_ref, bf_ref, dof_ref, svf_ref, gf_ref, dbf_ref, dstf_ref, masks_f)
            _gla_bwd_chunk(ci, qr_ref, kr_ref, vr_ref, br_ref, dor_ref, svr_ref, gr_ref, dbr_ref, dstr_ref, masks_r)
            return carry

        lax.fori_loop(0, cpb, chunk, 0)

    g_shape = jax.ShapeDtypeStruct((s, wide), _CD)
    db_shape = jax.ShapeDtypeStruct((s, GLA_K_TOTAL), F32)
    return pl.pallas_call(
        body,
        name=name,
        grid=(nb,),
        in_specs=[qf, kf, vf, bf, of, sf, qr, kr, vr, br, orr, sr],
        out_specs=[gf, bf, gr, br],
        out_shape=[g_shape, db_shape, g_shape, db_shape],
        scratch_shapes=[pltpu.VMEM((GLA_DV, GLA_K_TOTAL), F32)] * 2,
        compiler_params=_cp(("arbitrary",)),
    )(z, z, z, b_f, do, st_f, z, z, z, b_b, do, st_b)


HALO = 8


def _halo_specs(s, ts, width, col):
    last = s // HALO - 1
    per = ts // HALO
    prev = pl.BlockSpec((HALO, width), lambda i: (jnp.maximum(i * per - 1, 0), col))
    nxt = pl.BlockSpec((HALO, width), lambda i: (jnp.minimum((i + 1) * per, last), col))
    return prev, nxt


def _group_ones():
    group = jnp.arange(CONV_WIDTH, dtype=jnp.int32) // CONV_GROUP
    return (group[:, None] == group[None, :]).astype(BF16)


_ONES_SPEC = pl.BlockSpec((CONV_WIDTH, CONV_WIDTH), lambda i: (0, 0))


def _conv_terms(cc_ext, cu_ext, cw, valid):
    n = cc_ext.shape[0]
    hc = jnp.where(valid, cc_ext * cu_ext, 0.0)
    hc_prev = pltpu.roll(hc, 1, 0)
    hc_next = pltpu.roll(hc, n - 1, 0)
    conv = cw[0:1] * hc_prev + cw[1:2] * hc + cw[2:3] * hc_next
    return hc, hc_prev, hc_next, conv


def _ext(prev_ref, cur_ref, next_ref):
    return jnp.concatenate([prev_ref[...], cur_ref[...], next_ref[...]], axis=0)


def _valid_rows(ts, s):
    row = lax.broadcasted_iota(jnp.int32, (ts + 2 * HALO, 1), 0) + (pl.program_id(0) * ts - HALO)
    return (row >= 0) & (row < s)


def _head_norm(o, gn):
    out = []
    for h in range(GLA_HEADS):
        oh = o[:, GLA_DV * h : GLA_DV * (h + 1)]
        r = lax.rsqrt(jnp.mean(oh * oh, axis=-1, keepdims=True) + EPS)
        out.append((oh * r, r))
    return out


def _mix_fwd(z, o_f, o_b, conv_w, conv_norm, gla_norm, *, name):
    s = z.shape[0]
    ts = _rows(s, light=True)
    cprev, cnext = _halo_specs(s, ts, CONV_WIDTH, 1)
    uprev, unext = _halo_specs(s, ts, CONV_WIDTH, 2)

    def body(cb_ref, cc_ref, cu_ref, ccp_ref, ccn_ref, cup_ref, cun_ref, g_ref, of_ref, ob_ref, cw_ref, cn_ref, gn_ref, ones_ref, y_ref):
        valid = _valid_rows(ts, s)
        _, _, _, conv = _conv_terms(_ext(ccp_ref, cc_ref, ccn_ref), _ext(cup_ref, cu_ref, cun_ref), cw_ref[...], valid)
        yc = cb_ref[...] * conv[HALO : HALO + ts]
        ms = _dot_split(yc * yc, ones_ref[...]) * (1.0 / CONV_GROUP)
        y_conv = yc * lax.rsqrt(ms + EPS) * cn_ref[...]
        gate = g_ref[...]
        silu = gate * _sigmoid(gate)
        gn = gn_ref[...]
        y_gla = jnp.concatenate([oh * gn for oh, _ in _head_norm(of_ref[...] + ob_ref[...], gn)], axis=1) * silu
        y_ref[...] = jnp.concatenate([y_conv, y_gla], axis=1).astype(y_ref.dtype)

    col = lambda c, w=CONV_WIDTH: pl.BlockSpec((ts, w), lambda i: (i, c))
    return pl.pallas_call(
        body,
        name=name,
        grid=(s // ts,),
        in_specs=[col(0), col(1), col(2), cprev, cnext, uprev, unext, col(3), col(0), col(0),
                  pl.BlockSpec((CONV_K, CONV_WIDTH), lambda i: (0, 0)), pl.BlockSpec((1, CONV_WIDTH), lambda i: (0, 0)),
                  pl.BlockSpec((1, GLA_DV), lambda i: (0, 0)), _ONES_SPEC],
        out_specs=pl.BlockSpec((ts, D_MODEL), lambda i: (i, 0)),
        out_shape=jax.ShapeDtypeStruct((s, D_MODEL), _CD),
        compiler_params=_cp(("parallel",)),
    )(z, z, z, z, z, z, z, z, o_f, o_b, conv_w, conv_norm, gla_norm, _group_ones())


def _mix_bwd(z, o_f, o_b, dy, conv_w, conv_norm, gla_norm, *, name):
    s = z.shape[0]
    ts = _rows(s)
    halos = [_halo_specs(s, ts, CONV_WIDTH, c) for c in (0, 1, 2)]
    dprev, dnext = _halo_specs(s, ts, CONV_WIDTH, 0)

    def body(cb_ref, cc_ref, cu_ref, cbp_ref, cbn_ref, ccp_ref, ccn_ref, cup_ref, cun_ref, g_ref, of_ref, ob_ref,
             dyc_ref, dyg_ref, dyp_ref, dyn_ref, cw_ref, cn_ref, gn_ref, ones_ref, dza_ref, do_ref, dcw_ref, dcn_ref, dgn_ref):
        n = ts + 2 * HALO
        valid = _valid_rows(ts, s)
        cw = cw_ref[...]
        cn = cn_ref[...]
        ones = ones_ref[...]
        cb = _ext(cbp_ref, cb_ref, cbn_ref)
        cc = _ext(ccp_ref, cc_ref, ccn_ref)
        cu = _ext(cup_ref, cu_ref, cun_ref)
        dy = _ext(dyp_ref, dyc_ref, dyn_ref)
        hc, hc_prev, hc_next, conv = _conv_terms(cc, cu, cw, valid)
        yc = cb * conv
        r = lax.rsqrt(_dot_split(yc * yc, ones) * (1.0 / CONV_GROUP) + EPS)
        yh = yc * r
        dyh = dy * cn
        dyc = r * (dyh - yh * (_dot_split(dyh * yh, ones) * (1.0 / CONV_GROUP)))
        dconv = jnp.where(valid, dyc * cb, 0.0)
        dhc = cw[0:1] * pltpu.roll(dconv, n - 1, 0) + cw[1:2] * dconv + cw[2:3] * pltpu.roll(dconv, 1, 0)
        mid = lambda a: a[HALO : HALO + ts]
        dza_ref[:, 0 : 3 * CONV_WIDTH] = jnp.concatenate([mid(dyc * conv), mid(dhc * cu), mid(dhc * cc)], axis=1).astype(dza_ref.dtype)
        dconv_m = mid(dconv)
        colsum = lambda a: jnp.sum(a, axis=0, keepdims=True)
        dcw = jnp.concatenate([colsum(dconv_m * mid(hc_prev)), colsum(dconv_m * mid(hc)), colsum(dconv_m * mid(hc_next))], axis=0)
        dcn = colsum(mid(dy * yh))

        gate = g_ref[...]
        sg = _sigmoid(gate)
        silu = gate * sg
        gn = gn_ref[...]
        dyg = dyg_ref[...]
        don = dyg * silu
        heads = _head_norm(of_ref[...] + ob_ref[...], gn)
        on = jnp.concatenate([oh * gn for oh, _ in heads], axis=1)
        dza_ref[:, 3 * CONV_WIDTH : ZA_COLS] = (dyg * on * (sg * (1.0 + gate * (1.0 - sg)))).astype(dza_ref.dtype)
        dgn = jnp.zeros((1, GLA_DV), F32)
        dos = []
        for h, (oh, rh) in enumerate(heads):
            donh = don[:, GLA_DV * h : GLA_DV * (h + 1)]
            dgn = dgn + colsum(donh * oh)
            doh = donh * gn
            dos.append(rh * (doh - oh * jnp.mean(doh * oh, axis=-1, keepdims=True)))
        do_ref[...] = jnp.concatenate(dos, axis=1)

        first = pl.program_id(0) == 0

        @pl.when(first)
        def _():
            dcw_ref[...] = dcw
            dcn_ref[...] = dcn
            dgn_ref[...] = dgn

        @pl.when(jnp.logical_not(first))
        def _():
            dcw_ref[...] += dcw
            dcn_ref[...] += dcn
            dgn_ref[...] += dgn

    col = lambda c, w=CONV_WIDTH: pl.BlockSpec((ts, w), lambda i: (i, c))
    cw_spec = pl.BlockSpec((CONV_K, CONV_WIDTH), lambda i: (0, 0))
    cn_spec = pl.BlockSpec((1, CONV_WIDTH), lambda i: (0, 0))
    gn_spec = pl.BlockSpec((1, GLA_DV), lambda i: (0, 0))
    return pl.pallas_call(
        body,
        name=name,
        grid=(s // ts,),
        in_specs=[col(0), col(1), col(2), halos[0][0], halos[0][1], halos[1][0], halos[1][1], halos[2][0], halos[2][1],
                  col(3), col(0), col(0), col(0), col(1), dprev, dnext, cw_spec, cn_spec, gn_spec, _ONES_SPEC],
        out_specs=[pl.BlockSpec((ts, ZA_COLS), lambda i: (i, 0)), col(0), cw_spec, cn_spec, gn_spec],
        out_shape=[
            jax.ShapeDtypeStruct((s, ZA_COLS), _CD),
            jax.ShapeDtypeStruct((s, GLA_V_TOTAL), F32),
            jax.ShapeDtypeStruct((CONV_K, CONV_WIDTH), F32),
            jax.ShapeDtypeStruct((1, CONV_WIDTH), F32),
            jax.ShapeDtypeStruct((1, GLA_DV), F32),
        ],
        compiler_params=_cp(("arbitrary",)),
    )(z, z, z, z, z, z, z, z, z, z, o_f, o_b, dy, dy, dy, dy, conv_w, conv_norm, gla_norm, _group_ones())


def _xa_probs(q_ref, kv_ref, h):
    qh = q_ref[:, XA_HEAD_DIM * h : XA_HEAD_DIM * (h + 1)]
    kh = kv_ref[:, XA_HEAD_DIM * h : XA_HEAD_DIM * (h + 1)]
    vh = kv_ref[:, D_MODEL + XA_HEAD_DIM * h : D_MODEL + XA_HEAD_DIM * (h + 1)]
    sc = _dot_nt(qh, kh) * (XA_HEAD_DIM**-0.5)
    e = jnp.exp(sc - jnp.max(sc, axis=-1, keepdims=True))
    return qh, kh, vh, e / jnp.sum(e, axis=-1, keepdims=True)


def _xattn_block(hx, w_xq, kv, w_xo, x1, gain, *, name):
    s, d = hx.shape
    ts = _rows(s)

    def body(h_ref, wq_ref, kv_ref, wo_ref, x_ref, g_ref, q_out, o_out, hm_out, x_out):
        q = _dot(h_ref[...], wq_ref[...]).astype(_CD)
        q_out[...] = q
        heads = []
        for h in range(XA_HEADS):
            _, _, vh, p = _xa_probs(q, kv_ref, h)
            heads.append(_dot(p, vh))
        o = jnp.concatenate(heads, axis=1).astype(_CD)
        o_out[...] = o
        x = _dot(o, wo_ref[...]) + x_ref[...]
        r = lax.rsqrt(jnp.mean(x * x, axis=-1, keepdims=True) + EPS)
        x_out[...] = x
        hm_out[...] = (x * r * g_ref[...]).astype(hm_out.dtype)

    tile = pl.BlockSpec((ts, d), lambda i: (i, 0))
    whole = lambda arr: pl.BlockSpec(arr.shape, lambda i: (0, 0))
    lo = jax.ShapeDtypeStruct((s, d), _CD)
    return pl.pallas_call(
        body,
        name=name,
        grid=(s // ts,),
        in_specs=[tile, whole(w_xq), whole(kv), whole(w_xo), tile, whole(gain)],
        out_specs=[tile] * 4,
        out_shape=[lo, lo, lo, jax.ShapeDtypeStruct((s, d), F32)],
        compiler_params=_cp(("parallel",)),
    )(hx, w_xq, kv, w_xo, x1, gain)


def _xattn_bwd(qx, kv, dx, w_xo, *, name):
    s = qx.shape[0]
    ts = _rows(s, light=True)

    def body(q_ref, kv_ref, dx_ref, w_ref, dq_ref, dkv_ref):
        do = _dot_nt(dx_ref[...], w_ref[...]).astype(_CD)
        dqs, dks, dvs = [], [], []
        for h in range(XA_HEADS):
            qh, kh, vh, p = _xa_probs(q_ref, kv_ref, h)
            doh = do[:, XA_HEAD_DIM * h : XA_HEAD_DIM * (h + 1)]
            dp = _dot_nt(doh, vh)
            ds = p * (dp - jnp.sum(dp * p, axis=-1, keepdims=True)) * (XA_HEAD_DIM**-0.5)
            dqs.append(_dot(ds, kh))
            dks.append(_dot_tn(ds, qh))
            dvs.append(_dot_tn(p, doh))
        dq_ref[...] = jnp.concatenate(dqs, axis=1).astype(dq_ref.dtype)
        dkv = jnp.concatenate(dks + dvs, axis=1)

        @pl.when(pl.program_id(0) == 0)
        def _():
            dkv_ref[...] = dkv

        @pl.when(pl.program_id(0) > 0)
        def _():
            dkv_ref[...] += dkv

    tile = pl.BlockSpec((ts, D_MODEL), lambda i: (i, 0))
    kv_spec = pl.BlockSpec((N_MEM, 2 * D_MODEL), lambda i: (0, 0))
    return pl.pallas_call(
        body,
        name=name,
        grid=(s // ts,),
        in_specs=[tile, kv_spec, tile, pl.BlockSpec((D_MODEL, D_MODEL), lambda i: (0, 0))],
        out_specs=[tile, kv_spec],
        out_shape=[jax.ShapeDtypeStruct((s, D_MODEL), _CD), jax.ShapeDtypeStruct((N_MEM, 2 * D_MODEL), F32)],
        compiler_params=_cp(("arbitrary",)),
    )(qx, kv, dx, w_xo)


def _adamw_math(w, g, m, v):
    m = ADAM_B1 * m + (1.0 - ADAM_B1) * g
    v = ADAM_B2 * v + (1.0 - ADAM_B2) * (g * g)
    m_hat = m / (1.0 - ADAM_B1**ADAM_STEP)
    v_hat = v / (1.0 - ADAM_B2**ADAM_STEP)
    delta = -ADAM_LR * (m_hat / (jnp.sqrt(v_hat) + ADAM_EPS) + ADAM_WD * w)
    return delta, m, v


def _adamw(w, m, v, shard_rows, off, *, transposed, name):
    r, c = w.shape
    by_columns = r % 256 != 0
    tr = 512 if (c if by_columns else r) % 512 == 0 and off % 512 == 0 else 256
    if by_columns:
        assert not transposed and off == 0
        g_spec = tile = pl.BlockSpec((r, tr), lambda i: (0, i))
    else:
        g_spec = pl.BlockSpec((c, tr), lambda i: (off // c, i)) if transposed else pl.BlockSpec((tr, c), lambda i: (off // tr + i, 0))
        tile = pl.BlockSpec((tr, c), lambda i: (i, 0))

    def body(w_ref, g_ref, m_ref, v_ref, go_ref, d_ref, nm_ref, nv_ref):
        g = g_ref[...].T if transposed else g_ref[...]
        go_ref[...] = g
        d_ref[...], nm_ref[...], nv_ref[...] = _adamw_math(w_ref[...], g, m_ref[...], v_ref[...])

    return pl.pallas_call(
        body,
        name=name,
        grid=((c if by_columns else r) // tr,),
        in_specs=[tile, g_spec, tile, tile],
        out_specs=[tile] * 4,
        out_shape=[jax.ShapeDtypeStruct((r, c), F32)] * 4,
        compiler_params=_cp(("parallel",)),
    )(w, shard_rows, m, v)


def _adamw_small(groups, *, name):
    n = len(groups)

    def body(*refs):
        ins, outs = refs[: 4 * n], refs[4 * n :]
        for i in range(n):
            w_ref, g_ref, m_ref, v_ref = ins[4 * i : 4 * i + 4]
            outs[3 * i][...], outs[3 * i + 1][...], outs[3 * i + 2][...] = _adamw_math(w_ref[...], g_ref[...], m_ref[...], v_ref[...])

    flat = [a for grp in groups for a in grp]
    vm = pl.BlockSpec(memory_space=pltpu.VMEM)
    res = pl.pallas_call(
        body,
        name=name,
        in_specs=[vm] * (4 * n),
        out_specs=[vm] * (3 * n),
        out_shape=[jax.ShapeDtypeStruct(grp[0].shape, F32) for grp in groups for _ in range(3)],
        compiler_params=_cp(),
    )(*flat)
    return [tuple(res[3 * i : 3 * i + 3]) for i in range(n)]


def _place():
    return lax.axis_index("x"), lax.axis_index("y"), lax.axis_index("c")


def _rel_chip(x, y, k):
    return (1 - x if k & 2 else x), (1 - y if k & 1 else y)


def _half(c, rh):
    return pl.ds(pl.multiple_of(c * rh, 16), rh)


HBM = pl.BlockSpec(memory_space=pltpu.HBM)
SEM = pl.BlockSpec(memory_space=pltpu.SEMAPHORE)
EFFECT = pltpu.SideEffectType.DATAFLOW_SIDE_EFFECTING


def _in_hbm(a):
    return pltpu.with_memory_space_constraint(a, pltpu.HBM)


def _gather_copies(p_ref, land_ref, send_sems, recv_sems):
    rh = p_ref.shape[0] // 2
    x, y, c = _place()
    rows = _half(c, rh)
    copies = []
    for k in range(1, N_CHIPS):
        cx, cy = _rel_chip(x, y, k)
        copies.append(pltpu.make_async_remote_copy(
            src_ref=p_ref.at[rows], dst_ref=land_ref.at[2 * x + y, rows], send_sem=send_sems.at[k - 1], recv_sem=recv_sems.at[k - 1],
            device_id=(cx, cy, c), device_id_type=MESH))
    copies.append(pltpu.make_async_remote_copy(
        src_ref=p_ref, dst_ref=land_ref.at[2 * x + y], send_sem=send_sems.at[N_CHIPS - 1], recv_sem=recv_sems.at[N_CHIPS - 1],
        device_id=(x, y, 1 - c), device_id_type=MESH))
    return copies


def _gather_start(pack, after, *, name):
    r, w = pack.shape

    def body(p_ref, land_ref, after_ref, send_sems, recv_sems, p_thru, land_thru, token):
        for cp in _gather_copies(p_ref, land_ref, send_sems, recv_sems):
            cp.start()
        token[...] = jnp.zeros_like(token)

    return pl.pallas_call(
        body,
        name=name,
        out_shape=(pltpu.SemaphoreType.DMA((N_CHIPS,)), pltpu.SemaphoreType.DMA((N_CHIPS,)), pltpu.HBM((r, w), pack.dtype),
                   pltpu.HBM((N_CHIPS, r, w), pack.dtype), jax.ShapeDtypeStruct((8, 128), F32)),
        in_specs=(HBM, HBM, ANY),
        out_specs=(SEM, SEM, HBM, HBM, pl.BlockSpec(memory_space=pltpu.VMEM)),
        input_output_aliases={0: 2, 1: 3},
        compiler_params=pltpu.CompilerParams(has_side_effects=EFFECT),
    )(_in_hbm(pack), _in_hbm(lax.empty((N_CHIPS, r, w), pack.dtype)), after)


def _gather_wait(send_sems, recv_sems, pack, land, after, *, name):
    def body(p_ref, land_ref, send_sems, recv_sems, after_ref, p_out, land_out):
        for cp in _gather_copies(p_ref, land_ref, send_sems, recv_sems):
            cp.wait_send()
            cp.wait_recv()

    return pl.pallas_call(
        body,
        name=name,
        out_shape=(pltpu.HBM(pack.shape, pack.dtype), pltpu.HBM(land.shape, land.dtype)),
        in_specs=(HBM, HBM, SEM, SEM, ANY),
        out_specs=(HBM, HBM),
        input_output_aliases={0: 0, 1: 1},
        compiler_params=pltpu.CompilerParams(has_side_effects=EFFECT),
    )(pack, land, send_sems, recv_sems, after)


def _gather_spread(land, *, name):
    n, r, w = land.shape
    rh = r // 2

    def body(land_ref, o_ref, send_sems, recv_sems):
        x, y, c = _place()
        rows = _half(c, rh)
        copies = []
        for k in range(1, N_CHIPS):
            cx, cy = _rel_chip(x, y, k)
            copies.append(pltpu.make_async_remote_copy(
                src_ref=land_ref.at[2 * cx + cy, rows], dst_ref=o_ref.at[2 * cx + cy, rows], send_sem=send_sems.at[k - 1],
                recv_sem=recv_sems.at[k - 1], device_id=(x, y, 1 - c), device_id_type=MESH))
        for cp in copies:
            cp.start()
        for cp in copies:
            cp.wait()

    return pl.pallas_call(
        body,
        name=name,
        in_specs=[ANY],
        out_specs=ANY,
        out_shape=jax.ShapeDtypeStruct(land.shape, land.dtype),
        input_output_aliases={0: 0},
        scratch_shapes=[pltpu.SemaphoreType.DMA((N_CHIPS - 1,)), pltpu.SemaphoreType.DMA((N_CHIPS - 1,))],
        compiler_params=pltpu.CompilerParams(has_side_effects=True),
    )(land)


N_PARTS = 2 * (N_CHIPS - 1)


def _scatter_copies(lo_ref, g_ref, land_lo_ref, land_f_ref, send_sems, recv_sems, starting):
    rh = g_ref.shape[1] // 2
    x, y, c = _place()
    copies = []
    for k in range(1, N_CHIPS):
        cx, cy = _rel_chip(x, y, k)
        for i in range(2):
            part = 2 * (k - 1) + (c if starting else i)
            copies.append(pltpu.make_async_remote_copy(
                src_ref=lo_ref.at[2 * cx + cy, pl.ds(i * rh, rh)], dst_ref=land_lo_ref.at[part],
                send_sem=send_sems.at[2 * (k - 1) + i], recv_sem=recv_sems.at[part], device_id=(cx, cy, i), device_id_type=MESH))
    copies.append(pltpu.make_async_remote_copy(
        src_ref=g_ref.at[2 * x + y, _half(1 - c, rh)], dst_ref=land_f_ref, send_sem=send_sems.at[N_PARTS], recv_sem=recv_sems.at[N_PARTS],
        device_id=(x, y, 1 - c), device_id_type=MESH))
    return copies


def _scatter_start(g_lo, g, *, name):
    n, r, w = g.shape
    rh = r // 2

    def body(lo_ref, g_ref, land_lo_ref, land_f_ref, send_sems, recv_sems, lo_thru, g_thru, land_lo_thru, land_f_thru, token):
        for cp in _scatter_copies(lo_ref, g_ref, land_lo_ref, land_f_ref, send_sems, recv_sems, True):
            cp.start()
        token[...] = jnp.zeros_like(token)

    return pl.pallas_call(
        body,
        name=name,
        out_shape=(pltpu.SemaphoreType.DMA((N_PARTS + 1,)), pltpu.SemaphoreType.DMA((N_PARTS + 1,)), pltpu.HBM(g_lo.shape, g_lo.dtype),
                   pltpu.HBM(g.shape, g.dtype), pltpu.HBM((N_PARTS, rh, w), g_lo.dtype), pltpu.HBM((rh, w), g.dtype),
                   jax.ShapeDtypeStruct((8, 128), F32)),
        in_specs=(HBM, HBM, HBM, HBM),
        out_specs=(SEM, SEM, HBM, HBM, HBM, HBM, pl.BlockSpec(memory_space=pltpu.VMEM)),
        input_output_aliases={0: 2, 1: 3, 2: 4, 3: 5},
        compiler_params=pltpu.CompilerParams(has_side_effects=EFFECT),
    )(_in_hbm(g_lo), _in_hbm(g), _in_hbm(lax.empty((N_PARTS, rh, w), g_lo.dtype)), _in_hbm(lax.empty((rh, w), g.dtype)))


def _scatter_wait(send_sems, recv_sems, g_lo, g, land_lo, land_f, after, *, name):
    def body(lo_ref, g_ref, land_lo_ref, land_f_ref, send_sems, recv_sems, after_ref, o0, o1, o2, o3):
        for cp in _scatter_copies(lo_ref, g_ref, land_lo_ref, land_f_ref, send_sems, recv_sems, False):
            cp.wait_send()
            cp.wait_recv()

    arrays = (g_lo, g, land_lo, land_f)
    return pl.pallas_call(
        body,
        name=name,
        out_shape=tuple(pltpu.HBM(a.shape, a.dtype) for a in arrays),
        in_specs=(HBM, HBM, HBM, HBM, SEM, SEM, ANY),
        out_specs=(HBM, HBM, HBM, HBM),
        input_output_aliases={0: 0, 1: 1, 2: 2, 3: 3},
        compiler_params=pltpu.CompilerParams(has_side_effects=EFFECT),
    )(*arrays, send_sems, recv_sems, after)


def _scatter_sum(g, land_lo, land_f, where, *, name):
    n, r, w = g.shape
    rh = r // 2
    tr = _pick(rh, (256, 160, 80))
    nt = rh // tr

    def body(where_ref, g_ref, f_ref, lo_ref, o_ref):
        acc = g_ref[0] + f_ref[...]
        for part in range(N_PARTS):
            acc = acc + lo_ref[part].astype(F32)
        o_ref[...] = acc

    return pl.pallas_call(
        body,
        name=name,
        grid_spec=pltpu.PrefetchScalarGridSpec(
            num_scalar_prefetch=1,
            grid=(nt,),
            in_specs=[pl.BlockSpec((1, tr, w), lambda i, wh: (wh[1], wh[0] * nt + i, 0)),
                      pl.BlockSpec((tr, w), lambda i, wh: (i, 0)),
                      pl.BlockSpec((N_PARTS, tr, w), lambda i, wh: (0, i, 0))],
            out_specs=pl.BlockSpec((tr, w), lambda i, wh: (wh[0] * nt + i, 0)),
        ),
        out_shape=jax.ShapeDtypeStruct((r, w), F32),
        compiler_params=_cp(("parallel",)),
    )(where, g, land_f, land_lo)


def _swap_all(shards, *, name):
    n = len(shards)

    def body(*refs):
        ins, outs = refs[:n], refs[n : 2 * n]
        send_sems, recv_sems = refs[2 * n :]
        x, y, c = _place()
        copies = []
        for i, (e_ref, o_ref) in enumerate(zip(ins, outs)):
            rows = _half(c, e_ref.shape[0] // 2)
            copies.append(pltpu.make_async_remote_copy(src_ref=e_ref.at[rows], dst_ref=o_ref.at[rows], send_sem=send_sems.at[i],
                                                       recv_sem=recv_sems.at[i], device_id=(x, y, 1 - c), device_id_type=MESH))
        for cp in copies:
            cp.start()
        for cp in copies:
            cp.wait()

    return pl.pallas_call(
        body,
        name=name,
        in_specs=[ANY] * n,
        out_specs=[ANY] * n,
        out_shape=[jax.ShapeDtypeStruct(e.shape, e.dtype) for e in shards],
        input_output_aliases={i: i for i in range(n)},
        scratch_shapes=[pltpu.SemaphoreType.DMA((n,)), pltpu.SemaphoreType.DMA((n,))],
        compiler_params=pltpu.CompilerParams(has_side_effects=True),
    )(*shards)


def _sum_small(small, after):
    n_dev = 8

    def body(s_ref, after_ref, o_ref, all_ref, send_sems, recv_sems):
        x, y, c = _place()
        me = 4 * x + 2 * y + c
        all_ref[me] = s_ref[...]
        copies = []
        for k in range(1, n_dev):
            cx, cy = _rel_chip(x, y, k >> 1)
            cc = 1 - c if k & 1 else c
            copies.append(pltpu.make_async_remote_copy(
                src_ref=s_ref, dst_ref=all_ref.at[me], send_sem=send_sems.at[k - 1], recv_sem=recv_sems.at[k - 1],
                device_id=(cx, cy, cc), device_id_type=MESH))
        for cp in copies:
            cp.start()
        for cp in copies:
            cp.wait()
        acc = all_ref[0]
        for a in range(1, n_dev):
            acc = acc + all_ref[a]
        o_ref[...] = acc

    vm = pl.BlockSpec(memory_space=pltpu.VMEM)
    return pl.pallas_call(
        body,
        name="sum_small",
        in_specs=[vm, ANY],
        out_specs=vm,
        out_shape=jax.ShapeDtypeStruct(small.shape, F32),
        scratch_shapes=[pltpu.VMEM((n_dev,) + small.shape, F32), pltpu.SemaphoreType.DMA((n_dev - 1,)), pltpu.SemaphoreType.DMA((n_dev - 1,))],
        compiler_params=pltpu.CompilerParams(has_side_effects=True),
    )(small, after)


MATS = {"w_in": (776, True), "w_out": (256, False), "w_xq": (256, False), "w_xkv": (512, True), "w_xo": (256, False),
        "w_up": (1024, True), "w_down": (1024, False)}
GATHER_FIRST = ("w_in",)
GATHER_REST = ("w_up", "w_down", "w_out", "w_xq", "w_xkv", "w_xo")
IN_PLACE = ("w_up", "w_down")
GRAD_GROUPS = (("w_up", "w_down"), ("w_out", "w_xq", "w_xkv", "w_xo"), ("w_in",))


def _group_rows(names):
    n = sum(MATS[name][0] for name in names)
    return n + (-n) % 32


def _pack(pieces, rows):
    p = jnp.concatenate(pieces, axis=0) if len(pieces) > 1 else pieces[0]
    return jnp.pad(p, ((0, rows - p.shape[0]), (0, 0))) if rows > p.shape[0] else p


SMALL = (
    ("mix_norm", 1024), ("conv_norm", 512), ("b_af", 256), ("b_ab", 256), ("gla_norm", 128), ("xa_norm", 1024), ("mem_norm", 1024),
    ("mlp_norm", 1024), ("final_norm", 1024), ("conv_w", 1536), ("w_af", 4096), ("w_ab", 4096), ("loss", 128),
)


def kernel(x, mem, mix_norm, w_in, conv_w, conv_norm, w_af, b_af, w_ab, b_ab, gla_norm, w_out, xa_norm, mem_norm, w_xq, w_xkv, w_xo, mlp_norm, w_up, w_down, final_norm, loss_target, m_mix_norm, m_w_in, m_conv_w, m_conv_norm, m_w_af, m_b_af, m_w_ab, m_b_ab, m_gla_norm, m_w_out, m_xa_norm, m_mem_norm, m_w_xq, m_w_xkv, m_w_xo, m_mlp_norm, m_w_up, m_w_down, m_final_norm, v_mix_norm, v_w_in, v_conv_w, v_conv_norm, v_w_af, v_b_af, v_w_ab, v_b_ab, v_gla_norm, v_w_out, v_xa_norm, v_mem_norm, v_w_xq, v_w_xkv, v_w_xo, v_mlp_norm, v_w_up, v_w_down, v_final_norm):
    given = dict(locals())
    xi, yi, ci = _place()
    chip = 2 * xi + yi
    where = jnp.stack([ci, chip]).astype(jnp.int32)

    lo = {name: (given[name][0].T if MATS[name][1] else given[name][0]).astype(_CD) for name in MATS}
    pack_rest = _pack([lo[name] for name in GATHER_REST], _group_rows(GATHER_REST))
    pack_first = _pack([lo[name] for name in GATHER_FIRST], _group_rows(GATHER_FIRST))
    xs, mems, tgt = x[0], mem[0], loss_target[0]
    behind = lambda gain, token: gain + token[0, 0]

    def placed(shard, full_shape, col):
        return lax.dynamic_update_slice(jnp.zeros(full_shape, F32), shard, (0, col)).reshape(-1, 128)

    sw = jnp.concatenate([
        placed(conv_w[0], (CONV_K, CONV_WIDTH), 128 * chip),
        placed(w_af[0], (GLA_LOWRANK, GLA_K_TOTAL), 64 * chip),
        placed(w_ab[0], (GLA_LOWRANK, GLA_K_TOTAL), 64 * chip),
    ], axis=0)
    sw = jnp.pad(sw, ((0, SMALL_ROWS - sw.shape[0]), (0, 0))) * (ci == 0).astype(F32)
    sw = _sum_small(sw, mix_norm)

    first_send, first_recv, pack_first, land_first, first_token = _gather_start(pack_first, sw, name="gather_first_start")
    rest_send, rest_recv, pack_rest, land_rest, rest_token = _gather_start(pack_rest, first_token, name="gather_rest_start")
    h1 = _rms_fwd(xs, behind(mix_norm, rest_token), name="norm_mix")
    pack_first, land_first = _gather_wait(first_send, first_recv, pack_first, land_first, h1, name="gather_first_wait")
    got_first = _gather_spread(land_first, name="gather_first_spread")

    def whole(got, off, rows):
        return got[:, off : off + rows].reshape(N_CHIPS * rows, D_MODEL)

    w_in_t = whole(got_first, 0, MATS["w_in"][0])
    w_za = jnp.concatenate([w_in_t[0:1536], w_in_t[2560:3072]], axis=0)
    w_zb = jnp.concatenate([w_in_t[1536:2560], w_in_t[3072:W_IN_COLS], jnp.zeros((ZB_COLS - 1056, D_MODEL), _CD)], axis=0)
    conv_w_full = sw[0:12].reshape(CONV_K, CONV_WIDTH)
    w_af_full = sw[12:44].reshape(GLA_LOWRANK, GLA_K_TOTAL)
    w_ab_full = sw[44:76].reshape(GLA_LOWRANK, GLA_K_TOTAL)
    waf_p = jnp.pad(w_af_full, ((0, 128 - GLA_LOWRANK), (0, 0))).astype(_CD)
    wab_p = jnp.pad(w_ab_full, ((GLA_LOWRANK, 128 - 2 * GLA_LOWRANK), (0, 0))).astype(_CD)

    z_a, z_b = _mm_two(h1, w_za, w_zb, name="proj_in")
    b_f, b_b = _gate_fwd(z_b, waf_p, wab_p, b_af, b_ab, name="gates")
    o_f, st_f, o_b, st_b = _gla_fwd(z_b, b_f, b_b, name="gla_scan")
    y = _mix_fwd(z_a, o_f, o_b, conv_w_full, conv_norm, gla_norm, name="mix_out")
    pack_rest, land_rest = _gather_wait(rest_send, rest_recv, pack_rest, land_rest, y, name="gather_rest_wait")
    gathered = _gather_spread(land_rest, name="gather_rest_spread")
    wt, off = {}, 0
    for name in GATHER_REST:
        wt[name] = (gathered, off, MATS[name][0]) if name in IN_PLACE else whole(gathered, off, MATS[name][0])
        off += MATS[name][0]
    x1, hx = _mm_rows(y, wt["w_out"], mode="nn", name="proj_out", rows=(xs,), vecs=(xa_norm,), out_rows=(F32, _CD),
                      epilogue=_ep_residual_norm, tm=1024)
    hmem = _rms_fwd(mems, mem_norm, name="norm_mem")
    kv = _mm(hmem, wt["w_xkv"], mode="nt", name="proj_xkv", out_dtypes=(_CD,))
    qx, ox, hm, x2 = _xattn_block(hx, wt["w_xq"], kv, wt["w_xo"], x1, mlp_norm, name="xattn_block")
    act, relu_u = _mm(hm, wt["w_up"], mode="nt", name="mlp_up", out_dtypes=(_CD, _CD), tm=2048,
                      epilogue=lambda acc: (jnp.square(jnp.maximum(acc, 0.0)), jnp.maximum(acc, 0.0)))
    dx3, dx3_lo, loss_part, g_final_norm = _mm_rows(
        act, wt["w_down"], mode="nn", name="mlp_down", rows=(x2, tgt), vecs=(final_norm.reshape(1, D_MODEL),),
        out_rows=(F32, _CD), out_vecs=(128, D_MODEL), epilogue=_ep_loss)

    grads_t = {}

    def start_group(names, tag):
        rows = _group_rows(names)
        g = jnp.stack([_pack([grads_t[name][a * MATS[name][0] : (a + 1) * MATS[name][0]] for name in names], rows) for a in range(N_CHIPS)])
        return _scatter_start(g.astype(_TD), g, name="grads_" + tag + "_start")

    def finish_group(state, after, tag):
        send_sems, recv_sems, g_lo, g, land_lo, land_f, _ = state
        g_lo, g, land_lo, land_f = _scatter_wait(send_sems, recv_sems, g_lo, g, land_lo, land_f, after, name="grads_" + tag + "_wait")
        return _scatter_sum(g, land_lo, land_f, where, name="grads_" + tag + "_sum")

    def new_packs(names):
        shape = (N_CHIPS, _group_rows(names), D_MODEL)
        return lax.empty(shape, F32), lax.empty(shape, _TD)

    def grad_into(packs, names, which, a, b, name):
        off = sum(MATS[other][0] for other in names[: names.index(which)])
        return _mm_tn_into(a, b, packs, rows=MATS[which][0], off=off, name=name)

    du = _mm(dx3_lo, wt["w_down"], mode="nt", name="mlp_down_dx", out_dtypes=(_CD,), extras=(relu_u,), tm=2048,
             epilogue=lambda acc, rr: (acc * (2.0 * rr.astype(F32)),))
    packs = new_packs(GRAD_GROUPS[0])
    packs = grad_into(packs, GRAD_GROUPS[0], "w_down", act, dx3_lo, "mlp_down_dw")
    packs = grad_into(packs, GRAD_GROUPS[0], "w_up", du, hm, "mlp_up_dw")
    mlp_state = _scatter_start(packs[1], packs[0], name="grads_mlp_start")
    dx2, dx2_lo, g_mlp_norm = _mm_rows(
        du, wt["w_up"], mode="nn", name="mlp_up_dx", rows=(x2, dx3), vecs=(behind(mlp_norm, mlp_state[-1]),),
        out_rows=(F32, _CD), out_vecs=(D_MODEL,), epilogue=_ep_norm_bwd)
    packs = new_packs(GRAD_GROUPS[1])
    packs = grad_into(packs, GRAD_GROUPS[1], "w_xo", ox, dx2_lo, "proj_xo_dw")
    dqx, dkv = _xattn_bwd(qx, kv, dx2_lo, wt["w_xo"], name="xattn_bwd")
    packs = grad_into(packs, GRAD_GROUPS[1], "w_xq", hx, dqx, "proj_xq_dw")
    dx1, dx1_lo, g_xa_norm = _mm_rows(
        dqx, wt["w_xq"], mode="nt", name="proj_xq_dx", rows=(x1, dx2), vecs=(xa_norm,),
        out_rows=(F32, _CD), out_vecs=(D_MODEL,), epilogue=_ep_norm_bwd, tm=1024)
    dkv_lo = dkv.astype(_CD)
    packs = grad_into(packs, GRAD_GROUPS[1], "w_xkv", dkv_lo, hmem, "proj_xkv_dw")
    dhmem = _mm(dkv_lo, wt["w_xkv"], mode="nn", name="proj_xkv_dx")
    g_mem_norm = _rms_gain_grad(mems, dhmem, name="norm_mem_bwd")
    dy = _mm(dx1_lo, wt["w_out"], mode="nt", name="proj_out_dx")
    packs = grad_into(packs, GRAD_GROUPS[1], "w_out", y, dx1_lo, "proj_out_dw")
    attn_state = _scatter_start(packs[1], packs[0], name="grads_attn_start")
    dz_a, do, g_conv_w, g_conv_norm, g_gla_norm = _mix_bwd(z_a, o_f, o_b, dy, conv_w_full, behind(conv_norm, attn_state[-1]), gla_norm, name="mix_out_bwd")
    dqkv_f, db_f, dqkv_b, db_b = _gla_bwd(z_b, b_f, b_b, do, st_f, st_b, name="gla_scan_bwd")
    dz_b, g_waf_p, g_wab_p, g_b_af, g_b_ab = _gate_bwd(z_b, waf_p, wab_p, b_af, b_ab, db_f, db_b, dqkv_f, dqkv_b, name="gates_bwd")
    g_za = _mm_tn(dz_a, h1, name="proj_in_a_dw")
    g_zb = _mm_tn(dz_b, h1, name="proj_in_b_dw")
    grads_t["w_in"] = jnp.concatenate([g_za[0:1536], g_zb[0:1024], g_za[1536:2048], g_zb[1024:1056]], axis=0)
    in_state = start_group(GRAD_GROUPS[2], "in")
    grad_x, g_mix_norm = _mm_rows(
        dz_a, w_za, mode="nn", name="proj_in_dx", more=((dz_b, w_zb),), rows=(xs, dx1), vecs=(behind(mix_norm, in_state[-1]),),
        out_rows=(F32,), out_vecs=(D_MODEL,), epilogue=_ep_norm_bwd)

    half_mlp = finish_group(mlp_state, grad_x, "mlp")
    half_attn = finish_group(attn_state, half_mlp, "attn")
    half_in = finish_group(in_state, half_attn, "in")
    shard_rows = {}
    for names, rows in zip(GRAD_GROUPS, _swap_all([half_mlp, half_attn, half_in], name="shards_to_sibling")):
        off = 0
        for name in names:
            shard_rows[name] = (rows, off)
            off += MATS[name][0]

    small_vals = dict(mix_norm=g_mix_norm, conv_norm=g_conv_norm, b_af=g_b_af, b_ab=g_b_ab, gla_norm=g_gla_norm, xa_norm=g_xa_norm,
                      mem_norm=g_mem_norm, mlp_norm=g_mlp_norm, final_norm=g_final_norm, conv_w=g_conv_w,
                      w_af=g_waf_p[0:GLA_LOWRANK], w_ab=g_wab_p[GLA_LOWRANK : 2 * GLA_LOWRANK], loss=loss_part)
    small = jnp.concatenate([small_vals[name].reshape(-1, 128) for name, _ in SMALL], axis=0)
    small = _sum_small(jnp.pad(small, ((0, SMALL_ROWS - small.shape[0]), (0, 0))), loss_part)
    g_small, off = {}, 0
    for name, n in SMALL:
        g_small[name] = small[off : off + n // 128]
        off += n // 128
    loss = g_small["loss"][0, 0]
    g_small["conv_w"] = lax.dynamic_slice(g_small["conv_w"].reshape(CONV_K, CONV_WIDTH), (0, 128 * chip), (CONV_K, 128))
    g_small["w_af"] = lax.dynamic_slice(g_small["w_af"].reshape(GLA_LOWRANK, GLA_K_TOTAL), (0, 64 * chip), (GLA_LOWRANK, 64))
    g_small["w_ab"] = lax.dynamic_slice(g_small["w_ab"].reshape(GLA_LOWRANK, GLA_K_TOTAL), (0, 64 * chip), (GLA_LOWRANK, 64))

    names = ["mix_norm", "w_in", "conv_w", "conv_norm", "w_af", "b_af", "w_ab", "b_ab", "gla_norm", "w_out", "xa_norm", "mem_norm",
             "w_xq", "w_xkv", "w_xo", "mlp_norm", "w_up", "w_down", "final_norm"]
    big_names = list(MATS)
    as2d = lambda a: a.reshape(1, -1) if a.ndim == 1 else a.reshape(a.shape[-2:])
    grads, deltas, new_m, new_v = {}, {}, {}, {}
    for name in big_names:
        rows, off = shard_rows[name]
        wmv = [as2d(given[name]), as2d(given["m_" + name]), as2d(given["v_" + name])]
        as_stored = name == "w_in"
        if as_stored:
            wmv = [a.T for a in wmv]
        res = _adamw(*wmv, rows, off, transposed=MATS[name][1] and not as_stored, name="adamw_" + name)
        grads[name], deltas[name], new_m[name], new_v[name] = [a.T for a in res] if as_stored else res
    small_names = [name for name in names if name not in big_names]
    groups = []
    for name in small_names:
        grads[name] = g_small[name].reshape(as2d(given[name]).shape)
        groups.append((as2d(given[name]), grads[name], as2d(given["m_" + name]), as2d(given["v_" + name])))
    for name, res in zip(small_names, _adamw_small(groups, name="adamw_small")):
        deltas[name], new_m[name], new_v[name] = res

    like = lambda name, a: a.reshape(given[name].shape)
    return (loss, grad_x[None], *[like(n, grads[n]) for n in names], *[like(n, deltas[n]) for n in names],
            *[like(n, new_m[n]) for n in names], *[like(n, new_v[n]) for n in names])
```

```python
import jax
import jax.numpy as jnp
from jax import lax
from jax.experimental import pallas as pl
from jax.experimental.pallas import tpu as pltpu

F32 = jnp.float32
BF16 = jnp.bfloat16
_CD = jnp.bfloat16
_TD = jnp.bfloat16

D_MODEL = 1024
N_MEM = 256
CONV_WIDTH = 512
CONV_GROUP = 64
CONV_K = 3
GLA_HEADS = 4
GLA_DK = 64
GLA_DV = 128
GLA_K_TOTAL = 256
GLA_V_TOTAL = 512
GLA_LOWRANK = 16
GLA_GATE_SCALE = 1.0 / 16.0
GLA_CHUNK = 64
XA_HEADS = 4
XA_HEAD_DIM = 256
D_FF = 4096
EPS = 1e-6
W_IN_COLS = 3104
ZA_COLS = 2048
ZB_COLS = 1152
LR_COL = 1024

ADAM_LR = 0.001
ADAM_B1 = 0.9
ADAM_B2 = 0.999
ADAM_EPS = 1e-08
ADAM_WD = 0.01
ADAM_STEP = 10

N_CHIPS = 4
SMALL_ROWS = 128

_TS = 512
_VMEM = 44 * 1024 * 1024
_VMEM_STREAM = 63 * 1024 * 1024
MESH = pl.DeviceIdType.MESH
ANY = pl.BlockSpec(memory_space=pl.ANY)


def _cp(sem=None, vmem=_VMEM, **kw):
    return pltpu.CompilerParams(dimension_semantics=sem, vmem_limit_bytes=vmem, **kw)


def _dot(a, b):
    return jnp.dot(a.astype(_CD), b.astype(_CD), preferred_element_type=F32)


def _dot_nt(a, b):
    return lax.dot_general(a.astype(_CD), b.astype(_CD), (((1,), (1,)), ((), ())), preferred_element_type=F32)


def _dot_tn(a, b):
    return lax.dot_general(a.astype(_CD), b.astype(_CD), (((0,), (0,)), ((), ())), preferred_element_type=F32)


def _dot_split(x, ones):
    hi = x.astype(BF16)
    r = x - hi.astype(F32)
    mid = r.astype(BF16)
    lo = (r - mid.astype(F32)).astype(BF16)
    d = lambda p: jnp.dot(p, ones, preferred_element_type=F32)
    return d(hi) + d(mid) + d(lo)


def _pick(n, cands=(1024, 640, 512, 256, 128)):
    for t in cands:
        if n % t == 0:
            return t
    return n


def _rows(s, light=False, times=2):
    return min(times * _TS if light else _TS, s)


def _sigmoid(v):
    e = jnp.exp(-jnp.abs(v))
    return jnp.where(v >= 0, 1.0 / (1.0 + e), e / (1.0 + e))


def _mm(a, b, *, mode, name, out_dtypes=(F32,), extras=(), epilogue=None, tm=None, tn=None, tk=None):
    m, k = a.shape
    placed = isinstance(b, tuple)
    if placed:
        b, b_off, tn = b
        assert mode == "nt" and b_off % tn == 0
        n = N_CHIPS * tn
    else:
        n = b.shape[1] if mode == "nn" else b.shape[0]
    tm = min(m, tm or 1024)
    tn = tn or _pick(n)
    tk = tk or _pick(k)
    nk = k // tk
    n_ex, n_out = len(extras), len(out_dtypes)

    def body(*refs):
        a_ref, b_ref = refs[:2]
        ex = refs[2 : 2 + n_ex]
        outs = refs[2 + n_ex : 2 + n_ex + n_out]
        part = _dot(a_ref[...], b_ref[...]) if mode == "nn" else _dot_nt(a_ref[...], b_ref[...])

        def finish(acc):
            res = epilogue(acc, *[e[...] for e in ex]) if epilogue else (acc,)
            for o, r in zip(outs, res):
                o[...] = r.astype(o.dtype)

        if nk == 1:
            finish(part)
        else:
            acc_ref = refs[-1]
            kk = pl.program_id(2)

            @pl.when(kk == 0)
            def _():
                acc_ref[...] = part

            @pl.when(kk > 0)
            def _():
                acc_ref[...] += part

            @pl.when(kk == nk - 1)
            def _():
                finish(acc_ref[...])

    if placed:
        b_spec = pl.BlockSpec((None, tn, tk), lambda i, j, kk: (j, b_off // tn, kk))
    else:
        b_spec = pl.BlockSpec((tk, tn), lambda i, j, kk: (kk, j)) if mode == "nn" else pl.BlockSpec((tn, tk), lambda i, j, kk: (j, kk))
    tile = pl.BlockSpec((tm, tn), lambda i, j, kk: (i, j))
    out = pl.pallas_call(
        body,
        name=name,
        grid=(m // tm, n // tn, nk),
        in_specs=[pl.BlockSpec((tm, tk), lambda i, j, kk: (i, kk)), b_spec] + [tile] * n_ex,
        out_specs=[tile] * n_out,
        out_shape=[jax.ShapeDtypeStruct((m, n), dt) for dt in out_dtypes],
        scratch_shapes=[pltpu.VMEM((tm, tn), F32)] if nk > 1 else [],
        compiler_params=_cp(("parallel", "parallel", "arbitrary")),
    )(a, b, *extras)
    return out[0] if n_out == 1 else out


def _mm_two(a, b1, b2, *, name, tm=512):
    m, k = a.shape
    tm = min(m, tm)

    def body(a_ref, b1_ref, b2_ref, o1_ref, o2_ref):
        av = a_ref[...]
        o1_ref[...] = _dot_nt(av, b1_ref[...])
        o2_ref[...] = _dot_nt(av, b2_ref[...])

    whole = lambda arr: pl.BlockSpec(arr.shape, lambda i: (0, 0))
    rows = lambda n: pl.BlockSpec((tm, n), lambda i: (i, 0))
    return pl.pallas_call(
        body,
        name=name,
        grid=(m // tm,),
        in_specs=[rows(k), whole(b1), whole(b2)],
        out_specs=[rows(b1.shape[0]), rows(b2.shape[0])],
        out_shape=[jax.ShapeDtypeStruct((m, b1.shape[0]), F32), jax.ShapeDtypeStruct((m, b2.shape[0]), F32)],
        compiler_params=_cp(("parallel",)),
    )(a, b1, b2)


def _mm_tn(a, b, *, name):
    s, m = a.shape
    n = b.shape[1]
    cap = max(128, (1 << 20) // n)
    tm = _pick(m, tuple(t for t in (512, 640, 384, 256, 128) if t <= max(cap, 128)))
    ts = min(s, 1 << (((1 << 22) // n).bit_length() - 1))
    ns = s // ts

    def body(a_ref, b_ref, o_ref):
        part = _dot_tn(a_ref[...], b_ref[...])
        if ns == 1:
            o_ref[...] = part
        else:
            ss = pl.program_id(1)

            @pl.when(ss == 0)
            def _():
                o_ref[...] = part

            @pl.when(ss > 0)
            def _():
                o_ref[...] += part

    return pl.pallas_call(
        body,
        name=name,
        grid=(m // tm, ns),
        in_specs=[pl.BlockSpec((ts, tm), lambda i, ss: (ss, i)), pl.BlockSpec((ts, n), lambda i, ss: (ss, 0))],
        out_specs=pl.BlockSpec((tm, n), lambda i, ss: (i, 0)),
        out_shape=jax.ShapeDtypeStruct((m, n), F32),
        compiler_params=_cp(("parallel", "arbitrary")),
    )(a, b)


RING = 3


def _mm_tn_into(a, b, packs, *, rows, off, name):
    s, m = a.shape
    n = b.shape[1]
    tm = 1024 if rows % 1024 == 0 and s >= 4096 else 512
    tr = min(tm, rows)
    per, chips = rows // tr, tm // tr
    ts = min(s, 1 << (((1 << 22) // (tm + n)).bit_length() - 1))
    ns = s // ts
    total = (m // tm) * ns

    def tiles(a_hbm, b_hbm, a_buf, b_buf, sems, t):
        t = jnp.int32(t)
        i, ss = t // ns, t % ns
        slot = t % RING
        return (
            pltpu.make_async_copy(a_hbm.at[pl.ds(ss * ts, ts), pl.ds(pl.multiple_of(i * tm, tm), tm)], a_buf.at[slot], sems.at[0, slot]),
            pltpu.make_async_copy(b_hbm.at[pl.ds(ss * ts, ts), :], b_buf.at[slot], sems.at[1, slot]),
        )

    def body(a_hbm, b_hbm, f_in, lo_in, f_ref, lo_ref, a_buf, b_buf, sems):
        t = pl.program_id(0) * ns + pl.program_id(1)

        @pl.when(t == 0)
        def _():
            for first in range(min(RING - 1, total)):
                for cp in tiles(a_hbm, b_hbm, a_buf, b_buf, sems, first):
                    cp.start()

        @pl.when(t + RING - 1 < total)
        def _():
            for cp in tiles(a_hbm, b_hbm, a_buf, b_buf, sems, t + RING - 1):
                cp.start()

        for cp in tiles(a_hbm, b_hbm, a_buf, b_buf, sems, t):
            cp.wait()
        slot = t % RING
        part = _dot_tn(a_buf[slot], b_buf[slot])
        pieces = [part[c * tr : (c + 1) * tr] for c in range(chips)]
        if ns == 1:
            for c, p in enumerate(pieces):
                f_ref[c] = p
                lo_ref[c] = p.astype(lo_ref.dtype)
        else:
            ss = pl.program_id(1)

            @pl.when(ss == 0)
            def _():
                for c, p in enumerate(pieces):
                    f_ref[c] = p

            @pl.when(ss > 0)
            def _():
                for c, p in enumerate(pieces):
                    f_ref[c] += p

            @pl.when(ss == ns - 1)
            def _():
                lo_ref[...] = f_ref[...].astype(lo_ref.dtype)

    spec = pl.BlockSpec((chips, tr, n), lambda i, ss: (i // per, off // tr + i % per, 0))
    return pl.pallas_call(
        body,
        name=name,
        grid=(m // tm, ns),
        in_specs=[ANY, ANY, ANY, ANY],
        out_specs=[spec, spec],
        out_shape=[jax.ShapeDtypeStruct(p.shape, p.dtype) for p in packs],
        input_output_aliases={2: 0, 3: 1},
        scratch_shapes=[pltpu.VMEM((RING, ts, tm), a.dtype), pltpu.VMEM((RING, ts, n), b.dtype), pltpu.SemaphoreType.DMA((2, RING))],
        compiler_params=_cp(("arbitrary", "arbitrary")),
    )(a, b, *packs)


def _mm_rows(a, b, *, mode, name, more=(), rows=(), vecs=(), out_rows=(), out_vecs=(), epilogue, tm=512):
    m, k = a.shape
    placed = isinstance(b, tuple)
    if placed:
        b, b_off, b_rows = b
        assert mode == "nn" and b_off % b_rows == 0 and k == N_CHIPS * b_rows
        n = b.shape[2]
        b_spec = pl.BlockSpec((N_CHIPS, b_rows, n), lambda i: (0, b_off // b_rows, 0))
    else:
        n = b.shape[1] if mode == "nn" else b.shape[0]
        b_spec = pl.BlockSpec(b.shape, lambda i: (0, 0))
    tm = min(m, tm)
    parts = 2 if tm % 256 == 0 else 1
    n_m, n_r, n_v, n_or, n_ov = 2 * len(more), len(rows), len(vecs), len(out_rows), len(out_vecs)

    def body(*refs):
        a_ref, b_ref = refs[:2]
        m_refs = refs[2 : 2 + n_m]
        rest = refs[2 + n_m :]
        r_refs = rest[:n_r]
        v_refs = rest[n_r : n_r + n_v]
        or_refs = rest[n_r + n_v : n_r + n_v + n_or]
        ov_refs = rest[n_r + n_v + n_or :]
        res_vecs = None
        bv = b_ref[...].reshape(k, n) if placed else b_ref[...]
        for p in range(parts):
            rs = slice(p * tm // parts, (p + 1) * tm // parts)
            acc = _dot(a_ref[rs, :], bv) if mode == "nn" else _dot_nt(a_ref[rs, :], bv)
            for a2_ref, b2_ref in zip(m_refs[0::2], m_refs[1::2]):
                acc = acc + _dot(a2_ref[rs, :], b2_ref[...])
            res_rows, part_vecs = epilogue(acc, [r[rs, :] for r in r_refs], [v[...] for v in v_refs])
            for o, r in zip(or_refs, res_rows):
                o[rs, :] = r.astype(o.dtype)
            res_vecs = part_vecs if res_vecs is None else [s + t for s, t in zip(res_vecs, part_vecs)]
        if n_ov:
            first = pl.program_id(0) == 0

            @pl.when(first)
            def _():
                for o, r in zip(ov_refs, res_vecs):
                    o[...] = r

            @pl.when(jnp.logical_not(first))
            def _():
                for o, r in zip(ov_refs, res_vecs):
                    o[...] += r

    tile = pl.BlockSpec((tm, n), lambda i: (i, 0))
    whole = lambda arr: pl.BlockSpec(arr.shape, lambda i: (0, 0))
    vec = lambda w: pl.BlockSpec((1, w), lambda i: (0, 0))
    out = pl.pallas_call(
        body,
        name=name,
        grid=(m // tm,),
        in_specs=[pl.BlockSpec((tm, k), lambda i: (i, 0)), b_spec]
        + [spec for a2, b2 in more for spec in (pl.BlockSpec((tm, a2.shape[1]), lambda i: (i, 0)), whole(b2))]
        + [tile] * n_r + [vec(v.shape[1]) for v in vecs],
        out_specs=[tile] * n_or + [vec(w) for w in out_vecs],
        out_shape=[jax.ShapeDtypeStruct((m, n), dt) for dt in out_rows] + [jax.ShapeDtypeStruct((1, w), F32) for w in out_vecs],
        compiler_params=_cp(("arbitrary",) if n_ov else ("parallel",)),
    )(a, b, *[x for pair in more for x in pair], *rows, *vecs)
    return out


def _ep_residual_norm(acc, rows, vecs):
    x = acc + rows[0]
    r = lax.rsqrt(jnp.mean(x * x, axis=-1, keepdims=True) + EPS)
    return [x, x * r * vecs[0]], []


def _ep_norm_bwd(acc, rows, vecs):
    dy = acc
    for extra in rows[2:]:
        dy = dy + extra
    x, dres = rows[0], rows[1]
    r = lax.rsqrt(jnp.mean(x * x, axis=-1, keepdims=True) + EPS)
    xh = x * r
    dxh = dy * vecs[0]
    dx = r * (dxh - xh * jnp.mean(dxh * xh, axis=-1, keepdims=True)) + dres
    return [dx, dx], [jnp.sum(dy * xh, axis=0, keepdims=True)]


def _ep_loss(acc, rows, vecs):
    x = acc + rows[0]
    d = x.shape[-1]
    r = lax.rsqrt(jnp.mean(x * x, axis=-1, keepdims=True) + EPS)
    xh = x * r
    err = xh * vecs[0] - rows[1]
    loss = jnp.zeros((1, 128), F32) + 0.5 * jnp.sum(jnp.mean(err * err, axis=-1, keepdims=True))
    dy = err * (1.0 / d)
    dxh = dy * vecs[0]
    dx = r * (dxh - xh * jnp.mean(dxh * xh, axis=-1, keepdims=True))
    return [dx, dx], [loss, jnp.sum(dy * xh, axis=0, keepdims=True)]


def _rms_fwd(x, g, *, name):
    s, d = x.shape
    ts = _rows(s, light=True, times=4)

    def body(x_ref, g_ref, o_ref):
        xf = x_ref[...]
        r = lax.rsqrt(jnp.mean(xf * xf, axis=-1, keepdims=True) + EPS)
        o_ref[...] = (xf * r * g_ref[...]).astype(o_ref.dtype)

    return pl.pallas_call(
        body,
        name=name,
        grid=(s // ts,),
        in_specs=[pl.BlockSpec((ts, d), lambda i: (i, 0)), pl.BlockSpec((1, d), lambda i: (0, 0))],
        out_specs=pl.BlockSpec((ts, d), lambda i: (i, 0)),
        out_shape=jax.ShapeDtypeStruct((s, d), _CD),
        compiler_params=_cp(("parallel",)),
    )(x, g)


def _rms_gain_grad(x, dy, *, name):
    s, d = x.shape
    ts = _rows(s)

    def body(x_ref, dy_ref, dg_ref):
        xf = x_ref[...]
        r = lax.rsqrt(jnp.mean(xf * xf, axis=-1, keepdims=True) + EPS)
        part = jnp.sum(dy_ref[...] * (xf * r), axis=0, keepdims=True)

        @pl.when(pl.program_id(0) == 0)
        def _():
            dg_ref[...] = part

        @pl.when(pl.program_id(0) > 0)
        def _():
            dg_ref[...] += part

    tile = pl.BlockSpec((ts, d), lambda i: (i, 0))
    return pl.pallas_call(
        body,
        name=name,
        grid=(s // ts,),
        in_specs=[tile, tile],
        out_specs=pl.BlockSpec((1, d), lambda i: (0, 0)),
        out_shape=jax.ShapeDtypeStruct((1, d), F32),
        compiler_params=_cp(("arbitrary",)),
    )(x, dy)


def _chunk_scan(v, row_in_chunk, suffix):
    t = v.shape[0]
    step = 1
    while step < GLA_CHUNK:
        if suffix:
            v = v + jnp.where(row_in_chunk < GLA_CHUNK - step, pltpu.roll(v, t - step, 0), 0.0)
        else:
            v = v + jnp.where(row_in_chunk >= step, pltpu.roll(v, step, 0), 0.0)
        step *= 2
    return v


def _gate_pre(lr, w_ref, b_ref):
    return _dot(lr, w_ref[...]) + b_ref[...]


def _gate_fwd(z, waf, wab, baf, bab, *, name):
    s = z.shape[0]
    ts = _rows(s, light=True, times=4)

    def body(lr_ref, waf_ref, wab_ref, baf_ref, bab_ref, bf_ref, bb_ref):
        lr = lr_ref[...]
        ric = lax.broadcasted_iota(jnp.int32, (ts, GLA_K_TOTAL), 0) & (GLA_CHUNK - 1)
        for w_ref, b_ref, o_ref, suffix in ((waf_ref, baf_ref, bf_ref, False), (wab_ref, bab_ref, bb_ref, True)):
            pre = _gate_pre(lr, w_ref, b_ref)
            la = (jnp.minimum(pre, 0.0) - jnp.log(1.0 + jnp.exp(-jnp.abs(pre)))) * GLA_GATE_SCALE
            o_ref[...] = _chunk_scan(la, ric, suffix)

    wspec = pl.BlockSpec((128, GLA_K_TOTAL), lambda i: (0, 0))
    bspec = pl.BlockSpec((1, GLA_K_TOTAL), lambda i: (0, 0))
    tile = pl.BlockSpec((ts, GLA_K_TOTAL), lambda i: (i, 0))
    return pl.pallas_call(
        body,
        name=name,
        grid=(s // ts,),
        in_specs=[pl.BlockSpec((ts, 128), lambda i: (i, LR_COL // 128)), wspec, wspec, bspec, bspec],
        out_specs=[tile, tile],
        out_shape=[jax.ShapeDtypeStruct((s, GLA_K_TOTAL), F32)] * 2,
        compiler_params=_cp(("parallel",)),
    )(z, waf, wab, baf, bab)


def _gate_bwd(z, waf, wab, baf, bab, dbf, dbb, dqkv_f, dqkv_b, *, name):
    s = z.shape[0]
    ts = _rows(s, light=True)

    def body(lr_ref, waf_ref, wab_ref, baf_ref, bab_ref, dbf_ref, dbb_ref, gf_ref, gb_ref, dzb_ref, dwf_ref, dwb_ref, dbaf_ref, dbab_ref):
        lr = lr_ref[...]
        ric = lax.broadcasted_iota(jnp.int32, (ts, GLA_K_TOTAL), 0) & (GLA_CHUNK - 1)
        first = pl.program_id(0) == 0
        dlr = None
        for w_ref, b_ref, db_ref, dw_ref, dbias_ref, suffix in (
            (waf_ref, baf_ref, dbf_ref, dwf_ref, dbaf_ref, True),
            (wab_ref, bab_ref, dbb_ref, dwb_ref, dbab_ref, False),
        ):
            pre = _gate_pre(lr, w_ref, b_ref)
            dla = _chunk_scan(db_ref[...], ric, suffix)
            dpre = dla * GLA_GATE_SCALE * _sigmoid(-pre)
            part = _dot_nt(dpre, w_ref[...])
            dlr = part if dlr is None else dlr + part
            dw = _dot_tn(lr, dpre)
            dbias = jnp.sum(dpre, axis=0, keepdims=True)

            @pl.when(first)
            def _():
                dw_ref[...] = dw
                dbias_ref[...] = dbias

            @pl.when(jnp.logical_not(first))
            def _():
                dw_ref[...] += dw
                dbias_ref[...] += dbias

        dqkv = gf_ref[...].astype(F32) + gb_ref[...].astype(F32)
        dzb_ref[...] = jnp.concatenate([dqkv, dlr], axis=1).astype(dzb_ref.dtype)

    wspec = pl.BlockSpec((128, GLA_K_TOTAL), lambda i: (0, 0))
    bspec = pl.BlockSpec((1, GLA_K_TOTAL), lambda i: (0, 0))
    tile = pl.BlockSpec((ts, GLA_K_TOTAL), lambda i: (i, 0))
    wide = pl.BlockSpec((ts, 2 * GLA_K_TOTAL + GLA_V_TOTAL), lambda i: (i, 0))
    return pl.pallas_call(
        body,
        name=name,
        grid=(s // ts,),
        in_specs=[pl.BlockSpec((ts, 128), lambda i: (i, LR_COL // 128)), wspec, wspec, bspec, bspec, tile, tile, wide, wide],
        out_specs=[pl.BlockSpec((ts, ZB_COLS), lambda i: (i, 0)), wspec, wspec, bspec, bspec],
        out_shape=[
            jax.ShapeDtypeStruct((s, ZB_COLS), _CD),
            jax.ShapeDtypeStruct((128, GLA_K_TOTAL), F32),
            jax.ShapeDtypeStruct((128, GLA_K_TOTAL), F32),
            jax.ShapeDtypeStruct((1, GLA_K_TOTAL), F32),
            jax.ShapeDtypeStruct((1, GLA_K_TOTAL), F32),
        ],
        compiler_params=_cp(("arbitrary",)),
    )(z, waf, wab, baf, bab, dbf, dbb, dqkv_f, dqkv_b)


def _gla_masks(rev):
    lane_head = lax.broadcasted_iota(jnp.int32, (1, GLA_K_TOTAL), 1) >> 6
    head_masks = [lane_head == h for h in range(GLA_HEADS)]
    t = lax.broadcasted_iota(jnp.int32, (GLA_HEADS * GLA_CHUNK, GLA_CHUNK), 0) & (GLA_CHUNK - 1)
    u = lax.broadcasted_iota(jnp.int32, (GLA_HEADS * GLA_CHUNK, GLA_CHUNK), 1)
    tri = (u > t) if rev else (u <= t)
    row = lax.broadcasted_iota(jnp.int32, (GLA_CHUNK, GLA_K_TOTAL), 0)
    total_row = row == (0 if rev else GLA_CHUNK - 1)
    return head_masks, tri, total_row


def _spread(a, head_masks):
    return jnp.concatenate([jnp.where(m, a, 0.0) for m in head_masks], axis=0)


def _stack(a):
    return jnp.concatenate([a[:, GLA_DV * h : GLA_DV * (h + 1)] for h in range(GLA_HEADS)], axis=0)


def _unstack(a):
    return jnp.concatenate([a[GLA_CHUNK * h : GLA_CHUNK * (h + 1)] for h in range(GLA_HEADS)], axis=1)


def _collect(a, head_masks):
    out = None
    for h, m in enumerate(head_masks):
        part = jnp.where(m, a[GLA_CHUNK * h : GLA_CHUNK * (h + 1)], 0.0)
        out = part if out is None else out + part
    return out


def _gla_chunk_terms(q_ref, k_ref, v_ref, b_ref, rows, head_masks, tri, total_row):
    q = q_ref[rows, :] * (GLA_DK**-0.5)
    k = k_ref[rows, :]
    v = v_ref[rows, :]
    b = b_ref[rows, :]
    eb = jnp.exp(b)
    enb = jnp.exp(-b)
    g = jnp.sum(jnp.where(total_row, b, 0.0), axis=0, keepdims=True)
    egb = jnp.exp(g - b)
    qt = q * eb
    kt = k * enb
    kh = k * egb
    q_heads = _spread(qt, head_masks)
    attn = jnp.where(tri, _dot_nt(q_heads, kt), 0.0)
    return v, eb, enb, egb, jnp.exp(g), qt, kt, kh, q_heads, attn


def _gla_specs(s, tb, rev_blocks):
    nb = s // tb
    rb = (lambda i: nb - 1 - i) if rev_blocks else (lambda i: i)
    q_spec = pl.BlockSpec((tb, GLA_K_TOTAL), lambda i: (rb(i), 0))
    k_spec = pl.BlockSpec((tb, GLA_K_TOTAL), lambda i: (rb(i), 1))
    v_spec = pl.BlockSpec((tb, GLA_V_TOTAL), lambda i: (rb(i), 1))
    b_spec = pl.BlockSpec((tb, GLA_K_TOTAL), lambda i: (rb(i), 0))
    o_spec = pl.BlockSpec((tb, GLA_V_TOTAL), lambda i: (rb(i), 0))
    st_spec = pl.BlockSpec((tb // GLA_CHUNK, GLA_DV, GLA_K_TOTAL), lambda i: (rb(i), 0, 0))
    return nb, q_spec, k_spec, v_spec, b_spec, o_spec, st_spec


def _gla_fwd_chunk(cidx, q_ref, k_ref, v_ref, b_ref, o_ref, sv_ref, st_ref, masks):
    head_masks, tri, total_row = masks
    rows = pl.ds(pl.multiple_of(cidx * GLA_CHUNK, GLA_CHUNK), GLA_CHUNK)
    v, _, _, _, eg, _, _, kh, q_heads, attn = _gla_chunk_terms(q_ref, k_ref, v_ref, b_ref, rows, head_masks, tri, total_row)
    o = jnp.concatenate(
        [_dot(attn[GLA_CHUNK * h : GLA_CHUNK * (h + 1)], v[:, GLA_DV * h : GLA_DV * (h + 1)]) for h in range(GLA_HEADS)], axis=1
    )
    st = st_ref[...]
    o_ref[rows, :] = o + _unstack(_dot_nt(q_heads, st))
    sv_ref[cidx] = st
    st_ref[...] = st * eg + _dot_tn(_stack(v), _spread(kh, head_masks))


def _gla_fwd(z, b_f, b_b, *, name):
    s = z.shape[0]
    tb = _rows(s)
    cpb = tb // GLA_CHUNK
    nb, qf, kf, vf, bf, of, sf = _gla_specs(s, tb, False)
    _, qr, kr, vr, br, orr, sr = _gla_specs(s, tb, True)

    def body(qf_ref, kf_ref, vf_ref, bf_ref, qr_ref, kr_ref, vr_ref, br_ref, of_ref, svf_ref, or_ref, svr_ref, stf_ref, str_ref):
        masks_f, masks_r = _gla_masks(False), _gla_masks(True)

        @pl.when(pl.program_id(0) == 0)
        def _():
            stf_ref[...] = jnp.zeros_like(stf_ref)
            str_ref[...] = jnp.zeros_like(str_ref)

        def chunk(ci, carry):
            _gla_fwd_chunk(ci, qf_ref, kf_ref, vf_ref, bf_ref, of_ref, svf_ref, stf_ref, masks_f)
            _gla_fwd_chunk(cpb - 1 - ci, qr_ref, kr_ref, vr_ref, br_ref, or_ref, svr_ref, str_ref, masks_r)
            return carry

        lax.fori_loop(0, cpb, chunk, 0)

    o_shape = jax.ShapeDtypeStruct((s, GLA_V_TOTAL), F32)
    st_shape = jax.ShapeDtypeStruct((s // GLA_CHUNK, GLA_DV, GLA_K_TOTAL), F32)
    return pl.pallas_call(
        body,
        name=name,
        grid=(nb,),
        in_specs=[qf, kf, vf, bf, qr, kr, vr, br],
        out_specs=[of, sf, orr, sr],
        out_shape=[o_shape, st_shape, o_shape, st_shape],
        scratch_shapes=[pltpu.VMEM((GLA_DV, GLA_K_TOTAL), F32)] * 2,
        compiler_params=_cp(("arbitrary",)),
    )(z, z, z, b_f, z, z, z, b_b)


def _gla_bwd_chunk(cidx, q_ref, k_ref, v_ref, b_ref, do_ref, sv_ref, dqkv_ref, db_ref, dst_ref, masks):
    head_masks, tri, total_row = masks
    rows = pl.ds(pl.multiple_of(cidx * GLA_CHUNK, GLA_CHUNK), GLA_CHUNK)
    v, eb, enb, egb, eg, qt, kt, kh, q_heads, attn = _gla_chunk_terms(q_ref, k_ref, v_ref, b_ref, rows, head_masks, tri, total_row)
    do_c = do_ref[rows, :]
    st = sv_ref[cidx]
    dst = dst_ref[...]
    do_s, v_s = _stack(do_c), _stack(v)
    hs = lambda a, h: a[GLA_CHUNK * h : GLA_CHUNK * (h + 1)]
    vs = lambda a, h: a[:, GLA_DV * h : GLA_DV * (h + 1)]
    dattn = jnp.concatenate([_dot_nt(vs(do_c, h), vs(v, h)) for h in range(GLA_HEADS)], axis=0)
    dattn = jnp.where(tri, dattn, 0.0)
    dv = jnp.concatenate([_dot_tn(hs(attn, h), vs(do_c, h)) for h in range(GLA_HEADS)], axis=1)
    dv = dv + _unstack(_dot_nt(_spread(kh, head_masks), dst))
    dqt = _collect(_dot(do_s, st) + _dot(dattn, kt), head_masks)
    dkt = _dot_tn(dattn, q_heads)
    dkh = _collect(_dot(v_s, dst), head_masks)
    dg = jnp.sum(dkh * kh, axis=0, keepdims=True) + jnp.sum(dst * st, axis=0, keepdims=True) * eg
    db = dqt * qt - dkt * kt - dkh * kh + jnp.where(total_row, dg, 0.0)
    dq = dqt * eb * (GLA_DK**-0.5)
    dk = dkt * enb + dkh * egb
    dqkv_ref[rows, :] = jnp.concatenate([dq, dk, dv], axis=1).astype(dqkv_ref.dtype)
    db_ref[rows, :] = db
    dst_ref[...] = dst * eg + _dot_tn(do_s, q_heads)


def _gla_bwd(z, b_f, b_b, do, st_f, st_b, *, name):
    s = z.shape[0]
    tb = _rows(s)
    cpb = tb // GLA_CHUNK
    wide = 2 * GLA_K_TOTAL + GLA_V_TOTAL
    nb, qf, kf, vf, bf, of, sf = _gla_specs(s, tb, True)
    _, qr, kr, vr, br, orr, sr = _gla_specs(s, tb, False)
    gf = pl.BlockSpec((tb, wide), lambda i: (nb - 1 - i, 0))
    gr = pl.BlockSpec((tb, wide), lambda i: (i, 0))

    def body(qf_ref, kf_ref, vf_ref, bf_ref, dof_ref, svf_ref, qr_ref, kr_ref, vr_ref, br_ref, dor_ref, svr_ref,
             gf_ref, dbf_ref, gr_ref, dbr_ref, dstf_ref, dstr_ref):
        masks_f, masks_r = _gla_masks(False), _gla_masks(True)

        @pl.when(pl.program_id(0) == 0)
        def _():
            dstf_ref[...] = jnp.zeros_like(dstf_ref)
            dstr_ref[...] = jnp.zeros_like(dstr_ref)

        def chunk(ci, carry):
            _gla_bwd_chunk(cpb - 1 - ci, qf_ref, kf_ref, vf_ref, bf_ref, dof_ref, svf_ref, gf_ref, dbf_ref, dstf_ref, masks_f)
            _gla_bwd_chunk(ci, qr_ref, kr_ref, vr_ref, br_ref, dor_ref, svr_ref, gr_ref, dbr_ref, dstr_ref, masks_r)
            return carry

        lax.fori_loop(0, cpb, chunk, 0)

    g_shape = jax.ShapeDtypeStruct((s, wide), _CD)
    db_shape = jax.ShapeDtypeStruct((s, GLA_K_TOTAL), F32)
    return pl.pallas_call(
        body,
        name=name,
        grid=(nb,),
        in_specs=[qf, kf, vf, bf, of, sf, qr, kr, vr, br, orr, sr],
        out_specs=[gf, bf, gr, br],
        out_shape=[g_shape, db_shape, g_shape, db_shape],
        scratch_shapes=[pltpu.VMEM((GLA_DV, GLA_K_TOTAL), F32)] * 2,
        compiler_params=_cp(("arbitrary",)),
    )(z, z, z, b_f, do, st_f, z, z, z, b_b, do, st_b)


HALO = 8


def _halo_specs(s, ts, width, col):
    last = s // HALO - 1
    per = ts // HALO
    prev = pl.BlockSpec((HALO, width), lambda i: (jnp.maximum(i * per - 1, 0), col))
    nxt = pl.BlockSpec((HALO, width), lambda i: (jnp.minimum((i + 1) * per, last), col))
    return prev, nxt


def _group_ones():
    group = jnp.arange(CONV_WIDTH, dtype=jnp.int32) // CONV_GROUP
    return (group[:, None] == group[None, :]).astype(BF16)


_ONES_SPEC = pl.BlockSpec((CONV_WIDTH, CONV_WIDTH), lambda i: (0, 0))


def _conv_terms(cc_ext, cu_ext, cw, valid):
    n = cc_ext.shape[0]
    hc = jnp.where(valid, cc_ext * cu_ext, 0.0)
    hc_prev = pltpu.roll(hc, 1, 0)
    hc_next = pltpu.roll(hc, n - 1, 0)
    conv = cw[0:1] * hc_prev + cw[1:2] * hc + cw[2:3] * hc_next
    return hc, hc_prev, hc_next, conv


def _ext(prev_ref, cur_ref, next_ref):
    return jnp.concatenate([prev_ref[...], cur_ref[...], next_ref[...]], axis=0)


def _valid_rows(ts, s):
    row = lax.broadcasted_iota(jnp.int32, (ts + 2 * HALO, 1), 0) + (pl.program_id(0) * ts - HALO)
    return (row >= 0) & (row < s)


def _head_norm(o, gn):
    out = []
    for h in range(GLA_HEADS):
        oh = o[:, GLA_DV * h : GLA_DV * (h + 1)]
        r = lax.rsqrt(jnp.mean(oh * oh, axis=-1, keepdims=True) + EPS)
        out.append((oh * r, r))
    return out


def _mix_fwd(z, o_f, o_b, conv_w, conv_norm, gla_norm, *, name):
    s = z.shape[0]
    ts = _rows(s, light=True)
    cprev, cnext = _halo_specs(s, ts, CONV_WIDTH, 1)
    uprev, unext = _halo_specs(s, ts, CONV_WIDTH, 2)

    def body(cb_ref, cc_ref, cu_ref, ccp_ref, ccn_ref, cup_ref, cun_ref, g_ref, of_ref, ob_ref, cw_ref, cn_ref, gn_ref, ones_ref, y_ref):
        valid = _valid_rows(ts, s)
        _, _, _, conv = _conv_terms(_ext(ccp_ref, cc_ref, ccn_ref), _ext(cup_ref, cu_ref, cun_ref), cw_ref[...], valid)
        yc = cb_ref[...] * conv[HALO : HALO + ts]
        ms = _dot_split(yc * yc, ones_ref[...]) * (1.0 / CONV_GROUP)
        y_conv = yc * lax.rsqrt(ms + EPS) * cn_ref[...]
        gate = g_ref[...]
        silu = gate * _sigmoid(gate)
        gn = gn_ref[...]
        y_gla = jnp.concatenate([oh * gn for oh, _ in _head_norm(of_ref[...] + ob_ref[...], gn)], axis=1) * silu
        y_ref[...] = jnp.concatenate([y_conv, y_gla], axis=1).astype(y_ref.dtype)

    col = lambda c, w=CONV_WIDTH: pl.BlockSpec((ts, w), lambda i: (i, c))
    return pl.pallas_call(
        body,
        name=name,
        grid=(s // ts,),
        in_specs=[col(0), col(1), col(2), cprev, cnext, uprev, unext, col(3), col(0), col(0),
                  pl.BlockSpec((CONV_K, CONV_WIDTH), lambda i: (0, 0)), pl.BlockSpec((1, CONV_WIDTH), lambda i: (0, 0)),
                  pl.BlockSpec((1, GLA_DV), lambda i: (0, 0)), _ONES_SPEC],
        out_specs=pl.BlockSpec((ts, D_MODEL), lambda i: (i, 0)),
        out_shape=jax.ShapeDtypeStruct((s, D_MODEL), _CD),
        compiler_params=_cp(("parallel",)),
    )(z, z, z, z, z, z, z, z, o_f, o_b, conv_w, conv_norm, gla_norm, _group_ones())


def _mix_bwd(z, o_f, o_b, dy, conv_w, conv_norm, gla_norm, *, name):
    s = z.shape[0]
    ts = _rows(s)
    halos = [_halo_specs(s, ts, CONV_WIDTH, c) for c in (0, 1, 2)]
    dprev, dnext = _halo_specs(s, ts, CONV_WIDTH, 0)

    def body(cb_ref, cc_ref, cu_ref, cbp_ref, cbn_ref, ccp_ref, ccn_ref, cup_ref, cun_ref, g_ref, of_ref, ob_ref,
             dyc_ref, dyg_ref, dyp_ref, dyn_ref, cw_ref, cn_ref, gn_ref, ones_ref, dza_ref, do_ref, dcw_ref, dcn_ref, dgn_ref):
        n = ts + 2 * HALO
        valid = _valid_rows(ts, s)
        cw = cw_ref[...]
        cn = cn_ref[...]
        ones = ones_ref[...]
        cb = _ext(cbp_ref, cb_ref, cbn_ref)
        cc = _ext(ccp_ref, cc_ref, ccn_ref)
        cu = _ext(cup_ref, cu_ref, cun_ref)
        dy = _ext(dyp_ref, dyc_ref, dyn_ref)
        hc, hc_prev, hc_next, conv = _conv_terms(cc, cu, cw, valid)
        yc = cb * conv
        r = lax.rsqrt(_dot_split(yc * yc, ones) * (1.0 / CONV_GROUP) + EPS)
        yh = yc * r
        dyh = dy * cn
        dyc = r * (dyh - yh * (_dot_split(dyh * yh, ones) * (1.0 / CONV_GROUP)))
        dconv = jnp.where(valid, dyc * cb, 0.0)
        dhc = cw[0:1] * pltpu.roll(dconv, n - 1, 0) + cw[1:2] * dconv + cw[2:3] * pltpu.roll(dconv, 1, 0)
        mid = lambda a: a[HALO : HALO + ts]
        dza_ref[:, 0 : 3 * CONV_WIDTH] = jnp.concatenate([mid(dyc * conv), mid(dhc * cu), mid(dhc * cc)], axis=1).astype(dza_ref.dtype)
        dconv_m = mid(dconv)
        colsum = lambda a: jnp.sum(a, axis=0, keepdims=True)
        dcw = jnp.concatenate([colsum(dconv_m * mid(hc_prev)), colsum(dconv_m * mid(hc)), colsum(dconv_m * mid(hc_next))], axis=0)
        dcn = colsum(mid(dy * yh))

        gate = g_ref[...]
        sg = _sigmoid(gate)
        silu = gate * sg
        gn = gn_ref[...]
        dyg = dyg_ref[...]
        don = dyg * silu
        heads = _head_norm(of_ref[...] + ob_ref[...], gn)
        on = jnp.concatenate([oh * gn for oh, _ in heads], axis=1)
        dza_ref[:, 3 * CONV_WIDTH : ZA_COLS] = (dyg * on * (sg * (1.0 + gate * (1.0 - sg)))).astype(dza_ref.dtype)
        dgn = jnp.zeros((1, GLA_DV), F32)
        dos = []
        for h, (oh, rh) in enumerate(heads):
            donh = don[:, GLA_DV * h : GLA_DV * (h + 1)]
            dgn = dgn + colsum(donh * oh)
            doh = donh * gn
            dos.append(rh * (doh - oh * jnp.mean(doh * oh, axis=-1, keepdims=True)))
        do_ref[...] = jnp.concatenate(dos, axis=1)

        first = pl.program_id(0) == 0

        @pl.when(first)
        def _():
            dcw_ref[...] = dcw
            dcn_ref[...] = dcn
            dgn_ref[...] = dgn

        @pl.when(jnp.logical_not(first))
        def _():
            dcw_ref[...] += dcw
            dcn_ref[...] += dcn
            dgn_ref[...] += dgn

    col = lambda c, w=CONV_WIDTH: pl.BlockSpec((ts, w), lambda i: (i, c))
    cw_spec = pl.BlockSpec((CONV_K, CONV_WIDTH), lambda i: (0, 0))
    cn_spec = pl.BlockSpec((1, CONV_WIDTH), lambda i: (0, 0))
    gn_spec = pl.BlockSpec((1, GLA_DV), lambda i: (0, 0))
    return pl.pallas_call(
        body,
        name=name,
        grid=(s // ts,),
        in_specs=[col(0), col(1), col(2), halos[0][0], halos[0][1], halos[1][0], halos[1][1], halos[2][0], halos[2][1],
                  col(3), col(0), col(0), col(0), col(1), dprev, dnext, cw_spec, cn_spec, gn_spec, _ONES_SPEC],
        out_specs=[pl.BlockSpec((ts, ZA_COLS), lambda i: (i, 0)), col(0), cw_spec, cn_spec, gn_spec],
        out_shape=[
            jax.ShapeDtypeStruct((s, ZA_COLS), _CD),
            jax.ShapeDtypeStruct((s, GLA_V_TOTAL), F32),
            jax.ShapeDtypeStruct((CONV_K, CONV_WIDTH), F32),
            jax.ShapeDtypeStruct((1, CONV_WIDTH), F32),
            jax.ShapeDtypeStruct((1, GLA_DV), F32),
        ],
        compiler_params=_cp(("arbitrary",)),
    )(z, z, z, z, z, z, z, z, z, z, o_f, o_b, dy, dy, dy, dy, conv_w, conv_norm, gla_norm, _group_ones())


def _xa_probs(q_ref, kv_ref, h):
    qh = q_ref[:, XA_HEAD_DIM * h : XA_HEAD_DIM * (h + 1)]
    kh = kv_ref[:, XA_HEAD_DIM * h : XA_HEAD_DIM * (h + 1)]
    vh = kv_ref[:, D_MODEL + XA_HEAD_DIM * h : D_MODEL + XA_HEAD_DIM * (h + 1)]
    sc = _dot_nt(qh, kh) * (XA_HEAD_DIM**-0.5)
    e = jnp.exp(sc - jnp.max(sc, axis=-1, keepdims=True))
    return qh, kh, vh, e / jnp.sum(e, axis=-1, keepdims=True)


def _xattn_block(hx, w_xq, kv, w_xo, x1, gain, *, name):
    s, d = hx.shape
    ts = _rows(s)

    def body(h_ref, wq_ref, kv_ref, wo_ref, x_ref, g_ref, q_out, o_out, hm_out, x_out):
        q = _dot(h_ref[...], wq_ref[...]).astype(_CD)
        q_out[...] = q
        heads = []
        for h in range(XA_HEADS):
            _, _, vh, p = _xa_probs(q, kv_ref, h)
            heads.append(_dot(p, vh))
        o = jnp.concatenate(heads, axis=1).astype(_CD)
        o_out[...] = o
        x = _dot(o, wo_ref[...]) + x_ref[...]
        r = lax.rsqrt(jnp.mean(x * x, axis=-1, keepdims=True) + EPS)
        x_out[...] = x
        hm_out[...] = (x * r * g_ref[...]).astype(hm_out.dtype)

    tile = pl.BlockSpec((ts, d), lambda i: (i, 0))
    whole = lambda arr: pl.BlockSpec(arr.shape, lambda i: (0, 0))
    lo = jax.ShapeDtypeStruct((s, d), _CD)
    return pl.pallas_call(
        body,
        name=name,
        grid=(s // ts,),
        in_specs=[tile, whole(w_xq), whole(kv), whole(w_xo), tile, whole(gain)],
        out_specs=[tile] * 4,
        out_shape=[lo, lo, lo, jax.ShapeDtypeStruct((s, d), F32)],
        compiler_params=_cp(("parallel",)),
    )(hx, w_xq, kv, w_xo, x1, gain)


def _xattn_bwd(qx, kv, dx, w_xo, *, name):
    s = qx.shape[0]
    ts = _rows(s, light=True)

    def body(q_ref, kv_ref, dx_ref, w_ref, dq_ref, dkv_ref):
        do = _dot_nt(dx_ref[...], w_ref[...]).astype(_CD)
        dqs, dks, dvs = [], [], []
        for h in range(XA_HEADS):
            qh, kh, vh, p = _xa_probs(q_ref, kv_ref, h)
            doh = do[:, XA_HEAD_DIM * h : XA_HEAD_DIM * (h + 1)]
            dp = _dot_nt(doh, vh)
            ds = p * (dp - jnp.sum(dp * p, axis=-1, keepdims=True)) * (XA_HEAD_DIM**-0.5)
            dqs.append(_dot(ds, kh))
            dks.append(_dot_tn(ds, qh))
            dvs.append(_dot_tn(p, doh))
        dq_ref[...] = jnp.concatenate(dqs, axis=1).astype(dq_ref.dtype)
        dkv = jnp.concatenate(dks + dvs, axis=1)

        @pl.when(pl.program_id(0) == 0)
        def _():
            dkv_ref[...] = dkv

        @pl.when(pl.program_id(0) > 0)
        def _():
            dkv_ref[...] += dkv

    tile = pl.BlockSpec((ts, D_MODEL), lambda i: (i, 0))
    kv_spec = pl.BlockSpec((N_MEM, 2 * D_MODEL), lambda i: (0, 0))
    return pl.pallas_call(
        body,
        name=name,
        grid=(s // ts,),
        in_specs=[tile, kv_spec, tile, pl.BlockSpec((D_MODEL, D_MODEL), lambda i: (0, 0))],
        out_specs=[tile, kv_spec],
        out_shape=[jax.ShapeDtypeStruct((s, D_MODEL), _CD), jax.ShapeDtypeStruct((N_MEM, 2 * D_MODEL), F32)],
        compiler_params=_cp(("arbitrary",)),
    )(qx, kv, dx, w_xo)


def _adamw_math(w, g, m, v):
    m = ADAM_B1 * m + (1.0 - ADAM_B1) * g
    v = ADAM_B2 * v + (1.0 - ADAM_B2) * (g * g)
    m_hat = m / (1.0 - ADAM_B1**ADAM_STEP)
    v_hat = v / (1.0 - ADAM_B2**ADAM_STEP)
    delta = -ADAM_LR * (m_hat / (jnp.sqrt(v_hat) + ADAM_EPS) + ADAM_WD * w)
    return delta, m, v


def _adamw(w, m, v, shard_rows, off, *, transposed, name):
    r, c = w.shape
    by_columns = r % 256 != 0
    tr = 512 if (c if by_columns else r) % 512 == 0 and off % 512 == 0 else 256
    if by_columns:
        assert not transposed and off == 0
        g_spec = tile = pl.BlockSpec((r, tr), lambda i: (0, i))
    else:
        g_spec = pl.BlockSpec((c, tr), lambda i: (off // c, i)) if transposed else pl.BlockSpec((tr, c), lambda i: (off // tr + i, 0))
        tile = pl.BlockSpec((tr, c), lambda i: (i, 0))

    def body(w_ref, g_ref, m_ref, v_ref, go_ref, d_ref, nm_ref, nv_ref):
        g = g_ref[...].T if transposed else g_ref[...]
        go_ref[...] = g
        d_ref[...], nm_ref[...], nv_ref[...] = _adamw_math(w_ref[...], g, m_ref[...], v_ref[...])

    return pl.pallas_call(
        body,
        name=name,
        grid=((c if by_columns else r) // tr,),
        in_specs=[tile, g_spec, tile, tile],
        out_specs=[tile] * 4,
        out_shape=[jax.ShapeDtypeStruct((r, c), F32)] * 4,
        compiler_params=_cp(("parallel",), vmem=_VMEM_STREAM),
    )(w, shard_rows, m, v)


def _adamw_small(groups, *, name):
    n = len(groups)

    def body(*refs):
        ins, outs = refs[: 4 * n], refs[4 * n :]
        for i in range(n):
            w_ref, g_ref, m_ref, v_ref = ins[4 * i : 4 * i + 4]
            outs[3 * i][...], outs[3 * i + 1][...], outs[3 * i + 2][...] = _adamw_math(w_ref[...], g_ref[...], m_ref[...], v_ref[...])

    flat = [a for grp in groups for a in grp]
    vm = pl.BlockSpec(memory_space=pltpu.VMEM)
    res = pl.pallas_call(
        body,
        name=name,
        in_specs=[vm] * (4 * n),
        out_specs=[vm] * (3 * n),
        out_shape=[jax.ShapeDtypeStruct(grp[0].shape, F32) for grp in groups for _ in range(3)],
        compiler_params=_cp(),
    )(*flat)
    return [tuple(res[3 * i : 3 * i + 3]) for i in range(n)]


def _place():
    return lax.axis_index("x"), lax.axis_index("y"), lax.axis_index("c")


def _rel_chip(x, y, k):
    return (1 - x if k & 2 else x), (1 - y if k & 1 else y)


def _half(c, rh):
    return pl.ds(pl.multiple_of(c * rh, 16), rh)


HBM = pl.BlockSpec(memory_space=pltpu.HBM)
SEM = pl.BlockSpec(memory_space=pltpu.SEMAPHORE)
EFFECT = pltpu.SideEffectType.DATAFLOW_SIDE_EFFECTING


def _in_hbm(a):
    return pltpu.with_memory_space_constraint(a, pltpu.HBM)


def _gather_copies(p_ref, land_ref, send_sems, recv_sems):
    rh = p_ref.shape[0] // 2
    x, y, c = _place()
    rows = _half(c, rh)
    copies = []
    for k in range(1, N_CHIPS):
        cx, cy = _rel_chip(x, y, k)
        copies.append(pltpu.make_async_remote_copy(
            src_ref=p_ref.at[rows], dst_ref=land_ref.at[2 * x + y, rows], send_sem=send_sems.at[k - 1], recv_sem=recv_sems.at[k - 1],
            device_id=(cx, cy, c), device_id_type=MESH))
    copies.append(pltpu.make_async_remote_copy(
        src_ref=p_ref, dst_ref=land_ref.at[2 * x + y], send_sem=send_sems.at[N_CHIPS - 1], recv_sem=recv_sems.at[N_CHIPS - 1],
        device_id=(x, y, 1 - c), device_id_type=MESH))
    return copies


def _gather_start(pack, after, *, name):
    r, w = pack.shape

    def body(p_ref, land_ref, after_ref, send_sems, recv_sems, p_thru, land_thru, token):
        for cp in _gather_copies(p_ref, land_ref, send_sems, recv_sems):
            cp.start()
        token[...] = jnp.zeros_like(token)

    return pl.pallas_call(
        body,
        name=name,
        out_shape=(pltpu.SemaphoreType.DMA((N_CHIPS,)), pltpu.SemaphoreType.DMA((N_CHIPS,)), pltpu.HBM((r, w), pack.dtype),
                   pltpu.HBM((N_CHIPS, r, w), pack.dtype), jax.ShapeDtypeStruct((8, 128), F32)),
        in_specs=(HBM, HBM, ANY),
        out_specs=(SEM, SEM, HBM, HBM, pl.BlockSpec(memory_space=pltpu.VMEM)),
        input_output_aliases={0: 2, 1: 3},
        compiler_params=pltpu.CompilerParams(has_side_effects=EFFECT),
    )(_in_hbm(pack), _in_hbm(lax.empty((N_CHIPS, r, w), pack.dtype)), after)


def _gather_wait(send_sems, recv_sems, pack, land, after, *, name):
    def body(p_ref, land_ref, send_sems, recv_sems, after_ref, p_out, land_out):
        for cp in _gather_copies(p_ref, land_ref, send_sems, recv_sems):
            cp.wait_send()
            cp.wait_recv()

    return pl.pallas_call(
        body,
        name=name,
        out_shape=(pltpu.HBM(pack.shape, pack.dtype), pltpu.HBM(land.shape, land.dtype)),
        in_specs=(HBM, HBM, SEM, SEM, ANY),
        out_specs=(HBM, HBM),
        input_output_aliases={0: 0, 1: 1},
        compiler_params=pltpu.CompilerParams(has_side_effects=EFFECT),
    )(pack, land, send_sems, recv_sems, after)


def _gather_spread(land, *, name):
    n, r, w = land.shape
    rh = r // 2

    def body(land_ref, o_ref, send_sems, recv_sems):
        x, y, c = _place()
        rows = _half(c, rh)
        copies = []
        for k in range(1, N_CHIPS):
            cx, cy = _rel_chip(x, y, k)
            copies.append(pltpu.make_async_remote_copy(
                src_ref=land_ref.at[2 * cx + cy, rows], dst_ref=o_ref.at[2 * cx + cy, rows], send_sem=send_sems.at[k - 1],
                recv_sem=recv_sems.at[k - 1], device_id=(x, y, 1 - c), device_id_type=MESH))
        for cp in copies:
            cp.start()
        for cp in copies:
            cp.wait()

    return pl.pallas_call(
        body,
        name=name,
        in_specs=[ANY],
        out_specs=ANY,
        out_shape=jax.ShapeDtypeStruct(land.shape, land.dtype),
        input_output_aliases={0: 0},
        scratch_shapes=[pltpu.SemaphoreType.DMA((N_CHIPS - 1,)), pltpu.SemaphoreType.DMA((N_CHIPS - 1,))],
        compiler_params=pltpu.CompilerParams(has_side_effects=True),
    )(land)


N_PARTS = 2 * (N_CHIPS - 1)


def _scatter_copies(lo_ref, g_ref, land_lo_ref, land_f_ref, send_sems, recv_sems, starting):
    rh = g_ref.shape[1] // 2
    x, y, c = _place()
    copies = []
    for k in range(1, N_CHIPS):
        cx, cy = _rel_chip(x, y, k)
        for i in range(2):
            part = 2 * (k - 1) + (c if starting else i)
            copies.append(pltpu.make_async_remote_copy(
                src_ref=lo_ref.at[2 * cx + cy, pl.ds(i * rh, rh)], dst_ref=land_lo_ref.at[part],
                send_sem=send_sems.at[2 * (k - 1) + i], recv_sem=recv_sems.at[part], device_id=(cx, cy, i), device_id_type=MESH))
    copies.append(pltpu.make_async_remote_copy(
        src_ref=g_ref.at[2 * x + y, _half(1 - c, rh)], dst_ref=land_f_ref, send_sem=send_sems.at[N_PARTS], recv_sem=recv_sems.at[N_PARTS],
        device_id=(x, y, 1 - c), device_id_type=MESH))
    return copies


def _scatter_start(g_lo, g, *, name):
    n, r, w = g.shape
    rh = r // 2

    def body(lo_ref, g_ref, land_lo_ref, land_f_ref, send_sems, recv_sems, lo_thru, g_thru, land_lo_thru, land_f_thru, token):
        for cp in _scatter_copies(lo_ref, g_ref, land_lo_ref, land_f_ref, send_sems, recv_sems, True):
            cp.start()
        token[...] = jnp.zeros_like(token)

    return pl.pallas_call(
        body,
        name=name,
        out_shape=(pltpu.SemaphoreType.DMA((N_PARTS + 1,)), pltpu.SemaphoreType.DMA((N_PARTS + 1,)), pltpu.HBM(g_lo.shape, g_lo.dtype),
                   pltpu.HBM(g.shape, g.dtype), pltpu.HBM((N_PARTS, rh, w), g_lo.dtype), pltpu.HBM((rh, w), g.dtype),
                   jax.ShapeDtypeStruct((8, 128), F32)),
        in_specs=(HBM, HBM, HBM, HBM),
        out_specs=(SEM, SEM, HBM, HBM, HBM, HBM, pl.BlockSpec(memory_space=pltpu.VMEM)),
        input_output_aliases={0: 2, 1: 3, 2: 4, 3: 5},
        compiler_params=pltpu.CompilerParams(has_side_effects=EFFECT),
    )(_in_hbm(g_lo), _in_hbm(g), _in_hbm(lax.empty((N_PARTS, rh, w), g_lo.dtype)), _in_hbm(lax.empty((rh, w), g.dtype)))


def _scatter_wait(send_sems, recv_sems, g_lo, g, land_lo, land_f, after, *, name):
    def body(lo_ref, g_ref, land_lo_ref, land_f_ref, send_sems, recv_sems, after_ref, o0, o1, o2, o3):
        for cp in _scatter_copies(lo_ref, g_ref, land_lo_ref, land_f_ref, send_sems, recv_sems, False):
            cp.wait_send()
            cp.wait_recv()

    arrays = (g_lo, g, land_lo, land_f)
    return pl.pallas_call(
        body,
        name=name,
        out_shape=tuple(pltpu.HBM(a.shape, a.dtype) for a in arrays),
        in_specs=(HBM, HBM, HBM, HBM, SEM, SEM, ANY),
        out_specs=(HBM, HBM, HBM, HBM),
        input_output_aliases={0: 0, 1: 1, 2: 2, 3: 3},
        compiler_params=pltpu.CompilerParams(has_side_effects=EFFECT),
    )(*arrays, send_sems, recv_sems, after)


def _scatter_sum(g, land_lo, land_f, where, *, name):
    n, r, w = g.shape
    rh = r // 2
    tr = _pick(rh, (256, 160, 80))
    nt = rh // tr

    def body(where_ref, g_ref, f_ref, lo_ref, o_ref):
        acc = g_ref[0] + f_ref[...]
        for part in range(N_PARTS):
            acc = acc + lo_ref[part].astype(F32)
        o_ref[...] = acc

    return pl.pallas_call(
        body,
        name=name,
        grid_spec=pltpu.PrefetchScalarGridSpec(
            num_scalar_prefetch=1,
            grid=(nt,),
            in_specs=[pl.BlockSpec((1, tr, w), lambda i, wh: (wh[1], wh[0] * nt + i, 0)),
                      pl.BlockSpec((tr, w), lambda i, wh: (i, 0)),
                      pl.BlockSpec((N_PARTS, tr, w), lambda i, wh: (0, i, 0))],
            out_specs=pl.BlockSpec((tr, w), lambda i, wh: (wh[0] * nt + i, 0)),
        ),
        out_shape=jax.ShapeDtypeStruct((r, w), F32),
        compiler_params=_cp(("parallel",), vmem=_VMEM_STREAM),
    )(where, g, land_f, land_lo)


def _swap_all(shards, *, name):
    n = len(shards)

    def body(*refs):
        ins, outs = refs[:n], refs[n : 2 * n]
        send_sems, recv_sems = refs[2 * n :]
        x, y, c = _place()
        copies = []
        for i, (e_ref, o_ref) in enumerate(zip(ins, outs)):
            rows = _half(c, e_ref.shape[0] // 2)
            copies.append(pltpu.make_async_remote_copy(src_ref=e_ref.at[rows], dst_ref=o_ref.at[rows], send_sem=send_sems.at[i],
                                                       recv_sem=recv_sems.at[i], device_id=(x, y, 1 - c), device_id_type=MESH))
        for cp in copies:
            cp.start()
        for cp in copies:
            cp.wait()

    return pl.pallas_call(
        body,
        name=name,
        in_specs=[ANY] * n,
        out_specs=[ANY] * n,
        out_shape=[jax.ShapeDtypeStruct(e.shape, e.dtype) for e in shards],
        input_output_aliases={i: i for i in range(n)},
        scratch_shapes=[pltpu.SemaphoreType.DMA((n,)), pltpu.SemaphoreType.DMA((n,))],
        compiler_params=pltpu.CompilerParams(has_side_effects=True),
    )(*shards)


def _sum_small(small, after):
    n_dev = 8

    def body(s_ref, after_ref, o_ref, all_ref, send_sems, recv_sems):
        x, y, c = _place()
        me = 4 * x + 2 * y + c
        all_ref[me] = s_ref[...]
        copies = []
        for k in range(1, n_dev):
            cx, cy = _rel_chip(x, y, k >> 1)
            cc = 1 - c if k & 1 else c
            copies.append(pltpu.make_async_remote_copy(
                src_ref=s_ref, dst_ref=all_ref.at[me], send_sem=send_sems.at[k - 1], recv_sem=recv_sems.at[k - 1],
                device_id=(cx, cy, cc), device_id_type=MESH))
        for cp in copies:
            cp.start()
        for cp in copies:
            cp.wait()
        acc = all_ref[0]
        for a in range(1, n_dev):
            acc = acc + all_ref[a]
        o_ref[...] = acc

    vm = pl.BlockSpec(memory_space=pltpu.VMEM)
    return pl.pallas_call(
        body,
        name="sum_small",
        in_specs=[vm, ANY],
        out_specs=vm,
        out_shape=jax.ShapeDtypeStruct(small.shape, F32),
        scratch_shapes=[pltpu.VMEM((n_dev,) + small.shape, F32), pltpu.SemaphoreType.DMA((n_dev - 1,)), pltpu.SemaphoreType.DMA((n_dev - 1,))],
        compiler_params=pltpu.CompilerParams(has_side_effects=True),
    )(small, after)


MATS = {"w_in": (776, True), "w_out": (256, False), "w_xq": (256, False), "w_xkv": (512, True), "w_xo": (256, False),
        "w_up": (1024, True), "w_down": (1024, False)}
GATHER_FIRST = ("w_in",)
GATHER_REST = ("w_up", "w_down", "w_out", "w_xq", "w_xkv", "w_xo")
IN_PLACE = ("w_up", "w_down")
GRAD_GROUPS = (("w_up", "w_down"), ("w_out", "w_xq", "w_xkv", "w_xo"), ("w_in",))


def _group_rows(names):
    n = sum(MATS[name][0] for name in names)
    return n + (-n) % 32


def _pack(pieces, rows):
    p = jnp.concatenate(pieces, axis=0) if len(pieces) > 1 else pieces[0]
    return jnp.pad(p, ((0, rows - p.shape[0]), (0, 0))) if rows > p.shape[0] else p


SMALL = (
    ("mix_norm", 1024), ("conv_norm", 512), ("b_af", 256), ("b_ab", 256), ("gla_norm", 128), ("xa_norm", 1024), ("mem_norm", 1024),
    ("mlp_norm", 1024), ("final_norm", 1024), ("conv_w", 1536), ("w_af", 4096), ("w_ab", 4096), ("loss", 128),
)


def kernel(x, mem, mix_norm, w_in, conv_w, conv_norm, w_af, b_af, w_ab, b_ab, gla_norm, w_out, xa_norm, mem_norm, w_xq, w_xkv, w_xo, mlp_norm, w_up, w_down, final_norm, loss_target, m_mix_norm, m_w_in, m_conv_w, m_conv_norm, m_w_af, m_b_af, m_w_ab, m_b_ab, m_gla_norm, m_w_out, m_xa_norm, m_mem_norm, m_w_xq, m_w_xkv, m_w_xo, m_mlp_norm, m_w_up, m_w_down, m_final_norm, v_mix_norm, v_w_in, v_conv_w, v_conv_norm, v_w_af, v_b_af, v_w_ab, v_b_ab, v_gla_norm, v_w_out, v_xa_norm, v_mem_norm, v_w_xq, v_w_xkv, v_w_xo, v_mlp_norm, v_w_up, v_w_down, v_final_norm):
    given = dict(locals())
    xi, yi, ci = _place()
    chip = 2 * xi + yi
    where = jnp.stack([ci, chip]).astype(jnp.int32)

    lo = {name: (given[name][0].T if MATS[name][1] else given[name][0]).astype(_CD) for name in MATS}
    pack_rest = _pack([lo[name] for name in GATHER_REST], _group_rows(GATHER_REST))
    pack_first = _pack([lo[name] for name in GATHER_FIRST], _group_rows(GATHER_FIRST))
    xs, mems, tgt = x[0], mem[0], loss_target[0]
    behind = lambda gain, token: gain + token[0, 0]

    def placed(shard, full_shape, col):
        return lax.dynamic_update_slice(jnp.zeros(full_shape, F32), shard, (0, col)).reshape(-1, 128)

    sw = jnp.concatenate([
        placed(conv_w[0], (CONV_K, CONV_WIDTH), 128 * chip),
        placed(w_af[0], (GLA_LOWRANK, GLA_K_TOTAL), 64 * chip),
        placed(w_ab[0], (GLA_LOWRANK, GLA_K_TOTAL), 64 * chip),
    ], axis=0)
    sw = jnp.pad(sw, ((0, SMALL_ROWS - sw.shape[0]), (0, 0))) * (ci == 0).astype(F32)
    sw = _sum_small(sw, mix_norm)

    first_send, first_recv, pack_first, land_first, first_token = _gather_start(pack_first, sw, name="gather_first_start")
    rest_send, rest_recv, pack_rest, land_rest, rest_token = _gather_start(pack_rest, first_token, name="gather_rest_start")
    h1 = _rms_fwd(xs, behind(mix_norm, rest_token), name="norm_mix")
    pack_first, land_first = _gather_wait(first_send, first_recv, pack_first, land_first, h1, name="gather_first_wait")
    got_first = _gather_spread(land_first, name="gather_first_spread")

    def whole(got, off, rows):
        return got[:, off : off + rows].reshape(N_CHIPS * rows, D_MODEL)

    w_in_t = whole(got_first, 0, MATS["w_in"][0])
    w_za = jnp.concatenate([w_in_t[0:1536], w_in_t[2560:3072]], axis=0)
    w_zb = jnp.concatenate([w_in_t[1536:2560], w_in_t[3072:W_IN_COLS], jnp.zeros((ZB_COLS - 1056, D_MODEL), _CD)], axis=0)
    conv_w_full = sw[0:12].reshape(CONV_K, CONV_WIDTH)
    w_af_full = sw[12:44].reshape(GLA_LOWRANK, GLA_K_TOTAL)
    w_ab_full = sw[44:76].reshape(GLA_LOWRANK, GLA_K_TOTAL)
    waf_p = jnp.pad(w_af_full, ((0, 128 - GLA_LOWRANK), (0, 0))).astype(_CD)
    wab_p = jnp.pad(w_ab_full, ((GLA_LOWRANK, 128 - 2 * GLA_LOWRANK), (0, 0))).astype(_CD)

    z_a, z_b = _mm_two(h1, w_za, w_zb, name="proj_in")
    b_f, b_b = _gate_fwd(z_b, waf_p, wab_p, b_af, b_ab, name="gates")
    o_f, st_f, o_b, st_b = _gla_fwd(z_b, b_f, b_b, name="gla_scan")
    y = _mix_fwd(z_a, o_f, o_b, conv_w_full, conv_norm, gla_norm, name="mix_out")
    pack_rest, land_rest = _gather_wait(rest_send, rest_recv, pack_rest, land_rest, y, name="gather_rest_wait")
    gathered = _gather_spread(land_rest, name="gather_rest_spread")
    wt, off = {}, 0
    for name in GATHER_REST:
        wt[name] = (gathered, off, MATS[name][0]) if name in IN_PLACE else whole(gathered, off, MATS[name][0])
        off += MATS[name][0]
    x1, hx = _mm_rows(y, wt["w_out"], mode="nn", name="proj_out", rows=(xs,), vecs=(xa_norm,), out_rows=(F32, _CD),
                      epilogue=_ep_residual_norm, tm=1024)
    hmem = _rms_fwd(mems, mem_norm, name="norm_mem")
    kv = _mm(hmem, wt["w_xkv"], mode="nt", name="proj_xkv", out_dtypes=(_CD,))
    qx, ox, hm, x2 = _xattn_block(hx, wt["w_xq"], kv, wt["w_xo"], x1, mlp_norm, name="xattn_block")
    act, relu_u = _mm(hm, wt["w_up"], mode="nt", name="mlp_up", out_dtypes=(_CD, _CD), tm=2048,
                      epilogue=lambda acc: (jnp.square(jnp.maximum(acc, 0.0)), jnp.maximum(acc, 0.0)))
    dx3, dx3_lo, loss_part, g_final_norm = _mm_rows(
        act, wt["w_down"], mode="nn", name="mlp_down", rows=(x2, tgt), vecs=(final_norm.reshape(1, D_MODEL),),
        out_rows=(F32, _CD), out_vecs=(128, D_MODEL), epilogue=_ep_loss)

    grads_t = {}

    def start_group(names, tag):
        rows = _group_rows(names)
        g = jnp.stack([_pack([grads_t[name][a * MATS[name][0] : (a + 1) * MATS[name][0]] for name in names], rows) for a in range(N_CHIPS)])
        return _scatter_start(g.astype(_TD), g, name="grads_" + tag + "_start")

    def finish_group(state, after, tag):
        send_sems, recv_sems, g_lo, g, land_lo, land_f, _ = state
        g_lo, g, land_lo, land_f = _scatter_wait(send_sems, recv_sems, g_lo, g, land_lo, land_f, after, name="grads_" + tag + "_wait")
        return _scatter_sum(g, land_lo, land_f, where, name="grads_" + tag + "_sum")

    def new_packs(names):
        shape = (N_CHIPS, _group_rows(names), D_MODEL)
        return lax.empty(shape, F32), lax.empty(shape, _TD)

    def grad_into(packs, names, which, a, b, name):
        off = sum(MATS[other][0] for other in names[: names.index(which)])
        return _mm_tn_into(a, b, packs, rows=MATS[which][0], off=off, name=name)

    du = _mm(dx3_lo, wt["w_down"], mode="nt", name="mlp_down_dx", out_dtypes=(_CD,), extras=(relu_u,), tm=2048,
             epilogue=lambda acc, rr: (acc * (2.0 * rr.astype(F32)),))
    packs = new_packs(GRAD_GROUPS[0])
    packs = grad_into(packs, GRAD_GROUPS[0], "w_down", act, dx3_lo, "mlp_down_dw")
    packs = grad_into(packs, GRAD_GROUPS[0], "w_up", du, hm, "mlp_up_dw")
    mlp_state = _scatter_start(packs[1], packs[0], name="grads_mlp_start")
    dx2, dx2_lo, g_mlp_norm = _mm_rows(
        du, wt["w_up"], mode="nn", name="mlp_up_dx", rows=(x2, dx3), vecs=(behind(mlp_norm, mlp_state[-1]),),
        out_rows=(F32, _CD), out_vecs=(D_MODEL,), epilogue=_ep_norm_bwd)
    packs = new_packs(GRAD_GROUPS[1])
    packs = grad_into(packs, GRAD_GROUPS[1], "w_xo", ox, dx2_lo, "proj_xo_dw")
    dqx, dkv = _xattn_bwd(qx, kv, dx2_lo, wt["w_xo"], name="xattn_bwd")
    packs = grad_into(packs, GRAD_GROUPS[1], "w_xq", hx, dqx, "proj_xq_dw")
    dx1, dx1_lo, g_xa_norm = _mm_rows(
        dqx, wt["w_xq"], mode="nt", name="proj_xq_dx", rows=(x1, dx2), vecs=(xa_norm,),
        out_rows=(F32, _CD), out_vecs=(D_MODEL,), epilogue=_ep_norm_bwd, tm=1024)
    dkv_lo = dkv.astype(_CD)
    packs = grad_into(packs, GRAD_GROUPS[1], "w_xkv", dkv_lo, hmem, "proj_xkv_dw")
    dhmem = _mm(dkv_lo, wt["w_xkv"], mode="nn", name="proj_xkv_dx")
    g_mem_norm = _rms_gain_grad(mems, dhmem, name="norm_mem_bwd")
    dy = _mm(dx1_lo, wt["w_out"], mode="nt", name="proj_out_dx")
    packs = grad_into(packs, GRAD_GROUPS[1], "w_out", y, dx1_lo, "proj_out_dw")
    attn_state = _scatter_start(packs[1], packs[0], name="grads_attn_start")
    dz_a, do, g_conv_w, g_conv_norm, g_gla_norm = _mix_bwd(z_a, o_f, o_b, dy, conv_w_full, behind(conv_norm, attn_state[-1]), gla_norm, name="mix_out_bwd")
    dqkv_f, db_f, dqkv_b, db_b = _gla_bwd(z_b, b_f, b_b, do, st_f, st_b, name="gla_scan_bwd")
    dz_b, g_waf_p, g_wab_p, g_b_af, g_b_ab = _gate_bwd(z_b, waf_p, wab_p, b_af, b_ab, db_f, db_b, dqkv_f, dqkv_b, name="gates_bwd")
    g_za = _mm_tn(dz_a, h1, name="proj_in_a_dw")
    g_zb = _mm_tn(dz_b, h1, name="proj_in_b_dw")
    grads_t["w_in"] = jnp.concatenate([g_za[0:1536], g_zb[0:1024], g_za[1536:2048], g_zb[1024:1056]], axis=0)
    in_state = start_group(GRAD_GROUPS[2], "in")
    grad_x, g_mix_norm = _mm_rows(
        dz_a, w_za, mode="nn", name="proj_in_dx", more=((dz_b, w_zb),), rows=(xs, dx1), vecs=(behind(mix_norm, in_state[-1]),),
        out_rows=(F32,), out_vecs=(D_MODEL,), epilogue=_ep_norm_bwd)

    half_mlp = finish_group(mlp_state, grad_x, "mlp")
    half_attn = finish_group(attn_state, half_mlp, "attn")
    half_in = finish_group(in_state, half_attn, "in")
    shard_rows = {}
    for names, rows in zip(GRAD_GROUPS, _swap_all([half_mlp, half_attn, half_in], name="shards_to_sibling")):
        off = 0
        for name in names:
            shard_rows[name] = (rows, off)
            off += MATS[name][0]

    small_vals = dict(mix_norm=g_mix_norm, conv_norm=g_conv_norm, b_af=g_b_af, b_ab=g_b_ab, gla_norm=g_gla_norm, xa_norm=g_xa_norm,
                      mem_norm=g_mem_norm, mlp_norm=g_mlp_norm, final_norm=g_final_norm, conv_w=g_conv_w,
                      w_af=g_waf_p[0:GLA_LOWRANK], w_ab=g_wab_p[GLA_LOWRANK : 2 * GLA_LOWRANK], loss=loss_part)
    small = jnp.concatenate([small_vals[name].reshape(-1, 128) for name, _ in SMALL], axis=0)
    small = _sum_small(jnp.pad(small, ((0, SMALL_ROWS - small.shape[0]), (0, 0))), loss_part)
    g_small, off = {}, 0
    for name, n in SMALL:
        g_small[name] = small[off : off + n // 128]
        off += n // 128
    loss = g_small["loss"][0, 0]
    g_small["conv_w"] = lax.dynamic_slice(g_small["conv_w"].reshape(CONV_K, CONV_WIDTH), (0, 128 * chip), (CONV_K, 128))
    g_small["w_af"] = lax.dynamic_slice(g_small["w_af"].reshape(GLA_LOWRANK, GLA_K_TOTAL), (0, 64 * chip), (GLA_LOWRANK, 64))
    g_small["w_ab"] = lax.dynamic_slice(g_small["w_ab"].reshape(GLA_LOWRANK, GLA_K_TOTAL), (0, 64 * chip), (GLA_LOWRANK, 64))

    names = ["mix_norm", "w_in", "conv_w", "conv_norm", "w_af", "b_af", "w_ab", "b_ab", "gla_norm", "w_out", "xa_norm", "mem_norm",
             "w_xq", "w_xkv", "w_xo", "mlp_norm", "w_up", "w_down", "final_norm"]
    big_names = list(MATS)
    as2d = lambda a: a.reshape(1, -1) if a.ndim == 1 else a.reshape(a.shape[-2:])
    grads, deltas, new_m, new_v = {}, {}, {}, {}
    for name in big_names:
        rows, off = shard_rows[name]
        wmv = [as2d(given[name]), as2d(given["m_" + name]), as2d(given["v_" + name])]
        as_stored = name == "w_in"
        if as_stored:
            wmv = [a.T for a in wmv]
        res = _adamw(*wmv, rows, off, transposed=MATS[name][1] and not as_stored, name="adamw_" + name)
        grads[name], deltas[name], new_m[name], new_v[name] = [a.T for a in res] if as_stored else res
    small_names = [name for name in names if name not in big_names]
    groups = []
    for name in small_names:
        grads[name] = g_small[name].reshape(as2d(given[name]).shape)
        groups.append((as2d(given[name]), grads[name], as2d(given["m_" + name]), as2d(given["v_" + name])))
    for name, res in zip(small_names, _adamw_small(groups, name="adamw_small")):
        deltas[name], new_m[name], new_v[name] = res

    like = lambda name, a: a.reshape(given[name].shape)
    return (loss, grad_x[None], *[like(n, grads[n]) for n in names], *[like(n, deltas[n]) for n in names],
            *[like(n, new_m[n]) for n in names], *[like(n, new_v[n]) for n in names])
```

```python
import jax
import jax.numpy as jnp
from jax import lax
from jax.experimental import pallas as pl
from jax.experimental.pallas import tpu as pltpu

F32 = jnp.float32
BF16 = jnp.bfloat16
_CD = jnp.bfloat16
_TD = jnp.bfloat16

D_MODEL = 1024
N_MEM = 256
CONV_WIDTH = 512
CONV_GROUP = 64
CONV_K = 3
GLA_HEADS = 4
GLA_DK = 64
GLA_DV = 128
GLA_K_TOTAL = 256
GLA_V_TOTAL = 512
GLA_LOWRANK = 16
GLA_GATE_SCALE = 1.0 / 16.0
GLA_CHUNK = 64
XA_HEADS = 4
XA_HEAD_DIM = 256
D_FF = 4096
EPS = 1e-6
W_IN_COLS = 3104
ZA_COLS = 2048
ZB_COLS = 1152
LR_COL = 1024

ADAM_LR = 0.001
ADAM_B1 = 0.9
ADAM_B2 = 0.999
ADAM_EPS = 1e-08
ADAM_WD = 0.01
ADAM_STEP = 10

N_CHIPS = 4
SMALL_ROWS = 128

_TS = 512
_VMEM = 44 * 1024 * 1024
_VMEM_STREAM = 63 * 1024 * 1024
MESH = pl.DeviceIdType.MESH
ANY = pl.BlockSpec(memory_space=pl.ANY)


def _cp(sem=None, vmem=_VMEM, **kw):
    return pltpu.CompilerParams(dimension_semantics=sem, vmem_limit_bytes=vmem, **kw)


def _dot(a, b):
    return jnp.dot(a.astype(_CD), b.astype(_CD), preferred_element_type=F32)


def _dot_nt(a, b):
    return lax.dot_general(a.astype(_CD), b.astype(_CD), (((1,), (1,)), ((), ())), preferred_element_type=F32)


def _dot_tn(a, b):
    return lax.dot_general(a.astype(_CD), b.astype(_CD), (((0,), (0,)), ((), ())), preferred_element_type=F32)


def _dot_split(x, ones):
    hi = x.astype(BF16)
    r = x - hi.astype(F32)
    mid = r.astype(BF16)
    lo = (r - mid.astype(F32)).astype(BF16)
    d = lambda p: jnp.dot(p, ones, preferred_element_type=F32)
    return d(hi) + d(mid) + d(lo)


def _pick(n, cands=(1024, 640, 512, 256, 128)):
    for t in cands:
        if n % t == 0:
            return t
    return n


def _rows(s, light=False, times=2):
    return min(times * _TS if light else _TS, s)


def _sigmoid(v):
    e = jnp.exp(-jnp.abs(v))
    return jnp.where(v >= 0, 1.0 / (1.0 + e), e / (1.0 + e))


def _mm(a, b, *, mode, name, out_dtypes=(F32,), extras=(), epilogue=None, tm=None, tn=None, tk=None):
    m, k = a.shape
    placed = isinstance(b, tuple)
    if placed:
        b, b_off, tn = b
        assert mode == "nt" and b_off % tn == 0
        n = N_CHIPS * tn
    else:
        n = b.shape[1] if mode == "nn" else b.shape[0]
    tm = min(m, tm or 1024)
    tn = tn or _pick(n)
    tk = tk or _pick(k)
    nk = k // tk
    n_ex, n_out = len(extras), len(out_dtypes)

    def body(*refs):
        a_ref, b_ref = refs[:2]
        ex = refs[2 : 2 + n_ex]
        outs = refs[2 + n_ex : 2 + n_ex + n_out]
        part = _dot(a_ref[...], b_ref[...]) if mode == "nn" else _dot_nt(a_ref[...], b_ref[...])

        def finish(acc):
            res = epilogue(acc, *[e[...] for e in ex]) if epilogue else (acc,)
            for o, r in zip(outs, res):
                o[...] = r.astype(o.dtype)

        if nk == 1:
            finish(part)
        else:
            acc_ref = refs[-1]
            kk = pl.program_id(2)

            @pl.when(kk == 0)
            def _():
                acc_ref[...] = part

            @pl.when(kk > 0)
            def _():
                acc_ref[...] += part

            @pl.when(kk == nk - 1)
            def _():
                finish(acc_ref[...])

    if placed:
        b_spec = pl.BlockSpec((None, tn, tk), lambda i, j, kk: (j, b_off // tn, kk))
    else:
        b_spec = pl.BlockSpec((tk, tn), lambda i, j, kk: (kk, j)) if mode == "nn" else pl.BlockSpec((tn, tk), lambda i, j, kk: (j, kk))
    tile = pl.BlockSpec((tm, tn), lambda i, j, kk: (i, j))
    out = pl.pallas_call(
        body,
        name=name,
        grid=(m // tm, n // tn, nk),
        in_specs=[pl.BlockSpec((tm, tk), lambda i, j, kk: (i, kk)), b_spec] + [tile] * n_ex,
        out_specs=[tile] * n_out,
        out_shape=[jax.ShapeDtypeStruct((m, n), dt) for dt in out_dtypes],
        scratch_shapes=[pltpu.VMEM((tm, tn), F32)] if nk > 1 else [],
        compiler_params=_cp(("parallel", "parallel", "arbitrary")),
    )(a, b, *extras)
    return out[0] if n_out == 1 else out


def _mm_two(a, b1, b2, *, name, tm=512):
    m, k = a.shape
    tm = min(m, tm)

    def body(a_ref, b1_ref, b2_ref, o1_ref, o2_ref):
        av = a_ref[...]
        o1_ref[...] = _dot_nt(av, b1_ref[...])
        o2_ref[...] = _dot_nt(av, b2_ref[...])

    whole = lambda arr: pl.BlockSpec(arr.shape, lambda i: (0, 0))
    rows = lambda n: pl.BlockSpec((tm, n), lambda i: (i, 0))
    return pl.pallas_call(
        body,
        name=name,
        grid=(m // tm,),
        in_specs=[rows(k), whole(b1), whole(b2)],
        out_specs=[rows(b1.shape[0]), rows(b2.shape[0])],
        out_shape=[jax.ShapeDtypeStruct((m, b1.shape[0]), F32), jax.ShapeDtypeStruct((m, b2.shape[0]), F32)],
        compiler_params=_cp(("parallel",)),
    )(a, b1, b2)


def _mm_tn(a, b, *, name):
    s, m = a.shape
    n = b.shape[1]
    cap = max(128, (1 << 20) // n)
    tm = _pick(m, tuple(t for t in (512, 640, 384, 256, 128) if t <= max(cap, 128)))
    ts = min(s, 1 << (((1 << 22) // n).bit_length() - 1))
    ns = s // ts

    def body(a_ref, b_ref, o_ref):
        part = _dot_tn(a_ref[...], b_ref[...])
        if ns == 1:
            o_ref[...] = part
        else:
            ss = pl.program_id(1)

            @pl.when(ss == 0)
            def _():
                o_ref[...] = part

            @pl.when(ss > 0)
            def _():
                o_ref[...] += part

    return pl.pallas_call(
        body,
        name=name,
        grid=(m // tm, ns),
        in_specs=[pl.BlockSpec((ts, tm), lambda i, ss: (ss, i)), pl.BlockSpec((ts, n), lambda i, ss: (ss, 0))],
        out_specs=pl.BlockSpec((tm, n), lambda i, ss: (i, 0)),
        out_shape=jax.ShapeDtypeStruct((m, n), F32),
        compiler_params=_cp(("parallel", "arbitrary")),
    )(a, b)


RING = 3


def _mm_tn_into(a, b, packs, *, rows, off, name):
    s, m = a.shape
    n = b.shape[1]
    tm = 1024 if rows % 1024 == 0 and s >= 4096 else 512
    tr = min(tm, rows)
    per, chips = rows // tr, tm // tr
    ts = min(s, 1 << (((1 << 22) // (tm + n)).bit_length() - 1))
    ns = s // ts
    total = (m // tm) * ns

    def tiles(a_hbm, b_hbm, a_buf, b_buf, sems, t):
        t = jnp.int32(t)
        i, ss = t // ns, t % ns
        slot = t % RING
        return (
            pltpu.make_async_copy(a_hbm.at[pl.ds(ss * ts, ts), pl.ds(pl.multiple_of(i * tm, tm), tm)], a_buf.at[slot], sems.at[0, slot]),
            pltpu.make_async_copy(b_hbm.at[pl.ds(ss * ts, ts), :], b_buf.at[slot], sems.at[1, slot]),
        )

    def body(a_hbm, b_hbm, f_in, lo_in, f_ref, lo_ref, a_buf, b_buf, sems):
        t = pl.program_id(0) * ns + pl.program_id(1)

        @pl.when(t == 0)
        def _():
            for first in range(min(RING - 1, total)):
                for lane, cp in enumerate(tiles(a_hbm, b_hbm, a_buf, b_buf, sems, first)):
                    cp.start(priority=lane)

        @pl.when(t + RING - 1 < total)
        def _():
            for lane, cp in enumerate(tiles(a_hbm, b_hbm, a_buf, b_buf, sems, t + RING - 1)):
                cp.start(priority=lane)

        for cp in tiles(a_hbm, b_hbm, a_buf, b_buf, sems, t):
            cp.wait()
        slot = t % RING
        part = _dot_tn(a_buf[slot], b_buf[slot])
        pieces = [part[c * tr : (c + 1) * tr] for c in range(chips)]
        if ns == 1:
            for c, p in enumerate(pieces):
                f_ref[c] = p
                lo_ref[c] = p.astype(lo_ref.dtype)
        else:
            ss = pl.program_id(1)

            @pl.when(ss == 0)
            def _():
                for c, p in enumerate(pieces):
                    f_ref[c] = p

            @pl.when(ss > 0)
            def _():
                for c, p in enumerate(pieces):
                    f_ref[c] += p

            @pl.when(ss == ns - 1)
            def _():
                lo_ref[...] = f_ref[...].astype(lo_ref.dtype)

    spec = pl.BlockSpec((chips, tr, n), lambda i, ss: (i // per, off // tr + i % per, 0))
    return pl.pallas_call(
        body,
        name=name,
        grid=(m // tm, ns),
        in_specs=[ANY, ANY, ANY, ANY],
        out_specs=[spec, spec],
        out_shape=[jax.ShapeDtypeStruct(p.shape, p.dtype) for p in packs],
        input_output_aliases={2: 0, 3: 1},
        scratch_shapes=[pltpu.VMEM((RING, ts, tm), a.dtype), pltpu.VMEM((RING, ts, n), b.dtype), pltpu.SemaphoreType.DMA((2, RING))],
        compiler_params=_cp(("arbitrary", "arbitrary")),
    )(a, b, *packs)


def _mm_rows(a, b, *, mode, name, more=(), rows=(), vecs=(), out_rows=(), out_vecs=(), epilogue, tm=512):
    m, k = a.shape
    placed = isinstance(b, tuple)
    if placed:
        b, b_off, b_rows = b
        assert mode == "nn" and b_off % b_rows == 0 and k == N_CHIPS * b_rows
        n = b.shape[2]
        b_spec = pl.BlockSpec((N_CHIPS, b_rows, n), lambda i: (0, b_off // b_rows, 0))
    else:
        n = b.shape[1] if mode == "nn" else b.shape[0]
        b_spec = pl.BlockSpec(b.shape, lambda i: (0, 0))
    tm = min(m, tm)
    parts = 2 if tm % 256 == 0 else 1
    n_m, n_r, n_v, n_or, n_ov = 2 * len(more), len(rows), len(vecs), len(out_rows), len(out_vecs)

    def body(*refs):
        a_ref, b_ref = refs[:2]
        m_refs = refs[2 : 2 + n_m]
        rest = refs[2 + n_m :]
        r_refs = rest[:n_r]
        v_refs = rest[n_r : n_r + n_v]
        or_refs = rest[n_r + n_v : n_r + n_v + n_or]
        ov_refs = rest[n_r + n_v + n_or :]
        res_vecs = None
        bv = b_ref[...].reshape(k, n) if placed else b_ref[...]
        for p in range(parts):
            rs = slice(p * tm // parts, (p + 1) * tm // parts)
            acc = _dot(a_ref[rs, :], bv) if mode == "nn" else _dot_nt(a_ref[rs, :], bv)
            for a2_ref, b2_ref in zip(m_refs[0::2], m_refs[1::2]):
                acc = acc + _dot(a2_ref[rs, :], b2_ref[...])
            res_rows, part_vecs = epilogue(acc, [r[rs, :] for r in r_refs], [v[...] for v in v_refs])
            for o, r in zip(or_refs, res_rows):
                o[rs, :] = r.astype(o.dtype)
            res_vecs = part_vecs if res_vecs is None else [s + t for s, t in zip(res_vecs, part_vecs)]
        if n_ov:
            first = pl.program_id(0) == 0

            @pl.when(first)
            def _():
                for o, r in zip(ov_refs, res_vecs):
                    o[...] = r

            @pl.when(jnp.logical_not(first))
            def _():
                for o, r in zip(ov_refs, res_vecs):
                    o[...] += r

    tile = pl.BlockSpec((tm, n), lambda i: (i, 0))
    whole = lambda arr: pl.BlockSpec(arr.shape, lambda i: (0, 0))
    vec = lambda w: pl.BlockSpec((1, w), lambda i: (0, 0))
    out = pl.pallas_call(
        body,
        name=name,
        grid=(m // tm,),
        in_specs=[pl.BlockSpec((tm, k), lambda i: (i, 0)), b_spec]
        + [spec for a2, b2 in more for spec in (pl.BlockSpec((tm, a2.shape[1]), lambda i: (i, 0)), whole(b2))]
        + [tile] * n_r + [vec(v.shape[1]) for v in vecs],
        out_specs=[tile] * n_or + [vec(w) for w in out_vecs],
        out_shape=[jax.ShapeDtypeStruct((m, n), dt) for dt in out_rows] + [jax.ShapeDtypeStruct((1, w), F32) for w in out_vecs],
        compiler_params=_cp(("arbitrary",) if n_ov else ("parallel",)),
    )(a, b, *[x for pair in more for x in pair], *rows, *vecs)
    return out


def _ep_residual_norm(acc, rows, vecs):
    x = acc + rows[0]
    r = lax.rsqrt(jnp.mean(x * x, axis=-1, keepdims=True) + EPS)
    return [x, x * r * vecs[0]], []


def _ep_norm_bwd(acc, rows, vecs):
    dy = acc
    for extra in rows[2:]:
        dy = dy + extra
    x, dres = rows[0], rows[1]
    r = lax.rsqrt(jnp.mean(x * x, axis=-1, keepdims=True) + EPS)
    xh = x * r
    dxh = dy * vecs[0]
    dx = r * (dxh - xh * jnp.mean(dxh * xh, axis=-1, keepdims=True)) + dres
    return [dx, dx], [jnp.sum(dy * xh, axis=0, keepdims=True)]


def _ep_loss(acc, rows, vecs):
    x = acc + rows[0]
    d = x.shape[-1]
    r = lax.rsqrt(jnp.mean(x * x, axis=-1, keepdims=True) + EPS)
    xh = x * r
    err = xh * vecs[0] - rows[1]
    loss = jnp.zeros((1, 128), F32) + 0.5 * jnp.sum(jnp.mean(err * err, axis=-1, keepdims=True))
    dy = err * (1.0 / d)
    dxh = dy * vecs[0]
    dx = r * (dxh - xh * jnp.mean(dxh * xh, axis=-1, keepdims=True))
    return [dx, dx], [loss, jnp.sum(dy * xh, axis=0, keepdims=True)]


def _rms_fwd(x, g, *, name):
    s, d = x.shape
    ts = _rows(s, light=True, times=4)

    def body(x_ref, g_ref, o_ref):
        xf = x_ref[...]
        r = lax.rsqrt(jnp.mean(xf * xf, axis=-1, keepdims=True) + EPS)
        o_ref[...] = (xf * r * g_ref[...]).astype(o_ref.dtype)

    return pl.pallas_call(
        body,
        name=name,
        grid=(s // ts,),
        in_specs=[pl.BlockSpec((ts, d), lambda i: (i, 0)), pl.BlockSpec((1, d), lambda i: (0, 0))],
        out_specs=pl.BlockSpec((ts, d), lambda i: (i, 0)),
        out_shape=jax.ShapeDtypeStruct((s, d), _CD),
        compiler_params=_cp(("parallel",)),
    )(x, g)


def _rms_gain_grad(x, dy, *, name):
    s, d = x.shape
    ts = _rows(s)

    def body(x_ref, dy_ref, dg_ref):
        xf = x_ref[...]
        r = lax.rsqrt(jnp.mean(xf * xf, axis=-1, keepdims=True) + EPS)
        part = jnp.sum(dy_ref[...] * (xf * r), axis=0, keepdims=True)

        @pl.when(pl.program_id(0) == 0)
        def _():
            dg_ref[...] = part

        @pl.when(pl.program_id(0) > 0)
        def _():
            dg_ref[...] += part

    tile = pl.BlockSpec((ts, d), lambda i: (i, 0))
    return pl.pallas_call(
        body,
        name=name,
        grid=(s // ts,),
        in_specs=[tile, tile],
        out_specs=pl.BlockSpec((1, d), lambda i: (0, 0)),
        out_shape=jax.ShapeDtypeStruct((1, d), F32),
        compiler_params=_cp(("arbitrary",)),
    )(x, dy)


def _chunk_scan(v, row_in_chunk, suffix):
    t = v.shape[0]
    step = 1
    while step < GLA_CHUNK:
        if suffix:
            v = v + jnp.where(row_in_chunk < GLA_CHUNK - step, pltpu.roll(v, t - step, 0), 0.0)
        else:
            v = v + jnp.where(row_in_chunk >= step, pltpu.roll(v, step, 0), 0.0)
        step *= 2
    return v


def _gate_pre(lr, w_ref, b_ref):
    return _dot(lr, w_ref[...]) + b_ref[...]


def _gate_fwd(z, waf, wab, baf, bab, *, name):
    s = z.shape[0]
    ts = _rows(s, light=True, times=4)

    def body(lr_ref, waf_ref, wab_ref, baf_ref, bab_ref, bf_ref, bb_ref):
        lr = lr_ref[...]
        ric = lax.broadcasted_iota(jnp.int32, (ts, GLA_K_TOTAL), 0) & (GLA_CHUNK - 1)
        for w_ref, b_ref, o_ref, suffix in ((waf_ref, baf_ref, bf_ref, False), (wab_ref, bab_ref, bb_ref, True)):
            pre = _gate_pre(lr, w_ref, b_ref)
            la = (jnp.minimum(pre, 0.0) - jnp.log(1.0 + jnp.exp(-jnp.abs(pre)))) * GLA_GATE_SCALE
            o_ref[...] = _chunk_scan(la, ric, suffix)

    wspec = pl.BlockSpec((128, GLA_K_TOTAL), lambda i: (0, 0))
    bspec = pl.BlockSpec((1, GLA_K_TOTAL), lambda i: (0, 0))
    tile = pl.BlockSpec((ts, GLA_K_TOTAL), lambda i: (i, 0))
    return pl.pallas_call(
        body,
        name=name,
        grid=(s // ts,),
        in_specs=[pl.BlockSpec((ts, 128), lambda i: (i, LR_COL // 128)), wspec, wspec, bspec, bspec],
        out_specs=[tile, tile],
        out_shape=[jax.ShapeDtypeStruct((s, GLA_K_TOTAL), F32)] * 2,
        compiler_params=_cp(("parallel",)),
    )(z, waf, wab, baf, bab)


def _gate_bwd(z, waf, wab, baf, bab, dbf, dbb, dqkv_f, dqkv_b, *, name):
    s = z.shape[0]
    ts = _rows(s, light=True)

    def body(lr_ref, waf_ref, wab_ref, baf_ref, bab_ref, dbf_ref, dbb_ref, gf_ref, gb_ref, dzb_ref, dwf_ref, dwb_ref, dbaf_ref, dbab_ref):
        lr = lr_ref[...]
        ric = lax.broadcasted_iota(jnp.int32, (ts, GLA_K_TOTAL), 0) & (GLA_CHUNK - 1)
        first = pl.program_id(0) == 0
        dlr = None
        for w_ref, b_ref, db_ref, dw_ref, dbias_ref, suffix in (
            (waf_ref, baf_ref, dbf_ref, dwf_ref, dbaf_ref, True),
            (wab_ref, bab_ref, dbb_ref, dwb_ref, dbab_ref, False),
        ):
            pre = _gate_pre(lr, w_ref, b_ref)
            dla = _chunk_scan(db_ref[...], ric, suffix)
            dpre = dla * GLA_GATE_SCALE * _sigmoid(-pre)
            part = _dot_nt(dpre, w_ref[...])
            dlr = part if dlr is None else dlr + part
            dw = _dot_tn(lr, dpre)
            dbias = jnp.sum(dpre, axis=0, keepdims=True)

            @pl.when(first)
            def _():
                dw_ref[...] = dw
                dbias_ref[...] = dbias

            @pl.when(jnp.logical_not(first))
            def _():
                dw_ref[...] += dw
                dbias_ref[...] += dbias

        dqkv = gf_ref[...].astype(F32) + gb_ref[...].astype(F32)
        dzb_ref[...] = jnp.concatenate([dqkv, dlr], axis=1).astype(dzb_ref.dtype)

    wspec = pl.BlockSpec((128, GLA_K_TOTAL), lambda i: (0, 0))
    bspec = pl.BlockSpec((1, GLA_K_TOTAL), lambda i: (0, 0))
    tile = pl.BlockSpec((ts, GLA_K_TOTAL), lambda i: (i, 0))
    wide = pl.BlockSpec((ts, 2 * GLA_K_TOTAL + GLA_V_TOTAL), lambda i: (i, 0))
    return pl.pallas_call(
        body,
        name=name,
        grid=(s // ts,),
        in_specs=[pl.BlockSpec((ts, 128), lambda i: (i, LR_COL // 128)), wspec, wspec, bspec, bspec, tile, tile, wide, wide],
        out_specs=[pl.BlockSpec((ts, ZB_COLS), lambda i: (i, 0)), wspec, wspec, bspec, bspec],
        out_shape=[
            jax.ShapeDtypeStruct((s, ZB_COLS), _CD),
            jax.ShapeDtypeStruct((128, GLA_K_TOTAL), F32),
            jax.ShapeDtypeStruct((128, GLA_K_TOTAL), F32),
            jax.ShapeDtypeStruct((1, GLA_K_TOTAL), F32),
            jax.ShapeDtypeStruct((1, GLA_K_TOTAL), F32),
        ],
        compiler_params=_cp(("arbitrary",)),
    )(z, waf, wab, baf, bab, dbf, dbb, dqkv_f, dqkv_b)


def _gla_masks(rev):
    lane_head = lax.broadcasted_iota(jnp.int32, (1, GLA_K_TOTAL), 1) >> 6
    head_masks = [lane_head == h for h in range(GLA_HEADS)]
    t = lax.broadcasted_iota(jnp.int32, (GLA_HEADS * GLA_CHUNK, GLA_CHUNK), 0) & (GLA_CHUNK - 1)
    u = lax.broadcasted_iota(jnp.int32, (GLA_HEADS * GLA_CHUNK, GLA_CHUNK), 1)
    tri = (u > t) if rev else (u <= t)
    row = lax.broadcasted_iota(jnp.int32, (GLA_CHUNK, GLA_K_TOTAL), 0)
    total_row = row == (0 if rev else GLA_CHUNK - 1)
    return head_masks, tri, total_row


def _spread(a, head_masks):
    return jnp.concatenate([jnp.where(m, a, 0.0) for m in head_masks], axis=0)


def _stack(a):
    return jnp.concatenate([a[:, GLA_DV * h : GLA_DV * (h + 1)] for h in range(GLA_HEADS)], axis=0)


def _unstack(a):
    return jnp.concatenate([a[GLA_CHUNK * h : GLA_CHUNK * (h + 1)] for h in range(GLA_HEADS)], axis=1)


def _collect(a, head_masks):
    out = None
    for h, m in enumerate(head_masks):
        part = jnp.where(m, a[GLA_CHUNK * h : GLA_CHUNK * (h + 1)], 0.0)
        out = part if out is None else out + part
    return out


def _gla_chunk_terms(q_ref, k_ref, v_ref, b_ref, rows, head_masks, tri, total_row):
    q = q_ref[rows, :] * (GLA_DK**-0.5)
    k = k_ref[rows, :]
    v = v_ref[rows, :]
    b = b_ref[rows, :]
    eb = jnp.exp(b)
    enb = jnp.exp(-b)
    g = jnp.sum(jnp.where(total_row, b, 0.0), axis=0, keepdims=True)
    egb = jnp.exp(g - b)
    qt = q * eb
    kt = k * enb
    kh = k * egb
    q_heads = _spread(qt, head_masks)
    attn = jnp.where(tri, _dot_nt(q_heads, kt), 0.0)
    return v, eb, enb, egb, jnp.exp(g), qt, kt, kh, q_heads, attn


def _gla_specs(s, tb, rev_blocks):
    nb = s // tb
    rb = (lambda i: nb - 1 - i) if rev_blocks else (lambda i: i)
    q_spec = pl.BlockSpec((tb, GLA_K_TOTAL), lambda i: (rb(i), 0))
    k_spec = pl.BlockSpec((tb, GLA_K_TOTAL), lambda i: (rb(i), 1))
    v_spec = pl.BlockSpec((tb, GLA_V_TOTAL), lambda i: (rb(i), 1))
    b_spec = pl.BlockSpec((tb, GLA_K_TOTAL), lambda i: (rb(i), 0))
    o_spec = pl.BlockSpec((tb, GLA_V_TOTAL), lambda i: (rb(i), 0))
    st_spec = pl.BlockSpec((tb // GLA_CHUNK, GLA_DV, GLA_K_TOTAL), lambda i: (rb(i), 0, 0))
    return nb, q_spec, k_spec, v_spec, b_spec, o_spec, st_spec


def _gla_fwd_chunk(cidx, q_ref, k_ref, v_ref, b_ref, o_ref, sv_ref, st_ref, masks):
    head_masks, tri, total_row = masks
    rows = pl.ds(pl.multiple_of(cidx * GLA_CHUNK, GLA_CHUNK), GLA_CHUNK)
    v, _, _, _, eg, _, _, kh, q_heads, attn = _gla_chunk_terms(q_ref, k_ref, v_ref, b_ref, rows, head_masks, tri, total_row)
    o = jnp.concatenate(
        [_dot(attn[GLA_CHUNK * h : GLA_CHUNK * (h + 1)], v[:, GLA_DV * h : GLA_DV * (h + 1)]) for h in range(GLA_HEADS)], axis=1
    )
    st = st_ref[...]
    o_ref[rows, :] = o + _unstack(_dot_nt(q_heads, st))
    sv_ref[cidx] = st
    st_ref[...] = st * eg + _dot_tn(_stack(v), _spread(kh, head_masks))


def _gla_fwd(z, b_f, b_b, *, name):
    s = z.shape[0]
    tb = _rows(s)
    cpb = tb // GLA_CHUNK
    nb, qf, kf, vf, bf, of, sf = _gla_specs(s, tb, False)
    _, qr, kr, vr, br, orr, sr = _gla_specs(s, tb, True)

    def body(qf_ref, kf_ref, vf_ref, bf_ref, qr_ref, kr_ref, vr_ref, br_ref, of_ref, svf_ref, or_ref, svr_ref, stf_ref, str_ref):
        masks_f, masks_r = _gla_masks(False), _gla_masks(True)

        @pl.when(pl.program_id(0) == 0)
        def _():
            stf_ref[...] = jnp.zeros_like(stf_ref)
            str_ref[...] = jnp.zeros_like(str_ref)

        def chunk(ci, carry):
            _gla_fwd_chunk(ci, qf_ref, kf_ref, vf_ref, bf_ref, of_ref, svf_ref, stf_ref, masks_f)
            _gla_fwd_chunk(cpb - 1 - ci, qr_ref, kr_ref, vr_ref, br_ref, or_ref, svr_ref, str_ref, masks_r)
            return carry

        lax.fori_loop(0, cpb, chunk, 0)

    o_shape = jax.ShapeDtypeStruct((s, GLA_V_TOTAL), F32)
    st_shape = jax.ShapeDtypeStruct((s // GLA_CHUNK, GLA_DV, GLA_K_TOTAL), F32)
    return pl.pallas_call(
        body,
        name=name,
        grid=(nb,),
        in_specs=[qf, kf, vf, bf, qr, kr, vr, br],
        out_specs=[of, sf, orr, sr],
        out_shape=[o_shape, st_shape, o_shape, st_shape],
        scratch_shapes=[pltpu.VMEM((GLA_DV, GLA_K_TOTAL), F32)] * 2,
        compiler_params=_cp(("arbitrary",)),
    )(z, z, z, b_f, z, z, z, b_b)


def _gla_bwd_chunk(cidx, q_ref, k_ref, v_ref, b_ref, do_ref, sv_ref, dqkv_ref, db_ref, dst_ref, masks):
    head_masks, tri, total_row = masks
    rows = pl.ds(pl.multiple_of(cidx * GLA_CHUNK, GLA_CHUNK), GLA_CHUNK)
    v, eb, enb, egb, eg, qt, kt, kh, q_heads, attn = _gla_chunk_terms(q_ref, k_ref, v_ref, b_ref, rows, head_masks, tri, total_row)
    do_c = do_ref[rows, :]
    st = sv_ref[cidx]
    dst = dst_ref[...]
    do_s, v_s = _stack(do_c), _stack(v)
    hs = lambda a, h: a[GLA_CHUNK * h : GLA_CHUNK * (h + 1)]
    vs = lambda a, h: a[:, GLA_DV * h : GLA_DV * (h + 1)]
    dattn = jnp.concatenate([_dot_nt(vs(do_c, h), vs(v, h)) for h in range(GLA_HEADS)], axis=0)
    dattn = jnp.where(tri, dattn, 0.0)
    dv = jnp.concatenate([_dot_tn(hs(attn, h), vs(do_c, h)) for h in range(GLA_HEADS)], axis=1)
    dv = dv + _unstack(_dot_nt(_spread(kh, head_masks), dst))
    dqt = _collect(_dot(do_s, st) + _dot(dattn, kt), head_masks)
    dkt = _dot_tn(dattn, q_heads)
    dkh = _collect(_dot(v_s, dst), head_masks)
    dg = jnp.sum(dkh * kh, axis=0, keepdims=True) + jnp.sum(dst * st, axis=0, keepdims=True) * eg
    db = dqt * qt - dkt * kt - dkh * kh + jnp.where(total_row, dg, 0.0)
    dq = dqt * eb * (GLA_DK**-0.5)
    dk = dkt * enb + dkh * egb
    dqkv_ref[rows, :] = jnp.concatenate([dq, dk, dv], axis=1).astype(dqkv_ref.dtype)
    db_ref[rows, :] = db
    dst_ref[...] = dst * eg + _dot_tn(do_s, q_heads)


def _gla_bwd(z, b_f, b_b, do, st_f, st_b, *, name):
    s = z.shape[0]
    tb = _rows(s)
    cpb = tb // GLA_CHUNK
    wide = 2 * GLA_K_TOTAL + GLA_V_TOTAL
    nb, qf, kf, vf, bf, of, sf = _gla_specs(s, tb, True)
    _, qr, kr, vr, br, orr, sr = _gla_specs(s, tb, False)
    gf = pl.BlockSpec((tb, wide), lambda i: (nb - 1 - i, 0))
    gr = pl.BlockSpec((tb, wide), lambda i: (i, 0))

    def body(qf_ref, kf_ref, vf_ref, bf_ref, dof_ref, svf_ref, qr_ref, kr_ref, vr_ref, br_ref, dor_ref, svr_ref,
             gf_ref, dbf_ref, gr_ref, dbr_ref, dstf_ref, dstr_ref):
        masks_f, masks_r = _gla_masks(False), _gla_masks(True)

        @pl.when(pl.program_id(0) == 0)
        def _():
            dstf_ref[...] = jnp.zeros_like(dstf_ref)
            dstr_ref[...] = jnp.zeros_like(dstr_ref)

        def chunk(ci, carry):
            _gla_bwd_chunk(cpb - 1 - ci, qf_ref, kf_ref, vf_ref, bf_ref, dof_ref, svf_ref, gf_ref, dbf_ref, dstf_ref, masks_f)
            _gla_bwd_chunk(ci, qr_ref, kr_ref, vr_ref, br_ref, dor_ref, svr_ref, gr_ref, dbr_ref, dstr_ref, masks_r)
            return carry

        lax.fori_loop(0, cpb, chunk, 0)

    g_shape = jax.ShapeDtypeStruct((s, wide), _CD)
    db_shape = jax.ShapeDtypeStruct((s, GLA_K_TOTAL), F32)
    return pl.pallas_call(
        body,
        name=name,
        grid=(nb,),
        in_specs=[qf, kf, vf, bf, of, sf, qr, kr, vr, br, orr, sr],
        out_specs=[gf, bf, gr, br],
        out_shape=[g_shape, db_shape, g_shape, db_shape],
        scratch_shapes=[pltpu.VMEM((GLA_DV, GLA_K_TOTAL), F32)] * 2,
        compiler_params=_cp(("arbitrary",)),
    )(z, z, z, b_f, do, st_f, z, z, z, b_b, do, st_b)


HALO = 8


def _halo_specs(s, ts, width, col):
    last = s // HALO - 1
    per = ts // HALO
    prev = pl.BlockSpec((HALO, width), lambda i: (jnp.maximum(i * per - 1, 0), col))
    nxt = pl.BlockSpec((HALO, width), lambda i: (jnp.minimum((i + 1) * per, last), col))
    return prev, nxt


def _group_ones():
    group = jnp.arange(CONV_WIDTH, dtype=jnp.int32) // CONV_GROUP
    return (group[:, None] == group[None, :]).astype(BF16)


_ONES_SPEC = pl.BlockSpec((CONV_WIDTH, CONV_WIDTH), lambda i: (0, 0))


def _conv_terms(cc_ext, cu_ext, cw, valid):
    n = cc_ext.shape[0]
    hc = jnp.where(valid, cc_ext * cu_ext, 0.0)
    hc_prev = pltpu.roll(hc, 1, 0)
    hc_next = pltpu.roll(hc, n - 1, 0)
    conv = cw[0:1] * hc_prev + cw[1:2] * hc + cw[2:3] * hc_next
    return hc, hc_prev, hc_next, conv


def _ext(prev_ref, cur_ref, next_ref):
    return jnp.concatenate([prev_ref[...], cur_ref[...], next_ref[...]], axis=0)


def _valid_rows(ts, s):
    row = lax.broadcasted_iota(jnp.int32, (ts + 2 * HALO, 1), 0) + (pl.program_id(0) * ts - HALO)
    return (row >= 0) & (row < s)


def _head_norm(o, gn):
    out = []
    for h in range(GLA_HEADS):
        oh = o[:, GLA_DV * h : GLA_DV * (h + 1)]
        r = lax.rsqrt(jnp.mean(oh * oh, axis=-1, keepdims=True) + EPS)
        out.append((oh * r, r))
    return out


def _mix_fwd(z, o_f, o_b, conv_w, conv_norm, gla_norm, *, name):
    s = z.shape[0]
    ts = _rows(s, light=True)
    cprev, cnext = _halo_specs(s, ts, CONV_WIDTH, 1)
    uprev, unext = _halo_specs(s, ts, CONV_WIDTH, 2)

    def body(cb_ref, cc_ref, cu_ref, ccp_ref, ccn_ref, cup_ref, cun_ref, g_ref, of_ref, ob_ref, cw_ref, cn_ref, gn_ref, ones_ref, y_ref):
        valid = _valid_rows(ts, s)
        _, _, _, conv = _conv_terms(_ext(ccp_ref, cc_ref, ccn_ref), _ext(cup_ref, cu_ref, cun_ref), cw_ref[...], valid)
        yc = cb_ref[...] * conv[HALO : HALO + ts]
        ms = _dot_split(yc * yc, ones_ref[...]) * (1.0 / CONV_GROUP)
        y_conv = yc * lax.rsqrt(ms + EPS) * cn_ref[...]
        gate = g_ref[...]
        silu = gate * _sigmoid(gate)
        gn = gn_ref[...]
        y_gla = jnp.concatenate([oh * gn for oh, _ in _head_norm(of_ref[...] + ob_ref[...], gn)], axis=1) * silu
        y_ref[...] = jnp.concatenate([y_conv, y_gla], axis=1).astype(y_ref.dtype)

    col = lambda c, w=CONV_WIDTH: pl.BlockSpec((ts, w), lambda i: (i, c))
    return pl.pallas_call(
        body,
        name=name,
        grid=(s // ts,),
        in_specs=[col(0), col(1), col(2), cprev, cnext, uprev, unext, col(3), col(0), col(0),
                  pl.BlockSpec((CONV_K, CONV_WIDTH), lambda i: (0, 0)), pl.BlockSpec((1, CONV_WIDTH), lambda i: (0, 0)),
                  pl.BlockSpec((1, GLA_DV), lambda i: (0, 0)), _ONES_SPEC],
        out_specs=pl.BlockSpec((ts, D_MODEL), lambda i: (i, 0)),
        out_shape=jax.ShapeDtypeStruct((s, D_MODEL), _CD),
        compiler_params=_cp(("parallel",)),
    )(z, z, z, z, z, z, z, z, o_f, o_b, conv_w, conv_norm, gla_norm, _group_ones())


def _mix_bwd(z, o_f, o_b, dy, conv_w, conv_norm, gla_norm, *, name):
    s = z.shape[0]
    ts = _rows(s)
    halos = [_halo_specs(s, ts, CONV_WIDTH, c) for c in (0, 1, 2)]
    dprev, dnext = _halo_specs(s, ts, CONV_WIDTH, 0)

    def body(cb_ref, cc_ref, cu_ref, cbp_ref, cbn_ref, ccp_ref, ccn_ref, cup_ref, cun_ref, g_ref, of_ref, ob_ref,
             dyc_ref, dyg_ref, dyp_ref, dyn_ref, cw_ref, cn_ref, gn_ref, ones_ref, dza_ref, do_ref, dcw_ref, dcn_ref, dgn_ref):
        n = ts + 2 * HALO
        valid = _valid_rows(ts, s)
        cw = cw_ref[...]
        cn = cn_ref[...]
        ones = ones_ref[...]
        cb = _ext(cbp_ref, cb_ref, cbn_ref)
        cc = _ext(ccp_ref, cc_ref, ccn_ref)
        cu = _ext(cup_ref, cu_ref, cun_ref)
        dy = _ext(dyp_ref, dyc_ref, dyn_ref)
        hc, hc_prev, hc_next, conv = _conv_terms(cc, cu, cw, valid)
        yc = cb * conv
        r = lax.rsqrt(_dot_split(yc * yc, ones) * (1.0 / CONV_GROUP) + EPS)
        yh = yc * r
        dyh = dy * cn
        dyc = r * (dyh - yh * (_dot_split(dyh * yh, ones) * (1.0 / CONV_GROUP)))
        dconv = jnp.where(valid, dyc * cb, 0.0)
        dhc = cw[0:1] * pltpu.roll(dconv, n - 1, 0) + cw[1:2] * dconv + cw[2:3] * pltpu.roll(dconv, 1, 0)
        mid = lambda a: a[HALO : HALO + ts]
        dza_ref[:, 0 : 3 * CONV_WIDTH] = jnp.concatenate([mid(dyc * conv), mid(dhc * cu), mid(dhc * cc)], axis=1).astype(dza_ref.dtype)
        dconv_m = mid(dconv)
        colsum = lambda a: jnp.sum(a, axis=0, keepdims=True)
        dcw = jnp.concatenate([colsum(dconv_m * mid(hc_prev)), colsum(dconv_m * mid(hc)), colsum(dconv_m * mid(hc_next))], axis=0)
        dcn = colsum(mid(dy * yh))

        gate = g_ref[...]
        sg = _sigmoid(gate)
        silu = gate * sg
        gn = gn_ref[...]
        dyg = dyg_ref[...]
        don = dyg * silu
        heads = _head_norm(of_ref[...] + ob_ref[...], gn)
        on = jnp.concatenate([oh * gn for oh, _ in heads], axis=1)
        dza_ref[:, 3 * CONV_WIDTH : ZA_COLS] = (dyg * on * (sg * (1.0 + gate * (1.0 - sg)))).astype(dza_ref.dtype)
        dgn = jnp.zeros((1, GLA_DV), F32)
        dos = []
        for h, (oh, rh) in enumerate(heads):
            donh = don[:, GLA_DV * h : GLA_DV * (h + 1)]
            dgn = dgn + colsum(donh * oh)
            doh = donh * gn
            dos.append(rh * (doh - oh * jnp.mean(doh * oh, axis=-1, keepdims=True)))
        do_ref[...] = jnp.concatenate(dos, axis=1)

        first = pl.program_id(0) == 0

        @pl.when(first)
        def _():
            dcw_ref[...] = dcw
            dcn_ref[...] = dcn
            dgn_ref[...] = dgn

        @pl.when(jnp.logical_not(first))
        def _():
            dcw_ref[...] += dcw
            dcn_ref[...] += dcn
            dgn_ref[...] += dgn

    col = lambda c, w=CONV_WIDTH: pl.BlockSpec((ts, w), lambda i: (i, c))
    cw_spec = pl.BlockSpec((CONV_K, CONV_WIDTH), lambda i: (0, 0))
    cn_spec = pl.BlockSpec((1, CONV_WIDTH), lambda i: (0, 0))
    gn_spec = pl.BlockSpec((1, GLA_DV), lambda i: (0, 0))
    return pl.pallas_call(
        body,
        name=name,
        grid=(s // ts,),
        in_specs=[col(0), col(1), col(2), halos[0][0], halos[0][1], halos[1][0], halos[1][1], halos[2][0], halos[2][1],
                  col(3), col(0), col(0), col(0), col(1), dprev, dnext, cw_spec, cn_spec, gn_spec, _ONES_SPEC],
        out_specs=[pl.BlockSpec((ts, ZA_COLS), lambda i: (i, 0)), col(0), cw_spec, cn_spec, gn_spec],
        out_shape=[
            jax.ShapeDtypeStruct((s, ZA_COLS), _CD),
            jax.ShapeDtypeStruct((s, GLA_V_TOTAL), F32),
            jax.ShapeDtypeStruct((CONV_K, CONV_WIDTH), F32),
            jax.ShapeDtypeStruct((1, CONV_WIDTH), F32),
            jax.ShapeDtypeStruct((1, GLA_DV), F32),
        ],
        compiler_params=_cp(("arbitrary",)),
    )(z, z, z, z, z, z, z, z, z, z, o_f, o_b, dy, dy, dy, dy, conv_w, conv_norm, gla_norm, _group_ones())


def _xa_probs(q_ref, kv_ref, h):
    qh = q_ref[:, XA_HEAD_DIM * h : XA_HEAD_DIM * (h + 1)]
    kh = kv_ref[:, XA_HEAD_DIM * h : XA_HEAD_DIM * (h + 1)]
    vh = kv_ref[:, D_MODEL + XA_HEAD_DIM * h : D_MODEL + XA_HEAD_DIM * (h + 1)]
    sc = _dot_nt(qh, kh) * (XA_HEAD_DIM**-0.5)
    e = jnp.exp(sc - jnp.max(sc, axis=-1, keepdims=True))
    return qh, kh, vh, e / jnp.sum(e, axis=-1, keepdims=True)


def _xattn_block(hx, w_xq, kv, w_xo, x1, gain, *, name):
    s, d = hx.shape
    ts = _rows(s)

    def body(h_ref, wq_ref, kv_ref, wo_ref, x_ref, g_ref, q_out, o_out, hm_out, x_out):
        q = _dot(h_ref[...], wq_ref[...]).astype(_CD)
        q_out[...] = q
        heads = []
        for h in range(XA_HEADS):
            _, _, vh, p = _xa_probs(q, kv_ref, h)
            heads.append(_dot(p, vh))
        o = jnp.concatenate(heads, axis=1).astype(_CD)
        o_out[...] = o
        x = _dot(o, wo_ref[...]) + x_ref[...]
        r = lax.rsqrt(jnp.mean(x * x, axis=-1, keepdims=True) + EPS)
        x_out[...] = x
        hm_out[...] = (x * r * g_ref[...]).astype(hm_out.dtype)

    tile = pl.BlockSpec((ts, d), lambda i: (i, 0))
    whole = lambda arr: pl.BlockSpec(arr.shape, lambda i: (0, 0))
    lo = jax.ShapeDtypeStruct((s, d), _CD)
    return pl.pallas_call(
        body,
        name=name,
        grid=(s // ts,),
        in_specs=[tile, whole(w_xq), whole(kv), whole(w_xo), tile, whole(gain)],
        out_specs=[tile] * 4,
        out_shape=[lo, lo, lo, jax.ShapeDtypeStruct((s, d), F32)],
        compiler_params=_cp(("parallel",)),
    )(hx, w_xq, kv, w_xo, x1, gain)


def _xattn_bwd(qx, kv, dx, w_xo, *, name):
    s = qx.shape[0]
    ts = _rows(s, light=True)

    def body(q_ref, kv_ref, dx_ref, w_ref, dq_ref, dkv_ref):
        do = _dot_nt(dx_ref[...], w_ref[...]).astype(_CD)
        dqs, dks, dvs = [], [], []
        for h in range(XA_HEADS):
            qh, kh, vh, p = _xa_probs(q_ref, kv_ref, h)
            doh = do[:, XA_HEAD_DIM * h : XA_HEAD_DIM * (h + 1)]
            dp = _dot_nt(doh, vh)
            ds = p * (dp - jnp.sum(dp * p, axis=-1, keepdims=True)) * (XA_HEAD_DIM**-0.5)
            dqs.append(_dot(ds, kh))
            dks.append(_dot_tn(ds, qh))
            dvs.append(_dot_tn(p, doh))
        dq_ref[...] = jnp.concatenate(dqs, axis=1).astype(dq_ref.dtype)
        dkv = jnp.concatenate(dks + dvs, axis=1)

        @pl.when(pl.program_id(0) == 0)
        def _():
            dkv_ref[...] = dkv

        @pl.when(pl.program_id(0) > 0)
        def _():
            dkv_ref[...] += dkv

    tile = pl.BlockSpec((ts, D_MODEL), lambda i: (i, 0))
    kv_spec = pl.BlockSpec((N_MEM, 2 * D_MODEL), lambda i: (0, 0))
    return pl.pallas_call(
        body,
        name=name,
        grid=(s // ts,),
        in_specs=[tile, kv_spec, tile, pl.BlockSpec((D_MODEL, D_MODEL), lambda i: (0, 0))],
        out_specs=[tile, kv_spec],
        out_shape=[jax.ShapeDtypeStruct((s, D_MODEL), _CD), jax.ShapeDtypeStruct((N_MEM, 2 * D_MODEL), F32)],
        compiler_params=_cp(("arbitrary",)),
    )(qx, kv, dx, w_xo)


def _adamw_math(w, g, m, v):
    m = ADAM_B1 * m + (1.0 - ADAM_B1) * g
    v = ADAM_B2 * v + (1.0 - ADAM_B2) * (g * g)
    m_hat = m / (1.0 - ADAM_B1**ADAM_STEP)
    v_hat = v / (1.0 - ADAM_B2**ADAM_STEP)
    delta = -ADAM_LR * (m_hat / (jnp.sqrt(v_hat) + ADAM_EPS) + ADAM_WD * w)
    return delta, m, v


def _adamw(w, m, v, shard_rows, off, *, transposed, name):
    r, c = w.shape
    by_columns = r % 256 != 0
    tr = 512 if (c if by_columns else r) % 512 == 0 and off % 512 == 0 else 256
    if by_columns:
        assert not transposed and off == 0
        g_spec = tile = pl.BlockSpec((r, tr), lambda i: (0, i))
    else:
        g_spec = pl.BlockSpec((c, tr), lambda i: (off // c, i)) if transposed else pl.BlockSpec((tr, c), lambda i: (off // tr + i, 0))
        tile = pl.BlockSpec((tr, c), lambda i: (i, 0))

    def body(w_ref, g_ref, m_ref, v_ref, go_ref, d_ref, nm_ref, nv_ref):
        g = g_ref[...].T if transposed else g_ref[...]
        go_ref[...] = g
        d_ref[...], nm_ref[...], nv_ref[...] = _adamw_math(w_ref[...], g, m_ref[...], v_ref[...])

    return pl.pallas_call(
        body,
        name=name,
        grid=((c if by_columns else r) // tr,),
        in_specs=[tile, g_spec, tile, tile],
        out_specs=[tile] * 4,
        out_shape=[jax.ShapeDtypeStruct((r, c), F32)] * 4,
        compiler_params=_cp(("parallel",), vmem=_VMEM_STREAM),
    )(w, shard_rows, m, v)


def _adamw_small(groups, *, name):
    n = len(groups)

    def body(*refs):
        ins, outs = refs[: 4 * n], refs[4 * n :]
        for i in range(n):
            w_ref, g_ref, m_ref, v_ref = ins[4 * i : 4 * i + 4]
            outs[3 * i][...], outs[3 * i + 1][...], outs[3 * i + 2][...] = _adamw_math(w_ref[...], g_ref[...], m_ref[...], v_ref[...])

    flat = [a for grp in groups for a in grp]
    vm = pl.BlockSpec(memory_space=pltpu.VMEM)
    res = pl.pallas_call(
        body,
        name=name,
        in_specs=[vm] * (4 * n),
        out_specs=[vm] * (3 * n),
        out_shape=[jax.ShapeDtypeStruct(grp[0].shape, F32) for grp in groups for _ in range(3)],
        compiler_params=_cp(),
    )(*flat)
    return [tuple(res[3 * i : 3 * i + 3]) for i in range(n)]


def _place():
    return lax.axis_index("x"), lax.axis_index("y"), lax.axis_index("c")


def _rel_chip(x, y, k):
    return (1 - x if k & 2 else x), (1 - y if k & 1 else y)


def _half(c, rh):
    return pl.ds(pl.multiple_of(c * rh, 16), rh)


HBM = pl.BlockSpec(memory_space=pltpu.HBM)
SEM = pl.BlockSpec(memory_space=pltpu.SEMAPHORE)
EFFECT = pltpu.SideEffectType.DATAFLOW_SIDE_EFFECTING


def _in_hbm(a):
    return pltpu.with_memory_space_constraint(a, pltpu.HBM)


def _gather_copies(p_ref, land_ref, send_sems, recv_sems):
    rh = p_ref.shape[0] // 2
    x, y, c = _place()
    rows = _half(c, rh)
    copies = []
    for k in range(1, N_CHIPS):
        cx, cy = _rel_chip(x, y, k)
        copies.append(pltpu.make_async_remote_copy(
            src_ref=p_ref.at[rows], dst_ref=land_ref.at[2 * x + y, rows], send_sem=send_sems.at[k - 1], recv_sem=recv_sems.at[k - 1],
            device_id=(cx, cy, c), device_id_type=MESH))
    copies.append(pltpu.make_async_remote_copy(
        src_ref=p_ref, dst_ref=land_ref.at[2 * x + y], send_sem=send_sems.at[N_CHIPS - 1], recv_sem=recv_sems.at[N_CHIPS - 1],
        device_id=(x, y, 1 - c), device_id_type=MESH))
    return copies


def _gather_start(pack, after, *, name):
    r, w = pack.shape

    def body(p_ref, land_ref, after_ref, send_sems, recv_sems, p_thru, land_thru, token):
        for cp in _gather_copies(p_ref, land_ref, send_sems, recv_sems):
            cp.start()
        token[...] = jnp.zeros_like(token)

    return pl.pallas_call(
        body,
        name=name,
        out_shape=(pltpu.SemaphoreType.DMA((N_CHIPS,)), pltpu.SemaphoreType.DMA((N_CHIPS,)), pltpu.HBM((r, w), pack.dtype),
                   pltpu.HBM((N_CHIPS, r, w), pack.dtype), jax.ShapeDtypeStruct((8, 128), F32)),
        in_specs=(HBM, HBM, ANY),
        out_specs=(SEM, SEM, HBM, HBM, pl.BlockSpec(memory_space=pltpu.VMEM)),
        input_output_aliases={0: 2, 1: 3},
        compiler_params=pltpu.CompilerParams(has_side_effects=EFFECT),
    )(_in_hbm(pack), _in_hbm(lax.empty((N_CHIPS, r, w), pack.dtype)), after)


def _gather_wait(send_sems, recv_sems, pack, land, after, *, name):
    def body(p_ref, land_ref, send_sems, recv_sems, after_ref, p_out, land_out):
        for cp in _gather_copies(p_ref, land_ref, send_sems, recv_sems):
            cp.wait_send()
            cp.wait_recv()

    return pl.pallas_call(
        body,
        name=name,
        out_shape=(pltpu.HBM(pack.shape, pack.dtype), pltpu.HBM(land.shape, land.dtype)),
        in_specs=(HBM, HBM, SEM, SEM, ANY),
        out_specs=(HBM, HBM),
        input_output_aliases={0: 0, 1: 1},
        compiler_params=pltpu.CompilerParams(has_side_effects=EFFECT),
    )(pack, land, send_sems, recv_sems, after)


def _gather_spread(land, *, name):
    n, r, w = land.shape
    rh = r // 2

    def body(land_ref, o_ref, send_sems, recv_sems):
        x, y, c = _place()
        rows = _half(c, rh)
        copies = []
        for k in range(1, N_CHIPS):
            cx, cy = _rel_chip(x, y, k)
            copies.append(pltpu.make_async_remote_copy(
                src_ref=land_ref.at[2 * cx + cy, rows], dst_ref=o_ref.at[2 * cx + cy, rows], send_sem=send_sems.at[k - 1],
                recv_sem=recv_sems.at[k - 1], device_id=(x, y, 1 - c), device_id_type=MESH))
        for cp in copies:
            cp.start()
        for cp in copies:
            cp.wait()

    return pl.pallas_call(
        body,
        name=name,
        in_specs=[ANY],
        out_specs=ANY,
        out_shape=jax.ShapeDtypeStruct(land.shape, land.dtype),
        input_output_aliases={0: 0},
        scratch_shapes=[pltpu.SemaphoreType.DMA((N_CHIPS - 1,)), pltpu.SemaphoreType.DMA((N_CHIPS - 1,))],
        compiler_params=pltpu.CompilerParams(has_side_effects=True),
    )(land)


N_PARTS = 2 * (N_CHIPS - 1)


def _scatter_copies(lo_ref, g_ref, land_lo_ref, land_f_ref, send_sems, recv_sems, starting):
    rh = g_ref.shape[1] // 2
    x, y, c = _place()
    copies = []
    for k in range(1, N_CHIPS):
        cx, cy = _rel_chip(x, y, k)
        for i in range(2):
            part = 2 * (k - 1) + (c if starting else i)
            copies.append(pltpu.make_async_remote_copy(
                src_ref=lo_ref.at[2 * cx + cy, pl.ds(i * rh, rh)], dst_ref=land_lo_ref.at[part],
                send_sem=send_sems.at[2 * (k - 1) + i], recv_sem=recv_sems.at[part], device_id=(cx, cy, i), device_id_type=MESH))
    copies.append(pltpu.make_async_remote_copy(
        src_ref=g_ref.at[2 * x + y, _half(1 - c, rh)], dst_ref=land_f_ref, send_sem=send_sems.at[N_PARTS], recv_sem=recv_sems.at[N_PARTS],
        device_id=(x, y, 1 - c), device_id_type=MESH))
    return copies


def _scatter_start(g_lo, g, *, name):
    n, r, w = g.shape
    rh = r // 2

    def body(lo_ref, g_ref, land_lo_ref, land_f_ref, send_sems, recv_sems, lo_thru, g_thru, land_lo_thru, land_f_thru, token):
        for cp in _scatter_copies(lo_ref, g_ref, land_lo_ref, land_f_ref, send_sems, recv_sems, True):
            cp.start()
        token[...] = jnp.zeros_like(token)

    return pl.pallas_call(
        body,
        name=name,
        out_shape=(pltpu.SemaphoreType.DMA((N_PARTS + 1,)), pltpu.SemaphoreType.DMA((N_PARTS + 1,)), pltpu.HBM(g_lo.shape, g_lo.dtype),
                   pltpu.HBM(g.shape, g.dtype), pltpu.HBM((N_PARTS, rh, w), g_lo.dtype), pltpu.HBM((rh, w), g.dtype),
                   jax.ShapeDtypeStruct((8, 128), F32)),
        in_specs=(HBM, HBM, HBM, HBM),
        out_specs=(SEM, SEM, HBM, HBM, HBM, HBM, pl.BlockSpec(memory_space=pltpu.VMEM)),
        input_output_aliases={0: 2, 1: 3, 2: 4, 3: 5},
        compiler_params=pltpu.CompilerParams(has_side_effects=EFFECT),
    )(_in_hbm(g_lo), _in_hbm(g), _in_hbm(lax.empty((N_PARTS, rh, w), g_lo.dtype)), _in_hbm(lax.empty((rh, w), g.dtype)))


def _scatter_wait(send_sems, recv_sems, g_lo, g, land_lo, land_f, after, *, name):
    def body(lo_ref, g_ref, land_lo_ref, land_f_ref, send_sems, recv_sems, after_ref, o0, o1, o2, o3):
        for cp in _scatter_copies(lo_ref, g_ref, land_lo_ref, land_f_ref, send_sems, recv_sems, False):
            cp.wait_send()
            cp.wait_recv()

    arrays = (g_lo, g, land_lo, land_f)
    return pl.pallas_call(
        body,
        name=name,
        out_shape=tuple(pltpu.HBM(a.shape, a.dtype) for a in arrays),
        in_specs=(HBM, HBM, HBM, HBM, SEM, SEM, ANY),
        out_specs=(HBM, HBM, HBM, HBM),
        input_output_aliases={0: 0, 1: 1, 2: 2, 3: 3},
        compiler_params=pltpu.CompilerParams(has_side_effects=EFFECT),
    )(*arrays, send_sems, recv_sems, after)


def _scatter_sum(g, land_lo, land_f, where, *, name):
    n, r, w = g.shape
    rh = r // 2
    tr = _pick(rh, (256, 160, 80))
    nt = rh // tr

    def body(where_ref, g_ref, f_ref, lo_ref, o_ref):
        acc = g_ref[0] + f_ref[...]
        for part in range(N_PARTS):
            acc = acc + lo_ref[part].astype(F32)
        o_ref[...] = acc

    return pl.pallas_call(
        body,
        name=name,
        grid_spec=pltpu.PrefetchScalarGridSpec(
            num_scalar_prefetch=1,
            grid=(nt,),
            in_specs=[pl.BlockSpec((1, tr, w), lambda i, wh: (wh[1], wh[0] * nt + i, 0)),
                      pl.BlockSpec((tr, w), lambda i, wh: (i, 0)),
                      pl.BlockSpec((N_PARTS, tr, w), lambda i, wh: (0, i, 0))],
            out_specs=pl.BlockSpec((tr, w), lambda i, wh: (wh[0] * nt + i, 0)),
        ),
        out_shape=jax.ShapeDtypeStruct((r, w), F32),
        compiler_params=_cp(("parallel",), vmem=_VMEM_STREAM),
    )(where, g, land_f, land_lo)


def _swap_all(shards, *, name):
    n = len(shards)

    def body(*refs):
        ins, outs = refs[:n], refs[n : 2 * n]
        send_sems, recv_sems = refs[2 * n :]
        x, y, c = _place()
        copies = []
        for i, (e_ref, o_ref) in enumerate(zip(ins, outs)):
            rows = _half(c, e_ref.shape[0] // 2)
            copies.append(pltpu.make_async_remote_copy(src_ref=e_ref.at[rows], dst_ref=o_ref.at[rows], send_sem=send_sems.at[i],
                                                       recv_sem=recv_sems.at[i], device_id=(x, y, 1 - c), device_id_type=MESH))
        for cp in copies:
            cp.start()
        for cp in copies:
            cp.wait()

    return pl.pallas_call(
        body,
        name=name,
        in_specs=[ANY] * n,
        out_specs=[ANY] * n,
        out_shape=[jax.ShapeDtypeStruct(e.shape, e.dtype) for e in shards],
        input_output_aliases={i: i for i in range(n)},
        scratch_shapes=[pltpu.SemaphoreType.DMA((n,)), pltpu.SemaphoreType.DMA((n,))],
        compiler_params=pltpu.CompilerParams(has_side_effects=True),
    )(*shards)


def _sum_small(small, after):
    n_dev = 8

    def body(s_ref, after_ref, o_ref, all_ref, send_sems, recv_sems):
        x, y, c = _place()
        me = 4 * x + 2 * y + c
        all_ref[me] = s_ref[...]
        copies = []
        for k in range(1, n_dev):
            cx, cy = _rel_chip(x, y, k >> 1)
            cc = 1 - c if k & 1 else c
            copies.append(pltpu.make_async_remote_copy(
                src_ref=s_ref, dst_ref=all_ref.at[me], send_sem=send_sems.at[k - 1], recv_sem=recv_sems.at[k - 1],
                device_id=(cx, cy, cc), device_id_type=MESH))
        for cp in copies:
            cp.start()
        for cp in copies:
            cp.wait()
        acc = all_ref[0]
        for a in range(1, n_dev):
            acc = acc + all_ref[a]
        o_ref[...] = acc

    vm = pl.BlockSpec(memory_space=pltpu.VMEM)
    return pl.pallas_call(
        body,
        name="sum_small",
        in_specs=[vm, ANY],
        out_specs=vm,
        out_shape=jax.ShapeDtypeStruct(small.shape, F32),
        scratch_shapes=[pltpu.VMEM((n_dev,) + small.shape, F32), pltpu.SemaphoreType.DMA((n_dev - 1,)), pltpu.SemaphoreType.DMA((n_dev - 1,))],
        compiler_params=pltpu.CompilerParams(has_side_effects=True),
    )(small, after)


MATS = {"w_in": (776, True), "w_out": (256, False), "w_xq": (256, False), "w_xkv": (512, True), "w_xo": (256, False),
        "w_up": (1024, True), "w_down": (1024, False)}
GATHER_FIRST = ("w_in",)
GATHER_REST = ("w_up", "w_down", "w_out", "w_xq", "w_xkv", "w_xo")
IN_PLACE = ("w_up", "w_down")
GRAD_GROUPS = (("w_up", "w_down"), ("w_out", "w_xq", "w_xkv", "w_xo"), ("w_in",))


def _group_rows(names):
    n = sum(MATS[name][0] for name in names)
    return n + (-n) % 32


def _pack(pieces, rows):
    p = jnp.concatenate(pieces, axis=0) if len(pieces) > 1 else pieces[0]
    return jnp.pad(p, ((0, rows - p.shape[0]), (0, 0))) if rows > p.shape[0] else p


SMALL = (
    ("mix_norm", 1024), ("conv_norm", 512), ("b_af", 256), ("b_ab", 256), ("gla_norm", 128), ("xa_norm", 1024), ("mem_norm", 1024),
    ("mlp_norm", 1024), ("final_norm", 1024), ("conv_w", 1536), ("w_af", 4096), ("w_ab", 4096), ("loss", 128),
)


def kernel(x, mem, mix_norm, w_in, conv_w, conv_norm, w_af, b_af, w_ab, b_ab, gla_norm, w_out, xa_norm, mem_norm, w_xq, w_xkv, w_xo, mlp_norm, w_up, w_down, final_norm, loss_target, m_mix_norm, m_w_in, m_conv_w, m_conv_norm, m_w_af, m_b_af, m_w_ab, m_b_ab, m_gla_norm, m_w_out, m_xa_norm, m_mem_norm, m_w_xq, m_w_xkv, m_w_xo, m_mlp_norm, m_w_up, m_w_down, m_final_norm, v_mix_norm, v_w_in, v_conv_w, v_conv_norm, v_w_af, v_b_af, v_w_ab, v_b_ab, v_gla_norm, v_w_out, v_xa_norm, v_mem_norm, v_w_xq, v_w_xkv, v_w_xo, v_mlp_norm, v_w_up, v_w_down, v_final_norm):
    given = dict(locals())
    xi, yi, ci = _place()
    chip = 2 * xi + yi
    where = jnp.stack([ci, chip]).astype(jnp.int32)

    lo = {name: (given[name][0].T if MATS[name][1] else given[name][0]).astype(_CD) for name in MATS}
    pack_rest = _pack([lo[name] for name in GATHER_REST], _group_rows(GATHER_REST))
    pack_first = _pack([lo[name] for name in GATHER_FIRST], _group_rows(GATHER_FIRST))
    xs, mems, tgt = x[0], mem[0], loss_target[0]
    behind = lambda gain, token: gain + token[0, 0]

    def placed(shard, full_shape, col):
        return lax.dynamic_update_slice(jnp.zeros(full_shape, F32), shard, (0, col)).reshape(-1, 128)

    sw = jnp.concatenate([
        placed(conv_w[0], (CONV_K, CONV_WIDTH), 128 * chip),
        placed(w_af[0], (GLA_LOWRANK, GLA_K_TOTAL), 64 * chip),
        placed(w_ab[0], (GLA_LOWRANK, GLA_K_TOTAL), 64 * chip),
    ], axis=0)
    sw = jnp.pad(sw, ((0, SMALL_ROWS - sw.shape[0]), (0, 0))) * (ci == 0).astype(F32)
    sw = _sum_small(sw, mix_norm)

    first_send, first_recv, pack_first, land_first, first_token = _gather_start(pack_first, sw, name="gather_first_start")
    rest_send, rest_recv, pack_rest, land_rest, rest_token = _gather_start(pack_rest, first_token, name="gather_rest_start")
    h1 = _rms_fwd(xs, behind(mix_norm, rest_token), name="norm_mix")
    pack_first, land_first = _gather_wait(first_send, first_recv, pack_first, land_first, h1, name="gather_first_wait")
    got_first = _gather_spread(land_first, name="gather_first_spread")

    def whole(got, off, rows):
        return got[:, off : off + rows].reshape(N_CHIPS * rows, D_MODEL)

    w_in_t = whole(got_first, 0, MATS["w_in"][0])
    w_za = jnp.concatenate([w_in_t[0:1536], w_in_t[2560:3072]], axis=0)
    w_zb = jnp.concatenate([w_in_t[1536:2560], w_in_t[3072:W_IN_COLS], jnp.zeros((ZB_COLS - 1056, D_MODEL), _CD)], axis=0)
    conv_w_full = sw[0:12].reshape(CONV_K, CONV_WIDTH)
    w_af_full = sw[12:44].reshape(GLA_LOWRANK, GLA_K_TOTAL)
    w_ab_full = sw[44:76].reshape(GLA_LOWRANK, GLA_K_TOTAL)
    waf_p = jnp.pad(w_af_full, ((0, 128 - GLA_LOWRANK), (0, 0))).astype(_CD)
    wab_p = jnp.pad(w_ab_full, ((GLA_LOWRANK, 128 - 2 * GLA_LOWRANK), (0, 0))).astype(_CD)

    z_a, z_b = _mm_two(h1, w_za, w_zb, name="proj_in")
    b_f, b_b = _gate_fwd(z_b, waf_p, wab_p, b_af, b_ab, name="gates")
    o_f, st_f, o_b, st_b = _gla_fwd(z_b, b_f, b_b, name="gla_scan")
    y = _mix_fwd(z_a, o_f, o_b, conv_w_full, conv_norm, gla_norm, name="mix_out")
    pack_rest, land_rest = _gather_wait(rest_send, rest_recv, pack_rest, land_rest, y, name="gather_rest_wait")
    gathered = _gather_spread(land_rest, name="gather_rest_spread")
    wt, off = {}, 0
    for name in GATHER_REST:
        wt[name] = (gathered, off, MATS[name][0]) if name in IN_PLACE else whole(gathered, off, MATS[name][0])
        off += MATS[name][0]
    x1, hx = _mm_rows(y, wt["w_out"], mode="nn", name="proj_out", rows=(xs,), vecs=(xa_norm,), out_rows=(F32, _CD),
                      epilogue=_ep_residual_norm, tm=1024)
    hmem = _rms_fwd(mems, mem_norm, name="norm_mem")
    kv = _mm(hmem, wt["w_xkv"], mode="nt", name="proj_xkv", out_dtypes=(_CD,))
    qx, ox, hm, x2 = _xattn_block(hx, wt["w_xq"], kv, wt["w_xo"], x1, mlp_norm, name="xattn_block")
    act, relu_u = _mm(hm, wt["w_up"], mode="nt", name="mlp_up", out_dtypes=(_CD, _CD), tm=2048,
                      epilogue=lambda acc: (jnp.square(jnp.maximum(acc, 0.0)), jnp.maximum(acc, 0.0)))
    dx3, dx3_lo, loss_part, g_final_norm = _mm_rows(
        act, wt["w_down"], mode="nn", name="mlp_down", rows=(x2, tgt), vecs=(final_norm.reshape(1, D_MODEL),),
        out_rows=(F32, _CD), out_vecs=(128, D_MODEL), epilogue=_ep_loss)

    grads_t = {}

    def start_group(names, tag):
        rows = _group_rows(names)
        g = jnp.stack([_pack([grads_t[name][a * MATS[name][0] : (a + 1) * MATS[name][0]] for name in names], rows) for a in range(N_CHIPS)])
        return _scatter_start(g.astype(_TD), g, name="grads_" + tag + "_start")

    def finish_group(state, after, tag):
        send_sems, recv_sems, g_lo, g, land_lo, land_f, _ = state
        g_lo, g, land_lo, land_f = _scatter_wait(send_sems, recv_sems, g_lo, g, land_lo, land_f, after, name="grads_" + tag + "_wait")
        return _scatter_sum(g, land_lo, land_f, where, name="grads_" + tag + "_sum")

    def new_packs(names):
        shape = (N_CHIPS, _group_rows(names), D_MODEL)
        return lax.empty(shape, F32), lax.empty(shape, _TD)

    def grad_into(packs, names, which, a, b, name):
        off = sum(MATS[other][0] for other in names[: names.index(which)])
        return _mm_tn_into(a, b, packs, rows=MATS[which][0], off=off, name=name)

    du = _mm(dx3_lo, wt["w_down"], mode="nt", name="mlp_down_dx", out_dtypes=(_CD,), extras=(relu_u,), tm=2048,
             epilogue=lambda acc, rr: (acc * (2.0 * rr.astype(F32)),))
    packs = new_packs(GRAD_GROUPS[0])
    packs = grad_into(packs, GRAD_GROUPS[0], "w_down", act, dx3_lo, "mlp_down_dw")
    packs = grad_into(packs, GRAD_GROUPS[0], "w_up", du, hm, "mlp_up_dw")
    mlp_state = _scatter_start(packs[1], packs[0], name="grads_mlp_start")
    dx2, dx2_lo, g_mlp_norm = _mm_rows(
        du, wt["w_up"], mode="nn", name="mlp_up_dx", rows=(x2, dx3), vecs=(behind(mlp_norm, mlp_state[-1]),),
        out_rows=(F32, _CD), out_vecs=(D_MODEL,), epilogue=_ep_norm_bwd)
    packs = new_packs(GRAD_GROUPS[1])
    packs = grad_into(packs, GRAD_GROUPS[1], "w_xo", ox, dx2_lo, "proj_xo_dw")
    dqx, dkv = _xattn_bwd(qx, kv, dx2_lo, wt["w_xo"], name="xattn_bwd")
    packs = grad_into(packs, GRAD_GROUPS[1], "w_xq", hx, dqx, "proj_xq_dw")
    dx1, dx1_lo, g_xa_norm = _mm_rows(
        dqx, wt["w_xq"], mode="nt", name="proj_xq_dx", rows=(x1, dx2), vecs=(xa_norm,),
        out_rows=(F32, _CD), out_vecs=(D_MODEL,), epilogue=_ep_norm_bwd, tm=1024)
    dkv_lo = dkv.astype(_CD)
    packs = grad_into(packs, GRAD_GROUPS[1], "w_xkv", dkv_lo, hmem, "proj_xkv_dw")
    dhmem = _mm(dkv_lo, wt["w_xkv"], mode="nn", name="proj_xkv_dx")
    g_mem_norm = _rms_gain_grad(mems, dhmem, name="norm_mem_bwd")
    dy = _mm(dx1_lo, wt["w_out"], mode="nt", name="proj_out_dx")
    packs = grad_into(packs, GRAD_GROUPS[1], "w_out", y, dx1_lo, "proj_out_dw")
    attn_state = _scatter_start(packs[1], packs[0], name="grads_attn_start")
    dz_a, do, g_conv_w, g_conv_norm, g_gla_norm = _mix_bwd(z_a, o_f, o_b, dy, conv_w_full, behind(conv_norm, attn_state[-1]), gla_norm, name="mix_out_bwd")
    dqkv_f, db_f, dqkv_b, db_b = _gla_bwd(z_b, b_f, b_b, do, st_f, st_b, name="gla_scan_bwd")
    dz_b, g_waf_p, g_wab_p, g_b_af, g_b_ab = _gate_bwd(z_b, waf_p, wab_p, b_af, b_ab, db_f, db_b, dqkv_f, dqkv_b, name="gates_bwd")
    g_za = _mm_tn(dz_a, h1, name="proj_in_a_dw")
    g_zb = _mm_tn(dz_b, h1, name="proj_in_b_dw")
    grads_t["w_in"] = jnp.concatenate([g_za[0:1536], g_zb[0:1024], g_za[1536:2048], g_zb[1024:1056]], axis=0)
    in_state = start_group(GRAD_GROUPS[2], "in")
    grad_x, g_mix_norm = _mm_rows(
        dz_a, w_za, mode="nn", name="proj_in_dx", more=((dz_b, w_zb),), rows=(xs, dx1), vecs=(behind(mix_norm, in_state[-1]),),
        out_rows=(F32,), out_vecs=(D_MODEL,), epilogue=_ep_norm_bwd)

    half_mlp = finish_group(mlp_state, grad_x, "mlp")
    half_attn = finish_group(attn_state, half_mlp, "attn")
    half_in = finish_group(in_state, half_attn, "in")
    shard_rows = {}
    for names, rows in zip(GRAD_GROUPS, _swap_all([half_mlp, half_attn, half_in], name="shards_to_sibling")):
        off = 0
        for name in names:
            shard_rows[name] = (rows, off)
            off += MATS[name][0]

    small_vals = dict(mix_norm=g_mix_norm, conv_norm=g_conv_norm, b_af=g_b_af, b_ab=g_b_ab, gla_norm=g_gla_norm, xa_norm=g_xa_norm,
                      mem_norm=g_mem_norm, mlp_norm=g_mlp_norm, final_norm=g_final_norm, conv_w=g_conv_w,
                      w_af=g_waf_p[0:GLA_LOWRANK], w_ab=g_wab_p[GLA_LOWRANK : 2 * GLA_LOWRANK], loss=loss_part)
    small = jnp.concatenate([small_vals[name].reshape(-1, 128) for name, _ in SMALL], axis=0)
    small = _sum_small(jnp.pad(small, ((0, SMALL_ROWS - small.shape[0]), (0, 0))), loss_part)
    g_small, off = {}, 0
    for name, n in SMALL:
        g_small[name] = small[off : off + n // 128]
        off += n // 128
    loss = g_small["loss"][0, 0]
    g_small["conv_w"] = lax.dynamic_slice(g_small["conv_w"].reshape(CONV_K, CONV_WIDTH), (0, 128 * chip), (CONV_K, 128))
    g_small["w_af"] = lax.dynamic_slice(g_small["w_af"].reshape(GLA_LOWRANK, GLA_K_TOTAL), (0, 64 * chip), (GLA_LOWRANK, 64))
    g_small["w_ab"] = lax.dynamic_slice(g_small["w_ab"].reshape(GLA_LOWRANK, GLA_K_TOTAL), (0, 64 * chip), (GLA_LOWRANK, 64))

    names = ["mix_norm", "w_in", "conv_w", "conv_norm", "w_af", "b_af", "w_ab", "b_ab", "gla_norm", "w_out", "xa_norm", "mem_norm",
             "w_xq", "w_xkv", "w_xo", "mlp_norm", "w_up", "w_down", "final_norm"]
    big_names = list(MATS)
    as2d = lambda a: a.reshape(1, -1) if a.ndim == 1 else a.reshape(a.shape[-2:])
    grads, deltas, new_m, new_v = {}, {}, {}, {}
    for name in big_names:
        rows, off = shard_rows[name]
        wmv = [as2d(given[name]), as2d(given["m_" + name]), as2d(given["v_" + name])]
        as_stored = name == "w_in"
        if as_stored:
            wmv = [a.T for a in wmv]
        res = _adamw(*wmv, rows, off, transposed=MATS[name][1] and not as_stored, name="adamw_" + name)
        grads[name], deltas[name], new_m[name], new_v[name] = [a.T for a in res] if as_stored else res
    small_names = [name for name in names if name not in big_names]
    groups = []
    for name in small_names:
        grads[name] = g_small[name].reshape(as2d(given[name]).shape)
        groups.append((as2d(given[name]), grads[name], as2d(given["m_" + name]), as2d(given["v_" + name])))
    for name, res in zip(small_names, _adamw_small(groups, name="adamw_small")):
        deltas[name], new_m[name], new_v[name] = res

    like = lambda name, a: a.reshape(given[name].shape)
    return (loss, grad_x[None], *[like(n, grads[n]) for n in names], *[like(n, deltas[n]) for n in names],
            *[like(n, new_m[n]) for n in names], *[like(n, new_v[n]) for n in names])
```
